```python
import jax
import jax.numpy as jnp
from jax import lax
import numpy as np

D_MODEL = 2048
BATCH = 8
SEQ = 2048
DEPTH = 1

CTX_LEN = 256
GRID_W = 64
NORM_EPS = 1e-6
N_DIR = 2
N_BRANCH = 2
HG_WIDTH = 1024
HG_HEAD_DIM = 128
HG_HEADS = HG_WIDTH // HG_HEAD_DIM
HG_CHUNK = 64
RW_WIDTH = 1024
RW_HEAD_DIM = 64
RW_HEADS = RW_WIDTH // RW_HEAD_DIM
RW_DECAY_LORA = 64
RW_A_LORA = 64
RW_GN_EPS = 64e-5
HG_COLS = 5 * HG_WIDTH
RW_SHIFT_COLS = 3 * RW_WIDTH + N_DIR * (RW_DECAY_LORA + RW_A_LORA)
RW_COLS = RW_SHIFT_COLS + RW_WIDTH
GATE_COLS = N_BRANCH * D_MODEL
N_COLS = HG_COLS + RW_COLS + GATE_COLS

kernel_name = "hybrid_hgrn2_rwkv7_prefix_block"


def rmsnorm(x, g):
    x32 = x.astype(jnp.float32)
    return x32 * lax.rsqrt(jnp.mean(x32 * x32, axis=-1, keepdims=True) + NORM_EPS) * g


def ada_modulation(cond, w, b):
    mod = jax.nn.silu(cond.astype(jnp.float32)) @ w + b
    shift, scale, gate = jnp.split(mod, 3, axis=-1)
    return shift[:, None], scale[:, None], gate[:, None]


def grid_shift(p, mu, rows, cols, vertical):
    b, t, ch = p.shape
    p4 = p.reshape(b, rows, cols, ch)
    zc = jnp.zeros_like(p4[:, :, :1])
    left = jnp.concatenate([zc, p4[:, :, :-1]], axis=2)
    right = jnp.concatenate([p4[:, :, 1:], zc], axis=2)
    out = p4 + mu[0] * (left - p4) + mu[1] * (right - p4)
    if vertical:
        zr = jnp.zeros_like(p4[:, :1])
        up = jnp.concatenate([zr, p4[:, :-1]], axis=1)
        down = jnp.concatenate([p4[:, 1:], zr], axis=1)
        out = out + mu[2] * (up - p4) + mu[3] * (down - p4)
    return out.reshape(b, t, ch)


def hgrn2_chunk_scan(q, k, v, g, s0):
    b, t, h, _ = q.shape
    dv = v.shape[-1]
    n = t // HG_CHUNK

    def to_chunks(a):
        return a.reshape(b, n, HG_CHUNK, h, a.shape[-1]).transpose(1, 0, 3, 2, 4)

    tri = jnp.tril(jnp.ones((HG_CHUNK, HG_CHUNK), dtype=bool))

    def step(s, inp):
        qi, ki, vi, gi = inp
        bc = jnp.cumsum(gi, axis=2)
        diff = bc[:, :, :, None, :] - bc[:, :, None, :, :]
        decay = jnp.where(tri[:, :, None], jnp.exp(jnp.minimum(diff, 0.0)), 0.0)
        attn = jnp.einsum('bhtk,bhsk,bhtsk->bhts', qi, ki, decay)
        o = jnp.einsum('bhts,bhsv->bhtv', attn, vi) + jnp.einsum('bhtk,bhkv->bhtv', qi * jnp.exp(bc), s)
        blast = bc[:, :, -1:, :]
        s_new = jnp.exp(blast[:, :, 0, :])[..., None] * s + jnp.einsum('bhsk,bhsv->bhkv', ki * jnp.exp(blast - bc), vi)
        return s_new, o

    s_fin, oc = lax.scan(step, s0, (to_chunks(q), to_chunks(k), to_chunks(v), to_chunks(g)))
    return oc.transpose(1, 0, 3, 2, 4).reshape(b, t, h, dv), s_fin


def hgrn2_mixer(p, lb, norm_g, s0):
    b, t, _ = p.shape
    q_raw, i_in, f_fwd, f_bwd, z = jnp.split(p, 5, axis=-1)
    heads = lambda a: a.reshape(b, t, HG_HEADS, HG_HEAD_DIM)
    flip = lambda a: jnp.flip(a, axis=1)
    q = heads(jax.nn.silu(q_raw))
    v = heads(i_in)
    fg_f = lb[0] + (1.0 - lb[0]) * jax.nn.sigmoid(f_fwd)
    fg_b = lb[1] + (1.0 - lb[1]) * jax.nn.sigmoid(f_bwd)
    o_f, s_f = hgrn2_chunk_scan(q, heads(1.0 - fg_f), v, heads(jnp.log(fg_f)), s0[0])
    o_b, s_b = hgrn2_chunk_scan(flip(q), flip(heads(1.0 - fg_b)), flip(v), flip(heads(jnp.log(fg_b))), s0[1])
    o = o_f + flip(o_b)
    o = o * lax.rsqrt(jnp.mean(o * o, axis=-1, keepdims=True) + NORM_EPS)
    out = o.reshape(b, t, HG_WIDTH) * norm_g * jax.nn.silu(z)
    return out, jnp.stack([s_f, s_b])


def rwkv7_scan(r, w, kk, kb, v, k, s0, reverse):
    tm = lambda a: jnp.moveaxis(a, 1, 0)

    def step(s, inp):
        r_t, w_t, kk_t, b_t, v_t, k_t = inp
        sa = jnp.einsum('bhvk,bhk->bhv', s, kk_t)
        s = s * w_t[:, :, None, :] - sa[..., None] * b_t[:, :, None, :] + v_t[..., None] * k_t[:, :, None, :]
        return s, jnp.einsum('bhvk,bhk->bhv', s, r_t)

    s_fin, y = lax.scan(step, s0, (tm(r), tm(w), tm(kk), tm(kb), tm(v), tm(k)), reverse=reverse)
    return jnp.moveaxis(y, 0, 1), s_fin


def rwkv7_mixer(p, rw, s0, rows, cols, vertical):
    mu, w0, w2, a0, a2, k_k, k_a, r_k, gn_g, gn_b = rw
    b, t, _ = p.shape
    heads = lambda a: a.reshape(b, t, RW_HEADS, RW_HEAD_DIM)
    sh = grid_shift(p[..., :RW_SHIFT_COLS], mu, rows, cols, vertical)
    z = p[..., RW_SHIFT_COLS:]
    r = sh[..., :RW_WIDTH]
    k = sh[..., RW_WIDTH:2 * RW_WIDTH]
    v = sh[..., 2 * RW_WIDTH:3 * RW_WIDTH]
    lora = sh[..., 3 * RW_WIDTH:]
    w_lo = lora[..., :N_DIR * RW_DECAY_LORA].reshape(b, t, N_DIR, RW_DECAY_LORA)
    a_lo = lora[..., N_DIR * RW_DECAY_LORA:].reshape(b, t, N_DIR, RW_A_LORA)
    kk = heads(k * k_k)
    kk = kk * lax.rsqrt(jnp.sum(kk * kk, axis=-1, keepdims=True) + 1e-12)
    y_sum = 0.0
    k_sum = 0.0
    states = []
    for d in range(N_DIR):
        w_log = -jax.nn.softplus(-(w0[d] + jnp.tanh(w_lo[:, :, d]) @ w2[d])) - 0.5
        decay = jnp.exp(-jnp.exp(w_log))
        a = jax.nn.sigmoid(a0[d] + a_lo[:, :, d] @ a2[d])
        k_d = k * (1.0 + (a - 1.0) * k_a)
        y_d, s_d = rwkv7_scan(heads(r), heads(decay), kk, kk * heads(a), heads(v), heads(k_d), s0[d], d == 1)
        y_sum = y_sum + y_d
        k_sum = k_sum + k_d
        states.append(s_d)
    mean = jnp.mean(y_sum, axis=-1, keepdims=True)
    var = jnp.mean(jnp.square(y_sum - mean), axis=-1, keepdims=True)
    y = ((y_sum - mean) * lax.rsqrt(var + RW_GN_EPS)).reshape(b, t, RW_WIDTH) * gn_g + gn_b
    bonus = jnp.sum(heads(r * k_sum * r_k), axis=-1, keepdims=True) * heads(v)
    out = (y + bonus.reshape(b, t, RW_WIDTH)) * jax.nn.silu(z)
    return out, jnp.stack(states)


def mixer_stream(s, cond, ada_w, ada_b, norm_g, w_in, lb, hg_norm_g, rw, hg_s0, rw_s0, rows, cols, vertical):
    shift, scale, gate = ada_modulation(cond, ada_w, ada_b)
    h = rmsnorm(s, norm_g) * (1.0 + scale) + shift
    proj = h @ w_in
    y_hg, hg_s = hgrn2_mixer(proj[..., :HG_COLS], lb, hg_norm_g, hg_s0)
    y_rw, rw_s = rwkv7_mixer(proj[..., HG_COLS:HG_COLS + RW_COLS], rw, rw_s0, rows, cols, vertical)
    return y_hg, y_rw, proj[..., HG_COLS + RW_COLS:], gate, hg_s, rw_s


def merge_branches(gate_p, y_hg, y_rw, w_hg_o, w_rw_o, w_o):
    g_hg, g_rw = jnp.split(gate_p, N_BRANCH, axis=-1)
    merged = jax.nn.sigmoid(g_hg) * (y_hg @ w_hg_o) + jax.nn.sigmoid(g_rw) * (y_rw @ w_rw_o)
    return merged @ w_o


def _fwd_setup_inputs(seed: int = 0) -> dict:
    key = jax.random.key(seed)
    ks = jax.random.split(key, 26)
    L, D = DEPTH, D_MODEL
    nrm = lambda k, shape, s: jax.random.normal(k, shape, jnp.float32) * s
    return {
        'x': nrm(ks[0], (BATCH, SEQ, D), 1.0),
        'c': nrm(ks[1], (BATCH, D), 1.0),
        'ctx': nrm(ks[2], (BATCH, CTX_LEN, D), 1.0),
        'c_ctx': nrm(ks[3], (D,), 1.0),
        'ada_w': nrm(ks[4], (L, D, 3 * D), 0.5 * D ** -0.5),
        'ada_b': nrm(ks[5], (L, 3 * D), 0.02),
        'norm_g': 1.0 + nrm(ks[6], (L, D), 0.02),
        'w_in': nrm(ks[7], (L, D, N_COLS), D ** -0.5),
        'hg_lb': nrm(ks[8], (N_DIR, L + 1, HG_WIDTH), 1.0),
        'hg_norm_g': 1.0 + nrm(ks[9], (L, HG_WIDTH), 0.02),
        'rw_mu': jax.random.uniform(ks[10], (L, 4, RW_SHIFT_COLS), jnp.float32, 0.0, 0.5),
        'rw_w0': jax.random.uniform(ks[11], (L, N_DIR, RW_WIDTH), jnp.float32, -5.0, 0.0),
        'rw_w2': nrm(ks[12], (L, N_DIR, RW_DECAY_LORA, RW_WIDTH), 0.5 * RW_DECAY_LORA ** -0.5),
        'rw_a0': nrm(ks[13], (L, N_DIR, RW_WIDTH), 0.1),
        'rw_a2': nrm(ks[14], (L, N_DIR, RW_A_LORA, RW_WIDTH), 0.5 * RW_A_LORA ** -0.5),
        'rw_kk': 0.85 + nrm(ks[15], (L, RW_WIDTH), 0.1),
        'rw_ka': 1.0 + nrm(ks[16], (L, RW_WIDTH), 0.1),
        'rw_rk': nrm(ks[17], (L, RW_WIDTH), 0.1),
        'rw_gn_g': 1.0 + nrm(ks[18], (L, RW_WIDTH), 0.02),
        'rw_gn_b': nrm(ks[19], (L, RW_WIDTH), 0.02),
        'w_hg_out': nrm(ks[20], (L, HG_WIDTH, D), HG_WIDTH ** -0.5),
        'w_rw_out': nrm(ks[21], (L, RW_WIDTH, D), RW_WIDTH ** -0.5),
        'w_out': nrm(ks[22], (L, D, D), D ** -0.5),
        'final_g': 1.0 + nrm(ks[23], (D,), 0.02),
    }


def _fwd_reference(x, c, ctx, c_ctx, ada_w, ada_b, norm_g, w_in, hg_lb, hg_norm_g, rw_mu, rw_w0, rw_w2, rw_a0,
              rw_a2, rw_kk, rw_ka, rw_rk, rw_gn_g, rw_gn_b, w_hg_out, w_rw_out, w_out, final_g):
    out_dtype = x.dtype
    b, seq, _ = x.shape
    ctx_len = ctx.shape[1]
    rows = seq // GRID_W
    x = x.astype(jnp.float32)
    ctx = ctx.astype(jnp.float32)
    lb_all = jnp.cumsum(jax.nn.softmax(hg_lb.astype(jnp.float32), axis=1), axis=1)
    hg_zero = jnp.zeros((N_DIR, b, HG_HEADS, HG_HEAD_DIM, HG_HEAD_DIM), jnp.float32)
    rw_zero = jnp.zeros((N_DIR, b, RW_HEADS, RW_HEAD_DIM, RW_HEAD_DIM), jnp.float32)
    for l in range(DEPTH):
        lb = lb_all[:, l]
        rw = (rw_mu[l], rw_w0[l], rw_w2[l], rw_a0[l], rw_a2[l], rw_kk[l], rw_ka[l], rw_rk[l], rw_gn_g[l], rw_gn_b[l])
        yc_hg, yc_rw, gc, gate_c, hg_s, rw_s = mixer_stream(
            ctx, c_ctx[None], ada_w[l], ada_b[l], norm_g[l], w_in[l], lb, hg_norm_g[l], rw,
            hg_zero, rw_zero, 1, ctx_len, False)
        y_hg, y_rw, gl, gate_x, _, _ = mixer_stream(
            x, c, ada_w[l], ada_b[l], norm_g[l], w_in[l], lb, hg_norm_g[l], rw,
            hg_s, rw_s, rows, GRID_W, True)
        x = x + gate_x * merge_branches(gl, y_hg, y_rw, w_hg_out[l], w_rw_out[l], w_out[l])
        if l < DEPTH - 1:
            ctx = ctx + gate_c * merge_branches(gc, yc_hg, yc_rw, w_hg_out[l], w_rw_out[l], w_out[l])
    return rmsnorm(x, final_g).astype(out_dtype)


import jax as _jax
import jax.numpy as _jnp

TWIN_FORMAT = 'train_step'
FWD_PARAMS = ['x', 'c', 'ctx', 'c_ctx', 'ada_w', 'ada_b', 'norm_g', 'w_in', 'hg_lb', 'hg_norm_g', 'rw_mu', 'rw_w0', 'rw_w2', 'rw_a0', 'rw_a2', 'rw_kk', 'rw_ka', 'rw_rk', 'rw_gn_g', 'rw_gn_b', 'w_hg_out', 'w_rw_out', 'w_out', 'final_g']
TWIN_WEIGHTS = ['c_ctx', 'ada_w', 'ada_b', 'norm_g', 'w_in', 'hg_lb', 'hg_norm_g', 'rw_mu', 'rw_w0', 'rw_w2', 'rw_a0', 'rw_a2', 'rw_kk', 'rw_ka', 'rw_rk', 'rw_gn_g', 'rw_gn_b', 'w_hg_out', 'w_rw_out', 'w_out', 'final_g']
TWIN_DIFF_INPUT = 'x'
TWIN_INPUTS = ['x', 'c', 'ctx', 'c_ctx', 'ada_w', 'ada_b', 'norm_g', 'w_in', 'hg_lb', 'hg_norm_g', 'rw_mu', 'rw_w0', 'rw_w2', 'rw_a0', 'rw_a2', 'rw_kk', 'rw_ka', 'rw_rk', 'rw_gn_g', 'rw_gn_b', 'w_hg_out', 'w_rw_out', 'w_out', 'final_g', 'loss_target', 'm_c_ctx', 'm_ada_w', 'm_ada_b', 'm_norm_g', 'm_w_in', 'm_hg_lb', 'm_hg_norm_g', 'm_rw_mu', 'm_rw_w0', 'm_rw_w2', 'm_rw_a0', 'm_rw_a2', 'm_rw_kk', 'm_rw_ka', 'm_rw_rk', 'm_rw_gn_g', 'm_rw_gn_b', 'm_w_hg_out', 'm_w_rw_out', 'm_w_out', 'm_final_g', 'v_c_ctx', 'v_ada_w', 'v_ada_b', 'v_norm_g', 'v_w_in', 'v_hg_lb', 'v_hg_norm_g', 'v_rw_mu', 'v_rw_w0', 'v_rw_w2', 'v_rw_a0', 'v_rw_a2', 'v_rw_kk', 'v_rw_ka', 'v_rw_rk', 'v_rw_gn_g', 'v_rw_gn_b', 'v_w_hg_out', 'v_w_rw_out', 'v_w_out', 'v_final_g']
TWIN_OUTPUTS = ['loss', 'grad_x', 'grad_c_ctx', 'grad_ada_w', 'grad_ada_b', 'grad_norm_g', 'grad_w_in', 'grad_hg_lb', 'grad_hg_norm_g', 'grad_rw_mu', 'grad_rw_w0', 'grad_rw_w2', 'grad_rw_a0', 'grad_rw_a2', 'grad_rw_kk', 'grad_rw_ka', 'grad_rw_rk', 'grad_rw_gn_g', 'grad_rw_gn_b', 'grad_w_hg_out', 'grad_w_rw_out', 'grad_w_out', 'grad_final_g', 'delta_c_ctx', 'delta_ada_w', 'delta_ada_b', 'delta_norm_g', 'delta_w_in', 'delta_hg_lb', 'delta_hg_norm_g', 'delta_rw_mu', 'delta_rw_w0', 'delta_rw_w2', 'delta_rw_a0', 'delta_rw_a2', 'delta_rw_kk', 'delta_rw_ka', 'delta_rw_rk', 'delta_rw_gn_g', 'delta_rw_gn_b', 'delta_w_hg_out', 'delta_w_rw_out', 'delta_w_out', 'delta_final_g', 'new_m_c_ctx', 'new_m_ada_w', 'new_m_ada_b', 'new_m_norm_g', 'new_m_w_in', 'new_m_hg_lb', 'new_m_hg_norm_g', 'new_m_rw_mu', 'new_m_rw_w0', 'new_m_rw_w2', 'new_m_rw_a0', 'new_m_rw_a2', 'new_m_rw_kk', 'new_m_rw_ka', 'new_m_rw_rk', 'new_m_rw_gn_g', 'new_m_rw_gn_b', 'new_m_w_hg_out', 'new_m_w_rw_out', 'new_m_w_out', 'new_m_final_g', 'new_v_c_ctx', 'new_v_ada_w', 'new_v_ada_b', 'new_v_norm_g', 'new_v_w_in', 'new_v_hg_lb', 'new_v_hg_norm_g', 'new_v_rw_mu', 'new_v_rw_w0', 'new_v_rw_w2', 'new_v_rw_a0', 'new_v_rw_a2', 'new_v_rw_kk', 'new_v_rw_ka', 'new_v_rw_rk', 'new_v_rw_gn_g', 'new_v_rw_gn_b', 'new_v_w_hg_out', 'new_v_w_rw_out', 'new_v_w_out', 'new_v_final_g']
TWIN_LEAF_KINDS = {'loss': 'loss', 'grad_x': 'grad_x', 'grad_c_ctx': 'grad_w', 'grad_ada_w': 'grad_w', 'grad_ada_b': 'grad_w', 'grad_norm_g': 'grad_w', 'grad_w_in': 'grad_w', 'grad_hg_lb': 'grad_w', 'grad_hg_norm_g': 'grad_w', 'grad_rw_mu': 'grad_w', 'grad_rw_w0': 'grad_w', 'grad_rw_w2': 'grad_w', 'grad_rw_a0': 'grad_w', 'grad_rw_a2': 'grad_w', 'grad_rw_kk': 'grad_w', 'grad_rw_ka': 'grad_w', 'grad_rw_rk': 'grad_w', 'grad_rw_gn_g': 'grad_w', 'grad_rw_gn_b': 'grad_w', 'grad_w_hg_out': 'grad_w', 'grad_w_rw_out': 'grad_w', 'grad_w_out': 'grad_w', 'grad_final_g': 'grad_w', 'delta_c_ctx': 'delta_w', 'delta_ada_w': 'delta_w', 'delta_ada_b': 'delta_w', 'delta_norm_g': 'delta_w', 'delta_w_in': 'delta_w', 'delta_hg_lb': 'delta_w', 'delta_hg_norm_g': 'delta_w', 'delta_rw_mu': 'delta_w', 'delta_rw_w0': 'delta_w', 'delta_rw_w2': 'delta_w', 'delta_rw_a0': 'delta_w', 'delta_rw_a2': 'delta_w', 'delta_rw_kk': 'delta_w', 'delta_rw_ka': 'delta_w', 'delta_rw_rk': 'delta_w', 'delta_rw_gn_g': 'delta_w', 'delta_rw_gn_b': 'delta_w', 'delta_w_hg_out': 'delta_w', 'delta_w_rw_out': 'delta_w', 'delta_w_out': 'delta_w', 'delta_final_g': 'delta_w', 'new_m_c_ctx': 'new_m', 'new_m_ada_w': 'new_m', 'new_m_ada_b': 'new_m', 'new_m_norm_g': 'new_m', 'new_m_w_in': 'new_m', 'new_m_hg_lb': 'new_m', 'new_m_hg_norm_g': 'new_m', 'new_m_rw_mu': 'new_m', 'new_m_rw_w0': 'new_m', 'new_m_rw_w2': 'new_m', 'new_m_rw_a0': 'new_m', 'new_m_rw_a2': 'new_m', 'new_m_rw_kk': 'new_m', 'new_m_rw_ka': 'new_m', 'new_m_rw_rk': 'new_m', 'new_m_rw_gn_g': 'new_m', 'new_m_rw_gn_b': 'new_m', 'new_m_w_hg_out': 'new_m', 'new_m_w_rw_out': 'new_m', 'new_m_w_out': 'new_m', 'new_m_final_g': 'new_m', 'new_v_c_ctx': 'new_v', 'new_v_ada_w': 'new_v', 'new_v_ada_b': 'new_v', 'new_v_norm_g': 'new_v', 'new_v_w_in': 'new_v', 'new_v_hg_lb': 'new_v', 'new_v_hg_norm_g': 'new_v', 'new_v_rw_mu': 'new_v', 'new_v_rw_w0': 'new_v', 'new_v_rw_w2': 'new_v', 'new_v_rw_a0': 'new_v', 'new_v_rw_a2': 'new_v', 'new_v_rw_kk': 'new_v', 'new_v_rw_ka': 'new_v', 'new_v_rw_rk': 'new_v', 'new_v_rw_gn_g': 'new_v', 'new_v_rw_gn_b': 'new_v', 'new_v_w_hg_out': 'new_v', 'new_v_w_rw_out': 'new_v', 'new_v_w_out': 'new_v', 'new_v_final_g': 'new_v'}


def _forward(args):
    return _fwd_reference(*[args[k] for k in FWD_PARAMS])


def _output_shape():
    out = _jax.eval_shape(lambda: _forward(_fwd_setup_inputs(0)))
    return out.shape, out.dtype

N_MICROBATCH = 1
ADAM_LR = 0.001
ADAM_B1 = 0.9
ADAM_B2 = 0.999
ADAM_EPS = 1e-08
ADAM_WD = 0.01
ADAM_STEP = 10
PER_EXAMPLE_BATCH_AXIS = {'x': 0, 'c': 0, 'ctx': 0, 'loss_target': 0}
SHARED_INPUTS = []
_WEIGHT_DTYPES = {'c_ctx': _jnp.float32, 'ada_w': _jnp.float32, 'ada_b': _jnp.float32, 'norm_g': _jnp.float32, 'w_in': _jnp.float32, 'hg_lb': _jnp.float32, 'hg_norm_g': _jnp.float32, 'rw_mu': _jnp.float32, 'rw_w0': _jnp.float32, 'rw_w2': _jnp.float32, 'rw_a0': _jnp.float32, 'rw_a2': _jnp.float32, 'rw_kk': _jnp.float32, 'rw_ka': _jnp.float32, 'rw_rk': _jnp.float32, 'rw_gn_g': _jnp.float32, 'rw_gn_b': _jnp.float32, 'w_hg_out': _jnp.float32, 'w_rw_out': _jnp.float32, 'w_out': _jnp.float32, 'final_g': _jnp.float32}
MOMENT_SCALE = {'c_ctx': 2.405176e-03, 'ada_w': 1.643381e-02, 'ada_b': 2.704977e-02, 'norm_g': 1.748480e-02, 'w_in': 7.211879e-03, 'hg_lb': 3.682063e-04, 'hg_norm_g': 9.858958e-03, 'rw_mu': 1.941267e-02, 'rw_w0': 3.374576e-03, 'rw_w2': 7.980953e-04, 'rw_a0': 2.791590e-03, 'rw_a2': 2.199010e-03, 'rw_kk': 2.151488e-02, 'rw_ka': 2.187845e-02, 'rw_rk': 3.571756e-02, 'rw_gn_g': 1.075601e-02, 'rw_gn_b': 9.786541e-03, 'w_hg_out': 6.718472e-03, 'w_rw_out': 7.172744e-03, 'w_out': 9.819430e-03, 'final_g': 7.995587e+00}


def _to_microbatches(a, axis):
    t = _jnp.moveaxis(a, axis, 0)
    t = t.reshape((N_MICROBATCH, t.shape[0] // N_MICROBATCH) + t.shape[1:])
    return _jnp.moveaxis(t, 1, axis + 1)


def setup_inputs(seed: int = 0) -> dict:
    inp = _fwd_setup_inputs(seed)
    key = _jax.random.fold_in(_jax.random.key(seed), 7919)
    shape, _ = _output_shape()
    out = dict(inp)
    out["loss_target"] = _jax.random.normal(_jax.random.fold_in(key, 0), shape, _jnp.float32)
    for i, name in enumerate(TWIN_WEIGHTS):
        w = inp[name].astype(_jnp.float32)
        if MOMENT_SCALE is None:
            s = _jnp.sqrt(_jnp.mean(_jnp.square(w)) + 1e-30)
        else:
            s = MOMENT_SCALE[name]
        km, kv = _jax.random.split(_jax.random.fold_in(key, i + 1))
        out[name] = w
        out["m_" + name] = s * _jax.random.normal(km, w.shape, _jnp.float32)
        out["v_" + name] = (s * s) * _jax.random.uniform(kv, w.shape, _jnp.float32, 0.5, 1.5)
    if N_MICROBATCH > 1:
        for name, axis in PER_EXAMPLE_BATCH_AXIS.items():
            out[name] = _to_microbatches(out[name], axis)
    return {'x': out['x'], 'c': out['c'], 'ctx': out['ctx'], 'c_ctx': out['c_ctx'], 'ada_w': out['ada_w'], 'ada_b': out['ada_b'], 'norm_g': out['norm_g'], 'w_in': out['w_in'], 'hg_lb': out['hg_lb'], 'hg_norm_g': out['hg_norm_g'], 'rw_mu': out['rw_mu'], 'rw_w0': out['rw_w0'], 'rw_w2': out['rw_w2'], 'rw_a0': out['rw_a0'], 'rw_a2': out['rw_a2'], 'rw_kk': out['rw_kk'], 'rw_ka': out['rw_ka'], 'rw_rk': out['rw_rk'], 'rw_gn_g': out['rw_gn_g'], 'rw_gn_b': out['rw_gn_b'], 'w_hg_out': out['w_hg_out'], 'w_rw_out': out['w_rw_out'], 'w_out': out['w_out'], 'final_g': out['final_g'], 'loss_target': out['loss_target'], 'm_c_ctx': out['m_c_ctx'], 'm_ada_w': out['m_ada_w'], 'm_ada_b': out['m_ada_b'], 'm_norm_g': out['m_norm_g'], 'm_w_in': out['m_w_in'], 'm_hg_lb': out['m_hg_lb'], 'm_hg_norm_g': out['m_hg_norm_g'], 'm_rw_mu': out['m_rw_mu'], 'm_rw_w0': out['m_rw_w0'], 'm_rw_w2': out['m_rw_w2'], 'm_rw_a0': out['m_rw_a0'], 'm_rw_a2': out['m_rw_a2'], 'm_rw_kk': out['m_rw_kk'], 'm_rw_ka': out['m_rw_ka'], 'm_rw_rk': out['m_rw_rk'], 'm_rw_gn_g': out['m_rw_gn_g'], 'm_rw_gn_b': out['m_rw_gn_b'], 'm_w_hg_out': out['m_w_hg_out'], 'm_w_rw_out': out['m_w_rw_out'], 'm_w_out': out['m_w_out'], 'm_final_g': out['m_final_g'], 'v_c_ctx': out['v_c_ctx'], 'v_ada_w': out['v_ada_w'], 'v_ada_b': out['v_ada_b'], 'v_norm_g': out['v_norm_g'], 'v_w_in': out['v_w_in'], 'v_hg_lb': out['v_hg_lb'], 'v_hg_norm_g': out['v_hg_norm_g'], 'v_rw_mu': out['v_rw_mu'], 'v_rw_w0': out['v_rw_w0'], 'v_rw_w2': out['v_rw_w2'], 'v_rw_a0': out['v_rw_a0'], 'v_rw_a2': out['v_rw_a2'], 'v_rw_kk': out['v_rw_kk'], 'v_rw_ka': out['v_rw_ka'], 'v_rw_rk': out['v_rw_rk'], 'v_rw_gn_g': out['v_rw_gn_g'], 'v_rw_gn_b': out['v_rw_gn_b'], 'v_w_hg_out': out['v_w_hg_out'], 'v_w_rw_out': out['v_w_rw_out'], 'v_w_out': out['v_w_out'], 'v_final_g': out['v_final_g']}


def _loss(weights, diff, rest, loss_target):
    with _jax.named_scope("forward"):
        args = {**rest, TWIN_DIFF_INPUT: diff, **{k: w.astype(_WEIGHT_DTYPES[k]) for k, w in weights.items()}}
        y = _forward(args)
    with _jax.named_scope("loss_head"):
        err = _jnp.square(y.astype(_jnp.float32) - loss_target)
        return 0.5 * _jnp.sum(_jnp.mean(err, axis=-1)) if err.ndim else 0.5 * err


def _adamw(w, g, m, v):
    m = ADAM_B1 * m + (1.0 - ADAM_B1) * g
    v = ADAM_B2 * v + (1.0 - ADAM_B2) * _jnp.square(g)
    m_hat = m / (1.0 - ADAM_B1 ** ADAM_STEP)
    v_hat = v / (1.0 - ADAM_B2 ** ADAM_STEP)
    delta = -ADAM_LR * (m_hat / (_jnp.sqrt(v_hat) + ADAM_EPS) + ADAM_WD * w)
    return delta, m, v


def reference(x, c, ctx, c_ctx, ada_w, ada_b, norm_g, w_in, hg_lb, hg_norm_g, rw_mu, rw_w0, rw_w2, rw_a0, rw_a2, rw_kk, rw_ka, rw_rk, rw_gn_g, rw_gn_b, w_hg_out, w_rw_out, w_out, final_g, loss_target, m_c_ctx, m_ada_w, m_ada_b, m_norm_g, m_w_in, m_hg_lb, m_hg_norm_g, m_rw_mu, m_rw_w0, m_rw_w2, m_rw_a0, m_rw_a2, m_rw_kk, m_rw_ka, m_rw_rk, m_rw_gn_g, m_rw_gn_b, m_w_hg_out, m_w_rw_out, m_w_out, m_final_g, v_c_ctx, v_ada_w, v_ada_b, v_norm_g, v_w_in, v_hg_lb, v_hg_norm_g, v_rw_mu, v_rw_w0, v_rw_w2, v_rw_a0, v_rw_a2, v_rw_kk, v_rw_ka, v_rw_rk, v_rw_gn_g, v_rw_gn_b, v_w_hg_out, v_w_rw_out, v_w_out, v_final_g):
    given = dict(x=x, c=c, ctx=ctx, c_ctx=c_ctx, ada_w=ada_w, ada_b=ada_b, norm_g=norm_g, w_in=w_in, hg_lb=hg_lb, hg_norm_g=hg_norm_g, rw_mu=rw_mu, rw_w0=rw_w0, rw_w2=rw_w2, rw_a0=rw_a0, rw_a2=rw_a2, rw_kk=rw_kk, rw_ka=rw_ka, rw_rk=rw_rk, rw_gn_g=rw_gn_g, rw_gn_b=rw_gn_b, w_hg_out=w_hg_out, w_rw_out=w_rw_out, w_out=w_out, final_g=final_g, loss_target=loss_target, m_c_ctx=m_c_ctx, m_ada_w=m_ada_w, m_ada_b=m_ada_b, m_norm_g=m_norm_g, m_w_in=m_w_in, m_hg_lb=m_hg_lb, m_hg_norm_g=m_hg_norm_g, m_rw_mu=m_rw_mu, m_rw_w0=m_rw_w0, m_rw_w2=m_rw_w2, m_rw_a0=m_rw_a0, m_rw_a2=m_rw_a2, m_rw_kk=m_rw_kk, m_rw_ka=m_rw_ka, m_rw_rk=m_rw_rk, m_rw_gn_g=m_rw_gn_g, m_rw_gn_b=m_rw_gn_b, m_w_hg_out=m_w_hg_out, m_w_rw_out=m_w_rw_out, m_w_out=m_w_out, m_final_g=m_final_g, v_c_ctx=v_c_ctx, v_ada_w=v_ada_w, v_ada_b=v_ada_b, v_norm_g=v_norm_g, v_w_in=v_w_in, v_hg_lb=v_hg_lb, v_hg_norm_g=v_hg_norm_g, v_rw_mu=v_rw_mu, v_rw_w0=v_rw_w0, v_rw_w2=v_rw_w2, v_rw_a0=v_rw_a0, v_rw_a2=v_rw_a2, v_rw_kk=v_rw_kk, v_rw_ka=v_rw_ka, v_rw_rk=v_rw_rk, v_rw_gn_g=v_rw_gn_g, v_rw_gn_b=v_rw_gn_b, v_w_hg_out=v_w_hg_out, v_w_rw_out=v_w_rw_out, v_w_out=v_w_out, v_final_g=v_final_g)
    weights = {n: given[n] for n in TWIN_WEIGHTS}
    shared = {n: given[n] for n in SHARED_INPUTS}
    per_example = {n: given[n] for n in ['x', 'c', 'ctx']}
    grad_fn = _jax.value_and_grad(_loss, argnums=(0, 1))

    def one_microbatch(ex, loss_target):
        ex = dict(ex)
        diff = ex.pop(TWIN_DIFF_INPUT)
        return grad_fn(weights, diff, {**shared, **ex}, loss_target)

    if N_MICROBATCH == 1:
        loss, (grad_w, grad_x) = one_microbatch(per_example, given["loss_target"])
    else:
        def body(carry, xs):
            loss_sum, grad_sum = carry
            l_k, (gw_k, gx_k) = one_microbatch(xs[0], xs[1])
            with _jax.named_scope("update"):
                return (loss_sum + l_k, _jax.tree.map(_jnp.add, grad_sum, gw_k)), gx_k

        init = (_jnp.zeros((), _jnp.float32), _jax.tree.map(_jnp.zeros_like, weights))
        (loss, grad_w), grad_x = _jax.lax.scan(body, init, (per_example, given["loss_target"]))
    with _jax.named_scope("update"):
        delta_w, new_m, new_v = {}, {}, {}
        for n in TWIN_WEIGHTS:
            delta_w[n], new_m[n], new_v[n] = _adamw(weights[n], grad_w[n], given["m_" + n], given["v_" + n])
    return (loss, grad_x, *[grad_w[n] for n in TWIN_WEIGHTS], *[delta_w[n] for n in TWIN_WEIGHTS],
            *[new_m[n] for n in TWIN_WEIGHTS], *[new_v[n] for n in TWIN_WEIGHTS])
```

```python
import functools

import jax
import jax.numpy as jnp
from jax import lax
from jax.experimental import pallas as pl
from jax.experimental.pallas import tpu as pltpu

D = 2048
T = 2048
TC = 256
GW = 64
C = 64
HGW = 1024
RWW = 1024
LORA = 64
NDEV = 8
TM = 256
TMV = 128
NORM_EPS = 1e-6
RW_GN_EPS = 64e-5
ADAM_LR, ADAM_B1, ADAM_B2, ADAM_EPS, ADAM_WD, ADAM_STEP = 0.001, 0.9, 0.999, 1e-08, 0.01, 10
WB = 256
V7X_VMEM_LIMIT = 56 * 1024 * 1024

F32 = jnp.float32
BF16 = jnp.bfloat16
HI = lax.Precision.HIGHEST
AXES = ("x", "y", "c")


def _dims():
    tt = T + TC
    sh = 3 * RWW + 4 * LORA
    hgc = 5 * HGW
    return dict(TT=tt, NCC=TC // C, NCH=tt // C, HG_H=HGW // 128, RW_P=RWW // 128, SH=sh, HGC=hgc,
                NCOLS=hgc + sh + RWW + 2 * D, NCT=TC // TM, NCTV=TC // TMV)


def _mm(a, b):
    return jnp.dot(a, b, precision=HI, preferred_element_type=F32)


def _mm_nt(a, b):
    return lax.dot_general(a, b, (((1,), (1,)), ((), ())), precision=HI, preferred_element_type=F32)


def _mm_tn(a, b):
    return lax.dot_general(a, b, (((0,), (0,)), ((), ())), precision=HI, preferred_element_type=F32)


def _silu(x):
    return x * jax.nn.sigmoid(x)


def _softplus(x):
    return jnp.maximum(x, 0.0) + jnp.log(1.0 + jnp.exp(-jnp.abs(x)))


def _iota2(shape, dim):
    return lax.broadcasted_iota(jnp.int32, shape, dim)


def _pair_mask():
    return (_iota2((128, 128), 0) < 64) == (_iota2((128, 128), 1) < 64)


def _seg64_sum(x):
    e = _pair_mask().astype(F32)
    parts = [_mm(x[:, g * 128:(g + 1) * 128], e) for g in range(x.shape[1] // 128)]
    return parts[0] if len(parts) == 1 else jnp.concatenate(parts, axis=1)


def _seg128_mean(x):
    parts = [jnp.broadcast_to(jnp.mean(x[:, g * 128:(g + 1) * 128], axis=1, keepdims=True), (x.shape[0], 128))
             for g in range(x.shape[1] // 128)]
    return parts[0] if len(parts) == 1 else jnp.concatenate(parts, axis=1)


def _scan_masks(d):
    lag = (_iota2((C, C), 0) - _iota2((C, C), 1)) * (1 - 2 * d)
    return lag >= 0, lag > 0


def _rw_chunk(s0, r, lw, kap, b, v, kd, d):
    inc, strict = _scan_masks(d)
    lane = _iota2((1, 128), 1)
    masks = ((lane < 64).astype(F32), (lane >= 64).astype(F32))
    lc = _mm(inc.astype(F32), lw)
    ltot = jnp.sum(lw, axis=0, keepdims=True)
    rt = r * jnp.exp(lc)
    kt = kap * jnp.exp(lc - lw)
    einv = jnp.exp(-lc)
    kh, bh = kd * einv, b * einv
    eend = jnp.exp(ltot - lc)
    kbar, bbar = kd * eend, b * eend
    lhs = jnp.concatenate([kt * masks[0], kt * masks[1], rt * masks[0], rt * masks[1]], axis=0)
    gk = _mm_nt(lhs, kh)
    gb = _mm_nt(lhs, bh)
    ks = _mm_nt(kt, s0)
    y = _mm_nt(rt, s0)
    eye = (_iota2((C, C), 0) == _iota2((C, C), 1)).astype(F32)
    u = jnp.zeros_like(v)
    for h in range(2):
        akk = jnp.where(strict, gk[h * C:(h + 1) * C], 0.0)
        akb = jnp.where(strict, gb[h * C:(h + 1) * C], 0.0)
        ark = jnp.where(inc, gk[(2 + h) * C:(3 + h) * C], 0.0)
        arb = jnp.where(inc, gb[(2 + h) * C:(3 + h) * C], 0.0)
        vh = v * masks[h]
        rhs = ks * masks[h] + _mm(akk, vh)
        p = -akb
        tm = eye + p
        n = 2
        while n < C:
            p = _mm(p, p)
            tm = tm + _mm(tm, p)
            n *= 2
        uh = _mm(tm, rhs)
        u = u + uh
        y = y + _mm(ark, vh) - _mm(arb, uh)
    upd = _mm_tn(jnp.concatenate([v, -u], axis=0), jnp.concatenate([kbar, bbar], axis=0))
    s1 = s0 * jnp.exp(ltot) + jnp.where(_pair_mask(), upd, 0.0)
    return y, s1


HG_SUB = 16


def _hg_chunk(s0, qr, iv, f, lb, d):
    inc, _ = _scan_masks(d)
    q = _silu(qr)
    fg = lb + (1.0 - lb) * jax.nn.sigmoid(f)
    k = 1.0 - fg
    g = jnp.log(fg)
    bc = _mm(inc.astype(F32), g)
    btot = jnp.sum(g, axis=0, keepdims=True)
    o_inter = _mm_nt(q * jnp.exp(bc), s0)
    rowi = _iota2((C, 1), 0)
    excl = bc - g
    outs = []
    for blk in range(C // HG_SUB):
        lo, hi = blk * HG_SUB, (blk + 1) * HG_SUB
        first = jnp.where(d == 0, lo, hi - 1)
        ref = jnp.sum(jnp.where(rowi == first, excl, 0.0), axis=0, keepdims=True)
        qi = q[lo:hi] * jnp.exp(bc[lo:hi] - ref)
        src = jnp.where(d == 0, hi - 1 - rowi, rowi - lo) >= 0
        ke = jnp.where(src, k * jnp.exp(jnp.where(src, ref - bc, 0.0)), 0.0)
        a = jnp.where(inc[lo:hi], _mm_nt(qi, ke), 0.0)
        outs.append(_mm(a, iv))
    o = o_inter + jnp.concatenate(outs, axis=0)
    s1 = s0 * jnp.exp(btot) + _mm_tn(iv, k * jnp.exp(btot - bc))
    return o, s1


def _h_math(x, ng, scale, shift):
    return x * lax.rsqrt(jnp.mean(x * x, axis=-1, keepdims=True) + NORM_EPS) * ng * (1.0 + scale) + shift


def _hg_post_math(o0, o1, z, g):
    o = o0 + o1
    on = o * lax.rsqrt(_seg128_mean(o * o) + NORM_EPS)
    return on * g * _silu(z)


def _rw_prep_math(k, pw0, pw1, pa0, pa1, w0, a0, kkp, kap_p):
    kk = k * kkp
    kap = kk * lax.rsqrt(_seg64_sum(kk * kk) + 1e-12)
    outs = []
    for d, (pw, pa) in enumerate(((pw0, pa0), (pw1, pa1))):
        w_log = -_softplus(-(w0[d:d + 1] + pw)) - 0.5
        lw = -jnp.exp(w_log)
        a = jax.nn.sigmoid(a0[d:d + 1] + pa)
        kd = k * (1.0 + (a - 1.0) * kap_p)
        outs += [lw, kap * a, kd]
    return (kap, *outs)


def _rw_post_math(y0, y1, kd0, kd1, r, v, z, rk, gng, gnb):
    ys, ksum = y0 + y1, kd0 + kd1
    mean = _seg64_sum(ys) * (1.0 / 64.0)
    cen = ys - mean
    var = _seg64_sum(cen * cen) * (1.0 / 64.0)
    yn = cen * lax.rsqrt(var + RW_GN_EPS) * gng + gnb
    bonus = _seg64_sum(r * ksum * rk) * v
    return (yn + bonus) * _silu(z)


def _head_math(x, mo, gate, fg, tgt):
    xo = x + gate * mo
    y = xo * lax.rsqrt(jnp.mean(xo * xo, axis=-1, keepdims=True) + NORM_EPS) * fg
    err = y - tgt
    return 0.5 * jnp.sum(jnp.mean(err * err, axis=-1, keepdims=True), axis=0, keepdims=True)


def _gate_math(ghg, grw, a, b):
    return jax.nn.sigmoid(ghg) * a + jax.nn.sigmoid(grw) * b


def _call(body, name, grid, in_specs, out_specs, out_shape, scratch=(), sem=None, aliases=None):
    return pl.pallas_call(
        body, name=name, grid=grid, in_specs=in_specs, out_specs=out_specs, out_shape=out_shape,
        scratch_shapes=list(scratch), input_output_aliases=aliases or {},
        compiler_params=pltpu.CompilerParams(dimension_semantics=sem, vmem_limit_bytes=V7X_VMEM_LIMIT))


def _full(shape):
    nd = len(shape)
    return pl.BlockSpec(shape, lambda *_: (0,) * nd)


def _sds(shape, dtype=F32):
    return jax.ShapeDtypeStruct(shape, dtype)


def _proj_nt(h, wt, wt_blk0, nblk, row0, nrows, name):
    rc = min(TM, nrows)

    def body(h_ref, w_ref, o_ref):
        w = w_ref[...]

        def step(i, carry):
            r = pl.multiple_of(i * rc, rc)
            o_ref[pl.ds(r, rc), :] = lax.dot_general(h_ref[pl.ds(row0 + r, rc), :], w, (((1,), (1,)), ((), ())),
                                                     preferred_element_type=F32)
            return carry

        lax.fori_loop(0, nrows // rc, step, 0)

    return _call(body, name, (nblk,), [_full(h.shape), pl.BlockSpec((WB, wt.shape[1]), lambda j: (wt_blk0 + j, 0))],
                 pl.BlockSpec((nrows, WB), lambda j: (0, j)), _sds((nrows, nblk * WB)), sem=("parallel",))(h, wt)


def _dw_tn(a, b, row0, out_rows, out_blk0, name, prev=None):
    m, n = a.shape
    k2 = b.shape[1]
    nblk = n // WB
    rc = min(TM, m)

    def body(*refs):
        a_ref, b_ref, o_ref, acc_ref = refs[0], refs[1], refs[-2], refs[-1]
        acc_ref[...] = jnp.zeros_like(acc_ref)

        def step(i, carry):
            r = pl.multiple_of(i * rc, rc)
            acc_ref[...] += lax.dot_general(a_ref[pl.ds(r, rc), :], b_ref[pl.ds(row0 + r, rc), :], (((0,), (0,)), ((), ())),
                                            preferred_element_type=F32)
            return carry

        lax.fori_loop(0, m // rc, step, 0)
        o_ref[...] = acc_ref[...].astype(BF16)

    in_specs = [pl.BlockSpec((m, WB), lambda j: (0, j)), _full(b.shape)]
    args = [a, b]
    aliases = None
    if prev is not None:
        in_specs.append(pl.BlockSpec(memory_space=pl.ANY))
        args.append(prev)
        aliases = {2: 0}
    return _call(body, name, (nblk,), in_specs, pl.BlockSpec((WB, k2), lambda j: (out_blk0 + j, 0)), _sds((out_rows, k2), BF16),
                 scratch=[pltpu.VMEM((WB, k2), F32)], sem=("arbitrary",), aliases=aliases)(*args)


def _mm_nn(a, w, w_blk0, name):
    kb = WB
    m, kc = a.shape
    n = w.shape[1]
    tmb = 3 * TM if m % (3 * TM) == 0 else (2 * TM if m % (2 * TM) == 0 else TM)

    def body(a_ref, w_ref, o_ref):
        @pl.when(pl.program_id(1) == 0)
        def _():
            o_ref[...] = jnp.zeros_like(o_ref)

        o_ref[...] += jnp.dot(a_ref[...], w_ref[...], preferred_element_type=F32)

    return _call(body, name, (m // tmb, kc // kb),
                 [pl.BlockSpec((tmb, kb), lambda i, k: (i, k)), pl.BlockSpec((kb, n), lambda i, k: (w_blk0 + k, 0))],
                 pl.BlockSpec((tmb, n), lambda i, k: (i, 0)), _sds((m, n)), sem=("parallel", "arbitrary"))(a, w)


def _mods_fwd(cc, ada_t, ada_b):
    tn = 512 if (3 * D) % 512 == 0 else 256

    def body(cc_ref, w_ref, b_ref, o_ref):
        s = _silu(cc_ref[...]).astype(BF16)
        o_ref[...] = lax.dot_general(s, w_ref[...], (((1,), (1,)), ((), ())), preferred_element_type=F32) + b_ref[...]

    return _call(body, "mods_fwd", (3 * D // tn,),
                 [_full((8, D)), pl.BlockSpec((tn, D), lambda j: (j, 0)), pl.BlockSpec((1, tn), lambda j: (0, j))],
                 pl.BlockSpec((8, tn), lambda j: (0, j)), _sds((8, 3 * D)), sem=("parallel",))(cc, ada_t, ada_b)


def _sel_mod(mods_ref, is_ctx, lo):
    return jnp.where(is_ctx, mods_ref[1:2, lo:lo + D], mods_ref[0:1, lo:lo + D])


def _h_fwd(xcat, mods, norm_g):
    dm = _dims()

    def body(x_ref, m_ref, g_ref, o_ref):
        is_ctx = pl.program_id(0) < dm["NCT"]
        o_ref[...] = _h_math(x_ref[...], g_ref[...], _sel_mod(m_ref, is_ctx, D), _sel_mod(m_ref, is_ctx, 0)).astype(BF16)

    return _call(body, "h_fwd", (dm["TT"] // TM,),
                 [pl.BlockSpec((TM, D), lambda i: (i, 0)), _full((8, 3 * D)), _full((1, D))],
                 pl.BlockSpec((TM, D), lambda i: (i, 0)), _sds((dm["TT"], D), BF16), sem=("parallel",))(xcat, mods, norm_g)


def _chunk_of(d, s):
    dm = _dims()
    ncc, nch = dm["NCC"], dm["NCH"]
    return jnp.where(d == 0, s, jnp.where(s < ncc, ncc - 1 - s, nch - 1 - (s - ncc)))


def _hg_scan_fwd(p_hg, lb3):
    dm = _dims()
    nh, nch, tt = dm["HG_H"], dm["NCH"], dm["TT"]

    def body(q_ref, i_ref, f_ref, lb_ref, o_ref, ck_ref, s_ref):
        d = pl.program_id(0)

        @pl.when(pl.program_id(2) == 0)
        def _():
            s_ref[...] = jnp.zeros_like(s_ref)

        s0 = s_ref[...]
        ck_ref[...] = s0
        o, s1 = _hg_chunk(s0, q_ref[...], i_ref[...], f_ref[...], lb_ref[...], d)
        o_ref[...] = o
        s_ref[...] = s1

    blk = lambda off: pl.BlockSpec((C, 128), lambda d, h, s: (_chunk_of(d, s), off * nh + h))
    return _call(body, "hg_scan_fwd", (2, nh, nch),
                 [blk(0), blk(1), pl.BlockSpec((C, 128), lambda d, h, s: (_chunk_of(d, s), (2 + d) * nh + h)),
                  pl.BlockSpec((None, 1, 128), lambda d, h, s: (d, 0, h))],
                 [pl.BlockSpec((None, C, 128), lambda d, h, s: (d, _chunk_of(d, s), h)),
                  pl.BlockSpec((None, None, None, 128, 128), lambda d, h, s: (d, h, s, 0, 0))],
                 [_sds((2, tt, HGW)), _sds((2, nh, nch, 128, 128))],
                 scratch=[pltpu.VMEM((128, 128), F32)], sem=("parallel", "parallel", "arbitrary"))(p_hg, p_hg, p_hg, lb3)


def _hg_scan_bwd(p_hg, lb3, ck, do):
    dm = _dims()
    nh, nch, tt = dm["HG_H"], dm["NCH"], dm["TT"]

    def body(q_ref, i_ref, f_ref, lb_ref, ck_ref, do_ref, dq_ref, di_ref, df_ref, dlb_ref, ds_ref):
        d = pl.program_id(0)

        @pl.when(pl.program_id(2) == 0)
        def _():
            ds_ref[...] = jnp.zeros_like(ds_ref)
            dlb_ref[...] = jnp.zeros_like(dlb_ref)

        _, vjp = jax.vjp(lambda s0, q, i, f, lb: _hg_chunk(s0, q, i, f, lb, d),
                         ck_ref[...], q_ref[...], i_ref[...], f_ref[...], lb_ref[...])
        ds0, dq, di, df, dlb = vjp((do_ref[...], ds_ref[...]))
        dq_ref[...] = dq
        di_ref[...] = di
        df_ref[...] = df
        dlb_ref[...] += dlb
        ds_ref[...] = ds0

    ch = lambda d, s: _chunk_of(d, nch - 1 - s)
    blk = lambda off: pl.BlockSpec((C, 128), lambda d, h, s: (ch(d, s), off * nh + h))
    oblk = pl.BlockSpec((None, C, 128), lambda d, h, s: (d, ch(d, s), h))
    return _call(body, "hg_scan_bwd", (2, nh, nch),
                 [blk(0), blk(1), pl.BlockSpec((C, 128), lambda d, h, s: (ch(d, s), (2 + d) * nh + h)),
                  pl.BlockSpec((None, 1, 128), lambda d, h, s: (d, 0, h)),
                  pl.BlockSpec((None, None, None, 128, 128), lambda d, h, s: (d, h, nch - 1 - s, 0, 0)),
                  pl.BlockSpec((C, 128), lambda d, h, s: (ch(d, s), h))],
                 [oblk, oblk, oblk, pl.BlockSpec((None, 1, 128), lambda d, h, s: (d, 0, h))],
                 [_sds((2, tt, HGW)), _sds((2, tt, HGW)), _sds((2, tt, HGW)), _sds((2, 1, HGW))],
                 scratch=[pltpu.VMEM((128, 128), F32)], sem=("parallel", "parallel", "arbitrary"))(p_hg, p_hg, p_hg, lb3, ck, do)


def _rw_scan_fwd(sh, kap, lw, b, kd):
    dm = _dims()
    npair, nch, tt = dm["RW_P"], dm["NCH"], dm["TT"]

    def body(r_ref, v_ref, kap_ref, lw_ref, b_ref, kd_ref, y_ref, ck_ref, s_ref):
        d = pl.program_id(0)

        @pl.when(pl.program_id(2) == 0)
        def _():
            s_ref[...] = jnp.zeros_like(s_ref)

        s0 = s_ref[...]
        ck_ref[...] = s0
        y, s1 = _rw_chunk(s0, r_ref[...], lw_ref[...], kap_ref[...], b_ref[...], v_ref[...], kd_ref[...], d)
        y_ref[...] = y
        s_ref[...] = s1

    two = lambda off: pl.BlockSpec((C, 128), lambda d, p, s: (_chunk_of(d, s), off * npair + p))
    three = pl.BlockSpec((None, C, 128), lambda d, p, s: (d, _chunk_of(d, s), p))
    return _call(body, "rw_scan_fwd", (2, npair, nch), [two(0), two(2), two(0), three, three, three],
                 [three, pl.BlockSpec((None, None, None, 128, 128), lambda d, p, s: (d, p, s, 0, 0))],
                 [_sds((2, tt, RWW)), _sds((2, npair, nch, 128, 128))],
                 scratch=[pltpu.VMEM((128, 128), F32)], sem=("parallel", "parallel", "arbitrary"))(sh, sh, kap, lw, b, kd)


def _rw_scan_bwd(sh, kap, lw, b, kd, ck, dy):
    dm = _dims()
    npair, nch, tt = dm["RW_P"], dm["NCH"], dm["TT"]

    def body(r_ref, v_ref, kap_ref, lw_ref, b_ref, kd_ref, ck_ref, dy_ref,
             dr_ref, dv_ref, dkap_ref, dlw_ref, db_ref, dkd_ref, ds_ref):
        d = pl.program_id(0)

        @pl.when(pl.program_id(2) == 0)
        def _():
            ds_ref[...] = jnp.zeros_like(ds_ref)

        _, vjp = jax.vjp(lambda s0, r, lw_, kap_, b_, v, kd_: _rw_chunk(s0, r, lw_, kap_, b_, v, kd_, d),
                         ck_ref[...], r_ref[...], lw_ref[...], kap_ref[...], b_ref[...], v_ref[...], kd_ref[...])
        ds0, dr, dlw, dkap, db, dv, dkd = vjp((dy_ref[...], ds_ref[...]))
        dr_ref[...] = dr
        dv_ref[...] = dv
        dkap_ref[...] = dkap
        dlw_ref[...] = dlw
        db_ref[...] = db
        dkd_ref[...] = dkd
        ds_ref[...] = ds0

    ch = lambda d, s: _chunk_of(d, nch - 1 - s)
    two = lambda off: pl.BlockSpec((C, 128), lambda d, p, s: (ch(d, s), off * npair + p))
    three = pl.BlockSpec((None, C, 128), lambda d, p, s: (d, ch(d, s), p))
    return _call(body, "rw_scan_bwd", (2, npair, nch),
                 [two(0), two(2), two(0), three, three, three,
                  pl.BlockSpec((None, None, None, 128, 128), lambda d, p, s: (d, p, nch - 1 - s, 0, 0)), two(0)],
                 [three] * 6, [_sds((2, tt, RWW))] * 6,
                 scratch=[pltpu.VMEM((128, 128), F32)], sem=("parallel", "parallel", "arbitrary"))(sh, sh, kap, lw, b, kd, ck, dy)


def _tile_pos(i):
    dm = _dims()
    is_ctx = i < dm["NCT"]
    rows = lax.broadcasted_iota(jnp.int32, (TM, 1), 0)
    tl = rows + (i - dm["NCT"]) * TM
    width = jnp.where(is_ctx, TC, GW)
    colp = jnp.where(is_ctx, rows, tl) % width
    return is_ctx, rows, tl, colp, width


def _shift_lr(cur, i):
    _, _, _, colp, width = _tile_pos(i)
    left = jnp.where(colp == 0, 0.0, pltpu.roll(cur, 1, 0))
    right = jnp.where(colp == width - 1, 0.0, pltpu.roll(cur, TM - 1, 0))
    return left, right


def _shift_ud(cur, prv, nxt, i):
    is_ctx, rows, tl, _, _ = _tile_pos(i)
    if TM > GW:
        up = jnp.where(rows >= GW, pltpu.roll(cur, GW, 0), pltpu.roll(prv, GW, 0))
        down = jnp.where(rows < TM - GW, pltpu.roll(cur, TM - GW, 0), pltpu.roll(nxt, TM - GW, 0))
    else:
        up, down = prv, nxt
    up = jnp.where(jnp.logical_or(is_ctx, tl < GW), 0.0, up)
    down = jnp.where(jnp.logical_or(is_ctx, tl >= T - GW), 0.0, down)
    return up, down


def _shift_specs():
    dm = _dims()
    nt = dm["TT"] // TM
    cw = 256 if dm["SH"] % 256 == 0 else 128
    cur = pl.BlockSpec((TM, cw), lambda j, i: (i, j))
    prv = pl.BlockSpec((TM, cw), lambda j, i: (jnp.maximum(i - 1, 0), j))
    nxt = pl.BlockSpec((TM, cw), lambda j, i: (jnp.minimum(i + 1, nt - 1), j))
    mu = pl.BlockSpec((4, cw), lambda j, i: (0, j))
    return nt, cw, cur, prv, nxt, mu


def _shift_fwd(p_rw, mu):
    dm = _dims()
    nt, cw, cur, prv, nxt, mus = _shift_specs()

    def body(c_ref, p_ref, n_ref, mu_ref, o_ref):
        i = pl.program_id(1)
        p, m = c_ref[...], mu_ref[...]
        left, right = _shift_lr(p, i)
        up, down = _shift_ud(p, p_ref[...], n_ref[...], i)
        vert = jnp.where(i < dm["NCT"], 0.0, 1.0)
        o_ref[...] = (p * (1.0 - m[0:1] - m[1:2] - vert * (m[2:3] + m[3:4]))
                      + m[0:1] * left + m[1:2] * right + m[2:3] * up + m[3:4] * down)

    return _call(body, "shift_fwd", (dm["SH"] // cw, nt), [cur, prv, nxt, mus], cur, _sds((dm["TT"], dm["SH"])),
                 sem=("parallel", "parallel"))(p_rw, p_rw, p_rw, mu)


def _shift_bwd(p_rw, dsh, mu):
    dm = _dims()
    nt, cw, cur, prv, nxt, mus = _shift_specs()

    def body(c_ref, p_ref, n_ref, gc_ref, gp_ref, gn_ref, mu_ref, dp_ref, dmu_ref):
        i = pl.program_id(1)
        p, g, m = c_ref[...], gc_ref[...], mu_ref[...]
        vert = jnp.where(i < dm["NCT"], 0.0, 1.0)
        _, from_right = _shift_lr(m[0:1] * g, i)
        from_left, _ = _shift_lr(m[1:2] * g, i)
        _, from_down = _shift_ud(m[2:3] * g, g, m[2:3] * gn_ref[...], i)
        from_up, _ = _shift_ud(m[3:4] * g, m[3:4] * gp_ref[...], g, i)
        dp = g * (1.0 - m[0:1] - m[1:2] - vert * (m[2:3] + m[3:4])) + from_right + from_left + from_down + from_up
        dp_ref[...] = dp.astype(BF16)

        @pl.when(i == 0)
        def _():
            dmu_ref[...] = jnp.zeros_like(dmu_ref)

        left, right = _shift_lr(p, i)
        up, down = _shift_ud(p, p_ref[...], n_ref[...], i)
        s = lambda a: jnp.sum(a, axis=0, keepdims=True)
        dmu_ref[...] += jnp.concatenate([s(g * (left - p)), s(g * (right - p)), vert * s(g * (up - p)), vert * s(g * (down - p))], axis=0)

    return _call(body, "shift_bwd", (dm["SH"] // cw, nt), [cur, prv, nxt, cur, prv, nxt, mus],
                 [cur, pl.BlockSpec((4, cw), lambda j, i: (0, j))], [_sds((dm["TT"], dm["SH"]), BF16), _sds((4, dm["SH"]))],
                 sem=("parallel", "arbitrary"))(p_rw, p_rw, p_rw, dsh, dsh, dsh, mu)


def _acc_out(ref, first, vals):
    @pl.when(first)
    def _():
        for r in ref:
            r[...] = jnp.zeros_like(r)

    for r, v in zip(ref, vals):
        r[...] += v


def _rowsum(a):
    return jnp.sum(a, axis=0, keepdims=True)


def _rw_prep_specs():
    dm = _dims()
    rw = RWW // 128
    tok = lambda width, cb: pl.BlockSpec((TMV, width), lambda i: (i, cb))
    ins = [pl.BlockSpec((TMV, RWW), lambda i: (i, 1)), tok(128, 3 * rw), tok(128, 3 * rw + 1),
           _full((2, RWW)), _full((128, RWW)), _full((128, RWW)), _full((2, RWW)), _full((128, RWW)), _full((128, RWW)),
           _full((1, RWW)), _full((1, RWW))]
    return dm, ins


def _rw_prep_fwd(sh, w0, w2p0, w2p1, a0, a2p0, a2p1, kkp, kap_p):
    dm, ins = _rw_prep_specs()
    tt = dm["TT"]

    def body(k_ref, low_ref, loa_ref, w0_ref, w20_ref, w21_ref, a0_ref, a20_ref, a21_ref, kk_ref, ka_ref, kap_ref, lw_ref, b_ref, kd_ref):
        tw, la = jnp.tanh(low_ref[...]), loa_ref[...]
        kap, lw0, b0, kd0, lw1, b1, kd1 = _rw_prep_math(
            k_ref[...], _mm(tw, w20_ref[...]), _mm(tw, w21_ref[...]), _mm(la, a20_ref[...]), _mm(la, a21_ref[...]),
            w0_ref[...], a0_ref[...], kk_ref[...], ka_ref[...])
        kap_ref[...] = kap
        lw_ref[0], lw_ref[1] = lw0, lw1
        b_ref[0], b_ref[1] = b0, b1
        kd_ref[0], kd_ref[1] = kd0, kd1

    two = pl.BlockSpec((2, TMV, RWW), lambda i: (0, i, 0))
    return _call(body, "rw_prep_fwd", (tt // TMV,), ins, [pl.BlockSpec((TMV, RWW), lambda i: (i, 0)), two, two, two],
                 [_sds((tt, RWW)), _sds((2, tt, RWW)), _sds((2, tt, RWW)), _sds((2, tt, RWW))], sem=("parallel",))(
        sh, sh, sh, w0, w2p0, w2p1, a0, a2p0, a2p1, kkp, kap_p)


def _rw_prep_bwd(sh, w0, w2p0, w2p1, a0, a2p0, a2p1, kkp, kap_p, dkap2, dlw, db, dkd, dr2, dv2, dkd_p, dr_p, dv_p):
    dm, ins = _rw_prep_specs()
    tt, sh_w = dm["TT"], dm["SH"]
    two = pl.BlockSpec((2, TMV, RWW), lambda i: (0, i, 0))
    one = pl.BlockSpec((TMV, RWW), lambda i: (i, 0))

    def body(k_ref, low_ref, loa_ref, w0_ref, w20_ref, w21_ref, a0_ref, a20_ref, a21_ref, kk_ref, ka_ref,
             dkap_ref, dlw_ref, db_ref, dkd_ref, dr_ref, dv_ref, dkdp_ref, drp_ref, dvp_ref,
             dsh_ref, dw0_ref, dw20_ref, dw21_ref, da0_ref, da20_ref, da21_ref, dkk_ref, dka_ref):
        tw, la = jnp.tanh(low_ref[...]), loa_ref[...]
        w20, w21, a20, a21 = w20_ref[...], w21_ref[...], a20_ref[...], a21_ref[...]
        _, vjp = jax.vjp(_rw_prep_math, k_ref[...], _mm(tw, w20), _mm(tw, w21), _mm(la, a20), _mm(la, a21),
                         w0_ref[...], a0_ref[...], kk_ref[...], ka_ref[...])
        dkp = dkdp_ref[...]
        dk, dpw0, dpw1, dpa0, dpa1, dw0, da0, dkk, dka = vjp(
            (dkap_ref[0] + dkap_ref[1], dlw_ref[0], db_ref[0], dkd_ref[0] + dkp, dlw_ref[1], db_ref[1], dkd_ref[1] + dkp))
        twt, lat = tw.T, la.T
        dtw = _mm_nt(dpw0, w20) + _mm_nt(dpw1, w21)
        dsh_ref[:, 0:RWW] = dr_ref[0] + dr_ref[1] + drp_ref[...]
        dsh_ref[:, RWW:2 * RWW] = dk
        dsh_ref[:, 2 * RWW:3 * RWW] = dv_ref[0] + dv_ref[1] + dvp_ref[...]
        dsh_ref[:, 3 * RWW:3 * RWW + 128] = dtw * (1.0 - tw * tw)
        dsh_ref[:, 3 * RWW + 128:3 * RWW + 256] = _mm_nt(dpa0, a20) + _mm_nt(dpa1, a21)
        _acc_out((dw0_ref, dw20_ref, dw21_ref, da0_ref, da20_ref, da21_ref, dkk_ref, dka_ref), pl.program_id(0) == 0,
                 (dw0, _mm(twt, dpw0), _mm(twt, dpw1), da0, _mm(lat, dpa0), _mm(lat, dpa1), dkk, dka))

    par = [_full((2, RWW)), _full((128, RWW)), _full((128, RWW)), _full((2, RWW)), _full((128, RWW)), _full((128, RWW)),
           _full((1, RWW)), _full((1, RWW))]
    return _call(body, "rw_prep_bwd", (tt // TMV,), ins + [two] * 6 + [one] * 3,
                 [pl.BlockSpec((TMV, sh_w), lambda i: (i, 0))] + par,
                 [_sds((tt, sh_w)), _sds((2, RWW)), _sds((128, RWW)), _sds((128, RWW)), _sds((2, RWW)), _sds((128, RWW)), _sds((128, RWW)),
                  _sds((1, RWW)), _sds((1, RWW))],
                 sem=("arbitrary",))(sh, sh, sh, w0, w2p0, w2p1, a0, a2p0, a2p1, kkp, kap_p, dkap2, dlw, db, dkd, dr2, dv2, dkd_p, dr_p, dv_p)


def _rw_post_specs():
    dm = _dims()
    nv = dm["NCTV"]
    rw = RWW // 128
    lat3 = lambda d: pl.BlockSpec((None, TMV, RWW), lambda i: (d, nv + i, 0))
    ins = [lat3(0), lat3(1), lat3(0), lat3(1),
           pl.BlockSpec((TMV, RWW), lambda i: (nv + i, 0)), pl.BlockSpec((TMV, RWW), lambda i: (nv + i, 2)),
           pl.BlockSpec((TMV, RWW), lambda i: (i, 0)), _full((1, RWW)), _full((1, RWW)), _full((1, RWW))]
    return dm, nv, ins


def _rw_post_fwd(y, kd, sh, p_z, rk, gng, gnb):
    dm, nv, ins = _rw_post_specs()

    def body(y0, y1, k0, k1, r, v, z, rk_ref, g_ref, b_ref, o_ref):
        o_ref[...] = _rw_post_math(y0[...], y1[...], k0[...], k1[...], r[...], v[...], z[...], rk_ref[...], g_ref[...], b_ref[...]).astype(BF16)

    return _call(body, "rw_post_fwd", (T // TMV,), ins, pl.BlockSpec((TMV, RWW), lambda i: (i, 0)), _sds((T, RWW), BF16),
                 sem=("parallel",))(y, y, kd, kd, sh, sh, p_z, rk, gng, gnb)


def _rw_post_bwd(y, kd, sh, p_z, rk, gng, gnb, dout):
    dm, nv, _ = _rw_post_specs()
    tt = dm["TT"]
    cl = lambda i: jnp.maximum(i - nv, 0)
    all3 = lambda d: pl.BlockSpec((None, TMV, RWW), lambda i: (d, i, 0))
    ins = [all3(0), all3(1), all3(0), all3(1),
           pl.BlockSpec((TMV, RWW), lambda i: (i, 0)), pl.BlockSpec((TMV, RWW), lambda i: (i, 2)),
           pl.BlockSpec((TMV, RWW), lambda i: (cl(i), 0)), _full((1, RWW)), _full((1, RWW)), _full((1, RWW)),
           pl.BlockSpec((TMV, RWW), lambda i: (cl(i), 0))]
    tok = pl.BlockSpec((TMV, RWW), lambda i: (i, 0))

    def body(y0, y1, k0, k1, r, v, z, rk_ref, g_ref, b_ref, do_ref, dy_ref, dkd_ref, dr_ref, dv_ref, dz_ref, drk_ref, dg_ref, db_ref):
        i = pl.program_id(0)
        lat = jnp.where(i >= nv, 1.0, 0.0)
        _, vjp = jax.vjp(_rw_post_math, y0[...], y1[...], k0[...], k1[...], r[...], v[...], z[...], rk_ref[...], g_ref[...], b_ref[...])
        dy0, _, dk0, _, dr, dv, dz, drk, dg, db = vjp(do_ref[...] * lat)
        dy_ref[...] = dy0
        dkd_ref[...] = dk0
        dr_ref[...] = dr
        dv_ref[...] = dv
        dz_ref[...] = dz.astype(BF16)
        _acc_out((drk_ref, dg_ref, db_ref), i == 0, (drk, dg, db))

    return _call(body, "rw_post_bwd", (tt // TMV,), ins,
                 [tok, tok, tok, tok, pl.BlockSpec((TMV, RWW), lambda i: (cl(i), 0)), _full((1, RWW)), _full((1, RWW)), _full((1, RWW))],
                 [_sds((tt, RWW))] * 4 + [_sds((T, RWW), BF16), _sds((1, RWW)), _sds((1, RWW)), _sds((1, RWW))],
                 sem=("arbitrary",))(y, y, kd, kd, sh, sh, p_z, rk, gng, gnb, dout)


def _hg_post_fwd(o, p_hg, hg_g):
    dm = _dims()
    nv = dm["NCTV"]
    lat3 = lambda d: pl.BlockSpec((None, TMV, HGW), lambda i: (d, nv + i, 0))

    def body(o0, o1, z, g, out):
        out[...] = _hg_post_math(o0[...], o1[...], z[...], g[...]).astype(BF16)

    return _call(body, "hg_post_fwd", (T // TMV,), [lat3(0), lat3(1), pl.BlockSpec((TMV, HGW), lambda i: (nv + i, 4)), _full((1, HGW))],
                 pl.BlockSpec((TMV, HGW), lambda i: (i, 0)), _sds((T, HGW), BF16), sem=("parallel",))(o, o, p_hg, hg_g)


def _hg_post_bwd(o, p_hg, hg_g, dout):
    dm = _dims()
    nv, tt = dm["NCTV"], dm["TT"]
    all3 = lambda d: pl.BlockSpec((None, TMV, HGW), lambda i: (d, i, 0))
    tok = pl.BlockSpec((TMV, HGW), lambda i: (i, 0))

    def body(o0, o1, z, g, do_in, do_ref, dz_ref, dg_ref):
        i = pl.program_id(0)
        lat = jnp.where(i >= nv, 1.0, 0.0)
        _, vjp = jax.vjp(_hg_post_math, o0[...], o1[...], z[...], g[...])
        d0, _, dz, dg = vjp(do_in[...] * lat)
        do_ref[...] = d0
        dz_ref[...] = dz
        _acc_out((dg_ref,), i == 0, (dg,))

    return _call(body, "hg_post_bwd", (tt // TMV,),
                 [all3(0), all3(1), pl.BlockSpec((TMV, HGW), lambda i: (i, 4)), _full((1, HGW)),
                  pl.BlockSpec((TMV, HGW), lambda i: (jnp.maximum(i - nv, 0), 0))],
                 [tok, tok, _full((1, HGW))], [_sds((tt, HGW)), _sds((tt, HGW)), _sds((1, HGW))], sem=("arbitrary",))(o, o, p_hg, hg_g, dout)


def _hg_dproj(dq, di, df, dz):
    dm = _dims()
    tt = dm["TT"]
    three = lambda d: pl.BlockSpec((None, TMV, HGW), lambda i: (d, i, 0))

    def body(q0, q1, i0, i1, f0, f1, z, o_ref):
        o_ref[:, 0:HGW] = (q0[...] + q1[...]).astype(BF16)
        o_ref[:, HGW:2 * HGW] = (i0[...] + i1[...]).astype(BF16)
        o_ref[:, 2 * HGW:3 * HGW] = f0[...].astype(BF16)
        o_ref[:, 3 * HGW:4 * HGW] = f1[...].astype(BF16)
        o_ref[:, 4 * HGW:5 * HGW] = z[...].astype(BF16)

    return _call(body, "hg_dproj", (tt // TMV,), [three(0), three(1)] * 3 + [pl.BlockSpec((TMV, HGW), lambda i: (i, 0))],
                 pl.BlockSpec((TMV, 5 * HGW), lambda i: (i, 0)), _sds((tt, 5 * HGW), BF16), sem=("parallel",))(dq, dq, di, di, df, df, dz)


def _merge1_fwd(out_hg, out_rw, whg_t, wrw_t, p_g):
    tok = pl.BlockSpec((TM, D), lambda i: (i, 0))

    def body(h_ref, r_ref, wh_ref, wr_ref, g1_ref, g2_ref, a_ref, b_ref, m_ref):
        nt = (((1,), (1,)), ((), ()))
        a = lax.dot_general(h_ref[...], wh_ref[...], nt, preferred_element_type=F32)
        b = lax.dot_general(r_ref[...], wr_ref[...], nt, preferred_element_type=F32)
        a_ref[...] = a
        b_ref[...] = b
        m_ref[...] = _gate_math(g1_ref[...], g2_ref[...], a, b).astype(BF16)

    return _call(body, "merge1_fwd", (T // TM,),
                 [pl.BlockSpec((TM, HGW), lambda i: (i, 0)), pl.BlockSpec((TM, RWW), lambda i: (i, 0)), _full((D, HGW)), _full((D, RWW)),
                  pl.BlockSpec((TM, D), lambda i: (i, 0)), pl.BlockSpec((TM, D), lambda i: (i, 1))],
                 [tok, tok, tok], [_sds((T, D)), _sds((T, D)), _sds((T, D), BF16)], sem=("parallel",))(out_hg, out_rw, whg_t, wrw_t, p_g, p_g)


def _head_fwd_bwd(x, merged, w_out, mods, final_g, tgt):
    tok = pl.BlockSpec((TM, D), lambda i: (i, 0))

    def body(x_ref, m_ref, w_ref, mods_ref, fg_ref, t_ref, dxo_ref, dmo_ref, loss_ref, dgate_ref, dfg_ref):
        mo = jnp.dot(m_ref[...], w_ref[...], preferred_element_type=F32)
        gate = mods_ref[0:1, 2 * D:3 * D]
        loss, vjp = jax.vjp(lambda x_, mo_, g_, fg_: _head_math(x_, mo_, g_, fg_, t_ref[...]), x_ref[...], mo, gate, fg_ref[...])
        dx, dmo, dgate, dfg = vjp(jnp.ones((1, 1), F32))
        dxo_ref[...] = dx
        dmo_ref[...] = dmo.astype(BF16)
        _acc_out((loss_ref, dgate_ref, dfg_ref), pl.program_id(0) == 0, (jnp.broadcast_to(loss, (1, 128)), dgate, dfg))

    return _call(body, "head_fwd_bwd", (T // TM,), [tok, tok, _full((D, D)), _full((8, 3 * D)), _full((1, D)), tok],
                 [tok, tok, _full((1, 128)), _full((1, D)), _full((1, D))],
                 [_sds((T, D)), _sds((T, D), BF16), _sds((1, 128)), _sds((1, D)), _sds((1, D))], sem=("arbitrary",))(
        x, merged, w_out, mods, final_g, tgt)


def _merge_bwd(dmo, w_out, p_g, a, b):
    tok = pl.BlockSpec((TM, D), lambda i: (i, 0))

    def body(dmo_ref, w_ref, g1_ref, g2_ref, a_ref, b_ref, da_ref, db_ref, dg_ref):
        dm_ = lax.dot_general(dmo_ref[...], w_ref[...], (((1,), (1,)), ((), ())), preferred_element_type=F32)
        _, vjp = jax.vjp(_gate_math, g1_ref[...], g2_ref[...], a_ref[...], b_ref[...])
        dg1, dg2, da, db = vjp(dm_)
        da_ref[...] = da.astype(BF16)
        db_ref[...] = db.astype(BF16)
        dg_ref[:, 0:D] = dg1.astype(BF16)
        dg_ref[:, D:2 * D] = dg2.astype(BF16)

    return _call(body, "merge_bwd", (T // TM,),
                 [tok, _full((D, D)), pl.BlockSpec((TM, D), lambda i: (i, 0)), pl.BlockSpec((TM, D), lambda i: (i, 1)), tok, tok],
                 [tok, tok, pl.BlockSpec((TM, 2 * D), lambda i: (i, 0))], [_sds((T, D), BF16), _sds((T, D), BF16), _sds((T, 2 * D), BF16)],
                 sem=("parallel",))(dmo, w_out, p_g, p_g, a, b)


def _h_bwd(xcat, mods, norm_g, dh_hg, dh_rw, dh_z, dh_g, dxo, dgate):
    dm = _dims()
    nct, tt = dm["NCT"], dm["TT"]
    tok = pl.BlockSpec((TM, D), lambda i: (i, 0))
    lat = pl.BlockSpec((TM, D), lambda i: (jnp.maximum(i - nct, 0), 0))

    def body(x_ref, m_ref, g_ref, d1, d2, d3, d4, dxo_ref, dgate_ref, gx_ref, dng_ref, dmods_ref):
        i = pl.program_id(0)
        is_ctx = i < nct
        latf = jnp.where(is_ctx, 0.0, 1.0)
        dh = d1[...] + d2[...] + latf * (d3[...] + d4[...])
        _, vjp = jax.vjp(_h_math, x_ref[...], g_ref[...], _sel_mod(m_ref, is_ctx, D), _sel_mod(m_ref, is_ctx, 0))
        dx, dng, dscale, dshift = vjp(dh)
        gx_ref[...] = dx + dxo_ref[...]

        @pl.when(i == 0)
        def _():
            dng_ref[...] = jnp.zeros_like(dng_ref)
            dmods_ref[...] = jnp.zeros_like(dmods_ref)
            dmods_ref[0:1, 2 * D:3 * D] = dgate_ref[...]

        dng_ref[...] += dng
        row = lax.broadcasted_iota(jnp.int32, (8, 1), 0)
        sel = jnp.where(row == jnp.where(is_ctx, 1, 0), 1.0, 0.0)
        dmods_ref[:, 0:D] += sel * dshift
        dmods_ref[:, D:2 * D] += sel * dscale

    return _call(body, "h_bwd", (tt // TM,), [tok, _full((8, 3 * D)), _full((1, D)), tok, tok, lat, lat, lat, _full((1, D))],
                 [lat, _full((1, D)), _full((8, 3 * D))], [_sds((T, D)), _sds((1, D)), _sds((8, 3 * D))], sem=("arbitrary",))(
        xcat, mods, norm_g, dh_hg, dh_rw, dh_z, dh_g, dxo, dgate)


def _mods_bwd(cc, ada_t, dmods, dmods_t):
    tn = 512 if (3 * D) % 512 == 0 else 256

    nsteps = 3 * D // tn

    def body(cc_ref, w_ref, dm_ref, dmt_ref, dcc_ref, dw_ref, db_ref):
        c_ = cc_ref[...]
        s, vjp = jax.vjp(_silu, c_)
        dm_ = dm_ref[...]
        ds = jnp.dot(dm_.astype(BF16), w_ref[...], preferred_element_type=F32)
        _acc_out((dcc_ref,), pl.program_id(0) == 0, (ds,))

        @pl.when(pl.program_id(0) == nsteps - 1)
        def _():
            dcc_ref[...] = vjp(dcc_ref[...])[0]

        dw_ref[...] = _mm(dmt_ref[...], s).astype(BF16)
        db_ref[...] = dm_[0:1] + dm_[1:2]

    return _call(body, "mods_bwd", (nsteps,),
                 [_full((8, D)), pl.BlockSpec((tn, D), lambda j: (j, 0)), pl.BlockSpec((8, tn), lambda j: (0, j)),
                  pl.BlockSpec((tn, 8), lambda j: (j, 0))],
                 [_full((8, D)), pl.BlockSpec((tn, D), lambda j: (j, 0)), pl.BlockSpec((1, tn), lambda j: (0, j))],
                 [_sds((8, D)), _sds((3 * D, D), BF16), _sds((1, 3 * D))], sem=("arbitrary",))(cc, ada_t, dmods, dmods_t)


def _lb_math(l0, l1):
    return jax.nn.sigmoid(l0 - l1)


def _lb_fwd(l0, l1):
    def body(a_ref, b_ref, o_ref):
        o_ref[...] = _lb_math(a_ref[...], b_ref[...])

    return _call(body, "lb_fwd", (1,), [_full((2, HGW))] * 2, _full((2, HGW)), _sds((2, HGW)))(l0, l1)


def _lb_bwd(l0, l1, dlb):
    def body(a_ref, b_ref, d_ref, da_ref, db_ref):
        _, vjp = jax.vjp(_lb_math, a_ref[...], b_ref[...])
        da_ref[...], db_ref[...] = vjp(d_ref[...])

    return _call(body, "lb_bwd", (1,), [_full((2, HGW))] * 3, [_full((2, HGW))] * 2, [_sds((2, HGW))] * 2)(l0, l1, dlb)


def _local_step(x, c, ctx, tgt, w):
    dm = _dims()
    tt, sh_w, hgc = dm["TT"], dm["SH"], dm["HGC"]
    nb = lambda cols: cols // WB
    xcat = jnp.concatenate([ctx, x], axis=0)
    cc = jnp.concatenate([c, w["c_ctx"], jnp.zeros((6, D), F32)], axis=0)
    mods = _mods_fwd(cc, w["ada_t"], w["ada_b"])
    h = _h_fwd(xcat, mods, w["norm_g"])
    win_t = w["win_t"]
    p_hg = _proj_nt(h, win_t, 0, nb(hgc), 0, tt, "proj_hg")
    p_rw = _proj_nt(h, win_t, nb(hgc), nb(sh_w), 0, tt, "proj_rw")
    p_z = _proj_nt(h, win_t, nb(hgc + sh_w), nb(RWW), TC, T, "proj_z")
    p_g = _proj_nt(h, win_t, nb(hgc + sh_w + RWW), nb(2 * D), TC, T, "proj_g")

    lb3 = _lb_fwd(w["lb0"], w["lb1"]).reshape(2, 1, HGW)
    o, hg_ck = _hg_scan_fwd(p_hg, lb3)
    out_hg = _hg_post_fwd(o, p_hg, w["hg_g"])

    sh = _shift_fwd(p_rw, w["mu"])
    zpad = jnp.zeros((LORA, RWW), F32)
    prep_w = (w["w0"], jnp.concatenate([w["w2"][0], zpad], 0), jnp.concatenate([zpad, w["w2"][1]], 0),
              w["a0"], jnp.concatenate([w["a2"][0], zpad], 0), jnp.concatenate([zpad, w["a2"][1]], 0), w["kk"], w["ka"])
    kap, lw, b, kd = _rw_prep_fwd(sh, *prep_w)
    y, rw_ck = _rw_scan_fwd(sh, kap, lw, b, kd)
    post_w = (w["rk"], w["gng"], w["gnb"])
    out_rw = _rw_post_fwd(y, kd, sh, p_z, *post_w)

    a, bb, merged = _merge1_fwd(out_hg, out_rw, w["whg_t"], w["wrw_t"], p_g)
    dxo, dmo, loss, dgate, d_fg = _head_fwd_bwd(x, merged, w["wout"], mods, w["final_g"], tgt)
    da, db, dp_g = _merge_bwd(dmo, w["wout"], p_g, a, bb)
    g_wout = _dw_tn(merged, dmo, 0, D, 0, "dw_out")
    g_whg = _dw_tn(da, out_hg, 0, D, 0, "dw_hg")
    g_wrw = _dw_tn(db, out_rw, 0, D, 0, "dw_rw")
    d_out_hg = _mm_nn(da, w["whg_t"], 0, "dx_hg")
    d_out_rw = _mm_nn(db, w["wrw_t"], 0, "dx_rw")

    do, dz_hg, d_hg_g = _hg_post_bwd(o, p_hg, w["hg_g"], d_out_hg)
    dq, di, df, dlb3 = _hg_scan_bwd(p_hg, lb3, hg_ck, do)
    dp_hg = _hg_dproj(dq, di, df, dz_hg)
    d_lb0, d_lb1 = _lb_bwd(w["lb0"], w["lb1"], dlb3.reshape(2, HGW))

    dy, dkd_p, dr_p, dv_p, dp_z, d_rk, d_gng, d_gnb = _rw_post_bwd(y, kd, sh, p_z, *post_w, d_out_rw)
    dr2, dv2, dkap2, dlw, db2, dkd2 = _rw_scan_bwd(sh, kap, lw, b, kd, rw_ck, dy)
    dsh, d_w0, d_w2p0, d_w2p1, d_a0, d_a2p0, d_a2p1, d_kk, d_ka = _rw_prep_bwd(
        sh, *prep_w, dkap2, dlw, db2, dkd2, dr2, dv2, dkd_p, dr_p, dv_p)
    dp_rw, d_mu = _shift_bwd(p_rw, dsh, w["mu"])

    g_win = _dw_tn(dp_hg, h, 0, dm["NCOLS"], 0, "dw_in_hg")
    g_win = _dw_tn(dp_rw, h, 0, dm["NCOLS"], nb(hgc), "dw_in_rw", prev=g_win)
    g_win = _dw_tn(dp_z, h, TC, dm["NCOLS"], nb(hgc + sh_w), "dw_in_z", prev=g_win)
    g_win = _dw_tn(dp_g, h, TC, dm["NCOLS"], nb(hgc + sh_w + RWW), "dw_in_g", prev=g_win)
    dh_hg = _mm_nn(dp_hg, win_t, 0, "dh_hg")
    dh_rw = _mm_nn(dp_rw, win_t, nb(hgc), "dh_rw")
    dh_z = _mm_nn(dp_z, win_t, nb(hgc + sh_w), "dh_z")
    dh_g = _mm_nn(dp_g, win_t, nb(hgc + sh_w + RWW), "dh_g")
    gx, d_ng, dmods = _h_bwd(xcat, mods, w["norm_g"], dh_hg, dh_rw, dh_z, dh_g, dxo, dgate)
    dcc, g_ada, d_ada_b = _mods_bwd(cc, w["ada_t"], dmods, dmods.T)

    big = dict(win_t=g_win, wout=g_wout, whg_t=g_whg, wrw_t=g_wrw, ada_t=g_ada)
    small = dict(c_ctx=dcc[1:2], ada_b=d_ada_b, norm_g=d_ng, lb0=d_lb0, lb1=d_lb1, hg_g=d_hg_g, mu=d_mu, w0=d_w0,
                 w2=jnp.stack([d_w2p0[:LORA], d_w2p1[LORA:]]), a0=d_a0, a2=jnp.stack([d_a2p0[:LORA], d_a2p1[LORA:]]),
                 kk=d_kk, ka=d_ka, rk=d_rk, gng=d_gng, gnb=d_gnb, final_g=d_fg)
    return loss, gx, big, small


def _exchange(bufs, gather, name):
    nbuf = len(bufs)
    npeer = NDEV - 1

    def body(*refs):
        ins, outs = refs[:nbuf], refs[nbuf:2 * nbuf]
        send_sems, recv_sems, loc_sems = refs[2 * nbuf:]
        mx, my, mc = lax.axis_index("x"), lax.axis_index("y"), lax.axis_index("c")
        me = 4 * mx + 2 * my + mc
        started = []
        for bi in range(nbuf):
            own = pltpu.make_async_copy(ins[bi] if gather else ins[bi].at[me], outs[bi].at[me], loc_sems.at[bi])
            own.start()
            started.append((own, None))
            for k in range(1, NDEV):
                px = 1 - mx if k & 4 else mx
                py = 1 - my if k & 2 else my
                pc = 1 - mc if k & 1 else mc
                peer = 4 * px + 2 * py + pc
                sem = bi * npeer + k - 1
                send = pltpu.make_async_remote_copy(
                    src_ref=ins[bi] if gather else ins[bi].at[peer], dst_ref=outs[bi].at[me],
                    send_sem=send_sems.at[sem], recv_sem=recv_sems.at[sem], device_id=(px, py, pc), device_id_type=pl.DeviceIdType.MESH)
                send.start()
                recv = pltpu.make_async_remote_copy(
                    src_ref=ins[bi] if gather else ins[bi].at[me], dst_ref=outs[bi].at[peer],
                    send_sem=send_sems.at[sem], recv_sem=recv_sems.at[sem], device_id=(px, py, pc), device_id_type=pl.DeviceIdType.MESH)
                started.append((send, recv))
        for first, recv in started:
            if recv is None:
                first.wait()
            else:
                first.wait_send()
                recv.wait_recv()

    hbm = pl.BlockSpec(memory_space=pl.ANY)
    shapes = [_sds((NDEV,) + (b.shape if gather else b.shape[1:]), b.dtype) for b in bufs]
    return pl.pallas_call(
        body, name=name, in_specs=[hbm] * nbuf, out_specs=[hbm] * nbuf, out_shape=shapes,
        scratch_shapes=[pltpu.SemaphoreType.DMA((nbuf * npeer,)), pltpu.SemaphoreType.DMA((nbuf * npeer,)), pltpu.SemaphoreType.DMA((nbuf,))],
    )(*bufs)


def _row_tile(rows, mult, cap):
    best = mult
    for t in range(mult, cap + 1, mult):
        if rows % t == 0:
            best = t
    assert rows % best == 0, (rows, mult)
    return best


def _reduce8(r, name):
    _, rows, cols = r.shape
    tr = _row_tile(rows, 16 if r.dtype.itemsize == 2 else 8, 128)

    def body(r_ref, o_ref):
        acc = r_ref[0].astype(F32)
        for j in range(1, NDEV):
            acc = acc + r_ref[j].astype(F32)
        o_ref[...] = acc

    return _call(body, name, (rows // tr,), [pl.BlockSpec((NDEV, tr, cols), lambda i: (0, i, 0))],
                 pl.BlockSpec((tr, cols), lambda i: (i, 0)), _sds((rows, cols)), sem=("parallel",))(r)


def _adamw(w, g, m, v, name):
    rows, cols = w.shape
    tr = _row_tile(rows, 8, 128)
    c1 = 1.0 - ADAM_B1 ** ADAM_STEP
    c2 = 1.0 - ADAM_B2 ** ADAM_STEP

    def body(w_ref, g_ref, m_ref, v_ref, d_ref, mo_ref, vo_ref):
        g_ = g_ref[...]
        m_ = ADAM_B1 * m_ref[...] + (1.0 - ADAM_B1) * g_
        v_ = ADAM_B2 * v_ref[...] + (1.0 - ADAM_B2) * (g_ * g_)
        d_ref[...] = -ADAM_LR * ((m_ / c1) / (jnp.sqrt(v_ / c2) + ADAM_EPS) + ADAM_WD * w_ref[...])
        mo_ref[...] = m_
        vo_ref[...] = v_

    blk = pl.BlockSpec((tr, cols), lambda i: (i, 0))
    return _call(body, name, (rows // tr,), [blk] * 4, [blk] * 3, [_sds((rows, cols))] * 3, sem=("parallel",))(w, g, m, v)


SLAB_PART = 1024


def _pack(arrs):
    parts = []
    for a in arrs:
        flat = a.reshape(-1)
        pad = (-flat.shape[0]) % SLAB_PART
        if pad:
            flat = jnp.concatenate([flat, jnp.zeros((pad,), flat.dtype)])
        parts.append(flat.reshape(-1, 128))
    return jnp.concatenate(parts, axis=0)


def _unpack(slab, shapes):
    outs, row = [], 0
    for s in shapes:
        n = 1
        for d in s:
            n *= d
        rows = (n + SLAB_PART - 1) // SLAB_PART * (SLAB_PART // 128)
        outs.append(slab[row:row + rows].reshape(-1)[:n].reshape(s))
        row += rows
    return outs


def _unshard_last(g, lead):
    nl = len(lead)
    return jnp.transpose(g, tuple(range(1, nl + 1)) + (0, nl + 1)).reshape(tuple(lead) + (-1,))


def kernel(x, c, ctx, c_ctx, ada_w, ada_b, norm_g, w_in, hg_lb, hg_norm_g, rw_mu, rw_w0, rw_w2, rw_a0, rw_a2, rw_kk, rw_ka, rw_rk, rw_gn_g, rw_gn_b, w_hg_out, w_rw_out, w_out, final_g, loss_target, m_c_ctx, m_ada_w, m_ada_b, m_norm_g, m_w_in, m_hg_lb, m_hg_norm_g, m_rw_mu, m_rw_w0, m_rw_w2, m_rw_a0, m_rw_a2, m_rw_kk, m_rw_ka, m_rw_rk, m_rw_gn_g, m_rw_gn_b, m_w_hg_out, m_w_rw_out, m_w_out, m_final_g, v_c_ctx, v_ada_w, v_ada_b, v_norm_g, v_w_in, v_hg_lb, v_hg_norm_g, v_rw_mu, v_rw_w0, v_rw_w2, v_rw_a0, v_rw_a2, v_rw_kk, v_rw_ka, v_rw_rk, v_rw_gn_g, v_rw_gn_b, v_w_hg_out, v_w_rw_out, v_w_out, v_final_g):
    dm = _dims()
    me = 4 * lax.axis_index("x") + 2 * lax.axis_index("y") + lax.axis_index("c")

    big_shards = [w_in[0].T.astype(BF16), w_out[0].astype(BF16), w_hg_out[0].T.astype(BF16), w_rw_out[0].T.astype(BF16),
                  ada_w[0].T.astype(BF16)]
    sharded_small = [hg_lb, rw_mu[0], rw_w0[0], rw_w2[0], rw_a0[0], rw_a2[0]]
    gathered = _exchange(big_shards + [_pack(sharded_small)], True, "gather_weights")
    win_t, wout, whg_t, wrw_t, ada_t = [g.reshape(-1, g.shape[2]) for g in gathered[:5]]
    per_dev = jax.vmap(lambda s: _unpack(s, [a.shape for a in sharded_small]))(gathered[5])
    hg_lb_f, mu_f, w0_f, w2_f, a0_f, a2_f = [_unshard_last(p, p.shape[1:-1]) for p in per_dev]

    w = dict(win_t=win_t, wout=wout, whg_t=whg_t, wrw_t=wrw_t, ada_t=ada_t, c_ctx=c_ctx[None], ada_b=ada_b, norm_g=norm_g,
             lb0=hg_lb_f[:, 0], lb1=hg_lb_f[:, 1], hg_g=hg_norm_g, mu=mu_f, w0=w0_f, w2=w2_f, a0=a0_f, a2=a2_f,
             kk=rw_kk, ka=rw_ka, rk=rw_rk, gng=rw_gn_g, gnb=rw_gn_b, final_g=final_g[None])
    loss_dev, grad_x, big, small = _local_step(x[0], c, ctx[0], loss_target[0], w)
    loss = lax.psum(loss_dev[0, 0], AXES)

    small_names = ["c_ctx", "ada_b", "norm_g", "lb0", "lb1", "hg_g", "mu", "w0", "w2", "a0", "a2", "kk", "ka", "rk", "gng", "gnb", "final_g"]
    small_slab = _pack([small[n] for n in small_names])
    big_names = ["win_t", "wout", "whg_t", "wrw_t", "ada_t"]
    recv = _exchange([big[n].reshape(NDEV, -1, big[n].shape[1]) for n in big_names], False, "scatter_grads")
    small_all = _exchange([small_slab], True, "gather_small_grads")[0]
    g_win_t, g_wout, g_whg_t, g_wrw_t, g_ada_t = [_reduce8(r, "reduce_" + n) for r, n in zip(recv, big_names)]
    sg = dict(zip(small_names, _unpack(_reduce8(small_all, "reduce_small"), [small[n].shape for n in small_names])))

    def my_shard(full):
        n = full.shape[-1] // NDEV
        return lax.dynamic_slice_in_dim(full, me * n, n, axis=full.ndim - 1)

    grads = dict(
        c_ctx=sg["c_ctx"][0], ada_w=g_ada_t.T[None], ada_b=sg["ada_b"], norm_g=sg["norm_g"], w_in=g_win_t.T[None],
        hg_lb=my_shard(jnp.stack([sg["lb0"], sg["lb1"]], axis=1)), hg_norm_g=sg["hg_g"], rw_mu=my_shard(sg["mu"])[None],
        rw_w0=my_shard(sg["w0"])[None], rw_w2=my_shard(sg["w2"])[None], rw_a0=my_shard(sg["a0"])[None], rw_a2=my_shard(sg["a2"])[None],
        rw_kk=sg["kk"], rw_ka=sg["ka"], rw_rk=sg["rk"], rw_gn_g=sg["gng"], rw_gn_b=sg["gnb"],
        w_hg_out=g_whg_t.T[None], w_rw_out=g_wrw_t.T[None], w_out=g_wout[None], final_g=sg["final_g"][0])
    weights = dict(c_ctx=c_ctx, ada_w=ada_w, ada_b=ada_b, norm_g=norm_g, w_in=w_in, hg_lb=hg_lb, hg_norm_g=hg_norm_g, rw_mu=rw_mu,
                   rw_w0=rw_w0, rw_w2=rw_w2, rw_a0=rw_a0, rw_a2=rw_a2, rw_kk=rw_kk, rw_ka=rw_ka, rw_rk=rw_rk, rw_gn_g=rw_gn_g,
                   rw_gn_b=rw_gn_b, w_hg_out=w_hg_out, w_rw_out=w_rw_out, w_out=w_out, final_g=final_g)
    m_in = dict(zip(weights, (m_c_ctx, m_ada_w, m_ada_b, m_norm_g, m_w_in, m_hg_lb, m_hg_norm_g, m_rw_mu, m_rw_w0, m_rw_w2, m_rw_a0,
                              m_rw_a2, m_rw_kk, m_rw_ka, m_rw_rk, m_rw_gn_g, m_rw_gn_b, m_w_hg_out, m_w_rw_out, m_w_out, m_final_g)))
    v_in = dict(zip(weights, (v_c_ctx, v_ada_w, v_ada_b, v_norm_g, v_w_in, v_hg_lb, v_hg_norm_g, v_rw_mu, v_rw_w0, v_rw_w2, v_rw_a0,
                              v_rw_a2, v_rw_kk, v_rw_ka, v_rw_rk, v_rw_gn_g, v_rw_gn_b, v_w_hg_out, v_w_rw_out, v_w_out, v_final_g)))

    big_w = ["ada_w", "w_in", "w_hg_out", "w_rw_out", "w_out"]
    delta, new_m, new_v = {}, {}, {}
    for n in big_w:
        shp = weights[n].shape
        two = lambda a: a.reshape(shp[-2], shp[-1])
        d_, m_, v_ = _adamw(two(weights[n]), two(grads[n]), two(m_in[n]), two(v_in[n]), "adamw_" + n)
        delta[n], new_m[n], new_v[n] = d_.reshape(shp), m_.reshape(shp), v_.reshape(shp)
    rest = [n for n in weights if n not in big_w]
    shapes = [weights[n].shape for n in rest]
    d_s, m_s, v_s = _adamw(_pack([weights[n] for n in rest]), _pack([grads[n] for n in rest]), _pack([m_in[n] for n in rest]),
                           _pack([v_in[n] for n in rest]), "adamw_small")
    for n, d_, m_, v_ in zip(rest, _unpack(d_s, shapes), _unpack(m_s, shapes), _unpack(v_s, shapes)):
        delta[n], new_m[n], new_v[n] = d_, m_, v_

    order = list(weights)
    return (loss, grad_x[None], *[grads[n] for n in order], *[delta[n] for n in order],
            *[new_m[n] for n in order], *[new_v[n] for n in order])
```

```python
import functools

import jax
import jax.numpy as jnp
from jax import lax
from jax.experimental import pallas as pl
from jax.experimental.pallas import tpu as pltpu

D = 2048
T = 2048
TC = 256
GW = 64
C = 64
HGW = 1024
RWW = 1024
LORA = 64
NDEV = 8
TM = 256
TMV = 128
NORM_EPS = 1e-6
RW_GN_EPS = 64e-5
ADAM_LR, ADAM_B1, ADAM_B2, ADAM_EPS, ADAM_WD, ADAM_STEP = 0.001, 0.9, 0.999, 1e-08, 0.01, 10
WB = 256
V7X_VMEM_LIMIT = 56 * 1024 * 1024

F32 = jnp.float32
BF16 = jnp.bfloat16
AXES = ("x", "y", "c")


def _dims():
    tt = T + TC
    sh = 3 * RWW + 4 * LORA
    hgc = 5 * HGW
    return dict(TT=tt, NCC=TC // C, NCH=tt // C, HG_H=HGW // 128, RW_P=RWW // 128, SH=sh, HGC=hgc,
                NCOLS=hgc + sh + RWW + 2 * D, NCT=TC // TM, NCTV=TC // TMV)


def _mm(a, b):
    return _dot3(a, b, "nn")


def _mm_nt(a, b):
    return _dot3(a, b, "nt")


def _mm_tn(a, b):
    return _dot3(a, b, "tn")


def _split(x):
    hi = x.astype(BF16)
    return hi, (x - hi.astype(F32)).astype(BF16)


_FORMS = {"nn": (((1,), (0,)), ((), ())), "nt": (((1,), (1,)), ((), ())), "tn": (((0,), (0,)), ((), ()))}


def _dot3_raw(a, b, form):
    ah, al = _split(a)
    bh, bl = _split(b)
    d = lambda x, y: lax.dot_general(x, y, _FORMS[form], preferred_element_type=F32)
    return d(ah, bh) + (d(ah, bl) + d(al, bh))


@functools.partial(jax.custom_vjp, nondiff_argnums=(2,))
def _dot3(a, b, form):
    return _dot3_raw(a, b, form)


def _dot3_fwd(a, b, form):
    return _dot3_raw(a, b, form), (a, b)


def _dot3_bwd(form, res, g):
    a, b = res
    if form == "nn":
        return _dot3(g, b, "nt"), _dot3(a, g, "tn")
    if form == "nt":
        return _dot3(g, b, "nn"), _dot3(g, a, "tn")
    return _dot3(b, g, "nt"), _dot3(a, g, "nn")


_dot3.defvjp(_dot3_fwd, _dot3_bwd)


def _scan_cumsum(inc, x):
    return _cumsum_vjp(inc.astype(BF16), x)


def _cumsum_raw(inc, x, form):
    h1 = x.astype(BF16)
    r1 = x - h1.astype(F32)
    h2 = r1.astype(BF16)
    h3 = (r1 - h2.astype(F32)).astype(BF16)
    d = lambda y: lax.dot_general(inc, y, _FORMS[form], preferred_element_type=F32)
    return d(h1) + (d(h2) + d(h3))


@jax.custom_vjp
def _cumsum_vjp(inc, x):
    return _cumsum_raw(inc, x, "nn")


_cumsum_vjp.defvjp(lambda inc, x: (_cumsum_raw(inc, x, "nn"), inc),
                   lambda inc, g: (jnp.zeros_like(inc), _cumsum_raw(inc, g, "tn")))


def _silu(x):
    return x * jax.nn.sigmoid(x)


def _softplus(x):
    return jnp.maximum(x, 0.0) + jnp.log(1.0 + jnp.exp(-jnp.abs(x)))


def _iota2(shape, dim):
    return lax.broadcasted_iota(jnp.int32, shape, dim)


def _pair_mask():
    return (_iota2((128, 128), 0) < 64) == (_iota2((128, 128), 1) < 64)


def _seg64_sum(x):
    e = _pair_mask().astype(F32)
    parts = [_mm(x[:, g * 128:(g + 1) * 128], e) for g in range(x.shape[1] // 128)]
    return parts[0] if len(parts) == 1 else jnp.concatenate(parts, axis=1)


def _seg128_mean(x):
    parts = [jnp.broadcast_to(jnp.mean(x[:, g * 128:(g + 1) * 128], axis=1, keepdims=True), (x.shape[0], 128))
             for g in range(x.shape[1] // 128)]
    return parts[0] if len(parts) == 1 else jnp.concatenate(parts, axis=1)


def _scan_masks(d):
    lag = (_iota2((C, C), 0) - _iota2((C, C), 1)) * (1 - 2 * d)
    return lag >= 0, lag > 0


def _rw_chunks(chains):
    s0, r, lw, kap, b, v, kd, ds = zip(*chains)
    ids = range(len(chains))
    inc, strict = zip(*[_scan_masks(d) for d in ds])
    lane = _iota2((1, 128), 1)
    hm = ((lane < 64).astype(F32), (lane >= 64).astype(F32))
    eye = (_iota2((C, C), 0) == _iota2((C, C), 1)).astype(F32)
    lc = [_scan_cumsum(inc[i], lw[i]) for i in ids]
    ltot = [jnp.sum(lw[i], axis=0, keepdims=True) for i in ids]
    rt = [r[i] * jnp.exp(lc[i]) for i in ids]
    kt = [kap[i] * jnp.exp(lc[i] - lw[i]) for i in ids]
    einv = [jnp.exp(-lc[i]) for i in ids]
    kh = [kd[i] * einv[i] for i in ids]
    bh = [b[i] * einv[i] for i in ids]
    eend = [jnp.exp(ltot[i] - lc[i]) for i in ids]
    kbar = [kd[i] * eend[i] for i in ids]
    bbar = [b[i] * eend[i] for i in ids]
    lhs = [jnp.concatenate([kt[i] * hm[0], kt[i] * hm[1], rt[i] * hm[0], rt[i] * hm[1]], axis=0) for i in ids]
    gk = [_mm_nt(lhs[i], kh[i]) for i in ids]
    gb = [_mm_nt(lhs[i], bh[i]) for i in ids]
    ks = [_mm_nt(kt[i], s0[i]) for i in ids]
    ys = [_mm_nt(rt[i], s0[i]) for i in ids]
    heads = [(i, h) for i in ids for h in range(2)]
    akk = {c: jnp.where(strict[c[0]], gk[c[0]][c[1] * C:(c[1] + 1) * C], 0.0) for c in heads}
    akb = {c: jnp.where(strict[c[0]], gb[c[0]][c[1] * C:(c[1] + 1) * C], 0.0) for c in heads}
    ark = {c: jnp.where(inc[c[0]], gk[c[0]][(2 + c[1]) * C:(3 + c[1]) * C], 0.0) for c in heads}
    arb = {c: jnp.where(inc[c[0]], gb[c[0]][(2 + c[1]) * C:(3 + c[1]) * C], 0.0) for c in heads}
    vh = {c: v[c[0]] * hm[c[1]] for c in heads}
    av = {c: _mm(akk[c], vh[c]) for c in heads}
    rhs = {c: ks[c[0]] * hm[c[1]] + av[c] for c in heads}
    p = {c: -akb[c] for c in heads}
    tm = {c: eye + p[c] for c in heads}
    n = 2
    while n < C:
        p = {c: _mm(p[c], p[c]) for c in heads}
        tm = {c: tm[c] + _mm(tm[c], p[c]) for c in heads}
        n *= 2
    uh = {c: _mm(tm[c], rhs[c]) for c in heads}
    yv = {c: _mm(ark[c], vh[c]) for c in heads}
    yu = {c: _mm(arb[c], uh[c]) for c in heads}
    u = [uh[i, 0] + uh[i, 1] for i in ids]
    y = [ys[i] + (yv[i, 0] - yu[i, 0]) + (yv[i, 1] - yu[i, 1]) for i in ids]
    upd = [_mm_tn(jnp.concatenate([v[i], -u[i]], axis=0), jnp.concatenate([kbar[i], bbar[i]], axis=0)) for i in ids]
    s1 = [s0[i] * jnp.exp(ltot[i]) + jnp.where(_pair_mask(), upd[i], 0.0) for i in ids]
    return [(y[i], s1[i]) for i in ids]


HG_SUB = 16


def _hg_chunks(chains):
    s0, qr, iv, f, lb, ds = zip(*chains)
    ids = range(len(chains))
    inc = [_scan_masks(d)[0] for d in ds]
    q = [_silu(qr[i]) for i in ids]
    fg = [lb[i] + (1.0 - lb[i]) * jax.nn.sigmoid(f[i]) for i in ids]
    k = [1.0 - fg[i] for i in ids]
    g = [jnp.log(fg[i]) for i in ids]
    bc = [_scan_cumsum(inc[i], g[i]) for i in ids]
    btot = [jnp.sum(g[i], axis=0, keepdims=True) for i in ids]
    o_inter = [_mm_nt(q[i] * jnp.exp(bc[i]), s0[i]) for i in ids]
    rowi = _iota2((C, 1), 0)
    outs = [[] for _ in ids]
    for blk in range(C // HG_SUB):
        lo, hi = blk * HG_SUB, (blk + 1) * HG_SUB
        first = [lo if ds[i] == 0 else hi - 1 for i in ids]
        ref = [jnp.sum(jnp.where(rowi == first[i], bc[i] - g[i], 0.0), axis=0, keepdims=True) for i in ids]
        qi = [q[i][lo:hi] * jnp.exp(bc[i][lo:hi] - ref[i]) for i in ids]
        src = [(rowi < hi) if ds[i] == 0 else (rowi >= lo) for i in ids]
        ke = [jnp.where(src[i], k[i] * jnp.exp(jnp.where(src[i], ref[i] - bc[i], 0.0)), 0.0) for i in ids]
        a = [jnp.where(inc[i][lo:hi], _mm_nt(qi[i], ke[i]), 0.0) for i in ids]
        part = [_mm(a[i], iv[i]) for i in ids]
        for i in ids:
            outs[i].append(part[i])
    o = [o_inter[i] + jnp.concatenate(outs[i], axis=0) for i in ids]
    upd = [_mm_tn(iv[i], k[i] * jnp.exp(btot[i] - bc[i])) for i in ids]
    s1 = [s0[i] * jnp.exp(btot[i]) + upd[i] for i in ids]
    return [(o[i], s1[i]) for i in ids]


def _lockstep(fn, ds, nargs):
    def flat_fn(*flat):
        return tuple(fn([tuple(flat[i * nargs:(i + 1) * nargs]) + (d,) for i, d in enumerate(ds)]))

    return flat_fn


def _h_math(x, ng, scale, shift):
    return x * lax.rsqrt(jnp.mean(x * x, axis=-1, keepdims=True) + NORM_EPS) * ng * (1.0 + scale) + shift


def _hg_post_math(o0, o1, z, g):
    o = o0 + o1
    on = o * lax.rsqrt(_seg128_mean(o * o) + NORM_EPS)
    return on * g * _silu(z)


def _rw_prep_math(k, pw0, pw1, pa0, pa1, w0, a0, kkp, kap_p):
    kk = k * kkp
    kap = kk * lax.rsqrt(_seg64_sum(kk * kk) + 1e-12)
    outs = []
    for d, (pw, pa) in enumerate(((pw0, pa0), (pw1, pa1))):
        w_log = -_softplus(-(w0[d:d + 1] + pw)) - 0.5
        lw = -jnp.exp(w_log)
        a = jax.nn.sigmoid(a0[d:d + 1] + pa)
        kd = k * (1.0 + (a - 1.0) * kap_p)
        outs += [lw, kap * a, kd]
    return (kap, *outs)


def _rw_post_math(y0, y1, kd0, kd1, r, v, z, rk, gng, gnb):
    ys, ksum = y0 + y1, kd0 + kd1
    mean = _seg64_sum(ys) * (1.0 / 64.0)
    cen = ys - mean
    var = _seg64_sum(cen * cen) * (1.0 / 64.0)
    yn = cen * lax.rsqrt(var + RW_GN_EPS) * gng + gnb
    bonus = _seg64_sum(r * ksum * rk) * v
    return (yn + bonus) * _silu(z)


def _head_math(x, mo, gate, fg, tgt):
    xo = x + gate * mo
    y = xo * lax.rsqrt(jnp.mean(xo * xo, axis=-1, keepdims=True) + NORM_EPS) * fg
    err = y - tgt
    return 0.5 * jnp.sum(jnp.mean(err * err, axis=-1, keepdims=True), axis=0, keepdims=True)


def _gate_math(ghg, grw, a, b):
    return jax.nn.sigmoid(ghg) * a + jax.nn.sigmoid(grw) * b


def _call(body, name, grid, in_specs, out_specs, out_shape, scratch=(), sem=None, aliases=None):
    return pl.pallas_call(
        body, name=name, grid=grid, in_specs=in_specs, out_specs=out_specs, out_shape=out_shape,
        scratch_shapes=list(scratch), input_output_aliases=aliases or {},
        compiler_params=pltpu.CompilerParams(dimension_semantics=sem, vmem_limit_bytes=V7X_VMEM_LIMIT))


def _full(shape):
    nd = len(shape)
    return pl.BlockSpec(shape, lambda *_: (0,) * nd)


def _sds(shape, dtype=F32):
    return jax.ShapeDtypeStruct(shape, dtype)


def _proj_nt(h, wt, wt_blk0, nblk, row0, nrows, name):
    rc = min(TM, nrows)

    def body(h_ref, w_ref, o_ref):
        w = w_ref[...]

        def step(i, carry):
            r = pl.multiple_of(i * rc, rc)
            o_ref[pl.ds(r, rc), :] = lax.dot_general(h_ref[pl.ds(row0 + r, rc), :], w, (((1,), (1,)), ((), ())),
                                                     preferred_element_type=F32)
            return carry

        lax.fori_loop(0, nrows // rc, step, 0)

    return _call(body, name, (nblk,), [_full(h.shape), pl.BlockSpec((WB, wt.shape[1]), lambda j: (wt_blk0 + j, 0))],
                 pl.BlockSpec((nrows, WB), lambda j: (0, j)), _sds((nrows, nblk * WB)), sem=("parallel",))(h, wt)


def _dw_tn(a, b, row0, out_rows, out_blk0, name, prev=None):
    m, n = a.shape
    k2 = b.shape[1]
    nblk = n // WB
    rc = min(TM, m)

    def body(*refs):
        a_ref, b_ref, o_ref, acc_ref = refs[0], refs[1], refs[-2], refs[-1]
        acc_ref[...] = jnp.zeros_like(acc_ref)

        def step(i, carry):
            r = pl.multiple_of(i * rc, rc)
            acc_ref[...] += lax.dot_general(a_ref[pl.ds(r, rc), :], b_ref[pl.ds(row0 + r, rc), :], (((0,), (0,)), ((), ())),
                                            preferred_element_type=F32)
            return carry

        lax.fori_loop(0, m // rc, step, 0)
        o_ref[...] = acc_ref[...].astype(BF16)

    in_specs = [pl.BlockSpec((m, WB), lambda j: (0, j)), _full(b.shape)]
    args = [a, b]
    aliases = None
    if prev is not None:
        in_specs.append(pl.BlockSpec(memory_space=pl.ANY))
        args.append(prev)
        aliases = {2: 0}
    return _call(body, name, (nblk,), in_specs, pl.BlockSpec((WB, k2), lambda j: (out_blk0 + j, 0)), _sds((out_rows, k2), BF16),
                 scratch=[pltpu.VMEM((WB, k2), F32)], sem=("arbitrary",), aliases=aliases)(*args)


def _mm_nn(a, w, w_blk0, name):
    kb = WB
    m, kc = a.shape
    n = w.shape[1]
    tmb = 3 * TM if m % (3 * TM) == 0 else (2 * TM if m % (2 * TM) == 0 else TM)

    def body(a_ref, w_ref, o_ref):
        @pl.when(pl.program_id(1) == 0)
        def _():
            o_ref[...] = jnp.zeros_like(o_ref)

        o_ref[...] += jnp.dot(a_ref[...], w_ref[...], preferred_element_type=F32)

    return _call(body, name, (m // tmb, kc // kb),
                 [pl.BlockSpec((tmb, kb), lambda i, k: (i, k)), pl.BlockSpec((kb, n), lambda i, k: (w_blk0 + k, 0))],
                 pl.BlockSpec((tmb, n), lambda i, k: (i, 0)), _sds((m, n)), sem=("parallel", "arbitrary"))(a, w)


def _mods_fwd(cc, ada_t, ada_b):
    tn = 512 if (3 * D) % 512 == 0 else 256

    def body(cc_ref, w_ref, b_ref, o_ref):
        s = _silu(cc_ref[...]).astype(BF16)
        o_ref[...] = lax.dot_general(s, w_ref[...], (((1,), (1,)), ((), ())), preferred_element_type=F32) + b_ref[...]

    return _call(body, "mods_fwd", (3 * D // tn,),
                 [_full((8, D)), pl.BlockSpec((tn, D), lambda j: (j, 0)), pl.BlockSpec((1, tn), lambda j: (0, j))],
                 pl.BlockSpec((8, tn), lambda j: (0, j)), _sds((8, 3 * D)), sem=("parallel",))(cc, ada_t, ada_b)


def _sel_mod(mods_ref, is_ctx, lo):
    return jnp.where(is_ctx, mods_ref[1:2, lo:lo + D], mods_ref[0:1, lo:lo + D])


def _h_fwd(xcat, mods, norm_g):
    dm = _dims()

    def body(x_ref, m_ref, g_ref, o_ref):
        is_ctx = pl.program_id(0) < dm["NCT"]
        o_ref[...] = _h_math(x_ref[...], g_ref[...], _sel_mod(m_ref, is_ctx, D), _sel_mod(m_ref, is_ctx, 0)).astype(BF16)

    return _call(body, "h_fwd", (dm["TT"] // TM,),
                 [pl.BlockSpec((TM, D), lambda i: (i, 0)), _full((8, 3 * D)), _full((1, D))],
                 pl.BlockSpec((TM, D), lambda i: (i, 0)), _sds((dm["TT"], D), BF16), sem=("parallel",))(xcat, mods, norm_g)


def _chunk_of(d, s):
    dm = _dims()
    ncc, nch = dm["NCC"], dm["NCH"]
    return s if d == 0 else jnp.where(s < ncc, ncc - 1 - s, nch - 1 - (s - ncc))


HG_GROUP = 4
RW_GROUP = 2


def _dir_specs(make):
    return [make(d) for d in range(2)]


def _lanes(j):
    return slice(j * 128, (j + 1) * 128)


def _hg_scan_fwd(p_hg, lb3):
    dm = _dims()
    nh, nch, tt = dm["HG_H"], dm["NCH"], dm["TT"]
    pp = min(HG_GROUP, nh)
    lw_ = 128 * pp

    def body(q0, q1, i0, i1, f0, f1, lb0, lb1, o0, o1, ck_ref, s_ref):
        @pl.when(pl.program_id(1) == 0)
        def _():
            s_ref[...] = jnp.zeros_like(s_ref)

        keys, chains = [], []
        for d, refs in enumerate(((q0, i0, f0, lb0), (q1, i1, f1, lb1))):
            vals = [r[...] for r in refs]
            for j in range(pp):
                keys.append((d, j))
                chains.append((s_ref[d, j], *[a[:, _lanes(j)] for a in vals], d))
        for (d, j), chain, (out, s1) in zip(keys, chains, _hg_chunks(chains)):
            ck_ref[d, j] = chain[0]
            (o0, o1)[d][:, _lanes(j)] = out
            s_ref[d, j] = s1

    blk = lambda off: _dir_specs(lambda d: pl.BlockSpec((C, lw_), lambda h, s: (_chunk_of(d, s), off * nh // pp + h)))
    return _call(body, "hg_scan_fwd", (nh // pp, nch),
                 blk(0) + blk(1) + _dir_specs(lambda d: pl.BlockSpec((C, lw_), lambda h, s: (_chunk_of(d, s), (2 + d) * nh // pp + h)))
                 + _dir_specs(lambda d: pl.BlockSpec((None, 1, lw_), lambda h, s: (d, 0, h))),
                 _dir_specs(lambda d: pl.BlockSpec((C, lw_), lambda h, s: (_chunk_of(d, s), h)))
                 + [pl.BlockSpec((2, pp, None, 128, 128), lambda h, s: (0, h, s, 0, 0))],
                 [_sds((tt, HGW)), _sds((tt, HGW)), _sds((2, nh, nch, 128, 128))],
                 scratch=[pltpu.VMEM((2, pp, 128, 128), F32)], sem=("parallel", "arbitrary"))(*([p_hg] * 6), lb3, lb3)


def _hg_scan_bwd(p_hg, lb3, ck, do):
    dm = _dims()
    nh, nch, tt = dm["HG_H"], dm["NCH"], dm["TT"]
    pp = min(HG_GROUP, nh)
    lw_ = 128 * pp

    def body(q0, q1, i0, i1, f0, f1, lb0, lb1, ck_ref, do0, do1, dq0, dq1, di0, di1, df0, df1, dlb_ref, ds_ref):
        @pl.when(pl.program_id(1) == 0)
        def _():
            ds_ref[...] = jnp.zeros_like(ds_ref)
            dlb_ref[...] = jnp.zeros_like(dlb_ref)

        keys, flat, cts = [], [], []
        for d, refs in enumerate(((q0, i0, f0, lb0, do0), (q1, i1, f1, lb1, do1))):
            *vals, do_ = [r[...] for r in refs]
            for j in range(pp):
                keys.append((d, j))
                flat += [ck_ref[d, j], *[a[:, _lanes(j)] for a in vals]]
                cts.append((do_[:, _lanes(j)], ds_ref[d, j]))
        _, vjp = jax.vjp(_lockstep(_hg_chunks, [d for d, _ in keys], 5), *flat)
        grads = vjp(tuple(cts))
        for n, (d, j) in enumerate(keys):
            ds0, dq_, di_, df_, dlb = grads[5 * n:5 * n + 5]
            (dq0, dq1)[d][:, _lanes(j)] = dq_
            (di0, di1)[d][:, _lanes(j)] = di_
            (df0, df1)[d][:, _lanes(j)] = df_
            dlb_ref[d, :, _lanes(j)] += dlb
            ds_ref[d, j] = ds0

    ch = lambda d, s: _chunk_of(d, nch - 1 - s)
    blk = lambda off: _dir_specs(lambda d: pl.BlockSpec((C, lw_), lambda h, s: (ch(d, s), off * nh // pp + h)))
    tok = _dir_specs(lambda d: pl.BlockSpec((C, lw_), lambda h, s: (ch(d, s), h)))
    return _call(body, "hg_scan_bwd", (nh // pp, nch),
                 blk(0) + blk(1) + _dir_specs(lambda d: pl.BlockSpec((C, lw_), lambda h, s: (ch(d, s), (2 + d) * nh // pp + h)))
                 + _dir_specs(lambda d: pl.BlockSpec((None, 1, lw_), lambda h, s: (d, 0, h)))
                 + [pl.BlockSpec((2, pp, None, 128, 128), lambda h, s: (0, h, nch - 1 - s, 0, 0))] + tok,
                 tok * 3 + [pl.BlockSpec((2, 1, lw_), lambda h, s: (0, 0, h))],
                 [_sds((tt, HGW))] * 6 + [_sds((2, 1, HGW))],
                 scratch=[pltpu.VMEM((2, pp, 128, 128), F32)], sem=("parallel", "arbitrary"))(*([p_hg] * 6), lb3, lb3, ck, do, do)


def _rw_scan_fwd(sh, kap, lw, b, kd):
    dm = _dims()
    npair, nch, tt = dm["RW_P"], dm["NCH"], dm["TT"]
    pp = min(RW_GROUP, npair)
    lw_ = 128 * pp

    def body(r0, r1, v0, v1, ka0, ka1, lw0, lw1, b0, b1, kd0, kd1, y0, y1, ck_ref, s_ref):
        @pl.when(pl.program_id(1) == 0)
        def _():
            s_ref[...] = jnp.zeros_like(s_ref)

        keys, chains = [], []
        for d, refs in enumerate(((r0, lw0, ka0, b0, v0, kd0), (r1, lw1, ka1, b1, v1, kd1))):
            vals = [r[...] for r in refs]
            for j in range(pp):
                keys.append((d, j))
                chains.append((s_ref[d, j], *[a[:, _lanes(j)] for a in vals], d))
        for (d, j), chain, (out, s1) in zip(keys, chains, _rw_chunks(chains)):
            ck_ref[d, j] = chain[0]
            (y0, y1)[d][:, _lanes(j)] = out
            s_ref[d, j] = s1

    two = lambda off: _dir_specs(lambda d: pl.BlockSpec((C, lw_), lambda p, s: (_chunk_of(d, s), off * npair // pp + p)))
    three = _dir_specs(lambda d: pl.BlockSpec((None, C, lw_), lambda p, s: (d, _chunk_of(d, s), p)))
    return _call(body, "rw_scan_fwd", (npair // pp, nch), two(0) + two(2) + two(0) + three * 3,
                 two(0) + [pl.BlockSpec((2, pp, None, 128, 128), lambda p, s: (0, p, s, 0, 0))],
                 [_sds((tt, RWW)), _sds((tt, RWW)), _sds((2, npair, nch, 128, 128))],
                 scratch=[pltpu.VMEM((2, pp, 128, 128), F32)], sem=("parallel", "arbitrary"))(
        sh, sh, sh, sh, kap, kap, lw, lw, b, b, kd, kd)


def _rw_scan_bwd(sh, kap, lw, b, kd, ck, dy):
    dm = _dims()
    npair, nch, tt = dm["RW_P"], dm["NCH"], dm["TT"]
    pp = min(RW_GROUP, npair)
    lw_ = 128 * pp

    def body(r0, r1, v0, v1, ka0, ka1, lw0, lw1, b0, b1, kd0, kd1, ck_ref, dy0, dy1,
             dr0, dr1, dv0, dv1, dka0, dka1, dlw0, dlw1, db0, db1, dkd0, dkd1, ds_ref):
        @pl.when(pl.program_id(1) == 0)
        def _():
            ds_ref[...] = jnp.zeros_like(ds_ref)

        keys, flat, cts = [], [], []
        for d, refs in enumerate(((r0, lw0, ka0, b0, v0, kd0, dy0), (r1, lw1, ka1, b1, v1, kd1, dy1))):
            *vals, dy_ = [r[...] for r in refs]
            for j in range(pp):
                keys.append((d, j))
                flat += [ck_ref[d, j], *[a[:, _lanes(j)] for a in vals]]
                cts.append((dy_[:, _lanes(j)], ds_ref[d, j]))
        _, vjp = jax.vjp(_lockstep(_rw_chunks, [d for d, _ in keys], 7), *flat)
        grads = vjp(tuple(cts))
        outs = ((dr0, dlw0, dka0, db0, dv0, dkd0), (dr1, dlw1, dka1, db1, dv1, dkd1))
        for n, (d, j) in enumerate(keys):
            ds0, *g_in = grads[7 * n:7 * n + 7]
            for o_ref, g in zip(outs[d], g_in):
                o_ref[:, _lanes(j)] = g
            ds_ref[d, j] = ds0

    ch = lambda d, s: _chunk_of(d, nch - 1 - s)
    two = lambda off: _dir_specs(lambda d: pl.BlockSpec((C, lw_), lambda p, s: (ch(d, s), off * npair // pp + p)))
    three = _dir_specs(lambda d: pl.BlockSpec((None, C, lw_), lambda p, s: (d, ch(d, s), p)))
    return _call(body, "rw_scan_bwd", (npair // pp, nch),
                 two(0) + two(2) + two(0) + three * 3
                 + [pl.BlockSpec((2, pp, None, 128, 128), lambda p, s: (0, p, nch - 1 - s, 0, 0))] + two(0),
                 two(0) * 6, [_sds((tt, RWW))] * 12,
                 scratch=[pltpu.VMEM((2, pp, 128, 128), F32)], sem=("parallel", "arbitrary"))(
        sh, sh, sh, sh, kap, kap, lw, lw, b, b, kd, kd, ck, dy, dy)


def _tile_pos(i):
    dm = _dims()
    is_ctx = i < dm["NCT"]
    rows = lax.broadcasted_iota(jnp.int32, (TM, 1), 0)
    tl = rows + (i - dm["NCT"]) * TM
    width = jnp.where(is_ctx, TC, GW)
    colp = jnp.where(is_ctx, rows, tl) % width
    return is_ctx, rows, tl, colp, width


def _shift_lr(cur, i):
    _, _, _, colp, width = _tile_pos(i)
    left = jnp.where(colp == 0, 0.0, pltpu.roll(cur, 1, 0))
    right = jnp.where(colp == width - 1, 0.0, pltpu.roll(cur, TM - 1, 0))
    return left, right


def _shift_ud(cur, prv, nxt, i):
    is_ctx, rows, tl, _, _ = _tile_pos(i)
    if TM > GW:
        up = jnp.where(rows >= GW, pltpu.roll(cur, GW, 0), pltpu.roll(prv, GW, 0))
        down = jnp.where(rows < TM - GW, pltpu.roll(cur, TM - GW, 0), pltpu.roll(nxt, TM - GW, 0))
    else:
        up, down = prv, nxt
    up = jnp.where(jnp.logical_or(is_ctx, tl < GW), 0.0, up)
    down = jnp.where(jnp.logical_or(is_ctx, tl >= T - GW), 0.0, down)
    return up, down


def _shift_specs():
    dm = _dims()
    nt = dm["TT"] // TM
    cw = 256 if dm["SH"] % 256 == 0 else 128
    cur = pl.BlockSpec((TM, cw), lambda j, i: (i, j))
    prv = pl.BlockSpec((TM, cw), lambda j, i: (jnp.maximum(i - 1, 0), j))
    nxt = pl.BlockSpec((TM, cw), lambda j, i: (jnp.minimum(i + 1, nt - 1), j))
    mu = pl.BlockSpec((4, cw), lambda j, i: (0, j))
    return nt, cw, cur, prv, nxt, mu


def _shift_fwd(p_rw, mu):
    dm = _dims()
    nt, cw, cur, prv, nxt, mus = _shift_specs()

    def body(c_ref, p_ref, n_ref, mu_ref, o_ref):
        i = pl.program_id(1)
        p, m = c_ref[...], mu_ref[...]
        left, right = _shift_lr(p, i)
        up, down = _shift_ud(p, p_ref[...], n_ref[...], i)
        vert = jnp.where(i < dm["NCT"], 0.0, 1.0)
        o_ref[...] = (p * (1.0 - m[0:1] - m[1:2] - vert * (m[2:3] + m[3:4]))
                      + m[0:1] * left + m[1:2] * right + m[2:3] * up + m[3:4] * down)

    return _call(body, "shift_fwd", (dm["SH"] // cw, nt), [cur, prv, nxt, mus], cur, _sds((dm["TT"], dm["SH"])),
                 sem=("parallel", "parallel"))(p_rw, p_rw, p_rw, mu)


def _shift_bwd(p_rw, dsh, mu):
    dm = _dims()
    nt, cw, cur, prv, nxt, mus = _shift_specs()

    def body(c_ref, p_ref, n_ref, gc_ref, gp_ref, gn_ref, mu_ref, dp_ref, dmu_ref):
        i = pl.program_id(1)
        p, g, m = c_ref[...], gc_ref[...], mu_ref[...]
        vert = jnp.where(i < dm["NCT"], 0.0, 1.0)
        _, from_right = _shift_lr(m[0:1] * g, i)
        from_left, _ = _shift_lr(m[1:2] * g, i)
        _, from_down = _shift_ud(m[2:3] * g, g, m[2:3] * gn_ref[...], i)
        from_up, _ = _shift_ud(m[3:4] * g, m[3:4] * gp_ref[...], g, i)
        dp = g * (1.0 - m[0:1] - m[1:2] - vert * (m[2:3] + m[3:4])) + from_right + from_left + from_down + from_up
        dp_ref[...] = dp.astype(BF16)

        @pl.when(i == 0)
        def _():
            dmu_ref[...] = jnp.zeros_like(dmu_ref)

        left, right = _shift_lr(p, i)
        up, down = _shift_ud(p, p_ref[...], n_ref[...], i)
        s = lambda a: jnp.sum(a, axis=0, keepdims=True)
        dmu_ref[...] += jnp.concatenate([s(g * (left - p)), s(g * (right - p)), vert * s(g * (up - p)), vert * s(g * (down - p))], axis=0)

    return _call(body, "shift_bwd", (dm["SH"] // cw, nt), [cur, prv, nxt, cur, prv, nxt, mus],
                 [cur, pl.BlockSpec((4, cw), lambda j, i: (0, j))], [_sds((dm["TT"], dm["SH"]), BF16), _sds((4, dm["SH"]))],
                 sem=("parallel", "arbitrary"))(p_rw, p_rw, p_rw, dsh, dsh, dsh, mu)


def _acc_out(ref, first, vals):
    @pl.when(first)
    def _():
        for r in ref:
            r[...] = jnp.zeros_like(r)

    for r, v in zip(ref, vals):
        r[...] += v


def _rowsum(a):
    return jnp.sum(a, axis=0, keepdims=True)


def _rw_prep_specs():
    dm = _dims()
    rw = RWW // 128
    tok = lambda width, cb: pl.BlockSpec((TMV, width), lambda i: (i, cb))
    ins = [pl.BlockSpec((TMV, RWW), lambda i: (i, 1)), tok(128, 3 * rw), tok(128, 3 * rw + 1),
           _full((2, RWW)), _full((128, RWW)), _full((128, RWW)), _full((2, RWW)), _full((128, RWW)), _full((128, RWW)),
           _full((1, RWW)), _full((1, RWW))]
    return dm, ins


def _rw_prep_fwd(sh, w0, w2p0, w2p1, a0, a2p0, a2p1, kkp, kap_p):
    dm, ins = _rw_prep_specs()
    tt = dm["TT"]

    def body(k_ref, low_ref, loa_ref, w0_ref, w20_ref, w21_ref, a0_ref, a20_ref, a21_ref, kk_ref, ka_ref, kap_ref, lw_ref, b_ref, kd_ref):
        tw, la = jnp.tanh(low_ref[...]), loa_ref[...]
        kap, lw0, b0, kd0, lw1, b1, kd1 = _rw_prep_math(
            k_ref[...], _mm(tw, w20_ref[...]), _mm(tw, w21_ref[...]), _mm(la, a20_ref[...]), _mm(la, a21_ref[...]),
            w0_ref[...], a0_ref[...], kk_ref[...], ka_ref[...])
        kap_ref[...] = kap
        lw_ref[0], lw_ref[1] = lw0, lw1
        b_ref[0], b_ref[1] = b0, b1
        kd_ref[0], kd_ref[1] = kd0, kd1

    two = pl.BlockSpec((2, TMV, RWW), lambda i: (0, i, 0))
    return _call(body, "rw_prep_fwd", (tt // TMV,), ins, [pl.BlockSpec((TMV, RWW), lambda i: (i, 0)), two, two, two],
                 [_sds((tt, RWW)), _sds((2, tt, RWW)), _sds((2, tt, RWW)), _sds((2, tt, RWW))], sem=("parallel",))(
        sh, sh, sh, w0, w2p0, w2p1, a0, a2p0, a2p1, kkp, kap_p)


def _rw_prep_bwd(sh, w0, w2p0, w2p1, a0, a2p0, a2p1, kkp, kap_p, scan_grads, dkd_p, dr_p, dv_p):
    dm, ins = _rw_prep_specs()
    tt, sh_w = dm["TT"], dm["SH"]
    one = pl.BlockSpec((TMV, RWW), lambda i: (i, 0))

    def body(k_ref, low_ref, loa_ref, w0_ref, w20_ref, w21_ref, a0_ref, a20_ref, a21_ref, kk_ref, ka_ref,
             dr0, dr1, dv0, dv1, dka0, dka1, dlw0, dlw1, db0, db1, dkd0, dkd1, dkdp_ref, drp_ref, dvp_ref,
             dsh_ref, dw0_ref, dw20_ref, dw21_ref, da0_ref, da20_ref, da21_ref, dkk_ref, dka_ref):
        tw, la = jnp.tanh(low_ref[...]), loa_ref[...]
        w20, w21, a20, a21 = w20_ref[...], w21_ref[...], a20_ref[...], a21_ref[...]
        _, vjp = jax.vjp(_rw_prep_math, k_ref[...], _mm(tw, w20), _mm(tw, w21), _mm(la, a20), _mm(la, a21),
                         w0_ref[...], a0_ref[...], kk_ref[...], ka_ref[...])
        dkp = dkdp_ref[...]
        dk, dpw0, dpw1, dpa0, dpa1, dw0, da0, dkk, dka = vjp(
            (dka0[...] + dka1[...], dlw0[...], db0[...], dkd0[...] + dkp, dlw1[...], db1[...], dkd1[...] + dkp))
        twt, lat = tw.T, la.T
        dtw = _mm_nt(dpw0, w20) + _mm_nt(dpw1, w21)
        dsh_ref[:, 0:RWW] = dr0[...] + dr1[...] + drp_ref[...]
        dsh_ref[:, RWW:2 * RWW] = dk
        dsh_ref[:, 2 * RWW:3 * RWW] = dv0[...] + dv1[...] + dvp_ref[...]
        dsh_ref[:, 3 * RWW:3 * RWW + 128] = dtw * (1.0 - tw * tw)
        dsh_ref[:, 3 * RWW + 128:3 * RWW + 256] = _mm_nt(dpa0, a20) + _mm_nt(dpa1, a21)
        _acc_out((dw0_ref, dw20_ref, dw21_ref, da0_ref, da20_ref, da21_ref, dkk_ref, dka_ref), pl.program_id(0) == 0,
                 (dw0, _mm(twt, dpw0), _mm(twt, dpw1), da0, _mm(lat, dpa0), _mm(lat, dpa1), dkk, dka))

    par = [_full((2, RWW)), _full((128, RWW)), _full((128, RWW)), _full((2, RWW)), _full((128, RWW)), _full((128, RWW)),
           _full((1, RWW)), _full((1, RWW))]
    return _call(body, "rw_prep_bwd", (tt // TMV,), ins + [one] * 15,
                 [pl.BlockSpec((TMV, sh_w), lambda i: (i, 0))] + par,
                 [_sds((tt, sh_w)), _sds((2, RWW)), _sds((128, RWW)), _sds((128, RWW)), _sds((2, RWW)), _sds((128, RWW)), _sds((128, RWW)),
                  _sds((1, RWW)), _sds((1, RWW))],
                 sem=("arbitrary",))(sh, sh, sh, w0, w2p0, w2p1, a0, a2p0, a2p1, kkp, kap_p, *scan_grads, dkd_p, dr_p, dv_p)


def _rw_post_specs():
    dm = _dims()
    nv = dm["NCTV"]
    rw = RWW // 128
    lat3 = lambda d: pl.BlockSpec((None, TMV, RWW), lambda i: (d, nv + i, 0))
    lat2 = pl.BlockSpec((TMV, RWW), lambda i: (nv + i, 0))
    ins = [lat2, lat2, lat3(0), lat3(1),
           pl.BlockSpec((TMV, RWW), lambda i: (nv + i, 0)), pl.BlockSpec((TMV, RWW), lambda i: (nv + i, 2)),
           pl.BlockSpec((TMV, RWW), lambda i: (i, 0)), _full((1, RWW)), _full((1, RWW)), _full((1, RWW))]
    return dm, nv, ins


def _rw_post_fwd(y0_, y1_, kd, sh, p_z, rk, gng, gnb):
    dm, nv, ins = _rw_post_specs()

    def body(y0, y1, k0, k1, r, v, z, rk_ref, g_ref, b_ref, o_ref):
        o_ref[...] = _rw_post_math(y0[...], y1[...], k0[...], k1[...], r[...], v[...], z[...], rk_ref[...], g_ref[...], b_ref[...]).astype(BF16)

    return _call(body, "rw_post_fwd", (T // TMV,), ins, pl.BlockSpec((TMV, RWW), lambda i: (i, 0)), _sds((T, RWW), BF16),
                 sem=("parallel",))(y0_, y1_, kd, kd, sh, sh, p_z, rk, gng, gnb)


def _rw_post_bwd(y0_, y1_, kd, sh, p_z, rk, gng, gnb, dout):
    dm, nv, _ = _rw_post_specs()
    tt = dm["TT"]
    cl = lambda i: jnp.maximum(i - nv, 0)
    all3 = lambda d: pl.BlockSpec((None, TMV, RWW), lambda i: (d, i, 0))
    all2 = pl.BlockSpec((TMV, RWW), lambda i: (i, 0))
    ins = [all2, all2, all3(0), all3(1),
           pl.BlockSpec((TMV, RWW), lambda i: (i, 0)), pl.BlockSpec((TMV, RWW), lambda i: (i, 2)),
           pl.BlockSpec((TMV, RWW), lambda i: (cl(i), 0)), _full((1, RWW)), _full((1, RWW)), _full((1, RWW)),
           pl.BlockSpec((TMV, RWW), lambda i: (cl(i), 0))]
    tok = pl.BlockSpec((TMV, RWW), lambda i: (i, 0))

    def body(y0, y1, k0, k1, r, v, z, rk_ref, g_ref, b_ref, do_ref, dy_ref, dkd_ref, dr_ref, dv_ref, dz_ref, drk_ref, dg_ref, db_ref):
        i = pl.program_id(0)
        lat = jnp.where(i >= nv, 1.0, 0.0)
        _, vjp = jax.vjp(_rw_post_math, y0[...], y1[...], k0[...], k1[...], r[...], v[...], z[...], rk_ref[...], g_ref[...], b_ref[...])
        dy0, _, dk0, _, dr, dv, dz, drk, dg, db = vjp(do_ref[...] * lat)
        dy_ref[...] = dy0
        dkd_ref[...] = dk0
        dr_ref[...] = dr
        dv_ref[...] = dv
        dz_ref[...] = dz.astype(BF16)
        _acc_out((drk_ref, dg_ref, db_ref), i == 0, (drk, dg, db))

    return _call(body, "rw_post_bwd", (tt // TMV,), ins,
                 [tok, tok, tok, tok, pl.BlockSpec((TMV, RWW), lambda i: (cl(i), 0)), _full((1, RWW)), _full((1, RWW)), _full((1, RWW))],
                 [_sds((tt, RWW))] * 4 + [_sds((T, RWW), BF16), _sds((1, RWW)), _sds((1, RWW)), _sds((1, RWW))],
                 sem=("arbitrary",))(y0_, y1_, kd, kd, sh, sh, p_z, rk, gng, gnb, dout)


def _hg_post_fwd(o0_, o1_, p_hg, hg_g):
    dm = _dims()
    nv = dm["NCTV"]
    lat3 = lambda d: pl.BlockSpec((TMV, HGW), lambda i: (nv + i, 0))

    def body(o0, o1, z, g, out):
        out[...] = _hg_post_math(o0[...], o1[...], z[...], g[...]).astype(BF16)

    return _call(body, "hg_post_fwd", (T // TMV,), [lat3(0), lat3(1), pl.BlockSpec((TMV, HGW), lambda i: (nv + i, 4)), _full((1, HGW))],
                 pl.BlockSpec((TMV, HGW), lambda i: (i, 0)), _sds((T, HGW), BF16), sem=("parallel",))(o0_, o1_, p_hg, hg_g)


def _hg_post_bwd(o0_, o1_, p_hg, hg_g, dout):
    dm = _dims()
    nv, tt = dm["NCTV"], dm["TT"]
    all3 = lambda d: pl.BlockSpec((TMV, HGW), lambda i: (i, 0))
    tok = pl.BlockSpec((TMV, HGW), lambda i: (i, 0))

    def body(o0, o1, z, g, do_in, do_ref, dz_ref, dg_ref):
        i = pl.program_id(0)
        lat = jnp.where(i >= nv, 1.0, 0.0)
        _, vjp = jax.vjp(_hg_post_math, o0[...], o1[...], z[...], g[...])
        d0, _, dz, dg = vjp(do_in[...] * lat)
        do_ref[...] = d0
        dz_ref[...] = dz
        _acc_out((dg_ref,), i == 0, (dg,))

    return _call(body, "hg_post_bwd", (tt // TMV,),
                 [all3(0), all3(1), pl.BlockSpec((TMV, HGW), lambda i: (i, 4)), _full((1, HGW)),
                  pl.BlockSpec((TMV, HGW), lambda i: (jnp.maximum(i - nv, 0), 0))],
                 [tok, tok, _full((1, HGW))], [_sds((tt, HGW)), _sds((tt, HGW)), _sds((1, HGW))], sem=("arbitrary",))(o0_, o1_, p_hg, hg_g, dout)


def _hg_dproj(dq0, dq1, di0, di1, df0, df1, dz):
    dm = _dims()
    tt = dm["TT"]
    tok = pl.BlockSpec((TMV, HGW), lambda i: (i, 0))

    def body(q0, q1, i0, i1, f0, f1, z, o_ref):
        o_ref[:, 0:HGW] = (q0[...] + q1[...]).astype(BF16)
        o_ref[:, HGW:2 * HGW] = (i0[...] + i1[...]).astype(BF16)
        o_ref[:, 2 * HGW:3 * HGW] = f0[...].astype(BF16)
        o_ref[:, 3 * HGW:4 * HGW] = f1[...].astype(BF16)
        o_ref[:, 4 * HGW:5 * HGW] = z[...].astype(BF16)

    return _call(body, "hg_dproj", (tt // TMV,), [tok] * 7,
                 pl.BlockSpec((TMV, 5 * HGW), lambda i: (i, 0)), _sds((tt, 5 * HGW), BF16), sem=("parallel",))(
        dq0, dq1, di0, di1, df0, df1, dz)


def _merge1_fwd(out_hg, out_rw, whg_t, wrw_t, p_g):
    tok = pl.BlockSpec((TM, D), lambda i: (i, 0))

    def body(h_ref, r_ref, wh_ref, wr_ref, g1_ref, g2_ref, a_ref, b_ref, m_ref):
        nt = (((1,), (1,)), ((), ()))
        a = lax.dot_general(h_ref[...], wh_ref[...], nt, preferred_element_type=F32)
        b = lax.dot_general(r_ref[...], wr_ref[...], nt, preferred_element_type=F32)
        a_ref[...] = a
        b_ref[...] = b
        m_ref[...] = _gate_math(g1_ref[...], g2_ref[...], a, b).astype(BF16)

    return _call(body, "merge1_fwd", (T // TM,),
                 [pl.BlockSpec((TM, HGW), lambda i: (i, 0)), pl.BlockSpec((TM, RWW), lambda i: (i, 0)), _full((D, HGW)), _full((D, RWW)),
                  pl.BlockSpec((TM, D), lambda i: (i, 0)), pl.BlockSpec((TM, D), lambda i: (i, 1))],
                 [tok, tok, tok], [_sds((T, D)), _sds((T, D)), _sds((T, D), BF16)], sem=("parallel",))(out_hg, out_rw, whg_t, wrw_t, p_g, p_g)


def _head_fwd_bwd(x, merged, w_out, mods, final_g, tgt):
    tok = pl.BlockSpec((TM, D), lambda i: (i, 0))

    def body(x_ref, m_ref, w_ref, mods_ref, fg_ref, t_ref, dxo_ref, dmo_ref, loss_ref, dgate_ref, dfg_ref):
        mo = jnp.dot(m_ref[...], w_ref[...], preferred_element_type=F32)
        gate = mods_ref[0:1, 2 * D:3 * D]
        loss, vjp = jax.vjp(lambda x_, mo_, g_, fg_: _head_math(x_, mo_, g_, fg_, t_ref[...]), x_ref[...], mo, gate, fg_ref[...])
        dx, dmo, dgate, dfg = vjp(jnp.ones((1, 1), F32))
        dxo_ref[...] = dx
        dmo_ref[...] = dmo.astype(BF16)
        _acc_out((loss_ref, dgate_ref, dfg_ref), pl.program_id(0) == 0, (jnp.broadcast_to(loss, (1, 128)), dgate, dfg))

    return _call(body, "head_fwd_bwd", (T // TM,), [tok, tok, _full((D, D)), _full((8, 3 * D)), _full((1, D)), tok],
                 [tok, tok, _full((1, 128)), _full((1, D)), _full((1, D))],
                 [_sds((T, D)), _sds((T, D), BF16), _sds((1, 128)), _sds((1, D)), _sds((1, D))], sem=("arbitrary",))(
        x, merged, w_out, mods, final_g, tgt)


def _merge_bwd(dmo, w_out, p_g, a, b):
    tok = pl.BlockSpec((TM, D), lambda i: (i, 0))

    def body(dmo_ref, w_ref, g1_ref, g2_ref, a_ref, b_ref, da_ref, db_ref, dg_ref):
        dm_ = lax.dot_general(dmo_ref[...], w_ref[...], (((1,), (1,)), ((), ())), preferred_element_type=F32)
        _, vjp = jax.vjp(_gate_math, g1_ref[...], g2_ref[...], a_ref[...], b_ref[...])
        dg1, dg2, da, db = vjp(dm_)
        da_ref[...] = da.astype(BF16)
        db_ref[...] = db.astype(BF16)
        dg_ref[:, 0:D] = dg1.astype(BF16)
        dg_ref[:, D:2 * D] = dg2.astype(BF16)

    return _call(body, "merge_bwd", (T // TM,),
                 [tok, _full((D, D)), pl.BlockSpec((TM, D), lambda i: (i, 0)), pl.BlockSpec((TM, D), lambda i: (i, 1)), tok, tok],
                 [tok, tok, pl.BlockSpec((TM, 2 * D), lambda i: (i, 0))], [_sds((T, D), BF16), _sds((T, D), BF16), _sds((T, 2 * D), BF16)],
                 sem=("parallel",))(dmo, w_out, p_g, p_g, a, b)


def _h_bwd(xcat, mods, norm_g, dh_hg, dh_rw, dh_z, dh_g, dxo, dgate):
    dm = _dims()
    nct, tt = dm["NCT"], dm["TT"]
    tok = pl.BlockSpec((TM, D), lambda i: (i, 0))
    lat = pl.BlockSpec((TM, D), lambda i: (jnp.maximum(i - nct, 0), 0))

    def body(x_ref, m_ref, g_ref, d1, d2, d3, d4, dxo_ref, dgate_ref, gx_ref, dng_ref, dmods_ref):
        i = pl.program_id(0)
        is_ctx = i < nct
        latf = jnp.where(is_ctx, 0.0, 1.0)
        dh = d1[...] + d2[...] + latf * (d3[...] + d4[...])
        _, vjp = jax.vjp(_h_math, x_ref[...], g_ref[...], _sel_mod(m_ref, is_ctx, D), _sel_mod(m_ref, is_ctx, 0))
        dx, dng, dscale, dshift = vjp(dh)
        gx_ref[...] = dx + dxo_ref[...]

        @pl.when(i == 0)
        def _():
            dng_ref[...] = jnp.zeros_like(dng_ref)
            dmods_ref[...] = jnp.zeros_like(dmods_ref)
            dmods_ref[0:1, 2 * D:3 * D] = dgate_ref[...]

        dng_ref[...] += dng
        row = lax.broadcasted_iota(jnp.int32, (8, 1), 0)
        sel = jnp.where(row == jnp.where(is_ctx, 1, 0), 1.0, 0.0)
        dmods_ref[:, 0:D] += sel * dshift
        dmods_ref[:, D:2 * D] += sel * dscale

    return _call(body, "h_bwd", (tt // TM,), [tok, _full((8, 3 * D)), _full((1, D)), tok, tok, lat, lat, lat, _full((1, D))],
                 [lat, _full((1, D)), _full((8, 3 * D))], [_sds((T, D)), _sds((1, D)), _sds((8, 3 * D))], sem=("arbitrary",))(
        xcat, mods, norm_g, dh_hg, dh_rw, dh_z, dh_g, dxo, dgate)


def _mods_bwd(cc, ada_t, dmods, dmods_t):
    tn = 512 if (3 * D) % 512 == 0 else 256

    nsteps = 3 * D // tn

    def body(cc_ref, w_ref, dm_ref, dmt_ref, dcc_ref, dw_ref, db_ref):
        c_ = cc_ref[...]
        s, vjp = jax.vjp(_silu, c_)
        dm_ = dm_ref[...]
        ds = jnp.dot(dm_.astype(BF16), w_ref[...], preferred_element_type=F32)
        _acc_out((dcc_ref,), pl.program_id(0) == 0, (ds,))

        @pl.when(pl.program_id(0) == nsteps - 1)
        def _():
            dcc_ref[...] = vjp(dcc_ref[...])[0]

        dw_ref[...] = _mm(dmt_ref[...], s).astype(BF16)
        db_ref[...] = dm_[0:1] + dm_[1:2]

    return _call(body, "mods_bwd", (nsteps,),
                 [_full((8, D)), pl.BlockSpec((tn, D), lambda j: (j, 0)), pl.BlockSpec((8, tn), lambda j: (0, j)),
                  pl.BlockSpec((tn, 8), lambda j: (j, 0))],
                 [_full((8, D)), pl.BlockSpec((tn, D), lambda j: (j, 0)), pl.BlockSpec((1, tn), lambda j: (0, j))],
                 [_sds((8, D)), _sds((3 * D, D), BF16), _sds((1, 3 * D))], sem=("arbitrary",))(cc, ada_t, dmods, dmods_t)


def _lb_math(l0, l1):
    return jax.nn.sigmoid(l0 - l1)


def _lb_fwd(l0, l1):
    def body(a_ref, b_ref, o_ref):
        o_ref[...] = _lb_math(a_ref[...], b_ref[...])

    return _call(body, "lb_fwd", (1,), [_full((2, HGW))] * 2, _full((2, HGW)), _sds((2, HGW)))(l0, l1)


def _lb_bwd(l0, l1, dlb):
    def body(a_ref, b_ref, d_ref, da_ref, db_ref):
        _, vjp = jax.vjp(_lb_math, a_ref[...], b_ref[...])
        da_ref[...], db_ref[...] = vjp(d_ref[...])

    return _call(body, "lb_bwd", (1,), [_full((2, HGW))] * 3, [_full((2, HGW))] * 2, [_sds((2, HGW))] * 2)(l0, l1, dlb)


def _local_step(x, c, ctx, tgt, w):
    dm = _dims()
    tt, sh_w, hgc = dm["TT"], dm["SH"], dm["HGC"]
    nb = lambda cols: cols // WB
    xcat = jnp.concatenate([ctx, x], axis=0)
    cc = jnp.concatenate([c, w["c_ctx"], jnp.zeros((6, D), F32)], axis=0)
    mods = _mods_fwd(cc, w["ada_t"], w["ada_b"])
    h = _h_fwd(xcat, mods, w["norm_g"])
    win_t = w["win_t"]
    p_hg = _proj_nt(h, win_t, 0, nb(hgc), 0, tt, "proj_hg")
    p_rw = _proj_nt(h, win_t, nb(hgc), nb(sh_w), 0, tt, "proj_rw")
    p_z = _proj_nt(h, win_t, nb(hgc + sh_w), nb(RWW), TC, T, "proj_z")
    p_g = _proj_nt(h, win_t, nb(hgc + sh_w + RWW), nb(2 * D), TC, T, "proj_g")

    lb3 = _lb_fwd(w["lb0"], w["lb1"]).reshape(2, 1, HGW)
    o0, o1, hg_ck = _hg_scan_fwd(p_hg, lb3)
    out_hg = _hg_post_fwd(o0, o1, p_hg, w["hg_g"])

    sh = _shift_fwd(p_rw, w["mu"])
    zpad = jnp.zeros((LORA, RWW), F32)
    prep_w = (w["w0"], jnp.concatenate([w["w2"][0], zpad], 0), jnp.concatenate([zpad, w["w2"][1]], 0),
              w["a0"], jnp.concatenate([w["a2"][0], zpad], 0), jnp.concatenate([zpad, w["a2"][1]], 0), w["kk"], w["ka"])
    kap, lw, b, kd = _rw_prep_fwd(sh, *prep_w)
    y0, y1, rw_ck = _rw_scan_fwd(sh, kap, lw, b, kd)
    post_w = (w["rk"], w["gng"], w["gnb"])
    out_rw = _rw_post_fwd(y0, y1, kd, sh, p_z, *post_w)

    a, bb, merged = _merge1_fwd(out_hg, out_rw, w["whg_t"], w["wrw_t"], p_g)
    dxo, dmo, loss, dgate, d_fg = _head_fwd_bwd(x, merged, w["wout"], mods, w["final_g"], tgt)
    da, db, dp_g = _merge_bwd(dmo, w["wout"], p_g, a, bb)
    g_wout = _dw_tn(merged, dmo, 0, D, 0, "dw_out")
    g_whg = _dw_tn(da, out_hg, 0, D, 0, "dw_hg")
    g_wrw = _dw_tn(db, out_rw, 0, D, 0, "dw_rw")
    d_out_hg = _mm_nn(da, w["whg_t"], 0, "dx_hg")
    d_out_rw = _mm_nn(db, w["wrw_t"], 0, "dx_rw")

    do, dz_hg, d_hg_g = _hg_post_bwd(o0, o1, p_hg, w["hg_g"], d_out_hg)
    *hg_grads, dlb3 = _hg_scan_bwd(p_hg, lb3, hg_ck, do)
    dp_hg = _hg_dproj(*hg_grads, dz_hg)
    d_lb0, d_lb1 = _lb_bwd(w["lb0"], w["lb1"], dlb3.reshape(2, HGW))

    dy, dkd_p, dr_p, dv_p, dp_z, d_rk, d_gng, d_gnb = _rw_post_bwd(y0, y1, kd, sh, p_z, *post_w, d_out_rw)
    rw_grads = _rw_scan_bwd(sh, kap, lw, b, kd, rw_ck, dy)
    dsh, d_w0, d_w2p0, d_w2p1, d_a0, d_a2p0, d_a2p1, d_kk, d_ka = _rw_prep_bwd(sh, *prep_w, rw_grads, dkd_p, dr_p, dv_p)
    dp_rw, d_mu = _shift_bwd(p_rw, dsh, w["mu"])

    g_win = _dw_tn(dp_hg, h, 0, dm["NCOLS"], 0, "dw_in_hg")
    g_win = _dw_tn(dp_rw, h, 0, dm["NCOLS"], nb(hgc), "dw_in_rw", prev=g_win)
    g_win = _dw_tn(dp_z, h, TC, dm["NCOLS"], nb(hgc + sh_w), "dw_in_z", prev=g_win)
    g_win = _dw_tn(dp_g, h, TC, dm["NCOLS"], nb(hgc + sh_w + RWW), "dw_in_g", prev=g_win)
    dh_hg = _mm_nn(dp_hg, win_t, 0, "dh_hg")
    dh_rw = _mm_nn(dp_rw, win_t, nb(hgc), "dh_rw")
    dh_z = _mm_nn(dp_z, win_t, nb(hgc + sh_w), "dh_z")
    dh_g = _mm_nn(dp_g, win_t, nb(hgc + sh_w + RWW), "dh_g")
    gx, d_ng, dmods = _h_bwd(xcat, mods, w["norm_g"], dh_hg, dh_rw, dh_z, dh_g, dxo, dgate)
    dcc, g_ada, d_ada_b = _mods_bwd(cc, w["ada_t"], dmods, dmods.T)

    big = dict(win_t=g_win, wout=g_wout, whg_t=g_whg, wrw_t=g_wrw, ada_t=g_ada)
    small = dict(c_ctx=dcc[1:2], ada_b=d_ada_b, norm_g=d_ng, lb0=d_lb0, lb1=d_lb1, hg_g=d_hg_g, mu=d_mu, w0=d_w0,
                 w2=jnp.stack([d_w2p0[:LORA], d_w2p1[LORA:]]), a0=d_a0, a2=jnp.stack([d_a2p0[:LORA], d_a2p1[LORA:]]),
                 kk=d_kk, ka=d_ka, rk=d_rk, gng=d_gng, gnb=d_gnb, final_g=d_fg)
    return loss, gx, big, small


def _exchange(bufs, gather, name):
    nbuf = len(bufs)
    npeer = NDEV - 1

    def body(*refs):
        ins, outs = refs[:nbuf], refs[nbuf:2 * nbuf]
        send_sems, recv_sems, loc_sems = refs[2 * nbuf:]
        mx, my, mc = lax.axis_index("x"), lax.axis_index("y"), lax.axis_index("c")
        me = 4 * mx + 2 * my + mc
        started = []
        for bi in range(nbuf):
            own = pltpu.make_async_copy(ins[bi] if gather else ins[bi].at[me], outs[bi].at[me], loc_sems.at[bi])
            own.start()
            started.append((own, None))
            for k in range(1, NDEV):
                px = 1 - mx if k & 4 else mx
                py = 1 - my if k & 2 else my
                pc = 1 - mc if k & 1 else mc
                peer = 4 * px + 2 * py + pc
                sem = bi * npeer + k - 1
                send = pltpu.make_async_remote_copy(
                    src_ref=ins[bi] if gather else ins[bi].at[peer], dst_ref=outs[bi].at[me],
                    send_sem=send_sems.at[sem], recv_sem=recv_sems.at[sem], device_id=(px, py, pc), device_id_type=pl.DeviceIdType.MESH)
                send.start()
                recv = pltpu.make_async_remote_copy(
                    src_ref=ins[bi] if gather else ins[bi].at[me], dst_ref=outs[bi].at[peer],
                    send_sem=send_sems.at[sem], recv_sem=recv_sems.at[sem], device_id=(px, py, pc), device_id_type=pl.DeviceIdType.MESH)
                started.append((send, recv))
        for first, recv in started:
            if recv is None:
                first.wait()
            else:
                first.wait_send()
                recv.wait_recv()

    hbm = pl.BlockSpec(memory_space=pl.ANY)
    shapes = [_sds((NDEV,) + (b.shape if gather else b.shape[1:]), b.dtype) for b in bufs]
    return pl.pallas_call(
        body, name=name, in_specs=[hbm] * nbuf, out_specs=[hbm] * nbuf, out_shape=shapes,
        scratch_shapes=[pltpu.SemaphoreType.DMA((nbuf * npeer,)), pltpu.SemaphoreType.DMA((nbuf * npeer,)), pltpu.SemaphoreType.DMA((nbuf,))],
    )(*bufs)


def _row_tile(rows, mult, cap):
    best = mult
    for t in range(mult, cap + 1, mult):
        if rows % t == 0:
            best = t
    assert rows % best == 0, (rows, mult)
    return best


def _reduce8(r, name):
    _, rows, cols = r.shape
    tr = _row_tile(rows, 16 if r.dtype.itemsize == 2 else 8, 128)

    def body(r_ref, o_ref):
        acc = r_ref[0].astype(F32)
        for j in range(1, NDEV):
            acc = acc + r_ref[j].astype(F32)
        o_ref[...] = acc

    return _call(body, name, (rows // tr,), [pl.BlockSpec((NDEV, tr, cols), lambda i: (0, i, 0))],
                 pl.BlockSpec((tr, cols), lambda i: (i, 0)), _sds((rows, cols)), sem=("parallel",))(r)


def _adamw(w, g, m, v, name):
    rows, cols = w.shape
    tr = _row_tile(rows, 8, 128)
    c1 = 1.0 - ADAM_B1 ** ADAM_STEP
    c2 = 1.0 - ADAM_B2 ** ADAM_STEP

    def body(w_ref, g_ref, m_ref, v_ref, d_ref, mo_ref, vo_ref):
        g_ = g_ref[...]
        m_ = ADAM_B1 * m_ref[...] + (1.0 - ADAM_B1) * g_
        v_ = ADAM_B2 * v_ref[...] + (1.0 - ADAM_B2) * (g_ * g_)
        d_ref[...] = -ADAM_LR * ((m_ / c1) / (jnp.sqrt(v_ / c2) + ADAM_EPS) + ADAM_WD * w_ref[...])
        mo_ref[...] = m_
        vo_ref[...] = v_

    blk = pl.BlockSpec((tr, cols), lambda i: (i, 0))
    return _call(body, name, (rows // tr,), [blk] * 4, [blk] * 3, [_sds((rows, cols))] * 3, sem=("parallel",))(w, g, m, v)


SLAB_PART = 1024


def _pack(arrs):
    parts = []
    for a in arrs:
        flat = a.reshape(-1)
        pad = (-flat.shape[0]) % SLAB_PART
        if pad:
            flat = jnp.concatenate([flat, jnp.zeros((pad,), flat.dtype)])
        parts.append(flat.reshape(-1, 128))
    return jnp.concatenate(parts, axis=0)


def _unpack(slab, shapes):
    outs, row = [], 0
    for s in shapes:
        n = 1
        for d in s:
            n *= d
        rows = (n + SLAB_PART - 1) // SLAB_PART * (SLAB_PART // 128)
        outs.append(slab[row:row + rows].reshape(-1)[:n].reshape(s))
        row += rows
    return outs


def _unshard_last(g, lead):
    nl = len(lead)
    return jnp.transpose(g, tuple(range(1, nl + 1)) + (0, nl + 1)).reshape(tuple(lead) + (-1,))


def kernel(x, c, ctx, c_ctx, ada_w, ada_b, norm_g, w_in, hg_lb, hg_norm_g, rw_mu, rw_w0, rw_w2, rw_a0, rw_a2, rw_kk, rw_ka, rw_rk, rw_gn_g, rw_gn_b, w_hg_out, w_rw_out, w_out, final_g, loss_target, m_c_ctx, m_ada_w, m_ada_b, m_norm_g, m_w_in, m_hg_lb, m_hg_norm_g, m_rw_mu, m_rw_w0, m_rw_w2, m_rw_a0, m_rw_a2, m_rw_kk, m_rw_ka, m_rw_rk, m_rw_gn_g, m_rw_gn_b, m_w_hg_out, m_w_rw_out, m_w_out, m_final_g, v_c_ctx, v_ada_w, v_ada_b, v_norm_g, v_w_in, v_hg_lb, v_hg_norm_g, v_rw_mu, v_rw_w0, v_rw_w2, v_rw_a0, v_rw_a2, v_rw_kk, v_rw_ka, v_rw_rk, v_rw_gn_g, v_rw_gn_b, v_w_hg_out, v_w_rw_out, v_w_out, v_final_g):
    dm = _dims()
    me = 4 * lax.axis_index("x") + 2 * lax.axis_index("y") + lax.axis_index("c")

    big_shards = [w_in[0].T.astype(BF16), w_out[0].astype(BF16), w_hg_out[0].T.astype(BF16), w_rw_out[0].T.astype(BF16),
                  ada_w[0].T.astype(BF16)]
    sharded_small = [hg_lb, rw_mu[0], rw_w0[0], rw_w2[0], rw_a0[0], rw_a2[0]]
    gathered = _exchange(big_shards + [_pack(sharded_small)], True, "gather_weights")
    win_t, wout, whg_t, wrw_t, ada_t = [g.reshape(-1, g.shape[2]) for g in gathered[:5]]
    per_dev = jax.vmap(lambda s: _unpack(s, [a.shape for a in sharded_small]))(gathered[5])
    hg_lb_f, mu_f, w0_f, w2_f, a0_f, a2_f = [_unshard_last(p, p.shape[1:-1]) for p in per_dev]

    w = dict(win_t=win_t, wout=wout, whg_t=whg_t, wrw_t=wrw_t, ada_t=ada_t, c_ctx=c_ctx[None], ada_b=ada_b, norm_g=norm_g,
             lb0=hg_lb_f[:, 0], lb1=hg_lb_f[:, 1], hg_g=hg_norm_g, mu=mu_f, w0=w0_f, w2=w2_f, a0=a0_f, a2=a2_f,
             kk=rw_kk, ka=rw_ka, rk=rw_rk, gng=rw_gn_g, gnb=rw_gn_b, final_g=final_g[None])
    loss_dev, grad_x, big, small = _local_step(x[0], c, ctx[0], loss_target[0], w)
    loss = lax.psum(loss_dev[0, 0], AXES)

    small_names = ["c_ctx", "ada_b", "norm_g", "lb0", "lb1", "hg_g", "mu", "w0", "w2", "a0", "a2", "kk", "ka", "rk", "gng", "gnb", "final_g"]
    small_slab = _pack([small[n] for n in small_names])
    big_names = ["win_t", "wout", "whg_t", "wrw_t", "ada_t"]
    recv = _exchange([big[n].reshape(NDEV, -1, big[n].shape[1]) for n in big_names], False, "scatter_grads")
    small_all = _exchange([small_slab], True, "gather_small_grads")[0]
    g_win_t, g_wout, g_whg_t, g_wrw_t, g_ada_t = [_reduce8(r, "reduce_" + n) for r, n in zip(recv, big_names)]
    sg = dict(zip(small_names, _unpack(_reduce8(small_all, "reduce_small"), [small[n].shape for n in small_names])))

    def my_shard(full):
        n = full.shape[-1] // NDEV
        return lax.dynamic_slice_in_dim(full, me * n, n, axis=full.ndim - 1)

    grads = dict(
        c_ctx=sg["c_ctx"][0], ada_w=g_ada_t.T[None], ada_b=sg["ada_b"], norm_g=sg["norm_g"], w_in=g_win_t.T[None],
        hg_lb=my_shard(jnp.stack([sg["lb0"], sg["lb1"]], axis=1)), hg_norm_g=sg["hg_g"], rw_mu=my_shard(sg["mu"])[None],
        rw_w0=my_shard(sg["w0"])[None], rw_w2=my_shard(sg["w2"])[None], rw_a0=my_shard(sg["a0"])[None], rw_a2=my_shard(sg["a2"])[None],
        rw_kk=sg["kk"], rw_ka=sg["ka"], rw_rk=sg["rk"], rw_gn_g=sg["gng"], rw_gn_b=sg["gnb"],
        w_hg_out=g_whg_t.T[None], w_rw_out=g_wrw_t.T[None], w_out=g_wout[None], final_g=sg["final_g"][0])
    weights = dict(c_ctx=c_ctx, ada_w=ada_w, ada_b=ada_b, norm_g=norm_g, w_in=w_in, hg_lb=hg_lb, hg_norm_g=hg_norm_g, rw_mu=rw_mu,
                   rw_w0=rw_w0, rw_w2=rw_w2, rw_a0=rw_a0, rw_a2=rw_a2, rw_kk=rw_kk, rw_ka=rw_ka, rw_rk=rw_rk, rw_gn_g=rw_gn_g,
                   rw_gn_b=rw_gn_b, w_hg_out=w_hg_out, w_rw_out=w_rw_out, w_out=w_out, final_g=final_g)
    m_in = dict(zip(weights, (m_c_ctx, m_ada_w, m_ada_b, m_norm_g, m_w_in, m_hg_lb, m_hg_norm_g, m_rw_mu, m_rw_w0, m_rw_w2, m_rw_a0,
                              m_rw_a2, m_rw_kk, m_rw_ka, m_rw_rk, m_rw_gn_g, m_rw_gn_b, m_w_hg_out, m_w_rw_out, m_w_out, m_final_g)))
    v_in = dict(zip(weights, (v_c_ctx, v_ada_w, v_ada_b, v_norm_g, v_w_in, v_hg_lb, v_hg_norm_g, v_rw_mu, v_rw_w0, v_rw_w2, v_rw_a0,
                              v_rw_a2, v_rw_kk, v_rw_ka, v_rw_rk, v_rw_gn_g, v_rw_gn_b, v_w_hg_out, v_w_rw_out, v_w_out, v_final_g)))

    big_w = ["ada_w", "w_in", "w_hg_out", "w_rw_out", "w_out"]
    delta, new_m, new_v = {}, {}, {}
    for n in big_w:
        shp = weights[n].shape
        two = lambda a: a.reshape(shp[-2], shp[-1])
        d_, m_, v_ = _adamw(two(weights[n]), two(grads[n]), two(m_in[n]), two(v_in[n]), "adamw_" + n)
        delta[n], new_m[n], new_v[n] = d_.reshape(shp), m_.reshape(shp), v_.reshape(shp)
    rest = [n for n in weights if n not in big_w]
    shapes = [weights[n].shape for n in rest]
    d_s, m_s, v_s = _adamw(_pack([weights[n] for n in rest]), _pack([grads[n] for n in rest]), _pack([m_in[n] for n in rest]),
                           _pack([v_in[n] for n in rest]), "adamw_small")
    for n, d_, m_, v_ in zip(rest, _unpack(d_s, shapes), _unpack(m_s, shapes), _unpack(v_s, shapes)):
        delta[n], new_m[n], new_v[n] = d_, m_, v_

    order = list(weights)
    return (loss, grad_x[None], *[grads[n] for n in order], *[delta[n] for n in order],
            *[new_m[n] for n in order], *[new_v[n] for n in order])
```

```python
import functools

import jax
import jax.numpy as jnp
from jax import lax
from jax.experimental import pallas as pl
from jax.experimental.pallas import tpu as pltpu

D = 2048
T = 2048
TC = 256
GW = 64
C = 64
HGW = 1024
RWW = 1024
LORA = 64
NDEV = 8
TM = 256
TMV = 128
NORM_EPS = 1e-6
RW_GN_EPS = 64e-5
ADAM_LR, ADAM_B1, ADAM_B2, ADAM_EPS, ADAM_WD, ADAM_STEP = 0.001, 0.9, 0.999, 1e-08, 0.01, 10
WB = 256
V7X_VMEM_LIMIT = 56 * 1024 * 1024

F32 = jnp.float32
BF16 = jnp.bfloat16
AXES = ("x", "y", "c")


def _dims():
    tt = T + TC
    sh = 3 * RWW + 4 * LORA
    hgc = 5 * HGW
    return dict(TT=tt, NCC=TC // C, NCH=tt // C, HG_H=HGW // 128, RW_P=RWW // 128, SH=sh, HGC=hgc,
                NCOLS=hgc + sh + RWW + 2 * D, NCT=TC // TM, NCTV=TC // TMV)


def _mm(a, b):
    return _dot3(a, b, "nn")


def _mm_nt(a, b):
    return _dot3(a, b, "nt")


def _mm_tn(a, b):
    return _dot3(a, b, "tn")


def _split(x):
    hi = x.astype(BF16)
    return hi, (x - hi.astype(F32)).astype(BF16)


_FORMS = {"nn": (((1,), (0,)), ((), ())), "nt": (((1,), (1,)), ((), ())), "tn": (((0,), (0,)), ((), ()))}


def _dot3_raw(a, b, form):
    ah, al = _split(a)
    bh, bl = _split(b)
    d = lambda x, y: lax.dot_general(x, y, _FORMS[form], preferred_element_type=F32)
    return d(ah, bh) + (d(ah, bl) + d(al, bh))


@functools.partial(jax.custom_vjp, nondiff_argnums=(2,))
def _dot3(a, b, form):
    return _dot3_raw(a, b, form)


def _dot3_fwd(a, b, form):
    return _dot3_raw(a, b, form), (a, b)


def _dot3_bwd(form, res, g):
    a, b = res
    if form == "nn":
        return _dot3(g, b, "nt"), _dot3(a, g, "tn")
    if form == "nt":
        return _dot3(g, b, "nn"), _dot3(g, a, "tn")
    return _dot3(b, g, "nt"), _dot3(a, g, "nn")


_dot3.defvjp(_dot3_fwd, _dot3_bwd)


def _scan_cumsum(inc, x):
    return _cumsum_vjp(inc.astype(BF16), x)


def _cumsum_raw(inc, x, form):
    h1 = x.astype(BF16)
    r1 = x - h1.astype(F32)
    h2 = r1.astype(BF16)
    h3 = (r1 - h2.astype(F32)).astype(BF16)
    d = lambda y: lax.dot_general(inc, y, _FORMS[form], preferred_element_type=F32)
    return d(h1) + (d(h2) + d(h3))


@jax.custom_vjp
def _cumsum_vjp(inc, x):
    return _cumsum_raw(inc, x, "nn")


_cumsum_vjp.defvjp(lambda inc, x: (_cumsum_raw(inc, x, "nn"), inc),
                   lambda inc, g: (jnp.zeros_like(inc), _cumsum_raw(inc, g, "tn")))


def _silu(x):
    return x * jax.nn.sigmoid(x)


def _softplus(x):
    return jnp.maximum(x, 0.0) + jnp.log(1.0 + jnp.exp(-jnp.abs(x)))


def _iota2(shape, dim):
    return lax.broadcasted_iota(jnp.int32, shape, dim)


def _pair_mask():
    return (_iota2((128, 128), 0) < 64) == (_iota2((128, 128), 1) < 64)


def _seg64_sum(x):
    e = _pair_mask().astype(F32)
    parts = [_mm(x[:, g * 128:(g + 1) * 128], e) for g in range(x.shape[1] // 128)]
    return parts[0] if len(parts) == 1 else jnp.concatenate(parts, axis=1)


def _seg128_mean(x):
    parts = [jnp.broadcast_to(jnp.mean(x[:, g * 128:(g + 1) * 128], axis=1, keepdims=True), (x.shape[0], 128))
             for g in range(x.shape[1] // 128)]
    return parts[0] if len(parts) == 1 else jnp.concatenate(parts, axis=1)


def _scan_masks(d):
    lag = (_iota2((C, C), 0) - _iota2((C, C), 1)) * (1 - 2 * d)
    return lag >= 0, lag > 0


def _rw_chunks(chains):
    s0, r, lw, kap, b, v, kd, ds = zip(*chains)
    ids = range(len(chains))
    inc, strict = zip(*[_scan_masks(d) for d in ds])
    lane = _iota2((1, 128), 1)
    hm = ((lane < 64).astype(F32), (lane >= 64).astype(F32))
    eye = (_iota2((C, C), 0) == _iota2((C, C), 1)).astype(F32)
    lc = [_scan_cumsum(inc[i], lw[i]) for i in ids]
    ltot = [jnp.sum(lw[i], axis=0, keepdims=True) for i in ids]
    rt = [r[i] * jnp.exp(lc[i]) for i in ids]
    kt = [kap[i] * jnp.exp(lc[i] - lw[i]) for i in ids]
    einv = [jnp.exp(-lc[i]) for i in ids]
    kh = [kd[i] * einv[i] for i in ids]
    bh = [b[i] * einv[i] for i in ids]
    eend = [jnp.exp(ltot[i] - lc[i]) for i in ids]
    kbar = [kd[i] * eend[i] for i in ids]
    bbar = [b[i] * eend[i] for i in ids]
    lhs = [jnp.concatenate([kt[i] * hm[0], kt[i] * hm[1], rt[i] * hm[0], rt[i] * hm[1]], axis=0) for i in ids]
    gk = [_mm_nt(lhs[i], kh[i]) for i in ids]
    gb = [_mm_nt(lhs[i], bh[i]) for i in ids]
    ks = [_mm_nt(kt[i], s0[i]) for i in ids]
    ys = [_mm_nt(rt[i], s0[i]) for i in ids]
    heads = [(i, h) for i in ids for h in range(2)]
    akk = {c: jnp.where(strict[c[0]], gk[c[0]][c[1] * C:(c[1] + 1) * C], 0.0) for c in heads}
    akb = {c: jnp.where(strict[c[0]], gb[c[0]][c[1] * C:(c[1] + 1) * C], 0.0) for c in heads}
    ark = {c: jnp.where(inc[c[0]], gk[c[0]][(2 + c[1]) * C:(3 + c[1]) * C], 0.0) for c in heads}
    arb = {c: jnp.where(inc[c[0]], gb[c[0]][(2 + c[1]) * C:(3 + c[1]) * C], 0.0) for c in heads}
    vh = {c: v[c[0]] * hm[c[1]] for c in heads}
    av = {c: _mm(akk[c], vh[c]) for c in heads}
    rhs = {c: ks[c[0]] * hm[c[1]] + av[c] for c in heads}
    p = {c: -akb[c] for c in heads}
    tm = {c: eye + p[c] for c in heads}
    n = 2
    while n < C:
        p = {c: _mm(p[c], p[c]) for c in heads}
        tm = {c: tm[c] + _mm(tm[c], p[c]) for c in heads}
        n *= 2
    uh = {c: _mm(tm[c], rhs[c]) for c in heads}
    yv = {c: _mm(ark[c], vh[c]) for c in heads}
    yu = {c: _mm(arb[c], uh[c]) for c in heads}
    u = [uh[i, 0] + uh[i, 1] for i in ids]
    y = [ys[i] + (yv[i, 0] - yu[i, 0]) + (yv[i, 1] - yu[i, 1]) for i in ids]
    upd = [_mm_tn(jnp.concatenate([v[i], -u[i]], axis=0), jnp.concatenate([kbar[i], bbar[i]], axis=0)) for i in ids]
    s1 = [s0[i] * jnp.exp(ltot[i]) + jnp.where(_pair_mask(), upd[i], 0.0) for i in ids]
    return [(y[i], s1[i]) for i in ids]


HG_SUB = 16


def _hg_chunks(chains):
    s0, qr, iv, f, lb, ds = zip(*chains)
    ids = range(len(chains))
    inc = [_scan_masks(d)[0] for d in ds]
    q = [_silu(qr[i]) for i in ids]
    fg = [lb[i] + (1.0 - lb[i]) * jax.nn.sigmoid(f[i]) for i in ids]
    k = [1.0 - fg[i] for i in ids]
    g = [jnp.log(fg[i]) for i in ids]
    bc = [_scan_cumsum(inc[i], g[i]) for i in ids]
    btot = [jnp.sum(g[i], axis=0, keepdims=True) for i in ids]
    o_inter = [_mm_nt(q[i] * jnp.exp(bc[i]), s0[i]) for i in ids]
    rowi = _iota2((C, 1), 0)
    outs = [[] for _ in ids]
    for blk in range(C // HG_SUB):
        lo, hi = blk * HG_SUB, (blk + 1) * HG_SUB
        first = [lo if ds[i] == 0 else hi - 1 for i in ids]
        ref = [jnp.sum(jnp.where(rowi == first[i], bc[i] - g[i], 0.0), axis=0, keepdims=True) for i in ids]
        qi = [q[i][lo:hi] * jnp.exp(bc[i][lo:hi] - ref[i]) for i in ids]
        src = [(rowi < hi) if ds[i] == 0 else (rowi >= lo) for i in ids]
        ke = [jnp.where(src[i], k[i] * jnp.exp(jnp.where(src[i], ref[i] - bc[i], 0.0)), 0.0) for i in ids]
        a = [jnp.where(inc[i][lo:hi], _mm_nt(qi[i], ke[i]), 0.0) for i in ids]
        part = [_mm(a[i], iv[i]) for i in ids]
        for i in ids:
            outs[i].append(part[i])
    o = [o_inter[i] + jnp.concatenate(outs[i], axis=0) for i in ids]
    upd = [_mm_tn(iv[i], k[i] * jnp.exp(btot[i] - bc[i])) for i in ids]
    s1 = [s0[i] * jnp.exp(btot[i]) + upd[i] for i in ids]
    return [(o[i], s1[i]) for i in ids]


def _lockstep(fn, ds, nargs):
    def flat_fn(*flat):
        return tuple(fn([tuple(flat[i * nargs:(i + 1) * nargs]) + (d,) for i, d in enumerate(ds)]))

    return flat_fn


def _h_math(x, ng, scale, shift):
    return x * lax.rsqrt(jnp.mean(x * x, axis=-1, keepdims=True) + NORM_EPS) * ng * (1.0 + scale) + shift


def _hg_post_math(o0, o1, z, g):
    o = o0 + o1
    on = o * lax.rsqrt(_seg128_mean(o * o) + NORM_EPS)
    return on * g * _silu(z)


def _rw_prep_math(k, pw0, pw1, pa0, pa1, w0, a0, kkp, kap_p):
    kk = k * kkp
    kap = kk * lax.rsqrt(_seg64_sum(kk * kk) + 1e-12)
    outs = []
    for d, (pw, pa) in enumerate(((pw0, pa0), (pw1, pa1))):
        w_log = -_softplus(-(w0[d:d + 1] + pw)) - 0.5
        lw = -jnp.exp(w_log)
        a = jax.nn.sigmoid(a0[d:d + 1] + pa)
        kd = k * (1.0 + (a - 1.0) * kap_p)
        outs += [lw, kap * a, kd]
    return (kap, *outs)


def _rw_post_math(y0, y1, kd0, kd1, r, v, z, rk, gng, gnb):
    ys, ksum = y0 + y1, kd0 + kd1
    mean = _seg64_sum(ys) * (1.0 / 64.0)
    cen = ys - mean
    var = _seg64_sum(cen * cen) * (1.0 / 64.0)
    yn = cen * lax.rsqrt(var + RW_GN_EPS) * gng + gnb
    bonus = _seg64_sum(r * ksum * rk) * v
    return (yn + bonus) * _silu(z)


def _head_math(x, mo, gate, fg, tgt):
    xo = x + gate * mo
    y = xo * lax.rsqrt(jnp.mean(xo * xo, axis=-1, keepdims=True) + NORM_EPS) * fg
    err = y - tgt
    return 0.5 * jnp.sum(jnp.mean(err * err, axis=-1, keepdims=True), axis=0, keepdims=True)


def _gate_math(ghg, grw, a, b):
    return jax.nn.sigmoid(ghg) * a + jax.nn.sigmoid(grw) * b


def _call(body, name, grid, in_specs, out_specs, out_shape, scratch=(), sem=None, aliases=None):
    return pl.pallas_call(
        body, name=name, grid=grid, in_specs=in_specs, out_specs=out_specs, out_shape=out_shape,
        scratch_shapes=list(scratch), input_output_aliases=aliases or {},
        compiler_params=pltpu.CompilerParams(dimension_semantics=sem, vmem_limit_bytes=V7X_VMEM_LIMIT))


def _full(shape):
    nd = len(shape)
    return pl.BlockSpec(shape, lambda *_: (0,) * nd)


def _sds(shape, dtype=F32):
    return jax.ShapeDtypeStruct(shape, dtype)


def _proj_nt(h, wt, wt_blk0, nblk, row0, nrows, name):
    rc = min(TM, nrows)

    def body(h_ref, w_ref, o_ref):
        w = w_ref[...]

        def step(i, carry):
            r = pl.multiple_of(i * rc, rc)
            o_ref[pl.ds(r, rc), :] = lax.dot_general(h_ref[pl.ds(row0 + r, rc), :], w, (((1,), (1,)), ((), ())),
                                                     preferred_element_type=F32)
            return carry

        lax.fori_loop(0, nrows // rc, step, 0)

    return _call(body, name, (nblk,), [_full(h.shape), pl.BlockSpec((WB, wt.shape[1]), lambda j: (wt_blk0 + j, 0))],
                 pl.BlockSpec((nrows, WB), lambda j: (0, j)), _sds((nrows, nblk * WB)), sem=("parallel",))(h, wt)


def _dw_tn(a, b, row0, out_rows, out_blk0, name, prev=None):
    m, n = a.shape
    k2 = b.shape[1]
    nblk = n // WB
    rc = min(TM, m)

    def body(*refs):
        a_ref, b_ref, o_ref, acc_ref = refs[0], refs[1], refs[-2], refs[-1]
        acc_ref[...] = jnp.zeros_like(acc_ref)

        def step(i, carry):
            r = pl.multiple_of(i * rc, rc)
            acc_ref[...] += lax.dot_general(a_ref[pl.ds(r, rc), :], b_ref[pl.ds(row0 + r, rc), :], (((0,), (0,)), ((), ())),
                                            preferred_element_type=F32)
            return carry

        lax.fori_loop(0, m // rc, step, 0)
        o_ref[...] = acc_ref[...].astype(BF16)

    in_specs = [pl.BlockSpec((m, WB), lambda j: (0, j)), _full(b.shape)]
    args = [a, b]
    aliases = None
    if prev is not None:
        in_specs.append(pl.BlockSpec(memory_space=pl.ANY))
        args.append(prev)
        aliases = {2: 0}
    return _call(body, name, (nblk,), in_specs, pl.BlockSpec((WB, k2), lambda j: (out_blk0 + j, 0)), _sds((out_rows, k2), BF16),
                 scratch=[pltpu.VMEM((WB, k2), F32)], sem=("arbitrary",), aliases=aliases)(*args)


def _mm_nn(a, w, w_blk0, name):
    kb = WB
    m, kc = a.shape
    n = w.shape[1]
    tmb = 3 * TM if m % (3 * TM) == 0 else (2 * TM if m % (2 * TM) == 0 else TM)

    def body(a_ref, w_ref, o_ref):
        @pl.when(pl.program_id(1) == 0)
        def _():
            o_ref[...] = jnp.zeros_like(o_ref)

        o_ref[...] += jnp.dot(a_ref[...], w_ref[...], preferred_element_type=F32)

    return _call(body, name, (m // tmb, kc // kb),
                 [pl.BlockSpec((tmb, kb), lambda i, k: (i, k)), pl.BlockSpec((kb, n), lambda i, k: (w_blk0 + k, 0))],
                 pl.BlockSpec((tmb, n), lambda i, k: (i, 0)), _sds((m, n)), sem=("parallel", "arbitrary"))(a, w)


def _mods_fwd(cc, ada_t, ada_b):
    tn = 512 if (3 * D) % 512 == 0 else 256

    def body(cc_ref, w_ref, b_ref, o_ref):
        s = _silu(cc_ref[...]).astype(BF16)
        o_ref[...] = lax.dot_general(s, w_ref[...], (((1,), (1,)), ((), ())), preferred_element_type=F32) + b_ref[...]

    return _call(body, "mods_fwd", (3 * D // tn,),
                 [_full((8, D)), pl.BlockSpec((tn, D), lambda j: (j, 0)), pl.BlockSpec((1, tn), lambda j: (0, j))],
                 pl.BlockSpec((8, tn), lambda j: (0, j)), _sds((8, 3 * D)), sem=("parallel",))(cc, ada_t, ada_b)


def _sel_mod(mods_ref, is_ctx, lo):
    return jnp.where(is_ctx, mods_ref[1:2, lo:lo + D], mods_ref[0:1, lo:lo + D])


def _h_fwd(xcat, mods, norm_g):
    dm = _dims()

    def body(x_ref, m_ref, g_ref, o_ref):
        is_ctx = pl.program_id(0) < dm["NCT"]
        o_ref[...] = _h_math(x_ref[...], g_ref[...], _sel_mod(m_ref, is_ctx, D), _sel_mod(m_ref, is_ctx, 0)).astype(BF16)

    return _call(body, "h_fwd", (dm["TT"] // TM,),
                 [pl.BlockSpec((TM, D), lambda i: (i, 0)), _full((8, 3 * D)), _full((1, D))],
                 pl.BlockSpec((TM, D), lambda i: (i, 0)), _sds((dm["TT"], D), BF16), sem=("parallel",))(xcat, mods, norm_g)


def _chunk_of(d, s):
    dm = _dims()
    ncc, nch = dm["NCC"], dm["NCH"]
    return s if d == 0 else jnp.where(s < ncc, ncc - 1 - s, nch - 1 - (s - ncc))


HG_GROUP = 4
RW_GROUP = 2


def _dir_specs(make):
    return [make(d) for d in range(2)]


def _lanes(j):
    return slice(j * 128, (j + 1) * 128)


def _hg_scan_fwd(p_hg, lb3):
    dm = _dims()
    nh, nch, tt = dm["HG_H"], dm["NCH"], dm["TT"]
    pp = min(HG_GROUP, nh)
    lw_ = 128 * pp

    def body(q0, q1, i0, i1, f0, f1, lb0, lb1, o0, o1, ck_ref, s_ref):
        @pl.when(pl.program_id(1) == 0)
        def _():
            s_ref[...] = jnp.zeros_like(s_ref)

        keys, chains = [], []
        for d, refs in enumerate(((q0, i0, f0, lb0), (q1, i1, f1, lb1))):
            vals = [r[...] for r in refs]
            for j in range(pp):
                keys.append((d, j))
                chains.append((s_ref[d, j], *[a[:, _lanes(j)] for a in vals], d))
        for (d, j), chain, (out, s1) in zip(keys, chains, _hg_chunks(chains)):
            ck_ref[d, j] = chain[0]
            (o0, o1)[d][:, _lanes(j)] = out
            s_ref[d, j] = s1

    blk = lambda off: _dir_specs(lambda d: pl.BlockSpec((C, lw_), lambda h, s: (_chunk_of(d, s), off * nh // pp + h)))
    return _call(body, "hg_scan_fwd", (nh // pp, nch),
                 blk(0) + blk(1) + _dir_specs(lambda d: pl.BlockSpec((C, lw_), lambda h, s: (_chunk_of(d, s), (2 + d) * nh // pp + h)))
                 + _dir_specs(lambda d: pl.BlockSpec((None, 1, lw_), lambda h, s: (d, 0, h))),
                 _dir_specs(lambda d: pl.BlockSpec((C, lw_), lambda h, s: (_chunk_of(d, s), h)))
                 + [pl.BlockSpec((2, pp, None, 128, 128), lambda h, s: (0, h, s, 0, 0))],
                 [_sds((tt, HGW)), _sds((tt, HGW)), _sds((2, nh, nch, 128, 128))],
                 scratch=[pltpu.VMEM((2, pp, 128, 128), F32)], sem=("parallel", "arbitrary"))(*([p_hg] * 6), lb3, lb3)


def _hg_scan_bwd(p_hg, lb3, ck, do):
    dm = _dims()
    nh, nch, tt = dm["HG_H"], dm["NCH"], dm["TT"]
    pp = min(HG_GROUP, nh)
    lw_ = 128 * pp

    def body(q0, q1, i0, i1, f0, f1, lb0, lb1, ck_ref, do0, do1, dq0, dq1, di0, di1, df0, df1, dlb_ref, ds_ref):
        @pl.when(pl.program_id(1) == 0)
        def _():
            ds_ref[...] = jnp.zeros_like(ds_ref)
            dlb_ref[...] = jnp.zeros_like(dlb_ref)

        keys, flat, cts = [], [], []
        for d, refs in enumerate(((q0, i0, f0, lb0, do0), (q1, i1, f1, lb1, do1))):
            *vals, do_ = [r[...] for r in refs]
            for j in range(pp):
                keys.append((d, j))
                flat += [ck_ref[d, j], *[a[:, _lanes(j)] for a in vals]]
                cts.append((do_[:, _lanes(j)], ds_ref[d, j]))
        _, vjp = jax.vjp(_lockstep(_hg_chunks, [d for d, _ in keys], 5), *flat)
        grads = vjp(tuple(cts))
        for n, (d, j) in enumerate(keys):
            ds0, dq_, di_, df_, dlb = grads[5 * n:5 * n + 5]
            (dq0, dq1)[d][:, _lanes(j)] = dq_
            (di0, di1)[d][:, _lanes(j)] = di_
            (df0, df1)[d][:, _lanes(j)] = df_
            dlb_ref[d, :, _lanes(j)] += dlb
            ds_ref[d, j] = ds0

    ch = lambda d, s: _chunk_of(d, nch - 1 - s)
    blk = lambda off: _dir_specs(lambda d: pl.BlockSpec((C, lw_), lambda h, s: (ch(d, s), off * nh // pp + h)))
    tok = _dir_specs(lambda d: pl.BlockSpec((C, lw_), lambda h, s: (ch(d, s), h)))
    return _call(body, "hg_scan_bwd", (nh // pp, nch),
                 blk(0) + blk(1) + _dir_specs(lambda d: pl.BlockSpec((C, lw_), lambda h, s: (ch(d, s), (2 + d) * nh // pp + h)))
                 + _dir_specs(lambda d: pl.BlockSpec((None, 1, lw_), lambda h, s: (d, 0, h)))
                 + [pl.BlockSpec((2, pp, None, 128, 128), lambda h, s: (0, h, nch - 1 - s, 0, 0))] + tok,
                 tok * 3 + [pl.BlockSpec((2, 1, lw_), lambda h, s: (0, 0, h))],
                 [_sds((tt, HGW))] * 6 + [_sds((2, 1, HGW))],
                 scratch=[pltpu.VMEM((2, pp, 128, 128), F32)], sem=("parallel", "arbitrary"))(*([p_hg] * 6), lb3, lb3, ck, do, do)


def _rw_scan_fwd(sh, kap, lw, b, kd):
    dm = _dims()
    npair, nch, tt = dm["RW_P"], dm["NCH"], dm["TT"]
    pp = min(RW_GROUP, npair)
    lw_ = 128 * pp

    def body(r0, r1, v0, v1, ka0, ka1, lw0, lw1, b0, b1, kd0, kd1, y0, y1, ck_ref, s_ref):
        @pl.when(pl.program_id(1) == 0)
        def _():
            s_ref[...] = jnp.zeros_like(s_ref)

        keys, chains = [], []
        for d, refs in enumerate(((r0, lw0, ka0, b0, v0, kd0), (r1, lw1, ka1, b1, v1, kd1))):
            vals = [r[...] for r in refs]
            for j in range(pp):
                keys.append((d, j))
                chains.append((s_ref[d, j], *[a[:, _lanes(j)] for a in vals], d))
        for (d, j), chain, (out, s1) in zip(keys, chains, _rw_chunks(chains)):
            ck_ref[d, j] = chain[0]
            (y0, y1)[d][:, _lanes(j)] = out
            s_ref[d, j] = s1

    two = lambda off: _dir_specs(lambda d: pl.BlockSpec((C, lw_), lambda p, s: (_chunk_of(d, s), off * npair // pp + p)))
    three = _dir_specs(lambda d: pl.BlockSpec((None, C, lw_), lambda p, s: (d, _chunk_of(d, s), p)))
    return _call(body, "rw_scan_fwd", (npair // pp, nch), two(0) + two(2) + two(0) + three * 3,
                 two(0) + [pl.BlockSpec((2, pp, None, 128, 128), lambda p, s: (0, p, s, 0, 0))],
                 [_sds((tt, RWW)), _sds((tt, RWW)), _sds((2, npair, nch, 128, 128))],
                 scratch=[pltpu.VMEM((2, pp, 128, 128), F32)], sem=("parallel", "arbitrary"))(
        sh, sh, sh, sh, kap, kap, lw, lw, b, b, kd, kd)


def _rw_scan_bwd(sh, kap, lw, b, kd, ck, dy):
    dm = _dims()
    npair, nch, tt = dm["RW_P"], dm["NCH"], dm["TT"]
    pp = min(RW_GROUP, npair)
    lw_ = 128 * pp

    def body(r0, r1, v0, v1, ka0, ka1, lw0, lw1, b0, b1, kd0, kd1, ck_ref, dy0, dy1,
             dr0, dr1, dv0, dv1, dka0, dka1, dlw0, dlw1, db0, db1, dkd0, dkd1, ds_ref):
        @pl.when(pl.program_id(1) == 0)
        def _():
            ds_ref[...] = jnp.zeros_like(ds_ref)

        keys, flat, cts = [], [], []
        for d, refs in enumerate(((r0, lw0, ka0, b0, v0, kd0, dy0), (r1, lw1, ka1, b1, v1, kd1, dy1))):
            *vals, dy_ = [r[...] for r in refs]
            for j in range(pp):
                keys.append((d, j))
                flat += [ck_ref[d, j], *[a[:, _lanes(j)] for a in vals]]
                cts.append((dy_[:, _lanes(j)], ds_ref[d, j]))
        _, vjp = jax.vjp(_lockstep(_rw_chunks, [d for d, _ in keys], 7), *flat)
        grads = vjp(tuple(cts))
        outs = ((dr0, dlw0, dka0, db0, dv0, dkd0), (dr1, dlw1, dka1, db1, dv1, dkd1))
        for n, (d, j) in enumerate(keys):
            ds0, *g_in = grads[7 * n:7 * n + 7]
            for o_ref, g in zip(outs[d], g_in):
                o_ref[:, _lanes(j)] = g
            ds_ref[d, j] = ds0

    ch = lambda d, s: _chunk_of(d, nch - 1 - s)
    two = lambda off: _dir_specs(lambda d: pl.BlockSpec((C, lw_), lambda p, s: (ch(d, s), off * npair // pp + p)))
    three = _dir_specs(lambda d: pl.BlockSpec((None, C, lw_), lambda p, s: (d, ch(d, s), p)))
    return _call(body, "rw_scan_bwd", (npair // pp, nch),
                 two(0) + two(2) + two(0) + three * 3
                 + [pl.BlockSpec((2, pp, None, 128, 128), lambda p, s: (0, p, nch - 1 - s, 0, 0))] + two(0),
                 two(0) * 6, [_sds((tt, RWW))] * 12,
                 scratch=[pltpu.VMEM((2, pp, 128, 128), F32)], sem=("parallel", "arbitrary"))(
        sh, sh, sh, sh, kap, kap, lw, lw, b, b, kd, kd, ck, dy, dy)


def _tile_pos(i):
    dm = _dims()
    is_ctx = i < dm["NCT"]
    rows = lax.broadcasted_iota(jnp.int32, (TM, 1), 0)
    tl = rows + (i - dm["NCT"]) * TM
    width = jnp.where(is_ctx, TC, GW)
    colp = jnp.where(is_ctx, rows, tl) % width
    return is_ctx, rows, tl, colp, width


def _shift_lr(cur, i):
    _, _, _, colp, width = _tile_pos(i)
    left = jnp.where(colp == 0, 0.0, pltpu.roll(cur, 1, 0))
    right = jnp.where(colp == width - 1, 0.0, pltpu.roll(cur, TM - 1, 0))
    return left, right


def _shift_ud(cur, prv, nxt, i):
    is_ctx, rows, tl, _, _ = _tile_pos(i)
    if TM > GW:
        up = jnp.where(rows >= GW, pltpu.roll(cur, GW, 0), pltpu.roll(prv, GW, 0))
        down = jnp.where(rows < TM - GW, pltpu.roll(cur, TM - GW, 0), pltpu.roll(nxt, TM - GW, 0))
    else:
        up, down = prv, nxt
    up = jnp.where(jnp.logical_or(is_ctx, tl < GW), 0.0, up)
    down = jnp.where(jnp.logical_or(is_ctx, tl >= T - GW), 0.0, down)
    return up, down


def _shift_specs():
    dm = _dims()
    nt = dm["TT"] // TM
    cw = 256 if dm["SH"] % 256 == 0 else 128
    cur = pl.BlockSpec((TM, cw), lambda j, i: (i, j))
    prv = pl.BlockSpec((TM, cw), lambda j, i: (jnp.maximum(i - 1, 0), j))
    nxt = pl.BlockSpec((TM, cw), lambda j, i: (jnp.minimum(i + 1, nt - 1), j))
    mu = pl.BlockSpec((4, cw), lambda j, i: (0, j))
    return nt, cw, cur, prv, nxt, mu


def _shift_fwd(p_rw, mu):
    dm = _dims()
    nt, cw, cur, prv, nxt, mus = _shift_specs()

    def body(c_ref, p_ref, n_ref, mu_ref, o_ref):
        i = pl.program_id(1)
        p, m = c_ref[...], mu_ref[...]
        left, right = _shift_lr(p, i)
        up, down = _shift_ud(p, p_ref[...], n_ref[...], i)
        vert = jnp.where(i < dm["NCT"], 0.0, 1.0)
        o_ref[...] = (p * (1.0 - m[0:1] - m[1:2] - vert * (m[2:3] + m[3:4]))
                      + m[0:1] * left + m[1:2] * right + m[2:3] * up + m[3:4] * down)

    return _call(body, "shift_fwd", (dm["SH"] // cw, nt), [cur, prv, nxt, mus], cur, _sds((dm["TT"], dm["SH"])),
                 sem=("parallel", "parallel"))(p_rw, p_rw, p_rw, mu)


def _shift_bwd(p_rw, dsh, mu):
    dm = _dims()
    nt, cw, cur, prv, nxt, mus = _shift_specs()

    def body(c_ref, p_ref, n_ref, gc_ref, gp_ref, gn_ref, mu_ref, dp_ref, dmu_ref):
        i = pl.program_id(1)
        p, g, m = c_ref[...], gc_ref[...], mu_ref[...]
        vert = jnp.where(i < dm["NCT"], 0.0, 1.0)
        _, from_right = _shift_lr(m[0:1] * g, i)
        from_left, _ = _shift_lr(m[1:2] * g, i)
        _, from_down = _shift_ud(m[2:3] * g, g, m[2:3] * gn_ref[...], i)
        from_up, _ = _shift_ud(m[3:4] * g, m[3:4] * gp_ref[...], g, i)
        dp = g * (1.0 - m[0:1] - m[1:2] - vert * (m[2:3] + m[3:4])) + from_right + from_left + from_down + from_up
        dp_ref[...] = dp.astype(BF16)

        @pl.when(i == 0)
        def _():
            dmu_ref[...] = jnp.zeros_like(dmu_ref)

        left, right = _shift_lr(p, i)
        up, down = _shift_ud(p, p_ref[...], n_ref[...], i)
        s = lambda a: jnp.sum(a, axis=0, keepdims=True)
        dmu_ref[...] += jnp.concatenate([s(g * (left - p)), s(g * (right - p)), vert * s(g * (up - p)), vert * s(g * (down - p))], axis=0)

    return _call(body, "shift_bwd", (dm["SH"] // cw, nt), [cur, prv, nxt, cur, prv, nxt, mus],
                 [cur, pl.BlockSpec((4, cw), lambda j, i: (0, j))], [_sds((dm["TT"], dm["SH"]), BF16), _sds((4, dm["SH"]))],
                 sem=("parallel", "arbitrary"))(p_rw, p_rw, p_rw, dsh, dsh, dsh, mu)


def _acc_out(ref, first, vals):
    @pl.when(first)
    def _():
        for r in ref:
            r[...] = jnp.zeros_like(r)

    for r, v in zip(ref, vals):
        r[...] += v


def _rowsum(a):
    return jnp.sum(a, axis=0, keepdims=True)


def _rw_prep_specs():
    dm = _dims()
    rw = RWW // 128
    tok = lambda width, cb: pl.BlockSpec((TMV, width), lambda i: (i, cb))
    ins = [pl.BlockSpec((TMV, RWW), lambda i: (i, 1)), tok(128, 3 * rw), tok(128, 3 * rw + 1),
           _full((2, RWW)), _full((128, RWW)), _full((128, RWW)), _full((2, RWW)), _full((128, RWW)), _full((128, RWW)),
           _full((1, RWW)), _full((1, RWW))]
    return dm, ins


def _rw_prep_fwd(sh, w0, w2p0, w2p1, a0, a2p0, a2p1, kkp, kap_p):
    dm, ins = _rw_prep_specs()
    tt = dm["TT"]

    def body(k_ref, low_ref, loa_ref, w0_ref, w20_ref, w21_ref, a0_ref, a20_ref, a21_ref, kk_ref, ka_ref, kap_ref, lw_ref, b_ref, kd_ref):
        tw, la = jnp.tanh(low_ref[...]), loa_ref[...]
        kap, lw0, b0, kd0, lw1, b1, kd1 = _rw_prep_math(
            k_ref[...], _mm(tw, w20_ref[...]), _mm(tw, w21_ref[...]), _mm(la, a20_ref[...]), _mm(la, a21_ref[...]),
            w0_ref[...], a0_ref[...], kk_ref[...], ka_ref[...])
        kap_ref[...] = kap
        lw_ref[0], lw_ref[1] = lw0, lw1
        b_ref[0], b_ref[1] = b0, b1
        kd_ref[0], kd_ref[1] = kd0, kd1

    two = pl.BlockSpec((2, TMV, RWW), lambda i: (0, i, 0))
    return _call(body, "rw_prep_fwd", (tt // TMV,), ins, [pl.BlockSpec((TMV, RWW), lambda i: (i, 0)), two, two, two],
                 [_sds((tt, RWW)), _sds((2, tt, RWW)), _sds((2, tt, RWW)), _sds((2, tt, RWW))], sem=("parallel",))(
        sh, sh, sh, w0, w2p0, w2p1, a0, a2p0, a2p1, kkp, kap_p)


def _rw_prep_bwd(sh, w0, w2p0, w2p1, a0, a2p0, a2p1, kkp, kap_p, scan_grads, dkd_p, dr_p, dv_p):
    dm, ins = _rw_prep_specs()
    tt, sh_w = dm["TT"], dm["SH"]
    one = pl.BlockSpec((TMV, RWW), lambda i: (i, 0))

    def body(k_ref, low_ref, loa_ref, w0_ref, w20_ref, w21_ref, a0_ref, a20_ref, a21_ref, kk_ref, ka_ref,
             dr0, dr1, dv0, dv1, dka0, dka1, dlw0, dlw1, db0, db1, dkd0, dkd1, dkdp_ref, drp_ref, dvp_ref,
             dsh_ref, dw0_ref, dw20_ref, dw21_ref, da0_ref, da20_ref, da21_ref, dkk_ref, dka_ref):
        tw, la = jnp.tanh(low_ref[...]), loa_ref[...]
        w20, w21, a20, a21 = w20_ref[...], w21_ref[...], a20_ref[...], a21_ref[...]
        _, vjp = jax.vjp(_rw_prep_math, k_ref[...], _mm(tw, w20), _mm(tw, w21), _mm(la, a20), _mm(la, a21),
                         w0_ref[...], a0_ref[...], kk_ref[...], ka_ref[...])
        dkp = dkdp_ref[...]
        dk, dpw0, dpw1, dpa0, dpa1, dw0, da0, dkk, dka = vjp(
            (dka0[...] + dka1[...], dlw0[...], db0[...], dkd0[...] + dkp, dlw1[...], db1[...], dkd1[...] + dkp))
        twt, lat = tw.T, la.T
        dtw = _mm_nt(dpw0, w20) + _mm_nt(dpw1, w21)
        dsh_ref[:, 0:RWW] = dr0[...] + dr1[...] + drp_ref[...]
        dsh_ref[:, RWW:2 * RWW] = dk
        dsh_ref[:, 2 * RWW:3 * RWW] = dv0[...] + dv1[...] + dvp_ref[...]
        dsh_ref[:, 3 * RWW:3 * RWW + 128] = dtw * (1.0 - tw * tw)
        dsh_ref[:, 3 * RWW + 128:3 * RWW + 256] = _mm_nt(dpa0, a20) + _mm_nt(dpa1, a21)
        _acc_out((dw0_ref, dw20_ref, dw21_ref, da0_ref, da20_ref, da21_ref, dkk_ref, dka_ref), pl.program_id(0) == 0,
                 (dw0, _mm(twt, dpw0), _mm(twt, dpw1), da0, _mm(lat, dpa0), _mm(lat, dpa1), dkk, dka))

    par = [_full((2, RWW)), _full((128, RWW)), _full((128, RWW)), _full((2, RWW)), _full((128, RWW)), _full((128, RWW)),
           _full((1, RWW)), _full((1, RWW))]
    return _call(body, "rw_prep_bwd", (tt // TMV,), ins + [one] * 15,
                 [pl.BlockSpec((TMV, sh_w), lambda i: (i, 0))] + par,
                 [_sds((tt, sh_w)), _sds((2, RWW)), _sds((128, RWW)), _sds((128, RWW)), _sds((2, RWW)), _sds((128, RWW)), _sds((128, RWW)),
                  _sds((1, RWW)), _sds((1, RWW))],
                 sem=("arbitrary",))(sh, sh, sh, w0, w2p0, w2p1, a0, a2p0, a2p1, kkp, kap_p, *scan_grads, dkd_p, dr_p, dv_p)


def _rw_post_specs():
    dm = _dims()
    nv = dm["NCTV"]
    rw = RWW // 128
    lat3 = lambda d: pl.BlockSpec((None, TMV, RWW), lambda i: (d, nv + i, 0))
    lat2 = pl.BlockSpec((TMV, RWW), lambda i: (nv + i, 0))
    ins = [lat2, lat2, lat3(0), lat3(1),
           pl.BlockSpec((TMV, RWW), lambda i: (nv + i, 0)), pl.BlockSpec((TMV, RWW), lambda i: (nv + i, 2)),
           pl.BlockSpec((TMV, RWW), lambda i: (i, 0)), _full((1, RWW)), _full((1, RWW)), _full((1, RWW))]
    return dm, nv, ins


def _rw_post_fwd(y0_, y1_, kd, sh, p_z, rk, gng, gnb):
    dm, nv, ins = _rw_post_specs()

    def body(y0, y1, k0, k1, r, v, z, rk_ref, g_ref, b_ref, o_ref):
        o_ref[...] = _rw_post_math(y0[...], y1[...], k0[...], k1[...], r[...], v[...], z[...], rk_ref[...], g_ref[...], b_ref[...]).astype(BF16)

    return _call(body, "rw_post_fwd", (T // TMV,), ins, pl.BlockSpec((TMV, RWW), lambda i: (i, 0)), _sds((T, RWW), BF16),
                 sem=("parallel",))(y0_, y1_, kd, kd, sh, sh, p_z, rk, gng, gnb)


def _rw_post_bwd(y0_, y1_, kd, sh, p_z, rk, gng, gnb, dout):
    dm, nv, _ = _rw_post_specs()
    tt = dm["TT"]
    cl = lambda i: jnp.maximum(i - nv, 0)
    all3 = lambda d: pl.BlockSpec((None, TMV, RWW), lambda i: (d, i, 0))
    all2 = pl.BlockSpec((TMV, RWW), lambda i: (i, 0))
    ins = [all2, all2, all3(0), all3(1),
           pl.BlockSpec((TMV, RWW), lambda i: (i, 0)), pl.BlockSpec((TMV, RWW), lambda i: (i, 2)),
           pl.BlockSpec((TMV, RWW), lambda i: (cl(i), 0)), _full((1, RWW)), _full((1, RWW)), _full((1, RWW)),
           pl.BlockSpec((TMV, RWW), lambda i: (cl(i), 0))]
    tok = pl.BlockSpec((TMV, RWW), lambda i: (i, 0))

    def body(y0, y1, k0, k1, r, v, z, rk_ref, g_ref, b_ref, do_ref, dy_ref, dkd_ref, dr_ref, dv_ref, dz_ref, drk_ref, dg_ref, db_ref):
        i = pl.program_id(0)
        lat = jnp.where(i >= nv, 1.0, 0.0)
        _, vjp = jax.vjp(_rw_post_math, y0[...], y1[...], k0[...], k1[...], r[...], v[...], z[...], rk_ref[...], g_ref[...], b_ref[...])
        dy0, _, dk0, _, dr, dv, dz, drk, dg, db = vjp(do_ref[...] * lat)
        dy_ref[...] = dy0
        dkd_ref[...] = dk0
        dr_ref[...] = dr
        dv_ref[...] = dv
        dz_ref[...] = dz.astype(BF16)
        _acc_out((drk_ref, dg_ref, db_ref), i == 0, (drk, dg, db))

    return _call(body, "rw_post_bwd", (tt // TMV,), ins,
                 [tok, tok, tok, tok, pl.BlockSpec((TMV, RWW), lambda i: (cl(i), 0)), _full((1, RWW)), _full((1, RWW)), _full((1, RWW))],
                 [_sds((tt, RWW))] * 4 + [_sds((T, RWW), BF16), _sds((1, RWW)), _sds((1, RWW)), _sds((1, RWW))],
                 sem=("arbitrary",))(y0_, y1_, kd, kd, sh, sh, p_z, rk, gng, gnb, dout)


def _hg_post_fwd(o0_, o1_, p_hg, hg_g):
    dm = _dims()
    nv = dm["NCTV"]
    lat3 = lambda d: pl.BlockSpec((TMV, HGW), lambda i: (nv + i, 0))

    def body(o0, o1, z, g, out):
        out[...] = _hg_post_math(o0[...], o1[...], z[...], g[...]).astype(BF16)

    return _call(body, "hg_post_fwd", (T // TMV,), [lat3(0), lat3(1), pl.BlockSpec((TMV, HGW), lambda i: (nv + i, 4)), _full((1, HGW))],
                 pl.BlockSpec((TMV, HGW), lambda i: (i, 0)), _sds((T, HGW), BF16), sem=("parallel",))(o0_, o1_, p_hg, hg_g)


def _hg_post_bwd(o0_, o1_, p_hg, hg_g, dout):
    dm = _dims()
    nv, tt = dm["NCTV"], dm["TT"]
    all3 = lambda d: pl.BlockSpec((TMV, HGW), lambda i: (i, 0))
    tok = pl.BlockSpec((TMV, HGW), lambda i: (i, 0))

    def body(o0, o1, z, g, do_in, do_ref, dz_ref, dg_ref):
        i = pl.program_id(0)
        lat = jnp.where(i >= nv, 1.0, 0.0)
        _, vjp = jax.vjp(_hg_post_math, o0[...], o1[...], z[...], g[...])
        d0, _, dz, dg = vjp(do_in[...] * lat)
        do_ref[...] = d0
        dz_ref[...] = dz
        _acc_out((dg_ref,), i == 0, (dg,))

    return _call(body, "hg_post_bwd", (tt // TMV,),
                 [all3(0), all3(1), pl.BlockSpec((TMV, HGW), lambda i: (i, 4)), _full((1, HGW)),
                  pl.BlockSpec((TMV, HGW), lambda i: (jnp.maximum(i - nv, 0), 0))],
                 [tok, tok, _full((1, HGW))], [_sds((tt, HGW)), _sds((tt, HGW)), _sds((1, HGW))], sem=("arbitrary",))(o0_, o1_, p_hg, hg_g, dout)


def _hg_dproj(dq0, dq1, di0, di1, df0, df1, dz):
    dm = _dims()
    tt = dm["TT"]
    tok = pl.BlockSpec((TMV, HGW), lambda i: (i, 0))

    def body(q0, q1, i0, i1, f0, f1, z, o_ref):
        o_ref[:, 0:HGW] = (q0[...] + q1[...]).astype(BF16)
        o_ref[:, HGW:2 * HGW] = (i0[...] + i1[...]).astype(BF16)
        o_ref[:, 2 * HGW:3 * HGW] = f0[...].astype(BF16)
        o_ref[:, 3 * HGW:4 * HGW] = f1[...].astype(BF16)
        o_ref[:, 4 * HGW:5 * HGW] = z[...].astype(BF16)

    return _call(body, "hg_dproj", (tt // TMV,), [tok] * 7,
                 pl.BlockSpec((TMV, 5 * HGW), lambda i: (i, 0)), _sds((tt, 5 * HGW), BF16), sem=("parallel",))(
        dq0, dq1, di0, di1, df0, df1, dz)


def _merge1_fwd(out_hg, out_rw, whg_t, wrw_t, p_g):
    tok = pl.BlockSpec((TM, D), lambda i: (i, 0))

    def body(h_ref, r_ref, wh_ref, wr_ref, g1_ref, g2_ref, a_ref, b_ref, m_ref):
        nt = (((1,), (1,)), ((), ()))
        a = lax.dot_general(h_ref[...], wh_ref[...], nt, preferred_element_type=F32)
        b = lax.dot_general(r_ref[...], wr_ref[...], nt, preferred_element_type=F32)
        a_ref[...] = a
        b_ref[...] = b
        m_ref[...] = _gate_math(g1_ref[...], g2_ref[...], a, b).astype(BF16)

    return _call(body, "merge1_fwd", (T // TM,),
                 [pl.BlockSpec((TM, HGW), lambda i: (i, 0)), pl.BlockSpec((TM, RWW), lambda i: (i, 0)), _full((D, HGW)), _full((D, RWW)),
                  pl.BlockSpec((TM, D), lambda i: (i, 0)), pl.BlockSpec((TM, D), lambda i: (i, 1))],
                 [tok, tok, tok], [_sds((T, D)), _sds((T, D)), _sds((T, D), BF16)], sem=("parallel",))(out_hg, out_rw, whg_t, wrw_t, p_g, p_g)


def _head_fwd_bwd(x, merged, w_out, mods, final_g, tgt):
    tok = pl.BlockSpec((TM, D), lambda i: (i, 0))

    def body(x_ref, m_ref, w_ref, mods_ref, fg_ref, t_ref, dxo_ref, dmo_ref, loss_ref, dgate_ref, dfg_ref):
        mo = jnp.dot(m_ref[...], w_ref[...], preferred_element_type=F32)
        gate = mods_ref[0:1, 2 * D:3 * D]
        loss, vjp = jax.vjp(lambda x_, mo_, g_, fg_: _head_math(x_, mo_, g_, fg_, t_ref[...]), x_ref[...], mo, gate, fg_ref[...])
        dx, dmo, dgate, dfg = vjp(jnp.ones((1, 1), F32))
        dxo_ref[...] = dx
        dmo_ref[...] = dmo.astype(BF16)
        _acc_out((loss_ref, dgate_ref, dfg_ref), pl.program_id(0) == 0, (jnp.broadcast_to(loss, (1, 128)), dgate, dfg))

    return _call(body, "head_fwd_bwd", (T // TM,), [tok, tok, _full((D, D)), _full((8, 3 * D)), _full((1, D)), tok],
                 [tok, tok, _full((1, 128)), _full((1, D)), _full((1, D))],
                 [_sds((T, D)), _sds((T, D), BF16), _sds((1, 128)), _sds((1, D)), _sds((1, D))], sem=("arbitrary",))(
        x, merged, w_out, mods, final_g, tgt)


def _merge_bwd(dmo, w_out, p_g, a, b):
    tok = pl.BlockSpec((TM, D), lambda i: (i, 0))

    def body(dmo_ref, w_ref, g1_ref, g2_ref, a_ref, b_ref, da_ref, db_ref, dg_ref):
        dm_ = lax.dot_general(dmo_ref[...], w_ref[...], (((1,), (1,)), ((), ())), preferred_element_type=F32)
        _, vjp = jax.vjp(_gate_math, g1_ref[...], g2_ref[...], a_ref[...], b_ref[...])
        dg1, dg2, da, db = vjp(dm_)
        da_ref[...] = da.astype(BF16)
        db_ref[...] = db.astype(BF16)
        dg_ref[:, 0:D] = dg1.astype(BF16)
        dg_ref[:, D:2 * D] = dg2.astype(BF16)

    return _call(body, "merge_bwd", (T // TM,),
                 [tok, _full((D, D)), pl.BlockSpec((TM, D), lambda i: (i, 0)), pl.BlockSpec((TM, D), lambda i: (i, 1)), tok, tok],
                 [tok, tok, pl.BlockSpec((TM, 2 * D), lambda i: (i, 0))], [_sds((T, D), BF16), _sds((T, D), BF16), _sds((T, 2 * D), BF16)],
                 sem=("parallel",))(dmo, w_out, p_g, p_g, a, b)


def _h_bwd(xcat, mods, norm_g, dh_hg, dh_rw, dh_z, dh_g, dxo, dgate):
    dm = _dims()
    nct, tt = dm["NCT"], dm["TT"]
    tok = pl.BlockSpec((TM, D), lambda i: (i, 0))
    lat = pl.BlockSpec((TM, D), lambda i: (jnp.maximum(i - nct, 0), 0))

    def body(x_ref, m_ref, g_ref, d1, d2, d3, d4, dxo_ref, dgate_ref, gx_ref, dng_ref, dmods_ref):
        i = pl.program_id(0)
        is_ctx = i < nct
        latf = jnp.where(is_ctx, 0.0, 1.0)
        dh = d1[...] + d2[...] + latf * (d3[...] + d4[...])
        _, vjp = jax.vjp(_h_math, x_ref[...], g_ref[...], _sel_mod(m_ref, is_ctx, D), _sel_mod(m_ref, is_ctx, 0))
        dx, dng, dscale, dshift = vjp(dh)
        gx_ref[...] = dx + dxo_ref[...]

        @pl.when(i == 0)
        def _():
            dng_ref[...] = jnp.zeros_like(dng_ref)
            dmods_ref[...] = jnp.zeros_like(dmods_ref)
            dmods_ref[0:1, 2 * D:3 * D] = dgate_ref[...]

        dng_ref[...] += dng
        row = lax.broadcasted_iota(jnp.int32, (8, 1), 0)
        sel = jnp.where(row == jnp.where(is_ctx, 1, 0), 1.0, 0.0)
        dmods_ref[:, 0:D] += sel * dshift
        dmods_ref[:, D:2 * D] += sel * dscale

    return _call(body, "h_bwd", (tt // TM,), [tok, _full((8, 3 * D)), _full((1, D)), tok, tok, lat, lat, lat, _full((1, D))],
                 [lat, _full((1, D)), _full((8, 3 * D))], [_sds((T, D)), _sds((1, D)), _sds((8, 3 * D))], sem=("arbitrary",))(
        xcat, mods, norm_g, dh_hg, dh_rw, dh_z, dh_g, dxo, dgate)


def _mods_bwd(cc, ada_t, dmods, dmods_t):
    tn = 512 if (3 * D) % 512 == 0 else 256

    nsteps = 3 * D // tn

    def body(cc_ref, w_ref, dm_ref, dmt_ref, dcc_ref, dw_ref, db_ref):
        c_ = cc_ref[...]
        s, vjp = jax.vjp(_silu, c_)
        dm_ = dm_ref[...]
        ds = jnp.dot(dm_.astype(BF16), w_ref[...], preferred_element_type=F32)
        _acc_out((dcc_ref,), pl.program_id(0) == 0, (ds,))

        @pl.when(pl.program_id(0) == nsteps - 1)
        def _():
            dcc_ref[...] = vjp(dcc_ref[...])[0]

        dw_ref[...] = _mm(dmt_ref[...], s).astype(BF16)
        db_ref[...] = dm_[0:1] + dm_[1:2]

    return _call(body, "mods_bwd", (nsteps,),
                 [_full((8, D)), pl.BlockSpec((tn, D), lambda j: (j, 0)), pl.BlockSpec((8, tn), lambda j: (0, j)),
                  pl.BlockSpec((tn, 8), lambda j: (j, 0))],
                 [_full((8, D)), pl.BlockSpec((tn, D), lambda j: (j, 0)), pl.BlockSpec((1, tn), lambda j: (0, j))],
                 [_sds((8, D)), _sds((3 * D, D), BF16), _sds((1, 3 * D))], sem=("arbitrary",))(cc, ada_t, dmods, dmods_t)


def _lb_math(l0, l1):
    return jax.nn.sigmoid(l0 - l1)


def _lb_fwd(l0, l1):
    def body(a_ref, b_ref, o_ref):
        o_ref[...] = _lb_math(a_ref[...], b_ref[...])

    return _call(body, "lb_fwd", (1,), [_full((2, HGW))] * 2, _full((2, HGW)), _sds((2, HGW)))(l0, l1)


def _lb_bwd(l0, l1, dlb):
    def body(a_ref, b_ref, d_ref, da_ref, db_ref):
        _, vjp = jax.vjp(_lb_math, a_ref[...], b_ref[...])
        da_ref[...], db_ref[...] = vjp(d_ref[...])

    return _call(body, "lb_bwd", (1,), [_full((2, HGW))] * 3, [_full((2, HGW))] * 2, [_sds((2, HGW))] * 2)(l0, l1, dlb)


def _local_step(x, c, ctx, tgt, w):
    dm = _dims()
    tt, sh_w, hgc = dm["TT"], dm["SH"], dm["HGC"]
    nb = lambda cols: cols // WB
    xcat = jnp.concatenate([ctx, x], axis=0)
    cc = jnp.concatenate([c, w["c_ctx"], jnp.zeros((6, D), F32)], axis=0)
    mods = _mods_fwd(cc, w["ada_t"], w["ada_b"])
    h = _h_fwd(xcat, mods, w["norm_g"])
    win_t = w["win_t"]
    p_hg = _proj_nt(h, win_t, 0, nb(hgc), 0, tt, "proj_hg")
    p_rw = _proj_nt(h, win_t, nb(hgc), nb(sh_w), 0, tt, "proj_rw")
    p_z = _proj_nt(h, win_t, nb(hgc + sh_w), nb(RWW), TC, T, "proj_z")
    p_g = _proj_nt(h, win_t, nb(hgc + sh_w + RWW), nb(2 * D), TC, T, "proj_g")

    lb3 = _lb_fwd(w["lb0"], w["lb1"]).reshape(2, 1, HGW)
    o0, o1, hg_ck = _hg_scan_fwd(p_hg, lb3)
    out_hg = _hg_post_fwd(o0, o1, p_hg, w["hg_g"])

    sh = _shift_fwd(p_rw, w["mu"])
    zpad = jnp.zeros((LORA, RWW), F32)
    prep_w = (w["w0"], jnp.concatenate([w["w2"][0], zpad], 0), jnp.concatenate([zpad, w["w2"][1]], 0),
              w["a0"], jnp.concatenate([w["a2"][0], zpad], 0), jnp.concatenate([zpad, w["a2"][1]], 0), w["kk"], w["ka"])
    kap, lw, b, kd = _rw_prep_fwd(sh, *prep_w)
    y0, y1, rw_ck = _rw_scan_fwd(sh, kap, lw, b, kd)
    post_w = (w["rk"], w["gng"], w["gnb"])
    out_rw = _rw_post_fwd(y0, y1, kd, sh, p_z, *post_w)

    a, bb, merged = _merge1_fwd(out_hg, out_rw, w["whg_t"], w["wrw_t"], p_g)
    dxo, dmo, loss, dgate, d_fg = _head_fwd_bwd(x, merged, w["wout"], mods, w["final_g"], tgt)
    da, db, dp_g = _merge_bwd(dmo, w["wout"], p_g, a, bb)
    g_wout = _dw_tn(merged, dmo, 0, D, 0, "dw_out")
    g_whg = _dw_tn(da, out_hg, 0, D, 0, "dw_hg")
    g_wrw = _dw_tn(db, out_rw, 0, D, 0, "dw_rw")
    d_out_hg = _mm_nn(da, w["whg_t"], 0, "dx_hg")
    d_out_rw = _mm_nn(db, w["wrw_t"], 0, "dx_rw")

    do, dz_hg, d_hg_g = _hg_post_bwd(o0, o1, p_hg, w["hg_g"], d_out_hg)
    *hg_grads, dlb3 = _hg_scan_bwd(p_hg, lb3, hg_ck, do)
    dp_hg = _hg_dproj(*hg_grads, dz_hg)
    d_lb0, d_lb1 = _lb_bwd(w["lb0"], w["lb1"], dlb3.reshape(2, HGW))

    dy, dkd_p, dr_p, dv_p, dp_z, d_rk, d_gng, d_gnb = _rw_post_bwd(y0, y1, kd, sh, p_z, *post_w, d_out_rw)
    rw_grads = _rw_scan_bwd(sh, kap, lw, b, kd, rw_ck, dy)
    dsh, d_w0, d_w2p0, d_w2p1, d_a0, d_a2p0, d_a2p1, d_kk, d_ka = _rw_prep_bwd(sh, *prep_w, rw_grads, dkd_p, dr_p, dv_p)
    dp_rw, d_mu = _shift_bwd(p_rw, dsh, w["mu"])

    g_win = _dw_tn(dp_hg, h, 0, dm["NCOLS"], 0, "dw_in_hg")
    g_win = _dw_tn(dp_rw, h, 0, dm["NCOLS"], nb(hgc), "dw_in_rw", prev=g_win)
    g_win = _dw_tn(dp_z, h, TC, dm["NCOLS"], nb(hgc + sh_w), "dw_in_z", prev=g_win)
    g_win = _dw_tn(dp_g, h, TC, dm["NCOLS"], nb(hgc + sh_w + RWW), "dw_in_g", prev=g_win)
    dh_hg = _mm_nn(dp_hg, win_t, 0, "dh_hg")
    dh_rw = _mm_nn(dp_rw, win_t, nb(hgc), "dh_rw")
    dh_z = _mm_nn(dp_z, win_t, nb(hgc + sh_w), "dh_z")
    dh_g = _mm_nn(dp_g, win_t, nb(hgc + sh_w + RWW), "dh_g")
    gx, d_ng, dmods = _h_bwd(xcat, mods, w["norm_g"], dh_hg, dh_rw, dh_z, dh_g, dxo, dgate)
    dcc, g_ada, d_ada_b = _mods_bwd(cc, w["ada_t"], dmods, dmods.T)

    big = dict(win_t=g_win, wout=g_wout, whg_t=g_whg, wrw_t=g_wrw, ada_t=g_ada)
    small = dict(c_ctx=dcc[1:2], ada_b=d_ada_b, norm_g=d_ng, lb0=d_lb0, lb1=d_lb1, hg_g=d_hg_g, mu=d_mu, w0=d_w0,
                 w2=jnp.stack([d_w2p0[:LORA], d_w2p1[LORA:]]), a0=d_a0, a2=jnp.stack([d_a2p0[:LORA], d_a2p1[LORA:]]),
                 kk=d_kk, ka=d_ka, rk=d_rk, gng=d_gng, gnb=d_gnb, final_g=d_fg)
    return loss, gx, big, small


MESH = pl.DeviceIdType.MESH


def _comm_call(body, name, bufs, out_shapes, nsem, nloc):
    hbm = pl.BlockSpec(memory_space=pl.ANY)
    return pl.pallas_call(
        body, name=name, in_specs=[hbm] * len(bufs), out_specs=[hbm] * len(out_shapes), out_shape=out_shapes,
        scratch_shapes=[pltpu.SemaphoreType.DMA((nsem,)), pltpu.SemaphoreType.DMA((nsem,)), pltpu.SemaphoreType.DMA((nloc,))],
    )(*bufs)


def _gather2(bufs, name):
    nbuf = len(bufs)

    def body(*refs):
        ins, outs = refs[:nbuf], refs[nbuf:2 * nbuf]
        send_sems, recv_sems, loc_sems = refs[2 * nbuf:]
        x, y, c = lax.axis_index("x"), lax.axis_index("y"), lax.axis_index("c")
        me, sib = (x, y, c), (x, y, 1 - c)
        chips = [(1 - x, y), (x, 1 - y), (1 - x, 1 - y)]

        def copy(bi, k, block, to, src=None):
            rows = outs[bi].at[4 * block[0] + 2 * block[1] + block[2]]
            return pltpu.make_async_remote_copy(src_ref=rows if src is None else src, dst_ref=rows, send_sem=send_sems.at[7 * bi + k],
                                                recv_sem=recv_sems.at[7 * bi + k], device_id=to, device_id_type=MESH)

        own = [pltpu.make_async_copy(ins[bi], outs[bi].at[4 * x + 2 * y + c], loc_sems.at[bi]) for bi in range(nbuf)]
        for cp in own:
            cp.start()
        first = []
        for bi in range(nbuf):
            first.append(copy(bi, 0, me, sib, src=ins[bi]))
            first += [copy(bi, 1 + j, me, (*chip, c), src=ins[bi]) for j, chip in enumerate(chips)]
        for cp in first:
            cp.start()
        passed = []
        for j, chip in enumerate(chips):
            for bi in range(nbuf):
                copy(bi, 1 + j, (*chip, c), me).wait_recv()
                passed.append(copy(bi, 4 + j, (*chip, c), sib))
                passed[-1].start()
        for bi in range(nbuf):
            copy(bi, 0, sib, me).wait_recv()
            for j, chip in enumerate(chips):
                copy(bi, 4 + j, (*chip, 1 - c), me).wait_recv()
        for cp in first + passed:
            cp.wait_send()
        for cp in own:
            cp.wait()

    return _comm_call(body, name, bufs, [_sds((NDEV,) + b.shape, b.dtype) for b in bufs], 7 * nbuf, nbuf)


def _pair_exchange(bufs, name):
    nbuf = len(bufs)

    def body(*refs):
        ins, kept, got = refs[:nbuf], refs[nbuf:2 * nbuf], refs[2 * nbuf:3 * nbuf]
        send_sems, recv_sems, loc_sems = refs[3 * nbuf:]
        x, y, c = lax.axis_index("x"), lax.axis_index("y"), lax.axis_index("c")
        copies = []
        for bi in range(nbuf):
            for q in range(4):
                loc = pltpu.make_async_copy(ins[bi].at[2 * q + c], kept[bi].at[q], loc_sems.at[4 * bi + q])
                rem = pltpu.make_async_remote_copy(src_ref=ins[bi].at[2 * q + 1 - c], dst_ref=got[bi].at[q], send_sem=send_sems.at[4 * bi + q],
                                                   recv_sem=recv_sems.at[4 * bi + q], device_id=(x, y, 1 - c), device_id_type=MESH)
                loc.start()
                rem.start()
                copies.append((loc, rem))
        for loc, rem in copies:
            rem.wait_send()
            rem.wait_recv()
            loc.wait()

    shapes = [_sds((4,) + b.shape[1:], b.dtype) for b in bufs]
    return _comm_call(body, name, bufs, shapes + shapes, 4 * nbuf, 4 * nbuf)


def _chip_exchange(bufs, name):
    nbuf = len(bufs)

    def body(*refs):
        ins, outs = refs[:nbuf], refs[nbuf:2 * nbuf]
        send_sems, recv_sems, loc_sems = refs[2 * nbuf:]
        x, y, c = lax.axis_index("x"), lax.axis_index("y"), lax.axis_index("c")
        myq = 2 * x + y
        copies = []
        for bi in range(nbuf):
            loc = pltpu.make_async_copy(ins[bi].at[myq], outs[bi].at[myq], loc_sems.at[bi])
            loc.start()
            copies.append((loc, None))
            for j, (qx, qy) in enumerate([(1 - x, y), (x, 1 - y), (1 - x, 1 - y)]):
                q = 2 * qx + qy
                send = pltpu.make_async_remote_copy(src_ref=ins[bi].at[q], dst_ref=outs[bi].at[myq], send_sem=send_sems.at[3 * bi + j],
                                                    recv_sem=recv_sems.at[3 * bi + j], device_id=(qx, qy, c), device_id_type=MESH)
                send.start()
                recv = pltpu.make_async_remote_copy(src_ref=ins[bi].at[myq], dst_ref=outs[bi].at[q], send_sem=send_sems.at[3 * bi + j],
                                                    recv_sem=recv_sems.at[3 * bi + j], device_id=(qx, qy, c), device_id_type=MESH)
                copies.append((send, recv))
        for first, recv in copies:
            if recv is None:
                first.wait()
            else:
                first.wait_send()
                recv.wait_recv()

    return _comm_call(body, name, bufs, [_sds(b.shape, b.dtype) for b in bufs], 3 * nbuf, nbuf)


def _pair_add(kept, got, name):
    _, rows, cols = kept.shape
    tr = _row_tile(rows, 16, 256)

    def body(a_ref, b_ref, o_ref):
        o_ref[...] = (a_ref[...].astype(F32) + b_ref[...].astype(F32)).astype(o_ref.dtype)

    blk = pl.BlockSpec((4, tr, cols), lambda i: (0, i, 0))
    return _call(body, name, (rows // tr,), [blk, blk], blk, _sds(kept.shape, kept.dtype), sem=("parallel",))(kept, got)


def _row_tile(rows, mult, cap):
    best = mult
    for t in range(mult, cap + 1, mult):
        if rows % t == 0:
            best = t
    assert rows % best == 0, (rows, mult)
    return best


def _reduce_sources(r, name):
    nsrc, rows, cols = r.shape
    tr = _row_tile(rows, 16 if r.dtype.itemsize == 2 else 8, 128)

    def body(r_ref, o_ref):
        acc = r_ref[0].astype(F32)
        for j in range(1, nsrc):
            acc = acc + r_ref[j].astype(F32)
        o_ref[...] = acc

    return _call(body, name, (rows // tr,), [pl.BlockSpec((nsrc, tr, cols), lambda i: (0, i, 0))],
                 pl.BlockSpec((tr, cols), lambda i: (i, 0)), _sds((rows, cols)), sem=("parallel",))(r)


def _adamw(w, g, m, v, name):
    rows, cols = w.shape
    tr = _row_tile(rows, 8, 128)
    c1 = 1.0 - ADAM_B1 ** ADAM_STEP
    c2 = 1.0 - ADAM_B2 ** ADAM_STEP

    def body(w_ref, g_ref, m_ref, v_ref, d_ref, mo_ref, vo_ref):
        g_ = g_ref[...]
        m_ = ADAM_B1 * m_ref[...] + (1.0 - ADAM_B1) * g_
        v_ = ADAM_B2 * v_ref[...] + (1.0 - ADAM_B2) * (g_ * g_)
        d_ref[...] = -ADAM_LR * ((m_ / c1) / (jnp.sqrt(v_ / c2) + ADAM_EPS) + ADAM_WD * w_ref[...])
        mo_ref[...] = m_
        vo_ref[...] = v_

    blk = pl.BlockSpec((tr, cols), lambda i: (i, 0))
    return _call(body, name, (rows // tr,), [blk] * 4, [blk] * 3, [_sds((rows, cols))] * 3, sem=("parallel",))(w, g, m, v)


SLAB_PART = 1024


def _pack(arrs):
    parts = []
    for a in arrs:
        flat = a.reshape(-1)
        pad = (-flat.shape[0]) % SLAB_PART
        if pad:
            flat = jnp.concatenate([flat, jnp.zeros((pad,), flat.dtype)])
        parts.append(flat.reshape(-1, 128))
    return jnp.concatenate(parts, axis=0)


def _unpack(slab, shapes):
    outs, row = [], 0
    for s in shapes:
        n = 1
        for d in s:
            n *= d
        rows = (n + SLAB_PART - 1) // SLAB_PART * (SLAB_PART // 128)
        outs.append(slab[row:row + rows].reshape(-1)[:n].reshape(s))
        row += rows
    return outs


def _unshard_last(g, lead):
    nl = len(lead)
    return jnp.transpose(g, tuple(range(1, nl + 1)) + (0, nl + 1)).reshape(tuple(lead) + (-1,))


def kernel(x, c, ctx, c_ctx, ada_w, ada_b, norm_g, w_in, hg_lb, hg_norm_g, rw_mu, rw_w0, rw_w2, rw_a0, rw_a2, rw_kk, rw_ka, rw_rk, rw_gn_g, rw_gn_b, w_hg_out, w_rw_out, w_out, final_g, loss_target, m_c_ctx, m_ada_w, m_ada_b, m_norm_g, m_w_in, m_hg_lb, m_hg_norm_g, m_rw_mu, m_rw_w0, m_rw_w2, m_rw_a0, m_rw_a2, m_rw_kk, m_rw_ka, m_rw_rk, m_rw_gn_g, m_rw_gn_b, m_w_hg_out, m_w_rw_out, m_w_out, m_final_g, v_c_ctx, v_ada_w, v_ada_b, v_norm_g, v_w_in, v_hg_lb, v_hg_norm_g, v_rw_mu, v_rw_w0, v_rw_w2, v_rw_a0, v_rw_a2, v_rw_kk, v_rw_ka, v_rw_rk, v_rw_gn_g, v_rw_gn_b, v_w_hg_out, v_w_rw_out, v_w_out, v_final_g):
    dm = _dims()
    me = 4 * lax.axis_index("x") + 2 * lax.axis_index("y") + lax.axis_index("c")

    big_shards = [w_in[0].T.astype(BF16), w_out[0].astype(BF16), w_hg_out[0].T.astype(BF16), w_rw_out[0].T.astype(BF16),
                  ada_w[0].T.astype(BF16)]
    sharded_small = [hg_lb, rw_mu[0], rw_w0[0], rw_w2[0], rw_a0[0], rw_a2[0]]
    gathered = _gather2(big_shards + [_pack(sharded_small)], "gather_weights")
    win_t, wout, whg_t, wrw_t, ada_t = [g.reshape(-1, g.shape[2]) for g in gathered[:5]]
    per_dev = jax.vmap(lambda s: _unpack(s, [a.shape for a in sharded_small]))(gathered[5])
    hg_lb_f, mu_f, w0_f, w2_f, a0_f, a2_f = [_unshard_last(p, p.shape[1:-1]) for p in per_dev]

    w = dict(win_t=win_t, wout=wout, whg_t=whg_t, wrw_t=wrw_t, ada_t=ada_t, c_ctx=c_ctx[None], ada_b=ada_b, norm_g=norm_g,
             lb0=hg_lb_f[:, 0], lb1=hg_lb_f[:, 1], hg_g=hg_norm_g, mu=mu_f, w0=w0_f, w2=w2_f, a0=a0_f, a2=a2_f,
             kk=rw_kk, ka=rw_ka, rk=rw_rk, gng=rw_gn_g, gnb=rw_gn_b, final_g=final_g[None])
    loss_dev, grad_x, big, small = _local_step(x[0], c, ctx[0], loss_target[0], w)
    loss = lax.psum(loss_dev[0, 0], AXES)

    small_names = ["c_ctx", "ada_b", "norm_g", "lb0", "lb1", "hg_g", "mu", "w0", "w2", "a0", "a2", "kk", "ka", "rk", "gng", "gnb", "final_g"]
    small_slab = _pack([small[n] for n in small_names])
    big_names = ["win_t", "wout", "whg_t", "wrw_t", "ada_t"]
    pair = _pair_exchange([big[n].reshape(NDEV, -1, big[n].shape[1]) for n in big_names], "pair_grads")
    chip_sums = [_pair_add(k, g, "pair_add_" + n) for k, g, n in zip(pair[:5], pair[5:], big_names)]
    recv = _chip_exchange(chip_sums, "chip_grads")
    small_all = _gather2([small_slab], "gather_small_grads")[0]
    g_win_t, g_wout, g_whg_t, g_wrw_t, g_ada_t = [_reduce_sources(r, "reduce_" + n) for r, n in zip(recv, big_names)]
    sg = dict(zip(small_names, _unpack(_reduce_sources(small_all, "reduce_small"), [small[n].shape for n in small_names])))

    def my_shard(full):
        n = full.shape[-1] // NDEV
        return lax.dynamic_slice_in_dim(full, me * n, n, axis=full.ndim - 1)

    grads = dict(
        c_ctx=sg["c_ctx"][0], ada_w=g_ada_t.T[None], ada_b=sg["ada_b"], norm_g=sg["norm_g"], w_in=g_win_t.T[None],
        hg_lb=my_shard(jnp.stack([sg["lb0"], sg["lb1"]], axis=1)), hg_norm_g=sg["hg_g"], rw_mu=my_shard(sg["mu"])[None],
        rw_w0=my_shard(sg["w0"])[None], rw_w2=my_shard(sg["w2"])[None], rw_a0=my_shard(sg["a0"])[None], rw_a2=my_shard(sg["a2"])[None],
        rw_kk=sg["kk"], rw_ka=sg["ka"], rw_rk=sg["rk"], rw_gn_g=sg["gng"], rw_gn_b=sg["gnb"],
        w_hg_out=g_whg_t.T[None], w_rw_out=g_wrw_t.T[None], w_out=g_wout[None], final_g=sg["final_g"][0])
    weights = dict(c_ctx=c_ctx, ada_w=ada_w, ada_b=ada_b, norm_g=norm_g, w_in=w_in, hg_lb=hg_lb, hg_norm_g=hg_norm_g, rw_mu=rw_mu,
                   rw_w0=rw_w0, rw_w2=rw_w2, rw_a0=rw_a0, rw_a2=rw_a2, rw_kk=rw_kk, rw_ka=rw_ka, rw_rk=rw_rk, rw_gn_g=rw_gn_g,
                   rw_gn_b=rw_gn_b, w_hg_out=w_hg_out, w_rw_out=w_rw_out, w_out=w_out, final_g=final_g)
    m_in = dict(zip(weights, (m_c_ctx, m_ada_w, m_ada_b, m_norm_g, m_w_in, m_hg_lb, m_hg_norm_g, m_rw_mu, m_rw_w0, m_rw_w2, m_rw_a0,
                              m_rw_a2, m_rw_kk, m_rw_ka, m_rw_rk, m_rw_gn_g, m_rw_gn_b, m_w_hg_out, m_w_rw_out, m_w_out, m_final_g)))
    v_in = dict(zip(weights, (v_c_ctx, v_ada_w, v_ada_b, v_norm_g, v_w_in, v_hg_lb, v_hg_norm_g, v_rw_mu, v_rw_w0, v_rw_w2, v_rw_a0,
                              v_rw_a2, v_rw_kk, v_rw_ka, v_rw_rk, v_rw_gn_g, v_rw_gn_b, v_w_hg_out, v_w_rw_out, v_w_out, v_final_g)))

    big_w = ["ada_w", "w_in", "w_hg_out", "w_rw_out", "w_out"]
    delta, new_m, new_v = {}, {}, {}
    for n in big_w:
        shp = weights[n].shape
        two = lambda a: a.reshape(shp[-2], shp[-1])
        d_, m_, v_ = _adamw(two(weights[n]), two(grads[n]), two(m_in[n]), two(v_in[n]), "adamw_" + n)
        delta[n], new_m[n], new_v[n] = d_.reshape(shp), m_.reshape(shp), v_.reshape(shp)
    rest = [n for n in weights if n not in big_w]
    shapes = [weights[n].shape for n in rest]
    d_s, m_s, v_s = _adamw(_pack([weights[n] for n in rest]), _pack([grads[n] for n in rest]), _pack([m_in[n] for n in rest]),
                           _pack([v_in[n] for n in rest]), "adamw_small")
    for n, d_, m_, v_ in zip(rest, _unpack(d_s, shapes), _unpack(m_s, shapes), _unpack(v_s, shapes)):
        delta[n], new_m[n], new_v[n] = d_, m_, v_

    order = list(weights)
    return (loss, grad_x[None], *[grads[n] for n in order], *[delta[n] for n in order],
            *[new_m[n] for n in order], *[new_v[n] for n in order])
```

```python
import functools

import jax
import jax.numpy as jnp
from jax import lax
from jax.experimental import pallas as pl
from jax.experimental.pallas import tpu as pltpu

D = 2048
T = 2048
TC = 256
GW = 64
C = 64
HGW = 1024
RWW = 1024
LORA = 64
NDEV = 8
TM = 256
TMV = 128
NORM_EPS = 1e-6
RW_GN_EPS = 64e-5
ADAM_LR, ADAM_B1, ADAM_B2, ADAM_EPS, ADAM_WD, ADAM_STEP = 0.001, 0.9, 0.999, 1e-08, 0.01, 10
WB = 256
V7X_VMEM_LIMIT = 56 * 1024 * 1024

F32 = jnp.float32
BF16 = jnp.bfloat16
AXES = ("x", "y", "c")


def _dims():
    tt = T + TC
    sh = 3 * RWW + 4 * LORA
    hgc = 5 * HGW
    return dict(TT=tt, NCC=TC // C, NCH=tt // C, HG_H=HGW // 128, RW_P=RWW // 128, SH=sh, HGC=hgc,
                NCOLS=hgc + sh + RWW + 2 * D, NCT=TC // TM, NCTV=TC // TMV)


def _mm(a, b):
    return _dot3(a, b, "nn")


def _mm_nt(a, b):
    return _dot3(a, b, "nt")


def _mm_tn(a, b):
    return _dot3(a, b, "tn")


def _split(x):
    hi = x.astype(BF16)
    return hi, (x - hi.astype(F32)).astype(BF16)


_FORMS = {"nn": (((1,), (0,)), ((), ())), "nt": (((1,), (1,)), ((), ())), "tn": (((0,), (0,)), ((), ()))}


def _dot3_raw(a, b, form):
    ah, al = _split(a)
    bh, bl = _split(b)
    d = lambda x, y: lax.dot_general(x, y, _FORMS[form], preferred_element_type=F32)
    return d(ah, bh) + (d(ah, bl) + d(al, bh))


@functools.partial(jax.custom_vjp, nondiff_argnums=(2,))
def _dot3(a, b, form):
    return _dot3_raw(a, b, form)


def _dot3_fwd(a, b, form):
    return _dot3_raw(a, b, form), (a, b)


def _dot3_bwd(form, res, g):
    a, b = res
    if form == "nn":
        return _dot3(g, b, "nt"), _dot3(a, g, "tn")
    if form == "nt":
        return _dot3(g, b, "nn"), _dot3(g, a, "tn")
    return _dot3(b, g, "nt"), _dot3(a, g, "nn")


_dot3.defvjp(_dot3_fwd, _dot3_bwd)


def _scan_cumsum(inc, x):
    return _cumsum_vjp(inc.astype(BF16), x)


def _cumsum_raw(inc, x, form):
    h1 = x.astype(BF16)
    r1 = x - h1.astype(F32)
    h2 = r1.astype(BF16)
    h3 = (r1 - h2.astype(F32)).astype(BF16)
    d = lambda y: lax.dot_general(inc, y, _FORMS[form], preferred_element_type=F32)
    return d(h1) + (d(h2) + d(h3))


@jax.custom_vjp
def _cumsum_vjp(inc, x):
    return _cumsum_raw(inc, x, "nn")


_cumsum_vjp.defvjp(lambda inc, x: (_cumsum_raw(inc, x, "nn"), inc),
                   lambda inc, g: (jnp.zeros_like(inc), _cumsum_raw(inc, g, "tn")))


def _silu(x):
    return x * jax.nn.sigmoid(x)


def _softplus(x):
    return jnp.maximum(x, 0.0) + jnp.log(1.0 + jnp.exp(-jnp.abs(x)))


def _iota2(shape, dim):
    return lax.broadcasted_iota(jnp.int32, shape, dim)


def _pair_mask():
    return (_iota2((128, 128), 0) < 64) == (_iota2((128, 128), 1) < 64)


def _seg64_sum(x):
    e = _pair_mask().astype(F32)
    parts = [_mm(x[:, g * 128:(g + 1) * 128], e) for g in range(x.shape[1] // 128)]
    return parts[0] if len(parts) == 1 else jnp.concatenate(parts, axis=1)


def _seg128_mean(x):
    parts = [jnp.broadcast_to(jnp.mean(x[:, g * 128:(g + 1) * 128], axis=1, keepdims=True), (x.shape[0], 128))
             for g in range(x.shape[1] // 128)]
    return parts[0] if len(parts) == 1 else jnp.concatenate(parts, axis=1)


def _scan_masks(d):
    lag = (_iota2((C, C), 0) - _iota2((C, C), 1)) * (1 - 2 * d)
    return lag >= 0, lag > 0


def _rw_chunks(chains):
    s0, r, lw, kap, b, v, kd, ds = zip(*chains)
    ids = range(len(chains))
    inc, strict = zip(*[_scan_masks(d) for d in ds])
    lane = _iota2((1, 128), 1)
    hm = ((lane < 64).astype(F32), (lane >= 64).astype(F32))
    eye = (_iota2((C, C), 0) == _iota2((C, C), 1)).astype(F32)
    lc = [_scan_cumsum(inc[i], lw[i]) for i in ids]
    ltot = [jnp.sum(lw[i], axis=0, keepdims=True) for i in ids]
    rt = [r[i] * jnp.exp(lc[i]) for i in ids]
    kt = [kap[i] * jnp.exp(lc[i] - lw[i]) for i in ids]
    einv = [jnp.exp(-lc[i]) for i in ids]
    kh = [kd[i] * einv[i] for i in ids]
    bh = [b[i] * einv[i] for i in ids]
    eend = [jnp.exp(ltot[i] - lc[i]) for i in ids]
    kbar = [kd[i] * eend[i] for i in ids]
    bbar = [b[i] * eend[i] for i in ids]
    lhs = [jnp.concatenate([kt[i] * hm[0], kt[i] * hm[1], rt[i] * hm[0], rt[i] * hm[1]], axis=0) for i in ids]
    gk = [_mm_nt(lhs[i], kh[i]) for i in ids]
    gb = [_mm_nt(lhs[i], bh[i]) for i in ids]
    ks = [_mm_nt(kt[i], s0[i]) for i in ids]
    ys = [_mm_nt(rt[i], s0[i]) for i in ids]
    heads = [(i, h) for i in ids for h in range(2)]
    akk = {c: jnp.where(strict[c[0]], gk[c[0]][c[1] * C:(c[1] + 1) * C], 0.0) for c in heads}
    akb = {c: jnp.where(strict[c[0]], gb[c[0]][c[1] * C:(c[1] + 1) * C], 0.0) for c in heads}
    ark = {c: jnp.where(inc[c[0]], gk[c[0]][(2 + c[1]) * C:(3 + c[1]) * C], 0.0) for c in heads}
    arb = {c: jnp.where(inc[c[0]], gb[c[0]][(2 + c[1]) * C:(3 + c[1]) * C], 0.0) for c in heads}
    vh = {c: v[c[0]] * hm[c[1]] for c in heads}
    av = {c: _mm(akk[c], vh[c]) for c in heads}
    rhs = {c: ks[c[0]] * hm[c[1]] + av[c] for c in heads}
    p = {c: -akb[c] for c in heads}
    tm = {c: eye + p[c] for c in heads}
    n = 2
    while n < C:
        p = {c: _mm(p[c], p[c]) for c in heads}
        tm = {c: tm[c] + _mm(tm[c], p[c]) for c in heads}
        n *= 2
    uh = {c: _mm(tm[c], rhs[c]) for c in heads}
    yv = {c: _mm(ark[c], vh[c]) for c in heads}
    yu = {c: _mm(arb[c], uh[c]) for c in heads}
    u = [uh[i, 0] + uh[i, 1] for i in ids]
    y = [ys[i] + (yv[i, 0] - yu[i, 0]) + (yv[i, 1] - yu[i, 1]) for i in ids]
    upd = [_mm_tn(jnp.concatenate([v[i], -u[i]], axis=0), jnp.concatenate([kbar[i], bbar[i]], axis=0)) for i in ids]
    s1 = [s0[i] * jnp.exp(ltot[i]) + jnp.where(_pair_mask(), upd[i], 0.0) for i in ids]
    return [(y[i], s1[i]) for i in ids]


HG_SUB = 16


def _hg_chunks(chains):
    s0, qr, iv, f, lb, ds = zip(*chains)
    ids = range(len(chains))
    inc = [_scan_masks(d)[0] for d in ds]
    q = [_silu(qr[i]) for i in ids]
    fg = [lb[i] + (1.0 - lb[i]) * jax.nn.sigmoid(f[i]) for i in ids]
    k = [1.0 - fg[i] for i in ids]
    g = [jnp.log(fg[i]) for i in ids]
    bc = [_scan_cumsum(inc[i], g[i]) for i in ids]
    btot = [jnp.sum(g[i], axis=0, keepdims=True) for i in ids]
    o_inter = [_mm_nt(q[i] * jnp.exp(bc[i]), s0[i]) for i in ids]
    rowi = _iota2((C, 1), 0)
    outs = [[] for _ in ids]
    for blk in range(C // HG_SUB):
        lo, hi = blk * HG_SUB, (blk + 1) * HG_SUB
        first = [lo if ds[i] == 0 else hi - 1 for i in ids]
        ref = [jnp.sum(jnp.where(rowi == first[i], bc[i] - g[i], 0.0), axis=0, keepdims=True) for i in ids]
        qi = [q[i][lo:hi] * jnp.exp(bc[i][lo:hi] - ref[i]) for i in ids]
        src = [(rowi < hi) if ds[i] == 0 else (rowi >= lo) for i in ids]
        ke = [jnp.where(src[i], k[i] * jnp.exp(jnp.where(src[i], ref[i] - bc[i], 0.0)), 0.0) for i in ids]
        a = [jnp.where(inc[i][lo:hi], _mm_nt(qi[i], ke[i]), 0.0) for i in ids]
        part = [_mm(a[i], iv[i]) for i in ids]
        for i in ids:
            outs[i].append(part[i])
    o = [o_inter[i] + jnp.concatenate(outs[i], axis=0) for i in ids]
    upd = [_mm_tn(iv[i], k[i] * jnp.exp(btot[i] - bc[i])) for i in ids]
    s1 = [s0[i] * jnp.exp(btot[i]) + upd[i] for i in ids]
    return [(o[i], s1[i]) for i in ids]


def _lockstep(fn, ds, nargs):
    def flat_fn(*flat):
        return tuple(fn([tuple(flat[i * nargs:(i + 1) * nargs]) + (d,) for i, d in enumerate(ds)]))

    return flat_fn


def _h_math(x, ng, scale, shift):
    return x * lax.rsqrt(jnp.mean(x * x, axis=-1, keepdims=True) + NORM_EPS) * ng * (1.0 + scale) + shift


def _hg_post_math(o0, o1, z, g):
    o = o0 + o1
    on = o * lax.rsqrt(_seg128_mean(o * o) + NORM_EPS)
    return on * g * _silu(z)


def _rw_prep_math(k, pw0, pw1, pa0, pa1, w0, a0, kkp, kap_p):
    kk = k * kkp
    kap = kk * lax.rsqrt(_seg64_sum(kk * kk) + 1e-12)
    outs = []
    for d, (pw, pa) in enumerate(((pw0, pa0), (pw1, pa1))):
        w_log = -_softplus(-(w0[d:d + 1] + pw)) - 0.5
        lw = -jnp.exp(w_log)
        a = jax.nn.sigmoid(a0[d:d + 1] + pa)
        kd = k * (1.0 + (a - 1.0) * kap_p)
        outs += [lw, kap * a, kd]
    return (kap, *outs)


def _rw_post_math(y0, y1, kd0, kd1, r, v, z, rk, gng, gnb):
    ys, ksum = y0 + y1, kd0 + kd1
    mean = _seg64_sum(ys) * (1.0 / 64.0)
    cen = ys - mean
    var = _seg64_sum(cen * cen) * (1.0 / 64.0)
    yn = cen * lax.rsqrt(var + RW_GN_EPS) * gng + gnb
    bonus = _seg64_sum(r * ksum * rk) * v
    return (yn + bonus) * _silu(z)


def _head_math(x, mo, gate, fg, tgt):
    xo = x + gate * mo
    y = xo * lax.rsqrt(jnp.mean(xo * xo, axis=-1, keepdims=True) + NORM_EPS) * fg
    err = y - tgt
    return 0.5 * jnp.sum(jnp.mean(err * err, axis=-1, keepdims=True), axis=0, keepdims=True)


def _gate_math(ghg, grw, a, b):
    return jax.nn.sigmoid(ghg) * a + jax.nn.sigmoid(grw) * b


def _call(body, name, grid, in_specs, out_specs, out_shape, scratch=(), sem=None, aliases=None):
    return pl.pallas_call(
        body, name=name, grid=grid, in_specs=in_specs, out_specs=out_specs, out_shape=out_shape,
        scratch_shapes=list(scratch), input_output_aliases=aliases or {},
        compiler_params=pltpu.CompilerParams(dimension_semantics=sem, vmem_limit_bytes=V7X_VMEM_LIMIT))


def _full(shape):
    nd = len(shape)
    return pl.BlockSpec(shape, lambda *_: (0,) * nd)


def _sds(shape, dtype=F32):
    return jax.ShapeDtypeStruct(shape, dtype)


def _proj_nt(h, wt, wt_blk0, nblk, row0, nrows, name):
    rc = min(TM, nrows)

    def body(h_ref, w_ref, o_ref):
        w = w_ref[...]

        def step(i, carry):
            r = pl.multiple_of(i * rc, rc)
            o_ref[pl.ds(r, rc), :] = lax.dot_general(h_ref[pl.ds(row0 + r, rc), :], w, (((1,), (1,)), ((), ())),
                                                     preferred_element_type=F32)
            return carry

        lax.fori_loop(0, nrows // rc, step, 0)

    return _call(body, name, (nblk,), [_full(h.shape), pl.BlockSpec((WB, wt.shape[1]), lambda j: (wt_blk0 + j, 0))],
                 pl.BlockSpec((nrows, WB), lambda j: (0, j)), _sds((nrows, nblk * WB)), sem=("parallel",))(h, wt)


def _dw_tn(a, b, row0, out_rows, out_blk0, name, prev=None):
    m, n = a.shape
    k2 = b.shape[1]
    nblk = n // WB
    rc = min(TM, m)

    def body(*refs):
        a_ref, b_ref, o_ref, acc_ref = refs[0], refs[1], refs[-2], refs[-1]
        acc_ref[...] = jnp.zeros_like(acc_ref)

        def step(i, carry):
            r = pl.multiple_of(i * rc, rc)
            acc_ref[...] += lax.dot_general(a_ref[pl.ds(r, rc), :], b_ref[pl.ds(row0 + r, rc), :], (((0,), (0,)), ((), ())),
                                            preferred_element_type=F32)
            return carry

        lax.fori_loop(0, m // rc, step, 0)
        o_ref[...] = acc_ref[...].astype(BF16)

    in_specs = [pl.BlockSpec((m, WB), lambda j: (0, j)), _full(b.shape)]
    args = [a, b]
    aliases = None
    if prev is not None:
        in_specs.append(pl.BlockSpec(memory_space=pl.ANY))
        args.append(prev)
        aliases = {2: 0}
    return _call(body, name, (nblk,), in_specs, pl.BlockSpec((WB, k2), lambda j: (out_blk0 + j, 0)), _sds((out_rows, k2), BF16),
                 scratch=[pltpu.VMEM((WB, k2), F32)], sem=("arbitrary",), aliases=aliases)(*args)


def _mm_nn(a, w, w_blk0, name):
    kb = WB
    m, kc = a.shape
    n = w.shape[1]
    tmb = 3 * TM if m % (3 * TM) == 0 else (2 * TM if m % (2 * TM) == 0 else TM)

    def body(a_ref, w_ref, o_ref):
        @pl.when(pl.program_id(1) == 0)
        def _():
            o_ref[...] = jnp.zeros_like(o_ref)

        o_ref[...] += jnp.dot(a_ref[...], w_ref[...], preferred_element_type=F32)

    return _call(body, name, (m // tmb, kc // kb),
                 [pl.BlockSpec((tmb, kb), lambda i, k: (i, k)), pl.BlockSpec((kb, n), lambda i, k: (w_blk0 + k, 0))],
                 pl.BlockSpec((tmb, n), lambda i, k: (i, 0)), _sds((m, n)), sem=("parallel", "arbitrary"))(a, w)


def _mods_fwd(cc, ada_t, ada_b):
    tn = 512 if (3 * D) % 512 == 0 else 256

    def body(cc_ref, w_ref, b_ref, o_ref):
        s = _silu(cc_ref[...]).astype(BF16)
        o_ref[...] = lax.dot_general(s, w_ref[...], (((1,), (1,)), ((), ())), preferred_element_type=F32) + b_ref[...]

    return _call(body, "mods_fwd", (3 * D // tn,),
                 [_full((8, D)), pl.BlockSpec((tn, D), lambda j: (j, 0)), pl.BlockSpec((1, tn), lambda j: (0, j))],
                 pl.BlockSpec((8, tn), lambda j: (0, j)), _sds((8, 3 * D)), sem=("parallel",))(cc, ada_t, ada_b)


def _sel_mod(mods_ref, is_ctx, lo):
    return jnp.where(is_ctx, mods_ref[1:2, lo:lo + D], mods_ref[0:1, lo:lo + D])


def _h_fwd(xcat, mods, norm_g):
    dm = _dims()

    def body(x_ref, m_ref, g_ref, o_ref):
        is_ctx = pl.program_id(0) < dm["NCT"]
        o_ref[...] = _h_math(x_ref[...], g_ref[...], _sel_mod(m_ref, is_ctx, D), _sel_mod(m_ref, is_ctx, 0)).astype(BF16)

    return _call(body, "h_fwd", (dm["TT"] // TM,),
                 [pl.BlockSpec((TM, D), lambda i: (i, 0)), _full((8, 3 * D)), _full((1, D))],
                 pl.BlockSpec((TM, D), lambda i: (i, 0)), _sds((dm["TT"], D), BF16), sem=("parallel",))(xcat, mods, norm_g)


def _chunk_of(d, s):
    dm = _dims()
    ncc, nch = dm["NCC"], dm["NCH"]
    return s if d == 0 else jnp.where(s < ncc, ncc - 1 - s, nch - 1 - (s - ncc))


HG_GROUP = 4
RW_GROUP = 2


def _dir_specs(make):
    return [make(d) for d in range(2)]


def _lanes(j):
    return slice(j * 128, (j + 1) * 128)


def _hg_scan_fwd(p_hg, lb3):
    dm = _dims()
    nh, nch, tt = dm["HG_H"], dm["NCH"], dm["TT"]
    pp = min(HG_GROUP, nh)
    lw_ = 128 * pp

    def body(q0, q1, i0, i1, f0, f1, lb0, lb1, o0, o1, ck_ref, s_ref):
        @pl.when(pl.program_id(1) == 0)
        def _():
            s_ref[...] = jnp.zeros_like(s_ref)

        keys, chains = [], []
        for d, refs in enumerate(((q0, i0, f0, lb0), (q1, i1, f1, lb1))):
            vals = [r[...] for r in refs]
            for j in range(pp):
                keys.append((d, j))
                chains.append((s_ref[d, j], *[a[:, _lanes(j)] for a in vals], d))
        for (d, j), chain, (out, s1) in zip(keys, chains, _hg_chunks(chains)):
            ck_ref[d, j] = chain[0]
            (o0, o1)[d][:, _lanes(j)] = out
            s_ref[d, j] = s1

    blk = lambda off: _dir_specs(lambda d: pl.BlockSpec((C, lw_), lambda h, s: (_chunk_of(d, s), off * nh // pp + h)))
    return _call(body, "hg_scan_fwd", (nh // pp, nch),
                 blk(0) + blk(1) + _dir_specs(lambda d: pl.BlockSpec((C, lw_), lambda h, s: (_chunk_of(d, s), (2 + d) * nh // pp + h)))
                 + _dir_specs(lambda d: pl.BlockSpec((None, 1, lw_), lambda h, s: (d, 0, h))),
                 _dir_specs(lambda d: pl.BlockSpec((C, lw_), lambda h, s: (_chunk_of(d, s), h)))
                 + [pl.BlockSpec((2, pp, None, 128, 128), lambda h, s: (0, h, s, 0, 0))],
                 [_sds((tt, HGW)), _sds((tt, HGW)), _sds((2, nh, nch, 128, 128))],
                 scratch=[pltpu.VMEM((2, pp, 128, 128), F32)], sem=("parallel", "arbitrary"))(*([p_hg] * 6), lb3, lb3)


def _hg_scan_bwd(p_hg, lb3, ck, do):
    dm = _dims()
    nh, nch, tt = dm["HG_H"], dm["NCH"], dm["TT"]
    pp = min(HG_GROUP, nh)
    lw_ = 128 * pp

    def body(q0, q1, i0, i1, f0, f1, lb0, lb1, ck_ref, do0, do1, dq0, dq1, di0, di1, df0, df1, dlb_ref, ds_ref):
        @pl.when(pl.program_id(1) == 0)
        def _():
            ds_ref[...] = jnp.zeros_like(ds_ref)
            dlb_ref[...] = jnp.zeros_like(dlb_ref)

        keys, flat, cts = [], [], []
        for d, refs in enumerate(((q0, i0, f0, lb0, do0), (q1, i1, f1, lb1, do1))):
            *vals, do_ = [r[...] for r in refs]
            for j in range(pp):
                keys.append((d, j))
                flat += [ck_ref[d, j], *[a[:, _lanes(j)] for a in vals]]
                cts.append((do_[:, _lanes(j)], ds_ref[d, j]))
        _, vjp = jax.vjp(_lockstep(_hg_chunks, [d for d, _ in keys], 5), *flat)
        grads = vjp(tuple(cts))
        for n, (d, j) in enumerate(keys):
            ds0, dq_, di_, df_, dlb = grads[5 * n:5 * n + 5]
            (dq0, dq1)[d][:, _lanes(j)] = dq_
            (di0, di1)[d][:, _lanes(j)] = di_
            (df0, df1)[d][:, _lanes(j)] = df_
            dlb_ref[d, :, _lanes(j)] += dlb
            ds_ref[d, j] = ds0

    ch = lambda d, s: _chunk_of(d, nch - 1 - s)
    blk = lambda off: _dir_specs(lambda d: pl.BlockSpec((C, lw_), lambda h, s: (ch(d, s), off * nh // pp + h)))
    tok = _dir_specs(lambda d: pl.BlockSpec((C, lw_), lambda h, s: (ch(d, s), h)))
    return _call(body, "hg_scan_bwd", (nh // pp, nch),
                 blk(0) + blk(1) + _dir_specs(lambda d: pl.BlockSpec((C, lw_), lambda h, s: (ch(d, s), (2 + d) * nh // pp + h)))
                 + _dir_specs(lambda d: pl.BlockSpec((None, 1, lw_), lambda h, s: (d, 0, h)))
                 + [pl.BlockSpec((2, pp, None, 128, 128), lambda h, s: (0, h, nch - 1 - s, 0, 0))] + tok,
                 tok * 3 + [pl.BlockSpec((2, 1, lw_), lambda h, s: (0, 0, h))],
                 [_sds((tt, HGW))] * 6 + [_sds((2, 1, HGW))],
                 scratch=[pltpu.VMEM((2, pp, 128, 128), F32)], sem=("parallel", "arbitrary"))(*([p_hg] * 6), lb3, lb3, ck, do, do)


def _rw_scan_fwd(sh, kap, lw, b, kd):
    dm = _dims()
    npair, nch, tt = dm["RW_P"], dm["NCH"], dm["TT"]
    pp = min(RW_GROUP, npair)
    lw_ = 128 * pp

    def body(r0, r1, v0, v1, ka0, ka1, lw0, lw1, b0, b1, kd0, kd1, y0, y1, ck_ref, s_ref):
        @pl.when(pl.program_id(1) == 0)
        def _():
            s_ref[...] = jnp.zeros_like(s_ref)

        keys, chains = [], []
        for d, refs in enumerate(((r0, lw0, ka0, b0, v0, kd0), (r1, lw1, ka1, b1, v1, kd1))):
            vals = [r[...] for r in refs]
            for j in range(pp):
                keys.append((d, j))
                chains.append((s_ref[d, j], *[a[:, _lanes(j)] for a in vals], d))
        for (d, j), chain, (out, s1) in zip(keys, chains, _rw_chunks(chains)):
            ck_ref[d, j] = chain[0]
            (y0, y1)[d][:, _lanes(j)] = out
            s_ref[d, j] = s1

    two = lambda off: _dir_specs(lambda d: pl.BlockSpec((C, lw_), lambda p, s: (_chunk_of(d, s), off * npair // pp + p)))
    three = _dir_specs(lambda d: pl.BlockSpec((None, C, lw_), lambda p, s: (d, _chunk_of(d, s), p)))
    return _call(body, "rw_scan_fwd", (npair // pp, nch), two(0) + two(2) + two(0) + three * 3,
                 two(0) + [pl.BlockSpec((2, pp, None, 128, 128), lambda p, s: (0, p, s, 0, 0))],
                 [_sds((tt, RWW)), _sds((tt, RWW)), _sds((2, npair, nch, 128, 128))],
                 scratch=[pltpu.VMEM((2, pp, 128, 128), F32)], sem=("parallel", "arbitrary"))(
        sh, sh, sh, sh, kap, kap, lw, lw, b, b, kd, kd)


def _rw_scan_bwd(sh, kap, lw, b, kd, ck, dy):
    dm = _dims()
    npair, nch, tt = dm["RW_P"], dm["NCH"], dm["TT"]
    pp = min(RW_GROUP, npair)
    lw_ = 128 * pp

    def body(r0, r1, v0, v1, ka0, ka1, lw0, lw1, b0, b1, kd0, kd1, ck_ref, dy0, dy1,
             dr0, dr1, dv0, dv1, dka0, dka1, dlw0, dlw1, db0, db1, dkd0, dkd1, ds_ref):
        @pl.when(pl.program_id(1) == 0)
        def _():
            ds_ref[...] = jnp.zeros_like(ds_ref)

        keys, flat, cts = [], [], []
        for d, refs in enumerate(((r0, lw0, ka0, b0, v0, kd0, dy0), (r1, lw1, ka1, b1, v1, kd1, dy1))):
            *vals, dy_ = [r[...] for r in refs]
            for j in range(pp):
                keys.append((d, j))
                flat += [ck_ref[d, j], *[a[:, _lanes(j)] for a in vals]]
                cts.append((dy_[:, _lanes(j)], ds_ref[d, j]))
        _, vjp = jax.vjp(_lockstep(_rw_chunks, [d for d, _ in keys], 7), *flat)
        grads = vjp(tuple(cts))
        outs = ((dr0, dlw0, dka0, db0, dv0, dkd0), (dr1, dlw1, dka1, db1, dv1, dkd1))
        for n, (d, j) in enumerate(keys):
            ds0, *g_in = grads[7 * n:7 * n + 7]
            for o_ref, g in zip(outs[d], g_in):
                o_ref[:, _lanes(j)] = g
            ds_ref[d, j] = ds0

    ch = lambda d, s: _chunk_of(d, nch - 1 - s)
    two = lambda off: _dir_specs(lambda d: pl.BlockSpec((C, lw_), lambda p, s: (ch(d, s), off * npair // pp + p)))
    three = _dir_specs(lambda d: pl.BlockSpec((None, C, lw_), lambda p, s: (d, ch(d, s), p)))
    return _call(body, "rw_scan_bwd", (npair // pp, nch),
                 two(0) + two(2) + two(0) + three * 3
                 + [pl.BlockSpec((2, pp, None, 128, 128), lambda p, s: (0, p, nch - 1 - s, 0, 0))] + two(0),
                 two(0) * 6, [_sds((tt, RWW))] * 12,
                 scratch=[pltpu.VMEM((2, pp, 128, 128), F32)], sem=("parallel", "arbitrary"))(
        sh, sh, sh, sh, kap, kap, lw, lw, b, b, kd, kd, ck, dy, dy)


def _tile_pos(i):
    dm = _dims()
    is_ctx = i < dm["NCT"]
    rows = lax.broadcasted_iota(jnp.int32, (TM, 1), 0)
    tl = rows + (i - dm["NCT"]) * TM
    width = jnp.where(is_ctx, TC, GW)
    colp = jnp.where(is_ctx, rows, tl) % width
    return is_ctx, rows, tl, colp, width


def _shift_lr(cur, i):
    _, _, _, colp, width = _tile_pos(i)
    left = jnp.where(colp == 0, 0.0, pltpu.roll(cur, 1, 0))
    right = jnp.where(colp == width - 1, 0.0, pltpu.roll(cur, TM - 1, 0))
    return left, right


def _shift_ud(cur, prv, nxt, i):
    is_ctx, rows, tl, _, _ = _tile_pos(i)
    if TM > GW:
        up = jnp.where(rows >= GW, pltpu.roll(cur, GW, 0), pltpu.roll(prv, GW, 0))
        down = jnp.where(rows < TM - GW, pltpu.roll(cur, TM - GW, 0), pltpu.roll(nxt, TM - GW, 0))
    else:
        up, down = prv, nxt
    up = jnp.where(jnp.logical_or(is_ctx, tl < GW), 0.0, up)
    down = jnp.where(jnp.logical_or(is_ctx, tl >= T - GW), 0.0, down)
    return up, down


def _shift_specs():
    dm = _dims()
    nt = dm["TT"] // TM
    cw = 256 if dm["SH"] % 256 == 0 else 128
    cur = pl.BlockSpec((TM, cw), lambda j, i: (i, j))
    prv = pl.BlockSpec((TM, cw), lambda j, i: (jnp.maximum(i - 1, 0), j))
    nxt = pl.BlockSpec((TM, cw), lambda j, i: (jnp.minimum(i + 1, nt - 1), j))
    mu = pl.BlockSpec((4, cw), lambda j, i: (0, j))
    return nt, cw, cur, prv, nxt, mu


def _shift_fwd(p_rw, mu):
    dm = _dims()
    nt, cw, cur, prv, nxt, mus = _shift_specs()

    def body(c_ref, p_ref, n_ref, mu_ref, o_ref):
        i = pl.program_id(1)
        p, m = c_ref[...], mu_ref[...]
        left, right = _shift_lr(p, i)
        up, down = _shift_ud(p, p_ref[...], n_ref[...], i)
        vert = jnp.where(i < dm["NCT"], 0.0, 1.0)
        o_ref[...] = (p * (1.0 - m[0:1] - m[1:2] - vert * (m[2:3] + m[3:4]))
                      + m[0:1] * left + m[1:2] * right + m[2:3] * up + m[3:4] * down)

    return _call(body, "shift_fwd", (dm["SH"] // cw, nt), [cur, prv, nxt, mus], cur, _sds((dm["TT"], dm["SH"])),
                 sem=("parallel", "parallel"))(p_rw, p_rw, p_rw, mu)


def _shift_bwd(p_rw, dsh, mu):
    dm = _dims()
    nt, cw, cur, prv, nxt, mus = _shift_specs()

    def body(c_ref, p_ref, n_ref, gc_ref, gp_ref, gn_ref, mu_ref, dp_ref, dmu_ref):
        i = pl.program_id(1)
        p, g, m = c_ref[...], gc_ref[...], mu_ref[...]
        vert = jnp.where(i < dm["NCT"], 0.0, 1.0)
        _, from_right = _shift_lr(m[0:1] * g, i)
        from_left, _ = _shift_lr(m[1:2] * g, i)
        _, from_down = _shift_ud(m[2:3] * g, g, m[2:3] * gn_ref[...], i)
        from_up, _ = _shift_ud(m[3:4] * g, m[3:4] * gp_ref[...], g, i)
        dp = g * (1.0 - m[0:1] - m[1:2] - vert * (m[2:3] + m[3:4])) + from_right + from_left + from_down + from_up
        dp_ref[...] = dp.astype(BF16)

        @pl.when(i == 0)
        def _():
            dmu_ref[...] = jnp.zeros_like(dmu_ref)

        left, right = _shift_lr(p, i)
        up, down = _shift_ud(p, p_ref[...], n_ref[...], i)
        s = lambda a: jnp.sum(a, axis=0, keepdims=True)
        dmu_ref[...] += jnp.concatenate([s(g * (left - p)), s(g * (right - p)), vert * s(g * (up - p)), vert * s(g * (down - p))], axis=0)

    return _call(body, "shift_bwd", (dm["SH"] // cw, nt), [cur, prv, nxt, cur, prv, nxt, mus],
                 [cur, pl.BlockSpec((4, cw), lambda j, i: (0, j))], [_sds((dm["TT"], dm["SH"]), BF16), _sds((4, dm["SH"]))],
                 sem=("parallel", "arbitrary"))(p_rw, p_rw, p_rw, dsh, dsh, dsh, mu)


def _acc_out(ref, first, vals):
    @pl.when(first)
    def _():
        for r in ref:
            r[...] = jnp.zeros_like(r)

    for r, v in zip(ref, vals):
        r[...] += v


def _rowsum(a):
    return jnp.sum(a, axis=0, keepdims=True)


def _rw_prep_specs():
    dm = _dims()
    rw = RWW // 128
    tok = lambda width, cb: pl.BlockSpec((TMV, width), lambda i: (i, cb))
    ins = [pl.BlockSpec((TMV, RWW), lambda i: (i, 1)), tok(128, 3 * rw), tok(128, 3 * rw + 1),
           _full((2, RWW)), _full((128, RWW)), _full((128, RWW)), _full((2, RWW)), _full((128, RWW)), _full((128, RWW)),
           _full((1, RWW)), _full((1, RWW))]
    return dm, ins


def _rw_prep_fwd(sh, w0, w2p0, w2p1, a0, a2p0, a2p1, kkp, kap_p):
    dm, ins = _rw_prep_specs()
    tt = dm["TT"]

    def body(k_ref, low_ref, loa_ref, w0_ref, w20_ref, w21_ref, a0_ref, a20_ref, a21_ref, kk_ref, ka_ref, kap_ref, lw_ref, b_ref, kd_ref):
        tw, la = jnp.tanh(low_ref[...]), loa_ref[...]
        kap, lw0, b0, kd0, lw1, b1, kd1 = _rw_prep_math(
            k_ref[...], _mm(tw, w20_ref[...]), _mm(tw, w21_ref[...]), _mm(la, a20_ref[...]), _mm(la, a21_ref[...]),
            w0_ref[...], a0_ref[...], kk_ref[...], ka_ref[...])
        kap_ref[...] = kap
        lw_ref[0], lw_ref[1] = lw0, lw1
        b_ref[0], b_ref[1] = b0, b1
        kd_ref[0], kd_ref[1] = kd0, kd1

    two = pl.BlockSpec((2, TMV, RWW), lambda i: (0, i, 0))
    return _call(body, "rw_prep_fwd", (tt // TMV,), ins, [pl.BlockSpec((TMV, RWW), lambda i: (i, 0)), two, two, two],
                 [_sds((tt, RWW)), _sds((2, tt, RWW)), _sds((2, tt, RWW)), _sds((2, tt, RWW))], sem=("parallel",))(
        sh, sh, sh, w0, w2p0, w2p1, a0, a2p0, a2p1, kkp, kap_p)


def _rw_prep_bwd(sh, w0, w2p0, w2p1, a0, a2p0, a2p1, kkp, kap_p, scan_grads, dkd_p, dr_p, dv_p):
    dm, ins = _rw_prep_specs()
    tt, sh_w = dm["TT"], dm["SH"]
    one = pl.BlockSpec((TMV, RWW), lambda i: (i, 0))

    def body(k_ref, low_ref, loa_ref, w0_ref, w20_ref, w21_ref, a0_ref, a20_ref, a21_ref, kk_ref, ka_ref,
             dr0, dr1, dv0, dv1, dka0, dka1, dlw0, dlw1, db0, db1, dkd0, dkd1, dkdp_ref, drp_ref, dvp_ref,
             dsh_ref, dw0_ref, dw20_ref, dw21_ref, da0_ref, da20_ref, da21_ref, dkk_ref, dka_ref):
        tw, la = jnp.tanh(low_ref[...]), loa_ref[...]
        w20, w21, a20, a21 = w20_ref[...], w21_ref[...], a20_ref[...], a21_ref[...]
        _, vjp = jax.vjp(_rw_prep_math, k_ref[...], _mm(tw, w20), _mm(tw, w21), _mm(la, a20), _mm(la, a21),
                         w0_ref[...], a0_ref[...], kk_ref[...], ka_ref[...])
        dkp = dkdp_ref[...]
        dk, dpw0, dpw1, dpa0, dpa1, dw0, da0, dkk, dka = vjp(
            (dka0[...] + dka1[...], dlw0[...], db0[...], dkd0[...] + dkp, dlw1[...], db1[...], dkd1[...] + dkp))
        twt, lat = tw.T, la.T
        dtw = _mm_nt(dpw0, w20) + _mm_nt(dpw1, w21)
        dsh_ref[:, 0:RWW] = dr0[...] + dr1[...] + drp_ref[...]
        dsh_ref[:, RWW:2 * RWW] = dk
        dsh_ref[:, 2 * RWW:3 * RWW] = dv0[...] + dv1[...] + dvp_ref[...]
        dsh_ref[:, 3 * RWW:3 * RWW + 128] = dtw * (1.0 - tw * tw)
        dsh_ref[:, 3 * RWW + 128:3 * RWW + 256] = _mm_nt(dpa0, a20) + _mm_nt(dpa1, a21)
        _acc_out((dw0_ref, dw20_ref, dw21_ref, da0_ref, da20_ref, da21_ref, dkk_ref, dka_ref), pl.program_id(0) == 0,
                 (dw0, _mm(twt, dpw0), _mm(twt, dpw1), da0, _mm(lat, dpa0), _mm(lat, dpa1), dkk, dka))

    par = [_full((2, RWW)), _full((128, RWW)), _full((128, RWW)), _full((2, RWW)), _full((128, RWW)), _full((128, RWW)),
           _full((1, RWW)), _full((1, RWW))]
    return _call(body, "rw_prep_bwd", (tt // TMV,), ins + [one] * 15,
                 [pl.BlockSpec((TMV, sh_w), lambda i: (i, 0))] + par,
                 [_sds((tt, sh_w)), _sds((2, RWW)), _sds((128, RWW)), _sds((128, RWW)), _sds((2, RWW)), _sds((128, RWW)), _sds((128, RWW)),
                  _sds((1, RWW)), _sds((1, RWW))],
                 sem=("arbitrary",))(sh, sh, sh, w0, w2p0, w2p1, a0, a2p0, a2p1, kkp, kap_p, *scan_grads, dkd_p, dr_p, dv_p)


def _rw_post_specs():
    dm = _dims()
    nv = dm["NCTV"]
    rw = RWW // 128
    lat3 = lambda d: pl.BlockSpec((None, TMV, RWW), lambda i: (d, nv + i, 0))
    lat2 = pl.BlockSpec((TMV, RWW), lambda i: (nv + i, 0))
    ins = [lat2, lat2, lat3(0), lat3(1),
           pl.BlockSpec((TMV, RWW), lambda i: (nv + i, 0)), pl.BlockSpec((TMV, RWW), lambda i: (nv + i, 2)),
           pl.BlockSpec((TMV, RWW), lambda i: (i, 0)), _full((1, RWW)), _full((1, RWW)), _full((1, RWW))]
    return dm, nv, ins


def _rw_post_fwd(y0_, y1_, kd, sh, p_z, rk, gng, gnb):
    dm, nv, ins = _rw_post_specs()

    def body(y0, y1, k0, k1, r, v, z, rk_ref, g_ref, b_ref, o_ref):
        o_ref[...] = _rw_post_math(y0[...], y1[...], k0[...], k1[...], r[...], v[...], z[...], rk_ref[...], g_ref[...], b_ref[...]).astype(BF16)

    return _call(body, "rw_post_fwd", (T // TMV,), ins, pl.BlockSpec((TMV, RWW), lambda i: (i, 0)), _sds((T, RWW), BF16),
                 sem=("parallel",))(y0_, y1_, kd, kd, sh, sh, p_z, rk, gng, gnb)


def _rw_post_bwd(y0_, y1_, kd, sh, p_z, rk, gng, gnb, dout):
    dm, nv, _ = _rw_post_specs()
    tt = dm["TT"]
    cl = lambda i: jnp.maximum(i - nv, 0)
    all3 = lambda d: pl.BlockSpec((None, TMV, RWW), lambda i: (d, i, 0))
    all2 = pl.BlockSpec((TMV, RWW), lambda i: (i, 0))
    ins = [all2, all2, all3(0), all3(1),
           pl.BlockSpec((TMV, RWW), lambda i: (i, 0)), pl.BlockSpec((TMV, RWW), lambda i: (i, 2)),
           pl.BlockSpec((TMV, RWW), lambda i: (cl(i), 0)), _full((1, RWW)), _full((1, RWW)), _full((1, RWW)),
           pl.BlockSpec((TMV, RWW), lambda i: (cl(i), 0))]
    tok = pl.BlockSpec((TMV, RWW), lambda i: (i, 0))

    def body(y0, y1, k0, k1, r, v, z, rk_ref, g_ref, b_ref, do_ref, dy_ref, dkd_ref, dr_ref, dv_ref, dz_ref, drk_ref, dg_ref, db_ref):
        i = pl.program_id(0)
        lat = jnp.where(i >= nv, 1.0, 0.0)
        _, vjp = jax.vjp(_rw_post_math, y0[...], y1[...], k0[...], k1[...], r[...], v[...], z[...], rk_ref[...], g_ref[...], b_ref[...])
        dy0, _, dk0, _, dr, dv, dz, drk, dg, db = vjp(do_ref[...] * lat)
        dy_ref[...] = dy0
        dkd_ref[...] = dk0
        dr_ref[...] = dr
        dv_ref[...] = dv
        dz_ref[...] = dz.astype(BF16)
        _acc_out((drk_ref, dg_ref, db_ref), i == 0, (drk, dg, db))

    return _call(body, "rw_post_bwd", (tt // TMV,), ins,
                 [tok, tok, tok, tok, pl.BlockSpec((TMV, RWW), lambda i: (cl(i), 0)), _full((1, RWW)), _full((1, RWW)), _full((1, RWW))],
                 [_sds((tt, RWW))] * 4 + [_sds((T, RWW), BF16), _sds((1, RWW)), _sds((1, RWW)), _sds((1, RWW))],
                 sem=("arbitrary",))(y0_, y1_, kd, kd, sh, sh, p_z, rk, gng, gnb, dout)


def _hg_post_fwd(o0_, o1_, p_hg, hg_g):
    dm = _dims()
    nv = dm["NCTV"]
    lat3 = lambda d: pl.BlockSpec((TMV, HGW), lambda i: (nv + i, 0))

    def body(o0, o1, z, g, out):
        out[...] = _hg_post_math(o0[...], o1[...], z[...], g[...]).astype(BF16)

    return _call(body, "hg_post_fwd", (T // TMV,), [lat3(0), lat3(1), pl.BlockSpec((TMV, HGW), lambda i: (nv + i, 4)), _full((1, HGW))],
                 pl.BlockSpec((TMV, HGW), lambda i: (i, 0)), _sds((T, HGW), BF16), sem=("parallel",))(o0_, o1_, p_hg, hg_g)


def _hg_post_bwd(o0_, o1_, p_hg, hg_g, dout):
    dm = _dims()
    nv, tt = dm["NCTV"], dm["TT"]
    all3 = lambda d: pl.BlockSpec((TMV, HGW), lambda i: (i, 0))
    tok = pl.BlockSpec((TMV, HGW), lambda i: (i, 0))

    def body(o0, o1, z, g, do_in, do_ref, dz_ref, dg_ref):
        i = pl.program_id(0)
        lat = jnp.where(i >= nv, 1.0, 0.0)
        _, vjp = jax.vjp(_hg_post_math, o0[...], o1[...], z[...], g[...])
        d0, _, dz, dg = vjp(do_in[...] * lat)
        do_ref[...] = d0
        dz_ref[...] = dz
        _acc_out((dg_ref,), i == 0, (dg,))

    return _call(body, "hg_post_bwd", (tt // TMV,),
                 [all3(0), all3(1), pl.BlockSpec((TMV, HGW), lambda i: (i, 4)), _full((1, HGW)),
                  pl.BlockSpec((TMV, HGW), lambda i: (jnp.maximum(i - nv, 0), 0))],
                 [tok, tok, _full((1, HGW))], [_sds((tt, HGW)), _sds((tt, HGW)), _sds((1, HGW))], sem=("arbitrary",))(o0_, o1_, p_hg, hg_g, dout)


def _hg_dproj(dq0, dq1, di0, di1, df0, df1, dz):
    dm = _dims()
    tt = dm["TT"]
    tok = pl.BlockSpec((TMV, HGW), lambda i: (i, 0))

    def body(q0, q1, i0, i1, f0, f1, z, o_ref):
        o_ref[:, 0:HGW] = (q0[...] + q1[...]).astype(BF16)
        o_ref[:, HGW:2 * HGW] = (i0[...] + i1[...]).astype(BF16)
        o_ref[:, 2 * HGW:3 * HGW] = f0[...].astype(BF16)
        o_ref[:, 3 * HGW:4 * HGW] = f1[...].astype(BF16)
        o_ref[:, 4 * HGW:5 * HGW] = z[...].astype(BF16)

    return _call(body, "hg_dproj", (tt // TMV,), [tok] * 7,
                 pl.BlockSpec((TMV, 5 * HGW), lambda i: (i, 0)), _sds((tt, 5 * HGW), BF16), sem=("parallel",))(
        dq0, dq1, di0, di1, df0, df1, dz)


def _merge1_fwd(out_hg, out_rw, whg_t, wrw_t, p_g):
    tok = pl.BlockSpec((TM, D), lambda i: (i, 0))

    def body(h_ref, r_ref, wh_ref, wr_ref, g1_ref, g2_ref, a_ref, b_ref, m_ref):
        nt = (((1,), (1,)), ((), ()))
        a = lax.dot_general(h_ref[...], wh_ref[...], nt, preferred_element_type=F32)
        b = lax.dot_general(r_ref[...], wr_ref[...], nt, preferred_element_type=F32)
        a_ref[...] = a
        b_ref[...] = b
        m_ref[...] = _gate_math(g1_ref[...], g2_ref[...], a, b).astype(BF16)

    return _call(body, "merge1_fwd", (T // TM,),
                 [pl.BlockSpec((TM, HGW), lambda i: (i, 0)), pl.BlockSpec((TM, RWW), lambda i: (i, 0)), _full((D, HGW)), _full((D, RWW)),
                  pl.BlockSpec((TM, D), lambda i: (i, 0)), pl.BlockSpec((TM, D), lambda i: (i, 1))],
                 [tok, tok, tok], [_sds((T, D)), _sds((T, D)), _sds((T, D), BF16)], sem=("parallel",))(out_hg, out_rw, whg_t, wrw_t, p_g, p_g)


def _head_fwd_bwd(x, merged, w_out, mods, final_g, tgt):
    tok = pl.BlockSpec((TM, D), lambda i: (i, 0))

    def body(x_ref, m_ref, w_ref, mods_ref, fg_ref, t_ref, dxo_ref, dmo_ref, loss_ref, dgate_ref, dfg_ref):
        mo = jnp.dot(m_ref[...], w_ref[...], preferred_element_type=F32)
        gate = mods_ref[0:1, 2 * D:3 * D]
        loss, vjp = jax.vjp(lambda x_, mo_, g_, fg_: _head_math(x_, mo_, g_, fg_, t_ref[...]), x_ref[...], mo, gate, fg_ref[...])
        dx, dmo, dgate, dfg = vjp(jnp.ones((1, 1), F32))
        dxo_ref[...] = dx
        dmo_ref[...] = dmo.astype(BF16)
        _acc_out((loss_ref, dgate_ref, dfg_ref), pl.program_id(0) == 0, (jnp.broadcast_to(loss, (1, 128)), dgate, dfg))

    return _call(body, "head_fwd_bwd", (T // TM,), [tok, tok, _full((D, D)), _full((8, 3 * D)), _full((1, D)), tok],
                 [tok, tok, _full((1, 128)), _full((1, D)), _full((1, D))],
                 [_sds((T, D)), _sds((T, D), BF16), _sds((1, 128)), _sds((1, D)), _sds((1, D))], sem=("arbitrary",))(
        x, merged, w_out, mods, final_g, tgt)


def _merge_bwd(dmo, w_out, p_g, a, b):
    tok = pl.BlockSpec((TM, D), lambda i: (i, 0))

    def body(dmo_ref, w_ref, g1_ref, g2_ref, a_ref, b_ref, da_ref, db_ref, dg_ref):
        dm_ = lax.dot_general(dmo_ref[...], w_ref[...], (((1,), (1,)), ((), ())), preferred_element_type=F32)
        _, vjp = jax.vjp(_gate_math, g1_ref[...], g2_ref[...], a_ref[...], b_ref[...])
        dg1, dg2, da, db = vjp(dm_)
        da_ref[...] = da.astype(BF16)
        db_ref[...] = db.astype(BF16)
        dg_ref[:, 0:D] = dg1.astype(BF16)
        dg_ref[:, D:2 * D] = dg2.astype(BF16)

    return _call(body, "merge_bwd", (T // TM,),
                 [tok, _full((D, D)), pl.BlockSpec((TM, D), lambda i: (i, 0)), pl.BlockSpec((TM, D), lambda i: (i, 1)), tok, tok],
                 [tok, tok, pl.BlockSpec((TM, 2 * D), lambda i: (i, 0))], [_sds((T, D), BF16), _sds((T, D), BF16), _sds((T, 2 * D), BF16)],
                 sem=("parallel",))(dmo, w_out, p_g, p_g, a, b)


def _h_bwd(xcat, mods, norm_g, dh_hg, dh_rw, dh_z, dh_g, dxo, dgate):
    dm = _dims()
    nct, tt = dm["NCT"], dm["TT"]
    tok = pl.BlockSpec((TM, D), lambda i: (i, 0))
    lat = pl.BlockSpec((TM, D), lambda i: (jnp.maximum(i - nct, 0), 0))

    def body(x_ref, m_ref, g_ref, d1, d2, d3, d4, dxo_ref, dgate_ref, gx_ref, dng_ref, dmods_ref):
        i = pl.program_id(0)
        is_ctx = i < nct
        latf = jnp.where(is_ctx, 0.0, 1.0)
        dh = d1[...] + d2[...] + latf * (d3[...] + d4[...])
        _, vjp = jax.vjp(_h_math, x_ref[...], g_ref[...], _sel_mod(m_ref, is_ctx, D), _sel_mod(m_ref, is_ctx, 0))
        dx, dng, dscale, dshift = vjp(dh)
        gx_ref[...] = dx + dxo_ref[...]

        @pl.when(i == 0)
        def _():
            dng_ref[...] = jnp.zeros_like(dng_ref)
            dmods_ref[...] = jnp.zeros_like(dmods_ref)
            dmods_ref[0:1, 2 * D:3 * D] = dgate_ref[...]

        dng_ref[...] += dng
        row = lax.broadcasted_iota(jnp.int32, (8, 1), 0)
        sel = jnp.where(row == jnp.where(is_ctx, 1, 0), 1.0, 0.0)
        dmods_ref[:, 0:D] += sel * dshift
        dmods_ref[:, D:2 * D] += sel * dscale

    return _call(body, "h_bwd", (tt // TM,), [tok, _full((8, 3 * D)), _full((1, D)), tok, tok, lat, lat, lat, _full((1, D))],
                 [lat, _full((1, D)), _full((8, 3 * D))], [_sds((T, D)), _sds((1, D)), _sds((8, 3 * D))], sem=("arbitrary",))(
        xcat, mods, norm_g, dh_hg, dh_rw, dh_z, dh_g, dxo, dgate)


def _mods_bwd(cc, ada_t, dmods, dmods_t):
    tn = 512 if (3 * D) % 512 == 0 else 256

    nsteps = 3 * D // tn

    def body(cc_ref, w_ref, dm_ref, dmt_ref, dcc_ref, dw_ref, db_ref):
        c_ = cc_ref[...]
        s, vjp = jax.vjp(_silu, c_)
        dm_ = dm_ref[...]
        ds = jnp.dot(dm_.astype(BF16), w_ref[...], preferred_element_type=F32)
        _acc_out((dcc_ref,), pl.program_id(0) == 0, (ds,))

        @pl.when(pl.program_id(0) == nsteps - 1)
        def _():
            dcc_ref[...] = vjp(dcc_ref[...])[0]

        dw_ref[...] = _mm(dmt_ref[...], s).astype(BF16)
        db_ref[...] = dm_[0:1] + dm_[1:2]

    return _call(body, "mods_bwd", (nsteps,),
                 [_full((8, D)), pl.BlockSpec((tn, D), lambda j: (j, 0)), pl.BlockSpec((8, tn), lambda j: (0, j)),
                  pl.BlockSpec((tn, 8), lambda j: (j, 0))],
                 [_full((8, D)), pl.BlockSpec((tn, D), lambda j: (j, 0)), pl.BlockSpec((1, tn), lambda j: (0, j))],
                 [_sds((8, D)), _sds((3 * D, D), BF16), _sds((1, 3 * D))], sem=("arbitrary",))(cc, ada_t, dmods, dmods_t)


def _lb_math(l0, l1):
    return jax.nn.sigmoid(l0 - l1)


def _lb_fwd(l0, l1):
    def body(a_ref, b_ref, o_ref):
        o_ref[...] = _lb_math(a_ref[...], b_ref[...])

    return _call(body, "lb_fwd", (1,), [_full((2, HGW))] * 2, _full((2, HGW)), _sds((2, HGW)))(l0, l1)


def _lb_bwd(l0, l1, dlb):
    def body(a_ref, b_ref, d_ref, da_ref, db_ref):
        _, vjp = jax.vjp(_lb_math, a_ref[...], b_ref[...])
        da_ref[...], db_ref[...] = vjp(d_ref[...])

    return _call(body, "lb_bwd", (1,), [_full((2, HGW))] * 3, [_full((2, HGW))] * 2, [_sds((2, HGW))] * 2)(l0, l1, dlb)


def _local_step(x, c, ctx, tgt, w):
    dm = _dims()
    tt, sh_w, hgc = dm["TT"], dm["SH"], dm["HGC"]
    nb = lambda cols: cols // WB
    xcat = jnp.concatenate([ctx, x], axis=0)
    cc = jnp.concatenate([c, w["c_ctx"], jnp.zeros((6, D), F32)], axis=0)
    mods = _mods_fwd(cc, w["ada_t"], w["ada_b"])
    h = _h_fwd(xcat, mods, w["norm_g"])
    win_t = w["win_t"]
    p_hg = _proj_nt(h, win_t, 0, nb(hgc), 0, tt, "proj_hg")
    p_rw = _proj_nt(h, win_t, nb(hgc), nb(sh_w), 0, tt, "proj_rw")
    p_z = _proj_nt(h, win_t, nb(hgc + sh_w), nb(RWW), TC, T, "proj_z")
    p_g = _proj_nt(h, win_t, nb(hgc + sh_w + RWW), nb(2 * D), TC, T, "proj_g")

    lb3 = _lb_fwd(w["lb0"], w["lb1"]).reshape(2, 1, HGW)
    o0, o1, hg_ck = _hg_scan_fwd(p_hg, lb3)
    out_hg = _hg_post_fwd(o0, o1, p_hg, w["hg_g"])

    sh = _shift_fwd(p_rw, w["mu"])
    zpad = jnp.zeros((LORA, RWW), F32)
    prep_w = (w["w0"], jnp.concatenate([w["w2"][0], zpad], 0), jnp.concatenate([zpad, w["w2"][1]], 0),
              w["a0"], jnp.concatenate([w["a2"][0], zpad], 0), jnp.concatenate([zpad, w["a2"][1]], 0), w["kk"], w["ka"])
    kap, lw, b, kd = _rw_prep_fwd(sh, *prep_w)
    y0, y1, rw_ck = _rw_scan_fwd(sh, kap, lw, b, kd)
    post_w = (w["rk"], w["gng"], w["gnb"])
    out_rw = _rw_post_fwd(y0, y1, kd, sh, p_z, *post_w)

    a, bb, merged = _merge1_fwd(out_hg, out_rw, w["whg_t"], w["wrw_t"], p_g)
    dxo, dmo, loss, dgate, d_fg = _head_fwd_bwd(x, merged, w["wout"], mods, w["final_g"], tgt)
    da, db, dp_g = _merge_bwd(dmo, w["wout"], p_g, a, bb)
    g_wout = _dw_tn(merged, dmo, 0, D, 0, "dw_out")
    g_whg = _dw_tn(da, out_hg, 0, D, 0, "dw_hg")
    g_wrw = _dw_tn(db, out_rw, 0, D, 0, "dw_rw")
    d_out_hg = _mm_nn(da, w["whg_t"], 0, "dx_hg")
    d_out_rw = _mm_nn(db, w["wrw_t"], 0, "dx_rw")

    do, dz_hg, d_hg_g = _hg_post_bwd(o0, o1, p_hg, w["hg_g"], d_out_hg)
    *hg_grads, dlb3 = _hg_scan_bwd(p_hg, lb3, hg_ck, do)
    dp_hg = _hg_dproj(*hg_grads, dz_hg)
    d_lb0, d_lb1 = _lb_bwd(w["lb0"], w["lb1"], dlb3.reshape(2, HGW))

    dy, dkd_p, dr_p, dv_p, dp_z, d_rk, d_gng, d_gnb = _rw_post_bwd(y0, y1, kd, sh, p_z, *post_w, d_out_rw)
    rw_grads = _rw_scan_bwd(sh, kap, lw, b, kd, rw_ck, dy)
    dsh, d_w0, d_w2p0, d_w2p1, d_a0, d_a2p0, d_a2p1, d_kk, d_ka = _rw_prep_bwd(sh, *prep_w, rw_grads, dkd_p, dr_p, dv_p)
    dp_rw, d_mu = _shift_bwd(p_rw, dsh, w["mu"])

    g_win = _dw_tn(dp_hg, h, 0, dm["NCOLS"], 0, "dw_in_hg")
    g_win = _dw_tn(dp_rw, h, 0, dm["NCOLS"], nb(hgc), "dw_in_rw", prev=g_win)
    g_win = _dw_tn(dp_z, h, TC, dm["NCOLS"], nb(hgc + sh_w), "dw_in_z", prev=g_win)
    g_win = _dw_tn(dp_g, h, TC, dm["NCOLS"], nb(hgc + sh_w + RWW), "dw_in_g", prev=g_win)
    dh_hg = _mm_nn(dp_hg, win_t, 0, "dh_hg")
    dh_rw = _mm_nn(dp_rw, win_t, nb(hgc), "dh_rw")
    dh_z = _mm_nn(dp_z, win_t, nb(hgc + sh_w), "dh_z")
    dh_g = _mm_nn(dp_g, win_t, nb(hgc + sh_w + RWW), "dh_g")
    gx, d_ng, dmods = _h_bwd(xcat, mods, w["norm_g"], dh_hg, dh_rw, dh_z, dh_g, dxo, dgate)
    dcc, g_ada, d_ada_b = _mods_bwd(cc, w["ada_t"], dmods, dmods.T)

    big = dict(win_t=g_win, wout=g_wout, whg_t=g_whg, wrw_t=g_wrw, ada_t=g_ada)
    small = dict(c_ctx=dcc[1:2], ada_b=d_ada_b, norm_g=d_ng, lb0=d_lb0, lb1=d_lb1, hg_g=d_hg_g, mu=d_mu, w0=d_w0,
                 w2=jnp.stack([d_w2p0[:LORA], d_w2p1[LORA:]]), a0=d_a0, a2=jnp.stack([d_a2p0[:LORA], d_a2p1[LORA:]]),
                 kk=d_kk, ka=d_ka, rk=d_rk, gng=d_gng, gnb=d_gnb, final_g=d_fg)
    return loss, gx, big, small


MESH = pl.DeviceIdType.MESH


def _comm_call(body, name, bufs, out_shapes, nsem, nloc):
    hbm = pl.BlockSpec(memory_space=pl.ANY)
    return pl.pallas_call(
        body, name=name, in_specs=[hbm] * len(bufs), out_specs=[hbm] * len(out_shapes), out_shape=out_shapes,
        scratch_shapes=[pltpu.SemaphoreType.DMA((nsem,)), pltpu.SemaphoreType.DMA((nsem,)), pltpu.SemaphoreType.DMA((nloc,))],
    )(*bufs)


def _gather2(bufs, name):
    nbuf = len(bufs)

    def body(*refs):
        ins, outs = refs[:nbuf], refs[nbuf:2 * nbuf]
        send_sems, recv_sems, loc_sems = refs[2 * nbuf:]
        x, y, c = lax.axis_index("x"), lax.axis_index("y"), lax.axis_index("c")
        me, sib = (x, y, c), (x, y, 1 - c)
        chips = [(1 - x, y), (x, 1 - y), (1 - x, 1 - y)]

        def copy(bi, k, block, to, src=None):
            rows = outs[bi].at[4 * block[0] + 2 * block[1] + block[2]]
            return pltpu.make_async_remote_copy(src_ref=rows if src is None else src, dst_ref=rows, send_sem=send_sems.at[7 * bi + k],
                                                recv_sem=recv_sems.at[7 * bi + k], device_id=to, device_id_type=MESH)

        own = [pltpu.make_async_copy(ins[bi], outs[bi].at[4 * x + 2 * y + c], loc_sems.at[bi]) for bi in range(nbuf)]
        for cp in own:
            cp.start()
        first = []
        for bi in range(nbuf):
            first.append(copy(bi, 0, me, sib, src=ins[bi]))
            first += [copy(bi, 1 + j, me, (*chip, c), src=ins[bi]) for j, chip in enumerate(chips)]
        for cp in first:
            cp.start()
        passed = []
        for j, chip in enumerate(chips):
            for bi in range(nbuf):
                copy(bi, 1 + j, (*chip, c), me).wait_recv()
                passed.append(copy(bi, 4 + j, (*chip, c), sib))
                passed[-1].start()
        for bi in range(nbuf):
            copy(bi, 0, sib, me).wait_recv()
            for j, chip in enumerate(chips):
                copy(bi, 4 + j, (*chip, 1 - c), me).wait_recv()
        for cp in first + passed:
            cp.wait_send()
        for cp in own:
            cp.wait()

    return _comm_call(body, name, bufs, [_sds((NDEV,) + b.shape, b.dtype) for b in bufs], 7 * nbuf, nbuf)


def _pair_exchange(bufs, name):
    nbuf = len(bufs)

    def body(*refs):
        ins, got = refs[:nbuf], refs[nbuf:2 * nbuf]
        send_sems, recv_sems, _ = refs[2 * nbuf:]
        x, y, c = lax.axis_index("x"), lax.axis_index("y"), lax.axis_index("c")
        copies = []
        for bi in range(nbuf):
            for q in range(4):
                copies.append(pltpu.make_async_remote_copy(
                    src_ref=ins[bi].at[2 * q + 1 - c], dst_ref=got[bi].at[q], send_sem=send_sems.at[4 * bi + q],
                    recv_sem=recv_sems.at[4 * bi + q], device_id=(x, y, 1 - c), device_id_type=MESH))
                copies[-1].start()
        for cp in copies:
            cp.wait_send()
            cp.wait_recv()

    return _comm_call(body, name, bufs, [_sds((4,) + b.shape[1:], b.dtype) for b in bufs], 4 * nbuf, 1)


def _chip_exchange(bufs, name):
    nbuf = len(bufs)

    def body(*refs):
        ins, outs = refs[:nbuf], refs[nbuf:2 * nbuf]
        send_sems, recv_sems, _ = refs[2 * nbuf:]
        x, y, c = lax.axis_index("x"), lax.axis_index("y"), lax.axis_index("c")
        myq = 2 * x + y
        copies = []
        for bi in range(nbuf):
            for j, (qx, qy) in enumerate([(1 - x, y), (x, 1 - y), (1 - x, 1 - y)]):
                q = 2 * qx + qy
                send = pltpu.make_async_remote_copy(src_ref=ins[bi].at[q], dst_ref=outs[bi].at[myq], send_sem=send_sems.at[3 * bi + j],
                                                    recv_sem=recv_sems.at[3 * bi + j], device_id=(qx, qy, c), device_id_type=MESH)
                send.start()
                recv = pltpu.make_async_remote_copy(src_ref=ins[bi].at[myq], dst_ref=outs[bi].at[q], send_sem=send_sems.at[3 * bi + j],
                                                    recv_sem=recv_sems.at[3 * bi + j], device_id=(qx, qy, c), device_id_type=MESH)
                copies.append((send, recv))
        for send, recv in copies:
            send.wait_send()
            recv.wait_recv()

    return _comm_call(body, name, bufs, [_sds(b.shape, b.dtype) for b in bufs], 3 * nbuf, 1)


def _prefetch_call(body, name, scalars, grid, in_specs, out_specs, out_shape, args):
    return pl.pallas_call(
        body, name=name, out_shape=out_shape,
        grid_spec=pltpu.PrefetchScalarGridSpec(num_scalar_prefetch=1, grid=grid, in_specs=in_specs, out_specs=out_specs),
        compiler_params=pltpu.CompilerParams(dimension_semantics=("parallel",) * len(grid), vmem_limit_bytes=V7X_VMEM_LIMIT))(scalars, *args)


def _pair_add(g, got, core, name):
    _, rows, cols = got.shape
    tr = _row_tile(rows, 16, 256)

    def body(c_ref, a_ref, b_ref, o_ref):
        o_ref[...] = (a_ref[...].astype(F32) + b_ref[...].astype(F32)).astype(o_ref.dtype)

    blk = pl.BlockSpec((4, tr, cols), lambda i, c_ref: (0, i, 0))
    mine = pl.BlockSpec((4, None, tr, cols), lambda i, c_ref: (0, c_ref[0], i, 0))
    return _prefetch_call(body, name, core, (rows // tr,), [mine, blk], blk, _sds(got.shape, got.dtype),
                          (g.reshape(4, 2, rows, cols), got))


def _reduce_chips(chip_sums, recv, slots, name):
    _, rows, cols = recv.shape
    tr = _row_tile(rows, 16, 128)

    def body(s_ref, own_ref, r1_ref, r2_ref, r3_ref, o_ref):
        o_ref[...] = ((own_ref[...].astype(F32) + r1_ref[...].astype(F32)) + r2_ref[...].astype(F32)) + r3_ref[...].astype(F32)

    pick = lambda k: pl.BlockSpec((None, tr, cols), lambda i, s_ref: (s_ref[k], i, 0))
    return _prefetch_call(body, name, slots, (rows // tr,), [pick(0), pick(1), pick(2), pick(3)],
                          pl.BlockSpec((tr, cols), lambda i, s_ref: (i, 0)), _sds((rows, cols)), (chip_sums, recv, recv, recv))


def _row_tile(rows, mult, cap):
    best = mult
    for t in range(mult, cap + 1, mult):
        if rows % t == 0:
            best = t
    assert rows % best == 0, (rows, mult)
    return best


def _reduce_sources(r, name):
    nsrc, rows, cols = r.shape
    tr = _row_tile(rows, 16 if r.dtype.itemsize == 2 else 8, 128)

    def body(r_ref, o_ref):
        acc = r_ref[0].astype(F32)
        for j in range(1, nsrc):
            acc = acc + r_ref[j].astype(F32)
        o_ref[...] = acc

    return _call(body, name, (rows // tr,), [pl.BlockSpec((nsrc, tr, cols), lambda i: (0, i, 0))],
                 pl.BlockSpec((tr, cols), lambda i: (i, 0)), _sds((rows, cols)), sem=("parallel",))(r)


def _adamw(w, g, m, v, name):
    rows, cols = w.shape
    tr = _row_tile(rows, 8, 128)
    c1 = 1.0 - ADAM_B1 ** ADAM_STEP
    c2 = 1.0 - ADAM_B2 ** ADAM_STEP

    def body(w_ref, g_ref, m_ref, v_ref, d_ref, mo_ref, vo_ref):
        g_ = g_ref[...]
        m_ = ADAM_B1 * m_ref[...] + (1.0 - ADAM_B1) * g_
        v_ = ADAM_B2 * v_ref[...] + (1.0 - ADAM_B2) * (g_ * g_)
        d_ref[...] = -ADAM_LR * ((m_ / c1) / (jnp.sqrt(v_ / c2) + ADAM_EPS) + ADAM_WD * w_ref[...])
        mo_ref[...] = m_
        vo_ref[...] = v_

    blk = pl.BlockSpec((tr, cols), lambda i: (i, 0))
    return _call(body, name, (rows // tr,), [blk] * 4, [blk] * 3, [_sds((rows, cols))] * 3, sem=("parallel",))(w, g, m, v)


SLAB_PART = 1024


def _pack(arrs):
    parts = []
    for a in arrs:
        flat = a.reshape(-1)
        pad = (-flat.shape[0]) % SLAB_PART
        if pad:
            flat = jnp.concatenate([flat, jnp.zeros((pad,), flat.dtype)])
        parts.append(flat.reshape(-1, 128))
    return jnp.concatenate(parts, axis=0)


def _unpack(slab, shapes):
    outs, row = [], 0
    for s in shapes:
        n = 1
        for d in s:
            n *= d
        rows = (n + SLAB_PART - 1) // SLAB_PART * (SLAB_PART // 128)
        outs.append(slab[row:row + rows].reshape(-1)[:n].reshape(s))
        row += rows
    return outs


def _unshard_last(g, lead):
    nl = len(lead)
    return jnp.transpose(g, tuple(range(1, nl + 1)) + (0, nl + 1)).reshape(tuple(lead) + (-1,))


def kernel(x, c, ctx, c_ctx, ada_w, ada_b, norm_g, w_in, hg_lb, hg_norm_g, rw_mu, rw_w0, rw_w2, rw_a0, rw_a2, rw_kk, rw_ka, rw_rk, rw_gn_g, rw_gn_b, w_hg_out, w_rw_out, w_out, final_g, loss_target, m_c_ctx, m_ada_w, m_ada_b, m_norm_g, m_w_in, m_hg_lb, m_hg_norm_g, m_rw_mu, m_rw_w0, m_rw_w2, m_rw_a0, m_rw_a2, m_rw_kk, m_rw_ka, m_rw_rk, m_rw_gn_g, m_rw_gn_b, m_w_hg_out, m_w_rw_out, m_w_out, m_final_g, v_c_ctx, v_ada_w, v_ada_b, v_norm_g, v_w_in, v_hg_lb, v_hg_norm_g, v_rw_mu, v_rw_w0, v_rw_w2, v_rw_a0, v_rw_a2, v_rw_kk, v_rw_ka, v_rw_rk, v_rw_gn_g, v_rw_gn_b, v_w_hg_out, v_w_rw_out, v_w_out, v_final_g):
    dm = _dims()
    me = 4 * lax.axis_index("x") + 2 * lax.axis_index("y") + lax.axis_index("c")

    big_shards = [w_in[0].T.astype(BF16), w_out[0].astype(BF16), w_hg_out[0].T.astype(BF16), w_rw_out[0].T.astype(BF16),
                  ada_w[0].T.astype(BF16)]
    sharded_small = [hg_lb, rw_mu[0], rw_w0[0], rw_w2[0], rw_a0[0], rw_a2[0]]
    gathered = _gather2(big_shards + [_pack(sharded_small)], "gather_weights")
    win_t, wout, whg_t, wrw_t, ada_t = [g.reshape(-1, g.shape[2]) for g in gathered[:5]]
    per_dev = jax.vmap(lambda s: _unpack(s, [a.shape for a in sharded_small]))(gathered[5])
    hg_lb_f, mu_f, w0_f, w2_f, a0_f, a2_f = [_unshard_last(p, p.shape[1:-1]) for p in per_dev]

    w = dict(win_t=win_t, wout=wout, whg_t=whg_t, wrw_t=wrw_t, ada_t=ada_t, c_ctx=c_ctx[None], ada_b=ada_b, norm_g=norm_g,
             lb0=hg_lb_f[:, 0], lb1=hg_lb_f[:, 1], hg_g=hg_norm_g, mu=mu_f, w0=w0_f, w2=w2_f, a0=a0_f, a2=a2_f,
             kk=rw_kk, ka=rw_ka, rk=rw_rk, gng=rw_gn_g, gnb=rw_gn_b, final_g=final_g[None])
    loss_dev, grad_x, big, small = _local_step(x[0], c, ctx[0], loss_target[0], w)
    loss = lax.psum(loss_dev[0, 0], AXES)

    small_names = ["c_ctx", "ada_b", "norm_g", "lb0", "lb1", "hg_g", "mu", "w0", "w2", "a0", "a2", "kk", "ka", "rk", "gng", "gnb", "final_g"]
    small_slab = _pack([small[n] for n in small_names])
    big_names = ["win_t", "wout", "whg_t", "wrw_t", "ada_t"]
    blocks = [big[n].reshape(NDEV, -1, big[n].shape[1]) for n in big_names]
    got = _pair_exchange(blocks, "pair_grads")
    core = lax.axis_index("c").astype(jnp.int32).reshape(1)
    chip_sums = [_pair_add(g, r, core, "pair_add_" + n) for g, r, n in zip(blocks, got, big_names)]
    recv = _chip_exchange(chip_sums, "chip_grads")
    small_all = _gather2([small_slab], "gather_small_grads")[0]
    my_chip = (2 * lax.axis_index("x") + lax.axis_index("y")).astype(jnp.int32)
    others = jnp.arange(3, dtype=jnp.int32)
    slots = jnp.concatenate([my_chip.reshape(1), others + (others >= my_chip).astype(jnp.int32)])
    g_win_t, g_wout, g_whg_t, g_wrw_t, g_ada_t = [_reduce_chips(cs, r, slots, "reduce_" + n)
                                                  for cs, r, n in zip(chip_sums, recv, big_names)]
    sg = dict(zip(small_names, _unpack(_reduce_sources(small_all, "reduce_small"), [small[n].shape for n in small_names])))

    def my_shard(full):
        n = full.shape[-1] // NDEV
        return lax.dynamic_slice_in_dim(full, me * n, n, axis=full.ndim - 1)

    grads = dict(
        c_ctx=sg["c_ctx"][0], ada_w=g_ada_t.T[None], ada_b=sg["ada_b"], norm_g=sg["norm_g"], w_in=g_win_t.T[None],
        hg_lb=my_shard(jnp.stack([sg["lb0"], sg["lb1"]], axis=1)), hg_norm_g=sg["hg_g"], rw_mu=my_shard(sg["mu"])[None],
        rw_w0=my_shard(sg["w0"])[None], rw_w2=my_shard(sg["w2"])[None], rw_a0=my_shard(sg["a0"])[None], rw_a2=my_shard(sg["a2"])[None],
        rw_kk=sg["kk"], rw_ka=sg["ka"], rw_rk=sg["rk"], rw_gn_g=sg["gng"], rw_gn_b=sg["gnb"],
        w_hg_out=g_whg_t.T[None], w_rw_out=g_wrw_t.T[None], w_out=g_wout[None], final_g=sg["final_g"][0])
    weights = dict(c_ctx=c_ctx, ada_w=ada_w, ada_b=ada_b, norm_g=norm_g, w_in=w_in, hg_lb=hg_lb, hg_norm_g=hg_norm_g, rw_mu=rw_mu,
                   rw_w0=rw_w0, rw_w2=rw_w2, rw_a0=rw_a0, rw_a2=rw_a2, rw_kk=rw_kk, rw_ka=rw_ka, rw_rk=rw_rk, rw_gn_g=rw_gn_g,
                   rw_gn_b=rw_gn_b, w_hg_out=w_hg_out, w_rw_out=w_rw_out, w_out=w_out, final_g=final_g)
    m_in = dict(zip(weights, (m_c_ctx, m_ada_w, m_ada_b, m_norm_g, m_w_in, m_hg_lb, m_hg_norm_g, m_rw_mu, m_rw_w0, m_rw_w2, m_rw_a0,
                              m_rw_a2, m_rw_kk, m_rw_ka, m_rw_rk, m_rw_gn_g, m_rw_gn_b, m_w_hg_out, m_w_rw_out, m_w_out, m_final_g)))
    v_in = dict(zip(weights, (v_c_ctx, v_ada_w, v_ada_b, v_norm_g, v_w_in, v_hg_lb, v_hg_norm_g, v_rw_mu, v_rw_w0, v_rw_w2, v_rw_a0,
                              v_rw_a2, v_rw_kk, v_rw_ka, v_rw_rk, v_rw_gn_g, v_rw_gn_b, v_w_hg_out, v_w_rw_out, v_w_out, v_final_g)))

    big_w = ["ada_w", "w_in", "w_hg_out", "w_rw_out", "w_out"]
    delta, new_m, new_v = {}, {}, {}
    for n in big_w:
        shp = weights[n].shape
        two = lambda a: a.reshape(shp[-2], shp[-1])
        d_, m_, v_ = _adamw(two(weights[n]), two(grads[n]), two(m_in[n]), two(v_in[n]), "adamw_" + n)
        delta[n], new_m[n], new_v[n] = d_.reshape(shp), m_.reshape(shp), v_.reshape(shp)
    rest = [n for n in weights if n not in big_w]
    shapes = [weights[n].shape for n in rest]
    d_s, m_s, v_s = _adamw(_pack([weights[n] for n in rest]), _pack([grads[n] for n in rest]), _pack([m_in[n] for n in rest]),
                           _pack([v_in[n] for n in rest]), "adamw_small")
    for n, d_, m_, v_ in zip(rest, _unpack(d_s, shapes), _unpack(m_s, shapes), _unpack(v_s, shapes)):
        delta[n], new_m[n], new_v[n] = d_, m_, v_

    order = list(weights)
    return (loss, grad_x[None], *[grads[n] for n in order], *[delta[n] for n in order],
            *[new_m[n] for n in order], *[new_v[n] for n in order])
```

```python
import functools

import jax
import jax.numpy as jnp
from jax import lax
from jax.experimental import pallas as pl
from jax.experimental.pallas import tpu as pltpu

D = 2048
T = 2048
TC = 256
GW = 64
C = 64
HGW = 1024
RWW = 1024
LORA = 64
NDEV = 8
TM = 256
TMV = 128
NORM_EPS = 1e-6
RW_GN_EPS = 64e-5
ADAM_LR, ADAM_B1, ADAM_B2, ADAM_EPS, ADAM_WD, ADAM_STEP = 0.001, 0.9, 0.999, 1e-08, 0.01, 10
WB = 256
V7X_VMEM_LIMIT = 56 * 1024 * 1024

F32 = jnp.float32
BF16 = jnp.bfloat16
AXES = ("x", "y", "c")


def _dims():
    tt = T + TC
    sh = 3 * RWW + 4 * LORA
    hgc = 5 * HGW
    return dict(TT=tt, NCC=TC // C, NCH=tt // C, HG_H=HGW // 128, RW_P=RWW // 128, SH=sh, HGC=hgc,
                NCOLS=hgc + sh + RWW + 2 * D, NCT=TC // TM, NCTV=TC // TMV)


def _mm(a, b):
    return _dot3(a, b, "nn")


def _mm_nt(a, b):
    return _dot3(a, b, "nt")


def _mm_tn(a, b):
    return _dot3(a, b, "tn")


def _split(x):
    hi = x.astype(BF16)
    return hi, (x - hi.astype(F32)).astype(BF16)


_FORMS = {"nn": (((1,), (0,)), ((), ())), "nt": (((1,), (1,)), ((), ())), "tn": (((0,), (0,)), ((), ()))}


def _dot3_raw(a, b, form):
    ah, al = _split(a)
    bh, bl = _split(b)
    d = lambda x, y: lax.dot_general(x, y, _FORMS[form], preferred_element_type=F32)
    return d(ah, bh) + (d(ah, bl) + d(al, bh))


@functools.partial(jax.custom_vjp, nondiff_argnums=(2,))
def _dot3(a, b, form):
    return _dot3_raw(a, b, form)


def _dot3_fwd(a, b, form):
    return _dot3_raw(a, b, form), (a, b)


def _dot3_bwd(form, res, g):
    a, b = res
    if form == "nn":
        return _dot3(g, b, "nt"), _dot3(a, g, "tn")
    if form == "nt":
        return _dot3(g, b, "nn"), _dot3(g, a, "tn")
    return _dot3(b, g, "nt"), _dot3(a, g, "nn")


_dot3.defvjp(_dot3_fwd, _dot3_bwd)


def _scan_cumsum(inc, x):
    return _cumsum_vjp(inc.astype(BF16), x)


def _cumsum_raw(inc, x, form):
    h1 = x.astype(BF16)
    r1 = x - h1.astype(F32)
    h2 = r1.astype(BF16)
    h3 = (r1 - h2.astype(F32)).astype(BF16)
    d = lambda y: lax.dot_general(inc, y, _FORMS[form], preferred_element_type=F32)
    return d(h1) + (d(h2) + d(h3))


@jax.custom_vjp
def _cumsum_vjp(inc, x):
    return _cumsum_raw(inc, x, "nn")


_cumsum_vjp.defvjp(lambda inc, x: (_cumsum_raw(inc, x, "nn"), inc),
                   lambda inc, g: (jnp.zeros_like(inc), _cumsum_raw(inc, g, "tn")))


def _silu(x):
    return x * jax.nn.sigmoid(x)


def _softplus(x):
    return jnp.maximum(x, 0.0) + jnp.log(1.0 + jnp.exp(-jnp.abs(x)))


def _iota2(shape, dim):
    return lax.broadcasted_iota(jnp.int32, shape, dim)


def _pair_mask():
    return (_iota2((128, 128), 0) < 64) == (_iota2((128, 128), 1) < 64)


def _seg64_sum(x):
    e = _pair_mask().astype(F32)
    parts = [_mm(x[:, g * 128:(g + 1) * 128], e) for g in range(x.shape[1] // 128)]
    return parts[0] if len(parts) == 1 else jnp.concatenate(parts, axis=1)


def _seg128_mean(x):
    parts = [jnp.broadcast_to(jnp.mean(x[:, g * 128:(g + 1) * 128], axis=1, keepdims=True), (x.shape[0], 128))
             for g in range(x.shape[1] // 128)]
    return parts[0] if len(parts) == 1 else jnp.concatenate(parts, axis=1)


def _scan_masks(d):
    lag = (_iota2((C, C), 0) - _iota2((C, C), 1)) * (1 - 2 * d)
    return lag >= 0, lag > 0


def _tri_inverse(a):
    eye = (_iota2((C, C), 0) == _iota2((C, C), 1)).astype(F32)
    p = [-x for x in a]
    tm = [eye + x for x in p]
    n = 2
    while n < C:
        p = [_mm(x, x) for x in p]
        tm = [t + _mm(t, x) for t, x in zip(tm, p)]
        n *= 2
    return tm


@jax.custom_vjp
def _tri_solve(a, rhs):
    return [_mm(t, r) for t, r in zip(_tri_inverse(a), rhs)]


def _tri_solve_fwd(a, rhs):
    tm = _tri_inverse(a)
    u = [_mm(t, r) for t, r in zip(tm, rhs)]
    return u, (tm, u)


def _tri_solve_bwd(res, g):
    tm, u = res
    d_rhs = [_mm_tn(t, x) for t, x in zip(tm, g)]
    return [-_mm_nt(d, x) for d, x in zip(d_rhs, u)], d_rhs


_tri_solve.defvjp(_tri_solve_fwd, _tri_solve_bwd)


def _rw_chunks(chains):
    s0, r, lw, kap, b, v, kd, ds = zip(*chains)
    ids = range(len(chains))
    inc, strict = zip(*[_scan_masks(d) for d in ds])
    lane = _iota2((1, 128), 1)
    hm = ((lane < 64).astype(F32), (lane >= 64).astype(F32))
    eye = (_iota2((C, C), 0) == _iota2((C, C), 1)).astype(F32)
    lc = [_scan_cumsum(inc[i], lw[i]) for i in ids]
    ltot = [jnp.sum(lw[i], axis=0, keepdims=True) for i in ids]
    rt = [r[i] * jnp.exp(lc[i]) for i in ids]
    kt = [kap[i] * jnp.exp(lc[i] - lw[i]) for i in ids]
    einv = [jnp.exp(-lc[i]) for i in ids]
    kh = [kd[i] * einv[i] for i in ids]
    bh = [b[i] * einv[i] for i in ids]
    eend = [jnp.exp(ltot[i] - lc[i]) for i in ids]
    kbar = [kd[i] * eend[i] for i in ids]
    bbar = [b[i] * eend[i] for i in ids]
    lhs = [jnp.concatenate([kt[i] * hm[0], kt[i] * hm[1], rt[i] * hm[0], rt[i] * hm[1]], axis=0) for i in ids]
    gk = [_mm_nt(lhs[i], kh[i]) for i in ids]
    gb = [_mm_nt(lhs[i], bh[i]) for i in ids]
    ks = [_mm_nt(kt[i], s0[i]) for i in ids]
    ys = [_mm_nt(rt[i], s0[i]) for i in ids]
    heads = [(i, h) for i in ids for h in range(2)]
    akk = {c: jnp.where(strict[c[0]], gk[c[0]][c[1] * C:(c[1] + 1) * C], 0.0) for c in heads}
    akb = {c: jnp.where(strict[c[0]], gb[c[0]][c[1] * C:(c[1] + 1) * C], 0.0) for c in heads}
    ark = {c: jnp.where(inc[c[0]], gk[c[0]][(2 + c[1]) * C:(3 + c[1]) * C], 0.0) for c in heads}
    arb = {c: jnp.where(inc[c[0]], gb[c[0]][(2 + c[1]) * C:(3 + c[1]) * C], 0.0) for c in heads}
    vh = {c: v[c[0]] * hm[c[1]] for c in heads}
    av = {c: _mm(akk[c], vh[c]) for c in heads}
    rhs = {c: ks[c[0]] * hm[c[1]] + av[c] for c in heads}
    uh = dict(zip(heads, _tri_solve([akb[c] for c in heads], [rhs[c] for c in heads])))
    yv = {c: _mm(ark[c], vh[c]) for c in heads}
    yu = {c: _mm(arb[c], uh[c]) for c in heads}
    u = [uh[i, 0] + uh[i, 1] for i in ids]
    y = [ys[i] + (yv[i, 0] - yu[i, 0]) + (yv[i, 1] - yu[i, 1]) for i in ids]
    upd = [_mm_tn(jnp.concatenate([v[i], -u[i]], axis=0), jnp.concatenate([kbar[i], bbar[i]], axis=0)) for i in ids]
    s1 = [s0[i] * jnp.exp(ltot[i]) + jnp.where(_pair_mask(), upd[i], 0.0) for i in ids]
    return [(y[i], s1[i]) for i in ids]


HG_SUB = 16


def _hg_chunks(chains):
    s0, qr, iv, f, lb, ds = zip(*chains)
    ids = range(len(chains))
    inc = [_scan_masks(d)[0] for d in ds]
    q = [_silu(qr[i]) for i in ids]
    fg = [lb[i] + (1.0 - lb[i]) * jax.nn.sigmoid(f[i]) for i in ids]
    k = [1.0 - fg[i] for i in ids]
    g = [jnp.log(fg[i]) for i in ids]
    bc = [_scan_cumsum(inc[i], g[i]) for i in ids]
    btot = [jnp.sum(g[i], axis=0, keepdims=True) for i in ids]
    o_inter = [_mm_nt(q[i] * jnp.exp(bc[i]), s0[i]) for i in ids]
    rowi = _iota2((C, 1), 0)
    outs = [[] for _ in ids]
    for blk in range(C // HG_SUB):
        lo, hi = blk * HG_SUB, (blk + 1) * HG_SUB
        first = [lo if ds[i] == 0 else hi - 1 for i in ids]
        ref = [jnp.sum(jnp.where(rowi == first[i], bc[i] - g[i], 0.0), axis=0, keepdims=True) for i in ids]
        qi = [q[i][lo:hi] * jnp.exp(bc[i][lo:hi] - ref[i]) for i in ids]
        src = [(rowi < hi) if ds[i] == 0 else (rowi >= lo) for i in ids]
        ke = [jnp.where(src[i], k[i] * jnp.exp(jnp.where(src[i], ref[i] - bc[i], 0.0)), 0.0) for i in ids]
        a = [jnp.where(inc[i][lo:hi], _mm_nt(qi[i], ke[i]), 0.0) for i in ids]
        part = [_mm(a[i], iv[i]) for i in ids]
        for i in ids:
            outs[i].append(part[i])
    o = [o_inter[i] + jnp.concatenate(outs[i], axis=0) for i in ids]
    upd = [_mm_tn(iv[i], k[i] * jnp.exp(btot[i] - bc[i])) for i in ids]
    s1 = [s0[i] * jnp.exp(btot[i]) + upd[i] for i in ids]
    return [(o[i], s1[i]) for i in ids]


def _lockstep(fn, ds, nargs):
    def flat_fn(*flat):
        return tuple(fn([tuple(flat[i * nargs:(i + 1) * nargs]) + (d,) for i, d in enumerate(ds)]))

    return flat_fn


def _h_math(x, ng, scale, shift):
    return x * lax.rsqrt(jnp.mean(x * x, axis=-1, keepdims=True) + NORM_EPS) * ng * (1.0 + scale) + shift


def _hg_post_math(o0, o1, z, g):
    o = o0 + o1
    on = o * lax.rsqrt(_seg128_mean(o * o) + NORM_EPS)
    return on * g * _silu(z)


def _rw_prep_math(k, pw0, pw1, pa0, pa1, w0, a0, kkp, kap_p):
    kk = k * kkp
    kap = kk * lax.rsqrt(_seg64_sum(kk * kk) + 1e-12)
    outs = []
    for d, (pw, pa) in enumerate(((pw0, pa0), (pw1, pa1))):
        w_log = -_softplus(-(w0[d:d + 1] + pw)) - 0.5
        lw = -jnp.exp(w_log)
        a = jax.nn.sigmoid(a0[d:d + 1] + pa)
        kd = k * (1.0 + (a - 1.0) * kap_p)
        outs += [lw, kap * a, kd]
    return (kap, *outs)


def _rw_post_math(y0, y1, kd0, kd1, r, v, z, rk, gng, gnb):
    ys, ksum = y0 + y1, kd0 + kd1
    mean = _seg64_sum(ys) * (1.0 / 64.0)
    cen = ys - mean
    var = _seg64_sum(cen * cen) * (1.0 / 64.0)
    yn = cen * lax.rsqrt(var + RW_GN_EPS) * gng + gnb
    bonus = _seg64_sum(r * ksum * rk) * v
    return (yn + bonus) * _silu(z)


def _head_math(x, mo, gate, fg, tgt):
    xo = x + gate * mo
    y = xo * lax.rsqrt(jnp.mean(xo * xo, axis=-1, keepdims=True) + NORM_EPS) * fg
    err = y - tgt
    return 0.5 * jnp.sum(jnp.mean(err * err, axis=-1, keepdims=True), axis=0, keepdims=True)


def _gate_math(ghg, grw, a, b):
    return jax.nn.sigmoid(ghg) * a + jax.nn.sigmoid(grw) * b


def _call(body, name, grid, in_specs, out_specs, out_shape, scratch=(), sem=None, aliases=None):
    return pl.pallas_call(
        body, name=name, grid=grid, in_specs=in_specs, out_specs=out_specs, out_shape=out_shape,
        scratch_shapes=list(scratch), input_output_aliases=aliases or {},
        compiler_params=pltpu.CompilerParams(dimension_semantics=sem, vmem_limit_bytes=V7X_VMEM_LIMIT))


def _full(shape):
    nd = len(shape)
    return pl.BlockSpec(shape, lambda *_: (0,) * nd)


def _sds(shape, dtype=F32):
    return jax.ShapeDtypeStruct(shape, dtype)


def _proj_nt(h, wt, wt_blk0, nblk, row0, nrows, name):
    rc = min(TM, nrows)

    def body(h_ref, w_ref, o_ref):
        w = w_ref[...]

        def step(i, carry):
            r = pl.multiple_of(i * rc, rc)
            o_ref[pl.ds(r, rc), :] = lax.dot_general(h_ref[pl.ds(row0 + r, rc), :], w, (((1,), (1,)), ((), ())),
                                                     preferred_element_type=F32)
            return carry

        lax.fori_loop(0, nrows // rc, step, 0)

    return _call(body, name, (nblk,), [_full(h.shape), pl.BlockSpec((WB, wt.shape[1]), lambda j: (wt_blk0 + j, 0))],
                 pl.BlockSpec((nrows, WB), lambda j: (0, j)), _sds((nrows, nblk * WB)), sem=("parallel",))(h, wt)


def _dw_tn(a, b, row0, out_rows, out_blk0, name, prev=None):
    m, n = a.shape
    k2 = b.shape[1]
    nblk = n // WB
    rc = min(TM, m)

    def body(*refs):
        a_ref, b_ref, o_ref, acc_ref = refs[0], refs[1], refs[-2], refs[-1]
        acc_ref[...] = jnp.zeros_like(acc_ref)

        def step(i, carry):
            r = pl.multiple_of(i * rc, rc)
            acc_ref[...] += lax.dot_general(a_ref[pl.ds(r, rc), :], b_ref[pl.ds(row0 + r, rc), :], (((0,), (0,)), ((), ())),
                                            preferred_element_type=F32)
            return carry

        lax.fori_loop(0, m // rc, step, 0)
        o_ref[...] = acc_ref[...].astype(BF16)

    in_specs = [pl.BlockSpec((m, WB), lambda j: (0, j)), _full(b.shape)]
    args = [a, b]
    aliases = None
    if prev is not None:
        in_specs.append(pl.BlockSpec(memory_space=pl.ANY))
        args.append(prev)
        aliases = {2: 0}
    return _call(body, name, (nblk,), in_specs, pl.BlockSpec((WB, k2), lambda j: (out_blk0 + j, 0)), _sds((out_rows, k2), BF16),
                 scratch=[pltpu.VMEM((WB, k2), F32)], sem=("arbitrary",), aliases=aliases)(*args)


def _mm_nn(a, w, w_blk0, name):
    kb = WB
    m, kc = a.shape
    n = w.shape[1]
    tmb = 3 * TM if m % (3 * TM) == 0 else (2 * TM if m % (2 * TM) == 0 else TM)

    def body(a_ref, w_ref, o_ref):
        @pl.when(pl.program_id(1) == 0)
        def _():
            o_ref[...] = jnp.zeros_like(o_ref)

        o_ref[...] += jnp.dot(a_ref[...], w_ref[...], preferred_element_type=F32)

    return _call(body, name, (m // tmb, kc // kb),
                 [pl.BlockSpec((tmb, kb), lambda i, k: (i, k)), pl.BlockSpec((kb, n), lambda i, k: (w_blk0 + k, 0))],
                 pl.BlockSpec((tmb, n), lambda i, k: (i, 0)), _sds((m, n)), sem=("parallel", "arbitrary"))(a, w)


def _mods_fwd(cc, ada_t, ada_b):
    tn = 512 if (3 * D) % 512 == 0 else 256

    def body(cc_ref, w_ref, b_ref, o_ref):
        s = _silu(cc_ref[...]).astype(BF16)
        o_ref[...] = lax.dot_general(s, w_ref[...], (((1,), (1,)), ((), ())), preferred_element_type=F32) + b_ref[...]

    return _call(body, "mods_fwd", (3 * D // tn,),
                 [_full((8, D)), pl.BlockSpec((tn, D), lambda j: (j, 0)), pl.BlockSpec((1, tn), lambda j: (0, j))],
                 pl.BlockSpec((8, tn), lambda j: (0, j)), _sds((8, 3 * D)), sem=("parallel",))(cc, ada_t, ada_b)


def _sel_mod(mods_ref, is_ctx, lo):
    return jnp.where(is_ctx, mods_ref[1:2, lo:lo + D], mods_ref[0:1, lo:lo + D])


def _h_fwd(xcat, mods, norm_g):
    dm = _dims()

    def body(x_ref, m_ref, g_ref, o_ref):
        is_ctx = pl.program_id(0) < dm["NCT"]
        o_ref[...] = _h_math(x_ref[...], g_ref[...], _sel_mod(m_ref, is_ctx, D), _sel_mod(m_ref, is_ctx, 0)).astype(BF16)

    return _call(body, "h_fwd", (dm["TT"] // TM,),
                 [pl.BlockSpec((TM, D), lambda i: (i, 0)), _full((8, 3 * D)), _full((1, D))],
                 pl.BlockSpec((TM, D), lambda i: (i, 0)), _sds((dm["TT"], D), BF16), sem=("parallel",))(xcat, mods, norm_g)


def _chunk_of(d, s):
    dm = _dims()
    ncc, nch = dm["NCC"], dm["NCH"]
    return s if d == 0 else jnp.where(s < ncc, ncc - 1 - s, nch - 1 - (s - ncc))


HG_GROUP = 4
RW_GROUP = 2


def _dir_specs(make):
    return [make(d) for d in range(2)]


def _lanes(j):
    return slice(j * 128, (j + 1) * 128)


def _hg_scan_fwd(p_hg, lb3):
    dm = _dims()
    nh, nch, tt = dm["HG_H"], dm["NCH"], dm["TT"]
    pp = min(HG_GROUP, nh)
    lw_ = 128 * pp

    def body(q0, q1, i0, i1, f0, f1, lb0, lb1, o0, o1, ck_ref, s_ref):
        @pl.when(pl.program_id(1) == 0)
        def _():
            s_ref[...] = jnp.zeros_like(s_ref)

        keys, chains = [], []
        for d, refs in enumerate(((q0, i0, f0, lb0), (q1, i1, f1, lb1))):
            vals = [r[...] for r in refs]
            for j in range(pp):
                keys.append((d, j))
                chains.append((s_ref[d, j], *[a[:, _lanes(j)] for a in vals], d))
        for (d, j), chain, (out, s1) in zip(keys, chains, _hg_chunks(chains)):
            ck_ref[d, j] = chain[0]
            (o0, o1)[d][:, _lanes(j)] = out
            s_ref[d, j] = s1

    blk = lambda off: _dir_specs(lambda d: pl.BlockSpec((C, lw_), lambda h, s: (_chunk_of(d, s), off * nh // pp + h)))
    return _call(body, "hg_scan_fwd", (nh // pp, nch),
                 blk(0) + blk(1) + _dir_specs(lambda d: pl.BlockSpec((C, lw_), lambda h, s: (_chunk_of(d, s), (2 + d) * nh // pp + h)))
                 + _dir_specs(lambda d: pl.BlockSpec((None, 1, lw_), lambda h, s: (d, 0, h))),
                 _dir_specs(lambda d: pl.BlockSpec((C, lw_), lambda h, s: (_chunk_of(d, s), h)))
                 + [pl.BlockSpec((2, pp, None, 128, 128), lambda h, s: (0, h, s, 0, 0))],
                 [_sds((tt, HGW)), _sds((tt, HGW)), _sds((2, nh, nch, 128, 128))],
                 scratch=[pltpu.VMEM((2, pp, 128, 128), F32)], sem=("parallel", "arbitrary"))(*([p_hg] * 6), lb3, lb3)


def _hg_scan_bwd(p_hg, lb3, ck, do):
    dm = _dims()
    nh, nch, tt = dm["HG_H"], dm["NCH"], dm["TT"]
    pp = min(HG_GROUP, nh)
    lw_ = 128 * pp

    def body(q0, q1, i0, i1, f0, f1, lb0, lb1, ck_ref, do0, do1, dq0, dq1, di0, di1, df0, df1, dlb_ref, ds_ref):
        @pl.when(pl.program_id(1) == 0)
        def _():
            ds_ref[...] = jnp.zeros_like(ds_ref)
            dlb_ref[...] = jnp.zeros_like(dlb_ref)

        keys, flat, cts = [], [], []
        for d, refs in enumerate(((q0, i0, f0, lb0, do0), (q1, i1, f1, lb1, do1))):
            *vals, do_ = [r[...] for r in refs]
            for j in range(pp):
                keys.append((d, j))
                flat += [ck_ref[d, j], *[a[:, _lanes(j)] for a in vals]]
                cts.append((do_[:, _lanes(j)], ds_ref[d, j]))
        _, vjp = jax.vjp(_lockstep(_hg_chunks, [d for d, _ in keys], 5), *flat)
        grads = vjp(tuple(cts))
        for n, (d, j) in enumerate(keys):
            ds0, dq_, di_, df_, dlb = grads[5 * n:5 * n + 5]
            (dq0, dq1)[d][:, _lanes(j)] = dq_
            (di0, di1)[d][:, _lanes(j)] = di_
            (df0, df1)[d][:, _lanes(j)] = df_
            dlb_ref[d, :, _lanes(j)] += dlb
            ds_ref[d, j] = ds0

    ch = lambda d, s: _chunk_of(d, nch - 1 - s)
    blk = lambda off: _dir_specs(lambda d: pl.BlockSpec((C, lw_), lambda h, s: (ch(d, s), off * nh // pp + h)))
    tok = _dir_specs(lambda d: pl.BlockSpec((C, lw_), lambda h, s: (ch(d, s), h)))
    return _call(body, "hg_scan_bwd", (nh // pp, nch),
                 blk(0) + blk(1) + _dir_specs(lambda d: pl.BlockSpec((C, lw_), lambda h, s: (ch(d, s), (2 + d) * nh // pp + h)))
                 + _dir_specs(lambda d: pl.BlockSpec((None, 1, lw_), lambda h, s: (d, 0, h)))
                 + [pl.BlockSpec((2, pp, None, 128, 128), lambda h, s: (0, h, nch - 1 - s, 0, 0))] + tok,
                 tok * 3 + [pl.BlockSpec((2, 1, lw_), lambda h, s: (0, 0, h))],
                 [_sds((tt, HGW))] * 6 + [_sds((2, 1, HGW))],
                 scratch=[pltpu.VMEM((2, pp, 128, 128), F32)], sem=("parallel", "arbitrary"))(*([p_hg] * 6), lb3, lb3, ck, do, do)


def _rw_scan_fwd(sh, kap, lw, b, kd):
    dm = _dims()
    npair, nch, tt = dm["RW_P"], dm["NCH"], dm["TT"]
    pp = min(RW_GROUP, npair)
    lw_ = 128 * pp

    def body(r0, r1, v0, v1, ka0, ka1, lw0, lw1, b0, b1, kd0, kd1, y0, y1, ck_ref, s_ref):
        @pl.when(pl.program_id(1) == 0)
        def _():
            s_ref[...] = jnp.zeros_like(s_ref)

        keys, chains = [], []
        for d, refs in enumerate(((r0, lw0, ka0, b0, v0, kd0), (r1, lw1, ka1, b1, v1, kd1))):
            vals = [r[...] for r in refs]
            for j in range(pp):
                keys.append((d, j))
                chains.append((s_ref[d, j], *[a[:, _lanes(j)] for a in vals], d))
        for (d, j), chain, (out, s1) in zip(keys, chains, _rw_chunks(chains)):
            ck_ref[d, j] = chain[0]
            (y0, y1)[d][:, _lanes(j)] = out
            s_ref[d, j] = s1

    two = lambda off: _dir_specs(lambda d: pl.BlockSpec((C, lw_), lambda p, s: (_chunk_of(d, s), off * npair // pp + p)))
    three = _dir_specs(lambda d: pl.BlockSpec((None, C, lw_), lambda p, s: (d, _chunk_of(d, s), p)))
    return _call(body, "rw_scan_fwd", (npair // pp, nch), two(0) + two(2) + two(0) + three * 3,
                 two(0) + [pl.BlockSpec((2, pp, None, 128, 128), lambda p, s: (0, p, s, 0, 0))],
                 [_sds((tt, RWW)), _sds((tt, RWW)), _sds((2, npair, nch, 128, 128))],
                 scratch=[pltpu.VMEM((2, pp, 128, 128), F32)], sem=("parallel", "arbitrary"))(
        sh, sh, sh, sh, kap, kap, lw, lw, b, b, kd, kd)


def _rw_scan_bwd(sh, kap, lw, b, kd, ck, dy):
    dm = _dims()
    npair, nch, tt = dm["RW_P"], dm["NCH"], dm["TT"]
    pp = min(RW_GROUP, npair)
    lw_ = 128 * pp

    def body(r0, r1, v0, v1, ka0, ka1, lw0, lw1, b0, b1, kd0, kd1, ck_ref, dy0, dy1,
             dr0, dr1, dv0, dv1, dka0, dka1, dlw0, dlw1, db0, db1, dkd0, dkd1, ds_ref):
        @pl.when(pl.program_id(1) == 0)
        def _():
            ds_ref[...] = jnp.zeros_like(ds_ref)

        keys, flat, cts = [], [], []
        for d, refs in enumerate(((r0, lw0, ka0, b0, v0, kd0, dy0), (r1, lw1, ka1, b1, v1, kd1, dy1))):
            *vals, dy_ = [r[...] for r in refs]
            for j in range(pp):
                keys.append((d, j))
                flat += [ck_ref[d, j], *[a[:, _lanes(j)] for a in vals]]
                cts.append((dy_[:, _lanes(j)], ds_ref[d, j]))
        _, vjp = jax.vjp(_lockstep(_rw_chunks, [d for d, _ in keys], 7), *flat)
        grads = vjp(tuple(cts))
        outs = ((dr0, dlw0, dka0, db0, dv0, dkd0), (dr1, dlw1, dka1, db1, dv1, dkd1))
        for n, (d, j) in enumerate(keys):
            ds0, *g_in = grads[7 * n:7 * n + 7]
            for o_ref, g in zip(outs[d], g_in):
                o_ref[:, _lanes(j)] = g
            ds_ref[d, j] = ds0

    ch = lambda d, s: _chunk_of(d, nch - 1 - s)
    two = lambda off: _dir_specs(lambda d: pl.BlockSpec((C, lw_), lambda p, s: (ch(d, s), off * npair // pp + p)))
    three = _dir_specs(lambda d: pl.BlockSpec((None, C, lw_), lambda p, s: (d, ch(d, s), p)))
    return _call(body, "rw_scan_bwd", (npair // pp, nch),
                 two(0) + two(2) + two(0) + three * 3
                 + [pl.BlockSpec((2, pp, None, 128, 128), lambda p, s: (0, p, nch - 1 - s, 0, 0))] + two(0),
                 two(0) * 6, [_sds((tt, RWW))] * 12,
                 scratch=[pltpu.VMEM((2, pp, 128, 128), F32)], sem=("parallel", "arbitrary"))(
        sh, sh, sh, sh, kap, kap, lw, lw, b, b, kd, kd, ck, dy, dy)


def _tile_pos(i):
    dm = _dims()
    is_ctx = i < dm["NCT"]
    rows = lax.broadcasted_iota(jnp.int32, (TM, 1), 0)
    tl = rows + (i - dm["NCT"]) * TM
    width = jnp.where(is_ctx, TC, GW)
    assert TC & (TC - 1) == 0 and GW & (GW - 1) == 0 and TM % GW == 0
    colp = rows & (width - 1)
    return is_ctx, rows, tl, colp, width


def _shift_lr(cur, i):
    _, _, _, colp, width = _tile_pos(i)
    left = jnp.where(colp == 0, 0.0, pltpu.roll(cur, 1, 0))
    right = jnp.where(colp == width - 1, 0.0, pltpu.roll(cur, TM - 1, 0))
    return left, right


def _shift_ud(cur, prv, nxt, i):
    is_ctx, rows, tl, _, _ = _tile_pos(i)
    if TM > GW:
        up = jnp.where(rows >= GW, pltpu.roll(cur, GW, 0), pltpu.roll(prv, GW, 0))
        down = jnp.where(rows < TM - GW, pltpu.roll(cur, TM - GW, 0), pltpu.roll(nxt, TM - GW, 0))
    else:
        up, down = prv, nxt
    up = jnp.where(jnp.logical_or(is_ctx, tl < GW), 0.0, up)
    down = jnp.where(jnp.logical_or(is_ctx, tl >= T - GW), 0.0, down)
    return up, down


def _shift_specs():
    dm = _dims()
    nt = dm["TT"] // TM
    cw = max(w for w in range(128, 1664 + 1, 128) if dm["SH"] % w == 0)
    cur = pl.BlockSpec((TM, cw), lambda j, i: (i, j))
    prv = pl.BlockSpec((TM, cw), lambda j, i: (jnp.maximum(i - 1, 0), j))
    nxt = pl.BlockSpec((TM, cw), lambda j, i: (jnp.minimum(i + 1, nt - 1), j))
    mu = pl.BlockSpec((4, cw), lambda j, i: (0, j))
    return nt, cw, cur, prv, nxt, mu


def _shift_fwd(p_rw, mu):
    dm = _dims()
    nt, cw, cur, prv, nxt, mus = _shift_specs()

    def body(c_ref, p_ref, n_ref, mu_ref, o_ref):
        i = pl.program_id(1)
        p, m = c_ref[...], mu_ref[...]
        left, right = _shift_lr(p, i)
        up, down = _shift_ud(p, p_ref[...], n_ref[...], i)
        vert = jnp.where(i < dm["NCT"], 0.0, 1.0)
        o_ref[...] = (p * (1.0 - m[0:1] - m[1:2] - vert * (m[2:3] + m[3:4]))
                      + m[0:1] * left + m[1:2] * right + m[2:3] * up + m[3:4] * down)

    return _call(body, "shift_fwd", (dm["SH"] // cw, nt), [cur, prv, nxt, mus], cur, _sds((dm["TT"], dm["SH"])),
                 sem=("parallel", "parallel"))(p_rw, p_rw, p_rw, mu)


def _shift_bwd(p_rw, dsh, mu):
    dm = _dims()
    nt, cw, cur, prv, nxt, mus = _shift_specs()

    def body(c_ref, p_ref, n_ref, gc_ref, gp_ref, gn_ref, mu_ref, dp_ref, dmu_ref):
        i = pl.program_id(1)
        p, g, m = c_ref[...], gc_ref[...], mu_ref[...]
        vert = jnp.where(i < dm["NCT"], 0.0, 1.0)
        _, from_right = _shift_lr(m[0:1] * g, i)
        from_left, _ = _shift_lr(m[1:2] * g, i)
        _, from_down = _shift_ud(m[2:3] * g, g, m[2:3] * gn_ref[...], i)
        from_up, _ = _shift_ud(m[3:4] * g, m[3:4] * gp_ref[...], g, i)
        dp = g * (1.0 - m[0:1] - m[1:2] - vert * (m[2:3] + m[3:4])) + from_right + from_left + from_down + from_up
        dp_ref[...] = dp.astype(BF16)

        @pl.when(i == 0)
        def _():
            dmu_ref[...] = jnp.zeros_like(dmu_ref)

        left, right = _shift_lr(p, i)
        up, down = _shift_ud(p, p_ref[...], n_ref[...], i)
        s = lambda a: jnp.sum(a, axis=0, keepdims=True)
        dmu_ref[...] += jnp.concatenate([s(g * (left - p)), s(g * (right - p)), vert * s(g * (up - p)), vert * s(g * (down - p))], axis=0)

    return _call(body, "shift_bwd", (dm["SH"] // cw, nt), [cur, prv, nxt, cur, prv, nxt, mus],
                 [cur, pl.BlockSpec((4, cw), lambda j, i: (0, j))], [_sds((dm["TT"], dm["SH"]), BF16), _sds((4, dm["SH"]))],
                 sem=("parallel", "arbitrary"))(p_rw, p_rw, p_rw, dsh, dsh, dsh, mu)


def _acc_out(ref, first, vals):
    @pl.when(first)
    def _():
        for r in ref:
            r[...] = jnp.zeros_like(r)

    for r, v in zip(ref, vals):
        r[...] += v


def _rowsum(a):
    return jnp.sum(a, axis=0, keepdims=True)


def _rw_prep_specs():
    dm = _dims()
    rw = RWW // 128
    tok = lambda width, cb: pl.BlockSpec((TMV, width), lambda i: (i, cb))
    ins = [pl.BlockSpec((TMV, RWW), lambda i: (i, 1)), tok(128, 3 * rw), tok(128, 3 * rw + 1),
           _full((2, RWW)), _full((128, RWW)), _full((128, RWW)), _full((2, RWW)), _full((128, RWW)), _full((128, RWW)),
           _full((1, RWW)), _full((1, RWW))]
    return dm, ins


def _rw_prep_fwd(sh, w0, w2p0, w2p1, a0, a2p0, a2p1, kkp, kap_p):
    dm, ins = _rw_prep_specs()
    tt = dm["TT"]

    def body(k_ref, low_ref, loa_ref, w0_ref, w20_ref, w21_ref, a0_ref, a20_ref, a21_ref, kk_ref, ka_ref, kap_ref, lw_ref, b_ref, kd_ref):
        tw, la = jnp.tanh(low_ref[...]), loa_ref[...]
        kap, lw0, b0, kd0, lw1, b1, kd1 = _rw_prep_math(
            k_ref[...], _mm(tw, w20_ref[...]), _mm(tw, w21_ref[...]), _mm(la, a20_ref[...]), _mm(la, a21_ref[...]),
            w0_ref[...], a0_ref[...], kk_ref[...], ka_ref[...])
        kap_ref[...] = kap
        lw_ref[0], lw_ref[1] = lw0, lw1
        b_ref[0], b_ref[1] = b0, b1
        kd_ref[0], kd_ref[1] = kd0, kd1

    two = pl.BlockSpec((2, TMV, RWW), lambda i: (0, i, 0))
    return _call(body, "rw_prep_fwd", (tt // TMV,), ins, [pl.BlockSpec((TMV, RWW), lambda i: (i, 0)), two, two, two],
                 [_sds((tt, RWW)), _sds((2, tt, RWW)), _sds((2, tt, RWW)), _sds((2, tt, RWW))], sem=("parallel",))(
        sh, sh, sh, w0, w2p0, w2p1, a0, a2p0, a2p1, kkp, kap_p)


def _rw_prep_bwd(sh, w0, w2p0, w2p1, a0, a2p0, a2p1, kkp, kap_p, scan_grads, dkd_p, dr_p, dv_p):
    dm, ins = _rw_prep_specs()
    tt, sh_w = dm["TT"], dm["SH"]
    one = pl.BlockSpec((TMV, RWW), lambda i: (i, 0))

    def body(k_ref, low_ref, loa_ref, w0_ref, w20_ref, w21_ref, a0_ref, a20_ref, a21_ref, kk_ref, ka_ref,
             dr0, dr1, dv0, dv1, dka0, dka1, dlw0, dlw1, db0, db1, dkd0, dkd1, dkdp_ref, drp_ref, dvp_ref,
             dsh_ref, dw0_ref, dw20_ref, dw21_ref, da0_ref, da20_ref, da21_ref, dkk_ref, dka_ref):
        tw, la = jnp.tanh(low_ref[...]), loa_ref[...]
        w20, w21, a20, a21 = w20_ref[...], w21_ref[...], a20_ref[...], a21_ref[...]
        _, vjp = jax.vjp(_rw_prep_math, k_ref[...], _mm(tw, w20), _mm(tw, w21), _mm(la, a20), _mm(la, a21),
                         w0_ref[...], a0_ref[...], kk_ref[...], ka_ref[...])
        dkp = dkdp_ref[...]
        dk, dpw0, dpw1, dpa0, dpa1, dw0, da0, dkk, dka = vjp(
            (dka0[...] + dka1[...], dlw0[...], db0[...], dkd0[...] + dkp, dlw1[...], db1[...], dkd1[...] + dkp))
        twt, lat = tw.T, la.T
        dtw = _mm_nt(dpw0, w20) + _mm_nt(dpw1, w21)
        dsh_ref[:, 0:RWW] = dr0[...] + dr1[...] + drp_ref[...]
        dsh_ref[:, RWW:2 * RWW] = dk
        dsh_ref[:, 2 * RWW:3 * RWW] = dv0[...] + dv1[...] + dvp_ref[...]
        dsh_ref[:, 3 * RWW:3 * RWW + 128] = dtw * (1.0 - tw * tw)
        dsh_ref[:, 3 * RWW + 128:3 * RWW + 256] = _mm_nt(dpa0, a20) + _mm_nt(dpa1, a21)
        _acc_out((dw0_ref, dw20_ref, dw21_ref, da0_ref, da20_ref, da21_ref, dkk_ref, dka_ref), pl.program_id(0) == 0,
                 (dw0, _mm(twt, dpw0), _mm(twt, dpw1), da0, _mm(lat, dpa0), _mm(lat, dpa1), dkk, dka))

    par = [_full((2, RWW)), _full((128, RWW)), _full((128, RWW)), _full((2, RWW)), _full((128, RWW)), _full((128, RWW)),
           _full((1, RWW)), _full((1, RWW))]
    return _call(body, "rw_prep_bwd", (tt // TMV,), ins + [one] * 15,
                 [pl.BlockSpec((TMV, sh_w), lambda i: (i, 0))] + par,
                 [_sds((tt, sh_w)), _sds((2, RWW)), _sds((128, RWW)), _sds((128, RWW)), _sds((2, RWW)), _sds((128, RWW)), _sds((128, RWW)),
                  _sds((1, RWW)), _sds((1, RWW))],
                 sem=("arbitrary",))(sh, sh, sh, w0, w2p0, w2p1, a0, a2p0, a2p1, kkp, kap_p, *scan_grads, dkd_p, dr_p, dv_p)


def _rw_post_specs():
    dm = _dims()
    nv = dm["NCTV"]
    rw = RWW // 128
    lat3 = lambda d: pl.BlockSpec((None, TMV, RWW), lambda i: (d, nv + i, 0))
    lat2 = pl.BlockSpec((TMV, RWW), lambda i: (nv + i, 0))
    ins = [lat2, lat2, lat3(0), lat3(1),
           pl.BlockSpec((TMV, RWW), lambda i: (nv + i, 0)), pl.BlockSpec((TMV, RWW), lambda i: (nv + i, 2)),
           pl.BlockSpec((TMV, RWW), lambda i: (i, 0)), _full((1, RWW)), _full((1, RWW)), _full((1, RWW))]
    return dm, nv, ins


def _rw_post_fwd(y0_, y1_, kd, sh, p_z, rk, gng, gnb):
    dm, nv, ins = _rw_post_specs()

    def body(y0, y1, k0, k1, r, v, z, rk_ref, g_ref, b_ref, o_ref):
        o_ref[...] = _rw_post_math(y0[...], y1[...], k0[...], k1[...], r[...], v[...], z[...], rk_ref[...], g_ref[...], b_ref[...]).astype(BF16)

    return _call(body, "rw_post_fwd", (T // TMV,), ins, pl.BlockSpec((TMV, RWW), lambda i: (i, 0)), _sds((T, RWW), BF16),
                 sem=("parallel",))(y0_, y1_, kd, kd, sh, sh, p_z, rk, gng, gnb)


def _rw_post_bwd(y0_, y1_, kd, sh, p_z, rk, gng, gnb, dout):
    dm, nv, _ = _rw_post_specs()
    tt = dm["TT"]
    cl = lambda i: jnp.maximum(i - nv, 0)
    all3 = lambda d: pl.BlockSpec((None, TMV, RWW), lambda i: (d, i, 0))
    all2 = pl.BlockSpec((TMV, RWW), lambda i: (i, 0))
    ins = [all2, all2, all3(0), all3(1),
           pl.BlockSpec((TMV, RWW), lambda i: (i, 0)), pl.BlockSpec((TMV, RWW), lambda i: (i, 2)),
           pl.BlockSpec((TMV, RWW), lambda i: (cl(i), 0)), _full((1, RWW)), _full((1, RWW)), _full((1, RWW)),
           pl.BlockSpec((TMV, RWW), lambda i: (cl(i), 0))]
    tok = pl.BlockSpec((TMV, RWW), lambda i: (i, 0))

    def body(y0, y1, k0, k1, r, v, z, rk_ref, g_ref, b_ref, do_ref, dy_ref, dkd_ref, dr_ref, dv_ref, dz_ref, drk_ref, dg_ref, db_ref):
        i = pl.program_id(0)
        lat = jnp.where(i >= nv, 1.0, 0.0)
        _, vjp = jax.vjp(_rw_post_math, y0[...], y1[...], k0[...], k1[...], r[...], v[...], z[...], rk_ref[...], g_ref[...], b_ref[...])
        dy0, _, dk0, _, dr, dv, dz, drk, dg, db = vjp(do_ref[...] * lat)
        dy_ref[...] = dy0
        dkd_ref[...] = dk0
        dr_ref[...] = dr
        dv_ref[...] = dv
        dz_ref[...] = dz.astype(BF16)
        _acc_out((drk_ref, dg_ref, db_ref), i == 0, (drk, dg, db))

    return _call(body, "rw_post_bwd", (tt // TMV,), ins,
                 [tok, tok, tok, tok, pl.BlockSpec((TMV, RWW), lambda i: (cl(i), 0)), _full((1, RWW)), _full((1, RWW)), _full((1, RWW))],
                 [_sds((tt, RWW))] * 4 + [_sds((T, RWW), BF16), _sds((1, RWW)), _sds((1, RWW)), _sds((1, RWW))],
                 sem=("arbitrary",))(y0_, y1_, kd, kd, sh, sh, p_z, rk, gng, gnb, dout)


def _hg_post_fwd(o0_, o1_, p_hg, hg_g):
    dm = _dims()
    nv = dm["NCTV"]
    lat3 = lambda d: pl.BlockSpec((TMV, HGW), lambda i: (nv + i, 0))

    def body(o0, o1, z, g, out):
        out[...] = _hg_post_math(o0[...], o1[...], z[...], g[...]).astype(BF16)

    return _call(body, "hg_post_fwd", (T // TMV,), [lat3(0), lat3(1), pl.BlockSpec((TMV, HGW), lambda i: (nv + i, 4)), _full((1, HGW))],
                 pl.BlockSpec((TMV, HGW), lambda i: (i, 0)), _sds((T, HGW), BF16), sem=("parallel",))(o0_, o1_, p_hg, hg_g)


def _hg_post_bwd(o0_, o1_, p_hg, hg_g, dout):
    dm = _dims()
    nv, tt = dm["NCTV"], dm["TT"]
    all3 = lambda d: pl.BlockSpec((TMV, HGW), lambda i: (i, 0))
    tok = pl.BlockSpec((TMV, HGW), lambda i: (i, 0))

    def body(o0, o1, z, g, do_in, do_ref, dz_ref, dg_ref):
        i = pl.program_id(0)
        lat = jnp.where(i >= nv, 1.0, 0.0)
        _, vjp = jax.vjp(_hg_post_math, o0[...], o1[...], z[...], g[...])
        d0, _, dz, dg = vjp(do_in[...] * lat)
        do_ref[...] = d0
        dz_ref[...] = dz
        _acc_out((dg_ref,), i == 0, (dg,))

    return _call(body, "hg_post_bwd", (tt // TMV,),
                 [all3(0), all3(1), pl.BlockSpec((TMV, HGW), lambda i: (i, 4)), _full((1, HGW)),
                  pl.BlockSpec((TMV, HGW), lambda i: (jnp.maximum(i - nv, 0), 0))],
                 [tok, tok, _full((1, HGW))], [_sds((tt, HGW)), _sds((tt, HGW)), _sds((1, HGW))], sem=("arbitrary",))(o0_, o1_, p_hg, hg_g, dout)


def _hg_dproj(dq0, dq1, di0, di1, df0, df1, dz):
    dm = _dims()
    tt = dm["TT"]
    tok = pl.BlockSpec((TMV, HGW), lambda i: (i, 0))

    def body(q0, q1, i0, i1, f0, f1, z, o_ref):
        o_ref[:, 0:HGW] = (q0[...] + q1[...]).astype(BF16)
        o_ref[:, HGW:2 * HGW] = (i0[...] + i1[...]).astype(BF16)
        o_ref[:, 2 * HGW:3 * HGW] = f0[...].astype(BF16)
        o_ref[:, 3 * HGW:4 * HGW] = f1[...].astype(BF16)
        o_ref[:, 4 * HGW:5 * HGW] = z[...].astype(BF16)

    return _call(body, "hg_dproj", (tt // TMV,), [tok] * 7,
                 pl.BlockSpec((TMV, 5 * HGW), lambda i: (i, 0)), _sds((tt, 5 * HGW), BF16), sem=("parallel",))(
        dq0, dq1, di0, di1, df0, df1, dz)


def _merge1_fwd(out_hg, out_rw, whg_t, wrw_t, p_g):
    tok = pl.BlockSpec((TM, D), lambda i: (i, 0))

    def body(h_ref, r_ref, wh_ref, wr_ref, g1_ref, g2_ref, a_ref, b_ref, m_ref):
        nt = (((1,), (1,)), ((), ()))
        a = lax.dot_general(h_ref[...], wh_ref[...], nt, preferred_element_type=F32)
        b = lax.dot_general(r_ref[...], wr_ref[...], nt, preferred_element_type=F32)
        a_ref[...] = a
        b_ref[...] = b
        m_ref[...] = _gate_math(g1_ref[...], g2_ref[...], a, b).astype(BF16)

    return _call(body, "merge1_fwd", (T // TM,),
                 [pl.BlockSpec((TM, HGW), lambda i: (i, 0)), pl.BlockSpec((TM, RWW), lambda i: (i, 0)), _full((D, HGW)), _full((D, RWW)),
                  pl.BlockSpec((TM, D), lambda i: (i, 0)), pl.BlockSpec((TM, D), lambda i: (i, 1))],
                 [tok, tok, tok], [_sds((T, D)), _sds((T, D)), _sds((T, D), BF16)], sem=("parallel",))(out_hg, out_rw, whg_t, wrw_t, p_g, p_g)


def _head_fwd_bwd(x, merged, w_out, mods, final_g, tgt):
    tok = pl.BlockSpec((TM, D), lambda i: (i, 0))

    def body(x_ref, m_ref, w_ref, mods_ref, fg_ref, t_ref, dxo_ref, dmo_ref, loss_ref, dgate_ref, dfg_ref):
        mo = jnp.dot(m_ref[...], w_ref[...], preferred_element_type=F32)
        gate = mods_ref[0:1, 2 * D:3 * D]
        loss, vjp = jax.vjp(lambda x_, mo_, g_, fg_: _head_math(x_, mo_, g_, fg_, t_ref[...]), x_ref[...], mo, gate, fg_ref[...])
        dx, dmo, dgate, dfg = vjp(jnp.ones((1, 1), F32))
        dxo_ref[...] = dx
        dmo_ref[...] = dmo.astype(BF16)
        _acc_out((loss_ref, dgate_ref, dfg_ref), pl.program_id(0) == 0, (jnp.broadcast_to(loss, (1, 128)), dgate, dfg))

    return _call(body, "head_fwd_bwd", (T // TM,), [tok, tok, _full((D, D)), _full((8, 3 * D)), _full((1, D)), tok],
                 [tok, tok, _full((1, 128)), _full((1, D)), _full((1, D))],
                 [_sds((T, D)), _sds((T, D), BF16), _sds((1, 128)), _sds((1, D)), _sds((1, D))], sem=("arbitrary",))(
        x, merged, w_out, mods, final_g, tgt)


def _merge_bwd(dmo, w_out, p_g, a, b):
    tok = pl.BlockSpec((TM, D), lambda i: (i, 0))

    def body(dmo_ref, w_ref, g1_ref, g2_ref, a_ref, b_ref, da_ref, db_ref, dg_ref):
        dm_ = lax.dot_general(dmo_ref[...], w_ref[...], (((1,), (1,)), ((), ())), preferred_element_type=F32)
        _, vjp = jax.vjp(_gate_math, g1_ref[...], g2_ref[...], a_ref[...], b_ref[...])
        dg1, dg2, da, db = vjp(dm_)
        da_ref[...] = da.astype(BF16)
        db_ref[...] = db.astype(BF16)
        dg_ref[:, 0:D] = dg1.astype(BF16)
        dg_ref[:, D:2 * D] = dg2.astype(BF16)

    return _call(body, "merge_bwd", (T // TM,),
                 [tok, _full((D, D)), pl.BlockSpec((TM, D), lambda i: (i, 0)), pl.BlockSpec((TM, D), lambda i: (i, 1)), tok, tok],
                 [tok, tok, pl.BlockSpec((TM, 2 * D), lambda i: (i, 0))], [_sds((T, D), BF16), _sds((T, D), BF16), _sds((T, 2 * D), BF16)],
                 sem=("parallel",))(dmo, w_out, p_g, p_g, a, b)


def _h_bwd(xcat, mods, norm_g, dh_hg, dh_rw, dh_z, dh_g, dxo, dgate):
    dm = _dims()
    nct, tt = dm["NCT"], dm["TT"]
    tok = pl.BlockSpec((TM, D), lambda i: (i, 0))
    lat = pl.BlockSpec((TM, D), lambda i: (jnp.maximum(i - nct, 0), 0))

    def body(x_ref, m_ref, g_ref, d1, d2, d3, d4, dxo_ref, dgate_ref, gx_ref, dng_ref, dmods_ref):
        i = pl.program_id(0)
        is_ctx = i < nct
        latf = jnp.where(is_ctx, 0.0, 1.0)
        dh = d1[...] + d2[...] + latf * (d3[...] + d4[...])
        _, vjp = jax.vjp(_h_math, x_ref[...], g_ref[...], _sel_mod(m_ref, is_ctx, D), _sel_mod(m_ref, is_ctx, 0))
        dx, dng, dscale, dshift = vjp(dh)
        gx_ref[...] = dx + dxo_ref[...]

        @pl.when(i == 0)
        def _():
            dng_ref[...] = jnp.zeros_like(dng_ref)
            dmods_ref[...] = jnp.zeros_like(dmods_ref)
            dmods_ref[0:1, 2 * D:3 * D] = dgate_ref[...]

        dng_ref[...] += dng
        row = lax.broadcasted_iota(jnp.int32, (8, 1), 0)
        sel = jnp.where(row == jnp.where(is_ctx, 1, 0), 1.0, 0.0)
        dmods_ref[:, 0:D] += sel * dshift
        dmods_ref[:, D:2 * D] += sel * dscale

    return _call(body, "h_bwd", (tt // TM,), [tok, _full((8, 3 * D)), _full((1, D)), tok, tok, lat, lat, lat, _full((1, D))],
                 [lat, _full((1, D)), _full((8, 3 * D))], [_sds((T, D)), _sds((1, D)), _sds((8, 3 * D))], sem=("arbitrary",))(
        xcat, mods, norm_g, dh_hg, dh_rw, dh_z, dh_g, dxo, dgate)


def _mods_bwd(cc, ada_t, dmods, dmods_t):
    tn = 512 if (3 * D) % 512 == 0 else 256

    nsteps = 3 * D // tn

    def body(cc_ref, w_ref, dm_ref, dmt_ref, dcc_ref, dw_ref, db_ref):
        c_ = cc_ref[...]
        s, vjp = jax.vjp(_silu, c_)
        dm_ = dm_ref[...]
        ds = jnp.dot(dm_.astype(BF16), w_ref[...], preferred_element_type=F32)
        _acc_out((dcc_ref,), pl.program_id(0) == 0, (ds,))

        @pl.when(pl.program_id(0) == nsteps - 1)
        def _():
            dcc_ref[...] = vjp(dcc_ref[...])[0]

        dw_ref[...] = _mm(dmt_ref[...], s).astype(BF16)
        db_ref[...] = dm_[0:1] + dm_[1:2]

    return _call(body, "mods_bwd", (nsteps,),
                 [_full((8, D)), pl.BlockSpec((tn, D), lambda j: (j, 0)), pl.BlockSpec((8, tn), lambda j: (0, j)),
                  pl.BlockSpec((tn, 8), lambda j: (j, 0))],
                 [_full((8, D)), pl.BlockSpec((tn, D), lambda j: (j, 0)), pl.BlockSpec((1, tn), lambda j: (0, j))],
                 [_sds((8, D)), _sds((3 * D, D), BF16), _sds((1, 3 * D))], sem=("arbitrary",))(cc, ada_t, dmods, dmods_t)


def _lb_math(l0, l1):
    return jax.nn.sigmoid(l0 - l1)


def _lb_fwd(l0, l1):
    def body(a_ref, b_ref, o_ref):
        o_ref[...] = _lb_math(a_ref[...], b_ref[...])

    return _call(body, "lb_fwd", (1,), [_full((2, HGW))] * 2, _full((2, HGW)), _sds((2, HGW)))(l0, l1)


def _lb_bwd(l0, l1, dlb):
    def body(a_ref, b_ref, d_ref, da_ref, db_ref):
        _, vjp = jax.vjp(_lb_math, a_ref[...], b_ref[...])
        da_ref[...], db_ref[...] = vjp(d_ref[...])

    return _call(body, "lb_bwd", (1,), [_full((2, HGW))] * 3, [_full((2, HGW))] * 2, [_sds((2, HGW))] * 2)(l0, l1, dlb)


def _local_step(x, c, ctx, tgt, w):
    dm = _dims()
    tt, sh_w, hgc = dm["TT"], dm["SH"], dm["HGC"]
    nb = lambda cols: cols // WB
    xcat = jnp.concatenate([ctx, x], axis=0)
    cc = jnp.concatenate([c, w["c_ctx"], jnp.zeros((6, D), F32)], axis=0)
    mods = _mods_fwd(cc, w["ada_t"], w["ada_b"])
    h = _h_fwd(xcat, mods, w["norm_g"])
    win_t = w["win_t"]
    p_hg = _proj_nt(h, win_t, 0, nb(hgc), 0, tt, "proj_hg")
    p_rw = _proj_nt(h, win_t, nb(hgc), nb(sh_w), 0, tt, "proj_rw")
    p_z = _proj_nt(h, win_t, nb(hgc + sh_w), nb(RWW), TC, T, "proj_z")
    p_g = _proj_nt(h, win_t, nb(hgc + sh_w + RWW), nb(2 * D), TC, T, "proj_g")

    lb3 = _lb_fwd(w["lb0"], w["lb1"]).reshape(2, 1, HGW)
    o0, o1, hg_ck = _hg_scan_fwd(p_hg, lb3)
    out_hg = _hg_post_fwd(o0, o1, p_hg, w["hg_g"])

    sh = _shift_fwd(p_rw, w["mu"])
    zpad = jnp.zeros((LORA, RWW), F32)
    prep_w = (w["w0"], jnp.concatenate([w["w2"][0], zpad], 0), jnp.concatenate([zpad, w["w2"][1]], 0),
              w["a0"], jnp.concatenate([w["a2"][0], zpad], 0), jnp.concatenate([zpad, w["a2"][1]], 0), w["kk"], w["ka"])
    kap, lw, b, kd = _rw_prep_fwd(sh, *prep_w)
    y0, y1, rw_ck = _rw_scan_fwd(sh, kap, lw, b, kd)
    post_w = (w["rk"], w["gng"], w["gnb"])
    out_rw = _rw_post_fwd(y0, y1, kd, sh, p_z, *post_w)

    a, bb, merged = _merge1_fwd(out_hg, out_rw, w["whg_t"], w["wrw_t"], p_g)
    dxo, dmo, loss, dgate, d_fg = _head_fwd_bwd(x, merged, w["wout"], mods, w["final_g"], tgt)
    da, db, dp_g = _merge_bwd(dmo, w["wout"], p_g, a, bb)
    g_wout = _dw_tn(merged, dmo, 0, D, 0, "dw_out")
    g_whg = _dw_tn(da, out_hg, 0, D, 0, "dw_hg")
    g_wrw = _dw_tn(db, out_rw, 0, D, 0, "dw_rw")
    d_out_hg = _mm_nn(da, w["whg_t"], 0, "dx_hg")
    d_out_rw = _mm_nn(db, w["wrw_t"], 0, "dx_rw")

    do, dz_hg, d_hg_g = _hg_post_bwd(o0, o1, p_hg, w["hg_g"], d_out_hg)
    *hg_grads, dlb3 = _hg_scan_bwd(p_hg, lb3, hg_ck, do)
    dp_hg = _hg_dproj(*hg_grads, dz_hg)
    d_lb0, d_lb1 = _lb_bwd(w["lb0"], w["lb1"], dlb3.reshape(2, HGW))

    dy, dkd_p, dr_p, dv_p, dp_z, d_rk, d_gng, d_gnb = _rw_post_bwd(y0, y1, kd, sh, p_z, *post_w, d_out_rw)
    rw_grads = _rw_scan_bwd(sh, kap, lw, b, kd, rw_ck, dy)
    dsh, d_w0, d_w2p0, d_w2p1, d_a0, d_a2p0, d_a2p1, d_kk, d_ka = _rw_prep_bwd(sh, *prep_w, rw_grads, dkd_p, dr_p, dv_p)
    dp_rw, d_mu = _shift_bwd(p_rw, dsh, w["mu"])

    g_win = _dw_tn(dp_hg, h, 0, dm["NCOLS"], 0, "dw_in_hg")
    g_win = _dw_tn(dp_rw, h, 0, dm["NCOLS"], nb(hgc), "dw_in_rw", prev=g_win)
    g_win = _dw_tn(dp_z, h, TC, dm["NCOLS"], nb(hgc + sh_w), "dw_in_z", prev=g_win)
    g_win = _dw_tn(dp_g, h, TC, dm["NCOLS"], nb(hgc + sh_w + RWW), "dw_in_g", prev=g_win)
    dh_hg = _mm_nn(dp_hg, win_t, 0, "dh_hg")
    dh_rw = _mm_nn(dp_rw, win_t, nb(hgc), "dh_rw")
    dh_z = _mm_nn(dp_z, win_t, nb(hgc + sh_w), "dh_z")
    dh_g = _mm_nn(dp_g, win_t, nb(hgc + sh_w + RWW), "dh_g")
    gx, d_ng, dmods = _h_bwd(xcat, mods, w["norm_g"], dh_hg, dh_rw, dh_z, dh_g, dxo, dgate)
    dcc, g_ada, d_ada_b = _mods_bwd(cc, w["ada_t"], dmods, dmods.T)

    big = dict(win_t=g_win, wout=g_wout, whg_t=g_whg, wrw_t=g_wrw, ada_t=g_ada)
    small = dict(c_ctx=dcc[1:2], ada_b=d_ada_b, norm_g=d_ng, lb0=d_lb0, lb1=d_lb1, hg_g=d_hg_g, mu=d_mu, w0=d_w0,
                 w2=jnp.stack([d_w2p0[:LORA], d_w2p1[LORA:]]), a0=d_a0, a2=jnp.stack([d_a2p0[:LORA], d_a2p1[LORA:]]),
                 kk=d_kk, ka=d_ka, rk=d_rk, gng=d_gng, gnb=d_gnb, final_g=d_fg)
    return loss, gx, big, small


MESH = pl.DeviceIdType.MESH


def _comm_call(body, name, bufs, out_shapes, nsem, nloc):
    hbm = pl.BlockSpec(memory_space=pl.ANY)
    return pl.pallas_call(
        body, name=name, in_specs=[hbm] * len(bufs), out_specs=[hbm] * len(out_shapes), out_shape=out_shapes,
        scratch_shapes=[pltpu.SemaphoreType.DMA((nsem,)), pltpu.SemaphoreType.DMA((nsem,)), pltpu.SemaphoreType.DMA((nloc,))],
    )(*bufs)


def _gather2(bufs, name):
    nbuf = len(bufs)

    def body(*refs):
        ins, outs = refs[:nbuf], refs[nbuf:2 * nbuf]
        send_sems, recv_sems, loc_sems = refs[2 * nbuf:]
        x, y, c = lax.axis_index("x"), lax.axis_index("y"), lax.axis_index("c")
        me, sib = (x, y, c), (x, y, 1 - c)
        chips = [(1 - x, y), (x, 1 - y), (1 - x, 1 - y)]

        def copy(bi, k, block, to, src=None):
            rows = outs[bi].at[4 * block[0] + 2 * block[1] + block[2]]
            return pltpu.make_async_remote_copy(src_ref=rows if src is None else src, dst_ref=rows, send_sem=send_sems.at[7 * bi + k],
                                                recv_sem=recv_sems.at[7 * bi + k], device_id=to, device_id_type=MESH)

        own = [pltpu.make_async_copy(ins[bi], outs[bi].at[4 * x + 2 * y + c], loc_sems.at[bi]) for bi in range(nbuf)]
        for cp in own:
            cp.start()
        first = []
        for bi in range(nbuf):
            first.append(copy(bi, 0, me, sib, src=ins[bi]))
            first += [copy(bi, 1 + j, me, (*chip, c), src=ins[bi]) for j, chip in enumerate(chips)]
        for cp in first:
            cp.start()
        passed = []
        for j, chip in enumerate(chips):
            for bi in range(nbuf):
                copy(bi, 1 + j, (*chip, c), me).wait_recv()
                passed.append(copy(bi, 4 + j, (*chip, c), sib))
                passed[-1].start()
        for bi in range(nbuf):
            copy(bi, 0, sib, me).wait_recv()
            for j, chip in enumerate(chips):
                copy(bi, 4 + j, (*chip, 1 - c), me).wait_recv()
        for cp in first + passed:
            cp.wait_send()
        for cp in own:
            cp.wait()

    return _comm_call(body, name, bufs, [_sds((NDEV,) + b.shape, b.dtype) for b in bufs], 7 * nbuf, nbuf)


def _pair_exchange(bufs, name):
    nbuf = len(bufs)

    def body(*refs):
        ins, got = refs[:nbuf], refs[nbuf:2 * nbuf]
        send_sems, recv_sems, _ = refs[2 * nbuf:]
        x, y, c = lax.axis_index("x"), lax.axis_index("y"), lax.axis_index("c")
        copies = []
        for bi in range(nbuf):
            for q in range(4):
                copies.append(pltpu.make_async_remote_copy(
                    src_ref=ins[bi].at[2 * q + 1 - c], dst_ref=got[bi].at[q], send_sem=send_sems.at[4 * bi + q],
                    recv_sem=recv_sems.at[4 * bi + q], device_id=(x, y, 1 - c), device_id_type=MESH))
                copies[-1].start()
        for cp in copies:
            cp.wait_send()
            cp.wait_recv()

    return _comm_call(body, name, bufs, [_sds((4,) + b.shape[1:], b.dtype) for b in bufs], 4 * nbuf, 1)


def _chip_exchange(bufs, name):
    nbuf = len(bufs)

    def body(*refs):
        ins, outs = refs[:nbuf], refs[nbuf:2 * nbuf]
        send_sems, recv_sems, _ = refs[2 * nbuf:]
        x, y, c = lax.axis_index("x"), lax.axis_index("y"), lax.axis_index("c")
        myq = 2 * x + y
        copies = []
        for bi in range(nbuf):
            for j, (qx, qy) in enumerate([(1 - x, y), (x, 1 - y), (1 - x, 1 - y)]):
                q = 2 * qx + qy
                send = pltpu.make_async_remote_copy(src_ref=ins[bi].at[q], dst_ref=outs[bi].at[myq], send_sem=send_sems.at[3 * bi + j],
                                                    recv_sem=recv_sems.at[3 * bi + j], device_id=(qx, qy, c), device_id_type=MESH)
                send.start()
                recv = pltpu.make_async_remote_copy(src_ref=ins[bi].at[myq], dst_ref=outs[bi].at[q], send_sem=send_sems.at[3 * bi + j],
                                                    recv_sem=recv_sems.at[3 * bi + j], device_id=(qx, qy, c), device_id_type=MESH)
                copies.append((send, recv))
        for send, recv in copies:
            send.wait_send()
            recv.wait_recv()

    return _comm_call(body, name, bufs, [_sds(b.shape, b.dtype) for b in bufs], 3 * nbuf, 1)


def _prefetch_call(body, name, scalars, grid, in_specs, out_specs, out_shape, args):
    return pl.pallas_call(
        body, name=name, out_shape=out_shape,
        grid_spec=pltpu.PrefetchScalarGridSpec(num_scalar_prefetch=1, grid=grid, in_specs=in_specs, out_specs=out_specs),
        compiler_params=pltpu.CompilerParams(dimension_semantics=("parallel",) * len(grid), vmem_limit_bytes=V7X_VMEM_LIMIT))(scalars, *args)


def _pair_add(g, got, core, name):
    _, rows, cols = got.shape
    tr = _row_tile(rows, 16, 256)

    def body(c_ref, a_ref, b_ref, o_ref):
        o_ref[...] = (a_ref[...].astype(F32) + b_ref[...].astype(F32)).astype(o_ref.dtype)

    blk = pl.BlockSpec((4, tr, cols), lambda i, c_ref: (0, i, 0))
    mine = pl.BlockSpec((4, None, tr, cols), lambda i, c_ref: (0, c_ref[0], i, 0))
    return _prefetch_call(body, name, core, (rows // tr,), [mine, blk], blk, _sds(got.shape, got.dtype),
                          (g.reshape(4, 2, rows, cols), got))


def _reduce_chips(chip_sums, recv, slots, name):
    _, rows, cols = recv.shape
    tr = _row_tile(rows, 16, 128)

    def body(s_ref, own_ref, r1_ref, r2_ref, r3_ref, o_ref):
        o_ref[...] = ((own_ref[...].astype(F32) + r1_ref[...].astype(F32)) + r2_ref[...].astype(F32)) + r3_ref[...].astype(F32)

    pick = lambda k: pl.BlockSpec((None, tr, cols), lambda i, s_ref: (s_ref[k], i, 0))
    return _prefetch_call(body, name, slots, (rows // tr,), [pick(0), pick(1), pick(2), pick(3)],
                          pl.BlockSpec((tr, cols), lambda i, s_ref: (i, 0)), _sds((rows, cols)), (chip_sums, recv, recv, recv))


def _row_tile(rows, mult, cap):
    best = mult
    for t in range(mult, cap + 1, mult):
        if rows % t == 0:
            best = t
    assert rows % best == 0, (rows, mult)
    return best


def _reduce_sources(r, name):
    nsrc, rows, cols = r.shape
    tr = _row_tile(rows, 16 if r.dtype.itemsize == 2 else 8, 128)

    def body(r_ref, o_ref):
        acc = r_ref[0].astype(F32)
        for j in range(1, nsrc):
            acc = acc + r_ref[j].astype(F32)
        o_ref[...] = acc

    return _call(body, name, (rows // tr,), [pl.BlockSpec((nsrc, tr, cols), lambda i: (0, i, 0))],
                 pl.BlockSpec((tr, cols), lambda i: (i, 0)), _sds((rows, cols)), sem=("parallel",))(r)


def _adamw(w, g, m, v, name):
    rows, cols = w.shape
    tr = _row_tile(rows, 8, 128)
    c1 = 1.0 - ADAM_B1 ** ADAM_STEP
    c2 = 1.0 - ADAM_B2 ** ADAM_STEP

    def body(w_ref, g_ref, m_ref, v_ref, d_ref, mo_ref, vo_ref):
        g_ = g_ref[...]
        m_ = ADAM_B1 * m_ref[...] + (1.0 - ADAM_B1) * g_
        v_ = ADAM_B2 * v_ref[...] + (1.0 - ADAM_B2) * (g_ * g_)
        d_ref[...] = -ADAM_LR * ((m_ / c1) / (jnp.sqrt(v_ / c2) + ADAM_EPS) + ADAM_WD * w_ref[...])
        mo_ref[...] = m_
        vo_ref[...] = v_

    blk = pl.BlockSpec((tr, cols), lambda i: (i, 0))
    return _call(body, name, (rows // tr,), [blk] * 4, [blk] * 3, [_sds((rows, cols))] * 3, sem=("parallel",))(w, g, m, v)


SLAB_PART = 1024


def _pack(arrs):
    parts = []
    for a in arrs:
        flat = a.reshape(-1)
        pad = (-flat.shape[0]) % SLAB_PART
        if pad:
            flat = jnp.concatenate([flat, jnp.zeros((pad,), flat.dtype)])
        parts.append(flat.reshape(-1, 128))
    return jnp.concatenate(parts, axis=0)


def _unpack(slab, shapes):
    outs, row = [], 0
    for s in shapes:
        n = 1
        for d in s:
            n *= d
        rows = (n + SLAB_PART - 1) // SLAB_PART * (SLAB_PART // 128)
        outs.append(slab[row:row + rows].reshape(-1)[:n].reshape(s))
        row += rows
    return outs


def _unshard_last(g, lead):
    nl = len(lead)
    return jnp.transpose(g, tuple(range(1, nl + 1)) + (0, nl + 1)).reshape(tuple(lead) + (-1,))


def kernel(x, c, ctx, c_ctx, ada_w, ada_b, norm_g, w_in, hg_lb, hg_norm_g, rw_mu, rw_w0, rw_w2, rw_a0, rw_a2, rw_kk, rw_ka, rw_rk, rw_gn_g, rw_gn_b, w_hg_out, w_rw_out, w_out, final_g, loss_target, m_c_ctx, m_ada_w, m_ada_b, m_norm_g, m_w_in, m_hg_lb, m_hg_norm_g, m_rw_mu, m_rw_w0, m_rw_w2, m_rw_a0, m_rw_a2, m_rw_kk, m_rw_ka, m_rw_rk, m_rw_gn_g, m_rw_gn_b, m_w_hg_out, m_w_rw_out, m_w_out, m_final_g, v_c_ctx, v_ada_w, v_ada_b, v_norm_g, v_w_in, v_hg_lb, v_hg_norm_g, v_rw_mu, v_rw_w0, v_rw_w2, v_rw_a0, v_rw_a2, v_rw_kk, v_rw_ka, v_rw_rk, v_rw_gn_g, v_rw_gn_b, v_w_hg_out, v_w_rw_out, v_w_out, v_final_g):
    dm = _dims()
    me = 4 * lax.axis_index("x") + 2 * lax.axis_index("y") + lax.axis_index("c")

    big_shards = [w_in[0].T.astype(BF16), w_out[0].astype(BF16), w_hg_out[0].T.astype(BF16), w_rw_out[0].T.astype(BF16),
                  ada_w[0].T.astype(BF16)]
    sharded_small = [hg_lb, rw_mu[0], rw_w0[0], rw_w2[0], rw_a0[0], rw_a2[0]]
    gathered = _gather2(big_shards + [_pack(sharded_small)], "gather_weights")
    win_t, wout, whg_t, wrw_t, ada_t = [g.reshape(-1, g.shape[2]) for g in gathered[:5]]
    per_dev = jax.vmap(lambda s: _unpack(s, [a.shape for a in sharded_small]))(gathered[5])
    hg_lb_f, mu_f, w0_f, w2_f, a0_f, a2_f = [_unshard_last(p, p.shape[1:-1]) for p in per_dev]

    w = dict(win_t=win_t, wout=wout, whg_t=whg_t, wrw_t=wrw_t, ada_t=ada_t, c_ctx=c_ctx[None], ada_b=ada_b, norm_g=norm_g,
             lb0=hg_lb_f[:, 0], lb1=hg_lb_f[:, 1], hg_g=hg_norm_g, mu=mu_f, w0=w0_f, w2=w2_f, a0=a0_f, a2=a2_f,
             kk=rw_kk, ka=rw_ka, rk=rw_rk, gng=rw_gn_g, gnb=rw_gn_b, final_g=final_g[None])
    loss_dev, grad_x, big, small = _local_step(x[0], c, ctx[0], loss_target[0], w)
    loss = lax.psum(loss_dev[0, 0], AXES)

    small_names = ["c_ctx", "ada_b", "norm_g", "lb0", "lb1", "hg_g", "mu", "w0", "w2", "a0", "a2", "kk", "ka", "rk", "gng", "gnb", "final_g"]
    small_slab = _pack([small[n] for n in small_names])
    big_names = ["win_t", "wout", "whg_t", "wrw_t", "ada_t"]
    blocks = [big[n].reshape(NDEV, -1, big[n].shape[1]) for n in big_names]
    got = _pair_exchange(blocks, "pair_grads")
    core = lax.axis_index("c").astype(jnp.int32).reshape(1)
    chip_sums = [_pair_add(g, r, core, "pair_add_" + n) for g, r, n in zip(blocks, got, big_names)]
    recv = _chip_exchange(chip_sums, "chip_grads")
    small_all = _gather2([small_slab], "gather_small_grads")[0]
    my_chip = (2 * lax.axis_index("x") + lax.axis_index("y")).astype(jnp.int32)
    others = jnp.arange(3, dtype=jnp.int32)
    slots = jnp.concatenate([my_chip.reshape(1), others + (others >= my_chip).astype(jnp.int32)])
    g_win_t, g_wout, g_whg_t, g_wrw_t, g_ada_t = [_reduce_chips(cs, r, slots, "reduce_" + n)
                                                  for cs, r, n in zip(chip_sums, recv, big_names)]
    sg = dict(zip(small_names, _unpack(_reduce_sources(small_all, "reduce_small"), [small[n].shape for n in small_names])))

    def my_shard(full):
        n = full.shape[-1] // NDEV
        return lax.dynamic_slice_in_dim(full, me * n, n, axis=full.ndim - 1)

    grads = dict(
        c_ctx=sg["c_ctx"][0], ada_w=g_ada_t.T[None], ada_b=sg["ada_b"], norm_g=sg["norm_g"], w_in=g_win_t.T[None],
        hg_lb=my_shard(jnp.stack([sg["lb0"], sg["lb1"]], axis=1)), hg_norm_g=sg["hg_g"], rw_mu=my_shard(sg["mu"])[None],
        rw_w0=my_shard(sg["w0"])[None], rw_w2=my_shard(sg["w2"])[None], rw_a0=my_shard(sg["a0"])[None], rw_a2=my_shard(sg["a2"])[None],
        rw_kk=sg["kk"], rw_ka=sg["ka"], rw_rk=sg["rk"], rw_gn_g=sg["gng"], rw_gn_b=sg["gnb"],
        w_hg_out=g_whg_t.T[None], w_rw_out=g_wrw_t.T[None], w_out=g_wout[None], final_g=sg["final_g"][0])
    weights = dict(c_ctx=c_ctx, ada_w=ada_w, ada_b=ada_b, norm_g=norm_g, w_in=w_in, hg_lb=hg_lb, hg_norm_g=hg_norm_g, rw_mu=rw_mu,
                   rw_w0=rw_w0, rw_w2=rw_w2, rw_a0=rw_a0, rw_a2=rw_a2, rw_kk=rw_kk, rw_ka=rw_ka, rw_rk=rw_rk, rw_gn_g=rw_gn_g,
                   rw_gn_b=rw_gn_b, w_hg_out=w_hg_out, w_rw_out=w_rw_out, w_out=w_out, final_g=final_g)
    m_in = dict(zip(weights, (m_c_ctx, m_ada_w, m_ada_b, m_norm_g, m_w_in, m_hg_lb, m_hg_norm_g, m_rw_mu, m_rw_w0, m_rw_w2, m_rw_a0,
                              m_rw_a2, m_rw_kk, m_rw_ka, m_rw_rk, m_rw_gn_g, m_rw_gn_b, m_w_hg_out, m_w_rw_out, m_w_out, m_final_g)))
    v_in = dict(zip(weights, (v_c_ctx, v_ada_w, v_ada_b, v_norm_g, v_w_in, v_hg_lb, v_hg_norm_g, v_rw_mu, v_rw_w0, v_rw_w2, v_rw_a0,
                              v_rw_a2, v_rw_kk, v_rw_ka, v_rw_rk, v_rw_gn_g, v_rw_gn_b, v_w_hg_out, v_w_rw_out, v_w_out, v_final_g)))

    big_w = ["ada_w", "w_in", "w_hg_out", "w_rw_out", "w_out"]
    delta, new_m, new_v = {}, {}, {}
    for n in big_w:
        shp = weights[n].shape
        two = lambda a: a.reshape(shp[-2], shp[-1])
        d_, m_, v_ = _adamw(two(weights[n]), two(grads[n]), two(m_in[n]), two(v_in[n]), "adamw_" + n)
        delta[n], new_m[n], new_v[n] = d_.reshape(shp), m_.reshape(shp), v_.reshape(shp)
    rest = [n for n in weights if n not in big_w]
    shapes = [weights[n].shape for n in rest]
    d_s, m_s, v_s = _adamw(_pack([weights[n] for n in rest]), _pack([grads[n] for n in rest]), _pack([m_in[n] for n in rest]),
                           _pack([v_in[n] for n in rest]), "adamw_small")
    for n, d_, m_, v_ in zip(rest, _unpack(d_s, shapes), _unpack(m_s, shapes), _unpack(v_s, shapes)):
        delta[n], new_m[n], new_v[n] = d_, m_, v_

    order = list(weights)
    return (loss, grad_x[None], *[grads[n] for n in order], *[delta[n] for n in order],
            *[new_m[n] for n in order], *[new_v[n] for n in order])
```

```python
import functools

import jax
import jax.numpy as jnp
from jax import lax
from jax.experimental import pallas as pl
from jax.experimental.pallas import tpu as pltpu

D = 2048
T = 2048
TC = 256
GW = 64
C = 64
HGW = 1024
RWW = 1024
LORA = 64
NDEV = 8
TM = 256
TMV = 128
NORM_EPS = 1e-6
RW_GN_EPS = 64e-5
ADAM_LR, ADAM_B1, ADAM_B2, ADAM_EPS, ADAM_WD, ADAM_STEP = 0.001, 0.9, 0.999, 1e-08, 0.01, 10
WB = 256
V7X_VMEM_LIMIT = 56 * 1024 * 1024

F32 = jnp.float32
BF16 = jnp.bfloat16
AXES = ("x", "y", "c")


def _dims():
    tt = T + TC
    sh = 3 * RWW + 4 * LORA
    hgc = 5 * HGW
    return dict(TT=tt, NCC=TC // C, NCH=tt // C, HG_H=HGW // 128, RW_P=RWW // 128, SH=sh, HGC=hgc,
                NCOLS=hgc + sh + RWW + 2 * D, NCT=TC // TM, NCTV=TC // TMV)


def _mm(a, b):
    return _dot3(a, b, "nn")


def _mm_nt(a, b):
    return _dot3(a, b, "nt")


def _mm_tn(a, b):
    return _dot3(a, b, "tn")


def _split(x):
    hi = x.astype(BF16)
    return hi, (x - hi.astype(F32)).astype(BF16)


_FORMS = {"nn": (((1,), (0,)), ((), ())), "nt": (((1,), (1,)), ((), ())), "tn": (((0,), (0,)), ((), ()))}


def _dot3_raw(a, b, form):
    ah, al = _split(a)
    bh, bl = _split(b)
    d = lambda x, y: lax.dot_general(x, y, _FORMS[form], preferred_element_type=F32)
    return d(ah, bh) + (d(ah, bl) + d(al, bh))


@functools.partial(jax.custom_vjp, nondiff_argnums=(2,))
def _dot3(a, b, form):
    return _dot3_raw(a, b, form)


def _dot3_fwd(a, b, form):
    return _dot3_raw(a, b, form), (a, b)


def _dot3_bwd(form, res, g):
    a, b = res
    if form == "nn":
        return _dot3(g, b, "nt"), _dot3(a, g, "tn")
    if form == "nt":
        return _dot3(g, b, "nn"), _dot3(g, a, "tn")
    return _dot3(b, g, "nt"), _dot3(a, g, "nn")


_dot3.defvjp(_dot3_fwd, _dot3_bwd)


def _scan_cumsum(inc, x):
    return _cumsum_vjp(inc.astype(BF16), x)


def _cumsum_raw(inc, x, form):
    h1 = x.astype(BF16)
    r1 = x - h1.astype(F32)
    h2 = r1.astype(BF16)
    h3 = (r1 - h2.astype(F32)).astype(BF16)
    d = lambda y: lax.dot_general(inc, y, _FORMS[form], preferred_element_type=F32)
    return d(h1) + (d(h2) + d(h3))


@jax.custom_vjp
def _cumsum_vjp(inc, x):
    return _cumsum_raw(inc, x, "nn")


_cumsum_vjp.defvjp(lambda inc, x: (_cumsum_raw(inc, x, "nn"), inc),
                   lambda inc, g: (jnp.zeros_like(inc), _cumsum_raw(inc, g, "tn")))


def _silu(x):
    return x * jax.nn.sigmoid(x)


def _softplus(x):
    return jnp.maximum(x, 0.0) + jnp.log(1.0 + jnp.exp(-jnp.abs(x)))


def _iota2(shape, dim):
    return lax.broadcasted_iota(jnp.int32, shape, dim)


def _pair_mask():
    return (_iota2((128, 128), 0) < 64) == (_iota2((128, 128), 1) < 64)


def _seg64_sum(x):
    e = _pair_mask().astype(F32)
    parts = [_mm(x[:, g * 128:(g + 1) * 128], e) for g in range(x.shape[1] // 128)]
    return parts[0] if len(parts) == 1 else jnp.concatenate(parts, axis=1)


def _seg128_mean(x):
    parts = [jnp.broadcast_to(jnp.mean(x[:, g * 128:(g + 1) * 128], axis=1, keepdims=True), (x.shape[0], 128))
             for g in range(x.shape[1] // 128)]
    return parts[0] if len(parts) == 1 else jnp.concatenate(parts, axis=1)


def _scan_masks(d):
    lag = (_iota2((C, C), 0) - _iota2((C, C), 1)) * (1 - 2 * d)
    return lag >= 0, lag > 0


def _tri_inverse(a):
    eye = (_iota2((C, C), 0) == _iota2((C, C), 1)).astype(F32)
    p = [-x for x in a]
    tm = [eye + x for x in p]
    n = 2
    while n < C:
        p = [_mm(x, x) for x in p]
        tm = [t + _mm(t, x) for t, x in zip(tm, p)]
        n *= 2
    return tm


@jax.custom_vjp
def _tri_solve(a, rhs):
    return [_mm(t, r) for t, r in zip(_tri_inverse(a), rhs)]


def _tri_solve_fwd(a, rhs):
    tm = _tri_inverse(a)
    u = [_mm(t, r) for t, r in zip(tm, rhs)]
    return u, (tm, u)


def _tri_solve_bwd(res, g):
    tm, u = res
    d_rhs = [_mm_tn(t, x) for t, x in zip(tm, g)]
    return [-_mm_nt(d, x) for d, x in zip(d_rhs, u)], d_rhs


_tri_solve.defvjp(_tri_solve_fwd, _tri_solve_bwd)


def _rw_chunks(chains):
    s0, r, lw, kap, b, v, kd, ds = zip(*chains)
    ids = range(len(chains))
    inc, strict = zip(*[_scan_masks(d) for d in ds])
    lane = _iota2((1, 128), 1)
    hm = ((lane < 64).astype(F32), (lane >= 64).astype(F32))
    eye = (_iota2((C, C), 0) == _iota2((C, C), 1)).astype(F32)
    lc = [_scan_cumsum(inc[i], lw[i]) for i in ids]
    ltot = [jnp.sum(lw[i], axis=0, keepdims=True) for i in ids]
    rt = [r[i] * jnp.exp(lc[i]) for i in ids]
    kt = [kap[i] * jnp.exp(lc[i] - lw[i]) for i in ids]
    einv = [jnp.exp(-lc[i]) for i in ids]
    kh = [kd[i] * einv[i] for i in ids]
    bh = [b[i] * einv[i] for i in ids]
    eend = [jnp.exp(ltot[i] - lc[i]) for i in ids]
    kbar = [kd[i] * eend[i] for i in ids]
    bbar = [b[i] * eend[i] for i in ids]
    lhs = [jnp.concatenate([kt[i] * hm[0], kt[i] * hm[1], rt[i] * hm[0], rt[i] * hm[1]], axis=0) for i in ids]
    gk = [_mm_nt(lhs[i], kh[i]) for i in ids]
    gb = [_mm_nt(lhs[i], bh[i]) for i in ids]
    ks = [_mm_nt(kt[i], s0[i]) for i in ids]
    ys = [_mm_nt(rt[i], s0[i]) for i in ids]
    heads = [(i, h) for i in ids for h in range(2)]
    akk = {c: jnp.where(strict[c[0]], gk[c[0]][c[1] * C:(c[1] + 1) * C], 0.0) for c in heads}
    akb = {c: jnp.where(strict[c[0]], gb[c[0]][c[1] * C:(c[1] + 1) * C], 0.0) for c in heads}
    ark = {c: jnp.where(inc[c[0]], gk[c[0]][(2 + c[1]) * C:(3 + c[1]) * C], 0.0) for c in heads}
    arb = {c: jnp.where(inc[c[0]], gb[c[0]][(2 + c[1]) * C:(3 + c[1]) * C], 0.0) for c in heads}
    vh = {c: v[c[0]] * hm[c[1]] for c in heads}
    av = {c: _mm(akk[c], vh[c]) for c in heads}
    rhs = {c: ks[c[0]] * hm[c[1]] + av[c] for c in heads}
    uh = dict(zip(heads, _tri_solve([akb[c] for c in heads], [rhs[c] for c in heads])))
    yv = {c: _mm(ark[c], vh[c]) for c in heads}
    yu = {c: _mm(arb[c], uh[c]) for c in heads}
    u = [uh[i, 0] + uh[i, 1] for i in ids]
    y = [ys[i] + (yv[i, 0] - yu[i, 0]) + (yv[i, 1] - yu[i, 1]) for i in ids]
    upd = [_mm_tn(jnp.concatenate([v[i], -u[i]], axis=0), jnp.concatenate([kbar[i], bbar[i]], axis=0)) for i in ids]
    s1 = [s0[i] * jnp.exp(ltot[i]) + jnp.where(_pair_mask(), upd[i], 0.0) for i in ids]
    return [(y[i], s1[i]) for i in ids]


HG_SUB = 16


def _hg_chunks(chains):
    s0, qr, iv, f, lb, ds = zip(*chains)
    ids = range(len(chains))
    inc = [_scan_masks(d)[0] for d in ds]
    q = [_silu(qr[i]) for i in ids]
    fg = [lb[i] + (1.0 - lb[i]) * jax.nn.sigmoid(f[i]) for i in ids]
    k = [1.0 - fg[i] for i in ids]
    g = [jnp.log(fg[i]) for i in ids]
    bc = [_scan_cumsum(inc[i], g[i]) for i in ids]
    btot = [jnp.sum(g[i], axis=0, keepdims=True) for i in ids]
    o_inter = [_mm_nt(q[i] * jnp.exp(bc[i]), s0[i]) for i in ids]
    rowi = _iota2((C, 1), 0)
    outs = [[] for _ in ids]
    for blk in range(C // HG_SUB):
        lo, hi = blk * HG_SUB, (blk + 1) * HG_SUB
        first = [lo if ds[i] == 0 else hi - 1 for i in ids]
        ref = [jnp.sum(jnp.where(rowi == first[i], bc[i] - g[i], 0.0), axis=0, keepdims=True) for i in ids]
        qi = [q[i][lo:hi] * jnp.exp(bc[i][lo:hi] - ref[i]) for i in ids]
        src = [(rowi < hi) if ds[i] == 0 else (rowi >= lo) for i in ids]
        ke = [jnp.where(src[i], k[i] * jnp.exp(jnp.where(src[i], ref[i] - bc[i], 0.0)), 0.0) for i in ids]
        a = [jnp.where(inc[i][lo:hi], _mm_nt(qi[i], ke[i]), 0.0) for i in ids]
        part = [_mm(a[i], iv[i]) for i in ids]
        for i in ids:
            outs[i].append(part[i])
    o = [o_inter[i] + jnp.concatenate(outs[i], axis=0) for i in ids]
    upd = [_mm_tn(iv[i], k[i] * jnp.exp(btot[i] - bc[i])) for i in ids]
    s1 = [s0[i] * jnp.exp(btot[i]) + upd[i] for i in ids]
    return [(o[i], s1[i]) for i in ids]


def _lockstep(fn, ds, nargs):
    def flat_fn(*flat):
        return tuple(fn([tuple(flat[i * nargs:(i + 1) * nargs]) + (d,) for i, d in enumerate(ds)]))

    return flat_fn


def _h_math(x, ng, scale, shift):
    return x * lax.rsqrt(jnp.mean(x * x, axis=-1, keepdims=True) + NORM_EPS) * ng * (1.0 + scale) + shift


def _hg_post_math(o0, o1, z, g):
    o = o0 + o1
    on = o * lax.rsqrt(_seg128_mean(o * o) + NORM_EPS)
    return on * g * _silu(z)


def _rw_prep_math(k, pw0, pw1, pa0, pa1, w0, a0, kkp, kap_p):
    kk = k * kkp
    kap = kk * lax.rsqrt(_seg64_sum(kk * kk) + 1e-12)
    outs = []
    for d, (pw, pa) in enumerate(((pw0, pa0), (pw1, pa1))):
        w_log = -_softplus(-(w0[d:d + 1] + pw)) - 0.5
        lw = -jnp.exp(w_log)
        a = jax.nn.sigmoid(a0[d:d + 1] + pa)
        kd = k * (1.0 + (a - 1.0) * kap_p)
        outs += [lw, kap * a, kd]
    return (kap, *outs)


def _rw_post_math(y0, y1, kd0, kd1, r, v, z, rk, gng, gnb):
    ys, ksum = y0 + y1, kd0 + kd1
    mean = _seg64_sum(ys) * (1.0 / 64.0)
    cen = ys - mean
    var = _seg64_sum(cen * cen) * (1.0 / 64.0)
    yn = cen * lax.rsqrt(var + RW_GN_EPS) * gng + gnb
    bonus = _seg64_sum(r * ksum * rk) * v
    return (yn + bonus) * _silu(z)


def _head_math(x, mo, gate, fg, tgt):
    xo = x + gate * mo
    y = xo * lax.rsqrt(jnp.mean(xo * xo, axis=-1, keepdims=True) + NORM_EPS) * fg
    err = y - tgt
    return 0.5 * jnp.sum(jnp.mean(err * err, axis=-1, keepdims=True), axis=0, keepdims=True)


def _gate_math(ghg, grw, a, b):
    return jax.nn.sigmoid(ghg) * a + jax.nn.sigmoid(grw) * b


def _call(body, name, grid, in_specs, out_specs, out_shape, scratch=(), sem=None, aliases=None):
    return pl.pallas_call(
        body, name=name, grid=grid, in_specs=in_specs, out_specs=out_specs, out_shape=out_shape,
        scratch_shapes=list(scratch), input_output_aliases=aliases or {},
        compiler_params=pltpu.CompilerParams(dimension_semantics=sem, vmem_limit_bytes=V7X_VMEM_LIMIT))


def _full(shape):
    nd = len(shape)
    return pl.BlockSpec(shape, lambda *_: (0,) * nd)


def _sds(shape, dtype=F32):
    return jax.ShapeDtypeStruct(shape, dtype)


def _proj_nt(h, wt, wt_blk0, nblk, row0, nrows, name):
    def body(h_ref, w_ref, o_ref):
        o_ref[...] = lax.dot_general(h_ref[row0:row0 + nrows, :], w_ref[...], (((1,), (1,)), ((), ())), preferred_element_type=F32)

    return _call(body, name, (nblk,), [_full(h.shape), pl.BlockSpec((WB, wt.shape[1]), lambda j: (wt_blk0 + j, 0))],
                 pl.BlockSpec((nrows, WB), lambda j: (0, j)), _sds((nrows, nblk * WB)), sem=("parallel",))(h, wt)


def _dw_tn(a, b, row0, out_rows, out_blk0, name, prev=None):
    m, n = a.shape
    k2 = b.shape[1]
    nblk = n // WB

    def body(*refs):
        a_ref, b_ref, o_ref = refs[0], refs[1], refs[-1]
        o_ref[...] = lax.dot_general(a_ref[...], b_ref[row0:row0 + m, :], (((0,), (0,)), ((), ())),
                                     preferred_element_type=F32).astype(BF16)

    in_specs = [pl.BlockSpec((m, WB), lambda j: (0, j)), _full(b.shape)]
    args = [a, b]
    aliases = None
    if prev is not None:
        in_specs.append(pl.BlockSpec(memory_space=pl.ANY))
        args.append(prev)
        aliases = {2: 0}
    return _call(body, name, (nblk,), in_specs, pl.BlockSpec((WB, k2), lambda j: (out_blk0 + j, 0)), _sds((out_rows, k2), BF16),
                 sem=("arbitrary",), aliases=aliases)(*args)


def _mm_nn(a, w, w_blk0, name):
    kb = WB
    m, kc = a.shape
    n = w.shape[1]
    tmb = 3 * TM if m % (3 * TM) == 0 else (2 * TM if m % (2 * TM) == 0 else TM)

    def body(a_ref, w_ref, o_ref):
        @pl.when(pl.program_id(1) == 0)
        def _():
            o_ref[...] = jnp.zeros_like(o_ref)

        o_ref[...] += jnp.dot(a_ref[...], w_ref[...], preferred_element_type=F32)

    return _call(body, name, (m // tmb, kc // kb),
                 [pl.BlockSpec((tmb, kb), lambda i, k: (i, k)), pl.BlockSpec((kb, n), lambda i, k: (w_blk0 + k, 0))],
                 pl.BlockSpec((tmb, n), lambda i, k: (i, 0)), _sds((m, n)), sem=("parallel", "arbitrary"))(a, w)


def _mods_fwd(cc, ada_t, ada_b):
    tn = 512 if (3 * D) % 512 == 0 else 256

    def body(cc_ref, w_ref, b_ref, o_ref):
        s = _silu(cc_ref[...]).astype(BF16)
        o_ref[...] = lax.dot_general(s, w_ref[...], (((1,), (1,)), ((), ())), preferred_element_type=F32) + b_ref[...]

    return _call(body, "mods_fwd", (3 * D // tn,),
                 [_full((8, D)), pl.BlockSpec((tn, D), lambda j: (j, 0)), pl.BlockSpec((1, tn), lambda j: (0, j))],
                 pl.BlockSpec((8, tn), lambda j: (0, j)), _sds((8, 3 * D)), sem=("parallel",))(cc, ada_t, ada_b)


def _sel_mod(mods_ref, is_ctx, lo):
    return jnp.where(is_ctx, mods_ref[1:2, lo:lo + D], mods_ref[0:1, lo:lo + D])


def _h_fwd(xcat, mods, norm_g):
    dm = _dims()

    def body(x_ref, m_ref, g_ref, o_ref):
        is_ctx = pl.program_id(0) < dm["NCT"]
        o_ref[...] = _h_math(x_ref[...], g_ref[...], _sel_mod(m_ref, is_ctx, D), _sel_mod(m_ref, is_ctx, 0)).astype(BF16)

    return _call(body, "h_fwd", (dm["TT"] // TM,),
                 [pl.BlockSpec((TM, D), lambda i: (i, 0)), _full((8, 3 * D)), _full((1, D))],
                 pl.BlockSpec((TM, D), lambda i: (i, 0)), _sds((dm["TT"], D), BF16), sem=("parallel",))(xcat, mods, norm_g)


def _chunk_of(d, s):
    dm = _dims()
    ncc, nch = dm["NCC"], dm["NCH"]
    return s if d == 0 else jnp.where(s < ncc, ncc - 1 - s, nch - 1 - (s - ncc))


HG_GROUP = 4
RW_GROUP = 2


def _dir_specs(make):
    return [make(d) for d in range(2)]


def _lanes(j):
    return slice(j * 128, (j + 1) * 128)


def _hg_scan_fwd(p_hg, lb3):
    dm = _dims()
    nh, nch, tt = dm["HG_H"], dm["NCH"], dm["TT"]
    pp = min(HG_GROUP, nh)
    lw_ = 128 * pp

    def body(q0, q1, i0, i1, f0, f1, lb0, lb1, o0, o1, ck_ref, s_ref):
        @pl.when(pl.program_id(1) == 0)
        def _():
            s_ref[...] = jnp.zeros_like(s_ref)

        keys, chains = [], []
        for d, refs in enumerate(((q0, i0, f0, lb0), (q1, i1, f1, lb1))):
            vals = [r[...] for r in refs]
            for j in range(pp):
                keys.append((d, j))
                chains.append((s_ref[d, j], *[a[:, _lanes(j)] for a in vals], d))
        for (d, j), chain, (out, s1) in zip(keys, chains, _hg_chunks(chains)):
            ck_ref[d, j] = chain[0]
            (o0, o1)[d][:, _lanes(j)] = out
            s_ref[d, j] = s1

    blk = lambda off: _dir_specs(lambda d: pl.BlockSpec((C, lw_), lambda h, s: (_chunk_of(d, s), off * nh // pp + h)))
    return _call(body, "hg_scan_fwd", (nh // pp, nch),
                 blk(0) + blk(1) + _dir_specs(lambda d: pl.BlockSpec((C, lw_), lambda h, s: (_chunk_of(d, s), (2 + d) * nh // pp + h)))
                 + _dir_specs(lambda d: pl.BlockSpec((None, 1, lw_), lambda h, s: (d, 0, h))),
                 _dir_specs(lambda d: pl.BlockSpec((C, lw_), lambda h, s: (_chunk_of(d, s), h)))
                 + [pl.BlockSpec((2, pp, None, 128, 128), lambda h, s: (0, h, s, 0, 0))],
                 [_sds((tt, HGW)), _sds((tt, HGW)), _sds((2, nh, nch, 128, 128))],
                 scratch=[pltpu.VMEM((2, pp, 128, 128), F32)], sem=("parallel", "arbitrary"))(*([p_hg] * 6), lb3, lb3)


def _hg_scan_bwd(p_hg, lb3, ck, do):
    dm = _dims()
    nh, nch, tt = dm["HG_H"], dm["NCH"], dm["TT"]
    pp = min(HG_GROUP, nh)
    lw_ = 128 * pp

    def body(q0, q1, i0, i1, f0, f1, lb0, lb1, ck_ref, do0, do1, dq0, dq1, di0, di1, df0, df1, dlb_ref, ds_ref):
        @pl.when(pl.program_id(1) == 0)
        def _():
            ds_ref[...] = jnp.zeros_like(ds_ref)
            dlb_ref[...] = jnp.zeros_like(dlb_ref)

        keys, flat, cts = [], [], []
        for d, refs in enumerate(((q0, i0, f0, lb0, do0), (q1, i1, f1, lb1, do1))):
            *vals, do_ = [r[...] for r in refs]
            for j in range(pp):
                keys.append((d, j))
                flat += [ck_ref[d, j], *[a[:, _lanes(j)] for a in vals]]
                cts.append((do_[:, _lanes(j)], ds_ref[d, j]))
        _, vjp = jax.vjp(_lockstep(_hg_chunks, [d for d, _ in keys], 5), *flat)
        grads = vjp(tuple(cts))
        for n, (d, j) in enumerate(keys):
            ds0, dq_, di_, df_, dlb = grads[5 * n:5 * n + 5]
            (dq0, dq1)[d][:, _lanes(j)] = dq_
            (di0, di1)[d][:, _lanes(j)] = di_
            (df0, df1)[d][:, _lanes(j)] = df_
            dlb_ref[d, :, _lanes(j)] += dlb
            ds_ref[d, j] = ds0

    ch = lambda d, s: _chunk_of(d, nch - 1 - s)
    blk = lambda off: _dir_specs(lambda d: pl.BlockSpec((C, lw_), lambda h, s: (ch(d, s), off * nh // pp + h)))
    tok = _dir_specs(lambda d: pl.BlockSpec((C, lw_), lambda h, s: (ch(d, s), h)))
    return _call(body, "hg_scan_bwd", (nh // pp, nch),
                 blk(0) + blk(1) + _dir_specs(lambda d: pl.BlockSpec((C, lw_), lambda h, s: (ch(d, s), (2 + d) * nh // pp + h)))
                 + _dir_specs(lambda d: pl.BlockSpec((None, 1, lw_), lambda h, s: (d, 0, h)))
                 + [pl.BlockSpec((2, pp, None, 128, 128), lambda h, s: (0, h, nch - 1 - s, 0, 0))] + tok,
                 tok * 3 + [pl.BlockSpec((2, 1, lw_), lambda h, s: (0, 0, h))],
                 [_sds((tt, HGW))] * 6 + [_sds((2, 1, HGW))],
                 scratch=[pltpu.VMEM((2, pp, 128, 128), F32)], sem=("parallel", "arbitrary"))(*([p_hg] * 6), lb3, lb3, ck, do, do)


def _rw_scan_fwd(sh, kap, lw, b, kd):
    dm = _dims()
    npair, nch, tt = dm["RW_P"], dm["NCH"], dm["TT"]
    pp = min(RW_GROUP, npair)
    lw_ = 128 * pp

    def body(r0, r1, v0, v1, ka0, ka1, lw0, lw1, b0, b1, kd0, kd1, y0, y1, ck_ref, s_ref):
        @pl.when(pl.program_id(1) == 0)
        def _():
            s_ref[...] = jnp.zeros_like(s_ref)

        keys, chains = [], []
        for d, refs in enumerate(((r0, lw0, ka0, b0, v0, kd0), (r1, lw1, ka1, b1, v1, kd1))):
            vals = [r[...] for r in refs]
            for j in range(pp):
                keys.append((d, j))
                chains.append((s_ref[d, j], *[a[:, _lanes(j)] for a in vals], d))
        for (d, j), chain, (out, s1) in zip(keys, chains, _rw_chunks(chains)):
            ck_ref[d, j] = chain[0]
            (y0, y1)[d][:, _lanes(j)] = out
            s_ref[d, j] = s1

    two = lambda off: _dir_specs(lambda d: pl.BlockSpec((C, lw_), lambda p, s: (_chunk_of(d, s), off * npair // pp + p)))
    three = _dir_specs(lambda d: pl.BlockSpec((None, C, lw_), lambda p, s: (d, _chunk_of(d, s), p)))
    return _call(body, "rw_scan_fwd", (npair // pp, nch), two(0) + two(2) + two(0) + three * 3,
                 two(0) + [pl.BlockSpec((2, pp, None, 128, 128), lambda p, s: (0, p, s, 0, 0))],
                 [_sds((tt, RWW)), _sds((tt, RWW)), _sds((2, npair, nch, 128, 128))],
                 scratch=[pltpu.VMEM((2, pp, 128, 128), F32)], sem=("parallel", "arbitrary"))(
        sh, sh, sh, sh, kap, kap, lw, lw, b, b, kd, kd)


def _rw_scan_bwd(sh, kap, lw, b, kd, ck, dy):
    dm = _dims()
    npair, nch, tt = dm["RW_P"], dm["NCH"], dm["TT"]
    pp = min(RW_GROUP, npair)
    lw_ = 128 * pp

    def body(r0, r1, v0, v1, ka0, ka1, lw0, lw1, b0, b1, kd0, kd1, ck_ref, dy0, dy1,
             dr0, dr1, dv0, dv1, dka0, dka1, dlw0, dlw1, db0, db1, dkd0, dkd1, ds_ref):
        @pl.when(pl.program_id(1) == 0)
        def _():
            ds_ref[...] = jnp.zeros_like(ds_ref)

        keys, flat, cts = [], [], []
        for d, refs in enumerate(((r0, lw0, ka0, b0, v0, kd0, dy0), (r1, lw1, ka1, b1, v1, kd1, dy1))):
            *vals, dy_ = [r[...] for r in refs]
            for j in range(pp):
                keys.append((d, j))
                flat += [ck_ref[d, j], *[a[:, _lanes(j)] for a in vals]]
                cts.append((dy_[:, _lanes(j)], ds_ref[d, j]))
        _, vjp = jax.vjp(_lockstep(_rw_chunks, [d for d, _ in keys], 7), *flat)
        grads = vjp(tuple(cts))
        outs = ((dr0, dlw0, dka0, db0, dv0, dkd0), (dr1, dlw1, dka1, db1, dv1, dkd1))
        for n, (d, j) in enumerate(keys):
            ds0, *g_in = grads[7 * n:7 * n + 7]
            for o_ref, g in zip(outs[d], g_in):
                o_ref[:, _lanes(j)] = g
            ds_ref[d, j] = ds0

    ch = lambda d, s: _chunk_of(d, nch - 1 - s)
    two = lambda off: _dir_specs(lambda d: pl.BlockSpec((C, lw_), lambda p, s: (ch(d, s), off * npair // pp + p)))
    three = _dir_specs(lambda d: pl.BlockSpec((None, C, lw_), lambda p, s: (d, ch(d, s), p)))
    return _call(body, "rw_scan_bwd", (npair // pp, nch),
                 two(0) + two(2) + two(0) + three * 3
                 + [pl.BlockSpec((2, pp, None, 128, 128), lambda p, s: (0, p, nch - 1 - s, 0, 0))] + two(0),
                 two(0) * 6, [_sds((tt, RWW))] * 12,
                 scratch=[pltpu.VMEM((2, pp, 128, 128), F32)], sem=("parallel", "arbitrary"))(
        sh, sh, sh, sh, kap, kap, lw, lw, b, b, kd, kd, ck, dy, dy)


def _tile_pos(i):
    dm = _dims()
    is_ctx = i < dm["NCT"]
    rows = lax.broadcasted_iota(jnp.int32, (TM, 1), 0)
    tl = rows + (i - dm["NCT"]) * TM
    width = jnp.where(is_ctx, TC, GW)
    assert TC & (TC - 1) == 0 and GW & (GW - 1) == 0 and TM % GW == 0
    colp = rows & (width - 1)
    return is_ctx, rows, tl, colp, width


def _shift_lr(cur, i):
    _, _, _, colp, width = _tile_pos(i)
    left = jnp.where(colp == 0, 0.0, pltpu.roll(cur, 1, 0))
    right = jnp.where(colp == width - 1, 0.0, pltpu.roll(cur, TM - 1, 0))
    return left, right


def _shift_ud(cur, prv, nxt, i):
    is_ctx, rows, tl, _, _ = _tile_pos(i)
    if TM > GW:
        up = jnp.where(rows >= GW, pltpu.roll(cur, GW, 0), pltpu.roll(prv, GW, 0))
        down = jnp.where(rows < TM - GW, pltpu.roll(cur, TM - GW, 0), pltpu.roll(nxt, TM - GW, 0))
    else:
        up, down = prv, nxt
    up = jnp.where(jnp.logical_or(is_ctx, tl < GW), 0.0, up)
    down = jnp.where(jnp.logical_or(is_ctx, tl >= T - GW), 0.0, down)
    return up, down


def _shift_specs():
    dm = _dims()
    nt = dm["TT"] // TM
    cw = max(w for w in range(128, 1664 + 1, 128) if dm["SH"] % w == 0)
    cur = pl.BlockSpec((TM, cw), lambda j, i: (i, j))
    prv = pl.BlockSpec((TM, cw), lambda j, i: (jnp.maximum(i - 1, 0), j))
    nxt = pl.BlockSpec((TM, cw), lambda j, i: (jnp.minimum(i + 1, nt - 1), j))
    mu = pl.BlockSpec((4, cw), lambda j, i: (0, j))
    return nt, cw, cur, prv, nxt, mu


def _shift_fwd(p_rw, mu):
    dm = _dims()
    nt, cw, cur, prv, nxt, mus = _shift_specs()

    def body(c_ref, p_ref, n_ref, mu_ref, o_ref):
        i = pl.program_id(1)
        p, m = c_ref[...], mu_ref[...]
        left, right = _shift_lr(p, i)
        up, down = _shift_ud(p, p_ref[...], n_ref[...], i)
        vert = jnp.where(i < dm["NCT"], 0.0, 1.0)
        o_ref[...] = (p * (1.0 - m[0:1] - m[1:2] - vert * (m[2:3] + m[3:4]))
                      + m[0:1] * left + m[1:2] * right + m[2:3] * up + m[3:4] * down)

    return _call(body, "shift_fwd", (dm["SH"] // cw, nt), [cur, prv, nxt, mus], cur, _sds((dm["TT"], dm["SH"])),
                 sem=("parallel", "parallel"))(p_rw, p_rw, p_rw, mu)


def _shift_bwd(p_rw, dsh, mu):
    dm = _dims()
    nt, cw, cur, prv, nxt, mus = _shift_specs()

    def body(c_ref, p_ref, n_ref, gc_ref, gp_ref, gn_ref, mu_ref, dp_ref, dmu_ref):
        i = pl.program_id(1)
        p, g, m = c_ref[...], gc_ref[...], mu_ref[...]
        vert = jnp.where(i < dm["NCT"], 0.0, 1.0)
        _, from_right = _shift_lr(m[0:1] * g, i)
        from_left, _ = _shift_lr(m[1:2] * g, i)
        _, from_down = _shift_ud(m[2:3] * g, g, m[2:3] * gn_ref[...], i)
        from_up, _ = _shift_ud(m[3:4] * g, m[3:4] * gp_ref[...], g, i)
        dp = g * (1.0 - m[0:1] - m[1:2] - vert * (m[2:3] + m[3:4])) + from_right + from_left + from_down + from_up
        dp_ref[...] = dp.astype(BF16)

        @pl.when(i == 0)
        def _():
            dmu_ref[...] = jnp.zeros_like(dmu_ref)

        left, right = _shift_lr(p, i)
        up, down = _shift_ud(p, p_ref[...], n_ref[...], i)
        s = lambda a: jnp.sum(a, axis=0, keepdims=True)
        dmu_ref[...] += jnp.concatenate([s(g * (left - p)), s(g * (right - p)), vert * s(g * (up - p)), vert * s(g * (down - p))], axis=0)

    return _call(body, "shift_bwd", (dm["SH"] // cw, nt), [cur, prv, nxt, cur, prv, nxt, mus],
                 [cur, pl.BlockSpec((4, cw), lambda j, i: (0, j))], [_sds((dm["TT"], dm["SH"]), BF16), _sds((4, dm["SH"]))],
                 sem=("parallel", "arbitrary"))(p_rw, p_rw, p_rw, dsh, dsh, dsh, mu)


def _acc_out(ref, first, vals):
    @pl.when(first)
    def _():
        for r in ref:
            r[...] = jnp.zeros_like(r)

    for r, v in zip(ref, vals):
        r[...] += v


def _rowsum(a):
    return jnp.sum(a, axis=0, keepdims=True)


def _rw_prep_specs():
    dm = _dims()
    rw = RWW // 128
    tok = lambda width, cb: pl.BlockSpec((TMV, width), lambda i: (i, cb))
    ins = [pl.BlockSpec((TMV, RWW), lambda i: (i, 1)), tok(128, 3 * rw), tok(128, 3 * rw + 1),
           _full((2, RWW)), _full((128, RWW)), _full((128, RWW)), _full((2, RWW)), _full((128, RWW)), _full((128, RWW)),
           _full((1, RWW)), _full((1, RWW))]
    return dm, ins


def _rw_prep_fwd(sh, w0, w2p0, w2p1, a0, a2p0, a2p1, kkp, kap_p):
    dm, ins = _rw_prep_specs()
    tt = dm["TT"]

    def body(k_ref, low_ref, loa_ref, w0_ref, w20_ref, w21_ref, a0_ref, a20_ref, a21_ref, kk_ref, ka_ref, kap_ref, lw_ref, b_ref, kd_ref):
        tw, la = jnp.tanh(low_ref[...]), loa_ref[...]
        kap, lw0, b0, kd0, lw1, b1, kd1 = _rw_prep_math(
            k_ref[...], _mm(tw, w20_ref[...]), _mm(tw, w21_ref[...]), _mm(la, a20_ref[...]), _mm(la, a21_ref[...]),
            w0_ref[...], a0_ref[...], kk_ref[...], ka_ref[...])
        kap_ref[...] = kap
        lw_ref[0], lw_ref[1] = lw0, lw1
        b_ref[0], b_ref[1] = b0, b1
        kd_ref[0], kd_ref[1] = kd0, kd1

    two = pl.BlockSpec((2, TMV, RWW), lambda i: (0, i, 0))
    return _call(body, "rw_prep_fwd", (tt // TMV,), ins, [pl.BlockSpec((TMV, RWW), lambda i: (i, 0)), two, two, two],
                 [_sds((tt, RWW)), _sds((2, tt, RWW)), _sds((2, tt, RWW)), _sds((2, tt, RWW))], sem=("parallel",))(
        sh, sh, sh, w0, w2p0, w2p1, a0, a2p0, a2p1, kkp, kap_p)


def _rw_prep_bwd(sh, w0, w2p0, w2p1, a0, a2p0, a2p1, kkp, kap_p, scan_grads, dkd_p, dr_p, dv_p):
    dm, ins = _rw_prep_specs()
    tt, sh_w = dm["TT"], dm["SH"]
    one = pl.BlockSpec((TMV, RWW), lambda i: (i, 0))

    def body(k_ref, low_ref, loa_ref, w0_ref, w20_ref, w21_ref, a0_ref, a20_ref, a21_ref, kk_ref, ka_ref,
             dr0, dr1, dv0, dv1, dka0, dka1, dlw0, dlw1, db0, db1, dkd0, dkd1, dkdp_ref, drp_ref, dvp_ref,
             dsh_ref, dw0_ref, dw20_ref, dw21_ref, da0_ref, da20_ref, da21_ref, dkk_ref, dka_ref):
        tw, la = jnp.tanh(low_ref[...]), loa_ref[...]
        w20, w21, a20, a21 = w20_ref[...], w21_ref[...], a20_ref[...], a21_ref[...]
        _, vjp = jax.vjp(_rw_prep_math, k_ref[...], _mm(tw, w20), _mm(tw, w21), _mm(la, a20), _mm(la, a21),
                         w0_ref[...], a0_ref[...], kk_ref[...], ka_ref[...])
        dkp = dkdp_ref[...]
        dk, dpw0, dpw1, dpa0, dpa1, dw0, da0, dkk, dka = vjp(
            (dka0[...] + dka1[...], dlw0[...], db0[...], dkd0[...] + dkp, dlw1[...], db1[...], dkd1[...] + dkp))
        twt, lat = tw.T, la.T
        dtw = _mm_nt(dpw0, w20) + _mm_nt(dpw1, w21)
        dsh_ref[:, 0:RWW] = dr0[...] + dr1[...] + drp_ref[...]
        dsh_ref[:, RWW:2 * RWW] = dk
        dsh_ref[:, 2 * RWW:3 * RWW] = dv0[...] + dv1[...] + dvp_ref[...]
        dsh_ref[:, 3 * RWW:3 * RWW + 128] = dtw * (1.0 - tw * tw)
        dsh_ref[:, 3 * RWW + 128:3 * RWW + 256] = _mm_nt(dpa0, a20) + _mm_nt(dpa1, a21)
        _acc_out((dw0_ref, dw20_ref, dw21_ref, da0_ref, da20_ref, da21_ref, dkk_ref, dka_ref), pl.program_id(0) == 0,
                 (dw0, _mm(twt, dpw0), _mm(twt, dpw1), da0, _mm(lat, dpa0), _mm(lat, dpa1), dkk, dka))

    par = [_full((2, RWW)), _full((128, RWW)), _full((128, RWW)), _full((2, RWW)), _full((128, RWW)), _full((128, RWW)),
           _full((1, RWW)), _full((1, RWW))]
    return _call(body, "rw_prep_bwd", (tt // TMV,), ins + [one] * 15,
                 [pl.BlockSpec((TMV, sh_w), lambda i: (i, 0))] + par,
                 [_sds((tt, sh_w)), _sds((2, RWW)), _sds((128, RWW)), _sds((128, RWW)), _sds((2, RWW)), _sds((128, RWW)), _sds((128, RWW)),
                  _sds((1, RWW)), _sds((1, RWW))],
                 sem=("arbitrary",))(sh, sh, sh, w0, w2p0, w2p1, a0, a2p0, a2p1, kkp, kap_p, *scan_grads, dkd_p, dr_p, dv_p)


def _rw_post_specs():
    dm = _dims()
    nv = dm["NCTV"]
    rw = RWW // 128
    lat3 = lambda d: pl.BlockSpec((None, TMV, RWW), lambda i: (d, nv + i, 0))
    lat2 = pl.BlockSpec((TMV, RWW), lambda i: (nv + i, 0))
    ins = [lat2, lat2, lat3(0), lat3(1),
           pl.BlockSpec((TMV, RWW), lambda i: (nv + i, 0)), pl.BlockSpec((TMV, RWW), lambda i: (nv + i, 2)),
           pl.BlockSpec((TMV, RWW), lambda i: (i, 0)), _full((1, RWW)), _full((1, RWW)), _full((1, RWW))]
    return dm, nv, ins


def _rw_post_fwd(y0_, y1_, kd, sh, p_z, rk, gng, gnb):
    dm, nv, ins = _rw_post_specs()

    def body(y0, y1, k0, k1, r, v, z, rk_ref, g_ref, b_ref, o_ref):
        o_ref[...] = _rw_post_math(y0[...], y1[...], k0[...], k1[...], r[...], v[...], z[...], rk_ref[...], g_ref[...], b_ref[...]).astype(BF16)

    return _call(body, "rw_post_fwd", (T // TMV,), ins, pl.BlockSpec((TMV, RWW), lambda i: (i, 0)), _sds((T, RWW), BF16),
                 sem=("parallel",))(y0_, y1_, kd, kd, sh, sh, p_z, rk, gng, gnb)


def _rw_post_bwd(y0_, y1_, kd, sh, p_z, rk, gng, gnb, dout):
    dm, nv, _ = _rw_post_specs()
    tt = dm["TT"]
    cl = lambda i: jnp.maximum(i - nv, 0)
    all3 = lambda d: pl.BlockSpec((None, TMV, RWW), lambda i: (d, i, 0))
    all2 = pl.BlockSpec((TMV, RWW), lambda i: (i, 0))
    ins = [all2, all2, all3(0), all3(1),
           pl.BlockSpec((TMV, RWW), lambda i: (i, 0)), pl.BlockSpec((TMV, RWW), lambda i: (i, 2)),
           pl.BlockSpec((TMV, RWW), lambda i: (cl(i), 0)), _full((1, RWW)), _full((1, RWW)), _full((1, RWW)),
           pl.BlockSpec((TMV, RWW), lambda i: (cl(i), 0))]
    tok = pl.BlockSpec((TMV, RWW), lambda i: (i, 0))

    def body(y0, y1, k0, k1, r, v, z, rk_ref, g_ref, b_ref, do_ref, dy_ref, dkd_ref, dr_ref, dv_ref, dz_ref, drk_ref, dg_ref, db_ref):
        i = pl.program_id(0)
        lat = jnp.where(i >= nv, 1.0, 0.0)
        _, vjp = jax.vjp(_rw_post_math, y0[...], y1[...], k0[...], k1[...], r[...], v[...], z[...], rk_ref[...], g_ref[...], b_ref[...])
        dy0, _, dk0, _, dr, dv, dz, drk, dg, db = vjp(do_ref[...] * lat)
        dy_ref[...] = dy0
        dkd_ref[...] = dk0
        dr_ref[...] = dr
        dv_ref[...] = dv
        dz_ref[...] = dz.astype(BF16)
        _acc_out((drk_ref, dg_ref, db_ref), i == 0, (drk, dg, db))

    return _call(body, "rw_post_bwd", (tt // TMV,), ins,
                 [tok, tok, tok, tok, pl.BlockSpec((TMV, RWW), lambda i: (cl(i), 0)), _full((1, RWW)), _full((1, RWW)), _full((1, RWW))],
                 [_sds((tt, RWW))] * 4 + [_sds((T, RWW), BF16), _sds((1, RWW)), _sds((1, RWW)), _sds((1, RWW))],
                 sem=("arbitrary",))(y0_, y1_, kd, kd, sh, sh, p_z, rk, gng, gnb, dout)


def _hg_post_fwd(o0_, o1_, p_hg, hg_g):
    dm = _dims()
    nv = dm["NCTV"]
    lat3 = lambda d: pl.BlockSpec((TMV, HGW), lambda i: (nv + i, 0))

    def body(o0, o1, z, g, out):
        out[...] = _hg_post_math(o0[...], o1[...], z[...], g[...]).astype(BF16)

    return _call(body, "hg_post_fwd", (T // TMV,), [lat3(0), lat3(1), pl.BlockSpec((TMV, HGW), lambda i: (nv + i, 4)), _full((1, HGW))],
                 pl.BlockSpec((TMV, HGW), lambda i: (i, 0)), _sds((T, HGW), BF16), sem=("parallel",))(o0_, o1_, p_hg, hg_g)


def _hg_post_bwd(o0_, o1_, p_hg, hg_g, dout):
    dm = _dims()
    nv, tt = dm["NCTV"], dm["TT"]
    all3 = lambda d: pl.BlockSpec((TMV, HGW), lambda i: (i, 0))
    tok = pl.BlockSpec((TMV, HGW), lambda i: (i, 0))

    def body(o0, o1, z, g, do_in, do_ref, dz_ref, dg_ref):
        i = pl.program_id(0)
        lat = jnp.where(i >= nv, 1.0, 0.0)
        _, vjp = jax.vjp(_hg_post_math, o0[...], o1[...], z[...], g[...])
        d0, _, dz, dg = vjp(do_in[...] * lat)
        do_ref[...] = d0
        dz_ref[...] = dz
        _acc_out((dg_ref,), i == 0, (dg,))

    return _call(body, "hg_post_bwd", (tt // TMV,),
                 [all3(0), all3(1), pl.BlockSpec((TMV, HGW), lambda i: (i, 4)), _full((1, HGW)),
                  pl.BlockSpec((TMV, HGW), lambda i: (jnp.maximum(i - nv, 0), 0))],
                 [tok, tok, _full((1, HGW))], [_sds((tt, HGW)), _sds((tt, HGW)), _sds((1, HGW))], sem=("arbitrary",))(o0_, o1_, p_hg, hg_g, dout)


def _hg_dproj(dq0, dq1, di0, di1, df0, df1, dz):
    dm = _dims()
    tt = dm["TT"]
    tok = pl.BlockSpec((TMV, HGW), lambda i: (i, 0))

    def body(q0, q1, i0, i1, f0, f1, z, o_ref):
        o_ref[:, 0:HGW] = (q0[...] + q1[...]).astype(BF16)
        o_ref[:, HGW:2 * HGW] = (i0[...] + i1[...]).astype(BF16)
        o_ref[:, 2 * HGW:3 * HGW] = f0[...].astype(BF16)
        o_ref[:, 3 * HGW:4 * HGW] = f1[...].astype(BF16)
        o_ref[:, 4 * HGW:5 * HGW] = z[...].astype(BF16)

    return _call(body, "hg_dproj", (tt // TMV,), [tok] * 7,
                 pl.BlockSpec((TMV, 5 * HGW), lambda i: (i, 0)), _sds((tt, 5 * HGW), BF16), sem=("parallel",))(
        dq0, dq1, di0, di1, df0, df1, dz)


def _merge1_fwd(out_hg, out_rw, whg_t, wrw_t, p_g):
    tok = pl.BlockSpec((TM, D), lambda i: (i, 0))

    def body(h_ref, r_ref, wh_ref, wr_ref, g1_ref, g2_ref, a_ref, b_ref, m_ref):
        nt = (((1,), (1,)), ((), ()))
        a = lax.dot_general(h_ref[...], wh_ref[...], nt, preferred_element_type=F32)
        b = lax.dot_general(r_ref[...], wr_ref[...], nt, preferred_element_type=F32)
        a_ref[...] = a
        b_ref[...] = b
        m_ref[...] = _gate_math(g1_ref[...], g2_ref[...], a, b).astype(BF16)

    return _call(body, "merge1_fwd", (T // TM,),
                 [pl.BlockSpec((TM, HGW), lambda i: (i, 0)), pl.BlockSpec((TM, RWW), lambda i: (i, 0)), _full((D, HGW)), _full((D, RWW)),
                  pl.BlockSpec((TM, D), lambda i: (i, 0)), pl.BlockSpec((TM, D), lambda i: (i, 1))],
                 [tok, tok, tok], [_sds((T, D)), _sds((T, D)), _sds((T, D), BF16)], sem=("parallel",))(out_hg, out_rw, whg_t, wrw_t, p_g, p_g)


def _head_fwd_bwd(x, merged, w_out, mods, final_g, tgt):
    tok = pl.BlockSpec((TM, D), lambda i: (i, 0))

    def body(x_ref, m_ref, w_ref, mods_ref, fg_ref, t_ref, dxo_ref, dmo_ref, loss_ref, dgate_ref, dfg_ref):
        mo = jnp.dot(m_ref[...], w_ref[...], preferred_element_type=F32)
        gate = mods_ref[0:1, 2 * D:3 * D]
        loss, vjp = jax.vjp(lambda x_, mo_, g_, fg_: _head_math(x_, mo_, g_, fg_, t_ref[...]), x_ref[...], mo, gate, fg_ref[...])
        dx, dmo, dgate, dfg = vjp(jnp.ones((1, 1), F32))
        dxo_ref[...] = dx
        dmo_ref[...] = dmo.astype(BF16)
        _acc_out((loss_ref, dgate_ref, dfg_ref), pl.program_id(0) == 0, (jnp.broadcast_to(loss, (1, 128)), dgate, dfg))

    return _call(body, "head_fwd_bwd", (T // TM,), [tok, tok, _full((D, D)), _full((8, 3 * D)), _full((1, D)), tok],
                 [tok, tok, _full((1, 128)), _full((1, D)), _full((1, D))],
                 [_sds((T, D)), _sds((T, D), BF16), _sds((1, 128)), _sds((1, D)), _sds((1, D))], sem=("arbitrary",))(
        x, merged, w_out, mods, final_g, tgt)


def _merge_bwd(dmo, w_out, p_g, a, b):
    tok = pl.BlockSpec((TM, D), lambda i: (i, 0))

    def body(dmo_ref, w_ref, g1_ref, g2_ref, a_ref, b_ref, da_ref, db_ref, dg_ref):
        dm_ = lax.dot_general(dmo_ref[...], w_ref[...], (((1,), (1,)), ((), ())), preferred_element_type=F32)
        _, vjp = jax.vjp(_gate_math, g1_ref[...], g2_ref[...], a_ref[...], b_ref[...])
        dg1, dg2, da, db = vjp(dm_)
        da_ref[...] = da.astype(BF16)
        db_ref[...] = db.astype(BF16)
        dg_ref[:, 0:D] = dg1.astype(BF16)
        dg_ref[:, D:2 * D] = dg2.astype(BF16)

    return _call(body, "merge_bwd", (T // TM,),
                 [tok, _full((D, D)), pl.BlockSpec((TM, D), lambda i: (i, 0)), pl.BlockSpec((TM, D), lambda i: (i, 1)), tok, tok],
                 [tok, tok, pl.BlockSpec((TM, 2 * D), lambda i: (i, 0))], [_sds((T, D), BF16), _sds((T, D), BF16), _sds((T, 2 * D), BF16)],
                 sem=("parallel",))(dmo, w_out, p_g, p_g, a, b)


def _h_bwd(xcat, mods, norm_g, dh_hg, dh_rw, dh_z, dh_g, dxo, dgate):
    dm = _dims()
    nct, tt = dm["NCT"], dm["TT"]
    tok = pl.BlockSpec((TM, D), lambda i: (i, 0))
    lat = pl.BlockSpec((TM, D), lambda i: (jnp.maximum(i - nct, 0), 0))

    def body(x_ref, m_ref, g_ref, d1, d2, d3, d4, dxo_ref, dgate_ref, gx_ref, dng_ref, dmods_ref):
        i = pl.program_id(0)
        is_ctx = i < nct
        latf = jnp.where(is_ctx, 0.0, 1.0)
        dh = d1[...] + d2[...] + latf * (d3[...] + d4[...])
        _, vjp = jax.vjp(_h_math, x_ref[...], g_ref[...], _sel_mod(m_ref, is_ctx, D), _sel_mod(m_ref, is_ctx, 0))
        dx, dng, dscale, dshift = vjp(dh)
        gx_ref[...] = dx + dxo_ref[...]

        @pl.when(i == 0)
        def _():
            dng_ref[...] = jnp.zeros_like(dng_ref)
            dmods_ref[...] = jnp.zeros_like(dmods_ref)
            dmods_ref[0:1, 2 * D:3 * D] = dgate_ref[...]

        dng_ref[...] += dng
        row = lax.broadcasted_iota(jnp.int32, (8, 1), 0)
        sel = jnp.where(row == jnp.where(is_ctx, 1, 0), 1.0, 0.0)
        dmods_ref[:, 0:D] += sel * dshift
        dmods_ref[:, D:2 * D] += sel * dscale

    return _call(body, "h_bwd", (tt // TM,), [tok, _full((8, 3 * D)), _full((1, D)), tok, tok, lat, lat, lat, _full((1, D))],
                 [lat, _full((1, D)), _full((8, 3 * D))], [_sds((T, D)), _sds((1, D)), _sds((8, 3 * D))], sem=("arbitrary",))(
        xcat, mods, norm_g, dh_hg, dh_rw, dh_z, dh_g, dxo, dgate)


def _mods_bwd(cc, ada_t, dmods, dmods_t):
    tn = 512 if (3 * D) % 512 == 0 else 256

    nsteps = 3 * D // tn

    def body(cc_ref, w_ref, dm_ref, dmt_ref, dcc_ref, dw_ref, db_ref):
        c_ = cc_ref[...]
        s, vjp = jax.vjp(_silu, c_)
        dm_ = dm_ref[...]
        ds = jnp.dot(dm_.astype(BF16), w_ref[...], preferred_element_type=F32)
        _acc_out((dcc_ref,), pl.program_id(0) == 0, (ds,))

        @pl.when(pl.program_id(0) == nsteps - 1)
        def _():
            dcc_ref[...] = vjp(dcc_ref[...])[0]

        dw_ref[...] = _mm(dmt_ref[...], s).astype(BF16)
        db_ref[...] = dm_[0:1] + dm_[1:2]

    return _call(body, "mods_bwd", (nsteps,),
                 [_full((8, D)), pl.BlockSpec((tn, D), lambda j: (j, 0)), pl.BlockSpec((8, tn), lambda j: (0, j)),
                  pl.BlockSpec((tn, 8), lambda j: (j, 0))],
                 [_full((8, D)), pl.BlockSpec((tn, D), lambda j: (j, 0)), pl.BlockSpec((1, tn), lambda j: (0, j))],
                 [_sds((8, D)), _sds((3 * D, D), BF16), _sds((1, 3 * D))], sem=("arbitrary",))(cc, ada_t, dmods, dmods_t)


def _lb_math(l0, l1):
    return jax.nn.sigmoid(l0 - l1)


def _lb_fwd(l0, l1):
    def body(a_ref, b_ref, o_ref):
        o_ref[...] = _lb_math(a_ref[...], b_ref[...])

    return _call(body, "lb_fwd", (1,), [_full((2, HGW))] * 2, _full((2, HGW)), _sds((2, HGW)))(l0, l1)


def _lb_bwd(l0, l1, dlb):
    def body(a_ref, b_ref, d_ref, da_ref, db_ref):
        _, vjp = jax.vjp(_lb_math, a_ref[...], b_ref[...])
        da_ref[...], db_ref[...] = vjp(d_ref[...])

    return _call(body, "lb_bwd", (1,), [_full((2, HGW))] * 3, [_full((2, HGW))] * 2, [_sds((2, HGW))] * 2)(l0, l1, dlb)


def _local_step(x, c, ctx, tgt, w):
    dm = _dims()
    tt, sh_w, hgc = dm["TT"], dm["SH"], dm["HGC"]
    nb = lambda cols: cols // WB
    xcat = jnp.concatenate([ctx, x], axis=0)
    cc = jnp.concatenate([c, w["c_ctx"], jnp.zeros((6, D), F32)], axis=0)
    mods = _mods_fwd(cc, w["ada_t"], w["ada_b"])
    h = _h_fwd(xcat, mods, w["norm_g"])
    win_t = w["win_t"]
    p_hg = _proj_nt(h, win_t, 0, nb(hgc), 0, tt, "proj_hg")
    p_rw = _proj_nt(h, win_t, nb(hgc), nb(sh_w), 0, tt, "proj_rw")
    p_z = _proj_nt(h, win_t, nb(hgc + sh_w), nb(RWW), TC, T, "proj_z")
    p_g = _proj_nt(h, win_t, nb(hgc + sh_w + RWW), nb(2 * D), TC, T, "proj_g")

    lb3 = _lb_fwd(w["lb0"], w["lb1"]).reshape(2, 1, HGW)
    o0, o1, hg_ck = _hg_scan_fwd(p_hg, lb3)
    out_hg = _hg_post_fwd(o0, o1, p_hg, w["hg_g"])

    sh = _shift_fwd(p_rw, w["mu"])
    zpad = jnp.zeros((LORA, RWW), F32)
    prep_w = (w["w0"], jnp.concatenate([w["w2"][0], zpad], 0), jnp.concatenate([zpad, w["w2"][1]], 0),
              w["a0"], jnp.concatenate([w["a2"][0], zpad], 0), jnp.concatenate([zpad, w["a2"][1]], 0), w["kk"], w["ka"])
    kap, lw, b, kd = _rw_prep_fwd(sh, *prep_w)
    y0, y1, rw_ck = _rw_scan_fwd(sh, kap, lw, b, kd)
    post_w = (w["rk"], w["gng"], w["gnb"])
    out_rw = _rw_post_fwd(y0, y1, kd, sh, p_z, *post_w)

    a, bb, merged = _merge1_fwd(out_hg, out_rw, w["whg_t"], w["wrw_t"], p_g)
    dxo, dmo, loss, dgate, d_fg = _head_fwd_bwd(x, merged, w["wout"], mods, w["final_g"], tgt)
    da, db, dp_g = _merge_bwd(dmo, w["wout"], p_g, a, bb)
    g_wout = _dw_tn(merged, dmo, 0, D, 0, "dw_out")
    g_whg = _dw_tn(da, out_hg, 0, D, 0, "dw_hg")
    g_wrw = _dw_tn(db, out_rw, 0, D, 0, "dw_rw")
    d_out_hg = _mm_nn(da, w["whg_t"], 0, "dx_hg")
    d_out_rw = _mm_nn(db, w["wrw_t"], 0, "dx_rw")

    do, dz_hg, d_hg_g = _hg_post_bwd(o0, o1, p_hg, w["hg_g"], d_out_hg)
    *hg_grads, dlb3 = _hg_scan_bwd(p_hg, lb3, hg_ck, do)
    dp_hg = _hg_dproj(*hg_grads, dz_hg)
    d_lb0, d_lb1 = _lb_bwd(w["lb0"], w["lb1"], dlb3.reshape(2, HGW))

    dy, dkd_p, dr_p, dv_p, dp_z, d_rk, d_gng, d_gnb = _rw_post_bwd(y0, y1, kd, sh, p_z, *post_w, d_out_rw)
    rw_grads = _rw_scan_bwd(sh, kap, lw, b, kd, rw_ck, dy)
    dsh, d_w0, d_w2p0, d_w2p1, d_a0, d_a2p0, d_a2p1, d_kk, d_ka = _rw_prep_bwd(sh, *prep_w, rw_grads, dkd_p, dr_p, dv_p)
    dp_rw, d_mu = _shift_bwd(p_rw, dsh, w["mu"])

    g_win = _dw_tn(dp_hg, h, 0, dm["NCOLS"], 0, "dw_in_hg")
    g_win = _dw_tn(dp_rw, h, 0, dm["NCOLS"], nb(hgc), "dw_in_rw", prev=g_win)
    g_win = _dw_tn(dp_z, h, TC, dm["NCOLS"], nb(hgc + sh_w), "dw_in_z", prev=g_win)
    g_win = _dw_tn(dp_g, h, TC, dm["NCOLS"], nb(hgc + sh_w + RWW), "dw_in_g", prev=g_win)
    dh_hg = _mm_nn(dp_hg, win_t, 0, "dh_hg")
    dh_rw = _mm_nn(dp_rw, win_t, nb(hgc), "dh_rw")
    dh_z = _mm_nn(dp_z, win_t, nb(hgc + sh_w), "dh_z")
    dh_g = _mm_nn(dp_g, win_t, nb(hgc + sh_w + RWW), "dh_g")
    gx, d_ng, dmods = _h_bwd(xcat, mods, w["norm_g"], dh_hg, dh_rw, dh_z, dh_g, dxo, dgate)
    dcc, g_ada, d_ada_b = _mods_bwd(cc, w["ada_t"], dmods, dmods.T)

    big = dict(win_t=g_win, wout=g_wout, whg_t=g_whg, wrw_t=g_wrw, ada_t=g_ada)
    small = dict(c_ctx=dcc[1:2], ada_b=d_ada_b, norm_g=d_ng, lb0=d_lb0, lb1=d_lb1, hg_g=d_hg_g, mu=d_mu, w0=d_w0,
                 w2=jnp.stack([d_w2p0[:LORA], d_w2p1[LORA:]]), a0=d_a0, a2=jnp.stack([d_a2p0[:LORA], d_a2p1[LORA:]]),
                 kk=d_kk, ka=d_ka, rk=d_rk, gng=d_gng, gnb=d_gnb, final_g=d_fg)
    return loss, gx, big, small


MESH = pl.DeviceIdType.MESH


def _comm_call(body, name, bufs, out_shapes, nsem, nloc):
    hbm = pl.BlockSpec(memory_space=pl.ANY)
    return pl.pallas_call(
        body, name=name, in_specs=[hbm] * len(bufs), out_specs=[hbm] * len(out_shapes), out_shape=out_shapes,
        scratch_shapes=[pltpu.SemaphoreType.DMA((nsem,)), pltpu.SemaphoreType.DMA((nsem,)), pltpu.SemaphoreType.DMA((nloc,))],
    )(*bufs)


def _gather2(bufs, name):
    nbuf = len(bufs)

    def body(*refs):
        ins, outs = refs[:nbuf], refs[nbuf:2 * nbuf]
        send_sems, recv_sems, loc_sems = refs[2 * nbuf:]
        x, y, c = lax.axis_index("x"), lax.axis_index("y"), lax.axis_index("c")
        me, sib = (x, y, c), (x, y, 1 - c)
        chips = [(1 - x, y), (x, 1 - y), (1 - x, 1 - y)]

        def copy(bi, k, block, to, src=None):
            rows = outs[bi].at[4 * block[0] + 2 * block[1] + block[2]]
            return pltpu.make_async_remote_copy(src_ref=rows if src is None else src, dst_ref=rows, send_sem=send_sems.at[7 * bi + k],
                                                recv_sem=recv_sems.at[7 * bi + k], device_id=to, device_id_type=MESH)

        own = [pltpu.make_async_copy(ins[bi], outs[bi].at[4 * x + 2 * y + c], loc_sems.at[bi]) for bi in range(nbuf)]
        for cp in own:
            cp.start()
        first = []
        for bi in range(nbuf):
            first.append(copy(bi, 0, me, sib, src=ins[bi]))
            first += [copy(bi, 1 + j, me, (*chip, c), src=ins[bi]) for j, chip in enumerate(chips)]
        for cp in first:
            cp.start()
        passed = []
        for j, chip in enumerate(chips):
            for bi in range(nbuf):
                copy(bi, 1 + j, (*chip, c), me).wait_recv()
                passed.append(copy(bi, 4 + j, (*chip, c), sib))
                passed[-1].start()
        for bi in range(nbuf):
            copy(bi, 0, sib, me).wait_recv()
            for j, chip in enumerate(chips):
                copy(bi, 4 + j, (*chip, 1 - c), me).wait_recv()
        for cp in first + passed:
            cp.wait_send()
        for cp in own:
            cp.wait()

    return _comm_call(body, name, bufs, [_sds((NDEV,) + b.shape, b.dtype) for b in bufs], 7 * nbuf, nbuf)


def _pair_exchange(bufs, name):
    nbuf = len(bufs)

    def body(*refs):
        ins, got = refs[:nbuf], refs[nbuf:2 * nbuf]
        send_sems, recv_sems, _ = refs[2 * nbuf:]
        x, y, c = lax.axis_index("x"), lax.axis_index("y"), lax.axis_index("c")
        copies = []
        for bi in range(nbuf):
            for q in range(4):
                copies.append(pltpu.make_async_remote_copy(
                    src_ref=ins[bi].at[2 * q + 1 - c], dst_ref=got[bi].at[q], send_sem=send_sems.at[4 * bi + q],
                    recv_sem=recv_sems.at[4 * bi + q], device_id=(x, y, 1 - c), device_id_type=MESH))
                copies[-1].start()
        for cp in copies:
            cp.wait_send()
            cp.wait_recv()

    return _comm_call(body, name, bufs, [_sds((4,) + b.shape[1:], b.dtype) for b in bufs], 4 * nbuf, 1)


def _chip_exchange(bufs, name):
    nbuf = len(bufs)

    def body(*refs):
        ins, outs = refs[:nbuf], refs[nbuf:2 * nbuf]
        send_sems, recv_sems, _ = refs[2 * nbuf:]
        x, y, c = lax.axis_index("x"), lax.axis_index("y"), lax.axis_index("c")
        myq = 2 * x + y
        copies = []
        for bi in range(nbuf):
            for j, (qx, qy) in enumerate([(1 - x, y), (x, 1 - y), (1 - x, 1 - y)]):
                q = 2 * qx + qy
                send = pltpu.make_async_remote_copy(src_ref=ins[bi].at[q], dst_ref=outs[bi].at[myq], send_sem=send_sems.at[3 * bi + j],
                                                    recv_sem=recv_sems.at[3 * bi + j], device_id=(qx, qy, c), device_id_type=MESH)
                send.start()
                recv = pltpu.make_async_remote_copy(src_ref=ins[bi].at[myq], dst_ref=outs[bi].at[q], send_sem=send_sems.at[3 * bi + j],
                                                    recv_sem=recv_sems.at[3 * bi + j], device_id=(qx, qy, c), device_id_type=MESH)
                copies.append((send, recv))
        for send, recv in copies:
            send.wait_send()
            recv.wait_recv()

    return _comm_call(body, name, bufs, [_sds(b.shape, b.dtype) for b in bufs], 3 * nbuf, 1)


def _prefetch_call(body, name, scalars, grid, in_specs, out_specs, out_shape, args):
    return pl.pallas_call(
        body, name=name, out_shape=out_shape,
        grid_spec=pltpu.PrefetchScalarGridSpec(num_scalar_prefetch=1, grid=grid, in_specs=in_specs, out_specs=out_specs),
        compiler_params=pltpu.CompilerParams(dimension_semantics=("parallel",) * len(grid), vmem_limit_bytes=V7X_VMEM_LIMIT))(scalars, *args)


def _pair_add(g, got, core, name):
    _, rows, cols = got.shape
    tr = _row_tile(rows, 16, 256)

    def body(c_ref, a_ref, b_ref, o_ref):
        o_ref[...] = (a_ref[...].astype(F32) + b_ref[...].astype(F32)).astype(o_ref.dtype)

    blk = pl.BlockSpec((4, tr, cols), lambda i, c_ref: (0, i, 0))
    mine = pl.BlockSpec((4, None, tr, cols), lambda i, c_ref: (0, c_ref[0], i, 0))
    return _prefetch_call(body, name, core, (rows // tr,), [mine, blk], blk, _sds(got.shape, got.dtype),
                          (g.reshape(4, 2, rows, cols), got))


def _reduce_chips(chip_sums, recv, slots, name):
    _, rows, cols = recv.shape
    tr = _row_tile(rows, 16, 128)

    def body(s_ref, own_ref, r1_ref, r2_ref, r3_ref, o_ref):
        o_ref[...] = ((own_ref[...].astype(F32) + r1_ref[...].astype(F32)) + r2_ref[...].astype(F32)) + r3_ref[...].astype(F32)

    pick = lambda k: pl.BlockSpec((None, tr, cols), lambda i, s_ref: (s_ref[k], i, 0))
    return _prefetch_call(body, name, slots, (rows // tr,), [pick(0), pick(1), pick(2), pick(3)],
                          pl.BlockSpec((tr, cols), lambda i, s_ref: (i, 0)), _sds((rows, cols)), (chip_sums, recv, recv, recv))


def _row_tile(rows, mult, cap):
    best = mult
    for t in range(mult, cap + 1, mult):
        if rows % t == 0:
            best = t
    assert rows % best == 0, (rows, mult)
    return best


def _reduce_sources(r, name):
    nsrc, rows, cols = r.shape
    tr = _row_tile(rows, 16 if r.dtype.itemsize == 2 else 8, 128)

    def body(r_ref, o_ref):
        acc = r_ref[0].astype(F32)
        for j in range(1, nsrc):
            acc = acc + r_ref[j].astype(F32)
        o_ref[...] = acc

    return _call(body, name, (rows // tr,), [pl.BlockSpec((nsrc, tr, cols), lambda i: (0, i, 0))],
                 pl.BlockSpec((tr, cols), lambda i: (i, 0)), _sds((rows, cols)), sem=("parallel",))(r)


def _adamw(w, g, m, v, name):
    rows, cols = w.shape
    tr = _row_tile(rows, 8, 128)
    c1 = 1.0 - ADAM_B1 ** ADAM_STEP
    c2 = 1.0 - ADAM_B2 ** ADAM_STEP

    def body(w_ref, g_ref, m_ref, v_ref, d_ref, mo_ref, vo_ref):
        g_ = g_ref[...]
        m_ = ADAM_B1 * m_ref[...] + (1.0 - ADAM_B1) * g_
        v_ = ADAM_B2 * v_ref[...] + (1.0 - ADAM_B2) * (g_ * g_)
        d_ref[...] = -ADAM_LR * ((m_ / c1) / (jnp.sqrt(v_ / c2) + ADAM_EPS) + ADAM_WD * w_ref[...])
        mo_ref[...] = m_
        vo_ref[...] = v_

    blk = pl.BlockSpec((tr, cols), lambda i: (i, 0))
    return _call(body, name, (rows // tr,), [blk] * 4, [blk] * 3, [_sds((rows, cols))] * 3, sem=("parallel",))(w, g, m, v)


SLAB_PART = 1024


def _pack(arrs):
    parts = []
    for a in arrs:
        flat = a.reshape(-1)
        pad = (-flat.shape[0]) % SLAB_PART
        if pad:
            flat = jnp.concatenate([flat, jnp.zeros((pad,), flat.dtype)])
        parts.append(flat.reshape(-1, 128))
    return jnp.concatenate(parts, axis=0)


def _unpack(slab, shapes):
    outs, row = [], 0
    for s in shapes:
        n = 1
        for d in s:
            n *= d
        rows = (n + SLAB_PART - 1) // SLAB_PART * (SLAB_PART // 128)
        outs.append(slab[row:row + rows].reshape(-1)[:n].reshape(s))
        row += rows
    return outs


def _unshard_last(g, lead):
    nl = len(lead)
    return jnp.transpose(g, tuple(range(1, nl + 1)) + (0, nl + 1)).reshape(tuple(lead) + (-1,))


def kernel(x, c, ctx, c_ctx, ada_w, ada_b, norm_g, w_in, hg_lb, hg_norm_g, rw_mu, rw_w0, rw_w2, rw_a0, rw_a2, rw_kk, rw_ka, rw_rk, rw_gn_g, rw_gn_b, w_hg_out, w_rw_out, w_out, final_g, loss_target, m_c_ctx, m_ada_w, m_ada_b, m_norm_g, m_w_in, m_hg_lb, m_hg_norm_g, m_rw_mu, m_rw_w0, m_rw_w2, m_rw_a0, m_rw_a2, m_rw_kk, m_rw_ka, m_rw_rk, m_rw_gn_g, m_rw_gn_b, m_w_hg_out, m_w_rw_out, m_w_out, m_final_g, v_c_ctx, v_ada_w, v_ada_b, v_norm_g, v_w_in, v_hg_lb, v_hg_norm_g, v_rw_mu, v_rw_w0, v_rw_w2, v_rw_a0, v_rw_a2, v_rw_kk, v_rw_ka, v_rw_rk, v_rw_gn_g, v_rw_gn_b, v_w_hg_out, v_w_rw_out, v_w_out, v_final_g):
    dm = _dims()
    me = 4 * lax.axis_index("x") + 2 * lax.axis_index("y") + lax.axis_index("c")

    big_shards = [w_in[0].T.astype(BF16), w_out[0].astype(BF16), w_hg_out[0].T.astype(BF16), w_rw_out[0].T.astype(BF16),
                  ada_w[0].T.astype(BF16)]
    sharded_small = [hg_lb, rw_mu[0], rw_w0[0], rw_w2[0], rw_a0[0], rw_a2[0]]
    gathered = _gather2(big_shards + [_pack(sharded_small)], "gather_weights")
    win_t, wout, whg_t, wrw_t, ada_t = [g.reshape(-1, g.shape[2]) for g in gathered[:5]]
    per_dev = jax.vmap(lambda s: _unpack(s, [a.shape for a in sharded_small]))(gathered[5])
    hg_lb_f, mu_f, w0_f, w2_f, a0_f, a2_f = [_unshard_last(p, p.shape[1:-1]) for p in per_dev]

    w = dict(win_t=win_t, wout=wout, whg_t=whg_t, wrw_t=wrw_t, ada_t=ada_t, c_ctx=c_ctx[None], ada_b=ada_b, norm_g=norm_g,
             lb0=hg_lb_f[:, 0], lb1=hg_lb_f[:, 1], hg_g=hg_norm_g, mu=mu_f, w0=w0_f, w2=w2_f, a0=a0_f, a2=a2_f,
             kk=rw_kk, ka=rw_ka, rk=rw_rk, gng=rw_gn_g, gnb=rw_gn_b, final_g=final_g[None])
    loss_dev, grad_x, big, small = _local_step(x[0], c, ctx[0], loss_target[0], w)
    loss = lax.psum(loss_dev[0, 0], AXES)

    small_names = ["c_ctx", "ada_b", "norm_g", "lb0", "lb1", "hg_g", "mu", "w0", "w2", "a0", "a2", "kk", "ka", "rk", "gng", "gnb", "final_g"]
    small_slab = _pack([small[n] for n in small_names])
    big_names = ["win_t", "wout", "whg_t", "wrw_t", "ada_t"]
    blocks = [big[n].reshape(NDEV, -1, big[n].shape[1]) for n in big_names]
    got = _pair_exchange(blocks, "pair_grads")
    core = lax.axis_index("c").astype(jnp.int32).reshape(1)
    chip_sums = [_pair_add(g, r, core, "pair_add_" + n) for g, r, n in zip(blocks, got, big_names)]
    recv = _chip_exchange(chip_sums, "chip_grads")
    small_all = _gather2([small_slab], "gather_small_grads")[0]
    my_chip = (2 * lax.axis_index("x") + lax.axis_index("y")).astype(jnp.int32)
    others = jnp.arange(3, dtype=jnp.int32)
    slots = jnp.concatenate([my_chip.reshape(1), others + (others >= my_chip).astype(jnp.int32)])
    g_win_t, g_wout, g_whg_t, g_wrw_t, g_ada_t = [_reduce_chips(cs, r, slots, "reduce_" + n)
                                                  for cs, r, n in zip(chip_sums, recv, big_names)]
    sg = dict(zip(small_names, _unpack(_reduce_sources(small_all, "reduce_small"), [small[n].shape for n in small_names])))

    def my_shard(full):
        n = full.shape[-1] // NDEV
        return lax.dynamic_slice_in_dim(full, me * n, n, axis=full.ndim - 1)

    grads = dict(
        c_ctx=sg["c_ctx"][0], ada_w=g_ada_t.T[None], ada_b=sg["ada_b"], norm_g=sg["norm_g"], w_in=g_win_t.T[None],
        hg_lb=my_shard(jnp.stack([sg["lb0"], sg["lb1"]], axis=1)), hg_norm_g=sg["hg_g"], rw_mu=my_shard(sg["mu"])[None],
        rw_w0=my_shard(sg["w0"])[None], rw_w2=my_shard(sg["w2"])[None], rw_a0=my_shard(sg["a0"])[None], rw_a2=my_shard(sg["a2"])[None],
        rw_kk=sg["kk"], rw_ka=sg["ka"], rw_rk=sg["rk"], rw_gn_g=sg["gng"], rw_gn_b=sg["gnb"],
        w_hg_out=g_whg_t.T[None], w_rw_out=g_wrw_t.T[None], w_out=g_wout[None], final_g=sg["final_g"][0])
    weights = dict(c_ctx=c_ctx, ada_w=ada_w, ada_b=ada_b, norm_g=norm_g, w_in=w_in, hg_lb=hg_lb, hg_norm_g=hg_norm_g, rw_mu=rw_mu,
                   rw_w0=rw_w0, rw_w2=rw_w2, rw_a0=rw_a0, rw_a2=rw_a2, rw_kk=rw_kk, rw_ka=rw_ka, rw_rk=rw_rk, rw_gn_g=rw_gn_g,
                   rw_gn_b=rw_gn_b, w_hg_out=w_hg_out, w_rw_out=w_rw_out, w_out=w_out, final_g=final_g)
    m_in = dict(zip(weights, (m_c_ctx, m_ada_w, m_ada_b, m_norm_g, m_w_in, m_hg_lb, m_hg_norm_g, m_rw_mu, m_rw_w0, m_rw_w2, m_rw_a0,
                              m_rw_a2, m_rw_kk, m_rw_ka, m_rw_rk, m_rw_gn_g, m_rw_gn_b, m_w_hg_out, m_w_rw_out, m_w_out, m_final_g)))
    v_in = dict(zip(weights, (v_c_ctx, v_ada_w, v_ada_b, v_norm_g, v_w_in, v_hg_lb, v_hg_norm_g, v_rw_mu, v_rw_w0, v_rw_w2, v_rw_a0,
                              v_rw_a2, v_rw_kk, v_rw_ka, v_rw_rk, v_rw_gn_g, v_rw_gn_b, v_w_hg_out, v_w_rw_out, v_w_out, v_final_g)))

    big_w = ["ada_w", "w_in", "w_hg_out", "w_rw_out", "w_out"]
    delta, new_m, new_v = {}, {}, {}
    for n in big_w:
        shp = weights[n].shape
        two = lambda a: a.reshape(shp[-2], shp[-1])
        d_, m_, v_ = _adamw(two(weights[n]), two(grads[n]), two(m_in[n]), two(v_in[n]), "adamw_" + n)
        delta[n], new_m[n], new_v[n] = d_.reshape(shp), m_.reshape(shp), v_.reshape(shp)
    rest = [n for n in weights if n not in big_w]
    shapes = [weights[n].shape for n in rest]
    d_s, m_s, v_s = _adamw(_pack([weights[n] for n in rest]), _pack([grads[n] for n in rest]), _pack([m_in[n] for n in rest]),
                           _pack([v_in[n] for n in rest]), "adamw_small")
    for n, d_, m_, v_ in zip(rest, _unpack(d_s, shapes), _unpack(m_s, shapes), _unpack(v_s, shapes)):
        delta[n], new_m[n], new_v[n] = d_, m_, v_

    order = list(weights)
    return (loss, grad_x[None], *[grads[n] for n in order], *[delta[n] for n in order],
            *[new_m[n] for n in order], *[new_v[n] for n in order])
```

```python
import functools

import jax
import jax.numpy as jnp
from jax import lax
from jax.experimental import pallas as pl
from jax.experimental.pallas import tpu as pltpu

D = 2048
T = 2048
TC = 256
GW = 64
C = 64
HGW = 1024
RWW = 1024
LORA = 64
NDEV = 8
TM = 256
TMV = 128
NORM_EPS = 1e-6
RW_GN_EPS = 64e-5
ADAM_LR, ADAM_B1, ADAM_B2, ADAM_EPS, ADAM_WD, ADAM_STEP = 0.001, 0.9, 0.999, 1e-08, 0.01, 10
WB = 256
V7X_VMEM_LIMIT = 56 * 1024 * 1024

F32 = jnp.float32
BF16 = jnp.bfloat16
AXES = ("x", "y", "c")


def _dims():
    tt = T + TC
    sh = 3 * RWW + 4 * LORA
    hgc = 5 * HGW
    return dict(TT=tt, NCC=TC // C, NCH=tt // C, HG_H=HGW // 128, RW_P=RWW // 128, SH=sh, HGC=hgc,
                NCOLS=hgc + sh + RWW + 2 * D, NCT=TC // TM, NCTV=TC // TMV)


def _mm(a, b):
    return _dot3(a, b, "nn")


def _mm_nt(a, b):
    return _dot3(a, b, "nt")


def _mm_tn(a, b):
    return _dot3(a, b, "tn")


def _split(x):
    hi = x.astype(BF16)
    return hi, (x - hi.astype(F32)).astype(BF16)


_FORMS = {"nn": (((1,), (0,)), ((), ())), "nt": (((1,), (1,)), ((), ())), "tn": (((0,), (0,)), ((), ()))}


def _dot3_raw(a, b, form):
    ah, al = _split(a)
    bh, bl = _split(b)
    d = lambda x, y: lax.dot_general(x, y, _FORMS[form], preferred_element_type=F32)
    return d(ah, bh) + (d(ah, bl) + d(al, bh))


@functools.partial(jax.custom_vjp, nondiff_argnums=(2,))
def _dot3(a, b, form):
    return _dot3_raw(a, b, form)


def _dot3_fwd(a, b, form):
    return _dot3_raw(a, b, form), (a, b)


def _dot3_bwd(form, res, g):
    a, b = res
    if form == "nn":
        return _dot3(g, b, "nt"), _dot3(a, g, "tn")
    if form == "nt":
        return _dot3(g, b, "nn"), _dot3(g, a, "tn")
    return _dot3(b, g, "nt"), _dot3(a, g, "nn")


_dot3.defvjp(_dot3_fwd, _dot3_bwd)


def _scan_cumsum(inc, x):
    return _cumsum_vjp(inc.astype(BF16), x)


def _cumsum_raw(inc, x, form):
    h1 = x.astype(BF16)
    r1 = x - h1.astype(F32)
    h2 = r1.astype(BF16)
    h3 = (r1 - h2.astype(F32)).astype(BF16)
    d = lambda y: lax.dot_general(inc, y, _FORMS[form], preferred_element_type=F32)
    return d(h1) + (d(h2) + d(h3))


@jax.custom_vjp
def _cumsum_vjp(inc, x):
    return _cumsum_raw(inc, x, "nn")


_cumsum_vjp.defvjp(lambda inc, x: (_cumsum_raw(inc, x, "nn"), inc),
                   lambda inc, g: (jnp.zeros_like(inc), _cumsum_raw(inc, g, "tn")))


def _silu(x):
    return x * jax.nn.sigmoid(x)


def _softplus(x):
    return jnp.maximum(x, 0.0) + jnp.log(1.0 + jnp.exp(-jnp.abs(x)))


def _iota2(shape, dim):
    return lax.broadcasted_iota(jnp.int32, shape, dim)


def _pair_mask():
    return (_iota2((128, 128), 0) < 64) == (_iota2((128, 128), 1) < 64)


def _seg64_sum(x):
    e = _pair_mask().astype(F32)
    parts = [_mm(x[:, g * 128:(g + 1) * 128], e) for g in range(x.shape[1] // 128)]
    return parts[0] if len(parts) == 1 else jnp.concatenate(parts, axis=1)


def _seg128_mean(x):
    parts = [jnp.broadcast_to(jnp.mean(x[:, g * 128:(g + 1) * 128], axis=1, keepdims=True), (x.shape[0], 128))
             for g in range(x.shape[1] // 128)]
    return parts[0] if len(parts) == 1 else jnp.concatenate(parts, axis=1)


def _scan_masks(d):
    lag = (_iota2((C, C), 0) - _iota2((C, C), 1)) * (1 - 2 * d)
    return lag >= 0, lag > 0


def _tri_inverse(a):
    eye = (_iota2((C, C), 0) == _iota2((C, C), 1)).astype(F32)
    p = [-x for x in a]
    tm = [eye + x for x in p]
    n = 2
    while n < C:
        p = [_mm(x, x) for x in p]
        tm = [t + _mm(t, x) for t, x in zip(tm, p)]
        n *= 2
    return tm


@jax.custom_vjp
def _tri_solve(a, rhs):
    return [_mm(t, r) for t, r in zip(_tri_inverse(a), rhs)]


def _tri_solve_fwd(a, rhs):
    tm = _tri_inverse(a)
    u = [_mm(t, r) for t, r in zip(tm, rhs)]
    return u, (tm, u)


def _tri_solve_bwd(res, g):
    tm, u = res
    d_rhs = [_mm_tn(t, x) for t, x in zip(tm, g)]
    return [-_mm_nt(d, x) for d, x in zip(d_rhs, u)], d_rhs


_tri_solve.defvjp(_tri_solve_fwd, _tri_solve_bwd)


def _rw_chunks(chains):
    s0, r, lw, kap, b, v, kd, ds = zip(*chains)
    ids = range(len(chains))
    inc, strict = zip(*[_scan_masks(d) for d in ds])
    lane = _iota2((1, 128), 1)
    hm = ((lane < 64).astype(F32), (lane >= 64).astype(F32))
    eye = (_iota2((C, C), 0) == _iota2((C, C), 1)).astype(F32)
    lc = [_scan_cumsum(inc[i], lw[i]) for i in ids]
    ltot = [jnp.sum(lw[i], axis=0, keepdims=True) for i in ids]
    rt = [r[i] * jnp.exp(lc[i]) for i in ids]
    kt = [kap[i] * jnp.exp(lc[i] - lw[i]) for i in ids]
    einv = [jnp.exp(-lc[i]) for i in ids]
    kh = [kd[i] * einv[i] for i in ids]
    bh = [b[i] * einv[i] for i in ids]
    eend = [jnp.exp(ltot[i] - lc[i]) for i in ids]
    kbar = [kd[i] * eend[i] for i in ids]
    bbar = [b[i] * eend[i] for i in ids]
    lhs = [jnp.concatenate([kt[i] * hm[0], kt[i] * hm[1], rt[i] * hm[0], rt[i] * hm[1]], axis=0) for i in ids]
    gk = [_mm_nt(lhs[i], kh[i]) for i in ids]
    gb = [_mm_nt(lhs[i], bh[i]) for i in ids]
    ks = [_mm_nt(kt[i], s0[i]) for i in ids]
    ys = [_mm_nt(rt[i], s0[i]) for i in ids]
    heads = [(i, h) for i in ids for h in range(2)]
    akk = {c: jnp.where(strict[c[0]], gk[c[0]][c[1] * C:(c[1] + 1) * C], 0.0) for c in heads}
    akb = {c: jnp.where(strict[c[0]], gb[c[0]][c[1] * C:(c[1] + 1) * C], 0.0) for c in heads}
    ark = {c: jnp.where(inc[c[0]], gk[c[0]][(2 + c[1]) * C:(3 + c[1]) * C], 0.0) for c in heads}
    arb = {c: jnp.where(inc[c[0]], gb[c[0]][(2 + c[1]) * C:(3 + c[1]) * C], 0.0) for c in heads}
    vh = {c: v[c[0]] * hm[c[1]] for c in heads}
    av = {c: _mm(akk[c], vh[c]) for c in heads}
    rhs = {c: ks[c[0]] * hm[c[1]] + av[c] for c in heads}
    uh = dict(zip(heads, _tri_solve([akb[c] for c in heads], [rhs[c] for c in heads])))
    yv = {c: _mm(ark[c], vh[c]) for c in heads}
    yu = {c: _mm(arb[c], uh[c]) for c in heads}
    u = [uh[i, 0] + uh[i, 1] for i in ids]
    y = [ys[i] + (yv[i, 0] - yu[i, 0]) + (yv[i, 1] - yu[i, 1]) for i in ids]
    upd = [_mm_tn(jnp.concatenate([v[i], -u[i]], axis=0), jnp.concatenate([kbar[i], bbar[i]], axis=0)) for i in ids]
    s1 = [s0[i] * jnp.exp(ltot[i]) + jnp.where(_pair_mask(), upd[i], 0.0) for i in ids]
    return [(y[i], s1[i]) for i in ids]


HG_SUB = 16


def _hg_chunks(chains):
    s0, qr, iv, f, lb, ds = zip(*chains)
    ids = range(len(chains))
    inc = [_scan_masks(d)[0] for d in ds]
    q = [_silu(qr[i]) for i in ids]
    fg = [lb[i] + (1.0 - lb[i]) * jax.nn.sigmoid(f[i]) for i in ids]
    k = [1.0 - fg[i] for i in ids]
    g = [jnp.log(fg[i]) for i in ids]
    bc = [_scan_cumsum(inc[i], g[i]) for i in ids]
    btot = [jnp.sum(g[i], axis=0, keepdims=True) for i in ids]
    o_inter = [_mm_nt(q[i] * jnp.exp(bc[i]), s0[i]) for i in ids]
    rowi = _iota2((C, 1), 0)
    outs = [[] for _ in ids]
    for blk in range(C // HG_SUB):
        lo, hi = blk * HG_SUB, (blk + 1) * HG_SUB
        first = [lo if ds[i] == 0 else hi - 1 for i in ids]
        ref = [jnp.sum(jnp.where(rowi == first[i], bc[i] - g[i], 0.0), axis=0, keepdims=True) for i in ids]
        qi = [q[i][lo:hi] * jnp.exp(bc[i][lo:hi] - ref[i]) for i in ids]
        src = [(rowi < hi) if ds[i] == 0 else (rowi >= lo) for i in ids]
        ke = [jnp.where(src[i], k[i] * jnp.exp(jnp.where(src[i], ref[i] - bc[i], 0.0)), 0.0) for i in ids]
        a = [jnp.where(inc[i][lo:hi], _mm_nt(qi[i], ke[i]), 0.0) for i in ids]
        part = [_mm(a[i], iv[i]) for i in ids]
        for i in ids:
            outs[i].append(part[i])
    o = [o_inter[i] + jnp.concatenate(outs[i], axis=0) for i in ids]
    upd = [_mm_tn(iv[i], k[i] * jnp.exp(btot[i] - bc[i])) for i in ids]
    s1 = [s0[i] * jnp.exp(btot[i]) + upd[i] for i in ids]
    return [(o[i], s1[i]) for i in ids]


def _lockstep(fn, ds, nargs):
    def flat_fn(*flat):
        return tuple(fn([tuple(flat[i * nargs:(i + 1) * nargs]) + (d,) for i, d in enumerate(ds)]))

    return flat_fn


def _h_math(x, ng, scale, shift):
    return x * lax.rsqrt(jnp.mean(x * x, axis=-1, keepdims=True) + NORM_EPS) * ng * (1.0 + scale) + shift


def _hg_post_math(o0, o1, z, g):
    o = o0 + o1
    on = o * lax.rsqrt(_seg128_mean(o * o) + NORM_EPS)
    return on * g * _silu(z)


def _rw_prep_math(k, pw0, pw1, pa0, pa1, w0, a0, kkp, kap_p):
    kk = k * kkp
    kap = kk * lax.rsqrt(_seg64_sum(kk * kk) + 1e-12)
    outs = []
    for d, (pw, pa) in enumerate(((pw0, pa0), (pw1, pa1))):
        w_log = -_softplus(-(w0[d:d + 1] + pw)) - 0.5
        lw = -jnp.exp(w_log)
        a = jax.nn.sigmoid(a0[d:d + 1] + pa)
        kd = k * (1.0 + (a - 1.0) * kap_p)
        outs += [lw, kap * a, kd]
    return (kap, *outs)


def _rw_post_math(y0, y1, kd0, kd1, r, v, z, rk, gng, gnb):
    ys, ksum = y0 + y1, kd0 + kd1
    mean = _seg64_sum(ys) * (1.0 / 64.0)
    cen = ys - mean
    var = _seg64_sum(cen * cen) * (1.0 / 64.0)
    yn = cen * lax.rsqrt(var + RW_GN_EPS) * gng + gnb
    bonus = _seg64_sum(r * ksum * rk) * v
    return (yn + bonus) * _silu(z)


def _head_math(x, mo, gate, fg, tgt):
    xo = x + gate * mo
    y = xo * lax.rsqrt(jnp.mean(xo * xo, axis=-1, keepdims=True) + NORM_EPS) * fg
    err = y - tgt
    return 0.5 * jnp.sum(jnp.mean(err * err, axis=-1, keepdims=True), axis=0, keepdims=True)


def _gate_math(ghg, grw, a, b):
    return jax.nn.sigmoid(ghg) * a + jax.nn.sigmoid(grw) * b


def _call(body, name, grid, in_specs, out_specs, out_shape, scratch=(), sem=None, aliases=None):
    return pl.pallas_call(
        body, name=name, grid=grid, in_specs=in_specs, out_specs=out_specs, out_shape=out_shape,
        scratch_shapes=list(scratch), input_output_aliases=aliases or {},
        compiler_params=pltpu.CompilerParams(dimension_semantics=sem, vmem_limit_bytes=V7X_VMEM_LIMIT))


def _full(shape):
    nd = len(shape)
    return pl.BlockSpec(shape, lambda *_: (0,) * nd)


def _sds(shape, dtype=F32):
    return jax.ShapeDtypeStruct(shape, dtype)


def _proj_nt(h, wt, wt_blk0, nblk, row0, nrows, name):
    def body(h_ref, w_ref, o_ref):
        o_ref[...] = lax.dot_general(h_ref[row0:row0 + nrows, :], w_ref[...], (((1,), (1,)), ((), ())), preferred_element_type=F32)

    return _call(body, name, (nblk,), [_full(h.shape), pl.BlockSpec((WB, wt.shape[1]), lambda j: (wt_blk0 + j, 0))],
                 pl.BlockSpec((nrows, WB), lambda j: (0, j)), _sds((nrows, nblk * WB)), sem=("parallel",))(h, wt)


def _dw_tn(a, b, row0, out_rows, out_blk0, name, prev=None):
    m, n = a.shape
    k2 = b.shape[1]
    nblk = n // WB

    def body(*refs):
        a_ref, b_ref, o_ref = refs[0], refs[1], refs[-1]
        o_ref[...] = lax.dot_general(a_ref[...], b_ref[row0:row0 + m, :], (((0,), (0,)), ((), ())),
                                     preferred_element_type=F32).astype(BF16)

    in_specs = [pl.BlockSpec((m, WB), lambda j: (0, j)), _full(b.shape)]
    args = [a, b]
    aliases = None
    if prev is not None:
        in_specs.append(pl.BlockSpec(memory_space=pl.ANY))
        args.append(prev)
        aliases = {2: 0}
    return _call(body, name, (nblk,), in_specs, pl.BlockSpec((WB, k2), lambda j: (out_blk0 + j, 0)), _sds((out_rows, k2), BF16),
                 sem=("arbitrary",), aliases=aliases)(*args)


def _mm_nn(a, w, w_blk0, name):
    kb = WB
    m, kc = a.shape
    n = w.shape[1]
    tmb = 3 * TM if m % (3 * TM) == 0 else (2 * TM if m % (2 * TM) == 0 else TM)

    def body(a_ref, w_ref, o_ref):
        @pl.when(pl.program_id(1) == 0)
        def _():
            o_ref[...] = jnp.zeros_like(o_ref)

        o_ref[...] += jnp.dot(a_ref[...], w_ref[...], preferred_element_type=F32)

    return _call(body, name, (m // tmb, kc // kb),
                 [pl.BlockSpec((tmb, kb), lambda i, k: (i, k)), pl.BlockSpec((kb, n), lambda i, k: (w_blk0 + k, 0))],
                 pl.BlockSpec((tmb, n), lambda i, k: (i, 0)), _sds((m, n)), sem=("parallel", "arbitrary"))(a, w)


def _ada_fwd(cc16, ada_shard):
    ncol = ada_shard.shape[1]

    def body(c_ref, w_ref, o_ref):
        o_ref[...] = _mm(_silu(c_ref[...]), w_ref[...])

    return _call(body, "ada_fwd", (1,), [_full((16, D)), _full((D, ncol))], _full((16, ncol)), _sds((16, ncol)))(cc16, ada_shard)


def _ada_bwd(cc16, ada_shard, dm_x, dm_c, dmods):
    ncol = ada_shard.shape[1]

    def body(c_ref, w_ref, dx_ref, dc_ref, dm_ref, dw_ref, dcc_ref, db_ref):
        s, vjp = jax.vjp(_silu, c_ref[...])
        dc_tot = dc_ref[0:1, :]
        for j in range(1, NDEV):
            dc_tot = dc_tot + dc_ref[j:j + 1, :]
        row = _iota2((8, 1), 0)
        dm16 = jnp.concatenate([dx_ref[...], jnp.where(row == 0, dc_tot, 0.0)], axis=0)
        dw_ref[...] = _mm_tn(s, dm16)
        dcc_ref[...] = vjp(_mm_nt(dm16, w_ref[...]))[0]
        db_ref[...] = dm_ref[0:1, :] + dm_ref[1:2, :]

    return _call(body, "ada_bwd", (1,), [_full((16, D)), _full((D, ncol)), _full((8, ncol)), _full((8, ncol)), _full((8, 3 * D))],
                 [_full((D, ncol)), _full((16, D)), _full((1, 3 * D))], [_sds((D, ncol)), _sds((16, D)), _sds((1, 3 * D))])(
        cc16, ada_shard, dm_x, dm_c, dmods)


def _sel_mod(mods_ref, is_ctx, lo):
    return jnp.where(is_ctx, mods_ref[1:2, lo:lo + D], mods_ref[0:1, lo:lo + D])


def _h_fwd(xcat, mods, norm_g):
    dm = _dims()

    def body(x_ref, m_ref, g_ref, o_ref):
        is_ctx = pl.program_id(0) < dm["NCT"]
        o_ref[...] = _h_math(x_ref[...], g_ref[...], _sel_mod(m_ref, is_ctx, D), _sel_mod(m_ref, is_ctx, 0)).astype(BF16)

    return _call(body, "h_fwd", (dm["TT"] // TM,),
                 [pl.BlockSpec((TM, D), lambda i: (i, 0)), _full((8, 3 * D)), _full((1, D))],
                 pl.BlockSpec((TM, D), lambda i: (i, 0)), _sds((dm["TT"], D), BF16), sem=("parallel",))(xcat, mods, norm_g)


def _chunk_of(d, s):
    dm = _dims()
    ncc, nch = dm["NCC"], dm["NCH"]
    return s if d == 0 else jnp.where(s < ncc, ncc - 1 - s, nch - 1 - (s - ncc))


HG_GROUP = 4
RW_GROUP = 4


def _dir_specs(make):
    return [make(d) for d in range(2)]


def _lanes(j):
    return slice(j * 128, (j + 1) * 128)


def _hg_scan_fwd(p_hg, lb3):
    dm = _dims()
    nh, nch, tt = dm["HG_H"], dm["NCH"], dm["TT"]
    pp = min(HG_GROUP, nh)
    lw_ = 128 * pp

    def body(q0, q1, i0, i1, f0, f1, lb0, lb1, o0, o1, ck_ref, s_ref):
        @pl.when(pl.program_id(1) == 0)
        def _():
            s_ref[...] = jnp.zeros_like(s_ref)

        keys, chains = [], []
        for d, refs in enumerate(((q0, i0, f0, lb0), (q1, i1, f1, lb1))):
            vals = [r[...] for r in refs]
            for j in range(pp):
                keys.append((d, j))
                chains.append((s_ref[d, j], *[a[:, _lanes(j)] for a in vals], d))
        for (d, j), chain, (out, s1) in zip(keys, chains, _hg_chunks(chains)):
            ck_ref[d, j] = chain[0]
            (o0, o1)[d][:, _lanes(j)] = out
            s_ref[d, j] = s1

    blk = lambda off: _dir_specs(lambda d: pl.BlockSpec((C, lw_), lambda h, s: (_chunk_of(d, s), off * nh // pp + h)))
    return _call(body, "hg_scan_fwd", (nh // pp, nch),
                 blk(0) + blk(1) + _dir_specs(lambda d: pl.BlockSpec((C, lw_), lambda h, s: (_chunk_of(d, s), (2 + d) * nh // pp + h)))
                 + _dir_specs(lambda d: pl.BlockSpec((None, 1, lw_), lambda h, s: (d, 0, h))),
                 _dir_specs(lambda d: pl.BlockSpec((C, lw_), lambda h, s: (_chunk_of(d, s), h)))
                 + [pl.BlockSpec((2, pp, None, 128, 128), lambda h, s: (0, h, s, 0, 0))],
                 [_sds((tt, HGW)), _sds((tt, HGW)), _sds((2, nh, nch, 128, 128))],
                 scratch=[pltpu.VMEM((2, pp, 128, 128), F32)], sem=("parallel", "arbitrary"))(*([p_hg] * 6), lb3, lb3)


def _hg_scan_bwd(p_hg, lb3, ck, do):
    dm = _dims()
    nh, nch, tt = dm["HG_H"], dm["NCH"], dm["TT"]
    pp = min(HG_GROUP, nh)
    lw_ = 128 * pp

    def body(q0, q1, i0, i1, f0, f1, lb0, lb1, ck_ref, do0, do1, dq0, dq1, di0, di1, df0, df1, dlb_ref, ds_ref):
        @pl.when(pl.program_id(1) == 0)
        def _():
            ds_ref[...] = jnp.zeros_like(ds_ref)
            dlb_ref[...] = jnp.zeros_like(dlb_ref)

        keys, flat, cts = [], [], []
        for d, refs in enumerate(((q0, i0, f0, lb0, do0), (q1, i1, f1, lb1, do1))):
            *vals, do_ = [r[...] for r in refs]
            for j in range(pp):
                keys.append((d, j))
                flat += [ck_ref[d, j], *[a[:, _lanes(j)] for a in vals]]
                cts.append((do_[:, _lanes(j)], ds_ref[d, j]))
        _, vjp = jax.vjp(_lockstep(_hg_chunks, [d for d, _ in keys], 5), *flat)
        grads = vjp(tuple(cts))
        for n, (d, j) in enumerate(keys):
            ds0, dq_, di_, df_, dlb = grads[5 * n:5 * n + 5]
            (dq0, dq1)[d][:, _lanes(j)] = dq_
            (di0, di1)[d][:, _lanes(j)] = di_
            (df0, df1)[d][:, _lanes(j)] = df_
            dlb_ref[d, :, _lanes(j)] += dlb
            ds_ref[d, j] = ds0

    ch = lambda d, s: _chunk_of(d, nch - 1 - s)
    blk = lambda off: _dir_specs(lambda d: pl.BlockSpec((C, lw_), lambda h, s: (ch(d, s), off * nh // pp + h)))
    tok = _dir_specs(lambda d: pl.BlockSpec((C, lw_), lambda h, s: (ch(d, s), h)))
    return _call(body, "hg_scan_bwd", (nh // pp, nch),
                 blk(0) + blk(1) + _dir_specs(lambda d: pl.BlockSpec((C, lw_), lambda h, s: (ch(d, s), (2 + d) * nh // pp + h)))
                 + _dir_specs(lambda d: pl.BlockSpec((None, 1, lw_), lambda h, s: (d, 0, h)))
                 + [pl.BlockSpec((2, pp, None, 128, 128), lambda h, s: (0, h, nch - 1 - s, 0, 0))] + tok,
                 tok * 3 + [pl.BlockSpec((2, 1, lw_), lambda h, s: (0, 0, h))],
                 [_sds((tt, HGW))] * 6 + [_sds((2, 1, HGW))],
                 scratch=[pltpu.VMEM((2, pp, 128, 128), F32)], sem=("parallel", "arbitrary"))(*([p_hg] * 6), lb3, lb3, ck, do, do)


def _rw_scan_fwd(sh, kap, lw, b, kd):
    dm = _dims()
    npair, nch, tt = dm["RW_P"], dm["NCH"], dm["TT"]
    pp = min(RW_GROUP, npair)
    lw_ = 128 * pp

    def body(r0, r1, v0, v1, ka0, ka1, lw0, lw1, b0, b1, kd0, kd1, y0, y1, ck_ref, s_ref):
        @pl.when(pl.program_id(1) == 0)
        def _():
            s_ref[...] = jnp.zeros_like(s_ref)

        keys, chains = [], []
        for d, refs in enumerate(((r0, lw0, ka0, b0, v0, kd0), (r1, lw1, ka1, b1, v1, kd1))):
            vals = [r[...] for r in refs]
            for j in range(pp):
                keys.append((d, j))
                chains.append((s_ref[d, j], *[a[:, _lanes(j)] for a in vals], d))
        for (d, j), chain, (out, s1) in zip(keys, chains, _rw_chunks(chains)):
            ck_ref[d, j] = chain[0]
            (y0, y1)[d][:, _lanes(j)] = out
            s_ref[d, j] = s1

    two = lambda off: _dir_specs(lambda d: pl.BlockSpec((C, lw_), lambda p, s: (_chunk_of(d, s), off * npair // pp + p)))
    three = _dir_specs(lambda d: pl.BlockSpec((None, C, lw_), lambda p, s: (d, _chunk_of(d, s), p)))
    return _call(body, "rw_scan_fwd", (npair // pp, nch), two(0) + two(2) + two(0) + three * 3,
                 two(0) + [pl.BlockSpec((2, pp, None, 128, 128), lambda p, s: (0, p, s, 0, 0))],
                 [_sds((tt, RWW)), _sds((tt, RWW)), _sds((2, npair, nch, 128, 128))],
                 scratch=[pltpu.VMEM((2, pp, 128, 128), F32)], sem=("parallel", "arbitrary"))(
        sh, sh, sh, sh, kap, kap, lw, lw, b, b, kd, kd)


def _rw_scan_bwd(sh, kap, lw, b, kd, ck, dy):
    dm = _dims()
    npair, nch, tt = dm["RW_P"], dm["NCH"], dm["TT"]
    pp = min(RW_GROUP, npair)
    lw_ = 128 * pp

    def body(r0, r1, v0, v1, ka0, ka1, lw0, lw1, b0, b1, kd0, kd1, ck_ref, dy0, dy1,
             dr0, dr1, dv0, dv1, dka0, dka1, dlw0, dlw1, db0, db1, dkd0, dkd1, ds_ref):
        @pl.when(pl.program_id(1) == 0)
        def _():
            ds_ref[...] = jnp.zeros_like(ds_ref)

        keys, flat, cts = [], [], []
        for d, refs in enumerate(((r0, lw0, ka0, b0, v0, kd0, dy0), (r1, lw1, ka1, b1, v1, kd1, dy1))):
            *vals, dy_ = [r[...] for r in refs]
            for j in range(pp):
                keys.append((d, j))
                flat += [ck_ref[d, j], *[a[:, _lanes(j)] for a in vals]]
                cts.append((dy_[:, _lanes(j)], ds_ref[d, j]))
        _, vjp = jax.vjp(_lockstep(_rw_chunks, [d for d, _ in keys], 7), *flat)
        grads = vjp(tuple(cts))
        outs = ((dr0, dlw0, dka0, db0, dv0, dkd0), (dr1, dlw1, dka1, db1, dv1, dkd1))
        for n, (d, j) in enumerate(keys):
            ds0, *g_in = grads[7 * n:7 * n + 7]
            for o_ref, g in zip(outs[d], g_in):
                o_ref[:, _lanes(j)] = g
            ds_ref[d, j] = ds0

    ch = lambda d, s: _chunk_of(d, nch - 1 - s)
    two = lambda off: _dir_specs(lambda d: pl.BlockSpec((C, lw_), lambda p, s: (ch(d, s), off * npair // pp + p)))
    three = _dir_specs(lambda d: pl.BlockSpec((None, C, lw_), lambda p, s: (d, ch(d, s), p)))
    return _call(body, "rw_scan_bwd", (npair // pp, nch),
                 two(0) + two(2) + two(0) + three * 3
                 + [pl.BlockSpec((2, pp, None, 128, 128), lambda p, s: (0, p, nch - 1 - s, 0, 0))] + two(0),
                 two(0) * 6, [_sds((tt, RWW))] * 12,
                 scratch=[pltpu.VMEM((2, pp, 128, 128), F32)], sem=("parallel", "arbitrary"))(
        sh, sh, sh, sh, kap, kap, lw, lw, b, b, kd, kd, ck, dy, dy)


def _tile_pos(i):
    dm = _dims()
    is_ctx = i < dm["NCT"]
    rows = lax.broadcasted_iota(jnp.int32, (TM, 1), 0)
    tl = rows + (i - dm["NCT"]) * TM
    width = jnp.where(is_ctx, TC, GW)
    assert TC & (TC - 1) == 0 and GW & (GW - 1) == 0 and TM % GW == 0
    colp = rows & (width - 1)
    return is_ctx, rows, tl, colp, width


def _shift_lr(cur, i):
    _, _, _, colp, width = _tile_pos(i)
    left = jnp.where(colp == 0, 0.0, pltpu.roll(cur, 1, 0))
    right = jnp.where(colp == width - 1, 0.0, pltpu.roll(cur, TM - 1, 0))
    return left, right


def _shift_ud(cur, prv, nxt, i):
    is_ctx, rows, tl, _, _ = _tile_pos(i)
    if TM > GW:
        up = jnp.where(rows >= GW, pltpu.roll(cur, GW, 0), pltpu.roll(prv, GW, 0))
        down = jnp.where(rows < TM - GW, pltpu.roll(cur, TM - GW, 0), pltpu.roll(nxt, TM - GW, 0))
    else:
        up, down = prv, nxt
    up = jnp.where(jnp.logical_or(is_ctx, tl < GW), 0.0, up)
    down = jnp.where(jnp.logical_or(is_ctx, tl >= T - GW), 0.0, down)
    return up, down


def _shift_specs():
    dm = _dims()
    nt = dm["TT"] // TM
    cw = max(w for w in range(128, 1664 + 1, 128) if dm["SH"] % w == 0)
    cur = pl.BlockSpec((TM, cw), lambda j, i: (i, j))
    prv = pl.BlockSpec((TM, cw), lambda j, i: (jnp.maximum(i - 1, 0), j))
    nxt = pl.BlockSpec((TM, cw), lambda j, i: (jnp.minimum(i + 1, nt - 1), j))
    mu = pl.BlockSpec((4, cw), lambda j, i: (0, j))
    return nt, cw, cur, prv, nxt, mu


def _shift_fwd(p_rw, mu):
    dm = _dims()
    nt, cw, cur, prv, nxt, mus = _shift_specs()

    def body(c_ref, p_ref, n_ref, mu_ref, o_ref):
        i = pl.program_id(1)
        p, m = c_ref[...], mu_ref[...]
        left, right = _shift_lr(p, i)
        up, down = _shift_ud(p, p_ref[...], n_ref[...], i)
        vert = jnp.where(i < dm["NCT"], 0.0, 1.0)
        o_ref[...] = (p * (1.0 - m[0:1] - m[1:2] - vert * (m[2:3] + m[3:4]))
                      + m[0:1] * left + m[1:2] * right + m[2:3] * up + m[3:4] * down)

    return _call(body, "shift_fwd", (dm["SH"] // cw, nt), [cur, prv, nxt, mus], cur, _sds((dm["TT"], dm["SH"])),
                 sem=("parallel", "parallel"))(p_rw, p_rw, p_rw, mu)


def _shift_bwd(p_rw, dsh, mu):
    dm = _dims()
    nt, cw, cur, prv, nxt, mus = _shift_specs()

    def body(c_ref, p_ref, n_ref, gc_ref, gp_ref, gn_ref, mu_ref, dp_ref, dmu_ref):
        i = pl.program_id(1)
        p, g, m = c_ref[...], gc_ref[...], mu_ref[...]
        vert = jnp.where(i < dm["NCT"], 0.0, 1.0)
        _, from_right = _shift_lr(m[0:1] * g, i)
        from_left, _ = _shift_lr(m[1:2] * g, i)
        _, from_down = _shift_ud(m[2:3] * g, g, m[2:3] * gn_ref[...], i)
        from_up, _ = _shift_ud(m[3:4] * g, m[3:4] * gp_ref[...], g, i)
        dp = g * (1.0 - m[0:1] - m[1:2] - vert * (m[2:3] + m[3:4])) + from_right + from_left + from_down + from_up
        dp_ref[...] = dp.astype(BF16)

        @pl.when(i == 0)
        def _():
            dmu_ref[...] = jnp.zeros_like(dmu_ref)

        left, right = _shift_lr(p, i)
        up, down = _shift_ud(p, p_ref[...], n_ref[...], i)
        s = lambda a: jnp.sum(a, axis=0, keepdims=True)
        dmu_ref[...] += jnp.concatenate([s(g * (left - p)), s(g * (right - p)), vert * s(g * (up - p)), vert * s(g * (down - p))], axis=0)

    return _call(body, "shift_bwd", (dm["SH"] // cw, nt), [cur, prv, nxt, cur, prv, nxt, mus],
                 [cur, pl.BlockSpec((4, cw), lambda j, i: (0, j))], [_sds((dm["TT"], dm["SH"]), BF16), _sds((4, dm["SH"]))],
                 sem=("parallel", "arbitrary"))(p_rw, p_rw, p_rw, dsh, dsh, dsh, mu)


def _acc_out(ref, first, vals):
    @pl.when(first)
    def _():
        for r in ref:
            r[...] = jnp.zeros_like(r)

    for r, v in zip(ref, vals):
        r[...] += v


def _rowsum(a):
    return jnp.sum(a, axis=0, keepdims=True)


def _rw_prep_specs():
    dm = _dims()
    rw = RWW // 128
    tok = lambda width, cb: pl.BlockSpec((TMV, width), lambda i: (i, cb))
    ins = [pl.BlockSpec((TMV, RWW), lambda i: (i, 1)), tok(128, 3 * rw), tok(128, 3 * rw + 1),
           _full((2, RWW)), _full((128, RWW)), _full((128, RWW)), _full((2, RWW)), _full((128, RWW)), _full((128, RWW)),
           _full((1, RWW)), _full((1, RWW))]
    return dm, ins


def _rw_prep_fwd(sh, w0, w2p0, w2p1, a0, a2p0, a2p1, kkp, kap_p):
    dm, ins = _rw_prep_specs()
    tt = dm["TT"]

    def body(k_ref, low_ref, loa_ref, w0_ref, w20_ref, w21_ref, a0_ref, a20_ref, a21_ref, kk_ref, ka_ref, kap_ref, lw_ref, b_ref, kd_ref):
        tw, la = jnp.tanh(low_ref[...]), loa_ref[...]
        kap, lw0, b0, kd0, lw1, b1, kd1 = _rw_prep_math(
            k_ref[...], _mm(tw, w20_ref[...]), _mm(tw, w21_ref[...]), _mm(la, a20_ref[...]), _mm(la, a21_ref[...]),
            w0_ref[...], a0_ref[...], kk_ref[...], ka_ref[...])
        kap_ref[...] = kap
        lw_ref[0], lw_ref[1] = lw0, lw1
        b_ref[0], b_ref[1] = b0, b1
        kd_ref[0], kd_ref[1] = kd0, kd1

    two = pl.BlockSpec((2, TMV, RWW), lambda i: (0, i, 0))
    return _call(body, "rw_prep_fwd", (tt // TMV,), ins, [pl.BlockSpec((TMV, RWW), lambda i: (i, 0)), two, two, two],
                 [_sds((tt, RWW)), _sds((2, tt, RWW)), _sds((2, tt, RWW)), _sds((2, tt, RWW))], sem=("parallel",))(
        sh, sh, sh, w0, w2p0, w2p1, a0, a2p0, a2p1, kkp, kap_p)


def _rw_prep_bwd(sh, w0, w2p0, w2p1, a0, a2p0, a2p1, kkp, kap_p, scan_grads, dkd_p, dr_p, dv_p):
    dm, ins = _rw_prep_specs()
    tt, sh_w = dm["TT"], dm["SH"]
    one = pl.BlockSpec((TMV, RWW), lambda i: (i, 0))

    def body(k_ref, low_ref, loa_ref, w0_ref, w20_ref, w21_ref, a0_ref, a20_ref, a21_ref, kk_ref, ka_ref,
             dr0, dr1, dv0, dv1, dka0, dka1, dlw0, dlw1, db0, db1, dkd0, dkd1, dkdp_ref, drp_ref, dvp_ref,
             dsh_ref, dw0_ref, dw20_ref, dw21_ref, da0_ref, da20_ref, da21_ref, dkk_ref, dka_ref):
        tw, la = jnp.tanh(low_ref[...]), loa_ref[...]
        w20, w21, a20, a21 = w20_ref[...], w21_ref[...], a20_ref[...], a21_ref[...]
        _, vjp = jax.vjp(_rw_prep_math, k_ref[...], _mm(tw, w20), _mm(tw, w21), _mm(la, a20), _mm(la, a21),
                         w0_ref[...], a0_ref[...], kk_ref[...], ka_ref[...])
        dkp = dkdp_ref[...]
        dk, dpw0, dpw1, dpa0, dpa1, dw0, da0, dkk, dka = vjp(
            (dka0[...] + dka1[...], dlw0[...], db0[...], dkd0[...] + dkp, dlw1[...], db1[...], dkd1[...] + dkp))
        twt, lat = tw.T, la.T
        dtw = _mm_nt(dpw0, w20) + _mm_nt(dpw1, w21)
        dsh_ref[:, 0:RWW] = dr0[...] + dr1[...] + drp_ref[...]
        dsh_ref[:, RWW:2 * RWW] = dk
        dsh_ref[:, 2 * RWW:3 * RWW] = dv0[...] + dv1[...] + dvp_ref[...]
        dsh_ref[:, 3 * RWW:3 * RWW + 128] = dtw * (1.0 - tw * tw)
        dsh_ref[:, 3 * RWW + 128:3 * RWW + 256] = _mm_nt(dpa0, a20) + _mm_nt(dpa1, a21)
        _acc_out((dw0_ref, dw20_ref, dw21_ref, da0_ref, da20_ref, da21_ref, dkk_ref, dka_ref), pl.program_id(0) == 0,
                 (dw0, _mm(twt, dpw0), _mm(twt, dpw1), da0, _mm(lat, dpa0), _mm(lat, dpa1), dkk, dka))

    par = [_full((2, RWW)), _full((128, RWW)), _full((128, RWW)), _full((2, RWW)), _full((128, RWW)), _full((128, RWW)),
           _full((1, RWW)), _full((1, RWW))]
    return _call(body, "rw_prep_bwd", (tt // TMV,), ins + [one] * 15,
                 [pl.BlockSpec((TMV, sh_w), lambda i: (i, 0))] + par,
                 [_sds((tt, sh_w)), _sds((2, RWW)), _sds((128, RWW)), _sds((128, RWW)), _sds((2, RWW)), _sds((128, RWW)), _sds((128, RWW)),
                  _sds((1, RWW)), _sds((1, RWW))],
                 sem=("arbitrary",))(sh, sh, sh, w0, w2p0, w2p1, a0, a2p0, a2p1, kkp, kap_p, *scan_grads, dkd_p, dr_p, dv_p)


def _rw_post_specs():
    dm = _dims()
    nv = dm["NCTV"]
    rw = RWW // 128
    lat3 = lambda d: pl.BlockSpec((None, TMV, RWW), lambda i: (d, nv + i, 0))
    lat2 = pl.BlockSpec((TMV, RWW), lambda i: (nv + i, 0))
    ins = [lat2, lat2, lat3(0), lat3(1),
           pl.BlockSpec((TMV, RWW), lambda i: (nv + i, 0)), pl.BlockSpec((TMV, RWW), lambda i: (nv + i, 2)),
           pl.BlockSpec((TMV, RWW), lambda i: (i, 0)), _full((1, RWW)), _full((1, RWW)), _full((1, RWW))]
    return dm, nv, ins


def _rw_post_fwd(y0_, y1_, kd, sh, p_z, rk, gng, gnb):
    dm, nv, ins = _rw_post_specs()

    def body(y0, y1, k0, k1, r, v, z, rk_ref, g_ref, b_ref, o_ref):
        o_ref[...] = _rw_post_math(y0[...], y1[...], k0[...], k1[...], r[...], v[...], z[...], rk_ref[...], g_ref[...], b_ref[...]).astype(BF16)

    return _call(body, "rw_post_fwd", (T // TMV,), ins, pl.BlockSpec((TMV, RWW), lambda i: (i, 0)), _sds((T, RWW), BF16),
                 sem=("parallel",))(y0_, y1_, kd, kd, sh, sh, p_z, rk, gng, gnb)


def _rw_post_bwd(y0_, y1_, kd, sh, p_z, rk, gng, gnb, dout):
    dm, nv, _ = _rw_post_specs()
    tt = dm["TT"]
    cl = lambda i: jnp.maximum(i - nv, 0)
    all3 = lambda d: pl.BlockSpec((None, TMV, RWW), lambda i: (d, i, 0))
    all2 = pl.BlockSpec((TMV, RWW), lambda i: (i, 0))
    ins = [all2, all2, all3(0), all3(1),
           pl.BlockSpec((TMV, RWW), lambda i: (i, 0)), pl.BlockSpec((TMV, RWW), lambda i: (i, 2)),
           pl.BlockSpec((TMV, RWW), lambda i: (cl(i), 0)), _full((1, RWW)), _full((1, RWW)), _full((1, RWW)),
           pl.BlockSpec((TMV, RWW), lambda i: (cl(i), 0))]
    tok = pl.BlockSpec((TMV, RWW), lambda i: (i, 0))

    def body(y0, y1, k0, k1, r, v, z, rk_ref, g_ref, b_ref, do_ref, dy_ref, dkd_ref, dr_ref, dv_ref, dz_ref, drk_ref, dg_ref, db_ref):
        i = pl.program_id(0)
        lat = jnp.where(i >= nv, 1.0, 0.0)
        _, vjp = jax.vjp(_rw_post_math, y0[...], y1[...], k0[...], k1[...], r[...], v[...], z[...], rk_ref[...], g_ref[...], b_ref[...])
        dy0, _, dk0, _, dr, dv, dz, drk, dg, db = vjp(do_ref[...] * lat)
        dy_ref[...] = dy0
        dkd_ref[...] = dk0
        dr_ref[...] = dr
        dv_ref[...] = dv
        dz_ref[...] = dz.astype(BF16)
        _acc_out((drk_ref, dg_ref, db_ref), i == 0, (drk, dg, db))

    return _call(body, "rw_post_bwd", (tt // TMV,), ins,
                 [tok, tok, tok, tok, pl.BlockSpec((TMV, RWW), lambda i: (cl(i), 0)), _full((1, RWW)), _full((1, RWW)), _full((1, RWW))],
                 [_sds((tt, RWW))] * 4 + [_sds((T, RWW), BF16), _sds((1, RWW)), _sds((1, RWW)), _sds((1, RWW))],
                 sem=("arbitrary",))(y0_, y1_, kd, kd, sh, sh, p_z, rk, gng, gnb, dout)


def _hg_post_fwd(o0_, o1_, p_hg, hg_g):
    dm = _dims()
    nv = dm["NCTV"]
    lat3 = lambda d: pl.BlockSpec((TMV, HGW), lambda i: (nv + i, 0))

    def body(o0, o1, z, g, out):
        out[...] = _hg_post_math(o0[...], o1[...], z[...], g[...]).astype(BF16)

    return _call(body, "hg_post_fwd", (T // TMV,), [lat3(0), lat3(1), pl.BlockSpec((TMV, HGW), lambda i: (nv + i, 4)), _full((1, HGW))],
                 pl.BlockSpec((TMV, HGW), lambda i: (i, 0)), _sds((T, HGW), BF16), sem=("parallel",))(o0_, o1_, p_hg, hg_g)


def _hg_post_bwd(o0_, o1_, p_hg, hg_g, dout):
    dm = _dims()
    nv, tt = dm["NCTV"], dm["TT"]
    all3 = lambda d: pl.BlockSpec((TMV, HGW), lambda i: (i, 0))
    tok = pl.BlockSpec((TMV, HGW), lambda i: (i, 0))

    def body(o0, o1, z, g, do_in, do_ref, dz_ref, dg_ref):
        i = pl.program_id(0)
        lat = jnp.where(i >= nv, 1.0, 0.0)
        _, vjp = jax.vjp(_hg_post_math, o0[...], o1[...], z[...], g[...])
        d0, _, dz, dg = vjp(do_in[...] * lat)
        do_ref[...] = d0
        dz_ref[...] = dz
        _acc_out((dg_ref,), i == 0, (dg,))

    return _call(body, "hg_post_bwd", (tt // TMV,),
                 [all3(0), all3(1), pl.BlockSpec((TMV, HGW), lambda i: (i, 4)), _full((1, HGW)),
                  pl.BlockSpec((TMV, HGW), lambda i: (jnp.maximum(i - nv, 0), 0))],
                 [tok, tok, _full((1, HGW))], [_sds((tt, HGW)), _sds((tt, HGW)), _sds((1, HGW))], sem=("arbitrary",))(o0_, o1_, p_hg, hg_g, dout)


def _hg_dproj(dq0, dq1, di0, di1, df0, df1, dz):
    dm = _dims()
    tt = dm["TT"]
    tok = pl.BlockSpec((TMV, HGW), lambda i: (i, 0))

    def body(q0, q1, i0, i1, f0, f1, z, o_ref):
        o_ref[:, 0:HGW] = (q0[...] + q1[...]).astype(BF16)
        o_ref[:, HGW:2 * HGW] = (i0[...] + i1[...]).astype(BF16)
        o_ref[:, 2 * HGW:3 * HGW] = f0[...].astype(BF16)
        o_ref[:, 3 * HGW:4 * HGW] = f1[...].astype(BF16)
        o_ref[:, 4 * HGW:5 * HGW] = z[...].astype(BF16)

    return _call(body, "hg_dproj", (tt // TMV,), [tok] * 7,
                 pl.BlockSpec((TMV, 5 * HGW), lambda i: (i, 0)), _sds((tt, 5 * HGW), BF16), sem=("parallel",))(
        dq0, dq1, di0, di1, df0, df1, dz)


def _merge1_fwd(out_hg, out_rw, whg_t, wrw_t, p_g):
    tok = pl.BlockSpec((TM, D), lambda i: (i, 0))

    def body(h_ref, r_ref, wh_ref, wr_ref, g1_ref, g2_ref, a_ref, b_ref, m_ref):
        nt = (((1,), (1,)), ((), ()))
        a = lax.dot_general(h_ref[...], wh_ref[...], nt, preferred_element_type=F32)
        b = lax.dot_general(r_ref[...], wr_ref[...], nt, preferred_element_type=F32)
        a_ref[...] = a
        b_ref[...] = b
        m_ref[...] = _gate_math(g1_ref[...], g2_ref[...], a, b).astype(BF16)

    return _call(body, "merge1_fwd", (T // TM,),
                 [pl.BlockSpec((TM, HGW), lambda i: (i, 0)), pl.BlockSpec((TM, RWW), lambda i: (i, 0)), _full((D, HGW)), _full((D, RWW)),
                  pl.BlockSpec((TM, D), lambda i: (i, 0)), pl.BlockSpec((TM, D), lambda i: (i, 1))],
                 [tok, tok, tok], [_sds((T, D)), _sds((T, D)), _sds((T, D), BF16)], sem=("parallel",))(out_hg, out_rw, whg_t, wrw_t, p_g, p_g)


def _head_fwd_bwd(x, merged, w_out, mods, final_g, tgt):
    tok = pl.BlockSpec((TM, D), lambda i: (i, 0))

    def body(x_ref, m_ref, w_ref, mods_ref, fg_ref, t_ref, dxo_ref, dmo_ref, loss_ref, dgate_ref, dfg_ref):
        mo = jnp.dot(m_ref[...], w_ref[...], preferred_element_type=F32)
        gate = mods_ref[0:1, 2 * D:3 * D]
        loss, vjp = jax.vjp(lambda x_, mo_, g_, fg_: _head_math(x_, mo_, g_, fg_, t_ref[...]), x_ref[...], mo, gate, fg_ref[...])
        dx, dmo, dgate, dfg = vjp(jnp.ones((1, 1), F32))
        dxo_ref[...] = dx
        dmo_ref[...] = dmo.astype(BF16)
        _acc_out((loss_ref, dgate_ref, dfg_ref), pl.program_id(0) == 0, (jnp.broadcast_to(loss, (1, 128)), dgate, dfg))

    return _call(body, "head_fwd_bwd", (T // TM,), [tok, tok, _full((D, D)), _full((8, 3 * D)), _full((1, D)), tok],
                 [tok, tok, _full((1, 128)), _full((1, D)), _full((1, D))],
                 [_sds((T, D)), _sds((T, D), BF16), _sds((1, 128)), _sds((1, D)), _sds((1, D))], sem=("arbitrary",))(
        x, merged, w_out, mods, final_g, tgt)


def _merge_bwd(dmo, w_out, p_g, a, b):
    tok = pl.BlockSpec((TM, D), lambda i: (i, 0))

    def body(dmo_ref, w_ref, g1_ref, g2_ref, a_ref, b_ref, da_ref, db_ref, dg_ref):
        dm_ = lax.dot_general(dmo_ref[...], w_ref[...], (((1,), (1,)), ((), ())), preferred_element_type=F32)
        _, vjp = jax.vjp(_gate_math, g1_ref[...], g2_ref[...], a_ref[...], b_ref[...])
        dg1, dg2, da, db = vjp(dm_)
        da_ref[...] = da.astype(BF16)
        db_ref[...] = db.astype(BF16)
        dg_ref[:, 0:D] = dg1.astype(BF16)
        dg_ref[:, D:2 * D] = dg2.astype(BF16)

    return _call(body, "merge_bwd", (T // TM,),
                 [tok, _full((D, D)), pl.BlockSpec((TM, D), lambda i: (i, 0)), pl.BlockSpec((TM, D), lambda i: (i, 1)), tok, tok],
                 [tok, tok, pl.BlockSpec((TM, 2 * D), lambda i: (i, 0))], [_sds((T, D), BF16), _sds((T, D), BF16), _sds((T, 2 * D), BF16)],
                 sem=("parallel",))(dmo, w_out, p_g, p_g, a, b)


def _h_bwd(xcat, mods, norm_g, dh_hg, dh_rw, dh_z, dh_g, dxo, dgate):
    dm = _dims()
    nct, tt = dm["NCT"], dm["TT"]
    tok = pl.BlockSpec((TM, D), lambda i: (i, 0))
    lat = pl.BlockSpec((TM, D), lambda i: (jnp.maximum(i - nct, 0), 0))

    def body(x_ref, m_ref, g_ref, d1, d2, d3, d4, dxo_ref, dgate_ref, gx_ref, dng_ref, dmods_ref):
        i = pl.program_id(0)
        is_ctx = i < nct
        latf = jnp.where(is_ctx, 0.0, 1.0)
        dh = d1[...] + d2[...] + latf * (d3[...] + d4[...])
        _, vjp = jax.vjp(_h_math, x_ref[...], g_ref[...], _sel_mod(m_ref, is_ctx, D), _sel_mod(m_ref, is_ctx, 0))
        dx, dng, dscale, dshift = vjp(dh)
        gx_ref[...] = dx + dxo_ref[...]

        @pl.when(i == 0)
        def _():
            dng_ref[...] = jnp.zeros_like(dng_ref)
            dmods_ref[...] = jnp.zeros_like(dmods_ref)
            dmods_ref[0:1, 2 * D:3 * D] = dgate_ref[...]

        dng_ref[...] += dng
        row = lax.broadcasted_iota(jnp.int32, (8, 1), 0)
        sel = jnp.where(row == jnp.where(is_ctx, 1, 0), 1.0, 0.0)
        dmods_ref[:, 0:D] += sel * dshift
        dmods_ref[:, D:2 * D] += sel * dscale

    return _call(body, "h_bwd", (tt // TM,), [tok, _full((8, 3 * D)), _full((1, D)), tok, tok, lat, lat, lat, _full((1, D))],
                 [lat, _full((1, D)), _full((8, 3 * D))], [_sds((T, D)), _sds((1, D)), _sds((8, 3 * D))], sem=("arbitrary",))(
        xcat, mods, norm_g, dh_hg, dh_rw, dh_z, dh_g, dxo, dgate)


def _lb_math(l0, l1):
    return jax.nn.sigmoid(l0 - l1)


def _lb_fwd(l0, l1):
    def body(a_ref, b_ref, o_ref):
        o_ref[...] = _lb_math(a_ref[...], b_ref[...])

    return _call(body, "lb_fwd", (1,), [_full((2, HGW))] * 2, _full((2, HGW)), _sds((2, HGW)))(l0, l1)


def _lb_bwd(l0, l1, dlb):
    def body(a_ref, b_ref, d_ref, da_ref, db_ref):
        _, vjp = jax.vjp(_lb_math, a_ref[...], b_ref[...])
        da_ref[...], db_ref[...] = vjp(d_ref[...])

    return _call(body, "lb_bwd", (1,), [_full((2, HGW))] * 3, [_full((2, HGW))] * 2, [_sds((2, HGW))] * 2)(l0, l1, dlb)


def _local_step(x, ctx, tgt, mods, w):
    dm = _dims()
    tt, sh_w, hgc = dm["TT"], dm["SH"], dm["HGC"]
    nb = lambda cols: cols // WB
    xcat = jnp.concatenate([ctx, x], axis=0)
    h = _h_fwd(xcat, mods, w["norm_g"])
    win_t = w["win_t"]
    p_hg = _proj_nt(h, win_t, 0, nb(hgc), 0, tt, "proj_hg")
    p_rw = _proj_nt(h, win_t, nb(hgc), nb(sh_w), 0, tt, "proj_rw")
    p_z = _proj_nt(h, win_t, nb(hgc + sh_w), nb(RWW), TC, T, "proj_z")
    p_g = _proj_nt(h, win_t, nb(hgc + sh_w + RWW), nb(2 * D), TC, T, "proj_g")

    lb3 = _lb_fwd(w["lb0"], w["lb1"]).reshape(2, 1, HGW)
    o0, o1, hg_ck = _hg_scan_fwd(p_hg, lb3)
    out_hg = _hg_post_fwd(o0, o1, p_hg, w["hg_g"])

    sh = _shift_fwd(p_rw, w["mu"])
    zpad = jnp.zeros((LORA, RWW), F32)
    prep_w = (w["w0"], jnp.concatenate([w["w2"][0], zpad], 0), jnp.concatenate([zpad, w["w2"][1]], 0),
              w["a0"], jnp.concatenate([w["a2"][0], zpad], 0), jnp.concatenate([zpad, w["a2"][1]], 0), w["kk"], w["ka"])
    kap, lw, b, kd = _rw_prep_fwd(sh, *prep_w)
    y0, y1, rw_ck = _rw_scan_fwd(sh, kap, lw, b, kd)
    post_w = (w["rk"], w["gng"], w["gnb"])
    out_rw = _rw_post_fwd(y0, y1, kd, sh, p_z, *post_w)

    a, bb, merged = _merge1_fwd(out_hg, out_rw, w["whg_t"], w["wrw_t"], p_g)
    dxo, dmo, loss, dgate, d_fg = _head_fwd_bwd(x, merged, w["wout"], mods, w["final_g"], tgt)
    da, db, dp_g = _merge_bwd(dmo, w["wout"], p_g, a, bb)
    g_wout = _dw_tn(merged, dmo, 0, D, 0, "dw_out")
    g_whg = _dw_tn(da, out_hg, 0, D, 0, "dw_hg")
    g_wrw = _dw_tn(db, out_rw, 0, D, 0, "dw_rw")
    d_out_hg = _mm_nn(da, w["whg_t"], 0, "dx_hg")
    d_out_rw = _mm_nn(db, w["wrw_t"], 0, "dx_rw")

    do, dz_hg, d_hg_g = _hg_post_bwd(o0, o1, p_hg, w["hg_g"], d_out_hg)
    *hg_grads, dlb3 = _hg_scan_bwd(p_hg, lb3, hg_ck, do)
    dp_hg = _hg_dproj(*hg_grads, dz_hg)
    d_lb0, d_lb1 = _lb_bwd(w["lb0"], w["lb1"], dlb3.reshape(2, HGW))

    dy, dkd_p, dr_p, dv_p, dp_z, d_rk, d_gng, d_gnb = _rw_post_bwd(y0, y1, kd, sh, p_z, *post_w, d_out_rw)
    rw_grads = _rw_scan_bwd(sh, kap, lw, b, kd, rw_ck, dy)
    dsh, d_w0, d_w2p0, d_w2p1, d_a0, d_a2p0, d_a2p1, d_kk, d_ka = _rw_prep_bwd(sh, *prep_w, rw_grads, dkd_p, dr_p, dv_p)
    dp_rw, d_mu = _shift_bwd(p_rw, dsh, w["mu"])

    g_win = _dw_tn(dp_hg, h, 0, dm["NCOLS"], 0, "dw_in_hg")
    g_win = _dw_tn(dp_rw, h, 0, dm["NCOLS"], nb(hgc), "dw_in_rw", prev=g_win)
    g_win = _dw_tn(dp_z, h, TC, dm["NCOLS"], nb(hgc + sh_w), "dw_in_z", prev=g_win)
    g_win = _dw_tn(dp_g, h, TC, dm["NCOLS"], nb(hgc + sh_w + RWW), "dw_in_g", prev=g_win)
    dh_hg = _mm_nn(dp_hg, win_t, 0, "dh_hg")
    dh_rw = _mm_nn(dp_rw, win_t, nb(hgc), "dh_rw")
    dh_z = _mm_nn(dp_z, win_t, nb(hgc + sh_w), "dh_z")
    dh_g = _mm_nn(dp_g, win_t, nb(hgc + sh_w + RWW), "dh_g")
    gx, d_ng, dmods = _h_bwd(xcat, mods, w["norm_g"], dh_hg, dh_rw, dh_z, dh_g, dxo, dgate)

    big = dict(win_t=g_win, wout=g_wout, whg_t=g_whg, wrw_t=g_wrw)
    small = dict(norm_g=d_ng, lb0=d_lb0, lb1=d_lb1, hg_g=d_hg_g, mu=d_mu, w0=d_w0,
                 w2=jnp.stack([d_w2p0[:LORA], d_w2p1[LORA:]]), a0=d_a0, a2=jnp.stack([d_a2p0[:LORA], d_a2p1[LORA:]]),
                 kk=d_kk, ka=d_ka, rk=d_rk, gng=d_gng, gnb=d_gnb, final_g=d_fg)
    return loss, gx, dmods, big, small


MESH = pl.DeviceIdType.MESH


def _comm_call(body, name, bufs, out_shapes, nsem, nloc):
    hbm = pl.BlockSpec(memory_space=pl.ANY)
    return pl.pallas_call(
        body, name=name, in_specs=[hbm] * len(bufs), out_specs=[hbm] * len(out_shapes), out_shape=out_shapes,
        scratch_shapes=[pltpu.SemaphoreType.DMA((nsem,)), pltpu.SemaphoreType.DMA((nsem,)), pltpu.SemaphoreType.DMA((nloc,))],
    )(*bufs)


def _gather2(bufs, name):
    nbuf = len(bufs)

    def body(*refs):
        ins, outs = refs[:nbuf], refs[nbuf:2 * nbuf]
        send_sems, recv_sems, loc_sems = refs[2 * nbuf:]
        x, y, c = lax.axis_index("x"), lax.axis_index("y"), lax.axis_index("c")
        me, sib = (x, y, c), (x, y, 1 - c)
        chips = [(1 - x, y), (x, 1 - y), (1 - x, 1 - y)]

        def copy(bi, k, block, to, src=None):
            rows = outs[bi].at[4 * block[0] + 2 * block[1] + block[2]]
            return pltpu.make_async_remote_copy(src_ref=rows if src is None else src, dst_ref=rows, send_sem=send_sems.at[7 * bi + k],
                                                recv_sem=recv_sems.at[7 * bi + k], device_id=to, device_id_type=MESH)

        own = [pltpu.make_async_copy(ins[bi], outs[bi].at[4 * x + 2 * y + c], loc_sems.at[bi]) for bi in range(nbuf)]
        for cp in own:
            cp.start()
        first = []
        for bi in range(nbuf):
            first.append(copy(bi, 0, me, sib, src=ins[bi]))
            first += [copy(bi, 1 + j, me, (*chip, c), src=ins[bi]) for j, chip in enumerate(chips)]
        for cp in first:
            cp.start()
        passed = []
        for j, chip in enumerate(chips):
            for bi in range(nbuf):
                copy(bi, 1 + j, (*chip, c), me).wait_recv()
                passed.append(copy(bi, 4 + j, (*chip, c), sib))
                passed[-1].start()
        for bi in range(nbuf):
            copy(bi, 0, sib, me).wait_recv()
            for j, chip in enumerate(chips):
                copy(bi, 4 + j, (*chip, 1 - c), me).wait_recv()
        for cp in first + passed:
            cp.wait_send()
        for cp in own:
            cp.wait()

    return _comm_call(body, name, bufs, [_sds((NDEV,) + b.shape, b.dtype) for b in bufs], 7 * nbuf, nbuf)


def _pair_exchange(bufs, name):
    nbuf = len(bufs)

    def body(*refs):
        ins, got = refs[:nbuf], refs[nbuf:2 * nbuf]
        send_sems, recv_sems, _ = refs[2 * nbuf:]
        x, y, c = lax.axis_index("x"), lax.axis_index("y"), lax.axis_index("c")
        copies = []
        for bi in range(nbuf):
            for q in range(4):
                copies.append(pltpu.make_async_remote_copy(
                    src_ref=ins[bi].at[2 * q + 1 - c], dst_ref=got[bi].at[q], send_sem=send_sems.at[4 * bi + q],
                    recv_sem=recv_sems.at[4 * bi + q], device_id=(x, y, 1 - c), device_id_type=MESH))
                copies[-1].start()
        for cp in copies:
            cp.wait_send()
            cp.wait_recv()

    return _comm_call(body, name, bufs, [_sds((4,) + b.shape[1:], b.dtype) for b in bufs], 4 * nbuf, 1)


def _chip_exchange(bufs, name):
    nbuf = len(bufs)

    def body(*refs):
        ins, outs = refs[:nbuf], refs[nbuf:2 * nbuf]
        send_sems, recv_sems, _ = refs[2 * nbuf:]
        x, y, c = lax.axis_index("x"), lax.axis_index("y"), lax.axis_index("c")
        myq = 2 * x + y
        copies = []
        for bi in range(nbuf):
            for j, (qx, qy) in enumerate([(1 - x, y), (x, 1 - y), (1 - x, 1 - y)]):
                q = 2 * qx + qy
                send = pltpu.make_async_remote_copy(src_ref=ins[bi].at[q], dst_ref=outs[bi].at[myq], send_sem=send_sems.at[3 * bi + j],
                                                    recv_sem=recv_sems.at[3 * bi + j], device_id=(qx, qy, c), device_id_type=MESH)
                send.start()
                recv = pltpu.make_async_remote_copy(src_ref=ins[bi].at[myq], dst_ref=outs[bi].at[q], send_sem=send_sems.at[3 * bi + j],
                                                    recv_sem=recv_sems.at[3 * bi + j], device_id=(qx, qy, c), device_id_type=MESH)
                copies.append((send, recv))
        for send, recv in copies:
            send.wait_send()
            recv.wait_recv()

    return _comm_call(body, name, bufs, [_sds(b.shape, b.dtype) for b in bufs], 3 * nbuf, 1)


def _prefetch_call(body, name, scalars, grid, in_specs, out_specs, out_shape, args):
    return pl.pallas_call(
        body, name=name, out_shape=out_shape,
        grid_spec=pltpu.PrefetchScalarGridSpec(num_scalar_prefetch=1, grid=grid, in_specs=in_specs, out_specs=out_specs),
        compiler_params=pltpu.CompilerParams(dimension_semantics=("parallel",) * len(grid), vmem_limit_bytes=V7X_VMEM_LIMIT))(scalars, *args)


def _pair_add(g, got, core, name):
    _, rows, cols = got.shape
    tr = _row_tile(rows, 16, 256)

    def body(c_ref, a_ref, b_ref, o_ref):
        o_ref[...] = (a_ref[...].astype(F32) + b_ref[...].astype(F32)).astype(o_ref.dtype)

    blk = pl.BlockSpec((4, tr, cols), lambda i, c_ref: (0, i, 0))
    mine = pl.BlockSpec((4, None, tr, cols), lambda i, c_ref: (0, c_ref[0], i, 0))
    return _prefetch_call(body, name, core, (rows // tr,), [mine, blk], blk, _sds(got.shape, got.dtype),
                          (g.reshape(4, 2, rows, cols), got))


def _reduce_chips(chip_sums, recv, slots, name):
    _, rows, cols = recv.shape
    tr = _row_tile(rows, 16, 128)

    def body(s_ref, own_ref, r1_ref, r2_ref, r3_ref, o_ref):
        o_ref[...] = ((own_ref[...].astype(F32) + r1_ref[...].astype(F32)) + r2_ref[...].astype(F32)) + r3_ref[...].astype(F32)

    pick = lambda k: pl.BlockSpec((None, tr, cols), lambda i, s_ref: (s_ref[k], i, 0))
    return _prefetch_call(body, name, slots, (rows // tr,), [pick(0), pick(1), pick(2), pick(3)],
                          pl.BlockSpec((tr, cols), lambda i, s_ref: (i, 0)), _sds((rows, cols)), (chip_sums, recv, recv, recv))


def _row_tile(rows, mult, cap):
    best = mult
    for t in range(mult, cap + 1, mult):
        if rows % t == 0:
            best = t
    assert rows % best == 0, (rows, mult)
    return best


def _reduce_sources(r, name):
    nsrc, rows, cols = r.shape
    tr = _row_tile(rows, 16 if r.dtype.itemsize == 2 else 8, 128)

    def body(r_ref, o_ref):
        acc = r_ref[0].astype(F32)
        for j in range(1, nsrc):
            acc = acc + r_ref[j].astype(F32)
        o_ref[...] = acc

    return _call(body, name, (rows // tr,), [pl.BlockSpec((nsrc, tr, cols), lambda i: (0, i, 0))],
                 pl.BlockSpec((tr, cols), lambda i: (i, 0)), _sds((rows, cols)), sem=("parallel",))(r)


def _adamw(w, g, m, v, name):
    rows, cols = w.shape
    tr = _row_tile(rows, 8, 128)
    c1 = 1.0 - ADAM_B1 ** ADAM_STEP
    c2 = 1.0 - ADAM_B2 ** ADAM_STEP

    def body(w_ref, g_ref, m_ref, v_ref, d_ref, mo_ref, vo_ref):
        g_ = g_ref[...]
        m_ = ADAM_B1 * m_ref[...] + (1.0 - ADAM_B1) * g_
        v_ = ADAM_B2 * v_ref[...] + (1.0 - ADAM_B2) * (g_ * g_)
        d_ref[...] = -ADAM_LR * ((m_ / c1) / (jnp.sqrt(v_ / c2) + ADAM_EPS) + ADAM_WD * w_ref[...])
        mo_ref[...] = m_
        vo_ref[...] = v_

    blk = pl.BlockSpec((tr, cols), lambda i: (i, 0))
    return _call(body, name, (rows // tr,), [blk] * 4, [blk] * 3, [_sds((rows, cols))] * 3, sem=("parallel",))(w, g, m, v)


SLAB_PART = 1024


def _pack(arrs):
    parts = []
    for a in arrs:
        flat = a.reshape(-1)
        pad = (-flat.shape[0]) % SLAB_PART
        if pad:
            flat = jnp.concatenate([flat, jnp.zeros((pad,), flat.dtype)])
        parts.append(flat.reshape(-1, 128))
    return jnp.concatenate(parts, axis=0)


def _unpack(slab, shapes):
    outs, row = [], 0
    for s in shapes:
        n = 1
        for d in s:
            n *= d
        rows = (n + SLAB_PART - 1) // SLAB_PART * (SLAB_PART // 128)
        outs.append(slab[row:row + rows].reshape(-1)[:n].reshape(s))
        row += rows
    return outs


def _unshard_last(g, lead):
    nl = len(lead)
    return jnp.transpose(g, tuple(range(1, nl + 1)) + (0, nl + 1)).reshape(tuple(lead) + (-1,))


def kernel(x, c, ctx, c_ctx, ada_w, ada_b, norm_g, w_in, hg_lb, hg_norm_g, rw_mu, rw_w0, rw_w2, rw_a0, rw_a2, rw_kk, rw_ka, rw_rk, rw_gn_g, rw_gn_b, w_hg_out, w_rw_out, w_out, final_g, loss_target, m_c_ctx, m_ada_w, m_ada_b, m_norm_g, m_w_in, m_hg_lb, m_hg_norm_g, m_rw_mu, m_rw_w0, m_rw_w2, m_rw_a0, m_rw_a2, m_rw_kk, m_rw_ka, m_rw_rk, m_rw_gn_g, m_rw_gn_b, m_w_hg_out, m_w_rw_out, m_w_out, m_final_g, v_c_ctx, v_ada_w, v_ada_b, v_norm_g, v_w_in, v_hg_lb, v_hg_norm_g, v_rw_mu, v_rw_w0, v_rw_w2, v_rw_a0, v_rw_a2, v_rw_kk, v_rw_ka, v_rw_rk, v_rw_gn_g, v_rw_gn_b, v_w_hg_out, v_w_rw_out, v_w_out, v_final_g):
    dm = _dims()
    me = 4 * lax.axis_index("x") + 2 * lax.axis_index("y") + lax.axis_index("c")

    big_shards = [w_in[0].T.astype(BF16), w_out[0].astype(BF16), w_hg_out[0].T.astype(BF16), w_rw_out[0].T.astype(BF16)]
    sharded_small = [hg_lb, rw_mu[0], rw_w0[0], rw_w2[0], rw_a0[0], rw_a2[0]]
    c_rows = jnp.concatenate([c, jnp.zeros((7, D), F32)], axis=0)
    gathered = _gather2(big_shards + [_pack(sharded_small), c_rows], "gather_weights")
    win_t, wout, whg_t, wrw_t = [g.reshape(-1, g.shape[2]) for g in gathered[:4]]
    per_dev = jax.vmap(lambda s: _unpack(s, [a.shape for a in sharded_small]))(gathered[4])
    hg_lb_f, mu_f, w0_f, w2_f, a0_f, a2_f = [_unshard_last(p, p.shape[1:-1]) for p in per_dev]

    ncol = ada_w.shape[2]
    cc16 = jnp.concatenate([gathered[5][:, 0], c_ctx[None], jnp.zeros((7, D), F32)], axis=0)
    mod_all = _gather2([_ada_fwd(cc16, ada_w[0])], "gather_mods")[0]
    mod_x = lax.dynamic_index_in_dim(mod_all, me, axis=1, keepdims=False).reshape(1, -1) + ada_b
    mod_c = mod_all[:, NDEV].reshape(1, -1) + ada_b
    mods = jnp.concatenate([mod_x, mod_c, jnp.zeros((6, 3 * D), F32)], axis=0)

    w = dict(win_t=win_t, wout=wout, whg_t=whg_t, wrw_t=wrw_t, norm_g=norm_g,
             lb0=hg_lb_f[:, 0], lb1=hg_lb_f[:, 1], hg_g=hg_norm_g, mu=mu_f, w0=w0_f, w2=w2_f, a0=a0_f, a2=a2_f,
             kk=rw_kk, ka=rw_ka, rk=rw_rk, gng=rw_gn_g, gnb=rw_gn_b, final_g=final_g[None])
    loss_dev, grad_x, dmods, big, small = _local_step(x[0], ctx[0], loss_target[0], mods, w)
    loss = lax.psum(loss_dev[0, 0], AXES)

    dmods_all = _gather2([dmods], "gather_dmods")[0]
    my_cols = lambda a: lax.dynamic_slice_in_dim(a, me * ncol, ncol, axis=1)
    g_ada, dcc16, d_ada_b = _ada_bwd(cc16, ada_w[0], my_cols(dmods_all[:, 0]), my_cols(dmods_all[:, 1]), dmods)
    small = dict(small, c_ctx=dcc16[NDEV:NDEV + 1], ada_b=d_ada_b)

    small_names = ["c_ctx", "ada_b", "norm_g", "lb0", "lb1", "hg_g", "mu", "w0", "w2", "a0", "a2", "kk", "ka", "rk", "gng", "gnb", "final_g"]
    small_slab = _pack([small[n] for n in small_names])
    big_names = ["win_t", "wout", "whg_t", "wrw_t"]
    blocks = [big[n].reshape(NDEV, -1, big[n].shape[1]) for n in big_names]
    got = _pair_exchange(blocks, "pair_grads")
    core = lax.axis_index("c").astype(jnp.int32).reshape(1)
    chip_sums = [_pair_add(g, r, core, "pair_add_" + n) for g, r, n in zip(blocks, got, big_names)]
    recv = _chip_exchange(chip_sums, "chip_grads")
    small_all = _gather2([small_slab], "gather_small_grads")[0]
    my_chip = (2 * lax.axis_index("x") + lax.axis_index("y")).astype(jnp.int32)
    others = jnp.arange(3, dtype=jnp.int32)
    slots = jnp.concatenate([my_chip.reshape(1), others + (others >= my_chip).astype(jnp.int32)])
    g_win_t, g_wout, g_whg_t, g_wrw_t = [_reduce_chips(cs, r, slots, "reduce_" + n) for cs, r, n in zip(chip_sums, recv, big_names)]
    sg = dict(zip(small_names, _unpack(_reduce_sources(small_all, "reduce_small"), [small[n].shape for n in small_names])))

    def my_shard(full):
        n = full.shape[-1] // NDEV
        return lax.dynamic_slice_in_dim(full, me * n, n, axis=full.ndim - 1)

    grads = dict(
        c_ctx=sg["c_ctx"][0], ada_w=g_ada[None], ada_b=sg["ada_b"], norm_g=sg["norm_g"], w_in=g_win_t.T[None],
        hg_lb=my_shard(jnp.stack([sg["lb0"], sg["lb1"]], axis=1)), hg_norm_g=sg["hg_g"], rw_mu=my_shard(sg["mu"])[None],
        rw_w0=my_shard(sg["w0"])[None], rw_w2=my_shard(sg["w2"])[None], rw_a0=my_shard(sg["a0"])[None], rw_a2=my_shard(sg["a2"])[None],
        rw_kk=sg["kk"], rw_ka=sg["ka"], rw_rk=sg["rk"], rw_gn_g=sg["gng"], rw_gn_b=sg["gnb"],
        w_hg_out=g_whg_t.T[None], w_rw_out=g_wrw_t.T[None], w_out=g_wout[None], final_g=sg["final_g"][0])
    weights = dict(c_ctx=c_ctx, ada_w=ada_w, ada_b=ada_b, norm_g=norm_g, w_in=w_in, hg_lb=hg_lb, hg_norm_g=hg_norm_g, rw_mu=rw_mu,
                   rw_w0=rw_w0, rw_w2=rw_w2, rw_a0=rw_a0, rw_a2=rw_a2, rw_kk=rw_kk, rw_ka=rw_ka, rw_rk=rw_rk, rw_gn_g=rw_gn_g,
                   rw_gn_b=rw_gn_b, w_hg_out=w_hg_out, w_rw_out=w_rw_out, w_out=w_out, final_g=final_g)
    m_in = dict(zip(weights, (m_c_ctx, m_ada_w, m_ada_b, m_norm_g, m_w_in, m_hg_lb, m_hg_norm_g, m_rw_mu, m_rw_w0, m_rw_w2, m_rw_a0,
                              m_rw_a2, m_rw_kk, m_rw_ka, m_rw_rk, m_rw_gn_g, m_rw_gn_b, m_w_hg_out, m_w_rw_out, m_w_out, m_final_g)))
    v_in = dict(zip(weights, (v_c_ctx, v_ada_w, v_ada_b, v_norm_g, v_w_in, v_hg_lb, v_hg_norm_g, v_rw_mu, v_rw_w0, v_rw_w2, v_rw_a0,
                              v_rw_a2, v_rw_kk, v_rw_ka, v_rw_rk, v_rw_gn_g, v_rw_gn_b, v_w_hg_out, v_w_rw_out, v_w_out, v_final_g)))

    big_w = ["ada_w", "w_in", "w_hg_out", "w_rw_out", "w_out"]
    delta, new_m, new_v = {}, {}, {}
    for n in big_w:
        shp = weights[n].shape
        two = lambda a: a.reshape(shp[-2], shp[-1])
        d_, m_, v_ = _adamw(two(weights[n]), two(grads[n]), two(m_in[n]), two(v_in[n]), "adamw_" + n)
        delta[n], new_m[n], new_v[n] = d_.reshape(shp), m_.reshape(shp), v_.reshape(shp)
    rest = [n for n in weights if n not in big_w]
    shapes = [weights[n].shape for n in rest]
    d_s, m_s, v_s = _adamw(_pack([weights[n] for n in rest]), _pack([grads[n] for n in rest]), _pack([m_in[n] for n in rest]),
                           _pack([v_in[n] for n in rest]), "adamw_small")
    for n, d_, m_, v_ in zip(rest, _unpack(d_s, shapes), _unpack(m_s, shapes), _unpack(v_s, shapes)):
        delta[n], new_m[n], new_v[n] = d_, m_, v_

    order = list(weights)
    return (loss, grad_x[None], *[grads[n] for n in order], *[delta[n] for n in order],
            *[new_m[n] for n in order], *[new_v[n] for n in order])
```

```python
import functools

import jax
import jax.numpy as jnp
from jax import lax
from jax.experimental import pallas as pl
from jax.experimental.pallas import tpu as pltpu

D = 2048
T = 2048
TC = 256
GW = 64
C = 64
HGW = 1024
RWW = 1024
LORA = 64
NDEV = 8
TM = 256
TMV = 128
NORM_EPS = 1e-6
RW_GN_EPS = 64e-5
ADAM_LR, ADAM_B1, ADAM_B2, ADAM_EPS, ADAM_WD, ADAM_STEP = 0.001, 0.9, 0.999, 1e-08, 0.01, 10
WB = 256
V7X_VMEM_LIMIT = 56 * 1024 * 1024

F32 = jnp.float32
BF16 = jnp.bfloat16
AXES = ("x", "y", "c")


def _dims():
    tt = T + TC
    sh = 3 * RWW + 4 * LORA
    hgc = 5 * HGW
    return dict(TT=tt, NCC=TC // C, NCH=tt // C, HG_H=HGW // 128, RW_P=RWW // 128, SH=sh, HGC=hgc,
                NCOLS=hgc + sh + RWW + 2 * D, NCT=TC // TM, NCTV=TC // TMV)


def _mm(a, b):
    return _dot3(a, b, "nn")


def _mm_nt(a, b):
    return _dot3(a, b, "nt")


def _mm_tn(a, b):
    return _dot3(a, b, "tn")


def _split(x):
    hi = x.astype(BF16)
    return hi, (x - hi.astype(F32)).astype(BF16)


_FORMS = {"nn": (((1,), (0,)), ((), ())), "nt": (((1,), (1,)), ((), ())), "tn": (((0,), (0,)), ((), ()))}


def _dot3_raw(a, b, form):
    ah, al = _split(a)
    bh, bl = _split(b)
    d = lambda x, y: lax.dot_general(x, y, _FORMS[form], preferred_element_type=F32)
    return d(ah, bh) + (d(ah, bl) + d(al, bh))


@functools.partial(jax.custom_vjp, nondiff_argnums=(2,))
def _dot3(a, b, form):
    return _dot3_raw(a, b, form)


def _dot3_fwd(a, b, form):
    return _dot3_raw(a, b, form), (a, b)


def _dot3_bwd(form, res, g):
    a, b = res
    if form == "nn":
        return _dot3(g, b, "nt"), _dot3(a, g, "tn")
    if form == "nt":
        return _dot3(g, b, "nn"), _dot3(g, a, "tn")
    return _dot3(b, g, "nt"), _dot3(a, g, "nn")


_dot3.defvjp(_dot3_fwd, _dot3_bwd)


def _scan_cumsum(inc, x):
    return _cumsum_vjp(inc.astype(BF16), x)


def _cumsum_raw(inc, x, form):
    h1 = x.astype(BF16)
    r1 = x - h1.astype(F32)
    h2 = r1.astype(BF16)
    h3 = (r1 - h2.astype(F32)).astype(BF16)
    d = lambda y: lax.dot_general(inc, y, _FORMS[form], preferred_element_type=F32)
    return d(h1) + (d(h2) + d(h3))


@jax.custom_vjp
def _cumsum_vjp(inc, x):
    return _cumsum_raw(inc, x, "nn")


_cumsum_vjp.defvjp(lambda inc, x: (_cumsum_raw(inc, x, "nn"), inc),
                   lambda inc, g: (jnp.zeros_like(inc), _cumsum_raw(inc, g, "tn")))


def _silu(x):
    return x * jax.nn.sigmoid(x)


def _softplus(x):
    return jnp.maximum(x, 0.0) + jnp.log(1.0 + jnp.exp(-jnp.abs(x)))


def _iota2(shape, dim):
    return lax.broadcasted_iota(jnp.int32, shape, dim)


def _pair_mask():
    return (_iota2((128, 128), 0) < 64) == (_iota2((128, 128), 1) < 64)


def _seg64_sum(x):
    e = _pair_mask().astype(F32)
    parts = [_mm(x[:, g * 128:(g + 1) * 128], e) for g in range(x.shape[1] // 128)]
    return parts[0] if len(parts) == 1 else jnp.concatenate(parts, axis=1)


def _seg128_mean(x):
    parts = [jnp.broadcast_to(jnp.mean(x[:, g * 128:(g + 1) * 128], axis=1, keepdims=True), (x.shape[0], 128))
             for g in range(x.shape[1] // 128)]
    return parts[0] if len(parts) == 1 else jnp.concatenate(parts, axis=1)


def _scan_masks(d):
    lag = (_iota2((C, C), 0) - _iota2((C, C), 1)) * (1 - 2 * d)
    return lag >= 0, lag > 0


def _tri_inverse(a):
    eye = (_iota2((C, C), 0) == _iota2((C, C), 1)).astype(F32)
    p = [-x for x in a]
    tm = [eye + x for x in p]
    n = 2
    while n < C:
        p = [_mm(x, x) for x in p]
        tm = [t + _mm(t, x) for t, x in zip(tm, p)]
        n *= 2
    return tm


@jax.custom_vjp
def _tri_solve(a, rhs):
    return [_mm(t, r) for t, r in zip(_tri_inverse(a), rhs)]


def _tri_solve_fwd(a, rhs):
    tm = _tri_inverse(a)
    u = [_mm(t, r) for t, r in zip(tm, rhs)]
    return u, (tm, u)


def _tri_solve_bwd(res, g):
    tm, u = res
    d_rhs = [_mm_tn(t, x) for t, x in zip(tm, g)]
    return [-_mm_nt(d, x) for d, x in zip(d_rhs, u)], d_rhs


_tri_solve.defvjp(_tri_solve_fwd, _tri_solve_bwd)


def _rw_chunks(chains):
    s0, r, lw, kap, b, v, kd, ds = zip(*chains)
    ids = range(len(chains))
    inc, strict = zip(*[_scan_masks(d) for d in ds])
    lane = _iota2((1, 128), 1)
    hm = ((lane < 64).astype(F32), (lane >= 64).astype(F32))
    eye = (_iota2((C, C), 0) == _iota2((C, C), 1)).astype(F32)
    lc = [_scan_cumsum(inc[i], lw[i]) for i in ids]
    ltot = [jnp.sum(lw[i], axis=0, keepdims=True) for i in ids]
    rt = [r[i] * jnp.exp(lc[i]) for i in ids]
    kt = [kap[i] * jnp.exp(lc[i] - lw[i]) for i in ids]
    einv = [jnp.exp(-lc[i]) for i in ids]
    kh = [kd[i] * einv[i] for i in ids]
    bh = [b[i] * einv[i] for i in ids]
    eend = [jnp.exp(ltot[i] - lc[i]) for i in ids]
    kbar = [kd[i] * eend[i] for i in ids]
    bbar = [b[i] * eend[i] for i in ids]
    lhs = [jnp.concatenate([kt[i] * hm[0], kt[i] * hm[1], rt[i] * hm[0], rt[i] * hm[1]], axis=0) for i in ids]
    gk = [_mm_nt(lhs[i], kh[i]) for i in ids]
    gb = [_mm_nt(lhs[i], bh[i]) for i in ids]
    ks = [_mm_nt(kt[i], s0[i]) for i in ids]
    ys = [_mm_nt(rt[i], s0[i]) for i in ids]
    heads = [(i, h) for i in ids for h in range(2)]
    akk = {c: jnp.where(strict[c[0]], gk[c[0]][c[1] * C:(c[1] + 1) * C], 0.0) for c in heads}
    akb = {c: jnp.where(strict[c[0]], gb[c[0]][c[1] * C:(c[1] + 1) * C], 0.0) for c in heads}
    ark = {c: jnp.where(inc[c[0]], gk[c[0]][(2 + c[1]) * C:(3 + c[1]) * C], 0.0) for c in heads}
    arb = {c: jnp.where(inc[c[0]], gb[c[0]][(2 + c[1]) * C:(3 + c[1]) * C], 0.0) for c in heads}
    vh = {c: v[c[0]] * hm[c[1]] for c in heads}
    av = {c: _mm(akk[c], vh[c]) for c in heads}
    rhs = {c: ks[c[0]] * hm[c[1]] + av[c] for c in heads}
    uh = dict(zip(heads, _tri_solve([akb[c] for c in heads], [rhs[c] for c in heads])))
    yv = {c: _mm(ark[c], vh[c]) for c in heads}
    yu = {c: _mm(arb[c], uh[c]) for c in heads}
    u = [uh[i, 0] + uh[i, 1] for i in ids]
    y = [ys[i] + (yv[i, 0] - yu[i, 0]) + (yv[i, 1] - yu[i, 1]) for i in ids]
    upd = [_mm_tn(jnp.concatenate([v[i], -u[i]], axis=0), jnp.concatenate([kbar[i], bbar[i]], axis=0)) for i in ids]
    s1 = [s0[i] * jnp.exp(ltot[i]) + jnp.where(_pair_mask(), upd[i], 0.0) for i in ids]
    return [(y[i], s1[i]) for i in ids]


HG_SUB = 16


def _hg_chunks(chains):
    s0, qr, iv, f, lb, ds = zip(*chains)
    ids = range(len(chains))
    inc = [_scan_masks(d)[0] for d in ds]
    q = [_silu(qr[i]) for i in ids]
    fg = [lb[i] + (1.0 - lb[i]) * jax.nn.sigmoid(f[i]) for i in ids]
    k = [1.0 - fg[i] for i in ids]
    g = [jnp.log(fg[i]) for i in ids]
    bc = [_scan_cumsum(inc[i], g[i]) for i in ids]
    btot = [jnp.sum(g[i], axis=0, keepdims=True) for i in ids]
    o_inter = [_mm_nt(q[i] * jnp.exp(bc[i]), s0[i]) for i in ids]
    rowi = _iota2((C, 1), 0)
    outs = [[] for _ in ids]
    for blk in range(C // HG_SUB):
        lo, hi = blk * HG_SUB, (blk + 1) * HG_SUB
        first = [lo if ds[i] == 0 else hi - 1 for i in ids]
        ref = [jnp.sum(jnp.where(rowi == first[i], bc[i] - g[i], 0.0), axis=0, keepdims=True) for i in ids]
        qi = [q[i][lo:hi] * jnp.exp(bc[i][lo:hi] - ref[i]) for i in ids]
        src = [(rowi < hi) if ds[i] == 0 else (rowi >= lo) for i in ids]
        ke = [jnp.where(src[i], k[i] * jnp.exp(jnp.where(src[i], ref[i] - bc[i], 0.0)), 0.0) for i in ids]
        a = [jnp.where(inc[i][lo:hi], _mm_nt(qi[i], ke[i]), 0.0) for i in ids]
        part = [_mm(a[i], iv[i]) for i in ids]
        for i in ids:
            outs[i].append(part[i])
    o = [o_inter[i] + jnp.concatenate(outs[i], axis=0) for i in ids]
    upd = [_mm_tn(iv[i], k[i] * jnp.exp(btot[i] - bc[i])) for i in ids]
    s1 = [s0[i] * jnp.exp(btot[i]) + upd[i] for i in ids]
    return [(o[i], s1[i]) for i in ids]


def _lockstep(fn, ds, nargs):
    def flat_fn(*flat):
        return tuple(fn([tuple(flat[i * nargs:(i + 1) * nargs]) + (d,) for i, d in enumerate(ds)]))

    return flat_fn


def _h_math(x, ng, scale, shift):
    return x * lax.rsqrt(jnp.mean(x * x, axis=-1, keepdims=True) + NORM_EPS) * ng * (1.0 + scale) + shift


def _hg_post_math(o0, o1, z, g):
    o = o0 + o1
    on = o * lax.rsqrt(_seg128_mean(o * o) + NORM_EPS)
    return on * g * _silu(z)


def _rw_prep_math(k, pw0, pw1, pa0, pa1, w0, a0, kkp, kap_p):
    kk = k * kkp
    kap = kk * lax.rsqrt(_seg64_sum(kk * kk) + 1e-12)
    outs = []
    for d, (pw, pa) in enumerate(((pw0, pa0), (pw1, pa1))):
        w_log = -_softplus(-(w0[d:d + 1] + pw)) - 0.5
        lw = -jnp.exp(w_log)
        a = jax.nn.sigmoid(a0[d:d + 1] + pa)
        kd = k * (1.0 + (a - 1.0) * kap_p)
        outs += [lw, kap * a, kd]
    return (kap, *outs)


def _rw_post_math(y0, y1, kd0, kd1, r, v, z, rk, gng, gnb):
    ys, ksum = y0 + y1, kd0 + kd1
    mean = _seg64_sum(ys) * (1.0 / 64.0)
    cen = ys - mean
    var = _seg64_sum(cen * cen) * (1.0 / 64.0)
    yn = cen * lax.rsqrt(var + RW_GN_EPS) * gng + gnb
    bonus = _seg64_sum(r * ksum * rk) * v
    return (yn + bonus) * _silu(z)


def _head_math(x, mo, gate, fg, tgt):
    xo = x + gate * mo
    y = xo * lax.rsqrt(jnp.mean(xo * xo, axis=-1, keepdims=True) + NORM_EPS) * fg
    err = y - tgt
    return 0.5 * jnp.sum(jnp.mean(err * err, axis=-1, keepdims=True), axis=0, keepdims=True)


def _gate_math(ghg, grw, a, b):
    return jax.nn.sigmoid(ghg) * a + jax.nn.sigmoid(grw) * b


def _call(body, name, grid, in_specs, out_specs, out_shape, scratch=(), sem=None, aliases=None):
    return pl.pallas_call(
        body, name=name, grid=grid, in_specs=in_specs, out_specs=out_specs, out_shape=out_shape,
        scratch_shapes=list(scratch), input_output_aliases=aliases or {},
        compiler_params=pltpu.CompilerParams(dimension_semantics=sem, vmem_limit_bytes=V7X_VMEM_LIMIT))


def _full(shape):
    nd = len(shape)
    return pl.BlockSpec(shape, lambda *_: (0,) * nd)


def _sds(shape, dtype=F32):
    return jax.ShapeDtypeStruct(shape, dtype)


def _proj_nt(h, wt, wt_blk0, nblk, row0, nrows, name):
    def body(h_ref, w_ref, o_ref):
        o_ref[...] = lax.dot_general(h_ref[row0:row0 + nrows, :], w_ref[...], (((1,), (1,)), ((), ())), preferred_element_type=F32)

    return _call(body, name, (nblk,), [_full(h.shape), pl.BlockSpec((WB, wt.shape[1]), lambda j: (wt_blk0 + j, 0))],
                 pl.BlockSpec((nrows, WB), lambda j: (0, j)), _sds((nrows, nblk * WB)), sem=("parallel",))(h, wt)


def _dw_tn(a, b, row0, out_rows, out_blk0, name, prev=None):
    m, n = a.shape
    k2 = b.shape[1]
    nblk = n // WB

    def body(*refs):
        a_ref, b_ref, o_ref = refs[0], refs[1], refs[-1]
        o_ref[...] = lax.dot_general(a_ref[...], b_ref[row0:row0 + m, :], (((0,), (0,)), ((), ())),
                                     preferred_element_type=F32).astype(BF16)

    in_specs = [pl.BlockSpec((m, WB), lambda j: (0, j)), _full(b.shape)]
    args = [a, b]
    aliases = None
    if prev is not None:
        in_specs.append(pl.BlockSpec(memory_space=pl.ANY))
        args.append(prev)
        aliases = {2: 0}
    return _call(body, name, (nblk,), in_specs, pl.BlockSpec((WB, k2), lambda j: (out_blk0 + j, 0)), _sds((out_rows, k2), BF16),
                 sem=("arbitrary",), aliases=aliases)(*args)


def _mm_nn(a, w, w_blk0, name, after=None):
    m, kc = a.shape
    n = w.shape[1]
    kb = max(k for k in (WB, 2 * WB, 4 * WB) if kc % k == 0 and (w_blk0 * WB) % k == 0)
    w_blk0 = w_blk0 * WB // kb
    tmb = 3 * TM if m % (3 * TM) == 0 else (2 * TM if m % (2 * TM) == 0 else TM)

    def body(a_ref, w_ref, *rest):
        o_ref = rest[-1]

        @pl.when(pl.program_id(1) == 0)
        def _():
            o_ref[...] = jnp.zeros_like(o_ref)

        o_ref[...] += jnp.dot(a_ref[...], w_ref[...], preferred_element_type=F32)

    extra = [] if after is None else [pl.BlockSpec(memory_space=pl.ANY)]
    return _call(body, name, (m // tmb, kc // kb),
                 [pl.BlockSpec((tmb, kb), lambda i, k: (i, k)), pl.BlockSpec((kb, n), lambda i, k: (w_blk0 + k, 0))] + extra,
                 pl.BlockSpec((tmb, n), lambda i, k: (i, 0)), _sds((m, n)), sem=("parallel", "arbitrary"))(
        a, w, *([] if after is None else [after]))


def _ada_fwd(cc16, ada_shard):
    ncol = ada_shard.shape[1]

    def body(c_ref, w_ref, o_ref):
        o_ref[...] = _mm(_silu(c_ref[...]), w_ref[...])

    return _call(body, "ada_fwd", (1,), [_full((16, D)), _full((D, ncol))], _full((16, ncol)), _sds((16, ncol)))(cc16, ada_shard)


def _ada_bwd(cc16, ada_shard, dm_x, dm_c, dmods):
    ncol = ada_shard.shape[1]

    def body(c_ref, w_ref, dx_ref, dc_ref, dm_ref, dw_ref, dcc_ref, db_ref):
        s, vjp = jax.vjp(_silu, c_ref[...])
        dc_tot = dc_ref[0:1, :]
        for j in range(1, NDEV):
            dc_tot = dc_tot + dc_ref[j:j + 1, :]
        row = _iota2((8, 1), 0)
        dm16 = jnp.concatenate([dx_ref[...], jnp.where(row == 0, dc_tot, 0.0)], axis=0)
        dw_ref[...] = _mm_tn(s, dm16)
        dcc_ref[...] = vjp(_mm_nt(dm16, w_ref[...]))[0]
        db_ref[...] = dm_ref[0:1, :] + dm_ref[1:2, :]

    return _call(body, "ada_bwd", (1,), [_full((16, D)), _full((D, ncol)), _full((8, ncol)), _full((8, ncol)), _full((8, 3 * D))],
                 [_full((D, ncol)), _full((16, D)), _full((1, 3 * D))], [_sds((D, ncol)), _sds((16, D)), _sds((1, 3 * D))])(
        cc16, ada_shard, dm_x, dm_c, dmods)


def _sel_mod(mods_ref, is_ctx, lo):
    return jnp.where(is_ctx, mods_ref[1:2, lo:lo + D], mods_ref[0:1, lo:lo + D])


def _h_fwd(xcat, mods, norm_g):
    dm = _dims()

    def body(x_ref, m_ref, g_ref, o_ref):
        is_ctx = pl.program_id(0) < dm["NCT"]
        o_ref[...] = _h_math(x_ref[...], g_ref[...], _sel_mod(m_ref, is_ctx, D), _sel_mod(m_ref, is_ctx, 0)).astype(BF16)

    return _call(body, "h_fwd", (dm["TT"] // TM,),
                 [pl.BlockSpec((TM, D), lambda i: (i, 0)), _full((8, 3 * D)), _full((1, D))],
                 pl.BlockSpec((TM, D), lambda i: (i, 0)), _sds((dm["TT"], D), BF16), sem=("parallel",))(xcat, mods, norm_g)


def _chunk_of(d, s):
    dm = _dims()
    ncc, nch = dm["NCC"], dm["NCH"]
    return s if d == 0 else jnp.where(s < ncc, ncc - 1 - s, nch - 1 - (s - ncc))


HG_GROUP = 4
RW_GROUP = 4


def _dir_specs(make):
    return [make(d) for d in range(2)]


def _lanes(j):
    return slice(j * 128, (j + 1) * 128)


def _hg_scan_fwd(p_hg, lb3):
    dm = _dims()
    nh, nch, tt = dm["HG_H"], dm["NCH"], dm["TT"]
    pp = min(HG_GROUP, nh)
    lw_ = 128 * pp

    def body(q0, q1, i0, i1, f0, f1, lb0, lb1, o0, o1, ck_ref, s_ref):
        @pl.when(pl.program_id(1) == 0)
        def _():
            s_ref[...] = jnp.zeros_like(s_ref)

        keys, chains = [], []
        for d, refs in enumerate(((q0, i0, f0, lb0), (q1, i1, f1, lb1))):
            vals = [r[...] for r in refs]
            for j in range(pp):
                keys.append((d, j))
                chains.append((s_ref[d, j], *[a[:, _lanes(j)] for a in vals], d))
        for (d, j), chain, (out, s1) in zip(keys, chains, _hg_chunks(chains)):
            ck_ref[d, j] = chain[0]
            (o0, o1)[d][:, _lanes(j)] = out
            s_ref[d, j] = s1

    blk = lambda off: _dir_specs(lambda d: pl.BlockSpec((C, lw_), lambda h, s: (_chunk_of(d, s), off * nh // pp + h)))
    return _call(body, "hg_scan_fwd", (nh // pp, nch),
                 blk(0) + blk(1) + _dir_specs(lambda d: pl.BlockSpec((C, lw_), lambda h, s: (_chunk_of(d, s), (2 + d) * nh // pp + h)))
                 + _dir_specs(lambda d: pl.BlockSpec((None, 1, lw_), lambda h, s: (d, 0, h))),
                 _dir_specs(lambda d: pl.BlockSpec((C, lw_), lambda h, s: (_chunk_of(d, s), h)))
                 + [pl.BlockSpec((2, pp, None, 128, 128), lambda h, s: (0, h, s, 0, 0))],
                 [_sds((tt, HGW)), _sds((tt, HGW)), _sds((2, nh, nch, 128, 128))],
                 scratch=[pltpu.VMEM((2, pp, 128, 128), F32)], sem=("parallel", "arbitrary"))(*([p_hg] * 6), lb3, lb3)


def _hg_scan_bwd(p_hg, lb3, ck, do):
    dm = _dims()
    nh, nch, tt = dm["HG_H"], dm["NCH"], dm["TT"]
    pp = min(HG_GROUP, nh)
    lw_ = 128 * pp

    def body(q0, q1, i0, i1, f0, f1, lb0, lb1, ck_ref, do0, do1, dq0, dq1, di0, di1, df0, df1, dlb_ref, ds_ref):
        @pl.when(pl.program_id(1) == 0)
        def _():
            ds_ref[...] = jnp.zeros_like(ds_ref)
            dlb_ref[...] = jnp.zeros_like(dlb_ref)

        keys, flat, cts = [], [], []
        for d, refs in enumerate(((q0, i0, f0, lb0, do0), (q1, i1, f1, lb1, do1))):
            *vals, do_ = [r[...] for r in refs]
            for j in range(pp):
                keys.append((d, j))
                flat += [ck_ref[d, j], *[a[:, _lanes(j)] for a in vals]]
                cts.append((do_[:, _lanes(j)], ds_ref[d, j]))
        _, vjp = jax.vjp(_lockstep(_hg_chunks, [d for d, _ in keys], 5), *flat)
        grads = vjp(tuple(cts))
        for n, (d, j) in enumerate(keys):
            ds0, dq_, di_, df_, dlb = grads[5 * n:5 * n + 5]
            (dq0, dq1)[d][:, _lanes(j)] = dq_
            (di0, di1)[d][:, _lanes(j)] = di_
            (df0, df1)[d][:, _lanes(j)] = df_
            dlb_ref[d, :, _lanes(j)] += dlb
            ds_ref[d, j] = ds0

    ch = lambda d, s: _chunk_of(d, nch - 1 - s)
    blk = lambda off: _dir_specs(lambda d: pl.BlockSpec((C, lw_), lambda h, s: (ch(d, s), off * nh // pp + h)))
    tok = _dir_specs(lambda d: pl.BlockSpec((C, lw_), lambda h, s: (ch(d, s), h)))
    return _call(body, "hg_scan_bwd", (nh // pp, nch),
                 blk(0) + blk(1) + _dir_specs(lambda d: pl.BlockSpec((C, lw_), lambda h, s: (ch(d, s), (2 + d) * nh // pp + h)))
                 + _dir_specs(lambda d: pl.BlockSpec((None, 1, lw_), lambda h, s: (d, 0, h)))
                 + [pl.BlockSpec((2, pp, None, 128, 128), lambda h, s: (0, h, nch - 1 - s, 0, 0))] + tok,
                 tok * 3 + [pl.BlockSpec((2, 1, lw_), lambda h, s: (0, 0, h))],
                 [_sds((tt, HGW))] * 6 + [_sds((2, 1, HGW))],
                 scratch=[pltpu.VMEM((2, pp, 128, 128), F32)], sem=("parallel", "arbitrary"))(*([p_hg] * 6), lb3, lb3, ck, do, do)


def _rw_scan_fwd(sh, kap, lw, b, kd):
    dm = _dims()
    npair, nch, tt = dm["RW_P"], dm["NCH"], dm["TT"]
    pp = min(RW_GROUP, npair)
    lw_ = 128 * pp

    def body(r0, r1, v0, v1, ka0, ka1, lw0, lw1, b0, b1, kd0, kd1, y0, y1, ck_ref, s_ref):
        @pl.when(pl.program_id(1) == 0)
        def _():
            s_ref[...] = jnp.zeros_like(s_ref)

        keys, chains = [], []
        for d, refs in enumerate(((r0, lw0, ka0, b0, v0, kd0), (r1, lw1, ka1, b1, v1, kd1))):
            vals = [r[...] for r in refs]
            for j in range(pp):
                keys.append((d, j))
                chains.append((s_ref[d, j], *[a[:, _lanes(j)] for a in vals], d))
        for (d, j), chain, (out, s1) in zip(keys, chains, _rw_chunks(chains)):
            ck_ref[d, j] = chain[0]
            (y0, y1)[d][:, _lanes(j)] = out
            s_ref[d, j] = s1

    two = lambda off: _dir_specs(lambda d: pl.BlockSpec((C, lw_), lambda p, s: (_chunk_of(d, s), off * npair // pp + p)))
    three = _dir_specs(lambda d: pl.BlockSpec((None, C, lw_), lambda p, s: (d, _chunk_of(d, s), p)))
    return _call(body, "rw_scan_fwd", (npair // pp, nch), two(0) + two(2) + two(0) + three * 3,
                 two(0) + [pl.BlockSpec((2, pp, None, 128, 128), lambda p, s: (0, p, s, 0, 0))],
                 [_sds((tt, RWW)), _sds((tt, RWW)), _sds((2, npair, nch, 128, 128))],
                 scratch=[pltpu.VMEM((2, pp, 128, 128), F32)], sem=("parallel", "arbitrary"))(
        sh, sh, sh, sh, kap, kap, lw, lw, b, b, kd, kd)


def _rw_scan_bwd(sh, kap, lw, b, kd, ck, dy):
    dm = _dims()
    npair, nch, tt = dm["RW_P"], dm["NCH"], dm["TT"]
    pp = min(RW_GROUP, npair)
    lw_ = 128 * pp

    def body(r0, r1, v0, v1, ka0, ka1, lw0, lw1, b0, b1, kd0, kd1, ck_ref, dy0, dy1,
             dr0, dr1, dv0, dv1, dka0, dka1, dlw0, dlw1, db0, db1, dkd0, dkd1, ds_ref):
        @pl.when(pl.program_id(1) == 0)
        def _():
            ds_ref[...] = jnp.zeros_like(ds_ref)

        keys, flat, cts = [], [], []
        for d, refs in enumerate(((r0, lw0, ka0, b0, v0, kd0, dy0), (r1, lw1, ka1, b1, v1, kd1, dy1))):
            *vals, dy_ = [r[...] for r in refs]
            for j in range(pp):
                keys.append((d, j))
                flat += [ck_ref[d, j], *[a[:, _lanes(j)] for a in vals]]
                cts.append((dy_[:, _lanes(j)], ds_ref[d, j]))
        _, vjp = jax.vjp(_lockstep(_rw_chunks, [d for d, _ in keys], 7), *flat)
        grads = vjp(tuple(cts))
        outs = ((dr0, dlw0, dka0, db0, dv0, dkd0), (dr1, dlw1, dka1, db1, dv1, dkd1))
        for n, (d, j) in enumerate(keys):
            ds0, *g_in = grads[7 * n:7 * n + 7]
            for o_ref, g in zip(outs[d], g_in):
                o_ref[:, _lanes(j)] = g
            ds_ref[d, j] = ds0

    ch = lambda d, s: _chunk_of(d, nch - 1 - s)
    two = lambda off: _dir_specs(lambda d: pl.BlockSpec((C, lw_), lambda p, s: (ch(d, s), off * npair // pp + p)))
    three = _dir_specs(lambda d: pl.BlockSpec((None, C, lw_), lambda p, s: (d, ch(d, s), p)))
    return _call(body, "rw_scan_bwd", (npair // pp, nch),
                 two(0) + two(2) + two(0) + three * 3
                 + [pl.BlockSpec((2, pp, None, 128, 128), lambda p, s: (0, p, nch - 1 - s, 0, 0))] + two(0),
                 two(0) * 6, [_sds((tt, RWW))] * 12,
                 scratch=[pltpu.VMEM((2, pp, 128, 128), F32)], sem=("parallel", "arbitrary"))(
        sh, sh, sh, sh, kap, kap, lw, lw, b, b, kd, kd, ck, dy, dy)


def _tile_pos(i):
    dm = _dims()
    is_ctx = i < dm["NCT"]
    rows = lax.broadcasted_iota(jnp.int32, (TM, 1), 0)
    tl = rows + (i - dm["NCT"]) * TM
    width = jnp.where(is_ctx, TC, GW)
    assert TC & (TC - 1) == 0 and GW & (GW - 1) == 0 and TM % GW == 0
    colp = rows & (width - 1)
    return is_ctx, rows, tl, colp, width


def _shift_lr(cur, i):
    _, _, _, colp, width = _tile_pos(i)
    left = jnp.where(colp == 0, 0.0, pltpu.roll(cur, 1, 0))
    right = jnp.where(colp == width - 1, 0.0, pltpu.roll(cur, TM - 1, 0))
    return left, right


def _shift_ud(cur, prv, nxt, i):
    is_ctx, rows, tl, _, _ = _tile_pos(i)
    if TM > GW:
        up = jnp.where(rows >= GW, pltpu.roll(cur, GW, 0), pltpu.roll(prv, GW, 0))
        down = jnp.where(rows < TM - GW, pltpu.roll(cur, TM - GW, 0), pltpu.roll(nxt, TM - GW, 0))
    else:
        up, down = prv, nxt
    up = jnp.where(jnp.logical_or(is_ctx, tl < GW), 0.0, up)
    down = jnp.where(jnp.logical_or(is_ctx, tl >= T - GW), 0.0, down)
    return up, down


def _shift_specs():
    dm = _dims()
    nt = dm["TT"] // TM
    cw = max(w for w in range(128, 1664 + 1, 128) if dm["SH"] % w == 0)
    cur = pl.BlockSpec((TM, cw), lambda j, i: (i, j))
    prv = pl.BlockSpec((TM, cw), lambda j, i: (jnp.maximum(i - 1, 0), j))
    nxt = pl.BlockSpec((TM, cw), lambda j, i: (jnp.minimum(i + 1, nt - 1), j))
    mu = pl.BlockSpec((4, cw), lambda j, i: (0, j))
    return nt, cw, cur, prv, nxt, mu


def _shift_fwd(p_rw, mu):
    dm = _dims()
    nt, cw, cur, prv, nxt, mus = _shift_specs()

    def body(c_ref, p_ref, n_ref, mu_ref, o_ref):
        i = pl.program_id(1)
        p, m = c_ref[...], mu_ref[...]
        left, right = _shift_lr(p, i)
        up, down = _shift_ud(p, p_ref[...], n_ref[...], i)
        vert = jnp.where(i < dm["NCT"], 0.0, 1.0)
        o_ref[...] = (p * (1.0 - m[0:1] - m[1:2] - vert * (m[2:3] + m[3:4]))
                      + m[0:1] * left + m[1:2] * right + m[2:3] * up + m[3:4] * down)

    return _call(body, "shift_fwd", (dm["SH"] // cw, nt), [cur, prv, nxt, mus], cur, _sds((dm["TT"], dm["SH"])),
                 sem=("parallel", "parallel"))(p_rw, p_rw, p_rw, mu)


def _shift_bwd(p_rw, dsh, mu):
    dm = _dims()
    nt, cw, cur, prv, nxt, mus = _shift_specs()

    def body(c_ref, p_ref, n_ref, gc_ref, gp_ref, gn_ref, mu_ref, dp_ref, dmu_ref):
        i = pl.program_id(1)
        p, g, m = c_ref[...], gc_ref[...], mu_ref[...]
        vert = jnp.where(i < dm["NCT"], 0.0, 1.0)
        _, from_right = _shift_lr(m[0:1] * g, i)
        from_left, _ = _shift_lr(m[1:2] * g, i)
        _, from_down = _shift_ud(m[2:3] * g, g, m[2:3] * gn_ref[...], i)
        from_up, _ = _shift_ud(m[3:4] * g, m[3:4] * gp_ref[...], g, i)
        dp = g * (1.0 - m[0:1] - m[1:2] - vert * (m[2:3] + m[3:4])) + from_right + from_left + from_down + from_up
        dp_ref[...] = dp.astype(BF16)

        @pl.when(i == 0)
        def _():
            dmu_ref[...] = jnp.zeros_like(dmu_ref)

        left, right = _shift_lr(p, i)
        up, down = _shift_ud(p, p_ref[...], n_ref[...], i)
        s = lambda a: jnp.sum(a, axis=0, keepdims=True)
        dmu_ref[...] += jnp.concatenate([s(g * (left - p)), s(g * (right - p)), vert * s(g * (up - p)), vert * s(g * (down - p))], axis=0)

    return _call(body, "shift_bwd", (dm["SH"] // cw, nt), [cur, prv, nxt, cur, prv, nxt, mus],
                 [cur, pl.BlockSpec((4, cw), lambda j, i: (0, j))], [_sds((dm["TT"], dm["SH"]), BF16), _sds((4, dm["SH"]))],
                 sem=("parallel", "arbitrary"))(p_rw, p_rw, p_rw, dsh, dsh, dsh, mu)


def _acc_out(ref, first, vals):
    @pl.when(first)
    def _():
        for r in ref:
            r[...] = jnp.zeros_like(r)

    for r, v in zip(ref, vals):
        r[...] += v


def _rowsum(a):
    return jnp.sum(a, axis=0, keepdims=True)


def _rw_prep_specs():
    dm = _dims()
    rw = RWW // 128
    tok = lambda width, cb: pl.BlockSpec((TMV, width), lambda i: (i, cb))
    ins = [pl.BlockSpec((TMV, RWW), lambda i: (i, 1)), tok(128, 3 * rw), tok(128, 3 * rw + 1),
           _full((2, RWW)), _full((128, RWW)), _full((128, RWW)), _full((2, RWW)), _full((128, RWW)), _full((128, RWW)),
           _full((1, RWW)), _full((1, RWW))]
    return dm, ins


def _rw_prep_fwd(sh, w0, w2p0, w2p1, a0, a2p0, a2p1, kkp, kap_p):
    dm, ins = _rw_prep_specs()
    tt = dm["TT"]

    def body(k_ref, low_ref, loa_ref, w0_ref, w20_ref, w21_ref, a0_ref, a20_ref, a21_ref, kk_ref, ka_ref, kap_ref, lw_ref, b_ref, kd_ref):
        tw, la = jnp.tanh(low_ref[...]), loa_ref[...]
        kap, lw0, b0, kd0, lw1, b1, kd1 = _rw_prep_math(
            k_ref[...], _mm(tw, w20_ref[...]), _mm(tw, w21_ref[...]), _mm(la, a20_ref[...]), _mm(la, a21_ref[...]),
            w0_ref[...], a0_ref[...], kk_ref[...], ka_ref[...])
        kap_ref[...] = kap
        lw_ref[0], lw_ref[1] = lw0, lw1
        b_ref[0], b_ref[1] = b0, b1
        kd_ref[0], kd_ref[1] = kd0, kd1

    two = pl.BlockSpec((2, TMV, RWW), lambda i: (0, i, 0))
    return _call(body, "rw_prep_fwd", (tt // TMV,), ins, [pl.BlockSpec((TMV, RWW), lambda i: (i, 0)), two, two, two],
                 [_sds((tt, RWW)), _sds((2, tt, RWW)), _sds((2, tt, RWW)), _sds((2, tt, RWW))], sem=("parallel",))(
        sh, sh, sh, w0, w2p0, w2p1, a0, a2p0, a2p1, kkp, kap_p)


def _rw_prep_bwd(sh, w0, w2p0, w2p1, a0, a2p0, a2p1, kkp, kap_p, scan_grads, dkd_p, dr_p, dv_p):
    dm, ins = _rw_prep_specs()
    tt, sh_w = dm["TT"], dm["SH"]
    one = pl.BlockSpec((TMV, RWW), lambda i: (i, 0))

    def body(k_ref, low_ref, loa_ref, w0_ref, w20_ref, w21_ref, a0_ref, a20_ref, a21_ref, kk_ref, ka_ref,
             dr0, dr1, dv0, dv1, dka0, dka1, dlw0, dlw1, db0, db1, dkd0, dkd1, dkdp_ref, drp_ref, dvp_ref,
             dsh_ref, dw0_ref, dw20_ref, dw21_ref, da0_ref, da20_ref, da21_ref, dkk_ref, dka_ref):
        tw, la = jnp.tanh(low_ref[...]), loa_ref[...]
        w20, w21, a20, a21 = w20_ref[...], w21_ref[...], a20_ref[...], a21_ref[...]
        _, vjp = jax.vjp(_rw_prep_math, k_ref[...], _mm(tw, w20), _mm(tw, w21), _mm(la, a20), _mm(la, a21),
                         w0_ref[...], a0_ref[...], kk_ref[...], ka_ref[...])
        dkp = dkdp_ref[...]
        dk, dpw0, dpw1, dpa0, dpa1, dw0, da0, dkk, dka = vjp(
            (dka0[...] + dka1[...], dlw0[...], db0[...], dkd0[...] + dkp, dlw1[...], db1[...], dkd1[...] + dkp))
        twt, lat = tw.T, la.T
        dtw = _mm_nt(dpw0, w20) + _mm_nt(dpw1, w21)
        dsh_ref[:, 0:RWW] = dr0[...] + dr1[...] + drp_ref[...]
        dsh_ref[:, RWW:2 * RWW] = dk
        dsh_ref[:, 2 * RWW:3 * RWW] = dv0[...] + dv1[...] + dvp_ref[...]
        dsh_ref[:, 3 * RWW:3 * RWW + 128] = dtw * (1.0 - tw * tw)
        dsh_ref[:, 3 * RWW + 128:3 * RWW + 256] = _mm_nt(dpa0, a20) + _mm_nt(dpa1, a21)
        _acc_out((dw0_ref, dw20_ref, dw21_ref, da0_ref, da20_ref, da21_ref, dkk_ref, dka_ref), pl.program_id(0) == 0,
                 (dw0, _mm(twt, dpw0), _mm(twt, dpw1), da0, _mm(lat, dpa0), _mm(lat, dpa1), dkk, dka))

    par = [_full((2, RWW)), _full((128, RWW)), _full((128, RWW)), _full((2, RWW)), _full((128, RWW)), _full((128, RWW)),
           _full((1, RWW)), _full((1, RWW))]
    return _call(body, "rw_prep_bwd", (tt // TMV,), ins + [one] * 15,
                 [pl.BlockSpec((TMV, sh_w), lambda i: (i, 0))] + par,
                 [_sds((tt, sh_w)), _sds((2, RWW)), _sds((128, RWW)), _sds((128, RWW)), _sds((2, RWW)), _sds((128, RWW)), _sds((128, RWW)),
                  _sds((1, RWW)), _sds((1, RWW))],
                 sem=("arbitrary",))(sh, sh, sh, w0, w2p0, w2p1, a0, a2p0, a2p1, kkp, kap_p, *scan_grads, dkd_p, dr_p, dv_p)


def _rw_post_specs():
    dm = _dims()
    nv = dm["NCTV"]
    rw = RWW // 128
    lat3 = lambda d: pl.BlockSpec((None, TMV, RWW), lambda i: (d, nv + i, 0))
    lat2 = pl.BlockSpec((TMV, RWW), lambda i: (nv + i, 0))
    ins = [lat2, lat2, lat3(0), lat3(1),
           pl.BlockSpec((TMV, RWW), lambda i: (nv + i, 0)), pl.BlockSpec((TMV, RWW), lambda i: (nv + i, 2)),
           pl.BlockSpec((TMV, RWW), lambda i: (i, 0)), _full((1, RWW)), _full((1, RWW)), _full((1, RWW))]
    return dm, nv, ins


def _rw_post_fwd(y0_, y1_, kd, sh, p_z, rk, gng, gnb):
    dm, nv, ins = _rw_post_specs()

    def body(y0, y1, k0, k1, r, v, z, rk_ref, g_ref, b_ref, o_ref):
        o_ref[...] = _rw_post_math(y0[...], y1[...], k0[...], k1[...], r[...], v[...], z[...], rk_ref[...], g_ref[...], b_ref[...]).astype(BF16)

    return _call(body, "rw_post_fwd", (T // TMV,), ins, pl.BlockSpec((TMV, RWW), lambda i: (i, 0)), _sds((T, RWW), BF16),
                 sem=("parallel",))(y0_, y1_, kd, kd, sh, sh, p_z, rk, gng, gnb)


def _rw_post_bwd(y0_, y1_, kd, sh, p_z, rk, gng, gnb, dout):
    dm, nv, _ = _rw_post_specs()
    tt = dm["TT"]
    cl = lambda i: jnp.maximum(i - nv, 0)
    all3 = lambda d: pl.BlockSpec((None, TMV, RWW), lambda i: (d, i, 0))
    all2 = pl.BlockSpec((TMV, RWW), lambda i: (i, 0))
    ins = [all2, all2, all3(0), all3(1),
           pl.BlockSpec((TMV, RWW), lambda i: (i, 0)), pl.BlockSpec((TMV, RWW), lambda i: (i, 2)),
           pl.BlockSpec((TMV, RWW), lambda i: (cl(i), 0)), _full((1, RWW)), _full((1, RWW)), _full((1, RWW)),
           pl.BlockSpec((TMV, RWW), lambda i: (cl(i), 0))]
    tok = pl.BlockSpec((TMV, RWW), lambda i: (i, 0))

    def body(y0, y1, k0, k1, r, v, z, rk_ref, g_ref, b_ref, do_ref, dy_ref, dkd_ref, dr_ref, dv_ref, dz_ref, drk_ref, dg_ref, db_ref):
        i = pl.program_id(0)
        lat = jnp.where(i >= nv, 1.0, 0.0)
        _, vjp = jax.vjp(_rw_post_math, y0[...], y1[...], k0[...], k1[...], r[...], v[...], z[...], rk_ref[...], g_ref[...], b_ref[...])
        dy0, _, dk0, _, dr, dv, dz, drk, dg, db = vjp(do_ref[...] * lat)
        dy_ref[...] = dy0
        dkd_ref[...] = dk0
        dr_ref[...] = dr
        dv_ref[...] = dv
        dz_ref[...] = dz.astype(BF16)
        _acc_out((drk_ref, dg_ref, db_ref), i == 0, (drk, dg, db))

    return _call(body, "rw_post_bwd", (tt // TMV,), ins,
                 [tok, tok, tok, tok, pl.BlockSpec((TMV, RWW), lambda i: (cl(i), 0)), _full((1, RWW)), _full((1, RWW)), _full((1, RWW))],
                 [_sds((tt, RWW))] * 4 + [_sds((T, RWW), BF16), _sds((1, RWW)), _sds((1, RWW)), _sds((1, RWW))],
                 sem=("arbitrary",))(y0_, y1_, kd, kd, sh, sh, p_z, rk, gng, gnb, dout)


def _hg_post_fwd(o0_, o1_, p_hg, hg_g):
    dm = _dims()
    nv = dm["NCTV"]
    lat3 = lambda d: pl.BlockSpec((TMV, HGW), lambda i: (nv + i, 0))

    def body(o0, o1, z, g, out):
        out[...] = _hg_post_math(o0[...], o1[...], z[...], g[...]).astype(BF16)

    return _call(body, "hg_post_fwd", (T // TMV,), [lat3(0), lat3(1), pl.BlockSpec((TMV, HGW), lambda i: (nv + i, 4)), _full((1, HGW))],
                 pl.BlockSpec((TMV, HGW), lambda i: (i, 0)), _sds((T, HGW), BF16), sem=("parallel",))(o0_, o1_, p_hg, hg_g)


def _hg_post_bwd(o0_, o1_, p_hg, hg_g, dout):
    dm = _dims()
    nv, tt = dm["NCTV"], dm["TT"]
    all3 = lambda d: pl.BlockSpec((TMV, HGW), lambda i: (i, 0))
    tok = pl.BlockSpec((TMV, HGW), lambda i: (i, 0))

    def body(o0, o1, z, g, do_in, do_ref, dz_ref, dg_ref):
        i = pl.program_id(0)
        lat = jnp.where(i >= nv, 1.0, 0.0)
        _, vjp = jax.vjp(_hg_post_math, o0[...], o1[...], z[...], g[...])
        d0, _, dz, dg = vjp(do_in[...] * lat)
        do_ref[...] = d0
        dz_ref[...] = dz
        _acc_out((dg_ref,), i == 0, (dg,))

    return _call(body, "hg_post_bwd", (tt // TMV,),
                 [all3(0), all3(1), pl.BlockSpec((TMV, HGW), lambda i: (i, 4)), _full((1, HGW)),
                  pl.BlockSpec((TMV, HGW), lambda i: (jnp.maximum(i - nv, 0), 0))],
                 [tok, tok, _full((1, HGW))], [_sds((tt, HGW)), _sds((tt, HGW)), _sds((1, HGW))], sem=("arbitrary",))(o0_, o1_, p_hg, hg_g, dout)


def _hg_dproj(dq0, dq1, di0, di1, df0, df1, dz):
    dm = _dims()
    tt = dm["TT"]
    tok = pl.BlockSpec((TMV, HGW), lambda i: (i, 0))

    def body(q0, q1, i0, i1, f0, f1, z, o_ref):
        o_ref[:, 0:HGW] = (q0[...] + q1[...]).astype(BF16)
        o_ref[:, HGW:2 * HGW] = (i0[...] + i1[...]).astype(BF16)
        o_ref[:, 2 * HGW:3 * HGW] = f0[...].astype(BF16)
        o_ref[:, 3 * HGW:4 * HGW] = f1[...].astype(BF16)
        o_ref[:, 4 * HGW:5 * HGW] = z[...].astype(BF16)

    return _call(body, "hg_dproj", (tt // TMV,), [tok] * 7,
                 pl.BlockSpec((TMV, 5 * HGW), lambda i: (i, 0)), _sds((tt, 5 * HGW), BF16), sem=("parallel",))(
        dq0, dq1, di0, di1, df0, df1, dz)


def _merge1_fwd(out_hg, out_rw, whg_t, wrw_t, p_g):
    tok = pl.BlockSpec((TM, D), lambda i: (i, 0))

    def body(h_ref, r_ref, wh_ref, wr_ref, g1_ref, g2_ref, a_ref, b_ref, m_ref):
        nt = (((1,), (1,)), ((), ()))
        a = lax.dot_general(h_ref[...], wh_ref[...], nt, preferred_element_type=F32)
        b = lax.dot_general(r_ref[...], wr_ref[...], nt, preferred_element_type=F32)
        a_ref[...] = a
        b_ref[...] = b
        m_ref[...] = _gate_math(g1_ref[...], g2_ref[...], a, b).astype(BF16)

    return _call(body, "merge1_fwd", (T // TM,),
                 [pl.BlockSpec((TM, HGW), lambda i: (i, 0)), pl.BlockSpec((TM, RWW), lambda i: (i, 0)), _full((D, HGW)), _full((D, RWW)),
                  pl.BlockSpec((TM, D), lambda i: (i, 0)), pl.BlockSpec((TM, D), lambda i: (i, 1))],
                 [tok, tok, tok], [_sds((T, D)), _sds((T, D)), _sds((T, D), BF16)], sem=("parallel",))(out_hg, out_rw, whg_t, wrw_t, p_g, p_g)


def _head_fwd_bwd(x, merged, w_out, mods, final_g, tgt):
    tok = pl.BlockSpec((TM, D), lambda i: (i, 0))

    def body(x_ref, m_ref, w_ref, mods_ref, fg_ref, t_ref, dxo_ref, dmo_ref, loss_ref, dgate_ref, dfg_ref):
        mo = jnp.dot(m_ref[...], w_ref[...], preferred_element_type=F32)
        gate = mods_ref[0:1, 2 * D:3 * D]
        loss, vjp = jax.vjp(lambda x_, mo_, g_, fg_: _head_math(x_, mo_, g_, fg_, t_ref[...]), x_ref[...], mo, gate, fg_ref[...])
        dx, dmo, dgate, dfg = vjp(jnp.ones((1, 1), F32))
        dxo_ref[...] = dx
        dmo_ref[...] = dmo.astype(BF16)
        _acc_out((loss_ref, dgate_ref, dfg_ref), pl.program_id(0) == 0, (jnp.broadcast_to(loss, (1, 128)), dgate, dfg))

    return _call(body, "head_fwd_bwd", (T // TM,), [tok, tok, _full((D, D)), _full((8, 3 * D)), _full((1, D)), tok],
                 [tok, tok, _full((1, 128)), _full((1, D)), _full((1, D))],
                 [_sds((T, D)), _sds((T, D), BF16), _sds((1, 128)), _sds((1, D)), _sds((1, D))], sem=("arbitrary",))(
        x, merged, w_out, mods, final_g, tgt)


def _merge_bwd(dmo, w_out, p_g, a, b):
    tok = pl.BlockSpec((TM, D), lambda i: (i, 0))

    def body(dmo_ref, w_ref, g1_ref, g2_ref, a_ref, b_ref, da_ref, db_ref, dg_ref):
        dm_ = lax.dot_general(dmo_ref[...], w_ref[...], (((1,), (1,)), ((), ())), preferred_element_type=F32)
        _, vjp = jax.vjp(_gate_math, g1_ref[...], g2_ref[...], a_ref[...], b_ref[...])
        dg1, dg2, da, db = vjp(dm_)
        da_ref[...] = da.astype(BF16)
        db_ref[...] = db.astype(BF16)
        dg_ref[:, 0:D] = dg1.astype(BF16)
        dg_ref[:, D:2 * D] = dg2.astype(BF16)

    return _call(body, "merge_bwd", (T // TM,),
                 [tok, _full((D, D)), pl.BlockSpec((TM, D), lambda i: (i, 0)), pl.BlockSpec((TM, D), lambda i: (i, 1)), tok, tok],
                 [tok, tok, pl.BlockSpec((TM, 2 * D), lambda i: (i, 0))], [_sds((T, D), BF16), _sds((T, D), BF16), _sds((T, 2 * D), BF16)],
                 sem=("parallel",))(dmo, w_out, p_g, p_g, a, b)


def _h_bwd(xcat, mods, norm_g, dh_hg, dh_rw, dh_z, dh_g, dxo, dgate):
    dm = _dims()
    nct, tt = dm["NCT"], dm["TT"]
    tok = pl.BlockSpec((TM, D), lambda i: (i, 0))
    lat = pl.BlockSpec((TM, D), lambda i: (jnp.maximum(i - nct, 0), 0))

    def body(x_ref, m_ref, g_ref, d1, d2, d3, d4, dxo_ref, dgate_ref, gx_ref, dng_ref, dmods_ref):
        i = pl.program_id(0)
        is_ctx = i < nct
        latf = jnp.where(is_ctx, 0.0, 1.0)
        dh = d1[...] + d2[...] + latf * (d3[...] + d4[...])
        _, vjp = jax.vjp(_h_math, x_ref[...], g_ref[...], _sel_mod(m_ref, is_ctx, D), _sel_mod(m_ref, is_ctx, 0))
        dx, dng, dscale, dshift = vjp(dh)
        gx_ref[...] = dx + dxo_ref[...]

        @pl.when(i == 0)
        def _():
            dng_ref[...] = jnp.zeros_like(dng_ref)
            dmods_ref[...] = jnp.zeros_like(dmods_ref)
            dmods_ref[0:1, 2 * D:3 * D] = dgate_ref[...]

        dng_ref[...] += dng
        row = lax.broadcasted_iota(jnp.int32, (8, 1), 0)
        sel = jnp.where(row == jnp.where(is_ctx, 1, 0), 1.0, 0.0)
        dmods_ref[:, 0:D] += sel * dshift
        dmods_ref[:, D:2 * D] += sel * dscale

    return _call(body, "h_bwd", (tt // TM,), [tok, _full((8, 3 * D)), _full((1, D)), tok, tok, lat, lat, lat, _full((1, D))],
                 [lat, _full((1, D)), _full((8, 3 * D))], [_sds((T, D)), _sds((1, D)), _sds((8, 3 * D))], sem=("arbitrary",))(
        xcat, mods, norm_g, dh_hg, dh_rw, dh_z, dh_g, dxo, dgate)


def _lb_math(l0, l1):
    return jax.nn.sigmoid(l0 - l1)


def _lb_fwd(l0, l1):
    def body(a_ref, b_ref, o_ref):
        o_ref[...] = _lb_math(a_ref[...], b_ref[...])

    return _call(body, "lb_fwd", (1,), [_full((2, HGW))] * 2, _full((2, HGW)), _sds((2, HGW)))(l0, l1)


def _lb_bwd(l0, l1, dlb):
    def body(a_ref, b_ref, d_ref, da_ref, db_ref):
        _, vjp = jax.vjp(_lb_math, a_ref[...], b_ref[...])
        da_ref[...], db_ref[...] = vjp(d_ref[...])

    return _call(body, "lb_bwd", (1,), [_full((2, HGW))] * 3, [_full((2, HGW))] * 2, [_sds((2, HGW))] * 2)(l0, l1, dlb)


def _local_step(x, ctx, tgt, mods, w, start_exchange):
    dm = _dims()
    tt, sh_w, hgc = dm["TT"], dm["SH"], dm["HGC"]
    nb = lambda cols: cols // WB
    xcat = jnp.concatenate([ctx, x], axis=0)
    h = _h_fwd(xcat, mods, w["norm_g"])
    win_t = w["win_t"]
    p_hg = _proj_nt(h, win_t, 0, nb(hgc), 0, tt, "proj_hg")
    p_rw = _proj_nt(h, win_t, nb(hgc), nb(sh_w), 0, tt, "proj_rw")
    p_z = _proj_nt(h, win_t, nb(hgc + sh_w), nb(RWW), TC, T, "proj_z")
    p_g = _proj_nt(h, win_t, nb(hgc + sh_w + RWW), nb(2 * D), TC, T, "proj_g")

    lb3 = _lb_fwd(w["lb0"], w["lb1"]).reshape(2, 1, HGW)
    o0, o1, hg_ck = _hg_scan_fwd(p_hg, lb3)
    out_hg = _hg_post_fwd(o0, o1, p_hg, w["hg_g"])

    sh = _shift_fwd(p_rw, w["mu"])
    zpad = jnp.zeros((LORA, RWW), F32)
    prep_w = (w["w0"], jnp.concatenate([w["w2"][0], zpad], 0), jnp.concatenate([zpad, w["w2"][1]], 0),
              w["a0"], jnp.concatenate([w["a2"][0], zpad], 0), jnp.concatenate([zpad, w["a2"][1]], 0), w["kk"], w["ka"])
    kap, lw, b, kd = _rw_prep_fwd(sh, *prep_w)
    y0, y1, rw_ck = _rw_scan_fwd(sh, kap, lw, b, kd)
    post_w = (w["rk"], w["gng"], w["gnb"])
    out_rw = _rw_post_fwd(y0, y1, kd, sh, p_z, *post_w)

    a, bb, merged = _merge1_fwd(out_hg, out_rw, w["whg_t"], w["wrw_t"], p_g)
    dxo, dmo, loss, dgate, d_fg = _head_fwd_bwd(x, merged, w["wout"], mods, w["final_g"], tgt)
    da, db, dp_g = _merge_bwd(dmo, w["wout"], p_g, a, bb)
    g_wout = _dw_tn(merged, dmo, 0, D, 0, "dw_out")
    g_whg = _dw_tn(da, out_hg, 0, D, 0, "dw_hg")
    g_wrw = _dw_tn(db, out_rw, 0, D, 0, "dw_rw")
    d_out_hg = _mm_nn(da, w["whg_t"], 0, "dx_hg")
    d_out_rw = _mm_nn(db, w["wrw_t"], 0, "dx_rw")

    do, dz_hg, d_hg_g = _hg_post_bwd(o0, o1, p_hg, w["hg_g"], d_out_hg)
    *hg_grads, dlb3 = _hg_scan_bwd(p_hg, lb3, hg_ck, do)
    dp_hg = _hg_dproj(*hg_grads, dz_hg)
    d_lb0, d_lb1 = _lb_bwd(w["lb0"], w["lb1"], dlb3.reshape(2, HGW))

    dy, dkd_p, dr_p, dv_p, dp_z, d_rk, d_gng, d_gnb = _rw_post_bwd(y0, y1, kd, sh, p_z, *post_w, d_out_rw)
    rw_grads = _rw_scan_bwd(sh, kap, lw, b, kd, rw_ck, dy)
    dsh, d_w0, d_w2p0, d_w2p1, d_a0, d_a2p0, d_a2p1, d_kk, d_ka = _rw_prep_bwd(sh, *prep_w, rw_grads, dkd_p, dr_p, dv_p)
    dp_rw, d_mu = _shift_bwd(p_rw, dsh, w["mu"])

    g_win = _dw_tn(dp_hg, h, 0, dm["NCOLS"], 0, "dw_in_hg")
    g_win = _dw_tn(dp_rw, h, 0, dm["NCOLS"], nb(hgc), "dw_in_rw", prev=g_win)
    g_win = _dw_tn(dp_z, h, TC, dm["NCOLS"], nb(hgc + sh_w), "dw_in_z", prev=g_win)
    g_win = _dw_tn(dp_g, h, TC, dm["NCOLS"], nb(hgc + sh_w + RWW), "dw_in_g", prev=g_win)
    big = dict(win_t=g_win, wout=g_wout, whg_t=g_whg, wrw_t=g_wrw)
    token, in_flight = start_exchange(big)
    dh_hg = _mm_nn(dp_hg, win_t, 0, "dh_hg", after=token)
    dh_rw = _mm_nn(dp_rw, win_t, nb(hgc), "dh_rw")
    dh_z = _mm_nn(dp_z, win_t, nb(hgc + sh_w), "dh_z")
    dh_g = _mm_nn(dp_g, win_t, nb(hgc + sh_w + RWW), "dh_g")
    gx, d_ng, dmods = _h_bwd(xcat, mods, w["norm_g"], dh_hg, dh_rw, dh_z, dh_g, dxo, dgate)

    small = dict(norm_g=d_ng, lb0=d_lb0, lb1=d_lb1, hg_g=d_hg_g, mu=d_mu, w0=d_w0,
                 w2=jnp.stack([d_w2p0[:LORA], d_w2p1[LORA:]]), a0=d_a0, a2=jnp.stack([d_a2p0[:LORA], d_a2p1[LORA:]]),
                 kk=d_kk, ka=d_ka, rk=d_rk, gng=d_gng, gnb=d_gnb, final_g=d_fg)
    return loss, gx, dmods, in_flight, small


MESH = pl.DeviceIdType.MESH


def _comm_call(body, name, bufs, out_shapes, nsem, nloc):
    hbm = pl.BlockSpec(memory_space=pl.ANY)
    return pl.pallas_call(
        body, name=name, in_specs=[hbm] * len(bufs), out_specs=[hbm] * len(out_shapes), out_shape=out_shapes,
        scratch_shapes=[pltpu.SemaphoreType.DMA((nsem,)), pltpu.SemaphoreType.DMA((nsem,)), pltpu.SemaphoreType.DMA((nloc,))],
    )(*bufs)


def _gather2(bufs, name):
    nbuf = len(bufs)

    def body(*refs):
        ins, outs = refs[:nbuf], refs[nbuf:2 * nbuf]
        send_sems, recv_sems, loc_sems = refs[2 * nbuf:]
        x, y, c = lax.axis_index("x"), lax.axis_index("y"), lax.axis_index("c")
        me, sib = (x, y, c), (x, y, 1 - c)
        chips = [(1 - x, y), (x, 1 - y), (1 - x, 1 - y)]

        def copy(bi, k, block, to, src=None):
            rows = outs[bi].at[4 * block[0] + 2 * block[1] + block[2]]
            return pltpu.make_async_remote_copy(src_ref=rows if src is None else src, dst_ref=rows, send_sem=send_sems.at[7 * bi + k],
                                                recv_sem=recv_sems.at[7 * bi + k], device_id=to, device_id_type=MESH)

        own = [pltpu.make_async_copy(ins[bi], outs[bi].at[4 * x + 2 * y + c], loc_sems.at[bi]) for bi in range(nbuf)]
        for cp in own:
            cp.start()
        first = []
        for bi in range(nbuf):
            first.append(copy(bi, 0, me, sib, src=ins[bi]))
            first += [copy(bi, 1 + j, me, (*chip, c), src=ins[bi]) for j, chip in enumerate(chips)]
        for cp in first:
            cp.start()
        passed = []
        for j, chip in enumerate(chips):
            for bi in range(nbuf):
                copy(bi, 1 + j, (*chip, c), me).wait_recv()
                passed.append(copy(bi, 4 + j, (*chip, c), sib))
                passed[-1].start()
        for bi in range(nbuf):
            copy(bi, 0, sib, me).wait_recv()
            for j, chip in enumerate(chips):
                copy(bi, 4 + j, (*chip, 1 - c), me).wait_recv()
        for cp in first + passed:
            cp.wait_send()
        for cp in own:
            cp.wait()

    return _comm_call(body, name, bufs, [_sds((NDEV,) + b.shape, b.dtype) for b in bufs], 7 * nbuf, nbuf)


def _pair_exchange(bufs, name):
    nbuf = len(bufs)

    def body(*refs):
        ins, got = refs[:nbuf], refs[nbuf:2 * nbuf]
        send_sems, recv_sems, _ = refs[2 * nbuf:]
        x, y, c = lax.axis_index("x"), lax.axis_index("y"), lax.axis_index("c")
        copies = []
        for bi in range(nbuf):
            for q in range(4):
                copies.append(pltpu.make_async_remote_copy(
                    src_ref=ins[bi].at[2 * q + 1 - c], dst_ref=got[bi].at[q], send_sem=send_sems.at[4 * bi + q],
                    recv_sem=recv_sems.at[4 * bi + q], device_id=(x, y, 1 - c), device_id_type=MESH))
                copies[-1].start()
        for cp in copies:
            cp.wait_send()
            cp.wait_recv()

    return _comm_call(body, name, bufs, [_sds((4,) + b.shape[1:], b.dtype) for b in bufs], 4 * nbuf, 1)


def _chip_copies(srcs, lands, send_sems, recv_sems):
    x, y, c = lax.axis_index("x"), lax.axis_index("y"), lax.axis_index("c")
    myq = 2 * x + y
    pairs = []
    for bi in range(len(srcs)):
        for j, (qx, qy) in enumerate([(1 - x, y), (x, 1 - y), (1 - x, 1 - y)]):
            q = 2 * qx + qy
            mk = lambda src, dst: pltpu.make_async_remote_copy(
                src_ref=src, dst_ref=dst, send_sem=send_sems.at[3 * bi + j], recv_sem=recv_sems.at[3 * bi + j],
                device_id=(qx, qy, c), device_id_type=MESH)
            pairs.append((mk(srcs[bi].at[q], lands[bi].at[myq]), mk(srcs[bi].at[myq], lands[bi].at[q])))
    return pairs


def _chip_exchange_start(bufs, name):
    nbuf = len(bufs)
    hbm, sem = pl.BlockSpec(memory_space=pltpu.HBM), pl.BlockSpec(memory_space=pltpu.SEMAPHORE)

    def body(*refs):
        srcs, lands = refs[:nbuf], refs[nbuf:2 * nbuf]
        send_sems, recv_sems = refs[2 * nbuf], refs[2 * nbuf + 1]
        token = refs[-1]
        for send, _ in _chip_copies(srcs, lands, send_sems, recv_sems):
            send.start()
        token[...] = jnp.zeros_like(token)

    held = [pltpu.HBM(b.shape, b.dtype) for b in bufs]
    outs = pl.pallas_call(
        body, name=name, in_specs=[hbm] * (2 * nbuf), out_specs=(sem, sem, *[hbm] * (2 * nbuf), pl.BlockSpec(memory_space=pltpu.VMEM)),
        out_shape=(pltpu.SemaphoreType.DMA((3 * nbuf,)), pltpu.SemaphoreType.DMA((3 * nbuf,)), *held, *held, _sds((8, 128))),
        input_output_aliases={i: 2 + i for i in range(2 * nbuf)},
        compiler_params=pltpu.CompilerParams(has_side_effects=pltpu.SideEffectType.DATAFLOW_SIDE_EFFECTING),
    )(*[pltpu.with_memory_space_constraint(b, pltpu.HBM) for b in bufs],
      *[pltpu.with_memory_space_constraint(lax.empty(b.shape, b.dtype), pltpu.HBM) for b in bufs])
    return outs[0], outs[1], list(outs[2:2 + nbuf]), list(outs[2 + nbuf:2 + 2 * nbuf]), outs[-1]


def _chip_exchange_wait(send_sems, recv_sems, srcs, lands, after, name):
    nbuf = len(srcs)
    hbm, sem = pl.BlockSpec(memory_space=pltpu.HBM), pl.BlockSpec(memory_space=pltpu.SEMAPHORE)

    def body(*refs):
        src_refs, land_refs = refs[:nbuf], refs[nbuf:2 * nbuf]
        for send, recv in _chip_copies(src_refs, land_refs, refs[2 * nbuf], refs[2 * nbuf + 1]):
            send.wait_send()
            recv.wait_recv()

    held = [pltpu.HBM(b.shape, b.dtype) for b in srcs]
    outs = pl.pallas_call(
        body, name=name, in_specs=[hbm] * (2 * nbuf) + [sem, sem, pl.BlockSpec(memory_space=pl.ANY)], out_specs=[hbm] * (2 * nbuf),
        out_shape=(*held, *held), input_output_aliases={i: i for i in range(2 * nbuf)},
        compiler_params=pltpu.CompilerParams(has_side_effects=pltpu.SideEffectType.DATAFLOW_SIDE_EFFECTING),
    )(*srcs, *lands, send_sems, recv_sems, after)
    return list(outs[:nbuf]), list(outs[nbuf:])


def _prefetch_call(body, name, scalars, grid, in_specs, out_specs, out_shape, args):
    return pl.pallas_call(
        body, name=name, out_shape=out_shape,
        grid_spec=pltpu.PrefetchScalarGridSpec(num_scalar_prefetch=1, grid=grid, in_specs=in_specs, out_specs=out_specs),
        compiler_params=pltpu.CompilerParams(dimension_semantics=("parallel",) * len(grid), vmem_limit_bytes=V7X_VMEM_LIMIT))(scalars, *args)


def _pair_add(g, got, core, name):
    _, rows, cols = got.shape
    tr = _row_tile(rows, 16, 256)

    def body(c_ref, a_ref, b_ref, o_ref):
        o_ref[...] = (a_ref[...].astype(F32) + b_ref[...].astype(F32)).astype(o_ref.dtype)

    blk = pl.BlockSpec((4, tr, cols), lambda i, c_ref: (0, i, 0))
    mine = pl.BlockSpec((4, None, tr, cols), lambda i, c_ref: (0, c_ref[0], i, 0))
    return _prefetch_call(body, name, core, (rows // tr,), [mine, blk], blk, _sds(got.shape, got.dtype),
                          (g.reshape(4, 2, rows, cols), got))


def _reduce_chips(chip_sums, recv, slots, name):
    _, rows, cols = recv.shape
    tr = _row_tile(rows, 16, 128)

    def body(s_ref, own_ref, r1_ref, r2_ref, r3_ref, o_ref):
        o_ref[...] = ((own_ref[...].astype(F32) + r1_ref[...].astype(F32)) + r2_ref[...].astype(F32)) + r3_ref[...].astype(F32)

    pick = lambda k: pl.BlockSpec((None, tr, cols), lambda i, s_ref: (s_ref[k], i, 0))
    return _prefetch_call(body, name, slots, (rows // tr,), [pick(0), pick(1), pick(2), pick(3)],
                          pl.BlockSpec((tr, cols), lambda i, s_ref: (i, 0)), _sds((rows, cols)), (chip_sums, recv, recv, recv))


def _row_tile(rows, mult, cap):
    best = mult
    for t in range(mult, cap + 1, mult):
        if rows % t == 0:
            best = t
    assert rows % best == 0, (rows, mult)
    return best


def _reduce_sources(r, name):
    nsrc, rows, cols = r.shape
    tr = _row_tile(rows, 16 if r.dtype.itemsize == 2 else 8, 128)

    def body(r_ref, o_ref):
        acc = r_ref[0].astype(F32)
        for j in range(1, nsrc):
            acc = acc + r_ref[j].astype(F32)
        o_ref[...] = acc

    return _call(body, name, (rows // tr,), [pl.BlockSpec((nsrc, tr, cols), lambda i: (0, i, 0))],
                 pl.BlockSpec((tr, cols), lambda i: (i, 0)), _sds((rows, cols)), sem=("parallel",))(r)


def _adamw(w, g, m, v, name):
    rows, cols = w.shape
    tr = _row_tile(rows, 8, 128)
    c1 = 1.0 - ADAM_B1 ** ADAM_STEP
    c2 = 1.0 - ADAM_B2 ** ADAM_STEP

    def body(w_ref, g_ref, m_ref, v_ref, d_ref, mo_ref, vo_ref):
        g_ = g_ref[...]
        m_ = ADAM_B1 * m_ref[...] + (1.0 - ADAM_B1) * g_
        v_ = ADAM_B2 * v_ref[...] + (1.0 - ADAM_B2) * (g_ * g_)
        d_ref[...] = -ADAM_LR * ((m_ / c1) / (jnp.sqrt(v_ / c2) + ADAM_EPS) + ADAM_WD * w_ref[...])
        mo_ref[...] = m_
        vo_ref[...] = v_

    blk = pl.BlockSpec((tr, cols), lambda i: (i, 0))
    return _call(body, name, (rows // tr,), [blk] * 4, [blk] * 3, [_sds((rows, cols))] * 3, sem=("parallel",))(w, g, m, v)


SLAB_PART = 1024


def _pack(arrs):
    parts = []
    for a in arrs:
        flat = a.reshape(-1)
        pad = (-flat.shape[0]) % SLAB_PART
        if pad:
            flat = jnp.concatenate([flat, jnp.zeros((pad,), flat.dtype)])
        parts.append(flat.reshape(-1, 128))
    return jnp.concatenate(parts, axis=0)


def _unpack(slab, shapes):
    outs, row = [], 0
    for s in shapes:
        n = 1
        for d in s:
            n *= d
        rows = (n + SLAB_PART - 1) // SLAB_PART * (SLAB_PART // 128)
        outs.append(slab[row:row + rows].reshape(-1)[:n].reshape(s))
        row += rows
    return outs


def _unshard_last(g, lead):
    nl = len(lead)
    return jnp.transpose(g, tuple(range(1, nl + 1)) + (0, nl + 1)).reshape(tuple(lead) + (-1,))


def kernel(x, c, ctx, c_ctx, ada_w, ada_b, norm_g, w_in, hg_lb, hg_norm_g, rw_mu, rw_w0, rw_w2, rw_a0, rw_a2, rw_kk, rw_ka, rw_rk, rw_gn_g, rw_gn_b, w_hg_out, w_rw_out, w_out, final_g, loss_target, m_c_ctx, m_ada_w, m_ada_b, m_norm_g, m_w_in, m_hg_lb, m_hg_norm_g, m_rw_mu, m_rw_w0, m_rw_w2, m_rw_a0, m_rw_a2, m_rw_kk, m_rw_ka, m_rw_rk, m_rw_gn_g, m_rw_gn_b, m_w_hg_out, m_w_rw_out, m_w_out, m_final_g, v_c_ctx, v_ada_w, v_ada_b, v_norm_g, v_w_in, v_hg_lb, v_hg_norm_g, v_rw_mu, v_rw_w0, v_rw_w2, v_rw_a0, v_rw_a2, v_rw_kk, v_rw_ka, v_rw_rk, v_rw_gn_g, v_rw_gn_b, v_w_hg_out, v_w_rw_out, v_w_out, v_final_g):
    dm = _dims()
    me = 4 * lax.axis_index("x") + 2 * lax.axis_index("y") + lax.axis_index("c")

    big_shards = [w_in[0].T.astype(BF16), w_out[0].astype(BF16), w_hg_out[0].T.astype(BF16), w_rw_out[0].T.astype(BF16)]
    sharded_small = [hg_lb, rw_mu[0], rw_w0[0], rw_w2[0], rw_a0[0], rw_a2[0]]
    c_rows = jnp.concatenate([c, jnp.zeros((7, D), F32)], axis=0)
    gathered = _gather2(big_shards + [_pack(sharded_small), c_rows], "gather_weights")
    win_t, wout, whg_t, wrw_t = [g.reshape(-1, g.shape[2]) for g in gathered[:4]]
    per_dev = jax.vmap(lambda s: _unpack(s, [a.shape for a in sharded_small]))(gathered[4])
    hg_lb_f, mu_f, w0_f, w2_f, a0_f, a2_f = [_unshard_last(p, p.shape[1:-1]) for p in per_dev]

    ncol = ada_w.shape[2]
    cc16 = jnp.concatenate([gathered[5][:, 0], c_ctx[None], jnp.zeros((7, D), F32)], axis=0)
    mod_all = _gather2([_ada_fwd(cc16, ada_w[0])], "gather_mods")[0]
    mod_x = lax.dynamic_index_in_dim(mod_all, me, axis=1, keepdims=False).reshape(1, -1) + ada_b
    mod_c = mod_all[:, NDEV].reshape(1, -1) + ada_b
    mods = jnp.concatenate([mod_x, mod_c, jnp.zeros((6, 3 * D), F32)], axis=0)

    w = dict(win_t=win_t, wout=wout, whg_t=whg_t, wrw_t=wrw_t, norm_g=norm_g,
             lb0=hg_lb_f[:, 0], lb1=hg_lb_f[:, 1], hg_g=hg_norm_g, mu=mu_f, w0=w0_f, w2=w2_f, a0=a0_f, a2=a2_f,
             kk=rw_kk, ka=rw_ka, rk=rw_rk, gng=rw_gn_g, gnb=rw_gn_b, final_g=final_g[None])
    big_names = ["win_t", "wout", "whg_t", "wrw_t"]

    def start_exchange(big):
        blocks = [big[n].reshape(NDEV, -1, big[n].shape[1]) for n in big_names]
        got = _pair_exchange(blocks, "pair_grads")
        core = lax.axis_index("c").astype(jnp.int32).reshape(1)
        chip_sums = [_pair_add(g, r, core, "pair_add_" + n) for g, r, n in zip(blocks, got, big_names)]
        send_sems, recv_sems, held, lands, token = _chip_exchange_start(chip_sums, "chip_grads_start")
        return token, (send_sems, recv_sems, held, lands)

    loss_dev, grad_x, dmods, in_flight, small = _local_step(x[0], ctx[0], loss_target[0], mods, w, start_exchange)
    loss = lax.psum(loss_dev[0, 0], AXES)

    dmods_all = _gather2([dmods], "gather_dmods")[0]
    my_cols = lambda a: lax.dynamic_slice_in_dim(a, me * ncol, ncol, axis=1)
    g_ada, dcc16, d_ada_b = _ada_bwd(cc16, ada_w[0], my_cols(dmods_all[:, 0]), my_cols(dmods_all[:, 1]), dmods)
    small = dict(small, c_ctx=dcc16[NDEV:NDEV + 1], ada_b=d_ada_b)

    small_names = ["c_ctx", "ada_b", "norm_g", "lb0", "lb1", "hg_g", "mu", "w0", "w2", "a0", "a2", "kk", "ka", "rk", "gng", "gnb", "final_g"]
    small_slab = _pack([small[n] for n in small_names])
    small_all = _gather2([small_slab], "gather_small_grads")[0]
    small_sum = _reduce_sources(small_all, "reduce_small")
    sg = dict(zip(small_names, _unpack(small_sum, [small[n].shape for n in small_names])))

    chip_sums, recv = _chip_exchange_wait(*in_flight, small_sum, "chip_grads_wait")
    my_chip = (2 * lax.axis_index("x") + lax.axis_index("y")).astype(jnp.int32)
    others = jnp.arange(3, dtype=jnp.int32)
    slots = jnp.concatenate([my_chip.reshape(1), others + (others >= my_chip).astype(jnp.int32)])
    g_win_t, g_wout, g_whg_t, g_wrw_t = [_reduce_chips(cs, r, slots, "reduce_" + n) for cs, r, n in zip(chip_sums, recv, big_names)]

    def my_shard(full):
        n = full.shape[-1] // NDEV
        return lax.dynamic_slice_in_dim(full, me * n, n, axis=full.ndim - 1)

    grads = dict(
        c_ctx=sg["c_ctx"][0], ada_w=g_ada[None], ada_b=sg["ada_b"], norm_g=sg["norm_g"], w_in=g_win_t.T[None],
        hg_lb=my_shard(jnp.stack([sg["lb0"], sg["lb1"]], axis=1)), hg_norm_g=sg["hg_g"], rw_mu=my_shard(sg["mu"])[None],
        rw_w0=my_shard(sg["w0"])[None], rw_w2=my_shard(sg["w2"])[None], rw_a0=my_shard(sg["a0"])[None], rw_a2=my_shard(sg["a2"])[None],
        rw_kk=sg["kk"], rw_ka=sg["ka"], rw_rk=sg["rk"], rw_gn_g=sg["gng"], rw_gn_b=sg["gnb"],
        w_hg_out=g_whg_t.T[None], w_rw_out=g_wrw_t.T[None], w_out=g_wout[None], final_g=sg["final_g"][0])
    weights = dict(c_ctx=c_ctx, ada_w=ada_w, ada_b=ada_b, norm_g=norm_g, w_in=w_in, hg_lb=hg_lb, hg_norm_g=hg_norm_g, rw_mu=rw_mu,
                   rw_w0=rw_w0, rw_w2=rw_w2, rw_a0=rw_a0, rw_a2=rw_a2, rw_kk=rw_kk, rw_ka=rw_ka, rw_rk=rw_rk, rw_gn_g=rw_gn_g,
                   rw_gn_b=rw_gn_b, w_hg_out=w_hg_out, w_rw_out=w_rw_out, w_out=w_out, final_g=final_g)
    m_in = dict(zip(weights, (m_c_ctx, m_ada_w, m_ada_b, m_norm_g, m_w_in, m_hg_lb, m_hg_norm_g, m_rw_mu, m_rw_w0, m_rw_w2, m_rw_a0,
                              m_rw_a2, m_rw_kk, m_rw_ka, m_rw_rk, m_rw_gn_g, m_rw_gn_b, m_w_hg_out, m_w_rw_out, m_w_out, m_final_g)))
    v_in = dict(zip(weights, (v_c_ctx, v_ada_w, v_ada_b, v_norm_g, v_w_in, v_hg_lb, v_hg_norm_g, v_rw_mu, v_rw_w0, v_rw_w2, v_rw_a0,
                              v_rw_a2, v_rw_kk, v_rw_ka, v_rw_rk, v_rw_gn_g, v_rw_gn_b, v_w_hg_out, v_w_rw_out, v_w_out, v_final_g)))

    big_w = ["ada_w", "w_in", "w_hg_out", "w_rw_out", "w_out"]
    delta, new_m, new_v = {}, {}, {}
    for n in big_w:
        shp = weights[n].shape
        two = lambda a: a.reshape(shp[-2], shp[-1])
        d_, m_, v_ = _adamw(two(weights[n]), two(grads[n]), two(m_in[n]), two(v_in[n]), "adamw_" + n)
        delta[n], new_m[n], new_v[n] = d_.reshape(shp), m_.reshape(shp), v_.reshape(shp)
    rest = [n for n in weights if n not in big_w]
    shapes = [weights[n].shape for n in rest]
    d_s, m_s, v_s = _adamw(_pack([weights[n] for n in rest]), _pack([grads[n] for n in rest]), _pack([m_in[n] for n in rest]),
                           _pack([v_in[n] for n in rest]), "adamw_small")
    for n, d_, m_, v_ in zip(rest, _unpack(d_s, shapes), _unpack(m_s, shapes), _unpack(v_s, shapes)):
        delta[n], new_m[n], new_v[n] = d_, m_, v_

    order = list(weights)
    return (loss, grad_x[None], *[grads[n] for n in order], *[delta[n] for n in order],
            *[new_m[n] for n in order], *[new_v[n] for n in order])
```

```python
import functools

import jax
import jax.numpy as jnp
from jax import lax
from jax.experimental import pallas as pl
from jax.experimental.pallas import tpu as pltpu

D = 2048
T = 2048
TC = 256
GW = 64
C = 64
HGW = 1024
RWW = 1024
LORA = 64
NDEV = 8
TM = 256
TMV = 128
NORM_EPS = 1e-6
RW_GN_EPS = 64e-5
ADAM_LR, ADAM_B1, ADAM_B2, ADAM_EPS, ADAM_WD, ADAM_STEP = 0.001, 0.9, 0.999, 1e-08, 0.01, 10
WB = 256
V7X_VMEM_LIMIT = 56 * 1024 * 1024

F32 = jnp.float32
BF16 = jnp.bfloat16
AXES = ("x", "y", "c")


def _dims():
    tt = T + TC
    sh = 3 * RWW + 4 * LORA
    hgc = 5 * HGW
    return dict(TT=tt, NCC=TC // C, NCH=tt // C, HG_H=HGW // 128, RW_P=RWW // 128, SH=sh, HGC=hgc,
                NCOLS=hgc + sh + RWW + 2 * D, NCT=TC // TM, NCTV=TC // TMV)


def _mm(a, b):
    return _dot3(a, b, "nn")


def _mm_nt(a, b):
    return _dot3(a, b, "nt")


def _mm_tn(a, b):
    return _dot3(a, b, "tn")


def _split(x):
    hi = x.astype(BF16)
    return hi, (x - hi.astype(F32)).astype(BF16)


_FORMS = {"nn": (((1,), (0,)), ((), ())), "nt": (((1,), (1,)), ((), ())), "tn": (((0,), (0,)), ((), ()))}


def _dot3_raw(a, b, form):
    ah, al = _split(a)
    bh, bl = _split(b)
    d = lambda x, y: lax.dot_general(x, y, _FORMS[form], preferred_element_type=F32)
    return d(ah, bh) + (d(ah, bl) + d(al, bh))


@functools.partial(jax.custom_vjp, nondiff_argnums=(2,))
def _dot3(a, b, form):
    return _dot3_raw(a, b, form)


def _dot3_fwd(a, b, form):
    return _dot3_raw(a, b, form), (a, b)


def _dot3_bwd(form, res, g):
    a, b = res
    if form == "nn":
        return _dot3(g, b, "nt"), _dot3(a, g, "tn")
    if form == "nt":
        return _dot3(g, b, "nn"), _dot3(g, a, "tn")
    return _dot3(b, g, "nt"), _dot3(a, g, "nn")


_dot3.defvjp(_dot3_fwd, _dot3_bwd)


def _scan_cumsum(inc, x):
    return _cumsum_vjp(inc.astype(BF16), x)


def _cumsum_raw(inc, x, form):
    h1 = x.astype(BF16)
    r1 = x - h1.astype(F32)
    h2 = r1.astype(BF16)
    h3 = (r1 - h2.astype(F32)).astype(BF16)
    d = lambda y: lax.dot_general(inc, y, _FORMS[form], preferred_element_type=F32)
    return d(h1) + (d(h2) + d(h3))


@jax.custom_vjp
def _cumsum_vjp(inc, x):
    return _cumsum_raw(inc, x, "nn")


_cumsum_vjp.defvjp(lambda inc, x: (_cumsum_raw(inc, x, "nn"), inc),
                   lambda inc, g: (jnp.zeros_like(inc), _cumsum_raw(inc, g, "tn")))


def _silu(x):
    return x * jax.nn.sigmoid(x)


def _softplus(x):
    return jnp.maximum(x, 0.0) + jnp.log(1.0 + jnp.exp(-jnp.abs(x)))


def _iota2(shape, dim):
    return lax.broadcasted_iota(jnp.int32, shape, dim)


def _pair_mask():
    return (_iota2((128, 128), 0) < 64) == (_iota2((128, 128), 1) < 64)


def _seg64_sum(x):
    e = _pair_mask().astype(F32)
    parts = [_mm(x[:, g * 128:(g + 1) * 128], e) for g in range(x.shape[1] // 128)]
    return parts[0] if len(parts) == 1 else jnp.concatenate(parts, axis=1)


def _seg128_mean(x):
    parts = [jnp.broadcast_to(jnp.mean(x[:, g * 128:(g + 1) * 128], axis=1, keepdims=True), (x.shape[0], 128))
             for g in range(x.shape[1] // 128)]
    return parts[0] if len(parts) == 1 else jnp.concatenate(parts, axis=1)


def _scan_masks(d):
    lag = (_iota2((C, C), 0) - _iota2((C, C), 1)) * (1 - 2 * d)
    return lag >= 0, lag > 0


def _tri_inverse(a):
    eye = (_iota2((C, C), 0) == _iota2((C, C), 1)).astype(F32)
    p = [-x for x in a]
    tm = [eye + x for x in p]
    n = 2
    while n < C:
        p = [_mm(x, x) for x in p]
        tm = [t + _mm(t, x) for t, x in zip(tm, p)]
        n *= 2
    return tm


@jax.custom_vjp
def _tri_solve(tm, a, rhs):
    return [_mm(t, r) for t, r in zip(tm, rhs)]


def _tri_solve_fwd(tm, a, rhs):
    u = [_mm(t, r) for t, r in zip(tm, rhs)]
    return u, (tm, u)


def _tri_solve_bwd(res, g):
    tm, u = res
    d_rhs = [_mm_tn(t, x) for t, x in zip(tm, g)]
    return [jnp.zeros_like(t) for t in tm], [-_mm_nt(d, x) for d, x in zip(d_rhs, u)], d_rhs


_tri_solve.defvjp(_tri_solve_fwd, _tri_solve_bwd)


def _rw_chunks(chains, tms=None):
    s0, r, lw, kap, b, v, kd, ds = zip(*chains)
    ids = range(len(chains))
    inc, strict = zip(*[_scan_masks(d) for d in ds])
    lane = _iota2((1, 128), 1)
    hm = ((lane < 64).astype(F32), (lane >= 64).astype(F32))
    eye = (_iota2((C, C), 0) == _iota2((C, C), 1)).astype(F32)
    lc = [_scan_cumsum(inc[i], lw[i]) for i in ids]
    ltot = [jnp.sum(lw[i], axis=0, keepdims=True) for i in ids]
    rt = [r[i] * jnp.exp(lc[i]) for i in ids]
    kt = [kap[i] * jnp.exp(lc[i] - lw[i]) for i in ids]
    einv = [jnp.exp(-lc[i]) for i in ids]
    kh = [kd[i] * einv[i] for i in ids]
    bh = [b[i] * einv[i] for i in ids]
    eend = [jnp.exp(ltot[i] - lc[i]) for i in ids]
    kbar = [kd[i] * eend[i] for i in ids]
    bbar = [b[i] * eend[i] for i in ids]
    lhs = [jnp.concatenate([kt[i] * hm[0], kt[i] * hm[1], rt[i] * hm[0], rt[i] * hm[1]], axis=0) for i in ids]
    gk = [_mm_nt(lhs[i], kh[i]) for i in ids]
    gb = [_mm_nt(lhs[i], bh[i]) for i in ids]
    ks = [_mm_nt(kt[i], s0[i]) for i in ids]
    ys = [_mm_nt(rt[i], s0[i]) for i in ids]
    heads = [(i, h) for i in ids for h in range(2)]
    akk = {c: jnp.where(strict[c[0]], gk[c[0]][c[1] * C:(c[1] + 1) * C], 0.0) for c in heads}
    akb = {c: jnp.where(strict[c[0]], gb[c[0]][c[1] * C:(c[1] + 1) * C], 0.0) for c in heads}
    ark = {c: jnp.where(inc[c[0]], gk[c[0]][(2 + c[1]) * C:(3 + c[1]) * C], 0.0) for c in heads}
    arb = {c: jnp.where(inc[c[0]], gb[c[0]][(2 + c[1]) * C:(3 + c[1]) * C], 0.0) for c in heads}
    vh = {c: v[c[0]] * hm[c[1]] for c in heads}
    av = {c: _mm(akk[c], vh[c]) for c in heads}
    rhs = {c: ks[c[0]] * hm[c[1]] + av[c] for c in heads}
    akb_list = [akb[c] for c in heads]
    tm = _tri_inverse(akb_list) if tms is None else [tms[c] for c in heads]
    uh = dict(zip(heads, _tri_solve(tm, akb_list, [rhs[c] for c in heads])))
    yv = {c: _mm(ark[c], vh[c]) for c in heads}
    yu = {c: _mm(arb[c], uh[c]) for c in heads}
    u = [uh[i, 0] + uh[i, 1] for i in ids]
    y = [ys[i] + (yv[i, 0] - yu[i, 0]) + (yv[i, 1] - yu[i, 1]) for i in ids]
    upd = [_mm_tn(jnp.concatenate([v[i], -u[i]], axis=0), jnp.concatenate([kbar[i], bbar[i]], axis=0)) for i in ids]
    s1 = [s0[i] * jnp.exp(ltot[i]) + jnp.where(_pair_mask(), upd[i], 0.0) for i in ids]
    if tms is None:
        return [(y[i], s1[i], tm[2 * i], tm[2 * i + 1]) for i in ids]
    return [(y[i], s1[i]) for i in ids]


HG_SUB = 16


def _hg_chunks(chains):
    s0, qr, iv, f, lb, ds = zip(*chains)
    ids = range(len(chains))
    inc = [_scan_masks(d)[0] for d in ds]
    q = [_silu(qr[i]) for i in ids]
    fg = [lb[i] + (1.0 - lb[i]) * jax.nn.sigmoid(f[i]) for i in ids]
    k = [1.0 - fg[i] for i in ids]
    g = [jnp.log(fg[i]) for i in ids]
    bc = [_scan_cumsum(inc[i], g[i]) for i in ids]
    btot = [jnp.sum(g[i], axis=0, keepdims=True) for i in ids]
    o_inter = [_mm_nt(q[i] * jnp.exp(bc[i]), s0[i]) for i in ids]
    rowi = _iota2((C, 1), 0)
    outs = [[] for _ in ids]
    for blk in range(C // HG_SUB):
        lo, hi = blk * HG_SUB, (blk + 1) * HG_SUB
        first = [lo if ds[i] == 0 else hi - 1 for i in ids]
        ref = [jnp.sum(jnp.where(rowi == first[i], bc[i] - g[i], 0.0), axis=0, keepdims=True) for i in ids]
        qi = [q[i][lo:hi] * jnp.exp(bc[i][lo:hi] - ref[i]) for i in ids]
        src = [(rowi < hi) if ds[i] == 0 else (rowi >= lo) for i in ids]
        ke = [jnp.where(src[i], k[i] * jnp.exp(jnp.where(src[i], ref[i] - bc[i], 0.0)), 0.0) for i in ids]
        a = [jnp.where(inc[i][lo:hi], _mm_nt(qi[i], ke[i]), 0.0) for i in ids]
        part = [_mm(a[i], iv[i]) for i in ids]
        for i in ids:
            outs[i].append(part[i])
    o = [o_inter[i] + jnp.concatenate(outs[i], axis=0) for i in ids]
    upd = [_mm_tn(iv[i], k[i] * jnp.exp(btot[i] - bc[i])) for i in ids]
    s1 = [s0[i] * jnp.exp(btot[i]) + upd[i] for i in ids]
    return [(o[i], s1[i]) for i in ids]


def _lockstep(fn, ds, nargs):
    def flat_fn(*flat):
        return tuple(fn([tuple(flat[i * nargs:(i + 1) * nargs]) + (d,) for i, d in enumerate(ds)]))

    return flat_fn


def _h_math(x, ng, scale, shift):
    return x * lax.rsqrt(jnp.mean(x * x, axis=-1, keepdims=True) + NORM_EPS) * ng * (1.0 + scale) + shift


def _hg_post_math(o0, o1, z, g):
    o = o0 + o1
    on = o * lax.rsqrt(_seg128_mean(o * o) + NORM_EPS)
    return on * g * _silu(z)


def _rw_prep_math(k, pw0, pw1, pa0, pa1, w0, a0, kkp, kap_p):
    kk = k * kkp
    kap = kk * lax.rsqrt(_seg64_sum(kk * kk) + 1e-12)
    outs = []
    for d, (pw, pa) in enumerate(((pw0, pa0), (pw1, pa1))):
        w_log = -_softplus(-(w0[d:d + 1] + pw)) - 0.5
        lw = -jnp.exp(w_log)
        a = jax.nn.sigmoid(a0[d:d + 1] + pa)
        kd = k * (1.0 + (a - 1.0) * kap_p)
        outs += [lw, kap * a, kd]
    return (kap, *outs)


def _rw_post_math(y0, y1, kd0, kd1, r, v, z, rk, gng, gnb):
    ys, ksum = y0 + y1, kd0 + kd1
    mean = _seg64_sum(ys) * (1.0 / 64.0)
    cen = ys - mean
    var = _seg64_sum(cen * cen) * (1.0 / 64.0)
    yn = cen * lax.rsqrt(var + RW_GN_EPS) * gng + gnb
    bonus = _seg64_sum(r * ksum * rk) * v
    return (yn + bonus) * _silu(z)


def _head_math(x, mo, gate, fg, tgt):
    xo = x + gate * mo
    y = xo * lax.rsqrt(jnp.mean(xo * xo, axis=-1, keepdims=True) + NORM_EPS) * fg
    err = y - tgt
    return 0.5 * jnp.sum(jnp.mean(err * err, axis=-1, keepdims=True), axis=0, keepdims=True)


def _gate_math(ghg, grw, a, b):
    return jax.nn.sigmoid(ghg) * a + jax.nn.sigmoid(grw) * b


def _call(body, name, grid, in_specs, out_specs, out_shape, scratch=(), sem=None, aliases=None):
    return pl.pallas_call(
        body, name=name, grid=grid, in_specs=in_specs, out_specs=out_specs, out_shape=out_shape,
        scratch_shapes=list(scratch), input_output_aliases=aliases or {},
        compiler_params=pltpu.CompilerParams(dimension_semantics=sem, vmem_limit_bytes=V7X_VMEM_LIMIT))


def _full(shape):
    nd = len(shape)
    return pl.BlockSpec(shape, lambda *_: (0,) * nd)


def _sds(shape, dtype=F32):
    return jax.ShapeDtypeStruct(shape, dtype)


def _proj_nt(h, wt, wt_blk0, nblk, row0, nrows, name):
    def body(h_ref, w_ref, o_ref):
        o_ref[...] = lax.dot_general(h_ref[row0:row0 + nrows, :], w_ref[...], (((1,), (1,)), ((), ())), preferred_element_type=F32)

    return _call(body, name, (nblk,), [_full(h.shape), pl.BlockSpec((WB, wt.shape[1]), lambda j: (wt_blk0 + j, 0))],
                 pl.BlockSpec((nrows, WB), lambda j: (0, j)), _sds((nrows, nblk * WB)), sem=("parallel",))(h, wt)


def _dw_tn(a, b, row0, out_rows, out_blk0, name, prev=None):
    m, n = a.shape
    k2 = b.shape[1]
    nblk = n // WB

    def body(*refs):
        a_ref, b_ref, o_ref = refs[0], refs[1], refs[-1]
        o_ref[...] = lax.dot_general(a_ref[...], b_ref[row0:row0 + m, :], (((0,), (0,)), ((), ())),
                                     preferred_element_type=F32).astype(BF16)

    in_specs = [pl.BlockSpec((m, WB), lambda j: (0, j)), _full(b.shape)]
    args = [a, b]
    aliases = None
    if prev is not None:
        in_specs.append(pl.BlockSpec(memory_space=pl.ANY))
        args.append(prev)
        aliases = {2: 0}
    return _call(body, name, (nblk,), in_specs, pl.BlockSpec((WB, k2), lambda j: (out_blk0 + j, 0)), _sds((out_rows, k2), BF16),
                 sem=("arbitrary",), aliases=aliases)(*args)


def _mm_nn(a, w, w_blk0, name, after=None):
    m, kc = a.shape
    n = w.shape[1]
    kb = max(k for k in (WB, 2 * WB, 4 * WB) if kc % k == 0 and (w_blk0 * WB) % k == 0)
    w_blk0 = w_blk0 * WB // kb
    tmb = 3 * TM if m % (3 * TM) == 0 else (2 * TM if m % (2 * TM) == 0 else TM)

    def body(a_ref, w_ref, *rest):
        o_ref = rest[-1]

        @pl.when(pl.program_id(1) == 0)
        def _():
            o_ref[...] = jnp.zeros_like(o_ref)

        o_ref[...] += jnp.dot(a_ref[...], w_ref[...], preferred_element_type=F32)

    extra = [] if after is None else [pl.BlockSpec(memory_space=pl.ANY)]
    return _call(body, name, (m // tmb, kc // kb),
                 [pl.BlockSpec((tmb, kb), lambda i, k: (i, k)), pl.BlockSpec((kb, n), lambda i, k: (w_blk0 + k, 0))] + extra,
                 pl.BlockSpec((tmb, n), lambda i, k: (i, 0)), _sds((m, n)), sem=("parallel", "arbitrary"))(
        a, w, *([] if after is None else [after]))


def _ada_fwd(cc16, ada_shard):
    ncol = ada_shard.shape[1]

    def body(c_ref, w_ref, o_ref):
        o_ref[...] = _mm(_silu(c_ref[...]), w_ref[...])

    return _call(body, "ada_fwd", (1,), [_full((16, D)), _full((D, ncol))], _full((16, ncol)), _sds((16, ncol)))(cc16, ada_shard)


def _ada_bwd(cc16, ada_shard, dm_x, dm_c, dmods):
    ncol = ada_shard.shape[1]

    def body(c_ref, w_ref, dx_ref, dc_ref, dm_ref, dw_ref, dcc_ref, db_ref):
        s, vjp = jax.vjp(_silu, c_ref[...])
        dc_tot = dc_ref[0:1, :]
        for j in range(1, NDEV):
            dc_tot = dc_tot + dc_ref[j:j + 1, :]
        row = _iota2((8, 1), 0)
        dm16 = jnp.concatenate([dx_ref[...], jnp.where(row == 0, dc_tot, 0.0)], axis=0)
        dw_ref[...] = _mm_tn(s, dm16)
        dcc_ref[...] = vjp(_mm_nt(dm16, w_ref[...]))[0]
        db_ref[...] = dm_ref[0:1, :] + dm_ref[1:2, :]

    return _call(body, "ada_bwd", (1,), [_full((16, D)), _full((D, ncol)), _full((8, ncol)), _full((8, ncol)), _full((8, 3 * D))],
                 [_full((D, ncol)), _full((16, D)), _full((1, 3 * D))], [_sds((D, ncol)), _sds((16, D)), _sds((1, 3 * D))])(
        cc16, ada_shard, dm_x, dm_c, dmods)


def _sel_mod(mods_ref, is_ctx, lo):
    return jnp.where(is_ctx, mods_ref[1:2, lo:lo + D], mods_ref[0:1, lo:lo + D])


def _h_fwd(xcat, mods, norm_g):
    dm = _dims()

    def body(x_ref, m_ref, g_ref, o_ref):
        is_ctx = pl.program_id(0) < dm["NCT"]
        o_ref[...] = _h_math(x_ref[...], g_ref[...], _sel_mod(m_ref, is_ctx, D), _sel_mod(m_ref, is_ctx, 0)).astype(BF16)

    return _call(body, "h_fwd", (dm["TT"] // TM,),
                 [pl.BlockSpec((TM, D), lambda i: (i, 0)), _full((8, 3 * D)), _full((1, D))],
                 pl.BlockSpec((TM, D), lambda i: (i, 0)), _sds((dm["TT"], D), BF16), sem=("parallel",))(xcat, mods, norm_g)


def _chunk_of(d, s):
    dm = _dims()
    ncc, nch = dm["NCC"], dm["NCH"]
    return s if d == 0 else jnp.where(s < ncc, ncc - 1 - s, nch - 1 - (s - ncc))


HG_GROUP = 4
RW_GROUP = 4


def _dir_specs(make):
    return [make(d) for d in range(2)]


def _lanes(j):
    return slice(j * 128, (j + 1) * 128)


def _hg_scan_fwd(p_hg, lb3):
    dm = _dims()
    nh, nch, tt = dm["HG_H"], dm["NCH"], dm["TT"]
    pp = min(HG_GROUP, nh)
    lw_ = 128 * pp

    def body(q0, q1, i0, i1, f0, f1, lb0, lb1, o0, o1, ck_ref, s_ref):
        @pl.when(pl.program_id(1) == 0)
        def _():
            s_ref[...] = jnp.zeros_like(s_ref)

        keys, chains = [], []
        for d, refs in enumerate(((q0, i0, f0, lb0), (q1, i1, f1, lb1))):
            vals = [r[...] for r in refs]
            for j in range(pp):
                keys.append((d, j))
                chains.append((s_ref[d, j], *[a[:, _lanes(j)] for a in vals], d))
        for (d, j), chain, (out, s1) in zip(keys, chains, _hg_chunks(chains)):
            ck_ref[d, j] = chain[0]
            (o0, o1)[d][:, _lanes(j)] = out
            s_ref[d, j] = s1

    blk = lambda off: _dir_specs(lambda d: pl.BlockSpec((C, lw_), lambda h, s: (_chunk_of(d, s), off * nh // pp + h)))
    return _call(body, "hg_scan_fwd", (nh // pp, nch),
                 blk(0) + blk(1) + _dir_specs(lambda d: pl.BlockSpec((C, lw_), lambda h, s: (_chunk_of(d, s), (2 + d) * nh // pp + h)))
                 + _dir_specs(lambda d: pl.BlockSpec((None, 1, lw_), lambda h, s: (d, 0, h))),
                 _dir_specs(lambda d: pl.BlockSpec((C, lw_), lambda h, s: (_chunk_of(d, s), h)))
                 + [pl.BlockSpec((2, pp, None, 128, 128), lambda h, s: (0, h, s, 0, 0))],
                 [_sds((tt, HGW)), _sds((tt, HGW)), _sds((2, nh, nch, 128, 128))],
                 scratch=[pltpu.VMEM((2, pp, 128, 128), F32)], sem=("parallel", "arbitrary"))(*([p_hg] * 6), lb3, lb3)


def _hg_scan_bwd(p_hg, lb3, ck, do):
    dm = _dims()
    nh, nch, tt = dm["HG_H"], dm["NCH"], dm["TT"]
    pp = min(HG_GROUP, nh)
    lw_ = 128 * pp

    def body(q0, q1, i0, i1, f0, f1, lb0, lb1, ck_ref, do0, do1, dq0, dq1, di0, di1, df0, df1, dlb_ref, ds_ref):
        @pl.when(pl.program_id(1) == 0)
        def _():
            ds_ref[...] = jnp.zeros_like(ds_ref)
            dlb_ref[...] = jnp.zeros_like(dlb_ref)

        keys, flat, cts = [], [], []
        for d, refs in enumerate(((q0, i0, f0, lb0, do0), (q1, i1, f1, lb1, do1))):
            *vals, do_ = [r[...] for r in refs]
            for j in range(pp):
                keys.append((d, j))
                flat += [ck_ref[d, j], *[a[:, _lanes(j)] for a in vals]]
                cts.append((do_[:, _lanes(j)], ds_ref[d, j]))
        _, vjp = jax.vjp(_lockstep(_hg_chunks, [d for d, _ in keys], 5), *flat)
        grads = vjp(tuple(cts))
        for n, (d, j) in enumerate(keys):
            ds0, dq_, di_, df_, dlb = grads[5 * n:5 * n + 5]
            (dq0, dq1)[d][:, _lanes(j)] = dq_
            (di0, di1)[d][:, _lanes(j)] = di_
            (df0, df1)[d][:, _lanes(j)] = df_
            dlb_ref[d, :, _lanes(j)] += dlb
            ds_ref[d, j] = ds0

    ch = lambda d, s: _chunk_of(d, nch - 1 - s)
    blk = lambda off: _dir_specs(lambda d: pl.BlockSpec((C, lw_), lambda h, s: (ch(d, s), off * nh // pp + h)))
    tok = _dir_specs(lambda d: pl.BlockSpec((C, lw_), lambda h, s: (ch(d, s), h)))
    return _call(body, "hg_scan_bwd", (nh // pp, nch),
                 blk(0) + blk(1) + _dir_specs(lambda d: pl.BlockSpec((C, lw_), lambda h, s: (ch(d, s), (2 + d) * nh // pp + h)))
                 + _dir_specs(lambda d: pl.BlockSpec((None, 1, lw_), lambda h, s: (d, 0, h)))
                 + [pl.BlockSpec((2, pp, None, 128, 128), lambda h, s: (0, h, nch - 1 - s, 0, 0))] + tok,
                 tok * 3 + [pl.BlockSpec((2, 1, lw_), lambda h, s: (0, 0, h))],
                 [_sds((tt, HGW))] * 6 + [_sds((2, 1, HGW))],
                 scratch=[pltpu.VMEM((2, pp, 128, 128), F32)], sem=("parallel", "arbitrary"))(*([p_hg] * 6), lb3, lb3, ck, do, do)


def _rw_scan_fwd(sh, kap, lw, b, kd):
    dm = _dims()
    npair, nch, tt = dm["RW_P"], dm["NCH"], dm["TT"]
    pp = min(RW_GROUP, npair)
    lw_ = 128 * pp

    def body(r0, r1, v0, v1, ka0, ka1, lw0, lw1, b0, b1, kd0, kd1, y0, y1, ck_ref, tm_ref, s_ref):
        @pl.when(pl.program_id(1) == 0)
        def _():
            s_ref[...] = jnp.zeros_like(s_ref)

        keys, chains = [], []
        for d, refs in enumerate(((r0, lw0, ka0, b0, v0, kd0), (r1, lw1, ka1, b1, v1, kd1))):
            vals = [r[...] for r in refs]
            for j in range(pp):
                keys.append((d, j))
                chains.append((s_ref[d, j], *[a[:, _lanes(j)] for a in vals], d))
        for (d, j), chain, (out, s1, tm_a, tm_b) in zip(keys, chains, _rw_chunks(chains)):
            ck_ref[d, j] = chain[0]
            tm_ref[d, j, 0] = tm_a
            tm_ref[d, j, 1] = tm_b
            (y0, y1)[d][:, _lanes(j)] = out
            s_ref[d, j] = s1

    two = lambda off: _dir_specs(lambda d: pl.BlockSpec((C, lw_), lambda p, s: (_chunk_of(d, s), off * npair // pp + p)))
    three = _dir_specs(lambda d: pl.BlockSpec((None, C, lw_), lambda p, s: (d, _chunk_of(d, s), p)))
    return _call(body, "rw_scan_fwd", (npair // pp, nch), two(0) + two(2) + two(0) + three * 3,
                 two(0) + [pl.BlockSpec((2, pp, None, 128, 128), lambda p, s: (0, p, s, 0, 0)),
                           pl.BlockSpec((2, pp, None, 2, C, C), lambda p, s: (0, p, s, 0, 0, 0))],
                 [_sds((tt, RWW)), _sds((tt, RWW)), _sds((2, npair, nch, 128, 128)), _sds((2, npair, nch, 2, C, C))],
                 scratch=[pltpu.VMEM((2, pp, 128, 128), F32)], sem=("parallel", "arbitrary"))(
        sh, sh, sh, sh, kap, kap, lw, lw, b, b, kd, kd)


def _rw_scan_bwd(sh, kap, lw, b, kd, ck, tm_ck, dy):
    dm = _dims()
    npair, nch, tt = dm["RW_P"], dm["NCH"], dm["TT"]
    pp = min(RW_GROUP, npair)
    lw_ = 128 * pp

    def body(r0, r1, v0, v1, ka0, ka1, lw0, lw1, b0, b1, kd0, kd1, ck_ref, tm_ref, dy0, dy1,
             dr0, dr1, dv0, dv1, dka0, dka1, dlw0, dlw1, db0, db1, dkd0, dkd1, ds_ref):
        @pl.when(pl.program_id(1) == 0)
        def _():
            ds_ref[...] = jnp.zeros_like(ds_ref)

        keys, flat, cts, tms = [], [], [], {}
        for d, refs in enumerate(((r0, lw0, ka0, b0, v0, kd0, dy0), (r1, lw1, ka1, b1, v1, kd1, dy1))):
            *vals, dy_ = [r[...] for r in refs]
            for j in range(pp):
                tms[len(keys), 0], tms[len(keys), 1] = tm_ref[d, j, 0], tm_ref[d, j, 1]
                keys.append((d, j))
                flat += [ck_ref[d, j], *[a[:, _lanes(j)] for a in vals]]
                cts.append((dy_[:, _lanes(j)], ds_ref[d, j]))
        _, vjp = jax.vjp(_lockstep(functools.partial(_rw_chunks, tms=tms), [d for d, _ in keys], 7), *flat)
        grads = vjp(tuple(cts))
        outs = ((dr0, dlw0, dka0, db0, dv0, dkd0), (dr1, dlw1, dka1, db1, dv1, dkd1))
        for n, (d, j) in enumerate(keys):
            ds0, *g_in = grads[7 * n:7 * n + 7]
            for o_ref, g in zip(outs[d], g_in):
                o_ref[:, _lanes(j)] = g
            ds_ref[d, j] = ds0

    ch = lambda d, s: _chunk_of(d, nch - 1 - s)
    two = lambda off: _dir_specs(lambda d: pl.BlockSpec((C, lw_), lambda p, s: (ch(d, s), off * npair // pp + p)))
    three = _dir_specs(lambda d: pl.BlockSpec((None, C, lw_), lambda p, s: (d, ch(d, s), p)))
    return _call(body, "rw_scan_bwd", (npair // pp, nch),
                 two(0) + two(2) + two(0) + three * 3
                 + [pl.BlockSpec((2, pp, None, 128, 128), lambda p, s: (0, p, nch - 1 - s, 0, 0)),
                    pl.BlockSpec((2, pp, None, 2, C, C), lambda p, s: (0, p, nch - 1 - s, 0, 0, 0))] + two(0),
                 two(0) * 6, [_sds((tt, RWW))] * 12,
                 scratch=[pltpu.VMEM((2, pp, 128, 128), F32)], sem=("parallel", "arbitrary"))(
        sh, sh, sh, sh, kap, kap, lw, lw, b, b, kd, kd, ck, tm_ck, dy, dy)


def _tile_pos(i):
    dm = _dims()
    is_ctx = i < dm["NCT"]
    rows = lax.broadcasted_iota(jnp.int32, (TM, 1), 0)
    tl = rows + (i - dm["NCT"]) * TM
    width = jnp.where(is_ctx, TC, GW)
    assert TC & (TC - 1) == 0 and GW & (GW - 1) == 0 and TM % GW == 0
    colp = rows & (width - 1)
    return is_ctx, rows, tl, colp, width


def _shift_lr(cur, i):
    _, _, _, colp, width = _tile_pos(i)
    left = jnp.where(colp == 0, 0.0, pltpu.roll(cur, 1, 0))
    right = jnp.where(colp == width - 1, 0.0, pltpu.roll(cur, TM - 1, 0))
    return left, right


def _shift_ud(cur, prv, nxt, i):
    is_ctx, rows, tl, _, _ = _tile_pos(i)
    if TM > GW:
        up = jnp.where(rows >= GW, pltpu.roll(cur, GW, 0), pltpu.roll(prv, GW, 0))
        down = jnp.where(rows < TM - GW, pltpu.roll(cur, TM - GW, 0), pltpu.roll(nxt, TM - GW, 0))
    else:
        up, down = prv, nxt
    up = jnp.where(jnp.logical_or(is_ctx, tl < GW), 0.0, up)
    down = jnp.where(jnp.logical_or(is_ctx, tl >= T - GW), 0.0, down)
    return up, down


def _shift_specs():
    dm = _dims()
    nt = dm["TT"] // TM
    cw = max(w for w in range(128, 1664 + 1, 128) if dm["SH"] % w == 0)
    cur = pl.BlockSpec((TM, cw), lambda j, i: (i, j))
    prv = pl.BlockSpec((TM, cw), lambda j, i: (jnp.maximum(i - 1, 0), j))
    nxt = pl.BlockSpec((TM, cw), lambda j, i: (jnp.minimum(i + 1, nt - 1), j))
    mu = pl.BlockSpec((4, cw), lambda j, i: (0, j))
    return nt, cw, cur, prv, nxt, mu


def _shift_fwd(p_rw, mu):
    dm = _dims()
    nt, cw, cur, prv, nxt, mus = _shift_specs()

    def body(c_ref, p_ref, n_ref, mu_ref, o_ref):
        i = pl.program_id(1)
        p, m = c_ref[...], mu_ref[...]
        left, right = _shift_lr(p, i)
        up, down = _shift_ud(p, p_ref[...], n_ref[...], i)
        vert = jnp.where(i < dm["NCT"], 0.0, 1.0)
        o_ref[...] = (p * (1.0 - m[0:1] - m[1:2] - vert * (m[2:3] + m[3:4]))
                      + m[0:1] * left + m[1:2] * right + m[2:3] * up + m[3:4] * down)

    return _call(body, "shift_fwd", (dm["SH"] // cw, nt), [cur, prv, nxt, mus], cur, _sds((dm["TT"], dm["SH"])),
                 sem=("parallel", "parallel"))(p_rw, p_rw, p_rw, mu)


def _shift_bwd(p_rw, dsh, mu):
    dm = _dims()
    nt, cw, cur, prv, nxt, mus = _shift_specs()

    def body(c_ref, p_ref, n_ref, gc_ref, gp_ref, gn_ref, mu_ref, dp_ref, dmu_ref):
        i = pl.program_id(1)
        p, g, m = c_ref[...], gc_ref[...], mu_ref[...]
        vert = jnp.where(i < dm["NCT"], 0.0, 1.0)
        _, from_right = _shift_lr(m[0:1] * g, i)
        from_left, _ = _shift_lr(m[1:2] * g, i)
        _, from_down = _shift_ud(m[2:3] * g, g, m[2:3] * gn_ref[...], i)
        from_up, _ = _shift_ud(m[3:4] * g, m[3:4] * gp_ref[...], g, i)
        dp = g * (1.0 - m[0:1] - m[1:2] - vert * (m[2:3] + m[3:4])) + from_right + from_left + from_down + from_up
        dp_ref[...] = dp.astype(BF16)

        @pl.when(i == 0)
        def _():
            dmu_ref[...] = jnp.zeros_like(dmu_ref)

        left, right = _shift_lr(p, i)
        up, down = _shift_ud(p, p_ref[...], n_ref[...], i)
        s = lambda a: jnp.sum(a, axis=0, keepdims=True)
        dmu_ref[...] += jnp.concatenate([s(g * (left - p)), s(g * (right - p)), vert * s(g * (up - p)), vert * s(g * (down - p))], axis=0)

    return _call(body, "shift_bwd", (dm["SH"] // cw, nt), [cur, prv, nxt, cur, prv, nxt, mus],
                 [cur, pl.BlockSpec((4, cw), lambda j, i: (0, j))], [_sds((dm["TT"], dm["SH"]), BF16), _sds((4, dm["SH"]))],
                 sem=("parallel", "arbitrary"))(p_rw, p_rw, p_rw, dsh, dsh, dsh, mu)


def _acc_out(ref, first, vals):
    @pl.when(first)
    def _():
        for r in ref:
            r[...] = jnp.zeros_like(r)

    for r, v in zip(ref, vals):
        r[...] += v


def _rowsum(a):
    return jnp.sum(a, axis=0, keepdims=True)


def _rw_prep_specs():
    dm = _dims()
    rw = RWW // 128
    tok = lambda width, cb: pl.BlockSpec((TMV, width), lambda i: (i, cb))
    ins = [pl.BlockSpec((TMV, RWW), lambda i: (i, 1)), tok(128, 3 * rw), tok(128, 3 * rw + 1),
           _full((2, RWW)), _full((128, RWW)), _full((128, RWW)), _full((2, RWW)), _full((128, RWW)), _full((128, RWW)),
           _full((1, RWW)), _full((1, RWW))]
    return dm, ins


def _rw_prep_fwd(sh, w0, w2p0, w2p1, a0, a2p0, a2p1, kkp, kap_p):
    dm, ins = _rw_prep_specs()
    tt = dm["TT"]

    def body(k_ref, low_ref, loa_ref, w0_ref, w20_ref, w21_ref, a0_ref, a20_ref, a21_ref, kk_ref, ka_ref, kap_ref, lw_ref, b_ref, kd_ref):
        tw, la = jnp.tanh(low_ref[...]), loa_ref[...]
        kap, lw0, b0, kd0, lw1, b1, kd1 = _rw_prep_math(
            k_ref[...], _mm(tw, w20_ref[...]), _mm(tw, w21_ref[...]), _mm(la, a20_ref[...]), _mm(la, a21_ref[...]),
            w0_ref[...], a0_ref[...], kk_ref[...], ka_ref[...])
        kap_ref[...] = kap
        lw_ref[0], lw_ref[1] = lw0, lw1
        b_ref[0], b_ref[1] = b0, b1
        kd_ref[0], kd_ref[1] = kd0, kd1

    two = pl.BlockSpec((2, TMV, RWW), lambda i: (0, i, 0))
    return _call(body, "rw_prep_fwd", (tt // TMV,), ins, [pl.BlockSpec((TMV, RWW), lambda i: (i, 0)), two, two, two],
                 [_sds((tt, RWW)), _sds((2, tt, RWW)), _sds((2, tt, RWW)), _sds((2, tt, RWW))], sem=("parallel",))(
        sh, sh, sh, w0, w2p0, w2p1, a0, a2p0, a2p1, kkp, kap_p)


def _rw_prep_bwd(sh, w0, w2p0, w2p1, a0, a2p0, a2p1, kkp, kap_p, scan_grads, dkd_p, dr_p, dv_p):
    dm, ins = _rw_prep_specs()
    tt, sh_w = dm["TT"], dm["SH"]
    one = pl.BlockSpec((TMV, RWW), lambda i: (i, 0))

    def body(k_ref, low_ref, loa_ref, w0_ref, w20_ref, w21_ref, a0_ref, a20_ref, a21_ref, kk_ref, ka_ref,
             dr0, dr1, dv0, dv1, dka0, dka1, dlw0, dlw1, db0, db1, dkd0, dkd1, dkdp_ref, drp_ref, dvp_ref,
             dsh_ref, dw0_ref, dw20_ref, dw21_ref, da0_ref, da20_ref, da21_ref, dkk_ref, dka_ref):
        tw, la = jnp.tanh(low_ref[...]), loa_ref[...]
        w20, w21, a20, a21 = w20_ref[...], w21_ref[...], a20_ref[...], a21_ref[...]
        _, vjp = jax.vjp(_rw_prep_math, k_ref[...], _mm(tw, w20), _mm(tw, w21), _mm(la, a20), _mm(la, a21),
                         w0_ref[...], a0_ref[...], kk_ref[...], ka_ref[...])
        dkp = dkdp_ref[...]
        dk, dpw0, dpw1, dpa0, dpa1, dw0, da0, dkk, dka = vjp(
            (dka0[...] + dka1[...], dlw0[...], db0[...], dkd0[...] + dkp, dlw1[...], db1[...], dkd1[...] + dkp))
        twt, lat = tw.T, la.T
        dtw = _mm_nt(dpw0, w20) + _mm_nt(dpw1, w21)
        dsh_ref[:, 0:RWW] = dr0[...] + dr1[...] + drp_ref[...]
        dsh_ref[:, RWW:2 * RWW] = dk
        dsh_ref[:, 2 * RWW:3 * RWW] = dv0[...] + dv1[...] + dvp_ref[...]
        dsh_ref[:, 3 * RWW:3 * RWW + 128] = dtw * (1.0 - tw * tw)
        dsh_ref[:, 3 * RWW + 128:3 * RWW + 256] = _mm_nt(dpa0, a20) + _mm_nt(dpa1, a21)
        _acc_out((dw0_ref, dw20_ref, dw21_ref, da0_ref, da20_ref, da21_ref, dkk_ref, dka_ref), pl.program_id(0) == 0,
                 (dw0, _mm(twt, dpw0), _mm(twt, dpw1), da0, _mm(lat, dpa0), _mm(lat, dpa1), dkk, dka))

    par = [_full((2, RWW)), _full((128, RWW)), _full((128, RWW)), _full((2, RWW)), _full((128, RWW)), _full((128, RWW)),
           _full((1, RWW)), _full((1, RWW))]
    return _call(body, "rw_prep_bwd", (tt // TMV,), ins + [one] * 15,
                 [pl.BlockSpec((TMV, sh_w), lambda i: (i, 0))] + par,
                 [_sds((tt, sh_w)), _sds((2, RWW)), _sds((128, RWW)), _sds((128, RWW)), _sds((2, RWW)), _sds((128, RWW)), _sds((128, RWW)),
                  _sds((1, RWW)), _sds((1, RWW))],
                 sem=("arbitrary",))(sh, sh, sh, w0, w2p0, w2p1, a0, a2p0, a2p1, kkp, kap_p, *scan_grads, dkd_p, dr_p, dv_p)


def _rw_post_specs():
    dm = _dims()
    nv = dm["NCTV"]
    rw = RWW // 128
    lat3 = lambda d: pl.BlockSpec((None, TMV, RWW), lambda i: (d, nv + i, 0))
    lat2 = pl.BlockSpec((TMV, RWW), lambda i: (nv + i, 0))
    ins = [lat2, lat2, lat3(0), lat3(1),
           pl.BlockSpec((TMV, RWW), lambda i: (nv + i, 0)), pl.BlockSpec((TMV, RWW), lambda i: (nv + i, 2)),
           pl.BlockSpec((TMV, RWW), lambda i: (i, 0)), _full((1, RWW)), _full((1, RWW)), _full((1, RWW))]
    return dm, nv, ins


def _rw_post_fwd(y0_, y1_, kd, sh, p_z, rk, gng, gnb):
    dm, nv, ins = _rw_post_specs()

    def body(y0, y1, k0, k1, r, v, z, rk_ref, g_ref, b_ref, o_ref):
        o_ref[...] = _rw_post_math(y0[...], y1[...], k0[...], k1[...], r[...], v[...], z[...], rk_ref[...], g_ref[...], b_ref[...]).astype(BF16)

    return _call(body, "rw_post_fwd", (T // TMV,), ins, pl.BlockSpec((TMV, RWW), lambda i: (i, 0)), _sds((T, RWW), BF16),
                 sem=("parallel",))(y0_, y1_, kd, kd, sh, sh, p_z, rk, gng, gnb)


def _rw_post_bwd(y0_, y1_, kd, sh, p_z, rk, gng, gnb, dout):
    dm, nv, _ = _rw_post_specs()
    tt = dm["TT"]
    cl = lambda i: jnp.maximum(i - nv, 0)
    all3 = lambda d: pl.BlockSpec((None, TMV, RWW), lambda i: (d, i, 0))
    all2 = pl.BlockSpec((TMV, RWW), lambda i: (i, 0))
    ins = [all2, all2, all3(0), all3(1),
           pl.BlockSpec((TMV, RWW), lambda i: (i, 0)), pl.BlockSpec((TMV, RWW), lambda i: (i, 2)),
           pl.BlockSpec((TMV, RWW), lambda i: (cl(i), 0)), _full((1, RWW)), _full((1, RWW)), _full((1, RWW)),
           pl.BlockSpec((TMV, RWW), lambda i: (cl(i), 0))]
    tok = pl.BlockSpec((TMV, RWW), lambda i: (i, 0))

    def body(y0, y1, k0, k1, r, v, z, rk_ref, g_ref, b_ref, do_ref, dy_ref, dkd_ref, dr_ref, dv_ref, dz_ref, drk_ref, dg_ref, db_ref):
        i = pl.program_id(0)
        lat = jnp.where(i >= nv, 1.0, 0.0)
        _, vjp = jax.vjp(_rw_post_math, y0[...], y1[...], k0[...], k1[...], r[...], v[...], z[...], rk_ref[...], g_ref[...], b_ref[...])
        dy0, _, dk0, _, dr, dv, dz, drk, dg, db = vjp(do_ref[...] * lat)
        dy_ref[...] = dy0
        dkd_ref[...] = dk0
        dr_ref[...] = dr
        dv_ref[...] = dv
        dz_ref[...] = dz.astype(BF16)
        _acc_out((drk_ref, dg_ref, db_ref), i == 0, (drk, dg, db))

    return _call(body, "rw_post_bwd", (tt // TMV,), ins,
                 [tok, tok, tok, tok, pl.BlockSpec((TMV, RWW), lambda i: (cl(i), 0)), _full((1, RWW)), _full((1, RWW)), _full((1, RWW))],
                 [_sds((tt, RWW))] * 4 + [_sds((T, RWW), BF16), _sds((1, RWW)), _sds((1, RWW)), _sds((1, RWW))],
                 sem=("arbitrary",))(y0_, y1_, kd, kd, sh, sh, p_z, rk, gng, gnb, dout)


def _hg_post_fwd(o0_, o1_, p_hg, hg_g):
    dm = _dims()
    nv = dm["NCTV"]
    lat3 = lambda d: pl.BlockSpec((TMV, HGW), lambda i: (nv + i, 0))

    def body(o0, o1, z, g, out):
        out[...] = _hg_post_math(o0[...], o1[...], z[...], g[...]).astype(BF16)

    return _call(body, "hg_post_fwd", (T // TMV,), [lat3(0), lat3(1), pl.BlockSpec((TMV, HGW), lambda i: (nv + i, 4)), _full((1, HGW))],
                 pl.BlockSpec((TMV, HGW), lambda i: (i, 0)), _sds((T, HGW), BF16), sem=("parallel",))(o0_, o1_, p_hg, hg_g)


def _hg_post_bwd(o0_, o1_, p_hg, hg_g, dout):
    dm = _dims()
    nv, tt = dm["NCTV"], dm["TT"]
    all3 = lambda d: pl.BlockSpec((TMV, HGW), lambda i: (i, 0))
    tok = pl.BlockSpec((TMV, HGW), lambda i: (i, 0))

    def body(o0, o1, z, g, do_in, do_ref, dz_ref, dg_ref):
        i = pl.program_id(0)
        lat = jnp.where(i >= nv, 1.0, 0.0)
        _, vjp = jax.vjp(_hg_post_math, o0[...], o1[...], z[...], g[...])
        d0, _, dz, dg = vjp(do_in[...] * lat)
        do_ref[...] = d0
        dz_ref[...] = dz
        _acc_out((dg_ref,), i == 0, (dg,))

    return _call(body, "hg_post_bwd", (tt // TMV,),
                 [all3(0), all3(1), pl.BlockSpec((TMV, HGW), lambda i: (i, 4)), _full((1, HGW)),
                  pl.BlockSpec((TMV, HGW), lambda i: (jnp.maximum(i - nv, 0), 0))],
                 [tok, tok, _full((1, HGW))], [_sds((tt, HGW)), _sds((tt, HGW)), _sds((1, HGW))], sem=("arbitrary",))(o0_, o1_, p_hg, hg_g, dout)


def _hg_dproj(dq0, dq1, di0, di1, df0, df1, dz):
    dm = _dims()
    tt = dm["TT"]
    tok = pl.BlockSpec((TMV, HGW), lambda i: (i, 0))

    def body(q0, q1, i0, i1, f0, f1, z, o_ref):
        o_ref[:, 0:HGW] = (q0[...] + q1[...]).astype(BF16)
        o_ref[:, HGW:2 * HGW] = (i0[...] + i1[...]).astype(BF16)
        o_ref[:, 2 * HGW:3 * HGW] = f0[...].astype(BF16)
        o_ref[:, 3 * HGW:4 * HGW] = f1[...].astype(BF16)
        o_ref[:, 4 * HGW:5 * HGW] = z[...].astype(BF16)

    return _call(body, "hg_dproj", (tt // TMV,), [tok] * 7,
                 pl.BlockSpec((TMV, 5 * HGW), lambda i: (i, 0)), _sds((tt, 5 * HGW), BF16), sem=("parallel",))(
        dq0, dq1, di0, di1, df0, df1, dz)


def _merge1_fwd(out_hg, out_rw, whg_t, wrw_t, p_g):
    tok = pl.BlockSpec((TM, D), lambda i: (i, 0))

    def body(h_ref, r_ref, wh_ref, wr_ref, g1_ref, g2_ref, a_ref, b_ref, m_ref):
        nt = (((1,), (1,)), ((), ()))
        a = lax.dot_general(h_ref[...], wh_ref[...], nt, preferred_element_type=F32)
        b = lax.dot_general(r_ref[...], wr_ref[...], nt, preferred_element_type=F32)
        a_ref[...] = a
        b_ref[...] = b
        m_ref[...] = _gate_math(g1_ref[...], g2_ref[...], a, b).astype(BF16)

    return _call(body, "merge1_fwd", (T // TM,),
                 [pl.BlockSpec((TM, HGW), lambda i: (i, 0)), pl.BlockSpec((TM, RWW), lambda i: (i, 0)), _full((D, HGW)), _full((D, RWW)),
                  pl.BlockSpec((TM, D), lambda i: (i, 0)), pl.BlockSpec((TM, D), lambda i: (i, 1))],
                 [tok, tok, tok], [_sds((T, D)), _sds((T, D)), _sds((T, D), BF16)], sem=("parallel",))(out_hg, out_rw, whg_t, wrw_t, p_g, p_g)


def _head_fwd_bwd(x, merged, w_out, mods, final_g, tgt):
    tok = pl.BlockSpec((TM, D), lambda i: (i, 0))

    def body(x_ref, m_ref, w_ref, mods_ref, fg_ref, t_ref, dxo_ref, dmo_ref, loss_ref, dgate_ref, dfg_ref):
        mo = jnp.dot(m_ref[...], w_ref[...], preferred_element_type=F32)
        gate = mods_ref[0:1, 2 * D:3 * D]
        loss, vjp = jax.vjp(lambda x_, mo_, g_, fg_: _head_math(x_, mo_, g_, fg_, t_ref[...]), x_ref[...], mo, gate, fg_ref[...])
        dx, dmo, dgate, dfg = vjp(jnp.ones((1, 1), F32))
        dxo_ref[...] = dx
        dmo_ref[...] = dmo.astype(BF16)
        _acc_out((loss_ref, dgate_ref, dfg_ref), pl.program_id(0) == 0, (jnp.broadcast_to(loss, (1, 128)), dgate, dfg))

    return _call(body, "head_fwd_bwd", (T // TM,), [tok, tok, _full((D, D)), _full((8, 3 * D)), _full((1, D)), tok],
                 [tok, tok, _full((1, 128)), _full((1, D)), _full((1, D))],
                 [_sds((T, D)), _sds((T, D), BF16), _sds((1, 128)), _sds((1, D)), _sds((1, D))], sem=("arbitrary",))(
        x, merged, w_out, mods, final_g, tgt)


def _merge_bwd(dmo, w_out, p_g, a, b):
    tok = pl.BlockSpec((TM, D), lambda i: (i, 0))

    def body(dmo_ref, w_ref, g1_ref, g2_ref, a_ref, b_ref, da_ref, db_ref, dg_ref):
        dm_ = lax.dot_general(dmo_ref[...], w_ref[...], (((1,), (1,)), ((), ())), preferred_element_type=F32)
        _, vjp = jax.vjp(_gate_math, g1_ref[...], g2_ref[...], a_ref[...], b_ref[...])
        dg1, dg2, da, db = vjp(dm_)
        da_ref[...] = da.astype(BF16)
        db_ref[...] = db.astype(BF16)
        dg_ref[:, 0:D] = dg1.astype(BF16)
        dg_ref[:, D:2 * D] = dg2.astype(BF16)

    return _call(body, "merge_bwd", (T // TM,),
                 [tok, _full((D, D)), pl.BlockSpec((TM, D), lambda i: (i, 0)), pl.BlockSpec((TM, D), lambda i: (i, 1)), tok, tok],
                 [tok, tok, pl.BlockSpec((TM, 2 * D), lambda i: (i, 0))], [_sds((T, D), BF16), _sds((T, D), BF16), _sds((T, 2 * D), BF16)],
                 sem=("parallel",))(dmo, w_out, p_g, p_g, a, b)


def _h_bwd(xcat, mods, norm_g, dh_hg, dh_rw, dh_z, dh_g, dxo, dgate):
    dm = _dims()
    nct, tt = dm["NCT"], dm["TT"]
    tok = pl.BlockSpec((TM, D), lambda i: (i, 0))
    lat = pl.BlockSpec((TM, D), lambda i: (jnp.maximum(i - nct, 0), 0))

    def body(x_ref, m_ref, g_ref, d1, d2, d3, d4, dxo_ref, dgate_ref, gx_ref, dng_ref, dmods_ref):
        i = pl.program_id(0)
        is_ctx = i < nct
        latf = jnp.where(is_ctx, 0.0, 1.0)
        dh = d1[...] + d2[...] + latf * (d3[...] + d4[...])
        _, vjp = jax.vjp(_h_math, x_ref[...], g_ref[...], _sel_mod(m_ref, is_ctx, D), _sel_mod(m_ref, is_ctx, 0))
        dx, dng, dscale, dshift = vjp(dh)
        gx_ref[...] = dx + dxo_ref[...]

        @pl.when(i == 0)
        def _():
            dng_ref[...] = jnp.zeros_like(dng_ref)
            dmods_ref[...] = jnp.zeros_like(dmods_ref)
            dmods_ref[0:1, 2 * D:3 * D] = dgate_ref[...]

        dng_ref[...] += dng
        row = lax.broadcasted_iota(jnp.int32, (8, 1), 0)
        sel = jnp.where(row == jnp.where(is_ctx, 1, 0), 1.0, 0.0)
        dmods_ref[:, 0:D] += sel * dshift
        dmods_ref[:, D:2 * D] += sel * dscale

    return _call(body, "h_bwd", (tt // TM,), [tok, _full((8, 3 * D)), _full((1, D)), tok, tok, lat, lat, lat, _full((1, D))],
                 [lat, _full((1, D)), _full((8, 3 * D))], [_sds((T, D)), _sds((1, D)), _sds((8, 3 * D))], sem=("arbitrary",))(
        xcat, mods, norm_g, dh_hg, dh_rw, dh_z, dh_g, dxo, dgate)


def _lb_math(l0, l1):
    return jax.nn.sigmoid(l0 - l1)


def _lb_fwd(l0, l1):
    def body(a_ref, b_ref, o_ref):
        o_ref[...] = _lb_math(a_ref[...], b_ref[...])

    return _call(body, "lb_fwd", (1,), [_full((2, HGW))] * 2, _full((2, HGW)), _sds((2, HGW)))(l0, l1)


def _lb_bwd(l0, l1, dlb):
    def body(a_ref, b_ref, d_ref, da_ref, db_ref):
        _, vjp = jax.vjp(_lb_math, a_ref[...], b_ref[...])
        da_ref[...], db_ref[...] = vjp(d_ref[...])

    return _call(body, "lb_bwd", (1,), [_full((2, HGW))] * 3, [_full((2, HGW))] * 2, [_sds((2, HGW))] * 2)(l0, l1, dlb)


def _local_step(x, ctx, tgt, mods, w, start_exchange):
    dm = _dims()
    tt, sh_w, hgc = dm["TT"], dm["SH"], dm["HGC"]
    nb = lambda cols: cols // WB
    xcat = jnp.concatenate([ctx, x], axis=0)
    h = _h_fwd(xcat, mods, w["norm_g"])
    win_t = w["win_t"]
    p_hg = _proj_nt(h, win_t, 0, nb(hgc), 0, tt, "proj_hg")
    p_rw = _proj_nt(h, win_t, nb(hgc), nb(sh_w), 0, tt, "proj_rw")
    p_z = _proj_nt(h, win_t, nb(hgc + sh_w), nb(RWW), TC, T, "proj_z")
    p_g = _proj_nt(h, win_t, nb(hgc + sh_w + RWW), nb(2 * D), TC, T, "proj_g")

    lb3 = _lb_fwd(w["lb0"], w["lb1"]).reshape(2, 1, HGW)
    o0, o1, hg_ck = _hg_scan_fwd(p_hg, lb3)
    out_hg = _hg_post_fwd(o0, o1, p_hg, w["hg_g"])

    sh = _shift_fwd(p_rw, w["mu"])
    zpad = jnp.zeros((LORA, RWW), F32)
    prep_w = (w["w0"], jnp.concatenate([w["w2"][0], zpad], 0), jnp.concatenate([zpad, w["w2"][1]], 0),
              w["a0"], jnp.concatenate([w["a2"][0], zpad], 0), jnp.concatenate([zpad, w["a2"][1]], 0), w["kk"], w["ka"])
    kap, lw, b, kd = _rw_prep_fwd(sh, *prep_w)
    y0, y1, rw_ck, rw_tm = _rw_scan_fwd(sh, kap, lw, b, kd)
    post_w = (w["rk"], w["gng"], w["gnb"])
    out_rw = _rw_post_fwd(y0, y1, kd, sh, p_z, *post_w)

    a, bb, merged = _merge1_fwd(out_hg, out_rw, w["whg_t"], w["wrw_t"], p_g)
    dxo, dmo, loss, dgate, d_fg = _head_fwd_bwd(x, merged, w["wout"], mods, w["final_g"], tgt)
    da, db, dp_g = _merge_bwd(dmo, w["wout"], p_g, a, bb)
    g_wout = _dw_tn(merged, dmo, 0, D, 0, "dw_out")
    g_whg = _dw_tn(da, out_hg, 0, D, 0, "dw_hg")
    g_wrw = _dw_tn(db, out_rw, 0, D, 0, "dw_rw")
    d_out_hg = _mm_nn(da, w["whg_t"], 0, "dx_hg")
    d_out_rw = _mm_nn(db, w["wrw_t"], 0, "dx_rw")

    do, dz_hg, d_hg_g = _hg_post_bwd(o0, o1, p_hg, w["hg_g"], d_out_hg)
    *hg_grads, dlb3 = _hg_scan_bwd(p_hg, lb3, hg_ck, do)
    dp_hg = _hg_dproj(*hg_grads, dz_hg)
    d_lb0, d_lb1 = _lb_bwd(w["lb0"], w["lb1"], dlb3.reshape(2, HGW))

    dy, dkd_p, dr_p, dv_p, dp_z, d_rk, d_gng, d_gnb = _rw_post_bwd(y0, y1, kd, sh, p_z, *post_w, d_out_rw)
    rw_grads = _rw_scan_bwd(sh, kap, lw, b, kd, rw_ck, rw_tm, dy)
    dsh, d_w0, d_w2p0, d_w2p1, d_a0, d_a2p0, d_a2p1, d_kk, d_ka = _rw_prep_bwd(sh, *prep_w, rw_grads, dkd_p, dr_p, dv_p)
    dp_rw, d_mu = _shift_bwd(p_rw, dsh, w["mu"])

    g_win = _dw_tn(dp_hg, h, 0, dm["NCOLS"], 0, "dw_in_hg")
    g_win = _dw_tn(dp_rw, h, 0, dm["NCOLS"], nb(hgc), "dw_in_rw", prev=g_win)
    g_win = _dw_tn(dp_z, h, TC, dm["NCOLS"], nb(hgc + sh_w), "dw_in_z", prev=g_win)
    g_win = _dw_tn(dp_g, h, TC, dm["NCOLS"], nb(hgc + sh_w + RWW), "dw_in_g", prev=g_win)
    big = dict(win_t=g_win, wout=g_wout, whg_t=g_whg, wrw_t=g_wrw)
    token, in_flight = start_exchange(big)
    dh_hg = _mm_nn(dp_hg, win_t, 0, "dh_hg", after=token)
    dh_rw = _mm_nn(dp_rw, win_t, nb(hgc), "dh_rw")
    dh_z = _mm_nn(dp_z, win_t, nb(hgc + sh_w), "dh_z")
    dh_g = _mm_nn(dp_g, win_t, nb(hgc + sh_w + RWW), "dh_g")
    gx, d_ng, dmods = _h_bwd(xcat, mods, w["norm_g"], dh_hg, dh_rw, dh_z, dh_g, dxo, dgate)

    small = dict(norm_g=d_ng, lb0=d_lb0, lb1=d_lb1, hg_g=d_hg_g, mu=d_mu, w0=d_w0,
                 w2=jnp.stack([d_w2p0[:LORA], d_w2p1[LORA:]]), a0=d_a0, a2=jnp.stack([d_a2p0[:LORA], d_a2p1[LORA:]]),
                 kk=d_kk, ka=d_ka, rk=d_rk, gng=d_gng, gnb=d_gnb, final_g=d_fg)
    return loss, gx, dmods, in_flight, small


MESH = pl.DeviceIdType.MESH


def _comm_call(body, name, bufs, out_shapes, nsem, nloc):
    hbm = pl.BlockSpec(memory_space=pl.ANY)
    return pl.pallas_call(
        body, name=name, in_specs=[hbm] * len(bufs), out_specs=[hbm] * len(out_shapes), out_shape=out_shapes,
        scratch_shapes=[pltpu.SemaphoreType.DMA((nsem,)), pltpu.SemaphoreType.DMA((nsem,)), pltpu.SemaphoreType.DMA((nloc,))],
    )(*bufs)


def _gather2(bufs, name):
    nbuf = len(bufs)

    def body(*refs):
        ins, outs = refs[:nbuf], refs[nbuf:2 * nbuf]
        send_sems, recv_sems, loc_sems = refs[2 * nbuf:]
        x, y, c = lax.axis_index("x"), lax.axis_index("y"), lax.axis_index("c")
        me, sib = (x, y, c), (x, y, 1 - c)
        chips = [(1 - x, y), (x, 1 - y), (1 - x, 1 - y)]

        def copy(bi, k, block, to, src=None):
            rows = outs[bi].at[4 * block[0] + 2 * block[1] + block[2]]
            return pltpu.make_async_remote_copy(src_ref=rows if src is None else src, dst_ref=rows, send_sem=send_sems.at[7 * bi + k],
                                                recv_sem=recv_sems.at[7 * bi + k], device_id=to, device_id_type=MESH)

        own = [pltpu.make_async_copy(ins[bi], outs[bi].at[4 * x + 2 * y + c], loc_sems.at[bi]) for bi in range(nbuf)]
        for cp in own:
            cp.start()
        first = []
        for bi in range(nbuf):
            first.append(copy(bi, 0, me, sib, src=ins[bi]))
            first += [copy(bi, 1 + j, me, (*chip, c), src=ins[bi]) for j, chip in enumerate(chips)]
        for cp in first:
            cp.start()
        passed = []
        for j, chip in enumerate(chips):
            for bi in range(nbuf):
                copy(bi, 1 + j, (*chip, c), me).wait_recv()
                passed.append(copy(bi, 4 + j, (*chip, c), sib))
                passed[-1].start()
        for bi in range(nbuf):
            copy(bi, 0, sib, me).wait_recv()
            for j, chip in enumerate(chips):
                copy(bi, 4 + j, (*chip, 1 - c), me).wait_recv()
        for cp in first + passed:
            cp.wait_send()
        for cp in own:
            cp.wait()

    return _comm_call(body, name, bufs, [_sds((NDEV,) + b.shape, b.dtype) for b in bufs], 7 * nbuf, nbuf)


def _pair_exchange(bufs, name):
    nbuf = len(bufs)

    def body(*refs):
        ins, got = refs[:nbuf], refs[nbuf:2 * nbuf]
        send_sems, recv_sems, _ = refs[2 * nbuf:]
        x, y, c = lax.axis_index("x"), lax.axis_index("y"), lax.axis_index("c")
        copies = []
        for bi in range(nbuf):
            for q in range(4):
                copies.append(pltpu.make_async_remote_copy(
                    src_ref=ins[bi].at[2 * q + 1 - c], dst_ref=got[bi].at[q], send_sem=send_sems.at[4 * bi + q],
                    recv_sem=recv_sems.at[4 * bi + q], device_id=(x, y, 1 - c), device_id_type=MESH))
                copies[-1].start()
        for cp in copies:
            cp.wait_send()
            cp.wait_recv()

    return _comm_call(body, name, bufs, [_sds((4,) + b.shape[1:], b.dtype) for b in bufs], 4 * nbuf, 1)


def _chip_copies(srcs, lands, send_sems, recv_sems):
    x, y, c = lax.axis_index("x"), lax.axis_index("y"), lax.axis_index("c")
    myq = 2 * x + y
    pairs = []
    for bi in range(len(srcs)):
        for j, (qx, qy) in enumerate([(1 - x, y), (x, 1 - y), (1 - x, 1 - y)]):
            q = 2 * qx + qy
            mk = lambda src, dst: pltpu.make_async_remote_copy(
                src_ref=src, dst_ref=dst, send_sem=send_sems.at[3 * bi + j], recv_sem=recv_sems.at[3 * bi + j],
                device_id=(qx, qy, c), device_id_type=MESH)
            pairs.append((mk(srcs[bi].at[q], lands[bi].at[myq]), mk(srcs[bi].at[myq], lands[bi].at[q])))
    return pairs


def _chip_exchange_start(bufs, name):
    nbuf = len(bufs)
    hbm, sem = pl.BlockSpec(memory_space=pltpu.HBM), pl.BlockSpec(memory_space=pltpu.SEMAPHORE)

    def body(*refs):
        srcs, lands = refs[:nbuf], refs[nbuf:2 * nbuf]
        send_sems, recv_sems = refs[2 * nbuf], refs[2 * nbuf + 1]
        token = refs[-1]
        for send, _ in _chip_copies(srcs, lands, send_sems, recv_sems):
            send.start()
        token[...] = jnp.zeros_like(token)

    held = [pltpu.HBM(b.shape, b.dtype) for b in bufs]
    outs = pl.pallas_call(
        body, name=name, in_specs=[hbm] * (2 * nbuf), out_specs=(sem, sem, *[hbm] * (2 * nbuf), pl.BlockSpec(memory_space=pltpu.VMEM)),
        out_shape=(pltpu.SemaphoreType.DMA((3 * nbuf,)), pltpu.SemaphoreType.DMA((3 * nbuf,)), *held, *held, _sds((8, 128))),
        input_output_aliases={i: 2 + i for i in range(2 * nbuf)},
        compiler_params=pltpu.CompilerParams(has_side_effects=pltpu.SideEffectType.DATAFLOW_SIDE_EFFECTING),
    )(*[pltpu.with_memory_space_constraint(b, pltpu.HBM) for b in bufs],
      *[pltpu.with_memory_space_constraint(lax.empty(b.shape, b.dtype), pltpu.HBM) for b in bufs])
    return outs[0], outs[1], list(outs[2:2 + nbuf]), list(outs[2 + nbuf:2 + 2 * nbuf]), outs[-1]


def _chip_exchange_wait(send_sems, recv_sems, srcs, lands, after, name):
    nbuf = len(srcs)
    hbm, sem = pl.BlockSpec(memory_space=pltpu.HBM), pl.BlockSpec(memory_space=pltpu.SEMAPHORE)

    def body(*refs):
        src_refs, land_refs = refs[:nbuf], refs[nbuf:2 * nbuf]
        for send, recv in _chip_copies(src_refs, land_refs, refs[2 * nbuf], refs[2 * nbuf + 1]):
            send.wait_send()
            recv.wait_recv()

    held = [pltpu.HBM(b.shape, b.dtype) for b in srcs]
    outs = pl.pallas_call(
        body, name=name, in_specs=[hbm] * (2 * nbuf) + [sem, sem, pl.BlockSpec(memory_space=pl.ANY)], out_specs=[hbm] * (2 * nbuf),
        out_shape=(*held, *held), input_output_aliases={i: i for i in range(2 * nbuf)},
        compiler_params=pltpu.CompilerParams(has_side_effects=pltpu.SideEffectType.DATAFLOW_SIDE_EFFECTING),
    )(*srcs, *lands, send_sems, recv_sems, after)
    return list(outs[:nbuf]), list(outs[nbuf:])


def _prefetch_call(body, name, scalars, grid, in_specs, out_specs, out_shape, args):
    return pl.pallas_call(
        body, name=name, out_shape=out_shape,
        grid_spec=pltpu.PrefetchScalarGridSpec(num_scalar_prefetch=1, grid=grid, in_specs=in_specs, out_specs=out_specs),
        compiler_params=pltpu.CompilerParams(dimension_semantics=("parallel",) * len(grid), vmem_limit_bytes=V7X_VMEM_LIMIT))(scalars, *args)


def _pair_add(g, got, core, name):
    _, rows, cols = got.shape
    tr = _row_tile(rows, 16, 256)

    def body(c_ref, a_ref, b_ref, o_ref):
        o_ref[...] = (a_ref[...].astype(F32) + b_ref[...].astype(F32)).astype(o_ref.dtype)

    blk = pl.BlockSpec((4, tr, cols), lambda i, c_ref: (0, i, 0))
    mine = pl.BlockSpec((4, None, tr, cols), lambda i, c_ref: (0, c_ref[0], i, 0))
    return _prefetch_call(body, name, core, (rows // tr,), [mine, blk], blk, _sds(got.shape, got.dtype),
                          (g.reshape(4, 2, rows, cols), got))


def _reduce_chips(chip_sums, recv, slots, name):
    _, rows, cols = recv.shape
    tr = _row_tile(rows, 16, 128)

    def body(s_ref, own_ref, r1_ref, r2_ref, r3_ref, o_ref):
        o_ref[...] = ((own_ref[...].astype(F32) + r1_ref[...].astype(F32)) + r2_ref[...].astype(F32)) + r3_ref[...].astype(F32)

    pick = lambda k: pl.BlockSpec((None, tr, cols), lambda i, s_ref: (s_ref[k], i, 0))
    return _prefetch_call(body, name, slots, (rows // tr,), [pick(0), pick(1), pick(2), pick(3)],
                          pl.BlockSpec((tr, cols), lambda i, s_ref: (i, 0)), _sds((rows, cols)), (chip_sums, recv, recv, recv))


def _row_tile(rows, mult, cap):
    best = mult
    for t in range(mult, cap + 1, mult):
        if rows % t == 0:
            best = t
    assert rows % best == 0, (rows, mult)
    return best


def _reduce_sources(r, name):
    nsrc, rows, cols = r.shape
    tr = _row_tile(rows, 16 if r.dtype.itemsize == 2 else 8, 128)

    def body(r_ref, o_ref):
        acc = r_ref[0].astype(F32)
        for j in range(1, nsrc):
            acc = acc + r_ref[j].astype(F32)
        o_ref[...] = acc

    return _call(body, name, (rows // tr,), [pl.BlockSpec((nsrc, tr, cols), lambda i: (0, i, 0))],
                 pl.BlockSpec((tr, cols), lambda i: (i, 0)), _sds((rows, cols)), sem=("parallel",))(r)


def _adamw(w, g, m, v, name):
    rows, cols = w.shape
    tr = _row_tile(rows, 8, 128)
    c1 = 1.0 - ADAM_B1 ** ADAM_STEP
    c2 = 1.0 - ADAM_B2 ** ADAM_STEP

    def body(w_ref, g_ref, m_ref, v_ref, d_ref, mo_ref, vo_ref):
        g_ = g_ref[...]
        m_ = ADAM_B1 * m_ref[...] + (1.0 - ADAM_B1) * g_
        v_ = ADAM_B2 * v_ref[...] + (1.0 - ADAM_B2) * (g_ * g_)
        d_ref[...] = -ADAM_LR * ((m_ / c1) / (jnp.sqrt(v_ / c2) + ADAM_EPS) + ADAM_WD * w_ref[...])
        mo_ref[...] = m_
        vo_ref[...] = v_

    blk = pl.BlockSpec((tr, cols), lambda i: (i, 0))
    return _call(body, name, (rows // tr,), [blk] * 4, [blk] * 3, [_sds((rows, cols))] * 3, sem=("parallel",))(w, g, m, v)


SLAB_PART = 1024


def _pack(arrs):
    parts = []
    for a in arrs:
        flat = a.reshape(-1)
        pad = (-flat.shape[0]) % SLAB_PART
        if pad:
            flat = jnp.concatenate([flat, jnp.zeros((pad,), flat.dtype)])
        parts.append(flat.reshape(-1, 128))
    return jnp.concatenate(parts, axis=0)


def _unpack(slab, shapes):
    outs, row = [], 0
    for s in shapes:
        n = 1
        for d in s:
            n *= d
        rows = (n + SLAB_PART - 1) // SLAB_PART * (SLAB_PART // 128)
        outs.append(slab[row:row + rows].reshape(-1)[:n].reshape(s))
        row += rows
    return outs


def _unshard_last(g, lead):
    nl = len(lead)
    return jnp.transpose(g, tuple(range(1, nl + 1)) + (0, nl + 1)).reshape(tuple(lead) + (-1,))


def kernel(x, c, ctx, c_ctx, ada_w, ada_b, norm_g, w_in, hg_lb, hg_norm_g, rw_mu, rw_w0, rw_w2, rw_a0, rw_a2, rw_kk, rw_ka, rw_rk, rw_gn_g, rw_gn_b, w_hg_out, w_rw_out, w_out, final_g, loss_target, m_c_ctx, m_ada_w, m_ada_b, m_norm_g, m_w_in, m_hg_lb, m_hg_norm_g, m_rw_mu, m_rw_w0, m_rw_w2, m_rw_a0, m_rw_a2, m_rw_kk, m_rw_ka, m_rw_rk, m_rw_gn_g, m_rw_gn_b, m_w_hg_out, m_w_rw_out, m_w_out, m_final_g, v_c_ctx, v_ada_w, v_ada_b, v_norm_g, v_w_in, v_hg_lb, v_hg_norm_g, v_rw_mu, v_rw_w0, v_rw_w2, v_rw_a0, v_rw_a2, v_rw_kk, v_rw_ka, v_rw_rk, v_rw_gn_g, v_rw_gn_b, v_w_hg_out, v_w_rw_out, v_w_out, v_final_g):
    dm = _dims()
    me = 4 * lax.axis_index("x") + 2 * lax.axis_index("y") + lax.axis_index("c")

    big_shards = [w_in[0].T.astype(BF16), w_out[0].astype(BF16), w_hg_out[0].T.astype(BF16), w_rw_out[0].T.astype(BF16)]
    sharded_small = [hg_lb, rw_mu[0], rw_w0[0], rw_w2[0], rw_a0[0], rw_a2[0]]
    c_rows = jnp.concatenate([c, jnp.zeros((7, D), F32)], axis=0)
    gathered = _gather2(big_shards + [_pack(sharded_small), c_rows], "gather_weights")
    win_t, wout, whg_t, wrw_t = [g.reshape(-1, g.shape[2]) for g in gathered[:4]]
    per_dev = jax.vmap(lambda s: _unpack(s, [a.shape for a in sharded_small]))(gathered[4])
    hg_lb_f, mu_f, w0_f, w2_f, a0_f, a2_f = [_unshard_last(p, p.shape[1:-1]) for p in per_dev]

    ncol = ada_w.shape[2]
    cc16 = jnp.concatenate([gathered[5][:, 0], c_ctx[None], jnp.zeros((7, D), F32)], axis=0)
    mod_all = _gather2([_ada_fwd(cc16, ada_w[0])], "gather_mods")[0]
    mod_x = lax.dynamic_index_in_dim(mod_all, me, axis=1, keepdims=False).reshape(1, -1) + ada_b
    mod_c = mod_all[:, NDEV].reshape(1, -1) + ada_b
    mods = jnp.concatenate([mod_x, mod_c, jnp.zeros((6, 3 * D), F32)], axis=0)

    w = dict(win_t=win_t, wout=wout, whg_t=whg_t, wrw_t=wrw_t, norm_g=norm_g,
             lb0=hg_lb_f[:, 0], lb1=hg_lb_f[:, 1], hg_g=hg_norm_g, mu=mu_f, w0=w0_f, w2=w2_f, a0=a0_f, a2=a2_f,
             kk=rw_kk, ka=rw_ka, rk=rw_rk, gng=rw_gn_g, gnb=rw_gn_b, final_g=final_g[None])
    big_names = ["win_t", "wout", "whg_t", "wrw_t"]

    def start_exchange(big):
        blocks = [big[n].reshape(NDEV, -1, big[n].shape[1]) for n in big_names]
        got = _pair_exchange(blocks, "pair_grads")
        core = lax.axis_index("c").astype(jnp.int32).reshape(1)
        chip_sums = [_pair_add(g, r, core, "pair_add_" + n) for g, r, n in zip(blocks, got, big_names)]
        send_sems, recv_sems, held, lands, token = _chip_exchange_start(chip_sums, "chip_grads_start")
        return token, (send_sems, recv_sems, held, lands)

    loss_dev, grad_x, dmods, in_flight, small = _local_step(x[0], ctx[0], loss_target[0], mods, w, start_exchange)
    loss = lax.psum(loss_dev[0, 0], AXES)

    dmods_all = _gather2([dmods], "gather_dmods")[0]
    my_cols = lambda a: lax.dynamic_slice_in_dim(a, me * ncol, ncol, axis=1)
    g_ada, dcc16, d_ada_b = _ada_bwd(cc16, ada_w[0], my_cols(dmods_all[:, 0]), my_cols(dmods_all[:, 1]), dmods)
    small = dict(small, c_ctx=dcc16[NDEV:NDEV + 1], ada_b=d_ada_b)

    small_names = ["c_ctx", "ada_b", "norm_g", "lb0", "lb1", "hg_g", "mu", "w0", "w2", "a0", "a2", "kk", "ka", "rk", "gng", "gnb", "final_g"]
    small_slab = _pack([small[n] for n in small_names])
    small_all = _gather2([small_slab], "gather_small_grads")[0]
    small_sum = _reduce_sources(small_all, "reduce_small")
    sg = dict(zip(small_names, _unpack(small_sum, [small[n].shape for n in small_names])))

    chip_sums, recv = _chip_exchange_wait(*in_flight, small_sum, "chip_grads_wait")
    my_chip = (2 * lax.axis_index("x") + lax.axis_index("y")).astype(jnp.int32)
    others = jnp.arange(3, dtype=jnp.int32)
    slots = jnp.concatenate([my_chip.reshape(1), others + (others >= my_chip).astype(jnp.int32)])
    g_win_t, g_wout, g_whg_t, g_wrw_t = [_reduce_chips(cs, r, slots, "reduce_" + n) for cs, r, n in zip(chip_sums, recv, big_names)]

    def my_shard(full):
        n = full.shape[-1] // NDEV
        return lax.dynamic_slice_in_dim(full, me * n, n, axis=full.ndim - 1)

    grads = dict(
        c_ctx=sg["c_ctx"][0], ada_w=g_ada[None], ada_b=sg["ada_b"], norm_g=sg["norm_g"], w_in=g_win_t.T[None],
        hg_lb=my_shard(jnp.stack([sg["lb0"], sg["lb1"]], axis=1)), hg_norm_g=sg["hg_g"], rw_mu=my_shard(sg["mu"])[None],
        rw_w0=my_shard(sg["w0"])[None], rw_w2=my_shard(sg["w2"])[None], rw_a0=my_shard(sg["a0"])[None], rw_a2=my_shard(sg["a2"])[None],
        rw_kk=sg["kk"], rw_ka=sg["ka"], rw_rk=sg["rk"], rw_gn_g=sg["gng"], rw_gn_b=sg["gnb"],
        w_hg_out=g_whg_t.T[None], w_rw_out=g_wrw_t.T[None], w_out=g_wout[None], final_g=sg["final_g"][0])
    weights = dict(c_ctx=c_ctx, ada_w=ada_w, ada_b=ada_b, norm_g=norm_g, w_in=w_in, hg_lb=hg_lb, hg_norm_g=hg_norm_g, rw_mu=rw_mu,
                   rw_w0=rw_w0, rw_w2=rw_w2, rw_a0=rw_a0, rw_a2=rw_a2, rw_kk=rw_kk, rw_ka=rw_ka, rw_rk=rw_rk, rw_gn_g=rw_gn_g,
                   rw_gn_b=rw_gn_b, w_hg_out=w_hg_out, w_rw_out=w_rw_out, w_out=w_out, final_g=final_g)
    m_in = dict(zip(weights, (m_c_ctx, m_ada_w, m_ada_b, m_norm_g, m_w_in, m_hg_lb, m_hg_norm_g, m_rw_mu, m_rw_w0, m_rw_w2, m_rw_a0,
                              m_rw_a2, m_rw_kk, m_rw_ka, m_rw_rk, m_rw_gn_g, m_rw_gn_b, m_w_hg_out, m_w_rw_out, m_w_out, m_final_g)))
    v_in = dict(zip(weights, (v_c_ctx, v_ada_w, v_ada_b, v_norm_g, v_w_in, v_hg_lb, v_hg_norm_g, v_rw_mu, v_rw_w0, v_rw_w2, v_rw_a0,
                              v_rw_a2, v_rw_kk, v_rw_ka, v_rw_rk, v_rw_gn_g, v_rw_gn_b, v_w_hg_out, v_w_rw_out, v_w_out, v_final_g)))

    big_w = ["ada_w", "w_in", "w_hg_out", "w_rw_out", "w_out"]
    delta, new_m, new_v = {}, {}, {}
    for n in big_w:
        shp = weights[n].shape
        two = lambda a: a.reshape(shp[-2], shp[-1])
        d_, m_, v_ = _adamw(two(weights[n]), two(grads[n]), two(m_in[n]), two(v_in[n]), "adamw_" + n)
        delta[n], new_m[n], new_v[n] = d_.reshape(shp), m_.reshape(shp), v_.reshape(shp)
    rest = [n for n in weights if n not in big_w]
    shapes = [weights[n].shape for n in rest]
    d_s, m_s, v_s = _adamw(_pack([weights[n] for n in rest]), _pack([grads[n] for n in rest]), _pack([m_in[n] for n in rest]),
                           _pack([v_in[n] for n in rest]), "adamw_small")
    for n, d_, m_, v_ in zip(rest, _unpack(d_s, shapes), _unpack(m_s, shapes), _unpack(v_s, shapes)):
        delta[n], new_m[n], new_v[n] = d_, m_, v_

    order = list(weights)
    return (loss, grad_x[None], *[grads[n] for n in order], *[delta[n] for n in order],
            *[new_m[n] for n in order], *[new_v[n] for n in order])
```

```python
import functools

import jax
import jax.numpy as jnp
from jax import lax
from jax.experimental import pallas as pl
from jax.experimental.pallas import tpu as pltpu

D = 2048
T = 2048
TC = 256
GW = 64
C = 64
HGW = 1024
RWW = 1024
LORA = 64
NDEV = 8
TM = 256
TMV = 128
NORM_EPS = 1e-6
RW_GN_EPS = 64e-5
ADAM_LR, ADAM_B1, ADAM_B2, ADAM_EPS, ADAM_WD, ADAM_STEP = 0.001, 0.9, 0.999, 1e-08, 0.01, 10
WB = 256
V7X_VMEM_LIMIT = 56 * 1024 * 1024

F32 = jnp.float32
BF16 = jnp.bfloat16
AXES = ("x", "y", "c")


def _dims():
    tt = T + TC
    sh = 3 * RWW + 4 * LORA
    hgc = 5 * HGW
    return dict(TT=tt, NCC=TC // C, NCH=tt // C, HG_H=HGW // 128, RW_P=RWW // 128, SH=sh, HGC=hgc,
                NCOLS=hgc + sh + RWW + 2 * D, NCT=TC // TM, NCTV=TC // TMV)


def _mm(a, b):
    return _dot3(a, b, "nn")


def _mm_nt(a, b):
    return _dot3(a, b, "nt")


def _mm_tn(a, b):
    return _dot3(a, b, "tn")


def _split(x):
    hi = x.astype(BF16)
    return hi, (x - hi.astype(F32)).astype(BF16)


_FORMS = {"nn": (((1,), (0,)), ((), ())), "nt": (((1,), (1,)), ((), ())), "tn": (((0,), (0,)), ((), ()))}


def _dot3_raw(a, b, form):
    ah, al = _split(a)
    bh, bl = _split(b)
    d = lambda x, y: lax.dot_general(x, y, _FORMS[form], preferred_element_type=F32)
    return d(ah, bh) + (d(ah, bl) + d(al, bh))


@functools.partial(jax.custom_vjp, nondiff_argnums=(2,))
def _dot3(a, b, form):
    return _dot3_raw(a, b, form)


def _dot3_fwd(a, b, form):
    return _dot3_raw(a, b, form), (a, b)


def _dot3_bwd(form, res, g):
    a, b = res
    if form == "nn":
        return _dot3(g, b, "nt"), _dot3(a, g, "tn")
    if form == "nt":
        return _dot3(g, b, "nn"), _dot3(g, a, "tn")
    return _dot3(b, g, "nt"), _dot3(a, g, "nn")


_dot3.defvjp(_dot3_fwd, _dot3_bwd)


def _scan_cumsum(inc, x):
    return _cumsum_vjp(inc.astype(BF16), x)


def _cumsum_raw(inc, x, form):
    h1 = x.astype(BF16)
    r1 = x - h1.astype(F32)
    h2 = r1.astype(BF16)
    h3 = (r1 - h2.astype(F32)).astype(BF16)
    d = lambda y: lax.dot_general(inc, y, _FORMS[form], preferred_element_type=F32)
    return d(h1) + (d(h2) + d(h3))


@jax.custom_vjp
def _cumsum_vjp(inc, x):
    return _cumsum_raw(inc, x, "nn")


_cumsum_vjp.defvjp(lambda inc, x: (_cumsum_raw(inc, x, "nn"), inc),
                   lambda inc, g: (jnp.zeros_like(inc), _cumsum_raw(inc, g, "tn")))


def _silu(x):
    return x * jax.nn.sigmoid(x)


def _softplus(x):
    return jnp.maximum(x, 0.0) + jnp.log(1.0 + jnp.exp(-jnp.abs(x)))


def _iota2(shape, dim):
    return lax.broadcasted_iota(jnp.int32, shape, dim)


def _pair_mask():
    return (_iota2((128, 128), 0) < 64) == (_iota2((128, 128), 1) < 64)


def _seg64_sum(x):
    e = _pair_mask().astype(F32)
    parts = [_mm(x[:, g * 128:(g + 1) * 128], e) for g in range(x.shape[1] // 128)]
    return parts[0] if len(parts) == 1 else jnp.concatenate(parts, axis=1)


def _seg128_mean(x):
    parts = [jnp.broadcast_to(jnp.mean(x[:, g * 128:(g + 1) * 128], axis=1, keepdims=True), (x.shape[0], 128))
             for g in range(x.shape[1] // 128)]
    return parts[0] if len(parts) == 1 else jnp.concatenate(parts, axis=1)


def _scan_masks(d):
    lag = (_iota2((C, C), 0) - _iota2((C, C), 1)) * (1 - 2 * d)
    return lag >= 0, lag > 0


def _tri_inverse(a):
    eye = (_iota2((C, C), 0) == _iota2((C, C), 1)).astype(F32)
    p = [-x for x in a]
    tm = [eye + x for x in p]
    n = 2
    while n < C:
        p = [_mm(x, x) for x in p]
        tm = [t + _mm(t, x) for t, x in zip(tm, p)]
        n *= 2
    return tm


@jax.custom_vjp
def _tri_solve(tm, a, rhs):
    return [_mm(t, r) for t, r in zip(tm, rhs)]


def _tri_solve_fwd(tm, a, rhs):
    u = [_mm(t, r) for t, r in zip(tm, rhs)]
    return u, (tm, u)


def _tri_solve_bwd(res, g):
    tm, u = res
    d_rhs = [_mm_tn(t, x) for t, x in zip(tm, g)]
    return [jnp.zeros_like(t) for t in tm], [-_mm_nt(d, x) for d, x in zip(d_rhs, u)], d_rhs


_tri_solve.defvjp(_tri_solve_fwd, _tri_solve_bwd)


def _rw_chunks(chains, tms=None):
    s0, r, lw, kap, b, v, kd, ds = zip(*chains)
    ids = range(len(chains))
    inc, strict = zip(*[_scan_masks(d) for d in ds])
    lane = _iota2((1, 128), 1)
    hm = ((lane < 64).astype(F32), (lane >= 64).astype(F32))
    eye = (_iota2((C, C), 0) == _iota2((C, C), 1)).astype(F32)
    lc = [_scan_cumsum(inc[i], lw[i]) for i in ids]
    ltot = [jnp.sum(lw[i], axis=0, keepdims=True) for i in ids]
    rt = [r[i] * jnp.exp(lc[i]) for i in ids]
    kt = [kap[i] * jnp.exp(lc[i] - lw[i]) for i in ids]
    einv = [jnp.exp(-lc[i]) for i in ids]
    kh = [kd[i] * einv[i] for i in ids]
    bh = [b[i] * einv[i] for i in ids]
    eend = [jnp.exp(ltot[i] - lc[i]) for i in ids]
    kbar = [kd[i] * eend[i] for i in ids]
    bbar = [b[i] * eend[i] for i in ids]
    lhs = [jnp.concatenate([kt[i] * hm[0], kt[i] * hm[1], rt[i] * hm[0], rt[i] * hm[1]], axis=0) for i in ids]
    gk = [_mm_nt(lhs[i], kh[i]) for i in ids]
    gb = [_mm_nt(lhs[i], bh[i]) for i in ids]
    ks = [_mm_nt(kt[i], s0[i]) for i in ids]
    ys = [_mm_nt(rt[i], s0[i]) for i in ids]
    heads = [(i, h) for i in ids for h in range(2)]
    akk = {c: jnp.where(strict[c[0]], gk[c[0]][c[1] * C:(c[1] + 1) * C], 0.0) for c in heads}
    akb = {c: jnp.where(strict[c[0]], gb[c[0]][c[1] * C:(c[1] + 1) * C], 0.0) for c in heads}
    ark = {c: jnp.where(inc[c[0]], gk[c[0]][(2 + c[1]) * C:(3 + c[1]) * C], 0.0) for c in heads}
    arb = {c: jnp.where(inc[c[0]], gb[c[0]][(2 + c[1]) * C:(3 + c[1]) * C], 0.0) for c in heads}
    vh = {c: v[c[0]] * hm[c[1]] for c in heads}
    av = {c: _mm(akk[c], vh[c]) for c in heads}
    rhs = {c: ks[c[0]] * hm[c[1]] + av[c] for c in heads}
    akb_list = [akb[c] for c in heads]
    tm = _tri_inverse(akb_list) if tms is None else [tms[c] for c in heads]
    uh = dict(zip(heads, _tri_solve(tm, akb_list, [rhs[c] for c in heads])))
    yv = {c: _mm(ark[c], vh[c]) for c in heads}
    yu = {c: _mm(arb[c], uh[c]) for c in heads}
    u = [uh[i, 0] + uh[i, 1] for i in ids]
    y = [ys[i] + (yv[i, 0] - yu[i, 0]) + (yv[i, 1] - yu[i, 1]) for i in ids]
    upd = [_mm_tn(jnp.concatenate([v[i], -u[i]], axis=0), jnp.concatenate([kbar[i], bbar[i]], axis=0)) for i in ids]
    s1 = [s0[i] * jnp.exp(ltot[i]) + jnp.where(_pair_mask(), upd[i], 0.0) for i in ids]
    if tms is None:
        return [(y[i], s1[i], tm[2 * i], tm[2 * i + 1]) for i in ids]
    return [(y[i], s1[i]) for i in ids]


HG_SUB = 16


def _hg_chunks(chains):
    s0, qr, iv, f, lb, ds = zip(*chains)
    ids = range(len(chains))
    inc = [_scan_masks(d)[0] for d in ds]
    q = [_silu(qr[i]) for i in ids]
    fg = [lb[i] + (1.0 - lb[i]) * jax.nn.sigmoid(f[i]) for i in ids]
    k = [1.0 - fg[i] for i in ids]
    g = [jnp.log(fg[i]) for i in ids]
    bc = [_scan_cumsum(inc[i], g[i]) for i in ids]
    btot = [jnp.sum(g[i], axis=0, keepdims=True) for i in ids]
    o_inter = [_mm_nt(q[i] * jnp.exp(bc[i]), s0[i]) for i in ids]
    rowi = _iota2((C, 1), 0)
    outs = [[] for _ in ids]
    for blk in range(C // HG_SUB):
        lo, hi = blk * HG_SUB, (blk + 1) * HG_SUB
        first = [lo if ds[i] == 0 else hi - 1 for i in ids]
        ref = [jnp.sum(jnp.where(rowi == first[i], bc[i] - g[i], 0.0), axis=0, keepdims=True) for i in ids]
        qi = [q[i][lo:hi] * jnp.exp(bc[i][lo:hi] - ref[i]) for i in ids]
        src = [(rowi < hi) if ds[i] == 0 else (rowi >= lo) for i in ids]
        ke = [jnp.where(src[i], k[i] * jnp.exp(jnp.where(src[i], ref[i] - bc[i], 0.0)), 0.0) for i in ids]
        a = [jnp.where(inc[i][lo:hi], _mm_nt(qi[i], ke[i]), 0.0) for i in ids]
        part = [_mm(a[i], iv[i]) for i in ids]
        for i in ids:
            outs[i].append(part[i])
    o = [o_inter[i] + jnp.concatenate(outs[i], axis=0) for i in ids]
    upd = [_mm_tn(iv[i], k[i] * jnp.exp(btot[i] - bc[i])) for i in ids]
    s1 = [s0[i] * jnp.exp(btot[i]) + upd[i] for i in ids]
    return [(o[i], s1[i]) for i in ids]


def _lockstep(fn, ds, nargs):
    def flat_fn(*flat):
        return tuple(fn([tuple(flat[i * nargs:(i + 1) * nargs]) + (d,) for i, d in enumerate(ds)]))

    return flat_fn


def _h_math(x, ng, scale, shift):
    return x * lax.rsqrt(jnp.mean(x * x, axis=-1, keepdims=True) + NORM_EPS) * ng * (1.0 + scale) + shift


def _hg_post_math(o0, o1, z, g):
    o = o0 + o1
    on = o * lax.rsqrt(_seg128_mean(o * o) + NORM_EPS)
    return on * g * _silu(z)


def _rw_prep_math(k, pw0, pw1, pa0, pa1, w0, a0, kkp, kap_p):
    kk = k * kkp
    kap = kk * lax.rsqrt(_seg64_sum(kk * kk) + 1e-12)
    outs = []
    for d, (pw, pa) in enumerate(((pw0, pa0), (pw1, pa1))):
        w_log = -_softplus(-(w0[d:d + 1] + pw)) - 0.5
        lw = -jnp.exp(w_log)
        a = jax.nn.sigmoid(a0[d:d + 1] + pa)
        kd = k * (1.0 + (a - 1.0) * kap_p)
        outs += [lw, kap * a, kd]
    return (kap, *outs)


def _rw_post_math(y0, y1, kd0, kd1, r, v, z, rk, gng, gnb):
    ys, ksum = y0 + y1, kd0 + kd1
    mean = _seg64_sum(ys) * (1.0 / 64.0)
    cen = ys - mean
    var = _seg64_sum(cen * cen) * (1.0 / 64.0)
    yn = cen * lax.rsqrt(var + RW_GN_EPS) * gng + gnb
    bonus = _seg64_sum(r * ksum * rk) * v
    return (yn + bonus) * _silu(z)


def _head_math(x, mo, gate, fg, tgt):
    xo = x + gate * mo
    y = xo * lax.rsqrt(jnp.mean(xo * xo, axis=-1, keepdims=True) + NORM_EPS) * fg
    err = y - tgt
    return 0.5 * jnp.sum(jnp.mean(err * err, axis=-1, keepdims=True), axis=0, keepdims=True)


def _gate_math(ghg, grw, a, b):
    return jax.nn.sigmoid(ghg) * a + jax.nn.sigmoid(grw) * b


def _call(body, name, grid, in_specs, out_specs, out_shape, scratch=(), sem=None, aliases=None):
    return pl.pallas_call(
        body, name=name, grid=grid, in_specs=in_specs, out_specs=out_specs, out_shape=out_shape,
        scratch_shapes=list(scratch), input_output_aliases=aliases or {},
        compiler_params=pltpu.CompilerParams(dimension_semantics=sem, vmem_limit_bytes=V7X_VMEM_LIMIT))


def _full(shape):
    nd = len(shape)
    return pl.BlockSpec(shape, lambda *_: (0,) * nd)


def _sds(shape, dtype=F32):
    return jax.ShapeDtypeStruct(shape, dtype)


def _proj_nt(h, wt, wt_blk0, nblk, row0, nrows, name):
    def body(h_ref, w_ref, o_ref):
        o_ref[...] = lax.dot_general(h_ref[row0:row0 + nrows, :], w_ref[...], (((1,), (1,)), ((), ())), preferred_element_type=F32)

    return _call(body, name, (nblk,), [_full(h.shape), pl.BlockSpec((WB, wt.shape[1]), lambda j: (wt_blk0 + j, 0))],
                 pl.BlockSpec((nrows, WB), lambda j: (0, j)), _sds((nrows, nblk * WB)), sem=("parallel",))(h, wt)


def _dw_tn(a, b, row0, out_rows, out_blk0, name, prev=None):
    m, n = a.shape
    k2 = b.shape[1]
    nblk = n // WB

    def body(*refs):
        a_ref, b_ref, o_ref = refs[0], refs[1], refs[-1]
        o_ref[...] = lax.dot_general(a_ref[...], b_ref[row0:row0 + m, :], (((0,), (0,)), ((), ())),
                                     preferred_element_type=F32).astype(BF16)

    in_specs = [pl.BlockSpec((m, WB), lambda j: (0, j)), _full(b.shape)]
    args = [a, b]
    aliases = None
    if prev is not None:
        in_specs.append(pl.BlockSpec(memory_space=pl.ANY))
        args.append(prev)
        aliases = {2: 0}
    return _call(body, name, (nblk,), in_specs, pl.BlockSpec((WB, k2), lambda j: (out_blk0 + j, 0)), _sds((out_rows, k2), BF16),
                 sem=("arbitrary",), aliases=aliases)(*args)


def _mm_nn(a, w, w_blk0, name, after=None):
    m, kc = a.shape
    n = w.shape[1]
    kb = max(k for k in (WB, 2 * WB, 4 * WB) if kc % k == 0 and (w_blk0 * WB) % k == 0)
    w_blk0 = w_blk0 * WB // kb
    tmb = 3 * TM if m % (3 * TM) == 0 else (2 * TM if m % (2 * TM) == 0 else TM)

    def body(a_ref, w_ref, *rest):
        o_ref = rest[-1]

        @pl.when(pl.program_id(1) == 0)
        def _():
            o_ref[...] = jnp.zeros_like(o_ref)

        o_ref[...] += jnp.dot(a_ref[...], w_ref[...], preferred_element_type=F32)

    extra = [] if after is None else [pl.BlockSpec(memory_space=pl.ANY)]
    return _call(body, name, (m // tmb, kc // kb),
                 [pl.BlockSpec((tmb, kb), lambda i, k: (i, k)), pl.BlockSpec((kb, n), lambda i, k: (w_blk0 + k, 0))] + extra,
                 pl.BlockSpec((tmb, n), lambda i, k: (i, 0)), _sds((m, n)), sem=("parallel", "arbitrary"))(
        a, w, *([] if after is None else [after]))


def _ada_fwd(cc16, ada_shard):
    ncol = ada_shard.shape[1]

    def body(c_ref, w_ref, o_ref):
        o_ref[...] = _mm(_silu(c_ref[...]), w_ref[...])

    return _call(body, "ada_fwd", (1,), [_full((16, D)), _full((D, ncol))], _full((16, ncol)), _sds((16, ncol)))(cc16, ada_shard)


def _ada_bwd(cc16, ada_shard, dm_x, dm_c, dmods):
    ncol = ada_shard.shape[1]

    def body(c_ref, w_ref, dx_ref, dc_ref, dm_ref, dw_ref, dcc_ref, db_ref):
        s, vjp = jax.vjp(_silu, c_ref[...])
        dc_tot = dc_ref[0:1, :]
        for j in range(1, NDEV):
            dc_tot = dc_tot + dc_ref[j:j + 1, :]
        row = _iota2((8, 1), 0)
        dm16 = jnp.concatenate([dx_ref[...], jnp.where(row == 0, dc_tot, 0.0)], axis=0)
        dw_ref[...] = _mm_tn(s, dm16)
        dcc_ref[...] = vjp(_mm_nt(dm16, w_ref[...]))[0]
        db_ref[...] = dm_ref[0:1, :] + dm_ref[1:2, :]

    return _call(body, "ada_bwd", (1,), [_full((16, D)), _full((D, ncol)), _full((8, ncol)), _full((8, ncol)), _full((8, 3 * D))],
                 [_full((D, ncol)), _full((16, D)), _full((1, 3 * D))], [_sds((D, ncol)), _sds((16, D)), _sds((1, 3 * D))])(
        cc16, ada_shard, dm_x, dm_c, dmods)


def _sel_mod(mods_ref, is_ctx, lo):
    return jnp.where(is_ctx, mods_ref[1:2, lo:lo + D], mods_ref[0:1, lo:lo + D])


def _h_fwd(xcat, mods, norm_g):
    dm = _dims()

    def body(x_ref, m_ref, g_ref, o_ref):
        is_ctx = pl.program_id(0) < dm["NCT"]
        o_ref[...] = _h_math(x_ref[...], g_ref[...], _sel_mod(m_ref, is_ctx, D), _sel_mod(m_ref, is_ctx, 0)).astype(BF16)

    return _call(body, "h_fwd", (dm["TT"] // TM,),
                 [pl.BlockSpec((TM, D), lambda i: (i, 0)), _full((8, 3 * D)), _full((1, D))],
                 pl.BlockSpec((TM, D), lambda i: (i, 0)), _sds((dm["TT"], D), BF16), sem=("parallel",))(xcat, mods, norm_g)


def _chunk_of(d, s):
    dm = _dims()
    ncc, nch = dm["NCC"], dm["NCH"]
    return s if d == 0 else jnp.where(s < ncc, ncc - 1 - s, nch - 1 - (s - ncc))


HG_GROUP = 4
RW_GROUP = 4


def _dir_specs(make):
    return [make(d) for d in range(2)]


def _lanes(j):
    return slice(j * 128, (j + 1) * 128)


def _hg_scan_fwd(p_hg, lb3):
    dm = _dims()
    nh, nch, tt = dm["HG_H"], dm["NCH"], dm["TT"]
    pp = min(HG_GROUP, nh)
    lw_ = 128 * pp

    def body(q0, q1, i0, i1, f0, f1, lb0, lb1, o0, o1, ck_ref, s_ref):
        @pl.when(pl.program_id(1) == 0)
        def _():
            s_ref[...] = jnp.zeros_like(s_ref)

        keys, chains = [], []
        for d, refs in enumerate(((q0, i0, f0, lb0), (q1, i1, f1, lb1))):
            vals = [r[...] for r in refs]
            for j in range(pp):
                keys.append((d, j))
                chains.append((s_ref[d, j], *[a[:, _lanes(j)] for a in vals], d))
        for (d, j), chain, (out, s1) in zip(keys, chains, _hg_chunks(chains)):
            ck_ref[d, j] = chain[0]
            (o0, o1)[d][:, _lanes(j)] = out
            s_ref[d, j] = s1

    blk = lambda off: _dir_specs(lambda d: pl.BlockSpec((C, lw_), lambda h, s: (_chunk_of(d, s), off * nh // pp + h)))
    return _call(body, "hg_scan_fwd", (nh // pp, nch),
                 blk(0) + blk(1) + _dir_specs(lambda d: pl.BlockSpec((C, lw_), lambda h, s: (_chunk_of(d, s), (2 + d) * nh // pp + h)))
                 + _dir_specs(lambda d: pl.BlockSpec((None, 1, lw_), lambda h, s: (d, 0, h))),
                 _dir_specs(lambda d: pl.BlockSpec((C, lw_), lambda h, s: (_chunk_of(d, s), h)))
                 + [pl.BlockSpec((2, pp, None, 128, 128), lambda h, s: (0, h, s, 0, 0))],
                 [_sds((tt, HGW)), _sds((tt, HGW)), _sds((2, nh, nch, 128, 128))],
                 scratch=[pltpu.VMEM((2, pp, 128, 128), F32)], sem=("parallel", "arbitrary"))(*([p_hg] * 6), lb3, lb3)


def _hg_scan_bwd(p_hg, lb3, ck, do):
    dm = _dims()
    nh, nch, tt = dm["HG_H"], dm["NCH"], dm["TT"]
    pp = min(HG_GROUP, nh)
    lw_ = 128 * pp

    def body(q0, q1, i0, i1, f0, f1, lb0, lb1, ck_ref, do0, do1, dq0, dq1, di0, di1, df0, df1, dlb_ref, ds_ref):
        @pl.when(pl.program_id(1) == 0)
        def _():
            ds_ref[...] = jnp.zeros_like(ds_ref)
            dlb_ref[...] = jnp.zeros_like(dlb_ref)

        keys, flat, cts = [], [], []
        for d, refs in enumerate(((q0, i0, f0, lb0, do0), (q1, i1, f1, lb1, do1))):
            *vals, do_ = [r[...] for r in refs]
            for j in range(pp):
                keys.append((d, j))
                flat += [ck_ref[d, j], *[a[:, _lanes(j)] for a in vals]]
                cts.append((do_[:, _lanes(j)], ds_ref[d, j]))
        _, vjp = jax.vjp(_lockstep(_hg_chunks, [d for d, _ in keys], 5), *flat)
        grads = vjp(tuple(cts))
        for n, (d, j) in enumerate(keys):
            ds0, dq_, di_, df_, dlb = grads[5 * n:5 * n + 5]
            (dq0, dq1)[d][:, _lanes(j)] = dq_
            (di0, di1)[d][:, _lanes(j)] = di_
            (df0, df1)[d][:, _lanes(j)] = df_
            dlb_ref[d, :, _lanes(j)] += dlb
            ds_ref[d, j] = ds0

    ch = lambda d, s: _chunk_of(d, nch - 1 - s)
    blk = lambda off: _dir_specs(lambda d: pl.BlockSpec((C, lw_), lambda h, s: (ch(d, s), off * nh // pp + h)))
    tok = _dir_specs(lambda d: pl.BlockSpec((C, lw_), lambda h, s: (ch(d, s), h)))
    return _call(body, "hg_scan_bwd", (nh // pp, nch),
                 blk(0) + blk(1) + _dir_specs(lambda d: pl.BlockSpec((C, lw_), lambda h, s: (ch(d, s), (2 + d) * nh // pp + h)))
                 + _dir_specs(lambda d: pl.BlockSpec((None, 1, lw_), lambda h, s: (d, 0, h)))
                 + [pl.BlockSpec((2, pp, None, 128, 128), lambda h, s: (0, h, nch - 1 - s, 0, 0))] + tok,
                 tok * 3 + [pl.BlockSpec((2, 1, lw_), lambda h, s: (0, 0, h))],
                 [_sds((tt, HGW))] * 6 + [_sds((2, 1, HGW))],
                 scratch=[pltpu.VMEM((2, pp, 128, 128), F32)], sem=("parallel", "arbitrary"))(*([p_hg] * 6), lb3, lb3, ck, do, do)


def _rw_scan_fwd(sh, kap, lw, b, kd):
    dm = _dims()
    npair, nch, tt = dm["RW_P"], dm["NCH"], dm["TT"]
    pp = min(RW_GROUP, npair)
    lw_ = 128 * pp

    def body(r0, r1, v0, v1, ka0, ka1, lw0, lw1, b0, b1, kd0, kd1, y0, y1, ck_ref, tm_ref, s_ref):
        @pl.when(pl.program_id(1) == 0)
        def _():
            s_ref[...] = jnp.zeros_like(s_ref)

        keys, chains = [], []
        for d, refs in enumerate(((r0, lw0, ka0, b0, v0, kd0), (r1, lw1, ka1, b1, v1, kd1))):
            vals = [r[...] for r in refs]
            for j in range(pp):
                keys.append((d, j))
                chains.append((s_ref[d, j], *[a[:, _lanes(j)] for a in vals], d))
        for (d, j), chain, (out, s1, tm_a, tm_b) in zip(keys, chains, _rw_chunks(chains)):
            ck_ref[d, j] = chain[0]
            tm_ref[d, j, 0] = tm_a
            tm_ref[d, j, 1] = tm_b
            (y0, y1)[d][:, _lanes(j)] = out
            s_ref[d, j] = s1

    two = lambda off: _dir_specs(lambda d: pl.BlockSpec((C, lw_), lambda p, s: (_chunk_of(d, s), off * npair // pp + p)))
    three = _dir_specs(lambda d: pl.BlockSpec((None, C, lw_), lambda p, s: (d, _chunk_of(d, s), p)))
    return _call(body, "rw_scan_fwd", (npair // pp, nch), two(0) + two(2) + two(0) + three * 3,
                 two(0) + [pl.BlockSpec((2, pp, None, 128, 128), lambda p, s: (0, p, s, 0, 0)),
                           pl.BlockSpec((2, pp, None, 2, C, C), lambda p, s: (0, p, s, 0, 0, 0))],
                 [_sds((tt, RWW)), _sds((tt, RWW)), _sds((2, npair, nch, 128, 128)), _sds((2, npair, nch, 2, C, C))],
                 scratch=[pltpu.VMEM((2, pp, 128, 128), F32)], sem=("parallel", "arbitrary"))(
        sh, sh, sh, sh, kap, kap, lw, lw, b, b, kd, kd)


def _rw_scan_bwd(sh, kap, lw, b, kd, ck, tm_ck, dy):
    dm = _dims()
    npair, nch, tt = dm["RW_P"], dm["NCH"], dm["TT"]
    pp = min(RW_GROUP, npair)
    lw_ = 128 * pp

    def body(r0, r1, v0, v1, ka0, ka1, lw0, lw1, b0, b1, kd0, kd1, ck_ref, tm_ref, dy0, dy1,
             dr0, dr1, dv0, dv1, dka0, dka1, dlw0, dlw1, db0, db1, dkd0, dkd1, ds_ref):
        @pl.when(pl.program_id(1) == 0)
        def _():
            ds_ref[...] = jnp.zeros_like(ds_ref)

        keys, flat, cts, tms = [], [], [], {}
        for d, refs in enumerate(((r0, lw0, ka0, b0, v0, kd0, dy0), (r1, lw1, ka1, b1, v1, kd1, dy1))):
            *vals, dy_ = [r[...] for r in refs]
            for j in range(pp):
                tms[len(keys), 0], tms[len(keys), 1] = tm_ref[d, j, 0], tm_ref[d, j, 1]
                keys.append((d, j))
                flat += [ck_ref[d, j], *[a[:, _lanes(j)] for a in vals]]
                cts.append((dy_[:, _lanes(j)], ds_ref[d, j]))
        _, vjp = jax.vjp(_lockstep(functools.partial(_rw_chunks, tms=tms), [d for d, _ in keys], 7), *flat)
        grads = vjp(tuple(cts))
        outs = ((dr0, dlw0, dka0, db0, dv0, dkd0), (dr1, dlw1, dka1, db1, dv1, dkd1))
        for n, (d, j) in enumerate(keys):
            ds0, *g_in = grads[7 * n:7 * n + 7]
            for o_ref, g in zip(outs[d], g_in):
                o_ref[:, _lanes(j)] = g
            ds_ref[d, j] = ds0

    ch = lambda d, s: _chunk_of(d, nch - 1 - s)
    two = lambda off: _dir_specs(lambda d: pl.BlockSpec((C, lw_), lambda p, s: (ch(d, s), off * npair // pp + p)))
    three = _dir_specs(lambda d: pl.BlockSpec((None, C, lw_), lambda p, s: (d, ch(d, s), p)))
    return _call(body, "rw_scan_bwd", (npair // pp, nch),
                 two(0) + two(2) + two(0) + three * 3
                 + [pl.BlockSpec((2, pp, None, 128, 128), lambda p, s: (0, p, nch - 1 - s, 0, 0)),
                    pl.BlockSpec((2, pp, None, 2, C, C), lambda p, s: (0, p, nch - 1 - s, 0, 0, 0))] + two(0),
                 two(0) * 6, [_sds((tt, RWW))] * 12,
                 scratch=[pltpu.VMEM((2, pp, 128, 128), F32)], sem=("parallel", "arbitrary"))(
        sh, sh, sh, sh, kap, kap, lw, lw, b, b, kd, kd, ck, tm_ck, dy, dy)


def _tile_pos(i):
    dm = _dims()
    is_ctx = i < dm["NCT"]
    rows = lax.broadcasted_iota(jnp.int32, (TM, 1), 0)
    tl = rows + (i - dm["NCT"]) * TM
    width = jnp.where(is_ctx, TC, GW)
    assert TC & (TC - 1) == 0 and GW & (GW - 1) == 0 and TM % GW == 0
    colp = rows & (width - 1)
    return is_ctx, rows, tl, colp, width


def _shift_lr(cur, i):
    _, _, _, colp, width = _tile_pos(i)
    left = jnp.where(colp == 0, 0.0, pltpu.roll(cur, 1, 0))
    right = jnp.where(colp == width - 1, 0.0, pltpu.roll(cur, TM - 1, 0))
    return left, right


def _shift_ud(cur, prv, nxt, i):
    is_ctx, rows, tl, _, _ = _tile_pos(i)
    if TM > GW:
        up = jnp.where(rows >= GW, pltpu.roll(cur, GW, 0), pltpu.roll(prv, GW, 0))
        down = jnp.where(rows < TM - GW, pltpu.roll(cur, TM - GW, 0), pltpu.roll(nxt, TM - GW, 0))
    else:
        up, down = prv, nxt
    up = jnp.where(jnp.logical_or(is_ctx, tl < GW), 0.0, up)
    down = jnp.where(jnp.logical_or(is_ctx, tl >= T - GW), 0.0, down)
    return up, down


def _shift_specs():
    dm = _dims()
    nt = dm["TT"] // TM
    cw = max(w for w in range(128, 1664 + 1, 128) if dm["SH"] % w == 0)
    cur = pl.BlockSpec((TM, cw), lambda j, i: (i, j))
    prv = pl.BlockSpec((TM, cw), lambda j, i: (jnp.maximum(i - 1, 0), j))
    nxt = pl.BlockSpec((TM, cw), lambda j, i: (jnp.minimum(i + 1, nt - 1), j))
    mu = pl.BlockSpec((4, cw), lambda j, i: (0, j))
    return nt, cw, cur, prv, nxt, mu


def _shift_fwd(p_rw, mu):
    dm = _dims()
    nt, cw, cur, prv, nxt, mus = _shift_specs()

    def body(c_ref, p_ref, n_ref, mu_ref, o_ref):
        i = pl.program_id(1)
        p, m = c_ref[...], mu_ref[...]
        left, right = _shift_lr(p, i)
        up, down = _shift_ud(p, p_ref[...], n_ref[...], i)
        vert = jnp.where(i < dm["NCT"], 0.0, 1.0)
        o_ref[...] = (p * (1.0 - m[0:1] - m[1:2] - vert * (m[2:3] + m[3:4]))
                      + m[0:1] * left + m[1:2] * right + m[2:3] * up + m[3:4] * down)

    return _call(body, "shift_fwd", (dm["SH"] // cw, nt), [cur, prv, nxt, mus], cur, _sds((dm["TT"], dm["SH"])),
                 sem=("parallel", "parallel"))(p_rw, p_rw, p_rw, mu)


def _shift_bwd(p_rw, dsh, mu):
    dm = _dims()
    nt, cw, cur, prv, nxt, mus = _shift_specs()

    def body(c_ref, p_ref, n_ref, gc_ref, gp_ref, gn_ref, mu_ref, dp_ref, dmu_ref):
        i = pl.program_id(1)
        p, g, m = c_ref[...], gc_ref[...], mu_ref[...]
        vert = jnp.where(i < dm["NCT"], 0.0, 1.0)
        _, from_right = _shift_lr(m[0:1] * g, i)
        from_left, _ = _shift_lr(m[1:2] * g, i)
        _, from_down = _shift_ud(m[2:3] * g, g, m[2:3] * gn_ref[...], i)
        from_up, _ = _shift_ud(m[3:4] * g, m[3:4] * gp_ref[...], g, i)
        dp = g * (1.0 - m[0:1] - m[1:2] - vert * (m[2:3] + m[3:4])) + from_right + from_left + from_down + from_up
        dp_ref[...] = dp.astype(BF16)

        @pl.when(i == 0)
        def _():
            dmu_ref[...] = jnp.zeros_like(dmu_ref)

        left, right = _shift_lr(p, i)
        up, down = _shift_ud(p, p_ref[...], n_ref[...], i)
        s = lambda a: jnp.sum(a, axis=0, keepdims=True)
        dmu_ref[...] += jnp.concatenate([s(g * (left - p)), s(g * (right - p)), vert * s(g * (up - p)), vert * s(g * (down - p))], axis=0)

    return _call(body, "shift_bwd", (dm["SH"] // cw, nt), [cur, prv, nxt, cur, prv, nxt, mus],
                 [cur, pl.BlockSpec((4, cw), lambda j, i: (0, j))], [_sds((dm["TT"], dm["SH"]), BF16), _sds((4, dm["SH"]))],
                 sem=("parallel", "arbitrary"))(p_rw, p_rw, p_rw, dsh, dsh, dsh, mu)


def _acc_out(ref, first, vals):
    @pl.when(first)
    def _():
        for r in ref:
            r[...] = jnp.zeros_like(r)

    for r, v in zip(ref, vals):
        r[...] += v


def _rowsum(a):
    return jnp.sum(a, axis=0, keepdims=True)


def _rw_prep_specs():
    dm = _dims()
    rw = RWW // 128
    tok = lambda width, cb: pl.BlockSpec((TMV, width), lambda i: (i, cb))
    ins = [pl.BlockSpec((TMV, RWW), lambda i: (i, 1)), tok(128, 3 * rw), tok(128, 3 * rw + 1),
           _full((2, RWW)), _full((128, RWW)), _full((128, RWW)), _full((2, RWW)), _full((128, RWW)), _full((128, RWW)),
           _full((1, RWW)), _full((1, RWW))]
    return dm, ins


def _rw_prep_fwd(sh, w0, w2p0, w2p1, a0, a2p0, a2p1, kkp, kap_p):
    dm, ins = _rw_prep_specs()
    tt = dm["TT"]

    def body(k_ref, low_ref, loa_ref, w0_ref, w20_ref, w21_ref, a0_ref, a20_ref, a21_ref, kk_ref, ka_ref, kap_ref, lw_ref, b_ref, kd_ref):
        tw, la = jnp.tanh(low_ref[...]), loa_ref[...]
        kap, lw0, b0, kd0, lw1, b1, kd1 = _rw_prep_math(
            k_ref[...], _mm(tw, w20_ref[...]), _mm(tw, w21_ref[...]), _mm(la, a20_ref[...]), _mm(la, a21_ref[...]),
            w0_ref[...], a0_ref[...], kk_ref[...], ka_ref[...])
        kap_ref[...] = kap
        lw_ref[0], lw_ref[1] = lw0, lw1
        b_ref[0], b_ref[1] = b0, b1
        kd_ref[0], kd_ref[1] = kd0, kd1

    two = pl.BlockSpec((2, TMV, RWW), lambda i: (0, i, 0))
    return _call(body, "rw_prep_fwd", (tt // TMV,), ins, [pl.BlockSpec((TMV, RWW), lambda i: (i, 0)), two, two, two],
                 [_sds((tt, RWW)), _sds((2, tt, RWW)), _sds((2, tt, RWW)), _sds((2, tt, RWW))], sem=("parallel",))(
        sh, sh, sh, w0, w2p0, w2p1, a0, a2p0, a2p1, kkp, kap_p)


def _rw_prep_bwd(sh, w0, w2p0, w2p1, a0, a2p0, a2p1, kkp, kap_p, scan_grads, dkd_p, dr_p, dv_p):
    dm, ins = _rw_prep_specs()
    tt, sh_w = dm["TT"], dm["SH"]
    one = pl.BlockSpec((TMV, RWW), lambda i: (i, 0))

    def body(k_ref, low_ref, loa_ref, w0_ref, w20_ref, w21_ref, a0_ref, a20_ref, a21_ref, kk_ref, ka_ref,
             dr0, dr1, dv0, dv1, dka0, dka1, dlw0, dlw1, db0, db1, dkd0, dkd1, dkdp_ref, drp_ref, dvp_ref,
             dsh_ref, dw0_ref, dw20_ref, dw21_ref, da0_ref, da20_ref, da21_ref, dkk_ref, dka_ref):
        tw, la = jnp.tanh(low_ref[...]), loa_ref[...]
        w20, w21, a20, a21 = w20_ref[...], w21_ref[...], a20_ref[...], a21_ref[...]
        _, vjp = jax.vjp(_rw_prep_math, k_ref[...], _mm(tw, w20), _mm(tw, w21), _mm(la, a20), _mm(la, a21),
                         w0_ref[...], a0_ref[...], kk_ref[...], ka_ref[...])
        dkp = dkdp_ref[...]
        dk, dpw0, dpw1, dpa0, dpa1, dw0, da0, dkk, dka = vjp(
            (dka0[...] + dka1[...], dlw0[...], db0[...], dkd0[...] + dkp, dlw1[...], db1[...], dkd1[...] + dkp))
        twt, lat = tw.T, la.T
        dtw = _mm_nt(dpw0, w20) + _mm_nt(dpw1, w21)
        dsh_ref[:, 0:RWW] = dr0[...] + dr1[...] + drp_ref[...]
        dsh_ref[:, RWW:2 * RWW] = dk
        dsh_ref[:, 2 * RWW:3 * RWW] = dv0[...] + dv1[...] + dvp_ref[...]
        dsh_ref[:, 3 * RWW:3 * RWW + 128] = dtw * (1.0 - tw * tw)
        dsh_ref[:, 3 * RWW + 128:3 * RWW + 256] = _mm_nt(dpa0, a20) + _mm_nt(dpa1, a21)
        _acc_out((dw0_ref, dw20_ref, dw21_ref, da0_ref, da20_ref, da21_ref, dkk_ref, dka_ref), pl.program_id(0) == 0,
                 (dw0, _mm(twt, dpw0), _mm(twt, dpw1), da0, _mm(lat, dpa0), _mm(lat, dpa1), dkk, dka))

    par = [_full((2, RWW)), _full((128, RWW)), _full((128, RWW)), _full((2, RWW)), _full((128, RWW)), _full((128, RWW)),
           _full((1, RWW)), _full((1, RWW))]
    return _call(body, "rw_prep_bwd", (tt // TMV,), ins + [one] * 15,
                 [pl.BlockSpec((TMV, sh_w), lambda i: (i, 0))] + par,
                 [_sds((tt, sh_w)), _sds((2, RWW)), _sds((128, RWW)), _sds((128, RWW)), _sds((2, RWW)), _sds((128, RWW)), _sds((128, RWW)),
                  _sds((1, RWW)), _sds((1, RWW))],
                 sem=("arbitrary",))(sh, sh, sh, w0, w2p0, w2p1, a0, a2p0, a2p1, kkp, kap_p, *scan_grads, dkd_p, dr_p, dv_p)


def _rw_post_specs():
    dm = _dims()
    nv = dm["NCTV"]
    rw = RWW // 128
    lat3 = lambda d: pl.BlockSpec((None, TMV, RWW), lambda i: (d, nv + i, 0))
    lat2 = pl.BlockSpec((TMV, RWW), lambda i: (nv + i, 0))
    ins = [lat2, lat2, lat3(0), lat3(1),
           pl.BlockSpec((TMV, RWW), lambda i: (nv + i, 0)), pl.BlockSpec((TMV, RWW), lambda i: (nv + i, 2)),
           pl.BlockSpec((TMV, RWW), lambda i: (i, 0)), _full((1, RWW)), _full((1, RWW)), _full((1, RWW))]
    return dm, nv, ins


def _rw_post_fwd(y0_, y1_, kd, sh, p_z, rk, gng, gnb):
    dm, nv, ins = _rw_post_specs()

    def body(y0, y1, k0, k1, r, v, z, rk_ref, g_ref, b_ref, o_ref):
        o_ref[...] = _rw_post_math(y0[...], y1[...], k0[...], k1[...], r[...], v[...], z[...], rk_ref[...], g_ref[...], b_ref[...]).astype(BF16)

    return _call(body, "rw_post_fwd", (T // TMV,), ins, pl.BlockSpec((TMV, RWW), lambda i: (i, 0)), _sds((T, RWW), BF16),
                 sem=("parallel",))(y0_, y1_, kd, kd, sh, sh, p_z, rk, gng, gnb)


def _rw_post_bwd(y0_, y1_, kd, sh, p_z, rk, gng, gnb, dout):
    dm, nv, _ = _rw_post_specs()
    tt = dm["TT"]
    cl = lambda i: jnp.maximum(i - nv, 0)
    all3 = lambda d: pl.BlockSpec((None, TMV, RWW), lambda i: (d, i, 0))
    all2 = pl.BlockSpec((TMV, RWW), lambda i: (i, 0))
    ins = [all2, all2, all3(0), all3(1),
           pl.BlockSpec((TMV, RWW), lambda i: (i, 0)), pl.BlockSpec((TMV, RWW), lambda i: (i, 2)),
           pl.BlockSpec((TMV, RWW), lambda i: (cl(i), 0)), _full((1, RWW)), _full((1, RWW)), _full((1, RWW)),
           pl.BlockSpec((TMV, RWW), lambda i: (cl(i), 0))]
    tok = pl.BlockSpec((TMV, RWW), lambda i: (i, 0))

    def body(y0, y1, k0, k1, r, v, z, rk_ref, g_ref, b_ref, do_ref, dy_ref, dkd_ref, dr_ref, dv_ref, dz_ref, drk_ref, dg_ref, db_ref):
        i = pl.program_id(0)
        lat = jnp.where(i >= nv, 1.0, 0.0)
        _, vjp = jax.vjp(_rw_post_math, y0[...], y1[...], k0[...], k1[...], r[...], v[...], z[...], rk_ref[...], g_ref[...], b_ref[...])
        dy0, _, dk0, _, dr, dv, dz, drk, dg, db = vjp(do_ref[...] * lat)
        dy_ref[...] = dy0
        dkd_ref[...] = dk0
        dr_ref[...] = dr
        dv_ref[...] = dv
        dz_ref[...] = dz.astype(BF16)
        _acc_out((drk_ref, dg_ref, db_ref), i == 0, (drk, dg, db))

    return _call(body, "rw_post_bwd", (tt // TMV,), ins,
                 [tok, tok, tok, tok, pl.BlockSpec((TMV, RWW), lambda i: (cl(i), 0)), _full((1, RWW)), _full((1, RWW)), _full((1, RWW))],
                 [_sds((tt, RWW))] * 4 + [_sds((T, RWW), BF16), _sds((1, RWW)), _sds((1, RWW)), _sds((1, RWW))],
                 sem=("arbitrary",))(y0_, y1_, kd, kd, sh, sh, p_z, rk, gng, gnb, dout)


def _hg_post_fwd(o0_, o1_, p_hg, hg_g):
    dm = _dims()
    nv = dm["NCTV"]
    lat3 = lambda d: pl.BlockSpec((TMV, HGW), lambda i: (nv + i, 0))

    def body(o0, o1, z, g, out):
        out[...] = _hg_post_math(o0[...], o1[...], z[...], g[...]).astype(BF16)

    return _call(body, "hg_post_fwd", (T // TMV,), [lat3(0), lat3(1), pl.BlockSpec((TMV, HGW), lambda i: (nv + i, 4)), _full((1, HGW))],
                 pl.BlockSpec((TMV, HGW), lambda i: (i, 0)), _sds((T, HGW), BF16), sem=("parallel",))(o0_, o1_, p_hg, hg_g)


def _hg_post_bwd(o0_, o1_, p_hg, hg_g, dout):
    dm = _dims()
    nv, tt = dm["NCTV"], dm["TT"]
    all3 = lambda d: pl.BlockSpec((TMV, HGW), lambda i: (i, 0))
    tok = pl.BlockSpec((TMV, HGW), lambda i: (i, 0))

    def body(o0, o1, z, g, do_in, do_ref, dz_ref, dg_ref):
        i = pl.program_id(0)
        lat = jnp.where(i >= nv, 1.0, 0.0)
        _, vjp = jax.vjp(_hg_post_math, o0[...], o1[...], z[...], g[...])
        d0, _, dz, dg = vjp(do_in[...] * lat)
        do_ref[...] = d0
        dz_ref[...] = dz
        _acc_out((dg_ref,), i == 0, (dg,))

    return _call(body, "hg_post_bwd", (tt // TMV,),
                 [all3(0), all3(1), pl.BlockSpec((TMV, HGW), lambda i: (i, 4)), _full((1, HGW)),
                  pl.BlockSpec((TMV, HGW), lambda i: (jnp.maximum(i - nv, 0), 0))],
                 [tok, tok, _full((1, HGW))], [_sds((tt, HGW)), _sds((tt, HGW)), _sds((1, HGW))], sem=("arbitrary",))(o0_, o1_, p_hg, hg_g, dout)


def _hg_dproj(dq0, dq1, di0, di1, df0, df1, dz):
    dm = _dims()
    tt = dm["TT"]
    tok = pl.BlockSpec((TMV, HGW), lambda i: (i, 0))

    def body(q0, q1, i0, i1, f0, f1, z, o_ref):
        o_ref[:, 0:HGW] = (q0[...] + q1[...]).astype(BF16)
        o_ref[:, HGW:2 * HGW] = (i0[...] + i1[...]).astype(BF16)
        o_ref[:, 2 * HGW:3 * HGW] = f0[...].astype(BF16)
        o_ref[:, 3 * HGW:4 * HGW] = f1[...].astype(BF16)
        o_ref[:, 4 * HGW:5 * HGW] = z[...].astype(BF16)

    return _call(body, "hg_dproj", (tt // TMV,), [tok] * 7,
                 pl.BlockSpec((TMV, 5 * HGW), lambda i: (i, 0)), _sds((tt, 5 * HGW), BF16), sem=("parallel",))(
        dq0, dq1, di0, di1, df0, df1, dz)


def _merge1_fwd(out_hg, out_rw, whg_t, wrw_t, p_g):
    tok = pl.BlockSpec((TM, D), lambda i: (i, 0))

    def body(h_ref, r_ref, wh_ref, wr_ref, g1_ref, g2_ref, a_ref, b_ref, m_ref):
        nt = (((1,), (1,)), ((), ()))
        a = lax.dot_general(h_ref[...], wh_ref[...], nt, preferred_element_type=F32)
        b = lax.dot_general(r_ref[...], wr_ref[...], nt, preferred_element_type=F32)
        a_ref[...] = a
        b_ref[...] = b
        m_ref[...] = _gate_math(g1_ref[...], g2_ref[...], a, b).astype(BF16)

    return _call(body, "merge1_fwd", (T // TM,),
                 [pl.BlockSpec((TM, HGW), lambda i: (i, 0)), pl.BlockSpec((TM, RWW), lambda i: (i, 0)), _full((D, HGW)), _full((D, RWW)),
                  pl.BlockSpec((TM, D), lambda i: (i, 0)), pl.BlockSpec((TM, D), lambda i: (i, 1))],
                 [tok, tok, tok], [_sds((T, D)), _sds((T, D)), _sds((T, D), BF16)], sem=("parallel",))(out_hg, out_rw, whg_t, wrw_t, p_g, p_g)


def _head_fwd_bwd(x, merged, w_out, mods, final_g, tgt):
    tok = pl.BlockSpec((TM, D), lambda i: (i, 0))

    def body(x_ref, m_ref, w_ref, mods_ref, fg_ref, t_ref, dxo_ref, dmo_ref, loss_ref, dgate_ref, dfg_ref):
        mo = jnp.dot(m_ref[...], w_ref[...], preferred_element_type=F32)
        gate = mods_ref[0:1, 2 * D:3 * D]
        loss, vjp = jax.vjp(lambda x_, mo_, g_, fg_: _head_math(x_, mo_, g_, fg_, t_ref[...]), x_ref[...], mo, gate, fg_ref[...])
        dx, dmo, dgate, dfg = vjp(jnp.ones((1, 1), F32))
        dxo_ref[...] = dx
        dmo_ref[...] = dmo.astype(BF16)
        _acc_out((loss_ref, dgate_ref, dfg_ref), pl.program_id(0) == 0, (jnp.broadcast_to(loss, (1, 128)), dgate, dfg))

    return _call(body, "head_fwd_bwd", (T // TM,), [tok, tok, _full((D, D)), _full((8, 3 * D)), _full((1, D)), tok],
                 [tok, tok, _full((1, 128)), _full((1, D)), _full((1, D))],
                 [_sds((T, D)), _sds((T, D), BF16), _sds((1, 128)), _sds((1, D)), _sds((1, D))], sem=("arbitrary",))(
        x, merged, w_out, mods, final_g, tgt)


def _merge_bwd(dmo, w_out, p_g, a, b):
    tok = pl.BlockSpec((TM, D), lambda i: (i, 0))

    def body(dmo_ref, w_ref, g1_ref, g2_ref, a_ref, b_ref, da_ref, db_ref, dg_ref):
        dm_ = lax.dot_general(dmo_ref[...], w_ref[...], (((1,), (1,)), ((), ())), preferred_element_type=F32)
        _, vjp = jax.vjp(_gate_math, g1_ref[...], g2_ref[...], a_ref[...], b_ref[...])
        dg1, dg2, da, db = vjp(dm_)
        da_ref[...] = da.astype(BF16)
        db_ref[...] = db.astype(BF16)
        dg_ref[:, 0:D] = dg1.astype(BF16)
        dg_ref[:, D:2 * D] = dg2.astype(BF16)

    return _call(body, "merge_bwd", (T // TM,),
                 [tok, _full((D, D)), pl.BlockSpec((TM, D), lambda i: (i, 0)), pl.BlockSpec((TM, D), lambda i: (i, 1)), tok, tok],
                 [tok, tok, pl.BlockSpec((TM, 2 * D), lambda i: (i, 0))], [_sds((T, D), BF16), _sds((T, D), BF16), _sds((T, 2 * D), BF16)],
                 sem=("parallel",))(dmo, w_out, p_g, p_g, a, b)


def _h_bwd(xcat, mods, norm_g, dh_hg, dh_rw, dh_z, dh_g, dxo, dgate):
    dm = _dims()
    nct, tt = dm["NCT"], dm["TT"]
    tok = pl.BlockSpec((TM, D), lambda i: (i, 0))
    lat = pl.BlockSpec((TM, D), lambda i: (jnp.maximum(i - nct, 0), 0))

    def body(x_ref, m_ref, g_ref, d1, d2, d3, d4, dxo_ref, dgate_ref, gx_ref, dng_ref, dmods_ref):
        i = pl.program_id(0)
        is_ctx = i < nct
        latf = jnp.where(is_ctx, 0.0, 1.0)
        dh = d1[...] + d2[...] + latf * (d3[...] + d4[...])
        _, vjp = jax.vjp(_h_math, x_ref[...], g_ref[...], _sel_mod(m_ref, is_ctx, D), _sel_mod(m_ref, is_ctx, 0))
        dx, dng, dscale, dshift = vjp(dh)
        gx_ref[...] = dx + dxo_ref[...]

        @pl.when(i == 0)
        def _():
            dng_ref[...] = jnp.zeros_like(dng_ref)
            dmods_ref[...] = jnp.zeros_like(dmods_ref)
            dmods_ref[0:1, 2 * D:3 * D] = dgate_ref[...]

        dng_ref[...] += dng
        row = lax.broadcasted_iota(jnp.int32, (8, 1), 0)
        sel = jnp.where(row == jnp.where(is_ctx, 1, 0), 1.0, 0.0)
        dmods_ref[:, 0:D] += sel * dshift
        dmods_ref[:, D:2 * D] += sel * dscale

    return _call(body, "h_bwd", (tt // TM,), [tok, _full((8, 3 * D)), _full((1, D)), tok, tok, lat, lat, lat, _full((1, D))],
                 [lat, _full((1, D)), _full((8, 3 * D))], [_sds((T, D)), _sds((1, D)), _sds((8, 3 * D))], sem=("arbitrary",))(
        xcat, mods, norm_g, dh_hg, dh_rw, dh_z, dh_g, dxo, dgate)


def _lb_math(l0, l1):
    return jax.nn.sigmoid(l0 - l1)


def _lb_fwd(l0, l1):
    def body(a_ref, b_ref, o_ref):
        o_ref[...] = _lb_math(a_ref[...], b_ref[...])

    return _call(body, "lb_fwd", (1,), [_full((2, HGW))] * 2, _full((2, HGW)), _sds((2, HGW)))(l0, l1)


def _lb_bwd(l0, l1, dlb):
    def body(a_ref, b_ref, d_ref, da_ref, db_ref):
        _, vjp = jax.vjp(_lb_math, a_ref[...], b_ref[...])
        da_ref[...], db_ref[...] = vjp(d_ref[...])

    return _call(body, "lb_bwd", (1,), [_full((2, HGW))] * 3, [_full((2, HGW))] * 2, [_sds((2, HGW))] * 2)(l0, l1, dlb)


def _local_step(x, ctx, tgt, mods, w, start_exchange):
    dm = _dims()
    tt, sh_w, hgc = dm["TT"], dm["SH"], dm["HGC"]
    nb = lambda cols: cols // WB
    xcat = jnp.concatenate([ctx, x], axis=0)
    h = _h_fwd(xcat, mods, w["norm_g"])
    win_t = w["win_t"]
    p_hg = _proj_nt(h, win_t, 0, nb(hgc), 0, tt, "proj_hg")
    p_rw = _proj_nt(h, win_t, nb(hgc), nb(sh_w), 0, tt, "proj_rw")
    p_z = _proj_nt(h, win_t, nb(hgc + sh_w), nb(RWW), TC, T, "proj_z")
    p_g = _proj_nt(h, win_t, nb(hgc + sh_w + RWW), nb(2 * D), TC, T, "proj_g")

    lb3 = _lb_fwd(w["lb0"], w["lb1"]).reshape(2, 1, HGW)
    o0, o1, hg_ck = _hg_scan_fwd(p_hg, lb3)
    out_hg = _hg_post_fwd(o0, o1, p_hg, w["hg_g"])

    sh = _shift_fwd(p_rw, w["mu"])
    zpad = jnp.zeros((LORA, RWW), F32)
    prep_w = (w["w0"], jnp.concatenate([w["w2"][0], zpad], 0), jnp.concatenate([zpad, w["w2"][1]], 0),
              w["a0"], jnp.concatenate([w["a2"][0], zpad], 0), jnp.concatenate([zpad, w["a2"][1]], 0), w["kk"], w["ka"])
    kap, lw, b, kd = _rw_prep_fwd(sh, *prep_w)
    y0, y1, rw_ck, rw_tm = _rw_scan_fwd(sh, kap, lw, b, kd)
    post_w = (w["rk"], w["gng"], w["gnb"])
    out_rw = _rw_post_fwd(y0, y1, kd, sh, p_z, *post_w)

    a, bb, merged = _merge1_fwd(out_hg, out_rw, w["whg_t"], w["wrw_t"], p_g)
    dxo, dmo, loss, dgate, d_fg = _head_fwd_bwd(x, merged, w["wout"], mods, w["final_g"], tgt)
    da, db, dp_g = _merge_bwd(dmo, w["wout"], p_g, a, bb)
    g_wout = _dw_tn(merged, dmo, 0, D, 0, "dw_out")
    g_whg = _dw_tn(da, out_hg, 0, D, 0, "dw_hg")
    g_wrw = _dw_tn(db, out_rw, 0, D, 0, "dw_rw")
    token, early = start_exchange(dict(wout=g_wout, whg_t=g_whg, wrw_t=g_wrw), "early")
    d_out_hg = _mm_nn(da, w["whg_t"], 0, "dx_hg", after=token)
    d_out_rw = _mm_nn(db, w["wrw_t"], 0, "dx_rw")

    do, dz_hg, d_hg_g = _hg_post_bwd(o0, o1, p_hg, w["hg_g"], d_out_hg)
    *hg_grads, dlb3 = _hg_scan_bwd(p_hg, lb3, hg_ck, do)
    dp_hg = _hg_dproj(*hg_grads, dz_hg)
    d_lb0, d_lb1 = _lb_bwd(w["lb0"], w["lb1"], dlb3.reshape(2, HGW))

    dy, dkd_p, dr_p, dv_p, dp_z, d_rk, d_gng, d_gnb = _rw_post_bwd(y0, y1, kd, sh, p_z, *post_w, d_out_rw)
    rw_grads = _rw_scan_bwd(sh, kap, lw, b, kd, rw_ck, rw_tm, dy)
    dsh, d_w0, d_w2p0, d_w2p1, d_a0, d_a2p0, d_a2p1, d_kk, d_ka = _rw_prep_bwd(sh, *prep_w, rw_grads, dkd_p, dr_p, dv_p)
    dp_rw, d_mu = _shift_bwd(p_rw, dsh, w["mu"])

    g_win = _dw_tn(dp_hg, h, 0, dm["NCOLS"], 0, "dw_in_hg")
    g_win = _dw_tn(dp_rw, h, 0, dm["NCOLS"], nb(hgc), "dw_in_rw", prev=g_win)
    g_win = _dw_tn(dp_z, h, TC, dm["NCOLS"], nb(hgc + sh_w), "dw_in_z", prev=g_win)
    g_win = _dw_tn(dp_g, h, TC, dm["NCOLS"], nb(hgc + sh_w + RWW), "dw_in_g", prev=g_win)
    token, late = start_exchange(dict(win_t=g_win), "late")
    in_flight = (early, late)
    dh_hg = _mm_nn(dp_hg, win_t, 0, "dh_hg", after=token)
    dh_rw = _mm_nn(dp_rw, win_t, nb(hgc), "dh_rw")
    dh_z = _mm_nn(dp_z, win_t, nb(hgc + sh_w), "dh_z")
    dh_g = _mm_nn(dp_g, win_t, nb(hgc + sh_w + RWW), "dh_g")
    gx, d_ng, dmods = _h_bwd(xcat, mods, w["norm_g"], dh_hg, dh_rw, dh_z, dh_g, dxo, dgate)

    small = dict(norm_g=d_ng, lb0=d_lb0, lb1=d_lb1, hg_g=d_hg_g, mu=d_mu, w0=d_w0,
                 w2=jnp.stack([d_w2p0[:LORA], d_w2p1[LORA:]]), a0=d_a0, a2=jnp.stack([d_a2p0[:LORA], d_a2p1[LORA:]]),
                 kk=d_kk, ka=d_ka, rk=d_rk, gng=d_gng, gnb=d_gnb, final_g=d_fg)
    return loss, gx, dmods, in_flight, small


MESH = pl.DeviceIdType.MESH


def _comm_call(body, name, bufs, out_shapes, nsem, nloc):
    hbm = pl.BlockSpec(memory_space=pl.ANY)
    return pl.pallas_call(
        body, name=name, in_specs=[hbm] * len(bufs), out_specs=[hbm] * len(out_shapes), out_shape=out_shapes,
        scratch_shapes=[pltpu.SemaphoreType.DMA((nsem,)), pltpu.SemaphoreType.DMA((nsem,)), pltpu.SemaphoreType.DMA((nloc,))],
    )(*bufs)


def _gather2(bufs, name):
    nbuf = len(bufs)

    def body(*refs):
        ins, outs = refs[:nbuf], refs[nbuf:2 * nbuf]
        send_sems, recv_sems, loc_sems = refs[2 * nbuf:]
        x, y, c = lax.axis_index("x"), lax.axis_index("y"), lax.axis_index("c")
        me, sib = (x, y, c), (x, y, 1 - c)
        chips = [(1 - x, y), (x, 1 - y), (1 - x, 1 - y)]

        def copy(bi, k, block, to, src=None):
            rows = outs[bi].at[4 * block[0] + 2 * block[1] + block[2]]
            return pltpu.make_async_remote_copy(src_ref=rows if src is None else src, dst_ref=rows, send_sem=send_sems.at[7 * bi + k],
                                                recv_sem=recv_sems.at[7 * bi + k], device_id=to, device_id_type=MESH)

        own = [pltpu.make_async_copy(ins[bi], outs[bi].at[4 * x + 2 * y + c], loc_sems.at[bi]) for bi in range(nbuf)]
        for cp in own:
            cp.start()
        first = []
        for bi in range(nbuf):
            first.append(copy(bi, 0, me, sib, src=ins[bi]))
            first += [copy(bi, 1 + j, me, (*chip, c), src=ins[bi]) for j, chip in enumerate(chips)]
        for cp in first:
            cp.start()
        passed = []
        for j, chip in enumerate(chips):
            for bi in range(nbuf):
                copy(bi, 1 + j, (*chip, c), me).wait_recv()
                passed.append(copy(bi, 4 + j, (*chip, c), sib))
                passed[-1].start()
        for bi in range(nbuf):
            copy(bi, 0, sib, me).wait_recv()
            for j, chip in enumerate(chips):
                copy(bi, 4 + j, (*chip, 1 - c), me).wait_recv()
        for cp in first + passed:
            cp.wait_send()
        for cp in own:
            cp.wait()

    return _comm_call(body, name, bufs, [_sds((NDEV,) + b.shape, b.dtype) for b in bufs], 7 * nbuf, nbuf)


def _pair_exchange(bufs, name):
    nbuf = len(bufs)

    def body(*refs):
        ins, got = refs[:nbuf], refs[nbuf:2 * nbuf]
        send_sems, recv_sems, _ = refs[2 * nbuf:]
        x, y, c = lax.axis_index("x"), lax.axis_index("y"), lax.axis_index("c")
        copies = []
        for bi in range(nbuf):
            for q in range(4):
                copies.append(pltpu.make_async_remote_copy(
                    src_ref=ins[bi].at[2 * q + 1 - c], dst_ref=got[bi].at[q], send_sem=send_sems.at[4 * bi + q],
                    recv_sem=recv_sems.at[4 * bi + q], device_id=(x, y, 1 - c), device_id_type=MESH))
                copies[-1].start()
        for cp in copies:
            cp.wait_send()
            cp.wait_recv()

    return _comm_call(body, name, bufs, [_sds((4,) + b.shape[1:], b.dtype) for b in bufs], 4 * nbuf, 1)


def _chip_copies(srcs, lands, send_sems, recv_sems):
    x, y, c = lax.axis_index("x"), lax.axis_index("y"), lax.axis_index("c")
    myq = 2 * x + y
    pairs = []
    for bi in range(len(srcs)):
        for j, (qx, qy) in enumerate([(1 - x, y), (x, 1 - y), (1 - x, 1 - y)]):
            q = 2 * qx + qy
            mk = lambda src, dst: pltpu.make_async_remote_copy(
                src_ref=src, dst_ref=dst, send_sem=send_sems.at[3 * bi + j], recv_sem=recv_sems.at[3 * bi + j],
                device_id=(qx, qy, c), device_id_type=MESH)
            pairs.append((mk(srcs[bi].at[q], lands[bi].at[myq]), mk(srcs[bi].at[myq], lands[bi].at[q])))
    return pairs


def _chip_exchange_start(bufs, name):
    nbuf = len(bufs)
    hbm, sem = pl.BlockSpec(memory_space=pltpu.HBM), pl.BlockSpec(memory_space=pltpu.SEMAPHORE)

    def body(*refs):
        srcs, lands = refs[:nbuf], refs[nbuf:2 * nbuf]
        send_sems, recv_sems = refs[2 * nbuf], refs[2 * nbuf + 1]
        token = refs[-1]
        for send, _ in _chip_copies(srcs, lands, send_sems, recv_sems):
            send.start()
        token[...] = jnp.zeros_like(token)

    held = [pltpu.HBM(b.shape, b.dtype) for b in bufs]
    outs = pl.pallas_call(
        body, name=name, in_specs=[hbm] * (2 * nbuf), out_specs=(sem, sem, *[hbm] * (2 * nbuf), pl.BlockSpec(memory_space=pltpu.VMEM)),
        out_shape=(pltpu.SemaphoreType.DMA((3 * nbuf,)), pltpu.SemaphoreType.DMA((3 * nbuf,)), *held, *held, _sds((8, 128))),
        input_output_aliases={i: 2 + i for i in range(2 * nbuf)},
        compiler_params=pltpu.CompilerParams(has_side_effects=pltpu.SideEffectType.DATAFLOW_SIDE_EFFECTING),
    )(*[pltpu.with_memory_space_constraint(b, pltpu.HBM) for b in bufs],
      *[pltpu.with_memory_space_constraint(lax.empty(b.shape, b.dtype), pltpu.HBM) for b in bufs])
    return outs[0], outs[1], list(outs[2:2 + nbuf]), list(outs[2 + nbuf:2 + 2 * nbuf]), outs[-1]


def _chip_exchange_wait(send_sems, recv_sems, srcs, lands, after, name):
    nbuf = len(srcs)
    hbm, sem = pl.BlockSpec(memory_space=pltpu.HBM), pl.BlockSpec(memory_space=pltpu.SEMAPHORE)

    def body(*refs):
        src_refs, land_refs = refs[:nbuf], refs[nbuf:2 * nbuf]
        for send, recv in _chip_copies(src_refs, land_refs, refs[2 * nbuf], refs[2 * nbuf + 1]):
            send.wait_send()
            recv.wait_recv()

    held = [pltpu.HBM(b.shape, b.dtype) for b in srcs]
    outs = pl.pallas_call(
        body, name=name, in_specs=[hbm] * (2 * nbuf) + [sem, sem, pl.BlockSpec(memory_space=pl.ANY)], out_specs=[hbm] * (2 * nbuf),
        out_shape=(*held, *held), input_output_aliases={i: i for i in range(2 * nbuf)},
        compiler_params=pltpu.CompilerParams(has_side_effects=pltpu.SideEffectType.DATAFLOW_SIDE_EFFECTING),
    )(*srcs, *lands, send_sems, recv_sems, after)
    return list(outs[:nbuf]), list(outs[nbuf:])


def _prefetch_call(body, name, scalars, grid, in_specs, out_specs, out_shape, args):
    return pl.pallas_call(
        body, name=name, out_shape=out_shape,
        grid_spec=pltpu.PrefetchScalarGridSpec(num_scalar_prefetch=1, grid=grid, in_specs=in_specs, out_specs=out_specs),
        compiler_params=pltpu.CompilerParams(dimension_semantics=("parallel",) * len(grid), vmem_limit_bytes=V7X_VMEM_LIMIT))(scalars, *args)


def _pair_add(g, got, core, name):
    _, rows, cols = got.shape
    tr = _row_tile(rows, 16, 256)

    def body(c_ref, a_ref, b_ref, o_ref):
        o_ref[...] = (a_ref[...].astype(F32) + b_ref[...].astype(F32)).astype(o_ref.dtype)

    blk = pl.BlockSpec((4, tr, cols), lambda i, c_ref: (0, i, 0))
    mine = pl.BlockSpec((4, None, tr, cols), lambda i, c_ref: (0, c_ref[0], i, 0))
    return _prefetch_call(body, name, core, (rows // tr,), [mine, blk], blk, _sds(got.shape, got.dtype),
                          (g.reshape(4, 2, rows, cols), got))


def _reduce_chips(chip_sums, recv, slots, name):
    _, rows, cols = recv.shape
    tr = _row_tile(rows, 16, 128)

    def body(s_ref, own_ref, r1_ref, r2_ref, r3_ref, o_ref):
        o_ref[...] = ((own_ref[...].astype(F32) + r1_ref[...].astype(F32)) + r2_ref[...].astype(F32)) + r3_ref[...].astype(F32)

    pick = lambda k: pl.BlockSpec((None, tr, cols), lambda i, s_ref: (s_ref[k], i, 0))
    return _prefetch_call(body, name, slots, (rows // tr,), [pick(0), pick(1), pick(2), pick(3)],
                          pl.BlockSpec((tr, cols), lambda i, s_ref: (i, 0)), _sds((rows, cols)), (chip_sums, recv, recv, recv))


def _row_tile(rows, mult, cap):
    best = mult
    for t in range(mult, cap + 1, mult):
        if rows % t == 0:
            best = t
    assert rows % best == 0, (rows, mult)
    return best


def _reduce_sources(r, name):
    nsrc, rows, cols = r.shape
    tr = _row_tile(rows, 16 if r.dtype.itemsize == 2 else 8, 128)

    def body(r_ref, o_ref):
        acc = r_ref[0].astype(F32)
        for j in range(1, nsrc):
            acc = acc + r_ref[j].astype(F32)
        o_ref[...] = acc

    return _call(body, name, (rows // tr,), [pl.BlockSpec((nsrc, tr, cols), lambda i: (0, i, 0))],
                 pl.BlockSpec((tr, cols), lambda i: (i, 0)), _sds((rows, cols)), sem=("parallel",))(r)


def _adamw(w, g, m, v, name):
    rows, cols = w.shape
    tr = _row_tile(rows, 8, 128)
    c1 = 1.0 - ADAM_B1 ** ADAM_STEP
    c2 = 1.0 - ADAM_B2 ** ADAM_STEP

    def body(w_ref, g_ref, m_ref, v_ref, d_ref, mo_ref, vo_ref):
        g_ = g_ref[...]
        m_ = ADAM_B1 * m_ref[...] + (1.0 - ADAM_B1) * g_
        v_ = ADAM_B2 * v_ref[...] + (1.0 - ADAM_B2) * (g_ * g_)
        d_ref[...] = -ADAM_LR * ((m_ / c1) / (jnp.sqrt(v_ / c2) + ADAM_EPS) + ADAM_WD * w_ref[...])
        mo_ref[...] = m_
        vo_ref[...] = v_

    blk = pl.BlockSpec((tr, cols), lambda i: (i, 0))
    return _call(body, name, (rows // tr,), [blk] * 4, [blk] * 3, [_sds((rows, cols))] * 3, sem=("parallel",))(w, g, m, v)


SLAB_PART = 1024


def _pack(arrs):
    parts = []
    for a in arrs:
        flat = a.reshape(-1)
        pad = (-flat.shape[0]) % SLAB_PART
        if pad:
            flat = jnp.concatenate([flat, jnp.zeros((pad,), flat.dtype)])
        parts.append(flat.reshape(-1, 128))
    return jnp.concatenate(parts, axis=0)


def _unpack(slab, shapes):
    outs, row = [], 0
    for s in shapes:
        n = 1
        for d in s:
            n *= d
        rows = (n + SLAB_PART - 1) // SLAB_PART * (SLAB_PART // 128)
        outs.append(slab[row:row + rows].reshape(-1)[:n].reshape(s))
        row += rows
    return outs


def _unshard_last(g, lead):
    nl = len(lead)
    return jnp.transpose(g, tuple(range(1, nl + 1)) + (0, nl + 1)).reshape(tuple(lead) + (-1,))


def kernel(x, c, ctx, c_ctx, ada_w, ada_b, norm_g, w_in, hg_lb, hg_norm_g, rw_mu, rw_w0, rw_w2, rw_a0, rw_a2, rw_kk, rw_ka, rw_rk, rw_gn_g, rw_gn_b, w_hg_out, w_rw_out, w_out, final_g, loss_target, m_c_ctx, m_ada_w, m_ada_b, m_norm_g, m_w_in, m_hg_lb, m_hg_norm_g, m_rw_mu, m_rw_w0, m_rw_w2, m_rw_a0, m_rw_a2, m_rw_kk, m_rw_ka, m_rw_rk, m_rw_gn_g, m_rw_gn_b, m_w_hg_out, m_w_rw_out, m_w_out, m_final_g, v_c_ctx, v_ada_w, v_ada_b, v_norm_g, v_w_in, v_hg_lb, v_hg_norm_g, v_rw_mu, v_rw_w0, v_rw_w2, v_rw_a0, v_rw_a2, v_rw_kk, v_rw_ka, v_rw_rk, v_rw_gn_g, v_rw_gn_b, v_w_hg_out, v_w_rw_out, v_w_out, v_final_g):
    dm = _dims()
    me = 4 * lax.axis_index("x") + 2 * lax.axis_index("y") + lax.axis_index("c")

    big_shards = [w_in[0].T.astype(BF16), w_out[0].astype(BF16), w_hg_out[0].T.astype(BF16), w_rw_out[0].T.astype(BF16)]
    sharded_small = [hg_lb, rw_mu[0], rw_w0[0], rw_w2[0], rw_a0[0], rw_a2[0]]
    c_rows = jnp.concatenate([c, jnp.zeros((7, D), F32)], axis=0)
    gathered = _gather2(big_shards + [_pack(sharded_small), c_rows], "gather_weights")
    win_t, wout, whg_t, wrw_t = [g.reshape(-1, g.shape[2]) for g in gathered[:4]]
    per_dev = jax.vmap(lambda s: _unpack(s, [a.shape for a in sharded_small]))(gathered[4])
    hg_lb_f, mu_f, w0_f, w2_f, a0_f, a2_f = [_unshard_last(p, p.shape[1:-1]) for p in per_dev]

    ncol = ada_w.shape[2]
    cc16 = jnp.concatenate([gathered[5][:, 0], c_ctx[None], jnp.zeros((7, D), F32)], axis=0)
    mod_all = _gather2([_ada_fwd(cc16, ada_w[0])], "gather_mods")[0]
    mod_x = lax.dynamic_index_in_dim(mod_all, me, axis=1, keepdims=False).reshape(1, -1) + ada_b
    mod_c = mod_all[:, NDEV].reshape(1, -1) + ada_b
    mods = jnp.concatenate([mod_x, mod_c, jnp.zeros((6, 3 * D), F32)], axis=0)

    w = dict(win_t=win_t, wout=wout, whg_t=whg_t, wrw_t=wrw_t, norm_g=norm_g,
             lb0=hg_lb_f[:, 0], lb1=hg_lb_f[:, 1], hg_g=hg_norm_g, mu=mu_f, w0=w0_f, w2=w2_f, a0=a0_f, a2=a2_f,
             kk=rw_kk, ka=rw_ka, rk=rw_rk, gng=rw_gn_g, gnb=rw_gn_b, final_g=final_g[None])
    def start_exchange(grads_bf16, tag):
        names = list(grads_bf16)
        blocks = [grads_bf16[n].reshape(NDEV, -1, grads_bf16[n].shape[1]) for n in names]
        got = _pair_exchange(blocks, "pair_grads_" + tag)
        core = lax.axis_index("c").astype(jnp.int32).reshape(1)
        chip_sums = [_pair_add(g, r, core, "pair_add_" + n) for g, r, n in zip(blocks, got, names)]
        send_sems, recv_sems, held, lands, token = _chip_exchange_start(chip_sums, "chip_grads_start_" + tag)
        return token, (names, send_sems, recv_sems, held, lands)

    loss_dev, grad_x, dmods, in_flight, small = _local_step(x[0], ctx[0], loss_target[0], mods, w, start_exchange)
    loss = lax.psum(loss_dev[0, 0], AXES)

    dmods_all = _gather2([dmods], "gather_dmods")[0]
    my_cols = lambda a: lax.dynamic_slice_in_dim(a, me * ncol, ncol, axis=1)
    g_ada, dcc16, d_ada_b = _ada_bwd(cc16, ada_w[0], my_cols(dmods_all[:, 0]), my_cols(dmods_all[:, 1]), dmods)
    small = dict(small, c_ctx=dcc16[NDEV:NDEV + 1], ada_b=d_ada_b)

    small_names = ["c_ctx", "ada_b", "norm_g", "lb0", "lb1", "hg_g", "mu", "w0", "w2", "a0", "a2", "kk", "ka", "rk", "gng", "gnb", "final_g"]
    small_slab = _pack([small[n] for n in small_names])
    small_all = _gather2([small_slab], "gather_small_grads")[0]
    small_sum = _reduce_sources(small_all, "reduce_small")
    sg = dict(zip(small_names, _unpack(small_sum, [small[n].shape for n in small_names])))

    my_chip = (2 * lax.axis_index("x") + lax.axis_index("y")).astype(jnp.int32)
    others = jnp.arange(3, dtype=jnp.int32)
    slots = jnp.concatenate([my_chip.reshape(1), others + (others >= my_chip).astype(jnp.int32)])
    reduced = {}
    for tag, (names, *copies) in zip(("early", "late"), in_flight):
        chip_sums, recv = _chip_exchange_wait(*copies, small_sum, "chip_grads_wait_" + tag)
        reduced.update({n: _reduce_chips(cs, r, slots, "reduce_" + n) for n, cs, r in zip(names, chip_sums, recv)})
    g_win_t, g_wout, g_whg_t, g_wrw_t = [reduced[n] for n in ("win_t", "wout", "whg_t", "wrw_t")]

    def my_shard(full):
        n = full.shape[-1] // NDEV
        return lax.dynamic_slice_in_dim(full, me * n, n, axis=full.ndim - 1)

    grads = dict(
        c_ctx=sg["c_ctx"][0], ada_w=g_ada[None], ada_b=sg["ada_b"], norm_g=sg["norm_g"], w_in=g_win_t.T[None],
        hg_lb=my_shard(jnp.stack([sg["lb0"], sg["lb1"]], axis=1)), hg_norm_g=sg["hg_g"], rw_mu=my_shard(sg["mu"])[None],
        rw_w0=my_shard(sg["w0"])[None], rw_w2=my_shard(sg["w2"])[None], rw_a0=my_shard(sg["a0"])[None], rw_a2=my_shard(sg["a2"])[None],
        rw_kk=sg["kk"], rw_ka=sg["ka"], rw_rk=sg["rk"], rw_gn_g=sg["gng"], rw_gn_b=sg["gnb"],
        w_hg_out=g_whg_t.T[None], w_rw_out=g_wrw_t.T[None], w_out=g_wout[None], final_g=sg["final_g"][0])
    weights = dict(c_ctx=c_ctx, ada_w=ada_w, ada_b=ada_b, norm_g=norm_g, w_in=w_in, hg_lb=hg_lb, hg_norm_g=hg_norm_g, rw_mu=rw_mu,
                   rw_w0=rw_w0, rw_w2=rw_w2, rw_a0=rw_a0, rw_a2=rw_a2, rw_kk=rw_kk, rw_ka=rw_ka, rw_rk=rw_rk, rw_gn_g=rw_gn_g,
                   rw_gn_b=rw_gn_b, w_hg_out=w_hg_out, w_rw_out=w_rw_out, w_out=w_out, final_g=final_g)
    m_in = dict(zip(weights, (m_c_ctx, m_ada_w, m_ada_b, m_norm_g, m_w_in, m_hg_lb, m_hg_norm_g, m_rw_mu, m_rw_w0, m_rw_w2, m_rw_a0,
                              m_rw_a2, m_rw_kk, m_rw_ka, m_rw_rk, m_rw_gn_g, m_rw_gn_b, m_w_hg_out, m_w_rw_out, m_w_out, m_final_g)))
    v_in = dict(zip(weights, (v_c_ctx, v_ada_w, v_ada_b, v_norm_g, v_w_in, v_hg_lb, v_hg_norm_g, v_rw_mu, v_rw_w0, v_rw_w2, v_rw_a0,
                              v_rw_a2, v_rw_kk, v_rw_ka, v_rw_rk, v_rw_gn_g, v_rw_gn_b, v_w_hg_out, v_w_rw_out, v_w_out, v_final_g)))

    big_w = ["ada_w", "w_in", "w_hg_out", "w_rw_out", "w_out"]
    delta, new_m, new_v = {}, {}, {}
    for n in big_w:
        shp = weights[n].shape
        two = lambda a: a.reshape(shp[-2], shp[-1])
        d_, m_, v_ = _adamw(two(weights[n]), two(grads[n]), two(m_in[n]), two(v_in[n]), "adamw_" + n)
        delta[n], new_m[n], new_v[n] = d_.reshape(shp), m_.reshape(shp), v_.reshape(shp)
    rest = [n for n in weights if n not in big_w]
    shapes = [weights[n].shape for n in rest]
    d_s, m_s, v_s = _adamw(_pack([weights[n] for n in rest]), _pack([grads[n] for n in rest]), _pack([m_in[n] for n in rest]),
                           _pack([v_in[n] for n in rest]), "adamw_small")
    for n, d_, m_, v_ in zip(rest, _unpack(d_s, shapes), _unpack(m_s, shapes), _unpack(v_s, shapes)):
        delta[n], new_m[n], new_v[n] = d_, m_, v_

    order = list(weights)
    return (loss, grad_x[None], *[grads[n] for n in order], *[delta[n] for n in order],
            *[new_m[n] for n in order], *[new_v[n] for n in order])
```

```python
import functools

import jax
import jax.numpy as jnp
from jax import lax
from jax.experimental import pallas as pl
from jax.experimental.pallas import tpu as pltpu

D = 2048
T = 2048
TC = 256
GW = 64
C = 64
HGW = 1024
RWW = 1024
LORA = 64
NDEV = 8
TM = 256
TMV = 128
NORM_EPS = 1e-6
RW_GN_EPS = 64e-5
ADAM_LR, ADAM_B1, ADAM_B2, ADAM_EPS, ADAM_WD, ADAM_STEP = 0.001, 0.9, 0.999, 1e-08, 0.01, 10
WB = 256
V7X_VMEM_LIMIT = 56 * 1024 * 1024

F32 = jnp.float32
BF16 = jnp.bfloat16
AXES = ("x", "y", "c")


def _dims():
    tt = T + TC
    sh = 3 * RWW + 4 * LORA
    hgc = 5 * HGW
    return dict(TT=tt, NCC=TC // C, NCH=tt // C, HG_H=HGW // 128, RW_P=RWW // 128, SH=sh, HGC=hgc,
                NCOLS=hgc + sh + RWW + 2 * D, NCT=TC // TM, NCTV=TC // TMV)


def _mm(a, b):
    return _dot3(a, b, "nn")


def _mm_nt(a, b):
    return _dot3(a, b, "nt")


def _mm_tn(a, b):
    return _dot3(a, b, "tn")


def _split(x):
    hi = x.astype(BF16)
    return hi, (x - hi.astype(F32)).astype(BF16)


_FORMS = {"nn": (((1,), (0,)), ((), ())), "nt": (((1,), (1,)), ((), ())), "tn": (((0,), (0,)), ((), ()))}


def _dot3_raw(a, b, form):
    ah, al = _split(a)
    bh, bl = _split(b)
    d = lambda x, y: lax.dot_general(x, y, _FORMS[form], preferred_element_type=F32)
    return d(ah, bh) + (d(ah, bl) + d(al, bh))


@functools.partial(jax.custom_vjp, nondiff_argnums=(2,))
def _dot3(a, b, form):
    return _dot3_raw(a, b, form)


def _dot3_fwd(a, b, form):
    return _dot3_raw(a, b, form), (a, b)


def _dot3_bwd(form, res, g):
    a, b = res
    if form == "nn":
        return _dot3(g, b, "nt"), _dot3(a, g, "tn")
    if form == "nt":
        return _dot3(g, b, "nn"), _dot3(g, a, "tn")
    return _dot3(b, g, "nt"), _dot3(a, g, "nn")


_dot3.defvjp(_dot3_fwd, _dot3_bwd)


def _scan_cumsum(inc, x):
    return _cumsum_vjp(inc.astype(BF16), x)


def _cumsum_raw(inc, x, form):
    h1 = x.astype(BF16)
    r1 = x - h1.astype(F32)
    h2 = r1.astype(BF16)
    h3 = (r1 - h2.astype(F32)).astype(BF16)
    d = lambda y: lax.dot_general(inc, y, _FORMS[form], preferred_element_type=F32)
    return d(h1) + (d(h2) + d(h3))


@jax.custom_vjp
def _cumsum_vjp(inc, x):
    return _cumsum_raw(inc, x, "nn")


_cumsum_vjp.defvjp(lambda inc, x: (_cumsum_raw(inc, x, "nn"), inc),
                   lambda inc, g: (jnp.zeros_like(inc), _cumsum_raw(inc, g, "tn")))


def _silu(x):
    return x * jax.nn.sigmoid(x)


def _softplus(x):
    return jnp.maximum(x, 0.0) + jnp.log(1.0 + jnp.exp(-jnp.abs(x)))


def _iota2(shape, dim):
    return lax.broadcasted_iota(jnp.int32, shape, dim)


def _pair_mask():
    return (_iota2((128, 128), 0) < 64) == (_iota2((128, 128), 1) < 64)


def _seg64_sum(x):
    e = _pair_mask().astype(F32)
    parts = [_mm(x[:, g * 128:(g + 1) * 128], e) for g in range(x.shape[1] // 128)]
    return parts[0] if len(parts) == 1 else jnp.concatenate(parts, axis=1)


def _seg128_mean(x):
    parts = [jnp.broadcast_to(jnp.mean(x[:, g * 128:(g + 1) * 128], axis=1, keepdims=True), (x.shape[0], 128))
             for g in range(x.shape[1] // 128)]
    return parts[0] if len(parts) == 1 else jnp.concatenate(parts, axis=1)


def _scan_masks(d):
    lag = (_iota2((C, C), 0) - _iota2((C, C), 1)) * (1 - 2 * d)
    return lag >= 0, lag > 0


def _tri_inverse(a):
    eye = (_iota2((C, C), 0) == _iota2((C, C), 1)).astype(F32)
    p = [-x for x in a]
    tm = [eye + x for x in p]
    n = 2
    while n < C:
        p = [_mm(x, x) for x in p]
        tm = [t + _mm(t, x) for t, x in zip(tm, p)]
        n *= 2
    return tm


@jax.custom_vjp
def _tri_solve(tm, a, rhs):
    return [_mm(t, r) for t, r in zip(tm, rhs)]


def _tri_solve_fwd(tm, a, rhs):
    u = [_mm(t, r) for t, r in zip(tm, rhs)]
    return u, (tm, u)


def _tri_solve_bwd(res, g):
    tm, u = res
    d_rhs = [_mm_tn(t, x) for t, x in zip(tm, g)]
    return [jnp.zeros_like(t) for t in tm], [-_mm_nt(d, x) for d, x in zip(d_rhs, u)], d_rhs


_tri_solve.defvjp(_tri_solve_fwd, _tri_solve_bwd)


def _rw_chunks(chains, tms=None):
    s0, r, lw, kap, b, v, kd, ds = zip(*chains)
    ids = range(len(chains))
    inc, strict = zip(*[_scan_masks(d) for d in ds])
    lane = _iota2((1, 128), 1)
    hm = ((lane < 64).astype(F32), (lane >= 64).astype(F32))
    eye = (_iota2((C, C), 0) == _iota2((C, C), 1)).astype(F32)
    lc = [_scan_cumsum(inc[i], lw[i]) for i in ids]
    ltot = [jnp.sum(lw[i], axis=0, keepdims=True) for i in ids]
    rt = [r[i] * jnp.exp(lc[i]) for i in ids]
    kt = [kap[i] * jnp.exp(lc[i] - lw[i]) for i in ids]
    einv = [jnp.exp(-lc[i]) for i in ids]
    kh = [kd[i] * einv[i] for i in ids]
    bh = [b[i] * einv[i] for i in ids]
    eend = [jnp.exp(ltot[i] - lc[i]) for i in ids]
    kbar = [kd[i] * eend[i] for i in ids]
    bbar = [b[i] * eend[i] for i in ids]
    lhs = [jnp.concatenate([kt[i] * hm[0], kt[i] * hm[1], rt[i] * hm[0], rt[i] * hm[1]], axis=0) for i in ids]
    gk = [_mm_nt(lhs[i], kh[i]) for i in ids]
    gb = [_mm_nt(lhs[i], bh[i]) for i in ids]
    ks = [_mm_nt(kt[i], s0[i]) for i in ids]
    ys = [_mm_nt(rt[i], s0[i]) for i in ids]
    heads = [(i, h) for i in ids for h in range(2)]
    akk = {c: jnp.where(strict[c[0]], gk[c[0]][c[1] * C:(c[1] + 1) * C], 0.0) for c in heads}
    akb = {c: jnp.where(strict[c[0]], gb[c[0]][c[1] * C:(c[1] + 1) * C], 0.0) for c in heads}
    ark = {c: jnp.where(inc[c[0]], gk[c[0]][(2 + c[1]) * C:(3 + c[1]) * C], 0.0) for c in heads}
    arb = {c: jnp.where(inc[c[0]], gb[c[0]][(2 + c[1]) * C:(3 + c[1]) * C], 0.0) for c in heads}
    vh = {c: v[c[0]] * hm[c[1]] for c in heads}
    av = {c: _mm(akk[c], vh[c]) for c in heads}
    rhs = {c: ks[c[0]] * hm[c[1]] + av[c] for c in heads}
    akb_list = [akb[c] for c in heads]
    tm = _tri_inverse(akb_list) if tms is None else [tms[c] for c in heads]
    uh = dict(zip(heads, _tri_solve(tm, akb_list, [rhs[c] for c in heads])))
    yv = {c: _mm(ark[c], vh[c]) for c in heads}
    yu = {c: _mm(arb[c], uh[c]) for c in heads}
    u = [uh[i, 0] + uh[i, 1] for i in ids]
    y = [ys[i] + (yv[i, 0] - yu[i, 0]) + (yv[i, 1] - yu[i, 1]) for i in ids]
    upd = [_mm_tn(jnp.concatenate([v[i], -u[i]], axis=0), jnp.concatenate([kbar[i], bbar[i]], axis=0)) for i in ids]
    s1 = [s0[i] * jnp.exp(ltot[i]) + jnp.where(_pair_mask(), upd[i], 0.0) for i in ids]
    if tms is None:
        return [(y[i], s1[i], tm[2 * i], tm[2 * i + 1]) for i in ids]
    return [(y[i], s1[i]) for i in ids]


HG_SUB = 16


def _hg_chunks(chains):
    s0, qr, iv, f, lb, ds = zip(*chains)
    ids = range(len(chains))
    inc = [_scan_masks(d)[0] for d in ds]
    q = [_silu(qr[i]) for i in ids]
    fg = [lb[i] + (1.0 - lb[i]) * jax.nn.sigmoid(f[i]) for i in ids]
    k = [1.0 - fg[i] for i in ids]
    g = [jnp.log(fg[i]) for i in ids]
    bc = [_scan_cumsum(inc[i], g[i]) for i in ids]
    btot = [jnp.sum(g[i], axis=0, keepdims=True) for i in ids]
    o_inter = [_mm_nt(q[i] * jnp.exp(bc[i]), s0[i]) for i in ids]
    rowi = _iota2((C, 1), 0)
    outs = [[] for _ in ids]
    for blk in range(C // HG_SUB):
        lo, hi = blk * HG_SUB, (blk + 1) * HG_SUB
        first = [lo if ds[i] == 0 else hi - 1 for i in ids]
        ref = [jnp.sum(jnp.where(rowi == first[i], bc[i] - g[i], 0.0), axis=0, keepdims=True) for i in ids]
        qi = [q[i][lo:hi] * jnp.exp(bc[i][lo:hi] - ref[i]) for i in ids]
        src = [(rowi < hi) if ds[i] == 0 else (rowi >= lo) for i in ids]
        ke = [jnp.where(src[i], k[i] * jnp.exp(jnp.where(src[i], ref[i] - bc[i], 0.0)), 0.0) for i in ids]
        a = [jnp.where(inc[i][lo:hi], _mm_nt(qi[i], ke[i]), 0.0) for i in ids]
        part = [_mm(a[i], iv[i]) for i in ids]
        for i in ids:
            outs[i].append(part[i])
    o = [o_inter[i] + jnp.concatenate(outs[i], axis=0) for i in ids]
    upd = [_mm_tn(iv[i], k[i] * jnp.exp(btot[i] - bc[i])) for i in ids]
    s1 = [s0[i] * jnp.exp(btot[i]) + upd[i] for i in ids]
    return [(o[i], s1[i]) for i in ids]


def _lockstep(fn, ds, nargs):
    def flat_fn(*flat):
        return tuple(fn([tuple(flat[i * nargs:(i + 1) * nargs]) + (d,) for i, d in enumerate(ds)]))

    return flat_fn


def _h_math(x, ng, scale, shift):
    return x * lax.rsqrt(jnp.mean(x * x, axis=-1, keepdims=True) + NORM_EPS) * ng * (1.0 + scale) + shift


def _hg_post_math(o0, o1, z, g):
    o = o0 + o1
    on = o * lax.rsqrt(_seg128_mean(o * o) + NORM_EPS)
    return on * g * _silu(z)


def _rw_prep_math(k, pw0, pw1, pa0, pa1, w0, a0, kkp, kap_p):
    kk = k * kkp
    kap = kk * lax.rsqrt(_seg64_sum(kk * kk) + 1e-12)
    outs = []
    for d, (pw, pa) in enumerate(((pw0, pa0), (pw1, pa1))):
        w_log = -_softplus(-(w0[d:d + 1] + pw)) - 0.5
        lw = -jnp.exp(w_log)
        a = jax.nn.sigmoid(a0[d:d + 1] + pa)
        kd = k * (1.0 + (a - 1.0) * kap_p)
        outs += [lw, kap * a, kd]
    return (kap, *outs)


def _rw_post_math(y0, y1, kd0, kd1, r, v, z, rk, gng, gnb):
    ys, ksum = y0 + y1, kd0 + kd1
    mean = _seg64_sum(ys) * (1.0 / 64.0)
    cen = ys - mean
    var = _seg64_sum(cen * cen) * (1.0 / 64.0)
    yn = cen * lax.rsqrt(var + RW_GN_EPS) * gng + gnb
    bonus = _seg64_sum(r * ksum * rk) * v
    return (yn + bonus) * _silu(z)


def _head_math(x, mo, gate, fg, tgt):
    xo = x + gate * mo
    y = xo * lax.rsqrt(jnp.mean(xo * xo, axis=-1, keepdims=True) + NORM_EPS) * fg
    err = y - tgt
    return 0.5 * jnp.sum(jnp.mean(err * err, axis=-1, keepdims=True), axis=0, keepdims=True)


def _gate_math(ghg, grw, a, b):
    return jax.nn.sigmoid(ghg) * a + jax.nn.sigmoid(grw) * b


def _call(body, name, grid, in_specs, out_specs, out_shape, scratch=(), sem=None, aliases=None):
    return pl.pallas_call(
        body, name=name, grid=grid, in_specs=in_specs, out_specs=out_specs, out_shape=out_shape,
        scratch_shapes=list(scratch), input_output_aliases=aliases or {},
        compiler_params=pltpu.CompilerParams(dimension_semantics=sem, vmem_limit_bytes=V7X_VMEM_LIMIT))


def _full(shape):
    nd = len(shape)
    return pl.BlockSpec(shape, lambda *_: (0,) * nd)


def _sds(shape, dtype=F32):
    return jax.ShapeDtypeStruct(shape, dtype)


def _proj_nt(h, wt, wt_blk0, nblk, row0, nrows, name):
    def body(h_ref, w_ref, o_ref):
        o_ref[...] = lax.dot_general(h_ref[row0:row0 + nrows, :], w_ref[...], (((1,), (1,)), ((), ())), preferred_element_type=F32)

    return _call(body, name, (nblk,), [_full(h.shape), pl.BlockSpec((WB, wt.shape[1]), lambda j: (wt_blk0 + j, 0))],
                 pl.BlockSpec((nrows, WB), lambda j: (0, j)), _sds((nrows, nblk * WB)), sem=("parallel",))(h, wt)


def _dw_tn(a, b, row0, out_rows, out_blk0, name, prev=None):
    m, n = a.shape
    k2 = b.shape[1]
    nblk = n // WB

    def body(*refs):
        a_ref, b_ref, o_ref = refs[0], refs[1], refs[-1]
        o_ref[...] = lax.dot_general(a_ref[...], b_ref[row0:row0 + m, :], (((0,), (0,)), ((), ())),
                                     preferred_element_type=F32).astype(BF16)

    in_specs = [pl.BlockSpec((m, WB), lambda j: (0, j)), _full(b.shape)]
    args = [a, b]
    aliases = None
    if prev is not None:
        in_specs.append(pl.BlockSpec(memory_space=pl.ANY))
        args.append(prev)
        aliases = {2: 0}
    return _call(body, name, (nblk,), in_specs, pl.BlockSpec((WB, k2), lambda j: (out_blk0 + j, 0)), _sds((out_rows, k2), BF16),
                 sem=("arbitrary",), aliases=aliases)(*args)


def _mm_nn(a, w, w_blk0, name, after=None):
    m, kc = a.shape
    n = w.shape[1]
    kb = max(k for k in (WB, 2 * WB, 4 * WB) if kc % k == 0 and (w_blk0 * WB) % k == 0)
    w_blk0 = w_blk0 * WB // kb
    tmb = 3 * TM if m % (3 * TM) == 0 else (2 * TM if m % (2 * TM) == 0 else TM)

    def body(a_ref, w_ref, *rest):
        o_ref = rest[-1]

        @pl.when(pl.program_id(1) == 0)
        def _():
            o_ref[...] = jnp.zeros_like(o_ref)

        o_ref[...] += jnp.dot(a_ref[...], w_ref[...], preferred_element_type=F32)

    extra = [] if after is None else [pl.BlockSpec(memory_space=pl.ANY)]
    return _call(body, name, (m // tmb, kc // kb),
                 [pl.BlockSpec((tmb, kb), lambda i, k: (i, k)), pl.BlockSpec((kb, n), lambda i, k: (w_blk0 + k, 0))] + extra,
                 pl.BlockSpec((tmb, n), lambda i, k: (i, 0)), _sds((m, n)), sem=("parallel", "arbitrary"))(
        a, w, *([] if after is None else [after]))


def _ada_fwd(cc16, ada_shard):
    ncol = ada_shard.shape[1]

    def body(c_ref, w_ref, o_ref):
        o_ref[...] = _mm(_silu(c_ref[...]), w_ref[...])

    return _call(body, "ada_fwd", (1,), [_full((16, D)), _full((D, ncol))], _full((16, ncol)), _sds((16, ncol)))(cc16, ada_shard)


def _ada_bwd(cc16, ada_shard, dm_x, dm_c, dmods):
    ncol = ada_shard.shape[1]

    def body(c_ref, w_ref, dx_ref, dc_ref, dm_ref, dw_ref, dcc_ref, db_ref):
        s, vjp = jax.vjp(_silu, c_ref[...])
        dc_tot = dc_ref[0:1, :]
        for j in range(1, NDEV):
            dc_tot = dc_tot + dc_ref[j:j + 1, :]
        row = _iota2((8, 1), 0)
        dm16 = jnp.concatenate([dx_ref[...], jnp.where(row == 0, dc_tot, 0.0)], axis=0)
        dw_ref[...] = _mm_tn(s, dm16)
        dcc_ref[...] = vjp(_mm_nt(dm16, w_ref[...]))[0]
        db_ref[...] = dm_ref[0:1, :] + dm_ref[1:2, :]

    return _call(body, "ada_bwd", (1,), [_full((16, D)), _full((D, ncol)), _full((8, ncol)), _full((8, ncol)), _full((8, 3 * D))],
                 [_full((D, ncol)), _full((16, D)), _full((1, 3 * D))], [_sds((D, ncol)), _sds((16, D)), _sds((1, 3 * D))])(
        cc16, ada_shard, dm_x, dm_c, dmods)


def _sel_mod(mods_ref, is_ctx, lo):
    return jnp.where(is_ctx, mods_ref[1:2, lo:lo + D], mods_ref[0:1, lo:lo + D])


def _h_fwd(xcat, mods, norm_g):
    dm = _dims()

    def body(x_ref, m_ref, g_ref, o_ref):
        is_ctx = pl.program_id(0) < dm["NCT"]
        o_ref[...] = _h_math(x_ref[...], g_ref[...], _sel_mod(m_ref, is_ctx, D), _sel_mod(m_ref, is_ctx, 0)).astype(BF16)

    return _call(body, "h_fwd", (dm["TT"] // TM,),
                 [pl.BlockSpec((TM, D), lambda i: (i, 0)), _full((8, 3 * D)), _full((1, D))],
                 pl.BlockSpec((TM, D), lambda i: (i, 0)), _sds((dm["TT"], D), BF16), sem=("parallel",))(xcat, mods, norm_g)


def _chunk_of(d, s):
    dm = _dims()
    ncc, nch = dm["NCC"], dm["NCH"]
    return s if d == 0 else jnp.where(s < ncc, ncc - 1 - s, nch - 1 - (s - ncc))


HG_GROUP = 4
RW_GROUP = 4


def _dir_specs(make):
    return [make(d) for d in range(2)]


def _lanes(j):
    return slice(j * 128, (j + 1) * 128)


def _hg_scan_fwd(p_hg, lb3):
    dm = _dims()
    nh, nch, tt = dm["HG_H"], dm["NCH"], dm["TT"]
    pp = min(HG_GROUP, nh)
    lw_ = 128 * pp

    def body(q0, q1, i0, i1, f0, f1, lb0, lb1, o0, o1, ck_ref, s_ref):
        @pl.when(pl.program_id(1) == 0)
        def _():
            s_ref[...] = jnp.zeros_like(s_ref)

        keys, chains = [], []
        for d, refs in enumerate(((q0, i0, f0, lb0), (q1, i1, f1, lb1))):
            vals = [r[...] for r in refs]
            for j in range(pp):
                keys.append((d, j))
                chains.append((s_ref[d, j], *[a[:, _lanes(j)] for a in vals], d))
        for (d, j), chain, (out, s1) in zip(keys, chains, _hg_chunks(chains)):
            ck_ref[d, j] = chain[0]
            (o0, o1)[d][:, _lanes(j)] = out
            s_ref[d, j] = s1

    blk = lambda off: _dir_specs(lambda d: pl.BlockSpec((C, lw_), lambda h, s: (_chunk_of(d, s), off * nh // pp + h)))
    return _call(body, "hg_scan_fwd", (nh // pp, nch),
                 blk(0) + blk(1) + _dir_specs(lambda d: pl.BlockSpec((C, lw_), lambda h, s: (_chunk_of(d, s), (2 + d) * nh // pp + h)))
                 + _dir_specs(lambda d: pl.BlockSpec((None, 1, lw_), lambda h, s: (d, 0, h))),
                 _dir_specs(lambda d: pl.BlockSpec((C, lw_), lambda h, s: (_chunk_of(d, s), h)))
                 + [pl.BlockSpec((2, pp, None, 128, 128), lambda h, s: (0, h, s, 0, 0))],
                 [_sds((tt, HGW)), _sds((tt, HGW)), _sds((2, nh, nch, 128, 128))],
                 scratch=[pltpu.VMEM((2, pp, 128, 128), F32)], sem=("parallel", "arbitrary"))(*([p_hg] * 6), lb3, lb3)


def _hg_scan_bwd(p_hg, lb3, ck, do):
    dm = _dims()
    nh, nch, tt = dm["HG_H"], dm["NCH"], dm["TT"]
    pp = min(HG_GROUP, nh)
    lw_ = 128 * pp

    def body(q0, q1, i0, i1, f0, f1, lb0, lb1, ck_ref, do0, do1, dq0, dq1, di0, di1, df0, df1, dlb_ref, ds_ref):
        @pl.when(pl.program_id(1) == 0)
        def _():
            ds_ref[...] = jnp.zeros_like(ds_ref)
            dlb_ref[...] = jnp.zeros_like(dlb_ref)

        keys, flat, cts = [], [], []
        for d, refs in enumerate(((q0, i0, f0, lb0, do0), (q1, i1, f1, lb1, do1))):
            *vals, do_ = [r[...] for r in refs]
            for j in range(pp):
                keys.append((d, j))
                flat += [ck_ref[d, j], *[a[:, _lanes(j)] for a in vals]]
                cts.append((do_[:, _lanes(j)], ds_ref[d, j]))
        _, vjp = jax.vjp(_lockstep(_hg_chunks, [d for d, _ in keys], 5), *flat)
        grads = vjp(tuple(cts))
        for n, (d, j) in enumerate(keys):
            ds0, dq_, di_, df_, dlb = grads[5 * n:5 * n + 5]
            (dq0, dq1)[d][:, _lanes(j)] = dq_
            (di0, di1)[d][:, _lanes(j)] = di_
            (df0, df1)[d][:, _lanes(j)] = df_
            dlb_ref[d, :, _lanes(j)] += dlb
            ds_ref[d, j] = ds0

    ch = lambda d, s: _chunk_of(d, nch - 1 - s)
    blk = lambda off: _dir_specs(lambda d: pl.BlockSpec((C, lw_), lambda h, s: (ch(d, s), off * nh // pp + h)))
    tok = _dir_specs(lambda d: pl.BlockSpec((C, lw_), lambda h, s: (ch(d, s), h)))
    return _call(body, "hg_scan_bwd", (nh // pp, nch),
                 blk(0) + blk(1) + _dir_specs(lambda d: pl.BlockSpec((C, lw_), lambda h, s: (ch(d, s), (2 + d) * nh // pp + h)))
                 + _dir_specs(lambda d: pl.BlockSpec((None, 1, lw_), lambda h, s: (d, 0, h)))
                 + [pl.BlockSpec((2, pp, None, 128, 128), lambda h, s: (0, h, nch - 1 - s, 0, 0))] + tok,
                 tok * 3 + [pl.BlockSpec((2, 1, lw_), lambda h, s: (0, 0, h))],
                 [_sds((tt, HGW))] * 6 + [_sds((2, 1, HGW))],
                 scratch=[pltpu.VMEM((2, pp, 128, 128), F32)], sem=("parallel", "arbitrary"))(*([p_hg] * 6), lb3, lb3, ck, do, do)


def _rw_scan_fwd(sh, kap, lw, b, kd):
    dm = _dims()
    npair, nch, tt = dm["RW_P"], dm["NCH"], dm["TT"]
    pp = min(RW_GROUP, npair)
    lw_ = 128 * pp

    def body(r0, r1, v0, v1, ka0, ka1, lw0, lw1, b0, b1, kd0, kd1, y0, y1, ck_ref, tm_ref, s_ref):
        @pl.when(pl.program_id(1) == 0)
        def _():
            s_ref[...] = jnp.zeros_like(s_ref)

        keys, chains = [], []
        for d, refs in enumerate(((r0, lw0, ka0, b0, v0, kd0), (r1, lw1, ka1, b1, v1, kd1))):
            vals = [r[...] for r in refs]
            for j in range(pp):
                keys.append((d, j))
                chains.append((s_ref[d, j], *[a[:, _lanes(j)] for a in vals], d))
        for (d, j), chain, (out, s1, tm_a, tm_b) in zip(keys, chains, _rw_chunks(chains)):
            ck_ref[d, j] = chain[0]
            tm_ref[d, j, 0] = tm_a
            tm_ref[d, j, 1] = tm_b
            (y0, y1)[d][:, _lanes(j)] = out
            s_ref[d, j] = s1

    two = lambda off: _dir_specs(lambda d: pl.BlockSpec((C, lw_), lambda p, s: (_chunk_of(d, s), off * npair // pp + p)))
    three = _dir_specs(lambda d: pl.BlockSpec((None, C, lw_), lambda p, s: (d, _chunk_of(d, s), p)))
    return _call(body, "rw_scan_fwd", (npair // pp, nch), two(0) + two(2) + two(0) + three * 3,
                 two(0) + [pl.BlockSpec((2, pp, None, 128, 128), lambda p, s: (0, p, s, 0, 0)),
                           pl.BlockSpec((2, pp, None, 2, C, C), lambda p, s: (0, p, s, 0, 0, 0))],
                 [_sds((tt, RWW)), _sds((tt, RWW)), _sds((2, npair, nch, 128, 128)), _sds((2, npair, nch, 2, C, C))],
                 scratch=[pltpu.VMEM((2, pp, 128, 128), F32)], sem=("parallel", "arbitrary"))(
        sh, sh, sh, sh, kap, kap, lw, lw, b, b, kd, kd)


def _rw_scan_bwd(sh, kap, lw, b, kd, ck, tm_ck, dy):
    dm = _dims()
    npair, nch, tt = dm["RW_P"], dm["NCH"], dm["TT"]
    pp = min(RW_GROUP, npair)
    lw_ = 128 * pp

    def body(r0, r1, v0, v1, ka0, ka1, lw0, lw1, b0, b1, kd0, kd1, ck_ref, tm_ref, dy0, dy1,
             dr0, dr1, dv0, dv1, dka0, dka1, dlw0, dlw1, db0, db1, dkd0, dkd1, ds_ref):
        @pl.when(pl.program_id(1) == 0)
        def _():
            ds_ref[...] = jnp.zeros_like(ds_ref)

        keys, flat, cts, tms = [], [], [], {}
        for d, refs in enumerate(((r0, lw0, ka0, b0, v0, kd0, dy0), (r1, lw1, ka1, b1, v1, kd1, dy1))):
            *vals, dy_ = [r[...] for r in refs]
            for j in range(pp):
                tms[len(keys), 0], tms[len(keys), 1] = tm_ref[d, j, 0], tm_ref[d, j, 1]
                keys.append((d, j))
                flat += [ck_ref[d, j], *[a[:, _lanes(j)] for a in vals]]
                cts.append((dy_[:, _lanes(j)], ds_ref[d, j]))
        _, vjp = jax.vjp(_lockstep(functools.partial(_rw_chunks, tms=tms), [d for d, _ in keys], 7), *flat)
        grads = vjp(tuple(cts))
        outs = ((dr0, dlw0, dka0, db0, dv0, dkd0), (dr1, dlw1, dka1, db1, dv1, dkd1))
        for n, (d, j) in enumerate(keys):
            ds0, *g_in = grads[7 * n:7 * n + 7]
            for o_ref, g in zip(outs[d], g_in):
                o_ref[:, _lanes(j)] = g
            ds_ref[d, j] = ds0

    ch = lambda d, s: _chunk_of(d, nch - 1 - s)
    two = lambda off: _dir_specs(lambda d: pl.BlockSpec((C, lw_), lambda p, s: (ch(d, s), off * npair // pp + p)))
    three = _dir_specs(lambda d: pl.BlockSpec((None, C, lw_), lambda p, s: (d, ch(d, s), p)))
    return _call(body, "rw_scan_bwd", (npair // pp, nch),
                 two(0) + two(2) + two(0) + three * 3
                 + [pl.BlockSpec((2, pp, None, 128, 128), lambda p, s: (0, p, nch - 1 - s, 0, 0)),
                    pl.BlockSpec((2, pp, None, 2, C, C), lambda p, s: (0, p, nch - 1 - s, 0, 0, 0))] + two(0),
                 two(0) * 6, [_sds((tt, RWW))] * 12,
                 scratch=[pltpu.VMEM((2, pp, 128, 128), F32)], sem=("parallel", "arbitrary"))(
        sh, sh, sh, sh, kap, kap, lw, lw, b, b, kd, kd, ck, tm_ck, dy, dy)


def _tile_pos(i):
    dm = _dims()
    is_ctx = i < dm["NCT"]
    rows = lax.broadcasted_iota(jnp.int32, (TM, 1), 0)
    tl = rows + (i - dm["NCT"]) * TM
    width = jnp.where(is_ctx, TC, GW)
    assert TC & (TC - 1) == 0 and GW & (GW - 1) == 0 and TM % GW == 0
    colp = rows & (width - 1)
    return is_ctx, rows, tl, colp, width


def _shift_lr(cur, i):
    _, _, _, colp, width = _tile_pos(i)
    left = jnp.where(colp == 0, 0.0, pltpu.roll(cur, 1, 0))
    right = jnp.where(colp == width - 1, 0.0, pltpu.roll(cur, TM - 1, 0))
    return left, right


def _shift_ud(cur, prv, nxt, i):
    is_ctx, rows, tl, _, _ = _tile_pos(i)
    if TM > GW:
        up = jnp.where(rows >= GW, pltpu.roll(cur, GW, 0), pltpu.roll(prv, GW, 0))
        down = jnp.where(rows < TM - GW, pltpu.roll(cur, TM - GW, 0), pltpu.roll(nxt, TM - GW, 0))
    else:
        up, down = prv, nxt
    up = jnp.where(jnp.logical_or(is_ctx, tl < GW), 0.0, up)
    down = jnp.where(jnp.logical_or(is_ctx, tl >= T - GW), 0.0, down)
    return up, down


def _shift_specs():
    dm = _dims()
    nt = dm["TT"] // TM
    cw = max(w for w in range(128, 1664 + 1, 128) if dm["SH"] % w == 0)
    cur = pl.BlockSpec((TM, cw), lambda j, i: (i, j))
    prv = pl.BlockSpec((TM, cw), lambda j, i: (jnp.maximum(i - 1, 0), j))
    nxt = pl.BlockSpec((TM, cw), lambda j, i: (jnp.minimum(i + 1, nt - 1), j))
    mu = pl.BlockSpec((4, cw), lambda j, i: (0, j))
    return nt, cw, cur, prv, nxt, mu


def _shift_fwd(p_rw, mu):
    dm = _dims()
    nt, cw, cur, prv, nxt, mus = _shift_specs()

    def body(c_ref, p_ref, n_ref, mu_ref, o_ref):
        i = pl.program_id(1)
        p, m = c_ref[...], mu_ref[...]
        left, right = _shift_lr(p, i)
        up, down = _shift_ud(p, p_ref[...], n_ref[...], i)
        vert = jnp.where(i < dm["NCT"], 0.0, 1.0)
        o_ref[...] = (p * (1.0 - m[0:1] - m[1:2] - vert * (m[2:3] + m[3:4]))
                      + m[0:1] * left + m[1:2] * right + m[2:3] * up + m[3:4] * down)

    return _call(body, "shift_fwd", (dm["SH"] // cw, nt), [cur, prv, nxt, mus], cur, _sds((dm["TT"], dm["SH"])),
                 sem=("parallel", "parallel"))(p_rw, p_rw, p_rw, mu)


def _shift_bwd(p_rw, dsh, mu):
    dm = _dims()
    nt, cw, cur, prv, nxt, mus = _shift_specs()

    def body(c_ref, p_ref, n_ref, gc_ref, gp_ref, gn_ref, mu_ref, dp_ref, dmu_ref):
        i = pl.program_id(1)
        p, g, m = c_ref[...], gc_ref[...], mu_ref[...]
        vert = jnp.where(i < dm["NCT"], 0.0, 1.0)
        _, from_right = _shift_lr(m[0:1] * g, i)
        from_left, _ = _shift_lr(m[1:2] * g, i)
        _, from_down = _shift_ud(m[2:3] * g, g, m[2:3] * gn_ref[...], i)
        from_up, _ = _shift_ud(m[3:4] * g, m[3:4] * gp_ref[...], g, i)
        dp = g * (1.0 - m[0:1] - m[1:2] - vert * (m[2:3] + m[3:4])) + from_right + from_left + from_down + from_up
        dp_ref[...] = dp.astype(BF16)

        @pl.when(i == 0)
        def _():
            dmu_ref[...] = jnp.zeros_like(dmu_ref)

        left, right = _shift_lr(p, i)
        up, down = _shift_ud(p, p_ref[...], n_ref[...], i)
        s = lambda a: jnp.sum(a, axis=0, keepdims=True)
        dmu_ref[...] += jnp.concatenate([s(g * (left - p)), s(g * (right - p)), vert * s(g * (up - p)), vert * s(g * (down - p))], axis=0)

    return _call(body, "shift_bwd", (dm["SH"] // cw, nt), [cur, prv, nxt, cur, prv, nxt, mus],
                 [cur, pl.BlockSpec((4, cw), lambda j, i: (0, j))], [_sds((dm["TT"], dm["SH"]), BF16), _sds((4, dm["SH"]))],
                 sem=("parallel", "arbitrary"))(p_rw, p_rw, p_rw, dsh, dsh, dsh, mu)


def _acc_out(ref, first, vals):
    @pl.when(first)
    def _():
        for r in ref:
            r[...] = jnp.zeros_like(r)

    for r, v in zip(ref, vals):
        r[...] += v


def _rowsum(a):
    return jnp.sum(a, axis=0, keepdims=True)


def _rw_prep_specs():
    dm = _dims()
    rw = RWW // 128
    tok = lambda width, cb: pl.BlockSpec((TMV, width), lambda i: (i, cb))
    ins = [pl.BlockSpec((TMV, RWW), lambda i: (i, 1)), tok(128, 3 * rw), tok(128, 3 * rw + 1),
           _full((2, RWW)), _full((128, RWW)), _full((128, RWW)), _full((2, RWW)), _full((128, RWW)), _full((128, RWW)),
           _full((1, RWW)), _full((1, RWW))]
    return dm, ins


def _rw_prep_fwd(sh, w0, w2p0, w2p1, a0, a2p0, a2p1, kkp, kap_p):
    dm, ins = _rw_prep_specs()
    tt = dm["TT"]

    def body(k_ref, low_ref, loa_ref, w0_ref, w20_ref, w21_ref, a0_ref, a20_ref, a21_ref, kk_ref, ka_ref, kap_ref, lw_ref, b_ref, kd_ref):
        tw, la = jnp.tanh(low_ref[...]), loa_ref[...]
        kap, lw0, b0, kd0, lw1, b1, kd1 = _rw_prep_math(
            k_ref[...], _mm(tw, w20_ref[...]), _mm(tw, w21_ref[...]), _mm(la, a20_ref[...]), _mm(la, a21_ref[...]),
            w0_ref[...], a0_ref[...], kk_ref[...], ka_ref[...])
        kap_ref[...] = kap
        lw_ref[0], lw_ref[1] = lw0, lw1
        b_ref[0], b_ref[1] = b0, b1
        kd_ref[0], kd_ref[1] = kd0, kd1

    two = pl.BlockSpec((2, TMV, RWW), lambda i: (0, i, 0))
    return _call(body, "rw_prep_fwd", (tt // TMV,), ins, [pl.BlockSpec((TMV, RWW), lambda i: (i, 0)), two, two, two],
                 [_sds((tt, RWW)), _sds((2, tt, RWW)), _sds((2, tt, RWW)), _sds((2, tt, RWW))], sem=("parallel",))(
        sh, sh, sh, w0, w2p0, w2p1, a0, a2p0, a2p1, kkp, kap_p)


def _rw_prep_bwd(sh, w0, w2p0, w2p1, a0, a2p0, a2p1, kkp, kap_p, scan_grads, dkd_p, dr_p, dv_p):
    dm, ins = _rw_prep_specs()
    tt, sh_w = dm["TT"], dm["SH"]
    one = pl.BlockSpec((TMV, RWW), lambda i: (i, 0))

    def body(k_ref, low_ref, loa_ref, w0_ref, w20_ref, w21_ref, a0_ref, a20_ref, a21_ref, kk_ref, ka_ref,
             dr0, dr1, dv0, dv1, dka0, dka1, dlw0, dlw1, db0, db1, dkd0, dkd1, dkdp_ref, drp_ref, dvp_ref,
             dsh_ref, dw0_ref, dw20_ref, dw21_ref, da0_ref, da20_ref, da21_ref, dkk_ref, dka_ref):
        tw, la = jnp.tanh(low_ref[...]), loa_ref[...]
        w20, w21, a20, a21 = w20_ref[...], w21_ref[...], a20_ref[...], a21_ref[...]
        _, vjp = jax.vjp(_rw_prep_math, k_ref[...], _mm(tw, w20), _mm(tw, w21), _mm(la, a20), _mm(la, a21),
                         w0_ref[...], a0_ref[...], kk_ref[...], ka_ref[...])
        dkp = dkdp_ref[...]
        dk, dpw0, dpw1, dpa0, dpa1, dw0, da0, dkk, dka = vjp(
            (dka0[...] + dka1[...], dlw0[...], db0[...], dkd0[...] + dkp, dlw1[...], db1[...], dkd1[...] + dkp))
        twt, lat = tw.T, la.T
        dtw = _mm_nt(dpw0, w20) + _mm_nt(dpw1, w21)
        dsh_ref[:, 0:RWW] = dr0[...] + dr1[...] + drp_ref[...]
        dsh_ref[:, RWW:2 * RWW] = dk
        dsh_ref[:, 2 * RWW:3 * RWW] = dv0[...] + dv1[...] + dvp_ref[...]
        dsh_ref[:, 3 * RWW:3 * RWW + 128] = dtw * (1.0 - tw * tw)
        dsh_ref[:, 3 * RWW + 128:3 * RWW + 256] = _mm_nt(dpa0, a20) + _mm_nt(dpa1, a21)
        _acc_out((dw0_ref, dw20_ref, dw21_ref, da0_ref, da20_ref, da21_ref, dkk_ref, dka_ref), pl.program_id(0) == 0,
                 (dw0, _mm(twt, dpw0), _mm(twt, dpw1), da0, _mm(lat, dpa0), _mm(lat, dpa1), dkk, dka))

    par = [_full((2, RWW)), _full((128, RWW)), _full((128, RWW)), _full((2, RWW)), _full((128, RWW)), _full((128, RWW)),
           _full((1, RWW)), _full((1, RWW))]
    return _call(body, "rw_prep_bwd", (tt // TMV,), ins + [one] * 15,
                 [pl.BlockSpec((TMV, sh_w), lambda i: (i, 0))] + par,
                 [_sds((tt, sh_w)), _sds((2, RWW)), _sds((128, RWW)), _sds((128, RWW)), _sds((2, RWW)), _sds((128, RWW)), _sds((128, RWW)),
                  _sds((1, RWW)), _sds((1, RWW))],
                 sem=("arbitrary",))(sh, sh, sh, w0, w2p0, w2p1, a0, a2p0, a2p1, kkp, kap_p, *scan_grads, dkd_p, dr_p, dv_p)


def _rw_post_specs():
    dm = _dims()
    nv = dm["NCTV"]
    rw = RWW // 128
    lat3 = lambda d: pl.BlockSpec((None, TMV, RWW), lambda i: (d, nv + i, 0))
    lat2 = pl.BlockSpec((TMV, RWW), lambda i: (nv + i, 0))
    ins = [lat2, lat2, lat3(0), lat3(1),
           pl.BlockSpec((TMV, RWW), lambda i: (nv + i, 0)), pl.BlockSpec((TMV, RWW), lambda i: (nv + i, 2)),
           pl.BlockSpec((TMV, RWW), lambda i: (i, 0)), _full((1, RWW)), _full((1, RWW)), _full((1, RWW))]
    return dm, nv, ins


def _rw_post_fwd(y0_, y1_, kd, sh, p_z, rk, gng, gnb):
    dm, nv, ins = _rw_post_specs()

    def body(y0, y1, k0, k1, r, v, z, rk_ref, g_ref, b_ref, o_ref):
        o_ref[...] = _rw_post_math(y0[...], y1[...], k0[...], k1[...], r[...], v[...], z[...], rk_ref[...], g_ref[...], b_ref[...]).astype(BF16)

    return _call(body, "rw_post_fwd", (T // TMV,), ins, pl.BlockSpec((TMV, RWW), lambda i: (i, 0)), _sds((T, RWW), BF16),
                 sem=("parallel",))(y0_, y1_, kd, kd, sh, sh, p_z, rk, gng, gnb)


def _rw_post_bwd(y0_, y1_, kd, sh, p_z, rk, gng, gnb, dout):
    dm, nv, _ = _rw_post_specs()
    tt = dm["TT"]
    cl = lambda i: jnp.maximum(i - nv, 0)
    all3 = lambda d: pl.BlockSpec((None, TMV, RWW), lambda i: (d, i, 0))
    all2 = pl.BlockSpec((TMV, RWW), lambda i: (i, 0))
    ins = [all2, all2, all3(0), all3(1),
           pl.BlockSpec((TMV, RWW), lambda i: (i, 0)), pl.BlockSpec((TMV, RWW), lambda i: (i, 2)),
           pl.BlockSpec((TMV, RWW), lambda i: (cl(i), 0)), _full((1, RWW)), _full((1, RWW)), _full((1, RWW)),
           pl.BlockSpec((TMV, RWW), lambda i: (cl(i), 0))]
    tok = pl.BlockSpec((TMV, RWW), lambda i: (i, 0))

    def body(y0, y1, k0, k1, r, v, z, rk_ref, g_ref, b_ref, do_ref, dy_ref, dkd_ref, dr_ref, dv_ref, dz_ref, drk_ref, dg_ref, db_ref):
        i = pl.program_id(0)
        lat = jnp.where(i >= nv, 1.0, 0.0)
        _, vjp = jax.vjp(_rw_post_math, y0[...], y1[...], k0[...], k1[...], r[...], v[...], z[...], rk_ref[...], g_ref[...], b_ref[...])
        dy0, _, dk0, _, dr, dv, dz, drk, dg, db = vjp(do_ref[...] * lat)
        dy_ref[...] = dy0
        dkd_ref[...] = dk0
        dr_ref[...] = dr
        dv_ref[...] = dv
        dz_ref[...] = dz.astype(BF16)
        _acc_out((drk_ref, dg_ref, db_ref), i == 0, (drk, dg, db))

    return _call(body, "rw_post_bwd", (tt // TMV,), ins,
                 [tok, tok, tok, tok, pl.BlockSpec((TMV, RWW), lambda i: (cl(i), 0)), _full((1, RWW)), _full((1, RWW)), _full((1, RWW))],
                 [_sds((tt, RWW))] * 4 + [_sds((T, RWW), BF16), _sds((1, RWW)), _sds((1, RWW)), _sds((1, RWW))],
                 sem=("arbitrary",))(y0_, y1_, kd, kd, sh, sh, p_z, rk, gng, gnb, dout)


def _hg_post_fwd(o0_, o1_, p_hg, hg_g):
    dm = _dims()
    nv = dm["NCTV"]
    lat3 = lambda d: pl.BlockSpec((TMV, HGW), lambda i: (nv + i, 0))

    def body(o0, o1, z, g, out):
        out[...] = _hg_post_math(o0[...], o1[...], z[...], g[...]).astype(BF16)

    return _call(body, "hg_post_fwd", (T // TMV,), [lat3(0), lat3(1), pl.BlockSpec((TMV, HGW), lambda i: (nv + i, 4)), _full((1, HGW))],
                 pl.BlockSpec((TMV, HGW), lambda i: (i, 0)), _sds((T, HGW), BF16), sem=("parallel",))(o0_, o1_, p_hg, hg_g)


def _hg_post_bwd(o0_, o1_, p_hg, hg_g, dout):
    dm = _dims()
    nv, tt = dm["NCTV"], dm["TT"]
    all3 = lambda d: pl.BlockSpec((TMV, HGW), lambda i: (i, 0))
    tok = pl.BlockSpec((TMV, HGW), lambda i: (i, 0))

    def body(o0, o1, z, g, do_in, do_ref, dz_ref, dg_ref):
        i = pl.program_id(0)
        lat = jnp.where(i >= nv, 1.0, 0.0)
        _, vjp = jax.vjp(_hg_post_math, o0[...], o1[...], z[...], g[...])
        d0, _, dz, dg = vjp(do_in[...] * lat)
        do_ref[...] = d0
        dz_ref[...] = dz
        _acc_out((dg_ref,), i == 0, (dg,))

    return _call(body, "hg_post_bwd", (tt // TMV,),
                 [all3(0), all3(1), pl.BlockSpec((TMV, HGW), lambda i: (i, 4)), _full((1, HGW)),
                  pl.BlockSpec((TMV, HGW), lambda i: (jnp.maximum(i - nv, 0), 0))],
                 [tok, tok, _full((1, HGW))], [_sds((tt, HGW)), _sds((tt, HGW)), _sds((1, HGW))], sem=("arbitrary",))(o0_, o1_, p_hg, hg_g, dout)


def _hg_dproj(dq0, dq1, di0, di1, df0, df1, dz):
    dm = _dims()
    tt = dm["TT"]
    tok = pl.BlockSpec((TMV, HGW), lambda i: (i, 0))

    def body(q0, q1, i0, i1, f0, f1, z, o_ref):
        o_ref[:, 0:HGW] = (q0[...] + q1[...]).astype(BF16)
        o_ref[:, HGW:2 * HGW] = (i0[...] + i1[...]).astype(BF16)
        o_ref[:, 2 * HGW:3 * HGW] = f0[...].astype(BF16)
        o_ref[:, 3 * HGW:4 * HGW] = f1[...].astype(BF16)
        o_ref[:, 4 * HGW:5 * HGW] = z[...].astype(BF16)

    return _call(body, "hg_dproj", (tt // TMV,), [tok] * 7,
                 pl.BlockSpec((TMV, 5 * HGW), lambda i: (i, 0)), _sds((tt, 5 * HGW), BF16), sem=("parallel",))(
        dq0, dq1, di0, di1, df0, df1, dz)


def _merge1_fwd(out_hg, out_rw, whg_t, wrw_t, p_g):
    tok = pl.BlockSpec((TM, D), lambda i: (i, 0))

    def body(h_ref, r_ref, wh_ref, wr_ref, g1_ref, g2_ref, a_ref, b_ref, m_ref):
        nt = (((1,), (1,)), ((), ()))
        a = lax.dot_general(h_ref[...], wh_ref[...], nt, preferred_element_type=F32)
        b = lax.dot_general(r_ref[...], wr_ref[...], nt, preferred_element_type=F32)
        a_ref[...] = a
        b_ref[...] = b
        m_ref[...] = _gate_math(g1_ref[...], g2_ref[...], a, b).astype(BF16)

    return _call(body, "merge1_fwd", (T // TM,),
                 [pl.BlockSpec((TM, HGW), lambda i: (i, 0)), pl.BlockSpec((TM, RWW), lambda i: (i, 0)), _full((D, HGW)), _full((D, RWW)),
                  pl.BlockSpec((TM, D), lambda i: (i, 0)), pl.BlockSpec((TM, D), lambda i: (i, 1))],
                 [tok, tok, tok], [_sds((T, D)), _sds((T, D)), _sds((T, D), BF16)], sem=("parallel",))(out_hg, out_rw, whg_t, wrw_t, p_g, p_g)


def _head_fwd_bwd(x, merged, w_out, mods, final_g, tgt):
    tok = pl.BlockSpec((TM, D), lambda i: (i, 0))

    def body(x_ref, m_ref, w_ref, mods_ref, fg_ref, t_ref, dxo_ref, dmo_ref, loss_ref, dgate_ref, dfg_ref):
        mo = jnp.dot(m_ref[...], w_ref[...], preferred_element_type=F32)
        gate = mods_ref[0:1, 2 * D:3 * D]
        loss, vjp = jax.vjp(lambda x_, mo_, g_, fg_: _head_math(x_, mo_, g_, fg_, t_ref[...]), x_ref[...], mo, gate, fg_ref[...])
        dx, dmo, dgate, dfg = vjp(jnp.ones((1, 1), F32))
        dxo_ref[...] = dx
        dmo_ref[...] = dmo.astype(BF16)
        _acc_out((loss_ref, dgate_ref, dfg_ref), pl.program_id(0) == 0, (jnp.broadcast_to(loss, (1, 128)), dgate, dfg))

    return _call(body, "head_fwd_bwd", (T // TM,), [tok, tok, _full((D, D)), _full((8, 3 * D)), _full((1, D)), tok],
                 [tok, tok, _full((1, 128)), _full((1, D)), _full((1, D))],
                 [_sds((T, D)), _sds((T, D), BF16), _sds((1, 128)), _sds((1, D)), _sds((1, D))], sem=("arbitrary",))(
        x, merged, w_out, mods, final_g, tgt)


def _merge_bwd(dmo, w_out, p_g, a, b):
    tok = pl.BlockSpec((TM, D), lambda i: (i, 0))

    def body(dmo_ref, w_ref, g1_ref, g2_ref, a_ref, b_ref, da_ref, db_ref, dg_ref):
        dm_ = lax.dot_general(dmo_ref[...], w_ref[...], (((1,), (1,)), ((), ())), preferred_element_type=F32)
        _, vjp = jax.vjp(_gate_math, g1_ref[...], g2_ref[...], a_ref[...], b_ref[...])
        dg1, dg2, da, db = vjp(dm_)
        da_ref[...] = da.astype(BF16)
        db_ref[...] = db.astype(BF16)
        dg_ref[:, 0:D] = dg1.astype(BF16)
        dg_ref[:, D:2 * D] = dg2.astype(BF16)

    return _call(body, "merge_bwd", (T // TM,),
                 [tok, _full((D, D)), pl.BlockSpec((TM, D), lambda i: (i, 0)), pl.BlockSpec((TM, D), lambda i: (i, 1)), tok, tok],
                 [tok, tok, pl.BlockSpec((TM, 2 * D), lambda i: (i, 0))], [_sds((T, D), BF16), _sds((T, D), BF16), _sds((T, 2 * D), BF16)],
                 sem=("parallel",))(dmo, w_out, p_g, p_g, a, b)


def _h_bwd(xcat, mods, norm_g, dh_hg, dh_rw, dh_z, dh_g, dxo, dgate):
    dm = _dims()
    nct, tt = dm["NCT"], dm["TT"]
    tok = pl.BlockSpec((TM, D), lambda i: (i, 0))
    lat = pl.BlockSpec((TM, D), lambda i: (jnp.maximum(i - nct, 0), 0))

    def body(x_ref, m_ref, g_ref, d1, d2, d3, d4, dxo_ref, dgate_ref, gx_ref, dng_ref, dmods_ref):
        i = pl.program_id(0)
        is_ctx = i < nct
        latf = jnp.where(is_ctx, 0.0, 1.0)
        dh = d1[...] + d2[...] + latf * (d3[...] + d4[...])
        _, vjp = jax.vjp(_h_math, x_ref[...], g_ref[...], _sel_mod(m_ref, is_ctx, D), _sel_mod(m_ref, is_ctx, 0))
        dx, dng, dscale, dshift = vjp(dh)
        gx_ref[...] = dx + dxo_ref[...]

        @pl.when(i == 0)
        def _():
            dng_ref[...] = jnp.zeros_like(dng_ref)
            dmods_ref[...] = jnp.zeros_like(dmods_ref)
            dmods_ref[0:1, 2 * D:3 * D] = dgate_ref[...]

        dng_ref[...] += dng
        row = lax.broadcasted_iota(jnp.int32, (8, 1), 0)
        sel = jnp.where(row == jnp.where(is_ctx, 1, 0), 1.0, 0.0)
        dmods_ref[:, 0:D] += sel * dshift
        dmods_ref[:, D:2 * D] += sel * dscale

    return _call(body, "h_bwd", (tt // TM,), [tok, _full((8, 3 * D)), _full((1, D)), tok, tok, lat, lat, lat, _full((1, D))],
                 [lat, _full((1, D)), _full((8, 3 * D))], [_sds((T, D)), _sds((1, D)), _sds((8, 3 * D))], sem=("arbitrary",))(
        xcat, mods, norm_g, dh_hg, dh_rw, dh_z, dh_g, dxo, dgate)


def _lb_math(l0, l1):
    return jax.nn.sigmoid(l0 - l1)


def _lb_fwd(l0, l1):
    def body(a_ref, b_ref, o_ref):
        o_ref[...] = _lb_math(a_ref[...], b_ref[...])

    return _call(body, "lb_fwd", (1,), [_full((2, HGW))] * 2, _full((2, HGW)), _sds((2, HGW)))(l0, l1)


def _lb_bwd(l0, l1, dlb):
    def body(a_ref, b_ref, d_ref, da_ref, db_ref):
        _, vjp = jax.vjp(_lb_math, a_ref[...], b_ref[...])
        da_ref[...], db_ref[...] = vjp(d_ref[...])

    return _call(body, "lb_bwd", (1,), [_full((2, HGW))] * 3, [_full((2, HGW))] * 2, [_sds((2, HGW))] * 2)(l0, l1, dlb)


def _local_step(x, ctx, tgt, mods, w, start_exchange):
    dm = _dims()
    tt, sh_w, hgc = dm["TT"], dm["SH"], dm["HGC"]
    nb = lambda cols: cols // WB
    xcat = jnp.concatenate([ctx, x], axis=0)
    h = _h_fwd(xcat, mods, w["norm_g"])
    win_t = w["win_t"]
    p_hg = _proj_nt(h, win_t, 0, nb(hgc), 0, tt, "proj_hg")
    p_rw = _proj_nt(h, win_t, nb(hgc), nb(sh_w), 0, tt, "proj_rw")
    p_z = _proj_nt(h, win_t, nb(hgc + sh_w), nb(RWW), TC, T, "proj_z")
    p_g = _proj_nt(h, win_t, nb(hgc + sh_w + RWW), nb(2 * D), TC, T, "proj_g")

    lb3 = _lb_fwd(w["lb0"], w["lb1"]).reshape(2, 1, HGW)
    o0, o1, hg_ck = _hg_scan_fwd(p_hg, lb3)
    out_hg = _hg_post_fwd(o0, o1, p_hg, w["hg_g"])

    sh = _shift_fwd(p_rw, w["mu"])
    zpad = jnp.zeros((LORA, RWW), F32)
    prep_w = (w["w0"], jnp.concatenate([w["w2"][0], zpad], 0), jnp.concatenate([zpad, w["w2"][1]], 0),
              w["a0"], jnp.concatenate([w["a2"][0], zpad], 0), jnp.concatenate([zpad, w["a2"][1]], 0), w["kk"], w["ka"])
    kap, lw, b, kd = _rw_prep_fwd(sh, *prep_w)
    y0, y1, rw_ck, rw_tm = _rw_scan_fwd(sh, kap, lw, b, kd)
    post_w = (w["rk"], w["gng"], w["gnb"])
    out_rw = _rw_post_fwd(y0, y1, kd, sh, p_z, *post_w)

    a, bb, merged = _merge1_fwd(out_hg, out_rw, w["whg_t"], w["wrw_t"], p_g)
    dxo, dmo, loss, dgate, d_fg = _head_fwd_bwd(x, merged, w["wout"], mods, w["final_g"], tgt)
    da, db, dp_g = _merge_bwd(dmo, w["wout"], p_g, a, bb)
    g_wout = _dw_tn(merged, dmo, 0, D, 0, "dw_out")
    g_whg = _dw_tn(da, out_hg, 0, D, 0, "dw_hg")
    g_wrw = _dw_tn(db, out_rw, 0, D, 0, "dw_rw")
    token, early = start_exchange(dict(wout=g_wout, whg_t=g_whg, wrw_t=g_wrw), "early")
    d_out_hg = _mm_nn(da, w["whg_t"], 0, "dx_hg", after=token)
    d_out_rw = _mm_nn(db, w["wrw_t"], 0, "dx_rw")

    do, dz_hg, d_hg_g = _hg_post_bwd(o0, o1, p_hg, w["hg_g"], d_out_hg)
    *hg_grads, dlb3 = _hg_scan_bwd(p_hg, lb3, hg_ck, do)
    dp_hg = _hg_dproj(*hg_grads, dz_hg)
    d_lb0, d_lb1 = _lb_bwd(w["lb0"], w["lb1"], dlb3.reshape(2, HGW))

    dy, dkd_p, dr_p, dv_p, dp_z, d_rk, d_gng, d_gnb = _rw_post_bwd(y0, y1, kd, sh, p_z, *post_w, d_out_rw)
    rw_grads = _rw_scan_bwd(sh, kap, lw, b, kd, rw_ck, rw_tm, dy)
    dsh, d_w0, d_w2p0, d_w2p1, d_a0, d_a2p0, d_a2p1, d_kk, d_ka = _rw_prep_bwd(sh, *prep_w, rw_grads, dkd_p, dr_p, dv_p)
    dp_rw, d_mu = _shift_bwd(p_rw, dsh, w["mu"])

    g_win = _dw_tn(dp_hg, h, 0, dm["NCOLS"], 0, "dw_in_hg")
    g_win = _dw_tn(dp_rw, h, 0, dm["NCOLS"], nb(hgc), "dw_in_rw", prev=g_win)
    g_win = _dw_tn(dp_z, h, TC, dm["NCOLS"], nb(hgc + sh_w), "dw_in_z", prev=g_win)
    g_win = _dw_tn(dp_g, h, TC, dm["NCOLS"], nb(hgc + sh_w + RWW), "dw_in_g", prev=g_win)
    token, late = start_exchange(dict(win_t=g_win), "late")
    in_flight = (early, late)
    dh_hg = _mm_nn(dp_hg, win_t, 0, "dh_hg", after=token)
    dh_rw = _mm_nn(dp_rw, win_t, nb(hgc), "dh_rw")
    dh_z = _mm_nn(dp_z, win_t, nb(hgc + sh_w), "dh_z")
    dh_g = _mm_nn(dp_g, win_t, nb(hgc + sh_w + RWW), "dh_g")
    gx, d_ng, dmods = _h_bwd(xcat, mods, w["norm_g"], dh_hg, dh_rw, dh_z, dh_g, dxo, dgate)

    small = dict(norm_g=d_ng, lb0=d_lb0, lb1=d_lb1, hg_g=d_hg_g, mu=d_mu, w0=d_w0,
                 w2=jnp.stack([d_w2p0[:LORA], d_w2p1[LORA:]]), a0=d_a0, a2=jnp.stack([d_a2p0[:LORA], d_a2p1[LORA:]]),
                 kk=d_kk, ka=d_ka, rk=d_rk, gng=d_gng, gnb=d_gnb, final_g=d_fg)
    return loss, gx, dmods, in_flight, small


MESH = pl.DeviceIdType.MESH


def _comm_call(body, name, bufs, out_shapes, nsem, nloc):
    hbm = pl.BlockSpec(memory_space=pl.ANY)
    return pl.pallas_call(
        body, name=name, in_specs=[hbm] * len(bufs), out_specs=[hbm] * len(out_shapes), out_shape=out_shapes,
        scratch_shapes=[pltpu.SemaphoreType.DMA((nsem,)), pltpu.SemaphoreType.DMA((nsem,)), pltpu.SemaphoreType.DMA((nloc,))],
    )(*bufs)


GATHER_PIECE_BYTES = 4 * 1024 * 1024


def _gather2(bufs, name):
    pieces = []
    for bi, b in enumerate(bufs):
        rows = b.shape[0]
        if rows % 32 == 0 and b.size * b.dtype.itemsize > GATHER_PIECE_BYTES:
            pieces += [(bi, 0, rows // 2), (bi, rows // 2, rows // 2)]
        else:
            pieces.append((bi, 0, rows))
    nbuf = len(bufs)

    def body(*refs):
        ins, outs = refs[:nbuf], refs[nbuf:2 * nbuf]
        send_sems, recv_sems, _ = refs[2 * nbuf:]
        x, y, c = lax.axis_index("x"), lax.axis_index("y"), lax.axis_index("c")
        me, sib = (x, y, c), (x, y, 1 - c)
        flip = lambda a, b: a + b - 2 * a * b
        xn, yn, diag = (1 - x, y), (x, 1 - y), (1 - x, 1 - y)
        relay_from, relay_to = (flip(x, c), flip(1 - y, c)), (flip(1 - x, c), flip(y, c))

        def copy(pi, k, block, to, own=False):
            bi, r0, nr = pieces[pi]
            rows = outs[bi].at[4 * block[0] + 2 * block[1] + block[2], pl.ds(r0, nr)]
            return pltpu.make_async_remote_copy(src_ref=ins[bi].at[pl.ds(r0, nr)] if own else rows, dst_ref=rows,
                                                send_sem=send_sems.at[7 * pi + k], recv_sem=recv_sems.at[7 * pi + k],
                                                device_id=to, device_id_type=MESH)

        started = []
        for pi in range(len(pieces)):
            started += [copy(pi, 0, me, sib, own=True), copy(pi, 1, me, (*xn, c), own=True), copy(pi, 2, me, (*yn, c), own=True)]
        for cp in started:
            cp.start()
        for pi in range(len(pieces)):
            for k, chip in ((1, xn), (2, yn)):
                copy(pi, k, (*chip, c), me).wait_recv()
                started.append(copy(pi, 3 + k, (*chip, c), sib))
                started[-1].start()
            started.append(copy(pi, 3, (*relay_from, c), (*relay_to, c)))
            started[-1].start()
        for pi in range(len(pieces)):
            copy(pi, 3, (*diag, c), me).wait_recv()
            started.append(copy(pi, 6, (*diag, c), sib))
            started[-1].start()
        for pi in range(len(pieces)):
            copy(pi, 0, sib, me).wait_recv()
            for k, chip in ((4, xn), (5, yn), (6, diag)):
                copy(pi, k, (*chip, 1 - c), me).wait_recv()
        for cp in started:
            cp.wait_send()

    return _comm_call(body, name, bufs, [_sds((NDEV,) + b.shape, b.dtype) for b in bufs], 7 * len(pieces), 1)


def _gather_all(bufs, name):
    me = 4 * lax.axis_index("x") + 2 * lax.axis_index("y") + lax.axis_index("c")
    return [lax.dynamic_update_index_in_dim(g, b, me, axis=0) for g, b in zip(_gather2(bufs, name), bufs)]


def _pair_exchange(bufs, name):
    nbuf = len(bufs)

    def body(*refs):
        ins, got = refs[:nbuf], refs[nbuf:2 * nbuf]
        send_sems, recv_sems, _ = refs[2 * nbuf:]
        x, y, c = lax.axis_index("x"), lax.axis_index("y"), lax.axis_index("c")
        copies = []
        for bi in range(nbuf):
            for q in range(4):
                copies.append(pltpu.make_async_remote_copy(
                    src_ref=ins[bi].at[2 * q + 1 - c], dst_ref=got[bi].at[q], send_sem=send_sems.at[4 * bi + q],
                    recv_sem=recv_sems.at[4 * bi + q], device_id=(x, y, 1 - c), device_id_type=MESH))
                copies[-1].start()
        for cp in copies:
            cp.wait_send()
            cp.wait_recv()

    return _comm_call(body, name, bufs, [_sds((4,) + b.shape[1:], b.dtype) for b in bufs], 4 * nbuf, 1)


def _chip_copies(srcs, lands, send_sems, recv_sems):
    x, y, c = lax.axis_index("x"), lax.axis_index("y"), lax.axis_index("c")
    myq = 2 * x + y
    pairs = []
    for bi in range(len(srcs)):
        for j, (qx, qy) in enumerate([(1 - x, y), (x, 1 - y), (1 - x, 1 - y)]):
            q = 2 * qx + qy
            mk = lambda src, dst: pltpu.make_async_remote_copy(
                src_ref=src, dst_ref=dst, send_sem=send_sems.at[3 * bi + j], recv_sem=recv_sems.at[3 * bi + j],
                device_id=(qx, qy, c), device_id_type=MESH)
            pairs.append((mk(srcs[bi].at[q], lands[bi].at[myq]), mk(srcs[bi].at[myq], lands[bi].at[q])))
    return pairs


def _chip_exchange_start(bufs, name):
    nbuf = len(bufs)
    hbm, sem = pl.BlockSpec(memory_space=pltpu.HBM), pl.BlockSpec(memory_space=pltpu.SEMAPHORE)

    def body(*refs):
        srcs, lands = refs[:nbuf], refs[nbuf:2 * nbuf]
        send_sems, recv_sems = refs[2 * nbuf], refs[2 * nbuf + 1]
        token = refs[-1]
        for send, _ in _chip_copies(srcs, lands, send_sems, recv_sems):
            send.start()
        token[...] = jnp.zeros_like(token)

    held = [pltpu.HBM(b.shape, b.dtype) for b in bufs]
    outs = pl.pallas_call(
        body, name=name, in_specs=[hbm] * (2 * nbuf), out_specs=(sem, sem, *[hbm] * (2 * nbuf), pl.BlockSpec(memory_space=pltpu.VMEM)),
        out_shape=(pltpu.SemaphoreType.DMA((3 * nbuf,)), pltpu.SemaphoreType.DMA((3 * nbuf,)), *held, *held, _sds((8, 128))),
        input_output_aliases={i: 2 + i for i in range(2 * nbuf)},
        compiler_params=pltpu.CompilerParams(has_side_effects=pltpu.SideEffectType.DATAFLOW_SIDE_EFFECTING),
    )(*[pltpu.with_memory_space_constraint(b, pltpu.HBM) for b in bufs],
      *[pltpu.with_memory_space_constraint(lax.empty(b.shape, b.dtype), pltpu.HBM) for b in bufs])
    return outs[0], outs[1], list(outs[2:2 + nbuf]), list(outs[2 + nbuf:2 + 2 * nbuf]), outs[-1]


def _chip_exchange_wait(send_sems, recv_sems, srcs, lands, after, name):
    nbuf = len(srcs)
    hbm, sem = pl.BlockSpec(memory_space=pltpu.HBM), pl.BlockSpec(memory_space=pltpu.SEMAPHORE)

    def body(*refs):
        src_refs, land_refs = refs[:nbuf], refs[nbuf:2 * nbuf]
        for send, recv in _chip_copies(src_refs, land_refs, refs[2 * nbuf], refs[2 * nbuf + 1]):
            send.wait_send()
            recv.wait_recv()

    held = [pltpu.HBM(b.shape, b.dtype) for b in srcs]
    outs = pl.pallas_call(
        body, name=name, in_specs=[hbm] * (2 * nbuf) + [sem, sem, pl.BlockSpec(memory_space=pl.ANY)], out_specs=[hbm] * (2 * nbuf),
        out_shape=(*held, *held), input_output_aliases={i: i for i in range(2 * nbuf)},
        compiler_params=pltpu.CompilerParams(has_side_effects=pltpu.SideEffectType.DATAFLOW_SIDE_EFFECTING),
    )(*srcs, *lands, send_sems, recv_sems, after)
    return list(outs[:nbuf]), list(outs[nbuf:])


def _prefetch_call(body, name, scalars, grid, in_specs, out_specs, out_shape, args):
    return pl.pallas_call(
        body, name=name, out_shape=out_shape,
        grid_spec=pltpu.PrefetchScalarGridSpec(num_scalar_prefetch=1, grid=grid, in_specs=in_specs, out_specs=out_specs),
        compiler_params=pltpu.CompilerParams(dimension_semantics=("parallel",) * len(grid), vmem_limit_bytes=V7X_VMEM_LIMIT))(scalars, *args)


def _pair_add(g, got, core, name):
    _, rows, cols = got.shape
    tr = _row_tile(rows, 16, 256)

    def body(c_ref, a_ref, b_ref, o_ref):
        o_ref[...] = (a_ref[...].astype(F32) + b_ref[...].astype(F32)).astype(o_ref.dtype)

    blk = pl.BlockSpec((4, tr, cols), lambda i, c_ref: (0, i, 0))
    mine = pl.BlockSpec((4, None, tr, cols), lambda i, c_ref: (0, c_ref[0], i, 0))
    return _prefetch_call(body, name, core, (rows // tr,), [mine, blk], blk, _sds(got.shape, got.dtype),
                          (g.reshape(4, 2, rows, cols), got))


def _reduce_chips(chip_sums, recv, slots, name):
    _, rows, cols = recv.shape
    tr = _row_tile(rows, 16, 128)

    def body(s_ref, own_ref, r1_ref, r2_ref, r3_ref, o_ref):
        o_ref[...] = ((own_ref[...].astype(F32) + r1_ref[...].astype(F32)) + r2_ref[...].astype(F32)) + r3_ref[...].astype(F32)

    pick = lambda k: pl.BlockSpec((None, tr, cols), lambda i, s_ref: (s_ref[k], i, 0))
    return _prefetch_call(body, name, slots, (rows // tr,), [pick(0), pick(1), pick(2), pick(3)],
                          pl.BlockSpec((tr, cols), lambda i, s_ref: (i, 0)), _sds((rows, cols)), (chip_sums, recv, recv, recv))


def _row_tile(rows, mult, cap):
    best = mult
    for t in range(mult, cap + 1, mult):
        if rows % t == 0:
            best = t
    assert rows % best == 0, (rows, mult)
    return best


def _reduce_sources(r, name):
    nsrc, rows, cols = r.shape
    tr = _row_tile(rows, 16 if r.dtype.itemsize == 2 else 8, 128)

    def body(r_ref, o_ref):
        acc = r_ref[0].astype(F32)
        for j in range(1, nsrc):
            acc = acc + r_ref[j].astype(F32)
        o_ref[...] = acc

    return _call(body, name, (rows // tr,), [pl.BlockSpec((nsrc, tr, cols), lambda i: (0, i, 0))],
                 pl.BlockSpec((tr, cols), lambda i: (i, 0)), _sds((rows, cols)), sem=("parallel",))(r)


def _adamw(w, g, m, v, name):
    rows, cols = w.shape
    tr = _row_tile(rows, 8, 128)
    c1 = 1.0 - ADAM_B1 ** ADAM_STEP
    c2 = 1.0 - ADAM_B2 ** ADAM_STEP

    def body(w_ref, g_ref, m_ref, v_ref, d_ref, mo_ref, vo_ref):
        g_ = g_ref[...]
        m_ = ADAM_B1 * m_ref[...] + (1.0 - ADAM_B1) * g_
        v_ = ADAM_B2 * v_ref[...] + (1.0 - ADAM_B2) * (g_ * g_)
        d_ref[...] = -ADAM_LR * ((m_ / c1) / (jnp.sqrt(v_ / c2) + ADAM_EPS) + ADAM_WD * w_ref[...])
        mo_ref[...] = m_
        vo_ref[...] = v_

    blk = pl.BlockSpec((tr, cols), lambda i: (i, 0))
    return _call(body, name, (rows // tr,), [blk] * 4, [blk] * 3, [_sds((rows, cols))] * 3, sem=("parallel",))(w, g, m, v)


SLAB_PART = 1024


def _pack(arrs):
    parts = []
    for a in arrs:
        flat = a.reshape(-1)
        pad = (-flat.shape[0]) % SLAB_PART
        if pad:
            flat = jnp.concatenate([flat, jnp.zeros((pad,), flat.dtype)])
        parts.append(flat.reshape(-1, 128))
    return jnp.concatenate(parts, axis=0)


def _unpack(slab, shapes):
    outs, row = [], 0
    for s in shapes:
        n = 1
        for d in s:
            n *= d
        rows = (n + SLAB_PART - 1) // SLAB_PART * (SLAB_PART // 128)
        outs.append(slab[row:row + rows].reshape(-1)[:n].reshape(s))
        row += rows
    return outs


def _unshard_last(g, lead):
    nl = len(lead)
    return jnp.transpose(g, tuple(range(1, nl + 1)) + (0, nl + 1)).reshape(tuple(lead) + (-1,))


def kernel(x, c, ctx, c_ctx, ada_w, ada_b, norm_g, w_in, hg_lb, hg_norm_g, rw_mu, rw_w0, rw_w2, rw_a0, rw_a2, rw_kk, rw_ka, rw_rk, rw_gn_g, rw_gn_b, w_hg_out, w_rw_out, w_out, final_g, loss_target, m_c_ctx, m_ada_w, m_ada_b, m_norm_g, m_w_in, m_hg_lb, m_hg_norm_g, m_rw_mu, m_rw_w0, m_rw_w2, m_rw_a0, m_rw_a2, m_rw_kk, m_rw_ka, m_rw_rk, m_rw_gn_g, m_rw_gn_b, m_w_hg_out, m_w_rw_out, m_w_out, m_final_g, v_c_ctx, v_ada_w, v_ada_b, v_norm_g, v_w_in, v_hg_lb, v_hg_norm_g, v_rw_mu, v_rw_w0, v_rw_w2, v_rw_a0, v_rw_a2, v_rw_kk, v_rw_ka, v_rw_rk, v_rw_gn_g, v_rw_gn_b, v_w_hg_out, v_w_rw_out, v_w_out, v_final_g):
    dm = _dims()
    me = 4 * lax.axis_index("x") + 2 * lax.axis_index("y") + lax.axis_index("c")

    big_shards = [w_in[0].T.astype(BF16), w_out[0].astype(BF16), w_hg_out[0].T.astype(BF16), w_rw_out[0].T.astype(BF16)]
    sharded_small = [hg_lb, rw_mu[0], rw_w0[0], rw_w2[0], rw_a0[0], rw_a2[0]]
    c_rows = jnp.concatenate([c, jnp.zeros((7, D), F32)], axis=0)
    gathered = _gather_all(big_shards + [_pack(sharded_small), c_rows], "gather_weights")
    win_t, wout, whg_t, wrw_t = [g.reshape(-1, g.shape[2]) for g in gathered[:4]]
    per_dev = jax.vmap(lambda s: _unpack(s, [a.shape for a in sharded_small]))(gathered[4])
    hg_lb_f, mu_f, w0_f, w2_f, a0_f, a2_f = [_unshard_last(p, p.shape[1:-1]) for p in per_dev]

    ncol = ada_w.shape[2]
    cc16 = jnp.concatenate([gathered[5][:, 0], c_ctx[None], jnp.zeros((7, D), F32)], axis=0)
    mod_all = _gather_all([_ada_fwd(cc16, ada_w[0])], "gather_mods")[0]
    mod_x = lax.dynamic_index_in_dim(mod_all, me, axis=1, keepdims=False).reshape(1, -1) + ada_b
    mod_c = mod_all[:, NDEV].reshape(1, -1) + ada_b
    mods = jnp.concatenate([mod_x, mod_c, jnp.zeros((6, 3 * D), F32)], axis=0)

    w = dict(win_t=win_t, wout=wout, whg_t=whg_t, wrw_t=wrw_t, norm_g=norm_g,
             lb0=hg_lb_f[:, 0], lb1=hg_lb_f[:, 1], hg_g=hg_norm_g, mu=mu_f, w0=w0_f, w2=w2_f, a0=a0_f, a2=a2_f,
             kk=rw_kk, ka=rw_ka, rk=rw_rk, gng=rw_gn_g, gnb=rw_gn_b, final_g=final_g[None])
    def start_exchange(grads_bf16, tag):
        names = list(grads_bf16)
        blocks = [grads_bf16[n].reshape(NDEV, -1, grads_bf16[n].shape[1]) for n in names]
        got = _pair_exchange(blocks, "pair_grads_" + tag)
        core = lax.axis_index("c").astype(jnp.int32).reshape(1)
        chip_sums = [_pair_add(g, r, core, "pair_add_" + n) for g, r, n in zip(blocks, got, names)]
        send_sems, recv_sems, held, lands, token = _chip_exchange_start(chip_sums, "chip_grads_start_" + tag)
        return token, (names, send_sems, recv_sems, held, lands)

    loss_dev, grad_x, dmods, in_flight, small = _local_step(x[0], ctx[0], loss_target[0], mods, w, start_exchange)
    loss = lax.psum(loss_dev[0, 0], AXES)

    dmods_all = _gather_all([dmods], "gather_dmods")[0]
    my_cols = lambda a: lax.dynamic_slice_in_dim(a, me * ncol, ncol, axis=1)
    g_ada, dcc16, d_ada_b = _ada_bwd(cc16, ada_w[0], my_cols(dmods_all[:, 0]), my_cols(dmods_all[:, 1]), dmods)
    small = dict(small, c_ctx=dcc16[NDEV:NDEV + 1], ada_b=d_ada_b)

    small_names = ["c_ctx", "ada_b", "norm_g", "lb0", "lb1", "hg_g", "mu", "w0", "w2", "a0", "a2", "kk", "ka", "rk", "gng", "gnb", "final_g"]
    small_slab = _pack([small[n] for n in small_names])
    small_all = _gather_all([small_slab], "gather_small_grads")[0]
    small_sum = _reduce_sources(small_all, "reduce_small")
    sg = dict(zip(small_names, _unpack(small_sum, [small[n].shape for n in small_names])))

    my_chip = (2 * lax.axis_index("x") + lax.axis_index("y")).astype(jnp.int32)
    others = jnp.arange(3, dtype=jnp.int32)
    slots = jnp.concatenate([my_chip.reshape(1), others + (others >= my_chip).astype(jnp.int32)])
    reduced = {}
    for tag, (names, *copies) in zip(("early", "late"), in_flight):
        chip_sums, recv = _chip_exchange_wait(*copies, small_sum, "chip_grads_wait_" + tag)
        reduced.update({n: _reduce_chips(cs, r, slots, "reduce_" + n) for n, cs, r in zip(names, chip_sums, recv)})
    g_win_t, g_wout, g_whg_t, g_wrw_t = [reduced[n] for n in ("win_t", "wout", "whg_t", "wrw_t")]

    def my_shard(full):
        n = full.shape[-1] // NDEV
        return lax.dynamic_slice_in_dim(full, me * n, n, axis=full.ndim - 1)

    grads = dict(
        c_ctx=sg["c_ctx"][0], ada_w=g_ada[None], ada_b=sg["ada_b"], norm_g=sg["norm_g"], w_in=g_win_t.T[None],
        hg_lb=my_shard(jnp.stack([sg["lb0"], sg["lb1"]], axis=1)), hg_norm_g=sg["hg_g"], rw_mu=my_shard(sg["mu"])[None],
        rw_w0=my_shard(sg["w0"])[None], rw_w2=my_shard(sg["w2"])[None], rw_a0=my_shard(sg["a0"])[None], rw_a2=my_shard(sg["a2"])[None],
        rw_kk=sg["kk"], rw_ka=sg["ka"], rw_rk=sg["rk"], rw_gn_g=sg["gng"], rw_gn_b=sg["gnb"],
        w_hg_out=g_whg_t.T[None], w_rw_out=g_wrw_t.T[None], w_out=g_wout[None], final_g=sg["final_g"][0])
    weights = dict(c_ctx=c_ctx, ada_w=ada_w, ada_b=ada_b, norm_g=norm_g, w_in=w_in, hg_lb=hg_lb, hg_norm_g=hg_norm_g, rw_mu=rw_mu,
                   rw_w0=rw_w0, rw_w2=rw_w2, rw_a0=rw_a0, rw_a2=rw_a2, rw_kk=rw_kk, rw_ka=rw_ka, rw_rk=rw_rk, rw_gn_g=rw_gn_g,
                   rw_gn_b=rw_gn_b, w_hg_out=w_hg_out, w_rw_out=w_rw_out, w_out=w_out, final_g=final_g)
    m_in = dict(zip(weights, (m_c_ctx, m_ada_w, m_ada_b, m_norm_g, m_w_in, m_hg_lb, m_hg_norm_g, m_rw_mu, m_rw_w0, m_rw_w2, m_rw_a0,
                              m_rw_a2, m_rw_kk, m_rw_ka, m_rw_rk, m_rw_gn_g, m_rw_gn_b, m_w_hg_out, m_w_rw_out, m_w_out, m_final_g)))
    v_in = dict(zip(weights, (v_c_ctx, v_ada_w, v_ada_b, v_norm_g, v_w_in, v_hg_lb, v_hg_norm_g, v_rw_mu, v_rw_w0, v_rw_w2, v_rw_a0,
                              v_rw_a2, v_rw_kk, v_rw_ka, v_rw_rk, v_rw_gn_g, v_rw_gn_b, v_w_hg_out, v_w_rw_out, v_w_out, v_final_g)))

    big_w = ["ada_w", "w_in", "w_hg_out", "w_rw_out", "w_out"]
    delta, new_m, new_v = {}, {}, {}
    for n in big_w:
        shp = weights[n].shape
        two = lambda a: a.reshape(shp[-2], shp[-1])
        d_, m_, v_ = _adamw(two(weights[n]), two(grads[n]), two(m_in[n]), two(v_in[n]), "adamw_" + n)
        delta[n], new_m[n], new_v[n] = d_.reshape(shp), m_.reshape(shp), v_.reshape(shp)
    rest = [n for n in weights if n not in big_w]
    shapes = [weights[n].shape for n in rest]
    d_s, m_s, v_s = _adamw(_pack([weights[n] for n in rest]), _pack([grads[n] for n in rest]), _pack([m_in[n] for n in rest]),
                           _pack([v_in[n] for n in rest]), "adamw_small")
    for n, d_, m_, v_ in zip(rest, _unpack(d_s, shapes), _unpack(m_s, shapes), _unpack(v_s, shapes)):
        delta[n], new_m[n], new_v[n] = d_, m_, v_

    order = list(weights)
    return (loss, grad_x[None], *[grads[n] for n in order], *[delta[n] for n in order],
            *[new_m[n] for n in order], *[new_v[n] for n in order])
```

```python
import functools

import jax
import jax.numpy as jnp
from jax import lax
from jax.experimental import pallas as pl
from jax.experimental.pallas import tpu as pltpu

D = 2048
T = 2048
TC = 256
GW = 64
C = 64
HGW = 1024
RWW = 1024
LORA = 64
NDEV = 8
TM = 256
TMV = 128
NORM_EPS = 1e-6
RW_GN_EPS = 64e-5
ADAM_LR, ADAM_B1, ADAM_B2, ADAM_EPS, ADAM_WD, ADAM_STEP = 0.001, 0.9, 0.999, 1e-08, 0.01, 10
WB = 256
V7X_VMEM_LIMIT = 56 * 1024 * 1024

F32 = jnp.float32
BF16 = jnp.bfloat16
AXES = ("x", "y", "c")


def _dims():
    tt = T + TC
    sh = 3 * RWW + 4 * LORA
    hgc = 5 * HGW
    return dict(TT=tt, NCC=TC // C, NCH=tt // C, HG_H=HGW // 128, RW_P=RWW // 128, SH=sh, HGC=hgc,
                NCOLS=hgc + sh + RWW + 2 * D, NCT=TC // TM, NCTV=TC // TMV)


def _mm(a, b):
    return _dot3(a, b, "nn")


def _mm_nt(a, b):
    return _dot3(a, b, "nt")


def _mm_tn(a, b):
    return _dot3(a, b, "tn")


def _split(x):
    hi = x.astype(BF16)
    return hi, (x - hi.astype(F32)).astype(BF16)


_FORMS = {"nn": (((1,), (0,)), ((), ())), "nt": (((1,), (1,)), ((), ())), "tn": (((0,), (0,)), ((), ()))}


def _dot3_raw(a, b, form):
    ah, al = _split(a)
    bh, bl = _split(b)
    d = lambda x, y: lax.dot_general(x, y, _FORMS[form], preferred_element_type=F32)
    return d(ah, bh) + (d(ah, bl) + d(al, bh))


@functools.partial(jax.custom_vjp, nondiff_argnums=(2,))
def _dot3(a, b, form):
    return _dot3_raw(a, b, form)


def _dot3_fwd(a, b, form):
    return _dot3_raw(a, b, form), (a, b)


def _dot3_bwd(form, res, g):
    a, b = res
    if form == "nn":
        return _dot3(g, b, "nt"), _dot3(a, g, "tn")
    if form == "nt":
        return _dot3(g, b, "nn"), _dot3(g, a, "tn")
    return _dot3(b, g, "nt"), _dot3(a, g, "nn")


_dot3.defvjp(_dot3_fwd, _dot3_bwd)


def _scan_cumsum(inc, x):
    return _cumsum_vjp(inc.astype(BF16), x)


def _cumsum_raw(inc, x, form):
    h1 = x.astype(BF16)
    r1 = x - h1.astype(F32)
    h2 = r1.astype(BF16)
    h3 = (r1 - h2.astype(F32)).astype(BF16)
    d = lambda y: lax.dot_general(inc, y, _FORMS[form], preferred_element_type=F32)
    return d(h1) + (d(h2) + d(h3))


@jax.custom_vjp
def _cumsum_vjp(inc, x):
    return _cumsum_raw(inc, x, "nn")


_cumsum_vjp.defvjp(lambda inc, x: (_cumsum_raw(inc, x, "nn"), inc),
                   lambda inc, g: (jnp.zeros_like(inc), _cumsum_raw(inc, g, "tn")))


def _silu(x):
    return x * jax.nn.sigmoid(x)


def _softplus(x):
    return jnp.maximum(x, 0.0) + jnp.log(1.0 + jnp.exp(-jnp.abs(x)))


def _iota2(shape, dim):
    return lax.broadcasted_iota(jnp.int32, shape, dim)


def _pair_mask():
    return (_iota2((128, 128), 0) < 64) == (_iota2((128, 128), 1) < 64)


def _seg64_raw(x):
    e = _pair_mask().astype(BF16)
    parts = []
    for g in range(x.shape[1] // 128):
        hi, lo = _split(x[:, g * 128:(g + 1) * 128])
        parts.append(jnp.dot(hi, e, preferred_element_type=F32) + jnp.dot(lo, e, preferred_element_type=F32))
    return parts[0] if len(parts) == 1 else jnp.concatenate(parts, axis=1)


@jax.custom_vjp
def _seg64_sum(x):
    return _seg64_raw(x)


_seg64_sum.defvjp(lambda x: (_seg64_raw(x), None), lambda _, g: (_seg64_raw(g),))


def _mm1(a, b):
    return jnp.dot(a.astype(BF16), b.astype(BF16), preferred_element_type=F32)


def _mm1_nt(a, b):
    return lax.dot_general(a.astype(BF16), b.astype(BF16), (((1,), (1,)), ((), ())), preferred_element_type=F32)


def _seg128_mean(x):
    parts = [jnp.broadcast_to(jnp.mean(x[:, g * 128:(g + 1) * 128], axis=1, keepdims=True), (x.shape[0], 128))
             for g in range(x.shape[1] // 128)]
    return parts[0] if len(parts) == 1 else jnp.concatenate(parts, axis=1)


def _scan_masks(d):
    lag = (_iota2((C, C), 0) - _iota2((C, C), 1)) * (1 - 2 * d)
    return lag >= 0, lag > 0


def _tri_inverse(a):
    eye = (_iota2((C, C), 0) == _iota2((C, C), 1)).astype(F32)
    p = [-x for x in a]
    tm = [eye + x for x in p]
    n = 2
    while n < C:
        p = [_mm(x, x) for x in p]
        tm = [t + _mm(t, x) for t, x in zip(tm, p)]
        n *= 2
    return tm


@jax.custom_vjp
def _tri_solve(tm, a, rhs):
    return [_mm(t, r) for t, r in zip(tm, rhs)]


def _tri_solve_fwd(tm, a, rhs):
    u = [_mm(t, r) for t, r in zip(tm, rhs)]
    return u, (tm, u)


def _tri_solve_bwd(res, g):
    tm, u = res
    d_rhs = [_mm_tn(t, x) for t, x in zip(tm, g)]
    return [jnp.zeros_like(t) for t in tm], [-_mm_nt(d, x) for d, x in zip(d_rhs, u)], d_rhs


_tri_solve.defvjp(_tri_solve_fwd, _tri_solve_bwd)


def _rw_chunks(chains, tms=None):
    s0, r, lw, kap, b, v, kd, ds = zip(*chains)
    ids = range(len(chains))
    inc, strict = zip(*[_scan_masks(d) for d in ds])
    lane = _iota2((1, 128), 1)
    hm = ((lane < 64).astype(F32), (lane >= 64).astype(F32))
    lc = [_scan_cumsum(inc[i], lw[i]) for i in ids]
    ltot = [jnp.sum(lw[i], axis=0, keepdims=True) for i in ids]
    rt = [r[i] * jnp.exp(lc[i]) for i in ids]
    kt = [kap[i] * jnp.exp(lc[i] - lw[i]) for i in ids]
    einv = [jnp.exp(-lc[i]) for i in ids]
    kh = [kd[i] * einv[i] for i in ids]
    bh = [b[i] * einv[i] for i in ids]
    eend = [jnp.exp(ltot[i] - lc[i]) for i in ids]
    kbar = [kd[i] * eend[i] for i in ids]
    bbar = [b[i] * eend[i] for i in ids]
    lhs = [jnp.concatenate([kt[i] * hm[0], kt[i] * hm[1], rt[i] * hm[0], rt[i] * hm[1]], axis=0) for i in ids]
    gk = [_mm_nt(lhs[i], kh[i]) for i in ids]
    gb = [_mm_nt(lhs[i], bh[i]) for i in ids]
    ks = [_mm_nt(kt[i], s0[i]) for i in ids]
    ys = [_mm_nt(rt[i], s0[i]) for i in ids]
    heads = [(i, h) for i in ids for h in range(2)]
    akk = {c: jnp.where(strict[c[0]], gk[c[0]][c[1] * C:(c[1] + 1) * C], 0.0) for c in heads}
    akb = {c: jnp.where(strict[c[0]], gb[c[0]][c[1] * C:(c[1] + 1) * C], 0.0) for c in heads}
    ark = {c: jnp.where(inc[c[0]], gk[c[0]][(2 + c[1]) * C:(3 + c[1]) * C], 0.0) for c in heads}
    arb = {c: jnp.where(inc[c[0]], gb[c[0]][(2 + c[1]) * C:(3 + c[1]) * C], 0.0) for c in heads}
    vh = {c: v[c[0]] * hm[c[1]] for c in heads}
    av = {c: _mm(akk[c], vh[c]) for c in heads}
    rhs = {c: ks[c[0]] * hm[c[1]] + av[c] for c in heads}
    akb_list = [akb[c] for c in heads]
    tm = _tri_inverse(akb_list) if tms is None else [tms[c] for c in heads]
    uh = dict(zip(heads, _tri_solve(tm, akb_list, [rhs[c] for c in heads])))
    yv = {c: _mm(ark[c], vh[c]) for c in heads}
    yu = {c: _mm(arb[c], uh[c]) for c in heads}
    u = [uh[i, 0] + uh[i, 1] for i in ids]
    y = [ys[i] + (yv[i, 0] - yu[i, 0]) + (yv[i, 1] - yu[i, 1]) for i in ids]
    upd = [_mm_tn(jnp.concatenate([v[i], -u[i]], axis=0), jnp.concatenate([kbar[i], bbar[i]], axis=0)) for i in ids]
    s1 = [s0[i] * jnp.exp(ltot[i]) + jnp.where(_pair_mask(), upd[i], 0.0) for i in ids]
    if tms is None:
        return [(y[i], s1[i], tm[2 * i], tm[2 * i + 1]) for i in ids]
    return [(y[i], s1[i]) for i in ids]


HG_SUB = 16


def _hg_chunks(chains):
    s0, qr, iv, f, lb, ds = zip(*chains)
    ids = range(len(chains))
    inc = [_scan_masks(d)[0] for d in ds]
    q = [_silu(qr[i]) for i in ids]
    fg = [lb[i] + (1.0 - lb[i]) * jax.nn.sigmoid(f[i]) for i in ids]
    k = [1.0 - fg[i] for i in ids]
    g = [jnp.log(fg[i]) for i in ids]
    bc = [_scan_cumsum(inc[i], g[i]) for i in ids]
    btot = [jnp.sum(g[i], axis=0, keepdims=True) for i in ids]
    o_inter = [_mm_nt(q[i] * jnp.exp(bc[i]), s0[i]) for i in ids]
    rowi = _iota2((C, 1), 0)
    outs = [[] for _ in ids]
    for blk in range(C // HG_SUB):
        lo, hi = blk * HG_SUB, (blk + 1) * HG_SUB
        first = [lo if ds[i] == 0 else hi - 1 for i in ids]
        ref = [jnp.sum(jnp.where(rowi == first[i], bc[i] - g[i], 0.0), axis=0, keepdims=True) for i in ids]
        qi = [q[i][lo:hi] * jnp.exp(bc[i][lo:hi] - ref[i]) for i in ids]
        src = [(rowi < hi) if ds[i] == 0 else (rowi >= lo) for i in ids]
        ke = [jnp.where(src[i], k[i] * jnp.exp(jnp.where(src[i], ref[i] - bc[i], 0.0)), 0.0) for i in ids]
        a = [jnp.where(inc[i][lo:hi], _mm_nt(qi[i], ke[i]), 0.0) for i in ids]
        part = [_mm(a[i], iv[i]) for i in ids]
        for i in ids:
            outs[i].append(part[i])
    o = [o_inter[i] + jnp.concatenate(outs[i], axis=0) for i in ids]
    upd = [_mm_tn(iv[i], k[i] * jnp.exp(btot[i] - bc[i])) for i in ids]
    s1 = [s0[i] * jnp.exp(btot[i]) + upd[i] for i in ids]
    return [(o[i], s1[i]) for i in ids]


def _lockstep(fn, ds, nargs):
    def flat_fn(*flat):
        return tuple(fn([tuple(flat[i * nargs:(i + 1) * nargs]) + (d,) for i, d in enumerate(ds)]))

    return flat_fn


def _h_math(x, ng, scale, shift):
    return x * lax.rsqrt(jnp.mean(x * x, axis=-1, keepdims=True) + NORM_EPS) * ng * (1.0 + scale) + shift


def _hg_post_math(o0, o1, z, g):
    o = o0 + o1
    on = o * lax.rsqrt(_seg128_mean(o * o) + NORM_EPS)
    return on * g * _silu(z)


def _rw_prep_math(k, pw0, pw1, pa0, pa1, w0, a0, kkp, kap_p):
    kk = k * kkp
    kap = kk * lax.rsqrt(_seg64_sum(kk * kk) + 1e-12)
    outs = []
    for d, (pw, pa) in enumerate(((pw0, pa0), (pw1, pa1))):
        w_log = -_softplus(-(w0[d:d + 1] + pw)) - 0.5
        lw = -jnp.exp(w_log)
        a = jax.nn.sigmoid(a0[d:d + 1] + pa)
        kd = k * (1.0 + (a - 1.0) * kap_p)
        outs += [lw, kap * a, kd]
    return (kap, *outs)


def _rw_post_math(y0, y1, kd0, kd1, r, v, z, rk, gng, gnb):
    ys, ksum = y0 + y1, kd0 + kd1
    mean = _seg64_sum(ys) * (1.0 / 64.0)
    cen = ys - mean
    var = _seg64_sum(cen * cen) * (1.0 / 64.0)
    yn = cen * lax.rsqrt(var + RW_GN_EPS) * gng + gnb
    bonus = _seg64_sum(r * ksum * rk) * v
    return (yn + bonus) * _silu(z)


def _head_math(x, mo, gate, fg, tgt):
    xo = x + gate * mo
    y = xo * lax.rsqrt(jnp.mean(xo * xo, axis=-1, keepdims=True) + NORM_EPS) * fg
    err = y - tgt
    return 0.5 * jnp.sum(jnp.mean(err * err, axis=-1, keepdims=True), axis=0, keepdims=True)


def _gate_math(ghg, grw, a, b):
    return jax.nn.sigmoid(ghg) * a + jax.nn.sigmoid(grw) * b


def _call(body, name, grid, in_specs, out_specs, out_shape, scratch=(), sem=None, aliases=None):
    return pl.pallas_call(
        body, name=name, grid=grid, in_specs=in_specs, out_specs=out_specs, out_shape=out_shape,
        scratch_shapes=list(scratch), input_output_aliases=aliases or {},
        compiler_params=pltpu.CompilerParams(dimension_semantics=sem, vmem_limit_bytes=V7X_VMEM_LIMIT))


def _full(shape):
    nd = len(shape)
    return pl.BlockSpec(shape, lambda *_: (0,) * nd)


def _sds(shape, dtype=F32):
    return jax.ShapeDtypeStruct(shape, dtype)


def _proj_nt(h, wt, wt_blk0, nblk, row0, nrows, name):
    def body(h_ref, w_ref, o_ref):
        o_ref[...] = lax.dot_general(h_ref[row0:row0 + nrows, :], w_ref[...], (((1,), (1,)), ((), ())), preferred_element_type=F32)

    return _call(body, name, (nblk,), [_full(h.shape), pl.BlockSpec((WB, wt.shape[1]), lambda j: (wt_blk0 + j, 0))],
                 pl.BlockSpec((nrows, WB), lambda j: (0, j)), _sds((nrows, nblk * WB)), sem=("parallel",))(h, wt)


def _dw_tn(a, b, row0, out_rows, out_blk0, name, prev=None):
    m, n = a.shape
    k2 = b.shape[1]
    nblk = n // WB

    def body(*refs):
        a_ref, b_ref, o_ref = refs[0], refs[1], refs[-1]
        o_ref[...] = lax.dot_general(a_ref[...], b_ref[row0:row0 + m, :], (((0,), (0,)), ((), ())),
                                     preferred_element_type=F32).astype(BF16)

    in_specs = [pl.BlockSpec((m, WB), lambda j: (0, j)), _full(b.shape)]
    args = [a, b]
    aliases = None
    if prev is not None:
        in_specs.append(pl.BlockSpec(memory_space=pl.ANY))
        args.append(prev)
        aliases = {2: 0}
    return _call(body, name, (nblk,), in_specs, pl.BlockSpec((WB, k2), lambda j: (out_blk0 + j, 0)), _sds((out_rows, k2), BF16),
                 sem=("arbitrary",), aliases=aliases)(*args)


def _mm_nn(a, w, w_blk0, name, after=None):
    m, kc = a.shape
    n = w.shape[1]
    kb = max(k for k in (WB, 2 * WB, 4 * WB) if kc % k == 0 and (w_blk0 * WB) % k == 0)
    w_blk0 = w_blk0 * WB // kb
    tmb = 3 * TM if m % (3 * TM) == 0 else (2 * TM if m % (2 * TM) == 0 else TM)

    def body(a_ref, w_ref, *rest):
        o_ref = rest[-1]

        @pl.when(pl.program_id(1) == 0)
        def _():
            o_ref[...] = jnp.zeros_like(o_ref)

        o_ref[...] += jnp.dot(a_ref[...], w_ref[...], preferred_element_type=F32)

    extra = [] if after is None else [pl.BlockSpec(memory_space=pl.ANY)]
    return _call(body, name, (m // tmb, kc // kb),
                 [pl.BlockSpec((tmb, kb), lambda i, k: (i, k)), pl.BlockSpec((kb, n), lambda i, k: (w_blk0 + k, 0))] + extra,
                 pl.BlockSpec((tmb, n), lambda i, k: (i, 0)), _sds((m, n)), sem=("parallel", "arbitrary"))(
        a, w, *([] if after is None else [after]))


def _ada_fwd(cc16, ada_shard):
    ncol = ada_shard.shape[1]

    def body(c_ref, w_ref, o_ref):
        o_ref[...] = _mm(_silu(c_ref[...]), w_ref[...])

    return _call(body, "ada_fwd", (1,), [_full((16, D)), _full((D, ncol))], _full((16, ncol)), _sds((16, ncol)))(cc16, ada_shard)


def _ada_bwd(cc16, ada_shard, dm_x, dm_c, dmods):
    ncol = ada_shard.shape[1]

    def body(c_ref, w_ref, dx_ref, dc_ref, dm_ref, dw_ref, dcc_ref, db_ref):
        s, vjp = jax.vjp(_silu, c_ref[...])
        dc_tot = dc_ref[0:1, :]
        for j in range(1, NDEV):
            dc_tot = dc_tot + dc_ref[j:j + 1, :]
        row = _iota2((8, 1), 0)
        dm16 = jnp.concatenate([dx_ref[...], jnp.where(row == 0, dc_tot, 0.0)], axis=0)
        dw_ref[...] = _mm_tn(s, dm16)
        dcc_ref[...] = vjp(_mm_nt(dm16, w_ref[...]))[0]
        db_ref[...] = dm_ref[0:1, :] + dm_ref[1:2, :]

    return _call(body, "ada_bwd", (1,), [_full((16, D)), _full((D, ncol)), _full((8, ncol)), _full((8, ncol)), _full((8, 3 * D))],
                 [_full((D, ncol)), _full((16, D)), _full((1, 3 * D))], [_sds((D, ncol)), _sds((16, D)), _sds((1, 3 * D))])(
        cc16, ada_shard, dm_x, dm_c, dmods)


def _sel_mod(mods_ref, is_ctx, lo):
    return jnp.where(is_ctx, mods_ref[1:2, lo:lo + D], mods_ref[0:1, lo:lo + D])


def _h_fwd(xcat, mods, norm_g):
    dm = _dims()

    def body(x_ref, m_ref, g_ref, o_ref):
        is_ctx = pl.program_id(0) < dm["NCT"]
        o_ref[...] = _h_math(x_ref[...], g_ref[...], _sel_mod(m_ref, is_ctx, D), _sel_mod(m_ref, is_ctx, 0)).astype(BF16)

    return _call(body, "h_fwd", (dm["TT"] // TM,),
                 [pl.BlockSpec((TM, D), lambda i: (i, 0)), _full((8, 3 * D)), _full((1, D))],
                 pl.BlockSpec((TM, D), lambda i: (i, 0)), _sds((dm["TT"], D), BF16), sem=("parallel",))(xcat, mods, norm_g)


def _chunk_of(d, s):
    dm = _dims()
    ncc, nch = dm["NCC"], dm["NCH"]
    return s if d == 0 else jnp.where(s < ncc, ncc - 1 - s, nch - 1 - (s - ncc))


HG_GROUP = 8
RW_GROUP = 4


def _dir_specs(make):
    return [make(d) for d in range(2)]


def _lanes(j):
    return slice(j * 128, (j + 1) * 128)


def _hg_scan_fwd(p_hg, lb3):
    dm = _dims()
    nh, nch, tt = dm["HG_H"], dm["NCH"], dm["TT"]
    pp = min(HG_GROUP, nh)
    lw_ = 128 * pp

    def body(q0, q1, i0, i1, f0, f1, lb0, lb1, o0, o1, ck_ref, s_ref):
        @pl.when(pl.program_id(1) == 0)
        def _():
            s_ref[...] = jnp.zeros_like(s_ref)

        keys, chains = [], []
        for d, refs in enumerate(((q0, i0, f0, lb0), (q1, i1, f1, lb1))):
            vals = [r[...] for r in refs]
            for j in range(pp):
                keys.append((d, j))
                chains.append((s_ref[d, j], *[a[:, _lanes(j)] for a in vals], d))
        for (d, j), chain, (out, s1) in zip(keys, chains, _hg_chunks(chains)):
            ck_ref[d, j] = chain[0]
            (o0, o1)[d][:, _lanes(j)] = out
            s_ref[d, j] = s1

    blk = lambda off: _dir_specs(lambda d: pl.BlockSpec((C, lw_), lambda h, s: (_chunk_of(d, s), off * nh // pp + h)))
    return _call(body, "hg_scan_fwd", (nh // pp, nch),
                 blk(0) + blk(1) + _dir_specs(lambda d: pl.BlockSpec((C, lw_), lambda h, s: (_chunk_of(d, s), (2 + d) * nh // pp + h)))
                 + _dir_specs(lambda d: pl.BlockSpec((None, 1, lw_), lambda h, s: (d, 0, h))),
                 _dir_specs(lambda d: pl.BlockSpec((C, lw_), lambda h, s: (_chunk_of(d, s), h)))
                 + [pl.BlockSpec((2, pp, None, 128, 128), lambda h, s: (0, h, s, 0, 0))],
                 [_sds((tt, HGW)), _sds((tt, HGW)), _sds((2, nh, nch, 128, 128))],
                 scratch=[pltpu.VMEM((2, pp, 128, 128), F32)], sem=("parallel", "arbitrary"))(*([p_hg] * 6), lb3, lb3)


def _hg_scan_bwd(p_hg, lb3, ck, do):
    dm = _dims()
    nh, nch, tt = dm["HG_H"], dm["NCH"], dm["TT"]
    pp = min(HG_GROUP, nh)
    lw_ = 128 * pp

    def body(q0, q1, i0, i1, f0, f1, lb0, lb1, ck_ref, do0, do1, dq0, dq1, di0, di1, df0, df1, dlb_ref, ds_ref):
        @pl.when(pl.program_id(1) == 0)
        def _():
            ds_ref[...] = jnp.zeros_like(ds_ref)
            dlb_ref[...] = jnp.zeros_like(dlb_ref)

        keys, flat, cts = [], [], []
        for d, refs in enumerate(((q0, i0, f0, lb0, do0), (q1, i1, f1, lb1, do1))):
            *vals, do_ = [r[...] for r in refs]
            for j in range(pp):
                keys.append((d, j))
                flat += [ck_ref[d, j], *[a[:, _lanes(j)] for a in vals]]
                cts.append((do_[:, _lanes(j)], ds_ref[d, j]))
        _, vjp = jax.vjp(_lockstep(_hg_chunks, [d for d, _ in keys], 5), *flat)
        grads = vjp(tuple(cts))
        for n, (d, j) in enumerate(keys):
            ds0, dq_, di_, df_, dlb = grads[5 * n:5 * n + 5]
            (dq0, dq1)[d][:, _lanes(j)] = dq_
            (di0, di1)[d][:, _lanes(j)] = di_
            (df0, df1)[d][:, _lanes(j)] = df_
            dlb_ref[d, :, _lanes(j)] += dlb
            ds_ref[d, j] = ds0

    ch = lambda d, s: _chunk_of(d, nch - 1 - s)
    blk = lambda off: _dir_specs(lambda d: pl.BlockSpec((C, lw_), lambda h, s: (ch(d, s), off * nh // pp + h)))
    tok = _dir_specs(lambda d: pl.BlockSpec((C, lw_), lambda h, s: (ch(d, s), h)))
    return _call(body, "hg_scan_bwd", (nh // pp, nch),
                 blk(0) + blk(1) + _dir_specs(lambda d: pl.BlockSpec((C, lw_), lambda h, s: (ch(d, s), (2 + d) * nh // pp + h)))
                 + _dir_specs(lambda d: pl.BlockSpec((None, 1, lw_), lambda h, s: (d, 0, h)))
                 + [pl.BlockSpec((2, pp, None, 128, 128), lambda h, s: (0, h, nch - 1 - s, 0, 0))] + tok,
                 tok * 3 + [pl.BlockSpec((2, 1, lw_), lambda h, s: (0, 0, h))],
                 [_sds((tt, HGW))] * 6 + [_sds((2, 1, HGW))],
                 scratch=[pltpu.VMEM((2, pp, 128, 128), F32)], sem=("parallel", "arbitrary"))(*([p_hg] * 6), lb3, lb3, ck, do, do)


def _rw_scan_fwd(sh, kap, lw, b, kd):
    dm = _dims()
    npair, nch, tt = dm["RW_P"], dm["NCH"], dm["TT"]
    pp = min(RW_GROUP, npair)
    lw_ = 128 * pp

    def body(r0, r1, v0, v1, ka0, ka1, lw0, lw1, b0, b1, kd0, kd1, y0, y1, ck_ref, tm_ref, s_ref):
        @pl.when(pl.program_id(1) == 0)
        def _():
            s_ref[...] = jnp.zeros_like(s_ref)

        keys, chains = [], []
        for d, refs in enumerate(((r0, lw0, ka0, b0, v0, kd0), (r1, lw1, ka1, b1, v1, kd1))):
            vals = [r[...] for r in refs]
            for j in range(pp):
                keys.append((d, j))
                chains.append((s_ref[d, j], *[a[:, _lanes(j)] for a in vals], d))
        for (d, j), chain, (out, s1, tm_a, tm_b) in zip(keys, chains, _rw_chunks(chains)):
            ck_ref[d, j] = chain[0]
            tm_ref[d, j, 0] = tm_a
            tm_ref[d, j, 1] = tm_b
            (y0, y1)[d][:, _lanes(j)] = out
            s_ref[d, j] = s1

    two = lambda off: _dir_specs(lambda d: pl.BlockSpec((C, lw_), lambda p, s: (_chunk_of(d, s), off * npair // pp + p)))
    three = _dir_specs(lambda d: pl.BlockSpec((None, C, lw_), lambda p, s: (d, _chunk_of(d, s), p)))
    return _call(body, "rw_scan_fwd", (npair // pp, nch), two(0) + two(2) + two(0) + three * 3,
                 two(0) + [pl.BlockSpec((2, pp, None, 128, 128), lambda p, s: (0, p, s, 0, 0)),
                           pl.BlockSpec((2, pp, None, 2, C, C), lambda p, s: (0, p, s, 0, 0, 0))],
                 [_sds((tt, RWW)), _sds((tt, RWW)), _sds((2, npair, nch, 128, 128)), _sds((2, npair, nch, 2, C, C))],
                 scratch=[pltpu.VMEM((2, pp, 128, 128), F32)], sem=("parallel", "arbitrary"))(
        sh, sh, sh, sh, kap, kap, lw, lw, b, b, kd, kd)


def _rw_scan_bwd(sh, kap, lw, b, kd, ck, tm_ck, dy):
    dm = _dims()
    npair, nch, tt = dm["RW_P"], dm["NCH"], dm["TT"]
    pp = min(RW_GROUP, npair)
    lw_ = 128 * pp

    def body(r0, r1, v0, v1, ka0, ka1, lw0, lw1, b0, b1, kd0, kd1, ck_ref, tm_ref, dy0, dy1,
             dr0, dr1, dv0, dv1, dka0, dka1, dlw0, dlw1, db0, db1, dkd0, dkd1, ds_ref):
        @pl.when(pl.program_id(1) == 0)
        def _():
            ds_ref[...] = jnp.zeros_like(ds_ref)

        keys, flat, cts, tms = [], [], [], {}
        for d, refs in enumerate(((r0, lw0, ka0, b0, v0, kd0, dy0), (r1, lw1, ka1, b1, v1, kd1, dy1))):
            *vals, dy_ = [r[...] for r in refs]
            for j in range(pp):
                tms[len(keys), 0], tms[len(keys), 1] = tm_ref[d, j, 0], tm_ref[d, j, 1]
                keys.append((d, j))
                flat += [ck_ref[d, j], *[a[:, _lanes(j)] for a in vals]]
                cts.append((dy_[:, _lanes(j)], ds_ref[d, j]))
        _, vjp = jax.vjp(_lockstep(functools.partial(_rw_chunks, tms=tms), [d for d, _ in keys], 7), *flat)
        grads = vjp(tuple(cts))
        outs = ((dr0, dlw0, dka0, db0, dv0, dkd0), (dr1, dlw1, dka1, db1, dv1, dkd1))
        for n, (d, j) in enumerate(keys):
            ds0, *g_in = grads[7 * n:7 * n + 7]
            for o_ref, g in zip(outs[d], g_in):
                o_ref[:, _lanes(j)] = g
            ds_ref[d, j] = ds0

    ch = lambda d, s: _chunk_of(d, nch - 1 - s)
    two = lambda off: _dir_specs(lambda d: pl.BlockSpec((C, lw_), lambda p, s: (ch(d, s), off * npair // pp + p)))
    three = _dir_specs(lambda d: pl.BlockSpec((None, C, lw_), lambda p, s: (d, ch(d, s), p)))
    return _call(body, "rw_scan_bwd", (npair // pp, nch),
                 two(0) + two(2) + two(0) + three * 3
                 + [pl.BlockSpec((2, pp, None, 128, 128), lambda p, s: (0, p, nch - 1 - s, 0, 0)),
                    pl.BlockSpec((2, pp, None, 2, C, C), lambda p, s: (0, p, nch - 1 - s, 0, 0, 0))] + two(0),
                 two(0) * 6, [_sds((tt, RWW))] * 12,
                 scratch=[pltpu.VMEM((2, pp, 128, 128), F32)], sem=("parallel", "arbitrary"))(
        sh, sh, sh, sh, kap, kap, lw, lw, b, b, kd, kd, ck, tm_ck, dy, dy)


def _tile_pos(i):
    dm = _dims()
    is_ctx = i < dm["NCT"]
    rows = lax.broadcasted_iota(jnp.int32, (TM, 1), 0)
    tl = rows + (i - dm["NCT"]) * TM
    width = jnp.where(is_ctx, TC, GW)
    assert TC & (TC - 1) == 0 and GW & (GW - 1) == 0 and TM % GW == 0
    colp = rows & (width - 1)
    return is_ctx, rows, tl, colp, width


def _shift_lr(cur, i):
    _, _, _, colp, width = _tile_pos(i)
    left = jnp.where(colp == 0, 0.0, pltpu.roll(cur, 1, 0))
    right = jnp.where(colp == width - 1, 0.0, pltpu.roll(cur, TM - 1, 0))
    return left, right


def _shift_ud(cur, prv, nxt, i):
    is_ctx, rows, tl, _, _ = _tile_pos(i)
    if TM > GW:
        up = jnp.where(rows >= GW, pltpu.roll(cur, GW, 0), pltpu.roll(prv, GW, 0))
        down = jnp.where(rows < TM - GW, pltpu.roll(cur, TM - GW, 0), pltpu.roll(nxt, TM - GW, 0))
    else:
        up, down = prv, nxt
    up = jnp.where(jnp.logical_or(is_ctx, tl < GW), 0.0, up)
    down = jnp.where(jnp.logical_or(is_ctx, tl >= T - GW), 0.0, down)
    return up, down


def _shift_specs():
    dm = _dims()
    nt = dm["TT"] // TM
    cw = max(w for w in range(128, 1664 + 1, 128) if dm["SH"] % w == 0)
    cur = pl.BlockSpec((TM, cw), lambda j, i: (i, j))
    prv = pl.BlockSpec((TM, cw), lambda j, i: (jnp.maximum(i - 1, 0), j))
    nxt = pl.BlockSpec((TM, cw), lambda j, i: (jnp.minimum(i + 1, nt - 1), j))
    mu = pl.BlockSpec((4, cw), lambda j, i: (0, j))
    return nt, cw, cur, prv, nxt, mu


def _shift_fwd(p_rw, mu):
    dm = _dims()
    nt, cw, cur, prv, nxt, mus = _shift_specs()

    def body(c_ref, p_ref, n_ref, mu_ref, o_ref):
        i = pl.program_id(1)
        p, m = c_ref[...], mu_ref[...]
        left, right = _shift_lr(p, i)
        up, down = _shift_ud(p, p_ref[...], n_ref[...], i)
        vert = jnp.where(i < dm["NCT"], 0.0, 1.0)
        o_ref[...] = (p * (1.0 - m[0:1] - m[1:2] - vert * (m[2:3] + m[3:4]))
                      + m[0:1] * left + m[1:2] * right + m[2:3] * up + m[3:4] * down)

    return _call(body, "shift_fwd", (dm["SH"] // cw, nt), [cur, prv, nxt, mus], cur, _sds((dm["TT"], dm["SH"])),
                 sem=("parallel", "parallel"))(p_rw, p_rw, p_rw, mu)


def _shift_bwd(p_rw, dsh, mu):
    dm = _dims()
    nt, cw, cur, prv, nxt, mus = _shift_specs()

    def body(c_ref, p_ref, n_ref, gc_ref, gp_ref, gn_ref, mu_ref, dp_ref, dmu_ref):
        i = pl.program_id(1)
        p, g, m = c_ref[...], gc_ref[...], mu_ref[...]
        vert = jnp.where(i < dm["NCT"], 0.0, 1.0)
        _, from_right = _shift_lr(m[0:1] * g, i)
        from_left, _ = _shift_lr(m[1:2] * g, i)
        _, from_down = _shift_ud(m[2:3] * g, g, m[2:3] * gn_ref[...], i)
        from_up, _ = _shift_ud(m[3:4] * g, m[3:4] * gp_ref[...], g, i)
        dp = g * (1.0 - m[0:1] - m[1:2] - vert * (m[2:3] + m[3:4])) + from_right + from_left + from_down + from_up
        dp_ref[...] = dp.astype(BF16)

        @pl.when(i == 0)
        def _():
            dmu_ref[...] = jnp.zeros_like(dmu_ref)

        left, right = _shift_lr(p, i)
        up, down = _shift_ud(p, p_ref[...], n_ref[...], i)
        s = lambda a: jnp.sum(a, axis=0, keepdims=True)
        dmu_ref[...] += jnp.concatenate([s(g * (left - p)), s(g * (right - p)), vert * s(g * (up - p)), vert * s(g * (down - p))], axis=0)

    return _call(body, "shift_bwd", (dm["SH"] // cw, nt), [cur, prv, nxt, cur, prv, nxt, mus],
                 [cur, pl.BlockSpec((4, cw), lambda j, i: (0, j))], [_sds((dm["TT"], dm["SH"]), BF16), _sds((4, dm["SH"]))],
                 sem=("parallel", "arbitrary"))(p_rw, p_rw, p_rw, dsh, dsh, dsh, mu)


def _acc_out(ref, first, vals):
    @pl.when(first)
    def _():
        for r in ref:
            r[...] = jnp.zeros_like(r)

    for r, v in zip(ref, vals):
        r[...] += v


def _rowsum(a):
    return jnp.sum(a, axis=0, keepdims=True)


def _rw_prep_specs():
    dm = _dims()
    rw = RWW // 128
    tok = lambda width, cb: pl.BlockSpec((TMV, width), lambda i: (i, cb))
    ins = [pl.BlockSpec((TMV, RWW), lambda i: (i, 1)), tok(128, 3 * rw), tok(128, 3 * rw + 1),
           _full((2, RWW)), _full((128, RWW)), _full((128, RWW)), _full((2, RWW)), _full((128, RWW)), _full((128, RWW)),
           _full((1, RWW)), _full((1, RWW))]
    return dm, ins


def _rw_prep_fwd(sh, w0, w2p0, w2p1, a0, a2p0, a2p1, kkp, kap_p):
    dm, ins = _rw_prep_specs()
    tt = dm["TT"]

    def body(k_ref, low_ref, loa_ref, w0_ref, w20_ref, w21_ref, a0_ref, a20_ref, a21_ref, kk_ref, ka_ref, kap_ref, lw_ref, b_ref, kd_ref):
        tw, la = jnp.tanh(low_ref[...]), loa_ref[...]
        kap, lw0, b0, kd0, lw1, b1, kd1 = _rw_prep_math(
            k_ref[...], _mm1(tw, w20_ref[...]), _mm1(tw, w21_ref[...]), _mm1(la, a20_ref[...]), _mm1(la, a21_ref[...]),
            w0_ref[...], a0_ref[...], kk_ref[...], ka_ref[...])
        kap_ref[...] = kap
        lw_ref[0], lw_ref[1] = lw0, lw1
        b_ref[0], b_ref[1] = b0, b1
        kd_ref[0], kd_ref[1] = kd0, kd1

    two = pl.BlockSpec((2, TMV, RWW), lambda i: (0, i, 0))
    return _call(body, "rw_prep_fwd", (tt // TMV,), ins, [pl.BlockSpec((TMV, RWW), lambda i: (i, 0)), two, two, two],
                 [_sds((tt, RWW)), _sds((2, tt, RWW)), _sds((2, tt, RWW)), _sds((2, tt, RWW))], sem=("parallel",))(
        sh, sh, sh, w0, w2p0, w2p1, a0, a2p0, a2p1, kkp, kap_p)


def _rw_prep_bwd(sh, w0, w2p0, w2p1, a0, a2p0, a2p1, kkp, kap_p, scan_grads, dkd_p, dr_p, dv_p):
    dm, ins = _rw_prep_specs()
    tt, sh_w = dm["TT"], dm["SH"]
    one = pl.BlockSpec((TMV, RWW), lambda i: (i, 0))

    def body(k_ref, low_ref, loa_ref, w0_ref, w20_ref, w21_ref, a0_ref, a20_ref, a21_ref, kk_ref, ka_ref,
             dr0, dr1, dv0, dv1, dka0, dka1, dlw0, dlw1, db0, db1, dkd0, dkd1, dkdp_ref, drp_ref, dvp_ref,
             dsh_ref, dw0_ref, dw20_ref, dw21_ref, da0_ref, da20_ref, da21_ref, dkk_ref, dka_ref):
        tw, la = jnp.tanh(low_ref[...]), loa_ref[...]
        w20, w21, a20, a21 = w20_ref[...], w21_ref[...], a20_ref[...], a21_ref[...]
        _, vjp = jax.vjp(_rw_prep_math, k_ref[...], _mm1(tw, w20), _mm1(tw, w21), _mm1(la, a20), _mm1(la, a21),
                         w0_ref[...], a0_ref[...], kk_ref[...], ka_ref[...])
        dkp = dkdp_ref[...]
        dk, dpw0, dpw1, dpa0, dpa1, dw0, da0, dkk, dka = vjp(
            (dka0[...] + dka1[...], dlw0[...], db0[...], dkd0[...] + dkp, dlw1[...], db1[...], dkd1[...] + dkp))
        twt, lat = tw.T, la.T
        dtw = _mm1_nt(dpw0, w20) + _mm1_nt(dpw1, w21)
        dsh_ref[:, 0:RWW] = dr0[...] + dr1[...] + drp_ref[...]
        dsh_ref[:, RWW:2 * RWW] = dk
        dsh_ref[:, 2 * RWW:3 * RWW] = dv0[...] + dv1[...] + dvp_ref[...]
        dsh_ref[:, 3 * RWW:3 * RWW + 128] = dtw * (1.0 - tw * tw)
        dsh_ref[:, 3 * RWW + 128:3 * RWW + 256] = _mm1_nt(dpa0, a20) + _mm1_nt(dpa1, a21)
        _acc_out((dw0_ref, dw20_ref, dw21_ref, da0_ref, da20_ref, da21_ref, dkk_ref, dka_ref), pl.program_id(0) == 0,
                 (dw0, _mm1(twt, dpw0), _mm1(twt, dpw1), da0, _mm1(lat, dpa0), _mm1(lat, dpa1), dkk, dka))

    par = [_full((2, RWW)), _full((128, RWW)), _full((128, RWW)), _full((2, RWW)), _full((128, RWW)), _full((128, RWW)),
           _full((1, RWW)), _full((1, RWW))]
    return _call(body, "rw_prep_bwd", (tt // TMV,), ins + [one] * 15,
                 [pl.BlockSpec((TMV, sh_w), lambda i: (i, 0))] + par,
                 [_sds((tt, sh_w)), _sds((2, RWW)), _sds((128, RWW)), _sds((128, RWW)), _sds((2, RWW)), _sds((128, RWW)), _sds((128, RWW)),
                  _sds((1, RWW)), _sds((1, RWW))],
                 sem=("arbitrary",))(sh, sh, sh, w0, w2p0, w2p1, a0, a2p0, a2p1, kkp, kap_p, *scan_grads, dkd_p, dr_p, dv_p)


def _rw_post_specs():
    dm = _dims()
    nv = dm["NCTV"]
    rw = RWW // 128
    lat3 = lambda d: pl.BlockSpec((None, TMV, RWW), lambda i: (d, nv + i, 0))
    lat2 = pl.BlockSpec((TMV, RWW), lambda i: (nv + i, 0))
    ins = [lat2, lat2, lat3(0), lat3(1),
           pl.BlockSpec((TMV, RWW), lambda i: (nv + i, 0)), pl.BlockSpec((TMV, RWW), lambda i: (nv + i, 2)),
           pl.BlockSpec((TMV, RWW), lambda i: (i, 0)), _full((1, RWW)), _full((1, RWW)), _full((1, RWW))]
    return dm, nv, ins


def _rw_post_fwd(y0_, y1_, kd, sh, p_z, rk, gng, gnb):
    dm, nv, ins = _rw_post_specs()

    def body(y0, y1, k0, k1, r, v, z, rk_ref, g_ref, b_ref, o_ref):
        o_ref[...] = _rw_post_math(y0[...], y1[...], k0[...], k1[...], r[...], v[...], z[...], rk_ref[...], g_ref[...], b_ref[...]).astype(BF16)

    return _call(body, "rw_post_fwd", (T // TMV,), ins, pl.BlockSpec((TMV, RWW), lambda i: (i, 0)), _sds((T, RWW), BF16),
                 sem=("parallel",))(y0_, y1_, kd, kd, sh, sh, p_z, rk, gng, gnb)


def _rw_post_bwd(y0_, y1_, kd, sh, p_z, rk, gng, gnb, dout):
    dm, nv, _ = _rw_post_specs()
    tt = dm["TT"]
    cl = lambda i: jnp.maximum(i - nv, 0)
    all3 = lambda d: pl.BlockSpec((None, TMV, RWW), lambda i: (d, i, 0))
    all2 = pl.BlockSpec((TMV, RWW), lambda i: (i, 0))
    ins = [all2, all2, all3(0), all3(1),
           pl.BlockSpec((TMV, RWW), lambda i: (i, 0)), pl.BlockSpec((TMV, RWW), lambda i: (i, 2)),
           pl.BlockSpec((TMV, RWW), lambda i: (cl(i), 0)), _full((1, RWW)), _full((1, RWW)), _full((1, RWW)),
           pl.BlockSpec((TMV, RWW), lambda i: (cl(i), 0))]
    tok = pl.BlockSpec((TMV, RWW), lambda i: (i, 0))

    def body(y0, y1, k0, k1, r, v, z, rk_ref, g_ref, b_ref, do_ref, dy_ref, dkd_ref, dr_ref, dv_ref, dz_ref, drk_ref, dg_ref, db_ref):
        i = pl.program_id(0)
        lat = jnp.where(i >= nv, 1.0, 0.0)
        _, vjp = jax.vjp(_rw_post_math, y0[...], y1[...], k0[...], k1[...], r[...], v[...], z[...], rk_ref[...], g_ref[...], b_ref[...])
        dy0, _, dk0, _, dr, dv, dz, drk, dg, db = vjp(do_ref[...] * lat)
        dy_ref[...] = dy0
        dkd_ref[...] = dk0
        dr_ref[...] = dr
        dv_ref[...] = dv
        dz_ref[...] = dz.astype(BF16)
        _acc_out((drk_ref, dg_ref, db_ref), i == 0, (drk, dg, db))

    return _call(body, "rw_post_bwd", (tt // TMV,), ins,
                 [tok, tok, tok, tok, pl.BlockSpec((TMV, RWW), lambda i: (cl(i), 0)), _full((1, RWW)), _full((1, RWW)), _full((1, RWW))],
                 [_sds((tt, RWW))] * 4 + [_sds((T, RWW), BF16), _sds((1, RWW)), _sds((1, RWW)), _sds((1, RWW))],
                 sem=("arbitrary",))(y0_, y1_, kd, kd, sh, sh, p_z, rk, gng, gnb, dout)


def _hg_post_fwd(o0_, o1_, p_hg, hg_g):
    dm = _dims()
    nv = dm["NCTV"]
    lat3 = lambda d: pl.BlockSpec((TMV, HGW), lambda i: (nv + i, 0))

    def body(o0, o1, z, g, out):
        out[...] = _hg_post_math(o0[...], o1[...], z[...], g[...]).astype(BF16)

    return _call(body, "hg_post_fwd", (T // TMV,), [lat3(0), lat3(1), pl.BlockSpec((TMV, HGW), lambda i: (nv + i, 4)), _full((1, HGW))],
                 pl.BlockSpec((TMV, HGW), lambda i: (i, 0)), _sds((T, HGW), BF16), sem=("parallel",))(o0_, o1_, p_hg, hg_g)


def _hg_post_bwd(o0_, o1_, p_hg, hg_g, dout):
    dm = _dims()
    nv, tt = dm["NCTV"], dm["TT"]
    all3 = lambda d: pl.BlockSpec((TMV, HGW), lambda i: (i, 0))
    tok = pl.BlockSpec((TMV, HGW), lambda i: (i, 0))

    def body(o0, o1, z, g, do_in, do_ref, dz_ref, dg_ref):
        i = pl.program_id(0)
        lat = jnp.where(i >= nv, 1.0, 0.0)
        _, vjp = jax.vjp(_hg_post_math, o0[...], o1[...], z[...], g[...])
        d0, _, dz, dg = vjp(do_in[...] * lat)
        do_ref[...] = d0
        dz_ref[...] = dz
        _acc_out((dg_ref,), i == 0, (dg,))

    return _call(body, "hg_post_bwd", (tt // TMV,),
                 [all3(0), all3(1), pl.BlockSpec((TMV, HGW), lambda i: (i, 4)), _full((1, HGW)),
                  pl.BlockSpec((TMV, HGW), lambda i: (jnp.maximum(i - nv, 0), 0))],
                 [tok, tok, _full((1, HGW))], [_sds((tt, HGW)), _sds((tt, HGW)), _sds((1, HGW))], sem=("arbitrary",))(o0_, o1_, p_hg, hg_g, dout)


def _hg_dproj(dq0, dq1, di0, di1, df0, df1, dz):
    dm = _dims()
    tt = dm["TT"]
    tok = pl.BlockSpec((TMV, HGW), lambda i: (i, 0))

    def body(q0, q1, i0, i1, f0, f1, z, o_ref):
        o_ref[:, 0:HGW] = (q0[...] + q1[...]).astype(BF16)
        o_ref[:, HGW:2 * HGW] = (i0[...] + i1[...]).astype(BF16)
        o_ref[:, 2 * HGW:3 * HGW] = f0[...].astype(BF16)
        o_ref[:, 3 * HGW:4 * HGW] = f1[...].astype(BF16)
        o_ref[:, 4 * HGW:5 * HGW] = z[...].astype(BF16)

    return _call(body, "hg_dproj", (tt // TMV,), [tok] * 7,
                 pl.BlockSpec((TMV, 5 * HGW), lambda i: (i, 0)), _sds((tt, 5 * HGW), BF16), sem=("parallel",))(
        dq0, dq1, di0, di1, df0, df1, dz)


def _merge1_fwd(out_hg, out_rw, whg_t, wrw_t, p_g):
    tok = pl.BlockSpec((TM, D), lambda i: (i, 0))

    def body(h_ref, r_ref, wh_ref, wr_ref, g1_ref, g2_ref, a_ref, b_ref, m_ref):
        nt = (((1,), (1,)), ((), ()))
        a = lax.dot_general(h_ref[...], wh_ref[...], nt, preferred_element_type=F32)
        b = lax.dot_general(r_ref[...], wr_ref[...], nt, preferred_element_type=F32)
        a_ref[...] = a
        b_ref[...] = b
        m_ref[...] = _gate_math(g1_ref[...], g2_ref[...], a, b).astype(BF16)

    return _call(body, "merge1_fwd", (T // TM,),
                 [pl.BlockSpec((TM, HGW), lambda i: (i, 0)), pl.BlockSpec((TM, RWW), lambda i: (i, 0)), _full((D, HGW)), _full((D, RWW)),
                  pl.BlockSpec((TM, D), lambda i: (i, 0)), pl.BlockSpec((TM, D), lambda i: (i, 1))],
                 [tok, tok, tok], [_sds((T, D)), _sds((T, D)), _sds((T, D), BF16)], sem=("parallel",))(out_hg, out_rw, whg_t, wrw_t, p_g, p_g)


def _head_fwd_bwd(x, merged, w_out, mods, final_g, tgt):
    tok = pl.BlockSpec((TM, D), lambda i: (i, 0))

    def body(x_ref, m_ref, w_ref, mods_ref, fg_ref, t_ref, dxo_ref, dmo_ref, loss_ref, dgate_ref, dfg_ref):
        mo = jnp.dot(m_ref[...], w_ref[...], preferred_element_type=F32)
        gate = mods_ref[0:1, 2 * D:3 * D]
        loss, vjp = jax.vjp(lambda x_, mo_, g_, fg_: _head_math(x_, mo_, g_, fg_, t_ref[...]), x_ref[...], mo, gate, fg_ref[...])
        dx, dmo, dgate, dfg = vjp(jnp.ones((1, 1), F32))
        dxo_ref[...] = dx
        dmo_ref[...] = dmo.astype(BF16)
        _acc_out((loss_ref, dgate_ref, dfg_ref), pl.program_id(0) == 0, (jnp.broadcast_to(loss, (1, 128)), dgate, dfg))

    return _call(body, "head_fwd_bwd", (T // TM,), [tok, tok, _full((D, D)), _full((8, 3 * D)), _full((1, D)), tok],
                 [tok, tok, _full((1, 128)), _full((1, D)), _full((1, D))],
                 [_sds((T, D)), _sds((T, D), BF16), _sds((1, 128)), _sds((1, D)), _sds((1, D))], sem=("arbitrary",))(
        x, merged, w_out, mods, final_g, tgt)


def _merge_bwd(dmo, w_out, p_g, a, b):
    tok = pl.BlockSpec((TM, D), lambda i: (i, 0))

    def body(dmo_ref, w_ref, g1_ref, g2_ref, a_ref, b_ref, da_ref, db_ref, dg_ref):
        dm_ = lax.dot_general(dmo_ref[...], w_ref[...], (((1,), (1,)), ((), ())), preferred_element_type=F32)
        _, vjp = jax.vjp(_gate_math, g1_ref[...], g2_ref[...], a_ref[...], b_ref[...])
        dg1, dg2, da, db = vjp(dm_)
        da_ref[...] = da.astype(BF16)
        db_ref[...] = db.astype(BF16)
        dg_ref[:, 0:D] = dg1.astype(BF16)
        dg_ref[:, D:2 * D] = dg2.astype(BF16)

    return _call(body, "merge_bwd", (T // TM,),
                 [tok, _full((D, D)), pl.BlockSpec((TM, D), lambda i: (i, 0)), pl.BlockSpec((TM, D), lambda i: (i, 1)), tok, tok],
                 [tok, tok, pl.BlockSpec((TM, 2 * D), lambda i: (i, 0))], [_sds((T, D), BF16), _sds((T, D), BF16), _sds((T, 2 * D), BF16)],
                 sem=("parallel",))(dmo, w_out, p_g, p_g, a, b)


def _h_bwd(xcat, mods, norm_g, dh_hg, dh_rw, dh_z, dh_g, dxo, dgate):
    dm = _dims()
    nct, tt = dm["NCT"], dm["TT"]
    tok = pl.BlockSpec((TM, D), lambda i: (i, 0))
    lat = pl.BlockSpec((TM, D), lambda i: (jnp.maximum(i - nct, 0), 0))

    def body(x_ref, m_ref, g_ref, d1, d2, d3, d4, dxo_ref, dgate_ref, gx_ref, dng_ref, dmods_ref):
        i = pl.program_id(0)
        is_ctx = i < nct
        latf = jnp.where(is_ctx, 0.0, 1.0)
        dh = d1[...] + d2[...] + latf * (d3[...] + d4[...])
        _, vjp = jax.vjp(_h_math, x_ref[...], g_ref[...], _sel_mod(m_ref, is_ctx, D), _sel_mod(m_ref, is_ctx, 0))
        dx, dng, dscale, dshift = vjp(dh)
        gx_ref[...] = dx + dxo_ref[...]

        @pl.when(i == 0)
        def _():
            dng_ref[...] = jnp.zeros_like(dng_ref)
            dmods_ref[...] = jnp.zeros_like(dmods_ref)
            dmods_ref[0:1, 2 * D:3 * D] = dgate_ref[...]

        dng_ref[...] += dng
        row = lax.broadcasted_iota(jnp.int32, (8, 1), 0)
        sel = jnp.where(row == jnp.where(is_ctx, 1, 0), 1.0, 0.0)
        dmods_ref[:, 0:D] += sel * dshift
        dmods_ref[:, D:2 * D] += sel * dscale

    return _call(body, "h_bwd", (tt // TM,), [tok, _full((8, 3 * D)), _full((1, D)), tok, tok, lat, lat, lat, _full((1, D))],
                 [lat, _full((1, D)), _full((8, 3 * D))], [_sds((T, D)), _sds((1, D)), _sds((8, 3 * D))], sem=("arbitrary",))(
        xcat, mods, norm_g, dh_hg, dh_rw, dh_z, dh_g, dxo, dgate)


def _lb_math(l0, l1):
    return jax.nn.sigmoid(l0 - l1)


def _lb_fwd(l0, l1):
    def body(a_ref, b_ref, o_ref):
        o_ref[...] = _lb_math(a_ref[...], b_ref[...])

    return _call(body, "lb_fwd", (1,), [_full((2, HGW))] * 2, _full((2, HGW)), _sds((2, HGW)))(l0, l1)


def _lb_bwd(l0, l1, dlb):
    def body(a_ref, b_ref, d_ref, da_ref, db_ref):
        _, vjp = jax.vjp(_lb_math, a_ref[...], b_ref[...])
        da_ref[...], db_ref[...] = vjp(d_ref[...])

    return _call(body, "lb_bwd", (1,), [_full((2, HGW))] * 3, [_full((2, HGW))] * 2, [_sds((2, HGW))] * 2)(l0, l1, dlb)


def _local_step(x, ctx, tgt, mods, w, start_exchange):
    dm = _dims()
    tt, sh_w, hgc = dm["TT"], dm["SH"], dm["HGC"]
    nb = lambda cols: cols // WB
    xcat = jnp.concatenate([ctx, x], axis=0)
    h = _h_fwd(xcat, mods, w["norm_g"])
    win_t = w["win_t"]
    p_hg = _proj_nt(h, win_t, 0, nb(hgc), 0, tt, "proj_hg")
    p_rw = _proj_nt(h, win_t, nb(hgc), nb(sh_w), 0, tt, "proj_rw")
    p_z = _proj_nt(h, win_t, nb(hgc + sh_w), nb(RWW), TC, T, "proj_z")
    p_g = _proj_nt(h, win_t, nb(hgc + sh_w + RWW), nb(2 * D), TC, T, "proj_g")

    lb3 = _lb_fwd(w["lb0"], w["lb1"]).reshape(2, 1, HGW)
    o0, o1, hg_ck = _hg_scan_fwd(p_hg, lb3)
    out_hg = _hg_post_fwd(o0, o1, p_hg, w["hg_g"])

    sh = _shift_fwd(p_rw, w["mu"])
    zpad = jnp.zeros((LORA, RWW), F32)
    prep_w = (w["w0"], jnp.concatenate([w["w2"][0], zpad], 0), jnp.concatenate([zpad, w["w2"][1]], 0),
              w["a0"], jnp.concatenate([w["a2"][0], zpad], 0), jnp.concatenate([zpad, w["a2"][1]], 0), w["kk"], w["ka"])
    kap, lw, b, kd = _rw_prep_fwd(sh, *prep_w)
    y0, y1, rw_ck, rw_tm = _rw_scan_fwd(sh, kap, lw, b, kd)
    post_w = (w["rk"], w["gng"], w["gnb"])
    out_rw = _rw_post_fwd(y0, y1, kd, sh, p_z, *post_w)

    a, bb, merged = _merge1_fwd(out_hg, out_rw, w["whg_t"], w["wrw_t"], p_g)
    dxo, dmo, loss, dgate, d_fg = _head_fwd_bwd(x, merged, w["wout"], mods, w["final_g"], tgt)
    da, db, dp_g = _merge_bwd(dmo, w["wout"], p_g, a, bb)
    g_wout = _dw_tn(merged, dmo, 0, D, 0, "dw_out")
    g_whg = _dw_tn(da, out_hg, 0, D, 0, "dw_hg")
    g_wrw = _dw_tn(db, out_rw, 0, D, 0, "dw_rw")
    token, early = start_exchange(dict(wout=g_wout, whg_t=g_whg, wrw_t=g_wrw), "early")
    d_out_hg = _mm_nn(da, w["whg_t"], 0, "dx_hg", after=token)
    d_out_rw = _mm_nn(db, w["wrw_t"], 0, "dx_rw")

    do, dz_hg, d_hg_g = _hg_post_bwd(o0, o1, p_hg, w["hg_g"], d_out_hg)
    *hg_grads, dlb3 = _hg_scan_bwd(p_hg, lb3, hg_ck, do)
    dp_hg = _hg_dproj(*hg_grads, dz_hg)
    d_lb0, d_lb1 = _lb_bwd(w["lb0"], w["lb1"], dlb3.reshape(2, HGW))

    dy, dkd_p, dr_p, dv_p, dp_z, d_rk, d_gng, d_gnb = _rw_post_bwd(y0, y1, kd, sh, p_z, *post_w, d_out_rw)
    rw_grads = _rw_scan_bwd(sh, kap, lw, b, kd, rw_ck, rw_tm, dy)
    dsh, d_w0, d_w2p0, d_w2p1, d_a0, d_a2p0, d_a2p1, d_kk, d_ka = _rw_prep_bwd(sh, *prep_w, rw_grads, dkd_p, dr_p, dv_p)
    dp_rw, d_mu = _shift_bwd(p_rw, dsh, w["mu"])

    g_win = _dw_tn(dp_hg, h, 0, dm["NCOLS"], 0, "dw_in_hg")
    g_win = _dw_tn(dp_rw, h, 0, dm["NCOLS"], nb(hgc), "dw_in_rw", prev=g_win)
    g_win = _dw_tn(dp_z, h, TC, dm["NCOLS"], nb(hgc + sh_w), "dw_in_z", prev=g_win)
    g_win = _dw_tn(dp_g, h, TC, dm["NCOLS"], nb(hgc + sh_w + RWW), "dw_in_g", prev=g_win)
    token, late = start_exchange(dict(win_t=g_win), "late")
    in_flight = (early, late)
    dh_hg = _mm_nn(dp_hg, win_t, 0, "dh_hg", after=token)
    dh_rw = _mm_nn(dp_rw, win_t, nb(hgc), "dh_rw")
    dh_z = _mm_nn(dp_z, win_t, nb(hgc + sh_w), "dh_z")
    dh_g = _mm_nn(dp_g, win_t, nb(hgc + sh_w + RWW), "dh_g")
    gx, d_ng, dmods = _h_bwd(xcat, mods, w["norm_g"], dh_hg, dh_rw, dh_z, dh_g, dxo, dgate)

    small = dict(norm_g=d_ng, lb0=d_lb0, lb1=d_lb1, hg_g=d_hg_g, mu=d_mu, w0=d_w0,
                 w2=jnp.stack([d_w2p0[:LORA], d_w2p1[LORA:]]), a0=d_a0, a2=jnp.stack([d_a2p0[:LORA], d_a2p1[LORA:]]),
                 kk=d_kk, ka=d_ka, rk=d_rk, gng=d_gng, gnb=d_gnb, final_g=d_fg)
    return loss, gx, dmods, in_flight, small


MESH = pl.DeviceIdType.MESH


def _comm_call(body, name, bufs, out_shapes, nsem, nloc):
    hbm = pl.BlockSpec(memory_space=pl.ANY)
    return pl.pallas_call(
        body, name=name, in_specs=[hbm] * len(bufs), out_specs=[hbm] * len(out_shapes), out_shape=out_shapes,
        scratch_shapes=[pltpu.SemaphoreType.DMA((nsem,)), pltpu.SemaphoreType.DMA((nsem,)), pltpu.SemaphoreType.DMA((nloc,))],
    )(*bufs)


GATHER_PIECE_BYTES = 4 * 1024 * 1024


def _gather2(bufs, name):
    pieces = []
    for bi, b in enumerate(bufs):
        rows = b.shape[0]
        if rows % 32 == 0 and b.size * b.dtype.itemsize > GATHER_PIECE_BYTES:
            pieces += [(bi, 0, rows // 2), (bi, rows // 2, rows // 2)]
        else:
            pieces.append((bi, 0, rows))
    nbuf = len(bufs)

    def body(*refs):
        ins, outs = refs[:nbuf], refs[nbuf:2 * nbuf]
        send_sems, recv_sems, _ = refs[2 * nbuf:]
        x, y, c = lax.axis_index("x"), lax.axis_index("y"), lax.axis_index("c")
        me, sib = (x, y, c), (x, y, 1 - c)
        flip = lambda a, b: a + b - 2 * a * b
        xn, yn, diag = (1 - x, y), (x, 1 - y), (1 - x, 1 - y)
        relay_from, relay_to = (flip(x, c), flip(1 - y, c)), (flip(1 - x, c), flip(y, c))

        def copy(pi, k, block, to, own=False):
            bi, r0, nr = pieces[pi]
            rows = outs[bi].at[4 * block[0] + 2 * block[1] + block[2], pl.ds(r0, nr)]
            return pltpu.make_async_remote_copy(src_ref=ins[bi].at[pl.ds(r0, nr)] if own else rows, dst_ref=rows,
                                                send_sem=send_sems.at[7 * pi + k], recv_sem=recv_sems.at[7 * pi + k],
                                                device_id=to, device_id_type=MESH)

        started = []
        for pi in range(len(pieces)):
            started += [copy(pi, 0, me, sib, own=True), copy(pi, 1, me, (*xn, c), own=True), copy(pi, 2, me, (*yn, c), own=True)]
        for cp in started:
            cp.start()
        for pi in range(len(pieces)):
            for k, chip in ((1, xn), (2, yn)):
                copy(pi, k, (*chip, c), me).wait_recv()
                started.append(copy(pi, 3 + k, (*chip, c), sib))
                started[-1].start()
            started.append(copy(pi, 3, (*relay_from, c), (*relay_to, c)))
            started[-1].start()
        for pi in range(len(pieces)):
            copy(pi, 3, (*diag, c), me).wait_recv()
            started.append(copy(pi, 6, (*diag, c), sib))
            started[-1].start()
        for pi in range(len(pieces)):
            copy(pi, 0, sib, me).wait_recv()
            for k, chip in ((4, xn), (5, yn), (6, diag)):
                copy(pi, k, (*chip, 1 - c), me).wait_recv()
        for cp in started:
            cp.wait_send()

    return _comm_call(body, name, bufs, [_sds((NDEV,) + b.shape, b.dtype) for b in bufs], 7 * len(pieces), 1)


def _gather_all(bufs, name):
    me = 4 * lax.axis_index("x") + 2 * lax.axis_index("y") + lax.axis_index("c")
    return [lax.dynamic_update_index_in_dim(g, b, me, axis=0) for g, b in zip(_gather2(bufs, name), bufs)]


def _pair_exchange(bufs, name):
    nbuf = len(bufs)

    def body(*refs):
        ins, got = refs[:nbuf], refs[nbuf:2 * nbuf]
        send_sems, recv_sems, _ = refs[2 * nbuf:]
        x, y, c = lax.axis_index("x"), lax.axis_index("y"), lax.axis_index("c")
        copies = []
        for bi in range(nbuf):
            for q in range(4):
                copies.append(pltpu.make_async_remote_copy(
                    src_ref=ins[bi].at[2 * q + 1 - c], dst_ref=got[bi].at[q], send_sem=send_sems.at[4 * bi + q],
                    recv_sem=recv_sems.at[4 * bi + q], device_id=(x, y, 1 - c), device_id_type=MESH))
                copies[-1].start()
        for cp in copies:
            cp.wait_send()
            cp.wait_recv()

    return _comm_call(body, name, bufs, [_sds((4,) + b.shape[1:], b.dtype) for b in bufs], 4 * nbuf, 1)


def _chip_copies(srcs, lands, send_sems, recv_sems):
    x, y, c = lax.axis_index("x"), lax.axis_index("y"), lax.axis_index("c")
    myq = 2 * x + y
    pairs = []
    for bi in range(len(srcs)):
        for j, (qx, qy) in enumerate([(1 - x, y), (x, 1 - y), (1 - x, 1 - y)]):
            q = 2 * qx + qy
            mk = lambda src, dst: pltpu.make_async_remote_copy(
                src_ref=src, dst_ref=dst, send_sem=send_sems.at[3 * bi + j], recv_sem=recv_sems.at[3 * bi + j],
                device_id=(qx, qy, c), device_id_type=MESH)
            pairs.append((mk(srcs[bi].at[q], lands[bi].at[myq]), mk(srcs[bi].at[myq], lands[bi].at[q])))
    return pairs


def _chip_exchange_start(bufs, name):
    nbuf = len(bufs)
    hbm, sem = pl.BlockSpec(memory_space=pltpu.HBM), pl.BlockSpec(memory_space=pltpu.SEMAPHORE)

    def body(*refs):
        srcs, lands = refs[:nbuf], refs[nbuf:2 * nbuf]
        send_sems, recv_sems = refs[2 * nbuf], refs[2 * nbuf + 1]
        token = refs[-1]
        for send, _ in _chip_copies(srcs, lands, send_sems, recv_sems):
            send.start()
        token[...] = jnp.zeros_like(token)

    held = [pltpu.HBM(b.shape, b.dtype) for b in bufs]
    outs = pl.pallas_call(
        body, name=name, in_specs=[hbm] * (2 * nbuf), out_specs=(sem, sem, *[hbm] * (2 * nbuf), pl.BlockSpec(memory_space=pltpu.VMEM)),
        out_shape=(pltpu.SemaphoreType.DMA((3 * nbuf,)), pltpu.SemaphoreType.DMA((3 * nbuf,)), *held, *held, _sds((8, 128))),
        input_output_aliases={i: 2 + i for i in range(2 * nbuf)},
        compiler_params=pltpu.CompilerParams(has_side_effects=pltpu.SideEffectType.DATAFLOW_SIDE_EFFECTING),
    )(*[pltpu.with_memory_space_constraint(b, pltpu.HBM) for b in bufs],
      *[pltpu.with_memory_space_constraint(lax.empty(b.shape, b.dtype), pltpu.HBM) for b in bufs])
    return outs[0], outs[1], list(outs[2:2 + nbuf]), list(outs[2 + nbuf:2 + 2 * nbuf]), outs[-1]


def _chip_exchange_wait(send_sems, recv_sems, srcs, lands, after, name):
    nbuf = len(srcs)
    hbm, sem = pl.BlockSpec(memory_space=pltpu.HBM), pl.BlockSpec(memory_space=pltpu.SEMAPHORE)

    def body(*refs):
        src_refs, land_refs = refs[:nbuf], refs[nbuf:2 * nbuf]
        for send, recv in _chip_copies(src_refs, land_refs, refs[2 * nbuf], refs[2 * nbuf + 1]):
            send.wait_send()
            recv.wait_recv()

    held = [pltpu.HBM(b.shape, b.dtype) for b in srcs]
    outs = pl.pallas_call(
        body, name=name, in_specs=[hbm] * (2 * nbuf) + [sem, sem, pl.BlockSpec(memory_space=pl.ANY)], out_specs=[hbm] * (2 * nbuf),
        out_shape=(*held, *held), input_output_aliases={i: i for i in range(2 * nbuf)},
        compiler_params=pltpu.CompilerParams(has_side_effects=pltpu.SideEffectType.DATAFLOW_SIDE_EFFECTING),
    )(*srcs, *lands, send_sems, recv_sems, after)
    return list(outs[:nbuf]), list(outs[nbuf:])


def _prefetch_call(body, name, scalars, grid, in_specs, out_specs, out_shape, args):
    return pl.pallas_call(
        body, name=name, out_shape=out_shape,
        grid_spec=pltpu.PrefetchScalarGridSpec(num_scalar_prefetch=1, grid=grid, in_specs=in_specs, out_specs=out_specs),
        compiler_params=pltpu.CompilerParams(dimension_semantics=("parallel",) * len(grid), vmem_limit_bytes=V7X_VMEM_LIMIT))(scalars, *args)


def _pair_add(g, got, core, name):
    _, rows, cols = got.shape
    tr = _row_tile(rows, 16, 256)

    def body(c_ref, a_ref, b_ref, o_ref):
        o_ref[...] = (a_ref[...].astype(F32) + b_ref[...].astype(F32)).astype(o_ref.dtype)

    blk = pl.BlockSpec((4, tr, cols), lambda i, c_ref: (0, i, 0))
    mine = pl.BlockSpec((4, None, tr, cols), lambda i, c_ref: (0, c_ref[0], i, 0))
    return _prefetch_call(body, name, core, (rows // tr,), [mine, blk], blk, _sds(got.shape, got.dtype),
                          (g.reshape(4, 2, rows, cols), got))


def _reduce_chips(chip_sums, recv, slots, name):
    _, rows, cols = recv.shape
    tr = _row_tile(rows, 16, 128)

    def body(s_ref, own_ref, r1_ref, r2_ref, r3_ref, o_ref):
        o_ref[...] = ((own_ref[...].astype(F32) + r1_ref[...].astype(F32)) + r2_ref[...].astype(F32)) + r3_ref[...].astype(F32)

    pick = lambda k: pl.BlockSpec((None, tr, cols), lambda i, s_ref: (s_ref[k], i, 0))
    return _prefetch_call(body, name, slots, (rows // tr,), [pick(0), pick(1), pick(2), pick(3)],
                          pl.BlockSpec((tr, cols), lambda i, s_ref: (i, 0)), _sds((rows, cols)), (chip_sums, recv, recv, recv))


def _row_tile(rows, mult, cap):
    best = mult
    for t in range(mult, cap + 1, mult):
        if rows % t == 0:
            best = t
    assert rows % best == 0, (rows, mult)
    return best


def _reduce_sources(r, name):
    nsrc, rows, cols = r.shape
    tr = _row_tile(rows, 16 if r.dtype.itemsize == 2 else 8, 128)

    def body(r_ref, o_ref):
        acc = r_ref[0].astype(F32)
        for j in range(1, nsrc):
            acc = acc + r_ref[j].astype(F32)
        o_ref[...] = acc

    return _call(body, name, (rows // tr,), [pl.BlockSpec((nsrc, tr, cols), lambda i: (0, i, 0))],
                 pl.BlockSpec((tr, cols), lambda i: (i, 0)), _sds((rows, cols)), sem=("parallel",))(r)


def _adamw(w, g, m, v, name):
    rows, cols = w.shape
    tr = _row_tile(rows, 8, 128)
    c1 = 1.0 - ADAM_B1 ** ADAM_STEP
    c2 = 1.0 - ADAM_B2 ** ADAM_STEP

    def body(w_ref, g_ref, m_ref, v_ref, d_ref, mo_ref, vo_ref):
        g_ = g_ref[...]
        m_ = ADAM_B1 * m_ref[...] + (1.0 - ADAM_B1) * g_
        v_ = ADAM_B2 * v_ref[...] + (1.0 - ADAM_B2) * (g_ * g_)
        d_ref[...] = -ADAM_LR * ((m_ / c1) / (jnp.sqrt(v_ / c2) + ADAM_EPS) + ADAM_WD * w_ref[...])
        mo_ref[...] = m_
        vo_ref[...] = v_

    blk = pl.BlockSpec((tr, cols), lambda i: (i, 0))
    return _call(body, name, (rows // tr,), [blk] * 4, [blk] * 3, [_sds((rows, cols))] * 3, sem=("parallel",))(w, g, m, v)


SLAB_PART = 1024


def _pack(arrs):
    parts = []
    for a in arrs:
        flat = a.reshape(-1)
        pad = (-flat.shape[0]) % SLAB_PART
        if pad:
            flat = jnp.concatenate([flat, jnp.zeros((pad,), flat.dtype)])
        parts.append(flat.reshape(-1, 128))
    return jnp.concatenate(parts, axis=0)


def _unpack(slab, shapes):
    outs, row = [], 0
    for s in shapes:
        n = 1
        for d in s:
            n *= d
        rows = (n + SLAB_PART - 1) // SLAB_PART * (SLAB_PART // 128)
        outs.append(slab[row:row + rows].reshape(-1)[:n].reshape(s))
        row += rows
    return outs


def _unshard_last(g, lead):
    nl = len(lead)
    return jnp.transpose(g, tuple(range(1, nl + 1)) + (0, nl + 1)).reshape(tuple(lead) + (-1,))


def kernel(x, c, ctx, c_ctx, ada_w, ada_b, norm_g, w_in, hg_lb, hg_norm_g, rw_mu, rw_w0, rw_w2, rw_a0, rw_a2, rw_kk, rw_ka, rw_rk, rw_gn_g, rw_gn_b, w_hg_out, w_rw_out, w_out, final_g, loss_target, m_c_ctx, m_ada_w, m_ada_b, m_norm_g, m_w_in, m_hg_lb, m_hg_norm_g, m_rw_mu, m_rw_w0, m_rw_w2, m_rw_a0, m_rw_a2, m_rw_kk, m_rw_ka, m_rw_rk, m_rw_gn_g, m_rw_gn_b, m_w_hg_out, m_w_rw_out, m_w_out, m_final_g, v_c_ctx, v_ada_w, v_ada_b, v_norm_g, v_w_in, v_hg_lb, v_hg_norm_g, v_rw_mu, v_rw_w0, v_rw_w2, v_rw_a0, v_rw_a2, v_rw_kk, v_rw_ka, v_rw_rk, v_rw_gn_g, v_rw_gn_b, v_w_hg_out, v_w_rw_out, v_w_out, v_final_g):
    dm = _dims()
    me = 4 * lax.axis_index("x") + 2 * lax.axis_index("y") + lax.axis_index("c")

    big_shards = [w_in[0].T.astype(BF16), w_out[0].astype(BF16), w_hg_out[0].T.astype(BF16), w_rw_out[0].T.astype(BF16)]
    sharded_small = [hg_lb, rw_mu[0], rw_w0[0], rw_w2[0], rw_a0[0], rw_a2[0]]
    c_rows = jnp.concatenate([c, jnp.zeros((7, D), F32)], axis=0)
    gathered = _gather_all(big_shards + [_pack(sharded_small), c_rows], "gather_weights")
    win_t, wout, whg_t, wrw_t = [g.reshape(-1, g.shape[2]) for g in gathered[:4]]
    per_dev = jax.vmap(lambda s: _unpack(s, [a.shape for a in sharded_small]))(gathered[4])
    hg_lb_f, mu_f, w0_f, w2_f, a0_f, a2_f = [_unshard_last(p, p.shape[1:-1]) for p in per_dev]

    ncol = ada_w.shape[2]
    cc16 = jnp.concatenate([gathered[5][:, 0], c_ctx[None], jnp.zeros((7, D), F32)], axis=0)
    mod_all = _gather_all([_ada_fwd(cc16, ada_w[0])], "gather_mods")[0]
    mod_x = lax.dynamic_index_in_dim(mod_all, me, axis=1, keepdims=False).reshape(1, -1) + ada_b
    mod_c = mod_all[:, NDEV].reshape(1, -1) + ada_b
    mods = jnp.concatenate([mod_x, mod_c, jnp.zeros((6, 3 * D), F32)], axis=0)

    w = dict(win_t=win_t, wout=wout, whg_t=whg_t, wrw_t=wrw_t, norm_g=norm_g,
             lb0=hg_lb_f[:, 0], lb1=hg_lb_f[:, 1], hg_g=hg_norm_g, mu=mu_f, w0=w0_f, w2=w2_f, a0=a0_f, a2=a2_f,
             kk=rw_kk, ka=rw_ka, rk=rw_rk, gng=rw_gn_g, gnb=rw_gn_b, final_g=final_g[None])
    def start_exchange(grads_bf16, tag):
        names = list(grads_bf16)
        blocks = [grads_bf16[n].reshape(NDEV, -1, grads_bf16[n].shape[1]) for n in names]
        got = _pair_exchange(blocks, "pair_grads_" + tag)
        core = lax.axis_index("c").astype(jnp.int32).reshape(1)
        chip_sums = [_pair_add(g, r, core, "pair_add_" + n) for g, r, n in zip(blocks, got, names)]
        send_sems, recv_sems, held, lands, token = _chip_exchange_start(chip_sums, "chip_grads_start_" + tag)
        return token, (names, send_sems, recv_sems, held, lands)

    loss_dev, grad_x, dmods, in_flight, small = _local_step(x[0], ctx[0], loss_target[0], mods, w, start_exchange)
    loss = lax.psum(loss_dev[0, 0], AXES)

    dmods_all = _gather_all([dmods], "gather_dmods")[0]
    my_cols = lambda a: lax.dynamic_slice_in_dim(a, me * ncol, ncol, axis=1)
    g_ada, dcc16, d_ada_b = _ada_bwd(cc16, ada_w[0], my_cols(dmods_all[:, 0]), my_cols(dmods_all[:, 1]), dmods)
    small = dict(small, c_ctx=dcc16[NDEV:NDEV + 1], ada_b=d_ada_b)

    small_names = ["c_ctx", "ada_b", "norm_g", "lb0", "lb1", "hg_g", "mu", "w0", "w2", "a0", "a2", "kk", "ka", "rk", "gng", "gnb", "final_g"]
    small_slab = _pack([small[n] for n in small_names])
    small_all = _gather_all([small_slab], "gather_small_grads")[0]
    small_sum = _reduce_sources(small_all, "reduce_small")
    sg = dict(zip(small_names, _unpack(small_sum, [small[n].shape for n in small_names])))

    my_chip = (2 * lax.axis_index("x") + lax.axis_index("y")).astype(jnp.int32)
    others = jnp.arange(3, dtype=jnp.int32)
    slots = jnp.concatenate([my_chip.reshape(1), others + (others >= my_chip).astype(jnp.int32)])
    reduced = {}
    for tag, (names, *copies) in zip(("early", "late"), in_flight):
        chip_sums, recv = _chip_exchange_wait(*copies, small_sum, "chip_grads_wait_" + tag)
        reduced.update({n: _reduce_chips(cs, r, slots, "reduce_" + n) for n, cs, r in zip(names, chip_sums, recv)})
    g_win_t, g_wout, g_whg_t, g_wrw_t = [reduced[n] for n in ("win_t", "wout", "whg_t", "wrw_t")]

    def my_shard(full):
        n = full.shape[-1] // NDEV
        return lax.dynamic_slice_in_dim(full, me * n, n, axis=full.ndim - 1)

    grads = dict(
        c_ctx=sg["c_ctx"][0], ada_w=g_ada[None], ada_b=sg["ada_b"], norm_g=sg["norm_g"], w_in=g_win_t.T[None],
        hg_lb=my_shard(jnp.stack([sg["lb0"], sg["lb1"]], axis=1)), hg_norm_g=sg["hg_g"], rw_mu=my_shard(sg["mu"])[None],
        rw_w0=my_shard(sg["w0"])[None], rw_w2=my_shard(sg["w2"])[None], rw_a0=my_shard(sg["a0"])[None], rw_a2=my_shard(sg["a2"])[None],
        rw_kk=sg["kk"], rw_ka=sg["ka"], rw_rk=sg["rk"], rw_gn_g=sg["gng"], rw_gn_b=sg["gnb"],
        w_hg_out=g_whg_t.T[None], w_rw_out=g_wrw_t.T[None], w_out=g_wout[None], final_g=sg["final_g"][0])
    weights = dict(c_ctx=c_ctx, ada_w=ada_w, ada_b=ada_b, norm_g=norm_g, w_in=w_in, hg_lb=hg_lb, hg_norm_g=hg_norm_g, rw_mu=rw_mu,
                   rw_w0=rw_w0, rw_w2=rw_w2, rw_a0=rw_a0, rw_a2=rw_a2, rw_kk=rw_kk, rw_ka=rw_ka, rw_rk=rw_rk, rw_gn_g=rw_gn_g,
                   rw_gn_b=rw_gn_b, w_hg_out=w_hg_out, w_rw_out=w_rw_out, w_out=w_out, final_g=final_g)
    m_in = dict(zip(weights, (m_c_ctx, m_ada_w, m_ada_b, m_norm_g, m_w_in, m_hg_lb, m_hg_norm_g, m_rw_mu, m_rw_w0, m_rw_w2, m_rw_a0,
                              m_rw_a2, m_rw_kk, m_rw_ka, m_rw_rk, m_rw_gn_g, m_rw_gn_b, m_w_hg_out, m_w_rw_out, m_w_out, m_final_g)))
    v_in = dict(zip(weights, (v_c_ctx, v_ada_w, v_ada_b, v_norm_g, v_w_in, v_hg_lb, v_hg_norm_g, v_rw_mu, v_rw_w0, v_rw_w2, v_rw_a0,
                              v_rw_a2, v_rw_kk, v_rw_ka, v_rw_rk, v_rw_gn_g, v_rw_gn_b, v_w_hg_out, v_w_rw_out, v_w_out, v_final_g)))

    big_w = ["ada_w", "w_in", "w_hg_out", "w_rw_out", "w_out"]
    delta, new_m, new_v = {}, {}, {}
    for n in big_w:
        shp = weights[n].shape
        two = lambda a: a.reshape(shp[-2], shp[-1])
        d_, m_, v_ = _adamw(two(weights[n]), two(grads[n]), two(m_in[n]), two(v_in[n]), "adamw_" + n)
        delta[n], new_m[n], new_v[n] = d_.reshape(shp), m_.reshape(shp), v_.reshape(shp)
    rest = [n for n in weights if n not in big_w]
    shapes = [weights[n].shape for n in rest]
    d_s, m_s, v_s = _adamw(_pack([weights[n] for n in rest]), _pack([grads[n] for n in rest]), _pack([m_in[n] for n in rest]),
                           _pack([v_in[n] for n in rest]), "adamw_small")
    for n, d_, m_, v_ in zip(rest, _unpack(d_s, shapes), _unpack(m_s, shapes), _unpack(v_s, shapes)):
        delta[n], new_m[n], new_v[n] = d_, m_, v_

    order = list(weights)
    return (loss, grad_x[None], *[grads[n] for n in order], *[delta[n] for n in order],
            *[new_m[n] for n in order], *[new_v[n] for n in order])
```

```python
import functools

import jax
import jax.numpy as jnp
from jax import lax
from jax.experimental import pallas as pl
from jax.experimental.pallas import tpu as pltpu

D = 2048
T = 2048
TC = 256
GW = 64
C = 64
HGW = 1024
RWW = 1024
LORA = 64
NDEV = 8
TM = 256
TMV = 128
NORM_EPS = 1e-6
RW_GN_EPS = 64e-5
ADAM_LR, ADAM_B1, ADAM_B2, ADAM_EPS, ADAM_WD, ADAM_STEP = 0.001, 0.9, 0.999, 1e-08, 0.01, 10
WB = 256
V7X_VMEM_LIMIT = 56 * 1024 * 1024

F32 = jnp.float32
BF16 = jnp.bfloat16
AXES = ("x", "y", "c")


def _dims():
    tt = T + TC
    sh = 3 * RWW + 4 * LORA
    hgc = 5 * HGW
    return dict(TT=tt, NCC=TC // C, NCH=tt // C, HG_H=HGW // 128, RW_P=RWW // 128, SH=sh, HGC=hgc,
                NCOLS=hgc + sh + RWW + 2 * D, NCT=TC // TM, NCTV=TC // TMV)


def _mm(a, b):
    return _dot3(a, b, "nn")


def _mm_nt(a, b):
    return _dot3(a, b, "nt")


def _mm_tn(a, b):
    return _dot3(a, b, "tn")


def _split(x):
    hi = x.astype(BF16)
    return hi, (x - hi.astype(F32)).astype(BF16)


_FORMS = {"nn": (((1,), (0,)), ((), ())), "nt": (((1,), (1,)), ((), ())), "tn": (((0,), (0,)), ((), ()))}


def _dot3_raw(a, b, form):
    ah, al = _split(a)
    bh, bl = _split(b)
    d = lambda x, y: lax.dot_general(x, y, _FORMS[form], preferred_element_type=F32)
    return d(ah, bh) + (d(ah, bl) + d(al, bh))


@functools.partial(jax.custom_vjp, nondiff_argnums=(2,))
def _dot3(a, b, form):
    return _dot3_raw(a, b, form)


def _dot3_fwd(a, b, form):
    return _dot3_raw(a, b, form), (a, b)


def _dot3_bwd(form, res, g):
    a, b = res
    if form == "nn":
        return _dot3(g, b, "nt"), _dot3(a, g, "tn")
    if form == "nt":
        return _dot3(g, b, "nn"), _dot3(g, a, "tn")
    return _dot3(b, g, "nt"), _dot3(a, g, "nn")


_dot3.defvjp(_dot3_fwd, _dot3_bwd)


def _scan_cumsum(inc, x):
    return _cumsum_vjp(inc.astype(BF16), x)


def _cumsum_raw(inc, x, form):
    h1 = x.astype(BF16)
    r1 = x - h1.astype(F32)
    h2 = r1.astype(BF16)
    h3 = (r1 - h2.astype(F32)).astype(BF16)
    d = lambda y: lax.dot_general(inc, y, _FORMS[form], preferred_element_type=F32)
    return d(h1) + (d(h2) + d(h3))


@jax.custom_vjp
def _cumsum_vjp(inc, x):
    return _cumsum_raw(inc, x, "nn")


_cumsum_vjp.defvjp(lambda inc, x: (_cumsum_raw(inc, x, "nn"), inc),
                   lambda inc, g: (jnp.zeros_like(inc), _cumsum_raw(inc, g, "tn")))


def _silu(x):
    return x * jax.nn.sigmoid(x)


def _softplus(x):
    return jnp.maximum(x, 0.0) + jnp.log(1.0 + jnp.exp(-jnp.abs(x)))


def _iota2(shape, dim):
    return lax.broadcasted_iota(jnp.int32, shape, dim)


def _pair_mask():
    return (_iota2((128, 128), 0) < 64) == (_iota2((128, 128), 1) < 64)


def _seg64_raw(x):
    e = _pair_mask().astype(BF16)
    parts = []
    for g in range(x.shape[1] // 128):
        hi, lo = _split(x[:, g * 128:(g + 1) * 128])
        parts.append(jnp.dot(hi, e, preferred_element_type=F32) + jnp.dot(lo, e, preferred_element_type=F32))
    return parts[0] if len(parts) == 1 else jnp.concatenate(parts, axis=1)


@jax.custom_vjp
def _seg64_sum(x):
    return _seg64_raw(x)


_seg64_sum.defvjp(lambda x: (_seg64_raw(x), None), lambda _, g: (_seg64_raw(g),))


def _mm1(a, b):
    return jnp.dot(a.astype(BF16), b.astype(BF16), preferred_element_type=F32)


def _mm1_nt(a, b):
    return lax.dot_general(a.astype(BF16), b.astype(BF16), (((1,), (1,)), ((), ())), preferred_element_type=F32)


def _seg128_mean(x):
    parts = [jnp.broadcast_to(jnp.mean(x[:, g * 128:(g + 1) * 128], axis=1, keepdims=True), (x.shape[0], 128))
             for g in range(x.shape[1] // 128)]
    return parts[0] if len(parts) == 1 else jnp.concatenate(parts, axis=1)


def _scan_masks(d):
    lag = (_iota2((C, C), 0) - _iota2((C, C), 1)) * (1 - 2 * d)
    return lag >= 0, lag > 0


def _tri_inverse(a):
    eye = (_iota2((C, C), 0) == _iota2((C, C), 1)).astype(F32)
    p = [-x for x in a]
    tm = [eye + x for x in p]
    n = 2
    while n < C:
        p = [_mm(x, x) for x in p]
        tm = [t + _mm(t, x) for t, x in zip(tm, p)]
        n *= 2
    return tm


@jax.custom_vjp
def _tri_solve(tm, a, rhs):
    return [_mm(t, r) for t, r in zip(tm, rhs)]


def _tri_solve_fwd(tm, a, rhs):
    u = [_mm(t, r) for t, r in zip(tm, rhs)]
    return u, (tm, u)


def _tri_solve_bwd(res, g):
    tm, u = res
    d_rhs = [_mm_tn(t, x) for t, x in zip(tm, g)]
    return [jnp.zeros_like(t) for t in tm], [-_mm_nt(d, x) for d, x in zip(d_rhs, u)], d_rhs


_tri_solve.defvjp(_tri_solve_fwd, _tri_solve_bwd)


def _rw_chunks(chains, tms=None):
    s0, r, lw, kap, b, v, kd, ds = zip(*chains)
    ids = range(len(chains))
    inc, strict = zip(*[_scan_masks(d) for d in ds])
    lane = _iota2((1, 128), 1)
    hm = ((lane < 64).astype(F32), (lane >= 64).astype(F32))
    lc = [_scan_cumsum(inc[i], lw[i]) for i in ids]
    ltot = [jnp.sum(lw[i], axis=0, keepdims=True) for i in ids]
    rt = [r[i] * jnp.exp(lc[i]) for i in ids]
    kt = [kap[i] * jnp.exp(lc[i] - lw[i]) for i in ids]
    einv = [jnp.exp(-lc[i]) for i in ids]
    kh = [kd[i] * einv[i] for i in ids]
    bh = [b[i] * einv[i] for i in ids]
    eend = [jnp.exp(ltot[i] - lc[i]) for i in ids]
    kbar = [kd[i] * eend[i] for i in ids]
    bbar = [b[i] * eend[i] for i in ids]
    lhs = [jnp.concatenate([kt[i] * hm[0], kt[i] * hm[1], rt[i] * hm[0], rt[i] * hm[1]], axis=0) for i in ids]
    gk = [_mm_nt(lhs[i], kh[i]) for i in ids]
    gb = [_mm_nt(lhs[i], bh[i]) for i in ids]
    ks = [_mm_nt(kt[i], s0[i]) for i in ids]
    ys = [_mm_nt(rt[i], s0[i]) for i in ids]
    heads = [(i, h) for i in ids for h in range(2)]
    akk = {c: jnp.where(strict[c[0]], gk[c[0]][c[1] * C:(c[1] + 1) * C], 0.0) for c in heads}
    akb = {c: jnp.where(strict[c[0]], gb[c[0]][c[1] * C:(c[1] + 1) * C], 0.0) for c in heads}
    ark = {c: jnp.where(inc[c[0]], gk[c[0]][(2 + c[1]) * C:(3 + c[1]) * C], 0.0) for c in heads}
    arb = {c: jnp.where(inc[c[0]], gb[c[0]][(2 + c[1]) * C:(3 + c[1]) * C], 0.0) for c in heads}
    vh = {c: v[c[0]] * hm[c[1]] for c in heads}
    av = {c: _mm(akk[c], vh[c]) for c in heads}
    rhs = {c: ks[c[0]] * hm[c[1]] + av[c] for c in heads}
    akb_list = [akb[c] for c in heads]
    tm = _tri_inverse(akb_list) if tms is None else [tms[c] for c in heads]
    uh = dict(zip(heads, _tri_solve(tm, akb_list, [rhs[c] for c in heads])))
    yv = {c: _mm(ark[c], vh[c]) for c in heads}
    yu = {c: _mm(arb[c], uh[c]) for c in heads}
    u = [uh[i, 0] + uh[i, 1] for i in ids]
    y = [ys[i] + (yv[i, 0] - yu[i, 0]) + (yv[i, 1] - yu[i, 1]) for i in ids]
    upd = [_mm_tn(jnp.concatenate([v[i], -u[i]], axis=0), jnp.concatenate([kbar[i], bbar[i]], axis=0)) for i in ids]
    s1 = [s0[i] * jnp.exp(ltot[i]) + jnp.where(_pair_mask(), upd[i], 0.0) for i in ids]
    if tms is None:
        return [(y[i], s1[i], tm[2 * i], tm[2 * i + 1]) for i in ids]
    return [(y[i], s1[i]) for i in ids]


HG_SUB = 16


def _hg_chunks(chains):
    s0, qr, iv, f, lb, ds = zip(*chains)
    ids = range(len(chains))
    inc = [_scan_masks(d)[0] for d in ds]
    q = [_silu(qr[i]) for i in ids]
    fg = [lb[i] + (1.0 - lb[i]) * jax.nn.sigmoid(f[i]) for i in ids]
    k = [1.0 - fg[i] for i in ids]
    g = [jnp.log(fg[i]) for i in ids]
    bc = [_scan_cumsum(inc[i], g[i]) for i in ids]
    btot = [jnp.sum(g[i], axis=0, keepdims=True) for i in ids]
    o_inter = [_mm_nt(q[i] * jnp.exp(bc[i]), s0[i]) for i in ids]
    rowi = _iota2((C, 1), 0)
    outs = [[] for _ in ids]
    for blk in range(C // HG_SUB):
        lo, hi = blk * HG_SUB, (blk + 1) * HG_SUB
        first = [lo if ds[i] == 0 else hi - 1 for i in ids]
        ref = [jnp.sum(jnp.where(rowi == first[i], bc[i] - g[i], 0.0), axis=0, keepdims=True) for i in ids]
        qi = [q[i][lo:hi] * jnp.exp(bc[i][lo:hi] - ref[i]) for i in ids]
        src = [(rowi < hi) if ds[i] == 0 else (rowi >= lo) for i in ids]
        ke = [jnp.where(src[i], k[i] * jnp.exp(jnp.where(src[i], ref[i] - bc[i], 0.0)), 0.0) for i in ids]
        a = [jnp.where(inc[i][lo:hi], _mm_nt(qi[i], ke[i]), 0.0) for i in ids]
        part = [_mm(a[i], iv[i]) for i in ids]
        for i in ids:
            outs[i].append(part[i])
    o = [o_inter[i] + jnp.concatenate(outs[i], axis=0) for i in ids]
    upd = [_mm_tn(iv[i], k[i] * jnp.exp(btot[i] - bc[i])) for i in ids]
    s1 = [s0[i] * jnp.exp(btot[i]) + upd[i] for i in ids]
    return [(o[i], s1[i]) for i in ids]


def _lockstep(fn, ds, nargs):
    def flat_fn(*flat):
        return tuple(fn([tuple(flat[i * nargs:(i + 1) * nargs]) + (d,) for i, d in enumerate(ds)]))

    return flat_fn


def _h_math(x, ng, scale, shift):
    return x * lax.rsqrt(jnp.mean(x * x, axis=-1, keepdims=True) + NORM_EPS) * ng * (1.0 + scale) + shift


def _hg_post_math(o0, o1, z, g):
    o = o0 + o1
    on = o * lax.rsqrt(_seg128_mean(o * o) + NORM_EPS)
    return on * g * _silu(z)


def _rw_prep_math(k, pw0, pw1, pa0, pa1, w0, a0, kkp, kap_p):
    kk = k * kkp
    kap = kk * lax.rsqrt(_seg64_sum(kk * kk) + 1e-12)
    outs = []
    for d, (pw, pa) in enumerate(((pw0, pa0), (pw1, pa1))):
        w_log = -_softplus(-(w0[d:d + 1] + pw)) - 0.5
        lw = -jnp.exp(w_log)
        a = jax.nn.sigmoid(a0[d:d + 1] + pa)
        kd = k * (1.0 + (a - 1.0) * kap_p)
        outs += [lw, kap * a, kd]
    return (kap, *outs)


def _rw_post_math(y0, y1, kd0, kd1, r, v, z, rk, gng, gnb):
    ys, ksum = y0 + y1, kd0 + kd1
    mean = _seg64_sum(ys) * (1.0 / 64.0)
    cen = ys - mean
    var = _seg64_sum(cen * cen) * (1.0 / 64.0)
    yn = cen * lax.rsqrt(var + RW_GN_EPS) * gng + gnb
    bonus = _seg64_sum(r * ksum * rk) * v
    return (yn + bonus) * _silu(z)


def _head_math(x, mo, gate, fg, tgt):
    xo = x + gate * mo
    y = xo * lax.rsqrt(jnp.mean(xo * xo, axis=-1, keepdims=True) + NORM_EPS) * fg
    err = y - tgt
    return 0.5 * jnp.sum(jnp.mean(err * err, axis=-1, keepdims=True), axis=0, keepdims=True)


def _gate_math(ghg, grw, a, b):
    return jax.nn.sigmoid(ghg) * a + jax.nn.sigmoid(grw) * b


def _call(body, name, grid, in_specs, out_specs, out_shape, scratch=(), sem=None, aliases=None):
    return pl.pallas_call(
        body, name=name, grid=grid, in_specs=in_specs, out_specs=out_specs, out_shape=out_shape,
        scratch_shapes=list(scratch), input_output_aliases=aliases or {},
        compiler_params=pltpu.CompilerParams(dimension_semantics=sem, vmem_limit_bytes=V7X_VMEM_LIMIT))


def _full(shape):
    nd = len(shape)
    return pl.BlockSpec(shape, lambda *_: (0,) * nd)


def _sds(shape, dtype=F32):
    return jax.ShapeDtypeStruct(shape, dtype)


def _proj_nt(h, wt, wt_blk0, nblk, row0, nrows, name):
    def body(h_ref, w_ref, o_ref):
        o_ref[...] = lax.dot_general(h_ref[row0:row0 + nrows, :], w_ref[...], (((1,), (1,)), ((), ())), preferred_element_type=F32)

    return _call(body, name, (nblk,), [_full(h.shape), pl.BlockSpec((WB, wt.shape[1]), lambda j: (wt_blk0 + j, 0))],
                 pl.BlockSpec((nrows, WB), lambda j: (0, j)), _sds((nrows, nblk * WB)), sem=("parallel",))(h, wt)


def _dw_tn(a, b, row0, out_rows, out_blk0, name, prev=None):
    m, n = a.shape
    k2 = b.shape[1]
    nblk = n // WB

    def body(*refs):
        a_ref, b_ref, o_ref = refs[0], refs[1], refs[-1]
        o_ref[...] = lax.dot_general(a_ref[...], b_ref[row0:row0 + m, :], (((0,), (0,)), ((), ())),
                                     preferred_element_type=F32).astype(BF16)

    in_specs = [pl.BlockSpec((m, WB), lambda j: (0, j)), _full(b.shape)]
    args = [a, b]
    aliases = None
    if prev is not None:
        in_specs.append(pl.BlockSpec(memory_space=pl.ANY))
        args.append(prev)
        aliases = {2: 0}
    return _call(body, name, (nblk,), in_specs, pl.BlockSpec((WB, k2), lambda j: (out_blk0 + j, 0)), _sds((out_rows, k2), BF16),
                 sem=("arbitrary",), aliases=aliases)(*args)


def _mm_nn(a, w, w_blk0, name, after=None):
    m, kc = a.shape
    n = w.shape[1]
    kb = max(k for k in (WB, 2 * WB, 4 * WB) if kc % k == 0 and (w_blk0 * WB) % k == 0)
    w_blk0 = w_blk0 * WB // kb
    tmb = 3 * TM if m % (3 * TM) == 0 else (2 * TM if m % (2 * TM) == 0 else TM)

    def body(a_ref, w_ref, *rest):
        o_ref = rest[-1]

        @pl.when(pl.program_id(1) == 0)
        def _():
            o_ref[...] = jnp.zeros_like(o_ref)

        o_ref[...] += jnp.dot(a_ref[...], w_ref[...], preferred_element_type=F32)

    extra = [] if after is None else [pl.BlockSpec(memory_space=pl.ANY)]
    return _call(body, name, (m // tmb, kc // kb),
                 [pl.BlockSpec((tmb, kb), lambda i, k: (i, k)), pl.BlockSpec((kb, n), lambda i, k: (w_blk0 + k, 0))] + extra,
                 pl.BlockSpec((tmb, n), lambda i, k: (i, 0)), _sds((m, n)), sem=("parallel", "arbitrary"))(
        a, w, *([] if after is None else [after]))


def _ada_fwd(cc16, ada_shard):
    ncol = ada_shard.shape[1]

    def body(c_ref, w_ref, o_ref):
        o_ref[...] = _mm(_silu(c_ref[...]), w_ref[...])

    return _call(body, "ada_fwd", (1,), [_full((16, D)), _full((D, ncol))], _full((16, ncol)), _sds((16, ncol)))(cc16, ada_shard)


def _ada_bwd(cc16, ada_shard, dm_x, dm_c, dmods):
    ncol = ada_shard.shape[1]

    def body(c_ref, w_ref, dx_ref, dc_ref, dm_ref, dw_ref, dcc_ref, db_ref):
        s, vjp = jax.vjp(_silu, c_ref[...])
        dc_tot = dc_ref[0:1, :]
        for j in range(1, NDEV):
            dc_tot = dc_tot + dc_ref[j:j + 1, :]
        row = _iota2((8, 1), 0)
        dm16 = jnp.concatenate([dx_ref[...], jnp.where(row == 0, dc_tot, 0.0)], axis=0)
        dw_ref[...] = _mm_tn(s, dm16)
        dcc_ref[...] = vjp(_mm_nt(dm16, w_ref[...]))[0]
        db_ref[...] = dm_ref[0:1, :] + dm_ref[1:2, :]

    return _call(body, "ada_bwd", (1,), [_full((16, D)), _full((D, ncol)), _full((8, ncol)), _full((8, ncol)), _full((8, 3 * D))],
                 [_full((D, ncol)), _full((16, D)), _full((1, 3 * D))], [_sds((D, ncol)), _sds((16, D)), _sds((1, 3 * D))])(
        cc16, ada_shard, dm_x, dm_c, dmods)


def _sel_mod(mods_ref, is_ctx, lo):
    return jnp.where(is_ctx, mods_ref[1:2, lo:lo + D], mods_ref[0:1, lo:lo + D])


def _h_fwd(xcat, mods, norm_g):
    dm = _dims()

    def body(x_ref, m_ref, g_ref, o_ref):
        is_ctx = pl.program_id(0) < dm["NCT"]
        o_ref[...] = _h_math(x_ref[...], g_ref[...], _sel_mod(m_ref, is_ctx, D), _sel_mod(m_ref, is_ctx, 0)).astype(BF16)

    return _call(body, "h_fwd", (dm["TT"] // TM,),
                 [pl.BlockSpec((TM, D), lambda i: (i, 0)), _full((8, 3 * D)), _full((1, D))],
                 pl.BlockSpec((TM, D), lambda i: (i, 0)), _sds((dm["TT"], D), BF16), sem=("parallel",))(xcat, mods, norm_g)


def _chunk_of(d, s):
    dm = _dims()
    ncc, nch = dm["NCC"], dm["NCH"]
    return s if d == 0 else jnp.where(s < ncc, ncc - 1 - s, nch - 1 - (s - ncc))


HG_GROUP = 8
RW_GROUP = 8


def _dir_specs(make):
    return [make(d) for d in range(2)]


def _lanes(j):
    return slice(j * 128, (j + 1) * 128)


def _hg_scan_fwd(p_hg, lb3):
    dm = _dims()
    nh, nch, tt = dm["HG_H"], dm["NCH"], dm["TT"]
    pp = min(HG_GROUP, nh)
    lw_ = 128 * pp

    def body(q0, q1, i0, i1, f0, f1, lb0, lb1, o0, o1, ck_ref, s_ref):
        @pl.when(pl.program_id(1) == 0)
        def _():
            s_ref[...] = jnp.zeros_like(s_ref)

        keys, chains = [], []
        for d, refs in enumerate(((q0, i0, f0, lb0), (q1, i1, f1, lb1))):
            vals = [r[...] for r in refs]
            for j in range(pp):
                keys.append((d, j))
                chains.append((s_ref[d, j], *[a[:, _lanes(j)] for a in vals], d))
        for (d, j), chain, (out, s1) in zip(keys, chains, _hg_chunks(chains)):
            ck_ref[d, j] = chain[0]
            (o0, o1)[d][:, _lanes(j)] = out
            s_ref[d, j] = s1

    blk = lambda off: _dir_specs(lambda d: pl.BlockSpec((C, lw_), lambda h, s: (_chunk_of(d, s), off * nh // pp + h)))
    return _call(body, "hg_scan_fwd", (nh // pp, nch),
                 blk(0) + blk(1) + _dir_specs(lambda d: pl.BlockSpec((C, lw_), lambda h, s: (_chunk_of(d, s), (2 + d) * nh // pp + h)))
                 + _dir_specs(lambda d: pl.BlockSpec((None, 1, lw_), lambda h, s: (d, 0, h))),
                 _dir_specs(lambda d: pl.BlockSpec((C, lw_), lambda h, s: (_chunk_of(d, s), h)))
                 + [pl.BlockSpec((2, pp, None, 128, 128), lambda h, s: (0, h, s, 0, 0))],
                 [_sds((tt, HGW)), _sds((tt, HGW)), _sds((2, nh, nch, 128, 128))],
                 scratch=[pltpu.VMEM((2, pp, 128, 128), F32)], sem=("parallel", "arbitrary"))(*([p_hg] * 6), lb3, lb3)


def _hg_scan_bwd(p_hg, lb3, ck, do):
    dm = _dims()
    nh, nch, tt = dm["HG_H"], dm["NCH"], dm["TT"]
    pp = min(HG_GROUP, nh)
    lw_ = 128 * pp

    def body(q0, q1, i0, i1, f0, f1, lb0, lb1, ck_ref, do0, do1, dq0, dq1, di0, di1, df0, df1, dlb_ref, ds_ref):
        @pl.when(pl.program_id(1) == 0)
        def _():
            ds_ref[...] = jnp.zeros_like(ds_ref)
            dlb_ref[...] = jnp.zeros_like(dlb_ref)

        keys, flat, cts = [], [], []
        for d, refs in enumerate(((q0, i0, f0, lb0, do0), (q1, i1, f1, lb1, do1))):
            *vals, do_ = [r[...] for r in refs]
            for j in range(pp):
                keys.append((d, j))
                flat += [ck_ref[d, j], *[a[:, _lanes(j)] for a in vals]]
                cts.append((do_[:, _lanes(j)], ds_ref[d, j]))
        _, vjp = jax.vjp(_lockstep(_hg_chunks, [d for d, _ in keys], 5), *flat)
        grads = vjp(tuple(cts))
        for n, (d, j) in enumerate(keys):
            ds0, dq_, di_, df_, dlb = grads[5 * n:5 * n + 5]
            (dq0, dq1)[d][:, _lanes(j)] = dq_
            (di0, di1)[d][:, _lanes(j)] = di_
            (df0, df1)[d][:, _lanes(j)] = df_
            dlb_ref[d, :, _lanes(j)] += dlb
            ds_ref[d, j] = ds0

    ch = lambda d, s: _chunk_of(d, nch - 1 - s)
    blk = lambda off: _dir_specs(lambda d: pl.BlockSpec((C, lw_), lambda h, s: (ch(d, s), off * nh // pp + h)))
    tok = _dir_specs(lambda d: pl.BlockSpec((C, lw_), lambda h, s: (ch(d, s), h)))
    return _call(body, "hg_scan_bwd", (nh // pp, nch),
                 blk(0) + blk(1) + _dir_specs(lambda d: pl.BlockSpec((C, lw_), lambda h, s: (ch(d, s), (2 + d) * nh // pp + h)))
                 + _dir_specs(lambda d: pl.BlockSpec((None, 1, lw_), lambda h, s: (d, 0, h)))
                 + [pl.BlockSpec((2, pp, None, 128, 128), lambda h, s: (0, h, nch - 1 - s, 0, 0))] + tok,
                 tok * 3 + [pl.BlockSpec((2, 1, lw_), lambda h, s: (0, 0, h))],
                 [_sds((tt, HGW))] * 6 + [_sds((2, 1, HGW))],
                 scratch=[pltpu.VMEM((2, pp, 128, 128), F32)], sem=("parallel", "arbitrary"))(*([p_hg] * 6), lb3, lb3, ck, do, do)


def _rw_scan_fwd(sh, kap, lw, b, kd):
    dm = _dims()
    npair, nch, tt = dm["RW_P"], dm["NCH"], dm["TT"]
    pp = min(RW_GROUP, npair)
    lw_ = 128 * pp

    def body(r0, r1, v0, v1, ka0, ka1, lw0, lw1, b0, b1, kd0, kd1, y0, y1, ck_ref, tm_ref, s_ref):
        @pl.when(pl.program_id(1) == 0)
        def _():
            s_ref[...] = jnp.zeros_like(s_ref)

        keys, chains = [], []
        for d, refs in enumerate(((r0, lw0, ka0, b0, v0, kd0), (r1, lw1, ka1, b1, v1, kd1))):
            vals = [r[...] for r in refs]
            for j in range(pp):
                keys.append((d, j))
                chains.append((s_ref[d, j], *[a[:, _lanes(j)] for a in vals], d))
        for (d, j), chain, (out, s1, tm_a, tm_b) in zip(keys, chains, _rw_chunks(chains)):
            ck_ref[d, j] = chain[0]
            tm_ref[d, j, 0] = tm_a
            tm_ref[d, j, 1] = tm_b
            (y0, y1)[d][:, _lanes(j)] = out
            s_ref[d, j] = s1

    two = lambda off: _dir_specs(lambda d: pl.BlockSpec((C, lw_), lambda p, s: (_chunk_of(d, s), off * npair // pp + p)))
    three = _dir_specs(lambda d: pl.BlockSpec((None, C, lw_), lambda p, s: (d, _chunk_of(d, s), p)))
    return _call(body, "rw_scan_fwd", (npair // pp, nch), two(0) + two(2) + two(0) + three * 3,
                 two(0) + [pl.BlockSpec((2, pp, None, 128, 128), lambda p, s: (0, p, s, 0, 0)),
                           pl.BlockSpec((2, pp, None, 2, C, C), lambda p, s: (0, p, s, 0, 0, 0))],
                 [_sds((tt, RWW)), _sds((tt, RWW)), _sds((2, npair, nch, 128, 128)), _sds((2, npair, nch, 2, C, C))],
                 scratch=[pltpu.VMEM((2, pp, 128, 128), F32)], sem=("parallel", "arbitrary"))(
        sh, sh, sh, sh, kap, kap, lw, lw, b, b, kd, kd)


def _rw_scan_bwd(sh, kap, lw, b, kd, ck, tm_ck, dy):
    dm = _dims()
    npair, nch, tt = dm["RW_P"], dm["NCH"], dm["TT"]
    pp = min(RW_GROUP, npair)
    lw_ = 128 * pp

    def body(r0, r1, v0, v1, ka0, ka1, lw0, lw1, b0, b1, kd0, kd1, ck_ref, tm_ref, dy0, dy1,
             dr0, dr1, dv0, dv1, dka0, dka1, dlw0, dlw1, db0, db1, dkd0, dkd1, ds_ref):
        @pl.when(pl.program_id(1) == 0)
        def _():
            ds_ref[...] = jnp.zeros_like(ds_ref)

        keys, flat, cts, tms = [], [], [], {}
        for d, refs in enumerate(((r0, lw0, ka0, b0, v0, kd0, dy0), (r1, lw1, ka1, b1, v1, kd1, dy1))):
            *vals, dy_ = [r[...] for r in refs]
            for j in range(pp):
                tms[len(keys), 0], tms[len(keys), 1] = tm_ref[d, j, 0], tm_ref[d, j, 1]
                keys.append((d, j))
                flat += [ck_ref[d, j], *[a[:, _lanes(j)] for a in vals]]
                cts.append((dy_[:, _lanes(j)], ds_ref[d, j]))
        _, vjp = jax.vjp(_lockstep(functools.partial(_rw_chunks, tms=tms), [d for d, _ in keys], 7), *flat)
        grads = vjp(tuple(cts))
        outs = ((dr0, dlw0, dka0, db0, dv0, dkd0), (dr1, dlw1, dka1, db1, dv1, dkd1))
        for n, (d, j) in enumerate(keys):
            ds0, *g_in = grads[7 * n:7 * n + 7]
            for o_ref, g in zip(outs[d], g_in):
                o_ref[:, _lanes(j)] = g
            ds_ref[d, j] = ds0

    ch = lambda d, s: _chunk_of(d, nch - 1 - s)
    two = lambda off: _dir_specs(lambda d: pl.BlockSpec((C, lw_), lambda p, s: (ch(d, s), off * npair // pp + p)))
    three = _dir_specs(lambda d: pl.BlockSpec((None, C, lw_), lambda p, s: (d, ch(d, s), p)))
    return _call(body, "rw_scan_bwd", (npair // pp, nch),
                 two(0) + two(2) + two(0) + three * 3
                 + [pl.BlockSpec((2, pp, None, 128, 128), lambda p, s: (0, p, nch - 1 - s, 0, 0)),
                    pl.BlockSpec((2, pp, None, 2, C, C), lambda p, s: (0, p, nch - 1 - s, 0, 0, 0))] + two(0),
                 two(0) * 6, [_sds((tt, RWW))] * 12,
                 scratch=[pltpu.VMEM((2, pp, 128, 128), F32)], sem=("parallel", "arbitrary"))(
        sh, sh, sh, sh, kap, kap, lw, lw, b, b, kd, kd, ck, tm_ck, dy, dy)


def _tile_pos(i):
    dm = _dims()
    is_ctx = i < dm["NCT"]
    rows = lax.broadcasted_iota(jnp.int32, (TM, 1), 0)
    tl = rows + (i - dm["NCT"]) * TM
    width = jnp.where(is_ctx, TC, GW)
    assert TC & (TC - 1) == 0 and GW & (GW - 1) == 0 and TM % GW == 0
    colp = rows & (width - 1)
    return is_ctx, rows, tl, colp, width


def _shift_lr(cur, i):
    _, _, _, colp, width = _tile_pos(i)
    left = jnp.where(colp == 0, 0.0, pltpu.roll(cur, 1, 0))
    right = jnp.where(colp == width - 1, 0.0, pltpu.roll(cur, TM - 1, 0))
    return left, right


def _shift_ud(cur, prv, nxt, i):
    is_ctx, rows, tl, _, _ = _tile_pos(i)
    if TM > GW:
        up = jnp.where(rows >= GW, pltpu.roll(cur, GW, 0), pltpu.roll(prv, GW, 0))
        down = jnp.where(rows < TM - GW, pltpu.roll(cur, TM - GW, 0), pltpu.roll(nxt, TM - GW, 0))
    else:
        up, down = prv, nxt
    up = jnp.where(jnp.logical_or(is_ctx, tl < GW), 0.0, up)
    down = jnp.where(jnp.logical_or(is_ctx, tl >= T - GW), 0.0, down)
    return up, down


def _shift_specs():
    dm = _dims()
    nt = dm["TT"] // TM
    cw = max(w for w in range(128, 1664 + 1, 128) if dm["SH"] % w == 0)
    cur = pl.BlockSpec((TM, cw), lambda j, i: (i, j))
    prv = pl.BlockSpec((TM, cw), lambda j, i: (jnp.maximum(i - 1, 0), j))
    nxt = pl.BlockSpec((TM, cw), lambda j, i: (jnp.minimum(i + 1, nt - 1), j))
    mu = pl.BlockSpec((4, cw), lambda j, i: (0, j))
    return nt, cw, cur, prv, nxt, mu


def _shift_fwd(p_rw, mu):
    dm = _dims()
    nt, cw, cur, prv, nxt, mus = _shift_specs()

    def body(c_ref, p_ref, n_ref, mu_ref, o_ref):
        i = pl.program_id(1)
        p, m = c_ref[...], mu_ref[...]
        left, right = _shift_lr(p, i)
        up, down = _shift_ud(p, p_ref[...], n_ref[...], i)
        vert = jnp.where(i < dm["NCT"], 0.0, 1.0)
        o_ref[...] = (p * (1.0 - m[0:1] - m[1:2] - vert * (m[2:3] + m[3:4]))
                      + m[0:1] * left + m[1:2] * right + m[2:3] * up + m[3:4] * down)

    return _call(body, "shift_fwd", (dm["SH"] // cw, nt), [cur, prv, nxt, mus], cur, _sds((dm["TT"], dm["SH"])),
                 sem=("parallel", "parallel"))(p_rw, p_rw, p_rw, mu)


def _shift_bwd(p_rw, dsh, mu):
    dm = _dims()
    nt, cw, cur, prv, nxt, mus = _shift_specs()

    def body(c_ref, p_ref, n_ref, gc_ref, gp_ref, gn_ref, mu_ref, dp_ref, dmu_ref):
        i = pl.program_id(1)
        p, g, m = c_ref[...], gc_ref[...], mu_ref[...]
        vert = jnp.where(i < dm["NCT"], 0.0, 1.0)
        _, from_right = _shift_lr(m[0:1] * g, i)
        from_left, _ = _shift_lr(m[1:2] * g, i)
        _, from_down = _shift_ud(m[2:3] * g, g, m[2:3] * gn_ref[...], i)
        from_up, _ = _shift_ud(m[3:4] * g, m[3:4] * gp_ref[...], g, i)
        dp = g * (1.0 - m[0:1] - m[1:2] - vert * (m[2:3] + m[3:4])) + from_right + from_left + from_down + from_up
        dp_ref[...] = dp.astype(BF16)

        @pl.when(i == 0)
        def _():
            dmu_ref[...] = jnp.zeros_like(dmu_ref)

        left, right = _shift_lr(p, i)
        up, down = _shift_ud(p, p_ref[...], n_ref[...], i)
        s = lambda a: jnp.sum(a, axis=0, keepdims=True)
        dmu_ref[...] += jnp.concatenate([s(g * (left - p)), s(g * (right - p)), vert * s(g * (up - p)), vert * s(g * (down - p))], axis=0)

    return _call(body, "shift_bwd", (dm["SH"] // cw, nt), [cur, prv, nxt, cur, prv, nxt, mus],
                 [cur, pl.BlockSpec((4, cw), lambda j, i: (0, j))], [_sds((dm["TT"], dm["SH"]), BF16), _sds((4, dm["SH"]))],
                 sem=("parallel", "arbitrary"))(p_rw, p_rw, p_rw, dsh, dsh, dsh, mu)


def _acc_out(ref, first, vals):
    @pl.when(first)
    def _():
        for r in ref:
            r[...] = jnp.zeros_like(r)

    for r, v in zip(ref, vals):
        r[...] += v


def _rowsum(a):
    return jnp.sum(a, axis=0, keepdims=True)


def _rw_prep_specs():
    dm = _dims()
    rw = RWW // 128
    tok = lambda width, cb: pl.BlockSpec((TMV, width), lambda i: (i, cb))
    ins = [pl.BlockSpec((TMV, RWW), lambda i: (i, 1)), tok(128, 3 * rw), tok(128, 3 * rw + 1),
           _full((2, RWW)), _full((128, RWW)), _full((128, RWW)), _full((2, RWW)), _full((128, RWW)), _full((128, RWW)),
           _full((1, RWW)), _full((1, RWW))]
    return dm, ins


def _rw_prep_fwd(sh, w0, w2p0, w2p1, a0, a2p0, a2p1, kkp, kap_p):
    dm, ins = _rw_prep_specs()
    tt = dm["TT"]

    def body(k_ref, low_ref, loa_ref, w0_ref, w20_ref, w21_ref, a0_ref, a20_ref, a21_ref, kk_ref, ka_ref, kap_ref, lw_ref, b_ref, kd_ref):
        tw, la = jnp.tanh(low_ref[...]), loa_ref[...]
        kap, lw0, b0, kd0, lw1, b1, kd1 = _rw_prep_math(
            k_ref[...], _mm1(tw, w20_ref[...]), _mm1(tw, w21_ref[...]), _mm1(la, a20_ref[...]), _mm1(la, a21_ref[...]),
            w0_ref[...], a0_ref[...], kk_ref[...], ka_ref[...])
        kap_ref[...] = kap
        lw_ref[0], lw_ref[1] = lw0, lw1
        b_ref[0], b_ref[1] = b0, b1
        kd_ref[0], kd_ref[1] = kd0, kd1

    two = pl.BlockSpec((2, TMV, RWW), lambda i: (0, i, 0))
    return _call(body, "rw_prep_fwd", (tt // TMV,), ins, [pl.BlockSpec((TMV, RWW), lambda i: (i, 0)), two, two, two],
                 [_sds((tt, RWW)), _sds((2, tt, RWW)), _sds((2, tt, RWW)), _sds((2, tt, RWW))], sem=("parallel",))(
        sh, sh, sh, w0, w2p0, w2p1, a0, a2p0, a2p1, kkp, kap_p)


def _rw_prep_bwd(sh, w0, w2p0, w2p1, a0, a2p0, a2p1, kkp, kap_p, scan_grads, dkd_p, dr_p, dv_p):
    dm, ins = _rw_prep_specs()
    tt, sh_w = dm["TT"], dm["SH"]
    one = pl.BlockSpec((TMV, RWW), lambda i: (i, 0))

    def body(k_ref, low_ref, loa_ref, w0_ref, w20_ref, w21_ref, a0_ref, a20_ref, a21_ref, kk_ref, ka_ref,
             dr0, dr1, dv0, dv1, dka0, dka1, dlw0, dlw1, db0, db1, dkd0, dkd1, dkdp_ref, drp_ref, dvp_ref,
             dsh_ref, dw0_ref, dw20_ref, dw21_ref, da0_ref, da20_ref, da21_ref, dkk_ref, dka_ref):
        tw, la = jnp.tanh(low_ref[...]), loa_ref[...]
        w20, w21, a20, a21 = w20_ref[...], w21_ref[...], a20_ref[...], a21_ref[...]
        _, vjp = jax.vjp(_rw_prep_math, k_ref[...], _mm1(tw, w20), _mm1(tw, w21), _mm1(la, a20), _mm1(la, a21),
                         w0_ref[...], a0_ref[...], kk_ref[...], ka_ref[...])
        dkp = dkdp_ref[...]
        dk, dpw0, dpw1, dpa0, dpa1, dw0, da0, dkk, dka = vjp(
            (dka0[...] + dka1[...], dlw0[...], db0[...], dkd0[...] + dkp, dlw1[...], db1[...], dkd1[...] + dkp))
        twt, lat = tw.T, la.T
        dtw = _mm1_nt(dpw0, w20) + _mm1_nt(dpw1, w21)
        dsh_ref[:, 0:RWW] = dr0[...] + dr1[...] + drp_ref[...]
        dsh_ref[:, RWW:2 * RWW] = dk
        dsh_ref[:, 2 * RWW:3 * RWW] = dv0[...] + dv1[...] + dvp_ref[...]
        dsh_ref[:, 3 * RWW:3 * RWW + 128] = dtw * (1.0 - tw * tw)
        dsh_ref[:, 3 * RWW + 128:3 * RWW + 256] = _mm1_nt(dpa0, a20) + _mm1_nt(dpa1, a21)
        _acc_out((dw0_ref, dw20_ref, dw21_ref, da0_ref, da20_ref, da21_ref, dkk_ref, dka_ref), pl.program_id(0) == 0,
                 (dw0, _mm1(twt, dpw0), _mm1(twt, dpw1), da0, _mm1(lat, dpa0), _mm1(lat, dpa1), dkk, dka))

    par = [_full((2, RWW)), _full((128, RWW)), _full((128, RWW)), _full((2, RWW)), _full((128, RWW)), _full((128, RWW)),
           _full((1, RWW)), _full((1, RWW))]
    return _call(body, "rw_prep_bwd", (tt // TMV,), ins + [one] * 15,
                 [pl.BlockSpec((TMV, sh_w), lambda i: (i, 0))] + par,
                 [_sds((tt, sh_w)), _sds((2, RWW)), _sds((128, RWW)), _sds((128, RWW)), _sds((2, RWW)), _sds((128, RWW)), _sds((128, RWW)),
                  _sds((1, RWW)), _sds((1, RWW))],
                 sem=("arbitrary",))(sh, sh, sh, w0, w2p0, w2p1, a0, a2p0, a2p1, kkp, kap_p, *scan_grads, dkd_p, dr_p, dv_p)


def _rw_post_specs():
    dm = _dims()
    nv = dm["NCTV"]
    rw = RWW // 128
    lat3 = lambda d: pl.BlockSpec((None, TMV, RWW), lambda i: (d, nv + i, 0))
    lat2 = pl.BlockSpec((TMV, RWW), lambda i: (nv + i, 0))
    ins = [lat2, lat2, lat3(0), lat3(1),
           pl.BlockSpec((TMV, RWW), lambda i: (nv + i, 0)), pl.BlockSpec((TMV, RWW), lambda i: (nv + i, 2)),
           pl.BlockSpec((TMV, RWW), lambda i: (i, 0)), _full((1, RWW)), _full((1, RWW)), _full((1, RWW))]
    return dm, nv, ins


def _rw_post_fwd(y0_, y1_, kd, sh, p_z, rk, gng, gnb):
    dm, nv, ins = _rw_post_specs()

    def body(y0, y1, k0, k1, r, v, z, rk_ref, g_ref, b_ref, o_ref):
        o_ref[...] = _rw_post_math(y0[...], y1[...], k0[...], k1[...], r[...], v[...], z[...], rk_ref[...], g_ref[...], b_ref[...]).astype(BF16)

    return _call(body, "rw_post_fwd", (T // TMV,), ins, pl.BlockSpec((TMV, RWW), lambda i: (i, 0)), _sds((T, RWW), BF16),
                 sem=("parallel",))(y0_, y1_, kd, kd, sh, sh, p_z, rk, gng, gnb)


def _rw_post_bwd(y0_, y1_, kd, sh, p_z, rk, gng, gnb, dout):
    dm, nv, _ = _rw_post_specs()
    tt = dm["TT"]
    cl = lambda i: jnp.maximum(i - nv, 0)
    all3 = lambda d: pl.BlockSpec((None, TMV, RWW), lambda i: (d, i, 0))
    all2 = pl.BlockSpec((TMV, RWW), lambda i: (i, 0))
    ins = [all2, all2, all3(0), all3(1),
           pl.BlockSpec((TMV, RWW), lambda i: (i, 0)), pl.BlockSpec((TMV, RWW), lambda i: (i, 2)),
           pl.BlockSpec((TMV, RWW), lambda i: (cl(i), 0)), _full((1, RWW)), _full((1, RWW)), _full((1, RWW)),
           pl.BlockSpec((TMV, RWW), lambda i: (cl(i), 0))]
    tok = pl.BlockSpec((TMV, RWW), lambda i: (i, 0))

    def body(y0, y1, k0, k1, r, v, z, rk_ref, g_ref, b_ref, do_ref, dy_ref, dkd_ref, dr_ref, dv_ref, dz_ref, drk_ref, dg_ref, db_ref):
        i = pl.program_id(0)
        lat = jnp.where(i >= nv, 1.0, 0.0)
        _, vjp = jax.vjp(_rw_post_math, y0[...], y1[...], k0[...], k1[...], r[...], v[...], z[...], rk_ref[...], g_ref[...], b_ref[...])
        dy0, _, dk0, _, dr, dv, dz, drk, dg, db = vjp(do_ref[...] * lat)
        dy_ref[...] = dy0
        dkd_ref[...] = dk0
        dr_ref[...] = dr
        dv_ref[...] = dv
        dz_ref[...] = dz.astype(BF16)
        _acc_out((drk_ref, dg_ref, db_ref), i == 0, (drk, dg, db))

    return _call(body, "rw_post_bwd", (tt // TMV,), ins,
                 [tok, tok, tok, tok, pl.BlockSpec((TMV, RWW), lambda i: (cl(i), 0)), _full((1, RWW)), _full((1, RWW)), _full((1, RWW))],
                 [_sds((tt, RWW))] * 4 + [_sds((T, RWW), BF16), _sds((1, RWW)), _sds((1, RWW)), _sds((1, RWW))],
                 sem=("arbitrary",))(y0_, y1_, kd, kd, sh, sh, p_z, rk, gng, gnb, dout)


def _hg_post_fwd(o0_, o1_, p_hg, hg_g):
    dm = _dims()
    nv = dm["NCTV"]
    lat3 = lambda d: pl.BlockSpec((TMV, HGW), lambda i: (nv + i, 0))

    def body(o0, o1, z, g, out):
        out[...] = _hg_post_math(o0[...], o1[...], z[...], g[...]).astype(BF16)

    return _call(body, "hg_post_fwd", (T // TMV,), [lat3(0), lat3(1), pl.BlockSpec((TMV, HGW), lambda i: (nv + i, 4)), _full((1, HGW))],
                 pl.BlockSpec((TMV, HGW), lambda i: (i, 0)), _sds((T, HGW), BF16), sem=("parallel",))(o0_, o1_, p_hg, hg_g)


def _hg_post_bwd(o0_, o1_, p_hg, hg_g, dout):
    dm = _dims()
    nv, tt = dm["NCTV"], dm["TT"]
    all3 = lambda d: pl.BlockSpec((TMV, HGW), lambda i: (i, 0))
    tok = pl.BlockSpec((TMV, HGW), lambda i: (i, 0))

    def body(o0, o1, z, g, do_in, do_ref, dz_ref, dg_ref):
        i = pl.program_id(0)
        lat = jnp.where(i >= nv, 1.0, 0.0)
        _, vjp = jax.vjp(_hg_post_math, o0[...], o1[...], z[...], g[...])
        d0, _, dz, dg = vjp(do_in[...] * lat)
        do_ref[...] = d0
        dz_ref[...] = dz
        _acc_out((dg_ref,), i == 0, (dg,))

    return _call(body, "hg_post_bwd", (tt // TMV,),
                 [all3(0), all3(1), pl.BlockSpec((TMV, HGW), lambda i: (i, 4)), _full((1, HGW)),
                  pl.BlockSpec((TMV, HGW), lambda i: (jnp.maximum(i - nv, 0), 0))],
                 [tok, tok, _full((1, HGW))], [_sds((tt, HGW)), _sds((tt, HGW)), _sds((1, HGW))], sem=("arbitrary",))(o0_, o1_, p_hg, hg_g, dout)


def _hg_dproj(dq0, dq1, di0, di1, df0, df1, dz):
    dm = _dims()
    tt = dm["TT"]
    tok = pl.BlockSpec((TMV, HGW), lambda i: (i, 0))

    def body(q0, q1, i0, i1, f0, f1, z, o_ref):
        o_ref[:, 0:HGW] = (q0[...] + q1[...]).astype(BF16)
        o_ref[:, HGW:2 * HGW] = (i0[...] + i1[...]).astype(BF16)
        o_ref[:, 2 * HGW:3 * HGW] = f0[...].astype(BF16)
        o_ref[:, 3 * HGW:4 * HGW] = f1[...].astype(BF16)
        o_ref[:, 4 * HGW:5 * HGW] = z[...].astype(BF16)

    return _call(body, "hg_dproj", (tt // TMV,), [tok] * 7,
                 pl.BlockSpec((TMV, 5 * HGW), lambda i: (i, 0)), _sds((tt, 5 * HGW), BF16), sem=("parallel",))(
        dq0, dq1, di0, di1, df0, df1, dz)


def _merge1_fwd(out_hg, out_rw, whg_t, wrw_t, p_g):
    tok = pl.BlockSpec((TM, D), lambda i: (i, 0))

    def body(h_ref, r_ref, wh_ref, wr_ref, g1_ref, g2_ref, a_ref, b_ref, m_ref):
        nt = (((1,), (1,)), ((), ()))
        a = lax.dot_general(h_ref[...], wh_ref[...], nt, preferred_element_type=F32)
        b = lax.dot_general(r_ref[...], wr_ref[...], nt, preferred_element_type=F32)
        a_ref[...] = a
        b_ref[...] = b
        m_ref[...] = _gate_math(g1_ref[...], g2_ref[...], a, b).astype(BF16)

    return _call(body, "merge1_fwd", (T // TM,),
                 [pl.BlockSpec((TM, HGW), lambda i: (i, 0)), pl.BlockSpec((TM, RWW), lambda i: (i, 0)), _full((D, HGW)), _full((D, RWW)),
                  pl.BlockSpec((TM, D), lambda i: (i, 0)), pl.BlockSpec((TM, D), lambda i: (i, 1))],
                 [tok, tok, tok], [_sds((T, D)), _sds((T, D)), _sds((T, D), BF16)], sem=("parallel",))(out_hg, out_rw, whg_t, wrw_t, p_g, p_g)


def _head_fwd_bwd(x, merged, w_out, mods, final_g, tgt):
    tok = pl.BlockSpec((TM, D), lambda i: (i, 0))

    def body(x_ref, m_ref, w_ref, mods_ref, fg_ref, t_ref, dxo_ref, dmo_ref, loss_ref, dgate_ref, dfg_ref):
        mo = jnp.dot(m_ref[...], w_ref[...], preferred_element_type=F32)
        gate = mods_ref[0:1, 2 * D:3 * D]
        loss, vjp = jax.vjp(lambda x_, mo_, g_, fg_: _head_math(x_, mo_, g_, fg_, t_ref[...]), x_ref[...], mo, gate, fg_ref[...])
        dx, dmo, dgate, dfg = vjp(jnp.ones((1, 1), F32))
        dxo_ref[...] = dx
        dmo_ref[...] = dmo.astype(BF16)
        _acc_out((loss_ref, dgate_ref, dfg_ref), pl.program_id(0) == 0, (jnp.broadcast_to(loss, (1, 128)), dgate, dfg))

    return _call(body, "head_fwd_bwd", (T // TM,), [tok, tok, _full((D, D)), _full((8, 3 * D)), _full((1, D)), tok],
                 [tok, tok, _full((1, 128)), _full((1, D)), _full((1, D))],
                 [_sds((T, D)), _sds((T, D), BF16), _sds((1, 128)), _sds((1, D)), _sds((1, D))], sem=("arbitrary",))(
        x, merged, w_out, mods, final_g, tgt)


def _merge_bwd(dmo, w_out, p_g, a, b):
    tok = pl.BlockSpec((TM, D), lambda i: (i, 0))

    def body(dmo_ref, w_ref, g1_ref, g2_ref, a_ref, b_ref, da_ref, db_ref, dg_ref):
        dm_ = lax.dot_general(dmo_ref[...], w_ref[...], (((1,), (1,)), ((), ())), preferred_element_type=F32)
        _, vjp = jax.vjp(_gate_math, g1_ref[...], g2_ref[...], a_ref[...], b_ref[...])
        dg1, dg2, da, db = vjp(dm_)
        da_ref[...] = da.astype(BF16)
        db_ref[...] = db.astype(BF16)
        dg_ref[:, 0:D] = dg1.astype(BF16)
        dg_ref[:, D:2 * D] = dg2.astype(BF16)

    return _call(body, "merge_bwd", (T // TM,),
                 [tok, _full((D, D)), pl.BlockSpec((TM, D), lambda i: (i, 0)), pl.BlockSpec((TM, D), lambda i: (i, 1)), tok, tok],
                 [tok, tok, pl.BlockSpec((TM, 2 * D), lambda i: (i, 0))], [_sds((T, D), BF16), _sds((T, D), BF16), _sds((T, 2 * D), BF16)],
                 sem=("parallel",))(dmo, w_out, p_g, p_g, a, b)


def _h_bwd(xcat, mods, norm_g, dh_hg, dh_rw, dh_z, dh_g, dxo, dgate):
    dm = _dims()
    nct, tt = dm["NCT"], dm["TT"]
    tok = pl.BlockSpec((TM, D), lambda i: (i, 0))
    lat = pl.BlockSpec((TM, D), lambda i: (jnp.maximum(i - nct, 0), 0))

    def body(x_ref, m_ref, g_ref, d1, d2, d3, d4, dxo_ref, dgate_ref, gx_ref, dng_ref, dmods_ref):
        i = pl.program_id(0)
        is_ctx = i < nct
        latf = jnp.where(is_ctx, 0.0, 1.0)
        dh = d1[...] + d2[...] + latf * (d3[...] + d4[...])
        _, vjp = jax.vjp(_h_math, x_ref[...], g_ref[...], _sel_mod(m_ref, is_ctx, D), _sel_mod(m_ref, is_ctx, 0))
        dx, dng, dscale, dshift = vjp(dh)
        gx_ref[...] = dx + dxo_ref[...]

        @pl.when(i == 0)
        def _():
            dng_ref[...] = jnp.zeros_like(dng_ref)
            dmods_ref[...] = jnp.zeros_like(dmods_ref)
            dmods_ref[0:1, 2 * D:3 * D] = dgate_ref[...]

        dng_ref[...] += dng
        row = lax.broadcasted_iota(jnp.int32, (8, 1), 0)
        sel = jnp.where(row == jnp.where(is_ctx, 1, 0), 1.0, 0.0)
        dmods_ref[:, 0:D] += sel * dshift
        dmods_ref[:, D:2 * D] += sel * dscale

    return _call(body, "h_bwd", (tt // TM,), [tok, _full((8, 3 * D)), _full((1, D)), tok, tok, lat, lat, lat, _full((1, D))],
                 [lat, _full((1, D)), _full((8, 3 * D))], [_sds((T, D)), _sds((1, D)), _sds((8, 3 * D))], sem=("arbitrary",))(
        xcat, mods, norm_g, dh_hg, dh_rw, dh_z, dh_g, dxo, dgate)


def _lb_math(l0, l1):
    return jax.nn.sigmoid(l0 - l1)


def _lb_fwd(l0, l1):
    def body(a_ref, b_ref, o_ref):
        o_ref[...] = _lb_math(a_ref[...], b_ref[...])

    return _call(body, "lb_fwd", (1,), [_full((2, HGW))] * 2, _full((2, HGW)), _sds((2, HGW)))(l0, l1)


def _lb_bwd(l0, l1, dlb):
    def body(a_ref, b_ref, d_ref, da_ref, db_ref):
        _, vjp = jax.vjp(_lb_math, a_ref[...], b_ref[...])
        da_ref[...], db_ref[...] = vjp(d_ref[...])

    return _call(body, "lb_bwd", (1,), [_full((2, HGW))] * 3, [_full((2, HGW))] * 2, [_sds((2, HGW))] * 2)(l0, l1, dlb)


def _local_step(x, ctx, tgt, mods, w, start_exchange):
    dm = _dims()
    tt, sh_w, hgc = dm["TT"], dm["SH"], dm["HGC"]
    nb = lambda cols: cols // WB
    xcat = jnp.concatenate([ctx, x], axis=0)
    h = _h_fwd(xcat, mods, w["norm_g"])
    win_t = w["win_t"]
    p_hg = _proj_nt(h, win_t, 0, nb(hgc), 0, tt, "proj_hg")
    p_rw = _proj_nt(h, win_t, nb(hgc), nb(sh_w), 0, tt, "proj_rw")
    p_z = _proj_nt(h, win_t, nb(hgc + sh_w), nb(RWW), TC, T, "proj_z")
    p_g = _proj_nt(h, win_t, nb(hgc + sh_w + RWW), nb(2 * D), TC, T, "proj_g")

    lb3 = _lb_fwd(w["lb0"], w["lb1"]).reshape(2, 1, HGW)
    o0, o1, hg_ck = _hg_scan_fwd(p_hg, lb3)
    out_hg = _hg_post_fwd(o0, o1, p_hg, w["hg_g"])

    sh = _shift_fwd(p_rw, w["mu"])
    zpad = jnp.zeros((LORA, RWW), F32)
    prep_w = (w["w0"], jnp.concatenate([w["w2"][0], zpad], 0), jnp.concatenate([zpad, w["w2"][1]], 0),
              w["a0"], jnp.concatenate([w["a2"][0], zpad], 0), jnp.concatenate([zpad, w["a2"][1]], 0), w["kk"], w["ka"])
    kap, lw, b, kd = _rw_prep_fwd(sh, *prep_w)
    y0, y1, rw_ck, rw_tm = _rw_scan_fwd(sh, kap, lw, b, kd)
    post_w = (w["rk"], w["gng"], w["gnb"])
    out_rw = _rw_post_fwd(y0, y1, kd, sh, p_z, *post_w)

    a, bb, merged = _merge1_fwd(out_hg, out_rw, w["whg_t"], w["wrw_t"], p_g)
    dxo, dmo, loss, dgate, d_fg = _head_fwd_bwd(x, merged, w["wout"], mods, w["final_g"], tgt)
    da, db, dp_g = _merge_bwd(dmo, w["wout"], p_g, a, bb)
    g_wout = _dw_tn(merged, dmo, 0, D, 0, "dw_out")
    g_whg = _dw_tn(da, out_hg, 0, D, 0, "dw_hg")
    g_wrw = _dw_tn(db, out_rw, 0, D, 0, "dw_rw")
    token, early = start_exchange(dict(wout=g_wout, whg_t=g_whg, wrw_t=g_wrw), "early")
    d_out_hg = _mm_nn(da, w["whg_t"], 0, "dx_hg", after=token)
    d_out_rw = _mm_nn(db, w["wrw_t"], 0, "dx_rw")

    do, dz_hg, d_hg_g = _hg_post_bwd(o0, o1, p_hg, w["hg_g"], d_out_hg)
    *hg_grads, dlb3 = _hg_scan_bwd(p_hg, lb3, hg_ck, do)
    dp_hg = _hg_dproj(*hg_grads, dz_hg)
    d_lb0, d_lb1 = _lb_bwd(w["lb0"], w["lb1"], dlb3.reshape(2, HGW))

    dy, dkd_p, dr_p, dv_p, dp_z, d_rk, d_gng, d_gnb = _rw_post_bwd(y0, y1, kd, sh, p_z, *post_w, d_out_rw)
    rw_grads = _rw_scan_bwd(sh, kap, lw, b, kd, rw_ck, rw_tm, dy)
    dsh, d_w0, d_w2p0, d_w2p1, d_a0, d_a2p0, d_a2p1, d_kk, d_ka = _rw_prep_bwd(sh, *prep_w, rw_grads, dkd_p, dr_p, dv_p)
    dp_rw, d_mu = _shift_bwd(p_rw, dsh, w["mu"])

    g_win = _dw_tn(dp_hg, h, 0, dm["NCOLS"], 0, "dw_in_hg")
    g_win = _dw_tn(dp_rw, h, 0, dm["NCOLS"], nb(hgc), "dw_in_rw", prev=g_win)
    g_win = _dw_tn(dp_z, h, TC, dm["NCOLS"], nb(hgc + sh_w), "dw_in_z", prev=g_win)
    g_win = _dw_tn(dp_g, h, TC, dm["NCOLS"], nb(hgc + sh_w + RWW), "dw_in_g", prev=g_win)
    token, late = start_exchange(dict(win_t=g_win), "late")
    in_flight = (early, late)
    dh_hg = _mm_nn(dp_hg, win_t, 0, "dh_hg", after=token)
    dh_rw = _mm_nn(dp_rw, win_t, nb(hgc), "dh_rw")
    dh_z = _mm_nn(dp_z, win_t, nb(hgc + sh_w), "dh_z")
    dh_g = _mm_nn(dp_g, win_t, nb(hgc + sh_w + RWW), "dh_g")
    gx, d_ng, dmods = _h_bwd(xcat, mods, w["norm_g"], dh_hg, dh_rw, dh_z, dh_g, dxo, dgate)

    small = dict(norm_g=d_ng, lb0=d_lb0, lb1=d_lb1, hg_g=d_hg_g, mu=d_mu, w0=d_w0,
                 w2=jnp.stack([d_w2p0[:LORA], d_w2p1[LORA:]]), a0=d_a0, a2=jnp.stack([d_a2p0[:LORA], d_a2p1[LORA:]]),
                 kk=d_kk, ka=d_ka, rk=d_rk, gng=d_gng, gnb=d_gnb, final_g=d_fg)
    return loss, gx, dmods, in_flight, small


MESH = pl.DeviceIdType.MESH


def _comm_call(body, name, bufs, out_shapes, nsem, nloc):
    hbm = pl.BlockSpec(memory_space=pl.ANY)
    return pl.pallas_call(
        body, name=name, in_specs=[hbm] * len(bufs), out_specs=[hbm] * len(out_shapes), out_shape=out_shapes,
        scratch_shapes=[pltpu.SemaphoreType.DMA((nsem,)), pltpu.SemaphoreType.DMA((nsem,)), pltpu.SemaphoreType.DMA((nloc,))],
    )(*bufs)


GATHER_PIECE_BYTES = 4 * 1024 * 1024


def _gather2(bufs, name):
    pieces = []
    for bi, b in enumerate(bufs):
        rows = b.shape[0]
        if rows % 32 == 0 and b.size * b.dtype.itemsize > GATHER_PIECE_BYTES:
            pieces += [(bi, 0, rows // 2), (bi, rows // 2, rows // 2)]
        else:
            pieces.append((bi, 0, rows))
    nbuf = len(bufs)

    def body(*refs):
        ins, outs = refs[:nbuf], refs[nbuf:2 * nbuf]
        send_sems, recv_sems, _ = refs[2 * nbuf:]
        x, y, c = lax.axis_index("x"), lax.axis_index("y"), lax.axis_index("c")
        me, sib = (x, y, c), (x, y, 1 - c)
        flip = lambda a, b: a + b - 2 * a * b
        xn, yn, diag = (1 - x, y), (x, 1 - y), (1 - x, 1 - y)
        relay_from, relay_to = (flip(x, c), flip(1 - y, c)), (flip(1 - x, c), flip(y, c))

        def copy(pi, k, block, to, own=False):
            bi, r0, nr = pieces[pi]
            rows = outs[bi].at[4 * block[0] + 2 * block[1] + block[2], pl.ds(r0, nr)]
            return pltpu.make_async_remote_copy(src_ref=ins[bi].at[pl.ds(r0, nr)] if own else rows, dst_ref=rows,
                                                send_sem=send_sems.at[7 * pi + k], recv_sem=recv_sems.at[7 * pi + k],
                                                device_id=to, device_id_type=MESH)

        started = []
        for pi in range(len(pieces)):
            started += [copy(pi, 0, me, sib, own=True), copy(pi, 1, me, (*xn, c), own=True), copy(pi, 2, me, (*yn, c), own=True)]
        for cp in started:
            cp.start()
        for pi in range(len(pieces)):
            for k, chip in ((1, xn), (2, yn)):
                copy(pi, k, (*chip, c), me).wait_recv()
                started.append(copy(pi, 3 + k, (*chip, c), sib))
                started[-1].start()
            started.append(copy(pi, 3, (*relay_from, c), (*relay_to, c)))
            started[-1].start()
        for pi in range(len(pieces)):
            copy(pi, 3, (*diag, c), me).wait_recv()
            started.append(copy(pi, 6, (*diag, c), sib))
            started[-1].start()
        for pi in range(len(pieces)):
            copy(pi, 0, sib, me).wait_recv()
            for k, chip in ((4, xn), (5, yn), (6, diag)):
                copy(pi, k, (*chip, 1 - c), me).wait_recv()
        for cp in started:
            cp.wait_send()

    return _comm_call(body, name, bufs, [_sds((NDEV,) + b.shape, b.dtype) for b in bufs], 7 * len(pieces), 1)


def _gather_all(bufs, name):
    me = 4 * lax.axis_index("x") + 2 * lax.axis_index("y") + lax.axis_index("c")
    return [lax.dynamic_update_index_in_dim(g, b, me, axis=0) for g, b in zip(_gather2(bufs, name), bufs)]


def _pair_exchange(bufs, name):
    nbuf = len(bufs)

    def body(*refs):
        ins, got = refs[:nbuf], refs[nbuf:2 * nbuf]
        send_sems, recv_sems, _ = refs[2 * nbuf:]
        x, y, c = lax.axis_index("x"), lax.axis_index("y"), lax.axis_index("c")
        copies = []
        for bi in range(nbuf):
            for q in range(4):
                copies.append(pltpu.make_async_remote_copy(
                    src_ref=ins[bi].at[2 * q + 1 - c], dst_ref=got[bi].at[q], send_sem=send_sems.at[4 * bi + q],
                    recv_sem=recv_sems.at[4 * bi + q], device_id=(x, y, 1 - c), device_id_type=MESH))
                copies[-1].start()
        for cp in copies:
            cp.wait_send()
            cp.wait_recv()

    return _comm_call(body, name, bufs, [_sds((4,) + b.shape[1:], b.dtype) for b in bufs], 4 * nbuf, 1)


def _chip_copies(srcs, lands, send_sems, recv_sems):
    x, y, c = lax.axis_index("x"), lax.axis_index("y"), lax.axis_index("c")
    myq = 2 * x + y
    pairs = []
    for bi in range(len(srcs)):
        for j, (qx, qy) in enumerate([(1 - x, y), (x, 1 - y), (1 - x, 1 - y)]):
            q = 2 * qx + qy
            mk = lambda src, dst: pltpu.make_async_remote_copy(
                src_ref=src, dst_ref=dst, send_sem=send_sems.at[3 * bi + j], recv_sem=recv_sems.at[3 * bi + j],
                device_id=(qx, qy, c), device_id_type=MESH)
            pairs.append((mk(srcs[bi].at[q], lands[bi].at[myq]), mk(srcs[bi].at[myq], lands[bi].at[q])))
    return pairs


def _chip_exchange_start(bufs, name):
    nbuf = len(bufs)
    hbm, sem = pl.BlockSpec(memory_space=pltpu.HBM), pl.BlockSpec(memory_space=pltpu.SEMAPHORE)

    def body(*refs):
        srcs, lands = refs[:nbuf], refs[nbuf:2 * nbuf]
        send_sems, recv_sems = refs[2 * nbuf], refs[2 * nbuf + 1]
        token = refs[-1]
        for send, _ in _chip_copies(srcs, lands, send_sems, recv_sems):
            send.start()
        token[...] = jnp.zeros_like(token)

    held = [pltpu.HBM(b.shape, b.dtype) for b in bufs]
    outs = pl.pallas_call(
        body, name=name, in_specs=[hbm] * (2 * nbuf), out_specs=(sem, sem, *[hbm] * (2 * nbuf), pl.BlockSpec(memory_space=pltpu.VMEM)),
        out_shape=(pltpu.SemaphoreType.DMA((3 * nbuf,)), pltpu.SemaphoreType.DMA((3 * nbuf,)), *held, *held, _sds((8, 128))),
        input_output_aliases={i: 2 + i for i in range(2 * nbuf)},
        compiler_params=pltpu.CompilerParams(has_side_effects=pltpu.SideEffectType.DATAFLOW_SIDE_EFFECTING),
    )(*[pltpu.with_memory_space_constraint(b, pltpu.HBM) for b in bufs],
      *[pltpu.with_memory_space_constraint(lax.empty(b.shape, b.dtype), pltpu.HBM) for b in bufs])
    return outs[0], outs[1], list(outs[2:2 + nbuf]), list(outs[2 + nbuf:2 + 2 * nbuf]), outs[-1]


def _chip_exchange_wait(send_sems, recv_sems, srcs, lands, after, name):
    nbuf = len(srcs)
    hbm, sem = pl.BlockSpec(memory_space=pltpu.HBM), pl.BlockSpec(memory_space=pltpu.SEMAPHORE)

    def body(*refs):
        src_refs, land_refs = refs[:nbuf], refs[nbuf:2 * nbuf]
        for send, recv in _chip_copies(src_refs, land_refs, refs[2 * nbuf], refs[2 * nbuf + 1]):
            send.wait_send()
            recv.wait_recv()

    held = [pltpu.HBM(b.shape, b.dtype) for b in srcs]
    outs = pl.pallas_call(
        body, name=name, in_specs=[hbm] * (2 * nbuf) + [sem, sem, pl.BlockSpec(memory_space=pl.ANY)], out_specs=[hbm] * (2 * nbuf),
        out_shape=(*held, *held), input_output_aliases={i: i for i in range(2 * nbuf)},
        compiler_params=pltpu.CompilerParams(has_side_effects=pltpu.SideEffectType.DATAFLOW_SIDE_EFFECTING),
    )(*srcs, *lands, send_sems, recv_sems, after)
    return list(outs[:nbuf]), list(outs[nbuf:])


def _prefetch_call(body, name, scalars, grid, in_specs, out_specs, out_shape, args):
    return pl.pallas_call(
        body, name=name, out_shape=out_shape,
        grid_spec=pltpu.PrefetchScalarGridSpec(num_scalar_prefetch=1, grid=grid, in_specs=in_specs, out_specs=out_specs),
        compiler_params=pltpu.CompilerParams(dimension_semantics=("parallel",) * len(grid), vmem_limit_bytes=V7X_VMEM_LIMIT))(scalars, *args)


def _pair_add(g, got, core, name):
    _, rows, cols = got.shape
    tr = _row_tile(rows, 16, 256)

    def body(c_ref, a_ref, b_ref, o_ref):
        o_ref[...] = (a_ref[...].astype(F32) + b_ref[...].astype(F32)).astype(o_ref.dtype)

    blk = pl.BlockSpec((4, tr, cols), lambda i, c_ref: (0, i, 0))
    mine = pl.BlockSpec((4, None, tr, cols), lambda i, c_ref: (0, c_ref[0], i, 0))
    return _prefetch_call(body, name, core, (rows // tr,), [mine, blk], blk, _sds(got.shape, got.dtype),
                          (g.reshape(4, 2, rows, cols), got))


def _reduce_chips(chip_sums, recv, slots, name):
    _, rows, cols = recv.shape
    tr = _row_tile(rows, 16, 128)

    def body(s_ref, own_ref, r1_ref, r2_ref, r3_ref, o_ref):
        o_ref[...] = ((own_ref[...].astype(F32) + r1_ref[...].astype(F32)) + r2_ref[...].astype(F32)) + r3_ref[...].astype(F32)

    pick = lambda k: pl.BlockSpec((None, tr, cols), lambda i, s_ref: (s_ref[k], i, 0))
    return _prefetch_call(body, name, slots, (rows // tr,), [pick(0), pick(1), pick(2), pick(3)],
                          pl.BlockSpec((tr, cols), lambda i, s_ref: (i, 0)), _sds((rows, cols)), (chip_sums, recv, recv, recv))


def _row_tile(rows, mult, cap):
    best = mult
    for t in range(mult, cap + 1, mult):
        if rows % t == 0:
            best = t
    assert rows % best == 0, (rows, mult)
    return best


def _reduce_sources(r, name):
    nsrc, rows, cols = r.shape
    tr = _row_tile(rows, 16 if r.dtype.itemsize == 2 else 8, 128)

    def body(r_ref, o_ref):
        acc = r_ref[0].astype(F32)
        for j in range(1, nsrc):
            acc = acc + r_ref[j].astype(F32)
        o_ref[...] = acc

    return _call(body, name, (rows // tr,), [pl.BlockSpec((nsrc, tr, cols), lambda i: (0, i, 0))],
                 pl.BlockSpec((tr, cols), lambda i: (i, 0)), _sds((rows, cols)), sem=("parallel",))(r)


def _adamw(w, g, m, v, name):
    rows, cols = w.shape
    tr = _row_tile(rows, 8, 128)
    c1 = 1.0 - ADAM_B1 ** ADAM_STEP
    c2 = 1.0 - ADAM_B2 ** ADAM_STEP

    def body(w_ref, g_ref, m_ref, v_ref, d_ref, mo_ref, vo_ref):
        g_ = g_ref[...]
        m_ = ADAM_B1 * m_ref[...] + (1.0 - ADAM_B1) * g_
        v_ = ADAM_B2 * v_ref[...] + (1.0 - ADAM_B2) * (g_ * g_)
        d_ref[...] = -ADAM_LR * ((m_ / c1) / (jnp.sqrt(v_ / c2) + ADAM_EPS) + ADAM_WD * w_ref[...])
        mo_ref[...] = m_
        vo_ref[...] = v_

    blk = pl.BlockSpec((tr, cols), lambda i: (i, 0))
    return _call(body, name, (rows // tr,), [blk] * 4, [blk] * 3, [_sds((rows, cols))] * 3, sem=("parallel",))(w, g, m, v)


SLAB_PART = 1024


def _pack(arrs):
    parts = []
    for a in arrs:
        flat = a.reshape(-1)
        pad = (-flat.shape[0]) % SLAB_PART
        if pad:
            flat = jnp.concatenate([flat, jnp.zeros((pad,), flat.dtype)])
        parts.append(flat.reshape(-1, 128))
    return jnp.concatenate(parts, axis=0)


def _unpack(slab, shapes):
    outs, row = [], 0
    for s in shapes:
        n = 1
        for d in s:
            n *= d
        rows = (n + SLAB_PART - 1) // SLAB_PART * (SLAB_PART // 128)
        outs.append(slab[row:row + rows].reshape(-1)[:n].reshape(s))
        row += rows
    return outs


def _unshard_last(g, lead):
    nl = len(lead)
    return jnp.transpose(g, tuple(range(1, nl + 1)) + (0, nl + 1)).reshape(tuple(lead) + (-1,))


def kernel(x, c, ctx, c_ctx, ada_w, ada_b, norm_g, w_in, hg_lb, hg_norm_g, rw_mu, rw_w0, rw_w2, rw_a0, rw_a2, rw_kk, rw_ka, rw_rk, rw_gn_g, rw_gn_b, w_hg_out, w_rw_out, w_out, final_g, loss_target, m_c_ctx, m_ada_w, m_ada_b, m_norm_g, m_w_in, m_hg_lb, m_hg_norm_g, m_rw_mu, m_rw_w0, m_rw_w2, m_rw_a0, m_rw_a2, m_rw_kk, m_rw_ka, m_rw_rk, m_rw_gn_g, m_rw_gn_b, m_w_hg_out, m_w_rw_out, m_w_out, m_final_g, v_c_ctx, v_ada_w, v_ada_b, v_norm_g, v_w_in, v_hg_lb, v_hg_norm_g, v_rw_mu, v_rw_w0, v_rw_w2, v_rw_a0, v_rw_a2, v_rw_kk, v_rw_ka, v_rw_rk, v_rw_gn_g, v_rw_gn_b, v_w_hg_out, v_w_rw_out, v_w_out, v_final_g):
    dm = _dims()
    me = 4 * lax.axis_index("x") + 2 * lax.axis_index("y") + lax.axis_index("c")

    big_shards = [w_in[0].T.astype(BF16), w_out[0].astype(BF16), w_hg_out[0].T.astype(BF16), w_rw_out[0].T.astype(BF16)]
    sharded_small = [hg_lb, rw_mu[0], rw_w0[0], rw_w2[0], rw_a0[0], rw_a2[0]]
    c_rows = jnp.concatenate([c, jnp.zeros((7, D), F32)], axis=0)
    gathered = _gather_all(big_shards + [_pack(sharded_small), c_rows], "gather_weights")
    win_t, wout, whg_t, wrw_t = [g.reshape(-1, g.shape[2]) for g in gathered[:4]]
    per_dev = jax.vmap(lambda s: _unpack(s, [a.shape for a in sharded_small]))(gathered[4])
    hg_lb_f, mu_f, w0_f, w2_f, a0_f, a2_f = [_unshard_last(p, p.shape[1:-1]) for p in per_dev]

    ncol = ada_w.shape[2]
    cc16 = jnp.concatenate([gathered[5][:, 0], c_ctx[None], jnp.zeros((7, D), F32)], axis=0)
    mod_all = _gather_all([_ada_fwd(cc16, ada_w[0])], "gather_mods")[0]
    mod_x = lax.dynamic_index_in_dim(mod_all, me, axis=1, keepdims=False).reshape(1, -1) + ada_b
    mod_c = mod_all[:, NDEV].reshape(1, -1) + ada_b
    mods = jnp.concatenate([mod_x, mod_c, jnp.zeros((6, 3 * D), F32)], axis=0)

    w = dict(win_t=win_t, wout=wout, whg_t=whg_t, wrw_t=wrw_t, norm_g=norm_g,
             lb0=hg_lb_f[:, 0], lb1=hg_lb_f[:, 1], hg_g=hg_norm_g, mu=mu_f, w0=w0_f, w2=w2_f, a0=a0_f, a2=a2_f,
             kk=rw_kk, ka=rw_ka, rk=rw_rk, gng=rw_gn_g, gnb=rw_gn_b, final_g=final_g[None])
    def start_exchange(grads_bf16, tag):
        names = list(grads_bf16)
        blocks = [grads_bf16[n].reshape(NDEV, -1, grads_bf16[n].shape[1]) for n in names]
        got = _pair_exchange(blocks, "pair_grads_" + tag)
        core = lax.axis_index("c").astype(jnp.int32).reshape(1)
        chip_sums = [_pair_add(g, r, core, "pair_add_" + n) for g, r, n in zip(blocks, got, names)]
        send_sems, recv_sems, held, lands, token = _chip_exchange_start(chip_sums, "chip_grads_start_" + tag)
        return token, (names, send_sems, recv_sems, held, lands)

    loss_dev, grad_x, dmods, in_flight, small = _local_step(x[0], ctx[0], loss_target[0], mods, w, start_exchange)
    loss = lax.psum(loss_dev[0, 0], AXES)

    dmods_all = _gather_all([dmods], "gather_dmods")[0]
    my_cols = lambda a: lax.dynamic_slice_in_dim(a, me * ncol, ncol, axis=1)
    g_ada, dcc16, d_ada_b = _ada_bwd(cc16, ada_w[0], my_cols(dmods_all[:, 0]), my_cols(dmods_all[:, 1]), dmods)
    small = dict(small, c_ctx=dcc16[NDEV:NDEV + 1], ada_b=d_ada_b)

    small_names = ["c_ctx", "ada_b", "norm_g", "lb0", "lb1", "hg_g", "mu", "w0", "w2", "a0", "a2", "kk", "ka", "rk", "gng", "gnb", "final_g"]
    small_slab = _pack([small[n] for n in small_names])
    small_all = _gather_all([small_slab], "gather_small_grads")[0]
    small_sum = _reduce_sources(small_all, "reduce_small")
    sg = dict(zip(small_names, _unpack(small_sum, [small[n].shape for n in small_names])))

    my_chip = (2 * lax.axis_index("x") + lax.axis_index("y")).astype(jnp.int32)
    others = jnp.arange(3, dtype=jnp.int32)
    slots = jnp.concatenate([my_chip.reshape(1), others + (others >= my_chip).astype(jnp.int32)])
    reduced = {}
    for tag, (names, *copies) in zip(("early", "late"), in_flight):
        chip_sums, recv = _chip_exchange_wait(*copies, small_sum, "chip_grads_wait_" + tag)
        reduced.update({n: _reduce_chips(cs, r, slots, "reduce_" + n) for n, cs, r in zip(names, chip_sums, recv)})
    g_win_t, g_wout, g_whg_t, g_wrw_t = [reduced[n] for n in ("win_t", "wout", "whg_t", "wrw_t")]

    def my_shard(full):
        n = full.shape[-1] // NDEV
        return lax.dynamic_slice_in_dim(full, me * n, n, axis=full.ndim - 1)

    grads = dict(
        c_ctx=sg["c_ctx"][0], ada_w=g_ada[None], ada_b=sg["ada_b"], norm_g=sg["norm_g"], w_in=g_win_t.T[None],
        hg_lb=my_shard(jnp.stack([sg["lb0"], sg["lb1"]], axis=1)), hg_norm_g=sg["hg_g"], rw_mu=my_shard(sg["mu"])[None],
        rw_w0=my_shard(sg["w0"])[None], rw_w2=my_shard(sg["w2"])[None], rw_a0=my_shard(sg["a0"])[None], rw_a2=my_shard(sg["a2"])[None],
        rw_kk=sg["kk"], rw_ka=sg["ka"], rw_rk=sg["rk"], rw_gn_g=sg["gng"], rw_gn_b=sg["gnb"],
        w_hg_out=g_whg_t.T[None], w_rw_out=g_wrw_t.T[None], w_out=g_wout[None], final_g=sg["final_g"][0])
    weights = dict(c_ctx=c_ctx, ada_w=ada_w, ada_b=ada_b, norm_g=norm_g, w_in=w_in, hg_lb=hg_lb, hg_norm_g=hg_norm_g, rw_mu=rw_mu,
                   rw_w0=rw_w0, rw_w2=rw_w2, rw_a0=rw_a0, rw_a2=rw_a2, rw_kk=rw_kk, rw_ka=rw_ka, rw_rk=rw_rk, rw_gn_g=rw_gn_g,
                   rw_gn_b=rw_gn_b, w_hg_out=w_hg_out, w_rw_out=w_rw_out, w_out=w_out, final_g=final_g)
    m_in = dict(zip(weights, (m_c_ctx, m_ada_w, m_ada_b, m_norm_g, m_w_in, m_hg_lb, m_hg_norm_g, m_rw_mu, m_rw_w0, m_rw_w2, m_rw_a0,
                              m_rw_a2, m_rw_kk, m_rw_ka, m_rw_rk, m_rw_gn_g, m_rw_gn_b, m_w_hg_out, m_w_rw_out, m_w_out, m_final_g)))
    v_in = dict(zip(weights, (v_c_ctx, v_ada_w, v_ada_b, v_norm_g, v_w_in, v_hg_lb, v_hg_norm_g, v_rw_mu, v_rw_w0, v_rw_w2, v_rw_a0,
                              v_rw_a2, v_rw_kk, v_rw_ka, v_rw_rk, v_rw_gn_g, v_rw_gn_b, v_w_hg_out, v_w_rw_out, v_w_out, v_final_g)))

    big_w = ["ada_w", "w_in", "w_hg_out", "w_rw_out", "w_out"]
    delta, new_m, new_v = {}, {}, {}
    for n in big_w:
        shp = weights[n].shape
        two = lambda a: a.reshape(shp[-2], shp[-1])
        d_, m_, v_ = _adamw(two(weights[n]), two(grads[n]), two(m_in[n]), two(v_in[n]), "adamw_" + n)
        delta[n], new_m[n], new_v[n] = d_.reshape(shp), m_.reshape(shp), v_.reshape(shp)
    rest = [n for n in weights if n not in big_w]
    shapes = [weights[n].shape for n in rest]
    d_s, m_s, v_s = _adamw(_pack([weights[n] for n in rest]), _pack([grads[n] for n in rest]), _pack([m_in[n] for n in rest]),
                           _pack([v_in[n] for n in rest]), "adamw_small")
    for n, d_, m_, v_ in zip(rest, _unpack(d_s, shapes), _unpack(m_s, shapes), _unpack(v_s, shapes)):
        delta[n], new_m[n], new_v[n] = d_, m_, v_

    order = list(weights)
    return (loss, grad_x[None], *[grads[n] for n in order], *[delta[n] for n in order],
            *[new_m[n] for n in order], *[new_v[n] for n in order])
```

```python
import functools

import jax
import jax.numpy as jnp
from jax import lax
from jax.experimental import pallas as pl
from jax.experimental.pallas import tpu as pltpu

D = 2048
T = 2048
TC = 256
GW = 64
C = 64
HGW = 1024
RWW = 1024
LORA = 64
NDEV = 8
TM = 256
TMV = 128
NORM_EPS = 1e-6
RW_GN_EPS = 64e-5
ADAM_LR, ADAM_B1, ADAM_B2, ADAM_EPS, ADAM_WD, ADAM_STEP = 0.001, 0.9, 0.999, 1e-08, 0.01, 10
WB = 256
V7X_VMEM_LIMIT = 56 * 1024 * 1024

F32 = jnp.float32
BF16 = jnp.bfloat16
AXES = ("x", "y", "c")


def _dims():
    tt = T + TC
    sh = 3 * RWW + 4 * LORA
    hgc = 5 * HGW
    return dict(TT=tt, NCC=TC // C, NCH=tt // C, HG_H=HGW // 128, RW_P=RWW // 128, SH=sh, HGC=hgc,
                NCOLS=hgc + sh + RWW + 2 * D, NCT=TC // TM, NCTV=TC // TMV)


def _mm(a, b):
    return _dot3(a, b, "nn")


def _mm_nt(a, b):
    return _dot3(a, b, "nt")


def _mm_tn(a, b):
    return _dot3(a, b, "tn")


def _split(x):
    hi = x.astype(BF16)
    return hi, (x - hi.astype(F32)).astype(BF16)


_FORMS = {"nn": (((1,), (0,)), ((), ())), "nt": (((1,), (1,)), ((), ())), "tn": (((0,), (0,)), ((), ()))}


def _dot3_raw(a, b, form):
    ah, al = _split(a)
    bh, bl = _split(b)
    d = lambda x, y: lax.dot_general(x, y, _FORMS[form], preferred_element_type=F32)
    return d(ah, bh) + (d(ah, bl) + d(al, bh))


@functools.partial(jax.custom_vjp, nondiff_argnums=(2,))
def _dot3(a, b, form):
    return _dot3_raw(a, b, form)


def _dot3_fwd(a, b, form):
    return _dot3_raw(a, b, form), (a, b)


def _dot3_bwd(form, res, g):
    a, b = res
    if form == "nn":
        return _dot3(g, b, "nt"), _dot3(a, g, "tn")
    if form == "nt":
        return _dot3(g, b, "nn"), _dot3(g, a, "tn")
    return _dot3(b, g, "nt"), _dot3(a, g, "nn")


_dot3.defvjp(_dot3_fwd, _dot3_bwd)


def _scan_cumsum(inc, x):
    return _cumsum_vjp(inc.astype(BF16), x)


def _cumsum_raw(inc, x, form):
    h1 = x.astype(BF16)
    r1 = x - h1.astype(F32)
    h2 = r1.astype(BF16)
    h3 = (r1 - h2.astype(F32)).astype(BF16)
    d = lambda y: lax.dot_general(inc, y, _FORMS[form], preferred_element_type=F32)
    return d(h1) + (d(h2) + d(h3))


@jax.custom_vjp
def _cumsum_vjp(inc, x):
    return _cumsum_raw(inc, x, "nn")


_cumsum_vjp.defvjp(lambda inc, x: (_cumsum_raw(inc, x, "nn"), inc),
                   lambda inc, g: (jnp.zeros_like(inc), _cumsum_raw(inc, g, "tn")))


def _silu(x):
    return x * jax.nn.sigmoid(x)


def _softplus(x):
    return jnp.maximum(x, 0.0) + jnp.log(1.0 + jnp.exp(-jnp.abs(x)))


def _iota2(shape, dim):
    return lax.broadcasted_iota(jnp.int32, shape, dim)


def _pair_mask():
    return (_iota2((128, 128), 0) < 64) == (_iota2((128, 128), 1) < 64)


def _seg64_raw(x):
    e = _pair_mask().astype(BF16)
    parts = []
    for g in range(x.shape[1] // 128):
        hi, lo = _split(x[:, g * 128:(g + 1) * 128])
        parts.append(jnp.dot(hi, e, preferred_element_type=F32) + jnp.dot(lo, e, preferred_element_type=F32))
    return parts[0] if len(parts) == 1 else jnp.concatenate(parts, axis=1)


@jax.custom_vjp
def _seg64_sum(x):
    return _seg64_raw(x)


_seg64_sum.defvjp(lambda x: (_seg64_raw(x), None), lambda _, g: (_seg64_raw(g),))


def _mm1(a, b):
    return jnp.dot(a.astype(BF16), b.astype(BF16), preferred_element_type=F32)


def _mm1_nt(a, b):
    return lax.dot_general(a.astype(BF16), b.astype(BF16), (((1,), (1,)), ((), ())), preferred_element_type=F32)


def _seg128_mean(x):
    parts = [jnp.broadcast_to(jnp.mean(x[:, g * 128:(g + 1) * 128], axis=1, keepdims=True), (x.shape[0], 128))
             for g in range(x.shape[1] // 128)]
    return parts[0] if len(parts) == 1 else jnp.concatenate(parts, axis=1)


def _scan_masks(d):
    lag = (_iota2((C, C), 0) - _iota2((C, C), 1)) * (1 - 2 * d)
    return lag >= 0, lag > 0


def _tri_inverse(a):
    eye = (_iota2((C, C), 0) == _iota2((C, C), 1)).astype(F32)
    p = [-x for x in a]
    tm = [eye + x for x in p]
    n = 2
    while n < C:
        p = [_mm(x, x) for x in p]
        tm = [t + _mm(t, x) for t, x in zip(tm, p)]
        n *= 2
    return tm


@jax.custom_vjp
def _tri_solve(tm, a, rhs):
    return [_mm(t, r) for t, r in zip(tm, rhs)]


def _tri_solve_fwd(tm, a, rhs):
    u = [_mm(t, r) for t, r in zip(tm, rhs)]
    return u, (tm, u)


def _tri_solve_bwd(res, g):
    tm, u = res
    d_rhs = [_mm_tn(t, x) for t, x in zip(tm, g)]
    return [jnp.zeros_like(t) for t in tm], [-_mm_nt(d, x) for d, x in zip(d_rhs, u)], d_rhs


_tri_solve.defvjp(_tri_solve_fwd, _tri_solve_bwd)


def _rw_chunks(chains, tms=None):
    s0, r, lw, kap, b, v, kd, ds = zip(*chains)
    ids = range(len(chains))
    inc, strict = zip(*[_scan_masks(d) for d in ds])
    lane = _iota2((1, 128), 1)
    hm = ((lane < 64).astype(F32), (lane >= 64).astype(F32))
    lc = [_scan_cumsum(inc[i], lw[i]) for i in ids]
    ltot = [jnp.sum(lw[i], axis=0, keepdims=True) for i in ids]
    rt = [r[i] * jnp.exp(lc[i]) for i in ids]
    kt = [kap[i] * jnp.exp(lc[i] - lw[i]) for i in ids]
    einv = [jnp.exp(-lc[i]) for i in ids]
    kh = [kd[i] * einv[i] for i in ids]
    bh = [b[i] * einv[i] for i in ids]
    eend = [jnp.exp(ltot[i] - lc[i]) for i in ids]
    kbar = [kd[i] * eend[i] for i in ids]
    bbar = [b[i] * eend[i] for i in ids]
    lhs = [jnp.concatenate([kt[i] * hm[0], kt[i] * hm[1], rt[i] * hm[0], rt[i] * hm[1]], axis=0) for i in ids]
    gk = [_mm_nt(lhs[i], kh[i]) for i in ids]
    gb = [_mm_nt(lhs[i], bh[i]) for i in ids]
    ks = [_mm_nt(kt[i], s0[i]) for i in ids]
    ys = [_mm_nt(rt[i], s0[i]) for i in ids]
    heads = [(i, h) for i in ids for h in range(2)]
    akk = {c: jnp.where(strict[c[0]], gk[c[0]][c[1] * C:(c[1] + 1) * C], 0.0) for c in heads}
    akb = {c: jnp.where(strict[c[0]], gb[c[0]][c[1] * C:(c[1] + 1) * C], 0.0) for c in heads}
    ark = {c: jnp.where(inc[c[0]], gk[c[0]][(2 + c[1]) * C:(3 + c[1]) * C], 0.0) for c in heads}
    arb = {c: jnp.where(inc[c[0]], gb[c[0]][(2 + c[1]) * C:(3 + c[1]) * C], 0.0) for c in heads}
    vh = {c: v[c[0]] * hm[c[1]] for c in heads}
    av = {c: _mm(akk[c], vh[c]) for c in heads}
    rhs = {c: ks[c[0]] * hm[c[1]] + av[c] for c in heads}
    akb_list = [akb[c] for c in heads]
    tm = _tri_inverse(akb_list) if tms is None else [tms[c] for c in heads]
    uh = dict(zip(heads, _tri_solve(tm, akb_list, [rhs[c] for c in heads])))
    yv = {c: _mm(ark[c], vh[c]) for c in heads}
    yu = {c: _mm(arb[c], uh[c]) for c in heads}
    u = [uh[i, 0] + uh[i, 1] for i in ids]
    y = [ys[i] + (yv[i, 0] - yu[i, 0]) + (yv[i, 1] - yu[i, 1]) for i in ids]
    upd = [_mm_tn(jnp.concatenate([v[i], -u[i]], axis=0), jnp.concatenate([kbar[i], bbar[i]], axis=0)) for i in ids]
    s1 = [s0[i] * jnp.exp(ltot[i]) + jnp.where(_pair_mask(), upd[i], 0.0) for i in ids]
    if tms is None:
        return [(y[i], s1[i], tm[2 * i], tm[2 * i + 1]) for i in ids]
    return [(y[i], s1[i]) for i in ids]


HG_SUB = 16


def _hg_chunks(chains):
    s0, qr, iv, f, lb, ds = zip(*chains)
    ids = range(len(chains))
    inc = [_scan_masks(d)[0] for d in ds]
    q = [_silu(qr[i]) for i in ids]
    fg = [lb[i] + (1.0 - lb[i]) * jax.nn.sigmoid(f[i]) for i in ids]
    k = [1.0 - fg[i] for i in ids]
    g = [jnp.log(fg[i]) for i in ids]
    bc = [_scan_cumsum(inc[i], g[i]) for i in ids]
    btot = [jnp.sum(g[i], axis=0, keepdims=True) for i in ids]
    o_inter = [_mm_nt(q[i] * jnp.exp(bc[i]), s0[i]) for i in ids]
    rowi = _iota2((C, 1), 0)
    outs = [[] for _ in ids]
    for blk in range(C // HG_SUB):
        lo, hi = blk * HG_SUB, (blk + 1) * HG_SUB
        first = [lo if ds[i] == 0 else hi - 1 for i in ids]
        ref = [jnp.sum(jnp.where(rowi == first[i], bc[i] - g[i], 0.0), axis=0, keepdims=True) for i in ids]
        qi = [q[i][lo:hi] * jnp.exp(bc[i][lo:hi] - ref[i]) for i in ids]
        src = [(rowi < hi) if ds[i] == 0 else (rowi >= lo) for i in ids]
        ke = [jnp.where(src[i], k[i] * jnp.exp(jnp.where(src[i], ref[i] - bc[i], 0.0)), 0.0) for i in ids]
        a = [jnp.where(inc[i][lo:hi], _mm_nt(qi[i], ke[i]), 0.0) for i in ids]
        part = [_mm(a[i], iv[i]) for i in ids]
        for i in ids:
            outs[i].append(part[i])
    o = [o_inter[i] + jnp.concatenate(outs[i], axis=0) for i in ids]
    upd = [_mm_tn(iv[i], k[i] * jnp.exp(btot[i] - bc[i])) for i in ids]
    s1 = [s0[i] * jnp.exp(btot[i]) + upd[i] for i in ids]
    return [(o[i], s1[i]) for i in ids]


def _lockstep(fn, ds, nargs):
    def flat_fn(*flat):
        return tuple(fn([tuple(flat[i * nargs:(i + 1) * nargs]) + (d,) for i, d in enumerate(ds)]))

    return flat_fn


def _h_math(x, ng, scale, shift):
    return x * lax.rsqrt(jnp.mean(x * x, axis=-1, keepdims=True) + NORM_EPS) * ng * (1.0 + scale) + shift


def _hg_post_math(o0, o1, z, g):
    o = o0 + o1
    on = o * lax.rsqrt(_seg128_mean(o * o) + NORM_EPS)
    return on * g * _silu(z)


def _rw_prep_math(k, pw0, pw1, pa0, pa1, w0, a0, kkp, kap_p):
    kk = k * kkp
    kap = kk * lax.rsqrt(_seg64_sum(kk * kk) + 1e-12)
    outs = []
    for d, (pw, pa) in enumerate(((pw0, pa0), (pw1, pa1))):
        w_log = -_softplus(-(w0[d:d + 1] + pw)) - 0.5
        lw = -jnp.exp(w_log)
        a = jax.nn.sigmoid(a0[d:d + 1] + pa)
        kd = k * (1.0 + (a - 1.0) * kap_p)
        outs += [lw, kap * a, kd]
    return (kap, *outs)


def _rw_post_math(y0, y1, kd0, kd1, r, v, z, rk, gng, gnb):
    ys, ksum = y0 + y1, kd0 + kd1
    mean = _seg64_sum(ys) * (1.0 / 64.0)
    cen = ys - mean
    var = _seg64_sum(cen * cen) * (1.0 / 64.0)
    yn = cen * lax.rsqrt(var + RW_GN_EPS) * gng + gnb
    bonus = _seg64_sum(r * ksum * rk) * v
    return (yn + bonus) * _silu(z)


def _head_math(x, mo, gate, fg, tgt):
    xo = x + gate * mo
    y = xo * lax.rsqrt(jnp.mean(xo * xo, axis=-1, keepdims=True) + NORM_EPS) * fg
    err = y - tgt
    return 0.5 * jnp.sum(jnp.mean(err * err, axis=-1, keepdims=True), axis=0, keepdims=True)


def _gate_math(ghg, grw, a, b):
    return jax.nn.sigmoid(ghg) * a + jax.nn.sigmoid(grw) * b


def _call(body, name, grid, in_specs, out_specs, out_shape, scratch=(), sem=None, aliases=None):
    return pl.pallas_call(
        body, name=name, grid=grid, in_specs=in_specs, out_specs=out_specs, out_shape=out_shape,
        scratch_shapes=list(scratch), input_output_aliases=aliases or {},
        compiler_params=pltpu.CompilerParams(dimension_semantics=sem, vmem_limit_bytes=V7X_VMEM_LIMIT))


def _full(shape):
    nd = len(shape)
    return pl.BlockSpec(shape, lambda *_: (0,) * nd)


def _sds(shape, dtype=F32):
    return jax.ShapeDtypeStruct(shape, dtype)


def _proj_nt(h, wt, wt_blk0, nblk, row0, nrows, name):
    def body(h_ref, w_ref, o_ref):
        o_ref[...] = lax.dot_general(h_ref[row0:row0 + nrows, :], w_ref[...], (((1,), (1,)), ((), ())), preferred_element_type=F32)

    return _call(body, name, (nblk,), [_full(h.shape), pl.BlockSpec((WB, wt.shape[1]), lambda j: (wt_blk0 + j, 0))],
                 pl.BlockSpec((nrows, WB), lambda j: (0, j)), _sds((nrows, nblk * WB)), sem=("parallel",))(h, wt)


def _dw_tn(a, b, row0, out_rows, out_blk0, name, prev=None):
    m, n = a.shape
    k2 = b.shape[1]
    nblk = n // WB

    def body(*refs):
        a_ref, b_ref, o_ref = refs[0], refs[1], refs[-1]
        o_ref[...] = lax.dot_general(a_ref[...], b_ref[row0:row0 + m, :], (((0,), (0,)), ((), ())),
                                     preferred_element_type=F32).astype(BF16)

    in_specs = [pl.BlockSpec((m, WB), lambda j: (0, j)), _full(b.shape)]
    args = [a, b]
    aliases = None
    if prev is not None:
        in_specs.append(pl.BlockSpec(memory_space=pl.ANY))
        args.append(prev)
        aliases = {2: 0}
    return _call(body, name, (nblk,), in_specs, pl.BlockSpec((WB, k2), lambda j: (out_blk0 + j, 0)), _sds((out_rows, k2), BF16),
                 sem=("arbitrary",), aliases=aliases)(*args)


def _mm_nn(a, w, w_blk0, name, after=None):
    m, kc = a.shape
    n = w.shape[1]
    kb = max(k for k in (WB, 2 * WB, 4 * WB) if kc % k == 0 and (w_blk0 * WB) % k == 0)
    w_blk0 = w_blk0 * WB // kb
    tmb = 3 * TM if m % (3 * TM) == 0 else (2 * TM if m % (2 * TM) == 0 else TM)

    def body(a_ref, w_ref, *rest):
        o_ref = rest[-1]

        @pl.when(pl.program_id(1) == 0)
        def _():
            o_ref[...] = jnp.zeros_like(o_ref)

        o_ref[...] += jnp.dot(a_ref[...], w_ref[...], preferred_element_type=F32)

    extra = [] if after is None else [pl.BlockSpec(memory_space=pl.ANY)]
    return _call(body, name, (m // tmb, kc // kb),
                 [pl.BlockSpec((tmb, kb), lambda i, k: (i, k)), pl.BlockSpec((kb, n), lambda i, k: (w_blk0 + k, 0))] + extra,
                 pl.BlockSpec((tmb, n), lambda i, k: (i, 0)), _sds((m, n)), sem=("parallel", "arbitrary"))(
        a, w, *([] if after is None else [after]))


def _ada_fwd(cc16, ada_shard):
    ncol = ada_shard.shape[1]

    def body(c_ref, w_ref, o_ref):
        o_ref[...] = _mm(_silu(c_ref[...]), w_ref[...])

    return _call(body, "ada_fwd", (1,), [_full((16, D)), _full((D, ncol))], _full((16, ncol)), _sds((16, ncol)))(cc16, ada_shard)


def _ada_bwd(cc16, ada_shard, dm_x, dm_c, dmods):
    ncol = ada_shard.shape[1]

    def body(c_ref, w_ref, dx_ref, dc_ref, dm_ref, dw_ref, dcc_ref, db_ref):
        s, vjp = jax.vjp(_silu, c_ref[...])
        dc_tot = dc_ref[0:1, :]
        for j in range(1, NDEV):
            dc_tot = dc_tot + dc_ref[j:j + 1, :]
        row = _iota2((8, 1), 0)
        dm16 = jnp.concatenate([dx_ref[...], jnp.where(row == 0, dc_tot, 0.0)], axis=0)
        dw_ref[...] = _mm_tn(s, dm16)
        dcc_ref[...] = vjp(_mm_nt(dm16, w_ref[...]))[0]
        db_ref[...] = dm_ref[0:1, :] + dm_ref[1:2, :]

    return _call(body, "ada_bwd", (1,), [_full((16, D)), _full((D, ncol)), _full((8, ncol)), _full((8, ncol)), _full((8, 3 * D))],
                 [_full((D, ncol)), _full((16, D)), _full((1, 3 * D))], [_sds((D, ncol)), _sds((16, D)), _sds((1, 3 * D))])(
        cc16, ada_shard, dm_x, dm_c, dmods)


def _sel_mod(mods_ref, is_ctx, lo):
    return jnp.where(is_ctx, mods_ref[1:2, lo:lo + D], mods_ref[0:1, lo:lo + D])


def _token_specs():
    nct = _dims()["NCT"]
    return (pl.BlockSpec((TM, D), lambda i: (jnp.minimum(i, nct - 1), 0)),
            pl.BlockSpec((TM, D), lambda i: (jnp.maximum(i - nct, 0), 0)))


def _h_fwd(ctx, x, mods, norm_g):
    dm = _dims()

    def body(c_ref, x_ref, m_ref, g_ref, o_ref):
        is_ctx = pl.program_id(0) < dm["NCT"]
        tok = jnp.where(is_ctx, c_ref[...], x_ref[...])
        o_ref[...] = _h_math(tok, g_ref[...], _sel_mod(m_ref, is_ctx, D), _sel_mod(m_ref, is_ctx, 0)).astype(BF16)

    return _call(body, "h_fwd", (dm["TT"] // TM,), [*_token_specs(), _full((8, 3 * D)), _full((1, D))],
                 pl.BlockSpec((TM, D), lambda i: (i, 0)), _sds((dm["TT"], D), BF16), sem=("parallel",))(ctx, x, mods, norm_g)


def _chunk_of(d, s):
    dm = _dims()
    ncc, nch = dm["NCC"], dm["NCH"]
    return s if d == 0 else jnp.where(s < ncc, ncc - 1 - s, nch - 1 - (s - ncc))


HG_GROUP = 8
RW_GROUP = 8


def _dir_specs(make):
    return [make(d) for d in range(2)]


def _lanes(j):
    return slice(j * 128, (j + 1) * 128)


def _hg_scan_fwd(p_hg, lb3):
    dm = _dims()
    nh, nch, tt = dm["HG_H"], dm["NCH"], dm["TT"]
    pp = min(HG_GROUP, nh)
    lw_ = 128 * pp

    def body(q0, q1, i0, i1, f0, f1, lb0, lb1, o0, o1, ck_ref, s_ref):
        @pl.when(pl.program_id(1) == 0)
        def _():
            s_ref[...] = jnp.zeros_like(s_ref)

        keys, chains = [], []
        for d, refs in enumerate(((q0, i0, f0, lb0), (q1, i1, f1, lb1))):
            vals = [r[...] for r in refs]
            for j in range(pp):
                keys.append((d, j))
                chains.append((s_ref[d, j], *[a[:, _lanes(j)] for a in vals], d))
        for (d, j), chain, (out, s1) in zip(keys, chains, _hg_chunks(chains)):
            ck_ref[d, j] = chain[0]
            (o0, o1)[d][:, _lanes(j)] = out
            s_ref[d, j] = s1

    blk = lambda off: _dir_specs(lambda d: pl.BlockSpec((C, lw_), lambda h, s: (_chunk_of(d, s), off * nh // pp + h)))
    return _call(body, "hg_scan_fwd", (nh // pp, nch),
                 blk(0) + blk(1) + _dir_specs(lambda d: pl.BlockSpec((C, lw_), lambda h, s: (_chunk_of(d, s), (2 + d) * nh // pp + h)))
                 + _dir_specs(lambda d: pl.BlockSpec((None, 1, lw_), lambda h, s: (d, 0, h))),
                 _dir_specs(lambda d: pl.BlockSpec((C, lw_), lambda h, s: (_chunk_of(d, s), h)))
                 + [pl.BlockSpec((2, pp, None, 128, 128), lambda h, s: (0, h, s, 0, 0))],
                 [_sds((tt, HGW)), _sds((tt, HGW)), _sds((2, nh, nch, 128, 128))],
                 scratch=[pltpu.VMEM((2, pp, 128, 128), F32)], sem=("parallel", "arbitrary"))(*([p_hg] * 6), lb3, lb3)


def _hg_scan_bwd(p_hg, lb3, ck, do):
    dm = _dims()
    nh, nch, tt = dm["HG_H"], dm["NCH"], dm["TT"]
    pp = min(HG_GROUP, nh)
    lw_ = 128 * pp

    def body(q0, q1, i0, i1, f0, f1, lb0, lb1, ck_ref, do0, do1, dq0, dq1, di0, di1, df0, df1, dlb_ref, ds_ref):
        @pl.when(pl.program_id(1) == 0)
        def _():
            ds_ref[...] = jnp.zeros_like(ds_ref)
            dlb_ref[...] = jnp.zeros_like(dlb_ref)

        keys, flat, cts = [], [], []
        for d, refs in enumerate(((q0, i0, f0, lb0, do0), (q1, i1, f1, lb1, do1))):
            *vals, do_ = [r[...] for r in refs]
            for j in range(pp):
                keys.append((d, j))
                flat += [ck_ref[d, j], *[a[:, _lanes(j)] for a in vals]]
                cts.append((do_[:, _lanes(j)], ds_ref[d, j]))
        _, vjp = jax.vjp(_lockstep(_hg_chunks, [d for d, _ in keys], 5), *flat)
        grads = vjp(tuple(cts))
        for n, (d, j) in enumerate(keys):
            ds0, dq_, di_, df_, dlb = grads[5 * n:5 * n + 5]
            (dq0, dq1)[d][:, _lanes(j)] = dq_
            (di0, di1)[d][:, _lanes(j)] = di_
            (df0, df1)[d][:, _lanes(j)] = df_
            dlb_ref[d, :, _lanes(j)] += dlb
            ds_ref[d, j] = ds0

    ch = lambda d, s: _chunk_of(d, nch - 1 - s)
    blk = lambda off: _dir_specs(lambda d: pl.BlockSpec((C, lw_), lambda h, s: (ch(d, s), off * nh // pp + h)))
    tok = _dir_specs(lambda d: pl.BlockSpec((C, lw_), lambda h, s: (ch(d, s), h)))
    return _call(body, "hg_scan_bwd", (nh // pp, nch),
                 blk(0) + blk(1) + _dir_specs(lambda d: pl.BlockSpec((C, lw_), lambda h, s: (ch(d, s), (2 + d) * nh // pp + h)))
                 + _dir_specs(lambda d: pl.BlockSpec((None, 1, lw_), lambda h, s: (d, 0, h)))
                 + [pl.BlockSpec((2, pp, None, 128, 128), lambda h, s: (0, h, nch - 1 - s, 0, 0))] + tok,
                 tok * 3 + [pl.BlockSpec((2, 1, lw_), lambda h, s: (0, 0, h))],
                 [_sds((tt, HGW))] * 6 + [_sds((2, 1, HGW))],
                 scratch=[pltpu.VMEM((2, pp, 128, 128), F32)], sem=("parallel", "arbitrary"))(*([p_hg] * 6), lb3, lb3, ck, do, do)


def _rw_scan_fwd(sh, kap, lw, b, kd):
    dm = _dims()
    npair, nch, tt = dm["RW_P"], dm["NCH"], dm["TT"]
    pp = min(RW_GROUP, npair)
    lw_ = 128 * pp

    def body(r0, r1, v0, v1, ka0, ka1, lw0, lw1, b0, b1, kd0, kd1, y0, y1, ck_ref, tm_ref, s_ref):
        @pl.when(pl.program_id(1) == 0)
        def _():
            s_ref[...] = jnp.zeros_like(s_ref)

        keys, chains = [], []
        for d, refs in enumerate(((r0, lw0, ka0, b0, v0, kd0), (r1, lw1, ka1, b1, v1, kd1))):
            vals = [r[...] for r in refs]
            for j in range(pp):
                keys.append((d, j))
                chains.append((s_ref[d, j], *[a[:, _lanes(j)] for a in vals], d))
        for (d, j), chain, (out, s1, tm_a, tm_b) in zip(keys, chains, _rw_chunks(chains)):
            ck_ref[d, j] = chain[0]
            tm_ref[d, j, 0] = tm_a
            tm_ref[d, j, 1] = tm_b
            (y0, y1)[d][:, _lanes(j)] = out
            s_ref[d, j] = s1

    two = lambda off: _dir_specs(lambda d: pl.BlockSpec((C, lw_), lambda p, s: (_chunk_of(d, s), off * npair // pp + p)))
    three = _dir_specs(lambda d: pl.BlockSpec((None, C, lw_), lambda p, s: (d, _chunk_of(d, s), p)))
    return _call(body, "rw_scan_fwd", (npair // pp, nch), two(0) + two(2) + two(0) + three * 3,
                 two(0) + [pl.BlockSpec((2, pp, None, 128, 128), lambda p, s: (0, p, s, 0, 0)),
                           pl.BlockSpec((2, pp, None, 2, C, C), lambda p, s: (0, p, s, 0, 0, 0))],
                 [_sds((tt, RWW)), _sds((tt, RWW)), _sds((2, npair, nch, 128, 128)), _sds((2, npair, nch, 2, C, C))],
                 scratch=[pltpu.VMEM((2, pp, 128, 128), F32)], sem=("parallel", "arbitrary"))(
        sh, sh, sh, sh, kap, kap, lw, lw, b, b, kd, kd)


def _rw_scan_bwd(sh, kap, lw, b, kd, ck, tm_ck, dy):
    dm = _dims()
    npair, nch, tt = dm["RW_P"], dm["NCH"], dm["TT"]
    pp = min(RW_GROUP, npair)
    lw_ = 128 * pp

    def body(r0, r1, v0, v1, ka0, ka1, lw0, lw1, b0, b1, kd0, kd1, ck_ref, tm_ref, dy0, dy1,
             dr0, dr1, dv0, dv1, dka0, dka1, dlw0, dlw1, db0, db1, dkd0, dkd1, ds_ref):
        @pl.when(pl.program_id(1) == 0)
        def _():
            ds_ref[...] = jnp.zeros_like(ds_ref)

        keys, flat, cts, tms = [], [], [], {}
        for d, refs in enumerate(((r0, lw0, ka0, b0, v0, kd0, dy0), (r1, lw1, ka1, b1, v1, kd1, dy1))):
            *vals, dy_ = [r[...] for r in refs]
            for j in range(pp):
                tms[len(keys), 0], tms[len(keys), 1] = tm_ref[d, j, 0], tm_ref[d, j, 1]
                keys.append((d, j))
                flat += [ck_ref[d, j], *[a[:, _lanes(j)] for a in vals]]
                cts.append((dy_[:, _lanes(j)], ds_ref[d, j]))
        _, vjp = jax.vjp(_lockstep(functools.partial(_rw_chunks, tms=tms), [d for d, _ in keys], 7), *flat)
        grads = vjp(tuple(cts))
        outs = ((dr0, dlw0, dka0, db0, dv0, dkd0), (dr1, dlw1, dka1, db1, dv1, dkd1))
        for n, (d, j) in enumerate(keys):
            ds0, *g_in = grads[7 * n:7 * n + 7]
            for o_ref, g in zip(outs[d], g_in):
                o_ref[:, _lanes(j)] = g
            ds_ref[d, j] = ds0

    ch = lambda d, s: _chunk_of(d, nch - 1 - s)
    two = lambda off: _dir_specs(lambda d: pl.BlockSpec((C, lw_), lambda p, s: (ch(d, s), off * npair // pp + p)))
    three = _dir_specs(lambda d: pl.BlockSpec((None, C, lw_), lambda p, s: (d, ch(d, s), p)))
    return _call(body, "rw_scan_bwd", (npair // pp, nch),
                 two(0) + two(2) + two(0) + three * 3
                 + [pl.BlockSpec((2, pp, None, 128, 128), lambda p, s: (0, p, nch - 1 - s, 0, 0)),
                    pl.BlockSpec((2, pp, None, 2, C, C), lambda p, s: (0, p, nch - 1 - s, 0, 0, 0))] + two(0),
                 two(0) * 6, [_sds((tt, RWW))] * 12,
                 scratch=[pltpu.VMEM((2, pp, 128, 128), F32)], sem=("parallel", "arbitrary"))(
        sh, sh, sh, sh, kap, kap, lw, lw, b, b, kd, kd, ck, tm_ck, dy, dy)


def _tile_pos(i):
    dm = _dims()
    is_ctx = i < dm["NCT"]
    rows = lax.broadcasted_iota(jnp.int32, (TM, 1), 0)
    tl = rows + (i - dm["NCT"]) * TM
    width = jnp.where(is_ctx, TC, GW)
    assert TC & (TC - 1) == 0 and GW & (GW - 1) == 0 and TM % GW == 0
    colp = rows & (width - 1)
    return is_ctx, rows, tl, colp, width


def _shift_lr(cur, i):
    _, _, _, colp, width = _tile_pos(i)
    left = jnp.where(colp == 0, 0.0, pltpu.roll(cur, 1, 0))
    right = jnp.where(colp == width - 1, 0.0, pltpu.roll(cur, TM - 1, 0))
    return left, right


def _shift_ud(cur, prv, nxt, i):
    is_ctx, rows, tl, _, _ = _tile_pos(i)
    if TM > GW:
        up = jnp.where(rows >= GW, pltpu.roll(cur, GW, 0), pltpu.roll(prv, GW, 0))
        down = jnp.where(rows < TM - GW, pltpu.roll(cur, TM - GW, 0), pltpu.roll(nxt, TM - GW, 0))
    else:
        up, down = prv, nxt
    up = jnp.where(jnp.logical_or(is_ctx, tl < GW), 0.0, up)
    down = jnp.where(jnp.logical_or(is_ctx, tl >= T - GW), 0.0, down)
    return up, down


def _shift_specs():
    dm = _dims()
    nt = dm["TT"] // TM
    cw = max(w for w in range(128, 1664 + 1, 128) if dm["SH"] % w == 0)
    cur = pl.BlockSpec((TM, cw), lambda j, i: (i, j))
    prv = pl.BlockSpec((TM, cw), lambda j, i: (jnp.maximum(i - 1, 0), j))
    nxt = pl.BlockSpec((TM, cw), lambda j, i: (jnp.minimum(i + 1, nt - 1), j))
    mu = pl.BlockSpec((4, cw), lambda j, i: (0, j))
    return nt, cw, cur, prv, nxt, mu


def _shift_fwd(p_rw, mu):
    dm = _dims()
    nt, cw, cur, prv, nxt, mus = _shift_specs()

    def body(c_ref, p_ref, n_ref, mu_ref, o_ref):
        i = pl.program_id(1)
        p, m = c_ref[...], mu_ref[...]
        left, right = _shift_lr(p, i)
        up, down = _shift_ud(p, p_ref[...], n_ref[...], i)
        vert = jnp.where(i < dm["NCT"], 0.0, 1.0)
        o_ref[...] = (p * (1.0 - m[0:1] - m[1:2] - vert * (m[2:3] + m[3:4]))
                      + m[0:1] * left + m[1:2] * right + m[2:3] * up + m[3:4] * down)

    return _call(body, "shift_fwd", (dm["SH"] // cw, nt), [cur, prv, nxt, mus], cur, _sds((dm["TT"], dm["SH"])),
                 sem=("parallel", "parallel"))(p_rw, p_rw, p_rw, mu)


def _shift_bwd(p_rw, dsh, mu):
    dm = _dims()
    nt, cw, cur, prv, nxt, mus = _shift_specs()

    def body(c_ref, p_ref, n_ref, gc_ref, gp_ref, gn_ref, mu_ref, dp_ref, dmu_ref):
        i = pl.program_id(1)
        p, g, m = c_ref[...], gc_ref[...], mu_ref[...]
        vert = jnp.where(i < dm["NCT"], 0.0, 1.0)
        _, from_right = _shift_lr(m[0:1] * g, i)
        from_left, _ = _shift_lr(m[1:2] * g, i)
        _, from_down = _shift_ud(m[2:3] * g, g, m[2:3] * gn_ref[...], i)
        from_up, _ = _shift_ud(m[3:4] * g, m[3:4] * gp_ref[...], g, i)
        dp = g * (1.0 - m[0:1] - m[1:2] - vert * (m[2:3] + m[3:4])) + from_right + from_left + from_down + from_up
        dp_ref[...] = dp.astype(BF16)

        @pl.when(i == 0)
        def _():
            dmu_ref[...] = jnp.zeros_like(dmu_ref)

        left, right = _shift_lr(p, i)
        up, down = _shift_ud(p, p_ref[...], n_ref[...], i)
        s = lambda a: jnp.sum(a, axis=0, keepdims=True)
        dmu_ref[...] += jnp.concatenate([s(g * (left - p)), s(g * (right - p)), vert * s(g * (up - p)), vert * s(g * (down - p))], axis=0)

    return _call(body, "shift_bwd", (dm["SH"] // cw, nt), [cur, prv, nxt, cur, prv, nxt, mus],
                 [cur, pl.BlockSpec((4, cw), lambda j, i: (0, j))], [_sds((dm["TT"], dm["SH"]), BF16), _sds((4, dm["SH"]))],
                 sem=("parallel", "arbitrary"))(p_rw, p_rw, p_rw, dsh, dsh, dsh, mu)


def _acc_out(ref, first, vals):
    @pl.when(first)
    def _():
        for r in ref:
            r[...] = jnp.zeros_like(r)

    for r, v in zip(ref, vals):
        r[...] += v


def _rowsum(a):
    return jnp.sum(a, axis=0, keepdims=True)


def _rw_prep_specs():
    dm = _dims()
    rw = RWW // 128
    tok = lambda width, cb: pl.BlockSpec((TMV, width), lambda i: (i, cb))
    ins = [pl.BlockSpec((TMV, RWW), lambda i: (i, 1)), tok(128, 3 * rw), tok(128, 3 * rw + 1),
           _full((2, RWW)), _full((128, RWW)), _full((128, RWW)), _full((2, RWW)), _full((128, RWW)), _full((128, RWW)),
           _full((1, RWW)), _full((1, RWW))]
    return dm, ins


def _rw_prep_fwd(sh, w0, w2p0, w2p1, a0, a2p0, a2p1, kkp, kap_p):
    dm, ins = _rw_prep_specs()
    tt = dm["TT"]

    def body(k_ref, low_ref, loa_ref, w0_ref, w20_ref, w21_ref, a0_ref, a20_ref, a21_ref, kk_ref, ka_ref, kap_ref, lw_ref, b_ref, kd_ref):
        tw, la = jnp.tanh(low_ref[...]), loa_ref[...]
        kap, lw0, b0, kd0, lw1, b1, kd1 = _rw_prep_math(
            k_ref[...], _mm1(tw, w20_ref[...]), _mm1(tw, w21_ref[...]), _mm1(la, a20_ref[...]), _mm1(la, a21_ref[...]),
            w0_ref[...], a0_ref[...], kk_ref[...], ka_ref[...])
        kap_ref[...] = kap
        lw_ref[0], lw_ref[1] = lw0, lw1
        b_ref[0], b_ref[1] = b0, b1
        kd_ref[0], kd_ref[1] = kd0, kd1

    two = pl.BlockSpec((2, TMV, RWW), lambda i: (0, i, 0))
    return _call(body, "rw_prep_fwd", (tt // TMV,), ins, [pl.BlockSpec((TMV, RWW), lambda i: (i, 0)), two, two, two],
                 [_sds((tt, RWW)), _sds((2, tt, RWW)), _sds((2, tt, RWW)), _sds((2, tt, RWW))], sem=("parallel",))(
        sh, sh, sh, w0, w2p0, w2p1, a0, a2p0, a2p1, kkp, kap_p)


def _rw_prep_bwd(sh, w0, w2p0, w2p1, a0, a2p0, a2p1, kkp, kap_p, scan_grads, dkd_p, dr_p, dv_p):
    dm, ins = _rw_prep_specs()
    tt, sh_w = dm["TT"], dm["SH"]
    one = pl.BlockSpec((TMV, RWW), lambda i: (i, 0))

    def body(k_ref, low_ref, loa_ref, w0_ref, w20_ref, w21_ref, a0_ref, a20_ref, a21_ref, kk_ref, ka_ref,
             dr0, dr1, dv0, dv1, dka0, dka1, dlw0, dlw1, db0, db1, dkd0, dkd1, dkdp_ref, drp_ref, dvp_ref,
             dsh_ref, dw0_ref, dw20_ref, dw21_ref, da0_ref, da20_ref, da21_ref, dkk_ref, dka_ref):
        tw, la = jnp.tanh(low_ref[...]), loa_ref[...]
        w20, w21, a20, a21 = w20_ref[...], w21_ref[...], a20_ref[...], a21_ref[...]
        _, vjp = jax.vjp(_rw_prep_math, k_ref[...], _mm1(tw, w20), _mm1(tw, w21), _mm1(la, a20), _mm1(la, a21),
                         w0_ref[...], a0_ref[...], kk_ref[...], ka_ref[...])
        dkp = dkdp_ref[...]
        dk, dpw0, dpw1, dpa0, dpa1, dw0, da0, dkk, dka = vjp(
            (dka0[...] + dka1[...], dlw0[...], db0[...], dkd0[...] + dkp, dlw1[...], db1[...], dkd1[...] + dkp))
        twt, lat = tw.T, la.T
        dtw = _mm1_nt(dpw0, w20) + _mm1_nt(dpw1, w21)
        dsh_ref[:, 0:RWW] = dr0[...] + dr1[...] + drp_ref[...]
        dsh_ref[:, RWW:2 * RWW] = dk
        dsh_ref[:, 2 * RWW:3 * RWW] = dv0[...] + dv1[...] + dvp_ref[...]
        dsh_ref[:, 3 * RWW:3 * RWW + 128] = dtw * (1.0 - tw * tw)
        dsh_ref[:, 3 * RWW + 128:3 * RWW + 256] = _mm1_nt(dpa0, a20) + _mm1_nt(dpa1, a21)
        _acc_out((dw0_ref, dw20_ref, dw21_ref, da0_ref, da20_ref, da21_ref, dkk_ref, dka_ref), pl.program_id(0) == 0,
                 (dw0, _mm1(twt, dpw0), _mm1(twt, dpw1), da0, _mm1(lat, dpa0), _mm1(lat, dpa1), dkk, dka))

    par = [_full((2, RWW)), _full((128, RWW)), _full((128, RWW)), _full((2, RWW)), _full((128, RWW)), _full((128, RWW)),
           _full((1, RWW)), _full((1, RWW))]
    return _call(body, "rw_prep_bwd", (tt // TMV,), ins + [one] * 15,
                 [pl.BlockSpec((TMV, sh_w), lambda i: (i, 0))] + par,
                 [_sds((tt, sh_w)), _sds((2, RWW)), _sds((128, RWW)), _sds((128, RWW)), _sds((2, RWW)), _sds((128, RWW)), _sds((128, RWW)),
                  _sds((1, RWW)), _sds((1, RWW))],
                 sem=("arbitrary",))(sh, sh, sh, w0, w2p0, w2p1, a0, a2p0, a2p1, kkp, kap_p, *scan_grads, dkd_p, dr_p, dv_p)


def _rw_post_specs():
    dm = _dims()
    nv = dm["NCTV"]
    rw = RWW // 128
    lat3 = lambda d: pl.BlockSpec((None, TMV, RWW), lambda i: (d, nv + i, 0))
    lat2 = pl.BlockSpec((TMV, RWW), lambda i: (nv + i, 0))
    ins = [lat2, lat2, lat3(0), lat3(1),
           pl.BlockSpec((TMV, RWW), lambda i: (nv + i, 0)), pl.BlockSpec((TMV, RWW), lambda i: (nv + i, 2)),
           pl.BlockSpec((TMV, RWW), lambda i: (i, 0)), _full((1, RWW)), _full((1, RWW)), _full((1, RWW))]
    return dm, nv, ins


def _rw_post_fwd(y0_, y1_, kd, sh, p_z, rk, gng, gnb):
    dm, nv, ins = _rw_post_specs()

    def body(y0, y1, k0, k1, r, v, z, rk_ref, g_ref, b_ref, o_ref):
        o_ref[...] = _rw_post_math(y0[...], y1[...], k0[...], k1[...], r[...], v[...], z[...], rk_ref[...], g_ref[...], b_ref[...]).astype(BF16)

    return _call(body, "rw_post_fwd", (T // TMV,), ins, pl.BlockSpec((TMV, RWW), lambda i: (i, 0)), _sds((T, RWW), BF16),
                 sem=("parallel",))(y0_, y1_, kd, kd, sh, sh, p_z, rk, gng, gnb)


def _rw_post_bwd(y0_, y1_, kd, sh, p_z, rk, gng, gnb, dout):
    dm, nv, _ = _rw_post_specs()
    tt = dm["TT"]
    cl = lambda i: jnp.maximum(i - nv, 0)
    all3 = lambda d: pl.BlockSpec((None, TMV, RWW), lambda i: (d, i, 0))
    all2 = pl.BlockSpec((TMV, RWW), lambda i: (i, 0))
    ins = [all2, all2, all3(0), all3(1),
           pl.BlockSpec((TMV, RWW), lambda i: (i, 0)), pl.BlockSpec((TMV, RWW), lambda i: (i, 2)),
           pl.BlockSpec((TMV, RWW), lambda i: (cl(i), 0)), _full((1, RWW)), _full((1, RWW)), _full((1, RWW)),
           pl.BlockSpec((TMV, RWW), lambda i: (cl(i), 0))]
    tok = pl.BlockSpec((TMV, RWW), lambda i: (i, 0))

    def body(y0, y1, k0, k1, r, v, z, rk_ref, g_ref, b_ref, do_ref, dy_ref, dkd_ref, dr_ref, dv_ref, dz_ref, drk_ref, dg_ref, db_ref):
        i = pl.program_id(0)
        lat = jnp.where(i >= nv, 1.0, 0.0)
        _, vjp = jax.vjp(_rw_post_math, y0[...], y1[...], k0[...], k1[...], r[...], v[...], z[...], rk_ref[...], g_ref[...], b_ref[...])
        dy0, _, dk0, _, dr, dv, dz, drk, dg, db = vjp(do_ref[...] * lat)
        dy_ref[...] = dy0
        dkd_ref[...] = dk0
        dr_ref[...] = dr
        dv_ref[...] = dv
        dz_ref[...] = dz.astype(BF16)
        _acc_out((drk_ref, dg_ref, db_ref), i == 0, (drk, dg, db))

    return _call(body, "rw_post_bwd", (tt // TMV,), ins,
                 [tok, tok, tok, tok, pl.BlockSpec((TMV, RWW), lambda i: (cl(i), 0)), _full((1, RWW)), _full((1, RWW)), _full((1, RWW))],
                 [_sds((tt, RWW))] * 4 + [_sds((T, RWW), BF16), _sds((1, RWW)), _sds((1, RWW)), _sds((1, RWW))],
                 sem=("arbitrary",))(y0_, y1_, kd, kd, sh, sh, p_z, rk, gng, gnb, dout)


def _hg_post_fwd(o0_, o1_, p_hg, hg_g):
    dm = _dims()
    nv = dm["NCTV"]
    lat3 = lambda d: pl.BlockSpec((TMV, HGW), lambda i: (nv + i, 0))

    def body(o0, o1, z, g, out):
        out[...] = _hg_post_math(o0[...], o1[...], z[...], g[...]).astype(BF16)

    return _call(body, "hg_post_fwd", (T // TMV,), [lat3(0), lat3(1), pl.BlockSpec((TMV, HGW), lambda i: (nv + i, 4)), _full((1, HGW))],
                 pl.BlockSpec((TMV, HGW), lambda i: (i, 0)), _sds((T, HGW), BF16), sem=("parallel",))(o0_, o1_, p_hg, hg_g)


def _hg_post_bwd(o0_, o1_, p_hg, hg_g, dout):
    dm = _dims()
    nv, tt = dm["NCTV"], dm["TT"]
    all3 = lambda d: pl.BlockSpec((TMV, HGW), lambda i: (i, 0))
    tok = pl.BlockSpec((TMV, HGW), lambda i: (i, 0))

    def body(o0, o1, z, g, do_in, do_ref, dz_ref, dg_ref):
        i = pl.program_id(0)
        lat = jnp.where(i >= nv, 1.0, 0.0)
        _, vjp = jax.vjp(_hg_post_math, o0[...], o1[...], z[...], g[...])
        d0, _, dz, dg = vjp(do_in[...] * lat)
        do_ref[...] = d0
        dz_ref[...] = dz
        _acc_out((dg_ref,), i == 0, (dg,))

    return _call(body, "hg_post_bwd", (tt // TMV,),
                 [all3(0), all3(1), pl.BlockSpec((TMV, HGW), lambda i: (i, 4)), _full((1, HGW)),
                  pl.BlockSpec((TMV, HGW), lambda i: (jnp.maximum(i - nv, 0), 0))],
                 [tok, tok, _full((1, HGW))], [_sds((tt, HGW)), _sds((tt, HGW)), _sds((1, HGW))], sem=("arbitrary",))(o0_, o1_, p_hg, hg_g, dout)


def _hg_dproj(dq0, dq1, di0, di1, df0, df1, dz):
    dm = _dims()
    tt = dm["TT"]
    tok = pl.BlockSpec((TMV, HGW), lambda i: (i, 0))

    def body(q0, q1, i0, i1, f0, f1, z, o_ref):
        o_ref[:, 0:HGW] = (q0[...] + q1[...]).astype(BF16)
        o_ref[:, HGW:2 * HGW] = (i0[...] + i1[...]).astype(BF16)
        o_ref[:, 2 * HGW:3 * HGW] = f0[...].astype(BF16)
        o_ref[:, 3 * HGW:4 * HGW] = f1[...].astype(BF16)
        o_ref[:, 4 * HGW:5 * HGW] = z[...].astype(BF16)

    return _call(body, "hg_dproj", (tt // TMV,), [tok] * 7,
                 pl.BlockSpec((TMV, 5 * HGW), lambda i: (i, 0)), _sds((tt, 5 * HGW), BF16), sem=("parallel",))(
        dq0, dq1, di0, di1, df0, df1, dz)


def _merge1_fwd(out_hg, out_rw, whg_t, wrw_t, p_g):
    tok = pl.BlockSpec((TM, D), lambda i: (i, 0))

    def body(h_ref, r_ref, wh_ref, wr_ref, g1_ref, g2_ref, a_ref, b_ref, m_ref):
        nt = (((1,), (1,)), ((), ()))
        a = lax.dot_general(h_ref[...], wh_ref[...], nt, preferred_element_type=F32)
        b = lax.dot_general(r_ref[...], wr_ref[...], nt, preferred_element_type=F32)
        a_ref[...] = a
        b_ref[...] = b
        m_ref[...] = _gate_math(g1_ref[...], g2_ref[...], a, b).astype(BF16)

    return _call(body, "merge1_fwd", (T // TM,),
                 [pl.BlockSpec((TM, HGW), lambda i: (i, 0)), pl.BlockSpec((TM, RWW), lambda i: (i, 0)), _full((D, HGW)), _full((D, RWW)),
                  pl.BlockSpec((TM, D), lambda i: (i, 0)), pl.BlockSpec((TM, D), lambda i: (i, 1))],
                 [tok, tok, tok], [_sds((T, D)), _sds((T, D)), _sds((T, D), BF16)], sem=("parallel",))(out_hg, out_rw, whg_t, wrw_t, p_g, p_g)


def _head_fwd_bwd(x, merged, w_out, mods, final_g, tgt):
    tok = pl.BlockSpec((TM, D), lambda i: (i, 0))

    def body(x_ref, m_ref, w_ref, mods_ref, fg_ref, t_ref, dxo_ref, dmo_ref, loss_ref, dgate_ref, dfg_ref):
        mo = jnp.dot(m_ref[...], w_ref[...], preferred_element_type=F32)
        gate = mods_ref[0:1, 2 * D:3 * D]
        loss, vjp = jax.vjp(lambda x_, mo_, g_, fg_: _head_math(x_, mo_, g_, fg_, t_ref[...]), x_ref[...], mo, gate, fg_ref[...])
        dx, dmo, dgate, dfg = vjp(jnp.ones((1, 1), F32))
        dxo_ref[...] = dx
        dmo_ref[...] = dmo.astype(BF16)
        _acc_out((loss_ref, dgate_ref, dfg_ref), pl.program_id(0) == 0, (jnp.broadcast_to(loss, (1, 128)), dgate, dfg))

    return _call(body, "head_fwd_bwd", (T // TM,), [tok, tok, _full((D, D)), _full((8, 3 * D)), _full((1, D)), tok],
                 [tok, tok, _full((1, 128)), _full((1, D)), _full((1, D))],
                 [_sds((T, D)), _sds((T, D), BF16), _sds((1, 128)), _sds((1, D)), _sds((1, D))], sem=("arbitrary",))(
        x, merged, w_out, mods, final_g, tgt)


def _merge_bwd(dmo, w_out, p_g, a, b):
    tok = pl.BlockSpec((TM, D), lambda i: (i, 0))

    def body(dmo_ref, w_ref, g1_ref, g2_ref, a_ref, b_ref, da_ref, db_ref, dg_ref):
        dm_ = lax.dot_general(dmo_ref[...], w_ref[...], (((1,), (1,)), ((), ())), preferred_element_type=F32)
        _, vjp = jax.vjp(_gate_math, g1_ref[...], g2_ref[...], a_ref[...], b_ref[...])
        dg1, dg2, da, db = vjp(dm_)
        da_ref[...] = da.astype(BF16)
        db_ref[...] = db.astype(BF16)
        dg_ref[:, 0:D] = dg1.astype(BF16)
        dg_ref[:, D:2 * D] = dg2.astype(BF16)

    return _call(body, "merge_bwd", (T // TM,),
                 [tok, _full((D, D)), pl.BlockSpec((TM, D), lambda i: (i, 0)), pl.BlockSpec((TM, D), lambda i: (i, 1)), tok, tok],
                 [tok, tok, pl.BlockSpec((TM, 2 * D), lambda i: (i, 0))], [_sds((T, D), BF16), _sds((T, D), BF16), _sds((T, 2 * D), BF16)],
                 sem=("parallel",))(dmo, w_out, p_g, p_g, a, b)


def _h_bwd(ctx, x, mods, norm_g, dh_hg, dh_rw, dh_z, dh_g, dxo, dgate):
    dm = _dims()
    nct, tt = dm["NCT"], dm["TT"]
    tok = pl.BlockSpec((TM, D), lambda i: (i, 0))
    lat = pl.BlockSpec((TM, D), lambda i: (jnp.maximum(i - nct, 0), 0))

    def body(c_ref, x_ref, m_ref, g_ref, d1, d2, d3, d4, dxo_ref, dgate_ref, gx_ref, dng_ref, dmods_ref):
        i = pl.program_id(0)
        is_ctx = i < nct
        latf = jnp.where(is_ctx, 0.0, 1.0)
        dh = d1[...] + d2[...] + latf * (d3[...] + d4[...])
        _, vjp = jax.vjp(_h_math, jnp.where(is_ctx, c_ref[...], x_ref[...]), g_ref[...], _sel_mod(m_ref, is_ctx, D),
                         _sel_mod(m_ref, is_ctx, 0))
        dx, dng, dscale, dshift = vjp(dh)
        gx_ref[...] = dx + dxo_ref[...]

        @pl.when(i == 0)
        def _():
            dng_ref[...] = jnp.zeros_like(dng_ref)
            dmods_ref[...] = jnp.zeros_like(dmods_ref)
            dmods_ref[0:1, 2 * D:3 * D] = dgate_ref[...]

        dng_ref[...] += dng
        row = lax.broadcasted_iota(jnp.int32, (8, 1), 0)
        sel = jnp.where(row == jnp.where(is_ctx, 1, 0), 1.0, 0.0)
        dmods_ref[:, 0:D] += sel * dshift
        dmods_ref[:, D:2 * D] += sel * dscale

    return _call(body, "h_bwd", (tt // TM,),
                 [*_token_specs(), _full((8, 3 * D)), _full((1, D)), tok, tok, lat, lat, lat, _full((1, D))],
                 [lat, _full((1, D)), _full((8, 3 * D))], [_sds((T, D)), _sds((1, D)), _sds((8, 3 * D))], sem=("arbitrary",))(
        ctx, x, mods, norm_g, dh_hg, dh_rw, dh_z, dh_g, dxo, dgate)


def _lb_math(l0, l1):
    return jax.nn.sigmoid(l0 - l1)


def _lb_fwd(l0, l1):
    def body(a_ref, b_ref, o_ref):
        o_ref[...] = _lb_math(a_ref[...], b_ref[...])

    return _call(body, "lb_fwd", (1,), [_full((2, HGW))] * 2, _full((2, HGW)), _sds((2, HGW)))(l0, l1)


def _lb_bwd(l0, l1, dlb):
    def body(a_ref, b_ref, d_ref, da_ref, db_ref):
        _, vjp = jax.vjp(_lb_math, a_ref[...], b_ref[...])
        da_ref[...], db_ref[...] = vjp(d_ref[...])

    return _call(body, "lb_bwd", (1,), [_full((2, HGW))] * 3, [_full((2, HGW))] * 2, [_sds((2, HGW))] * 2)(l0, l1, dlb)


def _local_step(x, ctx, tgt, mods, w, start_exchange):
    dm = _dims()
    tt, sh_w, hgc = dm["TT"], dm["SH"], dm["HGC"]
    nb = lambda cols: cols // WB
    h = _h_fwd(ctx, x, mods, w["norm_g"])
    win_t = w["win_t"]
    p_hg = _proj_nt(h, win_t, 0, nb(hgc), 0, tt, "proj_hg")
    p_rw = _proj_nt(h, win_t, nb(hgc), nb(sh_w), 0, tt, "proj_rw")
    p_z = _proj_nt(h, win_t, nb(hgc + sh_w), nb(RWW), TC, T, "proj_z")
    p_g = _proj_nt(h, win_t, nb(hgc + sh_w + RWW), nb(2 * D), TC, T, "proj_g")

    lb3 = _lb_fwd(w["lb0"], w["lb1"]).reshape(2, 1, HGW)
    o0, o1, hg_ck = _hg_scan_fwd(p_hg, lb3)
    out_hg = _hg_post_fwd(o0, o1, p_hg, w["hg_g"])

    sh = _shift_fwd(p_rw, w["mu"])
    zpad = jnp.zeros((LORA, RWW), F32)
    prep_w = (w["w0"], jnp.concatenate([w["w2"][0], zpad], 0), jnp.concatenate([zpad, w["w2"][1]], 0),
              w["a0"], jnp.concatenate([w["a2"][0], zpad], 0), jnp.concatenate([zpad, w["a2"][1]], 0), w["kk"], w["ka"])
    kap, lw, b, kd = _rw_prep_fwd(sh, *prep_w)
    y0, y1, rw_ck, rw_tm = _rw_scan_fwd(sh, kap, lw, b, kd)
    post_w = (w["rk"], w["gng"], w["gnb"])
    out_rw = _rw_post_fwd(y0, y1, kd, sh, p_z, *post_w)

    a, bb, merged = _merge1_fwd(out_hg, out_rw, w["whg_t"], w["wrw_t"], p_g)
    dxo, dmo, loss, dgate, d_fg = _head_fwd_bwd(x, merged, w["wout"], mods, w["final_g"], tgt)
    da, db, dp_g = _merge_bwd(dmo, w["wout"], p_g, a, bb)
    g_wout = _dw_tn(merged, dmo, 0, D, 0, "dw_out")
    g_whg = _dw_tn(da, out_hg, 0, D, 0, "dw_hg")
    g_wrw = _dw_tn(db, out_rw, 0, D, 0, "dw_rw")
    token, early = start_exchange(dict(wout=g_wout, whg_t=g_whg, wrw_t=g_wrw), "early")
    d_out_hg = _mm_nn(da, w["whg_t"], 0, "dx_hg", after=token)
    d_out_rw = _mm_nn(db, w["wrw_t"], 0, "dx_rw")

    do, dz_hg, d_hg_g = _hg_post_bwd(o0, o1, p_hg, w["hg_g"], d_out_hg)
    *hg_grads, dlb3 = _hg_scan_bwd(p_hg, lb3, hg_ck, do)
    dp_hg = _hg_dproj(*hg_grads, dz_hg)
    d_lb0, d_lb1 = _lb_bwd(w["lb0"], w["lb1"], dlb3.reshape(2, HGW))

    dy, dkd_p, dr_p, dv_p, dp_z, d_rk, d_gng, d_gnb = _rw_post_bwd(y0, y1, kd, sh, p_z, *post_w, d_out_rw)
    rw_grads = _rw_scan_bwd(sh, kap, lw, b, kd, rw_ck, rw_tm, dy)
    dsh, d_w0, d_w2p0, d_w2p1, d_a0, d_a2p0, d_a2p1, d_kk, d_ka = _rw_prep_bwd(sh, *prep_w, rw_grads, dkd_p, dr_p, dv_p)
    dp_rw, d_mu = _shift_bwd(p_rw, dsh, w["mu"])

    g_win = _dw_tn(dp_hg, h, 0, dm["NCOLS"], 0, "dw_in_hg")
    g_win = _dw_tn(dp_rw, h, 0, dm["NCOLS"], nb(hgc), "dw_in_rw", prev=g_win)
    g_win = _dw_tn(dp_z, h, TC, dm["NCOLS"], nb(hgc + sh_w), "dw_in_z", prev=g_win)
    g_win = _dw_tn(dp_g, h, TC, dm["NCOLS"], nb(hgc + sh_w + RWW), "dw_in_g", prev=g_win)
    token, late = start_exchange(dict(win_t=g_win), "late")
    in_flight = (early, late)
    dh_hg = _mm_nn(dp_hg, win_t, 0, "dh_hg", after=token)
    dh_rw = _mm_nn(dp_rw, win_t, nb(hgc), "dh_rw")
    dh_z = _mm_nn(dp_z, win_t, nb(hgc + sh_w), "dh_z")
    dh_g = _mm_nn(dp_g, win_t, nb(hgc + sh_w + RWW), "dh_g")
    gx, d_ng, dmods = _h_bwd(ctx, x, mods, w["norm_g"], dh_hg, dh_rw, dh_z, dh_g, dxo, dgate)

    small = dict(norm_g=d_ng, lb0=d_lb0, lb1=d_lb1, hg_g=d_hg_g, mu=d_mu, w0=d_w0,
                 w2=jnp.stack([d_w2p0[:LORA], d_w2p1[LORA:]]), a0=d_a0, a2=jnp.stack([d_a2p0[:LORA], d_a2p1[LORA:]]),
                 kk=d_kk, ka=d_ka, rk=d_rk, gng=d_gng, gnb=d_gnb, final_g=d_fg)
    return loss, gx, dmods, in_flight, small


MESH = pl.DeviceIdType.MESH


def _comm_call(body, name, bufs, out_shapes, nsem, nloc):
    hbm = pl.BlockSpec(memory_space=pl.ANY)
    return pl.pallas_call(
        body, name=name, in_specs=[hbm] * len(bufs), out_specs=[hbm] * len(out_shapes), out_shape=out_shapes,
        scratch_shapes=[pltpu.SemaphoreType.DMA((nsem,)), pltpu.SemaphoreType.DMA((nsem,)), pltpu.SemaphoreType.DMA((nloc,))],
    )(*bufs)


GATHER_PIECE_BYTES = 4 * 1024 * 1024


def _gather2(bufs, name):
    pieces = []
    for bi, b in enumerate(bufs):
        rows = b.shape[0]
        if rows % 32 == 0 and b.size * b.dtype.itemsize > GATHER_PIECE_BYTES:
            pieces += [(bi, 0, rows // 2), (bi, rows // 2, rows // 2)]
        else:
            pieces.append((bi, 0, rows))
    nbuf = len(bufs)

    def body(*refs):
        ins, outs = refs[:nbuf], refs[nbuf:2 * nbuf]
        send_sems, recv_sems, _ = refs[2 * nbuf:]
        x, y, c = lax.axis_index("x"), lax.axis_index("y"), lax.axis_index("c")
        me, sib = (x, y, c), (x, y, 1 - c)
        flip = lambda a, b: a + b - 2 * a * b
        xn, yn, diag = (1 - x, y), (x, 1 - y), (1 - x, 1 - y)
        relay_from, relay_to = (flip(x, c), flip(1 - y, c)), (flip(1 - x, c), flip(y, c))

        def copy(pi, k, block, to, own=False):
            bi, r0, nr = pieces[pi]
            rows = outs[bi].at[4 * block[0] + 2 * block[1] + block[2], pl.ds(r0, nr)]
            return pltpu.make_async_remote_copy(src_ref=ins[bi].at[pl.ds(r0, nr)] if own else rows, dst_ref=rows,
                                                send_sem=send_sems.at[7 * pi + k], recv_sem=recv_sems.at[7 * pi + k],
                                                device_id=to, device_id_type=MESH)

        started = []
        for pi in range(len(pieces)):
            started += [copy(pi, 0, me, sib, own=True), copy(pi, 1, me, (*xn, c), own=True), copy(pi, 2, me, (*yn, c), own=True)]
        for cp in started:
            cp.start()
        for pi in range(len(pieces)):
            for k, chip in ((1, xn), (2, yn)):
                copy(pi, k, (*chip, c), me).wait_recv()
                started.append(copy(pi, 3 + k, (*chip, c), sib))
                started[-1].start()
            started.append(copy(pi, 3, (*relay_from, c), (*relay_to, c)))
            started[-1].start()
        for pi in range(len(pieces)):
            copy(pi, 3, (*diag, c), me).wait_recv()
            started.append(copy(pi, 6, (*diag, c), sib))
            started[-1].start()
        for pi in range(len(pieces)):
            copy(pi, 0, sib, me).wait_recv()
            for k, chip in ((4, xn), (5, yn), (6, diag)):
                copy(pi, k, (*chip, 1 - c), me).wait_recv()
        for cp in started:
            cp.wait_send()

    return _comm_call(body, name, bufs, [_sds((NDEV,) + b.shape, b.dtype) for b in bufs], 7 * len(pieces), 1)


def _gather_all(bufs, name):
    me = 4 * lax.axis_index("x") + 2 * lax.axis_index("y") + lax.axis_index("c")
    return [lax.dynamic_update_index_in_dim(g, b, me, axis=0) for g, b in zip(_gather2(bufs, name), bufs)]


def _pair_exchange(bufs, name):
    nbuf = len(bufs)

    def body(*refs):
        ins, got = refs[:nbuf], refs[nbuf:2 * nbuf]
        send_sems, recv_sems, _ = refs[2 * nbuf:]
        x, y, c = lax.axis_index("x"), lax.axis_index("y"), lax.axis_index("c")
        copies = []
        for bi in range(nbuf):
            for q in range(4):
                copies.append(pltpu.make_async_remote_copy(
                    src_ref=ins[bi].at[2 * q + 1 - c], dst_ref=got[bi].at[q], send_sem=send_sems.at[4 * bi + q],
                    recv_sem=recv_sems.at[4 * bi + q], device_id=(x, y, 1 - c), device_id_type=MESH))
                copies[-1].start()
        for cp in copies:
            cp.wait_send()
            cp.wait_recv()

    return _comm_call(body, name, bufs, [_sds((4,) + b.shape[1:], b.dtype) for b in bufs], 4 * nbuf, 1)


def _chip_copies(srcs, lands, send_sems, recv_sems):
    x, y, c = lax.axis_index("x"), lax.axis_index("y"), lax.axis_index("c")
    myq = 2 * x + y
    pairs = []
    for bi in range(len(srcs)):
        for j, (qx, qy) in enumerate([(1 - x, y), (x, 1 - y), (1 - x, 1 - y)]):
            q = 2 * qx + qy
            mk = lambda src, dst: pltpu.make_async_remote_copy(
                src_ref=src, dst_ref=dst, send_sem=send_sems.at[3 * bi + j], recv_sem=recv_sems.at[3 * bi + j],
                device_id=(qx, qy, c), device_id_type=MESH)
            pairs.append((mk(srcs[bi].at[q], lands[bi].at[myq]), mk(srcs[bi].at[myq], lands[bi].at[q])))
    return pairs


def _chip_exchange_start(bufs, name):
    nbuf = len(bufs)
    hbm, sem = pl.BlockSpec(memory_space=pltpu.HBM), pl.BlockSpec(memory_space=pltpu.SEMAPHORE)

    def body(*refs):
        srcs, lands = refs[:nbuf], refs[nbuf:2 * nbuf]
        send_sems, recv_sems = refs[2 * nbuf], refs[2 * nbuf + 1]
        token = refs[-1]
        for send, _ in _chip_copies(srcs, lands, send_sems, recv_sems):
            send.start()
        token[...] = jnp.zeros_like(token)

    held = [pltpu.HBM(b.shape, b.dtype) for b in bufs]
    outs = pl.pallas_call(
        body, name=name, in_specs=[hbm] * (2 * nbuf), out_specs=(sem, sem, *[hbm] * (2 * nbuf), pl.BlockSpec(memory_space=pltpu.VMEM)),
        out_shape=(pltpu.SemaphoreType.DMA((3 * nbuf,)), pltpu.SemaphoreType.DMA((3 * nbuf,)), *held, *held, _sds((8, 128))),
        input_output_aliases={i: 2 + i for i in range(2 * nbuf)},
        compiler_params=pltpu.CompilerParams(has_side_effects=pltpu.SideEffectType.DATAFLOW_SIDE_EFFECTING),
    )(*[pltpu.with_memory_space_constraint(b, pltpu.HBM) for b in bufs],
      *[pltpu.with_memory_space_constraint(lax.empty(b.shape, b.dtype), pltpu.HBM) for b in bufs])
    return outs[0], outs[1], list(outs[2:2 + nbuf]), list(outs[2 + nbuf:2 + 2 * nbuf]), outs[-1]


def _chip_exchange_wait(send_sems, recv_sems, srcs, lands, after, name):
    nbuf = len(srcs)
    hbm, sem = pl.BlockSpec(memory_space=pltpu.HBM), pl.BlockSpec(memory_space=pltpu.SEMAPHORE)

    def body(*refs):
        src_refs, land_refs = refs[:nbuf], refs[nbuf:2 * nbuf]
        for send, recv in _chip_copies(src_refs, land_refs, refs[2 * nbuf], refs[2 * nbuf + 1]):
            send.wait_send()
            recv.wait_recv()

    held = [pltpu.HBM(b.shape, b.dtype) for b in srcs]
    outs = pl.pallas_call(
        body, name=name, in_specs=[hbm] * (2 * nbuf) + [sem, sem, pl.BlockSpec(memory_space=pl.ANY)], out_specs=[hbm] * (2 * nbuf),
        out_shape=(*held, *held), input_output_aliases={i: i for i in range(2 * nbuf)},
        compiler_params=pltpu.CompilerParams(has_side_effects=pltpu.SideEffectType.DATAFLOW_SIDE_EFFECTING),
    )(*srcs, *lands, send_sems, recv_sems, after)
    return list(outs[:nbuf]), list(outs[nbuf:])


def _prefetch_call(body, name, scalars, grid, in_specs, out_specs, out_shape, args):
    return pl.pallas_call(
        body, name=name, out_shape=out_shape,
        grid_spec=pltpu.PrefetchScalarGridSpec(num_scalar_prefetch=1, grid=grid, in_specs=in_specs, out_specs=out_specs),
        compiler_params=pltpu.CompilerParams(dimension_semantics=("parallel",) * len(grid), vmem_limit_bytes=V7X_VMEM_LIMIT))(scalars, *args)


def _pair_add(g, got, core, name):
    _, rows, cols = got.shape
    tr = _row_tile(rows, 16, 256)

    def body(c_ref, a_ref, b_ref, o_ref):
        o_ref[...] = (a_ref[...].astype(F32) + b_ref[...].astype(F32)).astype(o_ref.dtype)

    blk = pl.BlockSpec((4, tr, cols), lambda i, c_ref: (0, i, 0))
    mine = pl.BlockSpec((4, None, tr, cols), lambda i, c_ref: (0, c_ref[0], i, 0))
    return _prefetch_call(body, name, core, (rows // tr,), [mine, blk], blk, _sds(got.shape, got.dtype),
                          (g.reshape(4, 2, rows, cols), got))


def _reduce_chips(chip_sums, recv, slots, name):
    _, rows, cols = recv.shape
    tr = _row_tile(rows, 16, 128)

    def body(s_ref, own_ref, r1_ref, r2_ref, r3_ref, o_ref):
        o_ref[...] = ((own_ref[...].astype(F32) + r1_ref[...].astype(F32)) + r2_ref[...].astype(F32)) + r3_ref[...].astype(F32)

    pick = lambda k: pl.BlockSpec((None, tr, cols), lambda i, s_ref: (s_ref[k], i, 0))
    return _prefetch_call(body, name, slots, (rows // tr,), [pick(0), pick(1), pick(2), pick(3)],
                          pl.BlockSpec((tr, cols), lambda i, s_ref: (i, 0)), _sds((rows, cols)), (chip_sums, recv, recv, recv))


def _row_tile(rows, mult, cap):
    best = mult
    for t in range(mult, cap + 1, mult):
        if rows % t == 0:
            best = t
    assert rows % best == 0, (rows, mult)
    return best


def _reduce_sources(r, name):
    nsrc, rows, cols = r.shape
    tr = _row_tile(rows, 16 if r.dtype.itemsize == 2 else 8, 128)

    def body(r_ref, o_ref):
        acc = r_ref[0].astype(F32)
        for j in range(1, nsrc):
            acc = acc + r_ref[j].astype(F32)
        o_ref[...] = acc

    return _call(body, name, (rows // tr,), [pl.BlockSpec((nsrc, tr, cols), lambda i: (0, i, 0))],
                 pl.BlockSpec((tr, cols), lambda i: (i, 0)), _sds((rows, cols)), sem=("parallel",))(r)


def _adamw(w, g, m, v, name):
    rows, cols = w.shape
    tr = _row_tile(rows, 8, 128)
    c1 = 1.0 - ADAM_B1 ** ADAM_STEP
    c2 = 1.0 - ADAM_B2 ** ADAM_STEP

    def body(w_ref, g_ref, m_ref, v_ref, d_ref, mo_ref, vo_ref):
        g_ = g_ref[...]
        m_ = ADAM_B1 * m_ref[...] + (1.0 - ADAM_B1) * g_
        v_ = ADAM_B2 * v_ref[...] + (1.0 - ADAM_B2) * (g_ * g_)
        d_ref[...] = -ADAM_LR * ((m_ / c1) / (jnp.sqrt(v_ / c2) + ADAM_EPS) + ADAM_WD * w_ref[...])
        mo_ref[...] = m_
        vo_ref[...] = v_

    blk = pl.BlockSpec((tr, cols), lambda i: (i, 0))
    return _call(body, name, (rows // tr,), [blk] * 4, [blk] * 3, [_sds((rows, cols))] * 3, sem=("parallel",))(w, g, m, v)


SLAB_PART = 1024


def _pack(arrs):
    parts = []
    for a in arrs:
        flat = a.reshape(-1)
        pad = (-flat.shape[0]) % SLAB_PART
        if pad:
            flat = jnp.concatenate([flat, jnp.zeros((pad,), flat.dtype)])
        parts.append(flat.reshape(-1, 128))
    return jnp.concatenate(parts, axis=0)


def _unpack(slab, shapes):
    outs, row = [], 0
    for s in shapes:
        n = 1
        for d in s:
            n *= d
        rows = (n + SLAB_PART - 1) // SLAB_PART * (SLAB_PART // 128)
        outs.append(slab[row:row + rows].reshape(-1)[:n].reshape(s))
        row += rows
    return outs


def _unshard_last(g, lead):
    nl = len(lead)
    return jnp.transpose(g, tuple(range(1, nl + 1)) + (0, nl + 1)).reshape(tuple(lead) + (-1,))


def kernel(x, c, ctx, c_ctx, ada_w, ada_b, norm_g, w_in, hg_lb, hg_norm_g, rw_mu, rw_w0, rw_w2, rw_a0, rw_a2, rw_kk, rw_ka, rw_rk, rw_gn_g, rw_gn_b, w_hg_out, w_rw_out, w_out, final_g, loss_target, m_c_ctx, m_ada_w, m_ada_b, m_norm_g, m_w_in, m_hg_lb, m_hg_norm_g, m_rw_mu, m_rw_w0, m_rw_w2, m_rw_a0, m_rw_a2, m_rw_kk, m_rw_ka, m_rw_rk, m_rw_gn_g, m_rw_gn_b, m_w_hg_out, m_w_rw_out, m_w_out, m_final_g, v_c_ctx, v_ada_w, v_ada_b, v_norm_g, v_w_in, v_hg_lb, v_hg_norm_g, v_rw_mu, v_rw_w0, v_rw_w2, v_rw_a0, v_rw_a2, v_rw_kk, v_rw_ka, v_rw_rk, v_rw_gn_g, v_rw_gn_b, v_w_hg_out, v_w_rw_out, v_w_out, v_final_g):
    dm = _dims()
    me = 4 * lax.axis_index("x") + 2 * lax.axis_index("y") + lax.axis_index("c")

    big_shards = [w_in[0].T.astype(BF16), w_out[0].astype(BF16), w_hg_out[0].T.astype(BF16), w_rw_out[0].T.astype(BF16)]
    sharded_small = [hg_lb, rw_mu[0], rw_w0[0], rw_w2[0], rw_a0[0], rw_a2[0]]
    c_rows = jnp.concatenate([c, jnp.zeros((7, D), F32)], axis=0)
    gathered = _gather_all(big_shards + [_pack(sharded_small), c_rows], "gather_weights")
    win_t, wout, whg_t, wrw_t = [g.reshape(-1, g.shape[2]) for g in gathered[:4]]
    per_dev = jax.vmap(lambda s: _unpack(s, [a.shape for a in sharded_small]))(gathered[4])
    hg_lb_f, mu_f, w0_f, w2_f, a0_f, a2_f = [_unshard_last(p, p.shape[1:-1]) for p in per_dev]

    ncol = ada_w.shape[2]
    cc16 = jnp.concatenate([gathered[5][:, 0], c_ctx[None], jnp.zeros((7, D), F32)], axis=0)
    mod_all = _gather_all([_ada_fwd(cc16, ada_w[0])], "gather_mods")[0]
    mod_x = lax.dynamic_index_in_dim(mod_all, me, axis=1, keepdims=False).reshape(1, -1) + ada_b
    mod_c = mod_all[:, NDEV].reshape(1, -1) + ada_b
    mods = jnp.concatenate([mod_x, mod_c, jnp.zeros((6, 3 * D), F32)], axis=0)

    w = dict(win_t=win_t, wout=wout, whg_t=whg_t, wrw_t=wrw_t, norm_g=norm_g,
             lb0=hg_lb_f[:, 0], lb1=hg_lb_f[:, 1], hg_g=hg_norm_g, mu=mu_f, w0=w0_f, w2=w2_f, a0=a0_f, a2=a2_f,
             kk=rw_kk, ka=rw_ka, rk=rw_rk, gng=rw_gn_g, gnb=rw_gn_b, final_g=final_g[None])
    def start_exchange(grads_bf16, tag):
        names = list(grads_bf16)
        blocks = [grads_bf16[n].reshape(NDEV, -1, grads_bf16[n].shape[1]) for n in names]
        got = _pair_exchange(blocks, "pair_grads_" + tag)
        core = lax.axis_index("c").astype(jnp.int32).reshape(1)
        chip_sums = [_pair_add(g, r, core, "pair_add_" + n) for g, r, n in zip(blocks, got, names)]
        send_sems, recv_sems, held, lands, token = _chip_exchange_start(chip_sums, "chip_grads_start_" + tag)
        return token, (names, send_sems, recv_sems, held, lands)

    loss_dev, grad_x, dmods, in_flight, small = _local_step(x[0], ctx[0], loss_target[0], mods, w, start_exchange)
    loss = lax.psum(loss_dev[0, 0], AXES)

    my_chip = (2 * lax.axis_index("x") + lax.axis_index("y")).astype(jnp.int32)
    others = jnp.arange(3, dtype=jnp.int32)
    slots = jnp.concatenate([my_chip.reshape(1), others + (others >= my_chip).astype(jnp.int32)])

    def finish_exchange(tag, flight, after):
        names, *copies = flight
        chip_sums, recv = _chip_exchange_wait(*copies, after, "chip_grads_wait_" + tag)
        return {n: _reduce_chips(cs, r, slots, "reduce_" + n) for n, cs, r in zip(names, chip_sums, recv)}

    reduced = finish_exchange("early", in_flight[0], grad_x)

    dmods_all = _gather_all([dmods], "gather_dmods")[0]
    my_cols = lambda a: lax.dynamic_slice_in_dim(a, me * ncol, ncol, axis=1)
    g_ada, dcc16, d_ada_b = _ada_bwd(cc16, ada_w[0], my_cols(dmods_all[:, 0]), my_cols(dmods_all[:, 1]), dmods)
    small = dict(small, c_ctx=dcc16[NDEV:NDEV + 1], ada_b=d_ada_b)

    small_names = ["c_ctx", "ada_b", "norm_g", "lb0", "lb1", "hg_g", "mu", "w0", "w2", "a0", "a2", "kk", "ka", "rk", "gng", "gnb", "final_g"]
    small_slab = _pack([small[n] for n in small_names])
    small_all = _gather_all([small_slab], "gather_small_grads")[0]
    small_sum = _reduce_sources(small_all, "reduce_small")
    sg = dict(zip(small_names, _unpack(small_sum, [small[n].shape for n in small_names])))

    reduced.update(finish_exchange("late", in_flight[1], small_sum))
    g_win_t, g_wout, g_whg_t, g_wrw_t = [reduced[n] for n in ("win_t", "wout", "whg_t", "wrw_t")]

    def my_shard(full):
        n = full.shape[-1] // NDEV
        return lax.dynamic_slice_in_dim(full, me * n, n, axis=full.ndim - 1)

    grads = dict(
        c_ctx=sg["c_ctx"][0], ada_w=g_ada[None], ada_b=sg["ada_b"], norm_g=sg["norm_g"], w_in=g_win_t.T[None],
        hg_lb=my_shard(jnp.stack([sg["lb0"], sg["lb1"]], axis=1)), hg_norm_g=sg["hg_g"], rw_mu=my_shard(sg["mu"])[None],
        rw_w0=my_shard(sg["w0"])[None], rw_w2=my_shard(sg["w2"])[None], rw_a0=my_shard(sg["a0"])[None], rw_a2=my_shard(sg["a2"])[None],
        rw_kk=sg["kk"], rw_ka=sg["ka"], rw_rk=sg["rk"], rw_gn_g=sg["gng"], rw_gn_b=sg["gnb"],
        w_hg_out=g_whg_t.T[None], w_rw_out=g_wrw_t.T[None], w_out=g_wout[None], final_g=sg["final_g"][0])
    weights = dict(c_ctx=c_ctx, ada_w=ada_w, ada_b=ada_b, norm_g=norm_g, w_in=w_in, hg_lb=hg_lb, hg_norm_g=hg_norm_g, rw_mu=rw_mu,
                   rw_w0=rw_w0, rw_w2=rw_w2, rw_a0=rw_a0, rw_a2=rw_a2, rw_kk=rw_kk, rw_ka=rw_ka, rw_rk=rw_rk, rw_gn_g=rw_gn_g,
                   rw_gn_b=rw_gn_b, w_hg_out=w_hg_out, w_rw_out=w_rw_out, w_out=w_out, final_g=final_g)
    m_in = dict(zip(weights, (m_c_ctx, m_ada_w, m_ada_b, m_norm_g, m_w_in, m_hg_lb, m_hg_norm_g, m_rw_mu, m_rw_w0, m_rw_w2, m_rw_a0,
                              m_rw_a2, m_rw_kk, m_rw_ka, m_rw_rk, m_rw_gn_g, m_rw_gn_b, m_w_hg_out, m_w_rw_out, m_w_out, m_final_g)))
    v_in = dict(zip(weights, (v_c_ctx, v_ada_w, v_ada_b, v_norm_g, v_w_in, v_hg_lb, v_hg_norm_g, v_rw_mu, v_rw_w0, v_rw_w2, v_rw_a0,
                              v_rw_a2, v_rw_kk, v_rw_ka, v_rw_rk, v_rw_gn_g, v_rw_gn_b, v_w_hg_out, v_w_rw_out, v_w_out, v_final_g)))

    big_w = ["ada_w", "w_in", "w_hg_out", "w_rw_out", "w_out"]
    delta, new_m, new_v = {}, {}, {}
    for n in big_w:
        shp = weights[n].shape
        two = lambda a: a.reshape(shp[-2], shp[-1])
        d_, m_, v_ = _adamw(two(weights[n]), two(grads[n]), two(m_in[n]), two(v_in[n]), "adamw_" + n)
        delta[n], new_m[n], new_v[n] = d_.reshape(shp), m_.reshape(shp), v_.reshape(shp)
    rest = [n for n in weights if n not in big_w]
    shapes = [weights[n].shape for n in rest]
    d_s, m_s, v_s = _adamw(_pack([weights[n] for n in rest]), _pack([grads[n] for n in rest]), _pack([m_in[n] for n in rest]),
                           _pack([v_in[n] for n in rest]), "adamw_small")
    for n, d_, m_, v_ in zip(rest, _unpack(d_s, shapes), _unpack(m_s, shapes), _unpack(v_s, shapes)):
        delta[n], new_m[n], new_v[n] = d_, m_, v_

    order = list(weights)
    return (loss, grad_x[None], *[grads[n] for n in order], *[delta[n] for n in order],
            *[new_m[n] for n in order], *[new_v[n] for n in order])
```

```python
import functools

import jax
import jax.numpy as jnp
from jax import lax
from jax.experimental import pallas as pl
from jax.experimental.pallas import tpu as pltpu

D = 2048
T = 2048
TC = 256
GW = 64
C = 64
HGW = 1024
RWW = 1024
LORA = 64
NDEV = 8
TM = 256
TMV = 128
NORM_EPS = 1e-6
RW_GN_EPS = 64e-5
ADAM_LR, ADAM_B1, ADAM_B2, ADAM_EPS, ADAM_WD, ADAM_STEP = 0.001, 0.9, 0.999, 1e-08, 0.01, 10
WB = 256
V7X_VMEM_LIMIT = 56 * 1024 * 1024

F32 = jnp.float32
BF16 = jnp.bfloat16
AXES = ("x", "y", "c")


def _dims():
    tt = T + TC
    sh = 3 * RWW + 4 * LORA
    hgc = 5 * HGW
    return dict(TT=tt, NCC=TC // C, NCH=tt // C, HG_H=HGW // 128, RW_P=RWW // 128, SH=sh, HGC=hgc,
                NCOLS=hgc + sh + RWW + 2 * D, NCT=TC // TM, NCTV=TC // TMV)


def _mm(a, b):
    return _dot3(a, b, "nn")


def _mm_nt(a, b):
    return _dot3(a, b, "nt")


def _mm_tn(a, b):
    return _dot3(a, b, "tn")


def _split(x):
    hi = x.astype(BF16)
    return hi, (x - hi.astype(F32)).astype(BF16)


_FORMS = {"nn": (((1,), (0,)), ((), ())), "nt": (((1,), (1,)), ((), ())), "tn": (((0,), (0,)), ((), ()))}


def _dot3_raw(a, b, form):
    ah, al = _split(a)
    bh, bl = _split(b)
    d = lambda x, y: lax.dot_general(x, y, _FORMS[form], preferred_element_type=F32)
    return d(ah, bh) + (d(ah, bl) + d(al, bh))


@functools.partial(jax.custom_vjp, nondiff_argnums=(2,))
def _dot3(a, b, form):
    return _dot3_raw(a, b, form)


def _dot3_fwd(a, b, form):
    return _dot3_raw(a, b, form), (a, b)


def _dot3_bwd(form, res, g):
    a, b = res
    if form == "nn":
        return _dot3(g, b, "nt"), _dot3(a, g, "tn")
    if form == "nt":
        return _dot3(g, b, "nn"), _dot3(g, a, "tn")
    return _dot3(b, g, "nt"), _dot3(a, g, "nn")


_dot3.defvjp(_dot3_fwd, _dot3_bwd)


def _scan_cumsum(inc, x):
    return _cumsum_vjp(inc.astype(BF16), x)


def _cumsum_raw(inc, x, form):
    h1 = x.astype(BF16)
    r1 = x - h1.astype(F32)
    h2 = r1.astype(BF16)
    h3 = (r1 - h2.astype(F32)).astype(BF16)
    d = lambda y: lax.dot_general(inc, y, _FORMS[form], preferred_element_type=F32)
    return d(h1) + (d(h2) + d(h3))


@jax.custom_vjp
def _cumsum_vjp(inc, x):
    return _cumsum_raw(inc, x, "nn")


_cumsum_vjp.defvjp(lambda inc, x: (_cumsum_raw(inc, x, "nn"), inc),
                   lambda inc, g: (jnp.zeros_like(inc), _cumsum_raw(inc, g, "tn")))


def _silu(x):
    return x * jax.nn.sigmoid(x)


def _softplus(x):
    return jnp.maximum(x, 0.0) + jnp.log(1.0 + jnp.exp(-jnp.abs(x)))


def _iota2(shape, dim):
    return lax.broadcasted_iota(jnp.int32, shape, dim)


def _pair_mask():
    return (_iota2((128, 128), 0) < 64) == (_iota2((128, 128), 1) < 64)


def _seg64_raw(x):
    e = _pair_mask().astype(BF16)
    parts = []
    for g in range(x.shape[1] // 128):
        hi, lo = _split(x[:, g * 128:(g + 1) * 128])
        parts.append(jnp.dot(hi, e, preferred_element_type=F32) + jnp.dot(lo, e, preferred_element_type=F32))
    return parts[0] if len(parts) == 1 else jnp.concatenate(parts, axis=1)


@jax.custom_vjp
def _seg64_sum(x):
    return _seg64_raw(x)


_seg64_sum.defvjp(lambda x: (_seg64_raw(x), None), lambda _, g: (_seg64_raw(g),))


def _mm1(a, b):
    return jnp.dot(a.astype(BF16), b.astype(BF16), preferred_element_type=F32)


def _mm1_nt(a, b):
    return lax.dot_general(a.astype(BF16), b.astype(BF16), (((1,), (1,)), ((), ())), preferred_element_type=F32)


def _seg128_mean(x):
    parts = [jnp.broadcast_to(jnp.mean(x[:, g * 128:(g + 1) * 128], axis=1, keepdims=True), (x.shape[0], 128))
             for g in range(x.shape[1] // 128)]
    return parts[0] if len(parts) == 1 else jnp.concatenate(parts, axis=1)


def _scan_masks(d):
    lag = (_iota2((C, C), 0) - _iota2((C, C), 1)) * (1 - 2 * d)
    return lag >= 0, lag > 0


def _tri_inverse(a):
    eye = (_iota2((C, C), 0) == _iota2((C, C), 1)).astype(F32)
    p = [-x for x in a]
    tm = [eye + x for x in p]
    n = 2
    while n < C:
        p = [_mm(x, x) for x in p]
        tm = [t + _mm(t, x) for t, x in zip(tm, p)]
        n *= 2
    return tm


@jax.custom_vjp
def _tri_solve(tm, a, rhs):
    return [_mm(t, r) for t, r in zip(tm, rhs)]


def _tri_solve_fwd(tm, a, rhs):
    u = [_mm(t, r) for t, r in zip(tm, rhs)]
    return u, (tm, u)


def _tri_solve_bwd(res, g):
    tm, u = res
    d_rhs = [_mm_tn(t, x) for t, x in zip(tm, g)]
    return [jnp.zeros_like(t) for t in tm], [-_mm_nt(d, x) for d, x in zip(d_rhs, u)], d_rhs


_tri_solve.defvjp(_tri_solve_fwd, _tri_solve_bwd)


def _rw_chunks(chains, tms=None):
    s0, r, lw, kap, b, v, kd, ds = zip(*chains)
    ids = range(len(chains))
    inc, strict = zip(*[_scan_masks(d) for d in ds])
    lane = _iota2((1, 128), 1)
    hm = ((lane < 64).astype(F32), (lane >= 64).astype(F32))
    lc = [_scan_cumsum(inc[i], lw[i]) for i in ids]
    ltot = [jnp.sum(lw[i], axis=0, keepdims=True) for i in ids]
    rt = [r[i] * jnp.exp(lc[i]) for i in ids]
    kt = [kap[i] * jnp.exp(lc[i] - lw[i]) for i in ids]
    einv = [jnp.exp(-lc[i]) for i in ids]
    kh = [kd[i] * einv[i] for i in ids]
    bh = [b[i] * einv[i] for i in ids]
    eend = [jnp.exp(ltot[i] - lc[i]) for i in ids]
    kbar = [kd[i] * eend[i] for i in ids]
    bbar = [b[i] * eend[i] for i in ids]
    lhs = [jnp.concatenate([kt[i] * hm[0], kt[i] * hm[1], rt[i] * hm[0], rt[i] * hm[1]], axis=0) for i in ids]
    gk = [_mm_nt(lhs[i], kh[i]) for i in ids]
    gb = [_mm_nt(lhs[i], bh[i]) for i in ids]
    ks = [_mm_nt(kt[i], s0[i]) for i in ids]
    ys = [_mm_nt(rt[i], s0[i]) for i in ids]
    heads = [(i, h) for i in ids for h in range(2)]
    akk = {c: jnp.where(strict[c[0]], gk[c[0]][c[1] * C:(c[1] + 1) * C], 0.0) for c in heads}
    akb = {c: jnp.where(strict[c[0]], gb[c[0]][c[1] * C:(c[1] + 1) * C], 0.0) for c in heads}
    ark = {c: jnp.where(inc[c[0]], gk[c[0]][(2 + c[1]) * C:(3 + c[1]) * C], 0.0) for c in heads}
    arb = {c: jnp.where(inc[c[0]], gb[c[0]][(2 + c[1]) * C:(3 + c[1]) * C], 0.0) for c in heads}
    vh = {c: v[c[0]] * hm[c[1]] for c in heads}
    av = {c: _mm(akk[c], vh[c]) for c in heads}
    rhs = {c: ks[c[0]] * hm[c[1]] + av[c] for c in heads}
    akb_list = [akb[c] for c in heads]
    tm = _tri_inverse(akb_list) if tms is None else [tms[c] for c in heads]
    uh = dict(zip(heads, _tri_solve(tm, akb_list, [rhs[c] for c in heads])))
    yv = {c: _mm(ark[c], vh[c]) for c in heads}
    yu = {c: _mm(arb[c], uh[c]) for c in heads}
    u = [uh[i, 0] + uh[i, 1] for i in ids]
    y = [ys[i] + (yv[i, 0] - yu[i, 0]) + (yv[i, 1] - yu[i, 1]) for i in ids]
    upd = [_mm_tn(jnp.concatenate([v[i], -u[i]], axis=0), jnp.concatenate([kbar[i], bbar[i]], axis=0)) for i in ids]
    s1 = [s0[i] * jnp.exp(ltot[i]) + jnp.where(_pair_mask(), upd[i], 0.0) for i in ids]
    if tms is None:
        return [(y[i], s1[i], tm[2 * i], tm[2 * i + 1]) for i in ids]
    return [(y[i], s1[i]) for i in ids]


HG_SUB = 16


def _hg_chunks(chains):
    s0, qr, iv, f, lb, ds = zip(*chains)
    ids = range(len(chains))
    inc = [_scan_masks(d)[0] for d in ds]
    q = [_silu(qr[i]) for i in ids]
    fg = [lb[i] + (1.0 - lb[i]) * jax.nn.sigmoid(f[i]) for i in ids]
    k = [1.0 - fg[i] for i in ids]
    g = [jnp.log(fg[i]) for i in ids]
    bc = [_scan_cumsum(inc[i], g[i]) for i in ids]
    btot = [jnp.sum(g[i], axis=0, keepdims=True) for i in ids]
    o_inter = [_mm_nt(q[i] * jnp.exp(bc[i]), s0[i]) for i in ids]
    rowi = _iota2((C, 1), 0)
    outs = [[] for _ in ids]
    for blk in range(C // HG_SUB):
        lo, hi = blk * HG_SUB, (blk + 1) * HG_SUB
        first = [lo if ds[i] == 0 else hi - 1 for i in ids]
        ref = [jnp.sum(jnp.where(rowi == first[i], bc[i] - g[i], 0.0), axis=0, keepdims=True) for i in ids]
        qi = [q[i][lo:hi] * jnp.exp(bc[i][lo:hi] - ref[i]) for i in ids]
        src = [(rowi < hi) if ds[i] == 0 else (rowi >= lo) for i in ids]
        ke = [jnp.where(src[i], k[i] * jnp.exp(jnp.where(src[i], ref[i] - bc[i], 0.0)), 0.0) for i in ids]
        a = [jnp.where(inc[i][lo:hi], _mm_nt(qi[i], ke[i]), 0.0) for i in ids]
        part = [_mm(a[i], iv[i]) for i in ids]
        for i in ids:
            outs[i].append(part[i])
    o = [o_inter[i] + jnp.concatenate(outs[i], axis=0) for i in ids]
    upd = [_mm_tn(iv[i], k[i] * jnp.exp(btot[i] - bc[i])) for i in ids]
    s1 = [s0[i] * jnp.exp(btot[i]) + upd[i] for i in ids]
    return [(o[i], s1[i]) for i in ids]


def _lockstep(fn, ds, nargs):
    def flat_fn(*flat):
        return tuple(fn([tuple(flat[i * nargs:(i + 1) * nargs]) + (d,) for i, d in enumerate(ds)]))

    return flat_fn


def _h_math(x, ng, scale, shift):
    return x * lax.rsqrt(jnp.mean(x * x, axis=-1, keepdims=True) + NORM_EPS) * ng * (1.0 + scale) + shift


def _hg_post_math(o0, o1, z, g):
    o = o0 + o1
    on = o * lax.rsqrt(_seg128_mean(o * o) + NORM_EPS)
    return on * g * _silu(z)


def _rw_prep_math(k, pw0, pw1, pa0, pa1, w0, a0, kkp, kap_p):
    kk = k * kkp
    kap = kk * lax.rsqrt(_seg64_sum(kk * kk) + 1e-12)
    outs = []
    for d, (pw, pa) in enumerate(((pw0, pa0), (pw1, pa1))):
        w_log = -_softplus(-(w0[d:d + 1] + pw)) - 0.5
        lw = -jnp.exp(w_log)
        a = jax.nn.sigmoid(a0[d:d + 1] + pa)
        kd = k * (1.0 + (a - 1.0) * kap_p)
        outs += [lw, kap * a, kd]
    return (kap, *outs)


def _rw_post_math(y0, y1, kd0, kd1, r, v, z, rk, gng, gnb):
    ys, ksum = y0 + y1, kd0 + kd1
    mean = _seg64_sum(ys) * (1.0 / 64.0)
    cen = ys - mean
    var = _seg64_sum(cen * cen) * (1.0 / 64.0)
    yn = cen * lax.rsqrt(var + RW_GN_EPS) * gng + gnb
    bonus = _seg64_sum(r * ksum * rk) * v
    return (yn + bonus) * _silu(z)


def _head_math(x, mo, gate, fg, tgt):
    xo = x + gate * mo
    y = xo * lax.rsqrt(jnp.mean(xo * xo, axis=-1, keepdims=True) + NORM_EPS) * fg
    err = y - tgt
    return 0.5 * jnp.sum(jnp.mean(err * err, axis=-1, keepdims=True), axis=0, keepdims=True)


def _gate_math(ghg, grw, a, b):
    return jax.nn.sigmoid(ghg) * a + jax.nn.sigmoid(grw) * b


def _call(body, name, grid, in_specs, out_specs, out_shape, scratch=(), sem=None, aliases=None):
    return pl.pallas_call(
        body, name=name, grid=grid, in_specs=in_specs, out_specs=out_specs, out_shape=out_shape,
        scratch_shapes=list(scratch), input_output_aliases=aliases or {},
        compiler_params=pltpu.CompilerParams(dimension_semantics=sem, vmem_limit_bytes=V7X_VMEM_LIMIT))


def _full(shape):
    nd = len(shape)
    return pl.BlockSpec(shape, lambda *_: (0,) * nd)


def _sds(shape, dtype=F32):
    return jax.ShapeDtypeStruct(shape, dtype)


def _proj_nt(h, wt, wt_blk0, nblk, row0, nrows, name):
    def body(h_ref, w_ref, o_ref):
        o_ref[...] = lax.dot_general(h_ref[row0:row0 + nrows, :], w_ref[...], (((1,), (1,)), ((), ())), preferred_element_type=F32)

    return _call(body, name, (nblk,), [_full(h.shape), pl.BlockSpec((WB, wt.shape[1]), lambda j: (wt_blk0 + j, 0))],
                 pl.BlockSpec((nrows, WB), lambda j: (0, j)), _sds((nrows, nblk * WB)), sem=("parallel",))(h, wt)


def _dw_tn(a, b, row0, out_rows, out_blk0, name, prev=None):
    m, n = a.shape
    k2 = b.shape[1]
    nblk = n // WB

    def body(*refs):
        a_ref, b_ref, o_ref = refs[0], refs[1], refs[-1]
        o_ref[...] = lax.dot_general(a_ref[...], b_ref[row0:row0 + m, :], (((0,), (0,)), ((), ())),
                                     preferred_element_type=F32).astype(BF16)

    in_specs = [pl.BlockSpec((m, WB), lambda j: (0, j)), _full(b.shape)]
    args = [a, b]
    aliases = None
    if prev is not None:
        in_specs.append(pl.BlockSpec(memory_space=pl.ANY))
        args.append(prev)
        aliases = {2: 0}
    return _call(body, name, (nblk,), in_specs, pl.BlockSpec((WB, k2), lambda j: (out_blk0 + j, 0)), _sds((out_rows, k2), BF16),
                 sem=("arbitrary",), aliases=aliases)(*args)


def _mm_nn(a, w, w_blk0, name, after=None):
    m, kc = a.shape
    n = w.shape[1]
    kb = max(k for k in (WB, 2 * WB, 4 * WB) if kc % k == 0 and (w_blk0 * WB) % k == 0)
    w_blk0 = w_blk0 * WB // kb
    tmb = 3 * TM if m % (3 * TM) == 0 else (2 * TM if m % (2 * TM) == 0 else TM)

    def body(a_ref, w_ref, *rest):
        o_ref = rest[-1]

        @pl.when(pl.program_id(1) == 0)
        def _():
            o_ref[...] = jnp.zeros_like(o_ref)

        o_ref[...] += jnp.dot(a_ref[...], w_ref[...], preferred_element_type=F32)

    extra = [] if after is None else [pl.BlockSpec(memory_space=pl.ANY)]
    return _call(body, name, (m // tmb, kc // kb),
                 [pl.BlockSpec((tmb, kb), lambda i, k: (i, k)), pl.BlockSpec((kb, n), lambda i, k: (w_blk0 + k, 0))] + extra,
                 pl.BlockSpec((tmb, n), lambda i, k: (i, 0)), _sds((m, n)), sem=("parallel", "arbitrary"))(
        a, w, *([] if after is None else [after]))


def _ada_fwd(cc16, ada_shard):
    ncol = ada_shard.shape[1]

    def body(c_ref, w_ref, o_ref):
        o_ref[...] = _mm(_silu(c_ref[...]), w_ref[...])

    return _call(body, "ada_fwd", (1,), [_full((16, D)), _full((D, ncol))], _full((16, ncol)), _sds((16, ncol)))(cc16, ada_shard)


def _ada_bwd(cc16, ada_shard, dm_x, dm_c, dmods):
    ncol = ada_shard.shape[1]

    def body(c_ref, w_ref, dx_ref, dc_ref, dm_ref, dw_ref, dcc_ref, db_ref):
        s, vjp = jax.vjp(_silu, c_ref[...])
        dc_tot = dc_ref[0:1, :]
        for j in range(1, NDEV):
            dc_tot = dc_tot + dc_ref[j:j + 1, :]
        row = _iota2((8, 1), 0)
        dm16 = jnp.concatenate([dx_ref[...], jnp.where(row == 0, dc_tot, 0.0)], axis=0)
        dw_ref[...] = _mm_tn(s, dm16)
        dcc_ref[...] = vjp(_mm_nt(dm16, w_ref[...]))[0]
        db_ref[...] = dm_ref[0:1, :] + dm_ref[1:2, :]

    return _call(body, "ada_bwd", (1,), [_full((16, D)), _full((D, ncol)), _full((8, ncol)), _full((8, ncol)), _full((8, 3 * D))],
                 [_full((D, ncol)), _full((16, D)), _full((1, 3 * D))], [_sds((D, ncol)), _sds((16, D)), _sds((1, 3 * D))])(
        cc16, ada_shard, dm_x, dm_c, dmods)


def _sel_mod(mods_ref, is_ctx, lo):
    return jnp.where(is_ctx, mods_ref[1:2, lo:lo + D], mods_ref[0:1, lo:lo + D])


def _token_specs():
    nct = _dims()["NCT"]
    return (pl.BlockSpec((TM, D), lambda i: (jnp.minimum(i, nct - 1), 0)),
            pl.BlockSpec((TM, D), lambda i: (jnp.maximum(i - nct, 0), 0)))


def _h_fwd(ctx, x, mods, norm_g):
    dm = _dims()

    def body(c_ref, x_ref, m_ref, g_ref, o_ref):
        is_ctx = pl.program_id(0) < dm["NCT"]
        tok = jnp.where(is_ctx, c_ref[...], x_ref[...])
        o_ref[...] = _h_math(tok, g_ref[...], _sel_mod(m_ref, is_ctx, D), _sel_mod(m_ref, is_ctx, 0)).astype(BF16)

    return _call(body, "h_fwd", (dm["TT"] // TM,), [*_token_specs(), _full((8, 3 * D)), _full((1, D))],
                 pl.BlockSpec((TM, D), lambda i: (i, 0)), _sds((dm["TT"], D), BF16), sem=("parallel",))(ctx, x, mods, norm_g)


def _chunk_of(d, s):
    dm = _dims()
    ncc, nch = dm["NCC"], dm["NCH"]
    return s if d == 0 else jnp.where(s < ncc, ncc - 1 - s, nch - 1 - (s - ncc))


HG_GROUP = 8
RW_GROUP = 8


def _dir_specs(make):
    return [make(d) for d in range(2)]


def _lanes(j):
    return slice(j * 128, (j + 1) * 128)


def _hg_scan_fwd(p_hg, lb3):
    dm = _dims()
    nh, nch, tt = dm["HG_H"], dm["NCH"], dm["TT"]
    pp = min(HG_GROUP, nh)
    lw_ = 128 * pp

    def body(q0, q1, i0, i1, f0, f1, lb0, lb1, o0, o1, ck_ref, s_ref):
        @pl.when(pl.program_id(1) == 0)
        def _():
            s_ref[...] = jnp.zeros_like(s_ref)

        keys, chains = [], []
        for d, refs in enumerate(((q0, i0, f0, lb0), (q1, i1, f1, lb1))):
            vals = [r[...] for r in refs]
            for j in range(pp):
                keys.append((d, j))
                chains.append((s_ref[d, j], *[a[:, _lanes(j)] for a in vals], d))
        for (d, j), chain, (out, s1) in zip(keys, chains, _hg_chunks(chains)):
            ck_ref[d, j] = chain[0]
            (o0, o1)[d][:, _lanes(j)] = out
            s_ref[d, j] = s1

    blk = lambda off: _dir_specs(lambda d: pl.BlockSpec((C, lw_), lambda h, s: (_chunk_of(d, s), off * nh // pp + h)))
    return _call(body, "hg_scan_fwd", (nh // pp, nch),
                 blk(0) + blk(1) + _dir_specs(lambda d: pl.BlockSpec((C, lw_), lambda h, s: (_chunk_of(d, s), (2 + d) * nh // pp + h)))
                 + _dir_specs(lambda d: pl.BlockSpec((None, 1, lw_), lambda h, s: (d, 0, h))),
                 _dir_specs(lambda d: pl.BlockSpec((C, lw_), lambda h, s: (_chunk_of(d, s), h)))
                 + [pl.BlockSpec((2, pp, None, 128, 128), lambda h, s: (0, h, s, 0, 0))],
                 [_sds((tt, HGW)), _sds((tt, HGW)), _sds((2, nh, nch, 128, 128))],
                 scratch=[pltpu.VMEM((2, pp, 128, 128), F32)], sem=("parallel", "arbitrary"))(*([p_hg] * 6), lb3, lb3)


def _hg_scan_bwd(p_hg, lb3, ck, do):
    dm = _dims()
    nh, nch, tt = dm["HG_H"], dm["NCH"], dm["TT"]
    pp = min(HG_GROUP, nh)
    lw_ = 128 * pp

    def body(q0, q1, i0, i1, f0, f1, lb0, lb1, ck_ref, do0, do1, dq0, dq1, di0, di1, df0, df1, dlb_ref, ds_ref):
        @pl.when(pl.program_id(1) == 0)
        def _():
            ds_ref[...] = jnp.zeros_like(ds_ref)
            dlb_ref[...] = jnp.zeros_like(dlb_ref)

        keys, flat, cts = [], [], []
        for d, refs in enumerate(((q0, i0, f0, lb0, do0), (q1, i1, f1, lb1, do1))):
            *vals, do_ = [r[...] for r in refs]
            for j in range(pp):
                keys.append((d, j))
                flat += [ck_ref[d, j], *[a[:, _lanes(j)] for a in vals]]
                cts.append((do_[:, _lanes(j)], ds_ref[d, j]))
        _, vjp = jax.vjp(_lockstep(_hg_chunks, [d for d, _ in keys], 5), *flat)
        grads = vjp(tuple(cts))
        for n, (d, j) in enumerate(keys):
            ds0, dq_, di_, df_, dlb = grads[5 * n:5 * n + 5]
            (dq0, dq1)[d][:, _lanes(j)] = dq_
            (di0, di1)[d][:, _lanes(j)] = di_
            (df0, df1)[d][:, _lanes(j)] = df_
            dlb_ref[d, :, _lanes(j)] += dlb
            ds_ref[d, j] = ds0

    ch = lambda d, s: _chunk_of(d, nch - 1 - s)
    blk = lambda off: _dir_specs(lambda d: pl.BlockSpec((C, lw_), lambda h, s: (ch(d, s), off * nh // pp + h)))
    tok = _dir_specs(lambda d: pl.BlockSpec((C, lw_), lambda h, s: (ch(d, s), h)))
    return _call(body, "hg_scan_bwd", (nh // pp, nch),
                 blk(0) + blk(1) + _dir_specs(lambda d: pl.BlockSpec((C, lw_), lambda h, s: (ch(d, s), (2 + d) * nh // pp + h)))
                 + _dir_specs(lambda d: pl.BlockSpec((None, 1, lw_), lambda h, s: (d, 0, h)))
                 + [pl.BlockSpec((2, pp, None, 128, 128), lambda h, s: (0, h, nch - 1 - s, 0, 0))] + tok,
                 tok * 3 + [pl.BlockSpec((2, 1, lw_), lambda h, s: (0, 0, h))],
                 [_sds((tt, HGW))] * 6 + [_sds((2, 1, HGW))],
                 scratch=[pltpu.VMEM((2, pp, 128, 128), F32)], sem=("parallel", "arbitrary"))(*([p_hg] * 6), lb3, lb3, ck, do, do)


def _rw_scan_fwd(sh, kap, lw, b, kd):
    dm = _dims()
    npair, nch, tt = dm["RW_P"], dm["NCH"], dm["TT"]
    pp = min(RW_GROUP, npair)
    lw_ = 128 * pp

    def body(r0, r1, v0, v1, ka0, ka1, lw0, lw1, b0, b1, kd0, kd1, y0, y1, ck_ref, tm_ref, s_ref):
        @pl.when(pl.program_id(1) == 0)
        def _():
            s_ref[...] = jnp.zeros_like(s_ref)

        keys, chains = [], []
        for d, refs in enumerate(((r0, lw0, ka0, b0, v0, kd0), (r1, lw1, ka1, b1, v1, kd1))):
            vals = [r[...] for r in refs]
            for j in range(pp):
                keys.append((d, j))
                chains.append((s_ref[d, j], *[a[:, _lanes(j)] for a in vals], d))
        for (d, j), chain, (out, s1, tm_a, tm_b) in zip(keys, chains, _rw_chunks(chains)):
            ck_ref[d, j] = chain[0]
            tm_ref[d, j, 0] = tm_a
            tm_ref[d, j, 1] = tm_b
            (y0, y1)[d][:, _lanes(j)] = out
            s_ref[d, j] = s1

    two = lambda off: _dir_specs(lambda d: pl.BlockSpec((C, lw_), lambda p, s: (_chunk_of(d, s), off * npair // pp + p)))
    three = _dir_specs(lambda d: pl.BlockSpec((None, C, lw_), lambda p, s: (d, _chunk_of(d, s), p)))
    return _call(body, "rw_scan_fwd", (npair // pp, nch), two(0) + two(2) + two(0) + three * 3,
                 two(0) + [pl.BlockSpec((2, pp, None, 128, 128), lambda p, s: (0, p, s, 0, 0)),
                           pl.BlockSpec((2, pp, None, 2, C, C), lambda p, s: (0, p, s, 0, 0, 0))],
                 [_sds((tt, RWW)), _sds((tt, RWW)), _sds((2, npair, nch, 128, 128)), _sds((2, npair, nch, 2, C, C))],
                 scratch=[pltpu.VMEM((2, pp, 128, 128), F32)], sem=("parallel", "arbitrary"))(
        sh, sh, sh, sh, kap, kap, lw, lw, b, b, kd, kd)


def _rw_scan_bwd(sh, kap, lw, b, kd, ck, tm_ck, dy):
    dm = _dims()
    npair, nch, tt = dm["RW_P"], dm["NCH"], dm["TT"]
    pp = min(RW_GROUP, npair)
    lw_ = 128 * pp

    def body(r0, r1, v0, v1, ka0, ka1, lw0, lw1, b0, b1, kd0, kd1, ck_ref, tm_ref, dy0, dy1,
             dr0, dr1, dv0, dv1, dka0, dka1, dlw0, dlw1, db0, db1, dkd0, dkd1, ds_ref):
        @pl.when(pl.program_id(1) == 0)
        def _():
            ds_ref[...] = jnp.zeros_like(ds_ref)

        keys, flat, cts, tms = [], [], [], {}
        for d, refs in enumerate(((r0, lw0, ka0, b0, v0, kd0, dy0), (r1, lw1, ka1, b1, v1, kd1, dy1))):
            *vals, dy_ = [r[...] for r in refs]
            for j in range(pp):
                tms[len(keys), 0], tms[len(keys), 1] = tm_ref[d, j, 0], tm_ref[d, j, 1]
                keys.append((d, j))
                flat += [ck_ref[d, j], *[a[:, _lanes(j)] for a in vals]]
                cts.append((dy_[:, _lanes(j)], ds_ref[d, j]))
        _, vjp = jax.vjp(_lockstep(functools.partial(_rw_chunks, tms=tms), [d for d, _ in keys], 7), *flat)
        grads = vjp(tuple(cts))
        outs = ((dr0, dlw0, dka0, db0, dv0, dkd0), (dr1, dlw1, dka1, db1, dv1, dkd1))
        for n, (d, j) in enumerate(keys):
            ds0, *g_in = grads[7 * n:7 * n + 7]
            for o_ref, g in zip(outs[d], g_in):
                o_ref[:, _lanes(j)] = g
            ds_ref[d, j] = ds0

    ch = lambda d, s: _chunk_of(d, nch - 1 - s)
    two = lambda off: _dir_specs(lambda d: pl.BlockSpec((C, lw_), lambda p, s: (ch(d, s), off * npair // pp + p)))
    three = _dir_specs(lambda d: pl.BlockSpec((None, C, lw_), lambda p, s: (d, ch(d, s), p)))
    return _call(body, "rw_scan_bwd", (npair // pp, nch),
                 two(0) + two(2) + two(0) + three * 3
                 + [pl.BlockSpec((2, pp, None, 128, 128), lambda p, s: (0, p, nch - 1 - s, 0, 0)),
                    pl.BlockSpec((2, pp, None, 2, C, C), lambda p, s: (0, p, nch - 1 - s, 0, 0, 0))] + two(0),
                 two(0) * 6, [_sds((tt, RWW))] * 12,
                 scratch=[pltpu.VMEM((2, pp, 128, 128), F32)], sem=("parallel", "arbitrary"))(
        sh, sh, sh, sh, kap, kap, lw, lw, b, b, kd, kd, ck, tm_ck, dy, dy)


def _tile_pos(i):
    dm = _dims()
    is_ctx = i < dm["NCT"]
    rows = lax.broadcasted_iota(jnp.int32, (TM, 1), 0)
    tl = rows + (i - dm["NCT"]) * TM
    width = jnp.where(is_ctx, TC, GW)
    assert TC & (TC - 1) == 0 and GW & (GW - 1) == 0 and TM % GW == 0
    colp = rows & (width - 1)
    return is_ctx, rows, tl, colp, width


def _shift_lr(cur, i):
    _, _, _, colp, width = _tile_pos(i)
    left = jnp.where(colp == 0, 0.0, pltpu.roll(cur, 1, 0))
    right = jnp.where(colp == width - 1, 0.0, pltpu.roll(cur, TM - 1, 0))
    return left, right


def _shift_ud(cur, prv, nxt, i):
    is_ctx, rows, tl, _, _ = _tile_pos(i)
    if TM > GW:
        up = jnp.where(rows >= GW, pltpu.roll(cur, GW, 0), pltpu.roll(prv, GW, 0))
        down = jnp.where(rows < TM - GW, pltpu.roll(cur, TM - GW, 0), pltpu.roll(nxt, TM - GW, 0))
    else:
        up, down = prv, nxt
    up = jnp.where(jnp.logical_or(is_ctx, tl < GW), 0.0, up)
    down = jnp.where(jnp.logical_or(is_ctx, tl >= T - GW), 0.0, down)
    return up, down


def _shift_specs():
    dm = _dims()
    nt = dm["TT"] // TM
    cw = max(w for w in range(128, 1664 + 1, 128) if dm["SH"] % w == 0)
    cur = pl.BlockSpec((TM, cw), lambda j, i: (i, j))
    prv = pl.BlockSpec((TM, cw), lambda j, i: (jnp.maximum(i - 1, 0), j))
    nxt = pl.BlockSpec((TM, cw), lambda j, i: (jnp.minimum(i + 1, nt - 1), j))
    mu = pl.BlockSpec((4, cw), lambda j, i: (0, j))
    return nt, cw, cur, prv, nxt, mu


def _shift_fwd(p_rw, mu):
    dm = _dims()
    nt, cw, cur, prv, nxt, mus = _shift_specs()

    def body(c_ref, p_ref, n_ref, mu_ref, o_ref):
        i = pl.program_id(1)
        p, m = c_ref[...], mu_ref[...]
        left, right = _shift_lr(p, i)
        up, down = _shift_ud(p, p_ref[...], n_ref[...], i)
        vert = jnp.where(i < dm["NCT"], 0.0, 1.0)
        o_ref[...] = (p * (1.0 - m[0:1] - m[1:2] - vert * (m[2:3] + m[3:4]))
                      + m[0:1] * left + m[1:2] * right + m[2:3] * up + m[3:4] * down)

    return _call(body, "shift_fwd", (dm["SH"] // cw, nt), [cur, prv, nxt, mus], cur, _sds((dm["TT"], dm["SH"])),
                 sem=("parallel", "parallel"))(p_rw, p_rw, p_rw, mu)


def _shift_bwd(p_rw, dsh, mu):
    dm = _dims()
    nt, cw, cur, prv, nxt, mus = _shift_specs()

    def body(c_ref, p_ref, n_ref, gc_ref, gp_ref, gn_ref, mu_ref, dp_ref, dmu_ref):
        i = pl.program_id(1)
        p, g, m = c_ref[...], gc_ref[...], mu_ref[...]
        vert = jnp.where(i < dm["NCT"], 0.0, 1.0)
        _, from_right = _shift_lr(m[0:1] * g, i)
        from_left, _ = _shift_lr(m[1:2] * g, i)
        _, from_down = _shift_ud(m[2:3] * g, g, m[2:3] * gn_ref[...], i)
        from_up, _ = _shift_ud(m[3:4] * g, m[3:4] * gp_ref[...], g, i)
        dp = g * (1.0 - m[0:1] - m[1:2] - vert * (m[2:3] + m[3:4])) + from_right + from_left + from_down + from_up
        dp_ref[...] = dp.astype(BF16)

        @pl.when(i == 0)
        def _():
            dmu_ref[...] = jnp.zeros_like(dmu_ref)

        left, right = _shift_lr(p, i)
        up, down = _shift_ud(p, p_ref[...], n_ref[...], i)
        s = lambda a: jnp.sum(a, axis=0, keepdims=True)
        dmu_ref[...] += jnp.concatenate([s(g * (left - p)), s(g * (right - p)), vert * s(g * (up - p)), vert * s(g * (down - p))], axis=0)

    return _call(body, "shift_bwd", (dm["SH"] // cw, nt), [cur, prv, nxt, cur, prv, nxt, mus],
                 [cur, pl.BlockSpec((4, cw), lambda j, i: (0, j))], [_sds((dm["TT"], dm["SH"]), BF16), _sds((4, dm["SH"]))],
                 sem=("parallel", "arbitrary"))(p_rw, p_rw, p_rw, dsh, dsh, dsh, mu)


def _acc_out(ref, first, vals):
    @pl.when(first)
    def _():
        for r in ref:
            r[...] = jnp.zeros_like(r)

    for r, v in zip(ref, vals):
        r[...] += v


def _rowsum(a):
    return jnp.sum(a, axis=0, keepdims=True)


def _rw_prep_specs():
    dm = _dims()
    rw = RWW // 128
    tok = lambda width, cb: pl.BlockSpec((TMV, width), lambda i: (i, cb))
    ins = [pl.BlockSpec((TMV, RWW), lambda i: (i, 1)), tok(128, 3 * rw), tok(128, 3 * rw + 1),
           _full((2, RWW)), _full((128, RWW)), _full((128, RWW)), _full((2, RWW)), _full((128, RWW)), _full((128, RWW)),
           _full((1, RWW)), _full((1, RWW))]
    return dm, ins


def _rw_prep_fwd(sh, w0, w2p0, w2p1, a0, a2p0, a2p1, kkp, kap_p):
    dm, ins = _rw_prep_specs()
    tt = dm["TT"]

    def body(k_ref, low_ref, loa_ref, w0_ref, w20_ref, w21_ref, a0_ref, a20_ref, a21_ref, kk_ref, ka_ref, kap_ref, lw_ref, b_ref, kd_ref):
        tw, la = jnp.tanh(low_ref[...]), loa_ref[...]
        kap, lw0, b0, kd0, lw1, b1, kd1 = _rw_prep_math(
            k_ref[...], _mm1(tw, w20_ref[...]), _mm1(tw, w21_ref[...]), _mm1(la, a20_ref[...]), _mm1(la, a21_ref[...]),
            w0_ref[...], a0_ref[...], kk_ref[...], ka_ref[...])
        kap_ref[...] = kap
        lw_ref[0], lw_ref[1] = lw0, lw1
        b_ref[0], b_ref[1] = b0, b1
        kd_ref[0], kd_ref[1] = kd0, kd1

    two = pl.BlockSpec((2, TMV, RWW), lambda i: (0, i, 0))
    return _call(body, "rw_prep_fwd", (tt // TMV,), ins, [pl.BlockSpec((TMV, RWW), lambda i: (i, 0)), two, two, two],
                 [_sds((tt, RWW)), _sds((2, tt, RWW)), _sds((2, tt, RWW)), _sds((2, tt, RWW))], sem=("parallel",))(
        sh, sh, sh, w0, w2p0, w2p1, a0, a2p0, a2p1, kkp, kap_p)


def _rw_prep_bwd(sh, w0, w2p0, w2p1, a0, a2p0, a2p1, kkp, kap_p, scan_grads, dkd_p, dr_p, dv_p):
    dm, ins = _rw_prep_specs()
    tt, sh_w = dm["TT"], dm["SH"]
    one = pl.BlockSpec((TMV, RWW), lambda i: (i, 0))

    def body(k_ref, low_ref, loa_ref, w0_ref, w20_ref, w21_ref, a0_ref, a20_ref, a21_ref, kk_ref, ka_ref,
             dr0, dr1, dv0, dv1, dka0, dka1, dlw0, dlw1, db0, db1, dkd0, dkd1, dkdp_ref, drp_ref, dvp_ref,
             dsh_ref, dw0_ref, dw20_ref, dw21_ref, da0_ref, da20_ref, da21_ref, dkk_ref, dka_ref):
        tw, la = jnp.tanh(low_ref[...]), loa_ref[...]
        w20, w21, a20, a21 = w20_ref[...], w21_ref[...], a20_ref[...], a21_ref[...]
        _, vjp = jax.vjp(_rw_prep_math, k_ref[...], _mm1(tw, w20), _mm1(tw, w21), _mm1(la, a20), _mm1(la, a21),
                         w0_ref[...], a0_ref[...], kk_ref[...], ka_ref[...])
        dkp = dkdp_ref[...]
        dk, dpw0, dpw1, dpa0, dpa1, dw0, da0, dkk, dka = vjp(
            (dka0[...] + dka1[...], dlw0[...], db0[...], dkd0[...] + dkp, dlw1[...], db1[...], dkd1[...] + dkp))
        twt, lat = tw.T, la.T
        dtw = _mm1_nt(dpw0, w20) + _mm1_nt(dpw1, w21)
        dsh_ref[:, 0:RWW] = dr0[...] + dr1[...] + drp_ref[...]
        dsh_ref[:, RWW:2 * RWW] = dk
        dsh_ref[:, 2 * RWW:3 * RWW] = dv0[...] + dv1[...] + dvp_ref[...]
        dsh_ref[:, 3 * RWW:3 * RWW + 128] = dtw * (1.0 - tw * tw)
        dsh_ref[:, 3 * RWW + 128:3 * RWW + 256] = _mm1_nt(dpa0, a20) + _mm1_nt(dpa1, a21)
        _acc_out((dw0_ref, dw20_ref, dw21_ref, da0_ref, da20_ref, da21_ref, dkk_ref, dka_ref), pl.program_id(0) == 0,
                 (dw0, _mm1(twt, dpw0), _mm1(twt, dpw1), da0, _mm1(lat, dpa0), _mm1(lat, dpa1), dkk, dka))

    par = [_full((2, RWW)), _full((128, RWW)), _full((128, RWW)), _full((2, RWW)), _full((128, RWW)), _full((128, RWW)),
           _full((1, RWW)), _full((1, RWW))]
    return _call(body, "rw_prep_bwd", (tt // TMV,), ins + [one] * 15,
                 [pl.BlockSpec((TMV, sh_w), lambda i: (i, 0))] + par,
                 [_sds((tt, sh_w)), _sds((2, RWW)), _sds((128, RWW)), _sds((128, RWW)), _sds((2, RWW)), _sds((128, RWW)), _sds((128, RWW)),
                  _sds((1, RWW)), _sds((1, RWW))],
                 sem=("arbitrary",))(sh, sh, sh, w0, w2p0, w2p1, a0, a2p0, a2p1, kkp, kap_p, *scan_grads, dkd_p, dr_p, dv_p)


def _rw_post_specs():
    dm = _dims()
    nv = dm["NCTV"]
    rw = RWW // 128
    lat3 = lambda d: pl.BlockSpec((None, TMV, RWW), lambda i: (d, nv + i, 0))
    lat2 = pl.BlockSpec((TMV, RWW), lambda i: (nv + i, 0))
    ins = [lat2, lat2, lat3(0), lat3(1),
           pl.BlockSpec((TMV, RWW), lambda i: (nv + i, 0)), pl.BlockSpec((TMV, RWW), lambda i: (nv + i, 2)),
           pl.BlockSpec((TMV, RWW), lambda i: (i, 0)), _full((1, RWW)), _full((1, RWW)), _full((1, RWW))]
    return dm, nv, ins


def _rw_post_fwd(y0_, y1_, kd, sh, p_z, rk, gng, gnb):
    dm, nv, ins = _rw_post_specs()

    def body(y0, y1, k0, k1, r, v, z, rk_ref, g_ref, b_ref, o_ref):
        o_ref[...] = _rw_post_math(y0[...], y1[...], k0[...], k1[...], r[...], v[...], z[...], rk_ref[...], g_ref[...], b_ref[...]).astype(BF16)

    return _call(body, "rw_post_fwd", (T // TMV,), ins, pl.BlockSpec((TMV, RWW), lambda i: (i, 0)), _sds((T, RWW), BF16),
                 sem=("parallel",))(y0_, y1_, kd, kd, sh, sh, p_z, rk, gng, gnb)


def _rw_post_bwd(y0_, y1_, kd, sh, p_z, rk, gng, gnb, dout):
    dm, nv, _ = _rw_post_specs()
    tt = dm["TT"]
    cl = lambda i: jnp.maximum(i - nv, 0)
    all3 = lambda d: pl.BlockSpec((None, TMV, RWW), lambda i: (d, i, 0))
    all2 = pl.BlockSpec((TMV, RWW), lambda i: (i, 0))
    ins = [all2, all2, all3(0), all3(1),
           pl.BlockSpec((TMV, RWW), lambda i: (i, 0)), pl.BlockSpec((TMV, RWW), lambda i: (i, 2)),
           pl.BlockSpec((TMV, RWW), lambda i: (cl(i), 0)), _full((1, RWW)), _full((1, RWW)), _full((1, RWW)),
           pl.BlockSpec((TMV, RWW), lambda i: (cl(i), 0))]
    tok = pl.BlockSpec((TMV, RWW), lambda i: (i, 0))

    def body(y0, y1, k0, k1, r, v, z, rk_ref, g_ref, b_ref, do_ref, dy_ref, dkd_ref, dr_ref, dv_ref, dz_ref, drk_ref, dg_ref, db_ref):
        i = pl.program_id(0)
        lat = jnp.where(i >= nv, 1.0, 0.0)
        _, vjp = jax.vjp(_rw_post_math, y0[...], y1[...], k0[...], k1[...], r[...], v[...], z[...], rk_ref[...], g_ref[...], b_ref[...])
        dy0, _, dk0, _, dr, dv, dz, drk, dg, db = vjp(do_ref[...] * lat)
        dy_ref[...] = dy0
        dkd_ref[...] = dk0
        dr_ref[...] = dr
        dv_ref[...] = dv
        dz_ref[...] = dz.astype(BF16)
        _acc_out((drk_ref, dg_ref, db_ref), i == 0, (drk, dg, db))

    return _call(body, "rw_post_bwd", (tt // TMV,), ins,
                 [tok, tok, tok, tok, pl.BlockSpec((TMV, RWW), lambda i: (cl(i), 0)), _full((1, RWW)), _full((1, RWW)), _full((1, RWW))],
                 [_sds((tt, RWW))] * 4 + [_sds((T, RWW), BF16), _sds((1, RWW)), _sds((1, RWW)), _sds((1, RWW))],
                 sem=("arbitrary",))(y0_, y1_, kd, kd, sh, sh, p_z, rk, gng, gnb, dout)


def _hg_post_fwd(o0_, o1_, p_hg, hg_g):
    dm = _dims()
    nv = dm["NCTV"]
    lat3 = lambda d: pl.BlockSpec((TMV, HGW), lambda i: (nv + i, 0))

    def body(o0, o1, z, g, out):
        out[...] = _hg_post_math(o0[...], o1[...], z[...], g[...]).astype(BF16)

    return _call(body, "hg_post_fwd", (T // TMV,), [lat3(0), lat3(1), pl.BlockSpec((TMV, HGW), lambda i: (nv + i, 4)), _full((1, HGW))],
                 pl.BlockSpec((TMV, HGW), lambda i: (i, 0)), _sds((T, HGW), BF16), sem=("parallel",))(o0_, o1_, p_hg, hg_g)


def _hg_post_bwd(o0_, o1_, p_hg, hg_g, dout):
    dm = _dims()
    nv, tt = dm["NCTV"], dm["TT"]
    all3 = lambda d: pl.BlockSpec((TMV, HGW), lambda i: (i, 0))
    tok = pl.BlockSpec((TMV, HGW), lambda i: (i, 0))

    def body(o0, o1, z, g, do_in, do_ref, dz_ref, dg_ref):
        i = pl.program_id(0)
        lat = jnp.where(i >= nv, 1.0, 0.0)
        _, vjp = jax.vjp(_hg_post_math, o0[...], o1[...], z[...], g[...])
        d0, _, dz, dg = vjp(do_in[...] * lat)
        do_ref[...] = d0
        dz_ref[...] = dz
        _acc_out((dg_ref,), i == 0, (dg,))

    return _call(body, "hg_post_bwd", (tt // TMV,),
                 [all3(0), all3(1), pl.BlockSpec((TMV, HGW), lambda i: (i, 4)), _full((1, HGW)),
                  pl.BlockSpec((TMV, HGW), lambda i: (jnp.maximum(i - nv, 0), 0))],
                 [tok, tok, _full((1, HGW))], [_sds((tt, HGW)), _sds((tt, HGW)), _sds((1, HGW))], sem=("arbitrary",))(o0_, o1_, p_hg, hg_g, dout)


def _hg_dproj(dq0, dq1, di0, di1, df0, df1, dz):
    dm = _dims()
    tt = dm["TT"]
    tok = pl.BlockSpec((TMV, HGW), lambda i: (i, 0))

    def body(q0, q1, i0, i1, f0, f1, z, o_ref):
        o_ref[:, 0:HGW] = (q0[...] + q1[...]).astype(BF16)
        o_ref[:, HGW:2 * HGW] = (i0[...] + i1[...]).astype(BF16)
        o_ref[:, 2 * HGW:3 * HGW] = f0[...].astype(BF16)
        o_ref[:, 3 * HGW:4 * HGW] = f1[...].astype(BF16)
        o_ref[:, 4 * HGW:5 * HGW] = z[...].astype(BF16)

    return _call(body, "hg_dproj", (tt // TMV,), [tok] * 7,
                 pl.BlockSpec((TMV, 5 * HGW), lambda i: (i, 0)), _sds((tt, 5 * HGW), BF16), sem=("parallel",))(
        dq0, dq1, di0, di1, df0, df1, dz)


def _merge1_fwd(out_hg, out_rw, whg_t, wrw_t, p_g):
    tok = pl.BlockSpec((TM, D), lambda i: (i, 0))

    def body(h_ref, r_ref, wh_ref, wr_ref, g1_ref, g2_ref, a_ref, b_ref, m_ref):
        nt = (((1,), (1,)), ((), ()))
        a = lax.dot_general(h_ref[...], wh_ref[...], nt, preferred_element_type=F32)
        b = lax.dot_general(r_ref[...], wr_ref[...], nt, preferred_element_type=F32)
        a_ref[...] = a
        b_ref[...] = b
        m_ref[...] = _gate_math(g1_ref[...], g2_ref[...], a, b).astype(BF16)

    return _call(body, "merge1_fwd", (T // TM,),
                 [pl.BlockSpec((TM, HGW), lambda i: (i, 0)), pl.BlockSpec((TM, RWW), lambda i: (i, 0)), _full((D, HGW)), _full((D, RWW)),
                  pl.BlockSpec((TM, D), lambda i: (i, 0)), pl.BlockSpec((TM, D), lambda i: (i, 1))],
                 [tok, tok, tok], [_sds((T, D)), _sds((T, D)), _sds((T, D), BF16)], sem=("parallel",))(out_hg, out_rw, whg_t, wrw_t, p_g, p_g)


def _head_fwd_bwd(x, merged, w_out, mods, final_g, tgt):
    tok = pl.BlockSpec((TM, D), lambda i: (i, 0))

    def body(x_ref, m_ref, w_ref, mods_ref, fg_ref, t_ref, dxo_ref, dmo_ref, loss_ref, dgate_ref, dfg_ref):
        mo = jnp.dot(m_ref[...], w_ref[...], preferred_element_type=F32)
        gate = mods_ref[0:1, 2 * D:3 * D]
        loss, vjp = jax.vjp(lambda x_, mo_, g_, fg_: _head_math(x_, mo_, g_, fg_, t_ref[...]), x_ref[...], mo, gate, fg_ref[...])
        dx, dmo, dgate, dfg = vjp(jnp.ones((1, 1), F32))
        dxo_ref[...] = dx
        dmo_ref[...] = dmo.astype(BF16)
        _acc_out((loss_ref, dgate_ref, dfg_ref), pl.program_id(0) == 0, (jnp.broadcast_to(loss, (1, 128)), dgate, dfg))

    return _call(body, "head_fwd_bwd", (T // TM,), [tok, tok, _full((D, D)), _full((8, 3 * D)), _full((1, D)), tok],
                 [tok, tok, _full((1, 128)), _full((1, D)), _full((1, D))],
                 [_sds((T, D)), _sds((T, D), BF16), _sds((1, 128)), _sds((1, D)), _sds((1, D))], sem=("arbitrary",))(
        x, merged, w_out, mods, final_g, tgt)


def _merge_bwd(dmo, w_out, p_g, a, b):
    tok = pl.BlockSpec((TM, D), lambda i: (i, 0))

    def body(dmo_ref, w_ref, g1_ref, g2_ref, a_ref, b_ref, da_ref, db_ref, dg_ref):
        dm_ = lax.dot_general(dmo_ref[...], w_ref[...], (((1,), (1,)), ((), ())), preferred_element_type=F32)
        _, vjp = jax.vjp(_gate_math, g1_ref[...], g2_ref[...], a_ref[...], b_ref[...])
        dg1, dg2, da, db = vjp(dm_)
        da_ref[...] = da.astype(BF16)
        db_ref[...] = db.astype(BF16)
        dg_ref[:, 0:D] = dg1.astype(BF16)
        dg_ref[:, D:2 * D] = dg2.astype(BF16)

    return _call(body, "merge_bwd", (T // TM,),
                 [tok, _full((D, D)), pl.BlockSpec((TM, D), lambda i: (i, 0)), pl.BlockSpec((TM, D), lambda i: (i, 1)), tok, tok],
                 [tok, tok, pl.BlockSpec((TM, 2 * D), lambda i: (i, 0))], [_sds((T, D), BF16), _sds((T, D), BF16), _sds((T, 2 * D), BF16)],
                 sem=("parallel",))(dmo, w_out, p_g, p_g, a, b)


def _h_bwd(ctx, x, mods, norm_g, dh_hg, dh_rw, dh_z, dh_g, dxo, dgate):
    dm = _dims()
    nct, tt = dm["NCT"], dm["TT"]
    tok = pl.BlockSpec((TM, D), lambda i: (i, 0))
    lat = pl.BlockSpec((TM, D), lambda i: (jnp.maximum(i - nct, 0), 0))

    def body(c_ref, x_ref, m_ref, g_ref, d1, d2, d3, d4, dxo_ref, dgate_ref, gx_ref, dng_ref, dmods_ref):
        i = pl.program_id(0)
        is_ctx = i < nct
        latf = jnp.where(is_ctx, 0.0, 1.0)
        dh = d1[...] + d2[...] + latf * (d3[...] + d4[...])
        _, vjp = jax.vjp(_h_math, jnp.where(is_ctx, c_ref[...], x_ref[...]), g_ref[...], _sel_mod(m_ref, is_ctx, D),
                         _sel_mod(m_ref, is_ctx, 0))
        dx, dng, dscale, dshift = vjp(dh)
        gx_ref[...] = dx + dxo_ref[...]

        @pl.when(i == 0)
        def _():
            dng_ref[...] = jnp.zeros_like(dng_ref)
            dmods_ref[...] = jnp.zeros_like(dmods_ref)
            dmods_ref[0:1, 2 * D:3 * D] = dgate_ref[...]

        dng_ref[...] += dng
        row = lax.broadcasted_iota(jnp.int32, (8, 1), 0)
        sel = jnp.where(row == jnp.where(is_ctx, 1, 0), 1.0, 0.0)
        dmods_ref[:, 0:D] += sel * dshift
        dmods_ref[:, D:2 * D] += sel * dscale

    return _call(body, "h_bwd", (tt // TM,),
                 [*_token_specs(), _full((8, 3 * D)), _full((1, D)), tok, tok, lat, lat, lat, _full((1, D))],
                 [lat, _full((1, D)), _full((8, 3 * D))], [_sds((T, D)), _sds((1, D)), _sds((8, 3 * D))], sem=("arbitrary",))(
        ctx, x, mods, norm_g, dh_hg, dh_rw, dh_z, dh_g, dxo, dgate)


def _lb_math(l0, l1):
    return jax.nn.sigmoid(l0 - l1)


def _lb_fwd(l0, l1):
    def body(a_ref, b_ref, o_ref):
        o_ref[...] = _lb_math(a_ref[...], b_ref[...])

    return _call(body, "lb_fwd", (1,), [_full((2, HGW))] * 2, _full((2, HGW)), _sds((2, HGW)))(l0, l1)


def _lb_bwd(l0, l1, dlb):
    def body(a_ref, b_ref, d_ref, da_ref, db_ref):
        _, vjp = jax.vjp(_lb_math, a_ref[...], b_ref[...])
        da_ref[...], db_ref[...] = vjp(d_ref[...])

    return _call(body, "lb_bwd", (1,), [_full((2, HGW))] * 3, [_full((2, HGW))] * 2, [_sds((2, HGW))] * 2)(l0, l1, dlb)


def _local_step(x, ctx, tgt, mods, w, start_exchange):
    dm = _dims()
    tt, sh_w, hgc = dm["TT"], dm["SH"], dm["HGC"]
    nb = lambda cols: cols // WB
    h = _h_fwd(ctx, x, mods, w["norm_g"])
    win_t = w["win_t"]
    p_hg = _proj_nt(h, win_t, 0, nb(hgc), 0, tt, "proj_hg")
    p_rw = _proj_nt(h, win_t, nb(hgc), nb(sh_w), 0, tt, "proj_rw")
    p_z = _proj_nt(h, win_t, nb(hgc + sh_w), nb(RWW), TC, T, "proj_z")
    p_g = _proj_nt(h, win_t, nb(hgc + sh_w + RWW), nb(2 * D), TC, T, "proj_g")

    lb3 = _lb_fwd(w["lb0"], w["lb1"]).reshape(2, 1, HGW)
    o0, o1, hg_ck = _hg_scan_fwd(p_hg, lb3)
    out_hg = _hg_post_fwd(o0, o1, p_hg, w["hg_g"])

    sh = _shift_fwd(p_rw, w["mu"])
    zpad = jnp.zeros((LORA, RWW), F32)
    prep_w = (w["w0"], jnp.concatenate([w["w2"][0], zpad], 0), jnp.concatenate([zpad, w["w2"][1]], 0),
              w["a0"], jnp.concatenate([w["a2"][0], zpad], 0), jnp.concatenate([zpad, w["a2"][1]], 0), w["kk"], w["ka"])
    kap, lw, b, kd = _rw_prep_fwd(sh, *prep_w)
    y0, y1, rw_ck, rw_tm = _rw_scan_fwd(sh, kap, lw, b, kd)
    post_w = (w["rk"], w["gng"], w["gnb"])
    out_rw = _rw_post_fwd(y0, y1, kd, sh, p_z, *post_w)

    late = w["late_weights"](out_rw)
    a, bb, merged = _merge1_fwd(out_hg, out_rw, late["whg_t"], late["wrw_t"], p_g)
    dxo, dmo, loss, dgate, d_fg = _head_fwd_bwd(x, merged, late["wout"], mods, w["final_g"], tgt)
    da, db, dp_g = _merge_bwd(dmo, late["wout"], p_g, a, bb)
    g_wout = _dw_tn(merged, dmo, 0, D, 0, "dw_out")
    g_whg = _dw_tn(da, out_hg, 0, D, 0, "dw_hg")
    g_wrw = _dw_tn(db, out_rw, 0, D, 0, "dw_rw")
    token, early = start_exchange(dict(wout=g_wout, whg_t=g_whg, wrw_t=g_wrw), "early")
    d_out_hg = _mm_nn(da, late["whg_t"], 0, "dx_hg", after=token)
    d_out_rw = _mm_nn(db, late["wrw_t"], 0, "dx_rw")

    do, dz_hg, d_hg_g = _hg_post_bwd(o0, o1, p_hg, w["hg_g"], d_out_hg)
    *hg_grads, dlb3 = _hg_scan_bwd(p_hg, lb3, hg_ck, do)
    dp_hg = _hg_dproj(*hg_grads, dz_hg)
    d_lb0, d_lb1 = _lb_bwd(w["lb0"], w["lb1"], dlb3.reshape(2, HGW))

    dy, dkd_p, dr_p, dv_p, dp_z, d_rk, d_gng, d_gnb = _rw_post_bwd(y0, y1, kd, sh, p_z, *post_w, d_out_rw)
    rw_grads = _rw_scan_bwd(sh, kap, lw, b, kd, rw_ck, rw_tm, dy)
    dsh, d_w0, d_w2p0, d_w2p1, d_a0, d_a2p0, d_a2p1, d_kk, d_ka = _rw_prep_bwd(sh, *prep_w, rw_grads, dkd_p, dr_p, dv_p)
    dp_rw, d_mu = _shift_bwd(p_rw, dsh, w["mu"])

    g_win = _dw_tn(dp_hg, h, 0, dm["NCOLS"], 0, "dw_in_hg")
    g_win = _dw_tn(dp_rw, h, 0, dm["NCOLS"], nb(hgc), "dw_in_rw", prev=g_win)
    g_win = _dw_tn(dp_z, h, TC, dm["NCOLS"], nb(hgc + sh_w), "dw_in_z", prev=g_win)
    g_win = _dw_tn(dp_g, h, TC, dm["NCOLS"], nb(hgc + sh_w + RWW), "dw_in_g", prev=g_win)
    token, late = start_exchange(dict(win_t=g_win), "late")
    in_flight = (early, late)
    dh_hg = _mm_nn(dp_hg, win_t, 0, "dh_hg", after=token)
    dh_rw = _mm_nn(dp_rw, win_t, nb(hgc), "dh_rw")
    dh_z = _mm_nn(dp_z, win_t, nb(hgc + sh_w), "dh_z")
    dh_g = _mm_nn(dp_g, win_t, nb(hgc + sh_w + RWW), "dh_g")
    gx, d_ng, dmods = _h_bwd(ctx, x, mods, w["norm_g"], dh_hg, dh_rw, dh_z, dh_g, dxo, dgate)

    small = dict(norm_g=d_ng, lb0=d_lb0, lb1=d_lb1, hg_g=d_hg_g, mu=d_mu, w0=d_w0,
                 w2=jnp.stack([d_w2p0[:LORA], d_w2p1[LORA:]]), a0=d_a0, a2=jnp.stack([d_a2p0[:LORA], d_a2p1[LORA:]]),
                 kk=d_kk, ka=d_ka, rk=d_rk, gng=d_gng, gnb=d_gnb, final_g=d_fg)
    return loss, gx, dmods, in_flight, small


MESH = pl.DeviceIdType.MESH


def _comm_call(body, name, bufs, out_shapes, nsem, nloc):
    hbm = pl.BlockSpec(memory_space=pl.ANY)
    return pl.pallas_call(
        body, name=name, in_specs=[hbm] * len(bufs), out_specs=[hbm] * len(out_shapes), out_shape=out_shapes,
        scratch_shapes=[pltpu.SemaphoreType.DMA((nsem,)), pltpu.SemaphoreType.DMA((nsem,)), pltpu.SemaphoreType.DMA((nloc,))],
    )(*bufs)


GATHER_PIECE_BYTES = 4 * 1024 * 1024


def _gather2(bufs, name):
    pieces = []
    for bi, b in enumerate(bufs):
        rows = b.shape[0]
        if rows % 32 == 0 and b.size * b.dtype.itemsize > GATHER_PIECE_BYTES:
            pieces += [(bi, 0, rows // 2), (bi, rows // 2, rows // 2)]
        else:
            pieces.append((bi, 0, rows))
    nbuf = len(bufs)

    def body(*refs):
        ins, outs = refs[:nbuf], refs[nbuf:2 * nbuf]
        send_sems, recv_sems, _ = refs[2 * nbuf:]
        x, y, c = lax.axis_index("x"), lax.axis_index("y"), lax.axis_index("c")
        me, sib = (x, y, c), (x, y, 1 - c)
        flip = lambda a, b: a + b - 2 * a * b
        xn, yn, diag = (1 - x, y), (x, 1 - y), (1 - x, 1 - y)
        relay_from, relay_to = (flip(x, c), flip(1 - y, c)), (flip(1 - x, c), flip(y, c))

        def copy(pi, k, block, to, own=False):
            bi, r0, nr = pieces[pi]
            rows = outs[bi].at[4 * block[0] + 2 * block[1] + block[2], pl.ds(r0, nr)]
            return pltpu.make_async_remote_copy(src_ref=ins[bi].at[pl.ds(r0, nr)] if own else rows, dst_ref=rows,
                                                send_sem=send_sems.at[7 * pi + k], recv_sem=recv_sems.at[7 * pi + k],
                                                device_id=to, device_id_type=MESH)

        started = []
        for pi in range(len(pieces)):
            started += [copy(pi, 0, me, sib, own=True), copy(pi, 1, me, (*xn, c), own=True), copy(pi, 2, me, (*yn, c), own=True)]
        for cp in started:
            cp.start()
        for pi in range(len(pieces)):
            for k, chip in ((1, xn), (2, yn)):
                copy(pi, k, (*chip, c), me).wait_recv()
                started.append(copy(pi, 3 + k, (*chip, c), sib))
                started[-1].start()
            started.append(copy(pi, 3, (*relay_from, c), (*relay_to, c)))
            started[-1].start()
        for pi in range(len(pieces)):
            copy(pi, 3, (*diag, c), me).wait_recv()
            started.append(copy(pi, 6, (*diag, c), sib))
            started[-1].start()
        for pi in range(len(pieces)):
            copy(pi, 0, sib, me).wait_recv()
            for k, chip in ((4, xn), (5, yn), (6, diag)):
                copy(pi, k, (*chip, 1 - c), me).wait_recv()
        for cp in started:
            cp.wait_send()

    return _comm_call(body, name, bufs, [_sds((NDEV,) + b.shape, b.dtype) for b in bufs], 7 * len(pieces), 1)


def _gather_all(bufs, name):
    me = 4 * lax.axis_index("x") + 2 * lax.axis_index("y") + lax.axis_index("c")
    return [lax.dynamic_update_index_in_dim(g, b, me, axis=0) for g, b in zip(_gather2(bufs, name), bufs)]


def _peer_copies(srcs, lands, send_sems, recv_sems):
    x, y, c = lax.axis_index("x"), lax.axis_index("y"), lax.axis_index("c")
    me = 4 * x + 2 * y + c
    pairs = []
    for bi in range(len(srcs)):
        for k in range(1, NDEV):
            px, py, pc = (1 - x if k & 4 else x), (1 - y if k & 2 else y), (1 - c if k & 1 else c)
            mk = lambda dst: pltpu.make_async_remote_copy(
                src_ref=srcs[bi], dst_ref=dst, send_sem=send_sems.at[7 * bi + k - 1], recv_sem=recv_sems.at[7 * bi + k - 1],
                device_id=(px, py, pc), device_id_type=MESH)
            pairs.append((mk(lands[bi].at[me]), mk(lands[bi].at[4 * px + 2 * py + pc])))
    return pairs


def _direct_gather_start(bufs, dep, name):
    nbuf = len(bufs)
    hbm, sem = pl.BlockSpec(memory_space=pltpu.HBM), pl.BlockSpec(memory_space=pltpu.SEMAPHORE)

    def body(*refs):
        srcs, lands = refs[:nbuf], refs[nbuf:2 * nbuf]
        for send, _ in _peer_copies(srcs, lands, refs[2 * nbuf + 1], refs[2 * nbuf + 2]):
            send.start()

    held = [pltpu.HBM(b.shape, b.dtype) for b in bufs]
    zones = [pltpu.HBM((NDEV,) + b.shape, b.dtype) for b in bufs]
    outs = pl.pallas_call(
        body, name=name, in_specs=[hbm] * (2 * nbuf) + [pl.BlockSpec(memory_space=pl.ANY)], out_specs=(sem, sem, *[hbm] * (2 * nbuf)),
        out_shape=(pltpu.SemaphoreType.DMA((7 * nbuf,)), pltpu.SemaphoreType.DMA((7 * nbuf,)), *held, *zones),
        input_output_aliases={i: 2 + i for i in range(2 * nbuf)},
        compiler_params=pltpu.CompilerParams(has_side_effects=pltpu.SideEffectType.DATAFLOW_SIDE_EFFECTING),
    )(*[pltpu.with_memory_space_constraint(b, pltpu.HBM) for b in bufs],
      *[pltpu.with_memory_space_constraint(lax.empty((NDEV,) + b.shape, b.dtype), pltpu.HBM) for b in bufs], dep)
    return outs[0], outs[1], list(outs[2:2 + nbuf]), list(outs[2 + nbuf:])


def _direct_gather_wait(send_sems, recv_sems, srcs, lands, after, name):
    nbuf = len(srcs)
    hbm, sem = pl.BlockSpec(memory_space=pltpu.HBM), pl.BlockSpec(memory_space=pltpu.SEMAPHORE)

    def body(*refs):
        for send, recv in _peer_copies(refs[:nbuf], refs[nbuf:2 * nbuf], refs[2 * nbuf], refs[2 * nbuf + 1]):
            send.wait_send()
            recv.wait_recv()

    outs = pl.pallas_call(
        body, name=name, in_specs=[hbm] * (2 * nbuf) + [sem, sem, pl.BlockSpec(memory_space=pl.ANY)], out_specs=[hbm] * (2 * nbuf),
        out_shape=(*[pltpu.HBM(b.shape, b.dtype) for b in srcs], *[pltpu.HBM(b.shape, b.dtype) for b in lands]),
        input_output_aliases={i: i for i in range(2 * nbuf)},
        compiler_params=pltpu.CompilerParams(has_side_effects=pltpu.SideEffectType.DATAFLOW_SIDE_EFFECTING),
    )(*srcs, *lands, send_sems, recv_sems, after)
    me = 4 * lax.axis_index("x") + 2 * lax.axis_index("y") + lax.axis_index("c")
    return [lax.dynamic_update_index_in_dim(g, b, me, axis=0) for g, b in zip(outs[nbuf:], outs[:nbuf])]


def _pair_exchange(bufs, name):
    nbuf = len(bufs)

    def body(*refs):
        ins, got = refs[:nbuf], refs[nbuf:2 * nbuf]
        send_sems, recv_sems, _ = refs[2 * nbuf:]
        x, y, c = lax.axis_index("x"), lax.axis_index("y"), lax.axis_index("c")
        copies = []
        for bi in range(nbuf):
            for q in range(4):
                copies.append(pltpu.make_async_remote_copy(
                    src_ref=ins[bi].at[2 * q + 1 - c], dst_ref=got[bi].at[q], send_sem=send_sems.at[4 * bi + q],
                    recv_sem=recv_sems.at[4 * bi + q], device_id=(x, y, 1 - c), device_id_type=MESH))
                copies[-1].start()
        for cp in copies:
            cp.wait_send()
            cp.wait_recv()

    return _comm_call(body, name, bufs, [_sds((4,) + b.shape[1:], b.dtype) for b in bufs], 4 * nbuf, 1)


def _chip_copies(srcs, lands, send_sems, recv_sems):
    x, y, c = lax.axis_index("x"), lax.axis_index("y"), lax.axis_index("c")
    myq = 2 * x + y
    pairs = []
    for bi in range(len(srcs)):
        for j, (qx, qy) in enumerate([(1 - x, y), (x, 1 - y), (1 - x, 1 - y)]):
            q = 2 * qx + qy
            mk = lambda src, dst: pltpu.make_async_remote_copy(
                src_ref=src, dst_ref=dst, send_sem=send_sems.at[3 * bi + j], recv_sem=recv_sems.at[3 * bi + j],
                device_id=(qx, qy, c), device_id_type=MESH)
            pairs.append((mk(srcs[bi].at[q], lands[bi].at[myq]), mk(srcs[bi].at[myq], lands[bi].at[q])))
    return pairs


def _chip_exchange_start(bufs, name):
    nbuf = len(bufs)
    hbm, sem = pl.BlockSpec(memory_space=pltpu.HBM), pl.BlockSpec(memory_space=pltpu.SEMAPHORE)

    def body(*refs):
        srcs, lands = refs[:nbuf], refs[nbuf:2 * nbuf]
        send_sems, recv_sems = refs[2 * nbuf], refs[2 * nbuf + 1]
        token = refs[-1]
        for send, _ in _chip_copies(srcs, lands, send_sems, recv_sems):
            send.start()
        token[...] = jnp.zeros_like(token)

    held = [pltpu.HBM(b.shape, b.dtype) for b in bufs]
    outs = pl.pallas_call(
        body, name=name, in_specs=[hbm] * (2 * nbuf), out_specs=(sem, sem, *[hbm] * (2 * nbuf), pl.BlockSpec(memory_space=pltpu.VMEM)),
        out_shape=(pltpu.SemaphoreType.DMA((3 * nbuf,)), pltpu.SemaphoreType.DMA((3 * nbuf,)), *held, *held, _sds((8, 128))),
        input_output_aliases={i: 2 + i for i in range(2 * nbuf)},
        compiler_params=pltpu.CompilerParams(has_side_effects=pltpu.SideEffectType.DATAFLOW_SIDE_EFFECTING),
    )(*[pltpu.with_memory_space_constraint(b, pltpu.HBM) for b in bufs],
      *[pltpu.with_memory_space_constraint(lax.empty(b.shape, b.dtype), pltpu.HBM) for b in bufs])
    return outs[0], outs[1], list(outs[2:2 + nbuf]), list(outs[2 + nbuf:2 + 2 * nbuf]), outs[-1]


def _chip_exchange_wait(send_sems, recv_sems, srcs, lands, after, name):
    nbuf = len(srcs)
    hbm, sem = pl.BlockSpec(memory_space=pltpu.HBM), pl.BlockSpec(memory_space=pltpu.SEMAPHORE)

    def body(*refs):
        src_refs, land_refs = refs[:nbuf], refs[nbuf:2 * nbuf]
        for send, recv in _chip_copies(src_refs, land_refs, refs[2 * nbuf], refs[2 * nbuf + 1]):
            send.wait_send()
            recv.wait_recv()

    held = [pltpu.HBM(b.shape, b.dtype) for b in srcs]
    outs = pl.pallas_call(
        body, name=name, in_specs=[hbm] * (2 * nbuf) + [sem, sem, pl.BlockSpec(memory_space=pl.ANY)], out_specs=[hbm] * (2 * nbuf),
        out_shape=(*held, *held), input_output_aliases={i: i for i in range(2 * nbuf)},
        compiler_params=pltpu.CompilerParams(has_side_effects=pltpu.SideEffectType.DATAFLOW_SIDE_EFFECTING),
    )(*srcs, *lands, send_sems, recv_sems, after)
    return list(outs[:nbuf]), list(outs[nbuf:])


def _prefetch_call(body, name, scalars, grid, in_specs, out_specs, out_shape, args):
    return pl.pallas_call(
        body, name=name, out_shape=out_shape,
        grid_spec=pltpu.PrefetchScalarGridSpec(num_scalar_prefetch=1, grid=grid, in_specs=in_specs, out_specs=out_specs),
        compiler_params=pltpu.CompilerParams(dimension_semantics=("parallel",) * len(grid), vmem_limit_bytes=V7X_VMEM_LIMIT))(scalars, *args)


def _pair_add(g, got, core, name):
    _, rows, cols = got.shape
    tr = _row_tile(rows, 16, 256)

    def body(c_ref, a_ref, b_ref, o_ref):
        o_ref[...] = (a_ref[...].astype(F32) + b_ref[...].astype(F32)).astype(o_ref.dtype)

    blk = pl.BlockSpec((4, tr, cols), lambda i, c_ref: (0, i, 0))
    mine = pl.BlockSpec((4, None, tr, cols), lambda i, c_ref: (0, c_ref[0], i, 0))
    return _prefetch_call(body, name, core, (rows // tr,), [mine, blk], blk, _sds(got.shape, got.dtype),
                          (g.reshape(4, 2, rows, cols), got))


def _reduce_chips(chip_sums, recv, slots, name):
    _, rows, cols = recv.shape
    tr = _row_tile(rows, 16, 128)

    def body(s_ref, own_ref, r1_ref, r2_ref, r3_ref, o_ref):
        o_ref[...] = ((own_ref[...].astype(F32) + r1_ref[...].astype(F32)) + r2_ref[...].astype(F32)) + r3_ref[...].astype(F32)

    pick = lambda k: pl.BlockSpec((None, tr, cols), lambda i, s_ref: (s_ref[k], i, 0))
    return _prefetch_call(body, name, slots, (rows // tr,), [pick(0), pick(1), pick(2), pick(3)],
                          pl.BlockSpec((tr, cols), lambda i, s_ref: (i, 0)), _sds((rows, cols)), (chip_sums, recv, recv, recv))


def _row_tile(rows, mult, cap):
    best = mult
    for t in range(mult, cap + 1, mult):
        if rows % t == 0:
            best = t
    assert rows % best == 0, (rows, mult)
    return best


def _reduce_sources(r, name):
    nsrc, rows, cols = r.shape
    tr = _row_tile(rows, 16 if r.dtype.itemsize == 2 else 8, 128)

    def body(r_ref, o_ref):
        acc = r_ref[0].astype(F32)
        for j in range(1, nsrc):
            acc = acc + r_ref[j].astype(F32)
        o_ref[...] = acc

    return _call(body, name, (rows // tr,), [pl.BlockSpec((nsrc, tr, cols), lambda i: (0, i, 0))],
                 pl.BlockSpec((tr, cols), lambda i: (i, 0)), _sds((rows, cols)), sem=("parallel",))(r)


def _adamw(w, g, m, v, name):
    rows, cols = w.shape
    tr = _row_tile(rows, 8, 128)
    c1 = 1.0 - ADAM_B1 ** ADAM_STEP
    c2 = 1.0 - ADAM_B2 ** ADAM_STEP

    def body(w_ref, g_ref, m_ref, v_ref, d_ref, mo_ref, vo_ref):
        g_ = g_ref[...]
        m_ = ADAM_B1 * m_ref[...] + (1.0 - ADAM_B1) * g_
        v_ = ADAM_B2 * v_ref[...] + (1.0 - ADAM_B2) * (g_ * g_)
        d_ref[...] = -ADAM_LR * ((m_ / c1) / (jnp.sqrt(v_ / c2) + ADAM_EPS) + ADAM_WD * w_ref[...])
        mo_ref[...] = m_
        vo_ref[...] = v_

    blk = pl.BlockSpec((tr, cols), lambda i: (i, 0))
    return _call(body, name, (rows // tr,), [blk] * 4, [blk] * 3, [_sds((rows, cols))] * 3, sem=("parallel",))(w, g, m, v)


SLAB_PART = 1024


def _pack(arrs):
    parts = []
    for a in arrs:
        flat = a.reshape(-1)
        pad = (-flat.shape[0]) % SLAB_PART
        if pad:
            flat = jnp.concatenate([flat, jnp.zeros((pad,), flat.dtype)])
        parts.append(flat.reshape(-1, 128))
    return jnp.concatenate(parts, axis=0)


def _unpack(slab, shapes):
    outs, row = [], 0
    for s in shapes:
        n = 1
        for d in s:
            n *= d
        rows = (n + SLAB_PART - 1) // SLAB_PART * (SLAB_PART // 128)
        outs.append(slab[row:row + rows].reshape(-1)[:n].reshape(s))
        row += rows
    return outs


def _unshard_last(g, lead):
    nl = len(lead)
    return jnp.transpose(g, tuple(range(1, nl + 1)) + (0, nl + 1)).reshape(tuple(lead) + (-1,))


def kernel(x, c, ctx, c_ctx, ada_w, ada_b, norm_g, w_in, hg_lb, hg_norm_g, rw_mu, rw_w0, rw_w2, rw_a0, rw_a2, rw_kk, rw_ka, rw_rk, rw_gn_g, rw_gn_b, w_hg_out, w_rw_out, w_out, final_g, loss_target, m_c_ctx, m_ada_w, m_ada_b, m_norm_g, m_w_in, m_hg_lb, m_hg_norm_g, m_rw_mu, m_rw_w0, m_rw_w2, m_rw_a0, m_rw_a2, m_rw_kk, m_rw_ka, m_rw_rk, m_rw_gn_g, m_rw_gn_b, m_w_hg_out, m_w_rw_out, m_w_out, m_final_g, v_c_ctx, v_ada_w, v_ada_b, v_norm_g, v_w_in, v_hg_lb, v_hg_norm_g, v_rw_mu, v_rw_w0, v_rw_w2, v_rw_a0, v_rw_a2, v_rw_kk, v_rw_ka, v_rw_rk, v_rw_gn_g, v_rw_gn_b, v_w_hg_out, v_w_rw_out, v_w_out, v_final_g):
    dm = _dims()
    me = 4 * lax.axis_index("x") + 2 * lax.axis_index("y") + lax.axis_index("c")

    sharded_small = [hg_lb, rw_mu[0], rw_w0[0], rw_w2[0], rw_a0[0], rw_a2[0]]
    c_rows = jnp.concatenate([c, jnp.zeros((7, D), F32)], axis=0)
    gathered = _gather_all([w_in[0].T.astype(BF16), _pack(sharded_small), c_rows], "gather_weights")
    win_t = gathered[0].reshape(-1, D)
    per_dev = jax.vmap(lambda s: _unpack(s, [a.shape for a in sharded_small]))(gathered[1])
    hg_lb_f, mu_f, w0_f, w2_f, a0_f, a2_f = [_unshard_last(p, p.shape[1:-1]) for p in per_dev]

    ncol = ada_w.shape[2]
    cc16 = jnp.concatenate([gathered[2][:, 0], c_ctx[None], jnp.zeros((7, D), F32)], axis=0)
    mod_all = _gather_all([_ada_fwd(cc16, ada_w[0])], "gather_mods")[0]
    mod_x = lax.dynamic_index_in_dim(mod_all, me, axis=1, keepdims=False).reshape(1, -1) + ada_b
    mod_c = mod_all[:, NDEV].reshape(1, -1) + ada_b
    mods = jnp.concatenate([mod_x, mod_c, jnp.zeros((6, 3 * D), F32)], axis=0)

    late_flight = _direct_gather_start([w_out[0].astype(BF16), w_hg_out[0].T.astype(BF16), w_rw_out[0].T.astype(BF16)],
                                       mod_all, "gather_late_start")

    def late_weights(after):
        got = _direct_gather_wait(*late_flight, after, "gather_late_wait")
        return dict(zip(("wout", "whg_t", "wrw_t"), [g.reshape(-1, g.shape[2]) for g in got]))

    w = dict(win_t=win_t, late_weights=late_weights, norm_g=norm_g,
             lb0=hg_lb_f[:, 0], lb1=hg_lb_f[:, 1], hg_g=hg_norm_g, mu=mu_f, w0=w0_f, w2=w2_f, a0=a0_f, a2=a2_f,
             kk=rw_kk, ka=rw_ka, rk=rw_rk, gng=rw_gn_g, gnb=rw_gn_b, final_g=final_g[None])
    def start_exchange(grads_bf16, tag):
        names = list(grads_bf16)
        blocks = [grads_bf16[n].reshape(NDEV, -1, grads_bf16[n].shape[1]) for n in names]
        got = _pair_exchange(blocks, "pair_grads_" + tag)
        core = lax.axis_index("c").astype(jnp.int32).reshape(1)
        chip_sums = [_pair_add(g, r, core, "pair_add_" + n) for g, r, n in zip(blocks, got, names)]
        send_sems, recv_sems, held, lands, token = _chip_exchange_start(chip_sums, "chip_grads_start_" + tag)
        return token, (names, send_sems, recv_sems, held, lands)

    loss_dev, grad_x, dmods, in_flight, small = _local_step(x[0], ctx[0], loss_target[0], mods, w, start_exchange)
    loss = lax.psum(loss_dev[0, 0], AXES)

    my_chip = (2 * lax.axis_index("x") + lax.axis_index("y")).astype(jnp.int32)
    others = jnp.arange(3, dtype=jnp.int32)
    slots = jnp.concatenate([my_chip.reshape(1), others + (others >= my_chip).astype(jnp.int32)])

    def finish_exchange(tag, flight, after):
        names, *copies = flight
        chip_sums, recv = _chip_exchange_wait(*copies, after, "chip_grads_wait_" + tag)
        return {n: _reduce_chips(cs, r, slots, "reduce_" + n) for n, cs, r in zip(names, chip_sums, recv)}

    reduced = finish_exchange("early", in_flight[0], grad_x)

    dmods_all = _gather_all([dmods], "gather_dmods")[0]
    my_cols = lambda a: lax.dynamic_slice_in_dim(a, me * ncol, ncol, axis=1)
    g_ada, dcc16, d_ada_b = _ada_bwd(cc16, ada_w[0], my_cols(dmods_all[:, 0]), my_cols(dmods_all[:, 1]), dmods)
    small = dict(small, c_ctx=dcc16[NDEV:NDEV + 1], ada_b=d_ada_b)

    small_names = ["c_ctx", "ada_b", "norm_g", "lb0", "lb1", "hg_g", "mu", "w0", "w2", "a0", "a2", "kk", "ka", "rk", "gng", "gnb", "final_g"]
    small_slab = _pack([small[n] for n in small_names])
    small_all = _gather_all([small_slab], "gather_small_grads")[0]
    small_sum = _reduce_sources(small_all, "reduce_small")
    sg = dict(zip(small_names, _unpack(small_sum, [small[n].shape for n in small_names])))

    reduced.update(finish_exchange("late", in_flight[1], small_sum))
    g_win_t, g_wout, g_whg_t, g_wrw_t = [reduced[n] for n in ("win_t", "wout", "whg_t", "wrw_t")]

    def my_shard(full):
        n = full.shape[-1] // NDEV
        return lax.dynamic_slice_in_dim(full, me * n, n, axis=full.ndim - 1)

    grads = dict(
        c_ctx=sg["c_ctx"][0], ada_w=g_ada[None], ada_b=sg["ada_b"], norm_g=sg["norm_g"], w_in=g_win_t.T[None],
        hg_lb=my_shard(jnp.stack([sg["lb0"], sg["lb1"]], axis=1)), hg_norm_g=sg["hg_g"], rw_mu=my_shard(sg["mu"])[None],
        rw_w0=my_shard(sg["w0"])[None], rw_w2=my_shard(sg["w2"])[None], rw_a0=my_shard(sg["a0"])[None], rw_a2=my_shard(sg["a2"])[None],
        rw_kk=sg["kk"], rw_ka=sg["ka"], rw_rk=sg["rk"], rw_gn_g=sg["gng"], rw_gn_b=sg["gnb"],
        w_hg_out=g_whg_t.T[None], w_rw_out=g_wrw_t.T[None], w_out=g_wout[None], final_g=sg["final_g"][0])
    weights = dict(c_ctx=c_ctx, ada_w=ada_w, ada_b=ada_b, norm_g=norm_g, w_in=w_in, hg_lb=hg_lb, hg_norm_g=hg_norm_g, rw_mu=rw_mu,
                   rw_w0=rw_w0, rw_w2=rw_w2, rw_a0=rw_a0, rw_a2=rw_a2, rw_kk=rw_kk, rw_ka=rw_ka, rw_rk=rw_rk, rw_gn_g=rw_gn_g,
                   rw_gn_b=rw_gn_b, w_hg_out=w_hg_out, w_rw_out=w_rw_out, w_out=w_out, final_g=final_g)
    m_in = dict(zip(weights, (m_c_ctx, m_ada_w, m_ada_b, m_norm_g, m_w_in, m_hg_lb, m_hg_norm_g, m_rw_mu, m_rw_w0, m_rw_w2, m_rw_a0,
                              m_rw_a2, m_rw_kk, m_rw_ka, m_rw_rk, m_rw_gn_g, m_rw_gn_b, m_w_hg_out, m_w_rw_out, m_w_out, m_final_g)))
    v_in = dict(zip(weights, (v_c_ctx, v_ada_w, v_ada_b, v_norm_g, v_w_in, v_hg_lb, v_hg_norm_g, v_rw_mu, v_rw_w0, v_rw_w2, v_rw_a0,
                              v_rw_a2, v_rw_kk, v_rw_ka, v_rw_rk, v_rw_gn_g, v_rw_gn_b, v_w_hg_out, v_w_rw_out, v_w_out, v_final_g)))

    big_w = ["ada_w", "w_in", "w_hg_out", "w_rw_out", "w_out"]
    delta, new_m, new_v = {}, {}, {}
    for n in big_w:
        shp = weights[n].shape
        two = lambda a: a.reshape(shp[-2], shp[-1])
        d_, m_, v_ = _adamw(two(weights[n]), two(grads[n]), two(m_in[n]), two(v_in[n]), "adamw_" + n)
        delta[n], new_m[n], new_v[n] = d_.reshape(shp), m_.reshape(shp), v_.reshape(shp)
    rest = [n for n in weights if n not in big_w]
    shapes = [weights[n].shape for n in rest]
    d_s, m_s, v_s = _adamw(_pack([weights[n] for n in rest]), _pack([grads[n] for n in rest]), _pack([m_in[n] for n in rest]),
                           _pack([v_in[n] for n in rest]), "adamw_small")
    for n, d_, m_, v_ in zip(rest, _unpack(d_s, shapes), _unpack(m_s, shapes), _unpack(v_s, shapes)):
        delta[n], new_m[n], new_v[n] = d_, m_, v_

    order = list(weights)
    return (loss, grad_x[None], *[grads[n] for n in order], *[delta[n] for n in order],
            *[new_m[n] for n in order], *[new_v[n] for n in order])
```

```python
import functools

import jax
import jax.numpy as jnp
from jax import lax
from jax.experimental import pallas as pl
from jax.experimental.pallas import tpu as pltpu

D = 2048
T = 2048
TC = 256
GW = 64
C = 64
HGW = 1024
RWW = 1024
LORA = 64
NDEV = 8
TM = 256
TMV = 128
NORM_EPS = 1e-6
RW_GN_EPS = 64e-5
ADAM_LR, ADAM_B1, ADAM_B2, ADAM_EPS, ADAM_WD, ADAM_STEP = 0.001, 0.9, 0.999, 1e-08, 0.01, 10
WB = 256
V7X_VMEM_LIMIT = 56 * 1024 * 1024

F32 = jnp.float32
BF16 = jnp.bfloat16
AXES = ("x", "y", "c")


def _dims():
    tt = T + TC
    sh = 3 * RWW + 4 * LORA
    hgc = 5 * HGW
    return dict(TT=tt, NCC=TC // C, NCH=tt // C, HG_H=HGW // 128, RW_P=RWW // 128, SH=sh, HGC=hgc,
                NCOLS=hgc + sh + RWW + 2 * D, NCT=TC // TM, NCTV=TC // TMV)


def _mm(a, b):
    return _dot3(a, b, "nn")


def _mm_nt(a, b):
    return _dot3(a, b, "nt")


def _mm_tn(a, b):
    return _dot3(a, b, "tn")


def _split(x):
    hi = x.astype(BF16)
    return hi, (x - hi.astype(F32)).astype(BF16)


_FORMS = {"nn": (((1,), (0,)), ((), ())), "nt": (((1,), (1,)), ((), ())), "tn": (((0,), (0,)), ((), ()))}


def _dot3_raw(a, b, form):
    ah, al = _split(a)
    bh, bl = _split(b)
    d = lambda x, y: lax.dot_general(x, y, _FORMS[form], preferred_element_type=F32)
    return d(ah, bh) + (d(ah, bl) + d(al, bh))


@functools.partial(jax.custom_vjp, nondiff_argnums=(2,))
def _dot3(a, b, form):
    return _dot3_raw(a, b, form)


def _dot3_fwd(a, b, form):
    return _dot3_raw(a, b, form), (a, b)


def _dot3_bwd(form, res, g):
    a, b = res
    if form == "nn":
        return _dot3(g, b, "nt"), _dot3(a, g, "tn")
    if form == "nt":
        return _dot3(g, b, "nn"), _dot3(g, a, "tn")
    return _dot3(b, g, "nt"), _dot3(a, g, "nn")


_dot3.defvjp(_dot3_fwd, _dot3_bwd)


def _scan_cumsum(inc, x):
    return _cumsum_vjp(inc.astype(BF16), x)


def _cumsum_raw(inc, x, form):
    h1 = x.astype(BF16)
    r1 = x - h1.astype(F32)
    h2 = r1.astype(BF16)
    h3 = (r1 - h2.astype(F32)).astype(BF16)
    d = lambda y: lax.dot_general(inc, y, _FORMS[form], preferred_element_type=F32)
    return d(h1) + (d(h2) + d(h3))


@jax.custom_vjp
def _cumsum_vjp(inc, x):
    return _cumsum_raw(inc, x, "nn")


_cumsum_vjp.defvjp(lambda inc, x: (_cumsum_raw(inc, x, "nn"), inc),
                   lambda inc, g: (jnp.zeros_like(inc), _cumsum_raw(inc, g, "tn")))


def _silu(x):
    return x * jax.nn.sigmoid(x)


def _softplus(x):
    return jnp.maximum(x, 0.0) + jnp.log(1.0 + jnp.exp(-jnp.abs(x)))


def _iota2(shape, dim):
    return lax.broadcasted_iota(jnp.int32, shape, dim)


def _pair_mask():
    return (_iota2((128, 128), 0) < 64) == (_iota2((128, 128), 1) < 64)


def _seg64_raw(x):
    e = _pair_mask().astype(BF16)
    parts = []
    for g in range(x.shape[1] // 128):
        hi, lo = _split(x[:, g * 128:(g + 1) * 128])
        parts.append(jnp.dot(hi, e, preferred_element_type=F32) + jnp.dot(lo, e, preferred_element_type=F32))
    return parts[0] if len(parts) == 1 else jnp.concatenate(parts, axis=1)


@jax.custom_vjp
def _seg64_sum(x):
    return _seg64_raw(x)


_seg64_sum.defvjp(lambda x: (_seg64_raw(x), None), lambda _, g: (_seg64_raw(g),))


def _mm1(a, b):
    return jnp.dot(a.astype(BF16), b.astype(BF16), preferred_element_type=F32)


def _mm1_nt(a, b):
    return lax.dot_general(a.astype(BF16), b.astype(BF16), (((1,), (1,)), ((), ())), preferred_element_type=F32)


def _seg128_mean(x):
    parts = [jnp.broadcast_to(jnp.mean(x[:, g * 128:(g + 1) * 128], axis=1, keepdims=True), (x.shape[0], 128))
             for g in range(x.shape[1] // 128)]
    return parts[0] if len(parts) == 1 else jnp.concatenate(parts, axis=1)


def _scan_masks(d):
    lag = (_iota2((C, C), 0) - _iota2((C, C), 1)) * (1 - 2 * d)
    return lag >= 0, lag > 0


def _tri_inverse(a):
    eye = (_iota2((C, C), 0) == _iota2((C, C), 1)).astype(F32)
    p = [-x for x in a]
    tm = [eye + x for x in p]
    n = 2
    while n < C:
        p = [_mm(x, x) for x in p]
        tm = [t + _mm(t, x) for t, x in zip(tm, p)]
        n *= 2
    return tm


@jax.custom_vjp
def _tri_solve(tm, a, rhs):
    return [_mm(t, r) for t, r in zip(tm, rhs)]


def _tri_solve_fwd(tm, a, rhs):
    u = [_mm(t, r) for t, r in zip(tm, rhs)]
    return u, (tm, u)


def _tri_solve_bwd(res, g):
    tm, u = res
    d_rhs = [_mm_tn(t, x) for t, x in zip(tm, g)]
    return [jnp.zeros_like(t) for t in tm], [-_mm_nt(d, x) for d, x in zip(d_rhs, u)], d_rhs


_tri_solve.defvjp(_tri_solve_fwd, _tri_solve_bwd)


def _rw_chunks(chains, tms=None):
    s0, r, lw, kap, b, v, kd, ds = zip(*chains)
    ids = range(len(chains))
    inc, strict = zip(*[_scan_masks(d) for d in ds])
    lane = _iota2((1, 128), 1)
    hm = ((lane < 64).astype(F32), (lane >= 64).astype(F32))
    lc = [_scan_cumsum(inc[i], lw[i]) for i in ids]
    ltot = [jnp.sum(lw[i], axis=0, keepdims=True) for i in ids]
    rt = [r[i] * jnp.exp(lc[i]) for i in ids]
    kt = [kap[i] * jnp.exp(lc[i] - lw[i]) for i in ids]
    einv = [jnp.exp(-lc[i]) for i in ids]
    kh = [kd[i] * einv[i] for i in ids]
    bh = [b[i] * einv[i] for i in ids]
    eend = [jnp.exp(ltot[i] - lc[i]) for i in ids]
    kbar = [kd[i] * eend[i] for i in ids]
    bbar = [b[i] * eend[i] for i in ids]
    lhs = [jnp.concatenate([kt[i] * hm[0], kt[i] * hm[1], rt[i] * hm[0], rt[i] * hm[1]], axis=0) for i in ids]
    gk = [_mm_nt(lhs[i], kh[i]) for i in ids]
    gb = [_mm_nt(lhs[i], bh[i]) for i in ids]
    ks = [_mm_nt(kt[i], s0[i]) for i in ids]
    ys = [_mm_nt(rt[i], s0[i]) for i in ids]
    heads = [(i, h) for i in ids for h in range(2)]
    akk = {c: jnp.where(strict[c[0]], gk[c[0]][c[1] * C:(c[1] + 1) * C], 0.0) for c in heads}
    akb = {c: jnp.where(strict[c[0]], gb[c[0]][c[1] * C:(c[1] + 1) * C], 0.0) for c in heads}
    ark = {c: jnp.where(inc[c[0]], gk[c[0]][(2 + c[1]) * C:(3 + c[1]) * C], 0.0) for c in heads}
    arb = {c: jnp.where(inc[c[0]], gb[c[0]][(2 + c[1]) * C:(3 + c[1]) * C], 0.0) for c in heads}
    vh = {c: v[c[0]] * hm[c[1]] for c in heads}
    av = {c: _mm(akk[c], vh[c]) for c in heads}
    rhs = {c: ks[c[0]] * hm[c[1]] + av[c] for c in heads}
    akb_list = [akb[c] for c in heads]
    tm = _tri_inverse(akb_list) if tms is None else [tms[c] for c in heads]
    uh = dict(zip(heads, _tri_solve(tm, akb_list, [rhs[c] for c in heads])))
    yv = {c: _mm(ark[c], vh[c]) for c in heads}
    yu = {c: _mm(arb[c], uh[c]) for c in heads}
    u = [uh[i, 0] + uh[i, 1] for i in ids]
    y = [ys[i] + (yv[i, 0] - yu[i, 0]) + (yv[i, 1] - yu[i, 1]) for i in ids]
    upd = [_mm_tn(jnp.concatenate([v[i], -u[i]], axis=0), jnp.concatenate([kbar[i], bbar[i]], axis=0)) for i in ids]
    s1 = [s0[i] * jnp.exp(ltot[i]) + jnp.where(_pair_mask(), upd[i], 0.0) for i in ids]
    if tms is None:
        return [(y[i], s1[i], tm[2 * i], tm[2 * i + 1]) for i in ids]
    return [(y[i], s1[i]) for i in ids]


HG_SUB = 16


def _hg_chunks(chains):
    s0, qr, iv, f, lb, ds = zip(*chains)
    ids = range(len(chains))
    inc = [_scan_masks(d)[0] for d in ds]
    q = [_silu(qr[i]) for i in ids]
    fg = [lb[i] + (1.0 - lb[i]) * jax.nn.sigmoid(f[i]) for i in ids]
    k = [1.0 - fg[i] for i in ids]
    g = [jnp.log(fg[i]) for i in ids]
    bc = [_scan_cumsum(inc[i], g[i]) for i in ids]
    btot = [jnp.sum(g[i], axis=0, keepdims=True) for i in ids]
    o_inter = [_mm_nt(q[i] * jnp.exp(bc[i]), s0[i]) for i in ids]
    rowi = _iota2((C, 1), 0)
    outs = [[] for _ in ids]
    for blk in range(C // HG_SUB):
        lo, hi = blk * HG_SUB, (blk + 1) * HG_SUB
        first = [lo if ds[i] == 0 else hi - 1 for i in ids]
        ref = [jnp.sum(jnp.where(rowi == first[i], bc[i] - g[i], 0.0), axis=0, keepdims=True) for i in ids]
        qi = [q[i][lo:hi] * jnp.exp(bc[i][lo:hi] - ref[i]) for i in ids]
        src = [(rowi < hi) if ds[i] == 0 else (rowi >= lo) for i in ids]
        ke = [jnp.where(src[i], k[i] * jnp.exp(jnp.where(src[i], ref[i] - bc[i], 0.0)), 0.0) for i in ids]
        a = [jnp.where(inc[i][lo:hi], _mm_nt(qi[i], ke[i]), 0.0) for i in ids]
        part = [_mm(a[i], iv[i]) for i in ids]
        for i in ids:
            outs[i].append(part[i])
    o = [o_inter[i] + jnp.concatenate(outs[i], axis=0) for i in ids]
    upd = [_mm_tn(iv[i], k[i] * jnp.exp(btot[i] - bc[i])) for i in ids]
    s1 = [s0[i] * jnp.exp(btot[i]) + upd[i] for i in ids]
    return [(o[i], s1[i]) for i in ids]


def _lockstep(fn, ds, nargs):
    def flat_fn(*flat):
        return tuple(fn([tuple(flat[i * nargs:(i + 1) * nargs]) + (d,) for i, d in enumerate(ds)]))

    return flat_fn


def _h_math(x, ng, scale, shift):
    return x * lax.rsqrt(jnp.mean(x * x, axis=-1, keepdims=True) + NORM_EPS) * ng * (1.0 + scale) + shift


def _hg_post_math(o0, o1, z, g):
    o = o0 + o1
    on = o * lax.rsqrt(_seg128_mean(o * o) + NORM_EPS)
    return on * g * _silu(z)


def _rw_prep_math(k, pw0, pw1, pa0, pa1, w0, a0, kkp, kap_p):
    kk = k * kkp
    kap = kk * lax.rsqrt(_seg64_sum(kk * kk) + 1e-12)
    outs = []
    for d, (pw, pa) in enumerate(((pw0, pa0), (pw1, pa1))):
        w_log = -_softplus(-(w0[d:d + 1] + pw)) - 0.5
        lw = -jnp.exp(w_log)
        a = jax.nn.sigmoid(a0[d:d + 1] + pa)
        kd = k * (1.0 + (a - 1.0) * kap_p)
        outs += [lw, kap * a, kd]
    return (kap, *outs)


def _rw_post_math(y0, y1, kd0, kd1, r, v, z, rk, gng, gnb):
    ys, ksum = y0 + y1, kd0 + kd1
    mean = _seg64_sum(ys) * (1.0 / 64.0)
    cen = ys - mean
    var = _seg64_sum(cen * cen) * (1.0 / 64.0)
    yn = cen * lax.rsqrt(var + RW_GN_EPS) * gng + gnb
    bonus = _seg64_sum(r * ksum * rk) * v
    return (yn + bonus) * _silu(z)


def _head_math(x, mo, gate, fg, tgt):
    xo = x + gate * mo
    y = xo * lax.rsqrt(jnp.mean(xo * xo, axis=-1, keepdims=True) + NORM_EPS) * fg
    err = y - tgt
    return 0.5 * jnp.sum(jnp.mean(err * err, axis=-1, keepdims=True), axis=0, keepdims=True)


def _gate_math(ghg, grw, a, b):
    return jax.nn.sigmoid(ghg) * a + jax.nn.sigmoid(grw) * b


def _call(body, name, grid, in_specs, out_specs, out_shape, scratch=(), sem=None, aliases=None):
    return pl.pallas_call(
        body, name=name, grid=grid, in_specs=in_specs, out_specs=out_specs, out_shape=out_shape,
        scratch_shapes=list(scratch), input_output_aliases=aliases or {},
        compiler_params=pltpu.CompilerParams(dimension_semantics=sem, vmem_limit_bytes=V7X_VMEM_LIMIT))


def _full(shape):
    nd = len(shape)
    return pl.BlockSpec(shape, lambda *_: (0,) * nd)


def _sds(shape, dtype=F32):
    return jax.ShapeDtypeStruct(shape, dtype)


def _proj_nt(h, wt, wt_blk0, nblk, row0, nrows, name):
    def body(h_ref, w_ref, o_ref):
        o_ref[...] = lax.dot_general(h_ref[row0:row0 + nrows, :], w_ref[...], (((1,), (1,)), ((), ())), preferred_element_type=F32)

    return _call(body, name, (nblk,), [_full(h.shape), pl.BlockSpec((WB, wt.shape[1]), lambda j: (wt_blk0 + j, 0))],
                 pl.BlockSpec((nrows, WB), lambda j: (0, j)), _sds((nrows, nblk * WB)), sem=("parallel",))(h, wt)


def _dw_tn(a, b, row0, out_rows, out_blk0, name, prev=None):
    m, n = a.shape
    k2 = b.shape[1]
    nblk = n // WB

    def body(*refs):
        a_ref, b_ref, o_ref = refs[0], refs[1], refs[-1]
        o_ref[...] = lax.dot_general(a_ref[...], b_ref[row0:row0 + m, :], (((0,), (0,)), ((), ())),
                                     preferred_element_type=F32).astype(BF16)

    in_specs = [pl.BlockSpec((m, WB), lambda j: (0, j)), _full(b.shape)]
    args = [a, b]
    aliases = None
    if prev is not None:
        in_specs.append(pl.BlockSpec(memory_space=pl.ANY))
        args.append(prev)
        aliases = {2: 0}
    return _call(body, name, (nblk,), in_specs, pl.BlockSpec((WB, k2), lambda j: (out_blk0 + j, 0)), _sds((out_rows, k2), BF16),
                 sem=("arbitrary",), aliases=aliases)(*args)


def _mm_nn(a, w, w_blk0, name, after=None):
    m, kc = a.shape
    n = w.shape[1]
    kb = max(k for k in (WB, 2 * WB, 4 * WB) if kc % k == 0 and (w_blk0 * WB) % k == 0)
    w_blk0 = w_blk0 * WB // kb
    tmb = 3 * TM if m % (3 * TM) == 0 else (2 * TM if m % (2 * TM) == 0 else TM)

    def body(a_ref, w_ref, *rest):
        o_ref = rest[-1]

        @pl.when(pl.program_id(1) == 0)
        def _():
            o_ref[...] = jnp.zeros_like(o_ref)

        o_ref[...] += jnp.dot(a_ref[...], w_ref[...], preferred_element_type=F32)

    extra = [] if after is None else [pl.BlockSpec(memory_space=pl.ANY)]
    return _call(body, name, (m // tmb, kc // kb),
                 [pl.BlockSpec((tmb, kb), lambda i, k: (i, k)), pl.BlockSpec((kb, n), lambda i, k: (w_blk0 + k, 0))] + extra,
                 pl.BlockSpec((tmb, n), lambda i, k: (i, 0)), _sds((m, n)), sem=("parallel", "arbitrary"))(
        a, w, *([] if after is None else [after]))


def _ada_fwd(cc16, ada_shard):
    ncol = ada_shard.shape[1]

    def body(c_ref, w_ref, o_ref):
        o_ref[...] = _mm(_silu(c_ref[...]), w_ref[...])

    return _call(body, "ada_fwd", (1,), [_full((16, D)), _full((D, ncol))], _full((16, ncol)), _sds((16, ncol)))(cc16, ada_shard)


def _ada_bwd(cc16, ada_shard, dm_x, dm_c, dmods):
    ncol = ada_shard.shape[1]

    def body(c_ref, w_ref, dx_ref, dc_ref, dm_ref, dw_ref, dcc_ref, db_ref):
        s, vjp = jax.vjp(_silu, c_ref[...])
        dc_tot = dc_ref[0:1, :]
        for j in range(1, NDEV):
            dc_tot = dc_tot + dc_ref[j:j + 1, :]
        row = _iota2((8, 1), 0)
        dm16 = jnp.concatenate([dx_ref[...], jnp.where(row == 0, dc_tot, 0.0)], axis=0)
        dw_ref[...] = _mm_tn(s, dm16)
        dcc_ref[...] = vjp(_mm_nt(dm16, w_ref[...]))[0]
        db_ref[...] = dm_ref[0:1, :] + dm_ref[1:2, :]

    return _call(body, "ada_bwd", (1,), [_full((16, D)), _full((D, ncol)), _full((8, ncol)), _full((8, ncol)), _full((8, 3 * D))],
                 [_full((D, ncol)), _full((16, D)), _full((1, 3 * D))], [_sds((D, ncol)), _sds((16, D)), _sds((1, 3 * D))])(
        cc16, ada_shard, dm_x, dm_c, dmods)


def _sel_mod(mods_ref, is_ctx, lo):
    return jnp.where(is_ctx, mods_ref[1:2, lo:lo + D], mods_ref[0:1, lo:lo + D])


def _token_specs():
    nct = _dims()["NCT"]
    return (pl.BlockSpec((TM, D), lambda i: (jnp.minimum(i, nct - 1), 0)),
            pl.BlockSpec((TM, D), lambda i: (jnp.maximum(i - nct, 0), 0)))


def _h_fwd(ctx, x, mods, norm_g):
    dm = _dims()

    def body(c_ref, x_ref, m_ref, g_ref, o_ref):
        is_ctx = pl.program_id(0) < dm["NCT"]
        tok = jnp.where(is_ctx, c_ref[...], x_ref[...])
        o_ref[...] = _h_math(tok, g_ref[...], _sel_mod(m_ref, is_ctx, D), _sel_mod(m_ref, is_ctx, 0)).astype(BF16)

    return _call(body, "h_fwd", (dm["TT"] // TM,), [*_token_specs(), _full((8, 3 * D)), _full((1, D))],
                 pl.BlockSpec((TM, D), lambda i: (i, 0)), _sds((dm["TT"], D), BF16), sem=("parallel",))(ctx, x, mods, norm_g)


def _chunk_of(d, s):
    dm = _dims()
    ncc, nch = dm["NCC"], dm["NCH"]
    return s if d == 0 else jnp.where(s < ncc, ncc - 1 - s, nch - 1 - (s - ncc))


HG_GROUP = 8
RW_GROUP = 8


def _dir_specs(make):
    return [make(d) for d in range(2)]


def _lanes(j):
    return slice(j * 128, (j + 1) * 128)


def _hg_scan_fwd(p_hg, lb3):
    dm = _dims()
    nh, nch, tt = dm["HG_H"], dm["NCH"], dm["TT"]
    pp = min(HG_GROUP, nh)
    lw_ = 128 * pp

    def body(q0, q1, i0, i1, f0, f1, lb0, lb1, o0, o1, ck_ref, s_ref):
        @pl.when(pl.program_id(1) == 0)
        def _():
            s_ref[...] = jnp.zeros_like(s_ref)

        keys, chains = [], []
        for d, refs in enumerate(((q0, i0, f0, lb0), (q1, i1, f1, lb1))):
            vals = [r[...] for r in refs]
            for j in range(pp):
                keys.append((d, j))
                chains.append((s_ref[d, j], *[a[:, _lanes(j)] for a in vals], d))
        for (d, j), chain, (out, s1) in zip(keys, chains, _hg_chunks(chains)):
            ck_ref[d, j] = chain[0]
            (o0, o1)[d][:, _lanes(j)] = out
            s_ref[d, j] = s1

    blk = lambda off: _dir_specs(lambda d: pl.BlockSpec((C, lw_), lambda h, s: (_chunk_of(d, s), off * nh // pp + h)))
    return _call(body, "hg_scan_fwd", (nh // pp, nch),
                 blk(0) + blk(1) + _dir_specs(lambda d: pl.BlockSpec((C, lw_), lambda h, s: (_chunk_of(d, s), (2 + d) * nh // pp + h)))
                 + _dir_specs(lambda d: pl.BlockSpec((None, 1, lw_), lambda h, s: (d, 0, h))),
                 _dir_specs(lambda d: pl.BlockSpec((C, lw_), lambda h, s: (_chunk_of(d, s), h)))
                 + [pl.BlockSpec((2, pp, None, 128, 128), lambda h, s: (0, h, s, 0, 0))],
                 [_sds((tt, HGW)), _sds((tt, HGW)), _sds((2, nh, nch, 128, 128))],
                 scratch=[pltpu.VMEM((2, pp, 128, 128), F32)], sem=("parallel", "arbitrary"))(*([p_hg] * 6), lb3, lb3)


def _hg_scan_bwd(p_hg, lb3, ck, do):
    dm = _dims()
    nh, nch, tt = dm["HG_H"], dm["NCH"], dm["TT"]
    pp = min(HG_GROUP, nh)
    lw_ = 128 * pp

    def body(q0, q1, i0, i1, f0, f1, lb0, lb1, ck_ref, do0, do1, dq0, dq1, di0, di1, df0, df1, dlb_ref, ds_ref):
        @pl.when(pl.program_id(1) == 0)
        def _():
            ds_ref[...] = jnp.zeros_like(ds_ref)
            dlb_ref[...] = jnp.zeros_like(dlb_ref)

        keys, flat, cts = [], [], []
        for d, refs in enumerate(((q0, i0, f0, lb0, do0), (q1, i1, f1, lb1, do1))):
            *vals, do_ = [r[...] for r in refs]
            for j in range(pp):
                keys.append((d, j))
                flat += [ck_ref[d, j], *[a[:, _lanes(j)] for a in vals]]
                cts.append((do_[:, _lanes(j)], ds_ref[d, j]))
        _, vjp = jax.vjp(_lockstep(_hg_chunks, [d for d, _ in keys], 5), *flat)
        grads = vjp(tuple(cts))
        for n, (d, j) in enumerate(keys):
            ds0, dq_, di_, df_, dlb = grads[5 * n:5 * n + 5]
            (dq0, dq1)[d][:, _lanes(j)] = dq_
            (di0, di1)[d][:, _lanes(j)] = di_
            (df0, df1)[d][:, _lanes(j)] = df_
            dlb_ref[d, :, _lanes(j)] += dlb
            ds_ref[d, j] = ds0

    ch = lambda d, s: _chunk_of(d, nch - 1 - s)
    blk = lambda off: _dir_specs(lambda d: pl.BlockSpec((C, lw_), lambda h, s: (ch(d, s), off * nh // pp + h)))
    tok = _dir_specs(lambda d: pl.BlockSpec((C, lw_), lambda h, s: (ch(d, s), h)))
    return _call(body, "hg_scan_bwd", (nh // pp, nch),
                 blk(0) + blk(1) + _dir_specs(lambda d: pl.BlockSpec((C, lw_), lambda h, s: (ch(d, s), (2 + d) * nh // pp + h)))
                 + _dir_specs(lambda d: pl.BlockSpec((None, 1, lw_), lambda h, s: (d, 0, h)))
                 + [pl.BlockSpec((2, pp, None, 128, 128), lambda h, s: (0, h, nch - 1 - s, 0, 0))] + tok,
                 tok * 3 + [pl.BlockSpec((2, 1, lw_), lambda h, s: (0, 0, h))],
                 [_sds((tt, HGW))] * 6 + [_sds((2, 1, HGW))],
                 scratch=[pltpu.VMEM((2, pp, 128, 128), F32)], sem=("parallel", "arbitrary"))(*([p_hg] * 6), lb3, lb3, ck, do, do)


def _rw_scan_fwd(sh, kap, lw, b, kd):
    dm = _dims()
    npair, nch, tt = dm["RW_P"], dm["NCH"], dm["TT"]
    pp = min(RW_GROUP, npair)
    lw_ = 128 * pp

    def body(r0, r1, v0, v1, ka0, ka1, lw0, lw1, b0, b1, kd0, kd1, y0, y1, ck_ref, tm_ref, s_ref):
        @pl.when(pl.program_id(1) == 0)
        def _():
            s_ref[...] = jnp.zeros_like(s_ref)

        keys, chains = [], []
        for d, refs in enumerate(((r0, lw0, ka0, b0, v0, kd0), (r1, lw1, ka1, b1, v1, kd1))):
            vals = [r[...] for r in refs]
            for j in range(pp):
                keys.append((d, j))
                chains.append((s_ref[d, j], *[a[:, _lanes(j)] for a in vals], d))
        for (d, j), chain, (out, s1, tm_a, tm_b) in zip(keys, chains, _rw_chunks(chains)):
            ck_ref[d, j] = chain[0]
            tm_ref[d, j, 0] = tm_a
            tm_ref[d, j, 1] = tm_b
            (y0, y1)[d][:, _lanes(j)] = out
            s_ref[d, j] = s1

    two = lambda off: _dir_specs(lambda d: pl.BlockSpec((C, lw_), lambda p, s: (_chunk_of(d, s), off * npair // pp + p)))
    three = _dir_specs(lambda d: pl.BlockSpec((None, C, lw_), lambda p, s: (d, _chunk_of(d, s), p)))
    return _call(body, "rw_scan_fwd", (npair // pp, nch), two(0) + two(2) + two(0) + three * 3,
                 two(0) + [pl.BlockSpec((2, pp, None, 128, 128), lambda p, s: (0, p, s, 0, 0)),
                           pl.BlockSpec((2, pp, None, 2, C, C), lambda p, s: (0, p, s, 0, 0, 0))],
                 [_sds((tt, RWW)), _sds((tt, RWW)), _sds((2, npair, nch, 128, 128)), _sds((2, npair, nch, 2, C, C))],
                 scratch=[pltpu.VMEM((2, pp, 128, 128), F32)], sem=("parallel", "arbitrary"))(
        sh, sh, sh, sh, kap, kap, lw, lw, b, b, kd, kd)


def _rw_scan_bwd(sh, kap, lw, b, kd, ck, tm_ck, dy):
    dm = _dims()
    npair, nch, tt = dm["RW_P"], dm["NCH"], dm["TT"]
    pp = min(RW_GROUP, npair)
    lw_ = 128 * pp

    def body(r0, r1, v0, v1, ka0, ka1, lw0, lw1, b0, b1, kd0, kd1, ck_ref, tm_ref, dy0, dy1,
             dr0, dr1, dv0, dv1, dka0, dka1, dlw0, dlw1, db0, db1, dkd0, dkd1, ds_ref):
        @pl.when(pl.program_id(1) == 0)
        def _():
            ds_ref[...] = jnp.zeros_like(ds_ref)

        keys, flat, cts, tms = [], [], [], {}
        for d, refs in enumerate(((r0, lw0, ka0, b0, v0, kd0, dy0), (r1, lw1, ka1, b1, v1, kd1, dy1))):
            *vals, dy_ = [r[...] for r in refs]
            for j in range(pp):
                tms[len(keys), 0], tms[len(keys), 1] = tm_ref[d, j, 0], tm_ref[d, j, 1]
                keys.append((d, j))
                flat += [ck_ref[d, j], *[a[:, _lanes(j)] for a in vals]]
                cts.append((dy_[:, _lanes(j)], ds_ref[d, j]))
        _, vjp = jax.vjp(_lockstep(functools.partial(_rw_chunks, tms=tms), [d for d, _ in keys], 7), *flat)
        grads = vjp(tuple(cts))
        outs = ((dr0, dlw0, dka0, db0, dv0, dkd0), (dr1, dlw1, dka1, db1, dv1, dkd1))
        for n, (d, j) in enumerate(keys):
            ds0, *g_in = grads[7 * n:7 * n + 7]
            for o_ref, g in zip(outs[d], g_in):
                o_ref[:, _lanes(j)] = g
            ds_ref[d, j] = ds0

    ch = lambda d, s: _chunk_of(d, nch - 1 - s)
    two = lambda off: _dir_specs(lambda d: pl.BlockSpec((C, lw_), lambda p, s: (ch(d, s), off * npair // pp + p)))
    three = _dir_specs(lambda d: pl.BlockSpec((None, C, lw_), lambda p, s: (d, ch(d, s), p)))
    return _call(body, "rw_scan_bwd", (npair // pp, nch),
                 two(0) + two(2) + two(0) + three * 3
                 + [pl.BlockSpec((2, pp, None, 128, 128), lambda p, s: (0, p, nch - 1 - s, 0, 0)),
                    pl.BlockSpec((2, pp, None, 2, C, C), lambda p, s: (0, p, nch - 1 - s, 0, 0, 0))] + two(0),
                 two(0) * 6, [_sds((tt, RWW))] * 12,
                 scratch=[pltpu.VMEM((2, pp, 128, 128), F32)], sem=("parallel", "arbitrary"))(
        sh, sh, sh, sh, kap, kap, lw, lw, b, b, kd, kd, ck, tm_ck, dy, dy)


def _tile_pos(i):
    dm = _dims()
    is_ctx = i < dm["NCT"]
    rows = lax.broadcasted_iota(jnp.int32, (TM, 1), 0)
    tl = rows + (i - dm["NCT"]) * TM
    width = jnp.where(is_ctx, TC, GW)
    assert TC & (TC - 1) == 0 and GW & (GW - 1) == 0 and TM % GW == 0
    colp = rows & (width - 1)
    return is_ctx, rows, tl, colp, width


def _shift_lr(cur, i):
    _, _, _, colp, width = _tile_pos(i)
    left = jnp.where(colp == 0, 0.0, pltpu.roll(cur, 1, 0))
    right = jnp.where(colp == width - 1, 0.0, pltpu.roll(cur, TM - 1, 0))
    return left, right


def _shift_ud(cur, prv, nxt, i):
    is_ctx, rows, tl, _, _ = _tile_pos(i)
    if TM > GW:
        up = jnp.where(rows >= GW, pltpu.roll(cur, GW, 0), pltpu.roll(prv, GW, 0))
        down = jnp.where(rows < TM - GW, pltpu.roll(cur, TM - GW, 0), pltpu.roll(nxt, TM - GW, 0))
    else:
        up, down = prv, nxt
    up = jnp.where(jnp.logical_or(is_ctx, tl < GW), 0.0, up)
    down = jnp.where(jnp.logical_or(is_ctx, tl >= T - GW), 0.0, down)
    return up, down


def _shift_specs():
    dm = _dims()
    nt = dm["TT"] // TM
    cw = max(w for w in range(128, 1664 + 1, 128) if dm["SH"] % w == 0)
    cur = pl.BlockSpec((TM, cw), lambda j, i: (i, j))
    prv = pl.BlockSpec((TM, cw), lambda j, i: (jnp.maximum(i - 1, 0), j))
    nxt = pl.BlockSpec((TM, cw), lambda j, i: (jnp.minimum(i + 1, nt - 1), j))
    mu = pl.BlockSpec((4, cw), lambda j, i: (0, j))
    return nt, cw, cur, prv, nxt, mu


def _shift_fwd(p_rw, mu):
    dm = _dims()
    nt, cw, cur, prv, nxt, mus = _shift_specs()

    def body(c_ref, p_ref, n_ref, mu_ref, o_ref):
        i = pl.program_id(1)
        p, m = c_ref[...], mu_ref[...]
        left, right = _shift_lr(p, i)
        up, down = _shift_ud(p, p_ref[...], n_ref[...], i)
        vert = jnp.where(i < dm["NCT"], 0.0, 1.0)
        o_ref[...] = (p * (1.0 - m[0:1] - m[1:2] - vert * (m[2:3] + m[3:4]))
                      + m[0:1] * left + m[1:2] * right + m[2:3] * up + m[3:4] * down)

    return _call(body, "shift_fwd", (dm["SH"] // cw, nt), [cur, prv, nxt, mus], cur, _sds((dm["TT"], dm["SH"])),
                 sem=("parallel", "parallel"))(p_rw, p_rw, p_rw, mu)


def _shift_bwd(p_rw, dsh, mu):
    dm = _dims()
    nt, cw, cur, prv, nxt, mus = _shift_specs()

    def body(c_ref, p_ref, n_ref, gc_ref, gp_ref, gn_ref, mu_ref, dp_ref, dmu_ref):
        i = pl.program_id(1)
        p, g, m = c_ref[...], gc_ref[...], mu_ref[...]
        vert = jnp.where(i < dm["NCT"], 0.0, 1.0)
        _, from_right = _shift_lr(m[0:1] * g, i)
        from_left, _ = _shift_lr(m[1:2] * g, i)
        _, from_down = _shift_ud(m[2:3] * g, g, m[2:3] * gn_ref[...], i)
        from_up, _ = _shift_ud(m[3:4] * g, m[3:4] * gp_ref[...], g, i)
        dp = g * (1.0 - m[0:1] - m[1:2] - vert * (m[2:3] + m[3:4])) + from_right + from_left + from_down + from_up
        dp_ref[...] = dp.astype(BF16)

        @pl.when(i == 0)
        def _():
            dmu_ref[...] = jnp.zeros_like(dmu_ref)

        left, right = _shift_lr(p, i)
        up, down = _shift_ud(p, p_ref[...], n_ref[...], i)
        s = lambda a: jnp.sum(a, axis=0, keepdims=True)
        dmu_ref[...] += jnp.concatenate([s(g * (left - p)), s(g * (right - p)), vert * s(g * (up - p)), vert * s(g * (down - p))], axis=0)

    return _call(body, "shift_bwd", (dm["SH"] // cw, nt), [cur, prv, nxt, cur, prv, nxt, mus],
                 [cur, pl.BlockSpec((4, cw), lambda j, i: (0, j))], [_sds((dm["TT"], dm["SH"]), BF16), _sds((4, dm["SH"]))],
                 sem=("parallel", "arbitrary"))(p_rw, p_rw, p_rw, dsh, dsh, dsh, mu)


def _acc_out(ref, first, vals):
    @pl.when(first)
    def _():
        for r in ref:
            r[...] = jnp.zeros_like(r)

    for r, v in zip(ref, vals):
        r[...] += v


def _rowsum(a):
    return jnp.sum(a, axis=0, keepdims=True)


def _rw_prep_specs():
    dm = _dims()
    rw = RWW // 128
    tok = lambda width, cb: pl.BlockSpec((TMV, width), lambda i: (i, cb))
    ins = [pl.BlockSpec((TMV, RWW), lambda i: (i, 1)), tok(128, 3 * rw), tok(128, 3 * rw + 1),
           _full((2, RWW)), _full((128, RWW)), _full((128, RWW)), _full((2, RWW)), _full((128, RWW)), _full((128, RWW)),
           _full((1, RWW)), _full((1, RWW))]
    return dm, ins


def _rw_prep_fwd(sh, w0, w2p0, w2p1, a0, a2p0, a2p1, kkp, kap_p):
    dm, ins = _rw_prep_specs()
    tt = dm["TT"]

    def body(k_ref, low_ref, loa_ref, w0_ref, w20_ref, w21_ref, a0_ref, a20_ref, a21_ref, kk_ref, ka_ref, kap_ref, lw_ref, b_ref, kd_ref):
        tw, la = jnp.tanh(low_ref[...]), loa_ref[...]
        kap, lw0, b0, kd0, lw1, b1, kd1 = _rw_prep_math(
            k_ref[...], _mm1(tw, w20_ref[...]), _mm1(tw, w21_ref[...]), _mm1(la, a20_ref[...]), _mm1(la, a21_ref[...]),
            w0_ref[...], a0_ref[...], kk_ref[...], ka_ref[...])
        kap_ref[...] = kap
        lw_ref[0], lw_ref[1] = lw0, lw1
        b_ref[0], b_ref[1] = b0, b1
        kd_ref[0], kd_ref[1] = kd0, kd1

    two = pl.BlockSpec((2, TMV, RWW), lambda i: (0, i, 0))
    return _call(body, "rw_prep_fwd", (tt // TMV,), ins, [pl.BlockSpec((TMV, RWW), lambda i: (i, 0)), two, two, two],
                 [_sds((tt, RWW)), _sds((2, tt, RWW)), _sds((2, tt, RWW)), _sds((2, tt, RWW))], sem=("parallel",))(
        sh, sh, sh, w0, w2p0, w2p1, a0, a2p0, a2p1, kkp, kap_p)


def _rw_prep_bwd(sh, w0, w2p0, w2p1, a0, a2p0, a2p1, kkp, kap_p, scan_grads, dkd_p, dr_p, dv_p):
    dm, ins = _rw_prep_specs()
    tt, sh_w = dm["TT"], dm["SH"]
    one = pl.BlockSpec((TMV, RWW), lambda i: (i, 0))

    def body(k_ref, low_ref, loa_ref, w0_ref, w20_ref, w21_ref, a0_ref, a20_ref, a21_ref, kk_ref, ka_ref,
             dr0, dr1, dv0, dv1, dka0, dka1, dlw0, dlw1, db0, db1, dkd0, dkd1, dkdp_ref, drp_ref, dvp_ref,
             dsh_ref, dw0_ref, dw20_ref, dw21_ref, da0_ref, da20_ref, da21_ref, dkk_ref, dka_ref):
        tw, la = jnp.tanh(low_ref[...]), loa_ref[...]
        w20, w21, a20, a21 = w20_ref[...], w21_ref[...], a20_ref[...], a21_ref[...]
        _, vjp = jax.vjp(_rw_prep_math, k_ref[...], _mm1(tw, w20), _mm1(tw, w21), _mm1(la, a20), _mm1(la, a21),
                         w0_ref[...], a0_ref[...], kk_ref[...], ka_ref[...])
        dkp = dkdp_ref[...]
        dk, dpw0, dpw1, dpa0, dpa1, dw0, da0, dkk, dka = vjp(
            (dka0[...] + dka1[...], dlw0[...], db0[...], dkd0[...] + dkp, dlw1[...], db1[...], dkd1[...] + dkp))
        twt, lat = tw.T, la.T
        dtw = _mm1_nt(dpw0, w20) + _mm1_nt(dpw1, w21)
        dsh_ref[:, 0:RWW] = dr0[...] + dr1[...] + drp_ref[...]
        dsh_ref[:, RWW:2 * RWW] = dk
        dsh_ref[:, 2 * RWW:3 * RWW] = dv0[...] + dv1[...] + dvp_ref[...]
        dsh_ref[:, 3 * RWW:3 * RWW + 128] = dtw * (1.0 - tw * tw)
        dsh_ref[:, 3 * RWW + 128:3 * RWW + 256] = _mm1_nt(dpa0, a20) + _mm1_nt(dpa1, a21)
        _acc_out((dw0_ref, dw20_ref, dw21_ref, da0_ref, da20_ref, da21_ref, dkk_ref, dka_ref), pl.program_id(0) == 0,
                 (dw0, _mm1(twt, dpw0), _mm1(twt, dpw1), da0, _mm1(lat, dpa0), _mm1(lat, dpa1), dkk, dka))

    par = [_full((2, RWW)), _full((128, RWW)), _full((128, RWW)), _full((2, RWW)), _full((128, RWW)), _full((128, RWW)),
           _full((1, RWW)), _full((1, RWW))]
    return _call(body, "rw_prep_bwd", (tt // TMV,), ins + [one] * 15,
                 [pl.BlockSpec((TMV, sh_w), lambda i: (i, 0))] + par,
                 [_sds((tt, sh_w)), _sds((2, RWW)), _sds((128, RWW)), _sds((128, RWW)), _sds((2, RWW)), _sds((128, RWW)), _sds((128, RWW)),
                  _sds((1, RWW)), _sds((1, RWW))],
                 sem=("arbitrary",))(sh, sh, sh, w0, w2p0, w2p1, a0, a2p0, a2p1, kkp, kap_p, *scan_grads, dkd_p, dr_p, dv_p)


def _rw_post_specs():
    dm = _dims()
    nv = dm["NCTV"]
    rw = RWW // 128
    lat3 = lambda d: pl.BlockSpec((None, TMV, RWW), lambda i: (d, nv + i, 0))
    lat2 = pl.BlockSpec((TMV, RWW), lambda i: (nv + i, 0))
    ins = [lat2, lat2, lat3(0), lat3(1),
           pl.BlockSpec((TMV, RWW), lambda i: (nv + i, 0)), pl.BlockSpec((TMV, RWW), lambda i: (nv + i, 2)),
           pl.BlockSpec((TMV, RWW), lambda i: (i, 0)), _full((1, RWW)), _full((1, RWW)), _full((1, RWW))]
    return dm, nv, ins


def _rw_post_fwd(y0_, y1_, kd, sh, p_z, rk, gng, gnb):
    dm, nv, ins = _rw_post_specs()

    def body(y0, y1, k0, k1, r, v, z, rk_ref, g_ref, b_ref, o_ref):
        o_ref[...] = _rw_post_math(y0[...], y1[...], k0[...], k1[...], r[...], v[...], z[...], rk_ref[...], g_ref[...], b_ref[...]).astype(BF16)

    return _call(body, "rw_post_fwd", (T // TMV,), ins, pl.BlockSpec((TMV, RWW), lambda i: (i, 0)), _sds((T, RWW), BF16),
                 sem=("parallel",))(y0_, y1_, kd, kd, sh, sh, p_z, rk, gng, gnb)


def _rw_post_bwd(y0_, y1_, kd, sh, p_z, rk, gng, gnb, dout):
    dm, nv, _ = _rw_post_specs()
    tt = dm["TT"]
    cl = lambda i: jnp.maximum(i - nv, 0)
    all3 = lambda d: pl.BlockSpec((None, TMV, RWW), lambda i: (d, i, 0))
    all2 = pl.BlockSpec((TMV, RWW), lambda i: (i, 0))
    ins = [all2, all2, all3(0), all3(1),
           pl.BlockSpec((TMV, RWW), lambda i: (i, 0)), pl.BlockSpec((TMV, RWW), lambda i: (i, 2)),
           pl.BlockSpec((TMV, RWW), lambda i: (cl(i), 0)), _full((1, RWW)), _full((1, RWW)), _full((1, RWW)),
           pl.BlockSpec((TMV, RWW), lambda i: (cl(i), 0))]
    tok = pl.BlockSpec((TMV, RWW), lambda i: (i, 0))

    def body(y0, y1, k0, k1, r, v, z, rk_ref, g_ref, b_ref, do_ref, dy_ref, dkd_ref, dr_ref, dv_ref, dz_ref, drk_ref, dg_ref, db_ref):
        i = pl.program_id(0)
        lat = jnp.where(i >= nv, 1.0, 0.0)
        _, vjp = jax.vjp(_rw_post_math, y0[...], y1[...], k0[...], k1[...], r[...], v[...], z[...], rk_ref[...], g_ref[...], b_ref[...])
        dy0, _, dk0, _, dr, dv, dz, drk, dg, db = vjp(do_ref[...] * lat)
        dy_ref[...] = dy0
        dkd_ref[...] = dk0
        dr_ref[...] = dr
        dv_ref[...] = dv
        dz_ref[...] = dz.astype(BF16)
        _acc_out((drk_ref, dg_ref, db_ref), i == 0, (drk, dg, db))

    return _call(body, "rw_post_bwd", (tt // TMV,), ins,
                 [tok, tok, tok, tok, pl.BlockSpec((TMV, RWW), lambda i: (cl(i), 0)), _full((1, RWW)), _full((1, RWW)), _full((1, RWW))],
                 [_sds((tt, RWW))] * 4 + [_sds((T, RWW), BF16), _sds((1, RWW)), _sds((1, RWW)), _sds((1, RWW))],
                 sem=("arbitrary",))(y0_, y1_, kd, kd, sh, sh, p_z, rk, gng, gnb, dout)


def _hg_post_fwd(o0_, o1_, p_hg, hg_g):
    dm = _dims()
    nv = dm["NCTV"]
    lat3 = lambda d: pl.BlockSpec((TMV, HGW), lambda i: (nv + i, 0))

    def body(o0, o1, z, g, out):
        out[...] = _hg_post_math(o0[...], o1[...], z[...], g[...]).astype(BF16)

    return _call(body, "hg_post_fwd", (T // TMV,), [lat3(0), lat3(1), pl.BlockSpec((TMV, HGW), lambda i: (nv + i, 4)), _full((1, HGW))],
                 pl.BlockSpec((TMV, HGW), lambda i: (i, 0)), _sds((T, HGW), BF16), sem=("parallel",))(o0_, o1_, p_hg, hg_g)


def _hg_post_bwd(o0_, o1_, p_hg, hg_g, dout):
    dm = _dims()
    nv, tt = dm["NCTV"], dm["TT"]
    all3 = lambda d: pl.BlockSpec((TMV, HGW), lambda i: (i, 0))
    tok = pl.BlockSpec((TMV, HGW), lambda i: (i, 0))

    def body(o0, o1, z, g, do_in, do_ref, dz_ref, dg_ref):
        i = pl.program_id(0)
        lat = jnp.where(i >= nv, 1.0, 0.0)
        _, vjp = jax.vjp(_hg_post_math, o0[...], o1[...], z[...], g[...])
        d0, _, dz, dg = vjp(do_in[...] * lat)
        do_ref[...] = d0
        dz_ref[...] = dz
        _acc_out((dg_ref,), i == 0, (dg,))

    return _call(body, "hg_post_bwd", (tt // TMV,),
                 [all3(0), all3(1), pl.BlockSpec((TMV, HGW), lambda i: (i, 4)), _full((1, HGW)),
                  pl.BlockSpec((TMV, HGW), lambda i: (jnp.maximum(i - nv, 0), 0))],
                 [tok, tok, _full((1, HGW))], [_sds((tt, HGW)), _sds((tt, HGW)), _sds((1, HGW))], sem=("arbitrary",))(o0_, o1_, p_hg, hg_g, dout)


def _hg_dproj(dq0, dq1, di0, di1, df0, df1, dz):
    dm = _dims()
    tt = dm["TT"]
    tok = pl.BlockSpec((TMV, HGW), lambda i: (i, 0))

    def body(q0, q1, i0, i1, f0, f1, z, o_ref):
        o_ref[:, 0:HGW] = (q0[...] + q1[...]).astype(BF16)
        o_ref[:, HGW:2 * HGW] = (i0[...] + i1[...]).astype(BF16)
        o_ref[:, 2 * HGW:3 * HGW] = f0[...].astype(BF16)
        o_ref[:, 3 * HGW:4 * HGW] = f1[...].astype(BF16)
        o_ref[:, 4 * HGW:5 * HGW] = z[...].astype(BF16)

    return _call(body, "hg_dproj", (tt // TMV,), [tok] * 7,
                 pl.BlockSpec((TMV, 5 * HGW), lambda i: (i, 0)), _sds((tt, 5 * HGW), BF16), sem=("parallel",))(
        dq0, dq1, di0, di1, df0, df1, dz)


def _merge1_fwd(out_hg, out_rw, whg_t, wrw_t, p_g):
    tok = pl.BlockSpec((TM, D), lambda i: (i, 0))

    def body(h_ref, r_ref, wh_ref, wr_ref, g1_ref, g2_ref, a_ref, b_ref, m_ref):
        nt = (((1,), (1,)), ((), ()))
        a = lax.dot_general(h_ref[...], wh_ref[...], nt, preferred_element_type=F32)
        b = lax.dot_general(r_ref[...], wr_ref[...], nt, preferred_element_type=F32)
        a_ref[...] = a
        b_ref[...] = b
        m_ref[...] = _gate_math(g1_ref[...], g2_ref[...], a, b).astype(BF16)

    return _call(body, "merge1_fwd", (T // TM,),
                 [pl.BlockSpec((TM, HGW), lambda i: (i, 0)), pl.BlockSpec((TM, RWW), lambda i: (i, 0)), _full((D, HGW)), _full((D, RWW)),
                  pl.BlockSpec((TM, D), lambda i: (i, 0)), pl.BlockSpec((TM, D), lambda i: (i, 1))],
                 [tok, tok, tok], [_sds((T, D)), _sds((T, D)), _sds((T, D), BF16)], sem=("parallel",))(out_hg, out_rw, whg_t, wrw_t, p_g, p_g)


def _head_fwd_bwd(x, merged, w_out, mods, final_g, tgt):
    tok = pl.BlockSpec((TM, D), lambda i: (i, 0))

    def body(x_ref, m_ref, w_ref, mods_ref, fg_ref, t_ref, dxo_ref, dmo_ref, loss_ref, dgate_ref, dfg_ref):
        mo = jnp.dot(m_ref[...], w_ref[...], preferred_element_type=F32)
        gate = mods_ref[0:1, 2 * D:3 * D]
        loss, vjp = jax.vjp(lambda x_, mo_, g_, fg_: _head_math(x_, mo_, g_, fg_, t_ref[...]), x_ref[...], mo, gate, fg_ref[...])
        dx, dmo, dgate, dfg = vjp(jnp.ones((1, 1), F32))
        dxo_ref[...] = dx
        dmo_ref[...] = dmo.astype(BF16)
        _acc_out((loss_ref, dgate_ref, dfg_ref), pl.program_id(0) == 0, (jnp.broadcast_to(loss, (1, 128)), dgate, dfg))

    return _call(body, "head_fwd_bwd", (T // TM,), [tok, tok, _full((D, D)), _full((8, 3 * D)), _full((1, D)), tok],
                 [tok, tok, _full((1, 128)), _full((1, D)), _full((1, D))],
                 [_sds((T, D)), _sds((T, D), BF16), _sds((1, 128)), _sds((1, D)), _sds((1, D))], sem=("arbitrary",))(
        x, merged, w_out, mods, final_g, tgt)


def _merge_bwd(dmo, w_out, p_g, a, b):
    tok = pl.BlockSpec((TM, D), lambda i: (i, 0))

    def body(dmo_ref, w_ref, g1_ref, g2_ref, a_ref, b_ref, da_ref, db_ref, dg_ref):
        dm_ = lax.dot_general(dmo_ref[...], w_ref[...], (((1,), (1,)), ((), ())), preferred_element_type=F32)
        _, vjp = jax.vjp(_gate_math, g1_ref[...], g2_ref[...], a_ref[...], b_ref[...])
        dg1, dg2, da, db = vjp(dm_)
        da_ref[...] = da.astype(BF16)
        db_ref[...] = db.astype(BF16)
        dg_ref[:, 0:D] = dg1.astype(BF16)
        dg_ref[:, D:2 * D] = dg2.astype(BF16)

    return _call(body, "merge_bwd", (T // TM,),
                 [tok, _full((D, D)), pl.BlockSpec((TM, D), lambda i: (i, 0)), pl.BlockSpec((TM, D), lambda i: (i, 1)), tok, tok],
                 [tok, tok, pl.BlockSpec((TM, 2 * D), lambda i: (i, 0))], [_sds((T, D), BF16), _sds((T, D), BF16), _sds((T, 2 * D), BF16)],
                 sem=("parallel",))(dmo, w_out, p_g, p_g, a, b)


def _h_bwd(ctx, x, mods, norm_g, dh_hg, dh_rw, dh_z, dh_g, dxo, dgate):
    dm = _dims()
    nct, tt = dm["NCT"], dm["TT"]
    tok = pl.BlockSpec((TM, D), lambda i: (i, 0))
    lat = pl.BlockSpec((TM, D), lambda i: (jnp.maximum(i - nct, 0), 0))

    def body(c_ref, x_ref, m_ref, g_ref, d1, d2, d3, d4, dxo_ref, dgate_ref, gx_ref, dng_ref, dmods_ref):
        i = pl.program_id(0)
        is_ctx = i < nct
        latf = jnp.where(is_ctx, 0.0, 1.0)
        dh = d1[...] + d2[...] + latf * (d3[...] + d4[...])
        _, vjp = jax.vjp(_h_math, jnp.where(is_ctx, c_ref[...], x_ref[...]), g_ref[...], _sel_mod(m_ref, is_ctx, D),
                         _sel_mod(m_ref, is_ctx, 0))
        dx, dng, dscale, dshift = vjp(dh)
        gx_ref[...] = dx + dxo_ref[...]

        @pl.when(i == 0)
        def _():
            dng_ref[...] = jnp.zeros_like(dng_ref)
            dmods_ref[...] = jnp.zeros_like(dmods_ref)
            dmods_ref[0:1, 2 * D:3 * D] = dgate_ref[...]

        dng_ref[...] += dng
        row = lax.broadcasted_iota(jnp.int32, (8, 1), 0)
        sel = jnp.where(row == jnp.where(is_ctx, 1, 0), 1.0, 0.0)
        dmods_ref[:, 0:D] += sel * dshift
        dmods_ref[:, D:2 * D] += sel * dscale

    return _call(body, "h_bwd", (tt // TM,),
                 [*_token_specs(), _full((8, 3 * D)), _full((1, D)), tok, tok, lat, lat, lat, _full((1, D))],
                 [lat, _full((1, D)), _full((8, 3 * D))], [_sds((T, D)), _sds((1, D)), _sds((8, 3 * D))], sem=("arbitrary",))(
        ctx, x, mods, norm_g, dh_hg, dh_rw, dh_z, dh_g, dxo, dgate)


def _lb_math(l0, l1):
    return jax.nn.sigmoid(l0 - l1)


def _lb_fwd(l0, l1):
    def body(a_ref, b_ref, o_ref):
        o_ref[...] = _lb_math(a_ref[...], b_ref[...])

    return _call(body, "lb_fwd", (1,), [_full((2, HGW))] * 2, _full((2, HGW)), _sds((2, HGW)))(l0, l1)


def _lb_bwd(l0, l1, dlb):
    def body(a_ref, b_ref, d_ref, da_ref, db_ref):
        _, vjp = jax.vjp(_lb_math, a_ref[...], b_ref[...])
        da_ref[...], db_ref[...] = vjp(d_ref[...])

    return _call(body, "lb_bwd", (1,), [_full((2, HGW))] * 3, [_full((2, HGW))] * 2, [_sds((2, HGW))] * 2)(l0, l1, dlb)


def _local_step(x, ctx, tgt, mods, w, start_exchange):
    dm = _dims()
    tt, sh_w, hgc = dm["TT"], dm["SH"], dm["HGC"]
    nb = lambda cols: cols // WB
    h = _h_fwd(ctx, x, mods, w["norm_g"])
    win_t = w["win_t"]
    p_hg = _proj_nt(h, win_t, 0, nb(hgc), 0, tt, "proj_hg")
    p_rw = _proj_nt(h, win_t, nb(hgc), nb(sh_w), 0, tt, "proj_rw")
    p_z = _proj_nt(h, win_t, nb(hgc + sh_w), nb(RWW), TC, T, "proj_z")
    p_g = _proj_nt(h, win_t, nb(hgc + sh_w + RWW), nb(2 * D), TC, T, "proj_g")

    lb3 = _lb_fwd(w["lb0"], w["lb1"]).reshape(2, 1, HGW)
    o0, o1, hg_ck = _hg_scan_fwd(p_hg, lb3)
    out_hg = _hg_post_fwd(o0, o1, p_hg, w["hg_g"])

    sh = _shift_fwd(p_rw, w["mu"])
    zpad = jnp.zeros((LORA, RWW), F32)
    prep_w = (w["w0"], jnp.concatenate([w["w2"][0], zpad], 0), jnp.concatenate([zpad, w["w2"][1]], 0),
              w["a0"], jnp.concatenate([w["a2"][0], zpad], 0), jnp.concatenate([zpad, w["a2"][1]], 0), w["kk"], w["ka"])
    kap, lw, b, kd = _rw_prep_fwd(sh, *prep_w)
    y0, y1, rw_ck, rw_tm = _rw_scan_fwd(sh, kap, lw, b, kd)
    post_w = (w["rk"], w["gng"], w["gnb"])
    out_rw = _rw_post_fwd(y0, y1, kd, sh, p_z, *post_w)

    late = w["late_weights"](out_rw)
    a, bb, merged = _merge1_fwd(out_hg, out_rw, late["whg_t"], late["wrw_t"], p_g)
    dxo, dmo, loss, dgate, d_fg = _head_fwd_bwd(x, merged, late["wout"], mods, w["final_g"], tgt)
    da, db, dp_g = _merge_bwd(dmo, late["wout"], p_g, a, bb)
    g_wout = _dw_tn(merged, dmo, 0, D, 0, "dw_out")
    g_whg = _dw_tn(da, out_hg, 0, D, 0, "dw_hg")
    g_wrw = _dw_tn(db, out_rw, 0, D, 0, "dw_rw")
    token, early = start_exchange(dict(wout=g_wout, whg_t=g_whg, wrw_t=g_wrw), "early")
    d_out_hg = _mm_nn(da, late["whg_t"], 0, "dx_hg", after=token)
    d_out_rw = _mm_nn(db, late["wrw_t"], 0, "dx_rw")

    do, dz_hg, d_hg_g = _hg_post_bwd(o0, o1, p_hg, w["hg_g"], d_out_hg)
    *hg_grads, dlb3 = _hg_scan_bwd(p_hg, lb3, hg_ck, do)
    dp_hg = _hg_dproj(*hg_grads, dz_hg)
    d_lb0, d_lb1 = _lb_bwd(w["lb0"], w["lb1"], dlb3.reshape(2, HGW))

    dy, dkd_p, dr_p, dv_p, dp_z, d_rk, d_gng, d_gnb = _rw_post_bwd(y0, y1, kd, sh, p_z, *post_w, d_out_rw)
    rw_grads = _rw_scan_bwd(sh, kap, lw, b, kd, rw_ck, rw_tm, dy)
    dsh, d_w0, d_w2p0, d_w2p1, d_a0, d_a2p0, d_a2p1, d_kk, d_ka = _rw_prep_bwd(sh, *prep_w, rw_grads, dkd_p, dr_p, dv_p)
    dp_rw, d_mu = _shift_bwd(p_rw, dsh, w["mu"])

    g_win = _dw_tn(dp_hg, h, 0, dm["NCOLS"], 0, "dw_in_hg")
    g_win = _dw_tn(dp_rw, h, 0, dm["NCOLS"], nb(hgc), "dw_in_rw", prev=g_win)
    g_win = _dw_tn(dp_z, h, TC, dm["NCOLS"], nb(hgc + sh_w), "dw_in_z", prev=g_win)
    g_win = _dw_tn(dp_g, h, TC, dm["NCOLS"], nb(hgc + sh_w + RWW), "dw_in_g", prev=g_win)
    token, late = start_exchange(dict(win_t=g_win), "late")
    in_flight = (early, late)
    dh_hg = _mm_nn(dp_hg, win_t, 0, "dh_hg", after=token)
    dh_rw = _mm_nn(dp_rw, win_t, nb(hgc), "dh_rw")
    dh_z = _mm_nn(dp_z, win_t, nb(hgc + sh_w), "dh_z")
    dh_g = _mm_nn(dp_g, win_t, nb(hgc + sh_w + RWW), "dh_g")
    gx, d_ng, dmods = _h_bwd(ctx, x, mods, w["norm_g"], dh_hg, dh_rw, dh_z, dh_g, dxo, dgate)

    small = dict(norm_g=d_ng, lb0=d_lb0, lb1=d_lb1, hg_g=d_hg_g, mu=d_mu, w0=d_w0,
                 w2=jnp.stack([d_w2p0[:LORA], d_w2p1[LORA:]]), a0=d_a0, a2=jnp.stack([d_a2p0[:LORA], d_a2p1[LORA:]]),
                 kk=d_kk, ka=d_ka, rk=d_rk, gng=d_gng, gnb=d_gnb, final_g=d_fg)
    return loss, gx, dmods, in_flight, small


MESH = pl.DeviceIdType.MESH


def _comm_call(body, name, bufs, out_shapes, nsem, nloc):
    hbm = pl.BlockSpec(memory_space=pl.ANY)
    return pl.pallas_call(
        body, name=name, in_specs=[hbm] * len(bufs), out_specs=[hbm] * len(out_shapes), out_shape=out_shapes,
        scratch_shapes=[pltpu.SemaphoreType.DMA((nsem,)), pltpu.SemaphoreType.DMA((nsem,)), pltpu.SemaphoreType.DMA((nloc,))],
    )(*bufs)


GATHER_PIECE_BYTES = 4 * 1024 * 1024


def _gather2(bufs, name):
    pieces = []
    for bi, b in enumerate(bufs):
        rows = b.shape[0]
        if rows % 32 == 0 and b.size * b.dtype.itemsize > GATHER_PIECE_BYTES:
            pieces += [(bi, 0, rows // 2), (bi, rows // 2, rows // 2)]
        else:
            pieces.append((bi, 0, rows))
    nbuf = len(bufs)

    def body(*refs):
        ins, outs = refs[:nbuf], refs[nbuf:2 * nbuf]
        send_sems, recv_sems, _ = refs[2 * nbuf:]
        x, y, c = lax.axis_index("x"), lax.axis_index("y"), lax.axis_index("c")
        me, sib = (x, y, c), (x, y, 1 - c)
        flip = lambda a, b: a + b - 2 * a * b
        xn, yn, diag = (1 - x, y), (x, 1 - y), (1 - x, 1 - y)
        relay_from, relay_to = (flip(x, c), flip(1 - y, c)), (flip(1 - x, c), flip(y, c))

        def copy(pi, k, block, to, own=False):
            bi, r0, nr = pieces[pi]
            rows = outs[bi].at[4 * block[0] + 2 * block[1] + block[2], pl.ds(r0, nr)]
            return pltpu.make_async_remote_copy(src_ref=ins[bi].at[pl.ds(r0, nr)] if own else rows, dst_ref=rows,
                                                send_sem=send_sems.at[7 * pi + k], recv_sem=recv_sems.at[7 * pi + k],
                                                device_id=to, device_id_type=MESH)

        started = []
        for pi in range(len(pieces)):
            started += [copy(pi, 0, me, sib, own=True), copy(pi, 1, me, (*xn, c), own=True), copy(pi, 2, me, (*yn, c), own=True)]
        for cp in started:
            cp.start()
        for pi in range(len(pieces)):
            for k, chip in ((1, xn), (2, yn)):
                copy(pi, k, (*chip, c), me).wait_recv()
                started.append(copy(pi, 3 + k, (*chip, c), sib))
                started[-1].start()
            started.append(copy(pi, 3, (*relay_from, c), (*relay_to, c)))
            started[-1].start()
        for pi in range(len(pieces)):
            copy(pi, 3, (*diag, c), me).wait_recv()
            started.append(copy(pi, 6, (*diag, c), sib))
            started[-1].start()
        for pi in range(len(pieces)):
            copy(pi, 0, sib, me).wait_recv()
            for k, chip in ((4, xn), (5, yn), (6, diag)):
                copy(pi, k, (*chip, 1 - c), me).wait_recv()
        for cp in started:
            cp.wait_send()

    return _comm_call(body, name, bufs, [_sds((NDEV,) + b.shape, b.dtype) for b in bufs], 7 * len(pieces), 1)


def _gather_all(bufs, name):
    me = 4 * lax.axis_index("x") + 2 * lax.axis_index("y") + lax.axis_index("c")
    return [lax.dynamic_update_index_in_dim(g, b, me, axis=0) for g, b in zip(_gather2(bufs, name), bufs)]


def _peer_copies(srcs, lands, send_sems, recv_sems):
    x, y, c = lax.axis_index("x"), lax.axis_index("y"), lax.axis_index("c")
    me = 4 * x + 2 * y + c
    pairs = []
    for bi in range(len(srcs)):
        for k in range(1, NDEV):
            px, py, pc = (1 - x if k & 4 else x), (1 - y if k & 2 else y), (1 - c if k & 1 else c)
            mk = lambda dst: pltpu.make_async_remote_copy(
                src_ref=srcs[bi], dst_ref=dst, send_sem=send_sems.at[7 * bi + k - 1], recv_sem=recv_sems.at[7 * bi + k - 1],
                device_id=(px, py, pc), device_id_type=MESH)
            pairs.append((mk(lands[bi].at[me]), mk(lands[bi].at[4 * px + 2 * py + pc])))
    return pairs


def _direct_gather_start(bufs, dep, name):
    nbuf = len(bufs)
    hbm, sem = pl.BlockSpec(memory_space=pltpu.HBM), pl.BlockSpec(memory_space=pltpu.SEMAPHORE)

    def body(*refs):
        srcs, lands = refs[:nbuf], refs[nbuf:2 * nbuf]
        for send, _ in _peer_copies(srcs, lands, refs[2 * nbuf + 1], refs[2 * nbuf + 2]):
            send.start()
        refs[-1][...] = jnp.zeros_like(refs[-1])

    held = [pltpu.HBM(b.shape, b.dtype) for b in bufs]
    zones = [pltpu.HBM((NDEV,) + b.shape, b.dtype) for b in bufs]
    outs = pl.pallas_call(
        body, name=name, in_specs=[hbm] * (2 * nbuf) + [pl.BlockSpec(memory_space=pl.ANY)],
        out_specs=(sem, sem, *[hbm] * (2 * nbuf), pl.BlockSpec(memory_space=pltpu.VMEM)),
        out_shape=(pltpu.SemaphoreType.DMA((7 * nbuf,)), pltpu.SemaphoreType.DMA((7 * nbuf,)), *held, *zones, _sds((8, 128))),
        input_output_aliases={i: 2 + i for i in range(2 * nbuf)},
        compiler_params=pltpu.CompilerParams(has_side_effects=pltpu.SideEffectType.DATAFLOW_SIDE_EFFECTING),
    )(*[pltpu.with_memory_space_constraint(b, pltpu.HBM) for b in bufs],
      *[pltpu.with_memory_space_constraint(lax.empty((NDEV,) + b.shape, b.dtype), pltpu.HBM) for b in bufs], dep)
    return (outs[0], outs[1], list(outs[2:2 + nbuf]), list(outs[2 + nbuf:2 + 2 * nbuf])), outs[-1]


def _direct_gather_wait(send_sems, recv_sems, srcs, lands, after, name):
    nbuf = len(srcs)
    hbm, sem = pl.BlockSpec(memory_space=pltpu.HBM), pl.BlockSpec(memory_space=pltpu.SEMAPHORE)

    def body(*refs):
        for send, recv in _peer_copies(refs[:nbuf], refs[nbuf:2 * nbuf], refs[2 * nbuf], refs[2 * nbuf + 1]):
            send.wait_send()
            recv.wait_recv()

    outs = pl.pallas_call(
        body, name=name, in_specs=[hbm] * (2 * nbuf) + [sem, sem, pl.BlockSpec(memory_space=pl.ANY)], out_specs=[hbm] * (2 * nbuf),
        out_shape=(*[pltpu.HBM(b.shape, b.dtype) for b in srcs], *[pltpu.HBM(b.shape, b.dtype) for b in lands]),
        input_output_aliases={i: i for i in range(2 * nbuf)},
        compiler_params=pltpu.CompilerParams(has_side_effects=pltpu.SideEffectType.DATAFLOW_SIDE_EFFECTING),
    )(*srcs, *lands, send_sems, recv_sems, after)
    me = 4 * lax.axis_index("x") + 2 * lax.axis_index("y") + lax.axis_index("c")
    return [lax.dynamic_update_index_in_dim(g, b, me, axis=0) for g, b in zip(outs[nbuf:], outs[:nbuf])]


def _pair_exchange(bufs, name):
    nbuf = len(bufs)

    def body(*refs):
        ins, got = refs[:nbuf], refs[nbuf:2 * nbuf]
        send_sems, recv_sems, _ = refs[2 * nbuf:]
        x, y, c = lax.axis_index("x"), lax.axis_index("y"), lax.axis_index("c")
        copies = []
        for bi in range(nbuf):
            for q in range(4):
                copies.append(pltpu.make_async_remote_copy(
                    src_ref=ins[bi].at[2 * q + 1 - c], dst_ref=got[bi].at[q], send_sem=send_sems.at[4 * bi + q],
                    recv_sem=recv_sems.at[4 * bi + q], device_id=(x, y, 1 - c), device_id_type=MESH))
                copies[-1].start()
        for cp in copies:
            cp.wait_send()
            cp.wait_recv()

    return _comm_call(body, name, bufs, [_sds((4,) + b.shape[1:], b.dtype) for b in bufs], 4 * nbuf, 1)


def _chip_copies(srcs, lands, send_sems, recv_sems):
    x, y, c = lax.axis_index("x"), lax.axis_index("y"), lax.axis_index("c")
    myq = 2 * x + y
    pairs = []
    for bi in range(len(srcs)):
        for j, (qx, qy) in enumerate([(1 - x, y), (x, 1 - y), (1 - x, 1 - y)]):
            q = 2 * qx + qy
            mk = lambda src, dst: pltpu.make_async_remote_copy(
                src_ref=src, dst_ref=dst, send_sem=send_sems.at[3 * bi + j], recv_sem=recv_sems.at[3 * bi + j],
                device_id=(qx, qy, c), device_id_type=MESH)
            pairs.append((mk(srcs[bi].at[q], lands[bi].at[myq]), mk(srcs[bi].at[myq], lands[bi].at[q])))
    return pairs


def _chip_exchange_start(bufs, name):
    nbuf = len(bufs)
    hbm, sem = pl.BlockSpec(memory_space=pltpu.HBM), pl.BlockSpec(memory_space=pltpu.SEMAPHORE)

    def body(*refs):
        srcs, lands = refs[:nbuf], refs[nbuf:2 * nbuf]
        send_sems, recv_sems = refs[2 * nbuf], refs[2 * nbuf + 1]
        token = refs[-1]
        for send, _ in _chip_copies(srcs, lands, send_sems, recv_sems):
            send.start()
        token[...] = jnp.zeros_like(token)

    held = [pltpu.HBM(b.shape, b.dtype) for b in bufs]
    outs = pl.pallas_call(
        body, name=name, in_specs=[hbm] * (2 * nbuf), out_specs=(sem, sem, *[hbm] * (2 * nbuf), pl.BlockSpec(memory_space=pltpu.VMEM)),
        out_shape=(pltpu.SemaphoreType.DMA((3 * nbuf,)), pltpu.SemaphoreType.DMA((3 * nbuf,)), *held, *held, _sds((8, 128))),
        input_output_aliases={i: 2 + i for i in range(2 * nbuf)},
        compiler_params=pltpu.CompilerParams(has_side_effects=pltpu.SideEffectType.DATAFLOW_SIDE_EFFECTING),
    )(*[pltpu.with_memory_space_constraint(b, pltpu.HBM) for b in bufs],
      *[pltpu.with_memory_space_constraint(lax.empty(b.shape, b.dtype), pltpu.HBM) for b in bufs])
    return outs[0], outs[1], list(outs[2:2 + nbuf]), list(outs[2 + nbuf:2 + 2 * nbuf]), outs[-1]


def _chip_exchange_wait(send_sems, recv_sems, srcs, lands, after, name):
    nbuf = len(srcs)
    hbm, sem = pl.BlockSpec(memory_space=pltpu.HBM), pl.BlockSpec(memory_space=pltpu.SEMAPHORE)

    def body(*refs):
        src_refs, land_refs = refs[:nbuf], refs[nbuf:2 * nbuf]
        for send, recv in _chip_copies(src_refs, land_refs, refs[2 * nbuf], refs[2 * nbuf + 1]):
            send.wait_send()
            recv.wait_recv()

    held = [pltpu.HBM(b.shape, b.dtype) for b in srcs]
    outs = pl.pallas_call(
        body, name=name, in_specs=[hbm] * (2 * nbuf) + [sem, sem, pl.BlockSpec(memory_space=pl.ANY)], out_specs=[hbm] * (2 * nbuf),
        out_shape=(*held, *held), input_output_aliases={i: i for i in range(2 * nbuf)},
        compiler_params=pltpu.CompilerParams(has_side_effects=pltpu.SideEffectType.DATAFLOW_SIDE_EFFECTING),
    )(*srcs, *lands, send_sems, recv_sems, after)
    return list(outs[:nbuf]), list(outs[nbuf:])


def _prefetch_call(body, name, scalars, grid, in_specs, out_specs, out_shape, args):
    return pl.pallas_call(
        body, name=name, out_shape=out_shape,
        grid_spec=pltpu.PrefetchScalarGridSpec(num_scalar_prefetch=1, grid=grid, in_specs=in_specs, out_specs=out_specs),
        compiler_params=pltpu.CompilerParams(dimension_semantics=("parallel",) * len(grid), vmem_limit_bytes=V7X_VMEM_LIMIT))(scalars, *args)


def _pair_add(g, got, core, name):
    _, rows, cols = got.shape
    tr = _row_tile(rows, 16, 256)

    def body(c_ref, a_ref, b_ref, o_ref):
        o_ref[...] = (a_ref[...].astype(F32) + b_ref[...].astype(F32)).astype(o_ref.dtype)

    blk = pl.BlockSpec((4, tr, cols), lambda i, c_ref: (0, i, 0))
    mine = pl.BlockSpec((4, None, tr, cols), lambda i, c_ref: (0, c_ref[0], i, 0))
    return _prefetch_call(body, name, core, (rows // tr,), [mine, blk], blk, _sds(got.shape, got.dtype),
                          (g.reshape(4, 2, rows, cols), got))


def _reduce_chips(chip_sums, recv, slots, name):
    _, rows, cols = recv.shape
    tr = _row_tile(rows, 16, 128)

    def body(s_ref, own_ref, r1_ref, r2_ref, r3_ref, o_ref):
        o_ref[...] = ((own_ref[...].astype(F32) + r1_ref[...].astype(F32)) + r2_ref[...].astype(F32)) + r3_ref[...].astype(F32)

    pick = lambda k: pl.BlockSpec((None, tr, cols), lambda i, s_ref: (s_ref[k], i, 0))
    return _prefetch_call(body, name, slots, (rows // tr,), [pick(0), pick(1), pick(2), pick(3)],
                          pl.BlockSpec((tr, cols), lambda i, s_ref: (i, 0)), _sds((rows, cols)), (chip_sums, recv, recv, recv))


def _row_tile(rows, mult, cap):
    best = mult
    for t in range(mult, cap + 1, mult):
        if rows % t == 0:
            best = t
    assert rows % best == 0, (rows, mult)
    return best


def _reduce_sources(r, name):
    nsrc, rows, cols = r.shape
    tr = _row_tile(rows, 16 if r.dtype.itemsize == 2 else 8, 128)

    def body(r_ref, o_ref):
        acc = r_ref[0].astype(F32)
        for j in range(1, nsrc):
            acc = acc + r_ref[j].astype(F32)
        o_ref[...] = acc

    return _call(body, name, (rows // tr,), [pl.BlockSpec((nsrc, tr, cols), lambda i: (0, i, 0))],
                 pl.BlockSpec((tr, cols), lambda i: (i, 0)), _sds((rows, cols)), sem=("parallel",))(r)


def _adamw(w, g, m, v, name):
    rows, cols = w.shape
    tr = _row_tile(rows, 8, 128)
    c1 = 1.0 - ADAM_B1 ** ADAM_STEP
    c2 = 1.0 - ADAM_B2 ** ADAM_STEP

    def body(w_ref, g_ref, m_ref, v_ref, d_ref, mo_ref, vo_ref):
        g_ = g_ref[...]
        m_ = ADAM_B1 * m_ref[...] + (1.0 - ADAM_B1) * g_
        v_ = ADAM_B2 * v_ref[...] + (1.0 - ADAM_B2) * (g_ * g_)
        d_ref[...] = -ADAM_LR * ((m_ / c1) / (jnp.sqrt(v_ / c2) + ADAM_EPS) + ADAM_WD * w_ref[...])
        mo_ref[...] = m_
        vo_ref[...] = v_

    blk = pl.BlockSpec((tr, cols), lambda i: (i, 0))
    return _call(body, name, (rows // tr,), [blk] * 4, [blk] * 3, [_sds((rows, cols))] * 3, sem=("parallel",))(w, g, m, v)


SLAB_PART = 1024


def _pack(arrs):
    parts = []
    for a in arrs:
        flat = a.reshape(-1)
        pad = (-flat.shape[0]) % SLAB_PART
        if pad:
            flat = jnp.concatenate([flat, jnp.zeros((pad,), flat.dtype)])
        parts.append(flat.reshape(-1, 128))
    return jnp.concatenate(parts, axis=0)


def _unpack(slab, shapes):
    outs, row = [], 0
    for s in shapes:
        n = 1
        for d in s:
            n *= d
        rows = (n + SLAB_PART - 1) // SLAB_PART * (SLAB_PART // 128)
        outs.append(slab[row:row + rows].reshape(-1)[:n].reshape(s))
        row += rows
    return outs


def _unshard_last(g, lead):
    nl = len(lead)
    return jnp.transpose(g, tuple(range(1, nl + 1)) + (0, nl + 1)).reshape(tuple(lead) + (-1,))


def kernel(x, c, ctx, c_ctx, ada_w, ada_b, norm_g, w_in, hg_lb, hg_norm_g, rw_mu, rw_w0, rw_w2, rw_a0, rw_a2, rw_kk, rw_ka, rw_rk, rw_gn_g, rw_gn_b, w_hg_out, w_rw_out, w_out, final_g, loss_target, m_c_ctx, m_ada_w, m_ada_b, m_norm_g, m_w_in, m_hg_lb, m_hg_norm_g, m_rw_mu, m_rw_w0, m_rw_w2, m_rw_a0, m_rw_a2, m_rw_kk, m_rw_ka, m_rw_rk, m_rw_gn_g, m_rw_gn_b, m_w_hg_out, m_w_rw_out, m_w_out, m_final_g, v_c_ctx, v_ada_w, v_ada_b, v_norm_g, v_w_in, v_hg_lb, v_hg_norm_g, v_rw_mu, v_rw_w0, v_rw_w2, v_rw_a0, v_rw_a2, v_rw_kk, v_rw_ka, v_rw_rk, v_rw_gn_g, v_rw_gn_b, v_w_hg_out, v_w_rw_out, v_w_out, v_final_g):
    dm = _dims()
    me = 4 * lax.axis_index("x") + 2 * lax.axis_index("y") + lax.axis_index("c")

    sharded_small = [hg_lb, rw_mu[0], rw_w0[0], rw_w2[0], rw_a0[0], rw_a2[0]]
    c_rows = jnp.concatenate([c, jnp.zeros((7, D), F32)], axis=0)
    gathered = _gather_all([w_in[0].T.astype(BF16), _pack(sharded_small), c_rows], "gather_weights")
    win_t = gathered[0].reshape(-1, D)
    per_dev = jax.vmap(lambda s: _unpack(s, [a.shape for a in sharded_small]))(gathered[1])
    hg_lb_f, mu_f, w0_f, w2_f, a0_f, a2_f = [_unshard_last(p, p.shape[1:-1]) for p in per_dev]

    ncol = ada_w.shape[2]
    cc16 = jnp.concatenate([gathered[2][:, 0], c_ctx[None], jnp.zeros((7, D), F32)], axis=0)
    mod_all = _gather_all([_ada_fwd(cc16, ada_w[0])], "gather_mods")[0]
    mod_x = lax.dynamic_index_in_dim(mod_all, me, axis=1, keepdims=False).reshape(1, -1) + ada_b
    mod_c = mod_all[:, NDEV].reshape(1, -1) + ada_b
    mods = jnp.concatenate([mod_x, mod_c, jnp.zeros((6, 3 * D), F32)], axis=0)

    late_flight, token = _direct_gather_start([w_out[0].astype(BF16), w_hg_out[0].T.astype(BF16), w_rw_out[0].T.astype(BF16)],
                                              mod_all, "gather_late_start")
    mods = mods + token[0:1, 0:1]

    def late_weights(after):
        got = _direct_gather_wait(*late_flight, after, "gather_late_wait")
        return dict(zip(("wout", "whg_t", "wrw_t"), [g.reshape(-1, g.shape[2]) for g in got]))

    w = dict(win_t=win_t, late_weights=late_weights, norm_g=norm_g,
             lb0=hg_lb_f[:, 0], lb1=hg_lb_f[:, 1], hg_g=hg_norm_g, mu=mu_f, w0=w0_f, w2=w2_f, a0=a0_f, a2=a2_f,
             kk=rw_kk, ka=rw_ka, rk=rw_rk, gng=rw_gn_g, gnb=rw_gn_b, final_g=final_g[None])
    def start_exchange(grads_bf16, tag):
        names = list(grads_bf16)
        blocks = [grads_bf16[n].reshape(NDEV, -1, grads_bf16[n].shape[1]) for n in names]
        got = _pair_exchange(blocks, "pair_grads_" + tag)
        core = lax.axis_index("c").astype(jnp.int32).reshape(1)
        chip_sums = [_pair_add(g, r, core, "pair_add_" + n) for g, r, n in zip(blocks, got, names)]
        send_sems, recv_sems, held, lands, token = _chip_exchange_start(chip_sums, "chip_grads_start_" + tag)
        return token, (names, send_sems, recv_sems, held, lands)

    loss_dev, grad_x, dmods, in_flight, small = _local_step(x[0], ctx[0], loss_target[0], mods, w, start_exchange)
    loss = lax.psum(loss_dev[0, 0], AXES)

    my_chip = (2 * lax.axis_index("x") + lax.axis_index("y")).astype(jnp.int32)
    others = jnp.arange(3, dtype=jnp.int32)
    slots = jnp.concatenate([my_chip.reshape(1), others + (others >= my_chip).astype(jnp.int32)])

    def finish_exchange(tag, flight, after):
        names, *copies = flight
        chip_sums, recv = _chip_exchange_wait(*copies, after, "chip_grads_wait_" + tag)
        return {n: _reduce_chips(cs, r, slots, "reduce_" + n) for n, cs, r in zip(names, chip_sums, recv)}

    reduced = finish_exchange("early", in_flight[0], grad_x)

    dmods_all = _gather_all([dmods], "gather_dmods")[0]
    my_cols = lambda a: lax.dynamic_slice_in_dim(a, me * ncol, ncol, axis=1)
    g_ada, dcc16, d_ada_b = _ada_bwd(cc16, ada_w[0], my_cols(dmods_all[:, 0]), my_cols(dmods_all[:, 1]), dmods)
    small = dict(small, c_ctx=dcc16[NDEV:NDEV + 1], ada_b=d_ada_b)

    small_names = ["c_ctx", "ada_b", "norm_g", "lb0", "lb1", "hg_g", "mu", "w0", "w2", "a0", "a2", "kk", "ka", "rk", "gng", "gnb", "final_g"]
    small_slab = _pack([small[n] for n in small_names])
    small_all = _gather_all([small_slab], "gather_small_grads")[0]
    small_sum = _reduce_sources(small_all, "reduce_small")
    sg = dict(zip(small_names, _unpack(small_sum, [small[n].shape for n in small_names])))

    reduced.update(finish_exchange("late", in_flight[1], small_sum))
    g_win_t, g_wout, g_whg_t, g_wrw_t = [reduced[n] for n in ("win_t", "wout", "whg_t", "wrw_t")]

    def my_shard(full):
        n = full.shape[-1] // NDEV
        return lax.dynamic_slice_in_dim(full, me * n, n, axis=full.ndim - 1)

    grads = dict(
        c_ctx=sg["c_ctx"][0], ada_w=g_ada[None], ada_b=sg["ada_b"], norm_g=sg["norm_g"], w_in=g_win_t.T[None],
        hg_lb=my_shard(jnp.stack([sg["lb0"], sg["lb1"]], axis=1)), hg_norm_g=sg["hg_g"], rw_mu=my_shard(sg["mu"])[None],
        rw_w0=my_shard(sg["w0"])[None], rw_w2=my_shard(sg["w2"])[None], rw_a0=my_shard(sg["a0"])[None], rw_a2=my_shard(sg["a2"])[None],
        rw_kk=sg["kk"], rw_ka=sg["ka"], rw_rk=sg["rk"], rw_gn_g=sg["gng"], rw_gn_b=sg["gnb"],
        w_hg_out=g_whg_t.T[None], w_rw_out=g_wrw_t.T[None], w_out=g_wout[None], final_g=sg["final_g"][0])
    weights = dict(c_ctx=c_ctx, ada_w=ada_w, ada_b=ada_b, norm_g=norm_g, w_in=w_in, hg_lb=hg_lb, hg_norm_g=hg_norm_g, rw_mu=rw_mu,
                   rw_w0=rw_w0, rw_w2=rw_w2, rw_a0=rw_a0, rw_a2=rw_a2, rw_kk=rw_kk, rw_ka=rw_ka, rw_rk=rw_rk, rw_gn_g=rw_gn_g,
                   rw_gn_b=rw_gn_b, w_hg_out=w_hg_out, w_rw_out=w_rw_out, w_out=w_out, final_g=final_g)
    m_in = dict(zip(weights, (m_c_ctx, m_ada_w, m_ada_b, m_norm_g, m_w_in, m_hg_lb, m_hg_norm_g, m_rw_mu, m_rw_w0, m_rw_w2, m_rw_a0,
                              m_rw_a2, m_rw_kk, m_rw_ka, m_rw_rk, m_rw_gn_g, m_rw_gn_b, m_w_hg_out, m_w_rw_out, m_w_out, m_final_g)))
    v_in = dict(zip(weights, (v_c_ctx, v_ada_w, v_ada_b, v_norm_g, v_w_in, v_hg_lb, v_hg_norm_g, v_rw_mu, v_rw_w0, v_rw_w2, v_rw_a0,
                              v_rw_a2, v_rw_kk, v_rw_ka, v_rw_rk, v_rw_gn_g, v_rw_gn_b, v_w_hg_out, v_w_rw_out, v_w_out, v_final_g)))

    big_w = ["ada_w", "w_in", "w_hg_out", "w_rw_out", "w_out"]
    delta, new_m, new_v = {}, {}, {}
    for n in big_w:
        shp = weights[n].shape
        two = lambda a: a.reshape(shp[-2], shp[-1])
        d_, m_, v_ = _adamw(two(weights[n]), two(grads[n]), two(m_in[n]), two(v_in[n]), "adamw_" + n)
        delta[n], new_m[n], new_v[n] = d_.reshape(shp), m_.reshape(shp), v_.reshape(shp)
    rest = [n for n in weights if n not in big_w]
    shapes = [weights[n].shape for n in rest]
    d_s, m_s, v_s = _adamw(_pack([weights[n] for n in rest]), _pack([grads[n] for n in rest]), _pack([m_in[n] for n in rest]),
                           _pack([v_in[n] for n in rest]), "adamw_small")
    for n, d_, m_, v_ in zip(rest, _unpack(d_s, shapes), _unpack(m_s, shapes), _unpack(v_s, shapes)):
        delta[n], new_m[n], new_v[n] = d_, m_, v_

    order = list(weights)
    return (loss, grad_x[None], *[grads[n] for n in order], *[delta[n] for n in order],
            *[new_m[n] for n in order], *[new_v[n] for n in order])
```

```python
import functools

import jax
import jax.numpy as jnp
from jax import lax
from jax.experimental import pallas as pl
from jax.experimental.pallas import tpu as pltpu

D = 2048
T = 2048
TC = 256
GW = 64
C = 64
HGW = 1024
RWW = 1024
LORA = 64
NDEV = 8
TM = 256
TMV = 128
NORM_EPS = 1e-6
RW_GN_EPS = 64e-5
ADAM_LR, ADAM_B1, ADAM_B2, ADAM_EPS, ADAM_WD, ADAM_STEP = 0.001, 0.9, 0.999, 1e-08, 0.01, 10
WB = 256
V7X_VMEM_LIMIT = 56 * 1024 * 1024

F32 = jnp.float32
BF16 = jnp.bfloat16
AXES = ("x", "y", "c")


def _dims():
    tt = T + TC
    sh = 3 * RWW + 4 * LORA
    hgc = 5 * HGW
    return dict(TT=tt, NCC=TC // C, NCH=tt // C, HG_H=HGW // 128, RW_P=RWW // 128, SH=sh, HGC=hgc,
                NCOLS=hgc + sh + RWW + 2 * D, NCT=TC // TM, NCTV=TC // TMV)


def _mm(a, b):
    return _dot3(a, b, "nn")


def _mm_nt(a, b):
    return _dot3(a, b, "nt")


def _mm_tn(a, b):
    return _dot3(a, b, "tn")


def _split(x):
    hi = x.astype(BF16)
    return hi, (x - hi.astype(F32)).astype(BF16)


_FORMS = {"nn": (((1,), (0,)), ((), ())), "nt": (((1,), (1,)), ((), ())), "tn": (((0,), (0,)), ((), ()))}


def _dot3_raw(a, b, form):
    ah, al = _split(a)
    bh, bl = _split(b)
    d = lambda x, y: lax.dot_general(x, y, _FORMS[form], preferred_element_type=F32)
    return d(ah, bh) + (d(ah, bl) + d(al, bh))


@functools.partial(jax.custom_vjp, nondiff_argnums=(2,))
def _dot3(a, b, form):
    return _dot3_raw(a, b, form)


def _dot3_fwd(a, b, form):
    return _dot3_raw(a, b, form), (a, b)


def _dot3_bwd(form, res, g):
    a, b = res
    if form == "nn":
        return _dot3(g, b, "nt"), _dot3(a, g, "tn")
    if form == "nt":
        return _dot3(g, b, "nn"), _dot3(g, a, "tn")
    return _dot3(b, g, "nt"), _dot3(a, g, "nn")


_dot3.defvjp(_dot3_fwd, _dot3_bwd)


def _scan_cumsum(inc, x):
    return _cumsum_vjp(inc.astype(BF16), x)


def _cumsum_raw(inc, x, form):
    h1 = x.astype(BF16)
    r1 = x - h1.astype(F32)
    h2 = r1.astype(BF16)
    h3 = (r1 - h2.astype(F32)).astype(BF16)
    d = lambda y: lax.dot_general(inc, y, _FORMS[form], preferred_element_type=F32)
    return d(h1) + (d(h2) + d(h3))


@jax.custom_vjp
def _cumsum_vjp(inc, x):
    return _cumsum_raw(inc, x, "nn")


_cumsum_vjp.defvjp(lambda inc, x: (_cumsum_raw(inc, x, "nn"), inc),
                   lambda inc, g: (jnp.zeros_like(inc), _cumsum_raw(inc, g, "tn")))


def _silu(x):
    return x * jax.nn.sigmoid(x)


def _softplus(x):
    return jnp.maximum(x, 0.0) + jnp.log(1.0 + jnp.exp(-jnp.abs(x)))


def _iota2(shape, dim):
    return lax.broadcasted_iota(jnp.int32, shape, dim)


def _pair_mask():
    return (_iota2((128, 128), 0) < 64) == (_iota2((128, 128), 1) < 64)


def _seg64_raw(x):
    e = _pair_mask().astype(BF16)
    parts = []
    for g in range(x.shape[1] // 128):
        hi, lo = _split(x[:, g * 128:(g + 1) * 128])
        parts.append(jnp.dot(hi, e, preferred_element_type=F32) + jnp.dot(lo, e, preferred_element_type=F32))
    return parts[0] if len(parts) == 1 else jnp.concatenate(parts, axis=1)


@jax.custom_vjp
def _seg64_sum(x):
    return _seg64_raw(x)


_seg64_sum.defvjp(lambda x: (_seg64_raw(x), None), lambda _, g: (_seg64_raw(g),))


def _mm1(a, b):
    return jnp.dot(a.astype(BF16), b.astype(BF16), preferred_element_type=F32)


def _mm1_nt(a, b):
    return lax.dot_general(a.astype(BF16), b.astype(BF16), (((1,), (1,)), ((), ())), preferred_element_type=F32)


def _seg128_mean(x):
    parts = [jnp.broadcast_to(jnp.mean(x[:, g * 128:(g + 1) * 128], axis=1, keepdims=True), (x.shape[0], 128))
             for g in range(x.shape[1] // 128)]
    return parts[0] if len(parts) == 1 else jnp.concatenate(parts, axis=1)


def _scan_masks(d):
    lag = (_iota2((C, C), 0) - _iota2((C, C), 1)) * (1 - 2 * d)
    return lag >= 0, lag > 0


def _tri_inverse(a):
    eye = (_iota2((C, C), 0) == _iota2((C, C), 1)).astype(F32)
    p = [-x for x in a]
    tm = [eye + x for x in p]
    n = 2
    while n < C:
        p = [_mm(x, x) for x in p]
        tm = [t + _mm(t, x) for t, x in zip(tm, p)]
        n *= 2
    return tm


@jax.custom_vjp
def _tri_solve(tm, a, rhs):
    return [_mm(t, r) for t, r in zip(tm, rhs)]


def _tri_solve_fwd(tm, a, rhs):
    u = [_mm(t, r) for t, r in zip(tm, rhs)]
    return u, (tm, u)


def _tri_solve_bwd(res, g):
    tm, u = res
    d_rhs = [_mm_tn(t, x) for t, x in zip(tm, g)]
    return [jnp.zeros_like(t) for t in tm], [-_mm_nt(d, x) for d, x in zip(d_rhs, u)], d_rhs


_tri_solve.defvjp(_tri_solve_fwd, _tri_solve_bwd)


def _rw_chunks(chains, tms=None):
    s0, r, lw, kap, b, v, kd, ds = zip(*chains)
    ids = range(len(chains))
    inc, strict = zip(*[_scan_masks(d) for d in ds])
    lane = _iota2((1, 128), 1)
    hm = ((lane < 64).astype(F32), (lane >= 64).astype(F32))
    lc = [_scan_cumsum(inc[i], lw[i]) for i in ids]
    ltot = [jnp.sum(lw[i], axis=0, keepdims=True) for i in ids]
    rt = [r[i] * jnp.exp(lc[i]) for i in ids]
    kt = [kap[i] * jnp.exp(lc[i] - lw[i]) for i in ids]
    einv = [jnp.exp(-lc[i]) for i in ids]
    kh = [kd[i] * einv[i] for i in ids]
    bh = [b[i] * einv[i] for i in ids]
    eend = [jnp.exp(ltot[i] - lc[i]) for i in ids]
    kbar = [kd[i] * eend[i] for i in ids]
    bbar = [b[i] * eend[i] for i in ids]
    lhs = [jnp.concatenate([kt[i] * hm[0], kt[i] * hm[1], rt[i] * hm[0], rt[i] * hm[1]], axis=0) for i in ids]
    gk = [_mm_nt(lhs[i], kh[i]) for i in ids]
    gb = [_mm_nt(lhs[i], bh[i]) for i in ids]
    ks = [_mm_nt(kt[i], s0[i]) for i in ids]
    ys = [_mm_nt(rt[i], s0[i]) for i in ids]
    heads = [(i, h) for i in ids for h in range(2)]
    akk = {c: jnp.where(strict[c[0]], gk[c[0]][c[1] * C:(c[1] + 1) * C], 0.0) for c in heads}
    akb = {c: jnp.where(strict[c[0]], gb[c[0]][c[1] * C:(c[1] + 1) * C], 0.0) for c in heads}
    ark = {c: jnp.where(inc[c[0]], gk[c[0]][(2 + c[1]) * C:(3 + c[1]) * C], 0.0) for c in heads}
    arb = {c: jnp.where(inc[c[0]], gb[c[0]][(2 + c[1]) * C:(3 + c[1]) * C], 0.0) for c in heads}
    vh = {c: v[c[0]] * hm[c[1]] for c in heads}
    av = {c: _mm(akk[c], vh[c]) for c in heads}
    rhs = {c: ks[c[0]] * hm[c[1]] + av[c] for c in heads}
    akb_list = [akb[c] for c in heads]
    tm = _tri_inverse(akb_list) if tms is None else [tms[c] for c in heads]
    uh = dict(zip(heads, _tri_solve(tm, akb_list, [rhs[c] for c in heads])))
    yv = {c: _mm(ark[c], vh[c]) for c in heads}
    yu = {c: _mm(arb[c], uh[c]) for c in heads}
    u = [uh[i, 0] + uh[i, 1] for i in ids]
    y = [ys[i] + (yv[i, 0] - yu[i, 0]) + (yv[i, 1] - yu[i, 1]) for i in ids]
    upd = [_mm_tn(jnp.concatenate([v[i], -u[i]], axis=0), jnp.concatenate([kbar[i], bbar[i]], axis=0)) for i in ids]
    s1 = [s0[i] * jnp.exp(ltot[i]) + jnp.where(_pair_mask(), upd[i], 0.0) for i in ids]
    if tms is None:
        return [(y[i], s1[i], tm[2 * i], tm[2 * i + 1]) for i in ids]
    return [(y[i], s1[i]) for i in ids]


HG_SUB = 16


def _hg_chunks(chains):
    s0, qr, iv, f, lb, ds = zip(*chains)
    ids = range(len(chains))
    inc = [_scan_masks(d)[0] for d in ds]
    q = [_silu(qr[i]) for i in ids]
    fg = [lb[i] + (1.0 - lb[i]) * jax.nn.sigmoid(f[i]) for i in ids]
    k = [1.0 - fg[i] for i in ids]
    g = [jnp.log(fg[i]) for i in ids]
    bc = [_scan_cumsum(inc[i], g[i]) for i in ids]
    btot = [jnp.sum(g[i], axis=0, keepdims=True) for i in ids]
    o_inter = [_mm_nt(q[i] * jnp.exp(bc[i]), s0[i]) for i in ids]
    rowi = _iota2((C, 1), 0)
    outs = [[] for _ in ids]
    for blk in range(C // HG_SUB):
        lo, hi = blk * HG_SUB, (blk + 1) * HG_SUB
        first = [lo if ds[i] == 0 else hi - 1 for i in ids]
        ref = [jnp.sum(jnp.where(rowi == first[i], bc[i] - g[i], 0.0), axis=0, keepdims=True) for i in ids]
        qi = [q[i][lo:hi] * jnp.exp(bc[i][lo:hi] - ref[i]) for i in ids]
        src = [(rowi < hi) if ds[i] == 0 else (rowi >= lo) for i in ids]
        ke = [jnp.where(src[i], k[i] * jnp.exp(jnp.where(src[i], ref[i] - bc[i], 0.0)), 0.0) for i in ids]
        a = [jnp.where(inc[i][lo:hi], _mm_nt(qi[i], ke[i]), 0.0) for i in ids]
        part = [_mm(a[i], iv[i]) for i in ids]
        for i in ids:
            outs[i].append(part[i])
    o = [o_inter[i] + jnp.concatenate(outs[i], axis=0) for i in ids]
    upd = [_mm_tn(iv[i], k[i] * jnp.exp(btot[i] - bc[i])) for i in ids]
    s1 = [s0[i] * jnp.exp(btot[i]) + upd[i] for i in ids]
    return [(o[i], s1[i]) for i in ids]


def _lockstep(fn, ds, nargs):
    def flat_fn(*flat):
        return tuple(fn([tuple(flat[i * nargs:(i + 1) * nargs]) + (d,) for i, d in enumerate(ds)]))

    return flat_fn


def _h_math(x, ng, scale, shift):
    return x * lax.rsqrt(jnp.mean(x * x, axis=-1, keepdims=True) + NORM_EPS) * ng * (1.0 + scale) + shift


def _hg_post_math(o0, o1, z, g):
    o = o0 + o1
    on = o * lax.rsqrt(_seg128_mean(o * o) + NORM_EPS)
    return on * g * _silu(z)


def _rw_prep_math(k, pw0, pw1, pa0, pa1, w0, a0, kkp, kap_p):
    kk = k * kkp
    kap = kk * lax.rsqrt(_seg64_sum(kk * kk) + 1e-12)
    outs = []
    for d, (pw, pa) in enumerate(((pw0, pa0), (pw1, pa1))):
        w_log = -_softplus(-(w0[d:d + 1] + pw)) - 0.5
        lw = -jnp.exp(w_log)
        a = jax.nn.sigmoid(a0[d:d + 1] + pa)
        kd = k * (1.0 + (a - 1.0) * kap_p)
        outs += [lw, kap * a, kd]
    return (kap, *outs)


def _rw_post_math(y0, y1, kd0, kd1, r, v, z, rk, gng, gnb):
    ys, ksum = y0 + y1, kd0 + kd1
    mean = _seg64_sum(ys) * (1.0 / 64.0)
    cen = ys - mean
    var = _seg64_sum(cen * cen) * (1.0 / 64.0)
    yn = cen * lax.rsqrt(var + RW_GN_EPS) * gng + gnb
    bonus = _seg64_sum(r * ksum * rk) * v
    return (yn + bonus) * _silu(z)


def _head_math(x, mo, gate, fg, tgt):
    xo = x + gate * mo
    y = xo * lax.rsqrt(jnp.mean(xo * xo, axis=-1, keepdims=True) + NORM_EPS) * fg
    err = y - tgt
    return 0.5 * jnp.sum(jnp.mean(err * err, axis=-1, keepdims=True), axis=0, keepdims=True)


def _gate_math(ghg, grw, a, b):
    return jax.nn.sigmoid(ghg) * a + jax.nn.sigmoid(grw) * b


def _call(body, name, grid, in_specs, out_specs, out_shape, scratch=(), sem=None, aliases=None):
    return pl.pallas_call(
        body, name=name, grid=grid, in_specs=in_specs, out_specs=out_specs, out_shape=out_shape,
        scratch_shapes=list(scratch), input_output_aliases=aliases or {},
        compiler_params=pltpu.CompilerParams(dimension_semantics=sem, vmem_limit_bytes=V7X_VMEM_LIMIT))


def _full(shape):
    nd = len(shape)
    return pl.BlockSpec(shape, lambda *_: (0,) * nd)


def _sds(shape, dtype=F32):
    return jax.ShapeDtypeStruct(shape, dtype)


def _proj_nt(h, wt, wt_blk0, nblk, row0, nrows, name):
    def body(h_ref, w_ref, o_ref):
        o_ref[...] = lax.dot_general(h_ref[row0:row0 + nrows, :], w_ref[...], (((1,), (1,)), ((), ())), preferred_element_type=F32)

    return _call(body, name, (nblk,), [_full(h.shape), pl.BlockSpec((WB, wt.shape[1]), lambda j: (wt_blk0 + j, 0))],
                 pl.BlockSpec((nrows, WB), lambda j: (0, j)), _sds((nrows, nblk * WB)), sem=("parallel",))(h, wt)


def _dw_tn(a, b, row0, out_rows, out_blk0, name, prev=None):
    m, n = a.shape
    k2 = b.shape[1]
    nblk = n // WB

    def body(*refs):
        a_ref, b_ref, o_ref = refs[0], refs[1], refs[-1]
        o_ref[...] = lax.dot_general(a_ref[...], b_ref[row0:row0 + m, :], (((0,), (0,)), ((), ())),
                                     preferred_element_type=F32).astype(BF16)

    in_specs = [pl.BlockSpec((m, WB), lambda j: (0, j)), _full(b.shape)]
    args = [a, b]
    aliases = None
    if prev is not None:
        in_specs.append(pl.BlockSpec(memory_space=pl.ANY))
        args.append(prev)
        aliases = {2: 0}
    return _call(body, name, (nblk,), in_specs, pl.BlockSpec((WB, k2), lambda j: (out_blk0 + j, 0)), _sds((out_rows, k2), BF16),
                 sem=("arbitrary",), aliases=aliases)(*args)


def _mm_nn(a, w, w_blk0, name, after=None):
    m, kc = a.shape
    n = w.shape[1]
    kb = max(k for k in (WB, 2 * WB, 4 * WB) if kc % k == 0 and (w_blk0 * WB) % k == 0)
    w_blk0 = w_blk0 * WB // kb
    tmb = 3 * TM if m % (3 * TM) == 0 else (2 * TM if m % (2 * TM) == 0 else TM)

    def body(a_ref, w_ref, *rest):
        o_ref = rest[-1]

        @pl.when(pl.program_id(1) == 0)
        def _():
            o_ref[...] = jnp.zeros_like(o_ref)

        o_ref[...] += jnp.dot(a_ref[...], w_ref[...], preferred_element_type=F32)

    extra = [] if after is None else [pl.BlockSpec(memory_space=pl.ANY)]
    return _call(body, name, (m // tmb, kc // kb),
                 [pl.BlockSpec((tmb, kb), lambda i, k: (i, k)), pl.BlockSpec((kb, n), lambda i, k: (w_blk0 + k, 0))] + extra,
                 pl.BlockSpec((tmb, n), lambda i, k: (i, 0)), _sds((m, n)), sem=("parallel", "arbitrary"))(
        a, w, *([] if after is None else [after]))


def _ada_fwd(cc16, ada_shard):
    ncol = ada_shard.shape[1]

    def body(c_ref, w_ref, o_ref):
        o_ref[...] = _mm(_silu(c_ref[...]), w_ref[...])

    return _call(body, "ada_fwd", (1,), [_full((16, D)), _full((D, ncol))], _full((16, ncol)), _sds((16, ncol)))(cc16, ada_shard)


def _ada_bwd(cc16, ada_shard, dm_x, dm_c, dmods):
    ncol = ada_shard.shape[1]

    def body(c_ref, w_ref, dx_ref, dc_ref, dm_ref, dw_ref, dcc_ref, db_ref):
        s, vjp = jax.vjp(_silu, c_ref[...])
        dc_tot = dc_ref[0:1, :]
        for j in range(1, NDEV):
            dc_tot = dc_tot + dc_ref[j:j + 1, :]
        row = _iota2((8, 1), 0)
        dm16 = jnp.concatenate([dx_ref[...], jnp.where(row == 0, dc_tot, 0.0)], axis=0)
        dw_ref[...] = _mm_tn(s, dm16)
        dcc_ref[...] = vjp(_mm_nt(dm16, w_ref[...]))[0]
        db_ref[...] = dm_ref[0:1, :] + dm_ref[1:2, :]

    return _call(body, "ada_bwd", (1,), [_full((16, D)), _full((D, ncol)), _full((8, ncol)), _full((8, ncol)), _full((8, 3 * D))],
                 [_full((D, ncol)), _full((16, D)), _full((1, 3 * D))], [_sds((D, ncol)), _sds((16, D)), _sds((1, 3 * D))])(
        cc16, ada_shard, dm_x, dm_c, dmods)


def _sel_mod(mods_ref, is_ctx, lo):
    return jnp.where(is_ctx, mods_ref[1:2, lo:lo + D], mods_ref[0:1, lo:lo + D])


def _token_specs():
    nct = _dims()["NCT"]
    return (pl.BlockSpec((TM, D), lambda i: (jnp.minimum(i, nct - 1), 0)),
            pl.BlockSpec((TM, D), lambda i: (jnp.maximum(i - nct, 0), 0)))


def _h_fwd(ctx, x, mods, norm_g):
    dm = _dims()

    def body(c_ref, x_ref, m_ref, g_ref, o_ref):
        is_ctx = pl.program_id(0) < dm["NCT"]
        tok = jnp.where(is_ctx, c_ref[...], x_ref[...])
        o_ref[...] = _h_math(tok, g_ref[...], _sel_mod(m_ref, is_ctx, D), _sel_mod(m_ref, is_ctx, 0)).astype(BF16)

    return _call(body, "h_fwd", (dm["TT"] // TM,), [*_token_specs(), _full((8, 3 * D)), _full((1, D))],
                 pl.BlockSpec((TM, D), lambda i: (i, 0)), _sds((dm["TT"], D), BF16), sem=("parallel",))(ctx, x, mods, norm_g)


def _chunk_of(d, s):
    dm = _dims()
    ncc, nch = dm["NCC"], dm["NCH"]
    return s if d == 0 else jnp.where(s < ncc, ncc - 1 - s, nch - 1 - (s - ncc))


HG_GROUP = 8
RW_GROUP = 8


def _dir_specs(make):
    return [make(d) for d in range(2)]


def _lanes(j):
    return slice(j * 128, (j + 1) * 128)


def _hg_scan_fwd(p_hg, lb3):
    dm = _dims()
    nh, nch, tt = dm["HG_H"], dm["NCH"], dm["TT"]
    pp = min(HG_GROUP, nh)
    lw_ = 128 * pp

    def body(q0, q1, i0, i1, f0, f1, lb0, lb1, o0, o1, ck_ref, s_ref):
        @pl.when(pl.program_id(1) == 0)
        def _():
            s_ref[...] = jnp.zeros_like(s_ref)

        keys, chains = [], []
        for d, refs in enumerate(((q0, i0, f0, lb0), (q1, i1, f1, lb1))):
            vals = [r[...] for r in refs]
            for j in range(pp):
                keys.append((d, j))
                chains.append((s_ref[d, j], *[a[:, _lanes(j)] for a in vals], d))
        for (d, j), chain, (out, s1) in zip(keys, chains, _hg_chunks(chains)):
            ck_ref[d, j] = chain[0]
            (o0, o1)[d][:, _lanes(j)] = out
            s_ref[d, j] = s1

    blk = lambda off: _dir_specs(lambda d: pl.BlockSpec((C, lw_), lambda h, s: (_chunk_of(d, s), off * nh // pp + h)))
    return _call(body, "hg_scan_fwd", (nh // pp, nch),
                 blk(0) + blk(1) + _dir_specs(lambda d: pl.BlockSpec((C, lw_), lambda h, s: (_chunk_of(d, s), (2 + d) * nh // pp + h)))
                 + _dir_specs(lambda d: pl.BlockSpec((None, 1, lw_), lambda h, s: (d, 0, h))),
                 _dir_specs(lambda d: pl.BlockSpec((C, lw_), lambda h, s: (_chunk_of(d, s), h)))
                 + [pl.BlockSpec((2, pp, None, 128, 128), lambda h, s: (0, h, s, 0, 0))],
                 [_sds((tt, HGW)), _sds((tt, HGW)), _sds((2, nh, nch, 128, 128))],
                 scratch=[pltpu.VMEM((2, pp, 128, 128), F32)], sem=("parallel", "arbitrary"))(*([p_hg] * 6), lb3, lb3)


def _hg_scan_bwd(p_hg, lb3, ck, do):
    dm = _dims()
    nh, nch, tt = dm["HG_H"], dm["NCH"], dm["TT"]
    pp = min(HG_GROUP, nh)
    lw_ = 128 * pp

    def body(q0, q1, i0, i1, f0, f1, lb0, lb1, ck_ref, do0, do1, dq0, dq1, di0, di1, df0, df1, dlb_ref, ds_ref):
        @pl.when(pl.program_id(1) == 0)
        def _():
            ds_ref[...] = jnp.zeros_like(ds_ref)
            dlb_ref[...] = jnp.zeros_like(dlb_ref)

        keys, flat, cts = [], [], []
        for d, refs in enumerate(((q0, i0, f0, lb0, do0), (q1, i1, f1, lb1, do1))):
            *vals, do_ = [r[...] for r in refs]
            for j in range(pp):
                keys.append((d, j))
                flat += [ck_ref[d, j], *[a[:, _lanes(j)] for a in vals]]
                cts.append((do_[:, _lanes(j)], ds_ref[d, j]))
        _, vjp = jax.vjp(_lockstep(_hg_chunks, [d for d, _ in keys], 5), *flat)
        grads = vjp(tuple(cts))
        for n, (d, j) in enumerate(keys):
            ds0, dq_, di_, df_, dlb = grads[5 * n:5 * n + 5]
            (dq0, dq1)[d][:, _lanes(j)] = dq_
            (di0, di1)[d][:, _lanes(j)] = di_
            (df0, df1)[d][:, _lanes(j)] = df_
            dlb_ref[d, :, _lanes(j)] += dlb
            ds_ref[d, j] = ds0

    ch = lambda d, s: _chunk_of(d, nch - 1 - s)
    blk = lambda off: _dir_specs(lambda d: pl.BlockSpec((C, lw_), lambda h, s: (ch(d, s), off * nh // pp + h)))
    tok = _dir_specs(lambda d: pl.BlockSpec((C, lw_), lambda h, s: (ch(d, s), h)))
    return _call(body, "hg_scan_bwd", (nh // pp, nch),
                 blk(0) + blk(1) + _dir_specs(lambda d: pl.BlockSpec((C, lw_), lambda h, s: (ch(d, s), (2 + d) * nh // pp + h)))
                 + _dir_specs(lambda d: pl.BlockSpec((None, 1, lw_), lambda h, s: (d, 0, h)))
                 + [pl.BlockSpec((2, pp, None, 128, 128), lambda h, s: (0, h, nch - 1 - s, 0, 0))] + tok,
                 tok * 3 + [pl.BlockSpec((2, 1, lw_), lambda h, s: (0, 0, h))],
                 [_sds((tt, HGW))] * 6 + [_sds((2, 1, HGW))],
                 scratch=[pltpu.VMEM((2, pp, 128, 128), F32)], sem=("parallel", "arbitrary"))(*([p_hg] * 6), lb3, lb3, ck, do, do)


def _rw_scan_fwd(sh, kap, lw, b, kd):
    dm = _dims()
    npair, nch, tt = dm["RW_P"], dm["NCH"], dm["TT"]
    pp = min(RW_GROUP, npair)
    lw_ = 128 * pp

    def body(r0, r1, v0, v1, ka0, ka1, lw0, lw1, b0, b1, kd0, kd1, y0, y1, ck_ref, tm_ref, s_ref):
        @pl.when(pl.program_id(1) == 0)
        def _():
            s_ref[...] = jnp.zeros_like(s_ref)

        keys, chains = [], []
        for d, refs in enumerate(((r0, lw0, ka0, b0, v0, kd0), (r1, lw1, ka1, b1, v1, kd1))):
            vals = [r[...] for r in refs]
            for j in range(pp):
                keys.append((d, j))
                chains.append((s_ref[d, j], *[a[:, _lanes(j)] for a in vals], d))
        for (d, j), chain, (out, s1, tm_a, tm_b) in zip(keys, chains, _rw_chunks(chains)):
            ck_ref[d, j] = chain[0]
            tm_ref[d, j, 0] = tm_a
            tm_ref[d, j, 1] = tm_b
            (y0, y1)[d][:, _lanes(j)] = out
            s_ref[d, j] = s1

    two = lambda off: _dir_specs(lambda d: pl.BlockSpec((C, lw_), lambda p, s: (_chunk_of(d, s), off * npair // pp + p)))
    three = _dir_specs(lambda d: pl.BlockSpec((None, C, lw_), lambda p, s: (d, _chunk_of(d, s), p)))
    return _call(body, "rw_scan_fwd", (npair // pp, nch), two(0) + two(2) + two(0) + three * 3,
                 two(0) + [pl.BlockSpec((2, pp, None, 128, 128), lambda p, s: (0, p, s, 0, 0)),
                           pl.BlockSpec((2, pp, None, 2, C, C), lambda p, s: (0, p, s, 0, 0, 0))],
                 [_sds((tt, RWW)), _sds((tt, RWW)), _sds((2, npair, nch, 128, 128)), _sds((2, npair, nch, 2, C, C))],
                 scratch=[pltpu.VMEM((2, pp, 128, 128), F32)], sem=("parallel", "arbitrary"))(
        sh, sh, sh, sh, kap, kap, lw, lw, b, b, kd, kd)


def _rw_scan_bwd(sh, kap, lw, b, kd, ck, tm_ck, dy):
    dm = _dims()
    npair, nch, tt = dm["RW_P"], dm["NCH"], dm["TT"]
    pp = min(RW_GROUP, npair)
    lw_ = 128 * pp

    def body(r0, r1, v0, v1, ka0, ka1, lw0, lw1, b0, b1, kd0, kd1, ck_ref, tm_ref, dy0, dy1,
             dr0, dr1, dv0, dv1, dka0, dka1, dlw0, dlw1, db0, db1, dkd0, dkd1, ds_ref):
        @pl.when(pl.program_id(1) == 0)
        def _():
            ds_ref[...] = jnp.zeros_like(ds_ref)

        keys, flat, cts, tms = [], [], [], {}
        for d, refs in enumerate(((r0, lw0, ka0, b0, v0, kd0, dy0), (r1, lw1, ka1, b1, v1, kd1, dy1))):
            *vals, dy_ = [r[...] for r in refs]
            for j in range(pp):
                tms[len(keys), 0], tms[len(keys), 1] = tm_ref[d, j, 0], tm_ref[d, j, 1]
                keys.append((d, j))
                flat += [ck_ref[d, j], *[a[:, _lanes(j)] for a in vals]]
                cts.append((dy_[:, _lanes(j)], ds_ref[d, j]))
        _, vjp = jax.vjp(_lockstep(functools.partial(_rw_chunks, tms=tms), [d for d, _ in keys], 7), *flat)
        grads = vjp(tuple(cts))
        outs = ((dr0, dlw0, dka0, db0, dv0, dkd0), (dr1, dlw1, dka1, db1, dv1, dkd1))
        for n, (d, j) in enumerate(keys):
            ds0, *g_in = grads[7 * n:7 * n + 7]
            for o_ref, g in zip(outs[d], g_in):
                o_ref[:, _lanes(j)] = g
            ds_ref[d, j] = ds0

    ch = lambda d, s: _chunk_of(d, nch - 1 - s)
    two = lambda off: _dir_specs(lambda d: pl.BlockSpec((C, lw_), lambda p, s: (ch(d, s), off * npair // pp + p)))
    three = _dir_specs(lambda d: pl.BlockSpec((None, C, lw_), lambda p, s: (d, ch(d, s), p)))
    return _call(body, "rw_scan_bwd", (npair // pp, nch),
                 two(0) + two(2) + two(0) + three * 3
                 + [pl.BlockSpec((2, pp, None, 128, 128), lambda p, s: (0, p, nch - 1 - s, 0, 0)),
                    pl.BlockSpec((2, pp, None, 2, C, C), lambda p, s: (0, p, nch - 1 - s, 0, 0, 0))] + two(0),
                 two(0) * 6, [_sds((tt, RWW))] * 12,
                 scratch=[pltpu.VMEM((2, pp, 128, 128), F32)], sem=("parallel", "arbitrary"))(
        sh, sh, sh, sh, kap, kap, lw, lw, b, b, kd, kd, ck, tm_ck, dy, dy)


def _tile_pos(i):
    dm = _dims()
    is_ctx = i < dm["NCT"]
    rows = lax.broadcasted_iota(jnp.int32, (TM, 1), 0)
    tl = rows + (i - dm["NCT"]) * TM
    width = jnp.where(is_ctx, TC, GW)
    assert TC & (TC - 1) == 0 and GW & (GW - 1) == 0 and TM % GW == 0
    colp = rows & (width - 1)
    return is_ctx, rows, tl, colp, width


def _shift_lr(cur, i):
    _, _, _, colp, width = _tile_pos(i)
    left = jnp.where(colp == 0, 0.0, pltpu.roll(cur, 1, 0))
    right = jnp.where(colp == width - 1, 0.0, pltpu.roll(cur, TM - 1, 0))
    return left, right


def _shift_ud(cur, prv, nxt, i):
    is_ctx, rows, tl, _, _ = _tile_pos(i)
    if TM > GW:
        up = jnp.where(rows >= GW, pltpu.roll(cur, GW, 0), pltpu.roll(prv, GW, 0))
        down = jnp.where(rows < TM - GW, pltpu.roll(cur, TM - GW, 0), pltpu.roll(nxt, TM - GW, 0))
    else:
        up, down = prv, nxt
    up = jnp.where(jnp.logical_or(is_ctx, tl < GW), 0.0, up)
    down = jnp.where(jnp.logical_or(is_ctx, tl >= T - GW), 0.0, down)
    return up, down


def _shift_specs():
    dm = _dims()
    nt = dm["TT"] // TM
    cw = max(w for w in range(128, 1664 + 1, 128) if dm["SH"] % w == 0)
    cur = pl.BlockSpec((TM, cw), lambda j, i: (i, j))
    prv = pl.BlockSpec((TM, cw), lambda j, i: (jnp.maximum(i - 1, 0), j))
    nxt = pl.BlockSpec((TM, cw), lambda j, i: (jnp.minimum(i + 1, nt - 1), j))
    mu = pl.BlockSpec((4, cw), lambda j, i: (0, j))
    return nt, cw, cur, prv, nxt, mu


def _shift_fwd(p_rw, mu):
    dm = _dims()
    nt, cw, cur, prv, nxt, mus = _shift_specs()

    def body(c_ref, p_ref, n_ref, mu_ref, o_ref):
        i = pl.program_id(1)
        p, m = c_ref[...], mu_ref[...]
        left, right = _shift_lr(p, i)
        up, down = _shift_ud(p, p_ref[...], n_ref[...], i)
        vert = jnp.where(i < dm["NCT"], 0.0, 1.0)
        o_ref[...] = (p * (1.0 - m[0:1] - m[1:2] - vert * (m[2:3] + m[3:4]))
                      + m[0:1] * left + m[1:2] * right + m[2:3] * up + m[3:4] * down)

    return _call(body, "shift_fwd", (dm["SH"] // cw, nt), [cur, prv, nxt, mus], cur, _sds((dm["TT"], dm["SH"])),
                 sem=("parallel", "parallel"))(p_rw, p_rw, p_rw, mu)


def _shift_bwd(p_rw, dsh, mu):
    dm = _dims()
    nt, cw, cur, prv, nxt, mus = _shift_specs()

    def body(c_ref, p_ref, n_ref, gc_ref, gp_ref, gn_ref, mu_ref, dp_ref, dmu_ref):
        i = pl.program_id(1)
        p, g, m = c_ref[...], gc_ref[...], mu_ref[...]
        vert = jnp.where(i < dm["NCT"], 0.0, 1.0)
        _, from_right = _shift_lr(m[0:1] * g, i)
        from_left, _ = _shift_lr(m[1:2] * g, i)
        _, from_down = _shift_ud(m[2:3] * g, g, m[2:3] * gn_ref[...], i)
        from_up, _ = _shift_ud(m[3:4] * g, m[3:4] * gp_ref[...], g, i)
        dp = g * (1.0 - m[0:1] - m[1:2] - vert * (m[2:3] + m[3:4])) + from_right + from_left + from_down + from_up
        dp_ref[...] = dp.astype(BF16)

        @pl.when(i == 0)
        def _():
            dmu_ref[...] = jnp.zeros_like(dmu_ref)

        left, right = _shift_lr(p, i)
        up, down = _shift_ud(p, p_ref[...], n_ref[...], i)
        s = lambda a: jnp.sum(a, axis=0, keepdims=True)
        dmu_ref[...] += jnp.concatenate([s(g * (left - p)), s(g * (right - p)), vert * s(g * (up - p)), vert * s(g * (down - p))], axis=0)

    return _call(body, "shift_bwd", (dm["SH"] // cw, nt), [cur, prv, nxt, cur, prv, nxt, mus],
                 [cur, pl.BlockSpec((4, cw), lambda j, i: (0, j))], [_sds((dm["TT"], dm["SH"]), BF16), _sds((4, dm["SH"]))],
                 sem=("parallel", "arbitrary"))(p_rw, p_rw, p_rw, dsh, dsh, dsh, mu)


def _acc_out(ref, first, vals):
    @pl.when(first)
    def _():
        for r in ref:
            r[...] = jnp.zeros_like(r)

    for r, v in zip(ref, vals):
        r[...] += v


def _rowsum(a):
    return jnp.sum(a, axis=0, keepdims=True)


def _rw_prep_specs():
    dm = _dims()
    rw = RWW // 128
    tok = lambda width, cb: pl.BlockSpec((TMV, width), lambda i: (i, cb))
    ins = [pl.BlockSpec((TMV, RWW), lambda i: (i, 1)), tok(128, 3 * rw), tok(128, 3 * rw + 1),
           _full((2, RWW)), _full((128, RWW)), _full((128, RWW)), _full((2, RWW)), _full((128, RWW)), _full((128, RWW)),
           _full((1, RWW)), _full((1, RWW))]
    return dm, ins


def _rw_prep_fwd(sh, w0, w2p0, w2p1, a0, a2p0, a2p1, kkp, kap_p):
    dm, ins = _rw_prep_specs()
    tt = dm["TT"]

    def body(k_ref, low_ref, loa_ref, w0_ref, w20_ref, w21_ref, a0_ref, a20_ref, a21_ref, kk_ref, ka_ref, kap_ref, lw_ref, b_ref, kd_ref):
        tw, la = jnp.tanh(low_ref[...]), loa_ref[...]
        kap, lw0, b0, kd0, lw1, b1, kd1 = _rw_prep_math(
            k_ref[...], _mm1(tw, w20_ref[...]), _mm1(tw, w21_ref[...]), _mm1(la, a20_ref[...]), _mm1(la, a21_ref[...]),
            w0_ref[...], a0_ref[...], kk_ref[...], ka_ref[...])
        kap_ref[...] = kap
        lw_ref[0], lw_ref[1] = lw0, lw1
        b_ref[0], b_ref[1] = b0, b1
        kd_ref[0], kd_ref[1] = kd0, kd1

    two = pl.BlockSpec((2, TMV, RWW), lambda i: (0, i, 0))
    return _call(body, "rw_prep_fwd", (tt // TMV,), ins, [pl.BlockSpec((TMV, RWW), lambda i: (i, 0)), two, two, two],
                 [_sds((tt, RWW)), _sds((2, tt, RWW)), _sds((2, tt, RWW)), _sds((2, tt, RWW))], sem=("parallel",))(
        sh, sh, sh, w0, w2p0, w2p1, a0, a2p0, a2p1, kkp, kap_p)


def _rw_prep_bwd(sh, w0, w2p0, w2p1, a0, a2p0, a2p1, kkp, kap_p, scan_grads, dkd_p, dr_p, dv_p):
    dm, ins = _rw_prep_specs()
    tt, sh_w = dm["TT"], dm["SH"]
    one = pl.BlockSpec((TMV, RWW), lambda i: (i, 0))

    def body(k_ref, low_ref, loa_ref, w0_ref, w20_ref, w21_ref, a0_ref, a20_ref, a21_ref, kk_ref, ka_ref,
             dr0, dr1, dv0, dv1, dka0, dka1, dlw0, dlw1, db0, db1, dkd0, dkd1, dkdp_ref, drp_ref, dvp_ref,
             dsh_ref, dw0_ref, dw20_ref, dw21_ref, da0_ref, da20_ref, da21_ref, dkk_ref, dka_ref):
        tw, la = jnp.tanh(low_ref[...]), loa_ref[...]
        w20, w21, a20, a21 = w20_ref[...], w21_ref[...], a20_ref[...], a21_ref[...]
        _, vjp = jax.vjp(_rw_prep_math, k_ref[...], _mm1(tw, w20), _mm1(tw, w21), _mm1(la, a20), _mm1(la, a21),
                         w0_ref[...], a0_ref[...], kk_ref[...], ka_ref[...])
        dkp = dkdp_ref[...]
        dk, dpw0, dpw1, dpa0, dpa1, dw0, da0, dkk, dka = vjp(
            (dka0[...] + dka1[...], dlw0[...], db0[...], dkd0[...] + dkp, dlw1[...], db1[...], dkd1[...] + dkp))
        twt, lat = tw.T, la.T
        dtw = _mm1_nt(dpw0, w20) + _mm1_nt(dpw1, w21)
        dsh_ref[:, 0:RWW] = dr0[...] + dr1[...] + drp_ref[...]
        dsh_ref[:, RWW:2 * RWW] = dk
        dsh_ref[:, 2 * RWW:3 * RWW] = dv0[...] + dv1[...] + dvp_ref[...]
        dsh_ref[:, 3 * RWW:3 * RWW + 128] = dtw * (1.0 - tw * tw)
        dsh_ref[:, 3 * RWW + 128:3 * RWW + 256] = _mm1_nt(dpa0, a20) + _mm1_nt(dpa1, a21)
        _acc_out((dw0_ref, dw20_ref, dw21_ref, da0_ref, da20_ref, da21_ref, dkk_ref, dka_ref), pl.program_id(0) == 0,
                 (dw0, _mm1(twt, dpw0), _mm1(twt, dpw1), da0, _mm1(lat, dpa0), _mm1(lat, dpa1), dkk, dka))

    par = [_full((2, RWW)), _full((128, RWW)), _full((128, RWW)), _full((2, RWW)), _full((128, RWW)), _full((128, RWW)),
           _full((1, RWW)), _full((1, RWW))]
    return _call(body, "rw_prep_bwd", (tt // TMV,), ins + [one] * 15,
                 [pl.BlockSpec((TMV, sh_w), lambda i: (i, 0))] + par,
                 [_sds((tt, sh_w)), _sds((2, RWW)), _sds((128, RWW)), _sds((128, RWW)), _sds((2, RWW)), _sds((128, RWW)), _sds((128, RWW)),
                  _sds((1, RWW)), _sds((1, RWW))],
                 sem=("arbitrary",))(sh, sh, sh, w0, w2p0, w2p1, a0, a2p0, a2p1, kkp, kap_p, *scan_grads, dkd_p, dr_p, dv_p)


def _rw_post_specs():
    dm = _dims()
    nv = dm["NCTV"]
    rw = RWW // 128
    lat3 = lambda d: pl.BlockSpec((None, TMV, RWW), lambda i: (d, nv + i, 0))
    lat2 = pl.BlockSpec((TMV, RWW), lambda i: (nv + i, 0))
    ins = [lat2, lat2, lat3(0), lat3(1),
           pl.BlockSpec((TMV, RWW), lambda i: (nv + i, 0)), pl.BlockSpec((TMV, RWW), lambda i: (nv + i, 2)),
           pl.BlockSpec((TMV, RWW), lambda i: (i, 0)), _full((1, RWW)), _full((1, RWW)), _full((1, RWW))]
    return dm, nv, ins


def _rw_post_fwd(y0_, y1_, kd, sh, p_z, rk, gng, gnb):
    dm, nv, ins = _rw_post_specs()

    def body(y0, y1, k0, k1, r, v, z, rk_ref, g_ref, b_ref, o_ref):
        o_ref[...] = _rw_post_math(y0[...], y1[...], k0[...], k1[...], r[...], v[...], z[...], rk_ref[...], g_ref[...], b_ref[...]).astype(BF16)

    return _call(body, "rw_post_fwd", (T // TMV,), ins, pl.BlockSpec((TMV, RWW), lambda i: (i, 0)), _sds((T, RWW), BF16),
                 sem=("parallel",))(y0_, y1_, kd, kd, sh, sh, p_z, rk, gng, gnb)


def _rw_post_bwd(y0_, y1_, kd, sh, p_z, rk, gng, gnb, dout):
    dm, nv, _ = _rw_post_specs()
    tt = dm["TT"]
    cl = lambda i: jnp.maximum(i - nv, 0)
    all3 = lambda d: pl.BlockSpec((None, TMV, RWW), lambda i: (d, i, 0))
    all2 = pl.BlockSpec((TMV, RWW), lambda i: (i, 0))
    ins = [all2, all2, all3(0), all3(1),
           pl.BlockSpec((TMV, RWW), lambda i: (i, 0)), pl.BlockSpec((TMV, RWW), lambda i: (i, 2)),
           pl.BlockSpec((TMV, RWW), lambda i: (cl(i), 0)), _full((1, RWW)), _full((1, RWW)), _full((1, RWW)),
           pl.BlockSpec((TMV, RWW), lambda i: (cl(i), 0))]
    tok = pl.BlockSpec((TMV, RWW), lambda i: (i, 0))

    def body(y0, y1, k0, k1, r, v, z, rk_ref, g_ref, b_ref, do_ref, dy_ref, dkd_ref, dr_ref, dv_ref, dz_ref, drk_ref, dg_ref, db_ref):
        i = pl.program_id(0)
        lat = jnp.where(i >= nv, 1.0, 0.0)
        _, vjp = jax.vjp(_rw_post_math, y0[...], y1[...], k0[...], k1[...], r[...], v[...], z[...], rk_ref[...], g_ref[...], b_ref[...])
        dy0, _, dk0, _, dr, dv, dz, drk, dg, db = vjp(do_ref[...] * lat)
        dy_ref[...] = dy0
        dkd_ref[...] = dk0
        dr_ref[...] = dr
        dv_ref[...] = dv
        dz_ref[...] = dz.astype(BF16)
        _acc_out((drk_ref, dg_ref, db_ref), i == 0, (drk, dg, db))

    return _call(body, "rw_post_bwd", (tt // TMV,), ins,
                 [tok, tok, tok, tok, pl.BlockSpec((TMV, RWW), lambda i: (cl(i), 0)), _full((1, RWW)), _full((1, RWW)), _full((1, RWW))],
                 [_sds((tt, RWW))] * 4 + [_sds((T, RWW), BF16), _sds((1, RWW)), _sds((1, RWW)), _sds((1, RWW))],
                 sem=("arbitrary",))(y0_, y1_, kd, kd, sh, sh, p_z, rk, gng, gnb, dout)


def _hg_post_fwd(o0_, o1_, p_hg, hg_g):
    dm = _dims()
    nv = dm["NCTV"]
    lat3 = lambda d: pl.BlockSpec((TMV, HGW), lambda i: (nv + i, 0))

    def body(o0, o1, z, g, out):
        out[...] = _hg_post_math(o0[...], o1[...], z[...], g[...]).astype(BF16)

    return _call(body, "hg_post_fwd", (T // TMV,), [lat3(0), lat3(1), pl.BlockSpec((TMV, HGW), lambda i: (nv + i, 4)), _full((1, HGW))],
                 pl.BlockSpec((TMV, HGW), lambda i: (i, 0)), _sds((T, HGW), BF16), sem=("parallel",))(o0_, o1_, p_hg, hg_g)


def _hg_post_bwd(o0_, o1_, p_hg, hg_g, dout):
    dm = _dims()
    nv, tt = dm["NCTV"], dm["TT"]
    all3 = lambda d: pl.BlockSpec((TMV, HGW), lambda i: (i, 0))
    tok = pl.BlockSpec((TMV, HGW), lambda i: (i, 0))

    def body(o0, o1, z, g, do_in, do_ref, dz_ref, dg_ref):
        i = pl.program_id(0)
        lat = jnp.where(i >= nv, 1.0, 0.0)
        _, vjp = jax.vjp(_hg_post_math, o0[...], o1[...], z[...], g[...])
        d0, _, dz, dg = vjp(do_in[...] * lat)
        do_ref[...] = d0
        dz_ref[...] = dz
        _acc_out((dg_ref,), i == 0, (dg,))

    return _call(body, "hg_post_bwd", (tt // TMV,),
                 [all3(0), all3(1), pl.BlockSpec((TMV, HGW), lambda i: (i, 4)), _full((1, HGW)),
                  pl.BlockSpec((TMV, HGW), lambda i: (jnp.maximum(i - nv, 0), 0))],
                 [tok, tok, _full((1, HGW))], [_sds((tt, HGW)), _sds((tt, HGW)), _sds((1, HGW))], sem=("arbitrary",))(o0_, o1_, p_hg, hg_g, dout)


def _hg_dproj(dq0, dq1, di0, di1, df0, df1, dz):
    dm = _dims()
    tt = dm["TT"]
    tok = pl.BlockSpec((TMV, HGW), lambda i: (i, 0))

    def body(q0, q1, i0, i1, f0, f1, z, o_ref):
        o_ref[:, 0:HGW] = (q0[...] + q1[...]).astype(BF16)
        o_ref[:, HGW:2 * HGW] = (i0[...] + i1[...]).astype(BF16)
        o_ref[:, 2 * HGW:3 * HGW] = f0[...].astype(BF16)
        o_ref[:, 3 * HGW:4 * HGW] = f1[...].astype(BF16)
        o_ref[:, 4 * HGW:5 * HGW] = z[...].astype(BF16)

    return _call(body, "hg_dproj", (tt // TMV,), [tok] * 7,
                 pl.BlockSpec((TMV, 5 * HGW), lambda i: (i, 0)), _sds((tt, 5 * HGW), BF16), sem=("parallel",))(
        dq0, dq1, di0, di1, df0, df1, dz)


def _merge1_fwd(out_hg, out_rw, whg_t, wrw_t, p_g):
    tok = pl.BlockSpec((TM, D), lambda i: (i, 0))

    def body(h_ref, r_ref, wh_ref, wr_ref, g1_ref, g2_ref, a_ref, b_ref, m_ref):
        nt = (((1,), (1,)), ((), ()))
        a = lax.dot_general(h_ref[...], wh_ref[...], nt, preferred_element_type=F32)
        b = lax.dot_general(r_ref[...], wr_ref[...], nt, preferred_element_type=F32)
        a_ref[...] = a
        b_ref[...] = b
        m_ref[...] = _gate_math(g1_ref[...], g2_ref[...], a, b).astype(BF16)

    return _call(body, "merge1_fwd", (T // TM,),
                 [pl.BlockSpec((TM, HGW), lambda i: (i, 0)), pl.BlockSpec((TM, RWW), lambda i: (i, 0)), _full((D, HGW)), _full((D, RWW)),
                  pl.BlockSpec((TM, D), lambda i: (i, 0)), pl.BlockSpec((TM, D), lambda i: (i, 1))],
                 [tok, tok, tok], [_sds((T, D)), _sds((T, D)), _sds((T, D), BF16)], sem=("parallel",))(out_hg, out_rw, whg_t, wrw_t, p_g, p_g)


def _head_fwd_bwd(x, merged, w_out, mods, final_g, tgt):
    tok = pl.BlockSpec((TM, D), lambda i: (i, 0))

    def body(x_ref, m_ref, w_ref, mods_ref, fg_ref, t_ref, dxo_ref, dmo_ref, loss_ref, dgate_ref, dfg_ref):
        mo = jnp.dot(m_ref[...], w_ref[...], preferred_element_type=F32)
        gate = mods_ref[0:1, 2 * D:3 * D]
        loss, vjp = jax.vjp(lambda x_, mo_, g_, fg_: _head_math(x_, mo_, g_, fg_, t_ref[...]), x_ref[...], mo, gate, fg_ref[...])
        dx, dmo, dgate, dfg = vjp(jnp.ones((1, 1), F32))
        dxo_ref[...] = dx
        dmo_ref[...] = dmo.astype(BF16)
        _acc_out((loss_ref, dgate_ref, dfg_ref), pl.program_id(0) == 0, (jnp.broadcast_to(loss, (1, 128)), dgate, dfg))

    return _call(body, "head_fwd_bwd", (T // TM,), [tok, tok, _full((D, D)), _full((8, 3 * D)), _full((1, D)), tok],
                 [tok, tok, _full((1, 128)), _full((1, D)), _full((1, D))],
                 [_sds((T, D)), _sds((T, D), BF16), _sds((1, 128)), _sds((1, D)), _sds((1, D))], sem=("arbitrary",))(
        x, merged, w_out, mods, final_g, tgt)


def _merge_bwd(dmo, w_out, p_g, a, b):
    tok = pl.BlockSpec((TM, D), lambda i: (i, 0))

    def body(dmo_ref, w_ref, g1_ref, g2_ref, a_ref, b_ref, da_ref, db_ref, dg_ref):
        dm_ = lax.dot_general(dmo_ref[...], w_ref[...], (((1,), (1,)), ((), ())), preferred_element_type=F32)
        _, vjp = jax.vjp(_gate_math, g1_ref[...], g2_ref[...], a_ref[...], b_ref[...])
        dg1, dg2, da, db = vjp(dm_)
        da_ref[...] = da.astype(BF16)
        db_ref[...] = db.astype(BF16)
        dg_ref[:, 0:D] = dg1.astype(BF16)
        dg_ref[:, D:2 * D] = dg2.astype(BF16)

    return _call(body, "merge_bwd", (T // TM,),
                 [tok, _full((D, D)), pl.BlockSpec((TM, D), lambda i: (i, 0)), pl.BlockSpec((TM, D), lambda i: (i, 1)), tok, tok],
                 [tok, tok, pl.BlockSpec((TM, 2 * D), lambda i: (i, 0))], [_sds((T, D), BF16), _sds((T, D), BF16), _sds((T, 2 * D), BF16)],
                 sem=("parallel",))(dmo, w_out, p_g, p_g, a, b)


def _h_bwd(ctx, x, mods, norm_g, dh_hg, dh_rw, dh_z, dh_g, dxo, dgate):
    dm = _dims()
    nct, tt = dm["NCT"], dm["TT"]
    tok = pl.BlockSpec((TM, D), lambda i: (i, 0))
    lat = pl.BlockSpec((TM, D), lambda i: (jnp.maximum(i - nct, 0), 0))

    def body(c_ref, x_ref, m_ref, g_ref, d1, d2, d3, d4, dxo_ref, dgate_ref, gx_ref, dng_ref, dmods_ref):
        i = pl.program_id(0)
        is_ctx = i < nct
        latf = jnp.where(is_ctx, 0.0, 1.0)
        dh = d1[...] + d2[...] + latf * (d3[...] + d4[...])
        _, vjp = jax.vjp(_h_math, jnp.where(is_ctx, c_ref[...], x_ref[...]), g_ref[...], _sel_mod(m_ref, is_ctx, D),
                         _sel_mod(m_ref, is_ctx, 0))
        dx, dng, dscale, dshift = vjp(dh)
        gx_ref[...] = dx + dxo_ref[...]

        @pl.when(i == 0)
        def _():
            dng_ref[...] = jnp.zeros_like(dng_ref)
            dmods_ref[...] = jnp.zeros_like(dmods_ref)
            dmods_ref[0:1, 2 * D:3 * D] = dgate_ref[...]

        dng_ref[...] += dng
        row = lax.broadcasted_iota(jnp.int32, (8, 1), 0)
        sel = jnp.where(row == jnp.where(is_ctx, 1, 0), 1.0, 0.0)
        dmods_ref[:, 0:D] += sel * dshift
        dmods_ref[:, D:2 * D] += sel * dscale

    return _call(body, "h_bwd", (tt // TM,),
                 [*_token_specs(), _full((8, 3 * D)), _full((1, D)), tok, tok, lat, lat, lat, _full((1, D))],
                 [lat, _full((1, D)), _full((8, 3 * D))], [_sds((T, D)), _sds((1, D)), _sds((8, 3 * D))], sem=("arbitrary",))(
        ctx, x, mods, norm_g, dh_hg, dh_rw, dh_z, dh_g, dxo, dgate)


def _lb_math(l0, l1):
    return jax.nn.sigmoid(l0 - l1)


def _lb_fwd(l0, l1):
    def body(a_ref, b_ref, o_ref):
        o_ref[...] = _lb_math(a_ref[...], b_ref[...])

    return _call(body, "lb_fwd", (1,), [_full((2, HGW))] * 2, _full((2, HGW)), _sds((2, HGW)))(l0, l1)


def _lb_bwd(l0, l1, dlb):
    def body(a_ref, b_ref, d_ref, da_ref, db_ref):
        _, vjp = jax.vjp(_lb_math, a_ref[...], b_ref[...])
        da_ref[...], db_ref[...] = vjp(d_ref[...])

    return _call(body, "lb_bwd", (1,), [_full((2, HGW))] * 3, [_full((2, HGW))] * 2, [_sds((2, HGW))] * 2)(l0, l1, dlb)


def _local_step(x, ctx, tgt, mods, w, start_exchange):
    dm = _dims()
    tt, sh_w, hgc = dm["TT"], dm["SH"], dm["HGC"]
    nb = lambda cols: cols // WB
    h = _h_fwd(ctx, x, mods, w["norm_g"])
    win_t = w["win_t"]
    p_hg = _proj_nt(h, win_t, 0, nb(hgc), 0, tt, "proj_hg")
    p_rw = _proj_nt(h, win_t, nb(hgc), nb(sh_w), 0, tt, "proj_rw")
    p_z = _proj_nt(h, win_t, nb(hgc + sh_w), nb(RWW), TC, T, "proj_z")
    p_g = _proj_nt(h, win_t, nb(hgc + sh_w + RWW), nb(2 * D), TC, T, "proj_g")

    lb3 = _lb_fwd(w["lb0"], w["lb1"]).reshape(2, 1, HGW)
    o0, o1, hg_ck = _hg_scan_fwd(p_hg, lb3)
    out_hg = _hg_post_fwd(o0, o1, p_hg, w["hg_g"])

    sh = _shift_fwd(p_rw, w["mu"])
    zpad = jnp.zeros((LORA, RWW), F32)
    prep_w = (w["w0"], jnp.concatenate([w["w2"][0], zpad], 0), jnp.concatenate([zpad, w["w2"][1]], 0),
              w["a0"], jnp.concatenate([w["a2"][0], zpad], 0), jnp.concatenate([zpad, w["a2"][1]], 0), w["kk"], w["ka"])
    kap, lw, b, kd = _rw_prep_fwd(sh, *prep_w)
    y0, y1, rw_ck, rw_tm = _rw_scan_fwd(sh, kap, lw, b, kd)
    post_w = (w["rk"], w["gng"], w["gnb"])
    out_rw = _rw_post_fwd(y0, y1, kd, sh, p_z, *post_w)

    late = w["late_weights"](out_rw)
    a, bb, merged = _merge1_fwd(out_hg, out_rw, late["whg_t"], late["wrw_t"], p_g)
    dxo, dmo, loss, dgate, d_fg = _head_fwd_bwd(x, merged, late["wout"], mods, w["final_g"], tgt)
    da, db, dp_g = _merge_bwd(dmo, late["wout"], p_g, a, bb)
    g_wout = _dw_tn(merged, dmo, 0, D, 0, "dw_out")
    g_whg = _dw_tn(da, out_hg, 0, D, 0, "dw_hg")
    g_wrw = _dw_tn(db, out_rw, 0, D, 0, "dw_rw")
    token, early = start_exchange(dict(wout=g_wout, whg_t=g_whg, wrw_t=g_wrw), "early")
    d_out_hg = _mm_nn(da, late["whg_t"], 0, "dx_hg", after=token)
    d_out_rw = _mm_nn(db, late["wrw_t"], 0, "dx_rw")

    do, dz_hg, d_hg_g = _hg_post_bwd(o0, o1, p_hg, w["hg_g"], d_out_hg)
    *hg_grads, dlb3 = _hg_scan_bwd(p_hg, lb3, hg_ck, do)
    dp_hg = _hg_dproj(*hg_grads, dz_hg)
    d_lb0, d_lb1 = _lb_bwd(w["lb0"], w["lb1"], dlb3.reshape(2, HGW))

    dy, dkd_p, dr_p, dv_p, dp_z, d_rk, d_gng, d_gnb = _rw_post_bwd(y0, y1, kd, sh, p_z, *post_w, d_out_rw)
    rw_grads = _rw_scan_bwd(sh, kap, lw, b, kd, rw_ck, rw_tm, dy)
    dsh, d_w0, d_w2p0, d_w2p1, d_a0, d_a2p0, d_a2p1, d_kk, d_ka = _rw_prep_bwd(sh, *prep_w, rw_grads, dkd_p, dr_p, dv_p)
    dp_rw, d_mu = _shift_bwd(p_rw, dsh, w["mu"])

    g_win = _dw_tn(dp_hg, h, 0, dm["NCOLS"], 0, "dw_in_hg")
    g_win = _dw_tn(dp_rw, h, 0, dm["NCOLS"], nb(hgc), "dw_in_rw", prev=g_win)
    g_win = _dw_tn(dp_z, h, TC, dm["NCOLS"], nb(hgc + sh_w), "dw_in_z", prev=g_win)
    g_win = _dw_tn(dp_g, h, TC, dm["NCOLS"], nb(hgc + sh_w + RWW), "dw_in_g", prev=g_win)
    token, late = start_exchange(dict(win_t=g_win), "late")
    in_flight = (early, late)
    dh_hg = _mm_nn(dp_hg, win_t, 0, "dh_hg", after=token)
    dh_rw = _mm_nn(dp_rw, win_t, nb(hgc), "dh_rw")
    dh_z = _mm_nn(dp_z, win_t, nb(hgc + sh_w), "dh_z")
    dh_g = _mm_nn(dp_g, win_t, nb(hgc + sh_w + RWW), "dh_g")
    gx, d_ng, dmods = _h_bwd(ctx, x, mods, w["norm_g"], dh_hg, dh_rw, dh_z, dh_g, dxo, dgate)

    small = dict(norm_g=d_ng, lb0=d_lb0, lb1=d_lb1, hg_g=d_hg_g, mu=d_mu, w0=d_w0,
                 w2=jnp.stack([d_w2p0[:LORA], d_w2p1[LORA:]]), a0=d_a0, a2=jnp.stack([d_a2p0[:LORA], d_a2p1[LORA:]]),
                 kk=d_kk, ka=d_ka, rk=d_rk, gng=d_gng, gnb=d_gnb, final_g=d_fg)
    return loss, gx, dmods, in_flight, small


MESH = pl.DeviceIdType.MESH


def _comm_call(body, name, bufs, out_shapes, nsem, nloc):
    hbm = pl.BlockSpec(memory_space=pl.ANY)
    return pl.pallas_call(
        body, name=name, in_specs=[hbm] * len(bufs), out_specs=[hbm] * len(out_shapes), out_shape=out_shapes,
        scratch_shapes=[pltpu.SemaphoreType.DMA((nsem,)), pltpu.SemaphoreType.DMA((nsem,)), pltpu.SemaphoreType.DMA((nloc,))],
    )(*bufs)


GATHER_PIECE_BYTES = 4 * 1024 * 1024


def _gather2(bufs, name):
    pieces = []
    for bi, b in enumerate(bufs):
        rows = b.shape[0]
        if rows % 32 == 0 and b.size * b.dtype.itemsize > GATHER_PIECE_BYTES:
            pieces += [(bi, 0, rows // 2), (bi, rows // 2, rows // 2)]
        else:
            pieces.append((bi, 0, rows))
    nbuf = len(bufs)

    def body(*refs):
        ins, outs = refs[:nbuf], refs[nbuf:2 * nbuf]
        send_sems, recv_sems, _ = refs[2 * nbuf:]
        x, y, c = lax.axis_index("x"), lax.axis_index("y"), lax.axis_index("c")
        me, sib = (x, y, c), (x, y, 1 - c)
        flip = lambda a, b: a + b - 2 * a * b
        xn, yn, diag = (1 - x, y), (x, 1 - y), (1 - x, 1 - y)
        relay_from, relay_to = (flip(x, c), flip(1 - y, c)), (flip(1 - x, c), flip(y, c))

        def copy(pi, k, block, to, own=False):
            bi, r0, nr = pieces[pi]
            rows = outs[bi].at[4 * block[0] + 2 * block[1] + block[2], pl.ds(r0, nr)]
            return pltpu.make_async_remote_copy(src_ref=ins[bi].at[pl.ds(r0, nr)] if own else rows, dst_ref=rows,
                                                send_sem=send_sems.at[7 * pi + k], recv_sem=recv_sems.at[7 * pi + k],
                                                device_id=to, device_id_type=MESH)

        started = []
        for pi in range(len(pieces)):
            started += [copy(pi, 0, me, sib, own=True), copy(pi, 1, me, (*xn, c), own=True), copy(pi, 2, me, (*yn, c), own=True)]
        for cp in started:
            cp.start()
        for pi in range(len(pieces)):
            for k, chip in ((1, xn), (2, yn)):
                copy(pi, k, (*chip, c), me).wait_recv()
                started.append(copy(pi, 3 + k, (*chip, c), sib))
                started[-1].start()
            started.append(copy(pi, 3, (*relay_from, c), (*relay_to, c)))
            started[-1].start()
        for pi in range(len(pieces)):
            copy(pi, 3, (*diag, c), me).wait_recv()
            started.append(copy(pi, 6, (*diag, c), sib))
            started[-1].start()
        for pi in range(len(pieces)):
            copy(pi, 0, sib, me).wait_recv()
            for k, chip in ((4, xn), (5, yn), (6, diag)):
                copy(pi, k, (*chip, 1 - c), me).wait_recv()
        for cp in started:
            cp.wait_send()

    return _comm_call(body, name, bufs, [_sds((NDEV,) + b.shape, b.dtype) for b in bufs], 7 * len(pieces), 1)


def _gather_all(bufs, name):
    me = 4 * lax.axis_index("x") + 2 * lax.axis_index("y") + lax.axis_index("c")
    return [lax.dynamic_update_index_in_dim(g, b, me, axis=0) for g, b in zip(_gather2(bufs, name), bufs)]


def _peer_copies(srcs, lands, send_sems, recv_sems):
    x, y, c = lax.axis_index("x"), lax.axis_index("y"), lax.axis_index("c")
    me = 4 * x + 2 * y + c
    pairs = []
    for bi in range(len(srcs)):
        for k in range(1, NDEV):
            px, py, pc = (1 - x if k & 4 else x), (1 - y if k & 2 else y), (1 - c if k & 1 else c)
            mk = lambda dst: pltpu.make_async_remote_copy(
                src_ref=srcs[bi], dst_ref=dst, send_sem=send_sems.at[7 * bi + k - 1], recv_sem=recv_sems.at[7 * bi + k - 1],
                device_id=(px, py, pc), device_id_type=MESH)
            pairs.append((mk(lands[bi].at[me]), mk(lands[bi].at[4 * px + 2 * py + pc])))
    return pairs


def _direct_gather_start(bufs, dep, name):
    nbuf = len(bufs)
    hbm, sem = pl.BlockSpec(memory_space=pltpu.HBM), pl.BlockSpec(memory_space=pltpu.SEMAPHORE)

    def body(*refs):
        srcs, lands = refs[:nbuf], refs[nbuf:2 * nbuf]
        for send, _ in _peer_copies(srcs, lands, refs[2 * nbuf + 1], refs[2 * nbuf + 2]):
            send.start()
        refs[-1][...] = jnp.zeros_like(refs[-1])

    held = [pltpu.HBM(b.shape, b.dtype) for b in bufs]
    zones = [pltpu.HBM((NDEV,) + b.shape, b.dtype) for b in bufs]
    outs = pl.pallas_call(
        body, name=name, in_specs=[hbm] * (2 * nbuf) + [pl.BlockSpec(memory_space=pl.ANY)],
        out_specs=(sem, sem, *[hbm] * (2 * nbuf), pl.BlockSpec(memory_space=pltpu.VMEM)),
        out_shape=(pltpu.SemaphoreType.DMA((7 * nbuf,)), pltpu.SemaphoreType.DMA((7 * nbuf,)), *held, *zones, _sds((8, 128))),
        input_output_aliases={i: 2 + i for i in range(2 * nbuf)},
        compiler_params=pltpu.CompilerParams(has_side_effects=pltpu.SideEffectType.DATAFLOW_SIDE_EFFECTING),
    )(*[pltpu.with_memory_space_constraint(b, pltpu.HBM) for b in bufs],
      *[pltpu.with_memory_space_constraint(lax.empty((NDEV,) + b.shape, b.dtype), pltpu.HBM) for b in bufs], dep)
    return (outs[0], outs[1], list(outs[2:2 + nbuf]), list(outs[2 + nbuf:2 + 2 * nbuf])), outs[-1]


def _direct_gather_wait(send_sems, recv_sems, srcs, lands, after, name):
    nbuf = len(srcs)
    hbm, sem = pl.BlockSpec(memory_space=pltpu.HBM), pl.BlockSpec(memory_space=pltpu.SEMAPHORE)

    def body(*refs):
        for send, recv in _peer_copies(refs[:nbuf], refs[nbuf:2 * nbuf], refs[2 * nbuf], refs[2 * nbuf + 1]):
            send.wait_send()
            recv.wait_recv()

    outs = pl.pallas_call(
        body, name=name, in_specs=[hbm] * (2 * nbuf) + [sem, sem, pl.BlockSpec(memory_space=pl.ANY)], out_specs=[hbm] * (2 * nbuf),
        out_shape=(*[pltpu.HBM(b.shape, b.dtype) for b in srcs], *[pltpu.HBM(b.shape, b.dtype) for b in lands]),
        input_output_aliases={i: i for i in range(2 * nbuf)},
        compiler_params=pltpu.CompilerParams(has_side_effects=pltpu.SideEffectType.DATAFLOW_SIDE_EFFECTING),
    )(*srcs, *lands, send_sems, recv_sems, after)
    me = 4 * lax.axis_index("x") + 2 * lax.axis_index("y") + lax.axis_index("c")
    return [lax.dynamic_update_index_in_dim(g, b, me, axis=0) for g, b in zip(outs[nbuf:], outs[:nbuf])]


def _pair_exchange(bufs, name):
    nbuf = len(bufs)

    def body(*refs):
        ins, got = refs[:nbuf], refs[nbuf:2 * nbuf]
        send_sems, recv_sems, _ = refs[2 * nbuf:]
        x, y, c = lax.axis_index("x"), lax.axis_index("y"), lax.axis_index("c")
        copies = []
        for bi in range(nbuf):
            for q in range(4):
                copies.append(pltpu.make_async_remote_copy(
                    src_ref=ins[bi].at[2 * q + 1 - c], dst_ref=got[bi].at[q], send_sem=send_sems.at[4 * bi + q],
                    recv_sem=recv_sems.at[4 * bi + q], device_id=(x, y, 1 - c), device_id_type=MESH))
                copies[-1].start()
        for cp in copies:
            cp.wait_send()
            cp.wait_recv()

    return _comm_call(body, name, bufs, [_sds((4,) + b.shape[1:], b.dtype) for b in bufs], 4 * nbuf, 1)


def _chip_copies(srcs, lands, send_sems, recv_sems):
    x, y, c = lax.axis_index("x"), lax.axis_index("y"), lax.axis_index("c")
    myq = 2 * x + y
    pairs = []
    for bi in range(len(srcs)):
        for j, (qx, qy) in enumerate([(1 - x, y), (x, 1 - y), (1 - x, 1 - y)]):
            q = 2 * qx + qy
            mk = lambda src, dst: pltpu.make_async_remote_copy(
                src_ref=src, dst_ref=dst, send_sem=send_sems.at[3 * bi + j], recv_sem=recv_sems.at[3 * bi + j],
                device_id=(qx, qy, c), device_id_type=MESH)
            pairs.append((mk(srcs[bi].at[q], lands[bi].at[myq]), mk(srcs[bi].at[myq], lands[bi].at[q])))
    return pairs


def _chip_exchange_start(bufs, name):
    nbuf = len(bufs)
    hbm, sem = pl.BlockSpec(memory_space=pltpu.HBM), pl.BlockSpec(memory_space=pltpu.SEMAPHORE)

    def body(*refs):
        srcs, lands = refs[:nbuf], refs[nbuf:2 * nbuf]
        send_sems, recv_sems = refs[2 * nbuf], refs[2 * nbuf + 1]
        token = refs[-1]
        for send, _ in _chip_copies(srcs, lands, send_sems, recv_sems):
            send.start()
        token[...] = jnp.zeros_like(token)

    held = [pltpu.HBM(b.shape, b.dtype) for b in bufs]
    outs = pl.pallas_call(
        body, name=name, in_specs=[hbm] * (2 * nbuf), out_specs=(sem, sem, *[hbm] * (2 * nbuf), pl.BlockSpec(memory_space=pltpu.VMEM)),
        out_shape=(pltpu.SemaphoreType.DMA((3 * nbuf,)), pltpu.SemaphoreType.DMA((3 * nbuf,)), *held, *held, _sds((8, 128))),
        input_output_aliases={i: 2 + i for i in range(2 * nbuf)},
        compiler_params=pltpu.CompilerParams(has_side_effects=pltpu.SideEffectType.DATAFLOW_SIDE_EFFECTING),
    )(*[pltpu.with_memory_space_constraint(b, pltpu.HBM) for b in bufs],
      *[pltpu.with_memory_space_constraint(lax.empty(b.shape, b.dtype), pltpu.HBM) for b in bufs])
    return outs[0], outs[1], list(outs[2:2 + nbuf]), list(outs[2 + nbuf:2 + 2 * nbuf]), outs[-1]


def _chip_exchange_wait(send_sems, recv_sems, srcs, lands, after, name):
    nbuf = len(srcs)
    hbm, sem = pl.BlockSpec(memory_space=pltpu.HBM), pl.BlockSpec(memory_space=pltpu.SEMAPHORE)

    def body(*refs):
        src_refs, land_refs = refs[:nbuf], refs[nbuf:2 * nbuf]
        for send, recv in _chip_copies(src_refs, land_refs, refs[2 * nbuf], refs[2 * nbuf + 1]):
            send.wait_send()
            recv.wait_recv()

    held = [pltpu.HBM(b.shape, b.dtype) for b in srcs]
    outs = pl.pallas_call(
        body, name=name, in_specs=[hbm] * (2 * nbuf) + [sem, sem, pl.BlockSpec(memory_space=pl.ANY)], out_specs=[hbm] * (2 * nbuf),
        out_shape=(*held, *held), input_output_aliases={i: i for i in range(2 * nbuf)},
        compiler_params=pltpu.CompilerParams(has_side_effects=pltpu.SideEffectType.DATAFLOW_SIDE_EFFECTING),
    )(*srcs, *lands, send_sems, recv_sems, after)
    return list(outs[:nbuf]), list(outs[nbuf:])


def _prefetch_call(body, name, scalars, grid, in_specs, out_specs, out_shape, args):
    return pl.pallas_call(
        body, name=name, out_shape=out_shape,
        grid_spec=pltpu.PrefetchScalarGridSpec(num_scalar_prefetch=1, grid=grid, in_specs=in_specs, out_specs=out_specs),
        compiler_params=pltpu.CompilerParams(dimension_semantics=("parallel",) * len(grid), vmem_limit_bytes=V7X_VMEM_LIMIT))(scalars, *args)


def _pair_add(g, got, core, name):
    _, rows, cols = got.shape
    tr = _row_tile(rows, 16, 1024)

    def body(c_ref, a_ref, b_ref, o_ref):
        o_ref[...] = (a_ref[...].astype(F32) + b_ref[...].astype(F32)).astype(o_ref.dtype)

    blk = pl.BlockSpec((None, tr, cols), lambda q, i, c_ref: (q, i, 0))
    mine = pl.BlockSpec((None, None, tr, cols), lambda q, i, c_ref: (q, c_ref[0], i, 0))
    return _prefetch_call(body, name, core, (4, rows // tr), [mine, blk], blk, _sds(got.shape, got.dtype),
                          (g.reshape(4, 2, rows, cols), got))


def _reduce_chips(chip_sums, recv, slots, name):
    _, rows, cols = recv.shape
    tr = _row_tile(rows, 16, 128)

    def body(s_ref, own_ref, r1_ref, r2_ref, r3_ref, o_ref):
        o_ref[...] = ((own_ref[...].astype(F32) + r1_ref[...].astype(F32)) + r2_ref[...].astype(F32)) + r3_ref[...].astype(F32)

    pick = lambda k: pl.BlockSpec((None, tr, cols), lambda i, s_ref: (s_ref[k], i, 0))
    return _prefetch_call(body, name, slots, (rows // tr,), [pick(0), pick(1), pick(2), pick(3)],
                          pl.BlockSpec((tr, cols), lambda i, s_ref: (i, 0)), _sds((rows, cols)), (chip_sums, recv, recv, recv))


def _row_tile(rows, mult, cap):
    best = mult
    for t in range(mult, cap + 1, mult):
        if rows % t == 0:
            best = t
    assert rows % best == 0, (rows, mult)
    return best


def _reduce_sources(r, name):
    nsrc, rows, cols = r.shape
    tr = _row_tile(rows, 16 if r.dtype.itemsize == 2 else 8, 128)

    def body(r_ref, o_ref):
        acc = r_ref[0].astype(F32)
        for j in range(1, nsrc):
            acc = acc + r_ref[j].astype(F32)
        o_ref[...] = acc

    return _call(body, name, (rows // tr,), [pl.BlockSpec((nsrc, tr, cols), lambda i: (0, i, 0))],
                 pl.BlockSpec((tr, cols), lambda i: (i, 0)), _sds((rows, cols)), sem=("parallel",))(r)


def _adamw(w, g, m, v, name):
    rows, cols = w.shape
    tr = _row_tile(rows, 8, 128)
    c1 = 1.0 - ADAM_B1 ** ADAM_STEP
    c2 = 1.0 - ADAM_B2 ** ADAM_STEP

    def body(w_ref, g_ref, m_ref, v_ref, d_ref, mo_ref, vo_ref):
        g_ = g_ref[...]
        m_ = ADAM_B1 * m_ref[...] + (1.0 - ADAM_B1) * g_
        v_ = ADAM_B2 * v_ref[...] + (1.0 - ADAM_B2) * (g_ * g_)
        d_ref[...] = -ADAM_LR * ((m_ / c1) / (jnp.sqrt(v_ / c2) + ADAM_EPS) + ADAM_WD * w_ref[...])
        mo_ref[...] = m_
        vo_ref[...] = v_

    blk = pl.BlockSpec((tr, cols), lambda i: (i, 0))
    return _call(body, name, (rows // tr,), [blk] * 4, [blk] * 3, [_sds((rows, cols))] * 3, sem=("parallel",))(w, g, m, v)


SLAB_PART = 1024


def _pack(arrs):
    parts = []
    for a in arrs:
        flat = a.reshape(-1)
        pad = (-flat.shape[0]) % SLAB_PART
        if pad:
            flat = jnp.concatenate([flat, jnp.zeros((pad,), flat.dtype)])
        parts.append(flat.reshape(-1, 128))
    return jnp.concatenate(parts, axis=0)


def _unpack(slab, shapes):
    outs, row = [], 0
    for s in shapes:
        n = 1
        for d in s:
            n *= d
        rows = (n + SLAB_PART - 1) // SLAB_PART * (SLAB_PART // 128)
        outs.append(slab[row:row + rows].reshape(-1)[:n].reshape(s))
        row += rows
    return outs


def _unshard_last(g, lead):
    nl = len(lead)
    return jnp.transpose(g, tuple(range(1, nl + 1)) + (0, nl + 1)).reshape(tuple(lead) + (-1,))


def kernel(x, c, ctx, c_ctx, ada_w, ada_b, norm_g, w_in, hg_lb, hg_norm_g, rw_mu, rw_w0, rw_w2, rw_a0, rw_a2, rw_kk, rw_ka, rw_rk, rw_gn_g, rw_gn_b, w_hg_out, w_rw_out, w_out, final_g, loss_target, m_c_ctx, m_ada_w, m_ada_b, m_norm_g, m_w_in, m_hg_lb, m_hg_norm_g, m_rw_mu, m_rw_w0, m_rw_w2, m_rw_a0, m_rw_a2, m_rw_kk, m_rw_ka, m_rw_rk, m_rw_gn_g, m_rw_gn_b, m_w_hg_out, m_w_rw_out, m_w_out, m_final_g, v_c_ctx, v_ada_w, v_ada_b, v_norm_g, v_w_in, v_hg_lb, v_hg_norm_g, v_rw_mu, v_rw_w0, v_rw_w2, v_rw_a0, v_rw_a2, v_rw_kk, v_rw_ka, v_rw_rk, v_rw_gn_g, v_rw_gn_b, v_w_hg_out, v_w_rw_out, v_w_out, v_final_g):
    dm = _dims()
    me = 4 * lax.axis_index("x") + 2 * lax.axis_index("y") + lax.axis_index("c")

    sharded_small = [hg_lb, rw_mu[0], rw_w0[0], rw_w2[0], rw_a0[0], rw_a2[0]]
    c_rows = jnp.concatenate([c, jnp.zeros((7, D), F32)], axis=0)
    gathered = _gather_all([w_in[0].T.astype(BF16), _pack(sharded_small), c_rows], "gather_weights")
    win_t = gathered[0].reshape(-1, D)
    per_dev = jax.vmap(lambda s: _unpack(s, [a.shape for a in sharded_small]))(gathered[1])
    hg_lb_f, mu_f, w0_f, w2_f, a0_f, a2_f = [_unshard_last(p, p.shape[1:-1]) for p in per_dev]

    ncol = ada_w.shape[2]
    cc16 = jnp.concatenate([gathered[2][:, 0], c_ctx[None], jnp.zeros((7, D), F32)], axis=0)
    mod_all = _gather_all([_ada_fwd(cc16, ada_w[0])], "gather_mods")[0]
    mod_x = lax.dynamic_index_in_dim(mod_all, me, axis=1, keepdims=False).reshape(1, -1) + ada_b
    mod_c = mod_all[:, NDEV].reshape(1, -1) + ada_b
    mods = jnp.concatenate([mod_x, mod_c, jnp.zeros((6, 3 * D), F32)], axis=0)

    late_flight, token = _direct_gather_start([w_out[0].astype(BF16), w_hg_out[0].T.astype(BF16), w_rw_out[0].T.astype(BF16)],
                                              mod_all, "gather_late_start")
    mods = mods + token[0:1, 0:1]

    def late_weights(after):
        got = _direct_gather_wait(*late_flight, after, "gather_late_wait")
        return dict(zip(("wout", "whg_t", "wrw_t"), [g.reshape(-1, g.shape[2]) for g in got]))

    w = dict(win_t=win_t, late_weights=late_weights, norm_g=norm_g,
             lb0=hg_lb_f[:, 0], lb1=hg_lb_f[:, 1], hg_g=hg_norm_g, mu=mu_f, w0=w0_f, w2=w2_f, a0=a0_f, a2=a2_f,
             kk=rw_kk, ka=rw_ka, rk=rw_rk, gng=rw_gn_g, gnb=rw_gn_b, final_g=final_g[None])
    def start_exchange(grads_bf16, tag):
        names = list(grads_bf16)
        blocks = [grads_bf16[n].reshape(NDEV, -1, grads_bf16[n].shape[1]) for n in names]
        got = _pair_exchange(blocks, "pair_grads_" + tag)
        core = lax.axis_index("c").astype(jnp.int32).reshape(1)
        chip_sums = [_pair_add(g, r, core, "pair_add_" + n) for g, r, n in zip(blocks, got, names)]
        send_sems, recv_sems, held, lands, token = _chip_exchange_start(chip_sums, "chip_grads_start_" + tag)
        return token, (names, send_sems, recv_sems, held, lands)

    loss_dev, grad_x, dmods, in_flight, small = _local_step(x[0], ctx[0], loss_target[0], mods, w, start_exchange)
    loss = lax.psum(loss_dev[0, 0], AXES)

    my_chip = (2 * lax.axis_index("x") + lax.axis_index("y")).astype(jnp.int32)
    others = jnp.arange(3, dtype=jnp.int32)
    slots = jnp.concatenate([my_chip.reshape(1), others + (others >= my_chip).astype(jnp.int32)])

    def finish_exchange(tag, flight, after):
        names, *copies = flight
        chip_sums, recv = _chip_exchange_wait(*copies, after, "chip_grads_wait_" + tag)
        return {n: _reduce_chips(cs, r, slots, "reduce_" + n) for n, cs, r in zip(names, chip_sums, recv)}

    reduced = finish_exchange("early", in_flight[0], grad_x)

    dmods_all = _gather_all([dmods], "gather_dmods")[0]
    my_cols = lambda a: lax.dynamic_slice_in_dim(a, me * ncol, ncol, axis=1)
    g_ada, dcc16, d_ada_b = _ada_bwd(cc16, ada_w[0], my_cols(dmods_all[:, 0]), my_cols(dmods_all[:, 1]), dmods)
    small = dict(small, c_ctx=dcc16[NDEV:NDEV + 1], ada_b=d_ada_b)

    small_names = ["c_ctx", "ada_b", "norm_g", "lb0", "lb1", "hg_g", "mu", "w0", "a0", "kk", "ka", "rk", "gng", "gnb", "final_g"]
    lora_names = ["w2", "a2"]
    small_all, lora_all = _gather_all([_pack([small[n] for n in small_names]), _pack([small[n] for n in lora_names]).astype(BF16)],
                                      "gather_small_grads")
    small_sum = _reduce_sources(small_all, "reduce_small")
    sg = dict(zip(small_names, _unpack(small_sum, [small[n].shape for n in small_names])))
    sg.update(zip(lora_names, _unpack(_reduce_sources(lora_all, "reduce_lora"), [small[n].shape for n in lora_names])))

    reduced.update(finish_exchange("late", in_flight[1], small_sum))
    g_win_t, g_wout, g_whg_t, g_wrw_t = [reduced[n] for n in ("win_t", "wout", "whg_t", "wrw_t")]

    def my_shard(full):
        n = full.shape[-1] // NDEV
        return lax.dynamic_slice_in_dim(full, me * n, n, axis=full.ndim - 1)

    grads = dict(
        c_ctx=sg["c_ctx"][0], ada_w=g_ada[None], ada_b=sg["ada_b"], norm_g=sg["norm_g"], w_in=g_win_t.T[None],
        hg_lb=my_shard(jnp.stack([sg["lb0"], sg["lb1"]], axis=1)), hg_norm_g=sg["hg_g"], rw_mu=my_shard(sg["mu"])[None],
        rw_w0=my_shard(sg["w0"])[None], rw_w2=my_shard(sg["w2"])[None], rw_a0=my_shard(sg["a0"])[None], rw_a2=my_shard(sg["a2"])[None],
        rw_kk=sg["kk"], rw_ka=sg["ka"], rw_rk=sg["rk"], rw_gn_g=sg["gng"], rw_gn_b=sg["gnb"],
        w_hg_out=g_whg_t.T[None], w_rw_out=g_wrw_t.T[None], w_out=g_wout[None], final_g=sg["final_g"][0])
    weights = dict(c_ctx=c_ctx, ada_w=ada_w, ada_b=ada_b, norm_g=norm_g, w_in=w_in, hg_lb=hg_lb, hg_norm_g=hg_norm_g, rw_mu=rw_mu,
                   rw_w0=rw_w0, rw_w2=rw_w2, rw_a0=rw_a0, rw_a2=rw_a2, rw_kk=rw_kk, rw_ka=rw_ka, rw_rk=rw_rk, rw_gn_g=rw_gn_g,
                   rw_gn_b=rw_gn_b, w_hg_out=w_hg_out, w_rw_out=w_rw_out, w_out=w_out, final_g=final_g)
    m_in = dict(zip(weights, (m_c_ctx, m_ada_w, m_ada_b, m_norm_g, m_w_in, m_hg_lb, m_hg_norm_g, m_rw_mu, m_rw_w0, m_rw_w2, m_rw_a0,
                              m_rw_a2, m_rw_kk, m_rw_ka, m_rw_rk, m_rw_gn_g, m_rw_gn_b, m_w_hg_out, m_w_rw_out, m_w_out, m_final_g)))
    v_in = dict(zip(weights, (v_c_ctx, v_ada_w, v_ada_b, v_norm_g, v_w_in, v_hg_lb, v_hg_norm_g, v_rw_mu, v_rw_w0, v_rw_w2, v_rw_a0,
                              v_rw_a2, v_rw_kk, v_rw_ka, v_rw_rk, v_rw_gn_g, v_rw_gn_b, v_w_hg_out, v_w_rw_out, v_w_out, v_final_g)))

    big_w = ["ada_w", "w_in", "w_hg_out", "w_rw_out", "w_out"]
    delta, new_m, new_v = {}, {}, {}
    for n in big_w:
        shp = weights[n].shape
        two = lambda a: a.reshape(shp[-2], shp[-1])
        d_, m_, v_ = _adamw(two(weights[n]), two(grads[n]), two(m_in[n]), two(v_in[n]), "adamw_" + n)
        delta[n], new_m[n], new_v[n] = d_.reshape(shp), m_.reshape(shp), v_.reshape(shp)
    rest = [n for n in weights if n not in big_w]
    shapes = [weights[n].shape for n in rest]
    d_s, m_s, v_s = _adamw(_pack([weights[n] for n in rest]), _pack([grads[n] for n in rest]), _pack([m_in[n] for n in rest]),
                           _pack([v_in[n] for n in rest]), "adamw_small")
    for n, d_, m_, v_ in zip(rest, _unpack(d_s, shapes), _unpack(m_s, shapes), _unpack(v_s, shapes)):
        delta[n], new_m[n], new_v[n] = d_, m_, v_

    order = list(weights)
    return (loss, grad_x[None], *[grads[n] for n in order], *[delta[n] for n in order],
            *[new_m[n] for n in order], *[new_v[n] for n in order])
```

```python
import functools

import jax
import jax.numpy as jnp
from jax import lax
from jax.experimental import pallas as pl
from jax.experimental.pallas import tpu as pltpu

D = 2048
T = 2048
TC = 256
GW = 64
C = 64
HGW = 1024
RWW = 1024
LORA = 64
NDEV = 8
TM = 256
TMV = 128
NORM_EPS = 1e-6
RW_GN_EPS = 64e-5
ADAM_LR, ADAM_B1, ADAM_B2, ADAM_EPS, ADAM_WD, ADAM_STEP = 0.001, 0.9, 0.999, 1e-08, 0.01, 10
WB = 256
V7X_VMEM_LIMIT = 56 * 1024 * 1024

F32 = jnp.float32
BF16 = jnp.bfloat16
AXES = ("x", "y", "c")


def _dims():
    tt = T + TC
    sh = 3 * RWW + 4 * LORA
    hgc = 5 * HGW
    return dict(TT=tt, NCC=TC // C, NCH=tt // C, HG_H=HGW // 128, RW_P=RWW // 128, SH=sh, HGC=hgc,
                NCOLS=hgc + sh + RWW + 2 * D, NCT=TC // TM, NCTV=TC // TMV)


def _mm(a, b):
    return _dot3(a, b, "nn")


def _mm_nt(a, b):
    return _dot3(a, b, "nt")


def _mm_tn(a, b):
    return _dot3(a, b, "tn")


def _split(x):
    hi = x.astype(BF16)
    return hi, (x - hi.astype(F32)).astype(BF16)


_FORMS = {"nn": (((1,), (0,)), ((), ())), "nt": (((1,), (1,)), ((), ())), "tn": (((0,), (0,)), ((), ()))}


def _dot3_raw(a, b, form):
    ah, al = _split(a)
    bh, bl = _split(b)
    d = lambda x, y: lax.dot_general(x, y, _FORMS[form], preferred_element_type=F32)
    return d(ah, bh) + (d(ah, bl) + d(al, bh))


@functools.partial(jax.custom_vjp, nondiff_argnums=(2,))
def _dot3(a, b, form):
    return _dot3_raw(a, b, form)


def _dot3_fwd(a, b, form):
    return _dot3_raw(a, b, form), (a, b)


def _dot3_bwd(form, res, g):
    a, b = res
    if form == "nn":
        return _dot3(g, b, "nt"), _dot3(a, g, "tn")
    if form == "nt":
        return _dot3(g, b, "nn"), _dot3(g, a, "tn")
    return _dot3(b, g, "nt"), _dot3(a, g, "nn")


_dot3.defvjp(_dot3_fwd, _dot3_bwd)


def _scan_cumsum(inc, x):
    return _cumsum_vjp(inc.astype(BF16), x)


def _cumsum_raw(inc, x, form):
    h1 = x.astype(BF16)
    r1 = x - h1.astype(F32)
    h2 = r1.astype(BF16)
    h3 = (r1 - h2.astype(F32)).astype(BF16)
    d = lambda y: lax.dot_general(inc, y, _FORMS[form], preferred_element_type=F32)
    return d(h1) + (d(h2) + d(h3))


@jax.custom_vjp
def _cumsum_vjp(inc, x):
    return _cumsum_raw(inc, x, "nn")


_cumsum_vjp.defvjp(lambda inc, x: (_cumsum_raw(inc, x, "nn"), inc),
                   lambda inc, g: (jnp.zeros_like(inc), _cumsum_raw(inc, g, "tn")))


def _silu(x):
    return x * jax.nn.sigmoid(x)


def _softplus(x):
    return jnp.maximum(x, 0.0) + jnp.log(1.0 + jnp.exp(-jnp.abs(x)))


def _iota2(shape, dim):
    return lax.broadcasted_iota(jnp.int32, shape, dim)


def _pair_mask():
    return (_iota2((128, 128), 0) < 64) == (_iota2((128, 128), 1) < 64)


def _seg64_raw(x):
    e = _pair_mask().astype(BF16)
    parts = []
    for g in range(x.shape[1] // 128):
        hi, lo = _split(x[:, g * 128:(g + 1) * 128])
        parts.append(jnp.dot(hi, e, preferred_element_type=F32) + jnp.dot(lo, e, preferred_element_type=F32))
    return parts[0] if len(parts) == 1 else jnp.concatenate(parts, axis=1)


@jax.custom_vjp
def _seg64_sum(x):
    return _seg64_raw(x)


_seg64_sum.defvjp(lambda x: (_seg64_raw(x), None), lambda _, g: (_seg64_raw(g),))


def _mm1(a, b):
    return jnp.dot(a.astype(BF16), b.astype(BF16), preferred_element_type=F32)


def _mm1_nt(a, b):
    return lax.dot_general(a.astype(BF16), b.astype(BF16), (((1,), (1,)), ((), ())), preferred_element_type=F32)


def _seg128_mean(x):
    parts = [jnp.broadcast_to(jnp.mean(x[:, g * 128:(g + 1) * 128], axis=1, keepdims=True), (x.shape[0], 128))
             for g in range(x.shape[1] // 128)]
    return parts[0] if len(parts) == 1 else jnp.concatenate(parts, axis=1)


def _scan_masks(d):
    lag = (_iota2((C, C), 0) - _iota2((C, C), 1)) * (1 - 2 * d)
    return lag >= 0, lag > 0


def _tri_inverse(a):
    eye = (_iota2((C, C), 0) == _iota2((C, C), 1)).astype(F32)
    p = [-x for x in a]
    tm = [eye + x for x in p]
    n = 2
    while n < C:
        p = [_mm(x, x) for x in p]
        tm = [t + _mm(t, x) for t, x in zip(tm, p)]
        n *= 2
    return tm


@jax.custom_vjp
def _tri_solve(tm, a, rhs):
    return [_mm(t, r) for t, r in zip(tm, rhs)]


def _tri_solve_fwd(tm, a, rhs):
    u = [_mm(t, r) for t, r in zip(tm, rhs)]
    return u, (tm, u)


def _tri_solve_bwd(res, g):
    tm, u = res
    d_rhs = [_mm_tn(t, x) for t, x in zip(tm, g)]
    return [jnp.zeros_like(t) for t in tm], [-_mm_nt(d, x) for d, x in zip(d_rhs, u)], d_rhs


_tri_solve.defvjp(_tri_solve_fwd, _tri_solve_bwd)


def _rw_chunks(chains, tms=None):
    s0, r, lw, kap, b, v, kd, ds = zip(*chains)
    ids = range(len(chains))
    inc, strict = zip(*[_scan_masks(d) for d in ds])
    lane = _iota2((1, 128), 1)
    hm = ((lane < 64).astype(F32), (lane >= 64).astype(F32))
    lc = [_scan_cumsum(inc[i], lw[i]) for i in ids]
    ltot = [jnp.sum(lw[i], axis=0, keepdims=True) for i in ids]
    rt = [r[i] * jnp.exp(lc[i]) for i in ids]
    kt = [kap[i] * jnp.exp(lc[i] - lw[i]) for i in ids]
    einv = [jnp.exp(-lc[i]) for i in ids]
    kh = [kd[i] * einv[i] for i in ids]
    bh = [b[i] * einv[i] for i in ids]
    eend = [jnp.exp(ltot[i] - lc[i]) for i in ids]
    kbar = [kd[i] * eend[i] for i in ids]
    bbar = [b[i] * eend[i] for i in ids]
    lhs = [jnp.concatenate([kt[i] * hm[0], kt[i] * hm[1], rt[i] * hm[0], rt[i] * hm[1]], axis=0) for i in ids]
    gk = [_mm_nt(lhs[i], kh[i]) for i in ids]
    gb = [_mm_nt(lhs[i], bh[i]) for i in ids]
    ks = [_mm_nt(kt[i], s0[i]) for i in ids]
    ys = [_mm_nt(rt[i], s0[i]) for i in ids]
    heads = [(i, h) for i in ids for h in range(2)]
    akk = {c: jnp.where(strict[c[0]], gk[c[0]][c[1] * C:(c[1] + 1) * C], 0.0) for c in heads}
    akb = {c: jnp.where(strict[c[0]], gb[c[0]][c[1] * C:(c[1] + 1) * C], 0.0) for c in heads}
    ark = {c: jnp.where(inc[c[0]], gk[c[0]][(2 + c[1]) * C:(3 + c[1]) * C], 0.0) for c in heads}
    arb = {c: jnp.where(inc[c[0]], gb[c[0]][(2 + c[1]) * C:(3 + c[1]) * C], 0.0) for c in heads}
    vh = {c: v[c[0]] * hm[c[1]] for c in heads}
    av = {c: _mm(akk[c], vh[c]) for c in heads}
    rhs = {c: ks[c[0]] * hm[c[1]] + av[c] for c in heads}
    akb_list = [akb[c] for c in heads]
    tm = _tri_inverse(akb_list) if tms is None else [tms[c] for c in heads]
    uh = dict(zip(heads, _tri_solve(tm, akb_list, [rhs[c] for c in heads])))
    yv = {c: _mm(ark[c], vh[c]) for c in heads}
    yu = {c: _mm(arb[c], uh[c]) for c in heads}
    u = [uh[i, 0] + uh[i, 1] for i in ids]
    y = [ys[i] + (yv[i, 0] - yu[i, 0]) + (yv[i, 1] - yu[i, 1]) for i in ids]
    upd = [_mm_tn(jnp.concatenate([v[i], -u[i]], axis=0), jnp.concatenate([kbar[i], bbar[i]], axis=0)) for i in ids]
    s1 = [s0[i] * jnp.exp(ltot[i]) + jnp.where(_pair_mask(), upd[i], 0.0) for i in ids]
    if tms is None:
        return [(y[i], s1[i], tm[2 * i], tm[2 * i + 1]) for i in ids]
    return [(y[i], s1[i]) for i in ids]


HG_SUB = 16


def _hg_chunks(chains):
    s0, qr, iv, f, lb, ds = zip(*chains)
    ids = range(len(chains))
    inc = [_scan_masks(d)[0] for d in ds]
    q = [_silu(qr[i]) for i in ids]
    fg = [lb[i] + (1.0 - lb[i]) * jax.nn.sigmoid(f[i]) for i in ids]
    k = [1.0 - fg[i] for i in ids]
    g = [jnp.log(fg[i]) for i in ids]
    bc = [_scan_cumsum(inc[i], g[i]) for i in ids]
    btot = [jnp.sum(g[i], axis=0, keepdims=True) for i in ids]
    o_inter = [_mm_nt(q[i] * jnp.exp(bc[i]), s0[i]) for i in ids]
    rowi = _iota2((C, 1), 0)
    outs = [[] for _ in ids]
    for blk in range(C // HG_SUB):
        lo, hi = blk * HG_SUB, (blk + 1) * HG_SUB
        first = [lo if ds[i] == 0 else hi - 1 for i in ids]
        ref = [jnp.sum(jnp.where(rowi == first[i], bc[i] - g[i], 0.0), axis=0, keepdims=True) for i in ids]
        qi = [q[i][lo:hi] * jnp.exp(bc[i][lo:hi] - ref[i]) for i in ids]
        src = [(rowi < hi) if ds[i] == 0 else (rowi >= lo) for i in ids]
        ke = [jnp.where(src[i], k[i] * jnp.exp(jnp.where(src[i], ref[i] - bc[i], 0.0)), 0.0) for i in ids]
        a = [jnp.where(inc[i][lo:hi], _mm_nt(qi[i], ke[i]), 0.0) for i in ids]
        part = [_mm(a[i], iv[i]) for i in ids]
        for i in ids:
            outs[i].append(part[i])
    o = [o_inter[i] + jnp.concatenate(outs[i], axis=0) for i in ids]
    upd = [_mm_tn(iv[i], k[i] * jnp.exp(btot[i] - bc[i])) for i in ids]
    s1 = [s0[i] * jnp.exp(btot[i]) + upd[i] for i in ids]
    return [(o[i], s1[i]) for i in ids]


def _lockstep(fn, ds, nargs):
    def flat_fn(*flat):
        return tuple(fn([tuple(flat[i * nargs:(i + 1) * nargs]) + (d,) for i, d in enumerate(ds)]))

    return flat_fn


def _h_math(x, ng, scale, shift):
    return x * lax.rsqrt(jnp.mean(x * x, axis=-1, keepdims=True) + NORM_EPS) * ng * (1.0 + scale) + shift


def _hg_post_math(o0, o1, z, g):
    o = o0 + o1
    on = o * lax.rsqrt(_seg128_mean(o * o) + NORM_EPS)
    return on * g * _silu(z)


def _rw_prep_math(k, pw0, pw1, pa0, pa1, w0, a0, kkp, kap_p):
    kk = k * kkp
    kap = kk * lax.rsqrt(_seg64_sum(kk * kk) + 1e-12)
    outs = []
    for d, (pw, pa) in enumerate(((pw0, pa0), (pw1, pa1))):
        w_log = -_softplus(-(w0[d:d + 1] + pw)) - 0.5
        lw = -jnp.exp(w_log)
        a = jax.nn.sigmoid(a0[d:d + 1] + pa)
        kd = k * (1.0 + (a - 1.0) * kap_p)
        outs += [lw, kap * a, kd]
    return (kap, *outs)


def _rw_post_math(y0, y1, kd0, kd1, r, v, z, rk, gng, gnb):
    ys, ksum = y0 + y1, kd0 + kd1
    mean = _seg64_sum(ys) * (1.0 / 64.0)
    cen = ys - mean
    var = _seg64_sum(cen * cen) * (1.0 / 64.0)
    yn = cen * lax.rsqrt(var + RW_GN_EPS) * gng + gnb
    bonus = _seg64_sum(r * ksum * rk) * v
    return (yn + bonus) * _silu(z)


def _head_math(x, mo, gate, fg, tgt):
    xo = x + gate * mo
    y = xo * lax.rsqrt(jnp.mean(xo * xo, axis=-1, keepdims=True) + NORM_EPS) * fg
    err = y - tgt
    return 0.5 * jnp.sum(jnp.mean(err * err, axis=-1, keepdims=True), axis=0, keepdims=True)


def _gate_math(ghg, grw, a, b):
    return jax.nn.sigmoid(ghg) * a + jax.nn.sigmoid(grw) * b


def _call(body, name, grid, in_specs, out_specs, out_shape, scratch=(), sem=None, aliases=None):
    return pl.pallas_call(
        body, name=name, grid=grid, in_specs=in_specs, out_specs=out_specs, out_shape=out_shape,
        scratch_shapes=list(scratch), input_output_aliases=aliases or {},
        compiler_params=pltpu.CompilerParams(dimension_semantics=sem, vmem_limit_bytes=V7X_VMEM_LIMIT))


def _full(shape):
    nd = len(shape)
    return pl.BlockSpec(shape, lambda *_: (0,) * nd)


def _sds(shape, dtype=F32):
    return jax.ShapeDtypeStruct(shape, dtype)


def _proj_nt(h, wt, wt_blk0, nblk, row0, nrows, name):
    def body(h_ref, w_ref, o_ref):
        o_ref[...] = lax.dot_general(h_ref[row0:row0 + nrows, :], w_ref[...], (((1,), (1,)), ((), ())), preferred_element_type=F32)

    return _call(body, name, (nblk,), [_full(h.shape), pl.BlockSpec((WB, wt.shape[1]), lambda j: (wt_blk0 + j, 0))],
                 pl.BlockSpec((nrows, WB), lambda j: (0, j)), _sds((nrows, nblk * WB)), sem=("parallel",))(h, wt)


def _dw_tn(a, b, row0, out_rows, out_blk0, name, prev=None):
    m, n = a.shape
    k2 = b.shape[1]
    nblk = n // WB

    def body(*refs):
        a_ref, b_ref, o_ref = refs[0], refs[1], refs[-1]
        o_ref[...] = lax.dot_general(a_ref[...], b_ref[row0:row0 + m, :], (((0,), (0,)), ((), ())),
                                     preferred_element_type=F32).astype(BF16)

    in_specs = [pl.BlockSpec((m, WB), lambda j: (0, j)), _full(b.shape)]
    args = [a, b]
    aliases = None
    if prev is not None:
        in_specs.append(pl.BlockSpec(memory_space=pl.ANY))
        args.append(prev)
        aliases = {2: 0}
    return _call(body, name, (nblk,), in_specs, pl.BlockSpec((WB, k2), lambda j: (out_blk0 + j, 0)), _sds((out_rows, k2), BF16),
                 sem=("arbitrary",), aliases=aliases)(*args)


def _mm_nn(a, w, w_blk0, name, after=None):
    m, kc = a.shape
    n = w.shape[1]
    kb = max(k for k in (WB, 2 * WB, 4 * WB) if kc % k == 0 and (w_blk0 * WB) % k == 0)
    w_blk0 = w_blk0 * WB // kb
    tmb = 3 * TM if m % (3 * TM) == 0 else (2 * TM if m % (2 * TM) == 0 else TM)

    def body(a_ref, w_ref, *rest):
        o_ref = rest[-1]

        @pl.when(pl.program_id(1) == 0)
        def _():
            o_ref[...] = jnp.zeros_like(o_ref)

        o_ref[...] += jnp.dot(a_ref[...], w_ref[...], preferred_element_type=F32)

    extra = [] if after is None else [pl.BlockSpec(memory_space=pl.ANY)]
    return _call(body, name, (m // tmb, kc // kb),
                 [pl.BlockSpec((tmb, kb), lambda i, k: (i, k)), pl.BlockSpec((kb, n), lambda i, k: (w_blk0 + k, 0))] + extra,
                 pl.BlockSpec((tmb, n), lambda i, k: (i, 0)), _sds((m, n)), sem=("parallel", "arbitrary"))(
        a, w, *([] if after is None else [after]))


def _ada_fwd(cc16, ada_shard):
    ncol = ada_shard.shape[1]

    def body(c_ref, w_ref, o_ref):
        o_ref[...] = _mm(_silu(c_ref[...]), w_ref[...])

    return _call(body, "ada_fwd", (1,), [_full((16, D)), _full((D, ncol))], _full((16, ncol)), _sds((16, ncol)))(cc16, ada_shard)


def _ada_bwd(cc16, ada_shard, dm_x, dm_c, dmods):
    ncol = ada_shard.shape[1]

    def body(c_ref, w_ref, dx_ref, dc_ref, dm_ref, dw_ref, dcc_ref, db_ref):
        s, vjp = jax.vjp(_silu, c_ref[...])
        dc_tot = dc_ref[0:1, :]
        for j in range(1, NDEV):
            dc_tot = dc_tot + dc_ref[j:j + 1, :]
        row = _iota2((8, 1), 0)
        dm16 = jnp.concatenate([dx_ref[...], jnp.where(row == 0, dc_tot, 0.0)], axis=0)
        dw_ref[...] = _mm_tn(s, dm16)
        dcc_ref[...] = vjp(_mm_nt(dm16, w_ref[...]))[0]
        db_ref[...] = dm_ref[0:1, :] + dm_ref[1:2, :]

    return _call(body, "ada_bwd", (1,), [_full((16, D)), _full((D, ncol)), _full((8, ncol)), _full((8, ncol)), _full((8, 3 * D))],
                 [_full((D, ncol)), _full((16, D)), _full((1, 3 * D))], [_sds((D, ncol)), _sds((16, D)), _sds((1, 3 * D))])(
        cc16, ada_shard, dm_x, dm_c, dmods)


def _sel_mod(mods_ref, is_ctx, lo):
    return jnp.where(is_ctx, mods_ref[1:2, lo:lo + D], mods_ref[0:1, lo:lo + D])


def _token_specs():
    nct = _dims()["NCT"]
    return (pl.BlockSpec((TM, D), lambda i: (jnp.minimum(i, nct - 1), 0)),
            pl.BlockSpec((TM, D), lambda i: (jnp.maximum(i - nct, 0), 0)))


def _h_fwd(ctx, x, mods, norm_g):
    dm = _dims()

    def body(c_ref, x_ref, m_ref, g_ref, o_ref):
        is_ctx = pl.program_id(0) < dm["NCT"]
        tok = jnp.where(is_ctx, c_ref[...], x_ref[...])
        o_ref[...] = _h_math(tok, g_ref[...], _sel_mod(m_ref, is_ctx, D), _sel_mod(m_ref, is_ctx, 0)).astype(BF16)

    return _call(body, "h_fwd", (dm["TT"] // TM,), [*_token_specs(), _full((8, 3 * D)), _full((1, D))],
                 pl.BlockSpec((TM, D), lambda i: (i, 0)), _sds((dm["TT"], D), BF16), sem=("parallel",))(ctx, x, mods, norm_g)


def _chunk_of(d, s):
    dm = _dims()
    ncc, nch = dm["NCC"], dm["NCH"]
    return s if d == 0 else jnp.where(s < ncc, ncc - 1 - s, nch - 1 - (s - ncc))


HG_GROUP = 8
RW_GROUP = 8


def _dir_specs(make):
    return [make(d) for d in range(2)]


def _lanes(j):
    return slice(j * 128, (j + 1) * 128)


def _hg_scan_fwd(p_hg, lb3):
    dm = _dims()
    nh, nch, tt = dm["HG_H"], dm["NCH"], dm["TT"]
    pp = min(HG_GROUP, nh)
    lw_ = 128 * pp

    def body(q0, q1, i0, i1, f0, f1, lb0, lb1, o0, o1, ck_ref, s_ref):
        @pl.when(pl.program_id(1) == 0)
        def _():
            s_ref[...] = jnp.zeros_like(s_ref)

        keys, chains = [], []
        for d, refs in enumerate(((q0, i0, f0, lb0), (q1, i1, f1, lb1))):
            vals = [r[...] for r in refs]
            for j in range(pp):
                keys.append((d, j))
                chains.append((s_ref[d, j], *[a[:, _lanes(j)] for a in vals], d))
        for (d, j), chain, (out, s1) in zip(keys, chains, _hg_chunks(chains)):
            ck_ref[d, j] = chain[0]
            (o0, o1)[d][:, _lanes(j)] = out
            s_ref[d, j] = s1

    blk = lambda off: _dir_specs(lambda d: pl.BlockSpec((C, lw_), lambda h, s: (_chunk_of(d, s), off * nh // pp + h)))
    return _call(body, "hg_scan_fwd", (nh // pp, nch),
                 blk(0) + blk(1) + _dir_specs(lambda d: pl.BlockSpec((C, lw_), lambda h, s: (_chunk_of(d, s), (2 + d) * nh // pp + h)))
                 + _dir_specs(lambda d: pl.BlockSpec((None, 1, lw_), lambda h, s: (d, 0, h))),
                 _dir_specs(lambda d: pl.BlockSpec((C, lw_), lambda h, s: (_chunk_of(d, s), h)))
                 + [pl.BlockSpec((2, pp, None, 128, 128), lambda h, s: (0, h, s, 0, 0))],
                 [_sds((tt, HGW)), _sds((tt, HGW)), _sds((2, nh, nch, 128, 128))],
                 scratch=[pltpu.VMEM((2, pp, 128, 128), F32)], sem=("parallel", "arbitrary"))(*([p_hg] * 6), lb3, lb3)


def _hg_scan_bwd(p_hg, lb3, ck, do):
    dm = _dims()
    nh, nch, tt = dm["HG_H"], dm["NCH"], dm["TT"]
    pp = min(HG_GROUP, nh)
    lw_ = 128 * pp

    def body(q0, q1, i0, i1, f0, f1, lb0, lb1, ck_ref, do0, do1, dq0, dq1, di0, di1, df0, df1, dlb_ref, ds_ref):
        @pl.when(pl.program_id(1) == 0)
        def _():
            ds_ref[...] = jnp.zeros_like(ds_ref)
            dlb_ref[...] = jnp.zeros_like(dlb_ref)

        keys, flat, cts = [], [], []
        for d, refs in enumerate(((q0, i0, f0, lb0, do0), (q1, i1, f1, lb1, do1))):
            *vals, do_ = [r[...] for r in refs]
            for j in range(pp):
                keys.append((d, j))
                flat += [ck_ref[d, j], *[a[:, _lanes(j)] for a in vals]]
                cts.append((do_[:, _lanes(j)], ds_ref[d, j]))
        _, vjp = jax.vjp(_lockstep(_hg_chunks, [d for d, _ in keys], 5), *flat)
        grads = vjp(tuple(cts))
        for n, (d, j) in enumerate(keys):
            ds0, dq_, di_, df_, dlb = grads[5 * n:5 * n + 5]
            (dq0, dq1)[d][:, _lanes(j)] = dq_
            (di0, di1)[d][:, _lanes(j)] = di_
            (df0, df1)[d][:, _lanes(j)] = df_
            dlb_ref[d, :, _lanes(j)] += dlb
            ds_ref[d, j] = ds0

    ch = lambda d, s: _chunk_of(d, nch - 1 - s)
    blk = lambda off: _dir_specs(lambda d: pl.BlockSpec((C, lw_), lambda h, s: (ch(d, s), off * nh // pp + h)))
    tok = _dir_specs(lambda d: pl.BlockSpec((C, lw_), lambda h, s: (ch(d, s), h)))
    return _call(body, "hg_scan_bwd", (nh // pp, nch),
                 blk(0) + blk(1) + _dir_specs(lambda d: pl.BlockSpec((C, lw_), lambda h, s: (ch(d, s), (2 + d) * nh // pp + h)))
                 + _dir_specs(lambda d: pl.BlockSpec((None, 1, lw_), lambda h, s: (d, 0, h)))
                 + [pl.BlockSpec((2, pp, None, 128, 128), lambda h, s: (0, h, nch - 1 - s, 0, 0))] + tok,
                 tok * 3 + [pl.BlockSpec((2, 1, lw_), lambda h, s: (0, 0, h))],
                 [_sds((tt, HGW))] * 6 + [_sds((2, 1, HGW))],
                 scratch=[pltpu.VMEM((2, pp, 128, 128), F32)], sem=("parallel", "arbitrary"))(*([p_hg] * 6), lb3, lb3, ck, do, do)


def _rw_scan_fwd(sh, kap, lw, b, kd):
    dm = _dims()
    npair, nch, tt = dm["RW_P"], dm["NCH"], dm["TT"]
    pp = min(RW_GROUP, npair)
    lw_ = 128 * pp

    def body(r0, r1, v0, v1, ka0, ka1, lw0, lw1, b0, b1, kd0, kd1, y0, y1, ck_ref, tm_ref, s_ref):
        @pl.when(pl.program_id(1) == 0)
        def _():
            s_ref[...] = jnp.zeros_like(s_ref)

        keys, chains = [], []
        for d, refs in enumerate(((r0, lw0, ka0, b0, v0, kd0), (r1, lw1, ka1, b1, v1, kd1))):
            vals = [r[...] for r in refs]
            for j in range(pp):
                keys.append((d, j))
                chains.append((s_ref[d, j], *[a[:, _lanes(j)] for a in vals], d))
        for (d, j), chain, (out, s1, tm_a, tm_b) in zip(keys, chains, _rw_chunks(chains)):
            ck_ref[d, j] = chain[0]
            tm_ref[d, j, 0] = tm_a
            tm_ref[d, j, 1] = tm_b
            (y0, y1)[d][:, _lanes(j)] = out
            s_ref[d, j] = s1

    two = lambda off: _dir_specs(lambda d: pl.BlockSpec((C, lw_), lambda p, s: (_chunk_of(d, s), off * npair // pp + p)))
    three = _dir_specs(lambda d: pl.BlockSpec((None, C, lw_), lambda p, s: (d, _chunk_of(d, s), p)))
    return _call(body, "rw_scan_fwd", (npair // pp, nch), two(0) + two(2) + two(0) + three * 3,
                 two(0) + [pl.BlockSpec((2, pp, None, 128, 128), lambda p, s: (0, p, s, 0, 0)),
                           pl.BlockSpec((2, pp, None, 2, C, C), lambda p, s: (0, p, s, 0, 0, 0))],
                 [_sds((tt, RWW)), _sds((tt, RWW)), _sds((2, npair, nch, 128, 128)), _sds((2, npair, nch, 2, C, C))],
                 scratch=[pltpu.VMEM((2, pp, 128, 128), F32)], sem=("parallel", "arbitrary"))(
        sh, sh, sh, sh, kap, kap, lw, lw, b, b, kd, kd)


def _rw_scan_bwd(sh, kap, lw, b, kd, ck, tm_ck, dy):
    dm = _dims()
    npair, nch, tt = dm["RW_P"], dm["NCH"], dm["TT"]
    pp = min(RW_GROUP, npair)
    lw_ = 128 * pp

    def body(r0, r1, v0, v1, ka0, ka1, lw0, lw1, b0, b1, kd0, kd1, ck_ref, tm_ref, dy0, dy1,
             dr0, dr1, dv0, dv1, dka0, dka1, dlw0, dlw1, db0, db1, dkd0, dkd1, ds_ref):
        @pl.when(pl.program_id(1) == 0)
        def _():
            ds_ref[...] = jnp.zeros_like(ds_ref)

        keys, flat, cts, tms = [], [], [], {}
        for d, refs in enumerate(((r0, lw0, ka0, b0, v0, kd0, dy0), (r1, lw1, ka1, b1, v1, kd1, dy1))):
            *vals, dy_ = [r[...] for r in refs]
            for j in range(pp):
                tms[len(keys), 0], tms[len(keys), 1] = tm_ref[d, j, 0], tm_ref[d, j, 1]
                keys.append((d, j))
                flat += [ck_ref[d, j], *[a[:, _lanes(j)] for a in vals]]
                cts.append((dy_[:, _lanes(j)], ds_ref[d, j]))
        _, vjp = jax.vjp(_lockstep(functools.partial(_rw_chunks, tms=tms), [d for d, _ in keys], 7), *flat)
        grads = vjp(tuple(cts))
        outs = ((dr0, dlw0, dka0, db0, dv0, dkd0), (dr1, dlw1, dka1, db1, dv1, dkd1))
        for n, (d, j) in enumerate(keys):
            ds0, *g_in = grads[7 * n:7 * n + 7]
            for o_ref, g in zip(outs[d], g_in):
                o_ref[:, _lanes(j)] = g
            ds_ref[d, j] = ds0

    ch = lambda d, s: _chunk_of(d, nch - 1 - s)
    two = lambda off: _dir_specs(lambda d: pl.BlockSpec((C, lw_), lambda p, s: (ch(d, s), off * npair // pp + p)))
    three = _dir_specs(lambda d: pl.BlockSpec((None, C, lw_), lambda p, s: (d, ch(d, s), p)))
    return _call(body, "rw_scan_bwd", (npair // pp, nch),
                 two(0) + two(2) + two(0) + three * 3
                 + [pl.BlockSpec((2, pp, None, 128, 128), lambda p, s: (0, p, nch - 1 - s, 0, 0)),
                    pl.BlockSpec((2, pp, None, 2, C, C), lambda p, s: (0, p, nch - 1 - s, 0, 0, 0))] + two(0),
                 two(0) * 6, [_sds((tt, RWW))] * 12,
                 scratch=[pltpu.VMEM((2, pp, 128, 128), F32)], sem=("parallel", "arbitrary"))(
        sh, sh, sh, sh, kap, kap, lw, lw, b, b, kd, kd, ck, tm_ck, dy, dy)


def _tile_pos(i):
    dm = _dims()
    is_ctx = i < dm["NCT"]
    rows = lax.broadcasted_iota(jnp.int32, (TM, 1), 0)
    tl = rows + (i - dm["NCT"]) * TM
    width = jnp.where(is_ctx, TC, GW)
    assert TC & (TC - 1) == 0 and GW & (GW - 1) == 0 and TM % GW == 0
    colp = rows & (width - 1)
    return is_ctx, rows, tl, colp, width


def _shift_lr(cur, i):
    _, _, _, colp, width = _tile_pos(i)
    left = jnp.where(colp == 0, 0.0, pltpu.roll(cur, 1, 0))
    right = jnp.where(colp == width - 1, 0.0, pltpu.roll(cur, TM - 1, 0))
    return left, right


def _shift_ud(cur, prv, nxt, i):
    is_ctx, rows, tl, _, _ = _tile_pos(i)
    if TM > GW:
        up = jnp.where(rows >= GW, pltpu.roll(cur, GW, 0), pltpu.roll(prv, GW, 0))
        down = jnp.where(rows < TM - GW, pltpu.roll(cur, TM - GW, 0), pltpu.roll(nxt, TM - GW, 0))
    else:
        up, down = prv, nxt
    up = jnp.where(jnp.logical_or(is_ctx, tl < GW), 0.0, up)
    down = jnp.where(jnp.logical_or(is_ctx, tl >= T - GW), 0.0, down)
    return up, down


def _shift_specs():
    dm = _dims()
    nt = dm["TT"] // TM
    cw = max(w for w in range(128, 1664 + 1, 128) if dm["SH"] % w == 0)
    cur = pl.BlockSpec((TM, cw), lambda j, i: (i, j))
    prv = pl.BlockSpec((TM, cw), lambda j, i: (jnp.maximum(i - 1, 0), j))
    nxt = pl.BlockSpec((TM, cw), lambda j, i: (jnp.minimum(i + 1, nt - 1), j))
    mu = pl.BlockSpec((4, cw), lambda j, i: (0, j))
    return nt, cw, cur, prv, nxt, mu


def _shift_fwd(p_rw, mu):
    dm = _dims()
    nt, cw, cur, prv, nxt, mus = _shift_specs()

    def body(c_ref, p_ref, n_ref, mu_ref, o_ref):
        i = pl.program_id(1)
        p, m = c_ref[...], mu_ref[...]
        left, right = _shift_lr(p, i)
        up, down = _shift_ud(p, p_ref[...], n_ref[...], i)
        vert = jnp.where(i < dm["NCT"], 0.0, 1.0)
        o_ref[...] = (p * (1.0 - m[0:1] - m[1:2] - vert * (m[2:3] + m[3:4]))
                      + m[0:1] * left + m[1:2] * right + m[2:3] * up + m[3:4] * down)

    return _call(body, "shift_fwd", (dm["SH"] // cw, nt), [cur, prv, nxt, mus], cur, _sds((dm["TT"], dm["SH"])),
                 sem=("parallel", "parallel"))(p_rw, p_rw, p_rw, mu)


def _shift_bwd(p_rw, dsh, mu):
    dm = _dims()
    nt, cw, cur, prv, nxt, mus = _shift_specs()

    def body(c_ref, p_ref, n_ref, gc_ref, gp_ref, gn_ref, mu_ref, dp_ref, dmu_ref):
        i = pl.program_id(1)
        p, g, m = c_ref[...], gc_ref[...], mu_ref[...]
        vert = jnp.where(i < dm["NCT"], 0.0, 1.0)
        _, from_right = _shift_lr(m[0:1] * g, i)
        from_left, _ = _shift_lr(m[1:2] * g, i)
        _, from_down = _shift_ud(m[2:3] * g, g, m[2:3] * gn_ref[...], i)
        from_up, _ = _shift_ud(m[3:4] * g, m[3:4] * gp_ref[...], g, i)
        dp = g * (1.0 - m[0:1] - m[1:2] - vert * (m[2:3] + m[3:4])) + from_right + from_left + from_down + from_up
        dp_ref[...] = dp.astype(BF16)

        @pl.when(i == 0)
        def _():
            dmu_ref[...] = jnp.zeros_like(dmu_ref)

        left, right = _shift_lr(p, i)
        up, down = _shift_ud(p, p_ref[...], n_ref[...], i)
        s = lambda a: jnp.sum(a, axis=0, keepdims=True)
        dmu_ref[...] += jnp.concatenate([s(g * (left - p)), s(g * (right - p)), vert * s(g * (up - p)), vert * s(g * (down - p))], axis=0)

    return _call(body, "shift_bwd", (dm["SH"] // cw, nt), [cur, prv, nxt, cur, prv, nxt, mus],
                 [cur, pl.BlockSpec((4, cw), lambda j, i: (0, j))], [_sds((dm["TT"], dm["SH"]), BF16), _sds((4, dm["SH"]))],
                 sem=("parallel", "arbitrary"))(p_rw, p_rw, p_rw, dsh, dsh, dsh, mu)


def _acc_out(ref, first, vals):
    @pl.when(first)
    def _():
        for r in ref:
            r[...] = jnp.zeros_like(r)

    for r, v in zip(ref, vals):
        r[...] += v


def _rowsum(a):
    return jnp.sum(a, axis=0, keepdims=True)


def _rw_prep_specs():
    dm = _dims()
    rw = RWW // 128
    tok = lambda width, cb: pl.BlockSpec((TMV, width), lambda i: (i, cb))
    ins = [pl.BlockSpec((TMV, RWW), lambda i: (i, 1)), tok(128, 3 * rw), tok(128, 3 * rw + 1),
           _full((2, RWW)), _full((128, RWW)), _full((128, RWW)), _full((2, RWW)), _full((128, RWW)), _full((128, RWW)),
           _full((1, RWW)), _full((1, RWW))]
    return dm, ins


def _rw_prep_fwd(sh, w0, w2p0, w2p1, a0, a2p0, a2p1, kkp, kap_p):
    dm, ins = _rw_prep_specs()
    tt = dm["TT"]

    def body(k_ref, low_ref, loa_ref, w0_ref, w20_ref, w21_ref, a0_ref, a20_ref, a21_ref, kk_ref, ka_ref, kap_ref, lw_ref, b_ref, kd_ref):
        tw, la = jnp.tanh(low_ref[...]), loa_ref[...]
        kap, lw0, b0, kd0, lw1, b1, kd1 = _rw_prep_math(
            k_ref[...], _mm1(tw, w20_ref[...]), _mm1(tw, w21_ref[...]), _mm1(la, a20_ref[...]), _mm1(la, a21_ref[...]),
            w0_ref[...], a0_ref[...], kk_ref[...], ka_ref[...])
        kap_ref[...] = kap
        lw_ref[0], lw_ref[1] = lw0, lw1
        b_ref[0], b_ref[1] = b0, b1
        kd_ref[0], kd_ref[1] = kd0, kd1

    two = pl.BlockSpec((2, TMV, RWW), lambda i: (0, i, 0))
    return _call(body, "rw_prep_fwd", (tt // TMV,), ins, [pl.BlockSpec((TMV, RWW), lambda i: (i, 0)), two, two, two],
                 [_sds((tt, RWW)), _sds((2, tt, RWW)), _sds((2, tt, RWW)), _sds((2, tt, RWW))], sem=("parallel",))(
        sh, sh, sh, w0, w2p0, w2p1, a0, a2p0, a2p1, kkp, kap_p)


def _rw_prep_bwd(sh, w0, w2p0, w2p1, a0, a2p0, a2p1, kkp, kap_p, scan_grads, dkd_p, dr_p, dv_p):
    dm, ins = _rw_prep_specs()
    tt, sh_w = dm["TT"], dm["SH"]
    one = pl.BlockSpec((TMV, RWW), lambda i: (i, 0))

    def body(k_ref, low_ref, loa_ref, w0_ref, w20_ref, w21_ref, a0_ref, a20_ref, a21_ref, kk_ref, ka_ref,
             dr0, dr1, dv0, dv1, dka0, dka1, dlw0, dlw1, db0, db1, dkd0, dkd1, dkdp_ref, drp_ref, dvp_ref,
             dsh_ref, dw0_ref, dw20_ref, dw21_ref, da0_ref, da20_ref, da21_ref, dkk_ref, dka_ref):
        tw, la = jnp.tanh(low_ref[...]), loa_ref[...]
        w20, w21, a20, a21 = w20_ref[...], w21_ref[...], a20_ref[...], a21_ref[...]
        _, vjp = jax.vjp(_rw_prep_math, k_ref[...], _mm1(tw, w20), _mm1(tw, w21), _mm1(la, a20), _mm1(la, a21),
                         w0_ref[...], a0_ref[...], kk_ref[...], ka_ref[...])
        dkp = dkdp_ref[...]
        dk, dpw0, dpw1, dpa0, dpa1, dw0, da0, dkk, dka = vjp(
            (dka0[...] + dka1[...], dlw0[...], db0[...], dkd0[...] + dkp, dlw1[...], db1[...], dkd1[...] + dkp))
        twt, lat = tw.T, la.T
        dtw = _mm1_nt(dpw0, w20) + _mm1_nt(dpw1, w21)
        dsh_ref[:, 0:RWW] = dr0[...] + dr1[...] + drp_ref[...]
        dsh_ref[:, RWW:2 * RWW] = dk
        dsh_ref[:, 2 * RWW:3 * RWW] = dv0[...] + dv1[...] + dvp_ref[...]
        dsh_ref[:, 3 * RWW:3 * RWW + 128] = dtw * (1.0 - tw * tw)
        dsh_ref[:, 3 * RWW + 128:3 * RWW + 256] = _mm1_nt(dpa0, a20) + _mm1_nt(dpa1, a21)
        _acc_out((dw0_ref, dw20_ref, dw21_ref, da0_ref, da20_ref, da21_ref, dkk_ref, dka_ref), pl.program_id(0) == 0,
                 (dw0, _mm1(twt, dpw0), _mm1(twt, dpw1), da0, _mm1(lat, dpa0), _mm1(lat, dpa1), dkk, dka))

    par = [_full((2, RWW)), _full((128, RWW)), _full((128, RWW)), _full((2, RWW)), _full((128, RWW)), _full((128, RWW)),
           _full((1, RWW)), _full((1, RWW))]
    return _call(body, "rw_prep_bwd", (tt // TMV,), ins + [one] * 15,
                 [pl.BlockSpec((TMV, sh_w), lambda i: (i, 0))] + par,
                 [_sds((tt, sh_w)), _sds((2, RWW)), _sds((128, RWW)), _sds((128, RWW)), _sds((2, RWW)), _sds((128, RWW)), _sds((128, RWW)),
                  _sds((1, RWW)), _sds((1, RWW))],
                 sem=("arbitrary",))(sh, sh, sh, w0, w2p0, w2p1, a0, a2p0, a2p1, kkp, kap_p, *scan_grads, dkd_p, dr_p, dv_p)


def _rw_post_specs():
    dm = _dims()
    nv = dm["NCTV"]
    rw = RWW // 128
    lat3 = lambda d: pl.BlockSpec((None, TMV, RWW), lambda i: (d, nv + i, 0))
    lat2 = pl.BlockSpec((TMV, RWW), lambda i: (nv + i, 0))
    ins = [lat2, lat2, lat3(0), lat3(1),
           pl.BlockSpec((TMV, RWW), lambda i: (nv + i, 0)), pl.BlockSpec((TMV, RWW), lambda i: (nv + i, 2)),
           pl.BlockSpec((TMV, RWW), lambda i: (i, 0)), _full((1, RWW)), _full((1, RWW)), _full((1, RWW))]
    return dm, nv, ins


def _rw_post_fwd(y0_, y1_, kd, sh, p_z, rk, gng, gnb):
    dm, nv, ins = _rw_post_specs()

    def body(y0, y1, k0, k1, r, v, z, rk_ref, g_ref, b_ref, o_ref):
        o_ref[...] = _rw_post_math(y0[...], y1[...], k0[...], k1[...], r[...], v[...], z[...], rk_ref[...], g_ref[...], b_ref[...]).astype(BF16)

    return _call(body, "rw_post_fwd", (T // TMV,), ins, pl.BlockSpec((TMV, RWW), lambda i: (i, 0)), _sds((T, RWW), BF16),
                 sem=("parallel",))(y0_, y1_, kd, kd, sh, sh, p_z, rk, gng, gnb)


def _rw_post_bwd(y0_, y1_, kd, sh, p_z, rk, gng, gnb, dout):
    dm, nv, _ = _rw_post_specs()
    tt = dm["TT"]
    cl = lambda i: jnp.maximum(i - nv, 0)
    all3 = lambda d: pl.BlockSpec((None, TMV, RWW), lambda i: (d, i, 0))
    all2 = pl.BlockSpec((TMV, RWW), lambda i: (i, 0))
    ins = [all2, all2, all3(0), all3(1),
           pl.BlockSpec((TMV, RWW), lambda i: (i, 0)), pl.BlockSpec((TMV, RWW), lambda i: (i, 2)),
           pl.BlockSpec((TMV, RWW), lambda i: (cl(i), 0)), _full((1, RWW)), _full((1, RWW)), _full((1, RWW)),
           pl.BlockSpec((TMV, RWW), lambda i: (cl(i), 0))]
    tok = pl.BlockSpec((TMV, RWW), lambda i: (i, 0))

    def body(y0, y1, k0, k1, r, v, z, rk_ref, g_ref, b_ref, do_ref, dy_ref, dkd_ref, dr_ref, dv_ref, dz_ref, drk_ref, dg_ref, db_ref):
        i = pl.program_id(0)
        lat = jnp.where(i >= nv, 1.0, 0.0)
        _, vjp = jax.vjp(_rw_post_math, y0[...], y1[...], k0[...], k1[...], r[...], v[...], z[...], rk_ref[...], g_ref[...], b_ref[...])
        dy0, _, dk0, _, dr, dv, dz, drk, dg, db = vjp(do_ref[...] * lat)
        dy_ref[...] = dy0
        dkd_ref[...] = dk0
        dr_ref[...] = dr
        dv_ref[...] = dv
        dz_ref[...] = dz.astype(BF16)
        _acc_out((drk_ref, dg_ref, db_ref), i == 0, (drk, dg, db))

    return _call(body, "rw_post_bwd", (tt // TMV,), ins,
                 [tok, tok, tok, tok, pl.BlockSpec((TMV, RWW), lambda i: (cl(i), 0)), _full((1, RWW)), _full((1, RWW)), _full((1, RWW))],
                 [_sds((tt, RWW))] * 4 + [_sds((T, RWW), BF16), _sds((1, RWW)), _sds((1, RWW)), _sds((1, RWW))],
                 sem=("arbitrary",))(y0_, y1_, kd, kd, sh, sh, p_z, rk, gng, gnb, dout)


def _hg_post_fwd(o0_, o1_, p_hg, hg_g):
    dm = _dims()
    nv = dm["NCTV"]
    lat3 = lambda d: pl.BlockSpec((TMV, HGW), lambda i: (nv + i, 0))

    def body(o0, o1, z, g, out):
        out[...] = _hg_post_math(o0[...], o1[...], z[...], g[...]).astype(BF16)

    return _call(body, "hg_post_fwd", (T // TMV,), [lat3(0), lat3(1), pl.BlockSpec((TMV, HGW), lambda i: (nv + i, 4)), _full((1, HGW))],
                 pl.BlockSpec((TMV, HGW), lambda i: (i, 0)), _sds((T, HGW), BF16), sem=("parallel",))(o0_, o1_, p_hg, hg_g)


def _hg_post_bwd(o0_, o1_, p_hg, hg_g, dout):
    dm = _dims()
    nv, tt = dm["NCTV"], dm["TT"]
    all3 = lambda d: pl.BlockSpec((TMV, HGW), lambda i: (i, 0))
    tok = pl.BlockSpec((TMV, HGW), lambda i: (i, 0))

    def body(o0, o1, z, g, do_in, do_ref, dz_ref, dg_ref):
        i = pl.program_id(0)
        lat = jnp.where(i >= nv, 1.0, 0.0)
        _, vjp = jax.vjp(_hg_post_math, o0[...], o1[...], z[...], g[...])
        d0, _, dz, dg = vjp(do_in[...] * lat)
        do_ref[...] = d0
        dz_ref[...] = dz
        _acc_out((dg_ref,), i == 0, (dg,))

    return _call(body, "hg_post_bwd", (tt // TMV,),
                 [all3(0), all3(1), pl.BlockSpec((TMV, HGW), lambda i: (i, 4)), _full((1, HGW)),
                  pl.BlockSpec((TMV, HGW), lambda i: (jnp.maximum(i - nv, 0), 0))],
                 [tok, tok, _full((1, HGW))], [_sds((tt, HGW)), _sds((tt, HGW)), _sds((1, HGW))], sem=("arbitrary",))(o0_, o1_, p_hg, hg_g, dout)


def _hg_dproj(dq0, dq1, di0, di1, df0, df1, dz):
    dm = _dims()
    tt = dm["TT"]
    tok = pl.BlockSpec((TMV, HGW), lambda i: (i, 0))

    def body(q0, q1, i0, i1, f0, f1, z, o_ref):
        o_ref[:, 0:HGW] = (q0[...] + q1[...]).astype(BF16)
        o_ref[:, HGW:2 * HGW] = (i0[...] + i1[...]).astype(BF16)
        o_ref[:, 2 * HGW:3 * HGW] = f0[...].astype(BF16)
        o_ref[:, 3 * HGW:4 * HGW] = f1[...].astype(BF16)
        o_ref[:, 4 * HGW:5 * HGW] = z[...].astype(BF16)

    return _call(body, "hg_dproj", (tt // TMV,), [tok] * 7,
                 pl.BlockSpec((TMV, 5 * HGW), lambda i: (i, 0)), _sds((tt, 5 * HGW), BF16), sem=("parallel",))(
        dq0, dq1, di0, di1, df0, df1, dz)


def _merge1_fwd(out_hg, out_rw, whg_t, wrw_t, p_g):
    tok = pl.BlockSpec((TM, D), lambda i: (i, 0))

    def body(h_ref, r_ref, wh_ref, wr_ref, g1_ref, g2_ref, a_ref, b_ref, m_ref):
        nt = (((1,), (1,)), ((), ()))
        a = lax.dot_general(h_ref[...], wh_ref[...], nt, preferred_element_type=F32)
        b = lax.dot_general(r_ref[...], wr_ref[...], nt, preferred_element_type=F32)
        a_ref[...] = a
        b_ref[...] = b
        m_ref[...] = _gate_math(g1_ref[...], g2_ref[...], a, b).astype(BF16)

    return _call(body, "merge1_fwd", (T // TM,),
                 [pl.BlockSpec((TM, HGW), lambda i: (i, 0)), pl.BlockSpec((TM, RWW), lambda i: (i, 0)), _full((D, HGW)), _full((D, RWW)),
                  pl.BlockSpec((TM, D), lambda i: (i, 0)), pl.BlockSpec((TM, D), lambda i: (i, 1))],
                 [tok, tok, tok], [_sds((T, D)), _sds((T, D)), _sds((T, D), BF16)], sem=("parallel",))(out_hg, out_rw, whg_t, wrw_t, p_g, p_g)


def _head_fwd_bwd(x, merged, w_out, mods, final_g, tgt):
    tok = pl.BlockSpec((TM, D), lambda i: (i, 0))

    def body(x_ref, m_ref, w_ref, mods_ref, fg_ref, t_ref, dxo_ref, dmo_ref, loss_ref, dgate_ref, dfg_ref):
        mo = jnp.dot(m_ref[...], w_ref[...], preferred_element_type=F32)
        gate = mods_ref[0:1, 2 * D:3 * D]
        loss, vjp = jax.vjp(lambda x_, mo_, g_, fg_: _head_math(x_, mo_, g_, fg_, t_ref[...]), x_ref[...], mo, gate, fg_ref[...])
        dx, dmo, dgate, dfg = vjp(jnp.ones((1, 1), F32))
        dxo_ref[...] = dx
        dmo_ref[...] = dmo.astype(BF16)
        _acc_out((loss_ref, dgate_ref, dfg_ref), pl.program_id(0) == 0, (jnp.broadcast_to(loss, (1, 128)), dgate, dfg))

    return _call(body, "head_fwd_bwd", (T // TM,), [tok, tok, _full((D, D)), _full((8, 3 * D)), _full((1, D)), tok],
                 [tok, tok, _full((1, 128)), _full((1, D)), _full((1, D))],
                 [_sds((T, D)), _sds((T, D), BF16), _sds((1, 128)), _sds((1, D)), _sds((1, D))], sem=("arbitrary",))(
        x, merged, w_out, mods, final_g, tgt)


def _merge_bwd(dmo, w_out, p_g, a, b):
    tok = pl.BlockSpec((TM, D), lambda i: (i, 0))

    def body(dmo_ref, w_ref, g1_ref, g2_ref, a_ref, b_ref, da_ref, db_ref, dg_ref):
        dm_ = lax.dot_general(dmo_ref[...], w_ref[...], (((1,), (1,)), ((), ())), preferred_element_type=F32)
        _, vjp = jax.vjp(_gate_math, g1_ref[...], g2_ref[...], a_ref[...], b_ref[...])
        dg1, dg2, da, db = vjp(dm_)
        da_ref[...] = da.astype(BF16)
        db_ref[...] = db.astype(BF16)
        dg_ref[:, 0:D] = dg1.astype(BF16)
        dg_ref[:, D:2 * D] = dg2.astype(BF16)

    return _call(body, "merge_bwd", (T // TM,),
                 [tok, _full((D, D)), pl.BlockSpec((TM, D), lambda i: (i, 0)), pl.BlockSpec((TM, D), lambda i: (i, 1)), tok, tok],
                 [tok, tok, pl.BlockSpec((TM, 2 * D), lambda i: (i, 0))], [_sds((T, D), BF16), _sds((T, D), BF16), _sds((T, 2 * D), BF16)],
                 sem=("parallel",))(dmo, w_out, p_g, p_g, a, b)


def _h_bwd(ctx, x, mods, norm_g, dh_hg, dh_rw, dh_z, dh_g, dxo, dgate):
    dm = _dims()
    nct, tt = dm["NCT"], dm["TT"]
    tok = pl.BlockSpec((TM, D), lambda i: (i, 0))
    lat = pl.BlockSpec((TM, D), lambda i: (jnp.maximum(i - nct, 0), 0))

    def body(c_ref, x_ref, m_ref, g_ref, d1, d2, d3, d4, dxo_ref, dgate_ref, gx_ref, dng_ref, dmods_ref):
        i = pl.program_id(0)
        is_ctx = i < nct
        latf = jnp.where(is_ctx, 0.0, 1.0)
        dh = d1[...] + d2[...] + latf * (d3[...] + d4[...])
        _, vjp = jax.vjp(_h_math, jnp.where(is_ctx, c_ref[...], x_ref[...]), g_ref[...], _sel_mod(m_ref, is_ctx, D),
                         _sel_mod(m_ref, is_ctx, 0))
        dx, dng, dscale, dshift = vjp(dh)
        gx_ref[...] = dx + dxo_ref[...]

        @pl.when(i == 0)
        def _():
            dng_ref[...] = jnp.zeros_like(dng_ref)
            dmods_ref[...] = jnp.zeros_like(dmods_ref)
            dmods_ref[0:1, 2 * D:3 * D] = dgate_ref[...]

        dng_ref[...] += dng
        row = lax.broadcasted_iota(jnp.int32, (8, 1), 0)
        sel = jnp.where(row == jnp.where(is_ctx, 1, 0), 1.0, 0.0)
        dmods_ref[:, 0:D] += sel * dshift
        dmods_ref[:, D:2 * D] += sel * dscale

    return _call(body, "h_bwd", (tt // TM,),
                 [*_token_specs(), _full((8, 3 * D)), _full((1, D)), tok, tok, lat, lat, lat, _full((1, D))],
                 [lat, _full((1, D)), _full((8, 3 * D))], [_sds((T, D)), _sds((1, D)), _sds((8, 3 * D))], sem=("arbitrary",))(
        ctx, x, mods, norm_g, dh_hg, dh_rw, dh_z, dh_g, dxo, dgate)


def _lb_math(l0, l1):
    return jax.nn.sigmoid(l0 - l1)


def _lb_fwd(l0, l1):
    def body(a_ref, b_ref, o_ref):
        o_ref[...] = _lb_math(a_ref[...], b_ref[...])

    return _call(body, "lb_fwd", (1,), [_full((2, HGW))] * 2, _full((2, HGW)), _sds((2, HGW)))(l0, l1)


def _lb_bwd(l0, l1, dlb):
    def body(a_ref, b_ref, d_ref, da_ref, db_ref):
        _, vjp = jax.vjp(_lb_math, a_ref[...], b_ref[...])
        da_ref[...], db_ref[...] = vjp(d_ref[...])

    return _call(body, "lb_bwd", (1,), [_full((2, HGW))] * 3, [_full((2, HGW))] * 2, [_sds((2, HGW))] * 2)(l0, l1, dlb)


def _local_step(x, ctx, tgt, mods, w, start_exchange):
    dm = _dims()
    tt, sh_w, hgc = dm["TT"], dm["SH"], dm["HGC"]
    nb = lambda cols: cols // WB
    h = _h_fwd(ctx, x, mods, w["norm_g"])
    win_t = w["win_t"]
    p_hg = _proj_nt(h, win_t, 0, nb(hgc), 0, tt, "proj_hg")
    p_rw = _proj_nt(h, win_t, nb(hgc), nb(sh_w), 0, tt, "proj_rw")
    p_z = _proj_nt(h, win_t, nb(hgc + sh_w), nb(RWW), TC, T, "proj_z")
    p_g = _proj_nt(h, win_t, nb(hgc + sh_w + RWW), nb(2 * D), TC, T, "proj_g")

    lb3 = _lb_fwd(w["lb0"], w["lb1"]).reshape(2, 1, HGW)
    o0, o1, hg_ck = _hg_scan_fwd(p_hg, lb3)
    out_hg = _hg_post_fwd(o0, o1, p_hg, w["hg_g"])

    sh = _shift_fwd(p_rw, w["mu"])
    zpad = jnp.zeros((LORA, RWW), F32)
    prep_w = (w["w0"], jnp.concatenate([w["w2"][0], zpad], 0), jnp.concatenate([zpad, w["w2"][1]], 0),
              w["a0"], jnp.concatenate([w["a2"][0], zpad], 0), jnp.concatenate([zpad, w["a2"][1]], 0), w["kk"], w["ka"])
    kap, lw, b, kd = _rw_prep_fwd(sh, *prep_w)
    y0, y1, rw_ck, rw_tm = _rw_scan_fwd(sh, kap, lw, b, kd)
    post_w = (w["rk"], w["gng"], w["gnb"])
    out_rw = _rw_post_fwd(y0, y1, kd, sh, p_z, *post_w)

    late = w["late_weights"](out_rw)
    a, bb, merged = _merge1_fwd(out_hg, out_rw, late["whg_t"], late["wrw_t"], p_g)
    dxo, dmo, loss, dgate, d_fg = _head_fwd_bwd(x, merged, late["wout"], mods, w["final_g"], tgt)
    da, db, dp_g = _merge_bwd(dmo, late["wout"], p_g, a, bb)
    g_wout = _dw_tn(merged, dmo, 0, D, 0, "dw_out")
    g_whg = _dw_tn(da, out_hg, 0, D, 0, "dw_hg")
    g_wrw = _dw_tn(db, out_rw, 0, D, 0, "dw_rw")
    token, early = start_exchange(dict(wout=g_wout, whg_t=g_whg, wrw_t=g_wrw), "early")
    d_out_hg = _mm_nn(da, late["whg_t"], 0, "dx_hg", after=token)
    d_out_rw = _mm_nn(db, late["wrw_t"], 0, "dx_rw")

    do, dz_hg, d_hg_g = _hg_post_bwd(o0, o1, p_hg, w["hg_g"], d_out_hg)
    *hg_grads, dlb3 = _hg_scan_bwd(p_hg, lb3, hg_ck, do)
    dp_hg = _hg_dproj(*hg_grads, dz_hg)
    d_lb0, d_lb1 = _lb_bwd(w["lb0"], w["lb1"], dlb3.reshape(2, HGW))

    dy, dkd_p, dr_p, dv_p, dp_z, d_rk, d_gng, d_gnb = _rw_post_bwd(y0, y1, kd, sh, p_z, *post_w, d_out_rw)
    rw_grads = _rw_scan_bwd(sh, kap, lw, b, kd, rw_ck, rw_tm, dy)
    dsh, d_w0, d_w2p0, d_w2p1, d_a0, d_a2p0, d_a2p1, d_kk, d_ka = _rw_prep_bwd(sh, *prep_w, rw_grads, dkd_p, dr_p, dv_p)
    dp_rw, d_mu = _shift_bwd(p_rw, dsh, w["mu"])

    g_win = _dw_tn(dp_hg, h, 0, dm["NCOLS"], 0, "dw_in_hg")
    g_win = _dw_tn(dp_rw, h, 0, dm["NCOLS"], nb(hgc), "dw_in_rw", prev=g_win)
    g_win = _dw_tn(dp_z, h, TC, dm["NCOLS"], nb(hgc + sh_w), "dw_in_z", prev=g_win)
    g_win = _dw_tn(dp_g, h, TC, dm["NCOLS"], nb(hgc + sh_w + RWW), "dw_in_g", prev=g_win)
    token, late = start_exchange(dict(win_t=g_win), "late")
    in_flight = (early, late)
    dh_hg = _mm_nn(dp_hg, win_t, 0, "dh_hg", after=token)
    dh_rw = _mm_nn(dp_rw, win_t, nb(hgc), "dh_rw")
    dh_z = _mm_nn(dp_z, win_t, nb(hgc + sh_w), "dh_z")
    dh_g = _mm_nn(dp_g, win_t, nb(hgc + sh_w + RWW), "dh_g")
    gx, d_ng, dmods = _h_bwd(ctx, x, mods, w["norm_g"], dh_hg, dh_rw, dh_z, dh_g, dxo, dgate)

    small = dict(norm_g=d_ng, lb0=d_lb0, lb1=d_lb1, hg_g=d_hg_g, mu=d_mu, w0=d_w0,
                 w2=jnp.stack([d_w2p0[:LORA], d_w2p1[LORA:]]), a0=d_a0, a2=jnp.stack([d_a2p0[:LORA], d_a2p1[LORA:]]),
                 kk=d_kk, ka=d_ka, rk=d_rk, gng=d_gng, gnb=d_gnb, final_g=d_fg)
    return loss, gx, dmods, in_flight, small


MESH = pl.DeviceIdType.MESH


def _comm_call(body, name, bufs, out_shapes, nsem, nloc):
    hbm = pl.BlockSpec(memory_space=pl.ANY)
    return pl.pallas_call(
        body, name=name, in_specs=[hbm] * len(bufs), out_specs=[hbm] * len(out_shapes), out_shape=out_shapes,
        scratch_shapes=[pltpu.SemaphoreType.DMA((nsem,)), pltpu.SemaphoreType.DMA((nsem,)), pltpu.SemaphoreType.DMA((nloc,))],
    )(*bufs)


GATHER_PIECE_BYTES = 4 * 1024 * 1024


def _gather2(bufs, name):
    pieces = []
    for bi, b in enumerate(bufs):
        rows = b.shape[0]
        if rows % 32 == 0 and b.size * b.dtype.itemsize > GATHER_PIECE_BYTES:
            pieces += [(bi, 0, rows // 2), (bi, rows // 2, rows // 2)]
        else:
            pieces.append((bi, 0, rows))
    nbuf = len(bufs)

    def body(*refs):
        ins, outs = refs[:nbuf], refs[nbuf:2 * nbuf]
        send_sems, recv_sems, _ = refs[2 * nbuf:]
        x, y, c = lax.axis_index("x"), lax.axis_index("y"), lax.axis_index("c")
        me, sib = (x, y, c), (x, y, 1 - c)
        flip = lambda a, b: a + b - 2 * a * b
        xn, yn, diag = (1 - x, y), (x, 1 - y), (1 - x, 1 - y)
        relay_from, relay_to = (flip(x, c), flip(1 - y, c)), (flip(1 - x, c), flip(y, c))

        def copy(pi, k, block, to, own=False):
            bi, r0, nr = pieces[pi]
            rows = outs[bi].at[4 * block[0] + 2 * block[1] + block[2], pl.ds(r0, nr)]
            return pltpu.make_async_remote_copy(src_ref=ins[bi].at[pl.ds(r0, nr)] if own else rows, dst_ref=rows,
                                                send_sem=send_sems.at[7 * pi + k], recv_sem=recv_sems.at[7 * pi + k],
                                                device_id=to, device_id_type=MESH)

        started = []
        for pi in range(len(pieces)):
            started += [copy(pi, 0, me, sib, own=True), copy(pi, 1, me, (*xn, c), own=True), copy(pi, 2, me, (*yn, c), own=True)]
        for cp in started:
            cp.start()
        for pi in range(len(pieces)):
            for k, chip in ((1, xn), (2, yn)):
                copy(pi, k, (*chip, c), me).wait_recv()
                started.append(copy(pi, 3 + k, (*chip, c), sib))
                started[-1].start()
            started.append(copy(pi, 3, (*relay_from, c), (*relay_to, c)))
            started[-1].start()
        for pi in range(len(pieces)):
            copy(pi, 3, (*diag, c), me).wait_recv()
            started.append(copy(pi, 6, (*diag, c), sib))
            started[-1].start()
        for pi in range(len(pieces)):
            copy(pi, 0, sib, me).wait_recv()
            for k, chip in ((4, xn), (5, yn), (6, diag)):
                copy(pi, k, (*chip, 1 - c), me).wait_recv()
        for cp in started:
            cp.wait_send()

    return _comm_call(body, name, bufs, [_sds((NDEV,) + b.shape, b.dtype) for b in bufs], 7 * len(pieces), 1)


def _gather_all(bufs, name):
    me = 4 * lax.axis_index("x") + 2 * lax.axis_index("y") + lax.axis_index("c")
    return [lax.dynamic_update_index_in_dim(g, b, me, axis=0) for g, b in zip(_gather2(bufs, name), bufs)]


def _peer_copies(srcs, lands, send_sems, recv_sems):
    x, y, c = lax.axis_index("x"), lax.axis_index("y"), lax.axis_index("c")
    me = 4 * x + 2 * y + c
    pairs = []
    for bi in range(len(srcs)):
        for k in range(1, NDEV):
            px, py, pc = (1 - x if k & 4 else x), (1 - y if k & 2 else y), (1 - c if k & 1 else c)
            mk = lambda dst: pltpu.make_async_remote_copy(
                src_ref=srcs[bi], dst_ref=dst, send_sem=send_sems.at[7 * bi + k - 1], recv_sem=recv_sems.at[7 * bi + k - 1],
                device_id=(px, py, pc), device_id_type=MESH)
            pairs.append((mk(lands[bi].at[me]), mk(lands[bi].at[4 * px + 2 * py + pc])))
    return pairs


def _direct_gather_start(bufs, dep, name):
    nbuf = len(bufs)
    hbm, sem = pl.BlockSpec(memory_space=pltpu.HBM), pl.BlockSpec(memory_space=pltpu.SEMAPHORE)

    def body(*refs):
        srcs, lands = refs[:nbuf], refs[nbuf:2 * nbuf]
        for send, _ in _peer_copies(srcs, lands, refs[2 * nbuf + 1], refs[2 * nbuf + 2]):
            send.start()
        refs[-1][...] = jnp.zeros_like(refs[-1])

    held = [pltpu.HBM(b.shape, b.dtype) for b in bufs]
    zones = [pltpu.HBM((NDEV,) + b.shape, b.dtype) for b in bufs]
    outs = pl.pallas_call(
        body, name=name, in_specs=[hbm] * (2 * nbuf) + [pl.BlockSpec(memory_space=pl.ANY)],
        out_specs=(sem, sem, *[hbm] * (2 * nbuf), pl.BlockSpec(memory_space=pltpu.VMEM)),
        out_shape=(pltpu.SemaphoreType.DMA((7 * nbuf,)), pltpu.SemaphoreType.DMA((7 * nbuf,)), *held, *zones, _sds((8, 128))),
        input_output_aliases={i: 2 + i for i in range(2 * nbuf)},
        compiler_params=pltpu.CompilerParams(has_side_effects=pltpu.SideEffectType.DATAFLOW_SIDE_EFFECTING),
    )(*[pltpu.with_memory_space_constraint(b, pltpu.HBM) for b in bufs],
      *[pltpu.with_memory_space_constraint(lax.empty((NDEV,) + b.shape, b.dtype), pltpu.HBM) for b in bufs], dep)
    return (outs[0], outs[1], list(outs[2:2 + nbuf]), list(outs[2 + nbuf:2 + 2 * nbuf])), outs[-1]


def _direct_gather_wait(send_sems, recv_sems, srcs, lands, after, name):
    nbuf = len(srcs)
    hbm, sem = pl.BlockSpec(memory_space=pltpu.HBM), pl.BlockSpec(memory_space=pltpu.SEMAPHORE)

    def body(*refs):
        for send, recv in _peer_copies(refs[:nbuf], refs[nbuf:2 * nbuf], refs[2 * nbuf], refs[2 * nbuf + 1]):
            send.wait_send()
            recv.wait_recv()

    outs = pl.pallas_call(
        body, name=name, in_specs=[hbm] * (2 * nbuf) + [sem, sem, pl.BlockSpec(memory_space=pl.ANY)], out_specs=[hbm] * (2 * nbuf),
        out_shape=(*[pltpu.HBM(b.shape, b.dtype) for b in srcs], *[pltpu.HBM(b.shape, b.dtype) for b in lands]),
        input_output_aliases={i: i for i in range(2 * nbuf)},
        compiler_params=pltpu.CompilerParams(has_side_effects=pltpu.SideEffectType.DATAFLOW_SIDE_EFFECTING),
    )(*srcs, *lands, send_sems, recv_sems, after)
    me = 4 * lax.axis_index("x") + 2 * lax.axis_index("y") + lax.axis_index("c")
    return [lax.dynamic_update_index_in_dim(g, b, me, axis=0) for g, b in zip(outs[nbuf:], outs[:nbuf])]


def _pair_exchange(bufs, name):
    nbuf = len(bufs)

    def body(*refs):
        ins, got = refs[:nbuf], refs[nbuf:2 * nbuf]
        send_sems, recv_sems, _ = refs[2 * nbuf:]
        x, y, c = lax.axis_index("x"), lax.axis_index("y"), lax.axis_index("c")
        copies = []
        for bi in range(nbuf):
            for q in range(4):
                copies.append(pltpu.make_async_remote_copy(
                    src_ref=ins[bi].at[2 * q + 1 - c], dst_ref=got[bi].at[q], send_sem=send_sems.at[4 * bi + q],
                    recv_sem=recv_sems.at[4 * bi + q], device_id=(x, y, 1 - c), device_id_type=MESH))
                copies[-1].start()
        for cp in copies:
            cp.wait_send()
            cp.wait_recv()

    return _comm_call(body, name, bufs, [_sds((4,) + b.shape[1:], b.dtype) for b in bufs], 4 * nbuf, 1)


def _chip_copies(srcs, lands, send_sems, recv_sems):
    x, y, c = lax.axis_index("x"), lax.axis_index("y"), lax.axis_index("c")
    myq = 2 * x + y
    pairs = []
    for bi in range(len(srcs)):
        for j, (qx, qy) in enumerate([(1 - x, y), (x, 1 - y), (1 - x, 1 - y)]):
            q = 2 * qx + qy
            mk = lambda src, dst: pltpu.make_async_remote_copy(
                src_ref=src, dst_ref=dst, send_sem=send_sems.at[3 * bi + j], recv_sem=recv_sems.at[3 * bi + j],
                device_id=(qx, qy, c), device_id_type=MESH)
            pairs.append((mk(srcs[bi].at[q], lands[bi].at[myq]), mk(srcs[bi].at[myq], lands[bi].at[q])))
    return pairs


def _chip_exchange_start(bufs, name):
    nbuf = len(bufs)
    hbm, sem = pl.BlockSpec(memory_space=pltpu.HBM), pl.BlockSpec(memory_space=pltpu.SEMAPHORE)

    def body(*refs):
        srcs, lands = refs[:nbuf], refs[nbuf:2 * nbuf]
        send_sems, recv_sems = refs[2 * nbuf], refs[2 * nbuf + 1]
        token = refs[-1]
        for send, _ in _chip_copies(srcs, lands, send_sems, recv_sems):
            send.start()
        token[...] = jnp.zeros_like(token)

    held = [pltpu.HBM(b.shape, b.dtype) for b in bufs]
    outs = pl.pallas_call(
        body, name=name, in_specs=[hbm] * (2 * nbuf), out_specs=(sem, sem, *[hbm] * (2 * nbuf), pl.BlockSpec(memory_space=pltpu.VMEM)),
        out_shape=(pltpu.SemaphoreType.DMA((3 * nbuf,)), pltpu.SemaphoreType.DMA((3 * nbuf,)), *held, *held, _sds((8, 128))),
        input_output_aliases={i: 2 + i for i in range(2 * nbuf)},
        compiler_params=pltpu.CompilerParams(has_side_effects=pltpu.SideEffectType.DATAFLOW_SIDE_EFFECTING),
    )(*[pltpu.with_memory_space_constraint(b, pltpu.HBM) for b in bufs],
      *[pltpu.with_memory_space_constraint(lax.empty(b.shape, b.dtype), pltpu.HBM) for b in bufs])
    return outs[0], outs[1], list(outs[2:2 + nbuf]), list(outs[2 + nbuf:2 + 2 * nbuf]), outs[-1]


def _chip_exchange_wait(send_sems, recv_sems, srcs, lands, after, name):
    nbuf = len(srcs)
    hbm, sem = pl.BlockSpec(memory_space=pltpu.HBM), pl.BlockSpec(memory_space=pltpu.SEMAPHORE)

    def body(*refs):
        src_refs, land_refs = refs[:nbuf], refs[nbuf:2 * nbuf]
        for send, recv in _chip_copies(src_refs, land_refs, refs[2 * nbuf], refs[2 * nbuf + 1]):
            send.wait_send()
            recv.wait_recv()

    held = [pltpu.HBM(b.shape, b.dtype) for b in srcs]
    outs = pl.pallas_call(
        body, name=name, in_specs=[hbm] * (2 * nbuf) + [sem, sem, pl.BlockSpec(memory_space=pl.ANY)], out_specs=[hbm] * (2 * nbuf),
        out_shape=(*held, *held), input_output_aliases={i: i for i in range(2 * nbuf)},
        compiler_params=pltpu.CompilerParams(has_side_effects=pltpu.SideEffectType.DATAFLOW_SIDE_EFFECTING),
    )(*srcs, *lands, send_sems, recv_sems, after)
    return list(outs[:nbuf]), list(outs[nbuf:])


def _prefetch_call(body, name, scalars, grid, in_specs, out_specs, out_shape, args):
    return pl.pallas_call(
        body, name=name, out_shape=out_shape,
        grid_spec=pltpu.PrefetchScalarGridSpec(num_scalar_prefetch=1, grid=grid, in_specs=in_specs, out_specs=out_specs),
        compiler_params=pltpu.CompilerParams(dimension_semantics=("parallel",) * len(grid), vmem_limit_bytes=V7X_VMEM_LIMIT))(scalars, *args)


def _pair_add(g, got, core, name):
    _, rows, cols = got.shape
    tr = _row_tile(rows, 16, 1024)

    def body(c_ref, a_ref, b_ref, o_ref):
        o_ref[...] = (a_ref[...].astype(F32) + b_ref[...].astype(F32)).astype(o_ref.dtype)

    blk = pl.BlockSpec((None, tr, cols), lambda q, i, c_ref: (q, i, 0))
    mine = pl.BlockSpec((None, None, tr, cols), lambda q, i, c_ref: (q, c_ref[0], i, 0))
    return _prefetch_call(body, name, core, (4, rows // tr), [mine, blk], blk, _sds(got.shape, got.dtype),
                          (g.reshape(4, 2, rows, cols), got))


def _reduce_chips(chip_sums, recv, slots, name):
    _, rows, cols = recv.shape
    cw = 256 if cols % 256 == 0 else cols

    def body(s_ref, own_ref, r1_ref, r2_ref, r3_ref, o_ref):
        o_ref[...] = ((own_ref[...].astype(F32) + r1_ref[...].astype(F32)) + r2_ref[...].astype(F32)) + r3_ref[...].astype(F32)

    pick = lambda k: pl.BlockSpec((None, rows, cw), lambda j, s_ref: (s_ref[k], 0, j))
    return _prefetch_call(body, name, slots, (cols // cw,), [pick(0), pick(1), pick(2), pick(3)],
                          pl.BlockSpec((rows, cw), lambda j, s_ref: (0, j)), _sds((rows, cols)), (chip_sums, recv, recv, recv))


def _row_tile(rows, mult, cap):
    best = mult
    for t in range(mult, cap + 1, mult):
        if rows % t == 0:
            best = t
    assert rows % best == 0, (rows, mult)
    return best


def _reduce_sources(r, name):
    nsrc, rows, cols = r.shape
    tr = _row_tile(rows, 16 if r.dtype.itemsize == 2 else 8, 128)

    def body(r_ref, o_ref):
        acc = r_ref[0].astype(F32)
        for j in range(1, nsrc):
            acc = acc + r_ref[j].astype(F32)
        o_ref[...] = acc

    return _call(body, name, (rows // tr,), [pl.BlockSpec((nsrc, tr, cols), lambda i: (0, i, 0))],
                 pl.BlockSpec((tr, cols), lambda i: (i, 0)), _sds((rows, cols)), sem=("parallel",))(r)


def _adamw(w, g, m, v, name):
    rows, cols = w.shape
    tr = _row_tile(rows, 8, 128)
    c1 = 1.0 - ADAM_B1 ** ADAM_STEP
    c2 = 1.0 - ADAM_B2 ** ADAM_STEP

    def body(w_ref, g_ref, m_ref, v_ref, d_ref, mo_ref, vo_ref):
        g_ = g_ref[...]
        m_ = ADAM_B1 * m_ref[...] + (1.0 - ADAM_B1) * g_
        v_ = ADAM_B2 * v_ref[...] + (1.0 - ADAM_B2) * (g_ * g_)
        d_ref[...] = -ADAM_LR * ((m_ / c1) / (jnp.sqrt(v_ / c2) + ADAM_EPS) + ADAM_WD * w_ref[...])
        mo_ref[...] = m_
        vo_ref[...] = v_

    blk = pl.BlockSpec((tr, cols), lambda i: (i, 0))
    return _call(body, name, (rows // tr,), [blk] * 4, [blk] * 3, [_sds((rows, cols))] * 3, sem=("parallel",))(w, g, m, v)


SLAB_PART = 1024


def _pack(arrs):
    parts = []
    for a in arrs:
        flat = a.reshape(-1)
        pad = (-flat.shape[0]) % SLAB_PART
        if pad:
            flat = jnp.concatenate([flat, jnp.zeros((pad,), flat.dtype)])
        parts.append(flat.reshape(-1, 128))
    return jnp.concatenate(parts, axis=0)


def _unpack(slab, shapes):
    outs, row = [], 0
    for s in shapes:
        n = 1
        for d in s:
            n *= d
        rows = (n + SLAB_PART - 1) // SLAB_PART * (SLAB_PART // 128)
        outs.append(slab[row:row + rows].reshape(-1)[:n].reshape(s))
        row += rows
    return outs


def _unshard_last(g, lead):
    nl = len(lead)
    return jnp.transpose(g, tuple(range(1, nl + 1)) + (0, nl + 1)).reshape(tuple(lead) + (-1,))


def kernel(x, c, ctx, c_ctx, ada_w, ada_b, norm_g, w_in, hg_lb, hg_norm_g, rw_mu, rw_w0, rw_w2, rw_a0, rw_a2, rw_kk, rw_ka, rw_rk, rw_gn_g, rw_gn_b, w_hg_out, w_rw_out, w_out, final_g, loss_target, m_c_ctx, m_ada_w, m_ada_b, m_norm_g, m_w_in, m_hg_lb, m_hg_norm_g, m_rw_mu, m_rw_w0, m_rw_w2, m_rw_a0, m_rw_a2, m_rw_kk, m_rw_ka, m_rw_rk, m_rw_gn_g, m_rw_gn_b, m_w_hg_out, m_w_rw_out, m_w_out, m_final_g, v_c_ctx, v_ada_w, v_ada_b, v_norm_g, v_w_in, v_hg_lb, v_hg_norm_g, v_rw_mu, v_rw_w0, v_rw_w2, v_rw_a0, v_rw_a2, v_rw_kk, v_rw_ka, v_rw_rk, v_rw_gn_g, v_rw_gn_b, v_w_hg_out, v_w_rw_out, v_w_out, v_final_g):
    dm = _dims()
    me = 4 * lax.axis_index("x") + 2 * lax.axis_index("y") + lax.axis_index("c")

    sharded_small = [hg_lb, rw_mu[0], rw_w0[0], rw_w2[0], rw_a0[0], rw_a2[0]]
    c_rows = jnp.concatenate([c, jnp.zeros((7, D), F32)], axis=0)
    gathered = _gather_all([w_in[0].T.astype(BF16), _pack(sharded_small), c_rows], "gather_weights")
    win_t = gathered[0].reshape(-1, D)
    per_dev = jax.vmap(lambda s: _unpack(s, [a.shape for a in sharded_small]))(gathered[1])
    hg_lb_f, mu_f, w0_f, w2_f, a0_f, a2_f = [_unshard_last(p, p.shape[1:-1]) for p in per_dev]

    ncol = ada_w.shape[2]
    cc16 = jnp.concatenate([gathered[2][:, 0], c_ctx[None], jnp.zeros((7, D), F32)], axis=0)
    mod_all = _gather_all([_ada_fwd(cc16, ada_w[0])], "gather_mods")[0]
    mod_x = lax.dynamic_index_in_dim(mod_all, me, axis=1, keepdims=False).reshape(1, -1) + ada_b
    mod_c = mod_all[:, NDEV].reshape(1, -1) + ada_b
    mods = jnp.concatenate([mod_x, mod_c, jnp.zeros((6, 3 * D), F32)], axis=0)

    late_flight, token = _direct_gather_start([w_out[0].astype(BF16), w_hg_out[0].T.astype(BF16), w_rw_out[0].T.astype(BF16)],
                                              mod_all, "gather_late_start")
    mods = mods + token[0:1, 0:1]

    def late_weights(after):
        got = _direct_gather_wait(*late_flight, after, "gather_late_wait")
        return dict(zip(("wout", "whg_t", "wrw_t"), [g.reshape(-1, g.shape[2]) for g in got]))

    w = dict(win_t=win_t, late_weights=late_weights, norm_g=norm_g,
             lb0=hg_lb_f[:, 0], lb1=hg_lb_f[:, 1], hg_g=hg_norm_g, mu=mu_f, w0=w0_f, w2=w2_f, a0=a0_f, a2=a2_f,
             kk=rw_kk, ka=rw_ka, rk=rw_rk, gng=rw_gn_g, gnb=rw_gn_b, final_g=final_g[None])
    def start_exchange(grads_bf16, tag):
        names = list(grads_bf16)
        blocks = [grads_bf16[n].reshape(NDEV, -1, grads_bf16[n].shape[1]) for n in names]
        got = _pair_exchange(blocks, "pair_grads_" + tag)
        core = lax.axis_index("c").astype(jnp.int32).reshape(1)
        chip_sums = [_pair_add(g, r, core, "pair_add_" + n) for g, r, n in zip(blocks, got, names)]
        send_sems, recv_sems, held, lands, token = _chip_exchange_start(chip_sums, "chip_grads_start_" + tag)
        return token, (names, send_sems, recv_sems, held, lands)

    loss_dev, grad_x, dmods, in_flight, small = _local_step(x[0], ctx[0], loss_target[0], mods, w, start_exchange)
    loss = lax.psum(loss_dev[0, 0], AXES)

    my_chip = (2 * lax.axis_index("x") + lax.axis_index("y")).astype(jnp.int32)
    others = jnp.arange(3, dtype=jnp.int32)
    slots = jnp.concatenate([my_chip.reshape(1), others + (others >= my_chip).astype(jnp.int32)])

    def finish_exchange(tag, flight, after):
        names, *copies = flight
        chip_sums, recv = _chip_exchange_wait(*copies, after, "chip_grads_wait_" + tag)
        return {n: _reduce_chips(cs, r, slots, "reduce_" + n) for n, cs, r in zip(names, chip_sums, recv)}

    reduced = finish_exchange("early", in_flight[0], grad_x)

    dmods_all = _gather_all([dmods], "gather_dmods")[0]
    my_cols = lambda a: lax.dynamic_slice_in_dim(a, me * ncol, ncol, axis=1)
    g_ada, dcc16, d_ada_b = _ada_bwd(cc16, ada_w[0], my_cols(dmods_all[:, 0]), my_cols(dmods_all[:, 1]), dmods)
    small = dict(small, c_ctx=dcc16[NDEV:NDEV + 1], ada_b=d_ada_b)

    small_names = ["c_ctx", "ada_b", "norm_g", "lb0", "lb1", "hg_g", "mu", "w0", "a0", "kk", "ka", "rk", "gng", "gnb", "final_g"]
    lora_names = ["w2", "a2"]
    small_all, lora_all = _gather_all([_pack([small[n] for n in small_names]), _pack([small[n] for n in lora_names]).astype(BF16)],
                                      "gather_small_grads")
    small_sum = _reduce_sources(small_all, "reduce_small")
    sg = dict(zip(small_names, _unpack(small_sum, [small[n].shape for n in small_names])))
    sg.update(zip(lora_names, _unpack(_reduce_sources(lora_all, "reduce_lora"), [small[n].shape for n in lora_names])))

    reduced.update(finish_exchange("late", in_flight[1], small_sum))
    g_win_t, g_wout, g_whg_t, g_wrw_t = [reduced[n] for n in ("win_t", "wout", "whg_t", "wrw_t")]

    def my_shard(full):
        n = full.shape[-1] // NDEV
        return lax.dynamic_slice_in_dim(full, me * n, n, axis=full.ndim - 1)

    grads = dict(
        c_ctx=sg["c_ctx"][0], ada_w=g_ada[None], ada_b=sg["ada_b"], norm_g=sg["norm_g"], w_in=g_win_t.T[None],
        hg_lb=my_shard(jnp.stack([sg["lb0"], sg["lb1"]], axis=1)), hg_norm_g=sg["hg_g"], rw_mu=my_shard(sg["mu"])[None],
        rw_w0=my_shard(sg["w0"])[None], rw_w2=my_shard(sg["w2"])[None], rw_a0=my_shard(sg["a0"])[None], rw_a2=my_shard(sg["a2"])[None],
        rw_kk=sg["kk"], rw_ka=sg["ka"], rw_rk=sg["rk"], rw_gn_g=sg["gng"], rw_gn_b=sg["gnb"],
        w_hg_out=g_whg_t.T[None], w_rw_out=g_wrw_t.T[None], w_out=g_wout[None], final_g=sg["final_g"][0])
    weights = dict(c_ctx=c_ctx, ada_w=ada_w, ada_b=ada_b, norm_g=norm_g, w_in=w_in, hg_lb=hg_lb, hg_norm_g=hg_norm_g, rw_mu=rw_mu,
                   rw_w0=rw_w0, rw_w2=rw_w2, rw_a0=rw_a0, rw_a2=rw_a2, rw_kk=rw_kk, rw_ka=rw_ka, rw_rk=rw_rk, rw_gn_g=rw_gn_g,
                   rw_gn_b=rw_gn_b, w_hg_out=w_hg_out, w_rw_out=w_rw_out, w_out=w_out, final_g=final_g)
    m_in = dict(zip(weights, (m_c_ctx, m_ada_w, m_ada_b, m_norm_g, m_w_in, m_hg_lb, m_hg_norm_g, m_rw_mu, m_rw_w0, m_rw_w2, m_rw_a0,
                              m_rw_a2, m_rw_kk, m_rw_ka, m_rw_rk, m_rw_gn_g, m_rw_gn_b, m_w_hg_out, m_w_rw_out, m_w_out, m_final_g)))
    v_in = dict(zip(weights, (v_c_ctx, v_ada_w, v_ada_b, v_norm_g, v_w_in, v_hg_lb, v_hg_norm_g, v_rw_mu, v_rw_w0, v_rw_w2, v_rw_a0,
                              v_rw_a2, v_rw_kk, v_rw_ka, v_rw_rk, v_rw_gn_g, v_rw_gn_b, v_w_hg_out, v_w_rw_out, v_w_out, v_final_g)))

    big_w = ["ada_w", "w_in", "w_hg_out", "w_rw_out", "w_out"]
    delta, new_m, new_v = {}, {}, {}
    for n in big_w:
        shp = weights[n].shape
        two = lambda a: a.reshape(shp[-2], shp[-1])
        d_, m_, v_ = _adamw(two(weights[n]), two(grads[n]), two(m_in[n]), two(v_in[n]), "adamw_" + n)
        delta[n], new_m[n], new_v[n] = d_.reshape(shp), m_.reshape(shp), v_.reshape(shp)
    rest = [n for n in weights if n not in big_w]
    shapes = [weights[n].shape for n in rest]
    d_s, m_s, v_s = _adamw(_pack([weights[n] for n in rest]), _pack([grads[n] for n in rest]), _pack([m_in[n] for n in rest]),
                           _pack([v_in[n] for n in rest]), "adamw_small")
    for n, d_, m_, v_ in zip(rest, _unpack(d_s, shapes), _unpack(m_s, shapes), _unpack(v_s, shapes)):
        delta[n], new_m[n], new_v[n] = d_, m_, v_

    order = list(weights)
    return (loss, grad_x[None], *[grads[n] for n in order], *[delta[n] for n in order],
            *[new_m[n] for n in order], *[new_v[n] for n in order])
```

```python
import functools

import jax
import jax.numpy as jnp
from jax import lax
from jax.experimental import pallas as pl
from jax.experimental.pallas import tpu as pltpu

D = 2048
T = 2048
TC = 256
GW = 64
C = 64
HGW = 1024
RWW = 1024
LORA = 64
NDEV = 8
TM = 256
TMV = 128
NORM_EPS = 1e-6
RW_GN_EPS = 64e-5
ADAM_LR, ADAM_B1, ADAM_B2, ADAM_EPS, ADAM_WD, ADAM_STEP = 0.001, 0.9, 0.999, 1e-08, 0.01, 10
WB = 256
V7X_VMEM_LIMIT = 56 * 1024 * 1024

F32 = jnp.float32
BF16 = jnp.bfloat16
AXES = ("x", "y", "c")


def _dims():
    tt = T + TC
    sh = 3 * RWW + 4 * LORA
    hgc = 5 * HGW
    return dict(TT=tt, NCC=TC // C, NCH=tt // C, HG_H=HGW // 128, RW_P=RWW // 128, SH=sh, HGC=hgc,
                NCOLS=hgc + sh + RWW + 2 * D, NCT=TC // TM, NCTV=TC // TMV)


def _mm(a, b):
    return _dot3(a, b, "nn")


def _mm_nt(a, b):
    return _dot3(a, b, "nt")


def _mm_tn(a, b):
    return _dot3(a, b, "tn")


def _split(x):
    hi = x.astype(BF16)
    return hi, (x - hi.astype(F32)).astype(BF16)


_FORMS = {"nn": (((1,), (0,)), ((), ())), "nt": (((1,), (1,)), ((), ())), "tn": (((0,), (0,)), ((), ()))}


def _dot3_raw(a, b, form):
    ah, al = _split(a)
    bh, bl = _split(b)
    d = lambda x, y: lax.dot_general(x, y, _FORMS[form], preferred_element_type=F32)
    return d(ah, bh) + (d(ah, bl) + d(al, bh))


@functools.partial(jax.custom_vjp, nondiff_argnums=(2,))
def _dot3(a, b, form):
    return _dot3_raw(a, b, form)


def _dot3_fwd(a, b, form):
    return _dot3_raw(a, b, form), (a, b)


def _dot3_bwd(form, res, g):
    a, b = res
    if form == "nn":
        return _dot3(g, b, "nt"), _dot3(a, g, "tn")
    if form == "nt":
        return _dot3(g, b, "nn"), _dot3(g, a, "tn")
    return _dot3(b, g, "nt"), _dot3(a, g, "nn")


_dot3.defvjp(_dot3_fwd, _dot3_bwd)


def _scan_cumsum(inc, x):
    return _cumsum_vjp(inc.astype(BF16), x)


def _cumsum_raw(inc, x, form):
    h1 = x.astype(BF16)
    r1 = x - h1.astype(F32)
    h2 = r1.astype(BF16)
    h3 = (r1 - h2.astype(F32)).astype(BF16)
    d = lambda y: lax.dot_general(inc, y, _FORMS[form], preferred_element_type=F32)
    return d(h1) + (d(h2) + d(h3))


@jax.custom_vjp
def _cumsum_vjp(inc, x):
    return _cumsum_raw(inc, x, "nn")


_cumsum_vjp.defvjp(lambda inc, x: (_cumsum_raw(inc, x, "nn"), inc),
                   lambda inc, g: (jnp.zeros_like(inc), _cumsum_raw(inc, g, "tn")))


def _silu(x):
    return x * jax.nn.sigmoid(x)


def _softplus(x):
    return jnp.maximum(x, 0.0) + jnp.log(1.0 + jnp.exp(-jnp.abs(x)))


def _iota2(shape, dim):
    return lax.broadcasted_iota(jnp.int32, shape, dim)


def _pair_mask():
    return (_iota2((128, 128), 0) < 64) == (_iota2((128, 128), 1) < 64)


def _seg64_raw(x):
    e = _pair_mask().astype(BF16)
    parts = []
    for g in range(x.shape[1] // 128):
        hi, lo = _split(x[:, g * 128:(g + 1) * 128])
        parts.append(jnp.dot(hi, e, preferred_element_type=F32) + jnp.dot(lo, e, preferred_element_type=F32))
    return parts[0] if len(parts) == 1 else jnp.concatenate(parts, axis=1)


@jax.custom_vjp
def _seg64_sum(x):
    return _seg64_raw(x)


_seg64_sum.defvjp(lambda x: (_seg64_raw(x), None), lambda _, g: (_seg64_raw(g),))


def _mm1(a, b):
    return jnp.dot(a.astype(BF16), b.astype(BF16), preferred_element_type=F32)


def _mm1_nt(a, b):
    return lax.dot_general(a.astype(BF16), b.astype(BF16), (((1,), (1,)), ((), ())), preferred_element_type=F32)


def _seg128_mean(x):
    parts = [jnp.broadcast_to(jnp.mean(x[:, g * 128:(g + 1) * 128], axis=1, keepdims=True), (x.shape[0], 128))
             for g in range(x.shape[1] // 128)]
    return parts[0] if len(parts) == 1 else jnp.concatenate(parts, axis=1)


def _scan_masks(d):
    lag = (_iota2((C, C), 0) - _iota2((C, C), 1)) * (1 - 2 * d)
    return lag >= 0, lag > 0


def _tri_inverse(a):
    eye = (_iota2((C, C), 0) == _iota2((C, C), 1)).astype(F32)
    p = [-x for x in a]
    tm = [eye + x for x in p]
    n = 2
    while n < C:
        p = [_mm(x, x) for x in p]
        tm = [t + _mm(t, x) for t, x in zip(tm, p)]
        n *= 2
    return tm


@jax.custom_vjp
def _tri_solve(tm, a, rhs):
    return [_mm(t, r) for t, r in zip(tm, rhs)]


def _tri_solve_fwd(tm, a, rhs):
    u = [_mm(t, r) for t, r in zip(tm, rhs)]
    return u, (tm, u)


def _tri_solve_bwd(res, g):
    tm, u = res
    d_rhs = [_mm_tn(t, x) for t, x in zip(tm, g)]
    return [jnp.zeros_like(t) for t in tm], [-_mm_nt(d, x) for d, x in zip(d_rhs, u)], d_rhs


_tri_solve.defvjp(_tri_solve_fwd, _tri_solve_bwd)


def _rw_chunks(chains, tms=None):
    s0, r, lw, kap, b, v, kd, ds = zip(*chains)
    ids = range(len(chains))
    inc, strict = zip(*[_scan_masks(d) for d in ds])
    lane = _iota2((1, 128), 1)
    hm = ((lane < 64).astype(F32), (lane >= 64).astype(F32))
    lc = [_scan_cumsum(inc[i], lw[i]) for i in ids]
    ltot = [jnp.sum(lw[i], axis=0, keepdims=True) for i in ids]
    rt = [r[i] * jnp.exp(lc[i]) for i in ids]
    kt = [kap[i] * jnp.exp(lc[i] - lw[i]) for i in ids]
    einv = [jnp.exp(-lc[i]) for i in ids]
    kh = [kd[i] * einv[i] for i in ids]
    bh = [b[i] * einv[i] for i in ids]
    eend = [jnp.exp(ltot[i] - lc[i]) for i in ids]
    kbar = [kd[i] * eend[i] for i in ids]
    bbar = [b[i] * eend[i] for i in ids]
    lhs = [jnp.concatenate([kt[i] * hm[0], kt[i] * hm[1], rt[i] * hm[0], rt[i] * hm[1]], axis=0) for i in ids]
    gk = [_mm_nt(lhs[i], kh[i]) for i in ids]
    gb = [_mm_nt(lhs[i], bh[i]) for i in ids]
    ks = [_mm_nt(kt[i], s0[i]) for i in ids]
    ys = [_mm_nt(rt[i], s0[i]) for i in ids]
    heads = [(i, h) for i in ids for h in range(2)]
    akk = {c: jnp.where(strict[c[0]], gk[c[0]][c[1] * C:(c[1] + 1) * C], 0.0) for c in heads}
    akb = {c: jnp.where(strict[c[0]], gb[c[0]][c[1] * C:(c[1] + 1) * C], 0.0) for c in heads}
    ark = {c: jnp.where(inc[c[0]], gk[c[0]][(2 + c[1]) * C:(3 + c[1]) * C], 0.0) for c in heads}
    arb = {c: jnp.where(inc[c[0]], gb[c[0]][(2 + c[1]) * C:(3 + c[1]) * C], 0.0) for c in heads}
    vh = {c: v[c[0]] * hm[c[1]] for c in heads}
    av = {c: _mm(akk[c], vh[c]) for c in heads}
    rhs = {c: ks[c[0]] * hm[c[1]] + av[c] for c in heads}
    akb_list = [akb[c] for c in heads]
    tm = _tri_inverse(akb_list) if tms is None else [tms[c] for c in heads]
    uh = dict(zip(heads, _tri_solve(tm, akb_list, [rhs[c] for c in heads])))
    yv = {c: _mm(ark[c], vh[c]) for c in heads}
    yu = {c: _mm(arb[c], uh[c]) for c in heads}
    u = [uh[i, 0] + uh[i, 1] for i in ids]
    y = [ys[i] + (yv[i, 0] - yu[i, 0]) + (yv[i, 1] - yu[i, 1]) for i in ids]
    upd = [_mm_tn(jnp.concatenate([v[i], -u[i]], axis=0), jnp.concatenate([kbar[i], bbar[i]], axis=0)) for i in ids]
    s1 = [s0[i] * jnp.exp(ltot[i]) + jnp.where(_pair_mask(), upd[i], 0.0) for i in ids]
    if tms is None:
        return [(y[i], s1[i], tm[2 * i], tm[2 * i + 1]) for i in ids]
    return [(y[i], s1[i]) for i in ids]


HG_SUB = 32


def _hg_chunks(chains):
    s0, qr, iv, f, lb, ds = zip(*chains)
    ids = range(len(chains))
    inc = [_scan_masks(d)[0] for d in ds]
    q = [_silu(qr[i]) for i in ids]
    fg = [lb[i] + (1.0 - lb[i]) * jax.nn.sigmoid(f[i]) for i in ids]
    k = [1.0 - fg[i] for i in ids]
    g = [jnp.log(fg[i]) for i in ids]
    bc = [_scan_cumsum(inc[i], g[i]) for i in ids]
    btot = [jnp.sum(g[i], axis=0, keepdims=True) for i in ids]
    o_inter = [_mm_nt(q[i] * jnp.exp(bc[i]), s0[i]) for i in ids]
    rowi = _iota2((C, 1), 0)
    outs = [[] for _ in ids]
    for blk in range(C // HG_SUB):
        lo, hi = blk * HG_SUB, (blk + 1) * HG_SUB
        first = [lo if ds[i] == 0 else hi - 1 for i in ids]
        ref = [jnp.sum(jnp.where(rowi == first[i], bc[i] - g[i], 0.0), axis=0, keepdims=True) for i in ids]
        qi = [q[i][lo:hi] * jnp.exp(bc[i][lo:hi] - ref[i]) for i in ids]
        src = [(rowi < hi) if ds[i] == 0 else (rowi >= lo) for i in ids]
        ke = [jnp.where(src[i], k[i] * jnp.exp(jnp.where(src[i], ref[i] - bc[i], 0.0)), 0.0) for i in ids]
        a = [jnp.where(inc[i][lo:hi], _mm_nt(qi[i], ke[i]), 0.0) for i in ids]
        part = [_mm(a[i], iv[i]) for i in ids]
        for i in ids:
            outs[i].append(part[i])
    o = [o_inter[i] + jnp.concatenate(outs[i], axis=0) for i in ids]
    upd = [_mm_tn(iv[i], k[i] * jnp.exp(btot[i] - bc[i])) for i in ids]
    s1 = [s0[i] * jnp.exp(btot[i]) + upd[i] for i in ids]
    return [(o[i], s1[i]) for i in ids]


def _lockstep(fn, ds, nargs):
    def flat_fn(*flat):
        return tuple(fn([tuple(flat[i * nargs:(i + 1) * nargs]) + (d,) for i, d in enumerate(ds)]))

    return flat_fn


def _h_math(x, ng, scale, shift):
    return x * lax.rsqrt(jnp.mean(x * x, axis=-1, keepdims=True) + NORM_EPS) * ng * (1.0 + scale) + shift


def _hg_post_math(o0, o1, z, g):
    o = o0 + o1
    on = o * lax.rsqrt(_seg128_mean(o * o) + NORM_EPS)
    return on * g * _silu(z)


def _rw_prep_math(k, pw0, pw1, pa0, pa1, w0, a0, kkp, kap_p):
    kk = k * kkp
    kap = kk * lax.rsqrt(_seg64_sum(kk * kk) + 1e-12)
    outs = []
    for d, (pw, pa) in enumerate(((pw0, pa0), (pw1, pa1))):
        w_log = -_softplus(-(w0[d:d + 1] + pw)) - 0.5
        lw = -jnp.exp(w_log)
        a = jax.nn.sigmoid(a0[d:d + 1] + pa)
        kd = k * (1.0 + (a - 1.0) * kap_p)
        outs += [lw, kap * a, kd]
    return (kap, *outs)


def _rw_post_math(y0, y1, kd0, kd1, r, v, z, rk, gng, gnb):
    ys, ksum = y0 + y1, kd0 + kd1
    mean = _seg64_sum(ys) * (1.0 / 64.0)
    cen = ys - mean
    var = _seg64_sum(cen * cen) * (1.0 / 64.0)
    yn = cen * lax.rsqrt(var + RW_GN_EPS) * gng + gnb
    bonus = _seg64_sum(r * ksum * rk) * v
    return (yn + bonus) * _silu(z)


def _head_math(x, mo, gate, fg, tgt):
    xo = x + gate * mo
    y = xo * lax.rsqrt(jnp.mean(xo * xo, axis=-1, keepdims=True) + NORM_EPS) * fg
    err = y - tgt
    return 0.5 * jnp.sum(jnp.mean(err * err, axis=-1, keepdims=True), axis=0, keepdims=True)


def _gate_math(ghg, grw, a, b):
    return jax.nn.sigmoid(ghg) * a + jax.nn.sigmoid(grw) * b


def _call(body, name, grid, in_specs, out_specs, out_shape, scratch=(), sem=None, aliases=None):
    return pl.pallas_call(
        body, name=name, grid=grid, in_specs=in_specs, out_specs=out_specs, out_shape=out_shape,
        scratch_shapes=list(scratch), input_output_aliases=aliases or {},
        compiler_params=pltpu.CompilerParams(dimension_semantics=sem, vmem_limit_bytes=V7X_VMEM_LIMIT))


def _full(shape):
    nd = len(shape)
    return pl.BlockSpec(shape, lambda *_: (0,) * nd)


def _sds(shape, dtype=F32):
    return jax.ShapeDtypeStruct(shape, dtype)


def _proj_nt(h, wt, wt_blk0, nblk, row0, nrows, name):
    def body(h_ref, w_ref, o_ref):
        o_ref[...] = lax.dot_general(h_ref[row0:row0 + nrows, :], w_ref[...], (((1,), (1,)), ((), ())), preferred_element_type=F32)

    return _call(body, name, (nblk,), [_full(h.shape), pl.BlockSpec((WB, wt.shape[1]), lambda j: (wt_blk0 + j, 0))],
                 pl.BlockSpec((nrows, WB), lambda j: (0, j)), _sds((nrows, nblk * WB)), sem=("parallel",))(h, wt)


def _dw_tn(a, b, row0, out_rows, out_blk0, name, prev=None):
    m, n = a.shape
    k2 = b.shape[1]
    nblk = n // WB

    def body(*refs):
        a_ref, b_ref, o_ref = refs[0], refs[1], refs[-1]
        o_ref[...] = lax.dot_general(a_ref[...], b_ref[row0:row0 + m, :], (((0,), (0,)), ((), ())),
                                     preferred_element_type=F32).astype(BF16)

    in_specs = [pl.BlockSpec((m, WB), lambda j: (0, j)), _full(b.shape)]
    args = [a, b]
    aliases = None
    if prev is not None:
        in_specs.append(pl.BlockSpec(memory_space=pl.ANY))
        args.append(prev)
        aliases = {2: 0}
    return _call(body, name, (nblk,), in_specs, pl.BlockSpec((WB, k2), lambda j: (out_blk0 + j, 0)), _sds((out_rows, k2), BF16),
                 sem=("arbitrary",), aliases=aliases)(*args)


def _mm_nn(a, w, w_blk0, name, after=None):
    m, kc = a.shape
    n = w.shape[1]
    kb = max(k for k in (WB, 2 * WB, 4 * WB) if kc % k == 0 and (w_blk0 * WB) % k == 0)
    w_blk0 = w_blk0 * WB // kb
    tmb = 3 * TM if m % (3 * TM) == 0 else (2 * TM if m % (2 * TM) == 0 else TM)

    def body(a_ref, w_ref, *rest):
        o_ref = rest[-1]

        @pl.when(pl.program_id(1) == 0)
        def _():
            o_ref[...] = jnp.zeros_like(o_ref)

        o_ref[...] += jnp.dot(a_ref[...], w_ref[...], preferred_element_type=F32)

    extra = [] if after is None else [pl.BlockSpec(memory_space=pl.ANY)]
    return _call(body, name, (m // tmb, kc // kb),
                 [pl.BlockSpec((tmb, kb), lambda i, k: (i, k)), pl.BlockSpec((kb, n), lambda i, k: (w_blk0 + k, 0))] + extra,
                 pl.BlockSpec((tmb, n), lambda i, k: (i, 0)), _sds((m, n)), sem=("parallel", "arbitrary"))(
        a, w, *([] if after is None else [after]))


def _ada_fwd(cc16, ada_shard):
    ncol = ada_shard.shape[1]

    def body(c_ref, w_ref, o_ref):
        o_ref[...] = _mm(_silu(c_ref[...]), w_ref[...])

    return _call(body, "ada_fwd", (1,), [_full((16, D)), _full((D, ncol))], _full((16, ncol)), _sds((16, ncol)))(cc16, ada_shard)


def _ada_bwd(cc16, ada_shard, dm_x, dm_c, dmods):
    ncol = ada_shard.shape[1]

    def body(c_ref, w_ref, dx_ref, dc_ref, dm_ref, dw_ref, dcc_ref, db_ref):
        s, vjp = jax.vjp(_silu, c_ref[...])
        dc_tot = dc_ref[0:1, :]
        for j in range(1, NDEV):
            dc_tot = dc_tot + dc_ref[j:j + 1, :]
        row = _iota2((8, 1), 0)
        dm16 = jnp.concatenate([dx_ref[...], jnp.where(row == 0, dc_tot, 0.0)], axis=0)
        dw_ref[...] = _mm_tn(s, dm16)
        dcc_ref[...] = vjp(_mm_nt(dm16, w_ref[...]))[0]
        db_ref[...] = dm_ref[0:1, :] + dm_ref[1:2, :]

    return _call(body, "ada_bwd", (1,), [_full((16, D)), _full((D, ncol)), _full((8, ncol)), _full((8, ncol)), _full((8, 3 * D))],
                 [_full((D, ncol)), _full((16, D)), _full((1, 3 * D))], [_sds((D, ncol)), _sds((16, D)), _sds((1, 3 * D))])(
        cc16, ada_shard, dm_x, dm_c, dmods)


def _sel_mod(mods_ref, is_ctx, lo):
    return jnp.where(is_ctx, mods_ref[1:2, lo:lo + D], mods_ref[0:1, lo:lo + D])


def _token_specs():
    nct = _dims()["NCT"]
    return (pl.BlockSpec((TM, D), lambda i: (jnp.minimum(i, nct - 1), 0)),
            pl.BlockSpec((TM, D), lambda i: (jnp.maximum(i - nct, 0), 0)))


def _h_fwd(ctx, x, mods, norm_g):
    dm = _dims()

    def body(c_ref, x_ref, m_ref, g_ref, o_ref):
        is_ctx = pl.program_id(0) < dm["NCT"]
        tok = jnp.where(is_ctx, c_ref[...], x_ref[...])
        o_ref[...] = _h_math(tok, g_ref[...], _sel_mod(m_ref, is_ctx, D), _sel_mod(m_ref, is_ctx, 0)).astype(BF16)

    return _call(body, "h_fwd", (dm["TT"] // TM,), [*_token_specs(), _full((8, 3 * D)), _full((1, D))],
                 pl.BlockSpec((TM, D), lambda i: (i, 0)), _sds((dm["TT"], D), BF16), sem=("parallel",))(ctx, x, mods, norm_g)


def _chunk_of(d, s):
    dm = _dims()
    ncc, nch = dm["NCC"], dm["NCH"]
    return s if d == 0 else jnp.where(s < ncc, ncc - 1 - s, nch - 1 - (s - ncc))


HG_GROUP = 8
RW_GROUP = 8


def _dir_specs(make):
    return [make(d) for d in range(2)]


def _lanes(j):
    return slice(j * 128, (j + 1) * 128)


def _hg_scan_fwd(p_hg, lb3):
    dm = _dims()
    nh, nch, tt = dm["HG_H"], dm["NCH"], dm["TT"]
    pp = min(HG_GROUP, nh)
    lw_ = 128 * pp

    def body(q0, q1, i0, i1, f0, f1, lb0, lb1, o0, o1, ck_ref, s_ref):
        @pl.when(pl.program_id(1) == 0)
        def _():
            s_ref[...] = jnp.zeros_like(s_ref)

        keys, chains = [], []
        for d, refs in enumerate(((q0, i0, f0, lb0), (q1, i1, f1, lb1))):
            vals = [r[...] for r in refs]
            for j in range(pp):
                keys.append((d, j))
                chains.append((s_ref[d, j], *[a[:, _lanes(j)] for a in vals], d))
        for (d, j), chain, (out, s1) in zip(keys, chains, _hg_chunks(chains)):
            ck_ref[d, j] = chain[0]
            (o0, o1)[d][:, _lanes(j)] = out
            s_ref[d, j] = s1

    blk = lambda off: _dir_specs(lambda d: pl.BlockSpec((C, lw_), lambda h, s: (_chunk_of(d, s), off * nh // pp + h)))
    return _call(body, "hg_scan_fwd", (nh // pp, nch),
                 blk(0) + blk(1) + _dir_specs(lambda d: pl.BlockSpec((C, lw_), lambda h, s: (_chunk_of(d, s), (2 + d) * nh // pp + h)))
                 + _dir_specs(lambda d: pl.BlockSpec((None, 1, lw_), lambda h, s: (d, 0, h))),
                 _dir_specs(lambda d: pl.BlockSpec((C, lw_), lambda h, s: (_chunk_of(d, s), h)))
                 + [pl.BlockSpec((2, pp, None, 128, 128), lambda h, s: (0, h, s, 0, 0))],
                 [_sds((tt, HGW)), _sds((tt, HGW)), _sds((2, nh, nch, 128, 128))],
                 scratch=[pltpu.VMEM((2, pp, 128, 128), F32)], sem=("parallel", "arbitrary"))(*([p_hg] * 6), lb3, lb3)


def _hg_scan_bwd(p_hg, lb3, ck, do):
    dm = _dims()
    nh, nch, tt = dm["HG_H"], dm["NCH"], dm["TT"]
    pp = min(HG_GROUP, nh)
    lw_ = 128 * pp

    def body(q0, q1, i0, i1, f0, f1, lb0, lb1, ck_ref, do0, do1, dq0, dq1, di0, di1, df0, df1, dlb_ref, ds_ref):
        @pl.when(pl.program_id(1) == 0)
        def _():
            ds_ref[...] = jnp.zeros_like(ds_ref)
            dlb_ref[...] = jnp.zeros_like(dlb_ref)

        keys, flat, cts = [], [], []
        for d, refs in enumerate(((q0, i0, f0, lb0, do0), (q1, i1, f1, lb1, do1))):
            *vals, do_ = [r[...] for r in refs]
            for j in range(pp):
                keys.append((d, j))
                flat += [ck_ref[d, j], *[a[:, _lanes(j)] for a in vals]]
                cts.append((do_[:, _lanes(j)], ds_ref[d, j]))
        _, vjp = jax.vjp(_lockstep(_hg_chunks, [d for d, _ in keys], 5), *flat)
        grads = vjp(tuple(cts))
        for n, (d, j) in enumerate(keys):
            ds0, dq_, di_, df_, dlb = grads[5 * n:5 * n + 5]
            (dq0, dq1)[d][:, _lanes(j)] = dq_
            (di0, di1)[d][:, _lanes(j)] = di_
            (df0, df1)[d][:, _lanes(j)] = df_
            dlb_ref[d, :, _lanes(j)] += dlb
            ds_ref[d, j] = ds0

    ch = lambda d, s: _chunk_of(d, nch - 1 - s)
    blk = lambda off: _dir_specs(lambda d: pl.BlockSpec((C, lw_), lambda h, s: (ch(d, s), off * nh // pp + h)))
    tok = _dir_specs(lambda d: pl.BlockSpec((C, lw_), lambda h, s: (ch(d, s), h)))
    return _call(body, "hg_scan_bwd", (nh // pp, nch),
                 blk(0) + blk(1) + _dir_specs(lambda d: pl.BlockSpec((C, lw_), lambda h, s: (ch(d, s), (2 + d) * nh // pp + h)))
                 + _dir_specs(lambda d: pl.BlockSpec((None, 1, lw_), lambda h, s: (d, 0, h)))
                 + [pl.BlockSpec((2, pp, None, 128, 128), lambda h, s: (0, h, nch - 1 - s, 0, 0))] + tok,
                 tok * 3 + [pl.BlockSpec((2, 1, lw_), lambda h, s: (0, 0, h))],
                 [_sds((tt, HGW))] * 6 + [_sds((2, 1, HGW))],
                 scratch=[pltpu.VMEM((2, pp, 128, 128), F32)], sem=("parallel", "arbitrary"))(*([p_hg] * 6), lb3, lb3, ck, do, do)


def _rw_scan_fwd(sh, kap, lw, b, kd):
    dm = _dims()
    npair, nch, tt = dm["RW_P"], dm["NCH"], dm["TT"]
    pp = min(RW_GROUP, npair)
    lw_ = 128 * pp

    def body(r0, r1, v0, v1, ka0, ka1, lw0, lw1, b0, b1, kd0, kd1, y0, y1, ck_ref, tm_ref, s_ref):
        @pl.when(pl.program_id(1) == 0)
        def _():
            s_ref[...] = jnp.zeros_like(s_ref)

        keys, chains = [], []
        for d, refs in enumerate(((r0, lw0, ka0, b0, v0, kd0), (r1, lw1, ka1, b1, v1, kd1))):
            vals = [r[...] for r in refs]
            for j in range(pp):
                keys.append((d, j))
                chains.append((s_ref[d, j], *[a[:, _lanes(j)] for a in vals], d))
        for (d, j), chain, (out, s1, tm_a, tm_b) in zip(keys, chains, _rw_chunks(chains)):
            ck_ref[d, j] = chain[0]
            tm_ref[d, j, 0] = tm_a
            tm_ref[d, j, 1] = tm_b
            (y0, y1)[d][:, _lanes(j)] = out
            s_ref[d, j] = s1

    two = lambda off: _dir_specs(lambda d: pl.BlockSpec((C, lw_), lambda p, s: (_chunk_of(d, s), off * npair // pp + p)))
    three = _dir_specs(lambda d: pl.BlockSpec((None, C, lw_), lambda p, s: (d, _chunk_of(d, s), p)))
    return _call(body, "rw_scan_fwd", (npair // pp, nch), two(0) + two(2) + two(0) + three * 3,
                 two(0) + [pl.BlockSpec((2, pp, None, 128, 128), lambda p, s: (0, p, s, 0, 0)),
                           pl.BlockSpec((2, pp, None, 2, C, C), lambda p, s: (0, p, s, 0, 0, 0))],
                 [_sds((tt, RWW)), _sds((tt, RWW)), _sds((2, npair, nch, 128, 128)), _sds((2, npair, nch, 2, C, C))],
                 scratch=[pltpu.VMEM((2, pp, 128, 128), F32)], sem=("parallel", "arbitrary"))(
        sh, sh, sh, sh, kap, kap, lw, lw, b, b, kd, kd)


def _rw_scan_bwd(sh, kap, lw, b, kd, ck, tm_ck, dy):
    dm = _dims()
    npair, nch, tt = dm["RW_P"], dm["NCH"], dm["TT"]
    pp = min(RW_GROUP, npair)
    lw_ = 128 * pp

    def body(r0, r1, v0, v1, ka0, ka1, lw0, lw1, b0, b1, kd0, kd1, ck_ref, tm_ref, dy0, dy1,
             dr0, dr1, dv0, dv1, dka0, dka1, dlw0, dlw1, db0, db1, dkd0, dkd1, ds_ref):
        @pl.when(pl.program_id(1) == 0)
        def _():
            ds_ref[...] = jnp.zeros_like(ds_ref)

        keys, flat, cts, tms = [], [], [], {}
        for d, refs in enumerate(((r0, lw0, ka0, b0, v0, kd0, dy0), (r1, lw1, ka1, b1, v1, kd1, dy1))):
            *vals, dy_ = [r[...] for r in refs]
            for j in range(pp):
                tms[len(keys), 0], tms[len(keys), 1] = tm_ref[d, j, 0], tm_ref[d, j, 1]
                keys.append((d, j))
                flat += [ck_ref[d, j], *[a[:, _lanes(j)] for a in vals]]
                cts.append((dy_[:, _lanes(j)], ds_ref[d, j]))
        _, vjp = jax.vjp(_lockstep(functools.partial(_rw_chunks, tms=tms), [d for d, _ in keys], 7), *flat)
        grads = vjp(tuple(cts))
        outs = ((dr0, dlw0, dka0, db0, dv0, dkd0), (dr1, dlw1, dka1, db1, dv1, dkd1))
        for n, (d, j) in enumerate(keys):
            ds0, *g_in = grads[7 * n:7 * n + 7]
            for o_ref, g in zip(outs[d], g_in):
                o_ref[:, _lanes(j)] = g
            ds_ref[d, j] = ds0

    ch = lambda d, s: _chunk_of(d, nch - 1 - s)
    two = lambda off: _dir_specs(lambda d: pl.BlockSpec((C, lw_), lambda p, s: (ch(d, s), off * npair // pp + p)))
    three = _dir_specs(lambda d: pl.BlockSpec((None, C, lw_), lambda p, s: (d, ch(d, s), p)))
    return _call(body, "rw_scan_bwd", (npair // pp, nch),
                 two(0) + two(2) + two(0) + three * 3
                 + [pl.BlockSpec((2, pp, None, 128, 128), lambda p, s: (0, p, nch - 1 - s, 0, 0)),
                    pl.BlockSpec((2, pp, None, 2, C, C), lambda p, s: (0, p, nch - 1 - s, 0, 0, 0))] + two(0),
                 two(0) * 6, [_sds((tt, RWW))] * 12,
                 scratch=[pltpu.VMEM((2, pp, 128, 128), F32)], sem=("parallel", "arbitrary"))(
        sh, sh, sh, sh, kap, kap, lw, lw, b, b, kd, kd, ck, tm_ck, dy, dy)


def _tile_pos(i):
    dm = _dims()
    is_ctx = i < dm["NCT"]
    rows = lax.broadcasted_iota(jnp.int32, (TM, 1), 0)
    tl = rows + (i - dm["NCT"]) * TM
    width = jnp.where(is_ctx, TC, GW)
    assert TC & (TC - 1) == 0 and GW & (GW - 1) == 0 and TM % GW == 0
    colp = rows & (width - 1)
    return is_ctx, rows, tl, colp, width


def _shift_lr(cur, i):
    _, _, _, colp, width = _tile_pos(i)
    left = jnp.where(colp == 0, 0.0, pltpu.roll(cur, 1, 0))
    right = jnp.where(colp == width - 1, 0.0, pltpu.roll(cur, TM - 1, 0))
    return left, right


def _shift_ud(cur, prv, nxt, i):
    is_ctx, rows, tl, _, _ = _tile_pos(i)
    if TM > GW:
        up = jnp.where(rows >= GW, pltpu.roll(cur, GW, 0), pltpu.roll(prv, GW, 0))
        down = jnp.where(rows < TM - GW, pltpu.roll(cur, TM - GW, 0), pltpu.roll(nxt, TM - GW, 0))
    else:
        up, down = prv, nxt
    up = jnp.where(jnp.logical_or(is_ctx, tl < GW), 0.0, up)
    down = jnp.where(jnp.logical_or(is_ctx, tl >= T - GW), 0.0, down)
    return up, down


def _shift_specs():
    dm = _dims()
    nt = dm["TT"] // TM
    cw = max(w for w in range(128, 1664 + 1, 128) if dm["SH"] % w == 0)
    cur = pl.BlockSpec((TM, cw), lambda j, i: (i, j))
    prv = pl.BlockSpec((TM, cw), lambda j, i: (jnp.maximum(i - 1, 0), j))
    nxt = pl.BlockSpec((TM, cw), lambda j, i: (jnp.minimum(i + 1, nt - 1), j))
    mu = pl.BlockSpec((4, cw), lambda j, i: (0, j))
    return nt, cw, cur, prv, nxt, mu


def _shift_fwd(p_rw, mu):
    dm = _dims()
    nt, cw, cur, prv, nxt, mus = _shift_specs()

    def body(c_ref, p_ref, n_ref, mu_ref, o_ref):
        i = pl.program_id(1)
        p, m = c_ref[...], mu_ref[...]
        left, right = _shift_lr(p, i)
        up, down = _shift_ud(p, p_ref[...], n_ref[...], i)
        vert = jnp.where(i < dm["NCT"], 0.0, 1.0)
        o_ref[...] = (p * (1.0 - m[0:1] - m[1:2] - vert * (m[2:3] + m[3:4]))
                      + m[0:1] * left + m[1:2] * right + m[2:3] * up + m[3:4] * down)

    return _call(body, "shift_fwd", (dm["SH"] // cw, nt), [cur, prv, nxt, mus], cur, _sds((dm["TT"], dm["SH"])),
                 sem=("parallel", "parallel"))(p_rw, p_rw, p_rw, mu)


def _shift_bwd(p_rw, dsh, mu):
    dm = _dims()
    nt, cw, cur, prv, nxt, mus = _shift_specs()

    def body(c_ref, p_ref, n_ref, gc_ref, gp_ref, gn_ref, mu_ref, dp_ref, dmu_ref):
        i = pl.program_id(1)
        p, g, m = c_ref[...], gc_ref[...], mu_ref[...]
        vert = jnp.where(i < dm["NCT"], 0.0, 1.0)
        _, from_right = _shift_lr(m[0:1] * g, i)
        from_left, _ = _shift_lr(m[1:2] * g, i)
        _, from_down = _shift_ud(m[2:3] * g, g, m[2:3] * gn_ref[...], i)
        from_up, _ = _shift_ud(m[3:4] * g, m[3:4] * gp_ref[...], g, i)
        dp = g * (1.0 - m[0:1] - m[1:2] - vert * (m[2:3] + m[3:4])) + from_right + from_left + from_down + from_up
        dp_ref[...] = dp.astype(BF16)

        @pl.when(i == 0)
        def _():
            dmu_ref[...] = jnp.zeros_like(dmu_ref)

        left, right = _shift_lr(p, i)
        up, down = _shift_ud(p, p_ref[...], n_ref[...], i)
        s = lambda a: jnp.sum(a, axis=0, keepdims=True)
        dmu_ref[...] += jnp.concatenate([s(g * (left - p)), s(g * (right - p)), vert * s(g * (up - p)), vert * s(g * (down - p))], axis=0)

    return _call(body, "shift_bwd", (dm["SH"] // cw, nt), [cur, prv, nxt, cur, prv, nxt, mus],
                 [cur, pl.BlockSpec((4, cw), lambda j, i: (0, j))], [_sds((dm["TT"], dm["SH"]), BF16), _sds((4, dm["SH"]))],
                 sem=("parallel", "arbitrary"))(p_rw, p_rw, p_rw, dsh, dsh, dsh, mu)


def _acc_out(ref, first, vals):
    @pl.when(first)
    def _():
        for r in ref:
            r[...] = jnp.zeros_like(r)

    for r, v in zip(ref, vals):
        r[...] += v


def _rowsum(a):
    return jnp.sum(a, axis=0, keepdims=True)


def _rw_prep_specs():
    dm = _dims()
    rw = RWW // 128
    tok = lambda width, cb: pl.BlockSpec((TMV, width), lambda i: (i, cb))
    ins = [pl.BlockSpec((TMV, RWW), lambda i: (i, 1)), tok(128, 3 * rw), tok(128, 3 * rw + 1),
           _full((2, RWW)), _full((128, RWW)), _full((128, RWW)), _full((2, RWW)), _full((128, RWW)), _full((128, RWW)),
           _full((1, RWW)), _full((1, RWW))]
    return dm, ins


def _rw_prep_fwd(sh, w0, w2p0, w2p1, a0, a2p0, a2p1, kkp, kap_p):
    dm, ins = _rw_prep_specs()
    tt = dm["TT"]

    def body(k_ref, low_ref, loa_ref, w0_ref, w20_ref, w21_ref, a0_ref, a20_ref, a21_ref, kk_ref, ka_ref, kap_ref, lw_ref, b_ref, kd_ref):
        tw, la = jnp.tanh(low_ref[...]), loa_ref[...]
        kap, lw0, b0, kd0, lw1, b1, kd1 = _rw_prep_math(
            k_ref[...], _mm1(tw, w20_ref[...]), _mm1(tw, w21_ref[...]), _mm1(la, a20_ref[...]), _mm1(la, a21_ref[...]),
            w0_ref[...], a0_ref[...], kk_ref[...], ka_ref[...])
        kap_ref[...] = kap
        lw_ref[0], lw_ref[1] = lw0, lw1
        b_ref[0], b_ref[1] = b0, b1
        kd_ref[0], kd_ref[1] = kd0, kd1

    two = pl.BlockSpec((2, TMV, RWW), lambda i: (0, i, 0))
    return _call(body, "rw_prep_fwd", (tt // TMV,), ins, [pl.BlockSpec((TMV, RWW), lambda i: (i, 0)), two, two, two],
                 [_sds((tt, RWW)), _sds((2, tt, RWW)), _sds((2, tt, RWW)), _sds((2, tt, RWW))], sem=("parallel",))(
        sh, sh, sh, w0, w2p0, w2p1, a0, a2p0, a2p1, kkp, kap_p)


def _rw_prep_bwd(sh, w0, w2p0, w2p1, a0, a2p0, a2p1, kkp, kap_p, scan_grads, dkd_p, dr_p, dv_p):
    dm, ins = _rw_prep_specs()
    tt, sh_w = dm["TT"], dm["SH"]
    one = pl.BlockSpec((TMV, RWW), lambda i: (i, 0))

    def body(k_ref, low_ref, loa_ref, w0_ref, w20_ref, w21_ref, a0_ref, a20_ref, a21_ref, kk_ref, ka_ref,
             dr0, dr1, dv0, dv1, dka0, dka1, dlw0, dlw1, db0, db1, dkd0, dkd1, dkdp_ref, drp_ref, dvp_ref,
             dsh_ref, dw0_ref, dw20_ref, dw21_ref, da0_ref, da20_ref, da21_ref, dkk_ref, dka_ref):
        tw, la = jnp.tanh(low_ref[...]), loa_ref[...]
        w20, w21, a20, a21 = w20_ref[...], w21_ref[...], a20_ref[...], a21_ref[...]
        _, vjp = jax.vjp(_rw_prep_math, k_ref[...], _mm1(tw, w20), _mm1(tw, w21), _mm1(la, a20), _mm1(la, a21),
                         w0_ref[...], a0_ref[...], kk_ref[...], ka_ref[...])
        dkp = dkdp_ref[...]
        dk, dpw0, dpw1, dpa0, dpa1, dw0, da0, dkk, dka = vjp(
            (dka0[...] + dka1[...], dlw0[...], db0[...], dkd0[...] + dkp, dlw1[...], db1[...], dkd1[...] + dkp))
        twt, lat = tw.T, la.T
        dtw = _mm1_nt(dpw0, w20) + _mm1_nt(dpw1, w21)
        dsh_ref[:, 0:RWW] = dr0[...] + dr1[...] + drp_ref[...]
        dsh_ref[:, RWW:2 * RWW] = dk
        dsh_ref[:, 2 * RWW:3 * RWW] = dv0[...] + dv1[...] + dvp_ref[...]
        dsh_ref[:, 3 * RWW:3 * RWW + 128] = dtw * (1.0 - tw * tw)
        dsh_ref[:, 3 * RWW + 128:3 * RWW + 256] = _mm1_nt(dpa0, a20) + _mm1_nt(dpa1, a21)
        _acc_out((dw0_ref, dw20_ref, dw21_ref, da0_ref, da20_ref, da21_ref, dkk_ref, dka_ref), pl.program_id(0) == 0,
                 (dw0, _mm1(twt, dpw0), _mm1(twt, dpw1), da0, _mm1(lat, dpa0), _mm1(lat, dpa1), dkk, dka))

    par = [_full((2, RWW)), _full((128, RWW)), _full((128, RWW)), _full((2, RWW)), _full((128, RWW)), _full((128, RWW)),
           _full((1, RWW)), _full((1, RWW))]
    return _call(body, "rw_prep_bwd", (tt // TMV,), ins + [one] * 15,
                 [pl.BlockSpec((TMV, sh_w), lambda i: (i, 0))] + par,
                 [_sds((tt, sh_w)), _sds((2, RWW)), _sds((128, RWW)), _sds((128, RWW)), _sds((2, RWW)), _sds((128, RWW)), _sds((128, RWW)),
                  _sds((1, RWW)), _sds((1, RWW))],
                 sem=("arbitrary",))(sh, sh, sh, w0, w2p0, w2p1, a0, a2p0, a2p1, kkp, kap_p, *scan_grads, dkd_p, dr_p, dv_p)


def _rw_post_specs():
    dm = _dims()
    nv = dm["NCTV"]
    rw = RWW // 128
    lat3 = lambda d: pl.BlockSpec((None, TMV, RWW), lambda i: (d, nv + i, 0))
    lat2 = pl.BlockSpec((TMV, RWW), lambda i: (nv + i, 0))
    ins = [lat2, lat2, lat3(0), lat3(1),
           pl.BlockSpec((TMV, RWW), lambda i: (nv + i, 0)), pl.BlockSpec((TMV, RWW), lambda i: (nv + i, 2)),
           pl.BlockSpec((TMV, RWW), lambda i: (i, 0)), _full((1, RWW)), _full((1, RWW)), _full((1, RWW))]
    return dm, nv, ins


def _rw_post_fwd(y0_, y1_, kd, sh, p_z, rk, gng, gnb):
    dm, nv, ins = _rw_post_specs()

    def body(y0, y1, k0, k1, r, v, z, rk_ref, g_ref, b_ref, o_ref):
        o_ref[...] = _rw_post_math(y0[...], y1[...], k0[...], k1[...], r[...], v[...], z[...], rk_ref[...], g_ref[...], b_ref[...]).astype(BF16)

    return _call(body, "rw_post_fwd", (T // TMV,), ins, pl.BlockSpec((TMV, RWW), lambda i: (i, 0)), _sds((T, RWW), BF16),
                 sem=("parallel",))(y0_, y1_, kd, kd, sh, sh, p_z, rk, gng, gnb)


def _rw_post_bwd(y0_, y1_, kd, sh, p_z, rk, gng, gnb, dout):
    dm, nv, _ = _rw_post_specs()
    tt = dm["TT"]
    cl = lambda i: jnp.maximum(i - nv, 0)
    all3 = lambda d: pl.BlockSpec((None, TMV, RWW), lambda i: (d, i, 0))
    all2 = pl.BlockSpec((TMV, RWW), lambda i: (i, 0))
    ins = [all2, all2, all3(0), all3(1),
           pl.BlockSpec((TMV, RWW), lambda i: (i, 0)), pl.BlockSpec((TMV, RWW), lambda i: (i, 2)),
           pl.BlockSpec((TMV, RWW), lambda i: (cl(i), 0)), _full((1, RWW)), _full((1, RWW)), _full((1, RWW)),
           pl.BlockSpec((TMV, RWW), lambda i: (cl(i), 0))]
    tok = pl.BlockSpec((TMV, RWW), lambda i: (i, 0))

    def body(y0, y1, k0, k1, r, v, z, rk_ref, g_ref, b_ref, do_ref, dy_ref, dkd_ref, dr_ref, dv_ref, dz_ref, drk_ref, dg_ref, db_ref):
        i = pl.program_id(0)
        lat = jnp.where(i >= nv, 1.0, 0.0)
        _, vjp = jax.vjp(_rw_post_math, y0[...], y1[...], k0[...], k1[...], r[...], v[...], z[...], rk_ref[...], g_ref[...], b_ref[...])
        dy0, _, dk0, _, dr, dv, dz, drk, dg, db = vjp(do_ref[...] * lat)
        dy_ref[...] = dy0
        dkd_ref[...] = dk0
        dr_ref[...] = dr
        dv_ref[...] = dv
        dz_ref[...] = dz.astype(BF16)
        _acc_out((drk_ref, dg_ref, db_ref), i == 0, (drk, dg, db))

    return _call(body, "rw_post_bwd", (tt // TMV,), ins,
                 [tok, tok, tok, tok, pl.BlockSpec((TMV, RWW), lambda i: (cl(i), 0)), _full((1, RWW)), _full((1, RWW)), _full((1, RWW))],
                 [_sds((tt, RWW))] * 4 + [_sds((T, RWW), BF16), _sds((1, RWW)), _sds((1, RWW)), _sds((1, RWW))],
                 sem=("arbitrary",))(y0_, y1_, kd, kd, sh, sh, p_z, rk, gng, gnb, dout)


def _hg_post_fwd(o0_, o1_, p_hg, hg_g):
    dm = _dims()
    nv = dm["NCTV"]
    lat3 = lambda d: pl.BlockSpec((TMV, HGW), lambda i: (nv + i, 0))

    def body(o0, o1, z, g, out):
        out[...] = _hg_post_math(o0[...], o1[...], z[...], g[...]).astype(BF16)

    return _call(body, "hg_post_fwd", (T // TMV,), [lat3(0), lat3(1), pl.BlockSpec((TMV, HGW), lambda i: (nv + i, 4)), _full((1, HGW))],
                 pl.BlockSpec((TMV, HGW), lambda i: (i, 0)), _sds((T, HGW), BF16), sem=("parallel",))(o0_, o1_, p_hg, hg_g)


def _hg_post_bwd(o0_, o1_, p_hg, hg_g, dout):
    dm = _dims()
    nv, tt = dm["NCTV"], dm["TT"]
    all3 = lambda d: pl.BlockSpec((TMV, HGW), lambda i: (i, 0))
    tok = pl.BlockSpec((TMV, HGW), lambda i: (i, 0))

    def body(o0, o1, z, g, do_in, do_ref, dz_ref, dg_ref):
        i = pl.program_id(0)
        lat = jnp.where(i >= nv, 1.0, 0.0)
        _, vjp = jax.vjp(_hg_post_math, o0[...], o1[...], z[...], g[...])
        d0, _, dz, dg = vjp(do_in[...] * lat)
        do_ref[...] = d0
        dz_ref[...] = dz
        _acc_out((dg_ref,), i == 0, (dg,))

    return _call(body, "hg_post_bwd", (tt // TMV,),
                 [all3(0), all3(1), pl.BlockSpec((TMV, HGW), lambda i: (i, 4)), _full((1, HGW)),
                  pl.BlockSpec((TMV, HGW), lambda i: (jnp.maximum(i - nv, 0), 0))],
                 [tok, tok, _full((1, HGW))], [_sds((tt, HGW)), _sds((tt, HGW)), _sds((1, HGW))], sem=("arbitrary",))(o0_, o1_, p_hg, hg_g, dout)


def _hg_dproj(dq0, dq1, di0, di1, df0, df1, dz):
    dm = _dims()
    tt = dm["TT"]
    tok = pl.BlockSpec((TMV, HGW), lambda i: (i, 0))

    def body(q0, q1, i0, i1, f0, f1, z, o_ref):
        o_ref[:, 0:HGW] = (q0[...] + q1[...]).astype(BF16)
        o_ref[:, HGW:2 * HGW] = (i0[...] + i1[...]).astype(BF16)
        o_ref[:, 2 * HGW:3 * HGW] = f0[...].astype(BF16)
        o_ref[:, 3 * HGW:4 * HGW] = f1[...].astype(BF16)
        o_ref[:, 4 * HGW:5 * HGW] = z[...].astype(BF16)

    return _call(body, "hg_dproj", (tt // TMV,), [tok] * 7,
                 pl.BlockSpec((TMV, 5 * HGW), lambda i: (i, 0)), _sds((tt, 5 * HGW), BF16), sem=("parallel",))(
        dq0, dq1, di0, di1, df0, df1, dz)


def _merge1_fwd(out_hg, out_rw, whg_t, wrw_t, p_g):
    tok = pl.BlockSpec((TM, D), lambda i: (i, 0))

    def body(h_ref, r_ref, wh_ref, wr_ref, g1_ref, g2_ref, a_ref, b_ref, m_ref):
        nt = (((1,), (1,)), ((), ()))
        a = lax.dot_general(h_ref[...], wh_ref[...], nt, preferred_element_type=F32)
        b = lax.dot_general(r_ref[...], wr_ref[...], nt, preferred_element_type=F32)
        a_ref[...] = a
        b_ref[...] = b
        m_ref[...] = _gate_math(g1_ref[...], g2_ref[...], a, b).astype(BF16)

    return _call(body, "merge1_fwd", (T // TM,),
                 [pl.BlockSpec((TM, HGW), lambda i: (i, 0)), pl.BlockSpec((TM, RWW), lambda i: (i, 0)), _full((D, HGW)), _full((D, RWW)),
                  pl.BlockSpec((TM, D), lambda i: (i, 0)), pl.BlockSpec((TM, D), lambda i: (i, 1))],
                 [tok, tok, tok], [_sds((T, D)), _sds((T, D)), _sds((T, D), BF16)], sem=("parallel",))(out_hg, out_rw, whg_t, wrw_t, p_g, p_g)


def _head_fwd_bwd(x, merged, w_out, mods, final_g, tgt):
    tok = pl.BlockSpec((TM, D), lambda i: (i, 0))

    def body(x_ref, m_ref, w_ref, mods_ref, fg_ref, t_ref, dxo_ref, dmo_ref, loss_ref, dgate_ref, dfg_ref):
        mo = jnp.dot(m_ref[...], w_ref[...], preferred_element_type=F32)
        gate = mods_ref[0:1, 2 * D:3 * D]
        loss, vjp = jax.vjp(lambda x_, mo_, g_, fg_: _head_math(x_, mo_, g_, fg_, t_ref[...]), x_ref[...], mo, gate, fg_ref[...])
        dx, dmo, dgate, dfg = vjp(jnp.ones((1, 1), F32))
        dxo_ref[...] = dx
        dmo_ref[...] = dmo.astype(BF16)
        _acc_out((loss_ref, dgate_ref, dfg_ref), pl.program_id(0) == 0, (jnp.broadcast_to(loss, (1, 128)), dgate, dfg))

    return _call(body, "head_fwd_bwd", (T // TM,), [tok, tok, _full((D, D)), _full((8, 3 * D)), _full((1, D)), tok],
                 [tok, tok, _full((1, 128)), _full((1, D)), _full((1, D))],
                 [_sds((T, D)), _sds((T, D), BF16), _sds((1, 128)), _sds((1, D)), _sds((1, D))], sem=("arbitrary",))(
        x, merged, w_out, mods, final_g, tgt)


def _merge_bwd(dmo, w_out, p_g, a, b):
    tok = pl.BlockSpec((TM, D), lambda i: (i, 0))

    def body(dmo_ref, w_ref, g1_ref, g2_ref, a_ref, b_ref, da_ref, db_ref, dg_ref):
        dm_ = lax.dot_general(dmo_ref[...], w_ref[...], (((1,), (1,)), ((), ())), preferred_element_type=F32)
        _, vjp = jax.vjp(_gate_math, g1_ref[...], g2_ref[...], a_ref[...], b_ref[...])
        dg1, dg2, da, db = vjp(dm_)
        da_ref[...] = da.astype(BF16)
        db_ref[...] = db.astype(BF16)
        dg_ref[:, 0:D] = dg1.astype(BF16)
        dg_ref[:, D:2 * D] = dg2.astype(BF16)

    return _call(body, "merge_bwd", (T // TM,),
                 [tok, _full((D, D)), pl.BlockSpec((TM, D), lambda i: (i, 0)), pl.BlockSpec((TM, D), lambda i: (i, 1)), tok, tok],
                 [tok, tok, pl.BlockSpec((TM, 2 * D), lambda i: (i, 0))], [_sds((T, D), BF16), _sds((T, D), BF16), _sds((T, 2 * D), BF16)],
                 sem=("parallel",))(dmo, w_out, p_g, p_g, a, b)


def _h_bwd(ctx, x, mods, norm_g, dh_hg, dh_rw, dh_z, dh_g, dxo, dgate):
    dm = _dims()
    nct, tt = dm["NCT"], dm["TT"]
    tok = pl.BlockSpec((TM, D), lambda i: (i, 0))
    lat = pl.BlockSpec((TM, D), lambda i: (jnp.maximum(i - nct, 0), 0))

    def body(c_ref, x_ref, m_ref, g_ref, d1, d2, d3, d4, dxo_ref, dgate_ref, gx_ref, dng_ref, dmods_ref):
        i = pl.program_id(0)
        is_ctx = i < nct
        latf = jnp.where(is_ctx, 0.0, 1.0)
        dh = d1[...] + d2[...] + latf * (d3[...] + d4[...])
        _, vjp = jax.vjp(_h_math, jnp.where(is_ctx, c_ref[...], x_ref[...]), g_ref[...], _sel_mod(m_ref, is_ctx, D),
                         _sel_mod(m_ref, is_ctx, 0))
        dx, dng, dscale, dshift = vjp(dh)
        gx_ref[...] = dx + dxo_ref[...]

        @pl.when(i == 0)
        def _():
            dng_ref[...] = jnp.zeros_like(dng_ref)
            dmods_ref[...] = jnp.zeros_like(dmods_ref)
            dmods_ref[0:1, 2 * D:3 * D] = dgate_ref[...]

        dng_ref[...] += dng
        row = lax.broadcasted_iota(jnp.int32, (8, 1), 0)
        sel = jnp.where(row == jnp.where(is_ctx, 1, 0), 1.0, 0.0)
        dmods_ref[:, 0:D] += sel * dshift
        dmods_ref[:, D:2 * D] += sel * dscale

    return _call(body, "h_bwd", (tt // TM,),
                 [*_token_specs(), _full((8, 3 * D)), _full((1, D)), tok, tok, lat, lat, lat, _full((1, D))],
                 [lat, _full((1, D)), _full((8, 3 * D))], [_sds((T, D)), _sds((1, D)), _sds((8, 3 * D))], sem=("arbitrary",))(
        ctx, x, mods, norm_g, dh_hg, dh_rw, dh_z, dh_g, dxo, dgate)


def _lb_math(l0, l1):
    return jax.nn.sigmoid(l0 - l1)


def _lb_fwd(l0, l1):
    def body(a_ref, b_ref, o_ref):
        o_ref[...] = _lb_math(a_ref[...], b_ref[...])

    return _call(body, "lb_fwd", (1,), [_full((2, HGW))] * 2, _full((2, HGW)), _sds((2, HGW)))(l0, l1)


def _lb_bwd(l0, l1, dlb):
    def body(a_ref, b_ref, d_ref, da_ref, db_ref):
        _, vjp = jax.vjp(_lb_math, a_ref[...], b_ref[...])
        da_ref[...], db_ref[...] = vjp(d_ref[...])

    return _call(body, "lb_bwd", (1,), [_full((2, HGW))] * 3, [_full((2, HGW))] * 2, [_sds((2, HGW))] * 2)(l0, l1, dlb)


def _local_step(x, ctx, tgt, mods, w, start_exchange):
    dm = _dims()
    tt, sh_w, hgc = dm["TT"], dm["SH"], dm["HGC"]
    nb = lambda cols: cols // WB
    h = _h_fwd(ctx, x, mods, w["norm_g"])
    win_t = w["win_t"]
    p_hg = _proj_nt(h, win_t, 0, nb(hgc), 0, tt, "proj_hg")
    p_rw = _proj_nt(h, win_t, nb(hgc), nb(sh_w), 0, tt, "proj_rw")
    p_z = _proj_nt(h, win_t, nb(hgc + sh_w), nb(RWW), TC, T, "proj_z")
    p_g = _proj_nt(h, win_t, nb(hgc + sh_w + RWW), nb(2 * D), TC, T, "proj_g")

    lb3 = _lb_fwd(w["lb0"], w["lb1"]).reshape(2, 1, HGW)
    o0, o1, hg_ck = _hg_scan_fwd(p_hg, lb3)
    out_hg = _hg_post_fwd(o0, o1, p_hg, w["hg_g"])

    sh = _shift_fwd(p_rw, w["mu"])
    zpad = jnp.zeros((LORA, RWW), F32)
    prep_w = (w["w0"], jnp.concatenate([w["w2"][0], zpad], 0), jnp.concatenate([zpad, w["w2"][1]], 0),
              w["a0"], jnp.concatenate([w["a2"][0], zpad], 0), jnp.concatenate([zpad, w["a2"][1]], 0), w["kk"], w["ka"])
    kap, lw, b, kd = _rw_prep_fwd(sh, *prep_w)
    y0, y1, rw_ck, rw_tm = _rw_scan_fwd(sh, kap, lw, b, kd)
    post_w = (w["rk"], w["gng"], w["gnb"])
    out_rw = _rw_post_fwd(y0, y1, kd, sh, p_z, *post_w)

    late = w["late_weights"](out_rw)
    a, bb, merged = _merge1_fwd(out_hg, out_rw, late["whg_t"], late["wrw_t"], p_g)
    dxo, dmo, loss, dgate, d_fg = _head_fwd_bwd(x, merged, late["wout"], mods, w["final_g"], tgt)
    da, db, dp_g = _merge_bwd(dmo, late["wout"], p_g, a, bb)
    g_wout = _dw_tn(merged, dmo, 0, D, 0, "dw_out")
    g_whg = _dw_tn(da, out_hg, 0, D, 0, "dw_hg")
    g_wrw = _dw_tn(db, out_rw, 0, D, 0, "dw_rw")
    token, early = start_exchange(dict(wout=g_wout, whg_t=g_whg, wrw_t=g_wrw), "early")
    d_out_hg = _mm_nn(da, late["whg_t"], 0, "dx_hg", after=token)
    d_out_rw = _mm_nn(db, late["wrw_t"], 0, "dx_rw")

    do, dz_hg, d_hg_g = _hg_post_bwd(o0, o1, p_hg, w["hg_g"], d_out_hg)
    *hg_grads, dlb3 = _hg_scan_bwd(p_hg, lb3, hg_ck, do)
    dp_hg = _hg_dproj(*hg_grads, dz_hg)
    d_lb0, d_lb1 = _lb_bwd(w["lb0"], w["lb1"], dlb3.reshape(2, HGW))

    dy, dkd_p, dr_p, dv_p, dp_z, d_rk, d_gng, d_gnb = _rw_post_bwd(y0, y1, kd, sh, p_z, *post_w, d_out_rw)
    rw_grads = _rw_scan_bwd(sh, kap, lw, b, kd, rw_ck, rw_tm, dy)
    dsh, d_w0, d_w2p0, d_w2p1, d_a0, d_a2p0, d_a2p1, d_kk, d_ka = _rw_prep_bwd(sh, *prep_w, rw_grads, dkd_p, dr_p, dv_p)
    dp_rw, d_mu = _shift_bwd(p_rw, dsh, w["mu"])

    g_win = _dw_tn(dp_hg, h, 0, dm["NCOLS"], 0, "dw_in_hg")
    g_win = _dw_tn(dp_rw, h, 0, dm["NCOLS"], nb(hgc), "dw_in_rw", prev=g_win)
    g_win = _dw_tn(dp_z, h, TC, dm["NCOLS"], nb(hgc + sh_w), "dw_in_z", prev=g_win)
    g_win = _dw_tn(dp_g, h, TC, dm["NCOLS"], nb(hgc + sh_w + RWW), "dw_in_g", prev=g_win)
    token, late = start_exchange(dict(win_t=g_win), "late")
    in_flight = (early, late)
    dh_hg = _mm_nn(dp_hg, win_t, 0, "dh_hg", after=token)
    dh_rw = _mm_nn(dp_rw, win_t, nb(hgc), "dh_rw")
    dh_z = _mm_nn(dp_z, win_t, nb(hgc + sh_w), "dh_z")
    dh_g = _mm_nn(dp_g, win_t, nb(hgc + sh_w + RWW), "dh_g")
    gx, d_ng, dmods = _h_bwd(ctx, x, mods, w["norm_g"], dh_hg, dh_rw, dh_z, dh_g, dxo, dgate)

    small = dict(norm_g=d_ng, lb0=d_lb0, lb1=d_lb1, hg_g=d_hg_g, mu=d_mu, w0=d_w0,
                 w2=jnp.stack([d_w2p0[:LORA], d_w2p1[LORA:]]), a0=d_a0, a2=jnp.stack([d_a2p0[:LORA], d_a2p1[LORA:]]),
                 kk=d_kk, ka=d_ka, rk=d_rk, gng=d_gng, gnb=d_gnb, final_g=d_fg)
    return loss, gx, dmods, in_flight, small


MESH = pl.DeviceIdType.MESH


def _comm_call(body, name, bufs, out_shapes, nsem, nloc):
    hbm = pl.BlockSpec(memory_space=pl.ANY)
    return pl.pallas_call(
        body, name=name, in_specs=[hbm] * len(bufs), out_specs=[hbm] * len(out_shapes), out_shape=out_shapes,
        scratch_shapes=[pltpu.SemaphoreType.DMA((nsem,)), pltpu.SemaphoreType.DMA((nsem,)), pltpu.SemaphoreType.DMA((nloc,))],
    )(*bufs)


GATHER_PIECE_BYTES = 4 * 1024 * 1024


def _gather2(bufs, name):
    pieces = []
    for bi, b in enumerate(bufs):
        rows = b.shape[0]
        if rows % 32 == 0 and b.size * b.dtype.itemsize > GATHER_PIECE_BYTES:
            pieces += [(bi, 0, rows // 2), (bi, rows // 2, rows // 2)]
        else:
            pieces.append((bi, 0, rows))
    nbuf = len(bufs)

    def body(*refs):
        ins, outs = refs[:nbuf], refs[nbuf:2 * nbuf]
        send_sems, recv_sems, _ = refs[2 * nbuf:]
        x, y, c = lax.axis_index("x"), lax.axis_index("y"), lax.axis_index("c")
        me, sib = (x, y, c), (x, y, 1 - c)
        flip = lambda a, b: a + b - 2 * a * b
        xn, yn, diag = (1 - x, y), (x, 1 - y), (1 - x, 1 - y)
        relay_from, relay_to = (flip(x, c), flip(1 - y, c)), (flip(1 - x, c), flip(y, c))

        def copy(pi, k, block, to, own=False):
            bi, r0, nr = pieces[pi]
            rows = outs[bi].at[4 * block[0] + 2 * block[1] + block[2], pl.ds(r0, nr)]
            return pltpu.make_async_remote_copy(src_ref=ins[bi].at[pl.ds(r0, nr)] if own else rows, dst_ref=rows,
                                                send_sem=send_sems.at[7 * pi + k], recv_sem=recv_sems.at[7 * pi + k],
                                                device_id=to, device_id_type=MESH)

        started = []
        for pi in range(len(pieces)):
            started += [copy(pi, 0, me, sib, own=True), copy(pi, 1, me, (*xn, c), own=True), copy(pi, 2, me, (*yn, c), own=True)]
        for cp in started:
            cp.start()
        for pi in range(len(pieces)):
            for k, chip in ((1, xn), (2, yn)):
                copy(pi, k, (*chip, c), me).wait_recv()
                started.append(copy(pi, 3 + k, (*chip, c), sib))
                started[-1].start()
            started.append(copy(pi, 3, (*relay_from, c), (*relay_to, c)))
            started[-1].start()
        for pi in range(len(pieces)):
            copy(pi, 3, (*diag, c), me).wait_recv()
            started.append(copy(pi, 6, (*diag, c), sib))
            started[-1].start()
        for pi in range(len(pieces)):
            copy(pi, 0, sib, me).wait_recv()
            for k, chip in ((4, xn), (5, yn), (6, diag)):
                copy(pi, k, (*chip, 1 - c), me).wait_recv()
        for cp in started:
            cp.wait_send()

    return _comm_call(body, name, bufs, [_sds((NDEV,) + b.shape, b.dtype) for b in bufs], 7 * len(pieces), 1)


def _gather_all(bufs, name):
    me = 4 * lax.axis_index("x") + 2 * lax.axis_index("y") + lax.axis_index("c")
    return [lax.dynamic_update_index_in_dim(g, b, me, axis=0) for g, b in zip(_gather2(bufs, name), bufs)]


def _peer_copies(srcs, lands, send_sems, recv_sems):
    x, y, c = lax.axis_index("x"), lax.axis_index("y"), lax.axis_index("c")
    me = 4 * x + 2 * y + c
    pairs = []
    for bi in range(len(srcs)):
        for k in range(1, NDEV):
            px, py, pc = (1 - x if k & 4 else x), (1 - y if k & 2 else y), (1 - c if k & 1 else c)
            mk = lambda dst: pltpu.make_async_remote_copy(
                src_ref=srcs[bi], dst_ref=dst, send_sem=send_sems.at[7 * bi + k - 1], recv_sem=recv_sems.at[7 * bi + k - 1],
                device_id=(px, py, pc), device_id_type=MESH)
            pairs.append((mk(lands[bi].at[me]), mk(lands[bi].at[4 * px + 2 * py + pc])))
    return pairs


def _direct_gather_start(bufs, dep, name):
    nbuf = len(bufs)
    hbm, sem = pl.BlockSpec(memory_space=pltpu.HBM), pl.BlockSpec(memory_space=pltpu.SEMAPHORE)

    def body(*refs):
        srcs, lands = refs[:nbuf], refs[nbuf:2 * nbuf]
        for send, _ in _peer_copies(srcs, lands, refs[2 * nbuf + 1], refs[2 * nbuf + 2]):
            send.start()
        refs[-1][...] = jnp.zeros_like(refs[-1])

    held = [pltpu.HBM(b.shape, b.dtype) for b in bufs]
    zones = [pltpu.HBM((NDEV,) + b.shape, b.dtype) for b in bufs]
    outs = pl.pallas_call(
        body, name=name, in_specs=[hbm] * (2 * nbuf) + [pl.BlockSpec(memory_space=pl.ANY)],
        out_specs=(sem, sem, *[hbm] * (2 * nbuf), pl.BlockSpec(memory_space=pltpu.VMEM)),
        out_shape=(pltpu.SemaphoreType.DMA((7 * nbuf,)), pltpu.SemaphoreType.DMA((7 * nbuf,)), *held, *zones, _sds((8, 128))),
        input_output_aliases={i: 2 + i for i in range(2 * nbuf)},
        compiler_params=pltpu.CompilerParams(has_side_effects=pltpu.SideEffectType.DATAFLOW_SIDE_EFFECTING),
    )(*[pltpu.with_memory_space_constraint(b, pltpu.HBM) for b in bufs],
      *[pltpu.with_memory_space_constraint(lax.empty((NDEV,) + b.shape, b.dtype), pltpu.HBM) for b in bufs], dep)
    return (outs[0], outs[1], list(outs[2:2 + nbuf]), list(outs[2 + nbuf:2 + 2 * nbuf])), outs[-1]


def _direct_gather_wait(send_sems, recv_sems, srcs, lands, after, name):
    nbuf = len(srcs)
    hbm, sem = pl.BlockSpec(memory_space=pltpu.HBM), pl.BlockSpec(memory_space=pltpu.SEMAPHORE)

    def body(*refs):
        for send, recv in _peer_copies(refs[:nbuf], refs[nbuf:2 * nbuf], refs[2 * nbuf], refs[2 * nbuf + 1]):
            send.wait_send()
            recv.wait_recv()

    outs = pl.pallas_call(
        body, name=name, in_specs=[hbm] * (2 * nbuf) + [sem, sem, pl.BlockSpec(memory_space=pl.ANY)], out_specs=[hbm] * (2 * nbuf),
        out_shape=(*[pltpu.HBM(b.shape, b.dtype) for b in srcs], *[pltpu.HBM(b.shape, b.dtype) for b in lands]),
        input_output_aliases={i: i for i in range(2 * nbuf)},
        compiler_params=pltpu.CompilerParams(has_side_effects=pltpu.SideEffectType.DATAFLOW_SIDE_EFFECTING),
    )(*srcs, *lands, send_sems, recv_sems, after)
    me = 4 * lax.axis_index("x") + 2 * lax.axis_index("y") + lax.axis_index("c")
    return [lax.dynamic_update_index_in_dim(g, b, me, axis=0) for g, b in zip(outs[nbuf:], outs[:nbuf])]


def _pair_exchange(bufs, name):
    nbuf = len(bufs)

    def body(*refs):
        ins, got = refs[:nbuf], refs[nbuf:2 * nbuf]
        send_sems, recv_sems, _ = refs[2 * nbuf:]
        x, y, c = lax.axis_index("x"), lax.axis_index("y"), lax.axis_index("c")
        copies = []
        for bi in range(nbuf):
            for q in range(4):
                copies.append(pltpu.make_async_remote_copy(
                    src_ref=ins[bi].at[2 * q + 1 - c], dst_ref=got[bi].at[q], send_sem=send_sems.at[4 * bi + q],
                    recv_sem=recv_sems.at[4 * bi + q], device_id=(x, y, 1 - c), device_id_type=MESH))
                copies[-1].start()
        for cp in copies:
            cp.wait_send()
            cp.wait_recv()

    return _comm_call(body, name, bufs, [_sds((4,) + b.shape[1:], b.dtype) for b in bufs], 4 * nbuf, 1)


def _chip_copies(srcs, lands, send_sems, recv_sems):
    x, y, c = lax.axis_index("x"), lax.axis_index("y"), lax.axis_index("c")
    myq = 2 * x + y
    pairs = []
    for bi in range(len(srcs)):
        for j, (qx, qy) in enumerate([(1 - x, y), (x, 1 - y), (1 - x, 1 - y)]):
            q = 2 * qx + qy
            mk = lambda src, dst: pltpu.make_async_remote_copy(
                src_ref=src, dst_ref=dst, send_sem=send_sems.at[3 * bi + j], recv_sem=recv_sems.at[3 * bi + j],
                device_id=(qx, qy, c), device_id_type=MESH)
            pairs.append((mk(srcs[bi].at[q], lands[bi].at[myq]), mk(srcs[bi].at[myq], lands[bi].at[q])))
    return pairs


def _chip_exchange_start(bufs, name):
    nbuf = len(bufs)
    hbm, sem = pl.BlockSpec(memory_space=pltpu.HBM), pl.BlockSpec(memory_space=pltpu.SEMAPHORE)

    def body(*refs):
        srcs, lands = refs[:nbuf], refs[nbuf:2 * nbuf]
        send_sems, recv_sems = refs[2 * nbuf], refs[2 * nbuf + 1]
        token = refs[-1]
        for send, _ in _chip_copies(srcs, lands, send_sems, recv_sems):
            send.start()
        token[...] = jnp.zeros_like(token)

    held = [pltpu.HBM(b.shape, b.dtype) for b in bufs]
    outs = pl.pallas_call(
        body, name=name, in_specs=[hbm] * (2 * nbuf), out_specs=(sem, sem, *[hbm] * (2 * nbuf), pl.BlockSpec(memory_space=pltpu.VMEM)),
        out_shape=(pltpu.SemaphoreType.DMA((3 * nbuf,)), pltpu.SemaphoreType.DMA((3 * nbuf,)), *held, *held, _sds((8, 128))),
        input_output_aliases={i: 2 + i for i in range(2 * nbuf)},
        compiler_params=pltpu.CompilerParams(has_side_effects=pltpu.SideEffectType.DATAFLOW_SIDE_EFFECTING),
    )(*[pltpu.with_memory_space_constraint(b, pltpu.HBM) for b in bufs],
      *[pltpu.with_memory_space_constraint(lax.empty(b.shape, b.dtype), pltpu.HBM) for b in bufs])
    return outs[0], outs[1], list(outs[2:2 + nbuf]), list(outs[2 + nbuf:2 + 2 * nbuf]), outs[-1]


def _chip_exchange_wait(send_sems, recv_sems, srcs, lands, after, name):
    nbuf = len(srcs)
    hbm, sem = pl.BlockSpec(memory_space=pltpu.HBM), pl.BlockSpec(memory_space=pltpu.SEMAPHORE)

    def body(*refs):
        src_refs, land_refs = refs[:nbuf], refs[nbuf:2 * nbuf]
        for send, recv in _chip_copies(src_refs, land_refs, refs[2 * nbuf], refs[2 * nbuf + 1]):
            send.wait_send()
            recv.wait_recv()

    held = [pltpu.HBM(b.shape, b.dtype) for b in srcs]
    outs = pl.pallas_call(
        body, name=name, in_specs=[hbm] * (2 * nbuf) + [sem, sem, pl.BlockSpec(memory_space=pl.ANY)], out_specs=[hbm] * (2 * nbuf),
        out_shape=(*held, *held), input_output_aliases={i: i for i in range(2 * nbuf)},
        compiler_params=pltpu.CompilerParams(has_side_effects=pltpu.SideEffectType.DATAFLOW_SIDE_EFFECTING),
    )(*srcs, *lands, send_sems, recv_sems, after)
    return list(outs[:nbuf]), list(outs[nbuf:])


def _prefetch_call(body, name, scalars, grid, in_specs, out_specs, out_shape, args):
    return pl.pallas_call(
        body, name=name, out_shape=out_shape,
        grid_spec=pltpu.PrefetchScalarGridSpec(num_scalar_prefetch=1, grid=grid, in_specs=in_specs, out_specs=out_specs),
        compiler_params=pltpu.CompilerParams(dimension_semantics=("parallel",) * len(grid), vmem_limit_bytes=V7X_VMEM_LIMIT))(scalars, *args)


def _pair_add(g, got, core, name):
    _, rows, cols = got.shape
    tr = _row_tile(rows, 16, 1024)

    def body(c_ref, a_ref, b_ref, o_ref):
        o_ref[...] = (a_ref[...].astype(F32) + b_ref[...].astype(F32)).astype(o_ref.dtype)

    blk = pl.BlockSpec((None, tr, cols), lambda q, i, c_ref: (q, i, 0))
    mine = pl.BlockSpec((None, None, tr, cols), lambda q, i, c_ref: (q, c_ref[0], i, 0))
    return _prefetch_call(body, name, core, (4, rows // tr), [mine, blk], blk, _sds(got.shape, got.dtype),
                          (g.reshape(4, 2, rows, cols), got))


def _reduce_chips(chip_sums, recv, slots, name):
    _, rows, cols = recv.shape
    cw = 256 if cols % 256 == 0 else cols

    def body(s_ref, own_ref, r1_ref, r2_ref, r3_ref, o_ref):
        o_ref[...] = ((own_ref[...].astype(F32) + r1_ref[...].astype(F32)) + r2_ref[...].astype(F32)) + r3_ref[...].astype(F32)

    pick = lambda k: pl.BlockSpec((None, rows, cw), lambda j, s_ref: (s_ref[k], 0, j))
    return _prefetch_call(body, name, slots, (cols // cw,), [pick(0), pick(1), pick(2), pick(3)],
                          pl.BlockSpec((rows, cw), lambda j, s_ref: (0, j)), _sds((rows, cols)), (chip_sums, recv, recv, recv))


def _row_tile(rows, mult, cap):
    best = mult
    for t in range(mult, cap + 1, mult):
        if rows % t == 0:
            best = t
    assert rows % best == 0, (rows, mult)
    return best


def _reduce_sources(r, name):
    nsrc, rows, cols = r.shape
    tr = _row_tile(rows, 16 if r.dtype.itemsize == 2 else 8, 128)

    def body(r_ref, o_ref):
        acc = r_ref[0].astype(F32)
        for j in range(1, nsrc):
            acc = acc + r_ref[j].astype(F32)
        o_ref[...] = acc

    return _call(body, name, (rows // tr,), [pl.BlockSpec((nsrc, tr, cols), lambda i: (0, i, 0))],
                 pl.BlockSpec((tr, cols), lambda i: (i, 0)), _sds((rows, cols)), sem=("parallel",))(r)


def _adamw(w, g, m, v, name):
    rows, cols = w.shape
    tr = _row_tile(rows, 8, 128)
    c1 = 1.0 - ADAM_B1 ** ADAM_STEP
    c2 = 1.0 - ADAM_B2 ** ADAM_STEP

    def body(w_ref, g_ref, m_ref, v_ref, d_ref, mo_ref, vo_ref):
        g_ = g_ref[...]
        m_ = ADAM_B1 * m_ref[...] + (1.0 - ADAM_B1) * g_
        v_ = ADAM_B2 * v_ref[...] + (1.0 - ADAM_B2) * (g_ * g_)
        d_ref[...] = -ADAM_LR * ((m_ / c1) / (jnp.sqrt(v_ / c2) + ADAM_EPS) + ADAM_WD * w_ref[...])
        mo_ref[...] = m_
        vo_ref[...] = v_

    blk = pl.BlockSpec((tr, cols), lambda i: (i, 0))
    return _call(body, name, (rows // tr,), [blk] * 4, [blk] * 3, [_sds((rows, cols))] * 3, sem=("parallel",))(w, g, m, v)


SLAB_PART = 1024


def _pack(arrs):
    parts = []
    for a in arrs:
        flat = a.reshape(-1)
        pad = (-flat.shape[0]) % SLAB_PART
        if pad:
            flat = jnp.concatenate([flat, jnp.zeros((pad,), flat.dtype)])
        parts.append(flat.reshape(-1, 128))
    return jnp.concatenate(parts, axis=0)


def _unpack(slab, shapes):
    outs, row = [], 0
    for s in shapes:
        n = 1
        for d in s:
            n *= d
        rows = (n + SLAB_PART - 1) // SLAB_PART * (SLAB_PART // 128)
        outs.append(slab[row:row + rows].reshape(-1)[:n].reshape(s))
        row += rows
    return outs


def _unshard_last(g, lead):
    nl = len(lead)
    return jnp.transpose(g, tuple(range(1, nl + 1)) + (0, nl + 1)).reshape(tuple(lead) + (-1,))


def kernel(x, c, ctx, c_ctx, ada_w, ada_b, norm_g, w_in, hg_lb, hg_norm_g, rw_mu, rw_w0, rw_w2, rw_a0, rw_a2, rw_kk, rw_ka, rw_rk, rw_gn_g, rw_gn_b, w_hg_out, w_rw_out, w_out, final_g, loss_target, m_c_ctx, m_ada_w, m_ada_b, m_norm_g, m_w_in, m_hg_lb, m_hg_norm_g, m_rw_mu, m_rw_w0, m_rw_w2, m_rw_a0, m_rw_a2, m_rw_kk, m_rw_ka, m_rw_rk, m_rw_gn_g, m_rw_gn_b, m_w_hg_out, m_w_rw_out, m_w_out, m_final_g, v_c_ctx, v_ada_w, v_ada_b, v_norm_g, v_w_in, v_hg_lb, v_hg_norm_g, v_rw_mu, v_rw_w0, v_rw_w2, v_rw_a0, v_rw_a2, v_rw_kk, v_rw_ka, v_rw_rk, v_rw_gn_g, v_rw_gn_b, v_w_hg_out, v_w_rw_out, v_w_out, v_final_g):
    dm = _dims()
    me = 4 * lax.axis_index("x") + 2 * lax.axis_index("y") + lax.axis_index("c")

    sharded_small = [hg_lb, rw_mu[0], rw_w0[0], rw_w2[0], rw_a0[0], rw_a2[0]]
    c_rows = jnp.concatenate([c, jnp.zeros((7, D), F32)], axis=0)
    gathered = _gather_all([w_in[0].T.astype(BF16), _pack(sharded_small), c_rows], "gather_weights")
    win_t = gathered[0].reshape(-1, D)
    per_dev = jax.vmap(lambda s: _unpack(s, [a.shape for a in sharded_small]))(gathered[1])
    hg_lb_f, mu_f, w0_f, w2_f, a0_f, a2_f = [_unshard_last(p, p.shape[1:-1]) for p in per_dev]

    ncol = ada_w.shape[2]
    cc16 = jnp.concatenate([gathered[2][:, 0], c_ctx[None], jnp.zeros((7, D), F32)], axis=0)
    mod_all = _gather_all([_ada_fwd(cc16, ada_w[0])], "gather_mods")[0]
    mod_x = lax.dynamic_index_in_dim(mod_all, me, axis=1, keepdims=False).reshape(1, -1) + ada_b
    mod_c = mod_all[:, NDEV].reshape(1, -1) + ada_b
    mods = jnp.concatenate([mod_x, mod_c, jnp.zeros((6, 3 * D), F32)], axis=0)

    late_flight, token = _direct_gather_start([w_out[0].astype(BF16), w_hg_out[0].T.astype(BF16), w_rw_out[0].T.astype(BF16)],
                                              mod_all, "gather_late_start")
    mods = mods + token[0:1, 0:1]

    def late_weights(after):
        got = _direct_gather_wait(*late_flight, after, "gather_late_wait")
        return dict(zip(("wout", "whg_t", "wrw_t"), [g.reshape(-1, g.shape[2]) for g in got]))

    w = dict(win_t=win_t, late_weights=late_weights, norm_g=norm_g,
             lb0=hg_lb_f[:, 0], lb1=hg_lb_f[:, 1], hg_g=hg_norm_g, mu=mu_f, w0=w0_f, w2=w2_f, a0=a0_f, a2=a2_f,
             kk=rw_kk, ka=rw_ka, rk=rw_rk, gng=rw_gn_g, gnb=rw_gn_b, final_g=final_g[None])
    def start_exchange(grads_bf16, tag):
        names = list(grads_bf16)
        blocks = [grads_bf16[n].reshape(NDEV, -1, grads_bf16[n].shape[1]) for n in names]
        got = _pair_exchange(blocks, "pair_grads_" + tag)
        core = lax.axis_index("c").astype(jnp.int32).reshape(1)
        chip_sums = [_pair_add(g, r, core, "pair_add_" + n) for g, r, n in zip(blocks, got, names)]
        send_sems, recv_sems, held, lands, token = _chip_exchange_start(chip_sums, "chip_grads_start_" + tag)
        return token, (names, send_sems, recv_sems, held, lands)

    loss_dev, grad_x, dmods, in_flight, small = _local_step(x[0], ctx[0], loss_target[0], mods, w, start_exchange)
    loss = lax.psum(loss_dev[0, 0], AXES)

    my_chip = (2 * lax.axis_index("x") + lax.axis_index("y")).astype(jnp.int32)
    others = jnp.arange(3, dtype=jnp.int32)
    slots = jnp.concatenate([my_chip.reshape(1), others + (others >= my_chip).astype(jnp.int32)])

    def finish_exchange(tag, flight, after):
        names, *copies = flight
        chip_sums, recv = _chip_exchange_wait(*copies, after, "chip_grads_wait_" + tag)
        return {n: _reduce_chips(cs, r, slots, "reduce_" + n) for n, cs, r in zip(names, chip_sums, recv)}

    reduced = finish_exchange("early", in_flight[0], grad_x)

    dmods_all = _gather_all([dmods], "gather_dmods")[0]
    my_cols = lambda a: lax.dynamic_slice_in_dim(a, me * ncol, ncol, axis=1)
    g_ada, dcc16, d_ada_b = _ada_bwd(cc16, ada_w[0], my_cols(dmods_all[:, 0]), my_cols(dmods_all[:, 1]), dmods)
    small = dict(small, c_ctx=dcc16[NDEV:NDEV + 1], ada_b=d_ada_b)

    small_names = ["c_ctx", "ada_b", "norm_g", "lb0", "lb1", "hg_g", "mu", "w0", "a0", "kk", "ka", "rk", "gng", "gnb", "final_g"]
    lora_names = ["w2", "a2"]
    small_all, lora_all = _gather_all([_pack([small[n] for n in small_names]), _pack([small[n] for n in lora_names]).astype(BF16)],
                                      "gather_small_grads")
    small_sum = _reduce_sources(small_all, "reduce_small")
    sg = dict(zip(small_names, _unpack(small_sum, [small[n].shape for n in small_names])))
    sg.update(zip(lora_names, _unpack(_reduce_sources(lora_all, "reduce_lora"), [small[n].shape for n in lora_names])))

    reduced.update(finish_exchange("late", in_flight[1], small_sum))
    g_win_t, g_wout, g_whg_t, g_wrw_t = [reduced[n] for n in ("win_t", "wout", "whg_t", "wrw_t")]

    def my_shard(full):
        n = full.shape[-1] // NDEV
        return lax.dynamic_slice_in_dim(full, me * n, n, axis=full.ndim - 1)

    grads = dict(
        c_ctx=sg["c_ctx"][0], ada_w=g_ada[None], ada_b=sg["ada_b"], norm_g=sg["norm_g"], w_in=g_win_t.T[None],
        hg_lb=my_shard(jnp.stack([sg["lb0"], sg["lb1"]], axis=1)), hg_norm_g=sg["hg_g"], rw_mu=my_shard(sg["mu"])[None],
        rw_w0=my_shard(sg["w0"])[None], rw_w2=my_shard(sg["w2"])[None], rw_a0=my_shard(sg["a0"])[None], rw_a2=my_shard(sg["a2"])[None],
        rw_kk=sg["kk"], rw_ka=sg["ka"], rw_rk=sg["rk"], rw_gn_g=sg["gng"], rw_gn_b=sg["gnb"],
        w_hg_out=g_whg_t.T[None], w_rw_out=g_wrw_t.T[None], w_out=g_wout[None], final_g=sg["final_g"][0])
    weights = dict(c_ctx=c_ctx, ada_w=ada_w, ada_b=ada_b, norm_g=norm_g, w_in=w_in, hg_lb=hg_lb, hg_norm_g=hg_norm_g, rw_mu=rw_mu,
                   rw_w0=rw_w0, rw_w2=rw_w2, rw_a0=rw_a0, rw_a2=rw_a2, rw_kk=rw_kk, rw_ka=rw_ka, rw_rk=rw_rk, rw_gn_g=rw_gn_g,
                   rw_gn_b=rw_gn_b, w_hg_out=w_hg_out, w_rw_out=w_rw_out, w_out=w_out, final_g=final_g)
    m_in = dict(zip(weights, (m_c_ctx, m_ada_w, m_ada_b, m_norm_g, m_w_in, m_hg_lb, m_hg_norm_g, m_rw_mu, m_rw_w0, m_rw_w2, m_rw_a0,
                              m_rw_a2, m_rw_kk, m_rw_ka, m_rw_rk, m_rw_gn_g, m_rw_gn_b, m_w_hg_out, m_w_rw_out, m_w_out, m_final_g)))
    v_in = dict(zip(weights, (v_c_ctx, v_ada_w, v_ada_b, v_norm_g, v_w_in, v_hg_lb, v_hg_norm_g, v_rw_mu, v_rw_w0, v_rw_w2, v_rw_a0,
                              v_rw_a2, v_rw_kk, v_rw_ka, v_rw_rk, v_rw_gn_g, v_rw_gn_b, v_w_hg_out, v_w_rw_out, v_w_out, v_final_g)))

    big_w = ["ada_w", "w_in", "w_hg_out", "w_rw_out", "w_out"]
    delta, new_m, new_v = {}, {}, {}
    for n in big_w:
        shp = weights[n].shape
        two = lambda a: a.reshape(shp[-2], shp[-1])
        d_, m_, v_ = _adamw(two(weights[n]), two(grads[n]), two(m_in[n]), two(v_in[n]), "adamw_" + n)
        delta[n], new_m[n], new_v[n] = d_.reshape(shp), m_.reshape(shp), v_.reshape(shp)
    rest = [n for n in weights if n not in big_w]
    shapes = [weights[n].shape for n in rest]
    d_s, m_s, v_s = _adamw(_pack([weights[n] for n in rest]), _pack([grads[n] for n in rest]), _pack([m_in[n] for n in rest]),
                           _pack([v_in[n] for n in rest]), "adamw_small")
    for n, d_, m_, v_ in zip(rest, _unpack(d_s, shapes), _unpack(m_s, shapes), _unpack(v_s, shapes)):
        delta[n], new_m[n], new_v[n] = d_, m_, v_

    order = list(weights)
    return (loss, grad_x[None], *[grads[n] for n in order], *[delta[n] for n in order],
            *[new_m[n] for n in order], *[new_v[n] for n in order])
```

```python
import functools

import jax
import jax.numpy as jnp
from jax import lax
from jax.experimental import pallas as pl
from jax.experimental.pallas import tpu as pltpu

D = 2048
T = 2048
TC = 256
GW = 64
C = 64
HGW = 1024
RWW = 1024
LORA = 64
NDEV = 8
TM = 256
TMV = 128
NORM_EPS = 1e-6
RW_GN_EPS = 64e-5
ADAM_LR, ADAM_B1, ADAM_B2, ADAM_EPS, ADAM_WD, ADAM_STEP = 0.001, 0.9, 0.999, 1e-08, 0.01, 10
WB = 256
V7X_VMEM_LIMIT = 56 * 1024 * 1024

F32 = jnp.float32
BF16 = jnp.bfloat16
AXES = ("x", "y", "c")


def _dims():
    tt = T + TC
    sh = 3 * RWW + 4 * LORA
    hgc = 5 * HGW
    return dict(TT=tt, NCC=TC // C, NCH=tt // C, HG_H=HGW // 128, RW_P=RWW // 128, SH=sh, HGC=hgc,
                NCOLS=hgc + sh + RWW + 2 * D, NCT=TC // TM, NCTV=TC // TMV)


def _mm(a, b):
    return _dot3(a, b, "nn")


def _mm_nt(a, b):
    return _dot3(a, b, "nt")


def _mm_tn(a, b):
    return _dot3(a, b, "tn")


def _split(x):
    hi = x.astype(BF16)
    return hi, (x - hi.astype(F32)).astype(BF16)


_FORMS = {"nn": (((1,), (0,)), ((), ())), "nt": (((1,), (1,)), ((), ())), "tn": (((0,), (0,)), ((), ()))}


def _dot3_raw(a, b, form):
    ah, al = _split(a)
    bh, bl = _split(b)
    d = lambda x, y: lax.dot_general(x, y, _FORMS[form], preferred_element_type=F32)
    return d(ah, bh) + (d(ah, bl) + d(al, bh))


@functools.partial(jax.custom_vjp, nondiff_argnums=(2,))
def _dot3(a, b, form):
    return _dot3_raw(a, b, form)


def _dot3_fwd(a, b, form):
    return _dot3_raw(a, b, form), (a, b)


def _dot2(a, b, form, wide):
    d = lambda x, y: lax.dot_general(x, y, _FORMS[form], preferred_element_type=F32)
    if wide == 0:
        hi, lo = _split(a)
        bb = b.astype(BF16)
        return d(hi, bb) + d(lo, bb)
    hi, lo = _split(b)
    aa = a.astype(BF16)
    return d(aa, hi) + d(aa, lo)


def _dot3_bwd(form, res, g):
    a, b = res
    if form == "nn":
        return _dot3(g, b, "nt"), _dot3(a, g, "tn")
    if form == "nt":
        return _dot3(g, b, "nn"), _dot3(g, a, "tn")
    return _dot3(b, g, "nt"), _dot3(a, g, "nn")


@functools.partial(jax.custom_vjp, nondiff_argnums=(2,))
def _dot3_lean(a, b, form):
    return _dot3_raw(a, b, form)


def _dot3_lean_bwd(form, res, g):
    a, b = res
    if form == "nn":
        return _dot2(g, b, "nt", 0), _dot2(a, g, "tn", 1)
    if form == "nt":
        return _dot2(g, b, "nn", 0), _dot2(g, a, "tn", 0)
    return _dot2(b, g, "nt", 1), _dot2(a, g, "nn", 1)


_dot3_lean.defvjp(_dot3_fwd, _dot3_lean_bwd)


_dot3.defvjp(_dot3_fwd, _dot3_bwd)


def _scan_cumsum(inc, x):
    return _cumsum_vjp(inc.astype(BF16), x)


def _cumsum_raw(inc, x, form):
    h1 = x.astype(BF16)
    r1 = x - h1.astype(F32)
    h2 = r1.astype(BF16)
    h3 = (r1 - h2.astype(F32)).astype(BF16)
    d = lambda y: lax.dot_general(inc, y, _FORMS[form], preferred_element_type=F32)
    return d(h1) + (d(h2) + d(h3))


@jax.custom_vjp
def _cumsum_vjp(inc, x):
    return _cumsum_raw(inc, x, "nn")


_cumsum_vjp.defvjp(lambda inc, x: (_cumsum_raw(inc, x, "nn"), inc),
                   lambda inc, g: (jnp.zeros_like(inc), _cumsum_raw(inc, g, "tn")))


def _silu(x):
    return x * jax.nn.sigmoid(x)


def _softplus(x):
    return jnp.maximum(x, 0.0) + jnp.log(1.0 + jnp.exp(-jnp.abs(x)))


def _iota2(shape, dim):
    return lax.broadcasted_iota(jnp.int32, shape, dim)


def _pair_mask():
    return (_iota2((128, 128), 0) < 64) == (_iota2((128, 128), 1) < 64)


def _seg64_raw(x):
    e = _pair_mask().astype(BF16)
    parts = []
    for g in range(x.shape[1] // 128):
        hi, lo = _split(x[:, g * 128:(g + 1) * 128])
        parts.append(jnp.dot(hi, e, preferred_element_type=F32) + jnp.dot(lo, e, preferred_element_type=F32))
    return parts[0] if len(parts) == 1 else jnp.concatenate(parts, axis=1)


@jax.custom_vjp
def _seg64_sum(x):
    return _seg64_raw(x)


_seg64_sum.defvjp(lambda x: (_seg64_raw(x), None), lambda _, g: (_seg64_raw(g),))


def _mm1(a, b):
    return jnp.dot(a.astype(BF16), b.astype(BF16), preferred_element_type=F32)


def _mm1_nt(a, b):
    return lax.dot_general(a.astype(BF16), b.astype(BF16), (((1,), (1,)), ((), ())), preferred_element_type=F32)


def _seg128_mean(x):
    parts = [jnp.broadcast_to(jnp.mean(x[:, g * 128:(g + 1) * 128], axis=1, keepdims=True), (x.shape[0], 128))
             for g in range(x.shape[1] // 128)]
    return parts[0] if len(parts) == 1 else jnp.concatenate(parts, axis=1)


def _scan_masks(d):
    lag = (_iota2((C, C), 0) - _iota2((C, C), 1)) * (1 - 2 * d)
    return lag >= 0, lag > 0


def _tri_inverse(a):
    eye = (_iota2((C, C), 0) == _iota2((C, C), 1)).astype(F32)
    p = [-x for x in a]
    tm = [eye + x for x in p]
    n = 2
    while n < C:
        p = [_mm(x, x) for x in p]
        tm = [t + _mm(t, x) for t, x in zip(tm, p)]
        n *= 2
    return tm


@jax.custom_vjp
def _tri_solve(tm, a, rhs):
    return [_mm(t, r) for t, r in zip(tm, rhs)]


def _tri_solve_fwd(tm, a, rhs):
    u = [_mm(t, r) for t, r in zip(tm, rhs)]
    return u, (tm, u)


def _tri_solve_bwd(res, g):
    tm, u = res
    d_rhs = [_mm_tn(t, x) for t, x in zip(tm, g)]
    return [jnp.zeros_like(t) for t in tm], [-_mm_nt(d, x) for d, x in zip(d_rhs, u)], d_rhs


_tri_solve.defvjp(_tri_solve_fwd, _tri_solve_bwd)


def _rw_chunks(chains, tms=None):
    s0, r, lw, kap, b, v, kd, ds = zip(*chains)
    ids = range(len(chains))
    _mm = lambda p, q: _dot3_lean(p, q, "nn")
    _mm_nt = lambda p, q: _dot3_lean(p, q, "nt")
    _mm_tn = lambda p, q: _dot3_lean(p, q, "tn")
    inc, strict = zip(*[_scan_masks(d) for d in ds])
    lane = _iota2((1, 128), 1)
    hm = ((lane < 64).astype(F32), (lane >= 64).astype(F32))
    lc = [_scan_cumsum(inc[i], lw[i]) for i in ids]
    ltot = [jnp.sum(lw[i], axis=0, keepdims=True) for i in ids]
    rt = [r[i] * jnp.exp(lc[i]) for i in ids]
    kt = [kap[i] * jnp.exp(lc[i] - lw[i]) for i in ids]
    einv = [jnp.exp(-lc[i]) for i in ids]
    kh = [kd[i] * einv[i] for i in ids]
    bh = [b[i] * einv[i] for i in ids]
    eend = [jnp.exp(ltot[i] - lc[i]) for i in ids]
    kbar = [kd[i] * eend[i] for i in ids]
    bbar = [b[i] * eend[i] for i in ids]
    lhs = [jnp.concatenate([kt[i] * hm[0], kt[i] * hm[1], rt[i] * hm[0], rt[i] * hm[1]], axis=0) for i in ids]
    gk = [_mm_nt(lhs[i], kh[i]) for i in ids]
    gb = [_mm_nt(lhs[i], bh[i]) for i in ids]
    ks = [_mm_nt(kt[i], s0[i]) for i in ids]
    ys = [_mm_nt(rt[i], s0[i]) for i in ids]
    heads = [(i, h) for i in ids for h in range(2)]
    akk = {c: jnp.where(strict[c[0]], gk[c[0]][c[1] * C:(c[1] + 1) * C], 0.0) for c in heads}
    akb = {c: jnp.where(strict[c[0]], gb[c[0]][c[1] * C:(c[1] + 1) * C], 0.0) for c in heads}
    ark = {c: jnp.where(inc[c[0]], gk[c[0]][(2 + c[1]) * C:(3 + c[1]) * C], 0.0) for c in heads}
    arb = {c: jnp.where(inc[c[0]], gb[c[0]][(2 + c[1]) * C:(3 + c[1]) * C], 0.0) for c in heads}
    vh = {c: v[c[0]] * hm[c[1]] for c in heads}
    av = {c: _mm(akk[c], vh[c]) for c in heads}
    rhs = {c: ks[c[0]] * hm[c[1]] + av[c] for c in heads}
    akb_list = [akb[c] for c in heads]
    tm = _tri_inverse(akb_list) if tms is None else [tms[c] for c in heads]
    uh = dict(zip(heads, _tri_solve(tm, akb_list, [rhs[c] for c in heads])))
    yv = {c: _mm(ark[c], vh[c]) for c in heads}
    yu = {c: _mm(arb[c], uh[c]) for c in heads}
    u = [uh[i, 0] + uh[i, 1] for i in ids]
    y = [ys[i] + (yv[i, 0] - yu[i, 0]) + (yv[i, 1] - yu[i, 1]) for i in ids]
    upd = [_mm_tn(jnp.concatenate([v[i], -u[i]], axis=0), jnp.concatenate([kbar[i], bbar[i]], axis=0)) for i in ids]
    s1 = [s0[i] * jnp.exp(ltot[i]) + jnp.where(_pair_mask(), upd[i], 0.0) for i in ids]
    if tms is None:
        return [(y[i], s1[i], tm[2 * i], tm[2 * i + 1]) for i in ids]
    return [(y[i], s1[i]) for i in ids]


HG_SUB = 32


def _hg_chunks(chains):
    s0, qr, iv, f, lb, ds = zip(*chains)
    ids = range(len(chains))
    inc = [_scan_masks(d)[0] for d in ds]
    q = [_silu(qr[i]) for i in ids]
    fg = [lb[i] + (1.0 - lb[i]) * jax.nn.sigmoid(f[i]) for i in ids]
    k = [1.0 - fg[i] for i in ids]
    g = [jnp.log(fg[i]) for i in ids]
    bc = [_scan_cumsum(inc[i], g[i]) for i in ids]
    btot = [jnp.sum(g[i], axis=0, keepdims=True) for i in ids]
    o_inter = [_mm_nt(q[i] * jnp.exp(bc[i]), s0[i]) for i in ids]
    rowi = _iota2((C, 1), 0)
    outs = [[] for _ in ids]
    for blk in range(C // HG_SUB):
        lo, hi = blk * HG_SUB, (blk + 1) * HG_SUB
        first = [lo if ds[i] == 0 else hi - 1 for i in ids]
        ref = [jnp.sum(jnp.where(rowi == first[i], bc[i] - g[i], 0.0), axis=0, keepdims=True) for i in ids]
        qi = [q[i][lo:hi] * jnp.exp(bc[i][lo:hi] - ref[i]) for i in ids]
        src = [(rowi < hi) if ds[i] == 0 else (rowi >= lo) for i in ids]
        ke = [jnp.where(src[i], k[i] * jnp.exp(jnp.where(src[i], ref[i] - bc[i], 0.0)), 0.0) for i in ids]
        a = [jnp.where(inc[i][lo:hi], _mm_nt(qi[i], ke[i]), 0.0) for i in ids]
        part = [_mm(a[i], iv[i]) for i in ids]
        for i in ids:
            outs[i].append(part[i])
    o = [o_inter[i] + jnp.concatenate(outs[i], axis=0) for i in ids]
    upd = [_mm_tn(iv[i], k[i] * jnp.exp(btot[i] - bc[i])) for i in ids]
    s1 = [s0[i] * jnp.exp(btot[i]) + upd[i] for i in ids]
    return [(o[i], s1[i]) for i in ids]


def _lockstep(fn, ds, nargs):
    def flat_fn(*flat):
        return tuple(fn([tuple(flat[i * nargs:(i + 1) * nargs]) + (d,) for i, d in enumerate(ds)]))

    return flat_fn


def _h_math(x, ng, scale, shift):
    return x * lax.rsqrt(jnp.mean(x * x, axis=-1, keepdims=True) + NORM_EPS) * ng * (1.0 + scale) + shift


def _hg_post_math(o0, o1, z, g):
    o = o0 + o1
    on = o * lax.rsqrt(_seg128_mean(o * o) + NORM_EPS)
    return on * g * _silu(z)


def _rw_prep_math(k, pw0, pw1, pa0, pa1, w0, a0, kkp, kap_p):
    kk = k * kkp
    kap = kk * lax.rsqrt(_seg64_sum(kk * kk) + 1e-12)
    outs = []
    for d, (pw, pa) in enumerate(((pw0, pa0), (pw1, pa1))):
        w_log = -_softplus(-(w0[d:d + 1] + pw)) - 0.5
        lw = -jnp.exp(w_log)
        a = jax.nn.sigmoid(a0[d:d + 1] + pa)
        kd = k * (1.0 + (a - 1.0) * kap_p)
        outs += [lw, kap * a, kd]
    return (kap, *outs)


def _rw_post_math(y0, y1, kd0, kd1, r, v, z, rk, gng, gnb):
    ys, ksum = y0 + y1, kd0 + kd1
    mean = _seg64_sum(ys) * (1.0 / 64.0)
    cen = ys - mean
    var = _seg64_sum(cen * cen) * (1.0 / 64.0)
    yn = cen * lax.rsqrt(var + RW_GN_EPS) * gng + gnb
    bonus = _seg64_sum(r * ksum * rk) * v
    return (yn + bonus) * _silu(z)


def _head_math(x, mo, gate, fg, tgt):
    xo = x + gate * mo
    y = xo * lax.rsqrt(jnp.mean(xo * xo, axis=-1, keepdims=True) + NORM_EPS) * fg
    err = y - tgt
    return 0.5 * jnp.sum(jnp.mean(err * err, axis=-1, keepdims=True), axis=0, keepdims=True)


def _gate_math(ghg, grw, a, b):
    return jax.nn.sigmoid(ghg) * a + jax.nn.sigmoid(grw) * b


def _call(body, name, grid, in_specs, out_specs, out_shape, scratch=(), sem=None, aliases=None):
    return pl.pallas_call(
        body, name=name, grid=grid, in_specs=in_specs, out_specs=out_specs, out_shape=out_shape,
        scratch_shapes=list(scratch), input_output_aliases=aliases or {},
        compiler_params=pltpu.CompilerParams(dimension_semantics=sem, vmem_limit_bytes=V7X_VMEM_LIMIT))


def _full(shape):
    nd = len(shape)
    return pl.BlockSpec(shape, lambda *_: (0,) * nd)


def _sds(shape, dtype=F32):
    return jax.ShapeDtypeStruct(shape, dtype)


def _proj_nt(h, wt, wt_blk0, nblk, row0, nrows, name):
    def body(h_ref, w_ref, o_ref):
        o_ref[...] = lax.dot_general(h_ref[row0:row0 + nrows, :], w_ref[...], (((1,), (1,)), ((), ())), preferred_element_type=F32)

    return _call(body, name, (nblk,), [_full(h.shape), pl.BlockSpec((WB, wt.shape[1]), lambda j: (wt_blk0 + j, 0))],
                 pl.BlockSpec((nrows, WB), lambda j: (0, j)), _sds((nrows, nblk * WB)), sem=("parallel",))(h, wt)


def _dw_tn(a, b, row0, out_rows, out_blk0, name, prev=None):
    m, n = a.shape
    k2 = b.shape[1]
    nblk = n // WB

    def body(*refs):
        a_ref, b_ref, o_ref = refs[0], refs[1], refs[-1]
        o_ref[...] = lax.dot_general(a_ref[...], b_ref[row0:row0 + m, :], (((0,), (0,)), ((), ())),
                                     preferred_element_type=F32).astype(BF16)

    in_specs = [pl.BlockSpec((m, WB), lambda j: (0, j)), _full(b.shape)]
    args = [a, b]
    aliases = None
    if prev is not None:
        in_specs.append(pl.BlockSpec(memory_space=pl.ANY))
        args.append(prev)
        aliases = {2: 0}
    return _call(body, name, (nblk,), in_specs, pl.BlockSpec((WB, k2), lambda j: (out_blk0 + j, 0)), _sds((out_rows, k2), BF16),
                 sem=("arbitrary",), aliases=aliases)(*args)


def _mm_nn(a, w, w_blk0, name, after=None):
    m, kc = a.shape
    n = w.shape[1]
    kb = max(k for k in (WB, 2 * WB, 4 * WB) if kc % k == 0 and (w_blk0 * WB) % k == 0)
    w_blk0 = w_blk0 * WB // kb
    tmb = 3 * TM if m % (3 * TM) == 0 else (2 * TM if m % (2 * TM) == 0 else TM)

    def body(a_ref, w_ref, *rest):
        o_ref = rest[-1]

        @pl.when(pl.program_id(1) == 0)
        def _():
            o_ref[...] = jnp.zeros_like(o_ref)

        o_ref[...] += jnp.dot(a_ref[...], w_ref[...], preferred_element_type=F32)

    extra = [] if after is None else [pl.BlockSpec(memory_space=pl.ANY)]
    return _call(body, name, (m // tmb, kc // kb),
                 [pl.BlockSpec((tmb, kb), lambda i, k: (i, k)), pl.BlockSpec((kb, n), lambda i, k: (w_blk0 + k, 0))] + extra,
                 pl.BlockSpec((tmb, n), lambda i, k: (i, 0)), _sds((m, n)), sem=("parallel", "arbitrary"))(
        a, w, *([] if after is None else [after]))


def _ada_fwd(cc16, ada_shard):
    ncol = ada_shard.shape[1]

    def body(c_ref, w_ref, o_ref):
        o_ref[...] = _mm(_silu(c_ref[...]), w_ref[...])

    return _call(body, "ada_fwd", (1,), [_full((16, D)), _full((D, ncol))], _full((16, ncol)), _sds((16, ncol)))(cc16, ada_shard)


def _ada_bwd(cc16, ada_shard, dm_x, dm_c, dmods):
    ncol = ada_shard.shape[1]

    def body(c_ref, w_ref, dx_ref, dc_ref, dm_ref, dw_ref, dcc_ref, db_ref):
        s, vjp = jax.vjp(_silu, c_ref[...])
        dc_tot = dc_ref[0:1, :]
        for j in range(1, NDEV):
            dc_tot = dc_tot + dc_ref[j:j + 1, :]
        row = _iota2((8, 1), 0)
        dm16 = jnp.concatenate([dx_ref[...], jnp.where(row == 0, dc_tot, 0.0)], axis=0)
        dw_ref[...] = _mm_tn(s, dm16)
        dcc_ref[...] = vjp(_mm_nt(dm16, w_ref[...]))[0]
        db_ref[...] = dm_ref[0:1, :] + dm_ref[1:2, :]

    return _call(body, "ada_bwd", (1,), [_full((16, D)), _full((D, ncol)), _full((8, ncol)), _full((8, ncol)), _full((8, 3 * D))],
                 [_full((D, ncol)), _full((16, D)), _full((1, 3 * D))], [_sds((D, ncol)), _sds((16, D)), _sds((1, 3 * D))])(
        cc16, ada_shard, dm_x, dm_c, dmods)


def _sel_mod(mods_ref, is_ctx, lo):
    return jnp.where(is_ctx, mods_ref[1:2, lo:lo + D], mods_ref[0:1, lo:lo + D])


def _token_specs():
    nct = _dims()["NCT"]
    return (pl.BlockSpec((TM, D), lambda i: (jnp.minimum(i, nct - 1), 0)),
            pl.BlockSpec((TM, D), lambda i: (jnp.maximum(i - nct, 0), 0)))


def _h_fwd(ctx, x, mods, norm_g):
    dm = _dims()

    def body(c_ref, x_ref, m_ref, g_ref, o_ref):
        is_ctx = pl.program_id(0) < dm["NCT"]
        tok = jnp.where(is_ctx, c_ref[...], x_ref[...])
        o_ref[...] = _h_math(tok, g_ref[...], _sel_mod(m_ref, is_ctx, D), _sel_mod(m_ref, is_ctx, 0)).astype(BF16)

    return _call(body, "h_fwd", (dm["TT"] // TM,), [*_token_specs(), _full((8, 3 * D)), _full((1, D))],
                 pl.BlockSpec((TM, D), lambda i: (i, 0)), _sds((dm["TT"], D), BF16), sem=("parallel",))(ctx, x, mods, norm_g)


def _chunk_of(d, s):
    dm = _dims()
    ncc, nch = dm["NCC"], dm["NCH"]
    return s if d == 0 else jnp.where(s < ncc, ncc - 1 - s, nch - 1 - (s - ncc))


HG_GROUP = 8
RW_GROUP = 8


def _dir_specs(make):
    return [make(d) for d in range(2)]


def _lanes(j):
    return slice(j * 128, (j + 1) * 128)


def _hg_scan_fwd(p_hg, lb3):
    dm = _dims()
    nh, nch, tt = dm["HG_H"], dm["NCH"], dm["TT"]
    pp = min(HG_GROUP, nh)
    lw_ = 128 * pp

    def body(q0, q1, i0, i1, f0, f1, lb0, lb1, o0, o1, ck_ref, s_ref):
        @pl.when(pl.program_id(1) == 0)
        def _():
            s_ref[...] = jnp.zeros_like(s_ref)

        keys, chains = [], []
        for d, refs in enumerate(((q0, i0, f0, lb0), (q1, i1, f1, lb1))):
            vals = [r[...] for r in refs]
            for j in range(pp):
                keys.append((d, j))
                chains.append((s_ref[d, j], *[a[:, _lanes(j)] for a in vals], d))
        for (d, j), chain, (out, s1) in zip(keys, chains, _hg_chunks(chains)):
            ck_ref[d, j] = chain[0]
            (o0, o1)[d][:, _lanes(j)] = out
            s_ref[d, j] = s1

    blk = lambda off: _dir_specs(lambda d: pl.BlockSpec((C, lw_), lambda h, s: (_chunk_of(d, s), off * nh // pp + h)))
    return _call(body, "hg_scan_fwd", (nh // pp, nch),
                 blk(0) + blk(1) + _dir_specs(lambda d: pl.BlockSpec((C, lw_), lambda h, s: (_chunk_of(d, s), (2 + d) * nh // pp + h)))
                 + _dir_specs(lambda d: pl.BlockSpec((None, 1, lw_), lambda h, s: (d, 0, h))),
                 _dir_specs(lambda d: pl.BlockSpec((C, lw_), lambda h, s: (_chunk_of(d, s), h)))
                 + [pl.BlockSpec((2, pp, None, 128, 128), lambda h, s: (0, h, s, 0, 0))],
                 [_sds((tt, HGW)), _sds((tt, HGW)), _sds((2, nh, nch, 128, 128))],
                 scratch=[pltpu.VMEM((2, pp, 128, 128), F32)], sem=("parallel", "arbitrary"))(*([p_hg] * 6), lb3, lb3)


def _hg_scan_bwd(p_hg, lb3, ck, do):
    dm = _dims()
    nh, nch, tt = dm["HG_H"], dm["NCH"], dm["TT"]
    pp = min(HG_GROUP, nh)
    lw_ = 128 * pp

    def body(q0, q1, i0, i1, f0, f1, lb0, lb1, ck_ref, do0, do1, dq0, dq1, di0, di1, df0, df1, dlb_ref, ds_ref):
        @pl.when(pl.program_id(1) == 0)
        def _():
            ds_ref[...] = jnp.zeros_like(ds_ref)
            dlb_ref[...] = jnp.zeros_like(dlb_ref)

        keys, flat, cts = [], [], []
        for d, refs in enumerate(((q0, i0, f0, lb0, do0), (q1, i1, f1, lb1, do1))):
            *vals, do_ = [r[...] for r in refs]
            for j in range(pp):
                keys.append((d, j))
                flat += [ck_ref[d, j], *[a[:, _lanes(j)] for a in vals]]
                cts.append((do_[:, _lanes(j)], ds_ref[d, j]))
        _, vjp = jax.vjp(_lockstep(_hg_chunks, [d for d, _ in keys], 5), *flat)
        grads = vjp(tuple(cts))
        for n, (d, j) in enumerate(keys):
            ds0, dq_, di_, df_, dlb = grads[5 * n:5 * n + 5]
            (dq0, dq1)[d][:, _lanes(j)] = dq_
            (di0, di1)[d][:, _lanes(j)] = di_
            (df0, df1)[d][:, _lanes(j)] = df_
            dlb_ref[d, :, _lanes(j)] += dlb
            ds_ref[d, j] = ds0

    ch = lambda d, s: _chunk_of(d, nch - 1 - s)
    blk = lambda off: _dir_specs(lambda d: pl.BlockSpec((C, lw_), lambda h, s: (ch(d, s), off * nh // pp + h)))
    tok = _dir_specs(lambda d: pl.BlockSpec((C, lw_), lambda h, s: (ch(d, s), h)))
    return _call(body, "hg_scan_bwd", (nh // pp, nch),
                 blk(0) + blk(1) + _dir_specs(lambda d: pl.BlockSpec((C, lw_), lambda h, s: (ch(d, s), (2 + d) * nh // pp + h)))
                 + _dir_specs(lambda d: pl.BlockSpec((None, 1, lw_), lambda h, s: (d, 0, h)))
                 + [pl.BlockSpec((2, pp, None, 128, 128), lambda h, s: (0, h, nch - 1 - s, 0, 0))] + tok,
                 tok * 3 + [pl.BlockSpec((2, 1, lw_), lambda h, s: (0, 0, h))],
                 [_sds((tt, HGW))] * 6 + [_sds((2, 1, HGW))],
                 scratch=[pltpu.VMEM((2, pp, 128, 128), F32)], sem=("parallel", "arbitrary"))(*([p_hg] * 6), lb3, lb3, ck, do, do)


def _rw_scan_fwd(sh, kap, lw, b, kd):
    dm = _dims()
    npair, nch, tt = dm["RW_P"], dm["NCH"], dm["TT"]
    pp = min(RW_GROUP, npair)
    lw_ = 128 * pp

    def body(r0, r1, v0, v1, ka0, ka1, lw0, lw1, b0, b1, kd0, kd1, y0, y1, ck_ref, tm_ref, s_ref):
        @pl.when(pl.program_id(1) == 0)
        def _():
            s_ref[...] = jnp.zeros_like(s_ref)

        keys, chains = [], []
        for d, refs in enumerate(((r0, lw0, ka0, b0, v0, kd0), (r1, lw1, ka1, b1, v1, kd1))):
            vals = [r[...] for r in refs]
            for j in range(pp):
                keys.append((d, j))
                chains.append((s_ref[d, j], *[a[:, _lanes(j)] for a in vals], d))
        for (d, j), chain, (out, s1, tm_a, tm_b) in zip(keys, chains, _rw_chunks(chains)):
            ck_ref[d, j] = chain[0]
            tm_ref[d, j, 0] = tm_a
            tm_ref[d, j, 1] = tm_b
            (y0, y1)[d][:, _lanes(j)] = out
            s_ref[d, j] = s1

    two = lambda off: _dir_specs(lambda d: pl.BlockSpec((C, lw_), lambda p, s: (_chunk_of(d, s), off * npair // pp + p)))
    three = _dir_specs(lambda d: pl.BlockSpec((None, C, lw_), lambda p, s: (d, _chunk_of(d, s), p)))
    return _call(body, "rw_scan_fwd", (npair // pp, nch), two(0) + two(2) + two(0) + three * 3,
                 two(0) + [pl.BlockSpec((2, pp, None, 128, 128), lambda p, s: (0, p, s, 0, 0)),
                           pl.BlockSpec((2, pp, None, 2, C, C), lambda p, s: (0, p, s, 0, 0, 0))],
                 [_sds((tt, RWW)), _sds((tt, RWW)), _sds((2, npair, nch, 128, 128)), _sds((2, npair, nch, 2, C, C))],
                 scratch=[pltpu.VMEM((2, pp, 128, 128), F32)], sem=("parallel", "arbitrary"))(
        sh, sh, sh, sh, kap, kap, lw, lw, b, b, kd, kd)


def _rw_scan_bwd(sh, kap, lw, b, kd, ck, tm_ck, dy):
    dm = _dims()
    npair, nch, tt = dm["RW_P"], dm["NCH"], dm["TT"]
    pp = min(RW_GROUP, npair)
    lw_ = 128 * pp

    def body(r0, r1, v0, v1, ka0, ka1, lw0, lw1, b0, b1, kd0, kd1, ck_ref, tm_ref, dy0, dy1,
             dr0, dr1, dv0, dv1, dka0, dka1, dlw0, dlw1, db0, db1, dkd0, dkd1, ds_ref):
        @pl.when(pl.program_id(1) == 0)
        def _():
            ds_ref[...] = jnp.zeros_like(ds_ref)

        keys, flat, cts, tms = [], [], [], {}
        for d, refs in enumerate(((r0, lw0, ka0, b0, v0, kd0, dy0), (r1, lw1, ka1, b1, v1, kd1, dy1))):
            *vals, dy_ = [r[...] for r in refs]
            for j in range(pp):
                tms[len(keys), 0], tms[len(keys), 1] = tm_ref[d, j, 0], tm_ref[d, j, 1]
                keys.append((d, j))
                flat += [ck_ref[d, j], *[a[:, _lanes(j)] for a in vals]]
                cts.append((dy_[:, _lanes(j)], ds_ref[d, j]))
        _, vjp = jax.vjp(_lockstep(functools.partial(_rw_chunks, tms=tms), [d for d, _ in keys], 7), *flat)
        grads = vjp(tuple(cts))
        outs = ((dr0, dlw0, dka0, db0, dv0, dkd0), (dr1, dlw1, dka1, db1, dv1, dkd1))
        for n, (d, j) in enumerate(keys):
            ds0, *g_in = grads[7 * n:7 * n + 7]
            for o_ref, g in zip(outs[d], g_in):
                o_ref[:, _lanes(j)] = g
            ds_ref[d, j] = ds0

    ch = lambda d, s: _chunk_of(d, nch - 1 - s)
    two = lambda off: _dir_specs(lambda d: pl.BlockSpec((C, lw_), lambda p, s: (ch(d, s), off * npair // pp + p)))
    three = _dir_specs(lambda d: pl.BlockSpec((None, C, lw_), lambda p, s: (d, ch(d, s), p)))
    return _call(body, "rw_scan_bwd", (npair // pp, nch),
                 two(0) + two(2) + two(0) + three * 3
                 + [pl.BlockSpec((2, pp, None, 128, 128), lambda p, s: (0, p, nch - 1 - s, 0, 0)),
                    pl.BlockSpec((2, pp, None, 2, C, C), lambda p, s: (0, p, nch - 1 - s, 0, 0, 0))] + two(0),
                 two(0) * 6, [_sds((tt, RWW))] * 12,
                 scratch=[pltpu.VMEM((2, pp, 128, 128), F32)], sem=("parallel", "arbitrary"))(
        sh, sh, sh, sh, kap, kap, lw, lw, b, b, kd, kd, ck, tm_ck, dy, dy)


def _tile_pos(i):
    dm = _dims()
    is_ctx = i < dm["NCT"]
    rows = lax.broadcasted_iota(jnp.int32, (TM, 1), 0)
    tl = rows + (i - dm["NCT"]) * TM
    width = jnp.where(is_ctx, TC, GW)
    assert TC & (TC - 1) == 0 and GW & (GW - 1) == 0 and TM % GW == 0
    colp = rows & (width - 1)
    return is_ctx, rows, tl, colp, width


def _shift_lr(cur, i):
    _, _, _, colp, width = _tile_pos(i)
    left = jnp.where(colp == 0, 0.0, pltpu.roll(cur, 1, 0))
    right = jnp.where(colp == width - 1, 0.0, pltpu.roll(cur, TM - 1, 0))
    return left, right


def _shift_ud(cur, prv, nxt, i):
    is_ctx, rows, tl, _, _ = _tile_pos(i)
    if TM > GW:
        up = jnp.where(rows >= GW, pltpu.roll(cur, GW, 0), pltpu.roll(prv, GW, 0))
        down = jnp.where(rows < TM - GW, pltpu.roll(cur, TM - GW, 0), pltpu.roll(nxt, TM - GW, 0))
    else:
        up, down = prv, nxt
    up = jnp.where(jnp.logical_or(is_ctx, tl < GW), 0.0, up)
    down = jnp.where(jnp.logical_or(is_ctx, tl >= T - GW), 0.0, down)
    return up, down


def _shift_specs():
    dm = _dims()
    nt = dm["TT"] // TM
    cw = max(w for w in range(128, 1664 + 1, 128) if dm["SH"] % w == 0)
    cur = pl.BlockSpec((TM, cw), lambda j, i: (i, j))
    prv = pl.BlockSpec((TM, cw), lambda j, i: (jnp.maximum(i - 1, 0), j))
    nxt = pl.BlockSpec((TM, cw), lambda j, i: (jnp.minimum(i + 1, nt - 1), j))
    mu = pl.BlockSpec((4, cw), lambda j, i: (0, j))
    return nt, cw, cur, prv, nxt, mu


def _shift_fwd(p_rw, mu):
    dm = _dims()
    nt, cw, cur, prv, nxt, mus = _shift_specs()

    def body(c_ref, p_ref, n_ref, mu_ref, o_ref):
        i = pl.program_id(1)
        p, m = c_ref[...], mu_ref[...]
        left, right = _shift_lr(p, i)
        up, down = _shift_ud(p, p_ref[...], n_ref[...], i)
        vert = jnp.where(i < dm["NCT"], 0.0, 1.0)
        o_ref[...] = (p * (1.0 - m[0:1] - m[1:2] - vert * (m[2:3] + m[3:4]))
                      + m[0:1] * left + m[1:2] * right + m[2:3] * up + m[3:4] * down)

    return _call(body, "shift_fwd", (dm["SH"] // cw, nt), [cur, prv, nxt, mus], cur, _sds((dm["TT"], dm["SH"])),
                 sem=("parallel", "parallel"))(p_rw, p_rw, p_rw, mu)


def _shift_bwd(p_rw, dsh, mu):
    dm = _dims()
    nt, cw, cur, prv, nxt, mus = _shift_specs()

    def body(c_ref, p_ref, n_ref, gc_ref, gp_ref, gn_ref, mu_ref, dp_ref, dmu_ref):
        i = pl.program_id(1)
        p, g, m = c_ref[...], gc_ref[...], mu_ref[...]
        vert = jnp.where(i < dm["NCT"], 0.0, 1.0)
        _, from_right = _shift_lr(m[0:1] * g, i)
        from_left, _ = _shift_lr(m[1:2] * g, i)
        _, from_down = _shift_ud(m[2:3] * g, g, m[2:3] * gn_ref[...], i)
        from_up, _ = _shift_ud(m[3:4] * g, m[3:4] * gp_ref[...], g, i)
        dp = g * (1.0 - m[0:1] - m[1:2] - vert * (m[2:3] + m[3:4])) + from_right + from_left + from_down + from_up
        dp_ref[...] = dp.astype(BF16)

        @pl.when(i == 0)
        def _():
            dmu_ref[...] = jnp.zeros_like(dmu_ref)

        left, right = _shift_lr(p, i)
        up, down = _shift_ud(p, p_ref[...], n_ref[...], i)
        s = lambda a: jnp.sum(a, axis=0, keepdims=True)
        dmu_ref[...] += jnp.concatenate([s(g * (left - p)), s(g * (right - p)), vert * s(g * (up - p)), vert * s(g * (down - p))], axis=0)

    return _call(body, "shift_bwd", (dm["SH"] // cw, nt), [cur, prv, nxt, cur, prv, nxt, mus],
                 [cur, pl.BlockSpec((4, cw), lambda j, i: (0, j))], [_sds((dm["TT"], dm["SH"]), BF16), _sds((4, dm["SH"]))],
                 sem=("parallel", "arbitrary"))(p_rw, p_rw, p_rw, dsh, dsh, dsh, mu)


def _acc_out(ref, first, vals):
    @pl.when(first)
    def _():
        for r in ref:
            r[...] = jnp.zeros_like(r)

    for r, v in zip(ref, vals):
        r[...] += v


def _rowsum(a):
    return jnp.sum(a, axis=0, keepdims=True)


def _rw_prep_specs():
    dm = _dims()
    rw = RWW // 128
    tok = lambda width, cb: pl.BlockSpec((TMV, width), lambda i: (i, cb))
    ins = [pl.BlockSpec((TMV, RWW), lambda i: (i, 1)), tok(128, 3 * rw), tok(128, 3 * rw + 1),
           _full((2, RWW)), _full((128, RWW)), _full((128, RWW)), _full((2, RWW)), _full((128, RWW)), _full((128, RWW)),
           _full((1, RWW)), _full((1, RWW))]
    return dm, ins


def _rw_prep_fwd(sh, w0, w2p0, w2p1, a0, a2p0, a2p1, kkp, kap_p):
    dm, ins = _rw_prep_specs()
    tt = dm["TT"]

    def body(k_ref, low_ref, loa_ref, w0_ref, w20_ref, w21_ref, a0_ref, a20_ref, a21_ref, kk_ref, ka_ref, kap_ref, lw_ref, b_ref, kd_ref):
        tw, la = jnp.tanh(low_ref[...]), loa_ref[...]
        kap, lw0, b0, kd0, lw1, b1, kd1 = _rw_prep_math(
            k_ref[...], _mm1(tw, w20_ref[...]), _mm1(tw, w21_ref[...]), _mm1(la, a20_ref[...]), _mm1(la, a21_ref[...]),
            w0_ref[...], a0_ref[...], kk_ref[...], ka_ref[...])
        kap_ref[...] = kap
        lw_ref[0], lw_ref[1] = lw0, lw1
        b_ref[0], b_ref[1] = b0, b1
        kd_ref[0], kd_ref[1] = kd0, kd1

    two = pl.BlockSpec((2, TMV, RWW), lambda i: (0, i, 0))
    return _call(body, "rw_prep_fwd", (tt // TMV,), ins, [pl.BlockSpec((TMV, RWW), lambda i: (i, 0)), two, two, two],
                 [_sds((tt, RWW)), _sds((2, tt, RWW)), _sds((2, tt, RWW)), _sds((2, tt, RWW))], sem=("parallel",))(
        sh, sh, sh, w0, w2p0, w2p1, a0, a2p0, a2p1, kkp, kap_p)


def _rw_prep_bwd(sh, w0, w2p0, w2p1, a0, a2p0, a2p1, kkp, kap_p, scan_grads, dkd_p, dr_p, dv_p):
    dm, ins = _rw_prep_specs()
    tt, sh_w = dm["TT"], dm["SH"]
    one = pl.BlockSpec((TMV, RWW), lambda i: (i, 0))

    def body(k_ref, low_ref, loa_ref, w0_ref, w20_ref, w21_ref, a0_ref, a20_ref, a21_ref, kk_ref, ka_ref,
             dr0, dr1, dv0, dv1, dka0, dka1, dlw0, dlw1, db0, db1, dkd0, dkd1, dkdp_ref, drp_ref, dvp_ref,
             dsh_ref, dw0_ref, dw20_ref, dw21_ref, da0_ref, da20_ref, da21_ref, dkk_ref, dka_ref):
        tw, la = jnp.tanh(low_ref[...]), loa_ref[...]
        w20, w21, a20, a21 = w20_ref[...], w21_ref[...], a20_ref[...], a21_ref[...]
        _, vjp = jax.vjp(_rw_prep_math, k_ref[...], _mm1(tw, w20), _mm1(tw, w21), _mm1(la, a20), _mm1(la, a21),
                         w0_ref[...], a0_ref[...], kk_ref[...], ka_ref[...])
        dkp = dkdp_ref[...]
        dk, dpw0, dpw1, dpa0, dpa1, dw0, da0, dkk, dka = vjp(
            (dka0[...] + dka1[...], dlw0[...], db0[...], dkd0[...] + dkp, dlw1[...], db1[...], dkd1[...] + dkp))
        twt, lat = tw.T, la.T
        dtw = _mm1_nt(dpw0, w20) + _mm1_nt(dpw1, w21)
        dsh_ref[:, 0:RWW] = dr0[...] + dr1[...] + drp_ref[...]
        dsh_ref[:, RWW:2 * RWW] = dk
        dsh_ref[:, 2 * RWW:3 * RWW] = dv0[...] + dv1[...] + dvp_ref[...]
        dsh_ref[:, 3 * RWW:3 * RWW + 128] = dtw * (1.0 - tw * tw)
        dsh_ref[:, 3 * RWW + 128:3 * RWW + 256] = _mm1_nt(dpa0, a20) + _mm1_nt(dpa1, a21)
        _acc_out((dw0_ref, dw20_ref, dw21_ref, da0_ref, da20_ref, da21_ref, dkk_ref, dka_ref), pl.program_id(0) == 0,
                 (dw0, _mm1(twt, dpw0), _mm1(twt, dpw1), da0, _mm1(lat, dpa0), _mm1(lat, dpa1), dkk, dka))

    par = [_full((2, RWW)), _full((128, RWW)), _full((128, RWW)), _full((2, RWW)), _full((128, RWW)), _full((128, RWW)),
           _full((1, RWW)), _full((1, RWW))]
    return _call(body, "rw_prep_bwd", (tt // TMV,), ins + [one] * 15,
                 [pl.BlockSpec((TMV, sh_w), lambda i: (i, 0))] + par,
                 [_sds((tt, sh_w)), _sds((2, RWW)), _sds((128, RWW)), _sds((128, RWW)), _sds((2, RWW)), _sds((128, RWW)), _sds((128, RWW)),
                  _sds((1, RWW)), _sds((1, RWW))],
                 sem=("arbitrary",))(sh, sh, sh, w0, w2p0, w2p1, a0, a2p0, a2p1, kkp, kap_p, *scan_grads, dkd_p, dr_p, dv_p)


def _rw_post_specs():
    dm = _dims()
    nv = dm["NCTV"]
    rw = RWW // 128
    lat3 = lambda d: pl.BlockSpec((None, TMV, RWW), lambda i: (d, nv + i, 0))
    lat2 = pl.BlockSpec((TMV, RWW), lambda i: (nv + i, 0))
    ins = [lat2, lat2, lat3(0), lat3(1),
           pl.BlockSpec((TMV, RWW), lambda i: (nv + i, 0)), pl.BlockSpec((TMV, RWW), lambda i: (nv + i, 2)),
           pl.BlockSpec((TMV, RWW), lambda i: (i, 0)), _full((1, RWW)), _full((1, RWW)), _full((1, RWW))]
    return dm, nv, ins


def _rw_post_fwd(y0_, y1_, kd, sh, p_z, rk, gng, gnb):
    dm, nv, ins = _rw_post_specs()

    def body(y0, y1, k0, k1, r, v, z, rk_ref, g_ref, b_ref, o_ref):
        o_ref[...] = _rw_post_math(y0[...], y1[...], k0[...], k1[...], r[...], v[...], z[...], rk_ref[...], g_ref[...], b_ref[...]).astype(BF16)

    return _call(body, "rw_post_fwd", (T // TMV,), ins, pl.BlockSpec((TMV, RWW), lambda i: (i, 0)), _sds((T, RWW), BF16),
                 sem=("parallel",))(y0_, y1_, kd, kd, sh, sh, p_z, rk, gng, gnb)


def _rw_post_bwd(y0_, y1_, kd, sh, p_z, rk, gng, gnb, dout):
    dm, nv, _ = _rw_post_specs()
    tt = dm["TT"]
    cl = lambda i: jnp.maximum(i - nv, 0)
    all3 = lambda d: pl.BlockSpec((None, TMV, RWW), lambda i: (d, i, 0))
    all2 = pl.BlockSpec((TMV, RWW), lambda i: (i, 0))
    ins = [all2, all2, all3(0), all3(1),
           pl.BlockSpec((TMV, RWW), lambda i: (i, 0)), pl.BlockSpec((TMV, RWW), lambda i: (i, 2)),
           pl.BlockSpec((TMV, RWW), lambda i: (cl(i), 0)), _full((1, RWW)), _full((1, RWW)), _full((1, RWW)),
           pl.BlockSpec((TMV, RWW), lambda i: (cl(i), 0))]
    tok = pl.BlockSpec((TMV, RWW), lambda i: (i, 0))

    def body(y0, y1, k0, k1, r, v, z, rk_ref, g_ref, b_ref, do_ref, dy_ref, dkd_ref, dr_ref, dv_ref, dz_ref, drk_ref, dg_ref, db_ref):
        i = pl.program_id(0)
        lat = jnp.where(i >= nv, 1.0, 0.0)
        _, vjp = jax.vjp(_rw_post_math, y0[...], y1[...], k0[...], k1[...], r[...], v[...], z[...], rk_ref[...], g_ref[...], b_ref[...])
        dy0, _, dk0, _, dr, dv, dz, drk, dg, db = vjp(do_ref[...] * lat)
        dy_ref[...] = dy0
        dkd_ref[...] = dk0
        dr_ref[...] = dr
        dv_ref[...] = dv
        dz_ref[...] = dz.astype(BF16)
        _acc_out((drk_ref, dg_ref, db_ref), i == 0, (drk, dg, db))

    return _call(body, "rw_post_bwd", (tt // TMV,), ins,
                 [tok, tok, tok, tok, pl.BlockSpec((TMV, RWW), lambda i: (cl(i), 0)), _full((1, RWW)), _full((1, RWW)), _full((1, RWW))],
                 [_sds((tt, RWW))] * 4 + [_sds((T, RWW), BF16), _sds((1, RWW)), _sds((1, RWW)), _sds((1, RWW))],
                 sem=("arbitrary",))(y0_, y1_, kd, kd, sh, sh, p_z, rk, gng, gnb, dout)


def _hg_post_fwd(o0_, o1_, p_hg, hg_g):
    dm = _dims()
    nv = dm["NCTV"]
    lat3 = lambda d: pl.BlockSpec((TMV, HGW), lambda i: (nv + i, 0))

    def body(o0, o1, z, g, out):
        out[...] = _hg_post_math(o0[...], o1[...], z[...], g[...]).astype(BF16)

    return _call(body, "hg_post_fwd", (T // TMV,), [lat3(0), lat3(1), pl.BlockSpec((TMV, HGW), lambda i: (nv + i, 4)), _full((1, HGW))],
                 pl.BlockSpec((TMV, HGW), lambda i: (i, 0)), _sds((T, HGW), BF16), sem=("parallel",))(o0_, o1_, p_hg, hg_g)


def _hg_post_bwd(o0_, o1_, p_hg, hg_g, dout):
    dm = _dims()
    nv, tt = dm["NCTV"], dm["TT"]
    all3 = lambda d: pl.BlockSpec((TMV, HGW), lambda i: (i, 0))
    tok = pl.BlockSpec((TMV, HGW), lambda i: (i, 0))

    def body(o0, o1, z, g, do_in, do_ref, dz_ref, dg_ref):
        i = pl.program_id(0)
        lat = jnp.where(i >= nv, 1.0, 0.0)
        _, vjp = jax.vjp(_hg_post_math, o0[...], o1[...], z[...], g[...])
        d0, _, dz, dg = vjp(do_in[...] * lat)
        do_ref[...] = d0
        dz_ref[...] = dz
        _acc_out((dg_ref,), i == 0, (dg,))

    return _call(body, "hg_post_bwd", (tt // TMV,),
                 [all3(0), all3(1), pl.BlockSpec((TMV, HGW), lambda i: (i, 4)), _full((1, HGW)),
                  pl.BlockSpec((TMV, HGW), lambda i: (jnp.maximum(i - nv, 0), 0))],
                 [tok, tok, _full((1, HGW))], [_sds((tt, HGW)), _sds((tt, HGW)), _sds((1, HGW))], sem=("arbitrary",))(o0_, o1_, p_hg, hg_g, dout)


def _hg_dproj(dq0, dq1, di0, di1, df0, df1, dz):
    dm = _dims()
    tt = dm["TT"]
    tok = pl.BlockSpec((TMV, HGW), lambda i: (i, 0))

    def body(q0, q1, i0, i1, f0, f1, z, o_ref):
        o_ref[:, 0:HGW] = (q0[...] + q1[...]).astype(BF16)
        o_ref[:, HGW:2 * HGW] = (i0[...] + i1[...]).astype(BF16)
        o_ref[:, 2 * HGW:3 * HGW] = f0[...].astype(BF16)
        o_ref[:, 3 * HGW:4 * HGW] = f1[...].astype(BF16)
        o_ref[:, 4 * HGW:5 * HGW] = z[...].astype(BF16)

    return _call(body, "hg_dproj", (tt // TMV,), [tok] * 7,
                 pl.BlockSpec((TMV, 5 * HGW), lambda i: (i, 0)), _sds((tt, 5 * HGW), BF16), sem=("parallel",))(
        dq0, dq1, di0, di1, df0, df1, dz)


def _merge1_fwd(out_hg, out_rw, whg_t, wrw_t, p_g):
    tok = pl.BlockSpec((TM, D), lambda i: (i, 0))

    def body(h_ref, r_ref, wh_ref, wr_ref, g1_ref, g2_ref, a_ref, b_ref, m_ref):
        nt = (((1,), (1,)), ((), ()))
        a = lax.dot_general(h_ref[...], wh_ref[...], nt, preferred_element_type=F32)
        b = lax.dot_general(r_ref[...], wr_ref[...], nt, preferred_element_type=F32)
        a_ref[...] = a
        b_ref[...] = b
        m_ref[...] = _gate_math(g1_ref[...], g2_ref[...], a, b).astype(BF16)

    return _call(body, "merge1_fwd", (T // TM,),
                 [pl.BlockSpec((TM, HGW), lambda i: (i, 0)), pl.BlockSpec((TM, RWW), lambda i: (i, 0)), _full((D, HGW)), _full((D, RWW)),
                  pl.BlockSpec((TM, D), lambda i: (i, 0)), pl.BlockSpec((TM, D), lambda i: (i, 1))],
                 [tok, tok, tok], [_sds((T, D)), _sds((T, D)), _sds((T, D), BF16)], sem=("parallel",))(out_hg, out_rw, whg_t, wrw_t, p_g, p_g)


def _head_fwd_bwd(x, merged, w_out, mods, final_g, tgt):
    tok = pl.BlockSpec((TM, D), lambda i: (i, 0))

    def body(x_ref, m_ref, w_ref, mods_ref, fg_ref, t_ref, dxo_ref, dmo_ref, loss_ref, dgate_ref, dfg_ref):
        mo = jnp.dot(m_ref[...], w_ref[...], preferred_element_type=F32)
        gate = mods_ref[0:1, 2 * D:3 * D]
        loss, vjp = jax.vjp(lambda x_, mo_, g_, fg_: _head_math(x_, mo_, g_, fg_, t_ref[...]), x_ref[...], mo, gate, fg_ref[...])
        dx, dmo, dgate, dfg = vjp(jnp.ones((1, 1), F32))
        dxo_ref[...] = dx
        dmo_ref[...] = dmo.astype(BF16)
        _acc_out((loss_ref, dgate_ref, dfg_ref), pl.program_id(0) == 0, (jnp.broadcast_to(loss, (1, 128)), dgate, dfg))

    return _call(body, "head_fwd_bwd", (T // TM,), [tok, tok, _full((D, D)), _full((8, 3 * D)), _full((1, D)), tok],
                 [tok, tok, _full((1, 128)), _full((1, D)), _full((1, D))],
                 [_sds((T, D)), _sds((T, D), BF16), _sds((1, 128)), _sds((1, D)), _sds((1, D))], sem=("arbitrary",))(
        x, merged, w_out, mods, final_g, tgt)


def _merge_bwd(dmo, w_out, p_g, a, b):
    tok = pl.BlockSpec((TM, D), lambda i: (i, 0))

    def body(dmo_ref, w_ref, g1_ref, g2_ref, a_ref, b_ref, da_ref, db_ref, dg_ref):
        dm_ = lax.dot_general(dmo_ref[...], w_ref[...], (((1,), (1,)), ((), ())), preferred_element_type=F32)
        _, vjp = jax.vjp(_gate_math, g1_ref[...], g2_ref[...], a_ref[...], b_ref[...])
        dg1, dg2, da, db = vjp(dm_)
        da_ref[...] = da.astype(BF16)
        db_ref[...] = db.astype(BF16)
        dg_ref[:, 0:D] = dg1.astype(BF16)
        dg_ref[:, D:2 * D] = dg2.astype(BF16)

    return _call(body, "merge_bwd", (T // TM,),
                 [tok, _full((D, D)), pl.BlockSpec((TM, D), lambda i: (i, 0)), pl.BlockSpec((TM, D), lambda i: (i, 1)), tok, tok],
                 [tok, tok, pl.BlockSpec((TM, 2 * D), lambda i: (i, 0))], [_sds((T, D), BF16), _sds((T, D), BF16), _sds((T, 2 * D), BF16)],
                 sem=("parallel",))(dmo, w_out, p_g, p_g, a, b)


def _h_bwd(ctx, x, mods, norm_g, dh_hg, dh_rw, dh_z, dh_g, dxo, dgate):
    dm = _dims()
    nct, tt = dm["NCT"], dm["TT"]
    tok = pl.BlockSpec((TM, D), lambda i: (i, 0))
    lat = pl.BlockSpec((TM, D), lambda i: (jnp.maximum(i - nct, 0), 0))

    def body(c_ref, x_ref, m_ref, g_ref, d1, d2, d3, d4, dxo_ref, dgate_ref, gx_ref, dng_ref, dmods_ref):
        i = pl.program_id(0)
        is_ctx = i < nct
        latf = jnp.where(is_ctx, 0.0, 1.0)
        dh = d1[...] + d2[...] + latf * (d3[...] + d4[...])
        _, vjp = jax.vjp(_h_math, jnp.where(is_ctx, c_ref[...], x_ref[...]), g_ref[...], _sel_mod(m_ref, is_ctx, D),
                         _sel_mod(m_ref, is_ctx, 0))
        dx, dng, dscale, dshift = vjp(dh)
        gx_ref[...] = dx + dxo_ref[...]

        @pl.when(i == 0)
        def _():
            dng_ref[...] = jnp.zeros_like(dng_ref)
            dmods_ref[...] = jnp.zeros_like(dmods_ref)
            dmods_ref[0:1, 2 * D:3 * D] = dgate_ref[...]

        dng_ref[...] += dng
        row = lax.broadcasted_iota(jnp.int32, (8, 1), 0)
        sel = jnp.where(row == jnp.where(is_ctx, 1, 0), 1.0, 0.0)
        dmods_ref[:, 0:D] += sel * dshift
        dmods_ref[:, D:2 * D] += sel * dscale

    return _call(body, "h_bwd", (tt // TM,),
                 [*_token_specs(), _full((8, 3 * D)), _full((1, D)), tok, tok, lat, lat, lat, _full((1, D))],
                 [lat, _full((1, D)), _full((8, 3 * D))], [_sds((T, D)), _sds((1, D)), _sds((8, 3 * D))], sem=("arbitrary",))(
        ctx, x, mods, norm_g, dh_hg, dh_rw, dh_z, dh_g, dxo, dgate)


def _lb_math(l0, l1):
    return jax.nn.sigmoid(l0 - l1)


def _lb_fwd(l0, l1):
    def body(a_ref, b_ref, o_ref):
        o_ref[...] = _lb_math(a_ref[...], b_ref[...])

    return _call(body, "lb_fwd", (1,), [_full((2, HGW))] * 2, _full((2, HGW)), _sds((2, HGW)))(l0, l1)


def _lb_bwd(l0, l1, dlb):
    def body(a_ref, b_ref, d_ref, da_ref, db_ref):
        _, vjp = jax.vjp(_lb_math, a_ref[...], b_ref[...])
        da_ref[...], db_ref[...] = vjp(d_ref[...])

    return _call(body, "lb_bwd", (1,), [_full((2, HGW))] * 3, [_full((2, HGW))] * 2, [_sds((2, HGW))] * 2)(l0, l1, dlb)


def _local_step(x, ctx, tgt, mods, w, start_exchange):
    dm = _dims()
    tt, sh_w, hgc = dm["TT"], dm["SH"], dm["HGC"]
    nb = lambda cols: cols // WB
    h = _h_fwd(ctx, x, mods, w["norm_g"])
    win_t = w["win_t"]
    p_hg = _proj_nt(h, win_t, 0, nb(hgc), 0, tt, "proj_hg")
    p_rw = _proj_nt(h, win_t, nb(hgc), nb(sh_w), 0, tt, "proj_rw")
    p_z = _proj_nt(h, win_t, nb(hgc + sh_w), nb(RWW), TC, T, "proj_z")
    p_g = _proj_nt(h, win_t, nb(hgc + sh_w + RWW), nb(2 * D), TC, T, "proj_g")

    lb3 = _lb_fwd(w["lb0"], w["lb1"]).reshape(2, 1, HGW)
    o0, o1, hg_ck = _hg_scan_fwd(p_hg, lb3)
    out_hg = _hg_post_fwd(o0, o1, p_hg, w["hg_g"])

    sh = _shift_fwd(p_rw, w["mu"])
    zpad = jnp.zeros((LORA, RWW), F32)
    prep_w = (w["w0"], jnp.concatenate([w["w2"][0], zpad], 0), jnp.concatenate([zpad, w["w2"][1]], 0),
              w["a0"], jnp.concatenate([w["a2"][0], zpad], 0), jnp.concatenate([zpad, w["a2"][1]], 0), w["kk"], w["ka"])
    kap, lw, b, kd = _rw_prep_fwd(sh, *prep_w)
    y0, y1, rw_ck, rw_tm = _rw_scan_fwd(sh, kap, lw, b, kd)
    post_w = (w["rk"], w["gng"], w["gnb"])
    out_rw = _rw_post_fwd(y0, y1, kd, sh, p_z, *post_w)

    late = w["late_weights"](out_rw)
    a, bb, merged = _merge1_fwd(out_hg, out_rw, late["whg_t"], late["wrw_t"], p_g)
    dxo, dmo, loss, dgate, d_fg = _head_fwd_bwd(x, merged, late["wout"], mods, w["final_g"], tgt)
    da, db, dp_g = _merge_bwd(dmo, late["wout"], p_g, a, bb)
    g_wout = _dw_tn(merged, dmo, 0, D, 0, "dw_out")
    g_whg = _dw_tn(da, out_hg, 0, D, 0, "dw_hg")
    g_wrw = _dw_tn(db, out_rw, 0, D, 0, "dw_rw")
    token, early = start_exchange(dict(wout=g_wout, whg_t=g_whg, wrw_t=g_wrw), "early")
    d_out_hg = _mm_nn(da, late["whg_t"], 0, "dx_hg", after=token)
    d_out_rw = _mm_nn(db, late["wrw_t"], 0, "dx_rw")

    do, dz_hg, d_hg_g = _hg_post_bwd(o0, o1, p_hg, w["hg_g"], d_out_hg)
    *hg_grads, dlb3 = _hg_scan_bwd(p_hg, lb3, hg_ck, do)
    dp_hg = _hg_dproj(*hg_grads, dz_hg)
    d_lb0, d_lb1 = _lb_bwd(w["lb0"], w["lb1"], dlb3.reshape(2, HGW))

    dy, dkd_p, dr_p, dv_p, dp_z, d_rk, d_gng, d_gnb = _rw_post_bwd(y0, y1, kd, sh, p_z, *post_w, d_out_rw)
    rw_grads = _rw_scan_bwd(sh, kap, lw, b, kd, rw_ck, rw_tm, dy)
    dsh, d_w0, d_w2p0, d_w2p1, d_a0, d_a2p0, d_a2p1, d_kk, d_ka = _rw_prep_bwd(sh, *prep_w, rw_grads, dkd_p, dr_p, dv_p)
    dp_rw, d_mu = _shift_bwd(p_rw, dsh, w["mu"])

    g_win = _dw_tn(dp_hg, h, 0, dm["NCOLS"], 0, "dw_in_hg")
    g_win = _dw_tn(dp_rw, h, 0, dm["NCOLS"], nb(hgc), "dw_in_rw", prev=g_win)
    g_win = _dw_tn(dp_z, h, TC, dm["NCOLS"], nb(hgc + sh_w), "dw_in_z", prev=g_win)
    g_win = _dw_tn(dp_g, h, TC, dm["NCOLS"], nb(hgc + sh_w + RWW), "dw_in_g", prev=g_win)
    token, late = start_exchange(dict(win_t=g_win), "late")
    in_flight = (early, late)
    dh_hg = _mm_nn(dp_hg, win_t, 0, "dh_hg", after=token)
    dh_rw = _mm_nn(dp_rw, win_t, nb(hgc), "dh_rw")
    dh_z = _mm_nn(dp_z, win_t, nb(hgc + sh_w), "dh_z")
    dh_g = _mm_nn(dp_g, win_t, nb(hgc + sh_w + RWW), "dh_g")
    gx, d_ng, dmods = _h_bwd(ctx, x, mods, w["norm_g"], dh_hg, dh_rw, dh_z, dh_g, dxo, dgate)

    small = dict(norm_g=d_ng, lb0=d_lb0, lb1=d_lb1, hg_g=d_hg_g, mu=d_mu, w0=d_w0,
                 w2=jnp.stack([d_w2p0[:LORA], d_w2p1[LORA:]]), a0=d_a0, a2=jnp.stack([d_a2p0[:LORA], d_a2p1[LORA:]]),
                 kk=d_kk, ka=d_ka, rk=d_rk, gng=d_gng, gnb=d_gnb, final_g=d_fg)
    return loss, gx, dmods, in_flight, small


MESH = pl.DeviceIdType.MESH


def _comm_call(body, name, bufs, out_shapes, nsem, nloc):
    hbm = pl.BlockSpec(memory_space=pl.ANY)
    return pl.pallas_call(
        body, name=name, in_specs=[hbm] * len(bufs), out_specs=[hbm] * len(out_shapes), out_shape=out_shapes,
        scratch_shapes=[pltpu.SemaphoreType.DMA((nsem,)), pltpu.SemaphoreType.DMA((nsem,)), pltpu.SemaphoreType.DMA((nloc,))],
    )(*bufs)


GATHER_PIECE_BYTES = 4 * 1024 * 1024


def _gather2(bufs, name):
    pieces = []
    for bi, b in enumerate(bufs):
        rows = b.shape[0]
        if rows % 32 == 0 and b.size * b.dtype.itemsize > GATHER_PIECE_BYTES:
            pieces += [(bi, 0, rows // 2), (bi, rows // 2, rows // 2)]
        else:
            pieces.append((bi, 0, rows))
    nbuf = len(bufs)

    def body(*refs):
        ins, outs = refs[:nbuf], refs[nbuf:2 * nbuf]
        send_sems, recv_sems, _ = refs[2 * nbuf:]
        x, y, c = lax.axis_index("x"), lax.axis_index("y"), lax.axis_index("c")
        me, sib = (x, y, c), (x, y, 1 - c)
        flip = lambda a, b: a + b - 2 * a * b
        xn, yn, diag = (1 - x, y), (x, 1 - y), (1 - x, 1 - y)
        relay_from, relay_to = (flip(x, c), flip(1 - y, c)), (flip(1 - x, c), flip(y, c))

        def copy(pi, k, block, to, own=False):
            bi, r0, nr = pieces[pi]
            rows = outs[bi].at[4 * block[0] + 2 * block[1] + block[2], pl.ds(r0, nr)]
            return pltpu.make_async_remote_copy(src_ref=ins[bi].at[pl.ds(r0, nr)] if own else rows, dst_ref=rows,
                                                send_sem=send_sems.at[7 * pi + k], recv_sem=recv_sems.at[7 * pi + k],
                                                device_id=to, device_id_type=MESH)

        started = []
        for pi in range(len(pieces)):
            started += [copy(pi, 0, me, sib, own=True), copy(pi, 1, me, (*xn, c), own=True), copy(pi, 2, me, (*yn, c), own=True)]
        for cp in started:
            cp.start()
        for pi in range(len(pieces)):
            for k, chip in ((1, xn), (2, yn)):
                copy(pi, k, (*chip, c), me).wait_recv()
                started.append(copy(pi, 3 + k, (*chip, c), sib))
                started[-1].start()
            started.append(copy(pi, 3, (*relay_from, c), (*relay_to, c)))
            started[-1].start()
        for pi in range(len(pieces)):
            copy(pi, 3, (*diag, c), me).wait_recv()
            started.append(copy(pi, 6, (*diag, c), sib))
            started[-1].start()
        for pi in range(len(pieces)):
            copy(pi, 0, sib, me).wait_recv()
            for k, chip in ((4, xn), (5, yn), (6, diag)):
                copy(pi, k, (*chip, 1 - c), me).wait_recv()
        for cp in started:
            cp.wait_send()

    return _comm_call(body, name, bufs, [_sds((NDEV,) + b.shape, b.dtype) for b in bufs], 7 * len(pieces), 1)


def _gather_all(bufs, name):
    me = 4 * lax.axis_index("x") + 2 * lax.axis_index("y") + lax.axis_index("c")
    return [lax.dynamic_update_index_in_dim(g, b, me, axis=0) for g, b in zip(_gather2(bufs, name), bufs)]


def _peer_copies(srcs, lands, send_sems, recv_sems):
    x, y, c = lax.axis_index("x"), lax.axis_index("y"), lax.axis_index("c")
    me = 4 * x + 2 * y + c
    pairs = []
    for bi in range(len(srcs)):
        for k in range(1, NDEV):
            px, py, pc = (1 - x if k & 4 else x), (1 - y if k & 2 else y), (1 - c if k & 1 else c)
            mk = lambda dst: pltpu.make_async_remote_copy(
                src_ref=srcs[bi], dst_ref=dst, send_sem=send_sems.at[7 * bi + k - 1], recv_sem=recv_sems.at[7 * bi + k - 1],
                device_id=(px, py, pc), device_id_type=MESH)
            pairs.append((mk(lands[bi].at[me]), mk(lands[bi].at[4 * px + 2 * py + pc])))
    return pairs


def _direct_gather_start(bufs, dep, name):
    nbuf = len(bufs)
    hbm, sem = pl.BlockSpec(memory_space=pltpu.HBM), pl.BlockSpec(memory_space=pltpu.SEMAPHORE)

    def body(*refs):
        srcs, lands = refs[:nbuf], refs[nbuf:2 * nbuf]
        for send, _ in _peer_copies(srcs, lands, refs[2 * nbuf + 1], refs[2 * nbuf + 2]):
            send.start()
        refs[-1][...] = jnp.zeros_like(refs[-1])

    held = [pltpu.HBM(b.shape, b.dtype) for b in bufs]
    zones = [pltpu.HBM((NDEV,) + b.shape, b.dtype) for b in bufs]
    outs = pl.pallas_call(
        body, name=name, in_specs=[hbm] * (2 * nbuf) + [pl.BlockSpec(memory_space=pl.ANY)],
        out_specs=(sem, sem, *[hbm] * (2 * nbuf), pl.BlockSpec(memory_space=pltpu.VMEM)),
        out_shape=(pltpu.SemaphoreType.DMA((7 * nbuf,)), pltpu.SemaphoreType.DMA((7 * nbuf,)), *held, *zones, _sds((8, 128))),
        input_output_aliases={i: 2 + i for i in range(2 * nbuf)},
        compiler_params=pltpu.CompilerParams(has_side_effects=pltpu.SideEffectType.DATAFLOW_SIDE_EFFECTING),
    )(*[pltpu.with_memory_space_constraint(b, pltpu.HBM) for b in bufs],
      *[pltpu.with_memory_space_constraint(lax.empty((NDEV,) + b.shape, b.dtype), pltpu.HBM) for b in bufs], dep)
    return (outs[0], outs[1], list(outs[2:2 + nbuf]), list(outs[2 + nbuf:2 + 2 * nbuf])), outs[-1]


def _direct_gather_wait(send_sems, recv_sems, srcs, lands, after, name):
    nbuf = len(srcs)
    hbm, sem = pl.BlockSpec(memory_space=pltpu.HBM), pl.BlockSpec(memory_space=pltpu.SEMAPHORE)

    def body(*refs):
        for send, recv in _peer_copies(refs[:nbuf], refs[nbuf:2 * nbuf], refs[2 * nbuf], refs[2 * nbuf + 1]):
            send.wait_send()
            recv.wait_recv()

    outs = pl.pallas_call(
        body, name=name, in_specs=[hbm] * (2 * nbuf) + [sem, sem, pl.BlockSpec(memory_space=pl.ANY)], out_specs=[hbm] * (2 * nbuf),
        out_shape=(*[pltpu.HBM(b.shape, b.dtype) for b in srcs], *[pltpu.HBM(b.shape, b.dtype) for b in lands]),
        input_output_aliases={i: i for i in range(2 * nbuf)},
        compiler_params=pltpu.CompilerParams(has_side_effects=pltpu.SideEffectType.DATAFLOW_SIDE_EFFECTING),
    )(*srcs, *lands, send_sems, recv_sems, after)
    me = 4 * lax.axis_index("x") + 2 * lax.axis_index("y") + lax.axis_index("c")
    return [lax.dynamic_update_index_in_dim(g, b, me, axis=0) for g, b in zip(outs[nbuf:], outs[:nbuf])]


def _pair_exchange(bufs, name):
    nbuf = len(bufs)

    def body(*refs):
        ins, got = refs[:nbuf], refs[nbuf:2 * nbuf]
        send_sems, recv_sems, _ = refs[2 * nbuf:]
        x, y, c = lax.axis_index("x"), lax.axis_index("y"), lax.axis_index("c")
        copies = []
        for bi in range(nbuf):
            for q in range(4):
                copies.append(pltpu.make_async_remote_copy(
                    src_ref=ins[bi].at[2 * q + 1 - c], dst_ref=got[bi].at[q], send_sem=send_sems.at[4 * bi + q],
                    recv_sem=recv_sems.at[4 * bi + q], device_id=(x, y, 1 - c), device_id_type=MESH))
                copies[-1].start()
        for cp in copies:
            cp.wait_send()
            cp.wait_recv()

    return _comm_call(body, name, bufs, [_sds((4,) + b.shape[1:], b.dtype) for b in bufs], 4 * nbuf, 1)


def _chip_copies(srcs, lands, send_sems, recv_sems):
    x, y, c = lax.axis_index("x"), lax.axis_index("y"), lax.axis_index("c")
    myq = 2 * x + y
    pairs = []
    for bi in range(len(srcs)):
        for j, (qx, qy) in enumerate([(1 - x, y), (x, 1 - y), (1 - x, 1 - y)]):
            q = 2 * qx + qy
            mk = lambda src, dst: pltpu.make_async_remote_copy(
                src_ref=src, dst_ref=dst, send_sem=send_sems.at[3 * bi + j], recv_sem=recv_sems.at[3 * bi + j],
                device_id=(qx, qy, c), device_id_type=MESH)
            pairs.append((mk(srcs[bi].at[q], lands[bi].at[myq]), mk(srcs[bi].at[myq], lands[bi].at[q])))
    return pairs


def _chip_exchange_start(bufs, name):
    nbuf = len(bufs)
    hbm, sem = pl.BlockSpec(memory_space=pltpu.HBM), pl.BlockSpec(memory_space=pltpu.SEMAPHORE)

    def body(*refs):
        srcs, lands = refs[:nbuf], refs[nbuf:2 * nbuf]
        send_sems, recv_sems = refs[2 * nbuf], refs[2 * nbuf + 1]
        token = refs[-1]
        for send, _ in _chip_copies(srcs, lands, send_sems, recv_sems):
            send.start()
        token[...] = jnp.zeros_like(token)

    held = [pltpu.HBM(b.shape, b.dtype) for b in bufs]
    outs = pl.pallas_call(
        body, name=name, in_specs=[hbm] * (2 * nbuf), out_specs=(sem, sem, *[hbm] * (2 * nbuf), pl.BlockSpec(memory_space=pltpu.VMEM)),
        out_shape=(pltpu.SemaphoreType.DMA((3 * nbuf,)), pltpu.SemaphoreType.DMA((3 * nbuf,)), *held, *held, _sds((8, 128))),
        input_output_aliases={i: 2 + i for i in range(2 * nbuf)},
        compiler_params=pltpu.CompilerParams(has_side_effects=pltpu.SideEffectType.DATAFLOW_SIDE_EFFECTING),
    )(*[pltpu.with_memory_space_constraint(b, pltpu.HBM) for b in bufs],
      *[pltpu.with_memory_space_constraint(lax.empty(b.shape, b.dtype), pltpu.HBM) for b in bufs])
    return outs[0], outs[1], list(outs[2:2 + nbuf]), list(outs[2 + nbuf:2 + 2 * nbuf]), outs[-1]


def _chip_exchange_wait(send_sems, recv_sems, srcs, lands, after, name):
    nbuf = len(srcs)
    hbm, sem = pl.BlockSpec(memory_space=pltpu.HBM), pl.BlockSpec(memory_space=pltpu.SEMAPHORE)

    def body(*refs):
        src_refs, land_refs = refs[:nbuf], refs[nbuf:2 * nbuf]
        for send, recv in _chip_copies(src_refs, land_refs, refs[2 * nbuf], refs[2 * nbuf + 1]):
            send.wait_send()
            recv.wait_recv()

    held = [pltpu.HBM(b.shape, b.dtype) for b in srcs]
    outs = pl.pallas_call(
        body, name=name, in_specs=[hbm] * (2 * nbuf) + [sem, sem, pl.BlockSpec(memory_space=pl.ANY)], out_specs=[hbm] * (2 * nbuf),
        out_shape=(*held, *held), input_output_aliases={i: i for i in range(2 * nbuf)},
        compiler_params=pltpu.CompilerParams(has_side_effects=pltpu.SideEffectType.DATAFLOW_SIDE_EFFECTING),
    )(*srcs, *lands, send_sems, recv_sems, after)
    return list(outs[:nbuf]), list(outs[nbuf:])


def _prefetch_call(body, name, scalars, grid, in_specs, out_specs, out_shape, args):
    return pl.pallas_call(
        body, name=name, out_shape=out_shape,
        grid_spec=pltpu.PrefetchScalarGridSpec(num_scalar_prefetch=1, grid=grid, in_specs=in_specs, out_specs=out_specs),
        compiler_params=pltpu.CompilerParams(dimension_semantics=("parallel",) * len(grid), vmem_limit_bytes=V7X_VMEM_LIMIT))(scalars, *args)


def _pair_add(g, got, core, name):
    _, rows, cols = got.shape
    tr = _row_tile(rows, 16, 1024)

    def body(c_ref, a_ref, b_ref, o_ref):
        o_ref[...] = (a_ref[...].astype(F32) + b_ref[...].astype(F32)).astype(o_ref.dtype)

    blk = pl.BlockSpec((None, tr, cols), lambda q, i, c_ref: (q, i, 0))
    mine = pl.BlockSpec((None, None, tr, cols), lambda q, i, c_ref: (q, c_ref[0], i, 0))
    return _prefetch_call(body, name, core, (4, rows // tr), [mine, blk], blk, _sds(got.shape, got.dtype),
                          (g.reshape(4, 2, rows, cols), got))


def _reduce_chips(chip_sums, recv, slots, name):
    _, rows, cols = recv.shape
    cw = 256 if cols % 256 == 0 else cols

    def body(s_ref, own_ref, r1_ref, r2_ref, r3_ref, o_ref):
        o_ref[...] = ((own_ref[...].astype(F32) + r1_ref[...].astype(F32)) + r2_ref[...].astype(F32)) + r3_ref[...].astype(F32)

    pick = lambda k: pl.BlockSpec((None, rows, cw), lambda j, s_ref: (s_ref[k], 0, j))
    return _prefetch_call(body, name, slots, (cols // cw,), [pick(0), pick(1), pick(2), pick(3)],
                          pl.BlockSpec((rows, cw), lambda j, s_ref: (0, j)), _sds((rows, cols)), (chip_sums, recv, recv, recv))


def _row_tile(rows, mult, cap):
    best = mult
    for t in range(mult, cap + 1, mult):
        if rows % t == 0:
            best = t
    assert rows % best == 0, (rows, mult)
    return best


def _reduce_sources(r, name):
    nsrc, rows, cols = r.shape
    tr = _row_tile(rows, 16 if r.dtype.itemsize == 2 else 8, 128)

    def body(r_ref, o_ref):
        acc = r_ref[0].astype(F32)
        for j in range(1, nsrc):
            acc = acc + r_ref[j].astype(F32)
        o_ref[...] = acc

    return _call(body, name, (rows // tr,), [pl.BlockSpec((nsrc, tr, cols), lambda i: (0, i, 0))],
                 pl.BlockSpec((tr, cols), lambda i: (i, 0)), _sds((rows, cols)), sem=("parallel",))(r)


def _adamw(w, g, m, v, name):
    rows, cols = w.shape
    tr = _row_tile(rows, 8, 128)
    c1 = 1.0 - ADAM_B1 ** ADAM_STEP
    c2 = 1.0 - ADAM_B2 ** ADAM_STEP

    def body(w_ref, g_ref, m_ref, v_ref, d_ref, mo_ref, vo_ref):
        g_ = g_ref[...]
        m_ = ADAM_B1 * m_ref[...] + (1.0 - ADAM_B1) * g_
        v_ = ADAM_B2 * v_ref[...] + (1.0 - ADAM_B2) * (g_ * g_)
        d_ref[...] = -ADAM_LR * ((m_ / c1) / (jnp.sqrt(v_ / c2) + ADAM_EPS) + ADAM_WD * w_ref[...])
        mo_ref[...] = m_
        vo_ref[...] = v_

    blk = pl.BlockSpec((tr, cols), lambda i: (i, 0))
    return _call(body, name, (rows // tr,), [blk] * 4, [blk] * 3, [_sds((rows, cols))] * 3, sem=("parallel",))(w, g, m, v)


SLAB_PART = 1024


def _pack(arrs):
    parts = []
    for a in arrs:
        flat = a.reshape(-1)
        pad = (-flat.shape[0]) % SLAB_PART
        if pad:
            flat = jnp.concatenate([flat, jnp.zeros((pad,), flat.dtype)])
        parts.append(flat.reshape(-1, 128))
    return jnp.concatenate(parts, axis=0)


def _unpack(slab, shapes):
    outs, row = [], 0
    for s in shapes:
        n = 1
        for d in s:
            n *= d
        rows = (n + SLAB_PART - 1) // SLAB_PART * (SLAB_PART // 128)
        outs.append(slab[row:row + rows].reshape(-1)[:n].reshape(s))
        row += rows
    return outs


def _unshard_last(g, lead):
    nl = len(lead)
    return jnp.transpose(g, tuple(range(1, nl + 1)) + (0, nl + 1)).reshape(tuple(lead) + (-1,))


def kernel(x, c, ctx, c_ctx, ada_w, ada_b, norm_g, w_in, hg_lb, hg_norm_g, rw_mu, rw_w0, rw_w2, rw_a0, rw_a2, rw_kk, rw_ka, rw_rk, rw_gn_g, rw_gn_b, w_hg_out, w_rw_out, w_out, final_g, loss_target, m_c_ctx, m_ada_w, m_ada_b, m_norm_g, m_w_in, m_hg_lb, m_hg_norm_g, m_rw_mu, m_rw_w0, m_rw_w2, m_rw_a0, m_rw_a2, m_rw_kk, m_rw_ka, m_rw_rk, m_rw_gn_g, m_rw_gn_b, m_w_hg_out, m_w_rw_out, m_w_out, m_final_g, v_c_ctx, v_ada_w, v_ada_b, v_norm_g, v_w_in, v_hg_lb, v_hg_norm_g, v_rw_mu, v_rw_w0, v_rw_w2, v_rw_a0, v_rw_a2, v_rw_kk, v_rw_ka, v_rw_rk, v_rw_gn_g, v_rw_gn_b, v_w_hg_out, v_w_rw_out, v_w_out, v_final_g):
    dm = _dims()
    me = 4 * lax.axis_index("x") + 2 * lax.axis_index("y") + lax.axis_index("c")

    sharded_small = [hg_lb, rw_mu[0], rw_w0[0], rw_w2[0], rw_a0[0], rw_a2[0]]
    c_rows = jnp.concatenate([c, jnp.zeros((7, D), F32)], axis=0)
    gathered = _gather_all([w_in[0].T.astype(BF16), _pack(sharded_small), c_rows], "gather_weights")
    win_t = gathered[0].reshape(-1, D)
    per_dev = jax.vmap(lambda s: _unpack(s, [a.shape for a in sharded_small]))(gathered[1])
    hg_lb_f, mu_f, w0_f, w2_f, a0_f, a2_f = [_unshard_last(p, p.shape[1:-1]) for p in per_dev]

    ncol = ada_w.shape[2]
    cc16 = jnp.concatenate([gathered[2][:, 0], c_ctx[None], jnp.zeros((7, D), F32)], axis=0)
    mod_all = _gather_all([_ada_fwd(cc16, ada_w[0])], "gather_mods")[0]
    mod_x = lax.dynamic_index_in_dim(mod_all, me, axis=1, keepdims=False).reshape(1, -1) + ada_b
    mod_c = mod_all[:, NDEV].reshape(1, -1) + ada_b
    mods = jnp.concatenate([mod_x, mod_c, jnp.zeros((6, 3 * D), F32)], axis=0)

    late_flight, token = _direct_gather_start([w_out[0].astype(BF16), w_hg_out[0].T.astype(BF16), w_rw_out[0].T.astype(BF16)],
                                              mod_all, "gather_late_start")
    mods = mods + token[0:1, 0:1]

    def late_weights(after):
        got = _direct_gather_wait(*late_flight, after, "gather_late_wait")
        return dict(zip(("wout", "whg_t", "wrw_t"), [g.reshape(-1, g.shape[2]) for g in got]))

    w = dict(win_t=win_t, late_weights=late_weights, norm_g=norm_g,
             lb0=hg_lb_f[:, 0], lb1=hg_lb_f[:, 1], hg_g=hg_norm_g, mu=mu_f, w0=w0_f, w2=w2_f, a0=a0_f, a2=a2_f,
             kk=rw_kk, ka=rw_ka, rk=rw_rk, gng=rw_gn_g, gnb=rw_gn_b, final_g=final_g[None])
    def start_exchange(grads_bf16, tag):
        names = list(grads_bf16)
        blocks = [grads_bf16[n].reshape(NDEV, -1, grads_bf16[n].shape[1]) for n in names]
        got = _pair_exchange(blocks, "pair_grads_" + tag)
        core = lax.axis_index("c").astype(jnp.int32).reshape(1)
        chip_sums = [_pair_add(g, r, core, "pair_add_" + n) for g, r, n in zip(blocks, got, names)]
        send_sems, recv_sems, held, lands, token = _chip_exchange_start(chip_sums, "chip_grads_start_" + tag)
        return token, (names, send_sems, recv_sems, held, lands)

    loss_dev, grad_x, dmods, in_flight, small = _local_step(x[0], ctx[0], loss_target[0], mods, w, start_exchange)
    loss = lax.psum(loss_dev[0, 0], AXES)

    my_chip = (2 * lax.axis_index("x") + lax.axis_index("y")).astype(jnp.int32)
    others = jnp.arange(3, dtype=jnp.int32)
    slots = jnp.concatenate([my_chip.reshape(1), others + (others >= my_chip).astype(jnp.int32)])

    def finish_exchange(tag, flight, after):
        names, *copies = flight
        chip_sums, recv = _chip_exchange_wait(*copies, after, "chip_grads_wait_" + tag)
        return {n: _reduce_chips(cs, r, slots, "reduce_" + n) for n, cs, r in zip(names, chip_sums, recv)}

    reduced = finish_exchange("early", in_flight[0], grad_x)

    dmods_all = _gather_all([dmods], "gather_dmods")[0]
    my_cols = lambda a: lax.dynamic_slice_in_dim(a, me * ncol, ncol, axis=1)
    g_ada, dcc16, d_ada_b = _ada_bwd(cc16, ada_w[0], my_cols(dmods_all[:, 0]), my_cols(dmods_all[:, 1]), dmods)
    small = dict(small, c_ctx=dcc16[NDEV:NDEV + 1], ada_b=d_ada_b)

    small_names = ["c_ctx", "ada_b", "norm_g", "lb0", "lb1", "hg_g", "mu", "w0", "a0", "kk", "ka", "rk", "gng", "gnb", "final_g"]
    lora_names = ["w2", "a2"]
    small_all, lora_all = _gather_all([_pack([small[n] for n in small_names]), _pack([small[n] for n in lora_names]).astype(BF16)],
                                      "gather_small_grads")
    small_sum = _reduce_sources(small_all, "reduce_small")
    sg = dict(zip(small_names, _unpack(small_sum, [small[n].shape for n in small_names])))
    sg.update(zip(lora_names, _unpack(_reduce_sources(lora_all, "reduce_lora"), [small[n].shape for n in lora_names])))

    reduced.update(finish_exchange("late", in_flight[1], small_sum))
    g_win_t, g_wout, g_whg_t, g_wrw_t = [reduced[n] for n in ("win_t", "wout", "whg_t", "wrw_t")]

    def my_shard(full):
        n = full.shape[-1] // NDEV
        return lax.dynamic_slice_in_dim(full, me * n, n, axis=full.ndim - 1)

    grads = dict(
        c_ctx=sg["c_ctx"][0], ada_w=g_ada[None], ada_b=sg["ada_b"], norm_g=sg["norm_g"], w_in=g_win_t.T[None],
        hg_lb=my_shard(jnp.stack([sg["lb0"], sg["lb1"]], axis=1)), hg_norm_g=sg["hg_g"], rw_mu=my_shard(sg["mu"])[None],
        rw_w0=my_shard(sg["w0"])[None], rw_w2=my_shard(sg["w2"])[None], rw_a0=my_shard(sg["a0"])[None], rw_a2=my_shard(sg["a2"])[None],
        rw_kk=sg["kk"], rw_ka=sg["ka"], rw_rk=sg["rk"], rw_gn_g=sg["gng"], rw_gn_b=sg["gnb"],
        w_hg_out=g_whg_t.T[None], w_rw_out=g_wrw_t.T[None], w_out=g_wout[None], final_g=sg["final_g"][0])
    weights = dict(c_ctx=c_ctx, ada_w=ada_w, ada_b=ada_b, norm_g=norm_g, w_in=w_in, hg_lb=hg_lb, hg_norm_g=hg_norm_g, rw_mu=rw_mu,
                   rw_w0=rw_w0, rw_w2=rw_w2, rw_a0=rw_a0, rw_a2=rw_a2, rw_kk=rw_kk, rw_ka=rw_ka, rw_rk=rw_rk, rw_gn_g=rw_gn_g,
                   rw_gn_b=rw_gn_b, w_hg_out=w_hg_out, w_rw_out=w_rw_out, w_out=w_out, final_g=final_g)
    m_in = dict(zip(weights, (m_c_ctx, m_ada_w, m_ada_b, m_norm_g, m_w_in, m_hg_lb, m_hg_norm_g, m_rw_mu, m_rw_w0, m_rw_w2, m_rw_a0,
                              m_rw_a2, m_rw_kk, m_rw_ka, m_rw_rk, m_rw_gn_g, m_rw_gn_b, m_w_hg_out, m_w_rw_out, m_w_out, m_final_g)))
    v_in = dict(zip(weights, (v_c_ctx, v_ada_w, v_ada_b, v_norm_g, v_w_in, v_hg_lb, v_hg_norm_g, v_rw_mu, v_rw_w0, v_rw_w2, v_rw_a0,
                              v_rw_a2, v_rw_kk, v_rw_ka, v_rw_rk, v_rw_gn_g, v_rw_gn_b, v_w_hg_out, v_w_rw_out, v_w_out, v_final_g)))

    big_w = ["ada_w", "w_in", "w_hg_out", "w_rw_out", "w_out"]
    delta, new_m, new_v = {}, {}, {}
    for n in big_w:
        shp = weights[n].shape
        two = lambda a: a.reshape(shp[-2], shp[-1])
        d_, m_, v_ = _adamw(two(weights[n]), two(grads[n]), two(m_in[n]), two(v_in[n]), "adamw_" + n)
        delta[n], new_m[n], new_v[n] = d_.reshape(shp), m_.reshape(shp), v_.reshape(shp)
    rest = [n for n in weights if n not in big_w]
    shapes = [weights[n].shape for n in rest]
    d_s, m_s, v_s = _adamw(_pack([weights[n] for n in rest]), _pack([grads[n] for n in rest]), _pack([m_in[n] for n in rest]),
                           _pack([v_in[n] for n in rest]), "adamw_small")
    for n, d_, m_, v_ in zip(rest, _unpack(d_s, shapes), _unpack(m_s, shapes), _unpack(v_s, shapes)):
        delta[n], new_m[n], new_v[n] = d_, m_, v_

    order = list(weights)
    return (loss, grad_x[None], *[grads[n] for n in order], *[delta[n] for n in order],
            *[new_m[n] for n in order], *[new_v[n] for n in order])
```

```python
import functools

import jax
import jax.numpy as jnp
from jax import lax
from jax.experimental import pallas as pl
from jax.experimental.pallas import tpu as pltpu

D = 2048
T = 2048
TC = 256
GW = 64
C = 64
HGW = 1024
RWW = 1024
LORA = 64
NDEV = 8
TM = 256
TMV = 128
NORM_EPS = 1e-6
RW_GN_EPS = 64e-5
ADAM_LR, ADAM_B1, ADAM_B2, ADAM_EPS, ADAM_WD, ADAM_STEP = 0.001, 0.9, 0.999, 1e-08, 0.01, 10
WB = 256
V7X_VMEM_LIMIT = 56 * 1024 * 1024

F32 = jnp.float32
BF16 = jnp.bfloat16
AXES = ("x", "y", "c")


def _dims():
    tt = T + TC
    sh = 3 * RWW + 4 * LORA
    hgc = 5 * HGW
    return dict(TT=tt, NCC=TC // C, NCH=tt // C, HG_H=HGW // 128, RW_P=RWW // 128, SH=sh, HGC=hgc,
                NCOLS=hgc + sh + RWW + 2 * D, NCT=TC // TM, NCTV=TC // TMV)


def _mm(a, b):
    return _dot3(a, b, "nn")


def _mm_nt(a, b):
    return _dot3(a, b, "nt")


def _mm_tn(a, b):
    return _dot3(a, b, "tn")


def _split(x):
    hi = x.astype(BF16)
    return hi, (x - hi.astype(F32)).astype(BF16)


_FORMS = {"nn": (((1,), (0,)), ((), ())), "nt": (((1,), (1,)), ((), ())), "tn": (((0,), (0,)), ((), ()))}


def _dot3_raw(a, b, form):
    ah, al = _split(a)
    bh, bl = _split(b)
    d = lambda x, y: lax.dot_general(x, y, _FORMS[form], preferred_element_type=F32)
    return d(ah, bh) + (d(ah, bl) + d(al, bh))


@functools.partial(jax.custom_vjp, nondiff_argnums=(2,))
def _dot3(a, b, form):
    return _dot3_raw(a, b, form)


def _dot3_fwd(a, b, form):
    return _dot3_raw(a, b, form), (a, b)


def _dot2(a, b, form, wide):
    d = lambda x, y: lax.dot_general(x, y, _FORMS[form], preferred_element_type=F32)
    if wide == 0:
        hi, lo = _split(a)
        bb = b.astype(BF16)
        return d(hi, bb) + d(lo, bb)
    hi, lo = _split(b)
    aa = a.astype(BF16)
    return d(aa, hi) + d(aa, lo)


def _dot3_bwd(form, res, g):
    a, b = res
    if form == "nn":
        return _dot3(g, b, "nt"), _dot3(a, g, "tn")
    if form == "nt":
        return _dot3(g, b, "nn"), _dot3(g, a, "tn")
    return _dot3(b, g, "nt"), _dot3(a, g, "nn")


@functools.partial(jax.custom_vjp, nondiff_argnums=(2,))
def _dot3_lean(a, b, form):
    return _dot3_raw(a, b, form)


def _dot3_lean_bwd(form, res, g):
    a, b = res
    if form == "nn":
        return _dot2(g, b, "nt", 0), _dot2(a, g, "tn", 1)
    if form == "nt":
        return _dot2(g, b, "nn", 0), _dot2(g, a, "tn", 0)
    return _dot2(b, g, "nt", 1), _dot2(a, g, "nn", 1)


_dot3_lean.defvjp(_dot3_fwd, _dot3_lean_bwd)


_dot3.defvjp(_dot3_fwd, _dot3_bwd)


def _scan_cumsum(inc, x):
    return _cumsum_vjp(inc.astype(BF16), x)


def _cumsum_raw(inc, x, form):
    h1 = x.astype(BF16)
    r1 = x - h1.astype(F32)
    h2 = r1.astype(BF16)
    h3 = (r1 - h2.astype(F32)).astype(BF16)
    d = lambda y: lax.dot_general(inc, y, _FORMS[form], preferred_element_type=F32)
    return d(h1) + (d(h2) + d(h3))


@jax.custom_vjp
def _cumsum_vjp(inc, x):
    return _cumsum_raw(inc, x, "nn")


_cumsum_vjp.defvjp(lambda inc, x: (_cumsum_raw(inc, x, "nn"), inc),
                   lambda inc, g: (jnp.zeros_like(inc), _cumsum_raw(inc, g, "tn")))


def _silu(x):
    return x * jax.nn.sigmoid(x)


def _softplus(x):
    return jnp.maximum(x, 0.0) + jnp.log(1.0 + jnp.exp(-jnp.abs(x)))


def _iota2(shape, dim):
    return lax.broadcasted_iota(jnp.int32, shape, dim)


def _pair_mask():
    return (_iota2((128, 128), 0) < 64) == (_iota2((128, 128), 1) < 64)


def _seg64_raw(x):
    e = _pair_mask().astype(BF16)
    parts = []
    for g in range(x.shape[1] // 128):
        hi, lo = _split(x[:, g * 128:(g + 1) * 128])
        parts.append(jnp.dot(hi, e, preferred_element_type=F32) + jnp.dot(lo, e, preferred_element_type=F32))
    return parts[0] if len(parts) == 1 else jnp.concatenate(parts, axis=1)


@jax.custom_vjp
def _seg64_sum(x):
    return _seg64_raw(x)


_seg64_sum.defvjp(lambda x: (_seg64_raw(x), None), lambda _, g: (_seg64_raw(g),))


def _mm1(a, b):
    return jnp.dot(a.astype(BF16), b.astype(BF16), preferred_element_type=F32)


def _mm1_nt(a, b):
    return lax.dot_general(a.astype(BF16), b.astype(BF16), (((1,), (1,)), ((), ())), preferred_element_type=F32)


def _seg128_mean(x):
    parts = [jnp.broadcast_to(jnp.mean(x[:, g * 128:(g + 1) * 128], axis=1, keepdims=True), (x.shape[0], 128))
             for g in range(x.shape[1] // 128)]
    return parts[0] if len(parts) == 1 else jnp.concatenate(parts, axis=1)


def _scan_masks(d):
    lag = (_iota2((C, C), 0) - _iota2((C, C), 1)) * (1 - 2 * d)
    return lag >= 0, lag > 0


def _tri_inverse(a):
    eye = (_iota2((C, C), 0) == _iota2((C, C), 1)).astype(F32)
    p = [-x for x in a]
    tm = [eye + x for x in p]
    n = 2
    while n < C:
        p = [_mm(x, x) for x in p]
        tm = [t + _mm(t, x) for t, x in zip(tm, p)]
        n *= 2
    return tm


@jax.custom_vjp
def _tri_solve(tm, a, rhs):
    return [_mm(t, r) for t, r in zip(tm, rhs)]


def _tri_solve_fwd(tm, a, rhs):
    u = [_mm(t, r) for t, r in zip(tm, rhs)]
    return u, (tm, u)


def _tri_solve_bwd(res, g):
    tm, u = res
    d_rhs = [_dot2(t, x, "tn", 1) for t, x in zip(tm, g)]
    return [jnp.zeros_like(t) for t in tm], [-_dot2(d, x, "nt", 0) for d, x in zip(d_rhs, u)], d_rhs


_tri_solve.defvjp(_tri_solve_fwd, _tri_solve_bwd)


def _rw_chunks(chains, tms=None):
    s0, r, lw, kap, b, v, kd, ds = zip(*chains)
    ids = range(len(chains))
    _mm = lambda p, q: _dot3_lean(p, q, "nn")
    _mm_nt = lambda p, q: _dot3_lean(p, q, "nt")
    _mm_tn = lambda p, q: _dot3_lean(p, q, "tn")
    inc, strict = zip(*[_scan_masks(d) for d in ds])
    lane = _iota2((1, 128), 1)
    hm = ((lane < 64).astype(F32), (lane >= 64).astype(F32))
    lc = [_scan_cumsum(inc[i], lw[i]) for i in ids]
    ltot = [jnp.sum(lw[i], axis=0, keepdims=True) for i in ids]
    rt = [r[i] * jnp.exp(lc[i]) for i in ids]
    kt = [kap[i] * jnp.exp(lc[i] - lw[i]) for i in ids]
    einv = [jnp.exp(-lc[i]) for i in ids]
    kh = [kd[i] * einv[i] for i in ids]
    bh = [b[i] * einv[i] for i in ids]
    eend = [jnp.exp(ltot[i] - lc[i]) for i in ids]
    kbar = [kd[i] * eend[i] for i in ids]
    bbar = [b[i] * eend[i] for i in ids]
    lhs = [jnp.concatenate([kt[i] * hm[0], kt[i] * hm[1], rt[i] * hm[0], rt[i] * hm[1]], axis=0) for i in ids]
    gk = [_mm_nt(lhs[i], kh[i]) for i in ids]
    gb = [_mm_nt(lhs[i], bh[i]) for i in ids]
    ks = [_mm_nt(kt[i], s0[i]) for i in ids]
    ys = [_mm_nt(rt[i], s0[i]) for i in ids]
    heads = [(i, h) for i in ids for h in range(2)]
    akk = {c: jnp.where(strict[c[0]], gk[c[0]][c[1] * C:(c[1] + 1) * C], 0.0) for c in heads}
    akb = {c: jnp.where(strict[c[0]], gb[c[0]][c[1] * C:(c[1] + 1) * C], 0.0) for c in heads}
    ark = {c: jnp.where(inc[c[0]], gk[c[0]][(2 + c[1]) * C:(3 + c[1]) * C], 0.0) for c in heads}
    arb = {c: jnp.where(inc[c[0]], gb[c[0]][(2 + c[1]) * C:(3 + c[1]) * C], 0.0) for c in heads}
    vh = {c: v[c[0]] * hm[c[1]] for c in heads}
    av = {c: _mm(akk[c], vh[c]) for c in heads}
    rhs = {c: ks[c[0]] * hm[c[1]] + av[c] for c in heads}
    akb_list = [akb[c] for c in heads]
    tm = _tri_inverse(akb_list) if tms is None else [tms[c] for c in heads]
    uh = dict(zip(heads, _tri_solve(tm, akb_list, [rhs[c] for c in heads])))
    yv = {c: _mm(ark[c], vh[c]) for c in heads}
    yu = {c: _mm(arb[c], uh[c]) for c in heads}
    u = [uh[i, 0] + uh[i, 1] for i in ids]
    y = [ys[i] + (yv[i, 0] - yu[i, 0]) + (yv[i, 1] - yu[i, 1]) for i in ids]
    upd = [_mm_tn(jnp.concatenate([v[i], -u[i]], axis=0), jnp.concatenate([kbar[i], bbar[i]], axis=0)) for i in ids]
    s1 = [s0[i] * jnp.exp(ltot[i]) + jnp.where(_pair_mask(), upd[i], 0.0) for i in ids]
    if tms is None:
        return [(y[i], s1[i], tm[2 * i], tm[2 * i + 1]) for i in ids]
    return [(y[i], s1[i]) for i in ids]


HG_SUB = 32


def _hg_chunks(chains):
    s0, qr, iv, f, lb, ds = zip(*chains)
    ids = range(len(chains))
    inc = [_scan_masks(d)[0] for d in ds]
    q = [_silu(qr[i]) for i in ids]
    fg = [lb[i] + (1.0 - lb[i]) * jax.nn.sigmoid(f[i]) for i in ids]
    k = [1.0 - fg[i] for i in ids]
    g = [jnp.log(fg[i]) for i in ids]
    bc = [_scan_cumsum(inc[i], g[i]) for i in ids]
    btot = [jnp.sum(g[i], axis=0, keepdims=True) for i in ids]
    o_inter = [_mm_nt(q[i] * jnp.exp(bc[i]), s0[i]) for i in ids]
    rowi = _iota2((C, 1), 0)
    outs = [[] for _ in ids]
    for blk in range(C // HG_SUB):
        lo, hi = blk * HG_SUB, (blk + 1) * HG_SUB
        first = [lo if ds[i] == 0 else hi - 1 for i in ids]
        ref = [jnp.sum(jnp.where(rowi == first[i], bc[i] - g[i], 0.0), axis=0, keepdims=True) for i in ids]
        qi = [q[i][lo:hi] * jnp.exp(bc[i][lo:hi] - ref[i]) for i in ids]
        src = [(rowi < hi) if ds[i] == 0 else (rowi >= lo) for i in ids]
        ke = [jnp.where(src[i], k[i] * jnp.exp(jnp.where(src[i], ref[i] - bc[i], 0.0)), 0.0) for i in ids]
        a = [jnp.where(inc[i][lo:hi], _mm_nt(qi[i], ke[i]), 0.0) for i in ids]
        part = [_mm(a[i], iv[i]) for i in ids]
        for i in ids:
            outs[i].append(part[i])
    o = [o_inter[i] + jnp.concatenate(outs[i], axis=0) for i in ids]
    upd = [_mm_tn(iv[i], k[i] * jnp.exp(btot[i] - bc[i])) for i in ids]
    s1 = [s0[i] * jnp.exp(btot[i]) + upd[i] for i in ids]
    return [(o[i], s1[i]) for i in ids]


def _lockstep(fn, ds, nargs):
    def flat_fn(*flat):
        return tuple(fn([tuple(flat[i * nargs:(i + 1) * nargs]) + (d,) for i, d in enumerate(ds)]))

    return flat_fn


def _h_math(x, ng, scale, shift):
    return x * lax.rsqrt(jnp.mean(x * x, axis=-1, keepdims=True) + NORM_EPS) * ng * (1.0 + scale) + shift


def _hg_post_math(o0, o1, z, g):
    o = o0 + o1
    on = o * lax.rsqrt(_seg128_mean(o * o) + NORM_EPS)
    return on * g * _silu(z)


def _rw_prep_math(k, pw0, pw1, pa0, pa1, w0, a0, kkp, kap_p):
    kk = k * kkp
    kap = kk * lax.rsqrt(_seg64_sum(kk * kk) + 1e-12)
    outs = []
    for d, (pw, pa) in enumerate(((pw0, pa0), (pw1, pa1))):
        w_log = -_softplus(-(w0[d:d + 1] + pw)) - 0.5
        lw = -jnp.exp(w_log)
        a = jax.nn.sigmoid(a0[d:d + 1] + pa)
        kd = k * (1.0 + (a - 1.0) * kap_p)
        outs += [lw, kap * a, kd]
    return (kap, *outs)


def _rw_post_math(y0, y1, kd0, kd1, r, v, z, rk, gng, gnb):
    ys, ksum = y0 + y1, kd0 + kd1
    mean = _seg64_sum(ys) * (1.0 / 64.0)
    cen = ys - mean
    var = _seg64_sum(cen * cen) * (1.0 / 64.0)
    yn = cen * lax.rsqrt(var + RW_GN_EPS) * gng + gnb
    bonus = _seg64_sum(r * ksum * rk) * v
    return (yn + bonus) * _silu(z)


def _head_math(x, mo, gate, fg, tgt):
    xo = x + gate * mo
    y = xo * lax.rsqrt(jnp.mean(xo * xo, axis=-1, keepdims=True) + NORM_EPS) * fg
    err = y - tgt
    return 0.5 * jnp.sum(jnp.mean(err * err, axis=-1, keepdims=True), axis=0, keepdims=True)


def _gate_math(ghg, grw, a, b):
    return jax.nn.sigmoid(ghg) * a + jax.nn.sigmoid(grw) * b


def _call(body, name, grid, in_specs, out_specs, out_shape, scratch=(), sem=None, aliases=None):
    return pl.pallas_call(
        body, name=name, grid=grid, in_specs=in_specs, out_specs=out_specs, out_shape=out_shape,
        scratch_shapes=list(scratch), input_output_aliases=aliases or {},
        compiler_params=pltpu.CompilerParams(dimension_semantics=sem, vmem_limit_bytes=V7X_VMEM_LIMIT))


def _full(shape):
    nd = len(shape)
    return pl.BlockSpec(shape, lambda *_: (0,) * nd)


def _sds(shape, dtype=F32):
    return jax.ShapeDtypeStruct(shape, dtype)


def _proj_nt(h, wt, wt_blk0, nblk, row0, nrows, name):
    def body(h_ref, w_ref, o_ref):
        o_ref[...] = lax.dot_general(h_ref[row0:row0 + nrows, :], w_ref[...], (((1,), (1,)), ((), ())), preferred_element_type=F32)

    return _call(body, name, (nblk,), [_full(h.shape), pl.BlockSpec((WB, wt.shape[1]), lambda j: (wt_blk0 + j, 0))],
                 pl.BlockSpec((nrows, WB), lambda j: (0, j)), _sds((nrows, nblk * WB)), sem=("parallel",))(h, wt)


def _dw_tn(a, b, row0, out_rows, out_blk0, name, prev=None):
    m, n = a.shape
    k2 = b.shape[1]
    nblk = n // WB

    def body(*refs):
        a_ref, b_ref, o_ref = refs[0], refs[1], refs[-1]
        o_ref[...] = lax.dot_general(a_ref[...], b_ref[row0:row0 + m, :], (((0,), (0,)), ((), ())),
                                     preferred_element_type=F32).astype(BF16)

    in_specs = [pl.BlockSpec((m, WB), lambda j: (0, j)), _full(b.shape)]
    args = [a, b]
    aliases = None
    if prev is not None:
        in_specs.append(pl.BlockSpec(memory_space=pl.ANY))
        args.append(prev)
        aliases = {2: 0}
    return _call(body, name, (nblk,), in_specs, pl.BlockSpec((WB, k2), lambda j: (out_blk0 + j, 0)), _sds((out_rows, k2), BF16),
                 sem=("arbitrary",), aliases=aliases)(*args)


def _mm_nn(a, w, w_blk0, name, after=None):
    m, kc = a.shape
    n = w.shape[1]
    kb = max(k for k in (WB, 2 * WB, 4 * WB) if kc % k == 0 and (w_blk0 * WB) % k == 0)
    w_blk0 = w_blk0 * WB // kb
    tmb = 3 * TM if m % (3 * TM) == 0 else (2 * TM if m % (2 * TM) == 0 else TM)

    def body(a_ref, w_ref, *rest):
        o_ref = rest[-1]

        @pl.when(pl.program_id(1) == 0)
        def _():
            o_ref[...] = jnp.zeros_like(o_ref)

        o_ref[...] += jnp.dot(a_ref[...], w_ref[...], preferred_element_type=F32)

    extra = [] if after is None else [pl.BlockSpec(memory_space=pl.ANY)]
    return _call(body, name, (m // tmb, kc // kb),
                 [pl.BlockSpec((tmb, kb), lambda i, k: (i, k)), pl.BlockSpec((kb, n), lambda i, k: (w_blk0 + k, 0))] + extra,
                 pl.BlockSpec((tmb, n), lambda i, k: (i, 0)), _sds((m, n)), sem=("parallel", "arbitrary"))(
        a, w, *([] if after is None else [after]))


def _ada_fwd(cc16, ada_shard):
    ncol = ada_shard.shape[1]

    def body(c_ref, w_ref, o_ref):
        o_ref[...] = _mm(_silu(c_ref[...]), w_ref[...])

    return _call(body, "ada_fwd", (1,), [_full((16, D)), _full((D, ncol))], _full((16, ncol)), _sds((16, ncol)))(cc16, ada_shard)


def _ada_bwd(cc16, ada_shard, dm_x, dm_c, dmods):
    ncol = ada_shard.shape[1]

    def body(c_ref, w_ref, dx_ref, dc_ref, dm_ref, dw_ref, dcc_ref, db_ref):
        s, vjp = jax.vjp(_silu, c_ref[...])
        dc_tot = dc_ref[0:1, :]
        for j in range(1, NDEV):
            dc_tot = dc_tot + dc_ref[j:j + 1, :]
        row = _iota2((8, 1), 0)
        dm16 = jnp.concatenate([dx_ref[...], jnp.where(row == 0, dc_tot, 0.0)], axis=0)
        dw_ref[...] = _mm_tn(s, dm16)
        dcc_ref[...] = vjp(_mm_nt(dm16, w_ref[...]))[0]
        db_ref[...] = dm_ref[0:1, :] + dm_ref[1:2, :]

    return _call(body, "ada_bwd", (1,), [_full((16, D)), _full((D, ncol)), _full((8, ncol)), _full((8, ncol)), _full((8, 3 * D))],
                 [_full((D, ncol)), _full((16, D)), _full((1, 3 * D))], [_sds((D, ncol)), _sds((16, D)), _sds((1, 3 * D))])(
        cc16, ada_shard, dm_x, dm_c, dmods)


def _sel_mod(mods_ref, is_ctx, lo):
    return jnp.where(is_ctx, mods_ref[1:2, lo:lo + D], mods_ref[0:1, lo:lo + D])


def _token_specs():
    nct = _dims()["NCT"]
    return (pl.BlockSpec((TM, D), lambda i: (jnp.minimum(i, nct - 1), 0)),
            pl.BlockSpec((TM, D), lambda i: (jnp.maximum(i - nct, 0), 0)))


def _h_fwd(ctx, x, mods, norm_g):
    dm = _dims()

    def body(c_ref, x_ref, m_ref, g_ref, o_ref):
        is_ctx = pl.program_id(0) < dm["NCT"]
        tok = jnp.where(is_ctx, c_ref[...], x_ref[...])
        o_ref[...] = _h_math(tok, g_ref[...], _sel_mod(m_ref, is_ctx, D), _sel_mod(m_ref, is_ctx, 0)).astype(BF16)

    return _call(body, "h_fwd", (dm["TT"] // TM,), [*_token_specs(), _full((8, 3 * D)), _full((1, D))],
                 pl.BlockSpec((TM, D), lambda i: (i, 0)), _sds((dm["TT"], D), BF16), sem=("parallel",))(ctx, x, mods, norm_g)


def _chunk_of(d, s):
    dm = _dims()
    ncc, nch = dm["NCC"], dm["NCH"]
    return s if d == 0 else jnp.where(s < ncc, ncc - 1 - s, nch - 1 - (s - ncc))


HG_GROUP = 8
RW_GROUP = 8


def _dir_specs(make):
    return [make(d) for d in range(2)]


def _lanes(j):
    return slice(j * 128, (j + 1) * 128)


def _hg_scan_fwd(p_hg, lb3):
    dm = _dims()
    nh, nch, tt = dm["HG_H"], dm["NCH"], dm["TT"]
    pp = min(HG_GROUP, nh)
    lw_ = 128 * pp

    def body(q0, q1, i0, i1, f0, f1, lb0, lb1, o0, o1, ck_ref, s_ref):
        @pl.when(pl.program_id(1) == 0)
        def _():
            s_ref[...] = jnp.zeros_like(s_ref)

        keys, chains = [], []
        for d, refs in enumerate(((q0, i0, f0, lb0), (q1, i1, f1, lb1))):
            vals = [r[...] for r in refs]
            for j in range(pp):
                keys.append((d, j))
                chains.append((s_ref[d, j], *[a[:, _lanes(j)] for a in vals], d))
        for (d, j), chain, (out, s1) in zip(keys, chains, _hg_chunks(chains)):
            ck_ref[d, j] = chain[0]
            (o0, o1)[d][:, _lanes(j)] = out
            s_ref[d, j] = s1

    blk = lambda off: _dir_specs(lambda d: pl.BlockSpec((C, lw_), lambda h, s: (_chunk_of(d, s), off * nh // pp + h)))
    return _call(body, "hg_scan_fwd", (nh // pp, nch),
                 blk(0) + blk(1) + _dir_specs(lambda d: pl.BlockSpec((C, lw_), lambda h, s: (_chunk_of(d, s), (2 + d) * nh // pp + h)))
                 + _dir_specs(lambda d: pl.BlockSpec((None, 1, lw_), lambda h, s: (d, 0, h))),
                 _dir_specs(lambda d: pl.BlockSpec((C, lw_), lambda h, s: (_chunk_of(d, s), h)))
                 + [pl.BlockSpec((2, pp, None, 128, 128), lambda h, s: (0, h, s, 0, 0))],
                 [_sds((tt, HGW)), _sds((tt, HGW)), _sds((2, nh, nch, 128, 128))],
                 scratch=[pltpu.VMEM((2, pp, 128, 128), F32)], sem=("parallel", "arbitrary"))(*([p_hg] * 6), lb3, lb3)


def _hg_scan_bwd(p_hg, lb3, ck, do):
    dm = _dims()
    nh, nch, tt = dm["HG_H"], dm["NCH"], dm["TT"]
    pp = min(HG_GROUP, nh)
    lw_ = 128 * pp

    def body(q0, q1, i0, i1, f0, f1, lb0, lb1, ck_ref, do0, do1, dq0, dq1, di0, di1, df0, df1, dlb_ref, ds_ref):
        @pl.when(pl.program_id(1) == 0)
        def _():
            ds_ref[...] = jnp.zeros_like(ds_ref)
            dlb_ref[...] = jnp.zeros_like(dlb_ref)

        keys, flat, cts = [], [], []
        for d, refs in enumerate(((q0, i0, f0, lb0, do0), (q1, i1, f1, lb1, do1))):
            *vals, do_ = [r[...] for r in refs]
            for j in range(pp):
                keys.append((d, j))
                flat += [ck_ref[d, j], *[a[:, _lanes(j)] for a in vals]]
                cts.append((do_[:, _lanes(j)], ds_ref[d, j]))
        _, vjp = jax.vjp(_lockstep(_hg_chunks, [d for d, _ in keys], 5), *flat)
        grads = vjp(tuple(cts))
        for n, (d, j) in enumerate(keys):
            ds0, dq_, di_, df_, dlb = grads[5 * n:5 * n + 5]
            (dq0, dq1)[d][:, _lanes(j)] = dq_
            (di0, di1)[d][:, _lanes(j)] = di_
            (df0, df1)[d][:, _lanes(j)] = df_
            dlb_ref[d, :, _lanes(j)] += dlb
            ds_ref[d, j] = ds0

    ch = lambda d, s: _chunk_of(d, nch - 1 - s)
    blk = lambda off: _dir_specs(lambda d: pl.BlockSpec((C, lw_), lambda h, s: (ch(d, s), off * nh // pp + h)))
    tok = _dir_specs(lambda d: pl.BlockSpec((C, lw_), lambda h, s: (ch(d, s), h)))
    return _call(body, "hg_scan_bwd", (nh // pp, nch),
                 blk(0) + blk(1) + _dir_specs(lambda d: pl.BlockSpec((C, lw_), lambda h, s: (ch(d, s), (2 + d) * nh // pp + h)))
                 + _dir_specs(lambda d: pl.BlockSpec((None, 1, lw_), lambda h, s: (d, 0, h)))
                 + [pl.BlockSpec((2, pp, None, 128, 128), lambda h, s: (0, h, nch - 1 - s, 0, 0))] + tok,
                 tok * 3 + [pl.BlockSpec((2, 1, lw_), lambda h, s: (0, 0, h))],
                 [_sds((tt, HGW))] * 6 + [_sds((2, 1, HGW))],
                 scratch=[pltpu.VMEM((2, pp, 128, 128), F32)], sem=("parallel", "arbitrary"))(*([p_hg] * 6), lb3, lb3, ck, do, do)


def _rw_scan_fwd(sh, kap, lw, b, kd):
    dm = _dims()
    npair, nch, tt = dm["RW_P"], dm["NCH"], dm["TT"]
    pp = min(RW_GROUP, npair)
    lw_ = 128 * pp

    def body(r0, r1, v0, v1, ka0, ka1, lw0, lw1, b0, b1, kd0, kd1, y0, y1, ck_ref, tm_ref, s_ref):
        @pl.when(pl.program_id(1) == 0)
        def _():
            s_ref[...] = jnp.zeros_like(s_ref)

        keys, chains = [], []
        for d, refs in enumerate(((r0, lw0, ka0, b0, v0, kd0), (r1, lw1, ka1, b1, v1, kd1))):
            vals = [r[...] for r in refs]
            for j in range(pp):
                keys.append((d, j))
                chains.append((s_ref[d, j], *[a[:, _lanes(j)] for a in vals], d))
        for (d, j), chain, (out, s1, tm_a, tm_b) in zip(keys, chains, _rw_chunks(chains)):
            ck_ref[d, j] = chain[0]
            tm_ref[d, j, 0] = tm_a
            tm_ref[d, j, 1] = tm_b
            (y0, y1)[d][:, _lanes(j)] = out
            s_ref[d, j] = s1

    two = lambda off: _dir_specs(lambda d: pl.BlockSpec((C, lw_), lambda p, s: (_chunk_of(d, s), off * npair // pp + p)))
    three = _dir_specs(lambda d: pl.BlockSpec((None, C, lw_), lambda p, s: (d, _chunk_of(d, s), p)))
    return _call(body, "rw_scan_fwd", (npair // pp, nch), two(0) + two(2) + two(0) + three * 3,
                 two(0) + [pl.BlockSpec((2, pp, None, 128, 128), lambda p, s: (0, p, s, 0, 0)),
                           pl.BlockSpec((2, pp, None, 2, C, C), lambda p, s: (0, p, s, 0, 0, 0))],
                 [_sds((tt, RWW)), _sds((tt, RWW)), _sds((2, npair, nch, 128, 128)), _sds((2, npair, nch, 2, C, C))],
                 scratch=[pltpu.VMEM((2, pp, 128, 128), F32)], sem=("parallel", "arbitrary"))(
        sh, sh, sh, sh, kap, kap, lw, lw, b, b, kd, kd)


def _rw_scan_bwd(sh, kap, lw, b, kd, ck, tm_ck, dy):
    dm = _dims()
    npair, nch, tt = dm["RW_P"], dm["NCH"], dm["TT"]
    pp = min(RW_GROUP, npair)
    lw_ = 128 * pp

    def body(r0, r1, v0, v1, ka0, ka1, lw0, lw1, b0, b1, kd0, kd1, ck_ref, tm_ref, dy0, dy1,
             dr0, dr1, dv0, dv1, dka0, dka1, dlw0, dlw1, db0, db1, dkd0, dkd1, ds_ref):
        @pl.when(pl.program_id(1) == 0)
        def _():
            ds_ref[...] = jnp.zeros_like(ds_ref)

        keys, flat, cts, tms = [], [], [], {}
        for d, refs in enumerate(((r0, lw0, ka0, b0, v0, kd0, dy0), (r1, lw1, ka1, b1, v1, kd1, dy1))):
            *vals, dy_ = [r[...] for r in refs]
            for j in range(pp):
                tms[len(keys), 0], tms[len(keys), 1] = tm_ref[d, j, 0], tm_ref[d, j, 1]
                keys.append((d, j))
                flat += [ck_ref[d, j], *[a[:, _lanes(j)] for a in vals]]
                cts.append((dy_[:, _lanes(j)], ds_ref[d, j]))
        _, vjp = jax.vjp(_lockstep(functools.partial(_rw_chunks, tms=tms), [d for d, _ in keys], 7), *flat)
        grads = vjp(tuple(cts))
        outs = ((dr0, dlw0, dka0, db0, dv0, dkd0), (dr1, dlw1, dka1, db1, dv1, dkd1))
        for n, (d, j) in enumerate(keys):
            ds0, *g_in = grads[7 * n:7 * n + 7]
            for o_ref, g in zip(outs[d], g_in):
                o_ref[:, _lanes(j)] = g
            ds_ref[d, j] = ds0

    ch = lambda d, s: _chunk_of(d, nch - 1 - s)
    two = lambda off: _dir_specs(lambda d: pl.BlockSpec((C, lw_), lambda p, s: (ch(d, s), off * npair // pp + p)))
    three = _dir_specs(lambda d: pl.BlockSpec((None, C, lw_), lambda p, s: (d, ch(d, s), p)))
    return _call(body, "rw_scan_bwd", (npair // pp, nch),
                 two(0) + two(2) + two(0) + three * 3
                 + [pl.BlockSpec((2, pp, None, 128, 128), lambda p, s: (0, p, nch - 1 - s, 0, 0)),
                    pl.BlockSpec((2, pp, None, 2, C, C), lambda p, s: (0, p, nch - 1 - s, 0, 0, 0))] + two(0),
                 two(0) * 6, [_sds((tt, RWW))] * 12,
                 scratch=[pltpu.VMEM((2, pp, 128, 128), F32)], sem=("parallel", "arbitrary"))(
        sh, sh, sh, sh, kap, kap, lw, lw, b, b, kd, kd, ck, tm_ck, dy, dy)


def _tile_pos(i):
    dm = _dims()
    is_ctx = i < dm["NCT"]
    rows = lax.broadcasted_iota(jnp.int32, (TM, 1), 0)
    tl = rows + (i - dm["NCT"]) * TM
    width = jnp.where(is_ctx, TC, GW)
    assert TC & (TC - 1) == 0 and GW & (GW - 1) == 0 and TM % GW == 0
    colp = rows & (width - 1)
    return is_ctx, rows, tl, colp, width


def _shift_lr(cur, i):
    _, _, _, colp, width = _tile_pos(i)
    left = jnp.where(colp == 0, 0.0, pltpu.roll(cur, 1, 0))
    right = jnp.where(colp == width - 1, 0.0, pltpu.roll(cur, TM - 1, 0))
    return left, right


def _shift_ud(cur, prv, nxt, i):
    is_ctx, rows, tl, _, _ = _tile_pos(i)
    if TM > GW:
        up = jnp.where(rows >= GW, pltpu.roll(cur, GW, 0), pltpu.roll(prv, GW, 0))
        down = jnp.where(rows < TM - GW, pltpu.roll(cur, TM - GW, 0), pltpu.roll(nxt, TM - GW, 0))
    else:
        up, down = prv, nxt
    up = jnp.where(jnp.logical_or(is_ctx, tl < GW), 0.0, up)
    down = jnp.where(jnp.logical_or(is_ctx, tl >= T - GW), 0.0, down)
    return up, down


def _shift_specs():
    dm = _dims()
    nt = dm["TT"] // TM
    cw = max(w for w in range(128, 1664 + 1, 128) if dm["SH"] % w == 0)
    cur = pl.BlockSpec((TM, cw), lambda j, i: (i, j))
    prv = pl.BlockSpec((TM, cw), lambda j, i: (jnp.maximum(i - 1, 0), j))
    nxt = pl.BlockSpec((TM, cw), lambda j, i: (jnp.minimum(i + 1, nt - 1), j))
    mu = pl.BlockSpec((4, cw), lambda j, i: (0, j))
    return nt, cw, cur, prv, nxt, mu


def _shift_fwd(p_rw, mu):
    dm = _dims()
    nt, cw, cur, prv, nxt, mus = _shift_specs()

    def body(c_ref, p_ref, n_ref, mu_ref, o_ref):
        i = pl.program_id(1)
        p, m = c_ref[...], mu_ref[...]
        left, right = _shift_lr(p, i)
        up, down = _shift_ud(p, p_ref[...], n_ref[...], i)
        vert = jnp.where(i < dm["NCT"], 0.0, 1.0)
        o_ref[...] = (p * (1.0 - m[0:1] - m[1:2] - vert * (m[2:3] + m[3:4]))
                      + m[0:1] * left + m[1:2] * right + m[2:3] * up + m[3:4] * down)

    return _call(body, "shift_fwd", (dm["SH"] // cw, nt), [cur, prv, nxt, mus], cur, _sds((dm["TT"], dm["SH"])),
                 sem=("parallel", "parallel"))(p_rw, p_rw, p_rw, mu)


def _shift_bwd(p_rw, dsh, mu):
    dm = _dims()
    nt, cw, cur, prv, nxt, mus = _shift_specs()

    def body(c_ref, p_ref, n_ref, gc_ref, gp_ref, gn_ref, mu_ref, dp_ref, dmu_ref):
        i = pl.program_id(1)
        p, g, m = c_ref[...], gc_ref[...], mu_ref[...]
        vert = jnp.where(i < dm["NCT"], 0.0, 1.0)
        _, from_right = _shift_lr(m[0:1] * g, i)
        from_left, _ = _shift_lr(m[1:2] * g, i)
        _, from_down = _shift_ud(m[2:3] * g, g, m[2:3] * gn_ref[...], i)
        from_up, _ = _shift_ud(m[3:4] * g, m[3:4] * gp_ref[...], g, i)
        dp = g * (1.0 - m[0:1] - m[1:2] - vert * (m[2:3] + m[3:4])) + from_right + from_left + from_down + from_up
        dp_ref[...] = dp.astype(BF16)

        @pl.when(i == 0)
        def _():
            dmu_ref[...] = jnp.zeros_like(dmu_ref)

        left, right = _shift_lr(p, i)
        up, down = _shift_ud(p, p_ref[...], n_ref[...], i)
        s = lambda a: jnp.sum(a, axis=0, keepdims=True)
        dmu_ref[...] += jnp.concatenate([s(g * (left - p)), s(g * (right - p)), vert * s(g * (up - p)), vert * s(g * (down - p))], axis=0)

    return _call(body, "shift_bwd", (dm["SH"] // cw, nt), [cur, prv, nxt, cur, prv, nxt, mus],
                 [cur, pl.BlockSpec((4, cw), lambda j, i: (0, j))], [_sds((dm["TT"], dm["SH"]), BF16), _sds((4, dm["SH"]))],
                 sem=("parallel", "arbitrary"))(p_rw, p_rw, p_rw, dsh, dsh, dsh, mu)


def _acc_out(ref, first, vals):
    @pl.when(first)
    def _():
        for r in ref:
            r[...] = jnp.zeros_like(r)

    for r, v in zip(ref, vals):
        r[...] += v


def _rowsum(a):
    return jnp.sum(a, axis=0, keepdims=True)


def _rw_prep_specs():
    dm = _dims()
    rw = RWW // 128
    tok = lambda width, cb: pl.BlockSpec((TMV, width), lambda i: (i, cb))
    ins = [pl.BlockSpec((TMV, RWW), lambda i: (i, 1)), tok(128, 3 * rw), tok(128, 3 * rw + 1),
           _full((2, RWW)), _full((128, RWW)), _full((128, RWW)), _full((2, RWW)), _full((128, RWW)), _full((128, RWW)),
           _full((1, RWW)), _full((1, RWW))]
    return dm, ins


def _rw_prep_fwd(sh, w0, w2p0, w2p1, a0, a2p0, a2p1, kkp, kap_p):
    dm, ins = _rw_prep_specs()
    tt = dm["TT"]

    def body(k_ref, low_ref, loa_ref, w0_ref, w20_ref, w21_ref, a0_ref, a20_ref, a21_ref, kk_ref, ka_ref, kap_ref, lw_ref, b_ref, kd_ref):
        tw, la = jnp.tanh(low_ref[...]), loa_ref[...]
        kap, lw0, b0, kd0, lw1, b1, kd1 = _rw_prep_math(
            k_ref[...], _mm1(tw, w20_ref[...]), _mm1(tw, w21_ref[...]), _mm1(la, a20_ref[...]), _mm1(la, a21_ref[...]),
            w0_ref[...], a0_ref[...], kk_ref[...], ka_ref[...])
        kap_ref[...] = kap
        lw_ref[0], lw_ref[1] = lw0, lw1
        b_ref[0], b_ref[1] = b0, b1
        kd_ref[0], kd_ref[1] = kd0, kd1

    two = pl.BlockSpec((2, TMV, RWW), lambda i: (0, i, 0))
    return _call(body, "rw_prep_fwd", (tt // TMV,), ins, [pl.BlockSpec((TMV, RWW), lambda i: (i, 0)), two, two, two],
                 [_sds((tt, RWW)), _sds((2, tt, RWW)), _sds((2, tt, RWW)), _sds((2, tt, RWW))], sem=("parallel",))(
        sh, sh, sh, w0, w2p0, w2p1, a0, a2p0, a2p1, kkp, kap_p)


def _rw_prep_bwd(sh, w0, w2p0, w2p1, a0, a2p0, a2p1, kkp, kap_p, scan_grads, dkd_p, dr_p, dv_p):
    dm, ins = _rw_prep_specs()
    tt, sh_w = dm["TT"], dm["SH"]
    one = pl.BlockSpec((TMV, RWW), lambda i: (i, 0))

    def body(k_ref, low_ref, loa_ref, w0_ref, w20_ref, w21_ref, a0_ref, a20_ref, a21_ref, kk_ref, ka_ref,
             dr0, dr1, dv0, dv1, dka0, dka1, dlw0, dlw1, db0, db1, dkd0, dkd1, dkdp_ref, drp_ref, dvp_ref,
             dsh_ref, dw0_ref, dw20_ref, dw21_ref, da0_ref, da20_ref, da21_ref, dkk_ref, dka_ref):
        tw, la = jnp.tanh(low_ref[...]), loa_ref[...]
        w20, w21, a20, a21 = w20_ref[...], w21_ref[...], a20_ref[...], a21_ref[...]
        _, vjp = jax.vjp(_rw_prep_math, k_ref[...], _mm1(tw, w20), _mm1(tw, w21), _mm1(la, a20), _mm1(la, a21),
                         w0_ref[...], a0_ref[...], kk_ref[...], ka_ref[...])
        dkp = dkdp_ref[...]
        dk, dpw0, dpw1, dpa0, dpa1, dw0, da0, dkk, dka = vjp(
            (dka0[...] + dka1[...], dlw0[...], db0[...], dkd0[...] + dkp, dlw1[...], db1[...], dkd1[...] + dkp))
        twt, lat = tw.T, la.T
        dtw = _mm1_nt(dpw0, w20) + _mm1_nt(dpw1, w21)
        dsh_ref[:, 0:RWW] = dr0[...] + dr1[...] + drp_ref[...]
        dsh_ref[:, RWW:2 * RWW] = dk
        dsh_ref[:, 2 * RWW:3 * RWW] = dv0[...] + dv1[...] + dvp_ref[...]
        dsh_ref[:, 3 * RWW:3 * RWW + 128] = dtw * (1.0 - tw * tw)
        dsh_ref[:, 3 * RWW + 128:3 * RWW + 256] = _mm1_nt(dpa0, a20) + _mm1_nt(dpa1, a21)
        _acc_out((dw0_ref, dw20_ref, dw21_ref, da0_ref, da20_ref, da21_ref, dkk_ref, dka_ref), pl.program_id(0) == 0,
                 (dw0, _mm1(twt, dpw0), _mm1(twt, dpw1), da0, _mm1(lat, dpa0), _mm1(lat, dpa1), dkk, dka))

    par = [_full((2, RWW)), _full((128, RWW)), _full((128, RWW)), _full((2, RWW)), _full((128, RWW)), _full((128, RWW)),
           _full((1, RWW)), _full((1, RWW))]
    return _call(body, "rw_prep_bwd", (tt // TMV,), ins + [one] * 15,
                 [pl.BlockSpec((TMV, sh_w), lambda i: (i, 0))] + par,
                 [_sds((tt, sh_w)), _sds((2, RWW)), _sds((128, RWW)), _sds((128, RWW)), _sds((2, RWW)), _sds((128, RWW)), _sds((128, RWW)),
                  _sds((1, RWW)), _sds((1, RWW))],
                 sem=("arbitrary",))(sh, sh, sh, w0, w2p0, w2p1, a0, a2p0, a2p1, kkp, kap_p, *scan_grads, dkd_p, dr_p, dv_p)


def _rw_post_specs():
    dm = _dims()
    nv = dm["NCTV"]
    rw = RWW // 128
    lat3 = lambda d: pl.BlockSpec((None, TMV, RWW), lambda i: (d, nv + i, 0))
    lat2 = pl.BlockSpec((TMV, RWW), lambda i: (nv + i, 0))
    ins = [lat2, lat2, lat3(0), lat3(1),
           pl.BlockSpec((TMV, RWW), lambda i: (nv + i, 0)), pl.BlockSpec((TMV, RWW), lambda i: (nv + i, 2)),
           pl.BlockSpec((TMV, RWW), lambda i: (i, 0)), _full((1, RWW)), _full((1, RWW)), _full((1, RWW))]
    return dm, nv, ins


def _rw_post_fwd(y0_, y1_, kd, sh, p_z, rk, gng, gnb):
    dm, nv, ins = _rw_post_specs()

    def body(y0, y1, k0, k1, r, v, z, rk_ref, g_ref, b_ref, o_ref):
        o_ref[...] = _rw_post_math(y0[...], y1[...], k0[...], k1[...], r[...], v[...], z[...], rk_ref[...], g_ref[...], b_ref[...]).astype(BF16)

    return _call(body, "rw_post_fwd", (T // TMV,), ins, pl.BlockSpec((TMV, RWW), lambda i: (i, 0)), _sds((T, RWW), BF16),
                 sem=("parallel",))(y0_, y1_, kd, kd, sh, sh, p_z, rk, gng, gnb)


def _rw_post_bwd(y0_, y1_, kd, sh, p_z, rk, gng, gnb, dout):
    dm, nv, _ = _rw_post_specs()
    tt = dm["TT"]
    cl = lambda i: jnp.maximum(i - nv, 0)
    all3 = lambda d: pl.BlockSpec((None, TMV, RWW), lambda i: (d, i, 0))
    all2 = pl.BlockSpec((TMV, RWW), lambda i: (i, 0))
    ins = [all2, all2, all3(0), all3(1),
           pl.BlockSpec((TMV, RWW), lambda i: (i, 0)), pl.BlockSpec((TMV, RWW), lambda i: (i, 2)),
           pl.BlockSpec((TMV, RWW), lambda i: (cl(i), 0)), _full((1, RWW)), _full((1, RWW)), _full((1, RWW)),
           pl.BlockSpec((TMV, RWW), lambda i: (cl(i), 0))]
    tok = pl.BlockSpec((TMV, RWW), lambda i: (i, 0))

    def body(y0, y1, k0, k1, r, v, z, rk_ref, g_ref, b_ref, do_ref, dy_ref, dkd_ref, dr_ref, dv_ref, dz_ref, drk_ref, dg_ref, db_ref):
        i = pl.program_id(0)
        lat = jnp.where(i >= nv, 1.0, 0.0)
        _, vjp = jax.vjp(_rw_post_math, y0[...], y1[...], k0[...], k1[...], r[...], v[...], z[...], rk_ref[...], g_ref[...], b_ref[...])
        dy0, _, dk0, _, dr, dv, dz, drk, dg, db = vjp(do_ref[...] * lat)
        dy_ref[...] = dy0
        dkd_ref[...] = dk0
        dr_ref[...] = dr
        dv_ref[...] = dv
        dz_ref[...] = dz.astype(BF16)
        _acc_out((drk_ref, dg_ref, db_ref), i == 0, (drk, dg, db))

    return _call(body, "rw_post_bwd", (tt // TMV,), ins,
                 [tok, tok, tok, tok, pl.BlockSpec((TMV, RWW), lambda i: (cl(i), 0)), _full((1, RWW)), _full((1, RWW)), _full((1, RWW))],
                 [_sds((tt, RWW))] * 4 + [_sds((T, RWW), BF16), _sds((1, RWW)), _sds((1, RWW)), _sds((1, RWW))],
                 sem=("arbitrary",))(y0_, y1_, kd, kd, sh, sh, p_z, rk, gng, gnb, dout)


def _hg_post_fwd(o0_, o1_, p_hg, hg_g):
    dm = _dims()
    nv = dm["NCTV"]
    lat3 = lambda d: pl.BlockSpec((TMV, HGW), lambda i: (nv + i, 0))

    def body(o0, o1, z, g, out):
        out[...] = _hg_post_math(o0[...], o1[...], z[...], g[...]).astype(BF16)

    return _call(body, "hg_post_fwd", (T // TMV,), [lat3(0), lat3(1), pl.BlockSpec((TMV, HGW), lambda i: (nv + i, 4)), _full((1, HGW))],
                 pl.BlockSpec((TMV, HGW), lambda i: (i, 0)), _sds((T, HGW), BF16), sem=("parallel",))(o0_, o1_, p_hg, hg_g)


def _hg_post_bwd(o0_, o1_, p_hg, hg_g, dout):
    dm = _dims()
    nv, tt = dm["NCTV"], dm["TT"]
    all3 = lambda d: pl.BlockSpec((TMV, HGW), lambda i: (i, 0))
    tok = pl.BlockSpec((TMV, HGW), lambda i: (i, 0))

    def body(o0, o1, z, g, do_in, do_ref, dz_ref, dg_ref):
        i = pl.program_id(0)
        lat = jnp.where(i >= nv, 1.0, 0.0)
        _, vjp = jax.vjp(_hg_post_math, o0[...], o1[...], z[...], g[...])
        d0, _, dz, dg = vjp(do_in[...] * lat)
        do_ref[...] = d0
        dz_ref[...] = dz
        _acc_out((dg_ref,), i == 0, (dg,))

    return _call(body, "hg_post_bwd", (tt // TMV,),
                 [all3(0), all3(1), pl.BlockSpec((TMV, HGW), lambda i: (i, 4)), _full((1, HGW)),
                  pl.BlockSpec((TMV, HGW), lambda i: (jnp.maximum(i - nv, 0), 0))],
                 [tok, tok, _full((1, HGW))], [_sds((tt, HGW)), _sds((tt, HGW)), _sds((1, HGW))], sem=("arbitrary",))(o0_, o1_, p_hg, hg_g, dout)


def _hg_dproj(dq0, dq1, di0, di1, df0, df1, dz):
    dm = _dims()
    tt = dm["TT"]
    tok = pl.BlockSpec((TMV, HGW), lambda i: (i, 0))

    def body(q0, q1, i0, i1, f0, f1, z, o_ref):
        o_ref[:, 0:HGW] = (q0[...] + q1[...]).astype(BF16)
        o_ref[:, HGW:2 * HGW] = (i0[...] + i1[...]).astype(BF16)
        o_ref[:, 2 * HGW:3 * HGW] = f0[...].astype(BF16)
        o_ref[:, 3 * HGW:4 * HGW] = f1[...].astype(BF16)
        o_ref[:, 4 * HGW:5 * HGW] = z[...].astype(BF16)

    return _call(body, "hg_dproj", (tt // TMV,), [tok] * 7,
                 pl.BlockSpec((TMV, 5 * HGW), lambda i: (i, 0)), _sds((tt, 5 * HGW), BF16), sem=("parallel",))(
        dq0, dq1, di0, di1, df0, df1, dz)


def _merge1_fwd(out_hg, out_rw, whg_t, wrw_t, p_g):
    tok = pl.BlockSpec((TM, D), lambda i: (i, 0))

    def body(h_ref, r_ref, wh_ref, wr_ref, g1_ref, g2_ref, a_ref, b_ref, m_ref):
        nt = (((1,), (1,)), ((), ()))
        a = lax.dot_general(h_ref[...], wh_ref[...], nt, preferred_element_type=F32)
        b = lax.dot_general(r_ref[...], wr_ref[...], nt, preferred_element_type=F32)
        a_ref[...] = a
        b_ref[...] = b
        m_ref[...] = _gate_math(g1_ref[...], g2_ref[...], a, b).astype(BF16)

    return _call(body, "merge1_fwd", (T // TM,),
                 [pl.BlockSpec((TM, HGW), lambda i: (i, 0)), pl.BlockSpec((TM, RWW), lambda i: (i, 0)), _full((D, HGW)), _full((D, RWW)),
                  pl.BlockSpec((TM, D), lambda i: (i, 0)), pl.BlockSpec((TM, D), lambda i: (i, 1))],
                 [tok, tok, tok], [_sds((T, D)), _sds((T, D)), _sds((T, D), BF16)], sem=("parallel",))(out_hg, out_rw, whg_t, wrw_t, p_g, p_g)


def _head_fwd_bwd(x, merged, w_out, mods, final_g, tgt):
    tok = pl.BlockSpec((TM, D), lambda i: (i, 0))

    def body(x_ref, m_ref, w_ref, mods_ref, fg_ref, t_ref, dxo_ref, dmo_ref, loss_ref, dgate_ref, dfg_ref):
        mo = jnp.dot(m_ref[...], w_ref[...], preferred_element_type=F32)
        gate = mods_ref[0:1, 2 * D:3 * D]
        loss, vjp = jax.vjp(lambda x_, mo_, g_, fg_: _head_math(x_, mo_, g_, fg_, t_ref[...]), x_ref[...], mo, gate, fg_ref[...])
        dx, dmo, dgate, dfg = vjp(jnp.ones((1, 1), F32))
        dxo_ref[...] = dx
        dmo_ref[...] = dmo.astype(BF16)
        _acc_out((loss_ref, dgate_ref, dfg_ref), pl.program_id(0) == 0, (jnp.broadcast_to(loss, (1, 128)), dgate, dfg))

    return _call(body, "head_fwd_bwd", (T // TM,), [tok, tok, _full((D, D)), _full((8, 3 * D)), _full((1, D)), tok],
                 [tok, tok, _full((1, 128)), _full((1, D)), _full((1, D))],
                 [_sds((T, D)), _sds((T, D), BF16), _sds((1, 128)), _sds((1, D)), _sds((1, D))], sem=("arbitrary",))(
        x, merged, w_out, mods, final_g, tgt)


def _merge_bwd(dmo, w_out, p_g, a, b):
    tok = pl.BlockSpec((TM, D), lambda i: (i, 0))

    def body(dmo_ref, w_ref, g1_ref, g2_ref, a_ref, b_ref, da_ref, db_ref, dg_ref):
        dm_ = lax.dot_general(dmo_ref[...], w_ref[...], (((1,), (1,)), ((), ())), preferred_element_type=F32)
        _, vjp = jax.vjp(_gate_math, g1_ref[...], g2_ref[...], a_ref[...], b_ref[...])
        dg1, dg2, da, db = vjp(dm_)
        da_ref[...] = da.astype(BF16)
        db_ref[...] = db.astype(BF16)
        dg_ref[:, 0:D] = dg1.astype(BF16)
        dg_ref[:, D:2 * D] = dg2.astype(BF16)

    return _call(body, "merge_bwd", (T // TM,),
                 [tok, _full((D, D)), pl.BlockSpec((TM, D), lambda i: (i, 0)), pl.BlockSpec((TM, D), lambda i: (i, 1)), tok, tok],
                 [tok, tok, pl.BlockSpec((TM, 2 * D), lambda i: (i, 0))], [_sds((T, D), BF16), _sds((T, D), BF16), _sds((T, 2 * D), BF16)],
                 sem=("parallel",))(dmo, w_out, p_g, p_g, a, b)


def _h_bwd(ctx, x, mods, norm_g, dh_hg, dh_rw, dh_z, dh_g, dxo, dgate):
    dm = _dims()
    nct, tt = dm["NCT"], dm["TT"]
    tok = pl.BlockSpec((TM, D), lambda i: (i, 0))
    lat = pl.BlockSpec((TM, D), lambda i: (jnp.maximum(i - nct, 0), 0))

    def body(c_ref, x_ref, m_ref, g_ref, d1, d2, d3, d4, dxo_ref, dgate_ref, gx_ref, dng_ref, dmods_ref):
        i = pl.program_id(0)
        is_ctx = i < nct
        latf = jnp.where(is_ctx, 0.0, 1.0)
        dh = d1[...] + d2[...] + latf * (d3[...] + d4[...])
        _, vjp = jax.vjp(_h_math, jnp.where(is_ctx, c_ref[...], x_ref[...]), g_ref[...], _sel_mod(m_ref, is_ctx, D),
                         _sel_mod(m_ref, is_ctx, 0))
        dx, dng, dscale, dshift = vjp(dh)
        gx_ref[...] = dx + dxo_ref[...]

        @pl.when(i == 0)
        def _():
            dng_ref[...] = jnp.zeros_like(dng_ref)
            dmods_ref[...] = jnp.zeros_like(dmods_ref)
            dmods_ref[0:1, 2 * D:3 * D] = dgate_ref[...]

        dng_ref[...] += dng
        row = lax.broadcasted_iota(jnp.int32, (8, 1), 0)
        sel = jnp.where(row == jnp.where(is_ctx, 1, 0), 1.0, 0.0)
        dmods_ref[:, 0:D] += sel * dshift
        dmods_ref[:, D:2 * D] += sel * dscale

    return _call(body, "h_bwd", (tt // TM,),
                 [*_token_specs(), _full((8, 3 * D)), _full((1, D)), tok, tok, lat, lat, lat, _full((1, D))],
                 [lat, _full((1, D)), _full((8, 3 * D))], [_sds((T, D)), _sds((1, D)), _sds((8, 3 * D))], sem=("arbitrary",))(
        ctx, x, mods, norm_g, dh_hg, dh_rw, dh_z, dh_g, dxo, dgate)


def _lb_math(l0, l1):
    return jax.nn.sigmoid(l0 - l1)


def _lb_fwd(l0, l1):
    def body(a_ref, b_ref, o_ref):
        o_ref[...] = _lb_math(a_ref[...], b_ref[...])

    return _call(body, "lb_fwd", (1,), [_full((2, HGW))] * 2, _full((2, HGW)), _sds((2, HGW)))(l0, l1)


def _lb_bwd(l0, l1, dlb):
    def body(a_ref, b_ref, d_ref, da_ref, db_ref):
        _, vjp = jax.vjp(_lb_math, a_ref[...], b_ref[...])
        da_ref[...], db_ref[...] = vjp(d_ref[...])

    return _call(body, "lb_bwd", (1,), [_full((2, HGW))] * 3, [_full((2, HGW))] * 2, [_sds((2, HGW))] * 2)(l0, l1, dlb)


def _local_step(x, ctx, tgt, mods, w, start_exchange):
    dm = _dims()
    tt, sh_w, hgc = dm["TT"], dm["SH"], dm["HGC"]
    nb = lambda cols: cols // WB
    h = _h_fwd(ctx, x, mods, w["norm_g"])
    win_t = w["win_t"]
    p_hg = _proj_nt(h, win_t, 0, nb(hgc), 0, tt, "proj_hg")
    p_rw = _proj_nt(h, win_t, nb(hgc), nb(sh_w), 0, tt, "proj_rw")
    p_z = _proj_nt(h, win_t, nb(hgc + sh_w), nb(RWW), TC, T, "proj_z")
    p_g = _proj_nt(h, win_t, nb(hgc + sh_w + RWW), nb(2 * D), TC, T, "proj_g")

    lb3 = _lb_fwd(w["lb0"], w["lb1"]).reshape(2, 1, HGW)
    o0, o1, hg_ck = _hg_scan_fwd(p_hg, lb3)
    out_hg = _hg_post_fwd(o0, o1, p_hg, w["hg_g"])

    sh = _shift_fwd(p_rw, w["mu"])
    zpad = jnp.zeros((LORA, RWW), F32)
    prep_w = (w["w0"], jnp.concatenate([w["w2"][0], zpad], 0), jnp.concatenate([zpad, w["w2"][1]], 0),
              w["a0"], jnp.concatenate([w["a2"][0], zpad], 0), jnp.concatenate([zpad, w["a2"][1]], 0), w["kk"], w["ka"])
    kap, lw, b, kd = _rw_prep_fwd(sh, *prep_w)
    y0, y1, rw_ck, rw_tm = _rw_scan_fwd(sh, kap, lw, b, kd)
    post_w = (w["rk"], w["gng"], w["gnb"])
    out_rw = _rw_post_fwd(y0, y1, kd, sh, p_z, *post_w)

    late = w["late_weights"](out_rw)
    a, bb, merged = _merge1_fwd(out_hg, out_rw, late["whg_t"], late["wrw_t"], p_g)
    dxo, dmo, loss, dgate, d_fg = _head_fwd_bwd(x, merged, late["wout"], mods, w["final_g"], tgt)
    da, db, dp_g = _merge_bwd(dmo, late["wout"], p_g, a, bb)
    g_wout = _dw_tn(merged, dmo, 0, D, 0, "dw_out")
    g_whg = _dw_tn(da, out_hg, 0, D, 0, "dw_hg")
    g_wrw = _dw_tn(db, out_rw, 0, D, 0, "dw_rw")
    token, early = start_exchange(dict(wout=g_wout, whg_t=g_whg, wrw_t=g_wrw), "early")
    d_out_hg = _mm_nn(da, late["whg_t"], 0, "dx_hg", after=token)
    d_out_rw = _mm_nn(db, late["wrw_t"], 0, "dx_rw")

    do, dz_hg, d_hg_g = _hg_post_bwd(o0, o1, p_hg, w["hg_g"], d_out_hg)
    *hg_grads, dlb3 = _hg_scan_bwd(p_hg, lb3, hg_ck, do)
    dp_hg = _hg_dproj(*hg_grads, dz_hg)
    d_lb0, d_lb1 = _lb_bwd(w["lb0"], w["lb1"], dlb3.reshape(2, HGW))

    dy, dkd_p, dr_p, dv_p, dp_z, d_rk, d_gng, d_gnb = _rw_post_bwd(y0, y1, kd, sh, p_z, *post_w, d_out_rw)
    rw_grads = _rw_scan_bwd(sh, kap, lw, b, kd, rw_ck, rw_tm, dy)
    dsh, d_w0, d_w2p0, d_w2p1, d_a0, d_a2p0, d_a2p1, d_kk, d_ka = _rw_prep_bwd(sh, *prep_w, rw_grads, dkd_p, dr_p, dv_p)
    dp_rw, d_mu = _shift_bwd(p_rw, dsh, w["mu"])

    g_win = _dw_tn(dp_hg, h, 0, dm["NCOLS"], 0, "dw_in_hg")
    g_win = _dw_tn(dp_rw, h, 0, dm["NCOLS"], nb(hgc), "dw_in_rw", prev=g_win)
    g_win = _dw_tn(dp_z, h, TC, dm["NCOLS"], nb(hgc + sh_w), "dw_in_z", prev=g_win)
    g_win = _dw_tn(dp_g, h, TC, dm["NCOLS"], nb(hgc + sh_w + RWW), "dw_in_g", prev=g_win)
    token, late = start_exchange(dict(win_t=g_win), "late")
    in_flight = (early, late)
    dh_hg = _mm_nn(dp_hg, win_t, 0, "dh_hg", after=token)
    dh_rw = _mm_nn(dp_rw, win_t, nb(hgc), "dh_rw")
    dh_z = _mm_nn(dp_z, win_t, nb(hgc + sh_w), "dh_z")
    dh_g = _mm_nn(dp_g, win_t, nb(hgc + sh_w + RWW), "dh_g")
    gx, d_ng, dmods = _h_bwd(ctx, x, mods, w["norm_g"], dh_hg, dh_rw, dh_z, dh_g, dxo, dgate)

    small = dict(norm_g=d_ng, lb0=d_lb0, lb1=d_lb1, hg_g=d_hg_g, mu=d_mu, w0=d_w0,
                 w2=jnp.stack([d_w2p0[:LORA], d_w2p1[LORA:]]), a0=d_a0, a2=jnp.stack([d_a2p0[:LORA], d_a2p1[LORA:]]),
                 kk=d_kk, ka=d_ka, rk=d_rk, gng=d_gng, gnb=d_gnb, final_g=d_fg)
    return loss, gx, dmods, in_flight, small


MESH = pl.DeviceIdType.MESH


def _comm_call(body, name, bufs, out_shapes, nsem, nloc):
    hbm = pl.BlockSpec(memory_space=pl.ANY)
    return pl.pallas_call(
        body, name=name, in_specs=[hbm] * len(bufs), out_specs=[hbm] * len(out_shapes), out_shape=out_shapes,
        scratch_shapes=[pltpu.SemaphoreType.DMA((nsem,)), pltpu.SemaphoreType.DMA((nsem,)), pltpu.SemaphoreType.DMA((nloc,))],
    )(*bufs)


GATHER_PIECE_BYTES = 4 * 1024 * 1024


def _gather2(bufs, name):
    pieces = []
    for bi, b in enumerate(bufs):
        rows = b.shape[0]
        if rows % 32 == 0 and b.size * b.dtype.itemsize > GATHER_PIECE_BYTES:
            pieces += [(bi, 0, rows // 2), (bi, rows // 2, rows // 2)]
        else:
            pieces.append((bi, 0, rows))
    nbuf = len(bufs)

    def body(*refs):
        ins, outs = refs[:nbuf], refs[nbuf:2 * nbuf]
        send_sems, recv_sems, _ = refs[2 * nbuf:]
        x, y, c = lax.axis_index("x"), lax.axis_index("y"), lax.axis_index("c")
        me, sib = (x, y, c), (x, y, 1 - c)
        flip = lambda a, b: a + b - 2 * a * b
        xn, yn, diag = (1 - x, y), (x, 1 - y), (1 - x, 1 - y)
        relay_from, relay_to = (flip(x, c), flip(1 - y, c)), (flip(1 - x, c), flip(y, c))

        def copy(pi, k, block, to, own=False):
            bi, r0, nr = pieces[pi]
            rows = outs[bi].at[4 * block[0] + 2 * block[1] + block[2], pl.ds(r0, nr)]
            return pltpu.make_async_remote_copy(src_ref=ins[bi].at[pl.ds(r0, nr)] if own else rows, dst_ref=rows,
                                                send_sem=send_sems.at[7 * pi + k], recv_sem=recv_sems.at[7 * pi + k],
                                                device_id=to, device_id_type=MESH)

        started = []
        for pi in range(len(pieces)):
            started += [copy(pi, 0, me, sib, own=True), copy(pi, 1, me, (*xn, c), own=True), copy(pi, 2, me, (*yn, c), own=True)]
        for cp in started:
            cp.start()
        for pi in range(len(pieces)):
            for k, chip in ((1, xn), (2, yn)):
                copy(pi, k, (*chip, c), me).wait_recv()
                started.append(copy(pi, 3 + k, (*chip, c), sib))
                started[-1].start()
            started.append(copy(pi, 3, (*relay_from, c), (*relay_to, c)))
            started[-1].start()
        for pi in range(len(pieces)):
            copy(pi, 3, (*diag, c), me).wait_recv()
            started.append(copy(pi, 6, (*diag, c), sib))
            started[-1].start()
        for pi in range(len(pieces)):
            copy(pi, 0, sib, me).wait_recv()
            for k, chip in ((4, xn), (5, yn), (6, diag)):
                copy(pi, k, (*chip, 1 - c), me).wait_recv()
        for cp in started:
            cp.wait_send()

    return _comm_call(body, name, bufs, [_sds((NDEV,) + b.shape, b.dtype) for b in bufs], 7 * len(pieces), 1)


def _gather_all(bufs, name):
    me = 4 * lax.axis_index("x") + 2 * lax.axis_index("y") + lax.axis_index("c")
    return [lax.dynamic_update_index_in_dim(g, b, me, axis=0) for g, b in zip(_gather2(bufs, name), bufs)]


def _peer_copies(srcs, lands, send_sems, recv_sems):
    x, y, c = lax.axis_index("x"), lax.axis_index("y"), lax.axis_index("c")
    me = 4 * x + 2 * y + c
    pairs = []
    for bi in range(len(srcs)):
        for k in range(1, NDEV):
            px, py, pc = (1 - x if k & 4 else x), (1 - y if k & 2 else y), (1 - c if k & 1 else c)
            mk = lambda dst: pltpu.make_async_remote_copy(
                src_ref=srcs[bi], dst_ref=dst, send_sem=send_sems.at[7 * bi + k - 1], recv_sem=recv_sems.at[7 * bi + k - 1],
                device_id=(px, py, pc), device_id_type=MESH)
            pairs.append((mk(lands[bi].at[me]), mk(lands[bi].at[4 * px + 2 * py + pc])))
    return pairs


def _direct_gather_start(bufs, dep, name):
    nbuf = len(bufs)
    hbm, sem = pl.BlockSpec(memory_space=pltpu.HBM), pl.BlockSpec(memory_space=pltpu.SEMAPHORE)

    def body(*refs):
        srcs, lands = refs[:nbuf], refs[nbuf:2 * nbuf]
        for send, _ in _peer_copies(srcs, lands, refs[2 * nbuf + 1], refs[2 * nbuf + 2]):
            send.start()
        refs[-1][...] = jnp.zeros_like(refs[-1])

    held = [pltpu.HBM(b.shape, b.dtype) for b in bufs]
    zones = [pltpu.HBM((NDEV,) + b.shape, b.dtype) for b in bufs]
    outs = pl.pallas_call(
        body, name=name, in_specs=[hbm] * (2 * nbuf) + [pl.BlockSpec(memory_space=pl.ANY)],
        out_specs=(sem, sem, *[hbm] * (2 * nbuf), pl.BlockSpec(memory_space=pltpu.VMEM)),
        out_shape=(pltpu.SemaphoreType.DMA((7 * nbuf,)), pltpu.SemaphoreType.DMA((7 * nbuf,)), *held, *zones, _sds((8, 128))),
        input_output_aliases={i: 2 + i for i in range(2 * nbuf)},
        compiler_params=pltpu.CompilerParams(has_side_effects=pltpu.SideEffectType.DATAFLOW_SIDE_EFFECTING),
    )(*[pltpu.with_memory_space_constraint(b, pltpu.HBM) for b in bufs],
      *[pltpu.with_memory_space_constraint(lax.empty((NDEV,) + b.shape, b.dtype), pltpu.HBM) for b in bufs], dep)
    return (outs[0], outs[1], list(outs[2:2 + nbuf]), list(outs[2 + nbuf:2 + 2 * nbuf])), outs[-1]


def _direct_gather_wait(send_sems, recv_sems, srcs, lands, after, name):
    nbuf = len(srcs)
    hbm, sem = pl.BlockSpec(memory_space=pltpu.HBM), pl.BlockSpec(memory_space=pltpu.SEMAPHORE)

    def body(*refs):
        for send, recv in _peer_copies(refs[:nbuf], refs[nbuf:2 * nbuf], refs[2 * nbuf], refs[2 * nbuf + 1]):
            send.wait_send()
            recv.wait_recv()

    outs = pl.pallas_call(
        body, name=name, in_specs=[hbm] * (2 * nbuf) + [sem, sem, pl.BlockSpec(memory_space=pl.ANY)], out_specs=[hbm] * (2 * nbuf),
        out_shape=(*[pltpu.HBM(b.shape, b.dtype) for b in srcs], *[pltpu.HBM(b.shape, b.dtype) for b in lands]),
        input_output_aliases={i: i for i in range(2 * nbuf)},
        compiler_params=pltpu.CompilerParams(has_side_effects=pltpu.SideEffectType.DATAFLOW_SIDE_EFFECTING),
    )(*srcs, *lands, send_sems, recv_sems, after)
    me = 4 * lax.axis_index("x") + 2 * lax.axis_index("y") + lax.axis_index("c")
    return [lax.dynamic_update_index_in_dim(g, b, me, axis=0) for g, b in zip(outs[nbuf:], outs[:nbuf])]


def _pair_exchange(bufs, name):
    nbuf = len(bufs)

    def body(*refs):
        ins, got = refs[:nbuf], refs[nbuf:2 * nbuf]
        send_sems, recv_sems, _ = refs[2 * nbuf:]
        x, y, c = lax.axis_index("x"), lax.axis_index("y"), lax.axis_index("c")
        copies = []
        for bi in range(nbuf):
            for q in range(4):
                copies.append(pltpu.make_async_remote_copy(
                    src_ref=ins[bi].at[2 * q + 1 - c], dst_ref=got[bi].at[q], send_sem=send_sems.at[4 * bi + q],
                    recv_sem=recv_sems.at[4 * bi + q], device_id=(x, y, 1 - c), device_id_type=MESH))
                copies[-1].start()
        for cp in copies:
            cp.wait_send()
            cp.wait_recv()

    return _comm_call(body, name, bufs, [_sds((4,) + b.shape[1:], b.dtype) for b in bufs], 4 * nbuf, 1)


def _chip_copies(srcs, lands, send_sems, recv_sems):
    x, y, c = lax.axis_index("x"), lax.axis_index("y"), lax.axis_index("c")
    myq = 2 * x + y
    pairs = []
    for bi in range(len(srcs)):
        for j, (qx, qy) in enumerate([(1 - x, y), (x, 1 - y), (1 - x, 1 - y)]):
            q = 2 * qx + qy
            mk = lambda src, dst: pltpu.make_async_remote_copy(
                src_ref=src, dst_ref=dst, send_sem=send_sems.at[3 * bi + j], recv_sem=recv_sems.at[3 * bi + j],
                device_id=(qx, qy, c), device_id_type=MESH)
            pairs.append((mk(srcs[bi].at[q], lands[bi].at[myq]), mk(srcs[bi].at[myq], lands[bi].at[q])))
    return pairs


def _chip_exchange_start(bufs, name):
    nbuf = len(bufs)
    hbm, sem = pl.BlockSpec(memory_space=pltpu.HBM), pl.BlockSpec(memory_space=pltpu.SEMAPHORE)

    def body(*refs):
        srcs, lands = refs[:nbuf], refs[nbuf:2 * nbuf]
        send_sems, recv_sems = refs[2 * nbuf], refs[2 * nbuf + 1]
        token = refs[-1]
        for send, _ in _chip_copies(srcs, lands, send_sems, recv_sems):
            send.start()
        token[...] = jnp.zeros_like(token)

    held = [pltpu.HBM(b.shape, b.dtype) for b in bufs]
    outs = pl.pallas_call(
        body, name=name, in_specs=[hbm] * (2 * nbuf), out_specs=(sem, sem, *[hbm] * (2 * nbuf), pl.BlockSpec(memory_space=pltpu.VMEM)),
        out_shape=(pltpu.SemaphoreType.DMA((3 * nbuf,)), pltpu.SemaphoreType.DMA((3 * nbuf,)), *held, *held, _sds((8, 128))),
        input_output_aliases={i: 2 + i for i in range(2 * nbuf)},
        compiler_params=pltpu.CompilerParams(has_side_effects=pltpu.SideEffectType.DATAFLOW_SIDE_EFFECTING),
    )(*[pltpu.with_memory_space_constraint(b, pltpu.HBM) for b in bufs],
      *[pltpu.with_memory_space_constraint(lax.empty(b.shape, b.dtype), pltpu.HBM) for b in bufs])
    return outs[0], outs[1], list(outs[2:2 + nbuf]), list(outs[2 + nbuf:2 + 2 * nbuf]), outs[-1]


def _chip_exchange_wait(send_sems, recv_sems, srcs, lands, after, name):
    nbuf = len(srcs)
    hbm, sem = pl.BlockSpec(memory_space=pltpu.HBM), pl.BlockSpec(memory_space=pltpu.SEMAPHORE)

    def body(*refs):
        src_refs, land_refs = refs[:nbuf], refs[nbuf:2 * nbuf]
        for send, recv in _chip_copies(src_refs, land_refs, refs[2 * nbuf], refs[2 * nbuf + 1]):
            send.wait_send()
            recv.wait_recv()

    held = [pltpu.HBM(b.shape, b.dtype) for b in srcs]
    outs = pl.pallas_call(
        body, name=name, in_specs=[hbm] * (2 * nbuf) + [sem, sem, pl.BlockSpec(memory_space=pl.ANY)], out_specs=[hbm] * (2 * nbuf),
        out_shape=(*held, *held), input_output_aliases={i: i for i in range(2 * nbuf)},
        compiler_params=pltpu.CompilerParams(has_side_effects=pltpu.SideEffectType.DATAFLOW_SIDE_EFFECTING),
    )(*srcs, *lands, send_sems, recv_sems, after)
    return list(outs[:nbuf]), list(outs[nbuf:])


def _prefetch_call(body, name, scalars, grid, in_specs, out_specs, out_shape, args):
    return pl.pallas_call(
        body, name=name, out_shape=out_shape,
        grid_spec=pltpu.PrefetchScalarGridSpec(num_scalar_prefetch=1, grid=grid, in_specs=in_specs, out_specs=out_specs),
        compiler_params=pltpu.CompilerParams(dimension_semantics=("parallel",) * len(grid), vmem_limit_bytes=V7X_VMEM_LIMIT))(scalars, *args)


def _pair_add(g, got, core, name):
    _, rows, cols = got.shape
    tr = _row_tile(rows, 16, 1024)

    def body(c_ref, a_ref, b_ref, o_ref):
        o_ref[...] = (a_ref[...].astype(F32) + b_ref[...].astype(F32)).astype(o_ref.dtype)

    blk = pl.BlockSpec((None, tr, cols), lambda q, i, c_ref: (q, i, 0))
    mine = pl.BlockSpec((None, None, tr, cols), lambda q, i, c_ref: (q, c_ref[0], i, 0))
    return _prefetch_call(body, name, core, (4, rows // tr), [mine, blk], blk, _sds(got.shape, got.dtype),
                          (g.reshape(4, 2, rows, cols), got))


def _reduce_chips(chip_sums, recv, slots, name):
    _, rows, cols = recv.shape
    cw = 256 if cols % 256 == 0 else cols

    def body(s_ref, own_ref, r1_ref, r2_ref, r3_ref, o_ref):
        o_ref[...] = ((own_ref[...].astype(F32) + r1_ref[...].astype(F32)) + r2_ref[...].astype(F32)) + r3_ref[...].astype(F32)

    pick = lambda k: pl.BlockSpec((None, rows, cw), lambda j, s_ref: (s_ref[k], 0, j))
    return _prefetch_call(body, name, slots, (cols // cw,), [pick(0), pick(1), pick(2), pick(3)],
                          pl.BlockSpec((rows, cw), lambda j, s_ref: (0, j)), _sds((rows, cols)), (chip_sums, recv, recv, recv))


def _row_tile(rows, mult, cap):
    best = mult
    for t in range(mult, cap + 1, mult):
        if rows % t == 0:
            best = t
    assert rows % best == 0, (rows, mult)
    return best


def _reduce_sources(r, name):
    nsrc, rows, cols = r.shape
    tr = _row_tile(rows, 16 if r.dtype.itemsize == 2 else 8, 128)

    def body(r_ref, o_ref):
        acc = r_ref[0].astype(F32)
        for j in range(1, nsrc):
            acc = acc + r_ref[j].astype(F32)
        o_ref[...] = acc

    return _call(body, name, (rows // tr,), [pl.BlockSpec((nsrc, tr, cols), lambda i: (0, i, 0))],
                 pl.BlockSpec((tr, cols), lambda i: (i, 0)), _sds((rows, cols)), sem=("parallel",))(r)


def _adamw(w, g, m, v, name):
    rows, cols = w.shape
    tr = _row_tile(rows, 8, 128)
    c1 = 1.0 - ADAM_B1 ** ADAM_STEP
    c2 = 1.0 - ADAM_B2 ** ADAM_STEP

    def body(w_ref, g_ref, m_ref, v_ref, d_ref, mo_ref, vo_ref):
        g_ = g_ref[...]
        m_ = ADAM_B1 * m_ref[...] + (1.0 - ADAM_B1) * g_
        v_ = ADAM_B2 * v_ref[...] + (1.0 - ADAM_B2) * (g_ * g_)
        d_ref[...] = -ADAM_LR * ((m_ / c1) / (jnp.sqrt(v_ / c2) + ADAM_EPS) + ADAM_WD * w_ref[...])
        mo_ref[...] = m_
        vo_ref[...] = v_

    blk = pl.BlockSpec((tr, cols), lambda i: (i, 0))
    return _call(body, name, (rows // tr,), [blk] * 4, [blk] * 3, [_sds((rows, cols))] * 3, sem=("parallel",))(w, g, m, v)


SLAB_PART = 1024


def _pack(arrs):
    parts = []
    for a in arrs:
        flat = a.reshape(-1)
        pad = (-flat.shape[0]) % SLAB_PART
        if pad:
            flat = jnp.concatenate([flat, jnp.zeros((pad,), flat.dtype)])
        parts.append(flat.reshape(-1, 128))
    return jnp.concatenate(parts, axis=0)


def _unpack(slab, shapes):
    outs, row = [], 0
    for s in shapes:
        n = 1
        for d in s:
            n *= d
        rows = (n + SLAB_PART - 1) // SLAB_PART * (SLAB_PART // 128)
        outs.append(slab[row:row + rows].reshape(-1)[:n].reshape(s))
        row += rows
    return outs


def _unshard_last(g, lead):
    nl = len(lead)
    return jnp.transpose(g, tuple(range(1, nl + 1)) + (0, nl + 1)).reshape(tuple(lead) + (-1,))


def kernel(x, c, ctx, c_ctx, ada_w, ada_b, norm_g, w_in, hg_lb, hg_norm_g, rw_mu, rw_w0, rw_w2, rw_a0, rw_a2, rw_kk, rw_ka, rw_rk, rw_gn_g, rw_gn_b, w_hg_out, w_rw_out, w_out, final_g, loss_target, m_c_ctx, m_ada_w, m_ada_b, m_norm_g, m_w_in, m_hg_lb, m_hg_norm_g, m_rw_mu, m_rw_w0, m_rw_w2, m_rw_a0, m_rw_a2, m_rw_kk, m_rw_ka, m_rw_rk, m_rw_gn_g, m_rw_gn_b, m_w_hg_out, m_w_rw_out, m_w_out, m_final_g, v_c_ctx, v_ada_w, v_ada_b, v_norm_g, v_w_in, v_hg_lb, v_hg_norm_g, v_rw_mu, v_rw_w0, v_rw_w2, v_rw_a0, v_rw_a2, v_rw_kk, v_rw_ka, v_rw_rk, v_rw_gn_g, v_rw_gn_b, v_w_hg_out, v_w_rw_out, v_w_out, v_final_g):
    dm = _dims()
    me = 4 * lax.axis_index("x") + 2 * lax.axis_index("y") + lax.axis_index("c")

    sharded_small = [hg_lb, rw_mu[0], rw_w0[0], rw_w2[0], rw_a0[0], rw_a2[0]]
    c_rows = jnp.concatenate([c, jnp.zeros((7, D), F32)], axis=0)
    gathered = _gather_all([w_in[0].T.astype(BF16), _pack(sharded_small), c_rows], "gather_weights")
    win_t = gathered[0].reshape(-1, D)
    per_dev = jax.vmap(lambda s: _unpack(s, [a.shape for a in sharded_small]))(gathered[1])
    hg_lb_f, mu_f, w0_f, w2_f, a0_f, a2_f = [_unshard_last(p, p.shape[1:-1]) for p in per_dev]

    ncol = ada_w.shape[2]
    cc16 = jnp.concatenate([gathered[2][:, 0], c_ctx[None], jnp.zeros((7, D), F32)], axis=0)
    mod_all = _gather_all([_ada_fwd(cc16, ada_w[0])], "gather_mods")[0]
    mod_x = lax.dynamic_index_in_dim(mod_all, me, axis=1, keepdims=False).reshape(1, -1) + ada_b
    mod_c = mod_all[:, NDEV].reshape(1, -1) + ada_b
    mods = jnp.concatenate([mod_x, mod_c, jnp.zeros((6, 3 * D), F32)], axis=0)

    late_flight, token = _direct_gather_start([w_out[0].astype(BF16), w_hg_out[0].T.astype(BF16), w_rw_out[0].T.astype(BF16)],
                                              mod_all, "gather_late_start")
    mods = mods + token[0:1, 0:1]

    def late_weights(after):
        got = _direct_gather_wait(*late_flight, after, "gather_late_wait")
        return dict(zip(("wout", "whg_t", "wrw_t"), [g.reshape(-1, g.shape[2]) for g in got]))

    w = dict(win_t=win_t, late_weights=late_weights, norm_g=norm_g,
             lb0=hg_lb_f[:, 0], lb1=hg_lb_f[:, 1], hg_g=hg_norm_g, mu=mu_f, w0=w0_f, w2=w2_f, a0=a0_f, a2=a2_f,
             kk=rw_kk, ka=rw_ka, rk=rw_rk, gng=rw_gn_g, gnb=rw_gn_b, final_g=final_g[None])
    def start_exchange(grads_bf16, tag):
        names = list(grads_bf16)
        blocks = [grads_bf16[n].reshape(NDEV, -1, grads_bf16[n].shape[1]) for n in names]
        got = _pair_exchange(blocks, "pair_grads_" + tag)
        core = lax.axis_index("c").astype(jnp.int32).reshape(1)
        chip_sums = [_pair_add(g, r, core, "pair_add_" + n) for g, r, n in zip(blocks, got, names)]
        send_sems, recv_sems, held, lands, token = _chip_exchange_start(chip_sums, "chip_grads_start_" + tag)
        return token, (names, send_sems, recv_sems, held, lands)

    loss_dev, grad_x, dmods, in_flight, small = _local_step(x[0], ctx[0], loss_target[0], mods, w, start_exchange)
    loss = lax.psum(loss_dev[0, 0], AXES)

    my_chip = (2 * lax.axis_index("x") + lax.axis_index("y")).astype(jnp.int32)
    others = jnp.arange(3, dtype=jnp.int32)
    slots = jnp.concatenate([my_chip.reshape(1), others + (others >= my_chip).astype(jnp.int32)])

    def finish_exchange(tag, flight, after):
        names, *copies = flight
        chip_sums, recv = _chip_exchange_wait(*copies, after, "chip_grads_wait_" + tag)
        return {n: _reduce_chips(cs, r, slots, "reduce_" + n) for n, cs, r in zip(names, chip_sums, recv)}

    reduced = finish_exchange("early", in_flight[0], grad_x)

    dmods_all = _gather_all([dmods], "gather_dmods")[0]
    my_cols = lambda a: lax.dynamic_slice_in_dim(a, me * ncol, ncol, axis=1)
    g_ada, dcc16, d_ada_b = _ada_bwd(cc16, ada_w[0], my_cols(dmods_all[:, 0]), my_cols(dmods_all[:, 1]), dmods)
    small = dict(small, c_ctx=dcc16[NDEV:NDEV + 1], ada_b=d_ada_b)

    small_names = ["c_ctx", "ada_b", "norm_g", "lb0", "lb1", "hg_g", "mu", "w0", "a0", "kk", "ka", "rk", "gng", "gnb", "final_g"]
    lora_names = ["w2", "a2"]
    small_all, lora_all = _gather_all([_pack([small[n] for n in small_names]), _pack([small[n] for n in lora_names]).astype(BF16)],
                                      "gather_small_grads")
    small_sum = _reduce_sources(small_all, "reduce_small")
    sg = dict(zip(small_names, _unpack(small_sum, [small[n].shape for n in small_names])))
    sg.update(zip(lora_names, _unpack(_reduce_sources(lora_all, "reduce_lora"), [small[n].shape for n in lora_names])))

    reduced.update(finish_exchange("late", in_flight[1], small_sum))
    g_win_t, g_wout, g_whg_t, g_wrw_t = [reduced[n] for n in ("win_t", "wout", "whg_t", "wrw_t")]

    def my_shard(full):
        n = full.shape[-1] // NDEV
        return lax.dynamic_slice_in_dim(full, me * n, n, axis=full.ndim - 1)

    grads = dict(
        c_ctx=sg["c_ctx"][0], ada_w=g_ada[None], ada_b=sg["ada_b"], norm_g=sg["norm_g"], w_in=g_win_t.T[None],
        hg_lb=my_shard(jnp.stack([sg["lb0"], sg["lb1"]], axis=1)), hg_norm_g=sg["hg_g"], rw_mu=my_shard(sg["mu"])[None],
        rw_w0=my_shard(sg["w0"])[None], rw_w2=my_shard(sg["w2"])[None], rw_a0=my_shard(sg["a0"])[None], rw_a2=my_shard(sg["a2"])[None],
        rw_kk=sg["kk"], rw_ka=sg["ka"], rw_rk=sg["rk"], rw_gn_g=sg["gng"], rw_gn_b=sg["gnb"],
        w_hg_out=g_whg_t.T[None], w_rw_out=g_wrw_t.T[None], w_out=g_wout[None], final_g=sg["final_g"][0])
    weights = dict(c_ctx=c_ctx, ada_w=ada_w, ada_b=ada_b, norm_g=norm_g, w_in=w_in, hg_lb=hg_lb, hg_norm_g=hg_norm_g, rw_mu=rw_mu,
                   rw_w0=rw_w0, rw_w2=rw_w2, rw_a0=rw_a0, rw_a2=rw_a2, rw_kk=rw_kk, rw_ka=rw_ka, rw_rk=rw_rk, rw_gn_g=rw_gn_g,
                   rw_gn_b=rw_gn_b, w_hg_out=w_hg_out, w_rw_out=w_rw_out, w_out=w_out, final_g=final_g)
    m_in = dict(zip(weights, (m_c_ctx, m_ada_w, m_ada_b, m_norm_g, m_w_in, m_hg_lb, m_hg_norm_g, m_rw_mu, m_rw_w0, m_rw_w2, m_rw_a0,
                              m_rw_a2, m_rw_kk, m_rw_ka, m_rw_rk, m_rw_gn_g, m_rw_gn_b, m_w_hg_out, m_w_rw_out, m_w_out, m_final_g)))
    v_in = dict(zip(weights, (v_c_ctx, v_ada_w, v_ada_b, v_norm_g, v_w_in, v_hg_lb, v_hg_norm_g, v_rw_mu, v_rw_w0, v_rw_w2, v_rw_a0,
                              v_rw_a2, v_rw_kk, v_rw_ka, v_rw_rk, v_rw_gn_g, v_rw_gn_b, v_w_hg_out, v_w_rw_out, v_w_out, v_final_g)))

    big_w = ["ada_w", "w_in", "w_hg_out", "w_rw_out", "w_out"]
    delta, new_m, new_v = {}, {}, {}
    for n in big_w:
        shp = weights[n].shape
        two = lambda a: a.reshape(shp[-2], shp[-1])
        d_, m_, v_ = _adamw(two(weights[n]), two(grads[n]), two(m_in[n]), two(v_in[n]), "adamw_" + n)
        delta[n], new_m[n], new_v[n] = d_.reshape(shp), m_.reshape(shp), v_.reshape(shp)
    rest = [n for n in weights if n not in big_w]
    shapes = [weights[n].shape for n in rest]
    d_s, m_s, v_s = _adamw(_pack([weights[n] for n in rest]), _pack([grads[n] for n in rest]), _pack([m_in[n] for n in rest]),
                           _pack([v_in[n] for n in rest]), "adamw_small")
    for n, d_, m_, v_ in zip(rest, _unpack(d_s, shapes), _unpack(m_s, shapes), _unpack(v_s, shapes)):
        delta[n], new_m[n], new_v[n] = d_, m_, v_

    order = list(weights)
    return (loss, grad_x[None], *[grads[n] for n in order], *[delta[n] for n in order],
            *[new_m[n] for n in order], *[new_v[n] for n in order])
```
